```python
import jax, jax.numpy as jnp
from jax import lax
import numpy as np

D_MODEL = 1024
BATCH = 8
SEQ = 2048
DEPTH = 1

GDN_HEADS = 8
GDN_HEAD_DIM = 64
FOX_HEADS = 8
FOX_HEAD_DIM = 64
GDN_WIDTH = GDN_HEADS * GDN_HEAD_DIM
FOX_WIDTH = FOX_HEADS * FOX_HEAD_DIM
D_MIX = GDN_WIDTH + FOX_WIDTH
CONV_K = 4
CHUNK = 64
Q_BLOCK = 128
D_FF = -(-8 * D_MODEL // (3 * 256)) * 256
EPS = 1e-6

SPLIT_SIZES = [
    GDN_WIDTH, GDN_WIDTH, GDN_WIDTH,
    GDN_WIDTH,
    GDN_HEADS, GDN_HEADS,
    FOX_WIDTH, FOX_WIDTH, FOX_WIDTH,
    FOX_WIDTH,
    FOX_HEADS,
]
D_IN = sum(SPLIT_SIZES)
SPLIT_POINTS = list(np.cumsum(SPLIT_SIZES)[:-1])

kernel_name = "hymba_gdn_fox_swiglu"


def rms_norm(x, w):
    xf = x.astype(jnp.float32)
    out = xf * lax.rsqrt(jnp.mean(xf * xf, axis=-1, keepdims=True) + EPS)
    return (out * w.astype(jnp.float32)).astype(x.dtype)


def l2_norm(x):
    xf = x.astype(jnp.float32)
    return xf * lax.rsqrt(jnp.sum(xf * xf, axis=-1, keepdims=True) + EPS)


def causal_depthwise_conv(x, w):
    c = x.shape[-1]
    return lax.conv_general_dilated(
        x, w.reshape(CONV_K, 1, c).astype(x.dtype), window_strides=(1,),
        padding=[(CONV_K - 1, 0)], dimension_numbers=("NWC", "WIO", "NWC"),
        feature_group_count=c)


def gated_delta_rule(q, k, v, beta, g):
    B, T, H, Dk = q.shape
    Dv = v.shape[-1]
    N = T // CHUNK

    def chunks(t):
        return t.reshape(B, N, CHUNK, H, -1).transpose(0, 3, 1, 2, 4)

    q = chunks(q.astype(jnp.float32)) * (Dk ** -0.5)
    k = chunks(k.astype(jnp.float32))
    v = chunks(v.astype(jnp.float32))
    beta = beta.astype(jnp.float32).reshape(B, N, CHUNK, H).transpose(0, 3, 1, 2)
    g = jnp.cumsum(g.astype(jnp.float32).reshape(B, N, CHUNK, H).transpose(0, 3, 1, 2), axis=-1)

    causal = jnp.tril(jnp.ones((CHUNK, CHUNK), dtype=bool))
    strict = jnp.tril(jnp.ones((CHUNK, CHUNK), dtype=bool), k=-1)
    decay = jnp.exp(jnp.where(causal, g[..., :, None] - g[..., None, :], -jnp.inf))

    k_beta = k * beta[..., None]
    v_beta = v * beta[..., None]
    L = jnp.where(strict, jnp.einsum("bhncd,bhnmd->bhncm", k_beta, k) * decay, 0.0)
    eye = jnp.eye(CHUNK, dtype=jnp.float32)
    Tm = lax.linalg.triangular_solve(eye + L, jnp.broadcast_to(eye, L.shape),
                                     left_side=True, lower=True, unit_diagonal=True)
    u = jnp.einsum("bhncm,bhnmd->bhncd", Tm, v_beta)
    w = jnp.einsum("bhncm,bhnmd->bhncd", Tm, k_beta * jnp.exp(g)[..., None])
    intra = jnp.where(causal, jnp.einsum("bhncd,bhnmd->bhncm", q, k) * decay, 0.0)

    def to_scan(t):
        return jnp.moveaxis(t, 2, 0)

    def step(S, xs):
        q_c, k_c, u_c, w_c, A_c, g_c = xs
        v_new = u_c - jnp.einsum("bhcd,bhde->bhce", w_c, S)
        o = (jnp.einsum("bhcd,bhde->bhce", q_c * jnp.exp(g_c)[..., None], S)
             + jnp.einsum("bhcm,bhme->bhce", A_c, v_new))
        g_last = g_c[..., -1]
        S = (S * jnp.exp(g_last)[..., None, None]
             + jnp.einsum("bhcd,bhce->bhde", k_c * jnp.exp(g_last[..., None] - g_c)[..., None], v_new))
        return S, o

    S0 = jnp.zeros((B, H, Dk, Dv), jnp.float32)
    _, o = lax.scan(step, S0, (to_scan(q), to_scan(k), to_scan(u), to_scan(w),
                               to_scan(intra), to_scan(g)))
    return o.transpose(1, 0, 3, 2, 4).reshape(B, T, H, Dv)


def forgetting_attention(q, k, v, log_f):
    B, T, H, D = q.shape
    nb = T // Q_BLOCK
    F = jnp.cumsum(log_f.astype(jnp.float32), axis=1).transpose(0, 2, 1)
    qb = q.reshape(B, nb, Q_BLOCK, H, D).transpose(1, 0, 2, 3, 4)
    Fq = F.reshape(B, H, nb, Q_BLOCK).transpose(2, 0, 1, 3)
    pos_k = jnp.arange(T)
    scale = D ** -0.5

    def block(args):
        i, q_i, F_i = args
        s = jnp.einsum("bqhd,bkhd->bhqk", q_i, k, preferred_element_type=jnp.float32) * scale
        s = s + (F_i[..., :, None] - F[:, :, None, :])
        pos_q = i * Q_BLOCK + jnp.arange(Q_BLOCK)
        s = jnp.where(pos_q[:, None] >= pos_k[None, :], s, -jnp.inf)
        p = jax.nn.softmax(s, axis=-1)
        return jnp.einsum("bhqk,bkhd->bqhd", p.astype(v.dtype), v)

    o = lax.map(block, (jnp.arange(nb), qb, Fq))
    return o.transpose(1, 0, 2, 3, 4).reshape(B, T, H, D)


def _fwd_setup_inputs(seed: int = 0) -> dict:
    key = jax.random.key(seed)
    ks = jax.random.split(key, 20)
    f32 = jnp.float32

    def normal(k, shape, fan_in):
        return jax.random.normal(k, shape, f32) * (fan_in ** -0.5)

    def gain(k, shape):
        return 1.0 + 0.02 * jax.random.normal(k, shape, f32)

    x = jax.random.normal(ks[0], (BATCH, SEQ, D_MODEL), f32)
    norm1_w = gain(ks[1], (DEPTH, D_MODEL))
    w_in = normal(ks[2], (DEPTH, D_MODEL, D_IN), D_MODEL)
    gdn_conv_w = normal(ks[3], (DEPTH, CONV_K, 3 * GDN_WIDTH), CONV_K)
    gdn_A_log = jnp.log(jax.random.uniform(ks[4], (DEPTH, GDN_HEADS), f32, 1.0, 16.0))
    dt = jnp.exp(jax.random.uniform(ks[5], (DEPTH, GDN_HEADS), f32, np.log(1e-3), np.log(1e-1)))
    gdn_dt_bias = dt + jnp.log(-jnp.expm1(-dt))
    gdn_out_norm_w = gain(ks[6], (DEPTH, GDN_HEAD_DIM))
    fox_f_bias = jax.random.uniform(ks[7], (DEPTH, FOX_HEADS), f32, 1.0, 5.0)
    fox_q_norm_w = gain(ks[8], (DEPTH, FOX_HEAD_DIM))
    fox_k_norm_w = gain(ks[9], (DEPTH, FOX_HEAD_DIM))
    w_out = normal(ks[10], (DEPTH, D_MIX, D_MODEL), D_MIX)
    norm2_w = gain(ks[11], (DEPTH, D_MODEL))
    w_ffn_gate = normal(ks[12], (DEPTH, D_MODEL, D_FF), D_MODEL)
    w_ffn_up = normal(ks[13], (DEPTH, D_MODEL, D_FF), D_MODEL)
    w_ffn_down = normal(ks[14], (DEPTH, D_FF, D_MODEL), D_FF)
    final_norm_w = gain(ks[15], (D_MODEL,))
    return {"x": x, "norm1_w": norm1_w, "w_in": w_in, "gdn_conv_w": gdn_conv_w,
            "gdn_A_log": gdn_A_log, "gdn_dt_bias": gdn_dt_bias,
            "gdn_out_norm_w": gdn_out_norm_w, "fox_f_bias": fox_f_bias,
            "fox_q_norm_w": fox_q_norm_w, "fox_k_norm_w": fox_k_norm_w,
            "w_out": w_out, "norm2_w": norm2_w, "w_ffn_gate": w_ffn_gate,
            "w_ffn_up": w_ffn_up, "w_ffn_down": w_ffn_down, "final_norm_w": final_norm_w}


def _fwd_reference(x, norm1_w, w_in, gdn_conv_w, gdn_A_log, gdn_dt_bias, gdn_out_norm_w,
              fox_f_bias, fox_q_norm_w, fox_k_norm_w, w_out, norm2_w, w_ffn_gate,
              w_ffn_up, w_ffn_down, final_norm_w):
    B, T, _ = x.shape
    for l in range(DEPTH):
        h = rms_norm(x, norm1_w[l])
        proj = h @ w_in[l]
        (g_q, g_k, g_v, g_z, g_b, g_a,
         f_q, f_k, f_v, f_gate, f_f) = jnp.split(proj, SPLIT_POINTS, axis=-1)

        qkv = jax.nn.silu(causal_depthwise_conv(jnp.concatenate([g_q, g_k, g_v], -1), gdn_conv_w[l]))
        g_q, g_k, g_v = jnp.split(qkv, 3, axis=-1)
        gq = l2_norm(g_q.reshape(B, T, GDN_HEADS, GDN_HEAD_DIM))
        gk = l2_norm(g_k.reshape(B, T, GDN_HEADS, GDN_HEAD_DIM))
        gv = g_v.reshape(B, T, GDN_HEADS, GDN_HEAD_DIM)
        beta = jax.nn.sigmoid(g_b.astype(jnp.float32))
        g_log = -jnp.exp(gdn_A_log[l].astype(jnp.float32)) * jax.nn.softplus(
            g_a.astype(jnp.float32) + gdn_dt_bias[l].astype(jnp.float32))
        o_gdn = gated_delta_rule(gq, gk, gv, beta, g_log)
        z = g_z.reshape(B, T, GDN_HEADS, GDN_HEAD_DIM).astype(jnp.float32)
        o_gdn = rms_norm(o_gdn, gdn_out_norm_w[l]) * jax.nn.silu(z)
        o_gdn = o_gdn.astype(x.dtype).reshape(B, T, GDN_WIDTH)

        fq = rms_norm(f_q.reshape(B, T, FOX_HEADS, FOX_HEAD_DIM), fox_q_norm_w[l])
        fk = rms_norm(f_k.reshape(B, T, FOX_HEADS, FOX_HEAD_DIM), fox_k_norm_w[l])
        fv = f_v.reshape(B, T, FOX_HEADS, FOX_HEAD_DIM)
        log_f = jax.nn.log_sigmoid(f_f.astype(jnp.float32) + fox_f_bias[l].astype(jnp.float32))
        o_fox = forgetting_attention(fq, fk, fv, log_f).reshape(B, T, FOX_WIDTH)
        o_fox = o_fox * jax.nn.sigmoid(f_gate)

        mix = jnp.concatenate([o_gdn, o_fox.astype(x.dtype)], axis=-1)
        x = x + mix @ w_out[l]

        h = rms_norm(x, norm2_w[l])
        x = x + (jax.nn.silu(h @ w_ffn_gate[l]) * (h @ w_ffn_up[l])) @ w_ffn_down[l]
    return rms_norm(x, final_norm_w)


import jax as _jax
import jax.numpy as _jnp

TWIN_FORMAT = 'train_step'
FWD_PARAMS = ['x', 'norm1_w', 'w_in', 'gdn_conv_w', 'gdn_A_log', 'gdn_dt_bias', 'gdn_out_norm_w', 'fox_f_bias', 'fox_q_norm_w', 'fox_k_norm_w', 'w_out', 'norm2_w', 'w_ffn_gate', 'w_ffn_up', 'w_ffn_down', 'final_norm_w']
TWIN_WEIGHTS = ['norm1_w', 'w_in', 'gdn_conv_w', 'gdn_A_log', 'gdn_dt_bias', 'gdn_out_norm_w', 'fox_f_bias', 'fox_q_norm_w', 'fox_k_norm_w', 'w_out', 'norm2_w', 'w_ffn_gate', 'w_ffn_up', 'w_ffn_down', 'final_norm_w']
TWIN_DIFF_INPUT = 'x'
TWIN_INPUTS = ['x', 'norm1_w', 'w_in', 'gdn_conv_w', 'gdn_A_log', 'gdn_dt_bias', 'gdn_out_norm_w', 'fox_f_bias', 'fox_q_norm_w', 'fox_k_norm_w', 'w_out', 'norm2_w', 'w_ffn_gate', 'w_ffn_up', 'w_ffn_down', 'final_norm_w', 'loss_target', 'm_norm1_w', 'm_w_in', 'm_gdn_conv_w', 'm_gdn_A_log', 'm_gdn_dt_bias', 'm_gdn_out_norm_w', 'm_fox_f_bias', 'm_fox_q_norm_w', 'm_fox_k_norm_w', 'm_w_out', 'm_norm2_w', 'm_w_ffn_gate', 'm_w_ffn_up', 'm_w_ffn_down', 'm_final_norm_w', 'v_norm1_w', 'v_w_in', 'v_gdn_conv_w', 'v_gdn_A_log', 'v_gdn_dt_bias', 'v_gdn_out_norm_w', 'v_fox_f_bias', 'v_fox_q_norm_w', 'v_fox_k_norm_w', 'v_w_out', 'v_norm2_w', 'v_w_ffn_gate', 'v_w_ffn_up', 'v_w_ffn_down', 'v_final_norm_w']
TWIN_OUTPUTS = ['loss', 'grad_x', 'grad_norm1_w', 'grad_w_in', 'grad_gdn_conv_w', 'grad_gdn_A_log', 'grad_gdn_dt_bias', 'grad_gdn_out_norm_w', 'grad_fox_f_bias', 'grad_fox_q_norm_w', 'grad_fox_k_norm_w', 'grad_w_out', 'grad_norm2_w', 'grad_w_ffn_gate', 'grad_w_ffn_up', 'grad_w_ffn_down', 'grad_final_norm_w', 'delta_norm1_w', 'delta_w_in', 'delta_gdn_conv_w', 'delta_gdn_A_log', 'delta_gdn_dt_bias', 'delta_gdn_out_norm_w', 'delta_fox_f_bias', 'delta_fox_q_norm_w', 'delta_fox_k_norm_w', 'delta_w_out', 'delta_norm2_w', 'delta_w_ffn_gate', 'delta_w_ffn_up', 'delta_w_ffn_down', 'delta_final_norm_w', 'new_m_norm1_w', 'new_m_w_in', 'new_m_gdn_conv_w', 'new_m_gdn_A_log', 'new_m_gdn_dt_bias', 'new_m_gdn_out_norm_w', 'new_m_fox_f_bias', 'new_m_fox_q_norm_w', 'new_m_fox_k_norm_w', 'new_m_w_out', 'new_m_norm2_w', 'new_m_w_ffn_gate', 'new_m_w_ffn_up', 'new_m_w_ffn_down', 'new_m_final_norm_w', 'new_v_norm1_w', 'new_v_w_in', 'new_v_gdn_conv_w', 'new_v_gdn_A_log', 'new_v_gdn_dt_bias', 'new_v_gdn_out_norm_w', 'new_v_fox_f_bias', 'new_v_fox_q_norm_w', 'new_v_fox_k_norm_w', 'new_v_w_out', 'new_v_norm2_w', 'new_v_w_ffn_gate', 'new_v_w_ffn_up', 'new_v_w_ffn_down', 'new_v_final_norm_w']
TWIN_LEAF_KINDS = {'loss': 'loss', 'grad_x': 'grad_x', 'grad_norm1_w': 'grad_w', 'grad_w_in': 'grad_w', 'grad_gdn_conv_w': 'grad_w', 'grad_gdn_A_log': 'grad_w', 'grad_gdn_dt_bias': 'grad_w', 'grad_gdn_out_norm_w': 'grad_w', 'grad_fox_f_bias': 'grad_w', 'grad_fox_q_norm_w': 'grad_w', 'grad_fox_k_norm_w': 'grad_w', 'grad_w_out': 'grad_w', 'grad_norm2_w': 'grad_w', 'grad_w_ffn_gate': 'grad_w', 'grad_w_ffn_up': 'grad_w', 'grad_w_ffn_down': 'grad_w', 'grad_final_norm_w': 'grad_w', 'delta_norm1_w': 'delta_w', 'delta_w_in': 'delta_w', 'delta_gdn_conv_w': 'delta_w', 'delta_gdn_A_log': 'delta_w', 'delta_gdn_dt_bias': 'delta_w', 'delta_gdn_out_norm_w': 'delta_w', 'delta_fox_f_bias': 'delta_w', 'delta_fox_q_norm_w': 'delta_w', 'delta_fox_k_norm_w': 'delta_w', 'delta_w_out': 'delta_w', 'delta_norm2_w': 'delta_w', 'delta_w_ffn_gate': 'delta_w', 'delta_w_ffn_up': 'delta_w', 'delta_w_ffn_down': 'delta_w', 'delta_final_norm_w': 'delta_w', 'new_m_norm1_w': 'new_m', 'new_m_w_in': 'new_m', 'new_m_gdn_conv_w': 'new_m', 'new_m_gdn_A_log': 'new_m', 'new_m_gdn_dt_bias': 'new_m', 'new_m_gdn_out_norm_w': 'new_m', 'new_m_fox_f_bias': 'new_m', 'new_m_fox_q_norm_w': 'new_m', 'new_m_fox_k_norm_w': 'new_m', 'new_m_w_out': 'new_m', 'new_m_norm2_w': 'new_m', 'new_m_w_ffn_gate': 'new_m', 'new_m_w_ffn_up': 'new_m', 'new_m_w_ffn_down': 'new_m', 'new_m_final_norm_w': 'new_m', 'new_v_norm1_w': 'new_v', 'new_v_w_in': 'new_v', 'new_v_gdn_conv_w': 'new_v', 'new_v_gdn_A_log': 'new_v', 'new_v_gdn_dt_bias': 'new_v', 'new_v_gdn_out_norm_w': 'new_v', 'new_v_fox_f_bias': 'new_v', 'new_v_fox_q_norm_w': 'new_v', 'new_v_fox_k_norm_w': 'new_v', 'new_v_w_out': 'new_v', 'new_v_norm2_w': 'new_v', 'new_v_w_ffn_gate': 'new_v', 'new_v_w_ffn_up': 'new_v', 'new_v_w_ffn_down': 'new_v', 'new_v_final_norm_w': 'new_v'}


def _forward(args):
    return _fwd_reference(*[args[k] for k in FWD_PARAMS])


def _output_shape():
    out = _jax.eval_shape(lambda: _forward(_fwd_setup_inputs(0)))
    return out.shape, out.dtype

N_MICROBATCH = 1
ADAM_LR = 0.001
ADAM_B1 = 0.9
ADAM_B2 = 0.999
ADAM_EPS = 1e-08
ADAM_WD = 0.01
ADAM_STEP = 10
PER_EXAMPLE_BATCH_AXIS = {'x': 0, 'loss_target': 0}
SHARED_INPUTS = []
_WEIGHT_DTYPES = {'norm1_w': _jnp.float32, 'w_in': _jnp.float32, 'gdn_conv_w': _jnp.float32, 'gdn_A_log': _jnp.float32, 'gdn_dt_bias': _jnp.float32, 'gdn_out_norm_w': _jnp.float32, 'fox_f_bias': _jnp.float32, 'fox_q_norm_w': _jnp.float32, 'fox_k_norm_w': _jnp.float32, 'w_out': _jnp.float32, 'norm2_w': _jnp.float32, 'w_ffn_gate': _jnp.float32, 'w_ffn_up': _jnp.float32, 'w_ffn_down': _jnp.float32, 'final_norm_w': _jnp.float32}
MOMENT_SCALE = {'norm1_w': 1.090987e-01, 'w_in': 5.322052e-02, 'gdn_conv_w': 7.066346e-02, 'gdn_A_log': 2.016608e-01, 'gdn_dt_bias': 1.922967e-01, 'gdn_out_norm_w': 2.559787e-01, 'fox_f_bias': 1.692015e-01, 'fox_q_norm_w': 5.981101e-02, 'fox_k_norm_w': 5.967275e-02, 'w_out': 5.640387e-02, 'norm2_w': 8.582479e-02, 'w_ffn_gate': 3.738191e-02, 'w_ffn_up': 3.618391e-02, 'w_ffn_down': 5.987456e-02, 'final_norm_w': 1.600599e+01}


def _to_microbatches(a, axis):
    t = _jnp.moveaxis(a, axis, 0)
    t = t.reshape((N_MICROBATCH, t.shape[0] // N_MICROBATCH) + t.shape[1:])
    return _jnp.moveaxis(t, 1, axis + 1)


def setup_inputs(seed: int = 0) -> dict:
    inp = _fwd_setup_inputs(seed)
    key = _jax.random.fold_in(_jax.random.key(seed), 7919)
    shape, _ = _output_shape()
    out = dict(inp)
    out["loss_target"] = _jax.random.normal(_jax.random.fold_in(key, 0), shape, _jnp.float32)
    for i, name in enumerate(TWIN_WEIGHTS):
        w = inp[name].astype(_jnp.float32)
        if MOMENT_SCALE is None:
            s = _jnp.sqrt(_jnp.mean(_jnp.square(w)) + 1e-30)
        else:
            s = MOMENT_SCALE[name]
        km, kv = _jax.random.split(_jax.random.fold_in(key, i + 1))
        out[name] = w
        out["m_" + name] = s * _jax.random.normal(km, w.shape, _jnp.float32)
        out["v_" + name] = (s * s) * _jax.random.uniform(kv, w.shape, _jnp.float32, 0.5, 1.5)
    if N_MICROBATCH > 1:
        for name, axis in PER_EXAMPLE_BATCH_AXIS.items():
            out[name] = _to_microbatches(out[name], axis)
    return {'x': out['x'], 'norm1_w': out['norm1_w'], 'w_in': out['w_in'], 'gdn_conv_w': out['gdn_conv_w'], 'gdn_A_log': out['gdn_A_log'], 'gdn_dt_bias': out['gdn_dt_bias'], 'gdn_out_norm_w': out['gdn_out_norm_w'], 'fox_f_bias': out['fox_f_bias'], 'fox_q_norm_w': out['fox_q_norm_w'], 'fox_k_norm_w': out['fox_k_norm_w'], 'w_out': out['w_out'], 'norm2_w': out['norm2_w'], 'w_ffn_gate': out['w_ffn_gate'], 'w_ffn_up': out['w_ffn_up'], 'w_ffn_down': out['w_ffn_down'], 'final_norm_w': out['final_norm_w'], 'loss_target': out['loss_target'], 'm_norm1_w': out['m_norm1_w'], 'm_w_in': out['m_w_in'], 'm_gdn_conv_w': out['m_gdn_conv_w'], 'm_gdn_A_log': out['m_gdn_A_log'], 'm_gdn_dt_bias': out['m_gdn_dt_bias'], 'm_gdn_out_norm_w': out['m_gdn_out_norm_w'], 'm_fox_f_bias': out['m_fox_f_bias'], 'm_fox_q_norm_w': out['m_fox_q_norm_w'], 'm_fox_k_norm_w': out['m_fox_k_norm_w'], 'm_w_out': out['m_w_out'], 'm_norm2_w': out['m_norm2_w'], 'm_w_ffn_gate': out['m_w_ffn_gate'], 'm_w_ffn_up': out['m_w_ffn_up'], 'm_w_ffn_down': out['m_w_ffn_down'], 'm_final_norm_w': out['m_final_norm_w'], 'v_norm1_w': out['v_norm1_w'], 'v_w_in': out['v_w_in'], 'v_gdn_conv_w': out['v_gdn_conv_w'], 'v_gdn_A_log': out['v_gdn_A_log'], 'v_gdn_dt_bias': out['v_gdn_dt_bias'], 'v_gdn_out_norm_w': out['v_gdn_out_norm_w'], 'v_fox_f_bias': out['v_fox_f_bias'], 'v_fox_q_norm_w': out['v_fox_q_norm_w'], 'v_fox_k_norm_w': out['v_fox_k_norm_w'], 'v_w_out': out['v_w_out'], 'v_norm2_w': out['v_norm2_w'], 'v_w_ffn_gate': out['v_w_ffn_gate'], 'v_w_ffn_up': out['v_w_ffn_up'], 'v_w_ffn_down': out['v_w_ffn_down'], 'v_final_norm_w': out['v_final_norm_w']}


def _loss(weights, diff, rest, loss_target):
    with _jax.named_scope("forward"):
        args = {**rest, TWIN_DIFF_INPUT: diff, **{k: w.astype(_WEIGHT_DTYPES[k]) for k, w in weights.items()}}
        y = _forward(args)
    with _jax.named_scope("loss_head"):
        err = _jnp.square(y.astype(_jnp.float32) - loss_target)
        return 0.5 * _jnp.sum(_jnp.mean(err, axis=-1)) if err.ndim else 0.5 * err


def _adamw(w, g, m, v):
    m = ADAM_B1 * m + (1.0 - ADAM_B1) * g
    v = ADAM_B2 * v + (1.0 - ADAM_B2) * _jnp.square(g)
    m_hat = m / (1.0 - ADAM_B1 ** ADAM_STEP)
    v_hat = v / (1.0 - ADAM_B2 ** ADAM_STEP)
    delta = -ADAM_LR * (m_hat / (_jnp.sqrt(v_hat) + ADAM_EPS) + ADAM_WD * w)
    return delta, m, v


def reference(x, norm1_w, w_in, gdn_conv_w, gdn_A_log, gdn_dt_bias, gdn_out_norm_w, fox_f_bias, fox_q_norm_w, fox_k_norm_w, w_out, norm2_w, w_ffn_gate, w_ffn_up, w_ffn_down, final_norm_w, loss_target, m_norm1_w, m_w_in, m_gdn_conv_w, m_gdn_A_log, m_gdn_dt_bias, m_gdn_out_norm_w, m_fox_f_bias, m_fox_q_norm_w, m_fox_k_norm_w, m_w_out, m_norm2_w, m_w_ffn_gate, m_w_ffn_up, m_w_ffn_down, m_final_norm_w, v_norm1_w, v_w_in, v_gdn_conv_w, v_gdn_A_log, v_gdn_dt_bias, v_gdn_out_norm_w, v_fox_f_bias, v_fox_q_norm_w, v_fox_k_norm_w, v_w_out, v_norm2_w, v_w_ffn_gate, v_w_ffn_up, v_w_ffn_down, v_final_norm_w):
    given = dict(x=x, norm1_w=norm1_w, w_in=w_in, gdn_conv_w=gdn_conv_w, gdn_A_log=gdn_A_log, gdn_dt_bias=gdn_dt_bias, gdn_out_norm_w=gdn_out_norm_w, fox_f_bias=fox_f_bias, fox_q_norm_w=fox_q_norm_w, fox_k_norm_w=fox_k_norm_w, w_out=w_out, norm2_w=norm2_w, w_ffn_gate=w_ffn_gate, w_ffn_up=w_ffn_up, w_ffn_down=w_ffn_down, final_norm_w=final_norm_w, loss_target=loss_target, m_norm1_w=m_norm1_w, m_w_in=m_w_in, m_gdn_conv_w=m_gdn_conv_w, m_gdn_A_log=m_gdn_A_log, m_gdn_dt_bias=m_gdn_dt_bias, m_gdn_out_norm_w=m_gdn_out_norm_w, m_fox_f_bias=m_fox_f_bias, m_fox_q_norm_w=m_fox_q_norm_w, m_fox_k_norm_w=m_fox_k_norm_w, m_w_out=m_w_out, m_norm2_w=m_norm2_w, m_w_ffn_gate=m_w_ffn_gate, m_w_ffn_up=m_w_ffn_up, m_w_ffn_down=m_w_ffn_down, m_final_norm_w=m_final_norm_w, v_norm1_w=v_norm1_w, v_w_in=v_w_in, v_gdn_conv_w=v_gdn_conv_w, v_gdn_A_log=v_gdn_A_log, v_gdn_dt_bias=v_gdn_dt_bias, v_gdn_out_norm_w=v_gdn_out_norm_w, v_fox_f_bias=v_fox_f_bias, v_fox_q_norm_w=v_fox_q_norm_w, v_fox_k_norm_w=v_fox_k_norm_w, v_w_out=v_w_out, v_norm2_w=v_norm2_w, v_w_ffn_gate=v_w_ffn_gate, v_w_ffn_up=v_w_ffn_up, v_w_ffn_down=v_w_ffn_down, v_final_norm_w=v_final_norm_w)
    weights = {n: given[n] for n in TWIN_WEIGHTS}
    shared = {n: given[n] for n in SHARED_INPUTS}
    per_example = {n: given[n] for n in ['x']}
    grad_fn = _jax.value_and_grad(_loss, argnums=(0, 1))

    def one_microbatch(ex, loss_target):
        ex = dict(ex)
        diff = ex.pop(TWIN_DIFF_INPUT)
        return grad_fn(weights, diff, {**shared, **ex}, loss_target)

    if N_MICROBATCH == 1:
        loss, (grad_w, grad_x) = one_microbatch(per_example, given["loss_target"])
    else:
        def body(carry, xs):
            loss_sum, grad_sum = carry
            l_k, (gw_k, gx_k) = one_microbatch(xs[0], xs[1])
            with _jax.named_scope("update"):
                return (loss_sum + l_k, _jax.tree.map(_jnp.add, grad_sum, gw_k)), gx_k

        init = (_jnp.zeros((), _jnp.float32), _jax.tree.map(_jnp.zeros_like, weights))
        (loss, grad_w), grad_x = _jax.lax.scan(body, init, (per_example, given["loss_target"]))
    with _jax.named_scope("update"):
        delta_w, new_m, new_v = {}, {}, {}
        for n in TWIN_WEIGHTS:
            delta_w[n], new_m[n], new_v[n] = _adamw(weights[n], grad_w[n], given["m_" + n], given["v_" + n])
    return (loss, grad_x, *[grad_w[n] for n in TWIN_WEIGHTS], *[delta_w[n] for n in TWIN_WEIGHTS],
            *[new_m[n] for n in TWIN_WEIGHTS], *[new_v[n] for n in TWIN_WEIGHTS])
```

```python
import functools

import jax
import jax.numpy as jnp
from jax import lax
from jax.experimental import pallas as pl
from jax.experimental.pallas import tpu as pltpu

F32 = jnp.float32
MXU = jnp.bfloat16
WIRE = jnp.bfloat16
HI = lax.Precision.HIGHEST
EPS = 1e-6

N_DEV = 8
HEADS = 8
DH = 64
WIDTH = HEADS * DH
CHUNK = 64
LANE = 128
ROW_ALIGN = 16
TB = 256
QB = 256
VMEM_LIMIT = 60 * 1024 * 1024

ADAM_LR = 0.001
ADAM_B1 = 0.9
ADAM_B2 = 0.999
ADAM_EPS = 1e-08
ADAM_WD = 0.01
ADAM_STEP = 10

MESH = pl.DeviceIdType.MESH


def _params(sem=None):
    return pltpu.CompilerParams(dimension_semantics=sem, vmem_limit_bytes=VMEM_LIMIT)


def _resident(shape):
    n = len(shape)
    return pl.BlockSpec(shape, lambda *_: (0,) * n, pipeline_mode=pl.Buffered(1))


def _dot(a, b):
    return jnp.dot(a.astype(MXU), b.astype(MXU), preferred_element_type=F32)


def _dot_nt(a, b):
    return lax.dot_general(a.astype(MXU), b.astype(MXU), (((1,), (1,)), ((), ())), preferred_element_type=F32)


def _dot_tn(a, b):
    return lax.dot_general(a.astype(MXU), b.astype(MXU), (((0,), (0,)), ((), ())), preferred_element_type=F32)


def _hdot(a, b):
    return jnp.dot(a, b, precision=HI, preferred_element_type=F32)


def _hdot_nt(a, b):
    return lax.dot_general(a, b, (((1,), (1,)), ((), ())), precision=HI, preferred_element_type=F32)


def _hdot_tn(a, b):
    return lax.dot_general(a, b, (((0,), (0,)), ((), ())), precision=HI, preferred_element_type=F32)


def _sigmoid(x):
    return 1.0 / (1.0 + jnp.exp(-x))


def _softplus(x):
    return jnp.maximum(x, 0.0) + jnp.log(1.0 + jnp.exp(-jnp.abs(x)))


def _head_sum_matrix():
    ri = lax.broadcasted_iota(jnp.int32, (LANE, LANE), 0) // DH
    ci = lax.broadcasted_iota(jnp.int32, (LANE, LANE), 1) // DH
    return (ri == ci).astype(F32)


def _shift_down(x, s, rows):
    return jnp.where(rows >= s, pltpu.roll(x, s, 0), 0.0)


def _shift_up(x, s, rows):
    t = x.shape[0]
    return jnp.where(rows < t - s, pltpu.roll(x, t - s, 0), 0.0)


def _inproj(x, n1w, w_main, w_small):
    t, d = x.shape
    nm = w_main.shape[1]
    tb = min(TB, t)

    def body(x_ref, nw_ref, wm_ref, ws_ref, h_ref, pm_ref, ps_ref):
        xv = x_ref[...]
        r = lax.rsqrt(jnp.mean(xv * xv, axis=-1, keepdims=True) + EPS)
        h = (xv * r * nw_ref[...]).astype(MXU)
        h_ref[...] = h
        pm_ref[...] = jnp.dot(h, wm_ref[...], preferred_element_type=F32)
        ps_ref[...] = jnp.dot(h, ws_ref[...], preferred_element_type=F32)

    return pl.pallas_call(
        body, name="inproj", grid=(t // tb,),
        in_specs=[pl.BlockSpec((tb, d), lambda i: (i, 0)), _resident((1, d)), _resident((d, nm)), _resident((d, LANE))],
        out_specs=[pl.BlockSpec((tb, d), lambda i: (i, 0)), pl.BlockSpec((tb, nm), lambda i: (i, 0)),
                   pl.BlockSpec((tb, LANE), lambda i: (i, 0))],
        out_shape=[jax.ShapeDtypeStruct((t, d), MXU), jax.ShapeDtypeStruct((t, nm), F32),
                   jax.ShapeDtypeStruct((t, LANE), F32)],
        compiler_params=_params(("arbitrary",)),
    )(x, n1w, w_main, w_small)


def _inproj_bwd(x, n1w, dx2, dgdn, dz, dfox, dfg, dps, w_main, w_small):
    t, d = x.shape
    tb = min(TB, t)
    w3 = 3 * WIDTH

    def body(x_ref, nw_ref, dx2_ref, dgdn_ref, dz_ref, dfox_ref, dfg_ref, dps_ref, wm_ref, ws_ref, gx_ref, dnw_ref):
        dh = _dot_nt(dgdn_ref[...], wm_ref[:, 0:w3])
        dh += _dot_nt(dz_ref[...], wm_ref[:, w3:w3 + WIDTH])
        dh += _dot_nt(dfox_ref[...], wm_ref[:, w3 + WIDTH:2 * w3 + WIDTH])
        dh += _dot_nt(dfg_ref[...], wm_ref[:, 2 * w3 + WIDTH:2 * w3 + 2 * WIDTH])
        dh += _dot_nt(dps_ref[...], ws_ref[...])
        xv = x_ref[...]
        r = lax.rsqrt(jnp.mean(xv * xv, axis=-1, keepdims=True) + EPS)
        xn = xv * r

        @pl.when(pl.program_id(0) == 0)
        def _():
            dnw_ref[...] = jnp.zeros_like(dnw_ref)

        dnw_ref[...] += jnp.sum(dh * xn, axis=0, keepdims=True)
        g = dh * nw_ref[...]
        gx_ref[...] = dx2_ref[...] + r * (g - xn * jnp.mean(g * xn, axis=-1, keepdims=True))

    def tok(n):
        return pl.BlockSpec((tb, n), lambda i: (i, 0))

    return pl.pallas_call(
        body, name="inproj_bwd", grid=(t // tb,),
        in_specs=[tok(d), _resident((1, d)), tok(d), tok(w3), tok(WIDTH), tok(w3), tok(WIDTH), tok(LANE),
                  _resident(w_main.shape), _resident(w_small.shape)],
        out_specs=[tok(d), pl.BlockSpec((1, d), lambda i: (0, 0))],
        out_shape=[jax.ShapeDtypeStruct((t, d), F32), jax.ShapeDtypeStruct((1, d), F32)],
        compiler_params=_params(("arbitrary",)),
    )(x, n1w, dx2, dgdn, dz, dfox, dfg, dps, w_main, w_small)


def _gate_lanes(shape):
    lane = lax.broadcasted_iota(jnp.int32, shape, 1)
    return lane < HEADS, (lane >= HEADS) & (lane < 2 * HEADS), (lane >= 2 * HEADS) & (lane < 3 * HEADS)


def _gates(ps, gparams):
    t = ps.shape[0]
    nb = t // LANE

    def body(ps_ref, gp_ref, out_ref, f_ref):
        p = ps_ref[...]
        is_b, is_a, is_f = _gate_lanes(p.shape)
        z = p + gp_ref[0:1, :]
        neg_exp_a = -jnp.exp(gp_ref[1:2, :])
        glog = neg_exp_a * _softplus(z)
        logf = -_softplus(-z)
        out_ref[...] = jnp.where(is_b, _sigmoid(p), jnp.where(is_a, glog, jnp.where(is_f, logf, 0.0)))
        ri = lax.broadcasted_iota(jnp.int32, (LANE, LANE), 0)
        ci = lax.broadcasted_iota(jnp.int32, (LANE, LANE), 1)
        tril = (ri >= ci).astype(F32)
        off = jnp.zeros((1, LANE), F32)
        for b in range(nb):
            blk = out_ref[b * LANE:(b + 1) * LANE, :]
            cb = _hdot(tril, jnp.where(is_f[:LANE], blk, 0.0)) + off
            f_ref[b * LANE:(b + 1) * LANE, :] = cb
            off = cb[LANE - 1:LANE, :]

    return pl.pallas_call(
        body, name="gates",
        out_shape=[jax.ShapeDtypeStruct((t, LANE), F32), jax.ShapeDtypeStruct((t, LANE), F32)],
        compiler_params=_params(),
    )(ps, gparams)


def _gates_bwd(ps, gparams, d1, d2):
    t = ps.shape[0]
    nb = t // LANE

    def body(ps_ref, gp_ref, d1_ref, d2_ref, dps_ref, sums_ref, dlf_ref):
        p = ps_ref[...]
        is_b, is_a, is_f = _gate_lanes(p.shape)
        d = d1_ref[...] + d2_ref[...]
        ri = lax.broadcasted_iota(jnp.int32, (LANE, LANE), 0)
        ci = lax.broadcasted_iota(jnp.int32, (LANE, LANE), 1)
        triu = (ri <= ci).astype(F32)
        off = jnp.zeros((1, LANE), F32)
        for b in reversed(range(nb)):
            blk = jnp.where(is_f[:LANE], d1_ref[b * LANE:(b + 1) * LANE, :] + d2_ref[b * LANE:(b + 1) * LANE, :], 0.0)
            cb = _hdot(triu, blk) + off
            dlf_ref[b * LANE:(b + 1) * LANE, :] = cb
            off = cb[0:1, :]
        z = p + gp_ref[0:1, :]
        neg_exp_a = -jnp.exp(gp_ref[1:2, :])
        sb = _sigmoid(p)
        glog = neg_exp_a * _softplus(z)
        dp = jnp.where(is_b, d * sb * (1.0 - sb),
                       jnp.where(is_a, d * neg_exp_a * _sigmoid(z),
                                 jnp.where(is_f, dlf_ref[...] * _sigmoid(-z), 0.0)))
        dps_ref[...] = dp
        sums_ref[...] = jnp.zeros_like(sums_ref)
        sums_ref[0:1, :] = jnp.sum(jnp.where(is_a, d * glog, 0.0), axis=0, keepdims=True)
        sums_ref[1:2, :] = jnp.sum(jnp.where(is_b, 0.0, dp), axis=0, keepdims=True)

    return pl.pallas_call(
        body, name="gates_bwd",
        out_shape=[jax.ShapeDtypeStruct((t, LANE), F32), jax.ShapeDtypeStruct((8, LANE), F32)],
        scratch_shapes=[pltpu.VMEM((t, LANE), F32)],
        compiler_params=_params(),
    )(ps, gparams, d1, d2)


def _conv(xv, w, rows):
    acc = w[3:4, :] * xv
    for s in range(1, 4):
        acc += w[3 - s:4 - s, :] * _shift_down(xv, s, rows)
    return acc


def _gdn_prep(pm, conv_w):
    t = pm.shape[0]
    nj = WIDTH // LANE

    def body(x_ref, w_ref, o_ref):
        kind = pl.program_id(0)
        xv = x_ref[...]
        rows = lax.broadcasted_iota(jnp.int32, xv.shape, 0)
        acc = _conv(xv, w_ref[...], rows)
        y = acc * _sigmoid(acc)
        ss = _hdot(y * y, _head_sum_matrix())
        out = jnp.where(kind < 2, y * lax.rsqrt(ss + EPS), y)
        o_ref[0, 0] = out[:, :DH]
        o_ref[0, 1] = out[:, DH:]

    return pl.pallas_call(
        body, name="gdn_prep", grid=(3, nj),
        in_specs=[pl.BlockSpec((t, LANE), lambda i, j: (0, i * nj + j)),
                  pl.BlockSpec((4, LANE), lambda i, j: (0, i * nj + j))],
        out_specs=pl.BlockSpec((1, 2, t, DH), lambda i, j: (i, j, 0, 0)),
        out_shape=jax.ShapeDtypeStruct((3, HEADS, t, DH), F32),
        compiler_params=_params(("arbitrary", "arbitrary")),
    )(pm, conv_w)


def _gdn_prep_bwd(pm, conv_w, dqkv):
    t = pm.shape[0]
    nj = WIDTH // LANE

    def body(x_ref, w_ref, d_ref, dx_ref, dw_ref):
        kind = pl.program_id(0)
        xv = x_ref[...]
        w = w_ref[...]
        rows = lax.broadcasted_iota(jnp.int32, xv.shape, 0)
        acc = _conv(xv, w, rows)
        sg = _sigmoid(acc)
        y = acc * sg
        hs = _head_sum_matrix()
        r = lax.rsqrt(_hdot(y * y, hs) + EPS)
        yn = y * r
        d = jnp.concatenate([d_ref[0, 0], d_ref[0, 1]], axis=1)
        dy = jnp.where(kind < 2, r * (d - yn * _hdot(d * yn, hs)), d)
        dacc = dy * sg * (1.0 + acc * (1.0 - sg))
        dx = w[3:4, :] * dacc
        dw_ref[3:4, :] = jnp.sum(dacc * xv, axis=0, keepdims=True)
        for s in range(1, 4):
            dx += w[3 - s:4 - s, :] * _shift_up(dacc, s, rows)
            dw_ref[3 - s:4 - s, :] = jnp.sum(dacc * _shift_down(xv, s, rows), axis=0, keepdims=True)
        dx_ref[...] = dx

    return pl.pallas_call(
        body, name="gdn_prep_bwd", grid=(3, nj),
        in_specs=[pl.BlockSpec((t, LANE), lambda i, j: (0, i * nj + j)),
                  pl.BlockSpec((4, LANE), lambda i, j: (0, i * nj + j)),
                  pl.BlockSpec((1, 2, t, DH), lambda i, j: (i, j, 0, 0))],
        out_specs=[pl.BlockSpec((t, LANE), lambda i, j: (0, i * nj + j)),
                   pl.BlockSpec((4, LANE), lambda i, j: (0, i * nj + j))],
        out_shape=[jax.ShapeDtypeStruct((t, 3 * WIDTH), F32), jax.ShapeDtypeStruct((4, 3 * WIDTH), F32)],
        compiler_params=_params(("arbitrary", "arbitrary")),
    )(pm, conv_w, dqkv)


FOX_COL0 = 4 * WIDTH // LANE


def _fox_prep(pm, nw):
    t = pm.shape[0]
    nj = WIDTH // LANE

    def body(x_ref, w_ref, o_ref):
        kind = pl.program_id(0)
        xv = x_ref[...]
        ms = _hdot(xv * xv, _head_sum_matrix()) * (1.0 / DH)
        yn = xv * lax.rsqrt(ms + EPS) * w_ref[pl.ds(kind, 1), :]
        out = jnp.where(kind < 2, yn, xv)
        o_ref[0, 0] = out[:, :DH]
        o_ref[0, 1] = out[:, DH:]

    return pl.pallas_call(
        body, name="fox_prep", grid=(3, nj),
        in_specs=[pl.BlockSpec((t, LANE), lambda i, j: (0, FOX_COL0 + i * nj + j)),
                  pl.BlockSpec((3, LANE), lambda i, j: (0, 0))],
        out_specs=pl.BlockSpec((1, 2, t, DH), lambda i, j: (i, j, 0, 0)),
        out_shape=jax.ShapeDtypeStruct((3, HEADS, t, DH), F32),
        compiler_params=_params(("arbitrary", "arbitrary")),
    )(pm, nw)


def _fox_prep_bwd(pm, nw, dqkv):
    t = pm.shape[0]
    nj = WIDTH // LANE

    def body(x_ref, w_ref, d_ref, dx_ref, dw_ref):
        kind = pl.program_id(0)
        xv = x_ref[...]
        hs = _head_sum_matrix()
        r = lax.rsqrt(_hdot(xv * xv, hs) * (1.0 / DH) + EPS)
        xn = xv * r
        d = jnp.concatenate([d_ref[0, 0], d_ref[0, 1]], axis=1)
        g = d * w_ref[pl.ds(kind, 1), :]
        dxn = r * (g - xn * _hdot(g * xn, hs) * (1.0 / DH))
        dx_ref[...] = jnp.where(kind < 2, dxn, d)
        dw_ref[0, 0] = jnp.sum(d * xn, axis=0, keepdims=True)

    return pl.pallas_call(
        body, name="fox_prep_bwd", grid=(3, nj),
        in_specs=[pl.BlockSpec((t, LANE), lambda i, j: (0, FOX_COL0 + i * nj + j)),
                  pl.BlockSpec((3, LANE), lambda i, j: (0, 0)),
                  pl.BlockSpec((1, 2, t, DH), lambda i, j: (i, j, 0, 0))],
        out_specs=[pl.BlockSpec((t, LANE), lambda i, j: (0, i * nj + j)),
                   pl.BlockSpec((1, 1, 1, LANE), lambda i, j: (i, j, 0, 0))],
        out_shape=[jax.ShapeDtypeStruct((t, 3 * WIDTH), F32), jax.ShapeDtypeStruct((3, nj, 1, LANE), F32)],
        compiler_params=_params(("arbitrary", "arbitrary")),
    )(pm, nw, dqkv)


def _gdn_chunk(q, k, v, beta, g):
    c = CHUNK
    ri = lax.broadcasted_iota(jnp.int32, (c, c), 0)
    ci = lax.broadcasted_iota(jnp.int32, (c, c), 1)
    tril = (ri >= ci).astype(F32)
    gc = _hdot(tril, g)
    gr = lax.dot_general(g, tril, (((0,), (1,)), ((), ())), precision=HI, preferred_element_type=F32)
    decay = jnp.exp(jnp.where(ri >= ci, gc - gr, -1e30))
    qs = q * (DH ** -0.5)
    kb = k * beta
    m = -jnp.where(ri > ci, _hdot_nt(kb, k) * decay, 0.0)
    inv = jnp.where(ri == ci, 1.0, 0.0) + m
    for _ in range(5):
        m = _hdot(m, m)
        inv = inv + _hdot(inv, m)
    egc = jnp.exp(gc)
    u = _hdot(inv, v * beta)
    w = _hdot(inv, kb * egc)
    attn = jnp.where(ri >= ci, _hdot_nt(qs, k) * decay, 0.0)
    g_last = jnp.sum(g, axis=0, keepdims=True)
    kd = k * jnp.exp(g_last - gc)
    return u, w, attn, qs * egc, kd, jnp.broadcast_to(jnp.exp(g_last), (1, DH))


def _chunk_rows(c):
    return pl.ds(pl.multiple_of(c * CHUNK, CHUNK), CHUNK)


def _gdn_fwd(qkv, beta, g):
    t = qkv.shape[2]
    nc = t // CHUNK

    def body(qkv_ref, b_ref, g_ref, o_ref, st_ref, u_s, w_s, a_s, qg_s, kd_s, eg_s):
        def prep(c, _):
            rows = _chunk_rows(c)
            outs = _gdn_chunk(qkv_ref[0, 0, rows, :], qkv_ref[1, 0, rows, :], qkv_ref[2, 0, rows, :],
                              b_ref[0, rows, :], g_ref[0, rows, :])
            for ref, val in zip((u_s, w_s, a_s, qg_s, kd_s, eg_s), outs):
                ref[c] = val
            return 0

        lax.fori_loop(0, nc, prep, 0, unroll=2)

        def step(c, s):
            st_ref[0, c] = s
            vn = u_s[c] - _hdot(w_s[c], s)
            o_ref[0, _chunk_rows(c), :] = _hdot(qg_s[c], s) + _hdot(a_s[c], vn)
            return s * eg_s[c] + _hdot_tn(kd_s[c], vn)

        lax.fori_loop(0, nc, step, jnp.zeros((DH, DH), F32))

    sq = pltpu.VMEM((nc, CHUNK, DH), F32)
    return pl.pallas_call(
        body, name="gdn_fwd", grid=(HEADS,),
        in_specs=[pl.BlockSpec((3, 1, t, DH), lambda h: (0, h, 0, 0)),
                  pl.BlockSpec((1, t, 1), lambda h: (h, 0, 0)), pl.BlockSpec((1, t, 1), lambda h: (h, 0, 0))],
        out_specs=[pl.BlockSpec((1, t, DH), lambda h: (h, 0, 0)),
                   pl.BlockSpec((1, nc, DH, DH), lambda h: (h, 0, 0, 0))],
        out_shape=[jax.ShapeDtypeStruct((HEADS, t, DH), F32), jax.ShapeDtypeStruct((HEADS, nc, DH, DH), F32)],
        scratch_shapes=[sq, sq, pltpu.VMEM((nc, CHUNK, CHUNK), F32), sq, sq, pltpu.VMEM((nc, 1, DH), F32)],
        compiler_params=_params(("arbitrary",)),
    )(qkv, beta, g)


def _gdn_bwd(qkv, beta, g, states, do):
    t = qkv.shape[2]
    nc = t // CHUNK

    def body(qkv_ref, b_ref, g_ref, st_ref, do_ref, dqkv_ref, db_ref, dg_ref,
             u_s, w_s, a_s, qg_s, kd_s, eg_s, du_s, dw_s, da_s, dqg_s, dkd_s, deg_s):
        def chunk_inputs(c):
            rows = _chunk_rows(c)
            return (qkv_ref[0, 0, rows, :], qkv_ref[1, 0, rows, :], qkv_ref[2, 0, rows, :],
                    b_ref[0, rows, :], g_ref[0, rows, :])

        def prep(c, _):
            for ref, val in zip((u_s, w_s, a_s, qg_s, kd_s, eg_s), _gdn_chunk(*chunk_inputs(c))):
                ref[c] = val
            return 0

        lax.fori_loop(0, nc, prep, 0, unroll=2)

        def step(i, ds):
            c = nc - 1 - i
            s = st_ref[0, c]
            dout = do_ref[0, _chunk_rows(c), :]
            vn = u_s[c] - _hdot(w_s[c], s)
            dvn = _hdot_tn(a_s[c], dout) + _hdot(kd_s[c], ds)
            da_s[c] = _hdot_nt(dout, vn)
            dqg_s[c] = _hdot_nt(dout, s)
            dkd_s[c] = _hdot_nt(vn, ds)
            deg_s[c] = jnp.sum(s * ds, axis=0, keepdims=True)
            du_s[c] = dvn
            dw_s[c] = -_hdot_nt(dvn, s)
            return ds * eg_s[c] + _hdot_tn(qg_s[c], dout) - _hdot_tn(w_s[c], dvn)

        lax.fori_loop(0, nc, step, jnp.zeros((DH, DH), F32))

        def back(c, _):
            rows = _chunk_rows(c)
            _, vjp = jax.vjp(_gdn_chunk, *chunk_inputs(c))
            dq, dk, dv, db, dg = vjp((du_s[c], dw_s[c], da_s[c], dqg_s[c], dkd_s[c], deg_s[c]))
            dqkv_ref[0, 0, rows, :] = dq
            dqkv_ref[1, 0, rows, :] = dk
            dqkv_ref[2, 0, rows, :] = dv
            db_ref[0, rows, :] = db
            dg_ref[0, rows, :] = dg
            return 0

        lax.fori_loop(0, nc, back, 0, unroll=2)

    sq = pltpu.VMEM((nc, CHUNK, DH), F32)
    sa = pltpu.VMEM((nc, CHUNK, CHUNK), F32)
    se = pltpu.VMEM((nc, 1, DH), F32)
    col = pl.BlockSpec((1, t, 1), lambda h: (h, 0, 0))
    return pl.pallas_call(
        body, name="gdn_bwd", grid=(HEADS,),
        in_specs=[pl.BlockSpec((3, 1, t, DH), lambda h: (0, h, 0, 0)), col, col,
                  pl.BlockSpec((1, nc, DH, DH), lambda h: (h, 0, 0, 0)),
                  pl.BlockSpec((1, t, DH), lambda h: (h, 0, 0))],
        out_specs=[pl.BlockSpec((3, 1, t, DH), lambda h: (0, h, 0, 0)), col, col],
        out_shape=[jax.ShapeDtypeStruct((3, HEADS, t, DH), F32), jax.ShapeDtypeStruct((HEADS, t, 1), F32),
                   jax.ShapeDtypeStruct((HEADS, t, 1), F32)],
        scratch_shapes=[sq, sq, sa, sq, sq, se, sq, sq, sa, sq, sq, se],
        compiler_params=_params(("arbitrary",)),
    )(qkv, beta, g, states, do)


def _causal_mask(i, j, qb):
    qpos = i * qb + lax.broadcasted_iota(jnp.int32, (qb, qb), 0)
    kpos = j * qb + lax.broadcasted_iota(jnp.int32, (qb, qb), 1)
    return qpos >= kpos


def _fox_fwd(qkv, fcol, frow):
    t = qkv.shape[2]
    qb = min(QB, t)
    nq = t // qb
    scale = DH ** -0.5

    def body(q_ref, k_ref, v_ref, fc_ref, fr_ref, o_ref, lse_ref):
        i = pl.program_id(1)
        q = q_ref[0, 0].astype(MXU)
        fq = fc_ref[0]

        def kstep(j, carry):
            m, l, acc = carry
            rows = pl.ds(pl.multiple_of(j * qb, qb), qb)
            s = _dot_nt(q, k_ref[0, 0, rows, :]) * scale + (fq - fr_ref[0, j])
            s = jnp.where(_causal_mask(i, j, qb), s, -1e30)
            m_new = jnp.maximum(m, jnp.max(s, axis=1, keepdims=True))
            p = jnp.exp(s - m_new)
            alpha = jnp.exp(m - m_new)
            l = alpha * l + jnp.sum(p, axis=1, keepdims=True)
            acc = alpha * acc + _dot(p, v_ref[0, 0, rows, :])
            return m_new, l, acc

        init = (jnp.full((qb, 1), -1e30, F32), jnp.zeros((qb, 1), F32), jnp.zeros((qb, DH), F32))
        m, l, acc = lax.fori_loop(0, i + 1, kstep, init)
        o_ref[0] = acc / l
        lse_ref[0] = m + jnp.log(l)

    return pl.pallas_call(
        body, name="fox_fwd", grid=(HEADS, nq),
        in_specs=[pl.BlockSpec((1, 1, qb, DH), lambda h, i: (0, h, i, 0)),
                  pl.BlockSpec((1, 1, t, DH), lambda h, i: (1, h, 0, 0)),
                  pl.BlockSpec((1, 1, t, DH), lambda h, i: (2, h, 0, 0)),
                  pl.BlockSpec((1, qb, 1), lambda h, i: (h, i, 0)),
                  pl.BlockSpec((1, nq, 1, qb), lambda h, i: (h, 0, 0, 0))],
        out_specs=[pl.BlockSpec((1, qb, DH), lambda h, i: (h, i, 0)), pl.BlockSpec((1, qb, 1), lambda h, i: (h, i, 0))],
        out_shape=[jax.ShapeDtypeStruct((HEADS, t, DH), F32), jax.ShapeDtypeStruct((HEADS, t, 1), F32)],
        compiler_params=_params(("arbitrary", "arbitrary")),
    )(qkv, qkv, qkv, fcol, frow)


def _fox_bwd(qkv, fcol, frow, o, lse, do):
    t = qkv.shape[2]
    qb = min(QB, t)
    nq = t // qb
    scale = DH ** -0.5

    def body(q_ref, k_ref, v_ref, fc_ref, fr_ref, o_ref, lse_ref, do_ref, dq_ref, dk_ref, dv_ref, dfq_ref, dfk_ref):
        j = pl.program_id(1)

        @pl.when(j == 0)
        def _():
            dq_ref[...] = jnp.zeros_like(dq_ref)
            dfq_ref[...] = jnp.zeros_like(dfq_ref)

        k = k_ref[0, 0].astype(MXU)
        v = v_ref[0, 0].astype(MXU)
        fk = fr_ref[0, 0]

        def qstep(i, carry):
            dk, dv, dfk = carry
            rows = pl.ds(pl.multiple_of(i * qb, qb), qb)
            q = q_ref[0, 0, rows, :].astype(MXU)
            dout = do_ref[0, rows, :]
            s = _dot_nt(q, k) * scale + (fc_ref[0, rows, :] - fk)
            s = jnp.where(_causal_mask(i, j, qb), s, -1e30)
            p = jnp.exp(s - lse_ref[0, rows, :])
            dp = _dot_nt(dout, v)
            drow = jnp.sum(dout * o_ref[0, rows, :], axis=1, keepdims=True)
            ds = p * (dp - drow)
            dq_ref[0, 0, rows, :] += _dot(ds, k) * scale
            dfq_ref[0, rows, :] += jnp.sum(ds, axis=1, keepdims=True)
            return (dk + _dot_tn(ds, q) * scale, dv + _dot_tn(p, dout), dfk - jnp.sum(ds, axis=0, keepdims=True))

        init = (jnp.zeros((qb, DH), F32), jnp.zeros((qb, DH), F32), jnp.zeros((1, qb), F32))
        dk, dv, dfk = lax.fori_loop(j, nq, qstep, init)
        dk_ref[0, 0] = dk
        dv_ref[0, 0] = dv
        dfk_ref[0, 0] = dfk

    full = pl.BlockSpec((1, t, DH), lambda h, j: (h, 0, 0))
    colf = pl.BlockSpec((1, t, 1), lambda h, j: (h, 0, 0))
    return pl.pallas_call(
        body, name="fox_bwd", grid=(HEADS, nq),
        in_specs=[pl.BlockSpec((1, 1, t, DH), lambda h, j: (0, h, 0, 0)),
                  pl.BlockSpec((1, 1, qb, DH), lambda h, j: (1, h, j, 0)),
                  pl.BlockSpec((1, 1, qb, DH), lambda h, j: (2, h, j, 0)),
                  colf, pl.BlockSpec((1, 1, 1, qb), lambda h, j: (h, j, 0, 0)), full, colf, full],
        out_specs=[pl.BlockSpec((1, 1, t, DH), lambda h, j: (0, h, 0, 0)),
                   pl.BlockSpec((1, 1, qb, DH), lambda h, j: (0, h, j, 0)),
                   pl.BlockSpec((1, 1, qb, DH), lambda h, j: (0, h, j, 0)),
                   colf, pl.BlockSpec((1, 1, 1, qb), lambda h, j: (h, j, 0, 0))],
        out_shape=[jax.ShapeDtypeStruct((1, HEADS, t, DH), F32), jax.ShapeDtypeStruct((1, HEADS, t, DH), F32),
                   jax.ShapeDtypeStruct((1, HEADS, t, DH), F32), jax.ShapeDtypeStruct((HEADS, t, 1), F32),
                   jax.ShapeDtypeStruct((HEADS, nq, 1, qb), F32)],
        compiler_params=_params(("arbitrary", "arbitrary")),
    )(qkv, qkv, qkv, fcol, frow, o, lse, do)


Z_COL0 = 3 * WIDTH // LANE
FGATE_COL0 = 7 * WIDTH // LANE


def _gdn_post(o, pm, onw):
    t = pm.shape[0]

    def body(o_ref, z_ref, w_ref, m_ref):
        z = z_ref[...]
        sz = z * _sigmoid(z)
        for hh in range(2):
            ov = o_ref[hh]
            n = ov * lax.rsqrt(jnp.mean(ov * ov, axis=-1, keepdims=True) + EPS) * w_ref[...]
            m_ref[:, hh * DH:(hh + 1) * DH] = (n * sz[:, hh * DH:(hh + 1) * DH]).astype(m_ref.dtype)

    return pl.pallas_call(
        body, name="gdn_post", grid=(WIDTH // LANE,),
        in_specs=[pl.BlockSpec((2, t, DH), lambda j: (j, 0, 0)), pl.BlockSpec((t, LANE), lambda j: (0, Z_COL0 + j)),
                  pl.BlockSpec((1, DH), lambda j: (0, 0))],
        out_specs=pl.BlockSpec((t, LANE), lambda j: (0, j)),
        out_shape=jax.ShapeDtypeStruct((t, WIDTH), MXU),
        compiler_params=_params(("arbitrary",)),
    )(o, pm, onw)


def _gdn_post_bwd(o, pm, onw, dmix):
    t = pm.shape[0]

    def body(o_ref, z_ref, w_ref, dm_ref, do_ref, dz_ref, dw_ref):
        @pl.when(pl.program_id(0) == 0)
        def _():
            dw_ref[...] = jnp.zeros_like(dw_ref)

        z = z_ref[...]
        sg = _sigmoid(z)
        sz = z * sg
        dsz = sg * (1.0 + z * (1.0 - sg))
        dm = dm_ref[...]
        for hh in range(2):
            cols = slice(hh * DH, (hh + 1) * DH)
            ov = o_ref[hh]
            r = lax.rsqrt(jnp.mean(ov * ov, axis=-1, keepdims=True) + EPS)
            xn = ov * r
            dmh = dm[:, cols]
            dn = dmh * sz[:, cols]
            dz_ref[:, cols] = dmh * (xn * w_ref[...]) * dsz[:, cols]
            dw_ref[...] += jnp.sum(dn * xn, axis=0, keepdims=True)
            g = dn * w_ref[...]
            do_ref[hh] = r * (g - xn * jnp.mean(g * xn, axis=-1, keepdims=True))

    return pl.pallas_call(
        body, name="gdn_post_bwd", grid=(WIDTH // LANE,),
        in_specs=[pl.BlockSpec((2, t, DH), lambda j: (j, 0, 0)), pl.BlockSpec((t, LANE), lambda j: (0, Z_COL0 + j)),
                  pl.BlockSpec((1, DH), lambda j: (0, 0)), pl.BlockSpec((t, LANE), lambda j: (0, j))],
        out_specs=[pl.BlockSpec((2, t, DH), lambda j: (j, 0, 0)), pl.BlockSpec((t, LANE), lambda j: (0, j)),
                   pl.BlockSpec((1, DH), lambda j: (0, 0))],
        out_shape=[jax.ShapeDtypeStruct((HEADS, t, DH), F32), jax.ShapeDtypeStruct((t, WIDTH), F32),
                   jax.ShapeDtypeStruct((1, DH), F32)],
        compiler_params=_params(("arbitrary",)),
    )(o, pm, onw, dmix)


def _fox_post(o, pm):
    t = pm.shape[0]

    def body(o_ref, g_ref, m_ref):
        sg = _sigmoid(g_ref[...])
        for hh in range(2):
            cols = slice(hh * DH, (hh + 1) * DH)
            m_ref[:, cols] = (o_ref[hh] * sg[:, cols]).astype(m_ref.dtype)

    return pl.pallas_call(
        body, name="fox_post", grid=(WIDTH // LANE,),
        in_specs=[pl.BlockSpec((2, t, DH), lambda j: (j, 0, 0)), pl.BlockSpec((t, LANE), lambda j: (0, FGATE_COL0 + j))],
        out_specs=pl.BlockSpec((t, LANE), lambda j: (0, j)),
        out_shape=jax.ShapeDtypeStruct((t, WIDTH), MXU),
        compiler_params=_params(("arbitrary",)),
    )(o, pm)


def _fox_post_bwd(o, pm, dmix):
    t = pm.shape[0]

    def body(o_ref, g_ref, dm_ref, do_ref, dg_ref):
        sg = _sigmoid(g_ref[...])
        dm = dm_ref[...]
        for hh in range(2):
            cols = slice(hh * DH, (hh + 1) * DH)
            do_ref[hh] = dm[:, cols] * sg[:, cols]
            dg_ref[:, cols] = dm[:, cols] * o_ref[hh] * (sg * (1.0 - sg))[:, cols]

    return pl.pallas_call(
        body, name="fox_post_bwd", grid=(WIDTH // LANE,),
        in_specs=[pl.BlockSpec((2, t, DH), lambda j: (j, 0, 0)), pl.BlockSpec((t, LANE), lambda j: (0, FGATE_COL0 + j)),
                  pl.BlockSpec((t, LANE), lambda j: (0, j))],
        out_specs=[pl.BlockSpec((2, t, DH), lambda j: (j, 0, 0)), pl.BlockSpec((t, LANE), lambda j: (0, j))],
        out_shape=[jax.ShapeDtypeStruct((HEADS, t, DH), F32), jax.ShapeDtypeStruct((t, WIDTH), F32)],
        compiler_params=_params(("arbitrary",)),
    )(o, pm, dmix)


def _tail(x, mixg, mixf, tgt, wo, n2w, wg, wu, wd, fw):
    t, d = x.shape
    dff = wg.shape[1]
    tb = min(TB, t)

    def body(x_ref, mg_ref, mf_ref, t_ref, wo_ref, n2_ref, wg_ref, wu_ref, wd_ref, fw_ref,
             h2_ref, act_ref, dgate_ref, dup_ref, dx3_ref, dx2_ref, dmg_ref, dmf_ref, dn2_ref, dfw_ref, loss_ref):
        @pl.when(pl.program_id(0) == 0)
        def _():
            dn2_ref[...] = jnp.zeros_like(dn2_ref)
            dfw_ref[...] = jnp.zeros_like(dfw_ref)
            loss_ref[...] = jnp.zeros_like(loss_ref)

        x2 = x_ref[...] + _dot(mg_ref[...], wo_ref[0:WIDTH, :]) + _dot(mf_ref[...], wo_ref[WIDTH:2 * WIDTH, :])
        r2 = lax.rsqrt(jnp.mean(x2 * x2, axis=-1, keepdims=True) + EPS)
        xn2 = x2 * r2
        h2 = (xn2 * n2_ref[...]).astype(MXU)
        h2_ref[...] = h2
        gate = _dot(h2, wg_ref[...])
        up = _dot(h2, wu_ref[...])
        sg = _sigmoid(gate)
        sl = gate * sg
        act = (sl * up).astype(MXU)
        act_ref[...] = act
        x3 = x2 + _dot(act, wd_ref[...])
        r3 = lax.rsqrt(jnp.mean(x3 * x3, axis=-1, keepdims=True) + EPS)
        xn3 = x3 * r3
        err = xn3 * fw_ref[...] - t_ref[...]
        loss_ref[...] += 0.5 * jnp.sum(jnp.mean(err * err, axis=-1, keepdims=True), axis=0, keepdims=True)
        dy = err * (1.0 / d)
        dfw_ref[...] += jnp.sum(dy * xn3, axis=0, keepdims=True)
        g3 = dy * fw_ref[...]
        dx3 = r3 * (g3 - xn3 * jnp.mean(g3 * xn3, axis=-1, keepdims=True))
        dx3_ref[...] = dx3.astype(MXU)
        dact = _dot_nt(dx3, wd_ref[...])
        dgate = (dact * up * (sg * (1.0 + gate * (1.0 - sg)))).astype(MXU)
        dup = (dact * sl).astype(MXU)
        dgate_ref[...] = dgate
        dup_ref[...] = dup
        dh2 = _dot_nt(dgate, wg_ref[...]) + _dot_nt(dup, wu_ref[...])
        dn2_ref[...] += jnp.sum(dh2 * xn2, axis=0, keepdims=True)
        g2 = dh2 * n2_ref[...]
        dx2 = dx3 + r2 * (g2 - xn2 * jnp.mean(g2 * xn2, axis=-1, keepdims=True))
        dx2_ref[...] = dx2
        dmg_ref[...] = _dot_nt(dx2, wo_ref[0:WIDTH, :])
        dmf_ref[...] = _dot_nt(dx2, wo_ref[WIDTH:2 * WIDTH, :])

    def tok(n):
        return pl.BlockSpec((tb, n), lambda i: (i, 0))

    acc = pl.BlockSpec((1, d), lambda i: (0, 0))
    sds = jax.ShapeDtypeStruct
    return pl.pallas_call(
        body, name="tail", grid=(t // tb,),
        in_specs=[tok(d), tok(WIDTH), tok(WIDTH), tok(d), _resident(wo.shape), _resident((1, d)),
                  _resident(wg.shape), _resident(wu.shape), _resident(wd.shape), _resident((1, d))],
        out_specs=[tok(d), tok(dff), tok(dff), tok(dff), tok(d), tok(d), tok(WIDTH), tok(WIDTH), acc, acc,
                   pl.BlockSpec((1, 1), lambda i: (0, 0))],
        out_shape=[sds((t, d), MXU), sds((t, dff), MXU), sds((t, dff), MXU), sds((t, dff), MXU), sds((t, d), MXU),
                   sds((t, d), F32), sds((t, WIDTH), F32), sds((t, WIDTH), F32), sds((1, d), F32), sds((1, d), F32),
                   sds((1, 1), F32)],
        compiler_params=_params(("arbitrary",)),
    )(x, mixg, mixf, tgt, wo, n2w, wg, wu, wd, fw)


def _wgrad(a, b, name):
    t, m = a.shape
    n = b.shape[1]
    bn = 256 if n % 256 == 0 else LANE

    def body(a_ref, b_ref, o_ref):
        o_ref[...] = _dot_tn(a_ref[...], b_ref[...]).astype(o_ref.dtype)

    return pl.pallas_call(
        body, name=name, grid=(n // bn,),
        in_specs=[_resident((t, m)), pl.BlockSpec((t, bn), lambda j: (0, j))],
        out_specs=pl.BlockSpec((m, bn), lambda j: (0, j)),
        out_shape=jax.ShapeDtypeStruct((m, n), WIRE),
        compiler_params=_params(("arbitrary",)),
    )(a, b)


def _split_w_in(w_in):
    a = 4 * WIDTH
    b = a + 2 * HEADS
    c = b + 4 * WIDTH
    main = jnp.concatenate([w_in[:, :a], w_in[:, b:c]], axis=1)
    small = jnp.concatenate([w_in[:, a:b], w_in[:, c:], jnp.zeros((w_in.shape[0], LANE - 3 * HEADS), w_in.dtype)], axis=1)
    return main, small


def _merge_dw_in(d_gdn, d_z, d_fox, d_fg, d_small):
    return jnp.concatenate([d_gdn, d_z, d_small[:, :2 * HEADS], d_fox, d_fg, d_small[:, 2 * HEADS:3 * HEADS]], axis=1)


def _lanes(*pieces):
    v = jnp.concatenate([p.reshape(-1).astype(F32) for p in pieces])
    return jnp.pad(v, (0, LANE - v.shape[0])).reshape(1, LANE)


def _heads_to_cols(a):
    return a[:, :, 0].T


def _local_step(x, tgt, n1w, w_in, conv_w, a_log, dt_bias, onw, f_bias, qnw, knw, wo, n2w, wg, wu, wd, fw):
    t, d = x.shape
    qb = min(QB, t)
    w_main, w_small = _split_w_in(w_in)
    n1w, n2w, fw = n1w.reshape(1, d), n2w.reshape(1, d), fw.reshape(1, d)
    onw = onw.reshape(1, DH)
    gparams = jnp.concatenate([_lanes(jnp.zeros(HEADS), dt_bias, f_bias), _lanes(jnp.zeros(HEADS), a_log),
                               jnp.zeros((6, LANE), F32)])
    fox_nw = jnp.stack([jnp.tile(qnw.reshape(-1), 2), jnp.tile(knw.reshape(-1), 2), jnp.ones((LANE,), F32)])

    h1, pm, ps = _inproj(x, n1w, w_main, w_small)
    gates, fsum = _gates(ps, gparams)
    beta = gates[:, 0:HEADS].T[:, :, None]
    glog = gates[:, HEADS:2 * HEADS].T[:, :, None]
    fh = fsum[:, 2 * HEADS:3 * HEADS].T
    fcol = fh[:, :, None]
    frow = fh.reshape(HEADS, t // qb, 1, qb)
    gqkv = _gdn_prep(pm, conv_w)
    o_gdn, states = _gdn_fwd(gqkv, beta, glog)
    mixg = _gdn_post(o_gdn, pm, onw)
    fqkv = _fox_prep(pm, fox_nw)
    o_fox, lse = _fox_fwd(fqkv, fcol, frow)
    mixf = _fox_post(o_fox, pm)

    (h2, act, dgate, dup, dx3, dx2, dmixg, dmixf, dn2w, dfw, loss) = _tail(x, mixg, mixf, tgt, wo, n2w, wg, wu, wd, fw)

    do_gdn, dz, donw = _gdn_post_bwd(o_gdn, pm, onw, dmixg)
    dgqkv, dbeta, dglog = _gdn_bwd(gqkv, beta, glog, states, do_gdn)
    dgdn, dconv = _gdn_prep_bwd(pm, conv_w, dgqkv)
    do_fox, dfg = _fox_post_bwd(o_fox, pm, dmixf)
    dfq, dfk, dfv, dfc, dfr = _fox_bwd(fqkv, fcol, frow, o_fox, lse, do_fox)
    dfox, dfnw = _fox_prep_bwd(pm, fox_nw, jnp.concatenate([dfq, dfk, dfv], axis=0))
    zpad = jnp.zeros((t, LANE - 3 * HEADS), F32)
    d1 = jnp.concatenate([_heads_to_cols(dbeta), _heads_to_cols(dglog), _heads_to_cols(dfc), zpad], axis=1)
    d2 = jnp.concatenate([jnp.zeros((t, 2 * HEADS), F32), dfr.reshape(HEADS, t).T, zpad], axis=1)
    dps, gsum = _gates_bwd(ps, gparams, d1, d2)
    grad_x, dn1w = _inproj_bwd(x, n1w, dx2, dgdn, dz, dfox, dfg, dps, w_main, w_small)

    dw_in = _merge_dw_in(_wgrad(h1, dgdn, "dw_in_gdn"), _wgrad(h1, dz, "dw_in_z"), _wgrad(h1, dfox, "dw_in_fox"),
                         _wgrad(h1, dfg, "dw_in_fgate"), _wgrad(h1, dps, "dw_in_small"))
    dwo = jnp.concatenate([_wgrad(mixg, dx2, "dw_out_gdn"), _wgrad(mixf, dx2, "dw_out_fox")], axis=0)
    dfnw = dfnw.sum(axis=1).reshape(3, 2, DH).sum(axis=1)
    grads = {
        "norm1_w": dn1w, "w_in": dw_in, "gdn_conv_w": dconv,
        "gdn_A_log": gsum[0, HEADS:2 * HEADS], "gdn_dt_bias": gsum[1, HEADS:2 * HEADS],
        "gdn_out_norm_w": donw, "fox_f_bias": gsum[1, 2 * HEADS:3 * HEADS],
        "fox_q_norm_w": dfnw[0], "fox_k_norm_w": dfnw[1], "w_out": dwo, "norm2_w": dn2w,
        "w_ffn_gate": _wgrad(h2, dgate, "dw_gate"), "w_ffn_up": _wgrad(h2, dup, "dw_up"),
        "w_ffn_down": _wgrad(act, dx3, "dw_down"), "final_norm_w": dfw,
    }
    return loss[0, 0], grad_x, grads


def _my_place():
    return lax.axis_index("x"), lax.axis_index("y"), lax.axis_index("c")


def _all_gather(shard):
    r, l = shard.shape

    def body(x_ref, out_ref, send_sems, recv_sems, local_sem):
        x, y, c = _my_place()
        me, sibling = (x, y, c), (x, y, 1 - c)
        chips = [(1 - x, y), (x, 1 - y), (1 - x, 1 - y)]

        def blk(px, py, pc):
            return out_ref.at[4 * px + 2 * py + pc]

        def copy(k, block, to, src=None):
            return pltpu.make_async_remote_copy(
                src_ref=blk(*block) if src is None else src, dst_ref=blk(*block),
                send_sem=send_sems.at[k], recv_sem=recv_sems.at[k], device_id=to, device_id_type=MESH)

        mine = pltpu.make_async_copy(x_ref, blk(*me), local_sem)
        mine.start()
        first = [copy(0, me, sibling, src=x_ref)]
        first += [copy(1 + j, me, (*chip, c), src=x_ref) for j, chip in enumerate(chips)]
        for cp in first:
            cp.start()
        passed = [copy(4 + j, (*chip, c), sibling) for j, chip in enumerate(chips)]
        for j, chip in enumerate(chips):
            copy(1 + j, (*chip, c), me).wait_recv()
            passed[j].start()
        copy(0, sibling, me).wait_recv()
        for j, chip in enumerate(chips):
            copy(4 + j, (*chip, 1 - c), me).wait_recv()
        for cp in first + passed:
            cp.wait_send()
        mine.wait()

    return pl.pallas_call(
        body, name="gather_weights",
        out_shape=jax.ShapeDtypeStruct((N_DEV, r, l), shard.dtype),
        in_specs=[pl.BlockSpec(memory_space=pl.ANY)], out_specs=pl.BlockSpec(memory_space=pl.ANY),
        scratch_shapes=[pltpu.SemaphoreType.DMA((7,)), pltpu.SemaphoreType.DMA((7,)), pltpu.SemaphoreType.DMA],
    )(shard)


def _exchange(big, small):
    def body(big_ref, small_ref, rbig_ref, rsmall_ref, send_sems, recv_sems, local_sems):
        x, y, c = _my_place()
        me = 4 * x + 2 * y + c
        local = [pltpu.make_async_copy(big_ref.at[me], rbig_ref.at[me], local_sems.at[0]),
                 pltpu.make_async_copy(small_ref, rsmall_ref.at[me], local_sems.at[1])]
        for cp in local:
            cp.start()
        sends, recvs = [], []
        for k in range(1, N_DEV):
            px = 1 - x if k & 4 else x
            py = 1 - y if k & 2 else y
            pc = 1 - c if k & 1 else c
            peer = 4 * px + 2 * py + pc
            sends.append(pltpu.make_async_remote_copy(
                src_ref=big_ref.at[peer], dst_ref=rbig_ref.at[me], send_sem=send_sems.at[k - 1],
                recv_sem=recv_sems.at[k - 1], device_id=(px, py, pc), device_id_type=MESH))
            sends.append(pltpu.make_async_remote_copy(
                src_ref=small_ref, dst_ref=rsmall_ref.at[me], send_sem=send_sems.at[6 + k],
                recv_sem=recv_sems.at[6 + k], device_id=(px, py, pc), device_id_type=MESH))
            recvs.append(pltpu.make_async_remote_copy(
                src_ref=big_ref.at[peer], dst_ref=rbig_ref.at[peer], send_sem=send_sems.at[k - 1],
                recv_sem=recv_sems.at[k - 1], device_id=(px, py, pc), device_id_type=MESH))
            recvs.append(pltpu.make_async_remote_copy(
                src_ref=small_ref, dst_ref=rsmall_ref.at[peer], send_sem=send_sems.at[6 + k],
                recv_sem=recv_sems.at[6 + k], device_id=(px, py, pc), device_id_type=MESH))
        for cp in sends:
            cp.start()
        for cp in recvs:
            cp.wait_recv()
        for cp in sends:
            cp.wait_send()
        for cp in local:
            cp.wait()

    n = 2 * (N_DEV - 1)
    return pl.pallas_call(
        body, name="exchange_grads",
        out_shape=[jax.ShapeDtypeStruct(big.shape, big.dtype), jax.ShapeDtypeStruct((N_DEV,) + small.shape, small.dtype)],
        in_specs=[pl.BlockSpec(memory_space=pl.ANY)] * 2, out_specs=[pl.BlockSpec(memory_space=pl.ANY)] * 2,
        scratch_shapes=[pltpu.SemaphoreType.DMA((n,)), pltpu.SemaphoreType.DMA((n,)), pltpu.SemaphoreType.DMA((2,))],
    )(big, small)


def _row_block(r, cap=1024):
    return max(b for b in range(8, cap + 1, 8) if r % b == 0)


def _adam(parts, w, m, v, name):
    r = w.shape[0]
    rb = _row_block(r)

    def body(p_ref, w_ref, m_ref, v_ref, g_ref, d_ref, nm_ref, nv_ref):
        g = p_ref[0].astype(F32)
        for s in range(1, N_DEV):
            g = g + p_ref[s].astype(F32)
        nm = ADAM_B1 * m_ref[...] + (1.0 - ADAM_B1) * g
        nv = ADAM_B2 * v_ref[...] + (1.0 - ADAM_B2) * (g * g)
        m_hat = nm / (1.0 - ADAM_B1 ** ADAM_STEP)
        v_hat = nv / (1.0 - ADAM_B2 ** ADAM_STEP)
        g_ref[...] = g
        d_ref[...] = -ADAM_LR * (m_hat / (jnp.sqrt(v_hat) + ADAM_EPS) + ADAM_WD * w_ref[...])
        nm_ref[...] = nm
        nv_ref[...] = nv

    row = pl.BlockSpec((rb, LANE), lambda i: (i, 0))
    return pl.pallas_call(
        body, name=name, grid=(r // rb,),
        in_specs=[pl.BlockSpec((N_DEV, rb, LANE), lambda i: (0, i, 0)), row, row, row],
        out_specs=[row] * 4, out_shape=[jax.ShapeDtypeStruct((r, LANE), F32)] * 4,
        compiler_params=_params(("arbitrary",)),
    )(parts, w, m, v)


MATRICES = (("w_in", 1), ("gdn_conv_w", 1), ("w_out", 0), ("w_ffn_gate", 1), ("w_ffn_up", 1), ("w_ffn_down", 0))
VECTORS = ("norm1_w", "norm2_w", "final_norm_w", "gdn_A_log", "gdn_dt_bias", "gdn_out_norm_w", "fox_f_bias",
           "fox_q_norm_w", "fox_k_norm_w")
WEIGHTS = ("norm1_w", "w_in", "gdn_conv_w", "gdn_A_log", "gdn_dt_bias", "gdn_out_norm_w", "fox_f_bias", "fox_q_norm_w",
           "fox_k_norm_w", "w_out", "norm2_w", "w_ffn_gate", "w_ffn_up", "w_ffn_down", "final_norm_w")


def _rows(flat, align):
    n = flat.shape[-1]
    per = align * LANE
    total = -(-n // per) * per
    pad = [(0, 0)] * (flat.ndim - 1) + [(0, total - n)]
    return jnp.pad(flat, pad).reshape(flat.shape[:-1] + (total // LANE, LANE))


def _f32_as_wire(a):
    if jnp.dtype(WIRE).itemsize == 4:
        return a.astype(WIRE)
    return lax.bitcast_convert_type(a, WIRE).reshape(a.shape[:-1] + (-1,))


def _wire_as_f32(a):
    if jnp.dtype(WIRE).itemsize == 4:
        return a.astype(F32)
    k = 4 // jnp.dtype(WIRE).itemsize
    return lax.bitcast_convert_type(a.reshape(a.shape[:-1] + (-1, k)), F32)


def _join_shards(seg, shape, axis):
    seg = seg.reshape((N_DEV,) + shape)
    if axis == 0:
        return seg.reshape(N_DEV * shape[0], shape[1])
    return seg.transpose(1, 0, 2).reshape(shape[0], N_DEV * shape[1])


def _cut_shards(full, axis):
    if axis == 0:
        return full.reshape(N_DEV, -1)
    rows, cols = full.shape
    return full.reshape(rows, N_DEV, cols // N_DEV).transpose(1, 0, 2).reshape(N_DEV, -1)


def kernel(x, norm1_w, w_in, gdn_conv_w, gdn_A_log, gdn_dt_bias, gdn_out_norm_w, fox_f_bias, fox_q_norm_w, fox_k_norm_w, w_out, norm2_w, w_ffn_gate, w_ffn_up, w_ffn_down, final_norm_w, loss_target, m_norm1_w, m_w_in, m_gdn_conv_w, m_gdn_A_log, m_gdn_dt_bias, m_gdn_out_norm_w, m_fox_f_bias, m_fox_q_norm_w, m_fox_k_norm_w, m_w_out, m_norm2_w, m_w_ffn_gate, m_w_ffn_up, m_w_ffn_down, m_final_norm_w, v_norm1_w, v_w_in, v_gdn_conv_w, v_gdn_A_log, v_gdn_dt_bias, v_gdn_out_norm_w, v_fox_f_bias, v_fox_q_norm_w, v_fox_k_norm_w, v_w_out, v_norm2_w, v_w_ffn_gate, v_w_ffn_up, v_w_ffn_down, v_final_norm_w):
    w = dict(norm1_w=norm1_w, w_in=w_in, gdn_conv_w=gdn_conv_w, gdn_A_log=gdn_A_log, gdn_dt_bias=gdn_dt_bias,
             gdn_out_norm_w=gdn_out_norm_w, fox_f_bias=fox_f_bias, fox_q_norm_w=fox_q_norm_w, fox_k_norm_w=fox_k_norm_w,
             w_out=w_out, norm2_w=norm2_w, w_ffn_gate=w_ffn_gate, w_ffn_up=w_ffn_up, w_ffn_down=w_ffn_down,
             final_norm_w=final_norm_w)
    m = dict(norm1_w=m_norm1_w, w_in=m_w_in, gdn_conv_w=m_gdn_conv_w, gdn_A_log=m_gdn_A_log, gdn_dt_bias=m_gdn_dt_bias,
             gdn_out_norm_w=m_gdn_out_norm_w, fox_f_bias=m_fox_f_bias, fox_q_norm_w=m_fox_q_norm_w,
             fox_k_norm_w=m_fox_k_norm_w, w_out=m_w_out, norm2_w=m_norm2_w, w_ffn_gate=m_w_ffn_gate,
             w_ffn_up=m_w_ffn_up, w_ffn_down=m_w_ffn_down, final_norm_w=m_final_norm_w)
    v = dict(norm1_w=v_norm1_w, w_in=v_w_in, gdn_conv_w=v_gdn_conv_w, gdn_A_log=v_gdn_A_log, gdn_dt_bias=v_gdn_dt_bias,
             gdn_out_norm_w=v_gdn_out_norm_w, fox_f_bias=v_fox_f_bias, fox_q_norm_w=v_fox_q_norm_w,
             fox_k_norm_w=v_fox_k_norm_w, w_out=v_w_out, norm2_w=v_norm2_w, w_ffn_gate=v_w_ffn_gate,
             w_ffn_up=v_w_ffn_up, w_ffn_down=v_w_ffn_down, final_norm_w=v_final_norm_w)
    shard_shape = {n: w[n].shape[1:] for n, _ in MATRICES}
    size = {n: shard_shape[n][0] * shard_shape[n][1] for n, _ in MATRICES}

    wire_flat = [w[n].reshape(-1).astype(WIRE) for n, _ in MATRICES if n != "gdn_conv_w"]
    wire_flat.append(_f32_as_wire(w["gdn_conv_w"].reshape(-1)))
    gathered = _all_gather(_rows(jnp.concatenate(wire_flat), ROW_ALIGN))
    gathered = gathered.reshape(N_DEV, -1)
    full, off = {}, 0
    for n, axis in MATRICES:
        if n != "gdn_conv_w":
            full[n] = _join_shards(gathered[:, off:off + size[n]], shard_shape[n], axis)
            off += size[n]
    k = 4 // jnp.dtype(WIRE).itemsize
    full["gdn_conv_w"] = _join_shards(_wire_as_f32(gathered[:, off:off + k * size["gdn_conv_w"]]),
                                      shard_shape["gdn_conv_w"], 1)

    loss, grad_x, grads = _local_step(
        x[0], loss_target[0], w["norm1_w"][0], full["w_in"], full["gdn_conv_w"], w["gdn_A_log"][0],
        w["gdn_dt_bias"][0], w["gdn_out_norm_w"][0], w["fox_f_bias"][0], w["fox_q_norm_w"][0], w["fox_k_norm_w"][0],
        full["w_out"], w["norm2_w"][0], full["w_ffn_gate"], full["w_ffn_up"], full["w_ffn_down"], w["final_norm_w"])

    big = _rows(jnp.concatenate([_cut_shards(grads[n].astype(WIRE), axis) for n, axis in MATRICES], axis=1), ROW_ALIGN)
    vec = _rows(jnp.concatenate([grads[n].reshape(-1) for n in VECTORS] + [loss.reshape(1)]), 8)
    rbig, rvec = _exchange(big, vec)

    def pack_matrices(d):
        return _rows(jnp.concatenate([d[n].reshape(-1) for n, _ in MATRICES]), ROW_ALIGN)

    def pack_vectors(d):
        return _rows(jnp.concatenate([d[n].reshape(-1) for n in VECTORS] + [jnp.zeros((1,), F32)]), 8)

    big_out = _adam(rbig, pack_matrices(w), pack_matrices(m), pack_matrices(v), "adam_matrices")
    vec_out = _adam(rvec, pack_vectors(w), pack_vectors(m), pack_vectors(v), "adam_vectors")

    results = []
    for bo, vo in zip(big_out, vec_out):
        bo, vo = bo.reshape(-1), vo.reshape(-1)
        d, off = {}, 0
        for n, _ in MATRICES:
            d[n] = bo[off:off + size[n]].reshape(w[n].shape)
            off += size[n]
        off = 0
        for n in VECTORS:
            d[n] = vo[off:off + w[n].size].reshape(w[n].shape)
            off += w[n].size
        results.append(d)
    total_loss = vec_out[0].reshape(-1)[sum(w[n].size for n in VECTORS)]
    return (total_loss, grad_x[None], *[d[n] for d in results for n in WEIGHTS])
```

```python
import functools

import jax
import jax.numpy as jnp
from jax import lax
from jax.experimental import pallas as pl
from jax.experimental.pallas import tpu as pltpu

F32 = jnp.float32
MXU = jnp.bfloat16
WIRE = jnp.bfloat16
HI = lax.Precision.HIGHEST
EPS = 1e-6

N_DEV = 8
HEADS = 8
DH = 64
WIDTH = HEADS * DH
CHUNK = 64
LANE = 128
ROW_ALIGN = 16
TB = 256
QB = 256
VMEM_LIMIT = 60 * 1024 * 1024

ADAM_LR = 0.001
ADAM_B1 = 0.9
ADAM_B2 = 0.999
ADAM_EPS = 1e-08
ADAM_WD = 0.01
ADAM_STEP = 10

MESH = pl.DeviceIdType.MESH


def _params(sem=None):
    return pltpu.CompilerParams(dimension_semantics=sem, vmem_limit_bytes=VMEM_LIMIT)


def _resident(shape):
    n = len(shape)
    return pl.BlockSpec(shape, lambda *_: (0,) * n, pipeline_mode=pl.Buffered(1))


def _dot(a, b):
    return jnp.dot(a.astype(MXU), b.astype(MXU), preferred_element_type=F32)


def _dot_nt(a, b):
    return lax.dot_general(a.astype(MXU), b.astype(MXU), (((1,), (1,)), ((), ())), preferred_element_type=F32)


def _dot_tn(a, b):
    return lax.dot_general(a.astype(MXU), b.astype(MXU), (((0,), (0,)), ((), ())), preferred_element_type=F32)


def _hdot(a, b):
    return jnp.dot(a, b, precision=HI, preferred_element_type=F32)


def _hdot_nt(a, b):
    return lax.dot_general(a, b, (((1,), (1,)), ((), ())), precision=HI, preferred_element_type=F32)


def _hdot_tn(a, b):
    return lax.dot_general(a, b, (((0,), (0,)), ((), ())), precision=HI, preferred_element_type=F32)


def _sigmoid(x):
    return 1.0 / (1.0 + jnp.exp(-x))


def _softplus(x):
    return jnp.maximum(x, 0.0) + jnp.log(1.0 + jnp.exp(-jnp.abs(x)))


def _head_sum_matrix():
    ri = lax.broadcasted_iota(jnp.int32, (LANE, LANE), 0) // DH
    ci = lax.broadcasted_iota(jnp.int32, (LANE, LANE), 1) // DH
    return (ri == ci).astype(F32)


def _shift_down(x, s, rows):
    return jnp.where(rows >= s, pltpu.roll(x, s, 0), 0.0)


def _shift_up(x, s, rows):
    t = x.shape[0]
    return jnp.where(rows < t - s, pltpu.roll(x, t - s, 0), 0.0)


def _inproj(x, n1w, w_main, w_small):
    t, d = x.shape
    nm = w_main.shape[1]
    tb = min(TB, t)

    def body(x_ref, nw_ref, wm_ref, ws_ref, h_ref, pm_ref, ps_ref):
        xv = x_ref[...]
        r = lax.rsqrt(jnp.mean(xv * xv, axis=-1, keepdims=True) + EPS)
        h = (xv * r * nw_ref[...]).astype(MXU)
        h_ref[...] = h
        pm_ref[...] = jnp.dot(h, wm_ref[...], preferred_element_type=F32)
        ps_ref[...] = jnp.dot(h, ws_ref[...], preferred_element_type=F32)

    return pl.pallas_call(
        body, name="inproj", grid=(t // tb,),
        in_specs=[pl.BlockSpec((tb, d), lambda i: (i, 0)), _resident((1, d)), _resident((d, nm)), _resident((d, LANE))],
        out_specs=[pl.BlockSpec((tb, d), lambda i: (i, 0)), pl.BlockSpec((tb, nm), lambda i: (i, 0)),
                   pl.BlockSpec((tb, LANE), lambda i: (i, 0))],
        out_shape=[jax.ShapeDtypeStruct((t, d), MXU), jax.ShapeDtypeStruct((t, nm), F32),
                   jax.ShapeDtypeStruct((t, LANE), F32)],
        compiler_params=_params(("arbitrary",)),
    )(x, n1w, w_main, w_small)


def _inproj_bwd(x, n1w, dx2, dgdn, dz, dfox, dfg, dps, w_main, w_small):
    t, d = x.shape
    tb = min(TB, t)
    w3 = 3 * WIDTH

    def body(x_ref, nw_ref, dx2_ref, dgdn_ref, dz_ref, dfox_ref, dfg_ref, dps_ref, wm_ref, ws_ref, gx_ref, dnw_ref):
        dh = _dot_nt(dgdn_ref[...], wm_ref[:, 0:w3])
        dh += _dot_nt(dz_ref[...], wm_ref[:, w3:w3 + WIDTH])
        dh += _dot_nt(dfox_ref[...], wm_ref[:, w3 + WIDTH:2 * w3 + WIDTH])
        dh += _dot_nt(dfg_ref[...], wm_ref[:, 2 * w3 + WIDTH:2 * w3 + 2 * WIDTH])
        dh += _dot_nt(dps_ref[...], ws_ref[...])
        xv = x_ref[...]
        r = lax.rsqrt(jnp.mean(xv * xv, axis=-1, keepdims=True) + EPS)
        xn = xv * r

        @pl.when(pl.program_id(0) == 0)
        def _():
            dnw_ref[...] = jnp.zeros_like(dnw_ref)

        dnw_ref[...] += jnp.sum(dh * xn, axis=0, keepdims=True)
        g = dh * nw_ref[...]
        gx_ref[...] = dx2_ref[...] + r * (g - xn * jnp.mean(g * xn, axis=-1, keepdims=True))

    def tok(n):
        return pl.BlockSpec((tb, n), lambda i: (i, 0))

    return pl.pallas_call(
        body, name="inproj_bwd", grid=(t // tb,),
        in_specs=[tok(d), _resident((1, d)), tok(d), tok(w3), tok(WIDTH), tok(w3), tok(WIDTH), tok(LANE),
                  _resident(w_main.shape), _resident(w_small.shape)],
        out_specs=[tok(d), pl.BlockSpec((1, d), lambda i: (0, 0))],
        out_shape=[jax.ShapeDtypeStruct((t, d), F32), jax.ShapeDtypeStruct((1, d), F32)],
        compiler_params=_params(("arbitrary",)),
    )(x, n1w, dx2, dgdn, dz, dfox, dfg, dps, w_main, w_small)


def _gate_lanes(shape):
    lane = lax.broadcasted_iota(jnp.int32, shape, 1)
    return lane < HEADS, (lane >= HEADS) & (lane < 2 * HEADS), (lane >= 2 * HEADS) & (lane < 3 * HEADS)


def _block_masks():
    ri = lax.broadcasted_iota(jnp.int32, (LANE, LANE), 0)
    ci = lax.broadcasted_iota(jnp.int32, (LANE, LANE), 1)
    same = (ri // CHUNK) == (ci // CHUNK)
    return ((ri >= ci).astype(F32), (ri <= ci).astype(F32), (same & (ri >= ci)).astype(F32),
            (same & (ri <= ci)).astype(F32), same.astype(F32))


def _gates(ps, gparams):
    t = ps.shape[0]
    nb = t // LANE

    def body(ps_ref, gp_ref, out_ref, run_ref, tot_ref):
        p = ps_ref[...]
        is_b, is_a, is_f = _gate_lanes(p.shape)
        z = p + gp_ref[0:1, :]
        neg_exp_a = -jnp.exp(gp_ref[1:2, :])
        glog = neg_exp_a * _softplus(z)
        logf = -_softplus(-z)
        out_ref[...] = jnp.where(is_b, _sigmoid(p), jnp.where(is_a, glog, jnp.where(is_f, logf, 0.0)))
        tril, _, tril_c, _, same_c = _block_masks()
        off = jnp.zeros((1, LANE), F32)
        for b in range(nb):
            rows = slice(b * LANE, (b + 1) * LANE)
            blk = out_ref[rows, :]
            ga = jnp.where(is_a[:LANE], blk, 0.0)
            fb = _hdot(tril, jnp.where(is_f[:LANE], blk, 0.0)) + off
            run_ref[rows, :] = fb + _hdot(tril_c, ga)
            tot_ref[rows, :] = _hdot(same_c, ga)
            off = fb[LANE - 1:LANE, :]

    return pl.pallas_call(
        body, name="gates",
        out_shape=[jax.ShapeDtypeStruct((t, LANE), F32)] * 3,
        compiler_params=_params(),
    )(ps, gparams)


def _gates_bwd(ps, gparams, d1, d2, d3):
    t = ps.shape[0]
    nb = t // LANE

    def body(ps_ref, gp_ref, d1_ref, d2_ref, d3_ref, dps_ref, sums_ref, dl_ref):
        p = ps_ref[...]
        is_b, is_a, is_f = _gate_lanes(p.shape)
        _, triu, _, triu_c, same_c = _block_masks()
        off = jnp.zeros((1, LANE), F32)
        for b in reversed(range(nb)):
            rows = slice(b * LANE, (b + 1) * LANE)
            s12 = d1_ref[rows, :] + d2_ref[rows, :]
            dlf = _hdot(triu, jnp.where(is_f[:LANE], s12, 0.0)) + off
            dla = (_hdot(triu_c, jnp.where(is_a[:LANE], s12, 0.0))
                   + _hdot(same_c, jnp.where(is_a[:LANE], d3_ref[rows, :], 0.0)))
            dl_ref[rows, :] = dlf + dla
            off = dlf[0:1, :]
        z = p + gp_ref[0:1, :]
        neg_exp_a = -jnp.exp(gp_ref[1:2, :])
        sb = _sigmoid(p)
        glog = neg_exp_a * _softplus(z)
        dl = dl_ref[...]
        dp = jnp.where(is_b, (d1_ref[...] + d2_ref[...]) * sb * (1.0 - sb),
                       jnp.where(is_a, dl * neg_exp_a * _sigmoid(z), jnp.where(is_f, dl * _sigmoid(-z), 0.0)))
        dps_ref[...] = dp
        sums_ref[...] = jnp.zeros_like(sums_ref)
        sums_ref[0:1, :] = jnp.sum(jnp.where(is_a, dl * glog, 0.0), axis=0, keepdims=True)
        sums_ref[1:2, :] = jnp.sum(jnp.where(is_b, 0.0, dp), axis=0, keepdims=True)

    return pl.pallas_call(
        body, name="gates_bwd",
        out_shape=[jax.ShapeDtypeStruct((t, LANE), F32), jax.ShapeDtypeStruct((8, LANE), F32)],
        scratch_shapes=[pltpu.VMEM((t, LANE), F32)],
        compiler_params=_params(),
    )(ps, gparams, d1, d2, d3)


def _conv(xv, w, rows):
    acc = w[3:4, :] * xv
    for s in range(1, 4):
        acc += w[3 - s:4 - s, :] * _shift_down(xv, s, rows)
    return acc


def _gdn_prep(pm, conv_w):
    t = pm.shape[0]
    nj = WIDTH // LANE

    def body(x_ref, w_ref, o_ref):
        kind = pl.program_id(0)
        xv = x_ref[...]
        rows = lax.broadcasted_iota(jnp.int32, xv.shape, 0)
        acc = _conv(xv, w_ref[...], rows)
        y = acc * _sigmoid(acc)
        ss = _hdot(y * y, _head_sum_matrix())
        out = jnp.where(kind < 2, y * lax.rsqrt(ss + EPS), y)
        o_ref[0, 0] = out[:, :DH]
        o_ref[0, 1] = out[:, DH:]

    return pl.pallas_call(
        body, name="gdn_prep", grid=(3, nj),
        in_specs=[pl.BlockSpec((t, LANE), lambda i, j: (0, i * nj + j)),
                  pl.BlockSpec((4, LANE), lambda i, j: (0, i * nj + j))],
        out_specs=pl.BlockSpec((1, 2, t, DH), lambda i, j: (i, j, 0, 0)),
        out_shape=jax.ShapeDtypeStruct((3, HEADS, t, DH), F32),
        compiler_params=_params(("arbitrary", "arbitrary")),
    )(pm, conv_w)


def _gdn_prep_bwd(pm, conv_w, dqkv):
    t = pm.shape[0]
    nj = WIDTH // LANE

    def body(x_ref, w_ref, d_ref, dx_ref, dw_ref):
        kind = pl.program_id(0)
        xv = x_ref[...]
        w = w_ref[...]
        rows = lax.broadcasted_iota(jnp.int32, xv.shape, 0)
        acc = _conv(xv, w, rows)
        sg = _sigmoid(acc)
        y = acc * sg
        hs = _head_sum_matrix()
        r = lax.rsqrt(_hdot(y * y, hs) + EPS)
        yn = y * r
        d = jnp.concatenate([d_ref[0, 0], d_ref[0, 1]], axis=1)
        dy = jnp.where(kind < 2, r * (d - yn * _hdot(d * yn, hs)), d)
        dacc = dy * sg * (1.0 + acc * (1.0 - sg))
        dx = w[3:4, :] * dacc
        dw_ref[3:4, :] = jnp.sum(dacc * xv, axis=0, keepdims=True)
        for s in range(1, 4):
            dx += w[3 - s:4 - s, :] * _shift_up(dacc, s, rows)
            dw_ref[3 - s:4 - s, :] = jnp.sum(dacc * _shift_down(xv, s, rows), axis=0, keepdims=True)
        dx_ref[...] = dx

    return pl.pallas_call(
        body, name="gdn_prep_bwd", grid=(3, nj),
        in_specs=[pl.BlockSpec((t, LANE), lambda i, j: (0, i * nj + j)),
                  pl.BlockSpec((4, LANE), lambda i, j: (0, i * nj + j)),
                  pl.BlockSpec((1, 2, t, DH), lambda i, j: (i, j, 0, 0))],
        out_specs=[pl.BlockSpec((t, LANE), lambda i, j: (0, i * nj + j)),
                   pl.BlockSpec((4, LANE), lambda i, j: (0, i * nj + j))],
        out_shape=[jax.ShapeDtypeStruct((t, 3 * WIDTH), F32), jax.ShapeDtypeStruct((4, 3 * WIDTH), F32)],
        compiler_params=_params(("arbitrary", "arbitrary")),
    )(pm, conv_w, dqkv)


FOX_COL0 = 4 * WIDTH // LANE


def _fox_prep(pm, nw):
    t = pm.shape[0]
    nj = WIDTH // LANE

    def body(x_ref, w_ref, o_ref):
        kind = pl.program_id(0)
        xv = x_ref[...]
        ms = _hdot(xv * xv, _head_sum_matrix()) * (1.0 / DH)
        yn = xv * lax.rsqrt(ms + EPS) * w_ref[pl.ds(kind, 1), :]
        out = jnp.where(kind < 2, yn, xv)
        o_ref[0, 0] = out[:, :DH]
        o_ref[0, 1] = out[:, DH:]

    return pl.pallas_call(
        body, name="fox_prep", grid=(3, nj),
        in_specs=[pl.BlockSpec((t, LANE), lambda i, j: (0, FOX_COL0 + i * nj + j)),
                  pl.BlockSpec((3, LANE), lambda i, j: (0, 0))],
        out_specs=pl.BlockSpec((1, 2, t, DH), lambda i, j: (i, j, 0, 0)),
        out_shape=jax.ShapeDtypeStruct((3, HEADS, t, DH), F32),
        compiler_params=_params(("arbitrary", "arbitrary")),
    )(pm, nw)


def _fox_prep_bwd(pm, nw, dqkv):
    t = pm.shape[0]
    nj = WIDTH // LANE

    def body(x_ref, w_ref, d_ref, dx_ref, dw_ref):
        kind = pl.program_id(0)
        xv = x_ref[...]
        hs = _head_sum_matrix()
        r = lax.rsqrt(_hdot(xv * xv, hs) * (1.0 / DH) + EPS)
        xn = xv * r
        d = jnp.concatenate([d_ref[0, 0], d_ref[0, 1]], axis=1)
        g = d * w_ref[pl.ds(kind, 1), :]
        dxn = r * (g - xn * _hdot(g * xn, hs) * (1.0 / DH))
        dx_ref[...] = jnp.where(kind < 2, dxn, d)
        dw_ref[0, 0] = jnp.sum(d * xn, axis=0, keepdims=True)

    return pl.pallas_call(
        body, name="fox_prep_bwd", grid=(3, nj),
        in_specs=[pl.BlockSpec((t, LANE), lambda i, j: (0, FOX_COL0 + i * nj + j)),
                  pl.BlockSpec((3, LANE), lambda i, j: (0, 0)),
                  pl.BlockSpec((1, 2, t, DH), lambda i, j: (i, j, 0, 0))],
        out_specs=[pl.BlockSpec((t, LANE), lambda i, j: (0, i * nj + j)),
                   pl.BlockSpec((1, 1, 1, LANE), lambda i, j: (i, j, 0, 0))],
        out_shape=[jax.ShapeDtypeStruct((t, 3 * WIDTH), F32), jax.ShapeDtypeStruct((3, nj, 1, LANE), F32)],
        compiler_params=_params(("arbitrary", "arbitrary")),
    )(pm, nw, dqkv)


SC = 256
CPS = SC // CHUNK
Q_SCALE = DH ** -0.5


def _sc_masks():
    ri = lax.broadcasted_iota(jnp.int32, (SC, SC), 0)
    ci = lax.broadcasted_iota(jnp.int32, (SC, SC), 1)
    same = (ri // CHUNK) == (ci // CHUNK)
    return same & (ri >= ci), same & (ri > ci), ri == ci


def _unit_lower_inverse(m, eye):
    inv = jnp.where(eye, 1.0, 0.0) + m
    for _ in range(5):
        m = _dot(m, m)
        inv = inv + _dot(inv, m)
    return inv


def _gdn_local(q, k, cols, grow, causal):
    beta, gc, gl = cols[:, 0:1], cols[:, 1:2], cols[:, 2:3]
    decay = jnp.exp(jnp.where(causal, gc - grow, -1e30))
    egc = jnp.exp(gc)
    ekd = jnp.exp(gl - gc)
    qs = q * Q_SCALE
    kb = k * beta
    kk = _dot_nt(kb, k)
    qk = _dot_nt(qs, k)
    return beta, gl, decay, egc, ekd, qs, kb, kk, qk, jnp.where(causal, qk * decay, 0.0)


def _chunk_rows(c):
    return pl.ds(pl.multiple_of(c * CHUNK, CHUNK), CHUNK)


def _sc_rows(b):
    return pl.ds(pl.multiple_of(b * SC, SC), SC)


def _gdn_fwd(qkv, cols, grow):
    t = qkv.shape[2]
    nc = t // CHUNK
    nsc = t // SC

    def body(qkv_ref, cols_ref, grow_ref, o_ref, st_ref, inv_ref, kc_s, qc_s, g_s, au_s):
        causal, strict, eye = _sc_masks()

        def local(b, _):
            rows = _sc_rows(b)
            q, k, v = qkv_ref[0, 0, rows, :], qkv_ref[1, 0, rows, :], qkv_ref[2, 0, rows, :]
            beta, _, decay, egc, ekd, qs, kb, kk, _, attn = _gdn_local(q, k, cols_ref[0, rows, :], grow_ref[0, b], causal)
            inv = _unit_lower_inverse(-jnp.where(strict, kk * decay, 0.0), eye)
            inv_ref[0, rows, :] = inv.astype(inv_ref.dtype)
            u = _dot(inv, v * beta)
            w = _dot(inv, kb * egc)
            g_s[rows, :] = qs * egc - _dot(attn, w)
            au_s[rows, :] = _dot(attn, u)
            kd = k * ekd
            for j in range(CPS):
                sl = slice(j * CHUNK, (j + 1) * CHUNK)
                kc_s[b * CPS + j] = _dot_tn(kd[sl], w[sl])
                qc_s[b * CPS + j] = _dot_tn(kd[sl], u[sl])
            return 0

        lax.fori_loop(0, nsc, local, 0)

        def step(c, s):
            rows = _chunk_rows(c)
            st_ref[0, c] = s
            o_ref[0, rows, :] = _dot(g_s[rows, :], s) + au_s[rows, :]
            egl = jnp.exp(cols_ref[0, pl.ds(c * CHUNK, 1), 2:3])
            return egl * s - _dot(kc_s[c], s) + qc_s[c]

        lax.fori_loop(0, nc, step, jnp.zeros((DH, DH), F32))

    return pl.pallas_call(
        body, name="gdn_fwd", grid=(HEADS,),
        in_specs=[pl.BlockSpec((3, 1, t, DH), lambda h: (0, h, 0, 0)), pl.BlockSpec((1, t, 8), lambda h: (h, 0, 0)),
                  pl.BlockSpec((1, nsc, 1, SC), lambda h: (h, 0, 0, 0))],
        out_specs=[pl.BlockSpec((1, t, DH), lambda h: (h, 0, 0)), pl.BlockSpec((1, nc, DH, DH), lambda h: (h, 0, 0, 0)),
                   pl.BlockSpec((1, t, SC), lambda h: (h, 0, 0))],
        out_shape=[jax.ShapeDtypeStruct((HEADS, t, DH), F32), jax.ShapeDtypeStruct((HEADS, nc, DH, DH), F32),
                   jax.ShapeDtypeStruct((HEADS, t, SC), MXU)],
        scratch_shapes=[pltpu.VMEM((nc, DH, DH), F32), pltpu.VMEM((nc, DH, DH), F32),
                        pltpu.VMEM((t, DH), F32), pltpu.VMEM((t, DH), F32)],
        compiler_params=_params(("arbitrary",)),
    )(qkv, cols, grow)


def _gdn_bwd(qkv, cols, grow, inv, states, do):
    t = qkv.shape[2]
    nc = t // CHUNK
    nsc = t // SC

    def body(qkv_ref, cols_ref, grow_ref, inv_ref, st_ref, do_ref, dqkv_ref, dcols_ref, dgrow_ref,
             u_s, w_s, kc_s, h_s, dsn_s):
        causal, strict, _ = _sc_masks()

        def local(b, _):
            rows = _sc_rows(b)
            q, k, v = qkv_ref[0, 0, rows, :], qkv_ref[1, 0, rows, :], qkv_ref[2, 0, rows, :]
            beta, _, _, egc, ekd, qs, kb, _, _, attn = _gdn_local(q, k, cols_ref[0, rows, :], grow_ref[0, b], causal)
            inv_b = inv_ref[0, rows, :]
            u = _dot(inv_b, v * beta)
            w = _dot(inv_b, kb * egc)
            u_s[rows, :] = u
            w_s[rows, :] = w
            g = qs * egc - _dot(attn, w)
            kd = k * ekd
            dout = do_ref[0, rows, :]
            for j in range(CPS):
                sl = slice(j * CHUNK, (j + 1) * CHUNK)
                kc_s[b * CPS + j] = _dot_tn(kd[sl], w[sl])
                h_s[b * CPS + j] = _dot_tn(g[sl], dout[sl])
            return 0

        lax.fori_loop(0, nsc, local, 0)

        def step(i, ds):
            c = nc - 1 - i
            dsn_s[c] = ds
            egl = jnp.exp(cols_ref[0, pl.ds(c * CHUNK, 1), 2:3])
            return egl * ds - _dot_tn(kc_s[c], ds) + h_s[c]

        lax.fori_loop(0, nc, step, jnp.zeros((DH, DH), F32))

        def back(b, _):
            rows = _sc_rows(b)
            q, k, v = qkv_ref[0, 0, rows, :], qkv_ref[1, 0, rows, :], qkv_ref[2, 0, rows, :]
            beta, gl, decay, egc, ekd, qs, kb, kk, qk, attn = _gdn_local(q, k, cols_ref[0, rows, :], grow_ref[0, b], causal)
            u, w = u_s[rows, :], w_s[rows, :]
            kd = k * ekd
            vb = v * beta
            kbe = kb * egc
            dout = do_ref[0, rows, :]
            first = lax.broadcasted_iota(jnp.int32, (CHUNK, 1), 0) == 0
            dg_p, dkd_p, dw_p, du_p, dgl_p = [], [], [], [], []
            for j in range(CPS):
                sl = slice(j * CHUNK, (j + 1) * CHUNK)
                s = st_ref[0, b * CPS + j]
                dsn = dsn_s[b * CPS + j]
                dkc = -_dot_nt(dsn, s)
                dg_p.append(_dot_nt(dout[sl], s))
                dkd_p.append(_dot_nt(w[sl], dkc) + _dot_nt(u[sl], dsn))
                dw_p.append(_dot(kd[sl], dkc))
                du_p.append(_dot(kd[sl], dsn))
                degl = jnp.sum(jnp.sum(s * dsn, axis=1, keepdims=True), axis=0, keepdims=True)
                dgl_p.append(jnp.where(first, degl * jnp.exp(gl[j * CHUNK:j * CHUNK + 1, :]), 0.0))
            dg, dkd = jnp.concatenate(dg_p, axis=0), jnp.concatenate(dkd_p, axis=0)
            da = jnp.where(causal, _dot_nt(dout, u) - _dot_nt(dg, w), 0.0)
            at = _dot_tn(attn, jnp.concatenate([dout, dg], axis=1))
            du = at[:, :DH] + jnp.concatenate(du_p, axis=0)
            dw = jnp.concatenate(dw_p, axis=0) - at[:, DH:]
            inv_t = inv_ref[0, rows, :].astype(F32).T
            it = _dot(inv_t, jnp.concatenate([du, dw], axis=1))
            dvb, dkbe = it[:, :DH], it[:, DH:]
            dinv = _dot_nt(du, vb) + _dot_nt(dw, kbe)
            dl = jnp.where(strict, -_dot(_dot(inv_t, dinv), inv_t), 0.0)
            dlogd = (dl * kk + da * qk) * decay
            dkk = dl * decay
            dqk = da * decay
            dkb = _dot(dkk, k) + dkbe * egc
            dqs = _dot(dqk, k) + dg * egc
            dk = (_dot_tn(jnp.concatenate([dkk, dqk], axis=0), jnp.concatenate([kb, qs], axis=0))
                  + dkd * ekd + dkb * beta)
            dkd_kd = jnp.sum(dkd * kd, axis=1, keepdims=True)
            dgc = (jnp.sum(dlogd, axis=1, keepdims=True) + jnp.sum(dg * qs, axis=1, keepdims=True) * egc
                   + jnp.sum(dkbe * kbe, axis=1, keepdims=True) - dkd_kd)
            dgl = dkd_kd + jnp.concatenate(dgl_p, axis=0)
            dbeta = jnp.sum(dkb * k, axis=1, keepdims=True) + jnp.sum(dvb * v, axis=1, keepdims=True)
            dqkv_ref[0, 0, rows, :] = dqs * Q_SCALE
            dqkv_ref[1, 0, rows, :] = dk
            dqkv_ref[2, 0, rows, :] = dvb * beta
            lane = lax.broadcasted_iota(jnp.int32, (SC, 8), 1)
            dcols_ref[0, rows, :] = jnp.where(lane == 0, dbeta, jnp.where(lane == 1, dgc, jnp.where(lane == 2, dgl, 0.0)))
            dgrow_ref[0, b] = -jnp.sum(dlogd, axis=0, keepdims=True)
            return 0

        lax.fori_loop(0, nsc, back, 0)

    colspec = pl.BlockSpec((1, t, 8), lambda h: (h, 0, 0))
    rowspec = pl.BlockSpec((1, nsc, 1, SC), lambda h: (h, 0, 0, 0))
    sq = pltpu.VMEM((nc, DH, DH), F32)
    return pl.pallas_call(
        body, name="gdn_bwd", grid=(HEADS,),
        in_specs=[pl.BlockSpec((3, 1, t, DH), lambda h: (0, h, 0, 0)), colspec, rowspec,
                  pl.BlockSpec((1, t, SC), lambda h: (h, 0, 0)), pl.BlockSpec((1, nc, DH, DH), lambda h: (h, 0, 0, 0)),
                  pl.BlockSpec((1, t, DH), lambda h: (h, 0, 0))],
        out_specs=[pl.BlockSpec((3, 1, t, DH), lambda h: (0, h, 0, 0)), colspec, rowspec],
        out_shape=[jax.ShapeDtypeStruct((3, HEADS, t, DH), F32), jax.ShapeDtypeStruct((HEADS, t, 8), F32),
                   jax.ShapeDtypeStruct((HEADS, nsc, 1, SC), F32)],
        scratch_shapes=[pltpu.VMEM((t, DH), F32), pltpu.VMEM((t, DH), F32), sq, sq, sq],
        compiler_params=_params(("arbitrary",)),
    )(qkv, cols, grow, inv, states, do)


def _causal_mask(i, j, qb):
    qpos = i * qb + lax.broadcasted_iota(jnp.int32, (qb, qb), 0)
    kpos = j * qb + lax.broadcasted_iota(jnp.int32, (qb, qb), 1)
    return qpos >= kpos


def _fox_fwd(qkv, fcol, frow):
    t = qkv.shape[2]
    qb = min(QB, t)
    nq = t // qb
    scale = DH ** -0.5

    def body(q_ref, k_ref, v_ref, fc_ref, fr_ref, o_ref, lse_ref):
        i = pl.program_id(1)
        q = q_ref[0, 0].astype(MXU)
        fq = fc_ref[0]

        def kstep(j, carry):
            m, l, acc = carry
            rows = pl.ds(pl.multiple_of(j * qb, qb), qb)
            s = _dot_nt(q, k_ref[0, 0, rows, :]) * scale + (fq - fr_ref[0, j])
            s = jnp.where(_causal_mask(i, j, qb), s, -1e30)
            m_new = jnp.maximum(m, jnp.max(s, axis=1, keepdims=True))
            p = jnp.exp(s - m_new)
            alpha = jnp.exp(m - m_new)
            l = alpha * l + jnp.sum(p, axis=1, keepdims=True)
            acc = alpha * acc + _dot(p, v_ref[0, 0, rows, :])
            return m_new, l, acc

        init = (jnp.full((qb, 1), -1e30, F32), jnp.zeros((qb, 1), F32), jnp.zeros((qb, DH), F32))
        m, l, acc = lax.fori_loop(0, i + 1, kstep, init)
        o_ref[0] = acc / l
        lse_ref[0] = m + jnp.log(l)

    return pl.pallas_call(
        body, name="fox_fwd", grid=(HEADS, nq),
        in_specs=[pl.BlockSpec((1, 1, qb, DH), lambda h, i: (0, h, i, 0)),
                  pl.BlockSpec((1, 1, t, DH), lambda h, i: (1, h, 0, 0)),
                  pl.BlockSpec((1, 1, t, DH), lambda h, i: (2, h, 0, 0)),
                  pl.BlockSpec((1, qb, 1), lambda h, i: (h, i, 0)),
                  pl.BlockSpec((1, nq, 1, qb), lambda h, i: (h, 0, 0, 0))],
        out_specs=[pl.BlockSpec((1, qb, DH), lambda h, i: (h, i, 0)), pl.BlockSpec((1, qb, 1), lambda h, i: (h, i, 0))],
        out_shape=[jax.ShapeDtypeStruct((HEADS, t, DH), F32), jax.ShapeDtypeStruct((HEADS, t, 1), F32)],
        compiler_params=_params(("arbitrary", "arbitrary")),
    )(qkv, qkv, qkv, fcol, frow)


def _fox_bwd(qkv, fcol, frow, o, lse, do):
    t = qkv.shape[2]
    qb = min(QB, t)
    nq = t // qb
    scale = DH ** -0.5

    def body(q_ref, k_ref, v_ref, fc_ref, fr_ref, o_ref, lse_ref, do_ref, dq_ref, dk_ref, dv_ref, dfq_ref, dfk_ref):
        j = pl.program_id(1)

        @pl.when(j == 0)
        def _():
            dq_ref[...] = jnp.zeros_like(dq_ref)
            dfq_ref[...] = jnp.zeros_like(dfq_ref)

        k = k_ref[0, 0].astype(MXU)
        v = v_ref[0, 0].astype(MXU)
        fk = fr_ref[0, 0]

        def qstep(i, carry):
            dk, dv, dfk = carry
            rows = pl.ds(pl.multiple_of(i * qb, qb), qb)
            q = q_ref[0, 0, rows, :].astype(MXU)
            dout = do_ref[0, rows, :]
            s = _dot_nt(q, k) * scale + (fc_ref[0, rows, :] - fk)
            s = jnp.where(_causal_mask(i, j, qb), s, -1e30)
            p = jnp.exp(s - lse_ref[0, rows, :])
            dp = _dot_nt(dout, v)
            drow = jnp.sum(dout * o_ref[0, rows, :], axis=1, keepdims=True)
            ds = p * (dp - drow)
            dq_ref[0, 0, rows, :] += _dot(ds, k) * scale
            dfq_ref[0, rows, :] += jnp.sum(ds, axis=1, keepdims=True)
            return (dk + _dot_tn(ds, q) * scale, dv + _dot_tn(p, dout), dfk - jnp.sum(ds, axis=0, keepdims=True))

        init = (jnp.zeros((qb, DH), F32), jnp.zeros((qb, DH), F32), jnp.zeros((1, qb), F32))
        dk, dv, dfk = lax.fori_loop(j, nq, qstep, init)
        dk_ref[0, 0] = dk
        dv_ref[0, 0] = dv
        dfk_ref[0, 0] = dfk

    full = pl.BlockSpec((1, t, DH), lambda h, j: (h, 0, 0))
    colf = pl.BlockSpec((1, t, 1), lambda h, j: (h, 0, 0))
    return pl.pallas_call(
        body, name="fox_bwd", grid=(HEADS, nq),
        in_specs=[pl.BlockSpec((1, 1, t, DH), lambda h, j: (0, h, 0, 0)),
                  pl.BlockSpec((1, 1, qb, DH), lambda h, j: (1, h, j, 0)),
                  pl.BlockSpec((1, 1, qb, DH), lambda h, j: (2, h, j, 0)),
                  colf, pl.BlockSpec((1, 1, 1, qb), lambda h, j: (h, j, 0, 0)), full, colf, full],
        out_specs=[pl.BlockSpec((1, 1, t, DH), lambda h, j: (0, h, 0, 0)),
                   pl.BlockSpec((1, 1, qb, DH), lambda h, j: (0, h, j, 0)),
                   pl.BlockSpec((1, 1, qb, DH), lambda h, j: (0, h, j, 0)),
                   colf, pl.BlockSpec((1, 1, 1, qb), lambda h, j: (h, j, 0, 0))],
        out_shape=[jax.ShapeDtypeStruct((1, HEADS, t, DH), F32), jax.ShapeDtypeStruct((1, HEADS, t, DH), F32),
                   jax.ShapeDtypeStruct((1, HEADS, t, DH), F32), jax.ShapeDtypeStruct((HEADS, t, 1), F32),
                   jax.ShapeDtypeStruct((HEADS, nq, 1, qb), F32)],
        compiler_params=_params(("arbitrary", "arbitrary")),
    )(qkv, qkv, qkv, fcol, frow, o, lse, do)


Z_COL0 = 3 * WIDTH // LANE
FGATE_COL0 = 7 * WIDTH // LANE


def _gdn_post(o, pm, onw):
    t = pm.shape[0]

    def body(o_ref, z_ref, w_ref, m_ref):
        z = z_ref[...]
        sz = z * _sigmoid(z)
        for hh in range(2):
            ov = o_ref[hh]
            n = ov * lax.rsqrt(jnp.mean(ov * ov, axis=-1, keepdims=True) + EPS) * w_ref[...]
            m_ref[:, hh * DH:(hh + 1) * DH] = (n * sz[:, hh * DH:(hh + 1) * DH]).astype(m_ref.dtype)

    return pl.pallas_call(
        body, name="gdn_post", grid=(WIDTH // LANE,),
        in_specs=[pl.BlockSpec((2, t, DH), lambda j: (j, 0, 0)), pl.BlockSpec((t, LANE), lambda j: (0, Z_COL0 + j)),
                  pl.BlockSpec((1, DH), lambda j: (0, 0))],
        out_specs=pl.BlockSpec((t, LANE), lambda j: (0, j)),
        out_shape=jax.ShapeDtypeStruct((t, WIDTH), MXU),
        compiler_params=_params(("arbitrary",)),
    )(o, pm, onw)


def _gdn_post_bwd(o, pm, onw, dmix):
    t = pm.shape[0]

    def body(o_ref, z_ref, w_ref, dm_ref, do_ref, dz_ref, dw_ref):
        @pl.when(pl.program_id(0) == 0)
        def _():
            dw_ref[...] = jnp.zeros_like(dw_ref)

        z = z_ref[...]
        sg = _sigmoid(z)
        sz = z * sg
        dsz = sg * (1.0 + z * (1.0 - sg))
        dm = dm_ref[...]
        for hh in range(2):
            cols = slice(hh * DH, (hh + 1) * DH)
            ov = o_ref[hh]
            r = lax.rsqrt(jnp.mean(ov * ov, axis=-1, keepdims=True) + EPS)
            xn = ov * r
            dmh = dm[:, cols]
            dn = dmh * sz[:, cols]
            dz_ref[:, cols] = dmh * (xn * w_ref[...]) * dsz[:, cols]
            dw_ref[...] += jnp.sum(dn * xn, axis=0, keepdims=True)
            g = dn * w_ref[...]
            do_ref[hh] = r * (g - xn * jnp.mean(g * xn, axis=-1, keepdims=True))

    return pl.pallas_call(
        body, name="gdn_post_bwd", grid=(WIDTH // LANE,),
        in_specs=[pl.BlockSpec((2, t, DH), lambda j: (j, 0, 0)), pl.BlockSpec((t, LANE), lambda j: (0, Z_COL0 + j)),
                  pl.BlockSpec((1, DH), lambda j: (0, 0)), pl.BlockSpec((t, LANE), lambda j: (0, j))],
        out_specs=[pl.BlockSpec((2, t, DH), lambda j: (j, 0, 0)), pl.BlockSpec((t, LANE), lambda j: (0, j)),
                   pl.BlockSpec((1, DH), lambda j: (0, 0))],
        out_shape=[jax.ShapeDtypeStruct((HEADS, t, DH), F32), jax.ShapeDtypeStruct((t, WIDTH), F32),
                   jax.ShapeDtypeStruct((1, DH), F32)],
        compiler_params=_params(("arbitrary",)),
    )(o, pm, onw, dmix)


def _fox_post(o, pm):
    t = pm.shape[0]

    def body(o_ref, g_ref, m_ref):
        sg = _sigmoid(g_ref[...])
        for hh in range(2):
            cols = slice(hh * DH, (hh + 1) * DH)
            m_ref[:, cols] = (o_ref[hh] * sg[:, cols]).astype(m_ref.dtype)

    return pl.pallas_call(
        body, name="fox_post", grid=(WIDTH // LANE,),
        in_specs=[pl.BlockSpec((2, t, DH), lambda j: (j, 0, 0)), pl.BlockSpec((t, LANE), lambda j: (0, FGATE_COL0 + j))],
        out_specs=pl.BlockSpec((t, LANE), lambda j: (0, j)),
        out_shape=jax.ShapeDtypeStruct((t, WIDTH), MXU),
        compiler_params=_params(("arbitrary",)),
    )(o, pm)


def _fox_post_bwd(o, pm, dmix):
    t = pm.shape[0]

    def body(o_ref, g_ref, dm_ref, do_ref, dg_ref):
        sg = _sigmoid(g_ref[...])
        dm = dm_ref[...]
        for hh in range(2):
            cols = slice(hh * DH, (hh + 1) * DH)
            do_ref[hh] = dm[:, cols] * sg[:, cols]
            dg_ref[:, cols] = dm[:, cols] * o_ref[hh] * (sg * (1.0 - sg))[:, cols]

    return pl.pallas_call(
        body, name="fox_post_bwd", grid=(WIDTH // LANE,),
        in_specs=[pl.BlockSpec((2, t, DH), lambda j: (j, 0, 0)), pl.BlockSpec((t, LANE), lambda j: (0, FGATE_COL0 + j)),
                  pl.BlockSpec((t, LANE), lambda j: (0, j))],
        out_specs=[pl.BlockSpec((2, t, DH), lambda j: (j, 0, 0)), pl.BlockSpec((t, LANE), lambda j: (0, j))],
        out_shape=[jax.ShapeDtypeStruct((HEADS, t, DH), F32), jax.ShapeDtypeStruct((t, WIDTH), F32)],
        compiler_params=_params(("arbitrary",)),
    )(o, pm, dmix)


def _tail(x, mixg, mixf, tgt, wo, n2w, wg, wu, wd, fw):
    t, d = x.shape
    dff = wg.shape[1]
    tb = min(TB, t)

    def body(x_ref, mg_ref, mf_ref, t_ref, wo_ref, n2_ref, wg_ref, wu_ref, wd_ref, fw_ref,
             h2_ref, act_ref, dgate_ref, dup_ref, dx3_ref, dx2_ref, dmg_ref, dmf_ref, dn2_ref, dfw_ref, loss_ref):
        @pl.when(pl.program_id(0) == 0)
        def _():
            dn2_ref[...] = jnp.zeros_like(dn2_ref)
            dfw_ref[...] = jnp.zeros_like(dfw_ref)
            loss_ref[...] = jnp.zeros_like(loss_ref)

        x2 = x_ref[...] + _dot(mg_ref[...], wo_ref[0:WIDTH, :]) + _dot(mf_ref[...], wo_ref[WIDTH:2 * WIDTH, :])
        r2 = lax.rsqrt(jnp.mean(x2 * x2, axis=-1, keepdims=True) + EPS)
        xn2 = x2 * r2
        h2 = (xn2 * n2_ref[...]).astype(MXU)
        h2_ref[...] = h2
        gate = _dot(h2, wg_ref[...])
        up = _dot(h2, wu_ref[...])
        sg = _sigmoid(gate)
        sl = gate * sg
        act = (sl * up).astype(MXU)
        act_ref[...] = act
        x3 = x2 + _dot(act, wd_ref[...])
        r3 = lax.rsqrt(jnp.mean(x3 * x3, axis=-1, keepdims=True) + EPS)
        xn3 = x3 * r3
        err = xn3 * fw_ref[...] - t_ref[...]
        loss_ref[...] += 0.5 * jnp.sum(jnp.mean(err * err, axis=-1, keepdims=True), axis=0, keepdims=True)
        dy = err * (1.0 / d)
        dfw_ref[...] += jnp.sum(dy * xn3, axis=0, keepdims=True)
        g3 = dy * fw_ref[...]
        dx3 = r3 * (g3 - xn3 * jnp.mean(g3 * xn3, axis=-1, keepdims=True))
        dx3_ref[...] = dx3.astype(MXU)
        dact = _dot_nt(dx3, wd_ref[...])
        dgate = (dact * up * (sg * (1.0 + gate * (1.0 - sg)))).astype(MXU)
        dup = (dact * sl).astype(MXU)
        dgate_ref[...] = dgate
        dup_ref[...] = dup
        dh2 = _dot_nt(dgate, wg_ref[...]) + _dot_nt(dup, wu_ref[...])
        dn2_ref[...] += jnp.sum(dh2 * xn2, axis=0, keepdims=True)
        g2 = dh2 * n2_ref[...]
        dx2 = dx3 + r2 * (g2 - xn2 * jnp.mean(g2 * xn2, axis=-1, keepdims=True))
        dx2_ref[...] = dx2
        dmg_ref[...] = _dot_nt(dx2, wo_ref[0:WIDTH, :])
        dmf_ref[...] = _dot_nt(dx2, wo_ref[WIDTH:2 * WIDTH, :])

    def tok(n):
        return pl.BlockSpec((tb, n), lambda i: (i, 0))

    acc = pl.BlockSpec((1, d), lambda i: (0, 0))
    sds = jax.ShapeDtypeStruct
    return pl.pallas_call(
        body, name="tail", grid=(t // tb,),
        in_specs=[tok(d), tok(WIDTH), tok(WIDTH), tok(d), _resident(wo.shape), _resident((1, d)),
                  _resident(wg.shape), _resident(wu.shape), _resident(wd.shape), _resident((1, d))],
        out_specs=[tok(d), tok(dff), tok(dff), tok(dff), tok(d), tok(d), tok(WIDTH), tok(WIDTH), acc, acc,
                   pl.BlockSpec((1, 1), lambda i: (0, 0))],
        out_shape=[sds((t, d), MXU), sds((t, dff), MXU), sds((t, dff), MXU), sds((t, dff), MXU), sds((t, d), MXU),
                   sds((t, d), F32), sds((t, WIDTH), F32), sds((t, WIDTH), F32), sds((1, d), F32), sds((1, d), F32),
                   sds((1, 1), F32)],
        compiler_params=_params(("arbitrary",)),
    )(x, mixg, mixf, tgt, wo, n2w, wg, wu, wd, fw)


def _wgrad(a, b, name):
    t, m = a.shape
    n = b.shape[1]
    bn = 256 if n % 256 == 0 else LANE

    def body(a_ref, b_ref, o_ref):
        o_ref[...] = _dot_tn(a_ref[...], b_ref[...]).astype(o_ref.dtype)

    return pl.pallas_call(
        body, name=name, grid=(n // bn,),
        in_specs=[_resident((t, m)), pl.BlockSpec((t, bn), lambda j: (0, j))],
        out_specs=pl.BlockSpec((m, bn), lambda j: (0, j)),
        out_shape=jax.ShapeDtypeStruct((m, n), WIRE),
        compiler_params=_params(("arbitrary",)),
    )(a, b)


def _split_w_in(w_in):
    a = 4 * WIDTH
    b = a + 2 * HEADS
    c = b + 4 * WIDTH
    main = jnp.concatenate([w_in[:, :a], w_in[:, b:c]], axis=1)
    small = jnp.concatenate([w_in[:, a:b], w_in[:, c:], jnp.zeros((w_in.shape[0], LANE - 3 * HEADS), w_in.dtype)], axis=1)
    return main, small


def _merge_dw_in(d_gdn, d_z, d_fox, d_fg, d_small):
    return jnp.concatenate([d_gdn, d_z, d_small[:, :2 * HEADS], d_fox, d_fg, d_small[:, 2 * HEADS:3 * HEADS]], axis=1)


def _lanes(*pieces):
    v = jnp.concatenate([p.reshape(-1).astype(F32) for p in pieces])
    return jnp.pad(v, (0, LANE - v.shape[0])).reshape(1, LANE)


def _heads_to_cols(a):
    return a[:, :, 0].T


def _local_step(x, tgt, n1w, w_in, conv_w, a_log, dt_bias, onw, f_bias, qnw, knw, wo, n2w, wg, wu, wd, fw):
    t, d = x.shape
    qb = min(QB, t)
    w_main, w_small = _split_w_in(w_in)
    n1w, n2w, fw = n1w.reshape(1, d), n2w.reshape(1, d), fw.reshape(1, d)
    onw = onw.reshape(1, DH)
    gparams = jnp.concatenate([_lanes(jnp.zeros(HEADS), dt_bias, f_bias), _lanes(jnp.zeros(HEADS), a_log),
                               jnp.zeros((6, LANE), F32)])
    fox_nw = jnp.stack([jnp.tile(qnw.reshape(-1), 2), jnp.tile(knw.reshape(-1), 2), jnp.ones((LANE,), F32)])

    h1, pm, ps = _inproj(x, n1w, w_main, w_small)
    gates, run, tot = _gates(ps, gparams)
    gch = run[:, HEADS:2 * HEADS].T
    zeros_ht = jnp.zeros_like(gch)
    cols = jnp.stack([gates[:, 0:HEADS].T, gch, tot[:, HEADS:2 * HEADS].T] + [zeros_ht] * 5, axis=-1)
    grow = gch.reshape(HEADS, t // SC, 1, SC)
    fh = run[:, 2 * HEADS:3 * HEADS].T
    fcol = fh[:, :, None]
    frow = fh.reshape(HEADS, t // qb, 1, qb)
    gqkv = _gdn_prep(pm, conv_w)
    o_gdn, states, inv = _gdn_fwd(gqkv, cols, grow)
    mixg = _gdn_post(o_gdn, pm, onw)
    fqkv = _fox_prep(pm, fox_nw)
    o_fox, lse = _fox_fwd(fqkv, fcol, frow)
    mixf = _fox_post(o_fox, pm)

    (h2, act, dgate, dup, dx3, dx2, dmixg, dmixf, dn2w, dfw, loss) = _tail(x, mixg, mixf, tgt, wo, n2w, wg, wu, wd, fw)

    do_gdn, dz, donw = _gdn_post_bwd(o_gdn, pm, onw, dmixg)
    dgqkv, dcols, dgrow = _gdn_bwd(gqkv, cols, grow, inv, states, do_gdn)
    dgdn, dconv = _gdn_prep_bwd(pm, conv_w, dgqkv)
    do_fox, dfg = _fox_post_bwd(o_fox, pm, dmixf)
    dfq, dfk, dfv, dfc, dfr = _fox_bwd(fqkv, fcol, frow, o_fox, lse, do_fox)
    dfox, dfnw = _fox_prep_bwd(pm, fox_nw, jnp.concatenate([dfq, dfk, dfv], axis=0))
    zpad = jnp.zeros((t, LANE - 3 * HEADS), F32)
    z8 = jnp.zeros((t, HEADS), F32)
    d1 = jnp.concatenate([dcols[:, :, 0].T, dcols[:, :, 1].T, _heads_to_cols(dfc), zpad], axis=1)
    d2 = jnp.concatenate([z8, dgrow.reshape(HEADS, t).T, dfr.reshape(HEADS, t).T, zpad], axis=1)
    d3 = jnp.concatenate([z8, dcols[:, :, 2].T, z8, zpad], axis=1)
    dps, gsum = _gates_bwd(ps, gparams, d1, d2, d3)
    grad_x, dn1w = _inproj_bwd(x, n1w, dx2, dgdn, dz, dfox, dfg, dps, w_main, w_small)

    dw_in = _merge_dw_in(_wgrad(h1, dgdn, "dw_in_gdn"), _wgrad(h1, dz, "dw_in_z"), _wgrad(h1, dfox, "dw_in_fox"),
                         _wgrad(h1, dfg, "dw_in_fgate"), _wgrad(h1, dps, "dw_in_small"))
    dwo = jnp.concatenate([_wgrad(mixg, dx2, "dw_out_gdn"), _wgrad(mixf, dx2, "dw_out_fox")], axis=0)
    dfnw = dfnw.sum(axis=1).reshape(3, 2, DH).sum(axis=1)
    grads = {
        "norm1_w": dn1w, "w_in": dw_in, "gdn_conv_w": dconv,
        "gdn_A_log": gsum[0, HEADS:2 * HEADS], "gdn_dt_bias": gsum[1, HEADS:2 * HEADS],
        "gdn_out_norm_w": donw, "fox_f_bias": gsum[1, 2 * HEADS:3 * HEADS],
        "fox_q_norm_w": dfnw[0], "fox_k_norm_w": dfnw[1], "w_out": dwo, "norm2_w": dn2w,
        "w_ffn_gate": _wgrad(h2, dgate, "dw_gate"), "w_ffn_up": _wgrad(h2, dup, "dw_up"),
        "w_ffn_down": _wgrad(act, dx3, "dw_down"), "final_norm_w": dfw,
    }
    return loss[0, 0], grad_x, grads


def _my_place():
    return lax.axis_index("x"), lax.axis_index("y"), lax.axis_index("c")


def _all_gather(shard):
    r, l = shard.shape

    def body(x_ref, out_ref, send_sems, recv_sems, local_sem):
        x, y, c = _my_place()
        me, sibling = (x, y, c), (x, y, 1 - c)
        chips = [(1 - x, y), (x, 1 - y), (1 - x, 1 - y)]

        def blk(px, py, pc):
            return out_ref.at[4 * px + 2 * py + pc]

        def copy(k, block, to, src=None):
            return pltpu.make_async_remote_copy(
                src_ref=blk(*block) if src is None else src, dst_ref=blk(*block),
                send_sem=send_sems.at[k], recv_sem=recv_sems.at[k], device_id=to, device_id_type=MESH)

        mine = pltpu.make_async_copy(x_ref, blk(*me), local_sem)
        mine.start()
        first = [copy(0, me, sibling, src=x_ref)]
        first += [copy(1 + j, me, (*chip, c), src=x_ref) for j, chip in enumerate(chips)]
        for cp in first:
            cp.start()
        passed = [copy(4 + j, (*chip, c), sibling) for j, chip in enumerate(chips)]
        for j, chip in enumerate(chips):
            copy(1 + j, (*chip, c), me).wait_recv()
            passed[j].start()
        copy(0, sibling, me).wait_recv()
        for j, chip in enumerate(chips):
            copy(4 + j, (*chip, 1 - c), me).wait_recv()
        for cp in first + passed:
            cp.wait_send()
        mine.wait()

    return pl.pallas_call(
        body, name="gather_weights",
        out_shape=jax.ShapeDtypeStruct((N_DEV, r, l), shard.dtype),
        in_specs=[pl.BlockSpec(memory_space=pl.ANY)], out_specs=pl.BlockSpec(memory_space=pl.ANY),
        scratch_shapes=[pltpu.SemaphoreType.DMA((7,)), pltpu.SemaphoreType.DMA((7,)), pltpu.SemaphoreType.DMA],
    )(shard)


def _exchange(big, small):
    def body(big_ref, small_ref, rbig_ref, rsmall_ref, send_sems, recv_sems, local_sems):
        x, y, c = _my_place()
        me = 4 * x + 2 * y + c
        local = [pltpu.make_async_copy(big_ref.at[me], rbig_ref.at[me], local_sems.at[0]),
                 pltpu.make_async_copy(small_ref, rsmall_ref.at[me], local_sems.at[1])]
        for cp in local:
            cp.start()
        sends, recvs = [], []
        for k in range(1, N_DEV):
            px = 1 - x if k & 4 else x
            py = 1 - y if k & 2 else y
            pc = 1 - c if k & 1 else c
            peer = 4 * px + 2 * py + pc
            sends.append(pltpu.make_async_remote_copy(
                src_ref=big_ref.at[peer], dst_ref=rbig_ref.at[me], send_sem=send_sems.at[k - 1],
                recv_sem=recv_sems.at[k - 1], device_id=(px, py, pc), device_id_type=MESH))
            sends.append(pltpu.make_async_remote_copy(
                src_ref=small_ref, dst_ref=rsmall_ref.at[me], send_sem=send_sems.at[6 + k],
                recv_sem=recv_sems.at[6 + k], device_id=(px, py, pc), device_id_type=MESH))
            recvs.append(pltpu.make_async_remote_copy(
                src_ref=big_ref.at[peer], dst_ref=rbig_ref.at[peer], send_sem=send_sems.at[k - 1],
                recv_sem=recv_sems.at[k - 1], device_id=(px, py, pc), device_id_type=MESH))
            recvs.append(pltpu.make_async_remote_copy(
                src_ref=small_ref, dst_ref=rsmall_ref.at[peer], send_sem=send_sems.at[6 + k],
                recv_sem=recv_sems.at[6 + k], device_id=(px, py, pc), device_id_type=MESH))
        for cp in sends:
            cp.start()
        for cp in recvs:
            cp.wait_recv()
        for cp in sends:
            cp.wait_send()
        for cp in local:
            cp.wait()

    n = 2 * (N_DEV - 1)
    return pl.pallas_call(
        body, name="exchange_grads",
        out_shape=[jax.ShapeDtypeStruct(big.shape, big.dtype), jax.ShapeDtypeStruct((N_DEV,) + small.shape, small.dtype)],
        in_specs=[pl.BlockSpec(memory_space=pl.ANY)] * 2, out_specs=[pl.BlockSpec(memory_space=pl.ANY)] * 2,
        scratch_shapes=[pltpu.SemaphoreType.DMA((n,)), pltpu.SemaphoreType.DMA((n,)), pltpu.SemaphoreType.DMA((2,))],
    )(big, small)


def _row_block(r, cap=1024):
    return max(b for b in range(8, cap + 1, 8) if r % b == 0)


def _adam(parts, w, m, v, name):
    r = w.shape[0]
    rb = _row_block(r)

    def body(p_ref, w_ref, m_ref, v_ref, g_ref, d_ref, nm_ref, nv_ref):
        g = p_ref[0].astype(F32)
        for s in range(1, N_DEV):
            g = g + p_ref[s].astype(F32)
        nm = ADAM_B1 * m_ref[...] + (1.0 - ADAM_B1) * g
        nv = ADAM_B2 * v_ref[...] + (1.0 - ADAM_B2) * (g * g)
        m_hat = nm / (1.0 - ADAM_B1 ** ADAM_STEP)
        v_hat = nv / (1.0 - ADAM_B2 ** ADAM_STEP)
        g_ref[...] = g
        d_ref[...] = -ADAM_LR * (m_hat / (jnp.sqrt(v_hat) + ADAM_EPS) + ADAM_WD * w_ref[...])
        nm_ref[...] = nm
        nv_ref[...] = nv

    row = pl.BlockSpec((rb, LANE), lambda i: (i, 0))
    return pl.pallas_call(
        body, name=name, grid=(r // rb,),
        in_specs=[pl.BlockSpec((N_DEV, rb, LANE), lambda i: (0, i, 0)), row, row, row],
        out_specs=[row] * 4, out_shape=[jax.ShapeDtypeStruct((r, LANE), F32)] * 4,
        compiler_params=_params(("arbitrary",)),
    )(parts, w, m, v)


MATRICES = (("w_in", 1), ("gdn_conv_w", 1), ("w_out", 0), ("w_ffn_gate", 1), ("w_ffn_up", 1), ("w_ffn_down", 0))
VECTORS = ("norm1_w", "norm2_w", "final_norm_w", "gdn_A_log", "gdn_dt_bias", "gdn_out_norm_w", "fox_f_bias",
           "fox_q_norm_w", "fox_k_norm_w")
WEIGHTS = ("norm1_w", "w_in", "gdn_conv_w", "gdn_A_log", "gdn_dt_bias", "gdn_out_norm_w", "fox_f_bias", "fox_q_norm_w",
           "fox_k_norm_w", "w_out", "norm2_w", "w_ffn_gate", "w_ffn_up", "w_ffn_down", "final_norm_w")


def _rows(flat, align):
    n = flat.shape[-1]
    per = align * LANE
    total = -(-n // per) * per
    pad = [(0, 0)] * (flat.ndim - 1) + [(0, total - n)]
    return jnp.pad(flat, pad).reshape(flat.shape[:-1] + (total // LANE, LANE))


def _f32_as_wire(a):
    if jnp.dtype(WIRE).itemsize == 4:
        return a.astype(WIRE)
    return lax.bitcast_convert_type(a, WIRE).reshape(a.shape[:-1] + (-1,))


def _wire_as_f32(a):
    if jnp.dtype(WIRE).itemsize == 4:
        return a.astype(F32)
    k = 4 // jnp.dtype(WIRE).itemsize
    return lax.bitcast_convert_type(a.reshape(a.shape[:-1] + (-1, k)), F32)


def _join_shards(seg, shape, axis):
    seg = seg.reshape((N_DEV,) + shape)
    if axis == 0:
        return seg.reshape(N_DEV * shape[0], shape[1])
    return seg.transpose(1, 0, 2).reshape(shape[0], N_DEV * shape[1])


def _cut_shards(full, axis):
    if axis == 0:
        return full.reshape(N_DEV, -1)
    rows, cols = full.shape
    return full.reshape(rows, N_DEV, cols // N_DEV).transpose(1, 0, 2).reshape(N_DEV, -1)


def kernel(x, norm1_w, w_in, gdn_conv_w, gdn_A_log, gdn_dt_bias, gdn_out_norm_w, fox_f_bias, fox_q_norm_w, fox_k_norm_w, w_out, norm2_w, w_ffn_gate, w_ffn_up, w_ffn_down, final_norm_w, loss_target, m_norm1_w, m_w_in, m_gdn_conv_w, m_gdn_A_log, m_gdn_dt_bias, m_gdn_out_norm_w, m_fox_f_bias, m_fox_q_norm_w, m_fox_k_norm_w, m_w_out, m_norm2_w, m_w_ffn_gate, m_w_ffn_up, m_w_ffn_down, m_final_norm_w, v_norm1_w, v_w_in, v_gdn_conv_w, v_gdn_A_log, v_gdn_dt_bias, v_gdn_out_norm_w, v_fox_f_bias, v_fox_q_norm_w, v_fox_k_norm_w, v_w_out, v_norm2_w, v_w_ffn_gate, v_w_ffn_up, v_w_ffn_down, v_final_norm_w):
    w = dict(norm1_w=norm1_w, w_in=w_in, gdn_conv_w=gdn_conv_w, gdn_A_log=gdn_A_log, gdn_dt_bias=gdn_dt_bias,
             gdn_out_norm_w=gdn_out_norm_w, fox_f_bias=fox_f_bias, fox_q_norm_w=fox_q_norm_w, fox_k_norm_w=fox_k_norm_w,
             w_out=w_out, norm2_w=norm2_w, w_ffn_gate=w_ffn_gate, w_ffn_up=w_ffn_up, w_ffn_down=w_ffn_down,
             final_norm_w=final_norm_w)
    m = dict(norm1_w=m_norm1_w, w_in=m_w_in, gdn_conv_w=m_gdn_conv_w, gdn_A_log=m_gdn_A_log, gdn_dt_bias=m_gdn_dt_bias,
             gdn_out_norm_w=m_gdn_out_norm_w, fox_f_bias=m_fox_f_bias, fox_q_norm_w=m_fox_q_norm_w,
             fox_k_norm_w=m_fox_k_norm_w, w_out=m_w_out, norm2_w=m_norm2_w, w_ffn_gate=m_w_ffn_gate,
             w_ffn_up=m_w_ffn_up, w_ffn_down=m_w_ffn_down, final_norm_w=m_final_norm_w)
    v = dict(norm1_w=v_norm1_w, w_in=v_w_in, gdn_conv_w=v_gdn_conv_w, gdn_A_log=v_gdn_A_log, gdn_dt_bias=v_gdn_dt_bias,
             gdn_out_norm_w=v_gdn_out_norm_w, fox_f_bias=v_fox_f_bias, fox_q_norm_w=v_fox_q_norm_w,
             fox_k_norm_w=v_fox_k_norm_w, w_out=v_w_out, norm2_w=v_norm2_w, w_ffn_gate=v_w_ffn_gate,
             w_ffn_up=v_w_ffn_up, w_ffn_down=v_w_ffn_down, final_norm_w=v_final_norm_w)
    shard_shape = {n: w[n].shape[1:] for n, _ in MATRICES}
    size = {n: shard_shape[n][0] * shard_shape[n][1] for n, _ in MATRICES}

    wire_flat = [w[n].reshape(-1).astype(WIRE) for n, _ in MATRICES if n != "gdn_conv_w"]
    wire_flat.append(_f32_as_wire(w["gdn_conv_w"].reshape(-1)))
    gathered = _all_gather(_rows(jnp.concatenate(wire_flat), ROW_ALIGN))
    gathered = gathered.reshape(N_DEV, -1)
    full, off = {}, 0
    for n, axis in MATRICES:
        if n != "gdn_conv_w":
            full[n] = _join_shards(gathered[:, off:off + size[n]], shard_shape[n], axis)
            off += size[n]
    k = 4 // jnp.dtype(WIRE).itemsize
    full["gdn_conv_w"] = _join_shards(_wire_as_f32(gathered[:, off:off + k * size["gdn_conv_w"]]),
                                      shard_shape["gdn_conv_w"], 1)

    loss, grad_x, grads = _local_step(
        x[0], loss_target[0], w["norm1_w"][0], full["w_in"], full["gdn_conv_w"], w["gdn_A_log"][0],
        w["gdn_dt_bias"][0], w["gdn_out_norm_w"][0], w["fox_f_bias"][0], w["fox_q_norm_w"][0], w["fox_k_norm_w"][0],
        full["w_out"], w["norm2_w"][0], full["w_ffn_gate"], full["w_ffn_up"], full["w_ffn_down"], w["final_norm_w"])

    big = _rows(jnp.concatenate([_cut_shards(grads[n].astype(WIRE), axis) for n, axis in MATRICES], axis=1), ROW_ALIGN)
    vec = _rows(jnp.concatenate([grads[n].reshape(-1) for n in VECTORS] + [loss.reshape(1)]), 8)
    rbig, rvec = _exchange(big, vec)

    def pack_matrices(d):
        return _rows(jnp.concatenate([d[n].reshape(-1) for n, _ in MATRICES]), ROW_ALIGN)

    def pack_vectors(d):
        return _rows(jnp.concatenate([d[n].reshape(-1) for n in VECTORS] + [jnp.zeros((1,), F32)]), 8)

    big_out = _adam(rbig, pack_matrices(w), pack_matrices(m), pack_matrices(v), "adam_matrices")
    vec_out = _adam(rvec, pack_vectors(w), pack_vectors(m), pack_vectors(v), "adam_vectors")

    results = []
    for bo, vo in zip(big_out, vec_out):
        bo, vo = bo.reshape(-1), vo.reshape(-1)
        d, off = {}, 0
        for n, _ in MATRICES:
            d[n] = bo[off:off + size[n]].reshape(w[n].shape)
            off += size[n]
        off = 0
        for n in VECTORS:
            d[n] = vo[off:off + w[n].size].reshape(w[n].shape)
            off += w[n].size
        results.append(d)
    total_loss = vec_out[0].reshape(-1)[sum(w[n].size for n in VECTORS)]
    return (total_loss, grad_x[None], *[d[n] for d in results for n in WEIGHTS])
```

```python
import functools

import jax
import jax.numpy as jnp
from jax import lax
from jax.experimental import pallas as pl
from jax.experimental.pallas import tpu as pltpu

F32 = jnp.float32
MXU = jnp.bfloat16
WIRE = jnp.bfloat16
HI = lax.Precision.HIGHEST
EPS = 1e-6

N_DEV = 8
HEADS = 8
DH = 64
WIDTH = HEADS * DH
CHUNK = 64
LANE = 128
ROW_ALIGN = 16
TB = 256
QB = 256
VMEM_LIMIT = 60 * 1024 * 1024

ADAM_LR = 0.001
ADAM_B1 = 0.9
ADAM_B2 = 0.999
ADAM_EPS = 1e-08
ADAM_WD = 0.01
ADAM_STEP = 10

MESH = pl.DeviceIdType.MESH


def _params(sem=None):
    return pltpu.CompilerParams(dimension_semantics=sem, vmem_limit_bytes=VMEM_LIMIT)


def _resident(shape):
    n = len(shape)
    return pl.BlockSpec(shape, lambda *_: (0,) * n, pipeline_mode=pl.Buffered(1))


def _dot(a, b):
    return jnp.dot(a.astype(MXU), b.astype(MXU), preferred_element_type=F32)


def _dot_nt(a, b):
    return lax.dot_general(a.astype(MXU), b.astype(MXU), (((1,), (1,)), ((), ())), preferred_element_type=F32)


def _dot_tn(a, b):
    return lax.dot_general(a.astype(MXU), b.astype(MXU), (((0,), (0,)), ((), ())), preferred_element_type=F32)


def _hdot(a, b):
    return jnp.dot(a, b, precision=HI, preferred_element_type=F32)


def _hdot_nt(a, b):
    return lax.dot_general(a, b, (((1,), (1,)), ((), ())), precision=HI, preferred_element_type=F32)


def _hdot_tn(a, b):
    return lax.dot_general(a, b, (((0,), (0,)), ((), ())), precision=HI, preferred_element_type=F32)


def _sigmoid(x):
    return 1.0 / (1.0 + jnp.exp(-x))


def _softplus(x):
    return jnp.maximum(x, 0.0) + jnp.log(1.0 + jnp.exp(-jnp.abs(x)))


def _head_sum_matrix():
    ri = lax.broadcasted_iota(jnp.int32, (LANE, LANE), 0) // DH
    ci = lax.broadcasted_iota(jnp.int32, (LANE, LANE), 1) // DH
    return (ri == ci).astype(F32)


def _shift_down(x, s, rows):
    return jnp.where(rows >= s, pltpu.roll(x, s, 0), 0.0)


def _shift_up(x, s, rows):
    t = x.shape[0]
    return jnp.where(rows < t - s, pltpu.roll(x, t - s, 0), 0.0)


def _inproj(x, n1w, w_main, w_small):
    t, d = x.shape
    nm = w_main.shape[1]
    tb = min(TB, t)

    def body(x_ref, nw_ref, wm_ref, ws_ref, h_ref, pm_ref, ps_ref):
        xv = x_ref[...]
        r = lax.rsqrt(jnp.mean(xv * xv, axis=-1, keepdims=True) + EPS)
        h = (xv * r * nw_ref[...]).astype(MXU)
        h_ref[...] = h
        pm_ref[...] = jnp.dot(h, wm_ref[...], preferred_element_type=F32)
        ps_ref[...] = jnp.dot(h, ws_ref[...], preferred_element_type=F32)

    return pl.pallas_call(
        body, name="inproj", grid=(t // tb,),
        in_specs=[pl.BlockSpec((tb, d), lambda i: (i, 0)), _resident((1, d)), _resident((d, nm)), _resident((d, LANE))],
        out_specs=[pl.BlockSpec((tb, d), lambda i: (i, 0)), pl.BlockSpec((tb, nm), lambda i: (i, 0)),
                   pl.BlockSpec((tb, LANE), lambda i: (i, 0))],
        out_shape=[jax.ShapeDtypeStruct((t, d), MXU), jax.ShapeDtypeStruct((t, nm), F32),
                   jax.ShapeDtypeStruct((t, LANE), F32)],
        compiler_params=_params(("arbitrary",)),
    )(x, n1w, w_main, w_small)


def _inproj_bwd(x, n1w, dx2, dgdn, dz, dfox, dfg, dps, w_main, w_small):
    t, d = x.shape
    tb = min(TB, t)
    w3 = 3 * WIDTH

    def body(x_ref, nw_ref, dx2_ref, dgdn_ref, dz_ref, dfox_ref, dfg_ref, dps_ref, wm_ref, ws_ref, gx_ref, dnw_ref):
        dh = _dot_nt(dgdn_ref[...], wm_ref[:, 0:w3])
        dh += _dot_nt(dz_ref[...], wm_ref[:, w3:w3 + WIDTH])
        dh += _dot_nt(dfox_ref[...], wm_ref[:, w3 + WIDTH:2 * w3 + WIDTH])
        dh += _dot_nt(dfg_ref[...], wm_ref[:, 2 * w3 + WIDTH:2 * w3 + 2 * WIDTH])
        dh += _dot_nt(dps_ref[...], ws_ref[...])
        xv = x_ref[...]
        r = lax.rsqrt(jnp.mean(xv * xv, axis=-1, keepdims=True) + EPS)
        xn = xv * r

        @pl.when(pl.program_id(0) == 0)
        def _():
            dnw_ref[...] = jnp.zeros_like(dnw_ref)

        dnw_ref[...] += jnp.sum(dh * xn, axis=0, keepdims=True)
        g = dh * nw_ref[...]
        gx_ref[...] = dx2_ref[...] + r * (g - xn * jnp.mean(g * xn, axis=-1, keepdims=True))

    def tok(n):
        return pl.BlockSpec((tb, n), lambda i: (i, 0))

    return pl.pallas_call(
        body, name="inproj_bwd", grid=(t // tb,),
        in_specs=[tok(d), _resident((1, d)), tok(d), tok(w3), tok(WIDTH), tok(w3), tok(WIDTH), tok(LANE),
                  _resident(w_main.shape), _resident(w_small.shape)],
        out_specs=[tok(d), pl.BlockSpec((1, d), lambda i: (0, 0))],
        out_shape=[jax.ShapeDtypeStruct((t, d), F32), jax.ShapeDtypeStruct((1, d), F32)],
        compiler_params=_params(("arbitrary",)),
    )(x, n1w, dx2, dgdn, dz, dfox, dfg, dps, w_main, w_small)


def _gate_lanes(shape):
    lane = lax.broadcasted_iota(jnp.int32, shape, 1)
    return lane < HEADS, (lane >= HEADS) & (lane < 2 * HEADS), (lane >= 2 * HEADS) & (lane < 3 * HEADS)


def _block_masks():
    ri = lax.broadcasted_iota(jnp.int32, (LANE, LANE), 0)
    ci = lax.broadcasted_iota(jnp.int32, (LANE, LANE), 1)
    same = (ri // CHUNK) == (ci // CHUNK)
    return ((ri >= ci).astype(F32), (ri <= ci).astype(F32), (same & (ri >= ci)).astype(F32),
            (same & (ri <= ci)).astype(F32), same.astype(F32))


def _gates(ps, gparams):
    t = ps.shape[0]
    nb = t // LANE

    def body(ps_ref, gp_ref, out_ref, run_ref, tot_ref, runt_ref):
        p = ps_ref[...]
        is_b, is_a, is_f = _gate_lanes(p.shape)
        z = p + gp_ref[0:1, :]
        neg_exp_a = -jnp.exp(gp_ref[1:2, :])
        glog = neg_exp_a * _softplus(z)
        logf = -_softplus(-z)
        out_ref[...] = jnp.where(is_b, _sigmoid(p), jnp.where(is_a, glog, jnp.where(is_f, logf, 0.0)))
        tril, _, tril_c, _, same_c = _block_masks()
        off = jnp.zeros((1, LANE), F32)
        for b in range(nb):
            rows = slice(b * LANE, (b + 1) * LANE)
            blk = out_ref[rows, :]
            ga = jnp.where(is_a[:LANE], blk, 0.0)
            fb = _hdot(tril, jnp.where(is_f[:LANE], blk, 0.0)) + off
            run = fb + _hdot(tril_c, ga)
            run_ref[rows, :] = run
            runt_ref[:, rows] = run.T
            tot_ref[rows, :] = _hdot(same_c, ga)
            off = fb[LANE - 1:LANE, :]

    return pl.pallas_call(
        body, name="gates",
        out_shape=[jax.ShapeDtypeStruct((t, LANE), F32)] * 3 + [jax.ShapeDtypeStruct((LANE, t), F32)],
        compiler_params=_params(),
    )(ps, gparams)


def _gates_bwd(ps, gparams, dcol_g, dtot_g, dcol_f, drow_g, drow_f):
    t = ps.shape[0]
    nb = t // LANE

    def body(ps_ref, gp_ref, dcg_ref, dtg_ref, dcf_ref, drg_ref, drf_ref, dps_ref, sums_ref, dl_ref, d0_ref, tr_ref):
        p = ps_ref[...]
        is_b, is_a, is_f = _gate_lanes(p.shape)
        _, triu, _, triu_c, same_c = _block_masks()
        tr_ref[...] = jnp.zeros_like(tr_ref)
        off = jnp.zeros((1, LANE), F32)
        for b in reversed(range(nb)):
            rows = slice(b * LANE, (b + 1) * LANE)
            tr_ref[HEADS:2 * HEADS, :] = drg_ref[:, rows]
            tr_ref[2 * HEADS:3 * HEADS, :] = drf_ref[:, rows]
            d = dcg_ref[rows, :] + dcf_ref[rows, :] + tr_ref[...].T
            d0_ref[rows, :] = d
            dlf = _hdot(triu, jnp.where(is_f[:LANE], d, 0.0)) + off
            dla = (_hdot(triu_c, jnp.where(is_a[:LANE], d, 0.0))
                   + _hdot(same_c, jnp.where(is_a[:LANE], dtg_ref[rows, :], 0.0)))
            dl_ref[rows, :] = dlf + dla
            off = dlf[0:1, :]
        z = p + gp_ref[0:1, :]
        neg_exp_a = -jnp.exp(gp_ref[1:2, :])
        sb = _sigmoid(p)
        glog = neg_exp_a * _softplus(z)
        dl = dl_ref[...]
        dp = jnp.where(is_b, d0_ref[...] * sb * (1.0 - sb),
                       jnp.where(is_a, dl * neg_exp_a * _sigmoid(z), jnp.where(is_f, dl * _sigmoid(-z), 0.0)))
        dps_ref[...] = dp
        s_a = jnp.sum(jnp.where(is_a, dl * glog, 0.0), axis=0, keepdims=True)
        s_p = jnp.sum(jnp.where(is_b, 0.0, dp), axis=0, keepdims=True)
        row = lax.broadcasted_iota(jnp.int32, (8, LANE), 0)
        from_a = pltpu.roll(jnp.where(row == 0, s_a, jnp.where(row == 1, s_p, 0.0)), LANE - HEADS, 1)
        from_f = pltpu.roll(jnp.where(row == 2, s_p, 0.0), LANE - 2 * HEADS, 1)
        lane = lax.broadcasted_iota(jnp.int32, (8, LANE), 1)
        sums_ref[...] = jnp.where(lane < HEADS, from_a + from_f, 0.0)

    return pl.pallas_call(
        body, name="gates_bwd",
        out_shape=[jax.ShapeDtypeStruct((t, LANE), F32), jax.ShapeDtypeStruct((8, LANE), F32)],
        scratch_shapes=[pltpu.VMEM((t, LANE), F32), pltpu.VMEM((t, LANE), F32), pltpu.VMEM((LANE, LANE), F32)],
        compiler_params=_params(),
    )(ps, gparams, dcol_g, dtot_g, dcol_f, drow_g, drow_f)


def _conv(xv, w, rows):
    acc = w[3:4, :] * xv
    for s in range(1, 4):
        acc += w[3 - s:4 - s, :] * _shift_down(xv, s, rows)
    return acc


def _gdn_prep(pm, conv_w):
    t = pm.shape[0]
    nj = WIDTH // LANE

    def body(x_ref, w_ref, o_ref):
        kind = pl.program_id(0)
        xv = x_ref[...]
        rows = lax.broadcasted_iota(jnp.int32, xv.shape, 0)
        acc = _conv(xv, w_ref[...], rows)
        y = acc * _sigmoid(acc)
        ss = _hdot(y * y, _head_sum_matrix())
        out = jnp.where(kind < 2, y * lax.rsqrt(ss + EPS), y)
        o_ref[0, 0] = out[:, :DH]
        o_ref[0, 1] = out[:, DH:]

    return pl.pallas_call(
        body, name="gdn_prep", grid=(3, nj),
        in_specs=[pl.BlockSpec((t, LANE), lambda i, j: (0, i * nj + j)),
                  pl.BlockSpec((4, LANE), lambda i, j: (0, i * nj + j))],
        out_specs=pl.BlockSpec((1, 2, t, DH), lambda i, j: (i, j, 0, 0)),
        out_shape=jax.ShapeDtypeStruct((3, HEADS, t, DH), F32),
        compiler_params=_params(("arbitrary", "arbitrary")),
    )(pm, conv_w)


def _gdn_prep_bwd(pm, conv_w, dqkv):
    t = pm.shape[0]
    nj = WIDTH // LANE

    def body(x_ref, w_ref, d_ref, dx_ref, dw_ref):
        kind = pl.program_id(0)
        xv = x_ref[...]
        w = w_ref[...]
        rows = lax.broadcasted_iota(jnp.int32, xv.shape, 0)
        acc = _conv(xv, w, rows)
        sg = _sigmoid(acc)
        y = acc * sg
        hs = _head_sum_matrix()
        r = lax.rsqrt(_hdot(y * y, hs) + EPS)
        yn = y * r
        d = jnp.concatenate([d_ref[0, 0], d_ref[0, 1]], axis=1)
        dy = jnp.where(kind < 2, r * (d - yn * _hdot(d * yn, hs)), d)
        dacc = dy * sg * (1.0 + acc * (1.0 - sg))
        dx = w[3:4, :] * dacc
        dw_ref[3:4, :] = jnp.sum(dacc * xv, axis=0, keepdims=True)
        for s in range(1, 4):
            dx += w[3 - s:4 - s, :] * _shift_up(dacc, s, rows)
            dw_ref[3 - s:4 - s, :] = jnp.sum(dacc * _shift_down(xv, s, rows), axis=0, keepdims=True)
        dx_ref[...] = dx

    return pl.pallas_call(
        body, name="gdn_prep_bwd", grid=(3, nj),
        in_specs=[pl.BlockSpec((t, LANE), lambda i, j: (0, i * nj + j)),
                  pl.BlockSpec((4, LANE), lambda i, j: (0, i * nj + j)),
                  pl.BlockSpec((1, 2, t, DH), lambda i, j: (i, j, 0, 0))],
        out_specs=[pl.BlockSpec((t, LANE), lambda i, j: (0, i * nj + j)),
                   pl.BlockSpec((4, LANE), lambda i, j: (0, i * nj + j))],
        out_shape=[jax.ShapeDtypeStruct((t, 3 * WIDTH), F32), jax.ShapeDtypeStruct((4, 3 * WIDTH), F32)],
        compiler_params=_params(("arbitrary", "arbitrary")),
    )(pm, conv_w, dqkv)


FOX_COL0 = 4 * WIDTH // LANE


def _fox_prep(pm, nw):
    t = pm.shape[0]
    nj = WIDTH // LANE

    def body(x_ref, w_ref, o_ref):
        kind = pl.program_id(0)
        xv = x_ref[...]
        ms = _hdot(xv * xv, _head_sum_matrix()) * (1.0 / DH)
        yn = xv * lax.rsqrt(ms + EPS) * w_ref[pl.ds(kind, 1), :]
        out = jnp.where(kind < 2, yn, xv)
        o_ref[0, 0] = out[:, :DH]
        o_ref[0, 1] = out[:, DH:]

    return pl.pallas_call(
        body, name="fox_prep", grid=(3, nj),
        in_specs=[pl.BlockSpec((t, LANE), lambda i, j: (0, FOX_COL0 + i * nj + j)),
                  pl.BlockSpec((3, LANE), lambda i, j: (0, 0))],
        out_specs=pl.BlockSpec((1, 2, t, DH), lambda i, j: (i, j, 0, 0)),
        out_shape=jax.ShapeDtypeStruct((3, HEADS, t, DH), F32),
        compiler_params=_params(("arbitrary", "arbitrary")),
    )(pm, nw)


def _fox_prep_bwd(pm, nw, dqkv):
    t = pm.shape[0]
    nj = WIDTH // LANE

    def body(x_ref, w_ref, d_ref, dx_ref, dw_ref):
        kind = pl.program_id(0)
        xv = x_ref[...]
        hs = _head_sum_matrix()
        r = lax.rsqrt(_hdot(xv * xv, hs) * (1.0 / DH) + EPS)
        xn = xv * r
        d = jnp.concatenate([d_ref[0, 0], d_ref[0, 1]], axis=1)
        g = d * w_ref[pl.ds(kind, 1), :]
        dxn = r * (g - xn * _hdot(g * xn, hs) * (1.0 / DH))
        dx_ref[...] = jnp.where(kind < 2, dxn, d)
        dw_ref[0, 0] = jnp.sum(d * xn, axis=0, keepdims=True)

    return pl.pallas_call(
        body, name="fox_prep_bwd", grid=(3, nj),
        in_specs=[pl.BlockSpec((t, LANE), lambda i, j: (0, FOX_COL0 + i * nj + j)),
                  pl.BlockSpec((3, LANE), lambda i, j: (0, 0)),
                  pl.BlockSpec((1, 2, t, DH), lambda i, j: (i, j, 0, 0))],
        out_specs=[pl.BlockSpec((t, LANE), lambda i, j: (0, i * nj + j)),
                   pl.BlockSpec((1, 1, 1, LANE), lambda i, j: (i, j, 0, 0))],
        out_shape=[jax.ShapeDtypeStruct((t, 3 * WIDTH), F32), jax.ShapeDtypeStruct((3, nj, 1, LANE), F32)],
        compiler_params=_params(("arbitrary", "arbitrary")),
    )(pm, nw, dqkv)


SC = 256
CPS = SC // CHUNK
Q_SCALE = DH ** -0.5


def _sc_masks():
    ri = lax.broadcasted_iota(jnp.int32, (SC, SC), 0)
    ci = lax.broadcasted_iota(jnp.int32, (SC, SC), 1)
    same = (ri // CHUNK) == (ci // CHUNK)
    return same & (ri >= ci), same & (ri > ci), ri == ci


def _unit_lower_inverse(m, eye):
    inv = jnp.where(eye, 1.0, 0.0) + m
    for _ in range(5):
        m = _dot(m, m)
        inv = inv + _dot(inv, m)
    return inv


def _lane_col(blk, lane_idx):
    lane = lax.broadcasted_iota(jnp.int32, blk.shape, 1)
    return jnp.sum(jnp.where(lane == lane_idx, blk, 0.0), axis=1, keepdims=True)


def _to_lane(col, lane_idx):
    lane = lax.broadcasted_iota(jnp.int32, (col.shape[0], LANE), 1)
    return jnp.where(lane == lane_idx, col, 0.0)


def _gdn_columns(gates_ref, run_ref, tot_ref, runt_ref, rows, h):
    return (_lane_col(gates_ref[rows, :], h), _lane_col(run_ref[rows, :], HEADS + h),
            _lane_col(tot_ref[rows, :], HEADS + h), runt_ref[pl.ds(h, 1), rows])


def _gdn_local(q, k, beta, gc, gl, grow, causal):
    decay = jnp.exp(jnp.where(causal, gc - grow, -1e30))
    egc = jnp.exp(gc)
    ekd = jnp.exp(gl - gc)
    qs = q * Q_SCALE
    kb = k * beta
    kk = _dot_nt(kb, k)
    qk = _dot_nt(qs, k)
    return beta, gl, decay, egc, ekd, qs, kb, kk, qk, jnp.where(causal, qk * decay, 0.0)


def _chunk_rows(c):
    return pl.ds(pl.multiple_of(c * CHUNK, CHUNK), CHUNK)


def _sc_rows(b):
    return pl.ds(pl.multiple_of(b * SC, SC), SC)


def _gdn_fwd(qkv, gates, run, tot, run_t):
    t = qkv.shape[2]
    nc = t // CHUNK
    nsc = t // SC

    def body(qkv_ref, gates_ref, run_ref, tot_ref, runt_ref, o_ref, st_ref, inv_ref, kc_s, qc_s, g_s, au_s):
        h = pl.program_id(0)
        causal, strict, eye = _sc_masks()

        def local(b, _):
            rows = _sc_rows(b)
            q, k, v = qkv_ref[0, 0, rows, :], qkv_ref[1, 0, rows, :], qkv_ref[2, 0, rows, :]
            beta, _, decay, egc, ekd, qs, kb, kk, _, attn = _gdn_local(
                q, k, *_gdn_columns(gates_ref, run_ref, tot_ref, runt_ref, rows, h), causal)
            inv = _unit_lower_inverse(-jnp.where(strict, kk * decay, 0.0), eye)
            inv_ref[0, rows, :] = inv.astype(inv_ref.dtype)
            u = _dot(inv, v * beta)
            w = _dot(inv, kb * egc)
            g_s[rows, :] = qs * egc - _dot(attn, w)
            au_s[rows, :] = _dot(attn, u)
            kd = k * ekd
            for j in range(CPS):
                sl = slice(j * CHUNK, (j + 1) * CHUNK)
                kc_s[b * CPS + j] = _dot_tn(kd[sl], w[sl])
                qc_s[b * CPS + j] = _dot_tn(kd[sl], u[sl])
            return 0

        lax.fori_loop(0, nsc, local, 0)

        def step(c, s):
            rows = _chunk_rows(c)
            st_ref[0, c] = s
            o_ref[0, rows, :] = _dot(g_s[rows, :], s) + au_s[rows, :]
            egl = jnp.exp(_lane_col(tot_ref[pl.ds(c * CHUNK, 1), :], HEADS + h))
            return egl * s - _dot(kc_s[c], s) + qc_s[c]

        lax.fori_loop(0, nc, step, jnp.zeros((DH, DH), F32))

    whole = pl.BlockSpec((t, LANE), lambda h: (0, 0))
    return pl.pallas_call(
        body, name="gdn_fwd", grid=(HEADS,),
        in_specs=[pl.BlockSpec((3, 1, t, DH), lambda h: (0, h, 0, 0)), whole, whole, whole,
                  pl.BlockSpec((HEADS, t), lambda h: (1, 0))],
        out_specs=[pl.BlockSpec((1, t, DH), lambda h: (h, 0, 0)), pl.BlockSpec((1, nc, DH, DH), lambda h: (h, 0, 0, 0)),
                   pl.BlockSpec((1, t, SC), lambda h: (h, 0, 0))],
        out_shape=[jax.ShapeDtypeStruct((HEADS, t, DH), F32), jax.ShapeDtypeStruct((HEADS, nc, DH, DH), F32),
                   jax.ShapeDtypeStruct((HEADS, t, SC), MXU)],
        scratch_shapes=[pltpu.VMEM((nc, DH, DH), F32), pltpu.VMEM((nc, DH, DH), F32),
                        pltpu.VMEM((t, DH), F32), pltpu.VMEM((t, DH), F32)],
        compiler_params=_params(("arbitrary",)),
    )(qkv, gates, run, tot, run_t)


def _gdn_bwd(qkv, gates, run, tot, run_t, inv, states, do):
    t = qkv.shape[2]
    nc = t // CHUNK
    nsc = t // SC

    def body(qkv_ref, gates_ref, run_ref, tot_ref, runt_ref, inv_ref, st_ref, do_ref,
             dqkv_ref, dcol_ref, dtot_ref, drow_ref, u_s, w_s, kc_s, h_s, dsn_s):
        h = pl.program_id(0)
        causal, strict, _ = _sc_masks()

        @pl.when(h == 0)
        def _():
            dcol_ref[...] = jnp.zeros_like(dcol_ref)
            dtot_ref[...] = jnp.zeros_like(dtot_ref)

        def local(b, _):
            rows = _sc_rows(b)
            q, k, v = qkv_ref[0, 0, rows, :], qkv_ref[1, 0, rows, :], qkv_ref[2, 0, rows, :]
            beta, _, _, egc, ekd, qs, kb, _, _, attn = _gdn_local(
                q, k, *_gdn_columns(gates_ref, run_ref, tot_ref, runt_ref, rows, h), causal)
            inv_b = inv_ref[0, rows, :]
            u = _dot(inv_b, v * beta)
            w = _dot(inv_b, kb * egc)
            u_s[rows, :] = u
            w_s[rows, :] = w
            g = qs * egc - _dot(attn, w)
            kd = k * ekd
            dout = do_ref[0, rows, :]
            for j in range(CPS):
                sl = slice(j * CHUNK, (j + 1) * CHUNK)
                kc_s[b * CPS + j] = _dot_tn(kd[sl], w[sl])
                h_s[b * CPS + j] = _dot_tn(g[sl], dout[sl])
            return 0

        lax.fori_loop(0, nsc, local, 0)

        def step(i, ds):
            c = nc - 1 - i
            dsn_s[c] = ds
            egl = jnp.exp(_lane_col(tot_ref[pl.ds(c * CHUNK, 1), :], HEADS + h))
            return egl * ds - _dot_tn(kc_s[c], ds) + h_s[c]

        lax.fori_loop(0, nc, step, jnp.zeros((DH, DH), F32))

        def back(b, _):
            rows = _sc_rows(b)
            q, k, v = qkv_ref[0, 0, rows, :], qkv_ref[1, 0, rows, :], qkv_ref[2, 0, rows, :]
            beta, gl, decay, egc, ekd, qs, kb, kk, qk, attn = _gdn_local(
                q, k, *_gdn_columns(gates_ref, run_ref, tot_ref, runt_ref, rows, h), causal)
            u, w = u_s[rows, :], w_s[rows, :]
            kd = k * ekd
            vb = v * beta
            kbe = kb * egc
            dout = do_ref[0, rows, :]
            first = lax.broadcasted_iota(jnp.int32, (CHUNK, 1), 0) == 0
            dg_p, dkd_p, dw_p, du_p, dgl_p = [], [], [], [], []
            for j in range(CPS):
                sl = slice(j * CHUNK, (j + 1) * CHUNK)
                s = st_ref[0, b * CPS + j]
                dsn = dsn_s[b * CPS + j]
                dkc = -_dot_nt(dsn, s)
                dg_p.append(_dot_nt(dout[sl], s))
                dkd_p.append(_dot_nt(w[sl], dkc) + _dot_nt(u[sl], dsn))
                dw_p.append(_dot(kd[sl], dkc))
                du_p.append(_dot(kd[sl], dsn))
                degl = jnp.sum(jnp.sum(s * dsn, axis=1, keepdims=True), axis=0, keepdims=True)
                dgl_p.append(jnp.where(first, degl * jnp.exp(gl[j * CHUNK:j * CHUNK + 1, :]), 0.0))
            dg, dkd = jnp.concatenate(dg_p, axis=0), jnp.concatenate(dkd_p, axis=0)
            da = jnp.where(causal, _dot_nt(dout, u) - _dot_nt(dg, w), 0.0)
            at = _dot_tn(attn, jnp.concatenate([dout, dg], axis=1))
            du = at[:, :DH] + jnp.concatenate(du_p, axis=0)
            dw = jnp.concatenate(dw_p, axis=0) - at[:, DH:]
            inv_t = inv_ref[0, rows, :].astype(F32).T
            it = _dot(inv_t, jnp.concatenate([du, dw], axis=1))
            dvb, dkbe = it[:, :DH], it[:, DH:]
            dinv = _dot_nt(du, vb) + _dot_nt(dw, kbe)
            dl = jnp.where(strict, -_dot(_dot(inv_t, dinv), inv_t), 0.0)
            dlogd = (dl * kk + da * qk) * decay
            dkk = dl * decay
            dqk = da * decay
            dkb = _dot(dkk, k) + dkbe * egc
            dqs = _dot(dqk, k) + dg * egc
            dk = (_dot_tn(jnp.concatenate([dkk, dqk], axis=0), jnp.concatenate([kb, qs], axis=0))
                  + dkd * ekd + dkb * beta)
            dkd_kd = jnp.sum(dkd * kd, axis=1, keepdims=True)
            dgc = (jnp.sum(dlogd, axis=1, keepdims=True) + jnp.sum(dg * qs, axis=1, keepdims=True) * egc
                   + jnp.sum(dkbe * kbe, axis=1, keepdims=True) - dkd_kd)
            dgl = dkd_kd + jnp.concatenate(dgl_p, axis=0)
            dbeta = jnp.sum(dkb * k, axis=1, keepdims=True) + jnp.sum(dvb * v, axis=1, keepdims=True)
            dqkv_ref[0, 0, rows, :] = dqs * Q_SCALE
            dqkv_ref[1, 0, rows, :] = dk
            dqkv_ref[2, 0, rows, :] = dvb * beta
            dcol_ref[rows, :] += _to_lane(dbeta, h) + _to_lane(dgc, HEADS + h)
            dtot_ref[rows, :] += _to_lane(dgl, HEADS + h)
            drow_ref[pl.ds(h, 1), rows] = -jnp.sum(dlogd, axis=0, keepdims=True)
            return 0

        lax.fori_loop(0, nsc, back, 0)

    whole = pl.BlockSpec((t, LANE), lambda h: (0, 0))
    rowspec = pl.BlockSpec((HEADS, t), lambda h: (0, 0))
    sq = pltpu.VMEM((nc, DH, DH), F32)
    return pl.pallas_call(
        body, name="gdn_bwd", grid=(HEADS,),
        in_specs=[pl.BlockSpec((3, 1, t, DH), lambda h: (0, h, 0, 0)), whole, whole, whole,
                  pl.BlockSpec((HEADS, t), lambda h: (1, 0)),
                  pl.BlockSpec((1, t, SC), lambda h: (h, 0, 0)), pl.BlockSpec((1, nc, DH, DH), lambda h: (h, 0, 0, 0)),
                  pl.BlockSpec((1, t, DH), lambda h: (h, 0, 0))],
        out_specs=[pl.BlockSpec((3, 1, t, DH), lambda h: (0, h, 0, 0)), whole, whole, rowspec],
        out_shape=[jax.ShapeDtypeStruct((3, HEADS, t, DH), F32), jax.ShapeDtypeStruct((t, LANE), F32),
                   jax.ShapeDtypeStruct((t, LANE), F32), jax.ShapeDtypeStruct((HEADS, t), F32)],
        scratch_shapes=[pltpu.VMEM((t, DH), F32), pltpu.VMEM((t, DH), F32), sq, sq, sq],
        compiler_params=_params(("arbitrary",)),
    )(qkv, gates, run, tot, run_t, inv, states, do)


def _causal_mask(i, j, qb):
    qpos = i * qb + lax.broadcasted_iota(jnp.int32, (qb, qb), 0)
    kpos = j * qb + lax.broadcasted_iota(jnp.int32, (qb, qb), 1)
    return qpos >= kpos


def _fox_fwd(qkv, run, run_t):
    t = qkv.shape[2]
    qb = min(QB, t)
    nq = t // qb
    scale = DH ** -0.5

    def body(q_ref, k_ref, v_ref, run_ref, runt_ref, o_ref, lse_ref):
        h = pl.program_id(0)
        i = pl.program_id(1)
        q = q_ref[0, 0].astype(MXU)
        fq = _lane_col(run_ref[...], 2 * HEADS + h)

        def kstep(j, carry):
            m, l, acc = carry
            rows = pl.ds(pl.multiple_of(j * qb, qb), qb)
            s = _dot_nt(q, k_ref[0, 0, rows, :]) * scale + (fq - runt_ref[pl.ds(h, 1), rows])
            s = jnp.where(_causal_mask(i, j, qb), s, -1e30)
            m_new = jnp.maximum(m, jnp.max(s, axis=1, keepdims=True))
            p = jnp.exp(s - m_new)
            alpha = jnp.exp(m - m_new)
            l = alpha * l + jnp.sum(p, axis=1, keepdims=True)
            acc = alpha * acc + _dot(p, v_ref[0, 0, rows, :])
            return m_new, l, acc

        init = (jnp.full((qb, 1), -1e30, F32), jnp.zeros((qb, 1), F32), jnp.zeros((qb, DH), F32))
        m, l, acc = lax.fori_loop(0, i + 1, kstep, init)
        o_ref[0] = acc / l
        lse_ref[0] = m + jnp.log(l)

    return pl.pallas_call(
        body, name="fox_fwd", grid=(HEADS, nq),
        in_specs=[pl.BlockSpec((1, 1, qb, DH), lambda h, i: (0, h, i, 0)),
                  pl.BlockSpec((1, 1, t, DH), lambda h, i: (1, h, 0, 0)),
                  pl.BlockSpec((1, 1, t, DH), lambda h, i: (2, h, 0, 0)),
                  pl.BlockSpec((qb, LANE), lambda h, i: (i, 0)),
                  pl.BlockSpec((HEADS, t), lambda h, i: (2, 0))],
        out_specs=[pl.BlockSpec((1, qb, DH), lambda h, i: (h, i, 0)), pl.BlockSpec((1, qb, 1), lambda h, i: (h, i, 0))],
        out_shape=[jax.ShapeDtypeStruct((HEADS, t, DH), F32), jax.ShapeDtypeStruct((HEADS, t, 1), F32)],
        compiler_params=_params(("arbitrary", "arbitrary")),
    )(qkv, qkv, qkv, run, run_t)


def _fox_bwd(qkv, run, run_t, o, lse, do):
    t = qkv.shape[2]
    qb = min(QB, t)
    nq = t // qb
    scale = DH ** -0.5

    def body(q_ref, k_ref, v_ref, run_ref, runt_ref, o_ref, lse_ref, do_ref, dqkv_ref, dcol_ref, drow_ref):
        h = pl.program_id(0)
        j = pl.program_id(1)

        @pl.when(j == 0)
        def _():
            dqkv_ref[0] = jnp.zeros((1, t, DH), F32)

        @pl.when((j == 0) & (h == 0))
        def _():
            dcol_ref[...] = jnp.zeros_like(dcol_ref)

        k = k_ref[0, 0].astype(MXU)
        v = v_ref[0, 0].astype(MXU)
        krows = pl.ds(pl.multiple_of(j * qb, qb), qb)
        fk = runt_ref[pl.ds(h, 1), krows]

        def qstep(i, carry):
            dk, dv, dfk = carry
            rows = pl.ds(pl.multiple_of(i * qb, qb), qb)
            q = q_ref[0, 0, rows, :].astype(MXU)
            dout = do_ref[0, rows, :]
            s = _dot_nt(q, k) * scale + (_lane_col(run_ref[rows, :], 2 * HEADS + h) - fk)
            s = jnp.where(_causal_mask(i, j, qb), s, -1e30)
            p = jnp.exp(s - lse_ref[0, rows, :])
            dp = _dot_nt(dout, v)
            drow = jnp.sum(dout * o_ref[0, rows, :], axis=1, keepdims=True)
            ds = p * (dp - drow)
            dqkv_ref[0, 0, rows, :] += _dot(ds, k) * scale
            dcol_ref[rows, :] += _to_lane(jnp.sum(ds, axis=1, keepdims=True), 2 * HEADS + h)
            return (dk + _dot_tn(ds, q) * scale, dv + _dot_tn(p, dout), dfk - jnp.sum(ds, axis=0, keepdims=True))

        init = (jnp.zeros((qb, DH), F32), jnp.zeros((qb, DH), F32), jnp.zeros((1, qb), F32))
        dk, dv, dfk = lax.fori_loop(j, nq, qstep, init)
        dqkv_ref[1, 0, krows, :] = dk
        dqkv_ref[2, 0, krows, :] = dv
        drow_ref[pl.ds(h, 1), krows] = dfk

    full = pl.BlockSpec((1, t, DH), lambda h, j: (h, 0, 0))
    return pl.pallas_call(
        body, name="fox_bwd", grid=(HEADS, nq),
        in_specs=[pl.BlockSpec((1, 1, t, DH), lambda h, j: (0, h, 0, 0)),
                  pl.BlockSpec((1, 1, qb, DH), lambda h, j: (1, h, j, 0)),
                  pl.BlockSpec((1, 1, qb, DH), lambda h, j: (2, h, j, 0)),
                  pl.BlockSpec((t, LANE), lambda h, j: (0, 0)), pl.BlockSpec((HEADS, t), lambda h, j: (2, 0)),
                  full, pl.BlockSpec((1, t, 1), lambda h, j: (h, 0, 0)), full],
        out_specs=[pl.BlockSpec((3, 1, t, DH), lambda h, j: (0, h, 0, 0)),
                   pl.BlockSpec((t, LANE), lambda h, j: (0, 0)), pl.BlockSpec((HEADS, t), lambda h, j: (0, 0))],
        out_shape=[jax.ShapeDtypeStruct((3, HEADS, t, DH), F32), jax.ShapeDtypeStruct((t, LANE), F32),
                   jax.ShapeDtypeStruct((HEADS, t), F32)],
        compiler_params=_params(("arbitrary", "arbitrary")),
    )(qkv, qkv, qkv, run, run_t, o, lse, do)


Z_COL0 = 3 * WIDTH // LANE
FGATE_COL0 = 7 * WIDTH // LANE


def _gdn_post(o, pm, onw):
    t = pm.shape[0]

    def body(o_ref, z_ref, w_ref, m_ref):
        z = z_ref[...]
        sz = z * _sigmoid(z)
        for hh in range(2):
            ov = o_ref[hh]
            n = ov * lax.rsqrt(jnp.mean(ov * ov, axis=-1, keepdims=True) + EPS) * w_ref[...]
            m_ref[:, hh * DH:(hh + 1) * DH] = (n * sz[:, hh * DH:(hh + 1) * DH]).astype(m_ref.dtype)

    return pl.pallas_call(
        body, name="gdn_post", grid=(WIDTH // LANE,),
        in_specs=[pl.BlockSpec((2, t, DH), lambda j: (j, 0, 0)), pl.BlockSpec((t, LANE), lambda j: (0, Z_COL0 + j)),
                  pl.BlockSpec((1, DH), lambda j: (0, 0))],
        out_specs=pl.BlockSpec((t, LANE), lambda j: (0, j)),
        out_shape=jax.ShapeDtypeStruct((t, WIDTH), MXU),
        compiler_params=_params(("arbitrary",)),
    )(o, pm, onw)


def _gdn_post_bwd(o, pm, onw, dmix):
    t = pm.shape[0]

    def body(o_ref, z_ref, w_ref, dm_ref, do_ref, dz_ref, dw_ref):
        @pl.when(pl.program_id(0) == 0)
        def _():
            dw_ref[...] = jnp.zeros_like(dw_ref)

        z = z_ref[...]
        sg = _sigmoid(z)
        sz = z * sg
        dsz = sg * (1.0 + z * (1.0 - sg))
        dm = dm_ref[...]
        for hh in range(2):
            cols = slice(hh * DH, (hh + 1) * DH)
            ov = o_ref[hh]
            r = lax.rsqrt(jnp.mean(ov * ov, axis=-1, keepdims=True) + EPS)
            xn = ov * r
            dmh = dm[:, cols]
            dn = dmh * sz[:, cols]
            dz_ref[:, cols] = dmh * (xn * w_ref[...]) * dsz[:, cols]
            dw_ref[...] += jnp.sum(dn * xn, axis=0, keepdims=True)
            g = dn * w_ref[...]
            do_ref[hh] = r * (g - xn * jnp.mean(g * xn, axis=-1, keepdims=True))

    return pl.pallas_call(
        body, name="gdn_post_bwd", grid=(WIDTH // LANE,),
        in_specs=[pl.BlockSpec((2, t, DH), lambda j: (j, 0, 0)), pl.BlockSpec((t, LANE), lambda j: (0, Z_COL0 + j)),
                  pl.BlockSpec((1, DH), lambda j: (0, 0)), pl.BlockSpec((t, LANE), lambda j: (0, j))],
        out_specs=[pl.BlockSpec((2, t, DH), lambda j: (j, 0, 0)), pl.BlockSpec((t, LANE), lambda j: (0, j)),
                   pl.BlockSpec((1, DH), lambda j: (0, 0))],
        out_shape=[jax.ShapeDtypeStruct((HEADS, t, DH), F32), jax.ShapeDtypeStruct((t, WIDTH), F32),
                   jax.ShapeDtypeStruct((1, DH), F32)],
        compiler_params=_params(("arbitrary",)),
    )(o, pm, onw, dmix)


def _fox_post(o, pm):
    t = pm.shape[0]

    def body(o_ref, g_ref, m_ref):
        sg = _sigmoid(g_ref[...])
        for hh in range(2):
            cols = slice(hh * DH, (hh + 1) * DH)
            m_ref[:, cols] = (o_ref[hh] * sg[:, cols]).astype(m_ref.dtype)

    return pl.pallas_call(
        body, name="fox_post", grid=(WIDTH // LANE,),
        in_specs=[pl.BlockSpec((2, t, DH), lambda j: (j, 0, 0)), pl.BlockSpec((t, LANE), lambda j: (0, FGATE_COL0 + j))],
        out_specs=pl.BlockSpec((t, LANE), lambda j: (0, j)),
        out_shape=jax.ShapeDtypeStruct((t, WIDTH), MXU),
        compiler_params=_params(("arbitrary",)),
    )(o, pm)


def _fox_post_bwd(o, pm, dmix):
    t = pm.shape[0]

    def body(o_ref, g_ref, dm_ref, do_ref, dg_ref):
        sg = _sigmoid(g_ref[...])
        dm = dm_ref[...]
        for hh in range(2):
            cols = slice(hh * DH, (hh + 1) * DH)
            do_ref[hh] = dm[:, cols] * sg[:, cols]
            dg_ref[:, cols] = dm[:, cols] * o_ref[hh] * (sg * (1.0 - sg))[:, cols]

    return pl.pallas_call(
        body, name="fox_post_bwd", grid=(WIDTH // LANE,),
        in_specs=[pl.BlockSpec((2, t, DH), lambda j: (j, 0, 0)), pl.BlockSpec((t, LANE), lambda j: (0, FGATE_COL0 + j)),
                  pl.BlockSpec((t, LANE), lambda j: (0, j))],
        out_specs=[pl.BlockSpec((2, t, DH), lambda j: (j, 0, 0)), pl.BlockSpec((t, LANE), lambda j: (0, j))],
        out_shape=[jax.ShapeDtypeStruct((HEADS, t, DH), F32), jax.ShapeDtypeStruct((t, WIDTH), F32)],
        compiler_params=_params(("arbitrary",)),
    )(o, pm, dmix)


def _tail(x, mixg, mixf, tgt, wo, n2w, wg, wu, wd, fw):
    t, d = x.shape
    dff = wg.shape[1]
    tb = min(TB, t)

    def body(x_ref, mg_ref, mf_ref, t_ref, wo_ref, n2_ref, wg_ref, wu_ref, wd_ref, fw_ref,
             h2_ref, act_ref, dgate_ref, dup_ref, dx3_ref, dx2_ref, dmg_ref, dmf_ref, dn2_ref, dfw_ref, loss_ref):
        @pl.when(pl.program_id(0) == 0)
        def _():
            dn2_ref[...] = jnp.zeros_like(dn2_ref)
            dfw_ref[...] = jnp.zeros_like(dfw_ref)
            loss_ref[...] = jnp.zeros_like(loss_ref)

        x2 = x_ref[...] + _dot(mg_ref[...], wo_ref[0:WIDTH, :]) + _dot(mf_ref[...], wo_ref[WIDTH:2 * WIDTH, :])
        r2 = lax.rsqrt(jnp.mean(x2 * x2, axis=-1, keepdims=True) + EPS)
        xn2 = x2 * r2
        h2 = (xn2 * n2_ref[...]).astype(MXU)
        h2_ref[...] = h2
        gate = _dot(h2, wg_ref[...])
        up = _dot(h2, wu_ref[...])
        sg = _sigmoid(gate)
        sl = gate * sg
        act = (sl * up).astype(MXU)
        act_ref[...] = act
        x3 = x2 + _dot(act, wd_ref[...])
        r3 = lax.rsqrt(jnp.mean(x3 * x3, axis=-1, keepdims=True) + EPS)
        xn3 = x3 * r3
        err = xn3 * fw_ref[...] - t_ref[...]
        loss_ref[...] += 0.5 * jnp.sum(jnp.mean(err * err, axis=-1, keepdims=True), axis=0, keepdims=True)
        dy = err * (1.0 / d)
        dfw_ref[...] += jnp.sum(dy * xn3, axis=0, keepdims=True)
        g3 = dy * fw_ref[...]
        dx3 = r3 * (g3 - xn3 * jnp.mean(g3 * xn3, axis=-1, keepdims=True))
        dx3_ref[...] = dx3.astype(MXU)
        dact = _dot_nt(dx3, wd_ref[...])
        dgate = (dact * up * (sg * (1.0 + gate * (1.0 - sg)))).astype(MXU)
        dup = (dact * sl).astype(MXU)
        dgate_ref[...] = dgate
        dup_ref[...] = dup
        dh2 = _dot_nt(dgate, wg_ref[...]) + _dot_nt(dup, wu_ref[...])
        dn2_ref[...] += jnp.sum(dh2 * xn2, axis=0, keepdims=True)
        g2 = dh2 * n2_ref[...]
        dx2 = dx3 + r2 * (g2 - xn2 * jnp.mean(g2 * xn2, axis=-1, keepdims=True))
        dx2_ref[...] = dx2
        dmg_ref[...] = _dot_nt(dx2, wo_ref[0:WIDTH, :])
        dmf_ref[...] = _dot_nt(dx2, wo_ref[WIDTH:2 * WIDTH, :])

    def tok(n):
        return pl.BlockSpec((tb, n), lambda i: (i, 0))

    acc = pl.BlockSpec((1, d), lambda i: (0, 0))
    sds = jax.ShapeDtypeStruct
    return pl.pallas_call(
        body, name="tail", grid=(t // tb,),
        in_specs=[tok(d), tok(WIDTH), tok(WIDTH), tok(d), _resident(wo.shape), _resident((1, d)),
                  _resident(wg.shape), _resident(wu.shape), _resident(wd.shape), _resident((1, d))],
        out_specs=[tok(d), tok(dff), tok(dff), tok(dff), tok(d), tok(d), tok(WIDTH), tok(WIDTH), acc, acc,
                   pl.BlockSpec((1, 1), lambda i: (0, 0))],
        out_shape=[sds((t, d), MXU), sds((t, dff), MXU), sds((t, dff), MXU), sds((t, dff), MXU), sds((t, d), MXU),
                   sds((t, d), F32), sds((t, WIDTH), F32), sds((t, WIDTH), F32), sds((1, d), F32), sds((1, d), F32),
                   sds((1, 1), F32)],
        compiler_params=_params(("arbitrary",)),
    )(x, mixg, mixf, tgt, wo, n2w, wg, wu, wd, fw)


def _wgrad(a, b, name):
    t, m = a.shape
    n = b.shape[1]
    bn = 256 if n % 256 == 0 else LANE

    def body(a_ref, b_ref, o_ref):
        o_ref[...] = _dot_tn(a_ref[...], b_ref[...]).astype(o_ref.dtype)

    return pl.pallas_call(
        body, name=name, grid=(n // bn,),
        in_specs=[_resident((t, m)), pl.BlockSpec((t, bn), lambda j: (0, j))],
        out_specs=pl.BlockSpec((m, bn), lambda j: (0, j)),
        out_shape=jax.ShapeDtypeStruct((m, n), WIRE),
        compiler_params=_params(("arbitrary",)),
    )(a, b)


def _split_w_in(w_in):
    a = 4 * WIDTH
    b = a + 2 * HEADS
    c = b + 4 * WIDTH
    main = jnp.concatenate([w_in[:, :a], w_in[:, b:c]], axis=1)
    small = jnp.concatenate([w_in[:, a:b], w_in[:, c:], jnp.zeros((w_in.shape[0], LANE - 3 * HEADS), w_in.dtype)], axis=1)
    return main, small


def _merge_dw_in(d_gdn, d_z, d_fox, d_fg, d_small):
    return jnp.concatenate([d_gdn, d_z, d_small[:, :2 * HEADS], d_fox, d_fg, d_small[:, 2 * HEADS:3 * HEADS]], axis=1)


def _lanes(*pieces):
    v = jnp.concatenate([p.reshape(-1).astype(F32) for p in pieces])
    return jnp.pad(v, (0, LANE - v.shape[0])).reshape(1, LANE)


def _vector_params(p):
    d = p["norm1_w"].size
    gparams = jnp.concatenate([_lanes(jnp.zeros(HEADS), p["gdn_dt_bias"], p["fox_f_bias"]),
                               _lanes(jnp.zeros(HEADS), p["gdn_A_log"]), jnp.zeros((6, LANE), F32)])
    fox_nw = jnp.stack([jnp.tile(p["fox_q_norm_w"].reshape(-1), 2), jnp.tile(p["fox_k_norm_w"].reshape(-1), 2),
                        jnp.ones((LANE,), F32)])
    return dict(n1w=p["norm1_w"].reshape(1, d), n2w=p["norm2_w"].reshape(1, d), fw=p["final_norm_w"].reshape(1, d),
                onw=p["gdn_out_norm_w"].reshape(1, DH), gparams=gparams, fox_nw=fox_nw)


def _mixer_forward(x, vp, w_main, w_small, conv_w):
    h1, pm, ps = _inproj(x, vp["n1w"], w_main, w_small)
    gates, run, tot, run_t = _gates(ps, vp["gparams"])
    gqkv = _gdn_prep(pm, conv_w)
    o_gdn, states, inv = _gdn_fwd(gqkv, gates, run, tot, run_t)
    mixg = _gdn_post(o_gdn, pm, vp["onw"])
    fqkv = _fox_prep(pm, vp["fox_nw"])
    o_fox, lse = _fox_fwd(fqkv, run, run_t)
    mixf = _fox_post(o_fox, pm)
    return dict(h1=h1, pm=pm, ps=ps, gates=gates, run=run, tot=tot, run_t=run_t, gqkv=gqkv, o_gdn=o_gdn, states=states,
                inv=inv, mixg=mixg, fqkv=fqkv, o_fox=o_fox, lse=lse, mixf=mixf)


def _mixer_backward(x, vp, w_main, w_small, conv_w, f, dx2, dmixg, dmixf, onw):
    pm = f["pm"]
    do_gdn, dz, donw = _gdn_post_bwd(f["o_gdn"], pm, onw, dmixg)
    dgqkv, dcol_g, dtot_g, drow_g = _gdn_bwd(f["gqkv"], f["gates"], f["run"], f["tot"], f["run_t"], f["inv"], f["states"], do_gdn)
    dgdn, dconv = _gdn_prep_bwd(pm, conv_w, dgqkv)
    do_fox, dfg = _fox_post_bwd(f["o_fox"], pm, dmixf)
    dfqkv, dcol_f, drow_f = _fox_bwd(f["fqkv"], f["run"], f["run_t"], f["o_fox"], f["lse"], do_fox)
    dfox, dfnw = _fox_prep_bwd(pm, vp["fox_nw"], dfqkv)
    dps, gsum = _gates_bwd(f["ps"], vp["gparams"], dcol_g, dtot_g, dcol_f, drow_g, drow_f)
    grad_x, dn1w = _inproj_bwd(x, vp["n1w"], dx2, dgdn, dz, dfox, dfg, dps, w_main, w_small)
    h1 = f["h1"]
    dw_in = _merge_dw_in(_wgrad(h1, dgdn, "dw_in_gdn"), _wgrad(h1, dz, "dw_in_z"), _wgrad(h1, dfox, "dw_in_fox"),
                         _wgrad(h1, dfg, "dw_in_fgate"), _wgrad(h1, dps, "dw_in_small"))
    return grad_x, dw_in, dconv, dict(dn1w=dn1w, gsum=gsum, donw=donw, dfnw=dfnw)


VECTORS = ("norm1_w", "norm2_w", "final_norm_w", "gdn_A_log", "gdn_dt_bias", "gdn_out_norm_w", "fox_f_bias",
           "fox_q_norm_w", "fox_k_norm_w")
VEC_ROWS = 16
LOSS_ROW = len(VECTORS)


def _pack_vectors(dn1w, dn2w, dfw, gsum, donw, dfnw, loss):
    d = dn1w.shape[1]

    def body(n1_ref, n2_ref, fw_ref, gs_ref, on_ref, fn_ref, loss_ref, o_ref):
        o_ref[...] = jnp.zeros_like(o_ref)
        o_ref[0:1, :] = n1_ref[...]
        o_ref[1:2, :] = n2_ref[...]
        o_ref[2:3, :] = fw_ref[...]
        o_ref[3:4, 0:HEADS] = gs_ref[0:1, 0:HEADS]
        o_ref[4:5, 0:HEADS] = gs_ref[1:2, 0:HEADS]
        o_ref[5:6, 0:DH] = on_ref[...]
        o_ref[6:7, 0:HEADS] = gs_ref[2:3, 0:HEADS]
        for kind in range(2):
            v = fn_ref[kind, 0]
            for j in range(1, fn_ref.shape[1]):
                v = v + fn_ref[kind, j]
            o_ref[7 + kind:8 + kind, 0:DH] = v[:, :DH] + v[:, DH:]
        o_ref[LOSS_ROW:LOSS_ROW + 1, 0:1] = loss_ref[...]

    return pl.pallas_call(body, name="pack_vectors", out_shape=jax.ShapeDtypeStruct((VEC_ROWS, d), F32),
                          compiler_params=_params())(dn1w, dn2w, dfw, gsum, donw, dfnw, loss)


def _local_step(x, tgt, p, w_in, conv_w, wo, wg, wu, wd):
    vp = _vector_params(p)
    w_main, w_small = _split_w_in(w_in)
    f = _mixer_forward(x, vp, w_main, w_small, conv_w)
    (h2, act, dgate, dup, dx3, dx2, dmixg, dmixf, dn2w, dfw, loss) = _tail(
        x, f["mixg"], f["mixf"], tgt, wo, vp["n2w"], wg, wu, wd, vp["fw"])
    grad_x, dw_in, dconv, small = _mixer_backward(x, vp, w_main, w_small, conv_w, f, dx2, dmixg, dmixf, vp["onw"])
    grads = {"w_in": dw_in, "gdn_conv_w": dconv,
             "w_out": jnp.concatenate([_wgrad(f["mixg"], dx2, "dw_out_gdn"), _wgrad(f["mixf"], dx2, "dw_out_fox")], axis=0),
             "w_ffn_gate": _wgrad(h2, dgate, "dw_gate"), "w_ffn_up": _wgrad(h2, dup, "dw_up"),
             "w_ffn_down": _wgrad(act, dx3, "dw_down")}
    vec = _pack_vectors(small["dn1w"], dn2w, dfw, small["gsum"], small["donw"], small["dfnw"], loss)
    return grad_x, grads, vec


def _my_place():
    return lax.axis_index("x"), lax.axis_index("y"), lax.axis_index("c")


def _peers():
    x, y, c = _my_place()
    peers = []
    for k in range(1, N_DEV):
        px = 1 - x if k & 4 else x
        py = 1 - y if k & 2 else y
        pc = 1 - c if k & 1 else c
        peers.append(((px, py, pc), 4 * px + 2 * py + pc))
    return 4 * x + 2 * y + c, peers


def _spread_copies(kind, srcs, lands, send_sems, recv_sems):
    me, peers = _peers()
    kinds = [kind] * len(srcs) if isinstance(kind, str) else kind
    remote, local = [], []
    for i, (kd, src, land) in enumerate(zip(kinds, srcs, lands)):
        for k, (dev, idx) in enumerate(peers):
            remote.append(pltpu.make_async_remote_copy(
                src_ref=src if kd == "gather" else src.at[idx], dst_ref=land.at[me],
                send_sem=send_sems.at[i * (N_DEV - 1) + k], recv_sem=recv_sems.at[i * (N_DEV - 1) + k],
                device_id=dev, device_id_type=MESH))
        local.append((src if kd == "gather" else src.at[me], land.at[me]))
    return remote, local


def _land_shape(kind, a):
    return (N_DEV,) + a.shape if kind == "gather" else a.shape


HBM = pl.BlockSpec(memory_space=pltpu.HBM)
SEM = pl.BlockSpec(memory_space=pltpu.SEMAPHORE)


def _hbm(a):
    return pltpu.with_memory_space_constraint(a, pltpu.HBM)


def _spread_start(groups, kind, name):
    flat = [a for g in groups for a in g]
    n = len(flat)
    ng = len(groups)

    def body(*refs):
        srcs, lands = refs[:n], refs[n:2 * n]
        sems = refs[2 * n:2 * n + 2 * ng]
        token = refs[4 * n + 2 * ng]
        local_sems = refs[4 * n + 2 * ng + 1]
        off = 0
        locals_ = []
        for g in range(ng):
            m = len(groups[g])
            remote, local = _spread_copies(kind, srcs[off:off + m], lands[off:off + m], sems[2 * g], sems[2 * g + 1])
            for cp in remote:
                cp.start()
            locals_ += local
            off += m
        copies = [pltpu.make_async_copy(s, d, local_sems.at[i]) for i, (s, d) in enumerate(locals_)]
        for cp in copies:
            cp.start()
        for cp in copies:
            cp.wait()
        token[...] = jnp.zeros_like(token)

    sem_shapes = []
    for g in groups:
        sem_shapes += [pltpu.SemaphoreType.DMA((len(g) * (N_DEV - 1),))] * 2
    lands = [lax.empty(_land_shape(kind, a), a.dtype) for a in flat]
    out = pl.pallas_call(
        body, name=name,
        out_shape=tuple(sem_shapes) + tuple(pltpu.HBM(a.shape, a.dtype) for a in flat + lands)
        + (jax.ShapeDtypeStruct((8, LANE), F32),),
        in_specs=[HBM] * (2 * n), out_specs=tuple([SEM] * (2 * ng) + [HBM] * (2 * n) + [pl.BlockSpec(memory_space=pltpu.VMEM)]),
        input_output_aliases={j: 2 * ng + j for j in range(2 * n)},
        scratch_shapes=[pltpu.SemaphoreType.DMA((n,))],
        compiler_params=pltpu.CompilerParams(has_side_effects=pltpu.SideEffectType.DATAFLOW_SIDE_EFFECTING),
    )(*[_hbm(a) for a in flat], *[_hbm(a) for a in lands])
    states, off = [], 0
    for g in range(ng):
        m = len(groups[g])
        states.append((list(out[2 * ng + off:2 * ng + off + m]), list(out[2 * ng + n + off:2 * ng + n + off + m]),
                       out[2 * g], out[2 * g + 1]))
        off += m
    return states, out[-1]


def _spread_wait(state, kind, after, name):
    srcs, lands, send_sems, recv_sems = state
    n = len(srcs)

    def body(*refs):
        remote, _ = _spread_copies(kind, refs[:n], refs[n:2 * n], refs[2 * n], refs[2 * n + 1])
        for cp in remote:
            cp.wait_send()
        for cp in remote:
            cp.wait_recv()

    out = pl.pallas_call(
        body, name=name,
        out_shape=tuple(pltpu.HBM(a.shape, a.dtype) for a in srcs + lands),
        in_specs=[HBM] * (2 * n) + [SEM, SEM, pl.BlockSpec(memory_space=pl.ANY)], out_specs=tuple([HBM] * (2 * n)),
        input_output_aliases={j: j for j in range(2 * n)},
        compiler_params=pltpu.CompilerParams(has_side_effects=pltpu.SideEffectType.DATAFLOW_SIDE_EFFECTING),
    )(*srcs, *lands, send_sems, recv_sems, after)
    return list(out[n:])


def _exchange(scatter, gather, name):
    kinds = ["scatter"] * len(scatter) + ["gather"] * len(gather)
    arrays = list(scatter) + list(gather)
    n = len(arrays)

    def body(*refs):
        send_sems, recv_sems, local_sems = refs[2 * n:]
        remote, local = _spread_copies(kinds, refs[:n], refs[n:2 * n], send_sems, recv_sems)
        copies = [pltpu.make_async_copy(s, d, local_sems.at[i]) for i, (s, d) in enumerate(local)]
        for cp in remote + copies:
            cp.start()
        for cp in remote:
            cp.wait_recv()
        for cp in remote:
            cp.wait_send()
        for cp in copies:
            cp.wait()

    return pl.pallas_call(
        body, name=name,
        out_shape=[jax.ShapeDtypeStruct(_land_shape(k, a), a.dtype) for k, a in zip(kinds, arrays)],
        in_specs=[pl.BlockSpec(memory_space=pl.ANY)] * n, out_specs=[pl.BlockSpec(memory_space=pl.ANY)] * n,
        scratch_shapes=[pltpu.SemaphoreType.DMA((n * (N_DEV - 1),)), pltpu.SemaphoreType.DMA((n * (N_DEV - 1),)),
                        pltpu.SemaphoreType.DMA((n,))],
    )(*arrays)


ADAM_ROWS = 128


def _adam_math(g, w, m, v):
    nm = ADAM_B1 * m + (1.0 - ADAM_B1) * g
    nv = ADAM_B2 * v + (1.0 - ADAM_B2) * (g * g)
    m_hat = nm / (1.0 - ADAM_B1 ** ADAM_STEP)
    v_hat = nv / (1.0 - ADAM_B2 ** ADAM_STEP)
    return -ADAM_LR * (m_hat / (jnp.sqrt(v_hat) + ADAM_EPS) + ADAM_WD * w), nm, nv


def _sum_parts(p_ref):
    g = p_ref[0].astype(F32)
    for s in range(1, N_DEV):
        g = g + p_ref[s].astype(F32)
    return g


def _adam_matrix(parts, w, m, v, name):
    _, r, c = w.shape
    rb = ADAM_ROWS if r % ADAM_ROWS == 0 else r

    def body(p_ref, w_ref, m_ref, v_ref, g_ref, d_ref, nm_ref, nv_ref):
        g = _sum_parts(p_ref)
        g_ref[0] = g
        d_ref[0], nm_ref[0], nv_ref[0] = _adam_math(g, w_ref[0], m_ref[0], v_ref[0])

    blk = pl.BlockSpec((1, rb, c), lambda i: (0, i, 0))
    return pl.pallas_call(
        body, name=name, grid=(r // rb,),
        in_specs=[pl.BlockSpec((N_DEV, rb, c), lambda i: (0, i, 0)), blk, blk, blk],
        out_specs=[blk] * 4, out_shape=[jax.ShapeDtypeStruct(w.shape, F32)] * 4,
        compiler_params=_params(("arbitrary",)),
    )(parts, w, m, v)


def _adam_vectors(parts, ws, ms, vs):
    nv = len(ws)

    def body(*refs):
        p_ref = refs[0]
        w_refs, m_refs, v_refs = refs[1:1 + nv], refs[1 + nv:1 + 2 * nv], refs[1 + 2 * nv:1 + 3 * nv]
        outs = refs[1 + 3 * nv:]
        g_all = _sum_parts(p_ref)
        for i in range(nv):
            n = w_refs[i].shape[1]
            g = g_all[i:i + 1, 0:n]
            d, nm, nvv = _adam_math(g, w_refs[i][...], m_refs[i][...], v_refs[i][...])
            outs[i][...] = g
            outs[nv + i][...] = d
            outs[2 * nv + i][...] = nm
            outs[3 * nv + i][...] = nvv
        outs[4 * nv][...] = g_all[LOSS_ROW:LOSS_ROW + 1, 0:1]

    shapes = [jax.ShapeDtypeStruct(a.shape, F32) for a in ws]
    out = pl.pallas_call(body, name="adam_vectors", out_shape=shapes * 4 + [jax.ShapeDtypeStruct((1, 1), F32)],
                         compiler_params=_params())(parts, *ws, *ms, *vs)
    return out[:nv], out[nv:2 * nv], out[2 * nv:3 * nv], out[3 * nv:4 * nv], out[4 * nv]


MATRICES = (("w_in", 1), ("gdn_conv_w", 1), ("w_out", 0), ("w_ffn_gate", 1), ("w_ffn_up", 1), ("w_ffn_down", 0))
WEIGHTS = ("norm1_w", "w_in", "gdn_conv_w", "gdn_A_log", "gdn_dt_bias", "gdn_out_norm_w", "fox_f_bias", "fox_q_norm_w",
           "fox_k_norm_w", "w_out", "norm2_w", "w_ffn_gate", "w_ffn_up", "w_ffn_down", "final_norm_w")


def _join(blocks, axis):
    _, r, c = blocks.shape
    if axis == 0:
        return blocks.reshape(N_DEV * r, c)
    return blocks.transpose(1, 0, 2).reshape(r, N_DEV * c)


def _cut(full, axis):
    r, c = full.shape
    if axis == 0:
        return full.reshape(N_DEV, r // N_DEV, c)
    return full.reshape(r, N_DEV, c // N_DEV).transpose(1, 0, 2)


def kernel(x, norm1_w, w_in, gdn_conv_w, gdn_A_log, gdn_dt_bias, gdn_out_norm_w, fox_f_bias, fox_q_norm_w, fox_k_norm_w, w_out, norm2_w, w_ffn_gate, w_ffn_up, w_ffn_down, final_norm_w, loss_target, m_norm1_w, m_w_in, m_gdn_conv_w, m_gdn_A_log, m_gdn_dt_bias, m_gdn_out_norm_w, m_fox_f_bias, m_fox_q_norm_w, m_fox_k_norm_w, m_w_out, m_norm2_w, m_w_ffn_gate, m_w_ffn_up, m_w_ffn_down, m_final_norm_w, v_norm1_w, v_w_in, v_gdn_conv_w, v_gdn_A_log, v_gdn_dt_bias, v_gdn_out_norm_w, v_fox_f_bias, v_fox_q_norm_w, v_fox_k_norm_w, v_w_out, v_norm2_w, v_w_ffn_gate, v_w_ffn_up, v_w_ffn_down, v_final_norm_w):
    w = dict(norm1_w=norm1_w, w_in=w_in, gdn_conv_w=gdn_conv_w, gdn_A_log=gdn_A_log, gdn_dt_bias=gdn_dt_bias,
             gdn_out_norm_w=gdn_out_norm_w, fox_f_bias=fox_f_bias, fox_q_norm_w=fox_q_norm_w, fox_k_norm_w=fox_k_norm_w,
             w_out=w_out, norm2_w=norm2_w, w_ffn_gate=w_ffn_gate, w_ffn_up=w_ffn_up, w_ffn_down=w_ffn_down,
             final_norm_w=final_norm_w)
    m = dict(norm1_w=m_norm1_w, w_in=m_w_in, gdn_conv_w=m_gdn_conv_w, gdn_A_log=m_gdn_A_log, gdn_dt_bias=m_gdn_dt_bias,
             gdn_out_norm_w=m_gdn_out_norm_w, fox_f_bias=m_fox_f_bias, fox_q_norm_w=m_fox_q_norm_w,
             fox_k_norm_w=m_fox_k_norm_w, w_out=m_w_out, norm2_w=m_norm2_w, w_ffn_gate=m_w_ffn_gate,
             w_ffn_up=m_w_ffn_up, w_ffn_down=m_w_ffn_down, final_norm_w=m_final_norm_w)
    v = dict(norm1_w=v_norm1_w, w_in=v_w_in, gdn_conv_w=v_gdn_conv_w, gdn_A_log=v_gdn_A_log, gdn_dt_bias=v_gdn_dt_bias,
             gdn_out_norm_w=v_gdn_out_norm_w, fox_f_bias=v_fox_f_bias, fox_q_norm_w=v_fox_q_norm_w,
             fox_k_norm_w=v_fox_k_norm_w, w_out=v_w_out, norm2_w=v_norm2_w, w_ffn_gate=v_w_ffn_gate,
             w_ffn_up=v_w_ffn_up, w_ffn_down=v_w_ffn_down, final_norm_w=v_final_norm_w)
    axis_of = dict(MATRICES)
    late = ("w_out", "w_ffn_gate", "w_ffn_up", "w_ffn_down")
    xs, tgt = x[0], loss_target[0]
    vp = _vector_params({n: w[n] for n in VECTORS})

    (st_in, st_late), token = _spread_start(
        [[w["w_in"][0].astype(WIRE), w["gdn_conv_w"][0]], [w[n][0].astype(WIRE) for n in late]], "gather", "gather_start")
    w_in_blocks, conv_blocks = _spread_wait(st_in, "gather", token, "gather_wait_in")
    w_main, w_small = _split_w_in(_join(w_in_blocks, 1))
    conv_w = _join(conv_blocks, 1)
    f = _mixer_forward(xs, vp, w_main, w_small, conv_w)
    full = {n: _join(b, axis_of[n]) for n, b in zip(late, _spread_wait(st_late, "gather", f["mixf"], "gather_wait_late"))}

    (h2, act, dgate, dup, dx3, dx2, dmixg, dmixf, dn2w, dfw, loss) = _tail(
        xs, f["mixg"], f["mixf"], tgt, full["w_out"], vp["n2w"], full["w_ffn_gate"], full["w_ffn_up"],
        full["w_ffn_down"], vp["fw"])
    dlate = {"w_out": jnp.concatenate([_wgrad(f["mixg"], dx2, "dw_out_gdn"), _wgrad(f["mixf"], dx2, "dw_out_fox")], axis=0),
             "w_ffn_gate": _wgrad(h2, dgate, "dw_gate"), "w_ffn_up": _wgrad(h2, dup, "dw_up"),
             "w_ffn_down": _wgrad(act, dx3, "dw_down")}

    (st_grads,), token = _spread_start([[_cut(dlate[n], axis_of[n]) for n in late]], "scatter", "grads_start")
    grad_x, dw_in, dconv, small = _mixer_backward(xs, vp, w_main, w_small, conv_w, f, dx2, dmixg, dmixf,
                                                  vp["onw"] + token[0:1, 0:DH])
    vec = _pack_vectors(small["dn1w"], dn2w, dfw, small["gsum"], small["donw"], small["dfnw"], loss)
    parts_in, parts_conv, parts_vec = _exchange([_cut(dw_in, 1), _cut(dconv, 1)], [vec], "exchange_last")
    parts = dict(zip(late, _spread_wait(st_grads, "scatter", parts_vec, "grads_wait")))
    parts["w_in"], parts["gdn_conv_w"] = parts_in, parts_conv

    results = [{}, {}, {}, {}]
    for n, _ in MATRICES:
        for d, a in zip(results, _adam_matrix(parts[n], w[n], m[n], v[n], "adam_" + n)):
            d[n] = a
    row = lambda a: a.reshape(1, -1)
    *vec_out, total_loss = _adam_vectors(parts_vec, [row(w[n]) for n in VECTORS], [row(m[n]) for n in VECTORS],
                                         [row(v[n]) for n in VECTORS])
    for d, arrs in zip(results, vec_out):
        for n, a in zip(VECTORS, arrs):
            d[n] = a.reshape(w[n].shape)
    return (total_loss[0, 0], grad_x[None], *[d[n] for d in results for n in WEIGHTS])
```

```python
import functools

import jax
import jax.numpy as jnp
from jax import lax
from jax.experimental import pallas as pl
from jax.experimental.pallas import tpu as pltpu

F32 = jnp.float32
MXU = jnp.bfloat16
WIRE = jnp.bfloat16
HI = lax.Precision.HIGHEST
EPS = 1e-6

N_DEV = 8
HEADS = 8
DH = 64
WIDTH = HEADS * DH
CHUNK = 64
LANE = 128
ROW_ALIGN = 16
TB = 256
QB = 256
VMEM_LIMIT = 60 * 1024 * 1024

ADAM_LR = 0.001
ADAM_B1 = 0.9
ADAM_B2 = 0.999
ADAM_EPS = 1e-08
ADAM_WD = 0.01
ADAM_STEP = 10

MESH = pl.DeviceIdType.MESH


def _params(sem=None):
    return pltpu.CompilerParams(dimension_semantics=sem, vmem_limit_bytes=VMEM_LIMIT)


def _resident(shape):
    n = len(shape)
    return pl.BlockSpec(shape, lambda *_: (0,) * n, pipeline_mode=pl.Buffered(1))


def _dot(a, b):
    return jnp.dot(a.astype(MXU), b.astype(MXU), preferred_element_type=F32)


def _dot_nt(a, b):
    return lax.dot_general(a.astype(MXU), b.astype(MXU), (((1,), (1,)), ((), ())), preferred_element_type=F32)


def _dot_tn(a, b):
    return lax.dot_general(a.astype(MXU), b.astype(MXU), (((0,), (0,)), ((), ())), preferred_element_type=F32)


def _hdot(a, b):
    return jnp.dot(a, b, precision=HI, preferred_element_type=F32)


def _hdot_nt(a, b):
    return lax.dot_general(a, b, (((1,), (1,)), ((), ())), precision=HI, preferred_element_type=F32)


def _hdot_tn(a, b):
    return lax.dot_general(a, b, (((0,), (0,)), ((), ())), precision=HI, preferred_element_type=F32)


def _sigmoid(x):
    return 1.0 / (1.0 + jnp.exp(-x))


def _softplus(x):
    return jnp.maximum(x, 0.0) + jnp.log(1.0 + jnp.exp(-jnp.abs(x)))


def _head_sum_matrix():
    ri = lax.broadcasted_iota(jnp.int32, (LANE, LANE), 0) // DH
    ci = lax.broadcasted_iota(jnp.int32, (LANE, LANE), 1) // DH
    return (ri == ci).astype(F32)


def _shift_down(x, s, rows):
    return jnp.where(rows >= s, pltpu.roll(x, s, 0), 0.0)


def _shift_up(x, s, rows):
    t = x.shape[0]
    return jnp.where(rows < t - s, pltpu.roll(x, t - s, 0), 0.0)


def _inproj(x, n1w, w_main, w_small):
    t, d = x.shape
    nm = w_main.shape[1]
    tb = min(TB, t)

    def body(x_ref, nw_ref, wm_ref, ws_ref, h_ref, pm_ref, ps_ref):
        xv = x_ref[...]
        r = lax.rsqrt(jnp.mean(xv * xv, axis=-1, keepdims=True) + EPS)
        h = (xv * r * nw_ref[...]).astype(MXU)
        h_ref[...] = h
        pm_ref[...] = jnp.dot(h, wm_ref[...], preferred_element_type=F32)
        ps_ref[...] = jnp.dot(h, ws_ref[...], preferred_element_type=F32)

    return pl.pallas_call(
        body, name="inproj", grid=(t // tb,),
        in_specs=[pl.BlockSpec((tb, d), lambda i: (i, 0)), _resident((1, d)), _resident((d, nm)), _resident((d, LANE))],
        out_specs=[pl.BlockSpec((tb, d), lambda i: (i, 0)), pl.BlockSpec((tb, nm), lambda i: (i, 0)),
                   pl.BlockSpec((tb, LANE), lambda i: (i, 0))],
        out_shape=[jax.ShapeDtypeStruct((t, d), MXU), jax.ShapeDtypeStruct((t, nm), F32),
                   jax.ShapeDtypeStruct((t, LANE), F32)],
        compiler_params=_params(("arbitrary",)),
    )(x, n1w, w_main, w_small)


def _inproj_bwd(x, n1w, dx2, dgdn, dz, dfox, dfg, dps, w_main, w_small):
    t, d = x.shape
    tb = min(TB, t)
    w3 = 3 * WIDTH

    def body(x_ref, nw_ref, dx2_ref, dgdn_ref, dz_ref, dfox_ref, dfg_ref, dps_ref, wm_ref, ws_ref, gx_ref, dnw_ref):
        dh = _dot_nt(dgdn_ref[...], wm_ref[:, 0:w3])
        dh += _dot_nt(dz_ref[...], wm_ref[:, w3:w3 + WIDTH])
        dh += _dot_nt(dfox_ref[...], wm_ref[:, w3 + WIDTH:2 * w3 + WIDTH])
        dh += _dot_nt(dfg_ref[...], wm_ref[:, 2 * w3 + WIDTH:2 * w3 + 2 * WIDTH])
        dh += _dot_nt(dps_ref[...], ws_ref[...])
        xv = x_ref[...]
        r = lax.rsqrt(jnp.mean(xv * xv, axis=-1, keepdims=True) + EPS)
        xn = xv * r

        @pl.when(pl.program_id(0) == 0)
        def _():
            dnw_ref[...] = jnp.zeros_like(dnw_ref)

        dnw_ref[...] += jnp.sum(dh * xn, axis=0, keepdims=True)
        g = dh * nw_ref[...]
        gx_ref[...] = dx2_ref[...] + r * (g - xn * jnp.mean(g * xn, axis=-1, keepdims=True))

    def tok(n):
        return pl.BlockSpec((tb, n), lambda i: (i, 0))

    return pl.pallas_call(
        body, name="inproj_bwd", grid=(t // tb,),
        in_specs=[tok(d), _resident((1, d)), tok(d), tok(w3), tok(WIDTH), tok(w3), tok(WIDTH), tok(LANE),
                  _resident(w_main.shape), _resident(w_small.shape)],
        out_specs=[tok(d), pl.BlockSpec((1, d), lambda i: (0, 0))],
        out_shape=[jax.ShapeDtypeStruct((t, d), F32), jax.ShapeDtypeStruct((1, d), F32)],
        compiler_params=_params(("arbitrary",)),
    )(x, n1w, dx2, dgdn, dz, dfox, dfg, dps, w_main, w_small)


def _gate_lanes(shape):
    lane = lax.broadcasted_iota(jnp.int32, shape, 1)
    return lane < HEADS, (lane >= HEADS) & (lane < 2 * HEADS), (lane >= 2 * HEADS) & (lane < 3 * HEADS)


def _block_masks():
    ri = lax.broadcasted_iota(jnp.int32, (LANE, LANE), 0)
    ci = lax.broadcasted_iota(jnp.int32, (LANE, LANE), 1)
    same = (ri // CHUNK) == (ci // CHUNK)
    return ((ri >= ci).astype(F32), (ri <= ci).astype(F32), (same & (ri >= ci)).astype(F32),
            (same & (ri <= ci)).astype(F32), same.astype(F32))


def _gates(ps, gparams):
    t = ps.shape[0]
    nb = t // LANE

    def body(ps_ref, gp_ref, out_ref, run_ref, tot_ref, runt_ref):
        p = ps_ref[...]
        is_b, is_a, is_f = _gate_lanes(p.shape)
        z = p + gp_ref[0:1, :]
        neg_exp_a = -jnp.exp(gp_ref[1:2, :])
        glog = neg_exp_a * _softplus(z)
        logf = -_softplus(-z)
        out_ref[...] = jnp.where(is_b, _sigmoid(p), jnp.where(is_a, glog, jnp.where(is_f, logf, 0.0)))
        tril, _, tril_c, _, same_c = _block_masks()
        off = jnp.zeros((1, LANE), F32)
        for b in range(nb):
            rows = slice(b * LANE, (b + 1) * LANE)
            blk = out_ref[rows, :]
            ga = jnp.where(is_a[:LANE], blk, 0.0)
            fb = _hdot(tril, jnp.where(is_f[:LANE], blk, 0.0)) + off
            run = fb + _hdot(tril_c, ga)
            run_ref[rows, :] = run
            runt_ref[:, rows] = run.T
            tot_ref[rows, :] = _hdot(same_c, ga)
            off = fb[LANE - 1:LANE, :]

    return pl.pallas_call(
        body, name="gates",
        out_shape=[jax.ShapeDtypeStruct((t, LANE), F32)] * 3 + [jax.ShapeDtypeStruct((LANE, t), F32)],
        compiler_params=_params(),
    )(ps, gparams)


def _gates_bwd(ps, gparams, dcol_g, dtot_g, dcol_f, drow_g, drow_f):
    t = ps.shape[0]
    nb = t // LANE

    def body(ps_ref, gp_ref, dcg_ref, dtg_ref, dcf_ref, drg_ref, drf_ref, dps_ref, sums_ref, dl_ref, d0_ref, tr_ref):
        p = ps_ref[...]
        is_b, is_a, is_f = _gate_lanes(p.shape)
        _, triu, _, triu_c, same_c = _block_masks()
        tr_ref[...] = jnp.zeros_like(tr_ref)
        off = jnp.zeros((1, LANE), F32)
        for b in reversed(range(nb)):
            rows = slice(b * LANE, (b + 1) * LANE)
            tr_ref[HEADS:2 * HEADS, :] = drg_ref[:, rows]
            tr_ref[2 * HEADS:3 * HEADS, :] = drf_ref[:, rows]
            d = dcg_ref[rows, :] + dcf_ref[rows, :] + tr_ref[...].T
            d0_ref[rows, :] = d
            dlf = _hdot(triu, jnp.where(is_f[:LANE], d, 0.0)) + off
            dla = (_hdot(triu_c, jnp.where(is_a[:LANE], d, 0.0))
                   + _hdot(same_c, jnp.where(is_a[:LANE], dtg_ref[rows, :], 0.0)))
            dl_ref[rows, :] = dlf + dla
            off = dlf[0:1, :]
        z = p + gp_ref[0:1, :]
        neg_exp_a = -jnp.exp(gp_ref[1:2, :])
        sb = _sigmoid(p)
        glog = neg_exp_a * _softplus(z)
        dl = dl_ref[...]
        dp = jnp.where(is_b, d0_ref[...] * sb * (1.0 - sb),
                       jnp.where(is_a, dl * neg_exp_a * _sigmoid(z), jnp.where(is_f, dl * _sigmoid(-z), 0.0)))
        dps_ref[...] = dp
        s_a = jnp.sum(jnp.where(is_a, dl * glog, 0.0), axis=0, keepdims=True)
        s_p = jnp.sum(jnp.where(is_b, 0.0, dp), axis=0, keepdims=True)
        row = lax.broadcasted_iota(jnp.int32, (8, LANE), 0)
        from_a = pltpu.roll(jnp.where(row == 0, s_a, jnp.where(row == 1, s_p, 0.0)), LANE - HEADS, 1)
        from_f = pltpu.roll(jnp.where(row == 2, s_p, 0.0), LANE - 2 * HEADS, 1)
        lane = lax.broadcasted_iota(jnp.int32, (8, LANE), 1)
        sums_ref[...] = jnp.where(lane < HEADS, from_a + from_f, 0.0)

    return pl.pallas_call(
        body, name="gates_bwd",
        out_shape=[jax.ShapeDtypeStruct((t, LANE), F32), jax.ShapeDtypeStruct((8, LANE), F32)],
        scratch_shapes=[pltpu.VMEM((t, LANE), F32), pltpu.VMEM((t, LANE), F32), pltpu.VMEM((LANE, LANE), F32)],
        compiler_params=_params(),
    )(ps, gparams, dcol_g, dtot_g, dcol_f, drow_g, drow_f)


def _conv(xv, w, rows):
    acc = w[3:4, :] * xv
    for s in range(1, 4):
        acc += w[3 - s:4 - s, :] * _shift_down(xv, s, rows)
    return acc


def _gdn_prep(pm, conv_w):
    t = pm.shape[0]
    nj = WIDTH // LANE

    def body(x_ref, w_ref, o_ref):
        kind = pl.program_id(0)
        xv = x_ref[...]
        rows = lax.broadcasted_iota(jnp.int32, xv.shape, 0)
        acc = _conv(xv, w_ref[...], rows)
        y = acc * _sigmoid(acc)
        ss = _hdot(y * y, _head_sum_matrix())
        out = jnp.where(kind < 2, y * lax.rsqrt(ss + EPS), y)
        o_ref[0, 0] = out[:, :DH]
        o_ref[0, 1] = out[:, DH:]

    return pl.pallas_call(
        body, name="gdn_prep", grid=(3, nj),
        in_specs=[pl.BlockSpec((t, LANE), lambda i, j: (0, i * nj + j)),
                  pl.BlockSpec((4, LANE), lambda i, j: (0, i * nj + j))],
        out_specs=pl.BlockSpec((1, 2, t, DH), lambda i, j: (i, j, 0, 0)),
        out_shape=jax.ShapeDtypeStruct((3, HEADS, t, DH), F32),
        compiler_params=_params(("arbitrary", "arbitrary")),
    )(pm, conv_w)


def _gdn_prep_bwd(pm, conv_w, dqkv):
    t = pm.shape[0]
    nj = WIDTH // LANE

    def body(x_ref, w_ref, d_ref, dx_ref, dw_ref):
        kind = pl.program_id(0)
        xv = x_ref[...]
        w = w_ref[...]
        rows = lax.broadcasted_iota(jnp.int32, xv.shape, 0)
        acc = _conv(xv, w, rows)
        sg = _sigmoid(acc)
        y = acc * sg
        hs = _head_sum_matrix()
        r = lax.rsqrt(_hdot(y * y, hs) + EPS)
        yn = y * r
        d = jnp.concatenate([d_ref[0, 0], d_ref[0, 1]], axis=1)
        dy = jnp.where(kind < 2, r * (d - yn * _hdot(d * yn, hs)), d)
        dacc = dy * sg * (1.0 + acc * (1.0 - sg))
        dx = w[3:4, :] * dacc
        dw_ref[3:4, :] = jnp.sum(dacc * xv, axis=0, keepdims=True)
        for s in range(1, 4):
            dx += w[3 - s:4 - s, :] * _shift_up(dacc, s, rows)
            dw_ref[3 - s:4 - s, :] = jnp.sum(dacc * _shift_down(xv, s, rows), axis=0, keepdims=True)
        dx_ref[...] = dx

    return pl.pallas_call(
        body, name="gdn_prep_bwd", grid=(3, nj),
        in_specs=[pl.BlockSpec((t, LANE), lambda i, j: (0, i * nj + j)),
                  pl.BlockSpec((4, LANE), lambda i, j: (0, i * nj + j)),
                  pl.BlockSpec((1, 2, t, DH), lambda i, j: (i, j, 0, 0))],
        out_specs=[pl.BlockSpec((t, LANE), lambda i, j: (0, i * nj + j)),
                   pl.BlockSpec((4, LANE), lambda i, j: (0, i * nj + j))],
        out_shape=[jax.ShapeDtypeStruct((t, 3 * WIDTH), F32), jax.ShapeDtypeStruct((4, 3 * WIDTH), F32)],
        compiler_params=_params(("arbitrary", "arbitrary")),
    )(pm, conv_w, dqkv)


FOX_COL0 = 4 * WIDTH // LANE


def _fox_prep(pm, nw):
    t = pm.shape[0]
    nj = WIDTH // LANE

    def body(x_ref, w_ref, o_ref):
        kind = pl.program_id(0)
        xv = x_ref[...]
        ms = _hdot(xv * xv, _head_sum_matrix()) * (1.0 / DH)
        yn = xv * lax.rsqrt(ms + EPS) * w_ref[pl.ds(kind, 1), :]
        out = jnp.where(kind < 2, yn, xv)
        o_ref[0, 0] = out[:, :DH]
        o_ref[0, 1] = out[:, DH:]

    return pl.pallas_call(
        body, name="fox_prep", grid=(3, nj),
        in_specs=[pl.BlockSpec((t, LANE), lambda i, j: (0, FOX_COL0 + i * nj + j)),
                  pl.BlockSpec((3, LANE), lambda i, j: (0, 0))],
        out_specs=pl.BlockSpec((1, 2, t, DH), lambda i, j: (i, j, 0, 0)),
        out_shape=jax.ShapeDtypeStruct((3, HEADS, t, DH), F32),
        compiler_params=_params(("arbitrary", "arbitrary")),
    )(pm, nw)


def _fox_prep_bwd(pm, nw, dqkv):
    t = pm.shape[0]
    nj = WIDTH // LANE

    def body(x_ref, w_ref, d_ref, dx_ref, dw_ref):
        kind = pl.program_id(0)
        xv = x_ref[...]
        hs = _head_sum_matrix()
        r = lax.rsqrt(_hdot(xv * xv, hs) * (1.0 / DH) + EPS)
        xn = xv * r
        d = jnp.concatenate([d_ref[0, 0], d_ref[0, 1]], axis=1)
        g = d * w_ref[pl.ds(kind, 1), :]
        dxn = r * (g - xn * _hdot(g * xn, hs) * (1.0 / DH))
        dx_ref[...] = jnp.where(kind < 2, dxn, d)
        dw_ref[0, 0] = jnp.sum(d * xn, axis=0, keepdims=True)

    return pl.pallas_call(
        body, name="fox_prep_bwd", grid=(3, nj),
        in_specs=[pl.BlockSpec((t, LANE), lambda i, j: (0, FOX_COL0 + i * nj + j)),
                  pl.BlockSpec((3, LANE), lambda i, j: (0, 0)),
                  pl.BlockSpec((1, 2, t, DH), lambda i, j: (i, j, 0, 0))],
        out_specs=[pl.BlockSpec((t, LANE), lambda i, j: (0, i * nj + j)),
                   pl.BlockSpec((1, 1, 1, LANE), lambda i, j: (i, j, 0, 0))],
        out_shape=[jax.ShapeDtypeStruct((t, 3 * WIDTH), F32), jax.ShapeDtypeStruct((3, nj, 1, LANE), F32)],
        compiler_params=_params(("arbitrary", "arbitrary")),
    )(pm, nw, dqkv)


SC = 256
CPS = SC // CHUNK
Q_SCALE = DH ** -0.5


def _sc_masks():
    ri = lax.broadcasted_iota(jnp.int32, (SC, SC), 0)
    ci = lax.broadcasted_iota(jnp.int32, (SC, SC), 1)
    same = (ri // CHUNK) == (ci // CHUNK)
    return same & (ri >= ci), same & (ri > ci), ri == ci


def _unit_lower_inverse(m, eye):
    inv = jnp.where(eye, 1.0, 0.0) + m
    for _ in range(5):
        m = _dot(m, m)
        inv = inv + _dot(inv, m)
    return inv


def _lane_col(blk, lane_idx):
    lane = lax.broadcasted_iota(jnp.int32, blk.shape, 1)
    return jnp.sum(jnp.where(lane == lane_idx, blk, 0.0), axis=1, keepdims=True)


def _to_lane(col, lane_idx):
    lane = lax.broadcasted_iota(jnp.int32, (col.shape[0], LANE), 1)
    return jnp.where(lane == lane_idx, col, 0.0)


def _gdn_columns(gates_ref, run_ref, tot_ref, runt_ref, rows, h):
    return (_lane_col(gates_ref[rows, :], h), _lane_col(run_ref[rows, :], HEADS + h),
            _lane_col(tot_ref[rows, :], HEADS + h), runt_ref[pl.ds(h, 1), rows])


def _gdn_local(q, k, beta, gc, gl, grow, causal):
    decay = jnp.exp(jnp.where(causal, gc - grow, -1e30))
    egc = jnp.exp(gc)
    ekd = jnp.exp(gl - gc)
    qs = q * Q_SCALE
    kb = k * beta
    kk = _dot_nt(kb, k)
    qk = _dot_nt(qs, k)
    return beta, gl, decay, egc, ekd, qs, kb, kk, qk, jnp.where(causal, qk * decay, 0.0)


def _chunk_rows(c):
    return pl.ds(pl.multiple_of(c * CHUNK, CHUNK), CHUNK)


def _sc_rows(b):
    return pl.ds(pl.multiple_of(b * SC, SC), SC)


def _gdn_fwd(qkv, gates, run, tot, run_t):
    t = qkv.shape[2]
    nc = t // CHUNK
    nsc = t // SC

    def body(qkv_ref, gates_ref, run_ref, tot_ref, runt_ref, o_ref, st_ref, inv_ref, kc_s, qc_s, g_s, au_s):
        h = pl.program_id(0)
        causal, strict, eye = _sc_masks()

        def local(b, _):
            rows = _sc_rows(b)
            q, k, v = qkv_ref[0, 0, rows, :], qkv_ref[1, 0, rows, :], qkv_ref[2, 0, rows, :]
            beta, _, decay, egc, ekd, qs, kb, kk, _, attn = _gdn_local(
                q, k, *_gdn_columns(gates_ref, run_ref, tot_ref, runt_ref, rows, h), causal)
            inv = _unit_lower_inverse(-jnp.where(strict, kk * decay, 0.0), eye)
            inv_ref[0, rows, :] = inv.astype(inv_ref.dtype)
            u = _dot(inv, v * beta)
            w = _dot(inv, kb * egc)
            g_s[rows, :] = qs * egc - _dot(attn, w)
            au_s[rows, :] = _dot(attn, u)
            kd = k * ekd
            for j in range(CPS):
                sl = slice(j * CHUNK, (j + 1) * CHUNK)
                kc_s[b * CPS + j] = _dot_tn(kd[sl], w[sl])
                qc_s[b * CPS + j] = _dot_tn(kd[sl], u[sl])
            return 0

        lax.fori_loop(0, nsc, local, 0)

        def step(c, s):
            rows = _chunk_rows(c)
            st_ref[0, c] = s
            o_ref[0, rows, :] = _dot(g_s[rows, :], s) + au_s[rows, :]
            egl = jnp.exp(_lane_col(tot_ref[pl.ds(c * CHUNK, 1), :], HEADS + h))
            return egl * s - _dot(kc_s[c], s) + qc_s[c]

        lax.fori_loop(0, nc, step, jnp.zeros((DH, DH), F32))

    whole = pl.BlockSpec((t, LANE), lambda h: (0, 0))
    return pl.pallas_call(
        body, name="gdn_fwd", grid=(HEADS,),
        in_specs=[pl.BlockSpec((3, 1, t, DH), lambda h: (0, h, 0, 0)), whole, whole, whole,
                  pl.BlockSpec((HEADS, t), lambda h: (1, 0))],
        out_specs=[pl.BlockSpec((1, t, DH), lambda h: (h, 0, 0)), pl.BlockSpec((1, nc, DH, DH), lambda h: (h, 0, 0, 0)),
                   pl.BlockSpec((1, t, SC), lambda h: (h, 0, 0))],
        out_shape=[jax.ShapeDtypeStruct((HEADS, t, DH), F32), jax.ShapeDtypeStruct((HEADS, nc, DH, DH), F32),
                   jax.ShapeDtypeStruct((HEADS, t, SC), MXU)],
        scratch_shapes=[pltpu.VMEM((nc, DH, DH), F32), pltpu.VMEM((nc, DH, DH), F32),
                        pltpu.VMEM((t, DH), F32), pltpu.VMEM((t, DH), F32)],
        compiler_params=_params(("arbitrary",)),
    )(qkv, gates, run, tot, run_t)


def _gdn_bwd(qkv, gates, run, tot, run_t, inv, states, do):
    t = qkv.shape[2]
    nc = t // CHUNK
    nsc = t // SC

    def body(qkv_ref, gates_ref, run_ref, tot_ref, runt_ref, inv_ref, st_ref, do_ref,
             dqkv_ref, dcol_ref, dtot_ref, drow_ref, u_s, w_s, kc_s, h_s, dsn_s):
        h = pl.program_id(0)
        causal, strict, _ = _sc_masks()

        @pl.when(h == 0)
        def _():
            dcol_ref[...] = jnp.zeros_like(dcol_ref)
            dtot_ref[...] = jnp.zeros_like(dtot_ref)

        def local(b, _):
            rows = _sc_rows(b)
            q, k, v = qkv_ref[0, 0, rows, :], qkv_ref[1, 0, rows, :], qkv_ref[2, 0, rows, :]
            beta, _, _, egc, ekd, qs, kb, _, _, attn = _gdn_local(
                q, k, *_gdn_columns(gates_ref, run_ref, tot_ref, runt_ref, rows, h), causal)
            inv_b = inv_ref[0, rows, :]
            u = _dot(inv_b, v * beta)
            w = _dot(inv_b, kb * egc)
            u_s[rows, :] = u
            w_s[rows, :] = w
            g = qs * egc - _dot(attn, w)
            kd = k * ekd
            dout = do_ref[0, rows, :]
            for j in range(CPS):
                sl = slice(j * CHUNK, (j + 1) * CHUNK)
                kc_s[b * CPS + j] = _dot_tn(kd[sl], w[sl])
                h_s[b * CPS + j] = _dot_tn(g[sl], dout[sl])
            return 0

        lax.fori_loop(0, nsc, local, 0)

        def step(i, ds):
            c = nc - 1 - i
            dsn_s[c] = ds
            egl = jnp.exp(_lane_col(tot_ref[pl.ds(c * CHUNK, 1), :], HEADS + h))
            return egl * ds - _dot_tn(kc_s[c], ds) + h_s[c]

        lax.fori_loop(0, nc, step, jnp.zeros((DH, DH), F32))

        def back(b, _):
            rows = _sc_rows(b)
            q, k, v = qkv_ref[0, 0, rows, :], qkv_ref[1, 0, rows, :], qkv_ref[2, 0, rows, :]
            beta, gl, decay, egc, ekd, qs, kb, kk, qk, attn = _gdn_local(
                q, k, *_gdn_columns(gates_ref, run_ref, tot_ref, runt_ref, rows, h), causal)
            u, w = u_s[rows, :], w_s[rows, :]
            kd = k * ekd
            vb = v * beta
            kbe = kb * egc
            dout = do_ref[0, rows, :]
            first = lax.broadcasted_iota(jnp.int32, (CHUNK, 1), 0) == 0
            dg_p, dkd_p, dw_p, du_p, dgl_p = [], [], [], [], []
            for j in range(CPS):
                sl = slice(j * CHUNK, (j + 1) * CHUNK)
                s = st_ref[0, b * CPS + j]
                dsn = dsn_s[b * CPS + j]
                dkc = -_dot_nt(dsn, s)
                dg_p.append(_dot_nt(dout[sl], s))
                dkd_p.append(_dot_nt(w[sl], dkc) + _dot_nt(u[sl], dsn))
                dw_p.append(_dot(kd[sl], dkc))
                du_p.append(_dot(kd[sl], dsn))
                degl = jnp.sum(jnp.sum(s * dsn, axis=1, keepdims=True), axis=0, keepdims=True)
                dgl_p.append(jnp.where(first, degl * jnp.exp(gl[j * CHUNK:j * CHUNK + 1, :]), 0.0))
            dg, dkd = jnp.concatenate(dg_p, axis=0), jnp.concatenate(dkd_p, axis=0)
            da = jnp.where(causal, _dot_nt(dout, u) - _dot_nt(dg, w), 0.0)
            at = _dot_tn(attn, jnp.concatenate([dout, dg], axis=1))
            du = at[:, :DH] + jnp.concatenate(du_p, axis=0)
            dw = jnp.concatenate(dw_p, axis=0) - at[:, DH:]
            inv_t = inv_ref[0, rows, :].astype(F32).T
            it = _dot(inv_t, jnp.concatenate([du, dw], axis=1))
            dvb, dkbe = it[:, :DH], it[:, DH:]
            dinv = _dot_nt(du, vb) + _dot_nt(dw, kbe)
            dl = jnp.where(strict, -_dot(_dot(inv_t, dinv), inv_t), 0.0)
            dlogd = (dl * kk + da * qk) * decay
            dkk = dl * decay
            dqk = da * decay
            dkb = _dot(dkk, k) + dkbe * egc
            dqs = _dot(dqk, k) + dg * egc
            dk = (_dot_tn(jnp.concatenate([dkk, dqk], axis=0), jnp.concatenate([kb, qs], axis=0))
                  + dkd * ekd + dkb * beta)
            dkd_kd = jnp.sum(dkd * kd, axis=1, keepdims=True)
            dgc = (jnp.sum(dlogd, axis=1, keepdims=True) + jnp.sum(dg * qs, axis=1, keepdims=True) * egc
                   + jnp.sum(dkbe * kbe, axis=1, keepdims=True) - dkd_kd)
            dgl = dkd_kd + jnp.concatenate(dgl_p, axis=0)
            dbeta = jnp.sum(dkb * k, axis=1, keepdims=True) + jnp.sum(dvb * v, axis=1, keepdims=True)
            dqkv_ref[0, 0, rows, :] = dqs * Q_SCALE
            dqkv_ref[1, 0, rows, :] = dk
            dqkv_ref[2, 0, rows, :] = dvb * beta
            dcol_ref[rows, :] += _to_lane(dbeta, h) + _to_lane(dgc, HEADS + h)
            dtot_ref[rows, :] += _to_lane(dgl, HEADS + h)
            drow_ref[pl.ds(h, 1), rows] = -jnp.sum(dlogd, axis=0, keepdims=True)
            return 0

        lax.fori_loop(0, nsc, back, 0)

    whole = pl.BlockSpec((t, LANE), lambda h: (0, 0))
    rowspec = pl.BlockSpec((HEADS, t), lambda h: (0, 0))
    sq = pltpu.VMEM((nc, DH, DH), F32)
    return pl.pallas_call(
        body, name="gdn_bwd", grid=(HEADS,),
        in_specs=[pl.BlockSpec((3, 1, t, DH), lambda h: (0, h, 0, 0)), whole, whole, whole,
                  pl.BlockSpec((HEADS, t), lambda h: (1, 0)),
                  pl.BlockSpec((1, t, SC), lambda h: (h, 0, 0)), pl.BlockSpec((1, nc, DH, DH), lambda h: (h, 0, 0, 0)),
                  pl.BlockSpec((1, t, DH), lambda h: (h, 0, 0))],
        out_specs=[pl.BlockSpec((3, 1, t, DH), lambda h: (0, h, 0, 0)), whole, whole, rowspec],
        out_shape=[jax.ShapeDtypeStruct((3, HEADS, t, DH), F32), jax.ShapeDtypeStruct((t, LANE), F32),
                   jax.ShapeDtypeStruct((t, LANE), F32), jax.ShapeDtypeStruct((HEADS, t), F32)],
        scratch_shapes=[pltpu.VMEM((t, DH), F32), pltpu.VMEM((t, DH), F32), sq, sq, sq],
        compiler_params=_params(("arbitrary",)),
    )(qkv, gates, run, tot, run_t, inv, states, do)


def _causal_mask(i, j, qb):
    qpos = i * qb + lax.broadcasted_iota(jnp.int32, (qb, qb), 0)
    kpos = j * qb + lax.broadcasted_iota(jnp.int32, (qb, qb), 1)
    return qpos >= kpos


def _fox_fwd(qkv, run, run_t):
    t = qkv.shape[2]
    qb = min(QB, t)
    nq = t // qb
    scale = DH ** -0.5

    def body(q_ref, k_ref, v_ref, run_ref, runt_ref, o_ref, lse_ref):
        h = pl.program_id(0)
        i = pl.program_id(1)
        q = q_ref[0, 0].astype(MXU)
        fq = _lane_col(run_ref[...], 2 * HEADS + h)

        def kstep(j, carry):
            m, l, acc = carry
            rows = pl.ds(pl.multiple_of(j * qb, qb), qb)
            s = _dot_nt(q, k_ref[0, 0, rows, :]) * scale + (fq - runt_ref[pl.ds(h, 1), rows])
            s = jnp.where(_causal_mask(i, j, qb), s, -1e30)
            m_new = jnp.maximum(m, jnp.max(s, axis=1, keepdims=True))
            p = jnp.exp(s - m_new)
            alpha = jnp.exp(m - m_new)
            l = alpha * l + jnp.sum(p, axis=1, keepdims=True)
            acc = alpha * acc + _dot(p, v_ref[0, 0, rows, :])
            return m_new, l, acc

        init = (jnp.full((qb, 1), -1e30, F32), jnp.zeros((qb, 1), F32), jnp.zeros((qb, DH), F32))
        m, l, acc = lax.fori_loop(0, i + 1, kstep, init)
        o_ref[0] = acc / l
        lse_ref[0] = m + jnp.log(l)

    return pl.pallas_call(
        body, name="fox_fwd", grid=(HEADS, nq),
        in_specs=[pl.BlockSpec((1, 1, qb, DH), lambda h, i: (0, h, i, 0)),
                  pl.BlockSpec((1, 1, t, DH), lambda h, i: (1, h, 0, 0)),
                  pl.BlockSpec((1, 1, t, DH), lambda h, i: (2, h, 0, 0)),
                  pl.BlockSpec((qb, LANE), lambda h, i: (i, 0)),
                  pl.BlockSpec((HEADS, t), lambda h, i: (2, 0))],
        out_specs=[pl.BlockSpec((1, qb, DH), lambda h, i: (h, i, 0)), pl.BlockSpec((1, qb, 1), lambda h, i: (h, i, 0))],
        out_shape=[jax.ShapeDtypeStruct((HEADS, t, DH), F32), jax.ShapeDtypeStruct((HEADS, t, 1), F32)],
        compiler_params=_params(("arbitrary", "arbitrary")),
    )(qkv, qkv, qkv, run, run_t)


def _fox_bwd(qkv, run, run_t, o, lse, do):
    t = qkv.shape[2]
    qb = min(QB, t)
    nq = t // qb
    scale = DH ** -0.5

    def body(q_ref, k_ref, v_ref, run_ref, runt_ref, o_ref, lse_ref, do_ref, dqkv_ref, dcol_ref, drow_ref):
        h = pl.program_id(0)
        j = pl.program_id(1)

        @pl.when(j == 0)
        def _():
            dqkv_ref[0] = jnp.zeros((1, t, DH), F32)

        @pl.when((j == 0) & (h == 0))
        def _():
            dcol_ref[...] = jnp.zeros_like(dcol_ref)

        k = k_ref[0, 0].astype(MXU)
        v = v_ref[0, 0].astype(MXU)
        krows = pl.ds(pl.multiple_of(j * qb, qb), qb)
        fk = runt_ref[pl.ds(h, 1), krows]

        def qstep(i, carry):
            dk, dv, dfk = carry
            rows = pl.ds(pl.multiple_of(i * qb, qb), qb)
            q = q_ref[0, 0, rows, :].astype(MXU)
            dout = do_ref[0, rows, :]
            s = _dot_nt(q, k) * scale + (_lane_col(run_ref[rows, :], 2 * HEADS + h) - fk)
            s = jnp.where(_causal_mask(i, j, qb), s, -1e30)
            p = jnp.exp(s - lse_ref[0, rows, :])
            dp = _dot_nt(dout, v)
            drow = jnp.sum(dout * o_ref[0, rows, :], axis=1, keepdims=True)
            ds = p * (dp - drow)
            dqkv_ref[0, 0, rows, :] += _dot(ds, k) * scale
            dcol_ref[rows, :] += _to_lane(jnp.sum(ds, axis=1, keepdims=True), 2 * HEADS + h)
            return (dk + _dot_tn(ds, q) * scale, dv + _dot_tn(p, dout), dfk - jnp.sum(ds, axis=0, keepdims=True))

        init = (jnp.zeros((qb, DH), F32), jnp.zeros((qb, DH), F32), jnp.zeros((1, qb), F32))
        dk, dv, dfk = lax.fori_loop(j, nq, qstep, init)
        dqkv_ref[1, 0, krows, :] = dk
        dqkv_ref[2, 0, krows, :] = dv
        drow_ref[pl.ds(h, 1), krows] = dfk

    full = pl.BlockSpec((1, t, DH), lambda h, j: (h, 0, 0))
    return pl.pallas_call(
        body, name="fox_bwd", grid=(HEADS, nq),
        in_specs=[pl.BlockSpec((1, 1, t, DH), lambda h, j: (0, h, 0, 0)),
                  pl.BlockSpec((1, 1, qb, DH), lambda h, j: (1, h, j, 0)),
                  pl.BlockSpec((1, 1, qb, DH), lambda h, j: (2, h, j, 0)),
                  pl.BlockSpec((t, LANE), lambda h, j: (0, 0)), pl.BlockSpec((HEADS, t), lambda h, j: (2, 0)),
                  full, pl.BlockSpec((1, t, 1), lambda h, j: (h, 0, 0)), full],
        out_specs=[pl.BlockSpec((3, 1, t, DH), lambda h, j: (0, h, 0, 0)),
                   pl.BlockSpec((t, LANE), lambda h, j: (0, 0)), pl.BlockSpec((HEADS, t), lambda h, j: (0, 0))],
        out_shape=[jax.ShapeDtypeStruct((3, HEADS, t, DH), F32), jax.ShapeDtypeStruct((t, LANE), F32),
                   jax.ShapeDtypeStruct((HEADS, t), F32)],
        compiler_params=_params(("arbitrary", "arbitrary")),
    )(qkv, qkv, qkv, run, run_t, o, lse, do)


Z_COL0 = 3 * WIDTH // LANE
FGATE_COL0 = 7 * WIDTH // LANE


def _gdn_post(o, pm, onw):
    t = pm.shape[0]

    def body(o_ref, z_ref, w_ref, m_ref):
        z = z_ref[...]
        sz = z * _sigmoid(z)
        for hh in range(2):
            ov = o_ref[hh]
            n = ov * lax.rsqrt(jnp.mean(ov * ov, axis=-1, keepdims=True) + EPS) * w_ref[...]
            m_ref[:, hh * DH:(hh + 1) * DH] = (n * sz[:, hh * DH:(hh + 1) * DH]).astype(m_ref.dtype)

    return pl.pallas_call(
        body, name="gdn_post", grid=(WIDTH // LANE,),
        in_specs=[pl.BlockSpec((2, t, DH), lambda j: (j, 0, 0)), pl.BlockSpec((t, LANE), lambda j: (0, Z_COL0 + j)),
                  pl.BlockSpec((1, DH), lambda j: (0, 0))],
        out_specs=pl.BlockSpec((t, LANE), lambda j: (0, j)),
        out_shape=jax.ShapeDtypeStruct((t, WIDTH), MXU),
        compiler_params=_params(("arbitrary",)),
    )(o, pm, onw)


def _gdn_post_bwd(o, pm, onw, dmix):
    t = pm.shape[0]

    def body(o_ref, z_ref, w_ref, dm_ref, do_ref, dz_ref, dw_ref):
        @pl.when(pl.program_id(0) == 0)
        def _():
            dw_ref[...] = jnp.zeros_like(dw_ref)

        z = z_ref[...]
        sg = _sigmoid(z)
        sz = z * sg
        dsz = sg * (1.0 + z * (1.0 - sg))
        dm = dm_ref[...]
        for hh in range(2):
            cols = slice(hh * DH, (hh + 1) * DH)
            ov = o_ref[hh]
            r = lax.rsqrt(jnp.mean(ov * ov, axis=-1, keepdims=True) + EPS)
            xn = ov * r
            dmh = dm[:, cols]
            dn = dmh * sz[:, cols]
            dz_ref[:, cols] = dmh * (xn * w_ref[...]) * dsz[:, cols]
            dw_ref[...] += jnp.sum(dn * xn, axis=0, keepdims=True)
            g = dn * w_ref[...]
            do_ref[hh] = r * (g - xn * jnp.mean(g * xn, axis=-1, keepdims=True))

    return pl.pallas_call(
        body, name="gdn_post_bwd", grid=(WIDTH // LANE,),
        in_specs=[pl.BlockSpec((2, t, DH), lambda j: (j, 0, 0)), pl.BlockSpec((t, LANE), lambda j: (0, Z_COL0 + j)),
                  pl.BlockSpec((1, DH), lambda j: (0, 0)), pl.BlockSpec((t, LANE), lambda j: (0, j))],
        out_specs=[pl.BlockSpec((2, t, DH), lambda j: (j, 0, 0)), pl.BlockSpec((t, LANE), lambda j: (0, j)),
                   pl.BlockSpec((1, DH), lambda j: (0, 0))],
        out_shape=[jax.ShapeDtypeStruct((HEADS, t, DH), F32), jax.ShapeDtypeStruct((t, WIDTH), F32),
                   jax.ShapeDtypeStruct((1, DH), F32)],
        compiler_params=_params(("arbitrary",)),
    )(o, pm, onw, dmix)


def _fox_post(o, pm):
    t = pm.shape[0]

    def body(o_ref, g_ref, m_ref):
        sg = _sigmoid(g_ref[...])
        for hh in range(2):
            cols = slice(hh * DH, (hh + 1) * DH)
            m_ref[:, cols] = (o_ref[hh] * sg[:, cols]).astype(m_ref.dtype)

    return pl.pallas_call(
        body, name="fox_post", grid=(WIDTH // LANE,),
        in_specs=[pl.BlockSpec((2, t, DH), lambda j: (j, 0, 0)), pl.BlockSpec((t, LANE), lambda j: (0, FGATE_COL0 + j))],
        out_specs=pl.BlockSpec((t, LANE), lambda j: (0, j)),
        out_shape=jax.ShapeDtypeStruct((t, WIDTH), MXU),
        compiler_params=_params(("arbitrary",)),
    )(o, pm)


def _fox_post_bwd(o, pm, dmix):
    t = pm.shape[0]

    def body(o_ref, g_ref, dm_ref, do_ref, dg_ref):
        sg = _sigmoid(g_ref[...])
        dm = dm_ref[...]
        for hh in range(2):
            cols = slice(hh * DH, (hh + 1) * DH)
            do_ref[hh] = dm[:, cols] * sg[:, cols]
            dg_ref[:, cols] = dm[:, cols] * o_ref[hh] * (sg * (1.0 - sg))[:, cols]

    return pl.pallas_call(
        body, name="fox_post_bwd", grid=(WIDTH // LANE,),
        in_specs=[pl.BlockSpec((2, t, DH), lambda j: (j, 0, 0)), pl.BlockSpec((t, LANE), lambda j: (0, FGATE_COL0 + j)),
                  pl.BlockSpec((t, LANE), lambda j: (0, j))],
        out_specs=[pl.BlockSpec((2, t, DH), lambda j: (j, 0, 0)), pl.BlockSpec((t, LANE), lambda j: (0, j))],
        out_shape=[jax.ShapeDtypeStruct((HEADS, t, DH), F32), jax.ShapeDtypeStruct((t, WIDTH), F32)],
        compiler_params=_params(("arbitrary",)),
    )(o, pm, dmix)


def _tail(x, mixg, mixf, tgt, wo, n2w, wg, wu, wd, fw):
    t, d = x.shape
    dff = wg.shape[1]
    tb = min(TB, t)

    def body(x_ref, mg_ref, mf_ref, t_ref, wo_ref, n2_ref, wg_ref, wu_ref, wd_ref, fw_ref,
             h2_ref, act_ref, dgate_ref, dup_ref, dx3_ref, dx2_ref, dmg_ref, dmf_ref, dn2_ref, dfw_ref, loss_ref):
        @pl.when(pl.program_id(0) == 0)
        def _():
            dn2_ref[...] = jnp.zeros_like(dn2_ref)
            dfw_ref[...] = jnp.zeros_like(dfw_ref)
            loss_ref[...] = jnp.zeros_like(loss_ref)

        x2 = x_ref[...] + _dot(mg_ref[...], wo_ref[0:WIDTH, :]) + _dot(mf_ref[...], wo_ref[WIDTH:2 * WIDTH, :])
        r2 = lax.rsqrt(jnp.mean(x2 * x2, axis=-1, keepdims=True) + EPS)
        xn2 = x2 * r2
        h2 = (xn2 * n2_ref[...]).astype(MXU)
        h2_ref[...] = h2
        gate = _dot(h2, wg_ref[...])
        up = _dot(h2, wu_ref[...])
        sg = _sigmoid(gate)
        sl = gate * sg
        act = (sl * up).astype(MXU)
        act_ref[...] = act
        x3 = x2 + _dot(act, wd_ref[...])
        r3 = lax.rsqrt(jnp.mean(x3 * x3, axis=-1, keepdims=True) + EPS)
        xn3 = x3 * r3
        err = xn3 * fw_ref[...] - t_ref[...]
        loss_ref[...] += 0.5 * jnp.sum(jnp.mean(err * err, axis=-1, keepdims=True), axis=0, keepdims=True)
        dy = err * (1.0 / d)
        dfw_ref[...] += jnp.sum(dy * xn3, axis=0, keepdims=True)
        g3 = dy * fw_ref[...]
        dx3 = r3 * (g3 - xn3 * jnp.mean(g3 * xn3, axis=-1, keepdims=True))
        dx3_ref[...] = dx3.astype(MXU)
        dact = _dot_nt(dx3, wd_ref[...])
        dgate = (dact * up * (sg * (1.0 + gate * (1.0 - sg)))).astype(MXU)
        dup = (dact * sl).astype(MXU)
        dgate_ref[...] = dgate
        dup_ref[...] = dup
        dh2 = _dot_nt(dgate, wg_ref[...]) + _dot_nt(dup, wu_ref[...])
        dn2_ref[...] += jnp.sum(dh2 * xn2, axis=0, keepdims=True)
        g2 = dh2 * n2_ref[...]
        dx2 = dx3 + r2 * (g2 - xn2 * jnp.mean(g2 * xn2, axis=-1, keepdims=True))
        dx2_ref[...] = dx2
        dmg_ref[...] = _dot_nt(dx2, wo_ref[0:WIDTH, :])
        dmf_ref[...] = _dot_nt(dx2, wo_ref[WIDTH:2 * WIDTH, :])

    def tok(n):
        return pl.BlockSpec((tb, n), lambda i: (i, 0))

    acc = pl.BlockSpec((1, d), lambda i: (0, 0))
    sds = jax.ShapeDtypeStruct
    return pl.pallas_call(
        body, name="tail", grid=(t // tb,),
        in_specs=[tok(d), tok(WIDTH), tok(WIDTH), tok(d), _resident(wo.shape), _resident((1, d)),
                  _resident(wg.shape), _resident(wu.shape), _resident(wd.shape), _resident((1, d))],
        out_specs=[tok(d), tok(dff), tok(dff), tok(dff), tok(d), tok(d), tok(WIDTH), tok(WIDTH), acc, acc,
                   pl.BlockSpec((1, 1), lambda i: (0, 0))],
        out_shape=[sds((t, d), MXU), sds((t, dff), MXU), sds((t, dff), MXU), sds((t, dff), MXU), sds((t, d), MXU),
                   sds((t, d), F32), sds((t, WIDTH), F32), sds((t, WIDTH), F32), sds((1, d), F32), sds((1, d), F32),
                   sds((1, 1), F32)],
        compiler_params=_params(("arbitrary",)),
    )(x, mixg, mixf, tgt, wo, n2w, wg, wu, wd, fw)


def _wgrad(a, b, name):
    t, m = a.shape
    n = b.shape[1]
    bn = 256 if n % 256 == 0 else LANE

    def body(a_ref, b_ref, o_ref):
        o_ref[...] = _dot_tn(a_ref[...], b_ref[...]).astype(o_ref.dtype)

    return pl.pallas_call(
        body, name=name, grid=(n // bn,),
        in_specs=[_resident((t, m)), pl.BlockSpec((t, bn), lambda j: (0, j))],
        out_specs=pl.BlockSpec((m, bn), lambda j: (0, j)),
        out_shape=jax.ShapeDtypeStruct((m, n), WIRE),
        compiler_params=_params(("arbitrary",)),
    )(a, b)


def _split_w_in(w_in):
    a = 4 * WIDTH
    b = a + 2 * HEADS
    c = b + 4 * WIDTH
    main = jnp.concatenate([w_in[:, :a], w_in[:, b:c]], axis=1)
    small = jnp.concatenate([w_in[:, a:b], w_in[:, c:], jnp.zeros((w_in.shape[0], LANE - 3 * HEADS), w_in.dtype)], axis=1)
    return main, small


def _merge_dw_in(d_gdn, d_z, d_fox, d_fg, d_small):
    return jnp.concatenate([d_gdn, d_z, d_small[:, :2 * HEADS], d_fox, d_fg, d_small[:, 2 * HEADS:3 * HEADS]], axis=1)


def _lanes(*pieces):
    v = jnp.concatenate([p.reshape(-1).astype(F32) for p in pieces])
    return jnp.pad(v, (0, LANE - v.shape[0])).reshape(1, LANE)


def _vector_params(p):
    d = p["norm1_w"].size
    gparams = jnp.concatenate([_lanes(jnp.zeros(HEADS), p["gdn_dt_bias"], p["fox_f_bias"]),
                               _lanes(jnp.zeros(HEADS), p["gdn_A_log"]), jnp.zeros((6, LANE), F32)])
    fox_nw = jnp.stack([jnp.tile(p["fox_q_norm_w"].reshape(-1), 2), jnp.tile(p["fox_k_norm_w"].reshape(-1), 2),
                        jnp.ones((LANE,), F32)])
    return dict(n1w=p["norm1_w"].reshape(1, d), n2w=p["norm2_w"].reshape(1, d), fw=p["final_norm_w"].reshape(1, d),
                onw=p["gdn_out_norm_w"].reshape(1, DH), gparams=gparams, fox_nw=fox_nw)


def _mixer_forward(x, vp, w_main, w_small, conv_w):
    h1, pm, ps = _inproj(x, vp["n1w"], w_main, w_small)
    gates, run, tot, run_t = _gates(ps, vp["gparams"])
    gqkv = _gdn_prep(pm, conv_w)
    o_gdn, states, inv = _gdn_fwd(gqkv, gates, run, tot, run_t)
    mixg = _gdn_post(o_gdn, pm, vp["onw"])
    fqkv = _fox_prep(pm, vp["fox_nw"])
    o_fox, lse = _fox_fwd(fqkv, run, run_t)
    mixf = _fox_post(o_fox, pm)
    return dict(h1=h1, pm=pm, ps=ps, gates=gates, run=run, tot=tot, run_t=run_t, gqkv=gqkv, o_gdn=o_gdn, states=states,
                inv=inv, mixg=mixg, fqkv=fqkv, o_fox=o_fox, lse=lse, mixf=mixf)


def _mixer_backward(x, vp, w_main, w_small, conv_w, f, dx2, dmixg, dmixf, onw):
    pm = f["pm"]
    do_gdn, dz, donw = _gdn_post_bwd(f["o_gdn"], pm, onw, dmixg)
    dgqkv, dcol_g, dtot_g, drow_g = _gdn_bwd(f["gqkv"], f["gates"], f["run"], f["tot"], f["run_t"], f["inv"], f["states"], do_gdn)
    dgdn, dconv = _gdn_prep_bwd(pm, conv_w, dgqkv)
    do_fox, dfg = _fox_post_bwd(f["o_fox"], pm, dmixf)
    dfqkv, dcol_f, drow_f = _fox_bwd(f["fqkv"], f["run"], f["run_t"], f["o_fox"], f["lse"], do_fox)
    dfox, dfnw = _fox_prep_bwd(pm, vp["fox_nw"], dfqkv)
    dps, gsum = _gates_bwd(f["ps"], vp["gparams"], dcol_g, dtot_g, dcol_f, drow_g, drow_f)
    grad_x, dn1w = _inproj_bwd(x, vp["n1w"], dx2, dgdn, dz, dfox, dfg, dps, w_main, w_small)
    h1 = f["h1"]
    dw_in = _merge_dw_in(_wgrad(h1, dgdn, "dw_in_gdn"), _wgrad(h1, dz, "dw_in_z"), _wgrad(h1, dfox, "dw_in_fox"),
                         _wgrad(h1, dfg, "dw_in_fgate"), _wgrad(h1, dps, "dw_in_small"))
    return grad_x, dw_in, dconv, dict(dn1w=dn1w, gsum=gsum, donw=donw, dfnw=dfnw)


VECTORS = ("norm1_w", "norm2_w", "final_norm_w", "gdn_A_log", "gdn_dt_bias", "gdn_out_norm_w", "fox_f_bias",
           "fox_q_norm_w", "fox_k_norm_w")
VEC_ROWS = 16
LOSS_ROW = len(VECTORS)


def _pack_vectors(dn1w, dn2w, dfw, gsum, donw, dfnw, loss):
    d = dn1w.shape[1]

    def body(n1_ref, n2_ref, fw_ref, gs_ref, on_ref, fn_ref, loss_ref, o_ref):
        o_ref[...] = jnp.zeros_like(o_ref)
        o_ref[0:1, :] = n1_ref[...]
        o_ref[1:2, :] = n2_ref[...]
        o_ref[2:3, :] = fw_ref[...]
        o_ref[3:4, 0:HEADS] = gs_ref[0:1, 0:HEADS]
        o_ref[4:5, 0:HEADS] = gs_ref[1:2, 0:HEADS]
        o_ref[5:6, 0:DH] = on_ref[...]
        o_ref[6:7, 0:HEADS] = gs_ref[2:3, 0:HEADS]
        for kind in range(2):
            v = fn_ref[kind, 0]
            for j in range(1, fn_ref.shape[1]):
                v = v + fn_ref[kind, j]
            o_ref[7 + kind:8 + kind, 0:DH] = v[:, :DH] + v[:, DH:]
        o_ref[LOSS_ROW:LOSS_ROW + 1, 0:1] = loss_ref[...]

    return pl.pallas_call(body, name="pack_vectors", out_shape=jax.ShapeDtypeStruct((VEC_ROWS, d), F32),
                          compiler_params=_params())(dn1w, dn2w, dfw, gsum, donw, dfnw, loss)


def _local_step(x, tgt, p, w_in, conv_w, wo, wg, wu, wd):
    vp = _vector_params(p)
    w_main, w_small = _split_w_in(w_in)
    f = _mixer_forward(x, vp, w_main, w_small, conv_w)
    (h2, act, dgate, dup, dx3, dx2, dmixg, dmixf, dn2w, dfw, loss) = _tail(
        x, f["mixg"], f["mixf"], tgt, wo, vp["n2w"], wg, wu, wd, vp["fw"])
    grad_x, dw_in, dconv, small = _mixer_backward(x, vp, w_main, w_small, conv_w, f, dx2, dmixg, dmixf, vp["onw"])
    grads = {"w_in": dw_in, "gdn_conv_w": dconv,
             "w_out": jnp.concatenate([_wgrad(f["mixg"], dx2, "dw_out_gdn"), _wgrad(f["mixf"], dx2, "dw_out_fox")], axis=0),
             "w_ffn_gate": _wgrad(h2, dgate, "dw_gate"), "w_ffn_up": _wgrad(h2, dup, "dw_up"),
             "w_ffn_down": _wgrad(act, dx3, "dw_down")}
    vec = _pack_vectors(small["dn1w"], dn2w, dfw, small["gsum"], small["donw"], small["dfnw"], loss)
    return grad_x, grads, vec


def _my_place():
    return lax.axis_index("x"), lax.axis_index("y"), lax.axis_index("c")


def _peers():
    x, y, c = _my_place()
    peers = []
    for k in range(1, N_DEV):
        px = 1 - x if k & 4 else x
        py = 1 - y if k & 2 else y
        pc = 1 - c if k & 1 else c
        peers.append(((px, py, pc), 4 * px + 2 * py + pc))
    return 4 * x + 2 * y + c, peers


def _spread_copies(kind, srcs, lands, send_sems, recv_sems):
    me, peers = _peers()
    kinds = [kind] * len(srcs) if isinstance(kind, str) else kind
    remote, local = [], []
    for i, (kd, src, land) in enumerate(zip(kinds, srcs, lands)):
        for k, (dev, idx) in enumerate(peers):
            remote.append(pltpu.make_async_remote_copy(
                src_ref=src if kd == "gather" else src.at[idx], dst_ref=land.at[me],
                send_sem=send_sems.at[i * (N_DEV - 1) + k], recv_sem=recv_sems.at[i * (N_DEV - 1) + k],
                device_id=dev, device_id_type=MESH))
        local.append((src if kd == "gather" else src.at[me], land.at[me]))
    return remote, local


def _land_shape(kind, a):
    return (N_DEV,) + a.shape if kind == "gather" else a.shape


HBM = pl.BlockSpec(memory_space=pltpu.HBM)
SEM = pl.BlockSpec(memory_space=pltpu.SEMAPHORE)


def _hbm(a):
    return pltpu.with_memory_space_constraint(a, pltpu.HBM)


def _spread_start(groups, kind, name):
    flat = [a for g in groups for a in g]
    n = len(flat)
    ng = len(groups)

    def body(*refs):
        srcs, lands = refs[:n], refs[n:2 * n]
        sems = refs[2 * n:2 * n + 2 * ng]
        token = refs[4 * n + 2 * ng]
        local_sems = refs[4 * n + 2 * ng + 1]
        off = 0
        remotes, locals_ = [], []
        for g in range(ng):
            m = len(groups[g])
            remote, local = _spread_copies(kind, srcs[off:off + m], lands[off:off + m], sems[2 * g], sems[2 * g + 1])
            remotes += remote
            locals_ += local
            off += m
        copies = [pltpu.make_async_copy(s, d, local_sems.at[i]) for i, (s, d) in enumerate(locals_)]
        for cp in copies:
            cp.start()
        for cp in copies:
            cp.wait()
        for cp in remotes:
            cp.start()
        token[...] = jnp.zeros_like(token)

    sem_shapes = []
    for g in groups:
        sem_shapes += [pltpu.SemaphoreType.DMA((len(g) * (N_DEV - 1),))] * 2
    lands = [lax.empty(_land_shape(kind, a), a.dtype) for a in flat]
    out = pl.pallas_call(
        body, name=name,
        out_shape=tuple(sem_shapes) + tuple(pltpu.HBM(a.shape, a.dtype) for a in flat + lands)
        + (jax.ShapeDtypeStruct((8, LANE), F32),),
        in_specs=[HBM] * (2 * n), out_specs=tuple([SEM] * (2 * ng) + [HBM] * (2 * n) + [pl.BlockSpec(memory_space=pltpu.VMEM)]),
        input_output_aliases={j: 2 * ng + j for j in range(2 * n)},
        scratch_shapes=[pltpu.SemaphoreType.DMA((n,))],
        compiler_params=pltpu.CompilerParams(has_side_effects=pltpu.SideEffectType.DATAFLOW_SIDE_EFFECTING),
    )(*[_hbm(a) for a in flat], *[_hbm(a) for a in lands])
    states, off = [], 0
    for g in range(ng):
        m = len(groups[g])
        states.append((list(out[2 * ng + off:2 * ng + off + m]), list(out[2 * ng + n + off:2 * ng + n + off + m]),
                       out[2 * g], out[2 * g + 1]))
        off += m
    return states, out[-1]


def _spread_wait(state, kind, after, name):
    srcs, lands, send_sems, recv_sems = state
    n = len(srcs)

    def body(*refs):
        remote, _ = _spread_copies(kind, refs[:n], refs[n:2 * n], refs[2 * n], refs[2 * n + 1])
        for cp in remote:
            cp.wait_send()
        for cp in remote:
            cp.wait_recv()

    out = pl.pallas_call(
        body, name=name,
        out_shape=tuple(pltpu.HBM(a.shape, a.dtype) for a in srcs + lands),
        in_specs=[HBM] * (2 * n) + [SEM, SEM, pl.BlockSpec(memory_space=pl.ANY)], out_specs=tuple([HBM] * (2 * n)),
        input_output_aliases={j: j for j in range(2 * n)},
        compiler_params=pltpu.CompilerParams(has_side_effects=pltpu.SideEffectType.DATAFLOW_SIDE_EFFECTING),
    )(*srcs, *lands, send_sems, recv_sems, after)
    return list(out[n:])


def _exchange(scatter, gather, name):
    kinds = ["scatter"] * len(scatter) + ["gather"] * len(gather)
    arrays = list(scatter) + list(gather)
    n = len(arrays)

    def body(*refs):
        send_sems, recv_sems, local_sems = refs[2 * n:]
        remote, local = _spread_copies(kinds, refs[:n], refs[n:2 * n], send_sems, recv_sems)
        copies = [pltpu.make_async_copy(s, d, local_sems.at[i]) for i, (s, d) in enumerate(local)]
        for cp in remote + copies:
            cp.start()
        for cp in remote:
            cp.wait_recv()
        for cp in remote:
            cp.wait_send()
        for cp in copies:
            cp.wait()

    return pl.pallas_call(
        body, name=name,
        out_shape=[jax.ShapeDtypeStruct(_land_shape(k, a), a.dtype) for k, a in zip(kinds, arrays)],
        in_specs=[pl.BlockSpec(memory_space=pl.ANY)] * n, out_specs=[pl.BlockSpec(memory_space=pl.ANY)] * n,
        scratch_shapes=[pltpu.SemaphoreType.DMA((n * (N_DEV - 1),)), pltpu.SemaphoreType.DMA((n * (N_DEV - 1),)),
                        pltpu.SemaphoreType.DMA((n,))],
    )(*arrays)


ADAM_ROWS = 128


def _adam_math(g, w, m, v):
    nm = ADAM_B1 * m + (1.0 - ADAM_B1) * g
    nv = ADAM_B2 * v + (1.0 - ADAM_B2) * (g * g)
    m_hat = nm / (1.0 - ADAM_B1 ** ADAM_STEP)
    v_hat = nv / (1.0 - ADAM_B2 ** ADAM_STEP)
    return -ADAM_LR * (m_hat / (jnp.sqrt(v_hat) + ADAM_EPS) + ADAM_WD * w), nm, nv


def _sum_parts(p_ref):
    g = p_ref[0].astype(F32)
    for s in range(1, N_DEV):
        g = g + p_ref[s].astype(F32)
    return g


def _adam_matrix(parts, w, m, v, name):
    _, r, c = w.shape
    rb = ADAM_ROWS if r % ADAM_ROWS == 0 else r

    def body(p_ref, w_ref, m_ref, v_ref, g_ref, d_ref, nm_ref, nv_ref):
        g = _sum_parts(p_ref)
        g_ref[0] = g
        d_ref[0], nm_ref[0], nv_ref[0] = _adam_math(g, w_ref[0], m_ref[0], v_ref[0])

    blk = pl.BlockSpec((1, rb, c), lambda i: (0, i, 0))
    return pl.pallas_call(
        body, name=name, grid=(r // rb,),
        in_specs=[pl.BlockSpec((N_DEV, rb, c), lambda i: (0, i, 0)), blk, blk, blk],
        out_specs=[blk] * 4, out_shape=[jax.ShapeDtypeStruct(w.shape, F32)] * 4,
        compiler_params=_params(("arbitrary",)),
    )(parts, w, m, v)


def _adam_vectors(parts, ws, ms, vs):
    nv = len(ws)

    def body(*refs):
        p_ref = refs[0]
        w_refs, m_refs, v_refs = refs[1:1 + nv], refs[1 + nv:1 + 2 * nv], refs[1 + 2 * nv:1 + 3 * nv]
        outs = refs[1 + 3 * nv:]
        g_all = _sum_parts(p_ref)
        for i in range(nv):
            n = w_refs[i].shape[1]
            g = g_all[i:i + 1, 0:n]
            d, nm, nvv = _adam_math(g, w_refs[i][...], m_refs[i][...], v_refs[i][...])
            outs[i][...] = g
            outs[nv + i][...] = d
            outs[2 * nv + i][...] = nm
            outs[3 * nv + i][...] = nvv
        outs[4 * nv][...] = g_all[LOSS_ROW:LOSS_ROW + 1, 0:1]

    shapes = [jax.ShapeDtypeStruct(a.shape, F32) for a in ws]
    out = pl.pallas_call(body, name="adam_vectors", out_shape=shapes * 4 + [jax.ShapeDtypeStruct((1, 1), F32)],
                         compiler_params=_params())(parts, *ws, *ms, *vs)
    return out[:nv], out[nv:2 * nv], out[2 * nv:3 * nv], out[3 * nv:4 * nv], out[4 * nv]


MATRICES = (("w_in", 1), ("gdn_conv_w", 1), ("w_out", 0), ("w_ffn_gate", 1), ("w_ffn_up", 1), ("w_ffn_down", 0))
WEIGHTS = ("norm1_w", "w_in", "gdn_conv_w", "gdn_A_log", "gdn_dt_bias", "gdn_out_norm_w", "fox_f_bias", "fox_q_norm_w",
           "fox_k_norm_w", "w_out", "norm2_w", "w_ffn_gate", "w_ffn_up", "w_ffn_down", "final_norm_w")


def _join(blocks, axis):
    _, r, c = blocks.shape
    if axis == 0:
        return blocks.reshape(N_DEV * r, c)
    return blocks.transpose(1, 0, 2).reshape(r, N_DEV * c)


def _cut(full, axis):
    r, c = full.shape
    if axis == 0:
        return full.reshape(N_DEV, r // N_DEV, c)
    return full.reshape(r, N_DEV, c // N_DEV).transpose(1, 0, 2)


def kernel(x, norm1_w, w_in, gdn_conv_w, gdn_A_log, gdn_dt_bias, gdn_out_norm_w, fox_f_bias, fox_q_norm_w, fox_k_norm_w, w_out, norm2_w, w_ffn_gate, w_ffn_up, w_ffn_down, final_norm_w, loss_target, m_norm1_w, m_w_in, m_gdn_conv_w, m_gdn_A_log, m_gdn_dt_bias, m_gdn_out_norm_w, m_fox_f_bias, m_fox_q_norm_w, m_fox_k_norm_w, m_w_out, m_norm2_w, m_w_ffn_gate, m_w_ffn_up, m_w_ffn_down, m_final_norm_w, v_norm1_w, v_w_in, v_gdn_conv_w, v_gdn_A_log, v_gdn_dt_bias, v_gdn_out_norm_w, v_fox_f_bias, v_fox_q_norm_w, v_fox_k_norm_w, v_w_out, v_norm2_w, v_w_ffn_gate, v_w_ffn_up, v_w_ffn_down, v_final_norm_w):
    w = dict(norm1_w=norm1_w, w_in=w_in, gdn_conv_w=gdn_conv_w, gdn_A_log=gdn_A_log, gdn_dt_bias=gdn_dt_bias,
             gdn_out_norm_w=gdn_out_norm_w, fox_f_bias=fox_f_bias, fox_q_norm_w=fox_q_norm_w, fox_k_norm_w=fox_k_norm_w,
             w_out=w_out, norm2_w=norm2_w, w_ffn_gate=w_ffn_gate, w_ffn_up=w_ffn_up, w_ffn_down=w_ffn_down,
             final_norm_w=final_norm_w)
    m = dict(norm1_w=m_norm1_w, w_in=m_w_in, gdn_conv_w=m_gdn_conv_w, gdn_A_log=m_gdn_A_log, gdn_dt_bias=m_gdn_dt_bias,
             gdn_out_norm_w=m_gdn_out_norm_w, fox_f_bias=m_fox_f_bias, fox_q_norm_w=m_fox_q_norm_w,
             fox_k_norm_w=m_fox_k_norm_w, w_out=m_w_out, norm2_w=m_norm2_w, w_ffn_gate=m_w_ffn_gate,
             w_ffn_up=m_w_ffn_up, w_ffn_down=m_w_ffn_down, final_norm_w=m_final_norm_w)
    v = dict(norm1_w=v_norm1_w, w_in=v_w_in, gdn_conv_w=v_gdn_conv_w, gdn_A_log=v_gdn_A_log, gdn_dt_bias=v_gdn_dt_bias,
             gdn_out_norm_w=v_gdn_out_norm_w, fox_f_bias=v_fox_f_bias, fox_q_norm_w=v_fox_q_norm_w,
             fox_k_norm_w=v_fox_k_norm_w, w_out=v_w_out, norm2_w=v_norm2_w, w_ffn_gate=v_w_ffn_gate,
             w_ffn_up=v_w_ffn_up, w_ffn_down=v_w_ffn_down, final_norm_w=v_final_norm_w)
    axis_of = dict(MATRICES)
    late = ("w_out", "w_ffn_gate", "w_ffn_up", "w_ffn_down")
    xs, tgt = x[0], loss_target[0]
    vp = _vector_params({n: w[n] for n in VECTORS})

    (st_in, st_late), token = _spread_start(
        [[w["w_in"][0].astype(WIRE), w["gdn_conv_w"][0]], [w[n][0].astype(WIRE) for n in late]], "gather", "gather_start")
    w_in_blocks, conv_blocks = _spread_wait(st_in, "gather", token, "gather_wait_in")
    w_main, w_small = _split_w_in(_join(w_in_blocks, 1))
    conv_w = _join(conv_blocks, 1)
    f = _mixer_forward(xs, vp, w_main, w_small, conv_w)
    full = {n: _join(b, axis_of[n]) for n, b in zip(late, _spread_wait(st_late, "gather", f["mixf"], "gather_wait_late"))}

    (h2, act, dgate, dup, dx3, dx2, dmixg, dmixf, dn2w, dfw, loss) = _tail(
        xs, f["mixg"], f["mixf"], tgt, full["w_out"], vp["n2w"], full["w_ffn_gate"], full["w_ffn_up"],
        full["w_ffn_down"], vp["fw"])
    dlate = {"w_out": jnp.concatenate([_wgrad(f["mixg"], dx2, "dw_out_gdn"), _wgrad(f["mixf"], dx2, "dw_out_fox")], axis=0),
             "w_ffn_gate": _wgrad(h2, dgate, "dw_gate"), "w_ffn_up": _wgrad(h2, dup, "dw_up"),
             "w_ffn_down": _wgrad(act, dx3, "dw_down")}

    (st_grads,), token = _spread_start([[_cut(dlate[n], axis_of[n]) for n in late]], "scatter", "grads_start")
    grad_x, dw_in, dconv, small = _mixer_backward(xs, vp, w_main, w_small, conv_w, f, dx2, dmixg, dmixf,
                                                  vp["onw"] + token[0:1, 0:DH])
    vec = _pack_vectors(small["dn1w"], dn2w, dfw, small["gsum"], small["donw"], small["dfnw"], loss)
    parts_in, parts_conv, parts_vec = _exchange([_cut(dw_in, 1), _cut(dconv, 1)], [vec], "exchange_last")
    parts = dict(zip(late, _spread_wait(st_grads, "scatter", parts_vec, "grads_wait")))
    parts["w_in"], parts["gdn_conv_w"] = parts_in, parts_conv

    results = [{}, {}, {}, {}]
    for n, _ in MATRICES:
        for d, a in zip(results, _adam_matrix(parts[n], w[n], m[n], v[n], "adam_" + n)):
            d[n] = a
    row = lambda a: a.reshape(1, -1)
    *vec_out, total_loss = _adam_vectors(parts_vec, [row(w[n]) for n in VECTORS], [row(m[n]) for n in VECTORS],
                                         [row(v[n]) for n in VECTORS])
    for d, arrs in zip(results, vec_out):
        for n, a in zip(VECTORS, arrs):
            d[n] = a.reshape(w[n].shape)
    return (total_loss[0, 0], grad_x[None], *[d[n] for d in results for n in WEIGHTS])
```

```python
import functools

import jax
import jax.numpy as jnp
from jax import lax
from jax.experimental import pallas as pl
from jax.experimental.pallas import tpu as pltpu

F32 = jnp.float32
MXU = jnp.bfloat16
WIRE = jnp.bfloat16
HI = lax.Precision.HIGHEST
EPS = 1e-6

N_DEV = 8
HEADS = 8
DH = 64
WIDTH = HEADS * DH
CHUNK = 64
LANE = 128
ROW_ALIGN = 16
TB = 256
QB = 256
VMEM_LIMIT = 60 * 1024 * 1024

ADAM_LR = 0.001
ADAM_B1 = 0.9
ADAM_B2 = 0.999
ADAM_EPS = 1e-08
ADAM_WD = 0.01
ADAM_STEP = 10

MESH = pl.DeviceIdType.MESH


def _params(sem=None):
    return pltpu.CompilerParams(dimension_semantics=sem, vmem_limit_bytes=VMEM_LIMIT)


def _resident(shape):
    n = len(shape)
    return pl.BlockSpec(shape, lambda *_: (0,) * n, pipeline_mode=pl.Buffered(1))


def _dot(a, b):
    return jnp.dot(a.astype(MXU), b.astype(MXU), preferred_element_type=F32)


def _dot_nt(a, b):
    return lax.dot_general(a.astype(MXU), b.astype(MXU), (((1,), (1,)), ((), ())), preferred_element_type=F32)


def _dot_tn(a, b):
    return lax.dot_general(a.astype(MXU), b.astype(MXU), (((0,), (0,)), ((), ())), preferred_element_type=F32)


def _hdot(a, b):
    return jnp.dot(a, b, precision=HI, preferred_element_type=F32)


def _hdot_nt(a, b):
    return lax.dot_general(a, b, (((1,), (1,)), ((), ())), precision=HI, preferred_element_type=F32)


def _hdot_tn(a, b):
    return lax.dot_general(a, b, (((0,), (0,)), ((), ())), precision=HI, preferred_element_type=F32)


def _sigmoid(x):
    return 1.0 / (1.0 + jnp.exp(-x))


def _softplus(x):
    return jnp.maximum(x, 0.0) + jnp.log(1.0 + jnp.exp(-jnp.abs(x)))


def _head_sum_matrix():
    ri = lax.broadcasted_iota(jnp.int32, (LANE, LANE), 0) // DH
    ci = lax.broadcasted_iota(jnp.int32, (LANE, LANE), 1) // DH
    return (ri == ci).astype(F32)


def _group_sum(a, ones_matrix):
    hi = a.astype(jnp.bfloat16)
    lo = (a - hi.astype(F32)).astype(jnp.bfloat16)
    m = ones_matrix.astype(jnp.bfloat16)
    return jnp.dot(hi, m, preferred_element_type=F32) + jnp.dot(lo, m, preferred_element_type=F32)


def _shift_down(x, s, rows):
    return jnp.where(rows >= s, pltpu.roll(x, s, 0), 0.0)


def _shift_up(x, s, rows):
    t = x.shape[0]
    return jnp.where(rows < t - s, pltpu.roll(x, t - s, 0), 0.0)


def _inproj(x, n1w, w_main, w_small):
    t, d = x.shape
    nm = w_main.shape[1]
    tb = min(TB, t)

    def body(x_ref, nw_ref, wm_ref, ws_ref, h_ref, pm_ref, ps_ref):
        xv = x_ref[...]
        r = lax.rsqrt(jnp.mean(xv * xv, axis=-1, keepdims=True) + EPS)
        h = (xv * r * nw_ref[...]).astype(MXU)
        h_ref[...] = h
        pm_ref[...] = jnp.dot(h, wm_ref[...], preferred_element_type=F32)
        ps_ref[...] = jnp.dot(h, ws_ref[...], preferred_element_type=F32)

    return pl.pallas_call(
        body, name="inproj", grid=(t // tb,),
        in_specs=[pl.BlockSpec((tb, d), lambda i: (i, 0)), _resident((1, d)), _resident((d, nm)), _resident((d, LANE))],
        out_specs=[pl.BlockSpec((tb, d), lambda i: (i, 0)), pl.BlockSpec((tb, nm), lambda i: (i, 0)),
                   pl.BlockSpec((tb, LANE), lambda i: (i, 0))],
        out_shape=[jax.ShapeDtypeStruct((t, d), MXU), jax.ShapeDtypeStruct((t, nm), F32),
                   jax.ShapeDtypeStruct((t, LANE), F32)],
        compiler_params=_params(("arbitrary",)),
    )(x, n1w, w_main, w_small)


def _inproj_bwd(x, n1w, dx2, dgdn, dz, dfox, dfg, dps, w_main, w_small):
    t, d = x.shape
    tb = min(TB, t)
    w3 = 3 * WIDTH

    def body(x_ref, nw_ref, dx2_ref, dgdn_ref, dz_ref, dfox_ref, dfg_ref, dps_ref, wm_ref, ws_ref, gx_ref, dnw_ref):
        dh = _dot_nt(dgdn_ref[...], wm_ref[:, 0:w3])
        dh += _dot_nt(dz_ref[...], wm_ref[:, w3:w3 + WIDTH])
        dh += _dot_nt(dfox_ref[...], wm_ref[:, w3 + WIDTH:2 * w3 + WIDTH])
        dh += _dot_nt(dfg_ref[...], wm_ref[:, 2 * w3 + WIDTH:2 * w3 + 2 * WIDTH])
        dh += _dot_nt(dps_ref[...], ws_ref[...])
        xv = x_ref[...]
        r = lax.rsqrt(jnp.mean(xv * xv, axis=-1, keepdims=True) + EPS)
        xn = xv * r

        @pl.when(pl.program_id(0) == 0)
        def _():
            dnw_ref[...] = jnp.zeros_like(dnw_ref)

        dnw_ref[...] += jnp.sum(dh * xn, axis=0, keepdims=True)
        g = dh * nw_ref[...]
        gx_ref[...] = dx2_ref[...] + r * (g - xn * jnp.mean(g * xn, axis=-1, keepdims=True))

    def tok(n):
        return pl.BlockSpec((tb, n), lambda i: (i, 0))

    return pl.pallas_call(
        body, name="inproj_bwd", grid=(t // tb,),
        in_specs=[tok(d), _resident((1, d)), tok(d), tok(w3), tok(WIDTH), tok(w3), tok(WIDTH), tok(LANE),
                  _resident(w_main.shape), _resident(w_small.shape)],
        out_specs=[tok(d), pl.BlockSpec((1, d), lambda i: (0, 0))],
        out_shape=[jax.ShapeDtypeStruct((t, d), F32), jax.ShapeDtypeStruct((1, d), F32)],
        compiler_params=_params(("arbitrary",)),
    )(x, n1w, dx2, dgdn, dz, dfox, dfg, dps, w_main, w_small)


def _gate_lanes(shape):
    lane = lax.broadcasted_iota(jnp.int32, shape, 1)
    return lane < HEADS, (lane >= HEADS) & (lane < 2 * HEADS), (lane >= 2 * HEADS) & (lane < 3 * HEADS)


def _block_masks():
    ri = lax.broadcasted_iota(jnp.int32, (LANE, LANE), 0)
    ci = lax.broadcasted_iota(jnp.int32, (LANE, LANE), 1)
    same = (ri // CHUNK) == (ci // CHUNK)
    return ((ri >= ci).astype(F32), (ri <= ci).astype(F32), (same & (ri >= ci)).astype(F32),
            (same & (ri <= ci)).astype(F32), same.astype(F32))


def _gates(ps, gparams):
    t = ps.shape[0]
    nb = t // LANE

    def body(ps_ref, gp_ref, out_ref, run_ref, tot_ref, runt_ref):
        p = ps_ref[...]
        is_b, is_a, is_f = _gate_lanes(p.shape)
        z = p + gp_ref[0:1, :]
        neg_exp_a = -jnp.exp(gp_ref[1:2, :])
        glog = neg_exp_a * _softplus(z)
        logf = -_softplus(-z)
        out_ref[...] = jnp.where(is_b, _sigmoid(p), jnp.where(is_a, glog, jnp.where(is_f, logf, 0.0)))
        tril, _, tril_c, _, same_c = _block_masks()
        off = jnp.zeros((1, LANE), F32)
        for b in range(nb):
            rows = slice(b * LANE, (b + 1) * LANE)
            blk = out_ref[rows, :]
            ga = jnp.where(is_a[:LANE], blk, 0.0)
            fb = _hdot(tril, jnp.where(is_f[:LANE], blk, 0.0)) + off
            run = fb + _hdot(tril_c, ga)
            run_ref[rows, :] = run
            runt_ref[:, rows] = run.T
            tot_ref[rows, :] = _hdot(same_c, ga)
            off = fb[LANE - 1:LANE, :]

    return pl.pallas_call(
        body, name="gates",
        out_shape=[jax.ShapeDtypeStruct((t, LANE), F32)] * 3 + [jax.ShapeDtypeStruct((LANE, t), F32)],
        compiler_params=_params(),
    )(ps, gparams)


def _gates_bwd(ps, gparams, dcol_g, dtot_g, dcol_f, drow_g, drow_f):
    t = ps.shape[0]
    nb = t // LANE

    def body(ps_ref, gp_ref, dcg_ref, dtg_ref, dcf_ref, drg_ref, drf_ref, dps_ref, sums_ref, dl_ref, d0_ref, tr_ref):
        p = ps_ref[...]
        is_b, is_a, is_f = _gate_lanes(p.shape)
        _, triu, _, triu_c, same_c = _block_masks()
        tr_ref[...] = jnp.zeros_like(tr_ref)
        off = jnp.zeros((1, LANE), F32)
        for b in reversed(range(nb)):
            rows = slice(b * LANE, (b + 1) * LANE)
            tr_ref[HEADS:2 * HEADS, :] = drg_ref[:, rows]
            tr_ref[2 * HEADS:3 * HEADS, :] = drf_ref[:, rows]
            d = dcg_ref[rows, :] + dcf_ref[rows, :] + tr_ref[...].T
            d0_ref[rows, :] = d
            dlf = _hdot(triu, jnp.where(is_f[:LANE], d, 0.0)) + off
            dla = (_hdot(triu_c, jnp.where(is_a[:LANE], d, 0.0))
                   + _hdot(same_c, jnp.where(is_a[:LANE], dtg_ref[rows, :], 0.0)))
            dl_ref[rows, :] = dlf + dla
            off = dlf[0:1, :]
        z = p + gp_ref[0:1, :]
        neg_exp_a = -jnp.exp(gp_ref[1:2, :])
        sb = _sigmoid(p)
        glog = neg_exp_a * _softplus(z)
        dl = dl_ref[...]
        dp = jnp.where(is_b, d0_ref[...] * sb * (1.0 - sb),
                       jnp.where(is_a, dl * neg_exp_a * _sigmoid(z), jnp.where(is_f, dl * _sigmoid(-z), 0.0)))
        dps_ref[...] = dp
        s_a = jnp.sum(jnp.where(is_a, dl * glog, 0.0), axis=0, keepdims=True)
        s_p = jnp.sum(jnp.where(is_b, 0.0, dp), axis=0, keepdims=True)
        row = lax.broadcasted_iota(jnp.int32, (8, LANE), 0)
        from_a = pltpu.roll(jnp.where(row == 0, s_a, jnp.where(row == 1, s_p, 0.0)), LANE - HEADS, 1)
        from_f = pltpu.roll(jnp.where(row == 2, s_p, 0.0), LANE - 2 * HEADS, 1)
        lane = lax.broadcasted_iota(jnp.int32, (8, LANE), 1)
        sums_ref[...] = jnp.where(lane < HEADS, from_a + from_f, 0.0)

    return pl.pallas_call(
        body, name="gates_bwd",
        out_shape=[jax.ShapeDtypeStruct((t, LANE), F32), jax.ShapeDtypeStruct((8, LANE), F32)],
        scratch_shapes=[pltpu.VMEM((t, LANE), F32), pltpu.VMEM((t, LANE), F32), pltpu.VMEM((LANE, LANE), F32)],
        compiler_params=_params(),
    )(ps, gparams, dcol_g, dtot_g, dcol_f, drow_g, drow_f)


def _conv(xv, w, rows):
    acc = w[3:4, :] * xv
    for s in range(1, 4):
        acc += w[3 - s:4 - s, :] * _shift_down(xv, s, rows)
    return acc


PREP_ROWS = 256
HALO = 8


def _tile_rows(r):
    return pl.ds(pl.multiple_of(r * PREP_ROWS, PREP_ROWS), PREP_ROWS)


def _gdn_prep(pm, conv_w):
    t = pm.shape[0]
    nj = WIDTH // LANE
    win = PREP_ROWS + HALO

    def body(x_ref, w_ref, o_ref, xp_ref):
        kind = pl.program_id(0)
        xp_ref[0:HALO, :] = jnp.zeros((HALO, LANE), F32)
        xp_ref[HALO:, :] = x_ref[...]
        w = w_ref[...]
        hs = _head_sum_matrix()
        rows = lax.broadcasted_iota(jnp.int32, (win, LANE), 0)

        def tile(r, _):
            xw = xp_ref[pl.ds(pl.multiple_of(r * PREP_ROWS, PREP_ROWS), win), :]
            acc = _conv(xw, w, rows)[HALO:]
            y = acc * _sigmoid(acc)
            out = jnp.where(kind < 2, y * lax.rsqrt(_group_sum(y * y, hs) + EPS), y)
            o_ref[0, 0, _tile_rows(r), :] = out[:, :DH]
            o_ref[0, 1, _tile_rows(r), :] = out[:, DH:]
            return 0

        lax.fori_loop(0, t // PREP_ROWS, tile, 0)

    return pl.pallas_call(
        body, name="gdn_prep", grid=(3, nj),
        in_specs=[pl.BlockSpec((t, LANE), lambda i, j: (0, i * nj + j)),
                  pl.BlockSpec((4, LANE), lambda i, j: (0, i * nj + j))],
        out_specs=pl.BlockSpec((1, 2, t, DH), lambda i, j: (i, j, 0, 0)),
        out_shape=jax.ShapeDtypeStruct((3, HEADS, t, DH), F32),
        scratch_shapes=[pltpu.VMEM((t + HALO, LANE), F32)],
        compiler_params=_params(("arbitrary", "arbitrary")),
    )(pm, conv_w)


def _gdn_prep_bwd(pm, conv_w, dqkv):
    t = pm.shape[0]
    nj = WIDTH // LANE
    win = PREP_ROWS + 2 * HALO

    def body(x_ref, w_ref, d_ref, dx_ref, dw_ref, xp_ref, dp_ref):
        kind = pl.program_id(0)
        zeros = jnp.zeros((HALO, LANE), F32)
        for ref in (xp_ref, dp_ref):
            ref[0:HALO, :] = zeros
            ref[HALO + t:, :] = zeros
        xp_ref[HALO:HALO + t, :] = x_ref[...]
        dp_ref[HALO:HALO + t, 0:DH] = d_ref[0, 0]
        dp_ref[HALO:HALO + t, DH:] = d_ref[0, 1]
        w = w_ref[...]
        hs = _head_sum_matrix()
        rows = lax.broadcasted_iota(jnp.int32, (win, LANE), 0)
        in_tile = (rows >= HALO) & (rows < HALO + PREP_ROWS)

        def tile(r, dw):
            start = pl.multiple_of(r * PREP_ROWS, PREP_ROWS)
            xw = xp_ref[pl.ds(start, win), :]
            d = dp_ref[pl.ds(start, win), :]
            acc = _conv(xw, w, rows)
            sg = _sigmoid(acc)
            y = acc * sg
            rn = lax.rsqrt(_group_sum(y * y, hs) + EPS)
            yn = y * rn
            dy = jnp.where(kind < 2, rn * (d - yn * _group_sum(d * yn, hs)), d)
            dacc = dy * sg * (1.0 + acc * (1.0 - sg))
            dx = w[3:4, :] * dacc
            for s in range(1, 4):
                dx += w[3 - s:4 - s, :] * _shift_up(dacc, s, rows)
            dx_ref[_tile_rows(r), :] = dx[HALO:HALO + PREP_ROWS]
            dm = jnp.where(in_tile, dacc, 0.0)
            return tuple(dw[i] + jnp.sum(dm * (xw if i == 3 else _shift_down(xw, 3 - i, rows)), axis=0, keepdims=True)
                         for i in range(4))

        dw = lax.fori_loop(0, t // PREP_ROWS, tile, tuple(jnp.zeros((1, LANE), F32) for _ in range(4)))
        for i in range(4):
            dw_ref[i:i + 1, :] = dw[i]

    return pl.pallas_call(
        body, name="gdn_prep_bwd", grid=(3, nj),
        in_specs=[pl.BlockSpec((t, LANE), lambda i, j: (0, i * nj + j)),
                  pl.BlockSpec((4, LANE), lambda i, j: (0, i * nj + j)),
                  pl.BlockSpec((1, 2, t, DH), lambda i, j: (i, j, 0, 0))],
        out_specs=[pl.BlockSpec((t, LANE), lambda i, j: (0, i * nj + j)),
                   pl.BlockSpec((4, LANE), lambda i, j: (0, i * nj + j))],
        out_shape=[jax.ShapeDtypeStruct((t, 3 * WIDTH), F32), jax.ShapeDtypeStruct((4, 3 * WIDTH), F32)],
        scratch_shapes=[pltpu.VMEM((t + 2 * HALO, LANE), F32), pltpu.VMEM((t + 2 * HALO, LANE), F32)],
        compiler_params=_params(("arbitrary", "arbitrary")),
    )(pm, conv_w, dqkv)


FOX_COL0 = 4 * WIDTH // LANE


def _fox_prep(pm, nw):
    t = pm.shape[0]
    nj = WIDTH // LANE

    def body(x_ref, w_ref, o_ref):
        kind = pl.program_id(0)
        hs = _head_sum_matrix()
        wk = w_ref[pl.ds(kind, 1), :]

        def tile(r, _):
            xv = x_ref[_tile_rows(r), :]
            ms = _group_sum(xv * xv, hs) * (1.0 / DH)
            out = jnp.where(kind < 2, xv * lax.rsqrt(ms + EPS) * wk, xv)
            o_ref[0, 0, _tile_rows(r), :] = out[:, :DH]
            o_ref[0, 1, _tile_rows(r), :] = out[:, DH:]
            return 0

        lax.fori_loop(0, t // PREP_ROWS, tile, 0)

    return pl.pallas_call(
        body, name="fox_prep", grid=(3, nj),
        in_specs=[pl.BlockSpec((t, LANE), lambda i, j: (0, FOX_COL0 + i * nj + j)),
                  pl.BlockSpec((3, LANE), lambda i, j: (0, 0))],
        out_specs=pl.BlockSpec((1, 2, t, DH), lambda i, j: (i, j, 0, 0)),
        out_shape=jax.ShapeDtypeStruct((3, HEADS, t, DH), F32),
        compiler_params=_params(("arbitrary", "arbitrary")),
    )(pm, nw)


def _fox_prep_bwd(pm, nw, dqkv):
    t = pm.shape[0]
    nj = WIDTH // LANE

    def body(x_ref, w_ref, d_ref, dx_ref, dw_ref):
        kind = pl.program_id(0)
        hs = _head_sum_matrix()
        wk = w_ref[pl.ds(kind, 1), :]

        def tile(r, dw):
            xv = x_ref[_tile_rows(r), :]
            rn = lax.rsqrt(_group_sum(xv * xv, hs) * (1.0 / DH) + EPS)
            xn = xv * rn
            d = jnp.concatenate([d_ref[0, 0, _tile_rows(r), :], d_ref[0, 1, _tile_rows(r), :]], axis=1)
            g = d * wk
            dxn = rn * (g - xn * _group_sum(g * xn, hs) * (1.0 / DH))
            dx_ref[_tile_rows(r), :] = jnp.where(kind < 2, dxn, d)
            return dw + jnp.sum(d * xn, axis=0, keepdims=True)

        dw_ref[0, 0] = lax.fori_loop(0, t // PREP_ROWS, tile, jnp.zeros((1, LANE), F32))

    return pl.pallas_call(
        body, name="fox_prep_bwd", grid=(3, nj),
        in_specs=[pl.BlockSpec((t, LANE), lambda i, j: (0, FOX_COL0 + i * nj + j)),
                  pl.BlockSpec((3, LANE), lambda i, j: (0, 0)),
                  pl.BlockSpec((1, 2, t, DH), lambda i, j: (i, j, 0, 0))],
        out_specs=[pl.BlockSpec((t, LANE), lambda i, j: (0, i * nj + j)),
                   pl.BlockSpec((1, 1, 1, LANE), lambda i, j: (i, j, 0, 0))],
        out_shape=[jax.ShapeDtypeStruct((t, 3 * WIDTH), F32), jax.ShapeDtypeStruct((3, nj, 1, LANE), F32)],
        compiler_params=_params(("arbitrary", "arbitrary")),
    )(pm, nw, dqkv)


SC = 256
CPS = SC // CHUNK
Q_SCALE = DH ** -0.5


def _sc_masks():
    ri = lax.broadcasted_iota(jnp.int32, (SC, SC), 0)
    ci = lax.broadcasted_iota(jnp.int32, (SC, SC), 1)
    same = (ri // CHUNK) == (ci // CHUNK)
    return same & (ri >= ci), same & (ri > ci), ri == ci


def _unit_lower_inverse(m, eye):
    inv = jnp.where(eye, 1.0, 0.0) + m
    for _ in range(5):
        m = _dot(m, m)
        inv = inv + _dot(inv, m)
    return inv


def _lane_col(blk, lane_idx):
    lane = lax.broadcasted_iota(jnp.int32, blk.shape, 1)
    return jnp.sum(jnp.where(lane == lane_idx, blk, 0.0), axis=1, keepdims=True)


def _to_lane(col, lane_idx):
    lane = lax.broadcasted_iota(jnp.int32, (col.shape[0], LANE), 1)
    return jnp.where(lane == lane_idx, col, 0.0)


def _gdn_columns(gates_ref, run_ref, tot_ref, runt_ref, rows, h):
    return (_lane_col(gates_ref[rows, :], h), _lane_col(run_ref[rows, :], HEADS + h),
            _lane_col(tot_ref[rows, :], HEADS + h), runt_ref[pl.ds(h, 1), rows])


def _gdn_local(q, k, beta, gc, gl, grow, causal):
    decay = jnp.exp(jnp.where(causal, gc - grow, -1e30))
    egc = jnp.exp(gc)
    ekd = jnp.exp(gl - gc)
    qs = q * Q_SCALE
    kb = k * beta
    kk = _dot_nt(kb, k)
    qk = _dot_nt(qs, k)
    return beta, gl, decay, egc, ekd, qs, kb, kk, qk, jnp.where(causal, qk * decay, 0.0)


def _chunk_rows(c):
    return pl.ds(pl.multiple_of(c * CHUNK, CHUNK), CHUNK)


def _sc_rows(b):
    return pl.ds(pl.multiple_of(b * SC, SC), SC)


def _gdn_fwd(qkv, gates, run, tot, run_t):
    t = qkv.shape[2]
    nc = t // CHUNK
    nsc = t // SC

    def body(qkv_ref, gates_ref, run_ref, tot_ref, runt_ref, o_ref, st_ref, inv_ref, kc_s, qc_s, g_s, au_s):
        h = pl.program_id(0)
        causal, strict, eye = _sc_masks()

        def local(b, _):
            rows = _sc_rows(b)
            q, k, v = qkv_ref[0, 0, rows, :], qkv_ref[1, 0, rows, :], qkv_ref[2, 0, rows, :]
            beta, _, decay, egc, ekd, qs, kb, kk, _, attn = _gdn_local(
                q, k, *_gdn_columns(gates_ref, run_ref, tot_ref, runt_ref, rows, h), causal)
            inv = _unit_lower_inverse(-jnp.where(strict, kk * decay, 0.0), eye)
            inv_ref[0, rows, :] = inv.astype(inv_ref.dtype)
            u = _dot(inv, v * beta)
            w = _dot(inv, kb * egc)
            g_s[rows, :] = qs * egc - _dot(attn, w)
            au_s[rows, :] = _dot(attn, u)
            kd = k * ekd
            for j in range(CPS):
                sl = slice(j * CHUNK, (j + 1) * CHUNK)
                kc_s[b * CPS + j] = _dot_tn(kd[sl], w[sl])
                qc_s[b * CPS + j] = _dot_tn(kd[sl], u[sl])
            return 0

        lax.fori_loop(0, nsc, local, 0)

        def step(c, s):
            rows = _chunk_rows(c)
            st_ref[0, c] = s
            o_ref[0, rows, :] = _dot(g_s[rows, :], s) + au_s[rows, :]
            egl = jnp.exp(_lane_col(tot_ref[pl.ds(c * CHUNK, 1), :], HEADS + h))
            return egl * s - _dot(kc_s[c], s) + qc_s[c]

        lax.fori_loop(0, nc, step, jnp.zeros((DH, DH), F32))

    whole = pl.BlockSpec((t, LANE), lambda h: (0, 0))
    return pl.pallas_call(
        body, name="gdn_fwd", grid=(HEADS,),
        in_specs=[pl.BlockSpec((3, 1, t, DH), lambda h: (0, h, 0, 0)), whole, whole, whole,
                  pl.BlockSpec((HEADS, t), lambda h: (1, 0))],
        out_specs=[pl.BlockSpec((1, t, DH), lambda h: (h, 0, 0)), pl.BlockSpec((1, nc, DH, DH), lambda h: (h, 0, 0, 0)),
                   pl.BlockSpec((1, t, SC), lambda h: (h, 0, 0))],
        out_shape=[jax.ShapeDtypeStruct((HEADS, t, DH), F32), jax.ShapeDtypeStruct((HEADS, nc, DH, DH), F32),
                   jax.ShapeDtypeStruct((HEADS, t, SC), MXU)],
        scratch_shapes=[pltpu.VMEM((nc, DH, DH), F32), pltpu.VMEM((nc, DH, DH), F32),
                        pltpu.VMEM((t, DH), F32), pltpu.VMEM((t, DH), F32)],
        compiler_params=_params(("arbitrary",)),
    )(qkv, gates, run, tot, run_t)


def _gdn_bwd(qkv, gates, run, tot, run_t, inv, states, do):
    t = qkv.shape[2]
    nc = t // CHUNK
    nsc = t // SC

    def body(qkv_ref, gates_ref, run_ref, tot_ref, runt_ref, inv_ref, st_ref, do_ref,
             dqkv_ref, dcol_ref, dtot_ref, drow_ref, u_s, w_s, kc_s, h_s, dsn_s):
        h = pl.program_id(0)
        causal, strict, _ = _sc_masks()

        @pl.when(h == 0)
        def _():
            dcol_ref[...] = jnp.zeros_like(dcol_ref)
            dtot_ref[...] = jnp.zeros_like(dtot_ref)

        def local(b, _):
            rows = _sc_rows(b)
            q, k, v = qkv_ref[0, 0, rows, :], qkv_ref[1, 0, rows, :], qkv_ref[2, 0, rows, :]
            beta, _, _, egc, ekd, qs, kb, _, _, attn = _gdn_local(
                q, k, *_gdn_columns(gates_ref, run_ref, tot_ref, runt_ref, rows, h), causal)
            inv_b = inv_ref[0, rows, :]
            u = _dot(inv_b, v * beta)
            w = _dot(inv_b, kb * egc)
            u_s[rows, :] = u
            w_s[rows, :] = w
            g = qs * egc - _dot(attn, w)
            kd = k * ekd
            dout = do_ref[0, rows, :]
            for j in range(CPS):
                sl = slice(j * CHUNK, (j + 1) * CHUNK)
                kc_s[b * CPS + j] = _dot_tn(kd[sl], w[sl])
                h_s[b * CPS + j] = _dot_tn(g[sl], dout[sl])
            return 0

        lax.fori_loop(0, nsc, local, 0)

        def step(i, ds):
            c = nc - 1 - i
            dsn_s[c] = ds
            egl = jnp.exp(_lane_col(tot_ref[pl.ds(c * CHUNK, 1), :], HEADS + h))
            return egl * ds - _dot_tn(kc_s[c], ds) + h_s[c]

        lax.fori_loop(0, nc, step, jnp.zeros((DH, DH), F32))

        def back(b, _):
            rows = _sc_rows(b)
            q, k, v = qkv_ref[0, 0, rows, :], qkv_ref[1, 0, rows, :], qkv_ref[2, 0, rows, :]
            beta, gl, decay, egc, ekd, qs, kb, kk, qk, attn = _gdn_local(
                q, k, *_gdn_columns(gates_ref, run_ref, tot_ref, runt_ref, rows, h), causal)
            u, w = u_s[rows, :], w_s[rows, :]
            kd = k * ekd
            vb = v * beta
            kbe = kb * egc
            dout = do_ref[0, rows, :]
            first = lax.broadcasted_iota(jnp.int32, (CHUNK, 1), 0) == 0
            dg_p, dkd_p, dw_p, du_p, dgl_p = [], [], [], [], []
            for j in range(CPS):
                sl = slice(j * CHUNK, (j + 1) * CHUNK)
                s = st_ref[0, b * CPS + j]
                dsn = dsn_s[b * CPS + j]
                dkc = -_dot_nt(dsn, s)
                dg_p.append(_dot_nt(dout[sl], s))
                dkd_p.append(_dot_nt(w[sl], dkc) + _dot_nt(u[sl], dsn))
                dw_p.append(_dot(kd[sl], dkc))
                du_p.append(_dot(kd[sl], dsn))
                degl = jnp.sum(jnp.sum(s * dsn, axis=1, keepdims=True), axis=0, keepdims=True)
                dgl_p.append(jnp.where(first, degl * jnp.exp(gl[j * CHUNK:j * CHUNK + 1, :]), 0.0))
            dg, dkd = jnp.concatenate(dg_p, axis=0), jnp.concatenate(dkd_p, axis=0)
            da = jnp.where(causal, _dot_nt(dout, u) - _dot_nt(dg, w), 0.0)
            at = _dot_tn(attn, jnp.concatenate([dout, dg], axis=1))
            du = at[:, :DH] + jnp.concatenate(du_p, axis=0)
            dw = jnp.concatenate(dw_p, axis=0) - at[:, DH:]
            inv_t = inv_ref[0, rows, :].astype(F32).T
            it = _dot(inv_t, jnp.concatenate([du, dw], axis=1))
            dvb, dkbe = it[:, :DH], it[:, DH:]
            dinv = _dot_nt(du, vb) + _dot_nt(dw, kbe)
            dl = jnp.where(strict, -_dot(_dot(inv_t, dinv), inv_t), 0.0)
            dlogd = (dl * kk + da * qk) * decay
            dkk = dl * decay
            dqk = da * decay
            dkb = _dot(dkk, k) + dkbe * egc
            dqs = _dot(dqk, k) + dg * egc
            dk = (_dot_tn(jnp.concatenate([dkk, dqk], axis=0), jnp.concatenate([kb, qs], axis=0))
                  + dkd * ekd + dkb * beta)
            dkd_kd = jnp.sum(dkd * kd, axis=1, keepdims=True)
            dgc = (jnp.sum(dlogd, axis=1, keepdims=True) + jnp.sum(dg * qs, axis=1, keepdims=True) * egc
                   + jnp.sum(dkbe * kbe, axis=1, keepdims=True) - dkd_kd)
            dgl = dkd_kd + jnp.concatenate(dgl_p, axis=0)
            dbeta = jnp.sum(dkb * k, axis=1, keepdims=True) + jnp.sum(dvb * v, axis=1, keepdims=True)
            dqkv_ref[0, 0, rows, :] = dqs * Q_SCALE
            dqkv_ref[1, 0, rows, :] = dk
            dqkv_ref[2, 0, rows, :] = dvb * beta
            dcol_ref[rows, :] += _to_lane(dbeta, h) + _to_lane(dgc, HEADS + h)
            dtot_ref[rows, :] += _to_lane(dgl, HEADS + h)
            drow_ref[pl.ds(h, 1), rows] = -jnp.sum(dlogd, axis=0, keepdims=True)
            return 0

        lax.fori_loop(0, nsc, back, 0)

    whole = pl.BlockSpec((t, LANE), lambda h: (0, 0))
    rowspec = pl.BlockSpec((HEADS, t), lambda h: (0, 0))
    sq = pltpu.VMEM((nc, DH, DH), F32)
    return pl.pallas_call(
        body, name="gdn_bwd", grid=(HEADS,),
        in_specs=[pl.BlockSpec((3, 1, t, DH), lambda h: (0, h, 0, 0)), whole, whole, whole,
                  pl.BlockSpec((HEADS, t), lambda h: (1, 0)),
                  pl.BlockSpec((1, t, SC), lambda h: (h, 0, 0)), pl.BlockSpec((1, nc, DH, DH), lambda h: (h, 0, 0, 0)),
                  pl.BlockSpec((1, t, DH), lambda h: (h, 0, 0))],
        out_specs=[pl.BlockSpec((3, 1, t, DH), lambda h: (0, h, 0, 0)), whole, whole, rowspec],
        out_shape=[jax.ShapeDtypeStruct((3, HEADS, t, DH), F32), jax.ShapeDtypeStruct((t, LANE), F32),
                   jax.ShapeDtypeStruct((t, LANE), F32), jax.ShapeDtypeStruct((HEADS, t), F32)],
        scratch_shapes=[pltpu.VMEM((t, DH), F32), pltpu.VMEM((t, DH), F32), sq, sq, sq],
        compiler_params=_params(("arbitrary",)),
    )(qkv, gates, run, tot, run_t, inv, states, do)


def _causal_mask(i, j, qb):
    qpos = i * qb + lax.broadcasted_iota(jnp.int32, (qb, qb), 0)
    kpos = j * qb + lax.broadcasted_iota(jnp.int32, (qb, qb), 1)
    return qpos >= kpos


def _fox_fwd(qkv, run, run_t):
    t = qkv.shape[2]
    qb = min(QB, t)
    nq = t // qb
    scale = DH ** -0.5

    def body(q_ref, k_ref, v_ref, run_ref, runt_ref, o_ref, lse_ref):
        h = pl.program_id(0)
        i = pl.program_id(1)
        q = q_ref[0, 0].astype(MXU)
        fq = _lane_col(run_ref[...], 2 * HEADS + h)

        def kstep(j, carry):
            m, l, acc = carry
            rows = pl.ds(pl.multiple_of(j * qb, qb), qb)
            s = _dot_nt(q, k_ref[0, 0, rows, :]) * scale + (fq - runt_ref[pl.ds(h, 1), rows])
            s = jnp.where(_causal_mask(i, j, qb), s, -1e30)
            m_new = jnp.maximum(m, jnp.max(s, axis=1, keepdims=True))
            p = jnp.exp(s - m_new)
            alpha = jnp.exp(m - m_new)
            l = alpha * l + jnp.sum(p, axis=1, keepdims=True)
            acc = alpha * acc + _dot(p, v_ref[0, 0, rows, :])
            return m_new, l, acc

        init = (jnp.full((qb, 1), -1e30, F32), jnp.zeros((qb, 1), F32), jnp.zeros((qb, DH), F32))
        m, l, acc = lax.fori_loop(0, i + 1, kstep, init)
        o_ref[0] = acc / l
        lse_ref[0] = m + jnp.log(l)

    return pl.pallas_call(
        body, name="fox_fwd", grid=(HEADS, nq),
        in_specs=[pl.BlockSpec((1, 1, qb, DH), lambda h, i: (0, h, i, 0)),
                  pl.BlockSpec((1, 1, t, DH), lambda h, i: (1, h, 0, 0)),
                  pl.BlockSpec((1, 1, t, DH), lambda h, i: (2, h, 0, 0)),
                  pl.BlockSpec((qb, LANE), lambda h, i: (i, 0)),
                  pl.BlockSpec((HEADS, t), lambda h, i: (2, 0))],
        out_specs=[pl.BlockSpec((1, qb, DH), lambda h, i: (h, i, 0)), pl.BlockSpec((1, qb, 1), lambda h, i: (h, i, 0))],
        out_shape=[jax.ShapeDtypeStruct((HEADS, t, DH), F32), jax.ShapeDtypeStruct((HEADS, t, 1), F32)],
        compiler_params=_params(("arbitrary", "arbitrary")),
    )(qkv, qkv, qkv, run, run_t)


def _fox_bwd(qkv, run, run_t, o, lse, do):
    t = qkv.shape[2]
    qb = min(QB, t)
    nq = t // qb
    scale = DH ** -0.5

    def body(q_ref, k_ref, v_ref, run_ref, runt_ref, o_ref, lse_ref, do_ref, dqkv_ref, dcol_ref, drow_ref):
        h = pl.program_id(0)
        j = pl.program_id(1)

        @pl.when(j == 0)
        def _():
            dqkv_ref[0] = jnp.zeros((1, t, DH), F32)

        @pl.when((j == 0) & (h == 0))
        def _():
            dcol_ref[...] = jnp.zeros_like(dcol_ref)

        k = k_ref[0, 0].astype(MXU)
        v = v_ref[0, 0].astype(MXU)
        krows = pl.ds(pl.multiple_of(j * qb, qb), qb)
        fk = runt_ref[pl.ds(h, 1), krows]

        def qstep(i, carry):
            dk, dv, dfk = carry
            rows = pl.ds(pl.multiple_of(i * qb, qb), qb)
            q = q_ref[0, 0, rows, :].astype(MXU)
            dout = do_ref[0, rows, :]
            s = _dot_nt(q, k) * scale + (_lane_col(run_ref[rows, :], 2 * HEADS + h) - fk)
            s = jnp.where(_causal_mask(i, j, qb), s, -1e30)
            p = jnp.exp(s - lse_ref[0, rows, :])
            dp = _dot_nt(dout, v)
            drow = jnp.sum(dout * o_ref[0, rows, :], axis=1, keepdims=True)
            ds = p * (dp - drow)
            dqkv_ref[0, 0, rows, :] += _dot(ds, k) * scale
            dcol_ref[rows, :] += _to_lane(jnp.sum(ds, axis=1, keepdims=True), 2 * HEADS + h)
            return (dk + _dot_tn(ds, q) * scale, dv + _dot_tn(p, dout), dfk - jnp.sum(ds, axis=0, keepdims=True))

        init = (jnp.zeros((qb, DH), F32), jnp.zeros((qb, DH), F32), jnp.zeros((1, qb), F32))
        dk, dv, dfk = lax.fori_loop(j, nq, qstep, init)
        dqkv_ref[1, 0, krows, :] = dk
        dqkv_ref[2, 0, krows, :] = dv
        drow_ref[pl.ds(h, 1), krows] = dfk

    full = pl.BlockSpec((1, t, DH), lambda h, j: (h, 0, 0))
    return pl.pallas_call(
        body, name="fox_bwd", grid=(HEADS, nq),
        in_specs=[pl.BlockSpec((1, 1, t, DH), lambda h, j: (0, h, 0, 0)),
                  pl.BlockSpec((1, 1, qb, DH), lambda h, j: (1, h, j, 0)),
                  pl.BlockSpec((1, 1, qb, DH), lambda h, j: (2, h, j, 0)),
                  pl.BlockSpec((t, LANE), lambda h, j: (0, 0)), pl.BlockSpec((HEADS, t), lambda h, j: (2, 0)),
                  full, pl.BlockSpec((1, t, 1), lambda h, j: (h, 0, 0)), full],
        out_specs=[pl.BlockSpec((3, 1, t, DH), lambda h, j: (0, h, 0, 0)),
                   pl.BlockSpec((t, LANE), lambda h, j: (0, 0)), pl.BlockSpec((HEADS, t), lambda h, j: (0, 0))],
        out_shape=[jax.ShapeDtypeStruct((3, HEADS, t, DH), F32), jax.ShapeDtypeStruct((t, LANE), F32),
                   jax.ShapeDtypeStruct((HEADS, t), F32)],
        compiler_params=_params(("arbitrary", "arbitrary")),
    )(qkv, qkv, qkv, run, run_t, o, lse, do)


Z_COL0 = 3 * WIDTH // LANE
FGATE_COL0 = 7 * WIDTH // LANE


def _gdn_post(o, pm, onw):
    t = pm.shape[0]

    def body(o_ref, z_ref, w_ref, m_ref):
        z = z_ref[...]
        sz = z * _sigmoid(z)
        for hh in range(2):
            ov = o_ref[hh]
            n = ov * lax.rsqrt(jnp.mean(ov * ov, axis=-1, keepdims=True) + EPS) * w_ref[...]
            m_ref[:, hh * DH:(hh + 1) * DH] = (n * sz[:, hh * DH:(hh + 1) * DH]).astype(m_ref.dtype)

    return pl.pallas_call(
        body, name="gdn_post", grid=(WIDTH // LANE,),
        in_specs=[pl.BlockSpec((2, t, DH), lambda j: (j, 0, 0)), pl.BlockSpec((t, LANE), lambda j: (0, Z_COL0 + j)),
                  pl.BlockSpec((1, DH), lambda j: (0, 0))],
        out_specs=pl.BlockSpec((t, LANE), lambda j: (0, j)),
        out_shape=jax.ShapeDtypeStruct((t, WIDTH), MXU),
        compiler_params=_params(("arbitrary",)),
    )(o, pm, onw)


def _gdn_post_bwd(o, pm, onw, dmix):
    t = pm.shape[0]

    def body(o_ref, z_ref, w_ref, dm_ref, do_ref, dz_ref, dw_ref):
        @pl.when(pl.program_id(0) == 0)
        def _():
            dw_ref[...] = jnp.zeros_like(dw_ref)

        z = z_ref[...]
        sg = _sigmoid(z)
        sz = z * sg
        dsz = sg * (1.0 + z * (1.0 - sg))
        dm = dm_ref[...]
        for hh in range(2):
            cols = slice(hh * DH, (hh + 1) * DH)
            ov = o_ref[hh]
            r = lax.rsqrt(jnp.mean(ov * ov, axis=-1, keepdims=True) + EPS)
            xn = ov * r
            dmh = dm[:, cols]
            dn = dmh * sz[:, cols]
            dz_ref[:, cols] = dmh * (xn * w_ref[...]) * dsz[:, cols]
            dw_ref[...] += jnp.sum(dn * xn, axis=0, keepdims=True)
            g = dn * w_ref[...]
            do_ref[hh] = r * (g - xn * jnp.mean(g * xn, axis=-1, keepdims=True))

    return pl.pallas_call(
        body, name="gdn_post_bwd", grid=(WIDTH // LANE,),
        in_specs=[pl.BlockSpec((2, t, DH), lambda j: (j, 0, 0)), pl.BlockSpec((t, LANE), lambda j: (0, Z_COL0 + j)),
                  pl.BlockSpec((1, DH), lambda j: (0, 0)), pl.BlockSpec((t, LANE), lambda j: (0, j))],
        out_specs=[pl.BlockSpec((2, t, DH), lambda j: (j, 0, 0)), pl.BlockSpec((t, LANE), lambda j: (0, j)),
                   pl.BlockSpec((1, DH), lambda j: (0, 0))],
        out_shape=[jax.ShapeDtypeStruct((HEADS, t, DH), F32), jax.ShapeDtypeStruct((t, WIDTH), F32),
                   jax.ShapeDtypeStruct((1, DH), F32)],
        compiler_params=_params(("arbitrary",)),
    )(o, pm, onw, dmix)


def _fox_post(o, pm):
    t = pm.shape[0]

    def body(o_ref, g_ref, m_ref):
        sg = _sigmoid(g_ref[...])
        for hh in range(2):
            cols = slice(hh * DH, (hh + 1) * DH)
            m_ref[:, cols] = (o_ref[hh] * sg[:, cols]).astype(m_ref.dtype)

    return pl.pallas_call(
        body, name="fox_post", grid=(WIDTH // LANE,),
        in_specs=[pl.BlockSpec((2, t, DH), lambda j: (j, 0, 0)), pl.BlockSpec((t, LANE), lambda j: (0, FGATE_COL0 + j))],
        out_specs=pl.BlockSpec((t, LANE), lambda j: (0, j)),
        out_shape=jax.ShapeDtypeStruct((t, WIDTH), MXU),
        compiler_params=_params(("arbitrary",)),
    )(o, pm)


def _fox_post_bwd(o, pm, dmix):
    t = pm.shape[0]

    def body(o_ref, g_ref, dm_ref, do_ref, dg_ref):
        sg = _sigmoid(g_ref[...])
        dm = dm_ref[...]
        for hh in range(2):
            cols = slice(hh * DH, (hh + 1) * DH)
            do_ref[hh] = dm[:, cols] * sg[:, cols]
            dg_ref[:, cols] = dm[:, cols] * o_ref[hh] * (sg * (1.0 - sg))[:, cols]

    return pl.pallas_call(
        body, name="fox_post_bwd", grid=(WIDTH // LANE,),
        in_specs=[pl.BlockSpec((2, t, DH), lambda j: (j, 0, 0)), pl.BlockSpec((t, LANE), lambda j: (0, FGATE_COL0 + j)),
                  pl.BlockSpec((t, LANE), lambda j: (0, j))],
        out_specs=[pl.BlockSpec((2, t, DH), lambda j: (j, 0, 0)), pl.BlockSpec((t, LANE), lambda j: (0, j))],
        out_shape=[jax.ShapeDtypeStruct((HEADS, t, DH), F32), jax.ShapeDtypeStruct((t, WIDTH), F32)],
        compiler_params=_params(("arbitrary",)),
    )(o, pm, dmix)


def _tail(x, mixg, mixf, tgt, wo, n2w, wg, wu, wd, fw):
    t, d = x.shape
    dff = wg.shape[1]
    tb = min(TB, t)

    def body(x_ref, mg_ref, mf_ref, t_ref, wo_ref, n2_ref, wg_ref, wu_ref, wd_ref, fw_ref,
             h2_ref, act_ref, dgate_ref, dup_ref, dx3_ref, dx2_ref, dmg_ref, dmf_ref, dn2_ref, dfw_ref, loss_ref):
        @pl.when(pl.program_id(0) == 0)
        def _():
            dn2_ref[...] = jnp.zeros_like(dn2_ref)
            dfw_ref[...] = jnp.zeros_like(dfw_ref)
            loss_ref[...] = jnp.zeros_like(loss_ref)

        x2 = x_ref[...] + _dot(mg_ref[...], wo_ref[0:WIDTH, :]) + _dot(mf_ref[...], wo_ref[WIDTH:2 * WIDTH, :])
        r2 = lax.rsqrt(jnp.mean(x2 * x2, axis=-1, keepdims=True) + EPS)
        xn2 = x2 * r2
        h2 = (xn2 * n2_ref[...]).astype(MXU)
        h2_ref[...] = h2
        gate = _dot(h2, wg_ref[...])
        up = _dot(h2, wu_ref[...])
        sg = _sigmoid(gate)
        sl = gate * sg
        act = (sl * up).astype(MXU)
        act_ref[...] = act
        x3 = x2 + _dot(act, wd_ref[...])
        r3 = lax.rsqrt(jnp.mean(x3 * x3, axis=-1, keepdims=True) + EPS)
        xn3 = x3 * r3
        err = xn3 * fw_ref[...] - t_ref[...]
        loss_ref[...] += 0.5 * jnp.sum(jnp.mean(err * err, axis=-1, keepdims=True), axis=0, keepdims=True)
        dy = err * (1.0 / d)
        dfw_ref[...] += jnp.sum(dy * xn3, axis=0, keepdims=True)
        g3 = dy * fw_ref[...]
        dx3 = r3 * (g3 - xn3 * jnp.mean(g3 * xn3, axis=-1, keepdims=True))
        dx3_ref[...] = dx3.astype(MXU)
        dact = _dot_nt(dx3, wd_ref[...])
        dgate = (dact * up * (sg * (1.0 + gate * (1.0 - sg)))).astype(MXU)
        dup = (dact * sl).astype(MXU)
        dgate_ref[...] = dgate
        dup_ref[...] = dup
        dh2 = _dot_nt(dgate, wg_ref[...]) + _dot_nt(dup, wu_ref[...])
        dn2_ref[...] += jnp.sum(dh2 * xn2, axis=0, keepdims=True)
        g2 = dh2 * n2_ref[...]
        dx2 = dx3 + r2 * (g2 - xn2 * jnp.mean(g2 * xn2, axis=-1, keepdims=True))
        dx2_ref[...] = dx2
        dmg_ref[...] = _dot_nt(dx2, wo_ref[0:WIDTH, :])
        dmf_ref[...] = _dot_nt(dx2, wo_ref[WIDTH:2 * WIDTH, :])

    def tok(n):
        return pl.BlockSpec((tb, n), lambda i: (i, 0))

    acc = pl.BlockSpec((1, d), lambda i: (0, 0))
    sds = jax.ShapeDtypeStruct
    return pl.pallas_call(
        body, name="tail", grid=(t // tb,),
        in_specs=[tok(d), tok(WIDTH), tok(WIDTH), tok(d), _resident(wo.shape), _resident((1, d)),
                  _resident(wg.shape), _resident(wu.shape), _resident(wd.shape), _resident((1, d))],
        out_specs=[tok(d), tok(dff), tok(dff), tok(dff), tok(d), tok(d), tok(WIDTH), tok(WIDTH), acc, acc,
                   pl.BlockSpec((1, 1), lambda i: (0, 0))],
        out_shape=[sds((t, d), MXU), sds((t, dff), MXU), sds((t, dff), MXU), sds((t, dff), MXU), sds((t, d), MXU),
                   sds((t, d), F32), sds((t, WIDTH), F32), sds((t, WIDTH), F32), sds((1, d), F32), sds((1, d), F32),
                   sds((1, 1), F32)],
        compiler_params=_params(("arbitrary",)),
    )(x, mixg, mixf, tgt, wo, n2w, wg, wu, wd, fw)


def _wgrad(a, b, name):
    t, m = a.shape
    n = b.shape[1]
    bn = 256 if n % 256 == 0 else LANE

    def body(a_ref, b_ref, o_ref):
        o_ref[...] = _dot_tn(a_ref[...], b_ref[...]).astype(o_ref.dtype)

    return pl.pallas_call(
        body, name=name, grid=(n // bn,),
        in_specs=[_resident((t, m)), pl.BlockSpec((t, bn), lambda j: (0, j))],
        out_specs=pl.BlockSpec((m, bn), lambda j: (0, j)),
        out_shape=jax.ShapeDtypeStruct((m, n), WIRE),
        compiler_params=_params(("arbitrary",)),
    )(a, b)


def _split_w_in(w_in):
    a = 4 * WIDTH
    b = a + 2 * HEADS
    c = b + 4 * WIDTH
    main = jnp.concatenate([w_in[:, :a], w_in[:, b:c]], axis=1)
    small = jnp.concatenate([w_in[:, a:b], w_in[:, c:], jnp.zeros((w_in.shape[0], LANE - 3 * HEADS), w_in.dtype)], axis=1)
    return main, small


def _merge_dw_in(d_gdn, d_z, d_fox, d_fg, d_small):
    return jnp.concatenate([d_gdn, d_z, d_small[:, :2 * HEADS], d_fox, d_fg, d_small[:, 2 * HEADS:3 * HEADS]], axis=1)


def _lanes(*pieces):
    v = jnp.concatenate([p.reshape(-1).astype(F32) for p in pieces])
    return jnp.pad(v, (0, LANE - v.shape[0])).reshape(1, LANE)


def _vector_params(p):
    d = p["norm1_w"].size
    gparams = jnp.concatenate([_lanes(jnp.zeros(HEADS), p["gdn_dt_bias"], p["fox_f_bias"]),
                               _lanes(jnp.zeros(HEADS), p["gdn_A_log"]), jnp.zeros((6, LANE), F32)])
    fox_nw = jnp.stack([jnp.tile(p["fox_q_norm_w"].reshape(-1), 2), jnp.tile(p["fox_k_norm_w"].reshape(-1), 2),
                        jnp.ones((LANE,), F32)])
    return dict(n1w=p["norm1_w"].reshape(1, d), n2w=p["norm2_w"].reshape(1, d), fw=p["final_norm_w"].reshape(1, d),
                onw=p["gdn_out_norm_w"].reshape(1, DH), gparams=gparams, fox_nw=fox_nw)


def _mixer_forward(x, vp, w_main, w_small, conv_w):
    h1, pm, ps = _inproj(x, vp["n1w"], w_main, w_small)
    gates, run, tot, run_t = _gates(ps, vp["gparams"])
    gqkv = _gdn_prep(pm, conv_w)
    o_gdn, states, inv = _gdn_fwd(gqkv, gates, run, tot, run_t)
    mixg = _gdn_post(o_gdn, pm, vp["onw"])
    fqkv = _fox_prep(pm, vp["fox_nw"])
    o_fox, lse = _fox_fwd(fqkv, run, run_t)
    mixf = _fox_post(o_fox, pm)
    return dict(h1=h1, pm=pm, ps=ps, gates=gates, run=run, tot=tot, run_t=run_t, gqkv=gqkv, o_gdn=o_gdn, states=states,
                inv=inv, mixg=mixg, fqkv=fqkv, o_fox=o_fox, lse=lse, mixf=mixf)


def _mixer_backward(x, vp, w_main, w_small, conv_w, f, dx2, dmixg, dmixf, onw):
    pm = f["pm"]
    do_gdn, dz, donw = _gdn_post_bwd(f["o_gdn"], pm, onw, dmixg)
    dgqkv, dcol_g, dtot_g, drow_g = _gdn_bwd(f["gqkv"], f["gates"], f["run"], f["tot"], f["run_t"], f["inv"], f["states"], do_gdn)
    dgdn, dconv = _gdn_prep_bwd(pm, conv_w, dgqkv)
    do_fox, dfg = _fox_post_bwd(f["o_fox"], pm, dmixf)
    dfqkv, dcol_f, drow_f = _fox_bwd(f["fqkv"], f["run"], f["run_t"], f["o_fox"], f["lse"], do_fox)
    dfox, dfnw = _fox_prep_bwd(pm, vp["fox_nw"], dfqkv)
    dps, gsum = _gates_bwd(f["ps"], vp["gparams"], dcol_g, dtot_g, dcol_f, drow_g, drow_f)
    grad_x, dn1w = _inproj_bwd(x, vp["n1w"], dx2, dgdn, dz, dfox, dfg, dps, w_main, w_small)
    h1 = f["h1"]
    dw_in = _merge_dw_in(_wgrad(h1, dgdn, "dw_in_gdn"), _wgrad(h1, dz, "dw_in_z"), _wgrad(h1, dfox, "dw_in_fox"),
                         _wgrad(h1, dfg, "dw_in_fgate"), _wgrad(h1, dps, "dw_in_small"))
    return grad_x, dw_in, dconv, dict(dn1w=dn1w, gsum=gsum, donw=donw, dfnw=dfnw)


VECTORS = ("norm1_w", "norm2_w", "final_norm_w", "gdn_A_log", "gdn_dt_bias", "gdn_out_norm_w", "fox_f_bias",
           "fox_q_norm_w", "fox_k_norm_w")
VEC_ROWS = 16
LOSS_ROW = len(VECTORS)


def _pack_vectors(dn1w, dn2w, dfw, gsum, donw, dfnw, loss):
    d = dn1w.shape[1]

    def body(n1_ref, n2_ref, fw_ref, gs_ref, on_ref, fn_ref, loss_ref, o_ref):
        o_ref[...] = jnp.zeros_like(o_ref)
        o_ref[0:1, :] = n1_ref[...]
        o_ref[1:2, :] = n2_ref[...]
        o_ref[2:3, :] = fw_ref[...]
        o_ref[3:4, 0:HEADS] = gs_ref[0:1, 0:HEADS]
        o_ref[4:5, 0:HEADS] = gs_ref[1:2, 0:HEADS]
        o_ref[5:6, 0:DH] = on_ref[...]
        o_ref[6:7, 0:HEADS] = gs_ref[2:3, 0:HEADS]
        for kind in range(2):
            v = fn_ref[kind, 0]
            for j in range(1, fn_ref.shape[1]):
                v = v + fn_ref[kind, j]
            o_ref[7 + kind:8 + kind, 0:DH] = v[:, :DH] + v[:, DH:]
        o_ref[LOSS_ROW:LOSS_ROW + 1, 0:1] = loss_ref[...]

    return pl.pallas_call(body, name="pack_vectors", out_shape=jax.ShapeDtypeStruct((VEC_ROWS, d), F32),
                          compiler_params=_params())(dn1w, dn2w, dfw, gsum, donw, dfnw, loss)


def _local_step(x, tgt, p, w_in, conv_w, wo, wg, wu, wd):
    vp = _vector_params(p)
    w_main, w_small = _split_w_in(w_in)
    f = _mixer_forward(x, vp, w_main, w_small, conv_w)
    (h2, act, dgate, dup, dx3, dx2, dmixg, dmixf, dn2w, dfw, loss) = _tail(
        x, f["mixg"], f["mixf"], tgt, wo, vp["n2w"], wg, wu, wd, vp["fw"])
    grad_x, dw_in, dconv, small = _mixer_backward(x, vp, w_main, w_small, conv_w, f, dx2, dmixg, dmixf, vp["onw"])
    grads = {"w_in": dw_in, "gdn_conv_w": dconv,
             "w_out": jnp.concatenate([_wgrad(f["mixg"], dx2, "dw_out_gdn"), _wgrad(f["mixf"], dx2, "dw_out_fox")], axis=0),
             "w_ffn_gate": _wgrad(h2, dgate, "dw_gate"), "w_ffn_up": _wgrad(h2, dup, "dw_up"),
             "w_ffn_down": _wgrad(act, dx3, "dw_down")}
    vec = _pack_vectors(small["dn1w"], dn2w, dfw, small["gsum"], small["donw"], small["dfnw"], loss)
    return grad_x, grads, vec


def _my_place():
    return lax.axis_index("x"), lax.axis_index("y"), lax.axis_index("c")


def _peers():
    x, y, c = _my_place()
    peers = []
    for k in range(1, N_DEV):
        px = 1 - x if k & 4 else x
        py = 1 - y if k & 2 else y
        pc = 1 - c if k & 1 else c
        peers.append(((px, py, pc), 4 * px + 2 * py + pc))
    return 4 * x + 2 * y + c, peers


def _spread_copies(kind, srcs, lands, send_sems, recv_sems):
    me, peers = _peers()
    kinds = [kind] * len(srcs) if isinstance(kind, str) else kind
    remote, local = [], []
    for i, (kd, src, land) in enumerate(zip(kinds, srcs, lands)):
        for k, (dev, idx) in enumerate(peers):
            remote.append(pltpu.make_async_remote_copy(
                src_ref=src if kd == "gather" else src.at[idx], dst_ref=land.at[me],
                send_sem=send_sems.at[i * (N_DEV - 1) + k], recv_sem=recv_sems.at[i * (N_DEV - 1) + k],
                device_id=dev, device_id_type=MESH))
        local.append((src if kd == "gather" else src.at[me], land.at[me]))
    return remote, local


def _land_shape(kind, a):
    return (N_DEV,) + a.shape if kind == "gather" else a.shape


HBM = pl.BlockSpec(memory_space=pltpu.HBM)
SEM = pl.BlockSpec(memory_space=pltpu.SEMAPHORE)


def _hbm(a):
    return pltpu.with_memory_space_constraint(a, pltpu.HBM)


def _spread_start(groups, kind, name):
    flat = [a for g in groups for a in g]
    n = len(flat)
    ng = len(groups)

    def body(*refs):
        srcs, lands = refs[:n], refs[n:2 * n]
        sems = refs[2 * n:2 * n + 2 * ng]
        token = refs[4 * n + 2 * ng]
        local_sems = refs[4 * n + 2 * ng + 1]
        off = 0
        remotes, locals_ = [], []
        for g in range(ng):
            m = len(groups[g])
            remote, local = _spread_copies(kind, srcs[off:off + m], lands[off:off + m], sems[2 * g], sems[2 * g + 1])
            remotes += remote
            locals_ += local
            off += m
        for cp in remotes:
            cp.start()
        copies = [pltpu.make_async_copy(s, d, local_sems.at[i]) for i, (s, d) in enumerate(locals_)]
        for cp in copies:
            cp.start()
        for cp in copies:
            cp.wait()
        token[...] = jnp.zeros_like(token)

    sem_shapes = []
    for g in groups:
        sem_shapes += [pltpu.SemaphoreType.DMA((len(g) * (N_DEV - 1),))] * 2
    lands = [lax.empty(_land_shape(kind, a), a.dtype) for a in flat]
    out = pl.pallas_call(
        body, name=name,
        out_shape=tuple(sem_shapes) + tuple(pltpu.HBM(a.shape, a.dtype) for a in flat + lands)
        + (jax.ShapeDtypeStruct((8, LANE), F32),),
        in_specs=[HBM] * (2 * n), out_specs=tuple([SEM] * (2 * ng) + [HBM] * (2 * n) + [pl.BlockSpec(memory_space=pltpu.VMEM)]),
        input_output_aliases={j: 2 * ng + j for j in range(2 * n)},
        scratch_shapes=[pltpu.SemaphoreType.DMA((n,))],
        compiler_params=pltpu.CompilerParams(has_side_effects=pltpu.SideEffectType.DATAFLOW_SIDE_EFFECTING),
    )(*[_hbm(a) for a in flat], *[_hbm(a) for a in lands])
    states, off = [], 0
    for g in range(ng):
        m = len(groups[g])
        states.append((list(out[2 * ng + off:2 * ng + off + m]), list(out[2 * ng + n + off:2 * ng + n + off + m]),
                       out[2 * g], out[2 * g + 1]))
        off += m
    return states, out[-1]


def _spread_wait(state, kind, after, name):
    srcs, lands, send_sems, recv_sems = state
    n = len(srcs)

    def body(*refs):
        remote, _ = _spread_copies(kind, refs[:n], refs[n:2 * n], refs[2 * n], refs[2 * n + 1])
        for cp in remote:
            cp.wait_send()
        for cp in remote:
            cp.wait_recv()

    out = pl.pallas_call(
        body, name=name,
        out_shape=tuple(pltpu.HBM(a.shape, a.dtype) for a in srcs + lands),
        in_specs=[HBM] * (2 * n) + [SEM, SEM, pl.BlockSpec(memory_space=pl.ANY)], out_specs=tuple([HBM] * (2 * n)),
        input_output_aliases={j: j for j in range(2 * n)},
        compiler_params=pltpu.CompilerParams(has_side_effects=pltpu.SideEffectType.DATAFLOW_SIDE_EFFECTING),
    )(*srcs, *lands, send_sems, recv_sems, after)
    return list(out[n:])


def _exchange(scatter, gather, name):
    kinds = ["scatter"] * len(scatter) + ["gather"] * len(gather)
    arrays = list(scatter) + list(gather)
    n = len(arrays)

    def body(*refs):
        send_sems, recv_sems, local_sems = refs[2 * n:]
        remote, local = _spread_copies(kinds, refs[:n], refs[n:2 * n], send_sems, recv_sems)
        copies = [pltpu.make_async_copy(s, d, local_sems.at[i]) for i, (s, d) in enumerate(local)]
        for cp in remote + copies:
            cp.start()
        for cp in remote:
            cp.wait_recv()
        for cp in remote:
            cp.wait_send()
        for cp in copies:
            cp.wait()

    return pl.pallas_call(
        body, name=name,
        out_shape=[jax.ShapeDtypeStruct(_land_shape(k, a), a.dtype) for k, a in zip(kinds, arrays)],
        in_specs=[pl.BlockSpec(memory_space=pl.ANY)] * n, out_specs=[pl.BlockSpec(memory_space=pl.ANY)] * n,
        scratch_shapes=[pltpu.SemaphoreType.DMA((n * (N_DEV - 1),)), pltpu.SemaphoreType.DMA((n * (N_DEV - 1),)),
                        pltpu.SemaphoreType.DMA((n,))],
    )(*arrays)


ADAM_ROWS = 128


def _adam_math(g, w, m, v):
    nm = ADAM_B1 * m + (1.0 - ADAM_B1) * g
    nv = ADAM_B2 * v + (1.0 - ADAM_B2) * (g * g)
    m_hat = nm / (1.0 - ADAM_B1 ** ADAM_STEP)
    v_hat = nv / (1.0 - ADAM_B2 ** ADAM_STEP)
    return -ADAM_LR * (m_hat / (jnp.sqrt(v_hat) + ADAM_EPS) + ADAM_WD * w), nm, nv


def _sum_parts(p_ref):
    g = p_ref[0].astype(F32)
    for s in range(1, N_DEV):
        g = g + p_ref[s].astype(F32)
    return g


def _adam_matrix(parts, w, m, v, name):
    _, r, c = w.shape
    rb = ADAM_ROWS if r % ADAM_ROWS == 0 else r

    def body(p_ref, w_ref, m_ref, v_ref, g_ref, d_ref, nm_ref, nv_ref):
        g = _sum_parts(p_ref)
        g_ref[0] = g
        d_ref[0], nm_ref[0], nv_ref[0] = _adam_math(g, w_ref[0], m_ref[0], v_ref[0])

    blk = pl.BlockSpec((1, rb, c), lambda i: (0, i, 0))
    return pl.pallas_call(
        body, name=name, grid=(r // rb,),
        in_specs=[pl.BlockSpec((N_DEV, rb, c), lambda i: (0, i, 0)), blk, blk, blk],
        out_specs=[blk] * 4, out_shape=[jax.ShapeDtypeStruct(w.shape, F32)] * 4,
        compiler_params=_params(("arbitrary",)),
    )(parts, w, m, v)


def _adam_vectors(parts, ws, ms, vs):
    nv = len(ws)

    def body(*refs):
        p_ref = refs[0]
        w_refs, m_refs, v_refs = refs[1:1 + nv], refs[1 + nv:1 + 2 * nv], refs[1 + 2 * nv:1 + 3 * nv]
        outs = refs[1 + 3 * nv:]
        g_all = _sum_parts(p_ref)
        for i in range(nv):
            n = w_refs[i].shape[1]
            g = g_all[i:i + 1, 0:n]
            d, nm, nvv = _adam_math(g, w_refs[i][...], m_refs[i][...], v_refs[i][...])
            outs[i][...] = g
            outs[nv + i][...] = d
            outs[2 * nv + i][...] = nm
            outs[3 * nv + i][...] = nvv
        outs[4 * nv][...] = g_all[LOSS_ROW:LOSS_ROW + 1, 0:1]

    shapes = [jax.ShapeDtypeStruct(a.shape, F32) for a in ws]
    out = pl.pallas_call(body, name="adam_vectors", out_shape=shapes * 4 + [jax.ShapeDtypeStruct((1, 1), F32)],
                         compiler_params=_params())(parts, *ws, *ms, *vs)
    return out[:nv], out[nv:2 * nv], out[2 * nv:3 * nv], out[3 * nv:4 * nv], out[4 * nv]


MATRICES = (("w_in", 1), ("gdn_conv_w", 1), ("w_out", 0), ("w_ffn_gate", 1), ("w_ffn_up", 1), ("w_ffn_down", 0))
WEIGHTS = ("norm1_w", "w_in", "gdn_conv_w", "gdn_A_log", "gdn_dt_bias", "gdn_out_norm_w", "fox_f_bias", "fox_q_norm_w",
           "fox_k_norm_w", "w_out", "norm2_w", "w_ffn_gate", "w_ffn_up", "w_ffn_down", "final_norm_w")


def _join(blocks, axis):
    _, r, c = blocks.shape
    if axis == 0:
        return blocks.reshape(N_DEV * r, c)
    return blocks.transpose(1, 0, 2).reshape(r, N_DEV * c)


def _cut(full, axis):
    r, c = full.shape
    if axis == 0:
        return full.reshape(N_DEV, r // N_DEV, c)
    return full.reshape(r, N_DEV, c // N_DEV).transpose(1, 0, 2)


def kernel(x, norm1_w, w_in, gdn_conv_w, gdn_A_log, gdn_dt_bias, gdn_out_norm_w, fox_f_bias, fox_q_norm_w, fox_k_norm_w, w_out, norm2_w, w_ffn_gate, w_ffn_up, w_ffn_down, final_norm_w, loss_target, m_norm1_w, m_w_in, m_gdn_conv_w, m_gdn_A_log, m_gdn_dt_bias, m_gdn_out_norm_w, m_fox_f_bias, m_fox_q_norm_w, m_fox_k_norm_w, m_w_out, m_norm2_w, m_w_ffn_gate, m_w_ffn_up, m_w_ffn_down, m_final_norm_w, v_norm1_w, v_w_in, v_gdn_conv_w, v_gdn_A_log, v_gdn_dt_bias, v_gdn_out_norm_w, v_fox_f_bias, v_fox_q_norm_w, v_fox_k_norm_w, v_w_out, v_norm2_w, v_w_ffn_gate, v_w_ffn_up, v_w_ffn_down, v_final_norm_w):
    w = dict(norm1_w=norm1_w, w_in=w_in, gdn_conv_w=gdn_conv_w, gdn_A_log=gdn_A_log, gdn_dt_bias=gdn_dt_bias,
             gdn_out_norm_w=gdn_out_norm_w, fox_f_bias=fox_f_bias, fox_q_norm_w=fox_q_norm_w, fox_k_norm_w=fox_k_norm_w,
             w_out=w_out, norm2_w=norm2_w, w_ffn_gate=w_ffn_gate, w_ffn_up=w_ffn_up, w_ffn_down=w_ffn_down,
             final_norm_w=final_norm_w)
    m = dict(norm1_w=m_norm1_w, w_in=m_w_in, gdn_conv_w=m_gdn_conv_w, gdn_A_log=m_gdn_A_log, gdn_dt_bias=m_gdn_dt_bias,
             gdn_out_norm_w=m_gdn_out_norm_w, fox_f_bias=m_fox_f_bias, fox_q_norm_w=m_fox_q_norm_w,
             fox_k_norm_w=m_fox_k_norm_w, w_out=m_w_out, norm2_w=m_norm2_w, w_ffn_gate=m_w_ffn_gate,
             w_ffn_up=m_w_ffn_up, w_ffn_down=m_w_ffn_down, final_norm_w=m_final_norm_w)
    v = dict(norm1_w=v_norm1_w, w_in=v_w_in, gdn_conv_w=v_gdn_conv_w, gdn_A_log=v_gdn_A_log, gdn_dt_bias=v_gdn_dt_bias,
             gdn_out_norm_w=v_gdn_out_norm_w, fox_f_bias=v_fox_f_bias, fox_q_norm_w=v_fox_q_norm_w,
             fox_k_norm_w=v_fox_k_norm_w, w_out=v_w_out, norm2_w=v_norm2_w, w_ffn_gate=v_w_ffn_gate,
             w_ffn_up=v_w_ffn_up, w_ffn_down=v_w_ffn_down, final_norm_w=v_final_norm_w)
    axis_of = dict(MATRICES)
    late = ("w_out", "w_ffn_gate", "w_ffn_up", "w_ffn_down")
    xs, tgt = x[0], loss_target[0]
    vp = _vector_params({n: w[n] for n in VECTORS})

    (st_in, st_late), token = _spread_start(
        [[w["w_in"][0].astype(WIRE), w["gdn_conv_w"][0]], [w[n][0].astype(WIRE) for n in late]], "gather", "gather_start")
    w_in_blocks, conv_blocks = _spread_wait(st_in, "gather", token, "gather_wait_in")
    w_main, w_small = _split_w_in(_join(w_in_blocks, 1))
    conv_w = _join(conv_blocks, 1)
    f = _mixer_forward(xs, vp, w_main, w_small, conv_w)
    full = {n: _join(b, axis_of[n]) for n, b in zip(late, _spread_wait(st_late, "gather", f["mixf"], "gather_wait_late"))}

    (h2, act, dgate, dup, dx3, dx2, dmixg, dmixf, dn2w, dfw, loss) = _tail(
        xs, f["mixg"], f["mixf"], tgt, full["w_out"], vp["n2w"], full["w_ffn_gate"], full["w_ffn_up"],
        full["w_ffn_down"], vp["fw"])
    dlate = {"w_out": jnp.concatenate([_wgrad(f["mixg"], dx2, "dw_out_gdn"), _wgrad(f["mixf"], dx2, "dw_out_fox")], axis=0),
             "w_ffn_gate": _wgrad(h2, dgate, "dw_gate"), "w_ffn_up": _wgrad(h2, dup, "dw_up"),
             "w_ffn_down": _wgrad(act, dx3, "dw_down")}

    (st_grads,), token = _spread_start([[_cut(dlate[n], axis_of[n]) for n in late]], "scatter", "grads_start")
    grad_x, dw_in, dconv, small = _mixer_backward(xs, vp, w_main, w_small, conv_w, f, dx2, dmixg, dmixf,
                                                  vp["onw"] + token[0:1, 0:DH])
    vec = _pack_vectors(small["dn1w"], dn2w, dfw, small["gsum"], small["donw"], small["dfnw"], loss)
    parts_in, parts_conv, parts_vec = _exchange([_cut(dw_in, 1), _cut(dconv, 1)], [vec], "exchange_last")
    parts = dict(zip(late, _spread_wait(st_grads, "scatter", parts_vec, "grads_wait")))
    parts["w_in"], parts["gdn_conv_w"] = parts_in, parts_conv

    results = [{}, {}, {}, {}]
    for n, _ in MATRICES:
        for d, a in zip(results, _adam_matrix(parts[n], w[n], m[n], v[n], "adam_" + n)):
            d[n] = a
    row = lambda a: a.reshape(1, -1)
    *vec_out, total_loss = _adam_vectors(parts_vec, [row(w[n]) for n in VECTORS], [row(m[n]) for n in VECTORS],
                                         [row(v[n]) for n in VECTORS])
    for d, arrs in zip(results, vec_out):
        for n, a in zip(VECTORS, arrs):
            d[n] = a.reshape(w[n].shape)
    return (total_loss[0, 0], grad_x[None], *[d[n] for d in results for n in WEIGHTS])
```

```python
import functools

import jax
import jax.numpy as jnp
from jax import lax
from jax.experimental import pallas as pl
from jax.experimental.pallas import tpu as pltpu

F32 = jnp.float32
MXU = jnp.bfloat16
WIRE = jnp.bfloat16
HI = lax.Precision.HIGHEST
EPS = 1e-6

N_DEV = 8
HEADS = 8
DH = 64
WIDTH = HEADS * DH
CHUNK = 64
LANE = 128
ROW_ALIGN = 16
TB = 256
QB = 256
VMEM_LIMIT = 60 * 1024 * 1024

ADAM_LR = 0.001
ADAM_B1 = 0.9
ADAM_B2 = 0.999
ADAM_EPS = 1e-08
ADAM_WD = 0.01
ADAM_STEP = 10

MESH = pl.DeviceIdType.MESH


def _params(sem=None):
    return pltpu.CompilerParams(dimension_semantics=sem, vmem_limit_bytes=VMEM_LIMIT)


def _resident(shape):
    n = len(shape)
    return pl.BlockSpec(shape, lambda *_: (0,) * n, pipeline_mode=pl.Buffered(1))


def _dot(a, b):
    return jnp.dot(a.astype(MXU), b.astype(MXU), preferred_element_type=F32)


def _dot_nt(a, b):
    return lax.dot_general(a.astype(MXU), b.astype(MXU), (((1,), (1,)), ((), ())), preferred_element_type=F32)


def _dot_tn(a, b):
    return lax.dot_general(a.astype(MXU), b.astype(MXU), (((0,), (0,)), ((), ())), preferred_element_type=F32)


def _hdot(a, b):
    return jnp.dot(a, b, precision=HI, preferred_element_type=F32)


def _hdot_nt(a, b):
    return lax.dot_general(a, b, (((1,), (1,)), ((), ())), precision=HI, preferred_element_type=F32)


def _hdot_tn(a, b):
    return lax.dot_general(a, b, (((0,), (0,)), ((), ())), precision=HI, preferred_element_type=F32)


def _sigmoid(x):
    return 0.5 * jnp.tanh(0.5 * x) + 0.5


def _softplus(x):
    return jnp.maximum(x, 0.0) + jnp.log(1.0 + jnp.exp(-jnp.abs(x)))


def _head_sum_matrix():
    ri = lax.broadcasted_iota(jnp.int32, (LANE, LANE), 0) // DH
    ci = lax.broadcasted_iota(jnp.int32, (LANE, LANE), 1) // DH
    return (ri == ci).astype(F32)


def _group_sum(a, ones_matrix):
    hi = a.astype(jnp.bfloat16)
    lo = (a - hi.astype(F32)).astype(jnp.bfloat16)
    m = ones_matrix.astype(jnp.bfloat16)
    return jnp.dot(hi, m, preferred_element_type=F32) + jnp.dot(lo, m, preferred_element_type=F32)


def _shift_down(x, s):
    return pltpu.roll(x, s, 0)


def _shift_up(x, s):
    return pltpu.roll(x, x.shape[0] - s, 0)


def _inproj(x, n1w, w_main, w_small):
    t, d = x.shape
    nm = w_main.shape[1]
    tb = min(TB, t)

    def body(x_ref, nw_ref, wm_ref, ws_ref, h_ref, pm_ref, ps_ref):
        xv = x_ref[...]
        r = lax.rsqrt(jnp.mean(xv * xv, axis=-1, keepdims=True) + EPS)
        h = (xv * r * nw_ref[...]).astype(MXU)
        h_ref[...] = h
        pm_ref[...] = jnp.dot(h, wm_ref[...], preferred_element_type=F32)
        ps_ref[...] = jnp.dot(h, ws_ref[...], preferred_element_type=F32)

    return pl.pallas_call(
        body, name="inproj", grid=(t // tb,),
        in_specs=[pl.BlockSpec((tb, d), lambda i: (i, 0)), _resident((1, d)), _resident((d, nm)), _resident((d, LANE))],
        out_specs=[pl.BlockSpec((tb, d), lambda i: (i, 0)), pl.BlockSpec((tb, nm), lambda i: (i, 0)),
                   pl.BlockSpec((tb, LANE), lambda i: (i, 0))],
        out_shape=[jax.ShapeDtypeStruct((t, d), MXU), jax.ShapeDtypeStruct((t, nm), F32),
                   jax.ShapeDtypeStruct((t, LANE), F32)],
        compiler_params=_params(("arbitrary",)),
    )(x, n1w, w_main, w_small)


def _inproj_bwd(x, n1w, dx2, dgdn, dz, dfox, dfg, dps, w_main, w_small):
    t, d = x.shape
    tb = min(TB, t)
    w3 = 3 * WIDTH

    def body(x_ref, nw_ref, dx2_ref, dgdn_ref, dz_ref, dfox_ref, dfg_ref, dps_ref, wm_ref, ws_ref, gx_ref, dnw_ref):
        dh = _dot_nt(dgdn_ref[...], wm_ref[:, 0:w3])
        dh += _dot_nt(dz_ref[...], wm_ref[:, w3:w3 + WIDTH])
        dh += _dot_nt(dfox_ref[...], wm_ref[:, w3 + WIDTH:2 * w3 + WIDTH])
        dh += _dot_nt(dfg_ref[...], wm_ref[:, 2 * w3 + WIDTH:2 * w3 + 2 * WIDTH])
        dh += _dot_nt(dps_ref[...], ws_ref[...])
        xv = x_ref[...]
        r = lax.rsqrt(jnp.mean(xv * xv, axis=-1, keepdims=True) + EPS)
        xn = xv * r

        @pl.when(pl.program_id(0) == 0)
        def _():
            dnw_ref[...] = jnp.zeros_like(dnw_ref)

        dnw_ref[...] += jnp.sum(dh * xn, axis=0, keepdims=True)
        g = dh * nw_ref[...]
        gx_ref[...] = dx2_ref[...] + r * (g - xn * jnp.mean(g * xn, axis=-1, keepdims=True))

    def tok(n):
        return pl.BlockSpec((tb, n), lambda i: (i, 0))

    return pl.pallas_call(
        body, name="inproj_bwd", grid=(t // tb,),
        in_specs=[tok(d), _resident((1, d)), tok(d), tok(w3), tok(WIDTH), tok(w3), tok(WIDTH), tok(LANE),
                  _resident(w_main.shape), _resident(w_small.shape)],
        out_specs=[tok(d), pl.BlockSpec((1, d), lambda i: (0, 0))],
        out_shape=[jax.ShapeDtypeStruct((t, d), F32), jax.ShapeDtypeStruct((1, d), F32)],
        compiler_params=_params(("arbitrary",)),
    )(x, n1w, dx2, dgdn, dz, dfox, dfg, dps, w_main, w_small)


def _gate_lanes(shape):
    lane = lax.broadcasted_iota(jnp.int32, shape, 1)
    return lane < HEADS, (lane >= HEADS) & (lane < 2 * HEADS), (lane >= 2 * HEADS) & (lane < 3 * HEADS)


def _block_masks():
    ri = lax.broadcasted_iota(jnp.int32, (LANE, LANE), 0)
    ci = lax.broadcasted_iota(jnp.int32, (LANE, LANE), 1)
    same = (ri // CHUNK) == (ci // CHUNK)
    return ((ri >= ci).astype(F32), (ri <= ci).astype(F32), (same & (ri >= ci)).astype(F32),
            (same & (ri <= ci)).astype(F32), same.astype(F32))


def _gates(ps, gparams):
    t = ps.shape[0]
    nb = t // LANE

    def body(ps_ref, gp_ref, out_ref, run_ref, tot_ref, runt_ref):
        p = ps_ref[...]
        is_b, is_a, is_f = _gate_lanes(p.shape)
        z = p + gp_ref[0:1, :]
        neg_exp_a = -jnp.exp(gp_ref[1:2, :])
        glog = neg_exp_a * _softplus(z)
        logf = -_softplus(-z)
        out_ref[...] = jnp.where(is_b, _sigmoid(p), jnp.where(is_a, glog, jnp.where(is_f, logf, 0.0)))
        tril, _, tril_c, _, same_c = _block_masks()
        off = jnp.zeros((1, LANE), F32)
        for b in range(nb):
            rows = slice(b * LANE, (b + 1) * LANE)
            blk = out_ref[rows, :]
            ga = jnp.where(is_a[:LANE], blk, 0.0)
            fb = _hdot(tril, jnp.where(is_f[:LANE], blk, 0.0)) + off
            run = fb + _hdot(tril_c, ga)
            run_ref[rows, :] = run
            runt_ref[:, rows] = run.T
            tot_ref[rows, :] = _hdot(same_c, ga)
            off = fb[LANE - 1:LANE, :]

    return pl.pallas_call(
        body, name="gates",
        out_shape=[jax.ShapeDtypeStruct((t, LANE), F32)] * 3 + [jax.ShapeDtypeStruct((LANE, t), F32)],
        compiler_params=_params(),
    )(ps, gparams)


def _gates_bwd(ps, gparams, dcol_g, dtot_g, dcol_f, drow_g, drow_f):
    t = ps.shape[0]
    nb = t // LANE

    def body(ps_ref, gp_ref, dcg_ref, dtg_ref, dcf_ref, drg_ref, drf_ref, dps_ref, sums_ref, dl_ref, d0_ref, tr_ref):
        p = ps_ref[...]
        is_b, is_a, is_f = _gate_lanes(p.shape)
        _, triu, _, triu_c, same_c = _block_masks()
        tr_ref[...] = jnp.zeros_like(tr_ref)
        off = jnp.zeros((1, LANE), F32)
        for b in reversed(range(nb)):
            rows = slice(b * LANE, (b + 1) * LANE)
            tr_ref[HEADS:2 * HEADS, :] = drg_ref[:, rows]
            tr_ref[2 * HEADS:3 * HEADS, :] = drf_ref[:, rows]
            d = dcg_ref[rows, :] + dcf_ref[rows, :] + tr_ref[...].T
            d0_ref[rows, :] = d
            dlf = _hdot(triu, jnp.where(is_f[:LANE], d, 0.0)) + off
            dla = (_hdot(triu_c, jnp.where(is_a[:LANE], d, 0.0))
                   + _hdot(same_c, jnp.where(is_a[:LANE], dtg_ref[rows, :], 0.0)))
            dl_ref[rows, :] = dlf + dla
            off = dlf[0:1, :]
        z = p + gp_ref[0:1, :]
        neg_exp_a = -jnp.exp(gp_ref[1:2, :])
        sb = _sigmoid(p)
        glog = neg_exp_a * _softplus(z)
        dl = dl_ref[...]
        dp = jnp.where(is_b, d0_ref[...] * sb * (1.0 - sb),
                       jnp.where(is_a, dl * neg_exp_a * _sigmoid(z), jnp.where(is_f, dl * _sigmoid(-z), 0.0)))
        dps_ref[...] = dp
        s_a = jnp.sum(jnp.where(is_a, dl * glog, 0.0), axis=0, keepdims=True)
        s_p = jnp.sum(jnp.where(is_b, 0.0, dp), axis=0, keepdims=True)
        row = lax.broadcasted_iota(jnp.int32, (8, LANE), 0)
        from_a = pltpu.roll(jnp.where(row == 0, s_a, jnp.where(row == 1, s_p, 0.0)), LANE - HEADS, 1)
        from_f = pltpu.roll(jnp.where(row == 2, s_p, 0.0), LANE - 2 * HEADS, 1)
        lane = lax.broadcasted_iota(jnp.int32, (8, LANE), 1)
        sums_ref[...] = jnp.where(lane < HEADS, from_a + from_f, 0.0)

    return pl.pallas_call(
        body, name="gates_bwd",
        out_shape=[jax.ShapeDtypeStruct((t, LANE), F32), jax.ShapeDtypeStruct((8, LANE), F32)],
        scratch_shapes=[pltpu.VMEM((t, LANE), F32), pltpu.VMEM((t, LANE), F32), pltpu.VMEM((LANE, LANE), F32)],
        compiler_params=_params(),
    )(ps, gparams, dcol_g, dtot_g, dcol_f, drow_g, drow_f)


def _conv(xv, w):
    acc = w[3:4, :] * xv
    for s in range(1, 4):
        acc += w[3 - s:4 - s, :] * _shift_down(xv, s)
    return acc


PREP_ROWS = 256
HALO = 8


def _tile_rows(r):
    return pl.ds(pl.multiple_of(r * PREP_ROWS, PREP_ROWS), PREP_ROWS)


def _gdn_prep(pm, conv_w):
    t = pm.shape[0]
    nj = WIDTH // LANE
    win = PREP_ROWS + HALO

    def body(x_ref, w_ref, o_ref, xp_ref):
        kind = pl.program_id(0)
        xp_ref[0:HALO, :] = jnp.zeros((HALO, LANE), F32)
        xp_ref[HALO:, :] = x_ref[...]
        w = w_ref[...]
        hs = _head_sum_matrix()

        def tile(r, _):
            xw = xp_ref[pl.ds(pl.multiple_of(r * PREP_ROWS, PREP_ROWS), win), :]
            acc = _conv(xw, w)[HALO:]
            y = acc * _sigmoid(acc)
            out = jnp.where(kind < 2, y * lax.rsqrt(_group_sum(y * y, hs) + EPS), y)
            o_ref[0, 0, _tile_rows(r), :] = out[:, :DH]
            o_ref[0, 1, _tile_rows(r), :] = out[:, DH:]
            return 0

        lax.fori_loop(0, t // PREP_ROWS, tile, 0)

    return pl.pallas_call(
        body, name="gdn_prep", grid=(3, nj),
        in_specs=[pl.BlockSpec((t, LANE), lambda i, j: (0, i * nj + j)),
                  pl.BlockSpec((4, LANE), lambda i, j: (0, i * nj + j))],
        out_specs=pl.BlockSpec((1, 2, t, DH), lambda i, j: (i, j, 0, 0)),
        out_shape=jax.ShapeDtypeStruct((3, HEADS, t, DH), F32),
        scratch_shapes=[pltpu.VMEM((t + HALO, LANE), F32)],
        compiler_params=_params(("arbitrary", "arbitrary")),
    )(pm, conv_w)


def _gdn_prep_bwd(pm, conv_w, dqkv):
    t = pm.shape[0]
    nj = WIDTH // LANE
    win = PREP_ROWS + 2 * HALO

    def body(x_ref, w_ref, d_ref, dx_ref, dw_ref, xp_ref, dp_ref):
        kind = pl.program_id(0)
        zeros = jnp.zeros((HALO, LANE), F32)
        for ref in (xp_ref, dp_ref):
            ref[0:HALO, :] = zeros
            ref[HALO + t:, :] = zeros
        xp_ref[HALO:HALO + t, :] = x_ref[...]
        dp_ref[HALO:HALO + t, 0:DH] = d_ref[0, 0]
        dp_ref[HALO:HALO + t, DH:] = d_ref[0, 1]
        w = w_ref[...]
        hs = _head_sum_matrix()
        rows = lax.broadcasted_iota(jnp.int32, (win, LANE), 0)
        in_tile = (rows >= HALO) & (rows < HALO + PREP_ROWS)

        def tile(r, dw):
            start = pl.multiple_of(r * PREP_ROWS, PREP_ROWS)
            xw = xp_ref[pl.ds(start, win), :]
            d = dp_ref[pl.ds(start, win), :]
            acc = _conv(xw, w)
            sg = _sigmoid(acc)
            y = acc * sg
            rn = lax.rsqrt(_group_sum(y * y, hs) + EPS)
            yn = y * rn
            dy = jnp.where(kind < 2, rn * (d - yn * _group_sum(d * yn, hs)), d)
            dacc = dy * sg * (1.0 + acc * (1.0 - sg))
            dx = w[3:4, :] * dacc
            for s in range(1, 4):
                dx += w[3 - s:4 - s, :] * _shift_up(dacc, s)
            dx_ref[_tile_rows(r), :] = dx[HALO:HALO + PREP_ROWS]
            dm = jnp.where(in_tile, dacc, 0.0)
            return tuple(dw[i] + jnp.sum(dm * (xw if i == 3 else _shift_down(xw, 3 - i)), axis=0, keepdims=True)
                         for i in range(4))

        dw = lax.fori_loop(0, t // PREP_ROWS, tile, tuple(jnp.zeros((1, LANE), F32) for _ in range(4)))
        for i in range(4):
            dw_ref[i:i + 1, :] = dw[i]

    return pl.pallas_call(
        body, name="gdn_prep_bwd", grid=(3, nj),
        in_specs=[pl.BlockSpec((t, LANE), lambda i, j: (0, i * nj + j)),
                  pl.BlockSpec((4, LANE), lambda i, j: (0, i * nj + j)),
                  pl.BlockSpec((1, 2, t, DH), lambda i, j: (i, j, 0, 0))],
        out_specs=[pl.BlockSpec((t, LANE), lambda i, j: (0, i * nj + j)),
                   pl.BlockSpec((4, LANE), lambda i, j: (0, i * nj + j))],
        out_shape=[jax.ShapeDtypeStruct((t, 3 * WIDTH), F32), jax.ShapeDtypeStruct((4, 3 * WIDTH), F32)],
        scratch_shapes=[pltpu.VMEM((t + 2 * HALO, LANE), F32), pltpu.VMEM((t + 2 * HALO, LANE), F32)],
        compiler_params=_params(("arbitrary", "arbitrary")),
    )(pm, conv_w, dqkv)


FOX_COL0 = 4 * WIDTH // LANE


def _fox_prep(pm, nw):
    t = pm.shape[0]
    nj = WIDTH // LANE

    def body(x_ref, w_ref, o_ref):
        kind = pl.program_id(0)
        hs = _head_sum_matrix()
        wk = w_ref[pl.ds(kind, 1), :]

        def tile(r, _):
            xv = x_ref[_tile_rows(r), :]
            ms = _group_sum(xv * xv, hs) * (1.0 / DH)
            out = jnp.where(kind < 2, xv * lax.rsqrt(ms + EPS) * wk, xv)
            o_ref[0, 0, _tile_rows(r), :] = out[:, :DH]
            o_ref[0, 1, _tile_rows(r), :] = out[:, DH:]
            return 0

        lax.fori_loop(0, t // PREP_ROWS, tile, 0)

    return pl.pallas_call(
        body, name="fox_prep", grid=(3, nj),
        in_specs=[pl.BlockSpec((t, LANE), lambda i, j: (0, FOX_COL0 + i * nj + j)),
                  pl.BlockSpec((3, LANE), lambda i, j: (0, 0))],
        out_specs=pl.BlockSpec((1, 2, t, DH), lambda i, j: (i, j, 0, 0)),
        out_shape=jax.ShapeDtypeStruct((3, HEADS, t, DH), F32),
        compiler_params=_params(("arbitrary", "arbitrary")),
    )(pm, nw)


def _fox_prep_bwd(pm, nw, dqkv):
    t = pm.shape[0]
    nj = WIDTH // LANE

    def body(x_ref, w_ref, d_ref, dx_ref, dw_ref):
        kind = pl.program_id(0)
        hs = _head_sum_matrix()
        wk = w_ref[pl.ds(kind, 1), :]

        def tile(r, dw):
            xv = x_ref[_tile_rows(r), :]
            rn = lax.rsqrt(_group_sum(xv * xv, hs) * (1.0 / DH) + EPS)
            xn = xv * rn
            d = jnp.concatenate([d_ref[0, 0, _tile_rows(r), :], d_ref[0, 1, _tile_rows(r), :]], axis=1)
            g = d * wk
            dxn = rn * (g - xn * _group_sum(g * xn, hs) * (1.0 / DH))
            dx_ref[_tile_rows(r), :] = jnp.where(kind < 2, dxn, d)
            return dw + jnp.sum(d * xn, axis=0, keepdims=True)

        dw_ref[0, 0] = lax.fori_loop(0, t // PREP_ROWS, tile, jnp.zeros((1, LANE), F32))

    return pl.pallas_call(
        body, name="fox_prep_bwd", grid=(3, nj),
        in_specs=[pl.BlockSpec((t, LANE), lambda i, j: (0, FOX_COL0 + i * nj + j)),
                  pl.BlockSpec((3, LANE), lambda i, j: (0, 0)),
                  pl.BlockSpec((1, 2, t, DH), lambda i, j: (i, j, 0, 0))],
        out_specs=[pl.BlockSpec((t, LANE), lambda i, j: (0, i * nj + j)),
                   pl.BlockSpec((1, 1, 1, LANE), lambda i, j: (i, j, 0, 0))],
        out_shape=[jax.ShapeDtypeStruct((t, 3 * WIDTH), F32), jax.ShapeDtypeStruct((3, nj, 1, LANE), F32)],
        compiler_params=_params(("arbitrary", "arbitrary")),
    )(pm, nw, dqkv)


SC = 256
CPS = SC // CHUNK
Q_SCALE = DH ** -0.5


def _sc_masks():
    ri = lax.broadcasted_iota(jnp.int32, (SC, SC), 0)
    ci = lax.broadcasted_iota(jnp.int32, (SC, SC), 1)
    same = (ri // CHUNK) == (ci // CHUNK)
    return same & (ri >= ci), same & (ri > ci), ri == ci


def _unit_lower_inverse(m, eye):
    inv = jnp.where(eye, 1.0, 0.0) + m
    for _ in range(5):
        m = _dot(m, m)
        inv = inv + _dot(inv, m)
    return inv


def _lane_col(blk, lane_idx):
    lane = lax.broadcasted_iota(jnp.int32, blk.shape, 1)
    return jnp.sum(jnp.where(lane == lane_idx, blk, 0.0), axis=1, keepdims=True)


def _to_lane(col, lane_idx):
    lane = lax.broadcasted_iota(jnp.int32, (col.shape[0], LANE), 1)
    return jnp.where(lane == lane_idx, col, 0.0)


def _gdn_columns(gates_ref, run_ref, tot_ref, runt_ref, rows, h):
    return (_lane_col(gates_ref[rows, :], h), _lane_col(run_ref[rows, :], HEADS + h),
            _lane_col(tot_ref[rows, :], HEADS + h), runt_ref[pl.ds(h, 1), rows])


def _gdn_local(q, k, beta, gc, gl, grow, causal):
    decay = jnp.exp(jnp.where(causal, gc - grow, -1e30))
    egc = jnp.exp(gc)
    ekd = jnp.exp(gl - gc)
    qs = q * Q_SCALE
    kb = k * beta
    kk = _dot_nt(kb, k)
    qk = _dot_nt(qs, k)
    return beta, gl, decay, egc, ekd, qs, kb, kk, qk, jnp.where(causal, qk * decay, 0.0)


def _chunk_rows(c):
    return pl.ds(pl.multiple_of(c * CHUNK, CHUNK), CHUNK)


def _sc_rows(b):
    return pl.ds(pl.multiple_of(b * SC, SC), SC)


def _gdn_fwd(qkv, gates, run, tot, run_t):
    t = qkv.shape[2]
    nc = t // CHUNK
    nsc = t // SC

    def body(qkv_ref, gates_ref, run_ref, tot_ref, runt_ref, o_ref, st_ref, inv_ref, kc_s, qc_s, g_s, au_s):
        h = pl.program_id(0)
        causal, strict, eye = _sc_masks()

        def local(b, _):
            rows = _sc_rows(b)
            q, k, v = qkv_ref[0, 0, rows, :], qkv_ref[1, 0, rows, :], qkv_ref[2, 0, rows, :]
            beta, _, decay, egc, ekd, qs, kb, kk, _, attn = _gdn_local(
                q, k, *_gdn_columns(gates_ref, run_ref, tot_ref, runt_ref, rows, h), causal)
            inv = _unit_lower_inverse(-jnp.where(strict, kk * decay, 0.0), eye)
            inv_ref[0, rows, :] = inv.astype(inv_ref.dtype)
            u = _dot(inv, v * beta)
            w = _dot(inv, kb * egc)
            g_s[rows, :] = qs * egc - _dot(attn, w)
            au_s[rows, :] = _dot(attn, u)
            kd = k * ekd
            for j in range(CPS):
                sl = slice(j * CHUNK, (j + 1) * CHUNK)
                kc_s[b * CPS + j] = _dot_tn(kd[sl], w[sl])
                qc_s[b * CPS + j] = _dot_tn(kd[sl], u[sl])
            return 0

        lax.fori_loop(0, nsc, local, 0)

        def step(c, s):
            rows = _chunk_rows(c)
            st_ref[0, c] = s
            o_ref[0, rows, :] = _dot(g_s[rows, :], s) + au_s[rows, :]
            egl = jnp.exp(_lane_col(tot_ref[pl.ds(c * CHUNK, 1), :], HEADS + h))
            return egl * s - _dot(kc_s[c], s) + qc_s[c]

        lax.fori_loop(0, nc, step, jnp.zeros((DH, DH), F32))

    whole = pl.BlockSpec((t, LANE), lambda h: (0, 0))
    return pl.pallas_call(
        body, name="gdn_fwd", grid=(HEADS,),
        in_specs=[pl.BlockSpec((3, 1, t, DH), lambda h: (0, h, 0, 0)), whole, whole, whole,
                  pl.BlockSpec((HEADS, t), lambda h: (1, 0))],
        out_specs=[pl.BlockSpec((1, t, DH), lambda h: (h, 0, 0)), pl.BlockSpec((1, nc, DH, DH), lambda h: (h, 0, 0, 0)),
                   pl.BlockSpec((1, t, SC), lambda h: (h, 0, 0))],
        out_shape=[jax.ShapeDtypeStruct((HEADS, t, DH), F32), jax.ShapeDtypeStruct((HEADS, nc, DH, DH), F32),
                   jax.ShapeDtypeStruct((HEADS, t, SC), MXU)],
        scratch_shapes=[pltpu.VMEM((nc, DH, DH), F32), pltpu.VMEM((nc, DH, DH), F32),
                        pltpu.VMEM((t, DH), F32), pltpu.VMEM((t, DH), F32)],
        compiler_params=_params(("arbitrary",)),
    )(qkv, gates, run, tot, run_t)


def _gdn_bwd(qkv, gates, run, tot, run_t, inv, states, do):
    t = qkv.shape[2]
    nc = t // CHUNK
    nsc = t // SC

    def body(qkv_ref, gates_ref, run_ref, tot_ref, runt_ref, inv_ref, st_ref, do_ref,
             dqkv_ref, dcol_ref, dtot_ref, drow_ref, u_s, w_s, kc_s, h_s, dsn_s):
        h = pl.program_id(0)
        causal, strict, _ = _sc_masks()

        @pl.when(h == 0)
        def _():
            dcol_ref[...] = jnp.zeros_like(dcol_ref)
            dtot_ref[...] = jnp.zeros_like(dtot_ref)

        def local(b, _):
            rows = _sc_rows(b)
            q, k, v = qkv_ref[0, 0, rows, :], qkv_ref[1, 0, rows, :], qkv_ref[2, 0, rows, :]
            beta, _, _, egc, ekd, qs, kb, _, _, attn = _gdn_local(
                q, k, *_gdn_columns(gates_ref, run_ref, tot_ref, runt_ref, rows, h), causal)
            inv_b = inv_ref[0, rows, :]
            u = _dot(inv_b, v * beta)
            w = _dot(inv_b, kb * egc)
            u_s[rows, :] = u
            w_s[rows, :] = w
            g = qs * egc - _dot(attn, w)
            kd = k * ekd
            dout = do_ref[0, rows, :]
            for j in range(CPS):
                sl = slice(j * CHUNK, (j + 1) * CHUNK)
                kc_s[b * CPS + j] = _dot_tn(kd[sl], w[sl])
                h_s[b * CPS + j] = _dot_tn(g[sl], dout[sl])
            return 0

        lax.fori_loop(0, nsc, local, 0)

        def step(i, ds):
            c = nc - 1 - i
            dsn_s[c] = ds
            egl = jnp.exp(_lane_col(tot_ref[pl.ds(c * CHUNK, 1), :], HEADS + h))
            return egl * ds - _dot_tn(kc_s[c], ds) + h_s[c]

        lax.fori_loop(0, nc, step, jnp.zeros((DH, DH), F32))

        def back(b, _):
            rows = _sc_rows(b)
            q, k, v = qkv_ref[0, 0, rows, :], qkv_ref[1, 0, rows, :], qkv_ref[2, 0, rows, :]
            beta, gl, decay, egc, ekd, qs, kb, kk, qk, attn = _gdn_local(
                q, k, *_gdn_columns(gates_ref, run_ref, tot_ref, runt_ref, rows, h), causal)
            u, w = u_s[rows, :], w_s[rows, :]
            kd = k * ekd
            vb = v * beta
            kbe = kb * egc
            dout = do_ref[0, rows, :]
            first = lax.broadcasted_iota(jnp.int32, (CHUNK, 1), 0) == 0
            dg_p, dkd_p, dw_p, du_p, dgl_p = [], [], [], [], []
            for j in range(CPS):
                sl = slice(j * CHUNK, (j + 1) * CHUNK)
                s = st_ref[0, b * CPS + j]
                dsn = dsn_s[b * CPS + j]
                dkc = -_dot_nt(dsn, s)
                dg_p.append(_dot_nt(dout[sl], s))
                dkd_p.append(_dot_nt(w[sl], dkc) + _dot_nt(u[sl], dsn))
                dw_p.append(_dot(kd[sl], dkc))
                du_p.append(_dot(kd[sl], dsn))
                degl = jnp.sum(jnp.sum(s * dsn, axis=1, keepdims=True), axis=0, keepdims=True)
                dgl_p.append(jnp.where(first, degl * jnp.exp(gl[j * CHUNK:j * CHUNK + 1, :]), 0.0))
            dg, dkd = jnp.concatenate(dg_p, axis=0), jnp.concatenate(dkd_p, axis=0)
            da = jnp.where(causal, _dot_nt(dout, u) - _dot_nt(dg, w), 0.0)
            at = _dot_tn(attn, jnp.concatenate([dout, dg], axis=1))
            du = at[:, :DH] + jnp.concatenate(du_p, axis=0)
            dw = jnp.concatenate(dw_p, axis=0) - at[:, DH:]
            inv_t = inv_ref[0, rows, :].astype(F32).T
            it = _dot(inv_t, jnp.concatenate([du, dw], axis=1))
            dvb, dkbe = it[:, :DH], it[:, DH:]
            dinv = _dot_nt(du, vb) + _dot_nt(dw, kbe)
            dl = jnp.where(strict, -_dot(_dot(inv_t, dinv), inv_t), 0.0)
            dlogd = (dl * kk + da * qk) * decay
            dkk = dl * decay
            dqk = da * decay
            dkb = _dot(dkk, k) + dkbe * egc
            dqs = _dot(dqk, k) + dg * egc
            dk = (_dot_tn(jnp.concatenate([dkk, dqk], axis=0), jnp.concatenate([kb, qs], axis=0))
                  + dkd * ekd + dkb * beta)
            dkd_kd = jnp.sum(dkd * kd, axis=1, keepdims=True)
            dgc = (jnp.sum(dlogd, axis=1, keepdims=True) + jnp.sum(dg * qs, axis=1, keepdims=True) * egc
                   + jnp.sum(dkbe * kbe, axis=1, keepdims=True) - dkd_kd)
            dgl = dkd_kd + jnp.concatenate(dgl_p, axis=0)
            dbeta = jnp.sum(dkb * k, axis=1, keepdims=True) + jnp.sum(dvb * v, axis=1, keepdims=True)
            dqkv_ref[0, 0, rows, :] = dqs * Q_SCALE
            dqkv_ref[1, 0, rows, :] = dk
            dqkv_ref[2, 0, rows, :] = dvb * beta
            dcol_ref[rows, :] += _to_lane(dbeta, h) + _to_lane(dgc, HEADS + h)
            dtot_ref[rows, :] += _to_lane(dgl, HEADS + h)
            drow_ref[pl.ds(h, 1), rows] = -jnp.sum(dlogd, axis=0, keepdims=True)
            return 0

        lax.fori_loop(0, nsc, back, 0)

    whole = pl.BlockSpec((t, LANE), lambda h: (0, 0))
    rowspec = pl.BlockSpec((HEADS, t), lambda h: (0, 0))
    sq = pltpu.VMEM((nc, DH, DH), F32)
    return pl.pallas_call(
        body, name="gdn_bwd", grid=(HEADS,),
        in_specs=[pl.BlockSpec((3, 1, t, DH), lambda h: (0, h, 0, 0)), whole, whole, whole,
                  pl.BlockSpec((HEADS, t), lambda h: (1, 0)),
                  pl.BlockSpec((1, t, SC), lambda h: (h, 0, 0)), pl.BlockSpec((1, nc, DH, DH), lambda h: (h, 0, 0, 0)),
                  pl.BlockSpec((1, t, DH), lambda h: (h, 0, 0))],
        out_specs=[pl.BlockSpec((3, 1, t, DH), lambda h: (0, h, 0, 0)), whole, whole, rowspec],
        out_shape=[jax.ShapeDtypeStruct((3, HEADS, t, DH), F32), jax.ShapeDtypeStruct((t, LANE), F32),
                   jax.ShapeDtypeStruct((t, LANE), F32), jax.ShapeDtypeStruct((HEADS, t), F32)],
        scratch_shapes=[pltpu.VMEM((t, DH), F32), pltpu.VMEM((t, DH), F32), sq, sq, sq],
        compiler_params=_params(("arbitrary",)),
    )(qkv, gates, run, tot, run_t, inv, states, do)


FOX_HP = 2


def _wide_t(a):
    r = a.shape[0]
    return jnp.concatenate([a, jnp.zeros((r, LANE - DH), F32)], axis=1).T[:DH]


def _tall_t(a):
    r = a.shape[1]
    return jnp.concatenate([a, jnp.zeros((LANE - DH, r), F32)], axis=0).T[:, :DH]


def _key_side_f(run_blk, head, qb):
    col = jnp.broadcast_to(_lane_col(run_blk, 2 * HEADS + head), (run_blk.shape[0], LANE))
    return jnp.concatenate([col] * (qb // LANE), axis=1)


def _diag_mask(qb):
    return lax.broadcasted_iota(jnp.int32, (qb, qb), 0) <= lax.broadcasted_iota(jnp.int32, (qb, qb), 1)


def _fox_fwd(qkv, run, run_t):
    t = qkv.shape[2]
    qb = min(QB, t)
    nq = t // qb
    hp_n = FOX_HP

    def body(q_ref, k_ref, v_ref, run_ref, runt_ref, o_ref, lse_ref, kb_s, vt_s):
        hp = pl.program_id(0)
        i = pl.program_id(1)

        @pl.when(i == 0)
        def _():
            for hh in range(hp_n):
                kb_s[hh] = k_ref[0, hh].astype(MXU)
                for b in range(nq):
                    rows = slice(b * qb, (b + 1) * qb)
                    vt_s[hh, :, rows] = _wide_t(v_ref[0, hh, rows, :]).astype(MXU)

        qrows = pl.ds(pl.multiple_of(i * qb, qb), qb)
        qs = [(q_ref[0, hh] * Q_SCALE).astype(MXU) for hh in range(hp_n)]
        fq = [runt_ref[pl.ds(hp * hp_n + hh, 1), qrows] for hh in range(hp_n)]

        def block_rows(j):
            return pl.ds(pl.multiple_of(j * qb, qb), qb)

        def scores(j):
            return tuple(_dot_nt(kb_s[hh, block_rows(j), :], qs[hh]) for hh in range(hp_n))

        def absorb(j, raw, state, diagonal):
            rows = block_rows(j)
            run_blk = run_ref[rows, :]
            stats, pv = [], []
            for hh in range(hp_n):
                m, l, _ = state[hh]
                st = raw[hh] + (fq[hh] - _key_side_f(run_blk, hp * hp_n + hh, qb))
                if diagonal:
                    st = jnp.where(_diag_mask(qb), st, -1e30)
                m_new = jnp.maximum(m, jnp.max(st, axis=0, keepdims=True))
                p = jnp.exp(st - m_new)
                alpha = jnp.exp(m - m_new)
                stats.append((m_new, alpha * l + jnp.sum(p, axis=0, keepdims=True), alpha))
                pv.append(_dot(vt_s[hh, :, rows], p))
            return tuple((stats[hh][0], stats[hh][1], stats[hh][2] * state[hh][2] + pv[hh]) for hh in range(hp_n))

        def kstep(j, carry):
            state, raw = carry
            ahead = scores(j + 1)
            return absorb(j, raw, state, False), ahead

        init = tuple((jnp.full((1, qb), -1e30, F32), jnp.zeros((1, qb), F32), jnp.zeros((DH, qb), F32))
                     for _ in range(hp_n))
        state, raw = lax.fori_loop(0, i, kstep, (init, scores(0)))
        state = absorb(i, raw, state, True)
        for hh in range(hp_n):
            m, l, acc = state[hh]
            o_ref[hh] = _tall_t(acc / l)
            lse_ref[pl.ds(hp * hp_n + hh, 1), qrows] = m + jnp.log(l)

    return pl.pallas_call(
        body, name="fox_fwd", grid=(HEADS // hp_n, nq),
        in_specs=[pl.BlockSpec((1, hp_n, qb, DH), lambda h, i: (0, h, i, 0)),
                  pl.BlockSpec((1, hp_n, t, DH), lambda h, i: (1, h, 0, 0)),
                  pl.BlockSpec((1, hp_n, t, DH), lambda h, i: (2, h, 0, 0)),
                  pl.BlockSpec((t, LANE), lambda h, i: (0, 0)),
                  pl.BlockSpec((HEADS, t), lambda h, i: (2, 0))],
        out_specs=[pl.BlockSpec((hp_n, qb, DH), lambda h, i: (h, i, 0)), pl.BlockSpec((HEADS, t), lambda h, i: (0, 0))],
        out_shape=[jax.ShapeDtypeStruct((HEADS, t, DH), F32), jax.ShapeDtypeStruct((HEADS, t), F32)],
        scratch_shapes=[pltpu.VMEM((hp_n, t, DH), MXU), pltpu.VMEM((hp_n, DH, t), MXU)],
        compiler_params=_params(("arbitrary", "arbitrary")),
    )(qkv, qkv, qkv, run, run_t)


def _fox_bwd(qkv, run, run_t, o, lse, do):
    t = qkv.shape[2]
    qb = min(QB, t)
    nq = t // qb
    hp_n = FOX_HP

    def body(q_ref, k_ref, v_ref, run_ref, runt_ref, o_ref, lse_ref, do_ref, dqkv_ref, dcol_ref, drow_ref, dqt_s):
        hp = pl.program_id(0)
        j = pl.program_id(1)

        @pl.when(j == 0)
        def _():
            dqt_s[...] = jnp.zeros_like(dqt_s)

        @pl.when((j == 0) & (hp == 0))
        def _():
            dcol_ref[...] = jnp.zeros_like(dcol_ref)
            drow_ref[...] = jnp.zeros_like(drow_ref)

        krows = pl.ds(pl.multiple_of(j * qb, qb), qb)
        run_blk = run_ref[krows, :]
        ones8 = jnp.ones((8, DH), MXU)
        kb, kt, vb, fk = [], [], [], []
        for hh in range(hp_n):
            kf = k_ref[0, hh]
            kb.append(kf.astype(MXU))
            kt.append(_wide_t(kf).astype(MXU))
            vb.append(v_ref[0, hh].astype(MXU))
            fk.append(_key_side_f(run_blk, hp * hp_n + hh, qb))

        def block_rows(i):
            return pl.ds(pl.multiple_of(i * qb, qb), qb)

        def products(i):
            rows = block_rows(i)
            out = []
            for hh in range(hp_n):
                dout = do_ref[hh, rows, :]
                x = dout * o_ref[hh, rows, :]
                x_hi = x.astype(MXU)
                out.append((_dot_nt(kb[hh], q_ref[0, hh, rows, :] * Q_SCALE), _dot_nt(vb[hh], dout),
                            (_dot_nt(ones8, x_hi) + _dot_nt(ones8, x - x_hi.astype(F32)))[0:1, :]))
            return tuple(out)

        def absorb(i, prods, acc, diagonal):
            rows = block_rows(i)
            pieces = []
            for hh in range(hp_n):
                head = hp * hp_n + hh
                raw, dpt, drow = prods[hh]
                off = runt_ref[pl.ds(head, 1), rows] - lse_ref[pl.ds(head, 1), rows]
                st = raw + (off - fk[hh])
                if diagonal:
                    st = jnp.where(_diag_mask(qb), st, -1e30)
                pt = jnp.exp(st)
                dst = pt * (dpt - drow)
                drow_ref[pl.ds(head, 1), rows] += jnp.sum(dst, axis=0, keepdims=True)
                folded = dst[:, 0:LANE]
                for c in range(1, qb // LANE):
                    folded = folded + dst[:, c * LANE:(c + 1) * LANE]
                pieces.append((_dot(dst, q_ref[0, hh, rows, :] * Q_SCALE), _dot(pt, do_ref[hh, rows, :]),
                               _dot(kt[hh], dst), folded))
            out = []
            for hh in range(hp_n):
                dqt_s[hh, :, rows] += pieces[hh][2]
                out.append((acc[hh][0] + pieces[hh][0], acc[hh][1] + pieces[hh][1], acc[hh][2] + pieces[hh][3]))
            return tuple(out)

        def qstep(i, carry):
            acc, prods = carry
            ahead = products(jnp.minimum(i + 1, nq - 1))
            return absorb(i, prods, acc, False), ahead

        init = tuple((jnp.zeros((qb, DH), F32), jnp.zeros((qb, DH), F32), jnp.zeros((qb, LANE), F32))
                     for _ in range(hp_n))
        first = products(j)
        ahead = products(jnp.minimum(j + 1, nq - 1))
        acc = absorb(j, first, init, True)
        acc, _ = lax.fori_loop(j + 1, nq, qstep, (acc, ahead))
        for hh in range(hp_n):
            dk, dv, ds_sum = acc[hh]
            dqkv_ref[1, hh, krows, :] = dk
            dqkv_ref[2, hh, krows, :] = dv
            dcol_ref[krows, :] += _to_lane(-jnp.sum(ds_sum, axis=1, keepdims=True), 2 * HEADS + hp * hp_n + hh)

        @pl.when(j == nq - 1)
        def _():
            for hh in range(hp_n):
                for b in range(nq):
                    rows = slice(b * qb, (b + 1) * qb)
                    dqkv_ref[0, hh, rows, :] = _tall_t(dqt_s[hh, :, rows]) * Q_SCALE

    full = pl.BlockSpec((hp_n, t, DH), lambda h, j: (h, 0, 0))
    rows8 = pl.BlockSpec((HEADS, t), lambda h, j: (0, 0))
    return pl.pallas_call(
        body, name="fox_bwd", grid=(HEADS // hp_n, nq),
        in_specs=[pl.BlockSpec((1, hp_n, t, DH), lambda h, j: (0, h, 0, 0)),
                  pl.BlockSpec((1, hp_n, qb, DH), lambda h, j: (1, h, j, 0)),
                  pl.BlockSpec((1, hp_n, qb, DH), lambda h, j: (2, h, j, 0)),
                  pl.BlockSpec((t, LANE), lambda h, j: (0, 0)), pl.BlockSpec((HEADS, t), lambda h, j: (2, 0)),
                  full, rows8, full],
        out_specs=[pl.BlockSpec((3, hp_n, t, DH), lambda h, j: (0, h, 0, 0)),
                   pl.BlockSpec((t, LANE), lambda h, j: (0, 0)), rows8],
        out_shape=[jax.ShapeDtypeStruct((3, HEADS, t, DH), F32), jax.ShapeDtypeStruct((t, LANE), F32),
                   jax.ShapeDtypeStruct((HEADS, t), F32)],
        scratch_shapes=[pltpu.VMEM((hp_n, DH, t), F32)],
        compiler_params=_params(("arbitrary", "arbitrary")),
    )(qkv, qkv, qkv, run, run_t, o, lse, do)


Z_COL0 = 3 * WIDTH // LANE
FGATE_COL0 = 7 * WIDTH // LANE


def _gdn_post(o, pm, onw):
    t = pm.shape[0]

    def body(o_ref, z_ref, w_ref, m_ref):
        z = z_ref[...]
        sz = z * _sigmoid(z)
        for hh in range(2):
            ov = o_ref[hh]
            n = ov * lax.rsqrt(jnp.mean(ov * ov, axis=-1, keepdims=True) + EPS) * w_ref[...]
            m_ref[:, hh * DH:(hh + 1) * DH] = (n * sz[:, hh * DH:(hh + 1) * DH]).astype(m_ref.dtype)

    return pl.pallas_call(
        body, name="gdn_post", grid=(WIDTH // LANE,),
        in_specs=[pl.BlockSpec((2, t, DH), lambda j: (j, 0, 0)), pl.BlockSpec((t, LANE), lambda j: (0, Z_COL0 + j)),
                  pl.BlockSpec((1, DH), lambda j: (0, 0))],
        out_specs=pl.BlockSpec((t, LANE), lambda j: (0, j)),
        out_shape=jax.ShapeDtypeStruct((t, WIDTH), MXU),
        compiler_params=_params(("arbitrary",)),
    )(o, pm, onw)


def _gdn_post_bwd(o, pm, onw, dmix):
    t = pm.shape[0]

    def body(o_ref, z_ref, w_ref, dm_ref, do_ref, dz_ref, dw_ref):
        @pl.when(pl.program_id(0) == 0)
        def _():
            dw_ref[...] = jnp.zeros_like(dw_ref)

        z = z_ref[...]
        sg = _sigmoid(z)
        sz = z * sg
        dsz = sg * (1.0 + z * (1.0 - sg))
        dm = dm_ref[...]
        for hh in range(2):
            cols = slice(hh * DH, (hh + 1) * DH)
            ov = o_ref[hh]
            r = lax.rsqrt(jnp.mean(ov * ov, axis=-1, keepdims=True) + EPS)
            xn = ov * r
            dmh = dm[:, cols]
            dn = dmh * sz[:, cols]
            dz_ref[:, cols] = dmh * (xn * w_ref[...]) * dsz[:, cols]
            dw_ref[...] += jnp.sum(dn * xn, axis=0, keepdims=True)
            g = dn * w_ref[...]
            do_ref[hh] = r * (g - xn * jnp.mean(g * xn, axis=-1, keepdims=True))

    return pl.pallas_call(
        body, name="gdn_post_bwd", grid=(WIDTH // LANE,),
        in_specs=[pl.BlockSpec((2, t, DH), lambda j: (j, 0, 0)), pl.BlockSpec((t, LANE), lambda j: (0, Z_COL0 + j)),
                  pl.BlockSpec((1, DH), lambda j: (0, 0)), pl.BlockSpec((t, LANE), lambda j: (0, j))],
        out_specs=[pl.BlockSpec((2, t, DH), lambda j: (j, 0, 0)), pl.BlockSpec((t, LANE), lambda j: (0, j)),
                   pl.BlockSpec((1, DH), lambda j: (0, 0))],
        out_shape=[jax.ShapeDtypeStruct((HEADS, t, DH), F32), jax.ShapeDtypeStruct((t, WIDTH), F32),
                   jax.ShapeDtypeStruct((1, DH), F32)],
        compiler_params=_params(("arbitrary",)),
    )(o, pm, onw, dmix)


def _fox_post(o, pm):
    t = pm.shape[0]

    def body(o_ref, g_ref, m_ref):
        sg = _sigmoid(g_ref[...])
        for hh in range(2):
            cols = slice(hh * DH, (hh + 1) * DH)
            m_ref[:, cols] = (o_ref[hh] * sg[:, cols]).astype(m_ref.dtype)

    return pl.pallas_call(
        body, name="fox_post", grid=(WIDTH // LANE,),
        in_specs=[pl.BlockSpec((2, t, DH), lambda j: (j, 0, 0)), pl.BlockSpec((t, LANE), lambda j: (0, FGATE_COL0 + j))],
        out_specs=pl.BlockSpec((t, LANE), lambda j: (0, j)),
        out_shape=jax.ShapeDtypeStruct((t, WIDTH), MXU),
        compiler_params=_params(("arbitrary",)),
    )(o, pm)


def _fox_post_bwd(o, pm, dmix):
    t = pm.shape[0]

    def body(o_ref, g_ref, dm_ref, do_ref, dg_ref):
        sg = _sigmoid(g_ref[...])
        dm = dm_ref[...]
        for hh in range(2):
            cols = slice(hh * DH, (hh + 1) * DH)
            do_ref[hh] = dm[:, cols] * sg[:, cols]
            dg_ref[:, cols] = dm[:, cols] * o_ref[hh] * (sg * (1.0 - sg))[:, cols]

    return pl.pallas_call(
        body, name="fox_post_bwd", grid=(WIDTH // LANE,),
        in_specs=[pl.BlockSpec((2, t, DH), lambda j: (j, 0, 0)), pl.BlockSpec((t, LANE), lambda j: (0, FGATE_COL0 + j)),
                  pl.BlockSpec((t, LANE), lambda j: (0, j))],
        out_specs=[pl.BlockSpec((2, t, DH), lambda j: (j, 0, 0)), pl.BlockSpec((t, LANE), lambda j: (0, j))],
        out_shape=[jax.ShapeDtypeStruct((HEADS, t, DH), F32), jax.ShapeDtypeStruct((t, WIDTH), F32)],
        compiler_params=_params(("arbitrary",)),
    )(o, pm, dmix)


def _tail(x, mixg, mixf, tgt, wo, n2w, wg, wu, wd, fw):
    t, d = x.shape
    dff = wg.shape[1]
    tb = min(TB, t)

    def body(x_ref, mg_ref, mf_ref, t_ref, wo_ref, n2_ref, wg_ref, wu_ref, wd_ref, fw_ref,
             h2_ref, act_ref, dgate_ref, dup_ref, dx3_ref, dx2_ref, dmg_ref, dmf_ref, dn2_ref, dfw_ref, loss_ref):
        @pl.when(pl.program_id(0) == 0)
        def _():
            dn2_ref[...] = jnp.zeros_like(dn2_ref)
            dfw_ref[...] = jnp.zeros_like(dfw_ref)
            loss_ref[...] = jnp.zeros_like(loss_ref)

        x2 = x_ref[...] + _dot(mg_ref[...], wo_ref[0:WIDTH, :]) + _dot(mf_ref[...], wo_ref[WIDTH:2 * WIDTH, :])
        r2 = lax.rsqrt(jnp.mean(x2 * x2, axis=-1, keepdims=True) + EPS)
        xn2 = x2 * r2
        h2 = (xn2 * n2_ref[...]).astype(MXU)
        h2_ref[...] = h2
        gate = _dot(h2, wg_ref[...])
        up = _dot(h2, wu_ref[...])
        sg = _sigmoid(gate)
        sl = gate * sg
        act = (sl * up).astype(MXU)
        act_ref[...] = act
        x3 = x2 + _dot(act, wd_ref[...])
        r3 = lax.rsqrt(jnp.mean(x3 * x3, axis=-1, keepdims=True) + EPS)
        xn3 = x3 * r3
        err = xn3 * fw_ref[...] - t_ref[...]
        loss_ref[...] += 0.5 * jnp.sum(jnp.mean(err * err, axis=-1, keepdims=True), axis=0, keepdims=True)
        dy = err * (1.0 / d)
        dfw_ref[...] += jnp.sum(dy * xn3, axis=0, keepdims=True)
        g3 = dy * fw_ref[...]
        dx3 = r3 * (g3 - xn3 * jnp.mean(g3 * xn3, axis=-1, keepdims=True))
        dx3_ref[...] = dx3.astype(MXU)
        dact = _dot_nt(dx3, wd_ref[...])
        dgate = (dact * up * (sg * (1.0 + gate * (1.0 - sg)))).astype(MXU)
        dup = (dact * sl).astype(MXU)
        dgate_ref[...] = dgate
        dup_ref[...] = dup
        dh2 = _dot_nt(dgate, wg_ref[...]) + _dot_nt(dup, wu_ref[...])
        dn2_ref[...] += jnp.sum(dh2 * xn2, axis=0, keepdims=True)
        g2 = dh2 * n2_ref[...]
        dx2 = dx3 + r2 * (g2 - xn2 * jnp.mean(g2 * xn2, axis=-1, keepdims=True))
        dx2_ref[...] = dx2
        dmg_ref[...] = _dot_nt(dx2, wo_ref[0:WIDTH, :])
        dmf_ref[...] = _dot_nt(dx2, wo_ref[WIDTH:2 * WIDTH, :])

    def tok(n):
        return pl.BlockSpec((tb, n), lambda i: (i, 0))

    acc = pl.BlockSpec((1, d), lambda i: (0, 0))
    sds = jax.ShapeDtypeStruct
    return pl.pallas_call(
        body, name="tail", grid=(t // tb,),
        in_specs=[tok(d), tok(WIDTH), tok(WIDTH), tok(d), _resident(wo.shape), _resident((1, d)),
                  _resident(wg.shape), _resident(wu.shape), _resident(wd.shape), _resident((1, d))],
        out_specs=[tok(d), tok(dff), tok(dff), tok(dff), tok(d), tok(d), tok(WIDTH), tok(WIDTH), acc, acc,
                   pl.BlockSpec((1, 1), lambda i: (0, 0))],
        out_shape=[sds((t, d), MXU), sds((t, dff), MXU), sds((t, dff), MXU), sds((t, dff), MXU), sds((t, d), MXU),
                   sds((t, d), F32), sds((t, WIDTH), F32), sds((t, WIDTH), F32), sds((1, d), F32), sds((1, d), F32),
                   sds((1, 1), F32)],
        compiler_params=_params(("arbitrary",)),
    )(x, mixg, mixf, tgt, wo, n2w, wg, wu, wd, fw)


def _wgrad(a, b, name):
    t, m = a.shape
    n = b.shape[1]
    bn = 256 if n % 256 == 0 else LANE

    def body(a_ref, b_ref, o_ref):
        o_ref[...] = _dot_tn(a_ref[...], b_ref[...]).astype(o_ref.dtype)

    return pl.pallas_call(
        body, name=name, grid=(n // bn,),
        in_specs=[_resident((t, m)), pl.BlockSpec((t, bn), lambda j: (0, j))],
        out_specs=pl.BlockSpec((m, bn), lambda j: (0, j)),
        out_shape=jax.ShapeDtypeStruct((m, n), WIRE),
        compiler_params=_params(("arbitrary",)),
    )(a, b)


def _split_w_in(w_in):
    a = 4 * WIDTH
    b = a + 2 * HEADS
    c = b + 4 * WIDTH
    main = jnp.concatenate([w_in[:, :a], w_in[:, b:c]], axis=1)
    small = jnp.concatenate([w_in[:, a:b], w_in[:, c:], jnp.zeros((w_in.shape[0], LANE - 3 * HEADS), w_in.dtype)], axis=1)
    return main, small


def _merge_dw_in(d_gdn, d_z, d_fox, d_fg, d_small):
    return jnp.concatenate([d_gdn, d_z, d_small[:, :2 * HEADS], d_fox, d_fg, d_small[:, 2 * HEADS:3 * HEADS]], axis=1)


def _lanes(*pieces):
    v = jnp.concatenate([p.reshape(-1).astype(F32) for p in pieces])
    return jnp.pad(v, (0, LANE - v.shape[0])).reshape(1, LANE)


def _vector_params(p):
    d = p["norm1_w"].size
    gparams = jnp.concatenate([_lanes(jnp.zeros(HEADS), p["gdn_dt_bias"], p["fox_f_bias"]),
                               _lanes(jnp.zeros(HEADS), p["gdn_A_log"]), jnp.zeros((6, LANE), F32)])
    fox_nw = jnp.stack([jnp.tile(p["fox_q_norm_w"].reshape(-1), 2), jnp.tile(p["fox_k_norm_w"].reshape(-1), 2),
                        jnp.ones((LANE,), F32)])
    return dict(n1w=p["norm1_w"].reshape(1, d), n2w=p["norm2_w"].reshape(1, d), fw=p["final_norm_w"].reshape(1, d),
                onw=p["gdn_out_norm_w"].reshape(1, DH), gparams=gparams, fox_nw=fox_nw)


def _mixer_forward(x, vp, w_main, w_small, conv_w):
    h1, pm, ps = _inproj(x, vp["n1w"], w_main, w_small)
    gates, run, tot, run_t = _gates(ps, vp["gparams"])
    gqkv = _gdn_prep(pm, conv_w)
    o_gdn, states, inv = _gdn_fwd(gqkv, gates, run, tot, run_t)
    mixg = _gdn_post(o_gdn, pm, vp["onw"])
    fqkv = _fox_prep(pm, vp["fox_nw"])
    o_fox, lse = _fox_fwd(fqkv, run, run_t)
    mixf = _fox_post(o_fox, pm)
    return dict(h1=h1, pm=pm, ps=ps, gates=gates, run=run, tot=tot, run_t=run_t, gqkv=gqkv, o_gdn=o_gdn, states=states,
                inv=inv, mixg=mixg, fqkv=fqkv, o_fox=o_fox, lse=lse, mixf=mixf)


def _mixer_backward(x, vp, w_main, w_small, conv_w, f, dx2, dmixg, dmixf, onw):
    pm = f["pm"]
    do_gdn, dz, donw = _gdn_post_bwd(f["o_gdn"], pm, onw, dmixg)
    dgqkv, dcol_g, dtot_g, drow_g = _gdn_bwd(f["gqkv"], f["gates"], f["run"], f["tot"], f["run_t"], f["inv"], f["states"], do_gdn)
    dgdn, dconv = _gdn_prep_bwd(pm, conv_w, dgqkv)
    do_fox, dfg = _fox_post_bwd(f["o_fox"], pm, dmixf)
    dfqkv, dcol_f, drow_f = _fox_bwd(f["fqkv"], f["run"], f["run_t"], f["o_fox"], f["lse"], do_fox)
    dfox, dfnw = _fox_prep_bwd(pm, vp["fox_nw"], dfqkv)
    dps, gsum = _gates_bwd(f["ps"], vp["gparams"], dcol_g, dtot_g, dcol_f, drow_g, drow_f)
    grad_x, dn1w = _inproj_bwd(x, vp["n1w"], dx2, dgdn, dz, dfox, dfg, dps, w_main, w_small)
    h1 = f["h1"]
    dw_in = _merge_dw_in(_wgrad(h1, dgdn, "dw_in_gdn"), _wgrad(h1, dz, "dw_in_z"), _wgrad(h1, dfox, "dw_in_fox"),
                         _wgrad(h1, dfg, "dw_in_fgate"), _wgrad(h1, dps, "dw_in_small"))
    return grad_x, dw_in, dconv, dict(dn1w=dn1w, gsum=gsum, donw=donw, dfnw=dfnw)


VECTORS = ("norm1_w", "norm2_w", "final_norm_w", "gdn_A_log", "gdn_dt_bias", "gdn_out_norm_w", "fox_f_bias",
           "fox_q_norm_w", "fox_k_norm_w")
VEC_ROWS = 16
LOSS_ROW = len(VECTORS)


def _pack_vectors(dn1w, dn2w, dfw, gsum, donw, dfnw, loss):
    d = dn1w.shape[1]

    def body(n1_ref, n2_ref, fw_ref, gs_ref, on_ref, fn_ref, loss_ref, o_ref):
        o_ref[...] = jnp.zeros_like(o_ref)
        o_ref[0:1, :] = n1_ref[...]
        o_ref[1:2, :] = n2_ref[...]
        o_ref[2:3, :] = fw_ref[...]
        o_ref[3:4, 0:HEADS] = gs_ref[0:1, 0:HEADS]
        o_ref[4:5, 0:HEADS] = gs_ref[1:2, 0:HEADS]
        o_ref[5:6, 0:DH] = on_ref[...]
        o_ref[6:7, 0:HEADS] = gs_ref[2:3, 0:HEADS]
        for kind in range(2):
            v = fn_ref[kind, 0]
            for j in range(1, fn_ref.shape[1]):
                v = v + fn_ref[kind, j]
            o_ref[7 + kind:8 + kind, 0:DH] = v[:, :DH] + v[:, DH:]
        o_ref[LOSS_ROW:LOSS_ROW + 1, 0:1] = loss_ref[...]

    return pl.pallas_call(body, name="pack_vectors", out_shape=jax.ShapeDtypeStruct((VEC_ROWS, d), F32),
                          compiler_params=_params())(dn1w, dn2w, dfw, gsum, donw, dfnw, loss)


def _local_step(x, tgt, p, w_in, conv_w, wo, wg, wu, wd):
    vp = _vector_params(p)
    w_main, w_small = _split_w_in(w_in)
    f = _mixer_forward(x, vp, w_main, w_small, conv_w)
    (h2, act, dgate, dup, dx3, dx2, dmixg, dmixf, dn2w, dfw, loss) = _tail(
        x, f["mixg"], f["mixf"], tgt, wo, vp["n2w"], wg, wu, wd, vp["fw"])
    grad_x, dw_in, dconv, small = _mixer_backward(x, vp, w_main, w_small, conv_w, f, dx2, dmixg, dmixf, vp["onw"])
    grads = {"w_in": dw_in, "gdn_conv_w": dconv,
             "w_out": jnp.concatenate([_wgrad(f["mixg"], dx2, "dw_out_gdn"), _wgrad(f["mixf"], dx2, "dw_out_fox")], axis=0),
             "w_ffn_gate": _wgrad(h2, dgate, "dw_gate"), "w_ffn_up": _wgrad(h2, dup, "dw_up"),
             "w_ffn_down": _wgrad(act, dx3, "dw_down")}
    vec = _pack_vectors(small["dn1w"], dn2w, dfw, small["gsum"], small["donw"], small["dfnw"], loss)
    return grad_x, grads, vec


def _my_place():
    return lax.axis_index("x"), lax.axis_index("y"), lax.axis_index("c")


def _peers():
    x, y, c = _my_place()
    peers = []
    for k in range(1, N_DEV):
        px = 1 - x if k & 4 else x
        py = 1 - y if k & 2 else y
        pc = 1 - c if k & 1 else c
        peers.append(((px, py, pc), 4 * px + 2 * py + pc))
    return 4 * x + 2 * y + c, peers


def _spread_copies(kind, srcs, lands, send_sems, recv_sems):
    me, peers = _peers()
    kinds = [kind] * len(srcs) if isinstance(kind, str) else kind
    remote, local = [], []
    for i, (kd, src, land) in enumerate(zip(kinds, srcs, lands)):
        for k, (dev, idx) in enumerate(peers):
            remote.append(pltpu.make_async_remote_copy(
                src_ref=src if kd == "gather" else src.at[idx], dst_ref=land.at[me],
                send_sem=send_sems.at[i * (N_DEV - 1) + k], recv_sem=recv_sems.at[i * (N_DEV - 1) + k],
                device_id=dev, device_id_type=MESH))
        local.append((src if kd == "gather" else src.at[me], land.at[me]))
    return remote, local


def _land_shape(kind, a):
    return (N_DEV,) + a.shape if kind == "gather" else a.shape


HBM = pl.BlockSpec(memory_space=pltpu.HBM)
SEM = pl.BlockSpec(memory_space=pltpu.SEMAPHORE)


def _hbm(a):
    return pltpu.with_memory_space_constraint(a, pltpu.HBM)


def _spread_start(groups, kind, name):
    flat = [a for g in groups for a in g]
    n = len(flat)
    ng = len(groups)

    def body(*refs):
        srcs, lands = refs[:n], refs[n:2 * n]
        sems = refs[2 * n:2 * n + 2 * ng]
        token = refs[4 * n + 2 * ng]
        local_sems = refs[4 * n + 2 * ng + 1]
        off = 0
        remotes, locals_ = [], []
        for g in range(ng):
            m = len(groups[g])
            remote, local = _spread_copies(kind, srcs[off:off + m], lands[off:off + m], sems[2 * g], sems[2 * g + 1])
            remotes += remote
            locals_ += local
            off += m
        for cp in remotes:
            cp.start()
        copies = [pltpu.make_async_copy(s, d, local_sems.at[i]) for i, (s, d) in enumerate(locals_)]
        for cp in copies:
            cp.start()
        for cp in copies:
            cp.wait()
        token[...] = jnp.zeros_like(token)

    sem_shapes = []
    for g in groups:
        sem_shapes += [pltpu.SemaphoreType.DMA((len(g) * (N_DEV - 1),))] * 2
    lands = [lax.empty(_land_shape(kind, a), a.dtype) for a in flat]
    out = pl.pallas_call(
        body, name=name,
        out_shape=tuple(sem_shapes) + tuple(pltpu.HBM(a.shape, a.dtype) for a in flat + lands)
        + (jax.ShapeDtypeStruct((8, LANE), F32),),
        in_specs=[HBM] * (2 * n), out_specs=tuple([SEM] * (2 * ng) + [HBM] * (2 * n) + [pl.BlockSpec(memory_space=pltpu.VMEM)]),
        input_output_aliases={j: 2 * ng + j for j in range(2 * n)},
        scratch_shapes=[pltpu.SemaphoreType.DMA((n,))],
        compiler_params=pltpu.CompilerParams(has_side_effects=pltpu.SideEffectType.DATAFLOW_SIDE_EFFECTING),
    )(*[_hbm(a) for a in flat], *[_hbm(a) for a in lands])
    states, off = [], 0
    for g in range(ng):
        m = len(groups[g])
        states.append((list(out[2 * ng + off:2 * ng + off + m]), list(out[2 * ng + n + off:2 * ng + n + off + m]),
                       out[2 * g], out[2 * g + 1]))
        off += m
    return states, out[-1]


def _spread_wait(state, kind, after, name):
    srcs, lands, send_sems, recv_sems = state
    n = len(srcs)

    def body(*refs):
        remote, _ = _spread_copies(kind, refs[:n], refs[n:2 * n], refs[2 * n], refs[2 * n + 1])
        for cp in remote:
            cp.wait_send()
        for cp in remote:
            cp.wait_recv()

    out = pl.pallas_call(
        body, name=name,
        out_shape=tuple(pltpu.HBM(a.shape, a.dtype) for a in srcs + lands),
        in_specs=[HBM] * (2 * n) + [SEM, SEM, pl.BlockSpec(memory_space=pl.ANY)], out_specs=tuple([HBM] * (2 * n)),
        input_output_aliases={j: j for j in range(2 * n)},
        compiler_params=pltpu.CompilerParams(has_side_effects=pltpu.SideEffectType.DATAFLOW_SIDE_EFFECTING),
    )(*srcs, *lands, send_sems, recv_sems, after)
    return list(out[n:])


def _exchange(scatter, gather, name):
    kinds = ["scatter"] * len(scatter) + ["gather"] * len(gather)
    arrays = list(scatter) + list(gather)
    n = len(arrays)

    def body(*refs):
        send_sems, recv_sems, local_sems = refs[2 * n:]
        remote, local = _spread_copies(kinds, refs[:n], refs[n:2 * n], send_sems, recv_sems)
        copies = [pltpu.make_async_copy(s, d, local_sems.at[i]) for i, (s, d) in enumerate(local)]
        for cp in remote + copies:
            cp.start()
        for cp in remote:
            cp.wait_recv()
        for cp in remote:
            cp.wait_send()
        for cp in copies:
            cp.wait()

    return pl.pallas_call(
        body, name=name,
        out_shape=[jax.ShapeDtypeStruct(_land_shape(k, a), a.dtype) for k, a in zip(kinds, arrays)],
        in_specs=[pl.BlockSpec(memory_space=pl.ANY)] * n, out_specs=[pl.BlockSpec(memory_space=pl.ANY)] * n,
        scratch_shapes=[pltpu.SemaphoreType.DMA((n * (N_DEV - 1),)), pltpu.SemaphoreType.DMA((n * (N_DEV - 1),)),
                        pltpu.SemaphoreType.DMA((n,))],
    )(*arrays)


ADAM_ROWS = 128


def _adam_math(g, w, m, v):
    nm = ADAM_B1 * m + (1.0 - ADAM_B1) * g
    nv = ADAM_B2 * v + (1.0 - ADAM_B2) * (g * g)
    m_hat = nm / (1.0 - ADAM_B1 ** ADAM_STEP)
    v_hat = nv / (1.0 - ADAM_B2 ** ADAM_STEP)
    return -ADAM_LR * (m_hat / (jnp.sqrt(v_hat) + ADAM_EPS) + ADAM_WD * w), nm, nv


def _sum_parts(p_ref):
    g = p_ref[0].astype(F32)
    for s in range(1, N_DEV):
        g = g + p_ref[s].astype(F32)
    return g


def _adam_matrix(parts, w, m, v, name):
    _, r, c = w.shape
    rb = ADAM_ROWS if r % ADAM_ROWS == 0 else r

    def body(p_ref, w_ref, m_ref, v_ref, g_ref, d_ref, nm_ref, nv_ref):
        g = _sum_parts(p_ref)
        g_ref[0] = g
        d_ref[0], nm_ref[0], nv_ref[0] = _adam_math(g, w_ref[0], m_ref[0], v_ref[0])

    blk = pl.BlockSpec((1, rb, c), lambda i: (0, i, 0))
    return pl.pallas_call(
        body, name=name, grid=(r // rb,),
        in_specs=[pl.BlockSpec((N_DEV, rb, c), lambda i: (0, i, 0)), blk, blk, blk],
        out_specs=[blk] * 4, out_shape=[jax.ShapeDtypeStruct(w.shape, F32)] * 4,
        compiler_params=_params(("arbitrary",)),
    )(parts, w, m, v)


def _adam_vectors(parts, ws, ms, vs):
    nv = len(ws)

    def body(*refs):
        p_ref = refs[0]
        w_refs, m_refs, v_refs = refs[1:1 + nv], refs[1 + nv:1 + 2 * nv], refs[1 + 2 * nv:1 + 3 * nv]
        outs = refs[1 + 3 * nv:]
        g_all = _sum_parts(p_ref)
        for i in range(nv):
            n = w_refs[i].shape[1]
            g = g_all[i:i + 1, 0:n]
            d, nm, nvv = _adam_math(g, w_refs[i][...], m_refs[i][...], v_refs[i][...])
            outs[i][...] = g
            outs[nv + i][...] = d
            outs[2 * nv + i][...] = nm
            outs[3 * nv + i][...] = nvv
        outs[4 * nv][...] = g_all[LOSS_ROW:LOSS_ROW + 1, 0:1]

    shapes = [jax.ShapeDtypeStruct(a.shape, F32) for a in ws]
    out = pl.pallas_call(body, name="adam_vectors", out_shape=shapes * 4 + [jax.ShapeDtypeStruct((1, 1), F32)],
                         compiler_params=_params())(parts, *ws, *ms, *vs)
    return out[:nv], out[nv:2 * nv], out[2 * nv:3 * nv], out[3 * nv:4 * nv], out[4 * nv]


MATRICES = (("w_in", 1), ("gdn_conv_w", 1), ("w_out", 0), ("w_ffn_gate", 1), ("w_ffn_up", 1), ("w_ffn_down", 0))
WEIGHTS = ("norm1_w", "w_in", "gdn_conv_w", "gdn_A_log", "gdn_dt_bias", "gdn_out_norm_w", "fox_f_bias", "fox_q_norm_w",
           "fox_k_norm_w", "w_out", "norm2_w", "w_ffn_gate", "w_ffn_up", "w_ffn_down", "final_norm_w")


def _join(blocks, axis):
    _, r, c = blocks.shape
    if axis == 0:
        return blocks.reshape(N_DEV * r, c)
    return blocks.transpose(1, 0, 2).reshape(r, N_DEV * c)


def _cut(full, axis):
    r, c = full.shape
    if axis == 0:
        return full.reshape(N_DEV, r // N_DEV, c)
    return full.reshape(r, N_DEV, c // N_DEV).transpose(1, 0, 2)


def kernel(x, norm1_w, w_in, gdn_conv_w, gdn_A_log, gdn_dt_bias, gdn_out_norm_w, fox_f_bias, fox_q_norm_w, fox_k_norm_w, w_out, norm2_w, w_ffn_gate, w_ffn_up, w_ffn_down, final_norm_w, loss_target, m_norm1_w, m_w_in, m_gdn_conv_w, m_gdn_A_log, m_gdn_dt_bias, m_gdn_out_norm_w, m_fox_f_bias, m_fox_q_norm_w, m_fox_k_norm_w, m_w_out, m_norm2_w, m_w_ffn_gate, m_w_ffn_up, m_w_ffn_down, m_final_norm_w, v_norm1_w, v_w_in, v_gdn_conv_w, v_gdn_A_log, v_gdn_dt_bias, v_gdn_out_norm_w, v_fox_f_bias, v_fox_q_norm_w, v_fox_k_norm_w, v_w_out, v_norm2_w, v_w_ffn_gate, v_w_ffn_up, v_w_ffn_down, v_final_norm_w):
    w = dict(norm1_w=norm1_w, w_in=w_in, gdn_conv_w=gdn_conv_w, gdn_A_log=gdn_A_log, gdn_dt_bias=gdn_dt_bias,
             gdn_out_norm_w=gdn_out_norm_w, fox_f_bias=fox_f_bias, fox_q_norm_w=fox_q_norm_w, fox_k_norm_w=fox_k_norm_w,
             w_out=w_out, norm2_w=norm2_w, w_ffn_gate=w_ffn_gate, w_ffn_up=w_ffn_up, w_ffn_down=w_ffn_down,
             final_norm_w=final_norm_w)
    m = dict(norm1_w=m_norm1_w, w_in=m_w_in, gdn_conv_w=m_gdn_conv_w, gdn_A_log=m_gdn_A_log, gdn_dt_bias=m_gdn_dt_bias,
             gdn_out_norm_w=m_gdn_out_norm_w, fox_f_bias=m_fox_f_bias, fox_q_norm_w=m_fox_q_norm_w,
             fox_k_norm_w=m_fox_k_norm_w, w_out=m_w_out, norm2_w=m_norm2_w, w_ffn_gate=m_w_ffn_gate,
             w_ffn_up=m_w_ffn_up, w_ffn_down=m_w_ffn_down, final_norm_w=m_final_norm_w)
    v = dict(norm1_w=v_norm1_w, w_in=v_w_in, gdn_conv_w=v_gdn_conv_w, gdn_A_log=v_gdn_A_log, gdn_dt_bias=v_gdn_dt_bias,
             gdn_out_norm_w=v_gdn_out_norm_w, fox_f_bias=v_fox_f_bias, fox_q_norm_w=v_fox_q_norm_w,
             fox_k_norm_w=v_fox_k_norm_w, w_out=v_w_out, norm2_w=v_norm2_w, w_ffn_gate=v_w_ffn_gate,
             w_ffn_up=v_w_ffn_up, w_ffn_down=v_w_ffn_down, final_norm_w=v_final_norm_w)
    axis_of = dict(MATRICES)
    late = ("w_out", "w_ffn_gate", "w_ffn_up", "w_ffn_down")
    xs, tgt = x[0], loss_target[0]
    vp = _vector_params({n: w[n] for n in VECTORS})

    (st_in, st_late), token = _spread_start(
        [[w["w_in"][0].astype(WIRE), w["gdn_conv_w"][0]], [w[n][0].astype(WIRE) for n in late]], "gather", "gather_start")
    w_in_blocks, conv_blocks = _spread_wait(st_in, "gather", token, "gather_wait_in")
    w_main, w_small = _split_w_in(_join(w_in_blocks, 1))
    conv_w = _join(conv_blocks, 1)
    f = _mixer_forward(xs, vp, w_main, w_small, conv_w)
    full = {n: _join(b, axis_of[n]) for n, b in zip(late, _spread_wait(st_late, "gather", f["mixf"], "gather_wait_late"))}

    (h2, act, dgate, dup, dx3, dx2, dmixg, dmixf, dn2w, dfw, loss) = _tail(
        xs, f["mixg"], f["mixf"], tgt, full["w_out"], vp["n2w"], full["w_ffn_gate"], full["w_ffn_up"],
        full["w_ffn_down"], vp["fw"])
    dlate = {"w_out": jnp.concatenate([_wgrad(f["mixg"], dx2, "dw_out_gdn"), _wgrad(f["mixf"], dx2, "dw_out_fox")], axis=0),
             "w_ffn_gate": _wgrad(h2, dgate, "dw_gate"), "w_ffn_up": _wgrad(h2, dup, "dw_up"),
             "w_ffn_down": _wgrad(act, dx3, "dw_down")}

    (st_grads,), token = _spread_start([[_cut(dlate[n], axis_of[n]) for n in late]], "scatter", "grads_start")
    grad_x, dw_in, dconv, small = _mixer_backward(xs, vp, w_main, w_small, conv_w, f, dx2, dmixg, dmixf,
                                                  vp["onw"] + token[0:1, 0:DH])
    vec = _pack_vectors(small["dn1w"], dn2w, dfw, small["gsum"], small["donw"], small["dfnw"], loss)
    parts_in, parts_conv, parts_vec = _exchange([_cut(dw_in, 1), _cut(dconv, 1)], [vec], "exchange_last")
    parts = dict(zip(late, _spread_wait(st_grads, "scatter", parts_vec, "grads_wait")))
    parts["w_in"], parts["gdn_conv_w"] = parts_in, parts_conv

    results = [{}, {}, {}, {}]
    for n, _ in MATRICES:
        for d, a in zip(results, _adam_matrix(parts[n], w[n], m[n], v[n], "adam_" + n)):
            d[n] = a
    row = lambda a: a.reshape(1, -1)
    *vec_out, total_loss = _adam_vectors(parts_vec, [row(w[n]) for n in VECTORS], [row(m[n]) for n in VECTORS],
                                         [row(v[n]) for n in VECTORS])
    for d, arrs in zip(results, vec_out):
        for n, a in zip(VECTORS, arrs):
            d[n] = a.reshape(w[n].shape)
    return (total_loss[0, 0], grad_x[None], *[d[n] for d in results for n in WEIGHTS])
```

```python
import functools

import jax
import jax.numpy as jnp
from jax import lax
from jax.experimental import pallas as pl
from jax.experimental.pallas import tpu as pltpu
from jax.experimental.pallas import tpu_sc as plsc

F32 = jnp.float32
MXU = jnp.bfloat16
WIRE = jnp.bfloat16
HI = lax.Precision.HIGHEST
EPS = 1e-6

N_DEV = 8
HEADS = 8
DH = 64
WIDTH = HEADS * DH
CHUNK = 64
LANE = 128
ROW_ALIGN = 16
TB = 256
QB = 256
VMEM_LIMIT = 60 * 1024 * 1024

ADAM_LR = 0.001
ADAM_B1 = 0.9
ADAM_B2 = 0.999
ADAM_EPS = 1e-08
ADAM_WD = 0.01
ADAM_STEP = 10

MESH = pl.DeviceIdType.MESH


def _params(sem=None):
    return pltpu.CompilerParams(dimension_semantics=sem, vmem_limit_bytes=VMEM_LIMIT)


def _resident(shape):
    n = len(shape)
    return pl.BlockSpec(shape, lambda *_: (0,) * n, pipeline_mode=pl.Buffered(1))


def _dot(a, b):
    return jnp.dot(a.astype(MXU), b.astype(MXU), preferred_element_type=F32)


def _dot_nt(a, b):
    return lax.dot_general(a.astype(MXU), b.astype(MXU), (((1,), (1,)), ((), ())), preferred_element_type=F32)


def _dot_tn(a, b):
    return lax.dot_general(a.astype(MXU), b.astype(MXU), (((0,), (0,)), ((), ())), preferred_element_type=F32)


def _hdot(a, b):
    return jnp.dot(a, b, precision=HI, preferred_element_type=F32)


def _hdot_nt(a, b):
    return lax.dot_general(a, b, (((1,), (1,)), ((), ())), precision=HI, preferred_element_type=F32)


def _hdot_tn(a, b):
    return lax.dot_general(a, b, (((0,), (0,)), ((), ())), precision=HI, preferred_element_type=F32)


def _sigmoid(x):
    return 0.5 * jnp.tanh(0.5 * x) + 0.5


def _softplus(x):
    return jnp.maximum(x, 0.0) + jnp.log(1.0 + jnp.exp(-jnp.abs(x)))


def _head_sum_matrix():
    ri = lax.broadcasted_iota(jnp.int32, (LANE, LANE), 0) // DH
    ci = lax.broadcasted_iota(jnp.int32, (LANE, LANE), 1) // DH
    return (ri == ci).astype(F32)


def _group_sum(a, ones_matrix):
    hi = a.astype(jnp.bfloat16)
    lo = (a - hi.astype(F32)).astype(jnp.bfloat16)
    m = ones_matrix.astype(jnp.bfloat16)
    return jnp.dot(hi, m, preferred_element_type=F32) + jnp.dot(lo, m, preferred_element_type=F32)


def _shift_down(x, s):
    return pltpu.roll(x, s, 0)


def _shift_up(x, s):
    return pltpu.roll(x, x.shape[0] - s, 0)


def _inproj(x, n1w, w_main, w_small):
    t, d = x.shape
    nm = w_main.shape[1]
    tb = min(TB, t)

    def body(x_ref, nw_ref, wm_ref, ws_ref, h_ref, pm_ref, ps_ref):
        xv = x_ref[...]
        r = lax.rsqrt(jnp.mean(xv * xv, axis=-1, keepdims=True) + EPS)
        h = (xv * r * nw_ref[...]).astype(MXU)
        h_ref[...] = h
        pm_ref[...] = jnp.dot(h, wm_ref[...], preferred_element_type=F32)
        ps_ref[...] = jnp.dot(h, ws_ref[...], preferred_element_type=F32)

    return pl.pallas_call(
        body, name="inproj", grid=(t // tb,),
        in_specs=[pl.BlockSpec((tb, d), lambda i: (i, 0)), _resident((1, d)), _resident((d, nm)), _resident((d, LANE))],
        out_specs=[pl.BlockSpec((tb, d), lambda i: (i, 0)), pl.BlockSpec((tb, nm), lambda i: (i, 0)),
                   pl.BlockSpec((tb, LANE), lambda i: (i, 0))],
        out_shape=[jax.ShapeDtypeStruct((t, d), MXU), jax.ShapeDtypeStruct((t, nm), F32),
                   jax.ShapeDtypeStruct((t, LANE), F32)],
        compiler_params=_params(("arbitrary",)),
    )(x, n1w, w_main, w_small)


def _inproj_bwd(x, n1w, dx2, dgdn, dz, dfox, dfg, dps, w_main, w_small):
    t, d = x.shape
    tb = min(TB, t)
    w3 = 3 * WIDTH

    def body(x_ref, nw_ref, dx2_ref, dgdn_ref, dz_ref, dfox_ref, dfg_ref, dps_ref, wm_ref, ws_ref, gx_ref, dnw_ref):
        dh = _dot_nt(dgdn_ref[...], wm_ref[:, 0:w3])
        dh += _dot_nt(dz_ref[...], wm_ref[:, w3:w3 + WIDTH])
        dh += _dot_nt(dfox_ref[...], wm_ref[:, w3 + WIDTH:2 * w3 + WIDTH])
        dh += _dot_nt(dfg_ref[...], wm_ref[:, 2 * w3 + WIDTH:2 * w3 + 2 * WIDTH])
        dh += _dot_nt(dps_ref[...], ws_ref[...])
        xv = x_ref[...]
        r = lax.rsqrt(jnp.mean(xv * xv, axis=-1, keepdims=True) + EPS)
        xn = xv * r

        @pl.when(pl.program_id(0) == 0)
        def _():
            dnw_ref[...] = jnp.zeros_like(dnw_ref)

        dnw_ref[...] += jnp.sum(dh * xn, axis=0, keepdims=True)
        g = dh * nw_ref[...]
        gx_ref[...] = dx2_ref[...] + r * (g - xn * jnp.mean(g * xn, axis=-1, keepdims=True))

    def tok(n):
        return pl.BlockSpec((tb, n), lambda i: (i, 0))

    return pl.pallas_call(
        body, name="inproj_bwd", grid=(t // tb,),
        in_specs=[tok(d), _resident((1, d)), tok(d), tok(w3), tok(WIDTH), tok(w3), tok(WIDTH), tok(LANE),
                  _resident(w_main.shape), _resident(w_small.shape)],
        out_specs=[tok(d), pl.BlockSpec((1, d), lambda i: (0, 0))],
        out_shape=[jax.ShapeDtypeStruct((t, d), F32), jax.ShapeDtypeStruct((1, d), F32)],
        compiler_params=_params(("arbitrary",)),
    )(x, n1w, dx2, dgdn, dz, dfox, dfg, dps, w_main, w_small)


def _gate_lanes(shape):
    lane = lax.broadcasted_iota(jnp.int32, shape, 1)
    return lane < HEADS, (lane >= HEADS) & (lane < 2 * HEADS), (lane >= 2 * HEADS) & (lane < 3 * HEADS)


def _block_masks():
    ri = lax.broadcasted_iota(jnp.int32, (LANE, LANE), 0)
    ci = lax.broadcasted_iota(jnp.int32, (LANE, LANE), 1)
    same = (ri // CHUNK) == (ci // CHUNK)
    return ((ri >= ci).astype(F32), (ri <= ci).astype(F32), (same & (ri >= ci)).astype(F32),
            (same & (ri <= ci)).astype(F32), same.astype(F32))


def _gates(ps, gparams):
    t = ps.shape[0]
    nb = t // LANE

    def body(ps_ref, gp_ref, out_ref, run_ref, tot_ref, runt_ref):
        p = ps_ref[...]
        is_b, is_a, is_f = _gate_lanes(p.shape)
        z = p + gp_ref[0:1, :]
        neg_exp_a = -jnp.exp(gp_ref[1:2, :])
        glog = neg_exp_a * _softplus(z)
        logf = -_softplus(-z)
        out_ref[...] = jnp.where(is_b, _sigmoid(p), jnp.where(is_a, glog, jnp.where(is_f, logf, 0.0)))
        tril, _, tril_c, _, same_c = _block_masks()
        off = jnp.zeros((1, LANE), F32)
        for b in range(nb):
            rows = slice(b * LANE, (b + 1) * LANE)
            blk = out_ref[rows, :]
            ga = jnp.where(is_a[:LANE], blk, 0.0)
            fb = _hdot(tril, jnp.where(is_f[:LANE], blk, 0.0)) + off
            run = fb + _hdot(tril_c, ga)
            run_ref[rows, :] = run
            runt_ref[:, rows] = run.T
            tot_ref[rows, :] = _hdot(same_c, ga)
            off = fb[LANE - 1:LANE, :]

    return pl.pallas_call(
        body, name="gates",
        out_shape=[jax.ShapeDtypeStruct((t, LANE), F32)] * 3 + [jax.ShapeDtypeStruct((LANE, t), F32)],
        compiler_params=_params(),
    )(ps, gparams)


def _gates_bwd(ps, gparams, dcol_g, dtot_g, dcol_f, drow_g, drow_f):
    t = ps.shape[0]
    nb = t // LANE

    def body(ps_ref, gp_ref, dcg_ref, dtg_ref, dcf_ref, drg_ref, drf_ref, dps_ref, sums_ref, dl_ref, d0_ref, tr_ref):
        p = ps_ref[...]
        is_b, is_a, is_f = _gate_lanes(p.shape)
        _, triu, _, triu_c, same_c = _block_masks()
        tr_ref[...] = jnp.zeros_like(tr_ref)
        off = jnp.zeros((1, LANE), F32)
        for b in reversed(range(nb)):
            rows = slice(b * LANE, (b + 1) * LANE)
            tr_ref[HEADS:2 * HEADS, :] = drg_ref[:, rows]
            tr_ref[2 * HEADS:3 * HEADS, :] = drf_ref[:, rows]
            d = dcg_ref[rows, :] + dcf_ref[rows, :] + tr_ref[...].T
            d0_ref[rows, :] = d
            dlf = _hdot(triu, jnp.where(is_f[:LANE], d, 0.0)) + off
            dla = (_hdot(triu_c, jnp.where(is_a[:LANE], d, 0.0))
                   + _hdot(same_c, jnp.where(is_a[:LANE], dtg_ref[rows, :], 0.0)))
            dl_ref[rows, :] = dlf + dla
            off = dlf[0:1, :]
        z = p + gp_ref[0:1, :]
        neg_exp_a = -jnp.exp(gp_ref[1:2, :])
        sb = _sigmoid(p)
        glog = neg_exp_a * _softplus(z)
        dl = dl_ref[...]
        dp = jnp.where(is_b, d0_ref[...] * sb * (1.0 - sb),
                       jnp.where(is_a, dl * neg_exp_a * _sigmoid(z), jnp.where(is_f, dl * _sigmoid(-z), 0.0)))
        dps_ref[...] = dp
        s_a = jnp.sum(jnp.where(is_a, dl * glog, 0.0), axis=0, keepdims=True)
        s_p = jnp.sum(jnp.where(is_b, 0.0, dp), axis=0, keepdims=True)
        row = lax.broadcasted_iota(jnp.int32, (8, LANE), 0)
        from_a = pltpu.roll(jnp.where(row == 0, s_a, jnp.where(row == 1, s_p, 0.0)), LANE - HEADS, 1)
        from_f = pltpu.roll(jnp.where(row == 2, s_p, 0.0), LANE - 2 * HEADS, 1)
        lane = lax.broadcasted_iota(jnp.int32, (8, LANE), 1)
        sums_ref[...] = jnp.where(lane < HEADS, from_a + from_f, 0.0)

    return pl.pallas_call(
        body, name="gates_bwd",
        out_shape=[jax.ShapeDtypeStruct((t, LANE), F32), jax.ShapeDtypeStruct((8, LANE), F32)],
        scratch_shapes=[pltpu.VMEM((t, LANE), F32), pltpu.VMEM((t, LANE), F32), pltpu.VMEM((LANE, LANE), F32)],
        compiler_params=_params(),
    )(ps, gparams, dcol_g, dtot_g, dcol_f, drow_g, drow_f)


def _conv(xv, w):
    acc = w[3:4, :] * xv
    for s in range(1, 4):
        acc += w[3 - s:4 - s, :] * _shift_down(xv, s)
    return acc


PREP_ROWS = 256
HALO = 8


def _tile_rows(r):
    return pl.ds(pl.multiple_of(r * PREP_ROWS, PREP_ROWS), PREP_ROWS)


def _gdn_prep(pm, conv_w):
    t = pm.shape[0]
    nj = WIDTH // LANE
    win = PREP_ROWS + HALO

    def body(x_ref, w_ref, o_ref, xp_ref):
        kind = pl.program_id(0)
        xp_ref[0:HALO, :] = jnp.zeros((HALO, LANE), F32)
        xp_ref[HALO:, :] = x_ref[...]
        w = w_ref[...]
        hs = _head_sum_matrix()

        def tile(r, _):
            xw = xp_ref[pl.ds(pl.multiple_of(r * PREP_ROWS, PREP_ROWS), win), :]
            acc = _conv(xw, w)[HALO:]
            y = acc * _sigmoid(acc)
            out = jnp.where(kind < 2, y * lax.rsqrt(_group_sum(y * y, hs) + EPS), y)
            o_ref[0, 0, _tile_rows(r), :] = out[:, :DH]
            o_ref[0, 1, _tile_rows(r), :] = out[:, DH:]
            return 0

        lax.fori_loop(0, t // PREP_ROWS, tile, 0)

    return pl.pallas_call(
        body, name="gdn_prep", grid=(3, nj),
        in_specs=[pl.BlockSpec((t, LANE), lambda i, j: (0, i * nj + j)),
                  pl.BlockSpec((4, LANE), lambda i, j: (0, i * nj + j))],
        out_specs=pl.BlockSpec((1, 2, t, DH), lambda i, j: (i, j, 0, 0)),
        out_shape=jax.ShapeDtypeStruct((3, HEADS, t, DH), F32),
        scratch_shapes=[pltpu.VMEM((t + HALO, LANE), F32)],
        compiler_params=_params(("arbitrary", "arbitrary")),
    )(pm, conv_w)


def _gdn_prep_bwd(pm, conv_w, dqkv):
    t = pm.shape[0]
    nj = WIDTH // LANE
    win = PREP_ROWS + 2 * HALO

    def body(x_ref, w_ref, d_ref, dx_ref, dw_ref, xp_ref, dp_ref):
        kind = pl.program_id(0)
        zeros = jnp.zeros((HALO, LANE), F32)
        for ref in (xp_ref, dp_ref):
            ref[0:HALO, :] = zeros
            ref[HALO + t:, :] = zeros
        xp_ref[HALO:HALO + t, :] = x_ref[...]
        dp_ref[HALO:HALO + t, 0:DH] = d_ref[0, 0]
        dp_ref[HALO:HALO + t, DH:] = d_ref[0, 1]
        w = w_ref[...]
        hs = _head_sum_matrix()
        rows = lax.broadcasted_iota(jnp.int32, (win, LANE), 0)
        in_tile = (rows >= HALO) & (rows < HALO + PREP_ROWS)

        def tile(r, dw):
            start = pl.multiple_of(r * PREP_ROWS, PREP_ROWS)
            xw = xp_ref[pl.ds(start, win), :]
            d = dp_ref[pl.ds(start, win), :]
            acc = _conv(xw, w)
            sg = _sigmoid(acc)
            y = acc * sg
            rn = lax.rsqrt(_group_sum(y * y, hs) + EPS)
            yn = y * rn
            dy = jnp.where(kind < 2, rn * (d - yn * _group_sum(d * yn, hs)), d)
            dacc = dy * sg * (1.0 + acc * (1.0 - sg))
            dx = w[3:4, :] * dacc
            for s in range(1, 4):
                dx += w[3 - s:4 - s, :] * _shift_up(dacc, s)
            dx_ref[_tile_rows(r), :] = dx[HALO:HALO + PREP_ROWS]
            dm = jnp.where(in_tile, dacc, 0.0)
            return tuple(dw[i] + jnp.sum(dm * (xw if i == 3 else _shift_down(xw, 3 - i)), axis=0, keepdims=True)
                         for i in range(4))

        dw = lax.fori_loop(0, t // PREP_ROWS, tile, tuple(jnp.zeros((1, LANE), F32) for _ in range(4)))
        for i in range(4):
            dw_ref[i:i + 1, :] = dw[i]

    return pl.pallas_call(
        body, name="gdn_prep_bwd", grid=(3, nj),
        in_specs=[pl.BlockSpec((t, LANE), lambda i, j: (0, i * nj + j)),
                  pl.BlockSpec((4, LANE), lambda i, j: (0, i * nj + j)),
                  pl.BlockSpec((1, 2, t, DH), lambda i, j: (i, j, 0, 0))],
        out_specs=[pl.BlockSpec((t, LANE), lambda i, j: (0, i * nj + j)),
                   pl.BlockSpec((4, LANE), lambda i, j: (0, i * nj + j))],
        out_shape=[jax.ShapeDtypeStruct((t, 3 * WIDTH), F32), jax.ShapeDtypeStruct((4, 3 * WIDTH), F32)],
        scratch_shapes=[pltpu.VMEM((t + 2 * HALO, LANE), F32), pltpu.VMEM((t + 2 * HALO, LANE), F32)],
        compiler_params=_params(("arbitrary", "arbitrary")),
    )(pm, conv_w, dqkv)


FOX_COL0 = 4 * WIDTH // LANE


def _fox_prep(pm, nw):
    t = pm.shape[0]
    nj = WIDTH // LANE

    def body(x_ref, w_ref, o_ref):
        kind = pl.program_id(0)
        hs = _head_sum_matrix()
        wk = w_ref[pl.ds(kind, 1), :]

        def tile(r, _):
            xv = x_ref[_tile_rows(r), :]
            ms = _group_sum(xv * xv, hs) * (1.0 / DH)
            out = jnp.where(kind < 2, xv * lax.rsqrt(ms + EPS) * wk, xv)
            o_ref[0, 0, _tile_rows(r), :] = out[:, :DH]
            o_ref[0, 1, _tile_rows(r), :] = out[:, DH:]
            return 0

        lax.fori_loop(0, t // PREP_ROWS, tile, 0)

    return pl.pallas_call(
        body, name="fox_prep", grid=(3, nj),
        in_specs=[pl.BlockSpec((t, LANE), lambda i, j: (0, FOX_COL0 + i * nj + j)),
                  pl.BlockSpec((3, LANE), lambda i, j: (0, 0))],
        out_specs=pl.BlockSpec((1, 2, t, DH), lambda i, j: (i, j, 0, 0)),
        out_shape=jax.ShapeDtypeStruct((3, HEADS, t, DH), F32),
        compiler_params=_params(("arbitrary", "arbitrary")),
    )(pm, nw)


def _fox_prep_bwd(pm, nw, dqkv):
    t = pm.shape[0]
    nj = WIDTH // LANE

    def body(x_ref, w_ref, d_ref, dx_ref, dw_ref):
        kind = pl.program_id(0)
        hs = _head_sum_matrix()
        wk = w_ref[pl.ds(kind, 1), :]

        def tile(r, dw):
            xv = x_ref[_tile_rows(r), :]
            rn = lax.rsqrt(_group_sum(xv * xv, hs) * (1.0 / DH) + EPS)
            xn = xv * rn
            d = jnp.concatenate([d_ref[0, 0, _tile_rows(r), :], d_ref[0, 1, _tile_rows(r), :]], axis=1)
            g = d * wk
            dxn = rn * (g - xn * _group_sum(g * xn, hs) * (1.0 / DH))
            dx_ref[_tile_rows(r), :] = jnp.where(kind < 2, dxn, d)
            return dw + jnp.sum(d * xn, axis=0, keepdims=True)

        dw_ref[0, 0] = lax.fori_loop(0, t // PREP_ROWS, tile, jnp.zeros((1, LANE), F32))

    return pl.pallas_call(
        body, name="fox_prep_bwd", grid=(3, nj),
        in_specs=[pl.BlockSpec((t, LANE), lambda i, j: (0, FOX_COL0 + i * nj + j)),
                  pl.BlockSpec((3, LANE), lambda i, j: (0, 0)),
                  pl.BlockSpec((1, 2, t, DH), lambda i, j: (i, j, 0, 0))],
        out_specs=[pl.BlockSpec((t, LANE), lambda i, j: (0, i * nj + j)),
                   pl.BlockSpec((1, 1, 1, LANE), lambda i, j: (i, j, 0, 0))],
        out_shape=[jax.ShapeDtypeStruct((t, 3 * WIDTH), F32), jax.ShapeDtypeStruct((3, nj, 1, LANE), F32)],
        compiler_params=_params(("arbitrary", "arbitrary")),
    )(pm, nw, dqkv)


SC = 256
CPS = SC // CHUNK
Q_SCALE = DH ** -0.5


def _sc_masks():
    ri = lax.broadcasted_iota(jnp.int32, (SC, SC), 0)
    ci = lax.broadcasted_iota(jnp.int32, (SC, SC), 1)
    same = (ri // CHUNK) == (ci // CHUNK)
    return same & (ri >= ci), same & (ri > ci), ri == ci


def _unit_lower_inverse(m, eye):
    inv = jnp.where(eye, 1.0, 0.0) + m
    for _ in range(5):
        m = _dot(m, m)
        inv = inv + _dot(inv, m)
    return inv


def _lane_col(blk, lane_idx):
    lane = lax.broadcasted_iota(jnp.int32, blk.shape, 1)
    return jnp.sum(jnp.where(lane == lane_idx, blk, 0.0), axis=1, keepdims=True)


def _to_lane(col, lane_idx):
    lane = lax.broadcasted_iota(jnp.int32, (col.shape[0], LANE), 1)
    return jnp.where(lane == lane_idx, col, 0.0)


def _gdn_columns(gates_ref, run_ref, tot_ref, runt_ref, rows, h):
    return (_lane_col(gates_ref[rows, :], h), _lane_col(run_ref[rows, :], HEADS + h),
            _lane_col(tot_ref[rows, :], HEADS + h), runt_ref[pl.ds(h, 1), rows])


def _gdn_local(q, k, beta, gc, gl, grow, causal):
    decay = jnp.exp(jnp.where(causal, gc - grow, -1e30))
    egc = jnp.exp(gc)
    ekd = jnp.exp(gl - gc)
    qs = q * Q_SCALE
    kb = k * beta
    kk = _dot_nt(kb, k)
    qk = _dot_nt(qs, k)
    return beta, gl, decay, egc, ekd, qs, kb, kk, qk, jnp.where(causal, qk * decay, 0.0)


def _chunk_rows(c):
    return pl.ds(pl.multiple_of(c * CHUNK, CHUNK), CHUNK)


def _sc_rows(b):
    return pl.ds(pl.multiple_of(b * SC, SC), SC)


def _gdn_fwd(qkv, gates, run, tot, run_t):
    t = qkv.shape[2]
    nc = t // CHUNK
    nsc = t // SC

    def body(qkv_ref, gates_ref, run_ref, tot_ref, runt_ref, o_ref, st_ref, inv_ref, kc_s, qc_s, g_s, au_s):
        h = pl.program_id(0)
        causal, strict, eye = _sc_masks()

        def local(b, _):
            rows = _sc_rows(b)
            q, k, v = qkv_ref[0, 0, rows, :], qkv_ref[1, 0, rows, :], qkv_ref[2, 0, rows, :]
            beta, _, decay, egc, ekd, qs, kb, kk, _, attn = _gdn_local(
                q, k, *_gdn_columns(gates_ref, run_ref, tot_ref, runt_ref, rows, h), causal)
            inv = _unit_lower_inverse(-jnp.where(strict, kk * decay, 0.0), eye)
            inv_ref[0, rows, :] = inv.astype(inv_ref.dtype)
            u = _dot(inv, v * beta)
            w = _dot(inv, kb * egc)
            g_s[rows, :] = qs * egc - _dot(attn, w)
            au_s[rows, :] = _dot(attn, u)
            kd = k * ekd
            for j in range(CPS):
                sl = slice(j * CHUNK, (j + 1) * CHUNK)
                kc_s[b * CPS + j] = _dot_tn(kd[sl], w[sl])
                qc_s[b * CPS + j] = _dot_tn(kd[sl], u[sl])
            return 0

        lax.fori_loop(0, nsc, local, 0)

        def step(c, s):
            rows = _chunk_rows(c)
            st_ref[0, c] = s
            o_ref[0, rows, :] = _dot(g_s[rows, :], s) + au_s[rows, :]
            egl = jnp.exp(_lane_col(tot_ref[pl.ds(c * CHUNK, 1), :], HEADS + h))
            return egl * s - _dot(kc_s[c], s) + qc_s[c]

        lax.fori_loop(0, nc, step, jnp.zeros((DH, DH), F32))

    whole = pl.BlockSpec((t, LANE), lambda h: (0, 0))
    return pl.pallas_call(
        body, name="gdn_fwd", grid=(HEADS,),
        in_specs=[pl.BlockSpec((3, 1, t, DH), lambda h: (0, h, 0, 0)), whole, whole, whole,
                  pl.BlockSpec((HEADS, t), lambda h: (1, 0))],
        out_specs=[pl.BlockSpec((1, t, DH), lambda h: (h, 0, 0)), pl.BlockSpec((1, nc, DH, DH), lambda h: (h, 0, 0, 0)),
                   pl.BlockSpec((1, t, SC), lambda h: (h, 0, 0))],
        out_shape=[jax.ShapeDtypeStruct((HEADS, t, DH), F32), jax.ShapeDtypeStruct((HEADS, nc, DH, DH), F32),
                   jax.ShapeDtypeStruct((HEADS, t, SC), MXU)],
        scratch_shapes=[pltpu.VMEM((nc, DH, DH), F32), pltpu.VMEM((nc, DH, DH), F32),
                        pltpu.VMEM((t, DH), F32), pltpu.VMEM((t, DH), F32)],
        compiler_params=_params(("arbitrary",)),
    )(qkv, gates, run, tot, run_t)


def _gdn_bwd(qkv, gates, run, tot, run_t, inv, states, do):
    t = qkv.shape[2]
    nc = t // CHUNK
    nsc = t // SC

    def body(qkv_ref, gates_ref, run_ref, tot_ref, runt_ref, inv_ref, st_ref, do_ref,
             dqkv_ref, dcol_ref, dtot_ref, drow_ref, u_s, w_s, kc_s, h_s, dsn_s):
        h = pl.program_id(0)
        causal, strict, _ = _sc_masks()

        @pl.when(h == 0)
        def _():
            dcol_ref[...] = jnp.zeros_like(dcol_ref)
            dtot_ref[...] = jnp.zeros_like(dtot_ref)

        def local(b, _):
            rows = _sc_rows(b)
            q, k, v = qkv_ref[0, 0, rows, :], qkv_ref[1, 0, rows, :], qkv_ref[2, 0, rows, :]
            beta, _, _, egc, ekd, qs, kb, _, _, attn = _gdn_local(
                q, k, *_gdn_columns(gates_ref, run_ref, tot_ref, runt_ref, rows, h), causal)
            inv_b = inv_ref[0, rows, :]
            u = _dot(inv_b, v * beta)
            w = _dot(inv_b, kb * egc)
            u_s[rows, :] = u
            w_s[rows, :] = w
            g = qs * egc - _dot(attn, w)
            kd = k * ekd
            dout = do_ref[0, rows, :]
            for j in range(CPS):
                sl = slice(j * CHUNK, (j + 1) * CHUNK)
                kc_s[b * CPS + j] = _dot_tn(kd[sl], w[sl])
                h_s[b * CPS + j] = _dot_tn(g[sl], dout[sl])
            return 0

        lax.fori_loop(0, nsc, local, 0)

        def step(i, ds):
            c = nc - 1 - i
            dsn_s[c] = ds
            egl = jnp.exp(_lane_col(tot_ref[pl.ds(c * CHUNK, 1), :], HEADS + h))
            return egl * ds - _dot_tn(kc_s[c], ds) + h_s[c]

        lax.fori_loop(0, nc, step, jnp.zeros((DH, DH), F32))

        def back(b, _):
            rows = _sc_rows(b)
            q, k, v = qkv_ref[0, 0, rows, :], qkv_ref[1, 0, rows, :], qkv_ref[2, 0, rows, :]
            beta, gl, decay, egc, ekd, qs, kb, kk, qk, attn = _gdn_local(
                q, k, *_gdn_columns(gates_ref, run_ref, tot_ref, runt_ref, rows, h), causal)
            u, w = u_s[rows, :], w_s[rows, :]
            kd = k * ekd
            vb = v * beta
            kbe = kb * egc
            dout = do_ref[0, rows, :]
            first = lax.broadcasted_iota(jnp.int32, (CHUNK, 1), 0) == 0
            dg_p, dkd_p, dw_p, du_p, dgl_p = [], [], [], [], []
            for j in range(CPS):
                sl = slice(j * CHUNK, (j + 1) * CHUNK)
                s = st_ref[0, b * CPS + j]
                dsn = dsn_s[b * CPS + j]
                dkc = -_dot_nt(dsn, s)
                dg_p.append(_dot_nt(dout[sl], s))
                dkd_p.append(_dot_nt(w[sl], dkc) + _dot_nt(u[sl], dsn))
                dw_p.append(_dot(kd[sl], dkc))
                du_p.append(_dot(kd[sl], dsn))
                degl = jnp.sum(jnp.sum(s * dsn, axis=1, keepdims=True), axis=0, keepdims=True)
                dgl_p.append(jnp.where(first, degl * jnp.exp(gl[j * CHUNK:j * CHUNK + 1, :]), 0.0))
            dg, dkd = jnp.concatenate(dg_p, axis=0), jnp.concatenate(dkd_p, axis=0)
            da = jnp.where(causal, _dot_nt(dout, u) - _dot_nt(dg, w), 0.0)
            at = _dot_tn(attn, jnp.concatenate([dout, dg], axis=1))
            du = at[:, :DH] + jnp.concatenate(du_p, axis=0)
            dw = jnp.concatenate(dw_p, axis=0) - at[:, DH:]
            inv_t = inv_ref[0, rows, :].astype(F32).T
            it = _dot(inv_t, jnp.concatenate([du, dw], axis=1))
            dvb, dkbe = it[:, :DH], it[:, DH:]
            dinv = _dot_nt(du, vb) + _dot_nt(dw, kbe)
            dl = jnp.where(strict, -_dot(_dot(inv_t, dinv), inv_t), 0.0)
            dlogd = (dl * kk + da * qk) * decay
            dkk = dl * decay
            dqk = da * decay
            dkb = _dot(dkk, k) + dkbe * egc
            dqs = _dot(dqk, k) + dg * egc
            dk = (_dot_tn(jnp.concatenate([dkk, dqk], axis=0), jnp.concatenate([kb, qs], axis=0))
                  + dkd * ekd + dkb * beta)
            dkd_kd = jnp.sum(dkd * kd, axis=1, keepdims=True)
            dgc = (jnp.sum(dlogd, axis=1, keepdims=True) + jnp.sum(dg * qs, axis=1, keepdims=True) * egc
                   + jnp.sum(dkbe * kbe, axis=1, keepdims=True) - dkd_kd)
            dgl = dkd_kd + jnp.concatenate(dgl_p, axis=0)
            dbeta = jnp.sum(dkb * k, axis=1, keepdims=True) + jnp.sum(dvb * v, axis=1, keepdims=True)
            dqkv_ref[0, 0, rows, :] = dqs * Q_SCALE
            dqkv_ref[1, 0, rows, :] = dk
            dqkv_ref[2, 0, rows, :] = dvb * beta
            dcol_ref[rows, :] += _to_lane(dbeta, h) + _to_lane(dgc, HEADS + h)
            dtot_ref[rows, :] += _to_lane(dgl, HEADS + h)
            drow_ref[pl.ds(h, 1), rows] = -jnp.sum(dlogd, axis=0, keepdims=True)
            return 0

        lax.fori_loop(0, nsc, back, 0)

    whole = pl.BlockSpec((t, LANE), lambda h: (0, 0))
    rowspec = pl.BlockSpec((HEADS, t), lambda h: (0, 0))
    sq = pltpu.VMEM((nc, DH, DH), F32)
    return pl.pallas_call(
        body, name="gdn_bwd", grid=(HEADS,),
        in_specs=[pl.BlockSpec((3, 1, t, DH), lambda h: (0, h, 0, 0)), whole, whole, whole,
                  pl.BlockSpec((HEADS, t), lambda h: (1, 0)),
                  pl.BlockSpec((1, t, SC), lambda h: (h, 0, 0)), pl.BlockSpec((1, nc, DH, DH), lambda h: (h, 0, 0, 0)),
                  pl.BlockSpec((1, t, DH), lambda h: (h, 0, 0))],
        out_specs=[pl.BlockSpec((3, 1, t, DH), lambda h: (0, h, 0, 0)), whole, whole, rowspec],
        out_shape=[jax.ShapeDtypeStruct((3, HEADS, t, DH), F32), jax.ShapeDtypeStruct((t, LANE), F32),
                   jax.ShapeDtypeStruct((t, LANE), F32), jax.ShapeDtypeStruct((HEADS, t), F32)],
        scratch_shapes=[pltpu.VMEM((t, DH), F32), pltpu.VMEM((t, DH), F32), sq, sq, sq],
        compiler_params=_params(("arbitrary",)),
    )(qkv, gates, run, tot, run_t, inv, states, do)


FOX_HP = 2


def _wide_t(a):
    r = a.shape[0]
    return jnp.concatenate([a, jnp.zeros((r, LANE - DH), F32)], axis=1).T[:DH]


def _tall_t(a):
    r = a.shape[1]
    return jnp.concatenate([a, jnp.zeros((LANE - DH, r), F32)], axis=0).T[:, :DH]


def _key_side_f(run_blk, head, qb):
    col = jnp.broadcast_to(_lane_col(run_blk, 2 * HEADS + head), (run_blk.shape[0], LANE))
    return jnp.concatenate([col] * (qb // LANE), axis=1)


def _diag_mask(qb):
    return lax.broadcasted_iota(jnp.int32, (qb, qb), 0) <= lax.broadcasted_iota(jnp.int32, (qb, qb), 1)


def _fox_fwd(qkv, run, run_t):
    t = qkv.shape[2]
    qb = min(QB, t)
    nq = t // qb
    hp_n = FOX_HP

    def body(q_ref, k_ref, v_ref, run_ref, runt_ref, o_ref, lse_ref, kb_s, vt_s):
        hp = pl.program_id(0)
        i = pl.program_id(1)

        @pl.when(i == 0)
        def _():
            for hh in range(hp_n):
                kb_s[hh] = k_ref[0, hh].astype(MXU)
                for b in range(nq):
                    rows = slice(b * qb, (b + 1) * qb)
                    vt_s[hh, :, rows] = _wide_t(v_ref[0, hh, rows, :]).astype(MXU)

        qrows = pl.ds(pl.multiple_of(i * qb, qb), qb)
        qs = [(q_ref[0, hh] * Q_SCALE).astype(MXU) for hh in range(hp_n)]
        fq = [runt_ref[pl.ds(hp * hp_n + hh, 1), qrows] for hh in range(hp_n)]

        def block_rows(j):
            return pl.ds(pl.multiple_of(j * qb, qb), qb)

        def scores(j):
            return tuple(_dot_nt(kb_s[hh, block_rows(j), :], qs[hh]) for hh in range(hp_n))

        def absorb(j, raw, state, diagonal):
            rows = block_rows(j)
            run_blk = run_ref[rows, :]
            stats, pv = [], []
            for hh in range(hp_n):
                m, l, _ = state[hh]
                st = raw[hh] + (fq[hh] - _key_side_f(run_blk, hp * hp_n + hh, qb))
                if diagonal:
                    st = jnp.where(_diag_mask(qb), st, -1e30)
                m_new = jnp.maximum(m, jnp.max(st, axis=0, keepdims=True))
                p = jnp.exp(st - m_new)
                alpha = jnp.exp(m - m_new)
                stats.append((m_new, alpha * l + jnp.sum(p, axis=0, keepdims=True), alpha))
                pv.append(_dot(vt_s[hh, :, rows], p))
            return tuple((stats[hh][0], stats[hh][1], stats[hh][2] * state[hh][2] + pv[hh]) for hh in range(hp_n))

        def kstep(j, carry):
            state, raw = carry
            ahead = scores(j + 1)
            return absorb(j, raw, state, False), ahead

        init = tuple((jnp.full((1, qb), -1e30, F32), jnp.zeros((1, qb), F32), jnp.zeros((DH, qb), F32))
                     for _ in range(hp_n))
        state, raw = lax.fori_loop(0, i, kstep, (init, scores(0)))
        state = absorb(i, raw, state, True)
        for hh in range(hp_n):
            m, l, acc = state[hh]
            o_ref[hh] = _tall_t(acc / l)
            lse_ref[pl.ds(hp * hp_n + hh, 1), qrows] = m + jnp.log(l)

    return pl.pallas_call(
        body, name="fox_fwd", grid=(HEADS // hp_n, nq),
        in_specs=[pl.BlockSpec((1, hp_n, qb, DH), lambda h, i: (0, h, i, 0)),
                  pl.BlockSpec((1, hp_n, t, DH), lambda h, i: (1, h, 0, 0)),
                  pl.BlockSpec((1, hp_n, t, DH), lambda h, i: (2, h, 0, 0)),
                  pl.BlockSpec((t, LANE), lambda h, i: (0, 0)),
                  pl.BlockSpec((HEADS, t), lambda h, i: (2, 0))],
        out_specs=[pl.BlockSpec((hp_n, qb, DH), lambda h, i: (h, i, 0)), pl.BlockSpec((HEADS, t), lambda h, i: (0, 0))],
        out_shape=[jax.ShapeDtypeStruct((HEADS, t, DH), F32), jax.ShapeDtypeStruct((HEADS, t), F32)],
        scratch_shapes=[pltpu.VMEM((hp_n, t, DH), MXU), pltpu.VMEM((hp_n, DH, t), MXU)],
        compiler_params=_params(("arbitrary", "arbitrary")),
    )(qkv, qkv, qkv, run, run_t)


def _fox_bwd(qkv, run, run_t, o, lse, do):
    t = qkv.shape[2]
    qb = min(QB, t)
    nq = t // qb
    hp_n = FOX_HP

    def body(q_ref, k_ref, v_ref, run_ref, runt_ref, o_ref, lse_ref, do_ref, dqkv_ref, dcol_ref, drow_ref, dqt_s):
        hp = pl.program_id(0)
        j = pl.program_id(1)

        @pl.when(j == 0)
        def _():
            dqt_s[...] = jnp.zeros_like(dqt_s)

        @pl.when((j == 0) & (hp == 0))
        def _():
            dcol_ref[...] = jnp.zeros_like(dcol_ref)
            drow_ref[...] = jnp.zeros_like(drow_ref)

        krows = pl.ds(pl.multiple_of(j * qb, qb), qb)
        run_blk = run_ref[krows, :]
        ones8 = jnp.ones((8, DH), MXU)
        kb, kt, vb, fk = [], [], [], []
        for hh in range(hp_n):
            kf = k_ref[0, hh]
            kb.append(kf.astype(MXU))
            kt.append(_wide_t(kf).astype(MXU))
            vb.append(v_ref[0, hh].astype(MXU))
            fk.append(_key_side_f(run_blk, hp * hp_n + hh, qb))

        def block_rows(i):
            return pl.ds(pl.multiple_of(i * qb, qb), qb)

        def products(i):
            rows = block_rows(i)
            out = []
            for hh in range(hp_n):
                dout = do_ref[hh, rows, :]
                x = dout * o_ref[hh, rows, :]
                x_hi = x.astype(MXU)
                out.append((_dot_nt(kb[hh], q_ref[0, hh, rows, :] * Q_SCALE), _dot_nt(vb[hh], dout),
                            (_dot_nt(ones8, x_hi) + _dot_nt(ones8, x - x_hi.astype(F32)))[0:1, :]))
            return tuple(out)

        def absorb(i, prods, acc, diagonal):
            rows = block_rows(i)
            pieces = []
            for hh in range(hp_n):
                head = hp * hp_n + hh
                raw, dpt, drow = prods[hh]
                off = runt_ref[pl.ds(head, 1), rows] - lse_ref[pl.ds(head, 1), rows]
                st = raw + (off - fk[hh])
                if diagonal:
                    st = jnp.where(_diag_mask(qb), st, -1e30)
                pt = jnp.exp(st)
                dst = pt * (dpt - drow)
                drow_ref[pl.ds(head, 1), rows] += jnp.sum(dst, axis=0, keepdims=True)
                folded = dst[:, 0:LANE]
                for c in range(1, qb // LANE):
                    folded = folded + dst[:, c * LANE:(c + 1) * LANE]
                pieces.append((_dot(dst, q_ref[0, hh, rows, :] * Q_SCALE), _dot(pt, do_ref[hh, rows, :]),
                               _dot(kt[hh], dst), folded))
            out = []
            for hh in range(hp_n):
                dqt_s[hh, :, rows] += pieces[hh][2]
                out.append((acc[hh][0] + pieces[hh][0], acc[hh][1] + pieces[hh][1], acc[hh][2] + pieces[hh][3]))
            return tuple(out)

        def qstep(i, carry):
            acc, prods = carry
            ahead = products(jnp.minimum(i + 1, nq - 1))
            return absorb(i, prods, acc, False), ahead

        init = tuple((jnp.zeros((qb, DH), F32), jnp.zeros((qb, DH), F32), jnp.zeros((qb, LANE), F32))
                     for _ in range(hp_n))
        first = products(j)
        ahead = products(jnp.minimum(j + 1, nq - 1))
        acc = absorb(j, first, init, True)
        acc, _ = lax.fori_loop(j + 1, nq, qstep, (acc, ahead))
        for hh in range(hp_n):
            dk, dv, ds_sum = acc[hh]
            dqkv_ref[1, hh, krows, :] = dk
            dqkv_ref[2, hh, krows, :] = dv
            dcol_ref[krows, :] += _to_lane(-jnp.sum(ds_sum, axis=1, keepdims=True), 2 * HEADS + hp * hp_n + hh)

        @pl.when(j == nq - 1)
        def _():
            for hh in range(hp_n):
                for b in range(nq):
                    rows = slice(b * qb, (b + 1) * qb)
                    dqkv_ref[0, hh, rows, :] = _tall_t(dqt_s[hh, :, rows]) * Q_SCALE

    full = pl.BlockSpec((hp_n, t, DH), lambda h, j: (h, 0, 0))
    rows8 = pl.BlockSpec((HEADS, t), lambda h, j: (0, 0))
    return pl.pallas_call(
        body, name="fox_bwd", grid=(HEADS // hp_n, nq),
        in_specs=[pl.BlockSpec((1, hp_n, t, DH), lambda h, j: (0, h, 0, 0)),
                  pl.BlockSpec((1, hp_n, qb, DH), lambda h, j: (1, h, j, 0)),
                  pl.BlockSpec((1, hp_n, qb, DH), lambda h, j: (2, h, j, 0)),
                  pl.BlockSpec((t, LANE), lambda h, j: (0, 0)), pl.BlockSpec((HEADS, t), lambda h, j: (2, 0)),
                  full, rows8, full],
        out_specs=[pl.BlockSpec((3, hp_n, t, DH), lambda h, j: (0, h, 0, 0)),
                   pl.BlockSpec((t, LANE), lambda h, j: (0, 0)), rows8],
        out_shape=[jax.ShapeDtypeStruct((3, HEADS, t, DH), F32), jax.ShapeDtypeStruct((t, LANE), F32),
                   jax.ShapeDtypeStruct((HEADS, t), F32)],
        scratch_shapes=[pltpu.VMEM((hp_n, DH, t), F32)],
        compiler_params=_params(("arbitrary", "arbitrary")),
    )(qkv, qkv, qkv, run, run_t, o, lse, do)


Z_COL0 = 3 * WIDTH // LANE
FGATE_COL0 = 7 * WIDTH // LANE


def _gdn_post(o, pm, onw):
    t = pm.shape[0]

    def body(o_ref, z_ref, w_ref, m_ref):
        z = z_ref[...]
        sz = z * _sigmoid(z)
        for hh in range(2):
            ov = o_ref[hh]
            n = ov * lax.rsqrt(jnp.mean(ov * ov, axis=-1, keepdims=True) + EPS) * w_ref[...]
            m_ref[:, hh * DH:(hh + 1) * DH] = (n * sz[:, hh * DH:(hh + 1) * DH]).astype(m_ref.dtype)

    return pl.pallas_call(
        body, name="gdn_post", grid=(WIDTH // LANE,),
        in_specs=[pl.BlockSpec((2, t, DH), lambda j: (j, 0, 0)), pl.BlockSpec((t, LANE), lambda j: (0, Z_COL0 + j)),
                  pl.BlockSpec((1, DH), lambda j: (0, 0))],
        out_specs=pl.BlockSpec((t, LANE), lambda j: (0, j)),
        out_shape=jax.ShapeDtypeStruct((t, WIDTH), MXU),
        compiler_params=_params(("arbitrary",)),
    )(o, pm, onw)


def _gdn_post_bwd(o, pm, onw, dmix):
    t = pm.shape[0]

    def body(o_ref, z_ref, w_ref, dm_ref, do_ref, dz_ref, dw_ref):
        @pl.when(pl.program_id(0) == 0)
        def _():
            dw_ref[...] = jnp.zeros_like(dw_ref)

        z = z_ref[...]
        sg = _sigmoid(z)
        sz = z * sg
        dsz = sg * (1.0 + z * (1.0 - sg))
        dm = dm_ref[...]
        for hh in range(2):
            cols = slice(hh * DH, (hh + 1) * DH)
            ov = o_ref[hh]
            r = lax.rsqrt(jnp.mean(ov * ov, axis=-1, keepdims=True) + EPS)
            xn = ov * r
            dmh = dm[:, cols]
            dn = dmh * sz[:, cols]
            dz_ref[:, cols] = dmh * (xn * w_ref[...]) * dsz[:, cols]
            dw_ref[...] += jnp.sum(dn * xn, axis=0, keepdims=True)
            g = dn * w_ref[...]
            do_ref[hh] = r * (g - xn * jnp.mean(g * xn, axis=-1, keepdims=True))

    return pl.pallas_call(
        body, name="gdn_post_bwd", grid=(WIDTH // LANE,),
        in_specs=[pl.BlockSpec((2, t, DH), lambda j: (j, 0, 0)), pl.BlockSpec((t, LANE), lambda j: (0, Z_COL0 + j)),
                  pl.BlockSpec((1, DH), lambda j: (0, 0)), pl.BlockSpec((t, LANE), lambda j: (0, j))],
        out_specs=[pl.BlockSpec((2, t, DH), lambda j: (j, 0, 0)), pl.BlockSpec((t, LANE), lambda j: (0, j)),
                   pl.BlockSpec((1, DH), lambda j: (0, 0))],
        out_shape=[jax.ShapeDtypeStruct((HEADS, t, DH), F32), jax.ShapeDtypeStruct((t, WIDTH), F32),
                   jax.ShapeDtypeStruct((1, DH), F32)],
        compiler_params=_params(("arbitrary",)),
    )(o, pm, onw, dmix)


def _fox_post(o, pm):
    t = pm.shape[0]

    def body(o_ref, g_ref, m_ref):
        sg = _sigmoid(g_ref[...])
        for hh in range(2):
            cols = slice(hh * DH, (hh + 1) * DH)
            m_ref[:, cols] = (o_ref[hh] * sg[:, cols]).astype(m_ref.dtype)

    return pl.pallas_call(
        body, name="fox_post", grid=(WIDTH // LANE,),
        in_specs=[pl.BlockSpec((2, t, DH), lambda j: (j, 0, 0)), pl.BlockSpec((t, LANE), lambda j: (0, FGATE_COL0 + j))],
        out_specs=pl.BlockSpec((t, LANE), lambda j: (0, j)),
        out_shape=jax.ShapeDtypeStruct((t, WIDTH), MXU),
        compiler_params=_params(("arbitrary",)),
    )(o, pm)


def _fox_post_bwd(o, pm, dmix):
    t = pm.shape[0]

    def body(o_ref, g_ref, dm_ref, do_ref, dg_ref):
        sg = _sigmoid(g_ref[...])
        dm = dm_ref[...]
        for hh in range(2):
            cols = slice(hh * DH, (hh + 1) * DH)
            do_ref[hh] = dm[:, cols] * sg[:, cols]
            dg_ref[:, cols] = dm[:, cols] * o_ref[hh] * (sg * (1.0 - sg))[:, cols]

    return pl.pallas_call(
        body, name="fox_post_bwd", grid=(WIDTH // LANE,),
        in_specs=[pl.BlockSpec((2, t, DH), lambda j: (j, 0, 0)), pl.BlockSpec((t, LANE), lambda j: (0, FGATE_COL0 + j)),
                  pl.BlockSpec((t, LANE), lambda j: (0, j))],
        out_specs=[pl.BlockSpec((2, t, DH), lambda j: (j, 0, 0)), pl.BlockSpec((t, LANE), lambda j: (0, j))],
        out_shape=[jax.ShapeDtypeStruct((HEADS, t, DH), F32), jax.ShapeDtypeStruct((t, WIDTH), F32)],
        compiler_params=_params(("arbitrary",)),
    )(o, pm, dmix)


def _tail(x, mixg, mixf, tgt, wo, n2w, wg, wu, wd, fw):
    t, d = x.shape
    dff = wg.shape[1]
    tb = min(TB, t)

    def body(x_ref, mg_ref, mf_ref, t_ref, wo_ref, n2_ref, wg_ref, wu_ref, wd_ref, fw_ref,
             h2_ref, act_ref, dgate_ref, dup_ref, dx3_ref, dx2_ref, dmg_ref, dmf_ref, dn2_ref, dfw_ref, loss_ref):
        @pl.when(pl.program_id(0) == 0)
        def _():
            dn2_ref[...] = jnp.zeros_like(dn2_ref)
            dfw_ref[...] = jnp.zeros_like(dfw_ref)
            loss_ref[...] = jnp.zeros_like(loss_ref)

        x2 = x_ref[...] + _dot(mg_ref[...], wo_ref[0:WIDTH, :]) + _dot(mf_ref[...], wo_ref[WIDTH:2 * WIDTH, :])
        r2 = lax.rsqrt(jnp.mean(x2 * x2, axis=-1, keepdims=True) + EPS)
        xn2 = x2 * r2
        h2 = (xn2 * n2_ref[...]).astype(MXU)
        h2_ref[...] = h2
        gate = _dot(h2, wg_ref[...])
        up = _dot(h2, wu_ref[...])
        sg = _sigmoid(gate)
        sl = gate * sg
        act = (sl * up).astype(MXU)
        act_ref[...] = act
        x3 = x2 + _dot(act, wd_ref[...])
        r3 = lax.rsqrt(jnp.mean(x3 * x3, axis=-1, keepdims=True) + EPS)
        xn3 = x3 * r3
        err = xn3 * fw_ref[...] - t_ref[...]
        loss_ref[...] += 0.5 * jnp.sum(jnp.mean(err * err, axis=-1, keepdims=True), axis=0, keepdims=True)
        dy = err * (1.0 / d)
        dfw_ref[...] += jnp.sum(dy * xn3, axis=0, keepdims=True)
        g3 = dy * fw_ref[...]
        dx3 = r3 * (g3 - xn3 * jnp.mean(g3 * xn3, axis=-1, keepdims=True))
        dx3_ref[...] = dx3.astype(MXU)
        dact = _dot_nt(dx3, wd_ref[...])
        dgate = (dact * up * (sg * (1.0 + gate * (1.0 - sg)))).astype(MXU)
        dup = (dact * sl).astype(MXU)
        dgate_ref[...] = dgate
        dup_ref[...] = dup
        dh2 = _dot_nt(dgate, wg_ref[...]) + _dot_nt(dup, wu_ref[...])
        dn2_ref[...] += jnp.sum(dh2 * xn2, axis=0, keepdims=True)
        g2 = dh2 * n2_ref[...]
        dx2 = dx3 + r2 * (g2 - xn2 * jnp.mean(g2 * xn2, axis=-1, keepdims=True))
        dx2_ref[...] = dx2
        dmg_ref[...] = _dot_nt(dx2, wo_ref[0:WIDTH, :])
        dmf_ref[...] = _dot_nt(dx2, wo_ref[WIDTH:2 * WIDTH, :])

    def tok(n):
        return pl.BlockSpec((tb, n), lambda i: (i, 0))

    acc = pl.BlockSpec((1, d), lambda i: (0, 0))
    sds = jax.ShapeDtypeStruct
    return pl.pallas_call(
        body, name="tail", grid=(t // tb,),
        in_specs=[tok(d), tok(WIDTH), tok(WIDTH), tok(d), _resident(wo.shape), _resident((1, d)),
                  _resident(wg.shape), _resident(wu.shape), _resident(wd.shape), _resident((1, d))],
        out_specs=[tok(d), tok(dff), tok(dff), tok(dff), tok(d), tok(d), tok(WIDTH), tok(WIDTH), acc, acc,
                   pl.BlockSpec((1, 1), lambda i: (0, 0))],
        out_shape=[sds((t, d), MXU), sds((t, dff), MXU), sds((t, dff), MXU), sds((t, dff), MXU), sds((t, d), MXU),
                   sds((t, d), F32), sds((t, WIDTH), F32), sds((t, WIDTH), F32), sds((1, d), F32), sds((1, d), F32),
                   sds((1, 1), F32)],
        compiler_params=_params(("arbitrary",)),
    )(x, mixg, mixf, tgt, wo, n2w, wg, wu, wd, fw)


def _wgrad(a, b, name):
    t, m = a.shape
    n = b.shape[1]
    bn = 256 if n % 256 == 0 else LANE

    def body(a_ref, b_ref, o_ref):
        o_ref[...] = _dot_tn(a_ref[...], b_ref[...]).astype(o_ref.dtype)

    return pl.pallas_call(
        body, name=name, grid=(n // bn,),
        in_specs=[_resident((t, m)), pl.BlockSpec((t, bn), lambda j: (0, j))],
        out_specs=pl.BlockSpec((m, bn), lambda j: (0, j)),
        out_shape=jax.ShapeDtypeStruct((m, n), WIRE),
        compiler_params=_params(("arbitrary",)),
    )(a, b)


def _split_w_in(w_in):
    a = 4 * WIDTH
    b = a + 2 * HEADS
    c = b + 4 * WIDTH
    main = jnp.concatenate([w_in[:, :a], w_in[:, b:c]], axis=1)
    small = jnp.concatenate([w_in[:, a:b], w_in[:, c:], jnp.zeros((w_in.shape[0], LANE - 3 * HEADS), w_in.dtype)], axis=1)
    return main, small


def _merge_dw_in(d_gdn, d_z, d_fox, d_fg, d_small):
    return jnp.concatenate([d_gdn, d_z, d_small[:, :2 * HEADS], d_fox, d_fg, d_small[:, 2 * HEADS:3 * HEADS]], axis=1)


def _lanes(*pieces):
    v = jnp.concatenate([p.reshape(-1).astype(F32) for p in pieces])
    return jnp.pad(v, (0, LANE - v.shape[0])).reshape(1, LANE)


def _vector_params(p):
    d = p["norm1_w"].size
    gparams = jnp.concatenate([_lanes(jnp.zeros(HEADS), p["gdn_dt_bias"], p["fox_f_bias"]),
                               _lanes(jnp.zeros(HEADS), p["gdn_A_log"]), jnp.zeros((6, LANE), F32)])
    fox_nw = jnp.stack([jnp.tile(p["fox_q_norm_w"].reshape(-1), 2), jnp.tile(p["fox_k_norm_w"].reshape(-1), 2),
                        jnp.ones((LANE,), F32)])
    return dict(n1w=p["norm1_w"].reshape(1, d), n2w=p["norm2_w"].reshape(1, d), fw=p["final_norm_w"].reshape(1, d),
                onw=p["gdn_out_norm_w"].reshape(1, DH), gparams=gparams, fox_nw=fox_nw)


def _mixer_forward(x, vp, w_main, w_small, conv_w):
    h1, pm, ps = _inproj(x, vp["n1w"], w_main, w_small)
    gates, run, tot, run_t = _gates(ps, vp["gparams"])
    gqkv = _gdn_prep(pm, conv_w)
    o_gdn, states, inv = _gdn_fwd(gqkv, gates, run, tot, run_t)
    mixg = _gdn_post(o_gdn, pm, vp["onw"])
    fqkv = _fox_prep(pm, vp["fox_nw"])
    o_fox, lse = _fox_fwd(fqkv, run, run_t)
    mixf = _fox_post(o_fox, pm)
    return dict(h1=h1, pm=pm, ps=ps, gates=gates, run=run, tot=tot, run_t=run_t, gqkv=gqkv, o_gdn=o_gdn, states=states,
                inv=inv, mixg=mixg, fqkv=fqkv, o_fox=o_fox, lse=lse, mixf=mixf)


def _mixer_backward(x, vp, w_main, w_small, conv_w, f, dx2, dmixg, dmixf, onw):
    pm = f["pm"]
    do_gdn, dz, donw = _gdn_post_bwd(f["o_gdn"], pm, onw, dmixg)
    dgqkv, dcol_g, dtot_g, drow_g = _gdn_bwd(f["gqkv"], f["gates"], f["run"], f["tot"], f["run_t"], f["inv"], f["states"], do_gdn)
    dgdn, dconv = _gdn_prep_bwd(pm, conv_w, dgqkv)
    do_fox, dfg = _fox_post_bwd(f["o_fox"], pm, dmixf)
    dfqkv, dcol_f, drow_f = _fox_bwd(f["fqkv"], f["run"], f["run_t"], f["o_fox"], f["lse"], do_fox)
    dfox, dfnw = _fox_prep_bwd(pm, vp["fox_nw"], dfqkv)
    dps, gsum = _gates_bwd(f["ps"], vp["gparams"], dcol_g, dtot_g, dcol_f, drow_g, drow_f)
    grad_x, dn1w = _inproj_bwd(x, vp["n1w"], dx2, dgdn, dz, dfox, dfg, dps, w_main, w_small)
    h1 = f["h1"]
    dw_in = _merge_dw_in(_wgrad(h1, dgdn, "dw_in_gdn"), _wgrad(h1, dz, "dw_in_z"), _wgrad(h1, dfox, "dw_in_fox"),
                         _wgrad(h1, dfg, "dw_in_fgate"), _wgrad(h1, dps, "dw_in_small"))
    return grad_x, dw_in, dconv, dict(dn1w=dn1w, gsum=gsum, donw=donw, dfnw=dfnw)


VECTORS = ("norm1_w", "norm2_w", "final_norm_w", "gdn_A_log", "gdn_dt_bias", "gdn_out_norm_w", "fox_f_bias",
           "fox_q_norm_w", "fox_k_norm_w")
VEC_ROWS = 16
LOSS_ROW = len(VECTORS)


def _pack_vectors(dn1w, dn2w, dfw, gsum, donw, dfnw, loss):
    d = dn1w.shape[1]

    def body(n1_ref, n2_ref, fw_ref, gs_ref, on_ref, fn_ref, loss_ref, o_ref):
        o_ref[...] = jnp.zeros_like(o_ref)
        o_ref[0:1, :] = n1_ref[...]
        o_ref[1:2, :] = n2_ref[...]
        o_ref[2:3, :] = fw_ref[...]
        o_ref[3:4, 0:HEADS] = gs_ref[0:1, 0:HEADS]
        o_ref[4:5, 0:HEADS] = gs_ref[1:2, 0:HEADS]
        o_ref[5:6, 0:DH] = on_ref[...]
        o_ref[6:7, 0:HEADS] = gs_ref[2:3, 0:HEADS]
        for kind in range(2):
            v = fn_ref[kind, 0]
            for j in range(1, fn_ref.shape[1]):
                v = v + fn_ref[kind, j]
            o_ref[7 + kind:8 + kind, 0:DH] = v[:, :DH] + v[:, DH:]
        o_ref[LOSS_ROW:LOSS_ROW + 1, 0:1] = loss_ref[...]

    return pl.pallas_call(body, name="pack_vectors", out_shape=jax.ShapeDtypeStruct((VEC_ROWS, d), F32),
                          compiler_params=_params())(dn1w, dn2w, dfw, gsum, donw, dfnw, loss)


def _local_step(x, tgt, p, w_in, conv_w, wo, wg, wu, wd):
    vp = _vector_params(p)
    w_main, w_small = _split_w_in(w_in)
    f = _mixer_forward(x, vp, w_main, w_small, conv_w)
    (h2, act, dgate, dup, dx3, dx2, dmixg, dmixf, dn2w, dfw, loss) = _tail(
        x, f["mixg"], f["mixf"], tgt, wo, vp["n2w"], wg, wu, wd, vp["fw"])
    grad_x, dw_in, dconv, small = _mixer_backward(x, vp, w_main, w_small, conv_w, f, dx2, dmixg, dmixf, vp["onw"])
    grads = {"w_in": dw_in, "gdn_conv_w": dconv,
             "w_out": jnp.concatenate([_wgrad(f["mixg"], dx2, "dw_out_gdn"), _wgrad(f["mixf"], dx2, "dw_out_fox")], axis=0),
             "w_ffn_gate": _wgrad(h2, dgate, "dw_gate"), "w_ffn_up": _wgrad(h2, dup, "dw_up"),
             "w_ffn_down": _wgrad(act, dx3, "dw_down")}
    vec = _pack_vectors(small["dn1w"], dn2w, dfw, small["gsum"], small["donw"], small["dfnw"], loss)
    return grad_x, grads, vec


def _my_place():
    return lax.axis_index("x"), lax.axis_index("y"), lax.axis_index("c")


def _peers():
    x, y, c = _my_place()
    peers = []
    for k in range(1, N_DEV):
        px = 1 - x if k & 4 else x
        py = 1 - y if k & 2 else y
        pc = 1 - c if k & 1 else c
        peers.append(((px, py, pc), 4 * px + 2 * py + pc))
    return 4 * x + 2 * y + c, peers


def _spread_copies(kind, srcs, lands, send_sems, recv_sems):
    me, peers = _peers()
    kinds = [kind] * len(srcs) if isinstance(kind, str) else kind
    remote, local = [], []
    for i, (kd, src, land) in enumerate(zip(kinds, srcs, lands)):
        for k, (dev, idx) in enumerate(peers):
            remote.append(pltpu.make_async_remote_copy(
                src_ref=src if kd == "gather" else src.at[idx], dst_ref=land.at[me],
                send_sem=send_sems.at[i * (N_DEV - 1) + k], recv_sem=recv_sems.at[i * (N_DEV - 1) + k],
                device_id=dev, device_id_type=MESH))
        local.append((src if kd == "gather" else src.at[me], land.at[me]))
    return remote, local


def _land_shape(kind, a):
    return (N_DEV,) + a.shape if kind == "gather" else a.shape


def _spread_async(arrays, kind, name, collective_id):
    n = len(arrays)
    hbm = pltpu.MemorySpace.HBM
    src_refs = [jax.new_ref(a, memory_space=hbm) for a in arrays]
    land_refs = [jax.empty_ref(jax.ShapeDtypeStruct(_land_shape(kind, a), a.dtype), memory_space=hbm) for a in arrays]

    @pl.kernel(mesh=plsc.ScalarSubcoreMesh(axis_name="sequencer", num_cores=1), name=name,
               scratch_types=(pltpu.SemaphoreType.DMA((n * (N_DEV - 1),)), pltpu.SemaphoreType.DMA((n * (N_DEV - 1),)),
                              pltpu.SemaphoreType.DMA((n,))),
               compiler_params=pltpu.CompilerParams(collective_id=collective_id))
    def launch(send_sems, recv_sems, local_sems):
        _, peers = _peers()
        barrier = pltpu.get_barrier_semaphore()
        for dev, _ in peers:
            pl.semaphore_signal(barrier, inc=1, device_id=dev, device_id_type=MESH)
        pl.semaphore_wait(barrier, N_DEV - 1)
        remote, local = _spread_copies(kind, src_refs, land_refs, send_sems, recv_sems)
        copies = [pltpu.make_async_copy(s, d, local_sems.at[i]) for i, (s, d) in enumerate(local)]
        for cp in remote + copies:
            cp.start()
        for cp in remote:
            cp.wait_recv()
        for cp in remote:
            cp.wait_send()
        for cp in copies:
            cp.wait()

    launch()
    return [r[...] for r in land_refs]


def _exchange(scatter, gather, name):
    kinds = ["scatter"] * len(scatter) + ["gather"] * len(gather)
    arrays = list(scatter) + list(gather)
    n = len(arrays)

    def body(*refs):
        send_sems, recv_sems, local_sems = refs[2 * n:]
        remote, local = _spread_copies(kinds, refs[:n], refs[n:2 * n], send_sems, recv_sems)
        copies = [pltpu.make_async_copy(s, d, local_sems.at[i]) for i, (s, d) in enumerate(local)]
        for cp in remote + copies:
            cp.start()
        for cp in remote:
            cp.wait_recv()
        for cp in remote:
            cp.wait_send()
        for cp in copies:
            cp.wait()

    return pl.pallas_call(
        body, name=name,
        out_shape=[jax.ShapeDtypeStruct(_land_shape(k, a), a.dtype) for k, a in zip(kinds, arrays)],
        in_specs=[pl.BlockSpec(memory_space=pl.ANY)] * n, out_specs=[pl.BlockSpec(memory_space=pl.ANY)] * n,
        scratch_shapes=[pltpu.SemaphoreType.DMA((n * (N_DEV - 1),)), pltpu.SemaphoreType.DMA((n * (N_DEV - 1),)),
                        pltpu.SemaphoreType.DMA((n,))],
    )(*arrays)


ADAM_ROWS = 128


def _adam_math(g, w, m, v):
    nm = ADAM_B1 * m + (1.0 - ADAM_B1) * g
    nv = ADAM_B2 * v + (1.0 - ADAM_B2) * (g * g)
    m_hat = nm / (1.0 - ADAM_B1 ** ADAM_STEP)
    v_hat = nv / (1.0 - ADAM_B2 ** ADAM_STEP)
    return -ADAM_LR * (m_hat / (jnp.sqrt(v_hat) + ADAM_EPS) + ADAM_WD * w), nm, nv


def _sum_parts(p_ref):
    g = p_ref[0].astype(F32)
    for s in range(1, N_DEV):
        g = g + p_ref[s].astype(F32)
    return g


def _adam_matrix(parts, w, m, v, name):
    _, r, c = w.shape
    rb = ADAM_ROWS if r % ADAM_ROWS == 0 else r

    def body(p_ref, w_ref, m_ref, v_ref, g_ref, d_ref, nm_ref, nv_ref):
        g = _sum_parts(p_ref)
        g_ref[0] = g
        d_ref[0], nm_ref[0], nv_ref[0] = _adam_math(g, w_ref[0], m_ref[0], v_ref[0])

    blk = pl.BlockSpec((1, rb, c), lambda i: (0, i, 0))
    return pl.pallas_call(
        body, name=name, grid=(r // rb,),
        in_specs=[pl.BlockSpec((N_DEV, rb, c), lambda i: (0, i, 0)), blk, blk, blk],
        out_specs=[blk] * 4, out_shape=[jax.ShapeDtypeStruct(w.shape, F32)] * 4,
        compiler_params=_params(("arbitrary",)),
    )(parts, w, m, v)


def _adam_vectors(parts, ws, ms, vs):
    nv = len(ws)

    def body(*refs):
        p_ref = refs[0]
        w_refs, m_refs, v_refs = refs[1:1 + nv], refs[1 + nv:1 + 2 * nv], refs[1 + 2 * nv:1 + 3 * nv]
        outs = refs[1 + 3 * nv:]
        g_all = _sum_parts(p_ref)
        for i in range(nv):
            n = w_refs[i].shape[1]
            g = g_all[i:i + 1, 0:n]
            d, nm, nvv = _adam_math(g, w_refs[i][...], m_refs[i][...], v_refs[i][...])
            outs[i][...] = g
            outs[nv + i][...] = d
            outs[2 * nv + i][...] = nm
            outs[3 * nv + i][...] = nvv
        outs[4 * nv][...] = g_all[LOSS_ROW:LOSS_ROW + 1, 0:1]

    shapes = [jax.ShapeDtypeStruct(a.shape, F32) for a in ws]
    out = pl.pallas_call(body, name="adam_vectors", out_shape=shapes * 4 + [jax.ShapeDtypeStruct((1, 1), F32)],
                         compiler_params=_params())(parts, *ws, *ms, *vs)
    return out[:nv], out[nv:2 * nv], out[2 * nv:3 * nv], out[3 * nv:4 * nv], out[4 * nv]


MATRICES = (("w_in", 1), ("gdn_conv_w", 1), ("w_out", 0), ("w_ffn_gate", 1), ("w_ffn_up", 1), ("w_ffn_down", 0))
WEIGHTS = ("norm1_w", "w_in", "gdn_conv_w", "gdn_A_log", "gdn_dt_bias", "gdn_out_norm_w", "fox_f_bias", "fox_q_norm_w",
           "fox_k_norm_w", "w_out", "norm2_w", "w_ffn_gate", "w_ffn_up", "w_ffn_down", "final_norm_w")


def _join(blocks, axis):
    _, r, c = blocks.shape
    if axis == 0:
        return blocks.reshape(N_DEV * r, c)
    return blocks.transpose(1, 0, 2).reshape(r, N_DEV * c)


def _cut(full, axis):
    r, c = full.shape
    if axis == 0:
        return full.reshape(N_DEV, r // N_DEV, c)
    return full.reshape(r, N_DEV, c // N_DEV).transpose(1, 0, 2)


def kernel(x, norm1_w, w_in, gdn_conv_w, gdn_A_log, gdn_dt_bias, gdn_out_norm_w, fox_f_bias, fox_q_norm_w, fox_k_norm_w, w_out, norm2_w, w_ffn_gate, w_ffn_up, w_ffn_down, final_norm_w, loss_target, m_norm1_w, m_w_in, m_gdn_conv_w, m_gdn_A_log, m_gdn_dt_bias, m_gdn_out_norm_w, m_fox_f_bias, m_fox_q_norm_w, m_fox_k_norm_w, m_w_out, m_norm2_w, m_w_ffn_gate, m_w_ffn_up, m_w_ffn_down, m_final_norm_w, v_norm1_w, v_w_in, v_gdn_conv_w, v_gdn_A_log, v_gdn_dt_bias, v_gdn_out_norm_w, v_fox_f_bias, v_fox_q_norm_w, v_fox_k_norm_w, v_w_out, v_norm2_w, v_w_ffn_gate, v_w_ffn_up, v_w_ffn_down, v_final_norm_w):
    w = dict(norm1_w=norm1_w, w_in=w_in, gdn_conv_w=gdn_conv_w, gdn_A_log=gdn_A_log, gdn_dt_bias=gdn_dt_bias,
             gdn_out_norm_w=gdn_out_norm_w, fox_f_bias=fox_f_bias, fox_q_norm_w=fox_q_norm_w, fox_k_norm_w=fox_k_norm_w,
             w_out=w_out, norm2_w=norm2_w, w_ffn_gate=w_ffn_gate, w_ffn_up=w_ffn_up, w_ffn_down=w_ffn_down,
             final_norm_w=final_norm_w)
    m = dict(norm1_w=m_norm1_w, w_in=m_w_in, gdn_conv_w=m_gdn_conv_w, gdn_A_log=m_gdn_A_log, gdn_dt_bias=m_gdn_dt_bias,
             gdn_out_norm_w=m_gdn_out_norm_w, fox_f_bias=m_fox_f_bias, fox_q_norm_w=m_fox_q_norm_w,
             fox_k_norm_w=m_fox_k_norm_w, w_out=m_w_out, norm2_w=m_norm2_w, w_ffn_gate=m_w_ffn_gate,
             w_ffn_up=m_w_ffn_up, w_ffn_down=m_w_ffn_down, final_norm_w=m_final_norm_w)
    v = dict(norm1_w=v_norm1_w, w_in=v_w_in, gdn_conv_w=v_gdn_conv_w, gdn_A_log=v_gdn_A_log, gdn_dt_bias=v_gdn_dt_bias,
             gdn_out_norm_w=v_gdn_out_norm_w, fox_f_bias=v_fox_f_bias, fox_q_norm_w=v_fox_q_norm_w,
             fox_k_norm_w=v_fox_k_norm_w, w_out=v_w_out, norm2_w=v_norm2_w, w_ffn_gate=v_w_ffn_gate,
             w_ffn_up=v_w_ffn_up, w_ffn_down=v_w_ffn_down, final_norm_w=v_final_norm_w)
    axis_of = dict(MATRICES)
    late = ("w_out", "w_ffn_gate", "w_ffn_up", "w_ffn_down")
    xs, tgt = x[0], loss_target[0]
    vp = _vector_params({n: w[n] for n in VECTORS})

    w_in_blocks, conv_blocks = _exchange([], [w["w_in"][0].astype(WIRE), w["gdn_conv_w"][0]], "gather_in")
    first, w_in_blocks = lax.optimization_barrier((w[late[0]][0].astype(WIRE), w_in_blocks))
    late_blocks = _spread_async([first] + [w[n][0].astype(WIRE) for n in late[1:]], "gather", "gather_late", 1)
    w_main, w_small = _split_w_in(_join(w_in_blocks, 1))
    conv_w = _join(conv_blocks, 1)
    f = _mixer_forward(xs, vp, w_main, w_small, conv_w)
    full = {n: _join(b, axis_of[n]) for n, b in zip(late, late_blocks)}

    (h2, act, dgate, dup, dx3, dx2, dmixg, dmixf, dn2w, dfw, loss) = _tail(
        xs, f["mixg"], f["mixf"], tgt, full["w_out"], vp["n2w"], full["w_ffn_gate"], full["w_ffn_up"],
        full["w_ffn_down"], vp["fw"])
    dlate = {"w_out": jnp.concatenate([_wgrad(f["mixg"], dx2, "dw_out_gdn"), _wgrad(f["mixf"], dx2, "dw_out_fox")], axis=0),
             "w_ffn_gate": _wgrad(h2, dgate, "dw_gate"), "w_ffn_up": _wgrad(h2, dup, "dw_up"),
             "w_ffn_down": _wgrad(act, dx3, "dw_down")}

    cuts = [_cut(dlate[n], axis_of[n]) for n in late]
    dmixg, *cuts = lax.optimization_barrier((dmixg, *cuts))
    parts = dict(zip(late, _spread_async(cuts, "scatter", "grads_late", 2)))
    grad_x, dw_in, dconv, small = _mixer_backward(xs, vp, w_main, w_small, conv_w, f, dx2, dmixg, dmixf, vp["onw"])
    vec = _pack_vectors(small["dn1w"], dn2w, dfw, small["gsum"], small["donw"], small["dfnw"], loss)
    parts["w_in"], parts["gdn_conv_w"], parts_vec = _exchange([_cut(dw_in, 1), _cut(dconv, 1)], [vec], "exchange_last")

    results = [{}, {}, {}, {}]
    for n, _ in MATRICES:
        for d, a in zip(results, _adam_matrix(parts[n], w[n], m[n], v[n], "adam_" + n)):
            d[n] = a
    row = lambda a: a.reshape(1, -1)
    *vec_out, total_loss = _adam_vectors(parts_vec, [row(w[n]) for n in VECTORS], [row(m[n]) for n in VECTORS],
                                         [row(v[n]) for n in VECTORS])
    for d, arrs in zip(results, vec_out):
        for n, a in zip(VECTORS, arrs):
            d[n] = a.reshape(w[n].shape)
    return (total_loss[0, 0], grad_x[None], *[d[n] for d in results for n in WEIGHTS])
```

```python
import functools

import jax
import jax.numpy as jnp
from jax import lax
from jax.experimental import pallas as pl
from jax.experimental.pallas import tpu as pltpu
from jax.experimental.pallas import tpu_sc as plsc

F32 = jnp.float32
MXU = jnp.bfloat16
WIRE = jnp.bfloat16
HI = lax.Precision.HIGHEST
EPS = 1e-6

N_DEV = 8
HEADS = 8
DH = 64
WIDTH = HEADS * DH
CHUNK = 64
LANE = 128
ROW_ALIGN = 16
TB = 256
QB = 256
VMEM_LIMIT = 60 * 1024 * 1024

ADAM_LR = 0.001
ADAM_B1 = 0.9
ADAM_B2 = 0.999
ADAM_EPS = 1e-08
ADAM_WD = 0.01
ADAM_STEP = 10

MESH = pl.DeviceIdType.MESH


def _params(sem=None):
    return pltpu.CompilerParams(dimension_semantics=sem, vmem_limit_bytes=VMEM_LIMIT)


def _resident(shape):
    n = len(shape)
    return pl.BlockSpec(shape, lambda *_: (0,) * n, pipeline_mode=pl.Buffered(1))


def _dot(a, b):
    return jnp.dot(a.astype(MXU), b.astype(MXU), preferred_element_type=F32)


def _dot_nt(a, b):
    return lax.dot_general(a.astype(MXU), b.astype(MXU), (((1,), (1,)), ((), ())), preferred_element_type=F32)


def _dot_tn(a, b):
    return lax.dot_general(a.astype(MXU), b.astype(MXU), (((0,), (0,)), ((), ())), preferred_element_type=F32)


def _hdot(a, b):
    return jnp.dot(a, b, precision=HI, preferred_element_type=F32)


def _hdot_nt(a, b):
    return lax.dot_general(a, b, (((1,), (1,)), ((), ())), precision=HI, preferred_element_type=F32)


def _hdot_tn(a, b):
    return lax.dot_general(a, b, (((0,), (0,)), ((), ())), precision=HI, preferred_element_type=F32)


def _sigmoid(x):
    return 0.5 * jnp.tanh(0.5 * x) + 0.5


def _softplus(x):
    return jnp.maximum(x, 0.0) + jnp.log(1.0 + jnp.exp(-jnp.abs(x)))


def _head_sum_matrix():
    ri = lax.broadcasted_iota(jnp.int32, (LANE, LANE), 0) // DH
    ci = lax.broadcasted_iota(jnp.int32, (LANE, LANE), 1) // DH
    return (ri == ci).astype(F32)


def _group_sum(a, ones_matrix):
    hi = a.astype(jnp.bfloat16)
    lo = (a - hi.astype(F32)).astype(jnp.bfloat16)
    m = ones_matrix.astype(jnp.bfloat16)
    return jnp.dot(hi, m, preferred_element_type=F32) + jnp.dot(lo, m, preferred_element_type=F32)


def _shift_down(x, s):
    return pltpu.roll(x, s, 0)


def _shift_up(x, s):
    return pltpu.roll(x, x.shape[0] - s, 0)


def _inproj(x, n1w, w_main, w_small):
    t, d = x.shape
    nm = w_main.shape[1]
    tb = min(TB, t)

    def body(x_ref, nw_ref, wm_ref, ws_ref, h_ref, pm_ref, ps_ref):
        xv = x_ref[...]
        r = lax.rsqrt(jnp.mean(xv * xv, axis=-1, keepdims=True) + EPS)
        h = (xv * r * nw_ref[...]).astype(MXU)
        h_ref[...] = h
        pm_ref[...] = jnp.dot(h, wm_ref[...], preferred_element_type=F32)
        ps_ref[...] = jnp.dot(h, ws_ref[...], preferred_element_type=F32)

    return pl.pallas_call(
        body, name="inproj", grid=(t // tb,),
        in_specs=[pl.BlockSpec((tb, d), lambda i: (i, 0)), _resident((1, d)), _resident((d, nm)), _resident((d, LANE))],
        out_specs=[pl.BlockSpec((tb, d), lambda i: (i, 0)), pl.BlockSpec((tb, nm), lambda i: (i, 0)),
                   pl.BlockSpec((tb, LANE), lambda i: (i, 0))],
        out_shape=[jax.ShapeDtypeStruct((t, d), MXU), jax.ShapeDtypeStruct((t, nm), F32),
                   jax.ShapeDtypeStruct((t, LANE), F32)],
        compiler_params=_params(("arbitrary",)),
    )(x, n1w, w_main, w_small)


def _inproj_bwd(x, n1w, dx2, dgdn, dz, dfox, dfg, dps, w_main, w_small):
    t, d = x.shape
    tb = min(TB, t)
    w3 = 3 * WIDTH

    def body(x_ref, nw_ref, dx2_ref, dgdn_ref, dz_ref, dfox_ref, dfg_ref, dps_ref, wm_ref, ws_ref, gx_ref, dnw_ref):
        dh = _dot_nt(dgdn_ref[...], wm_ref[:, 0:w3])
        dh += _dot_nt(dz_ref[...], wm_ref[:, w3:w3 + WIDTH])
        dh += _dot_nt(dfox_ref[...], wm_ref[:, w3 + WIDTH:2 * w3 + WIDTH])
        dh += _dot_nt(dfg_ref[...], wm_ref[:, 2 * w3 + WIDTH:2 * w3 + 2 * WIDTH])
        dh += _dot_nt(dps_ref[...], ws_ref[...])
        xv = x_ref[...]
        r = lax.rsqrt(jnp.mean(xv * xv, axis=-1, keepdims=True) + EPS)
        xn = xv * r

        @pl.when(pl.program_id(0) == 0)
        def _():
            dnw_ref[...] = jnp.zeros_like(dnw_ref)

        dnw_ref[...] += jnp.sum(dh * xn, axis=0, keepdims=True)
        g = dh * nw_ref[...]
        gx_ref[...] = dx2_ref[...] + r * (g - xn * jnp.mean(g * xn, axis=-1, keepdims=True))

    def tok(n):
        return pl.BlockSpec((tb, n), lambda i: (i, 0))

    return pl.pallas_call(
        body, name="inproj_bwd", grid=(t // tb,),
        in_specs=[tok(d), _resident((1, d)), tok(d), tok(w3), tok(WIDTH), tok(w3), tok(WIDTH), tok(LANE),
                  _resident(w_main.shape), _resident(w_small.shape)],
        out_specs=[tok(d), pl.BlockSpec((1, d), lambda i: (0, 0))],
        out_shape=[jax.ShapeDtypeStruct((t, d), F32), jax.ShapeDtypeStruct((1, d), F32)],
        compiler_params=_params(("arbitrary",)),
    )(x, n1w, dx2, dgdn, dz, dfox, dfg, dps, w_main, w_small)


def _gate_lanes(shape):
    lane = lax.broadcasted_iota(jnp.int32, shape, 1)
    return lane < HEADS, (lane >= HEADS) & (lane < 2 * HEADS), (lane >= 2 * HEADS) & (lane < 3 * HEADS)


def _block_masks():
    ri = lax.broadcasted_iota(jnp.int32, (LANE, LANE), 0)
    ci = lax.broadcasted_iota(jnp.int32, (LANE, LANE), 1)
    same = (ri // CHUNK) == (ci // CHUNK)
    return ((ri >= ci).astype(F32), (ri <= ci).astype(F32), (same & (ri >= ci)).astype(F32),
            (same & (ri <= ci)).astype(F32), same.astype(F32))


def _gates(ps, gparams):
    t = ps.shape[0]
    nb = t // LANE

    def body(ps_ref, gp_ref, out_ref, run_ref, tot_ref, runt_ref):
        p = ps_ref[...]
        is_b, is_a, is_f = _gate_lanes(p.shape)
        z = p + gp_ref[0:1, :]
        neg_exp_a = -jnp.exp(gp_ref[1:2, :])
        glog = neg_exp_a * _softplus(z)
        logf = -_softplus(-z)
        out_ref[...] = jnp.where(is_b, _sigmoid(p), jnp.where(is_a, glog, jnp.where(is_f, logf, 0.0)))
        tril, _, tril_c, _, same_c = _block_masks()
        off = jnp.zeros((1, LANE), F32)
        for b in range(nb):
            rows = slice(b * LANE, (b + 1) * LANE)
            blk = out_ref[rows, :]
            ga = jnp.where(is_a[:LANE], blk, 0.0)
            fb = _hdot(tril, jnp.where(is_f[:LANE], blk, 0.0)) + off
            run = fb + _hdot(tril_c, ga)
            run_ref[rows, :] = run
            runt_ref[:, rows] = run.T
            tot_ref[rows, :] = _hdot(same_c, ga)
            off = fb[LANE - 1:LANE, :]

    return pl.pallas_call(
        body, name="gates",
        out_shape=[jax.ShapeDtypeStruct((t, LANE), F32)] * 3 + [jax.ShapeDtypeStruct((LANE, t), F32)],
        compiler_params=_params(),
    )(ps, gparams)


def _gates_bwd(ps, gparams, dcol_g, dtot_g, dcol_f, drow_g, drow_f):
    t = ps.shape[0]
    nb = t // LANE

    def body(ps_ref, gp_ref, dcg_ref, dtg_ref, dcf_ref, drg_ref, drf_ref, dps_ref, sums_ref, dl_ref, d0_ref, tr_ref):
        p = ps_ref[...]
        is_b, is_a, is_f = _gate_lanes(p.shape)
        _, triu, _, triu_c, same_c = _block_masks()
        tr_ref[...] = jnp.zeros_like(tr_ref)
        off = jnp.zeros((1, LANE), F32)
        for b in reversed(range(nb)):
            rows = slice(b * LANE, (b + 1) * LANE)
            tr_ref[HEADS:2 * HEADS, :] = drg_ref[:, rows]
            tr_ref[2 * HEADS:3 * HEADS, :] = drf_ref[:, rows]
            d = dcg_ref[rows, :] + dcf_ref[rows, :] + tr_ref[...].T
            d0_ref[rows, :] = d
            dlf = _hdot(triu, jnp.where(is_f[:LANE], d, 0.0)) + off
            dla = (_hdot(triu_c, jnp.where(is_a[:LANE], d, 0.0))
                   + _hdot(same_c, jnp.where(is_a[:LANE], dtg_ref[rows, :], 0.0)))
            dl_ref[rows, :] = dlf + dla
            off = dlf[0:1, :]
        z = p + gp_ref[0:1, :]
        neg_exp_a = -jnp.exp(gp_ref[1:2, :])
        sb = _sigmoid(p)
        glog = neg_exp_a * _softplus(z)
        dl = dl_ref[...]
        dp = jnp.where(is_b, d0_ref[...] * sb * (1.0 - sb),
                       jnp.where(is_a, dl * neg_exp_a * _sigmoid(z), jnp.where(is_f, dl * _sigmoid(-z), 0.0)))
        dps_ref[...] = dp
        s_a = jnp.sum(jnp.where(is_a, dl * glog, 0.0), axis=0, keepdims=True)
        s_p = jnp.sum(jnp.where(is_b, 0.0, dp), axis=0, keepdims=True)
        row = lax.broadcasted_iota(jnp.int32, (8, LANE), 0)
        from_a = pltpu.roll(jnp.where(row == 0, s_a, jnp.where(row == 1, s_p, 0.0)), LANE - HEADS, 1)
        from_f = pltpu.roll(jnp.where(row == 2, s_p, 0.0), LANE - 2 * HEADS, 1)
        lane = lax.broadcasted_iota(jnp.int32, (8, LANE), 1)
        sums_ref[...] = jnp.where(lane < HEADS, from_a + from_f, 0.0)

    return pl.pallas_call(
        body, name="gates_bwd",
        out_shape=[jax.ShapeDtypeStruct((t, LANE), F32), jax.ShapeDtypeStruct((8, LANE), F32)],
        scratch_shapes=[pltpu.VMEM((t, LANE), F32), pltpu.VMEM((t, LANE), F32), pltpu.VMEM((LANE, LANE), F32)],
        compiler_params=_params(),
    )(ps, gparams, dcol_g, dtot_g, dcol_f, drow_g, drow_f)


def _conv(xv, w):
    acc = w[3:4, :] * xv
    for s in range(1, 4):
        acc += w[3 - s:4 - s, :] * _shift_down(xv, s)
    return acc


PREP_ROWS = 256
HALO = 8


def _tile_rows(r):
    return pl.ds(pl.multiple_of(r * PREP_ROWS, PREP_ROWS), PREP_ROWS)


def _gdn_prep(pm, conv_w):
    t = pm.shape[0]
    nj = WIDTH // LANE
    win = PREP_ROWS + HALO

    def body(x_ref, w_ref, o_ref, xp_ref):
        kind = pl.program_id(0)
        xp_ref[0:HALO, :] = jnp.zeros((HALO, LANE), F32)
        xp_ref[HALO:, :] = x_ref[...]
        w = w_ref[...]
        hs = _head_sum_matrix()

        def tile(r, _):
            xw = xp_ref[pl.ds(pl.multiple_of(r * PREP_ROWS, PREP_ROWS), win), :]
            acc = _conv(xw, w)[HALO:]
            y = acc * _sigmoid(acc)
            out = jnp.where(kind < 2, y * lax.rsqrt(_group_sum(y * y, hs) + EPS), y)
            o_ref[0, 0, _tile_rows(r), :] = out[:, :DH]
            o_ref[0, 1, _tile_rows(r), :] = out[:, DH:]
            return 0

        lax.fori_loop(0, t // PREP_ROWS, tile, 0)

    return pl.pallas_call(
        body, name="gdn_prep", grid=(3, nj),
        in_specs=[pl.BlockSpec((t, LANE), lambda i, j: (0, i * nj + j)),
                  pl.BlockSpec((4, LANE), lambda i, j: (0, i * nj + j))],
        out_specs=pl.BlockSpec((1, 2, t, DH), lambda i, j: (i, j, 0, 0)),
        out_shape=jax.ShapeDtypeStruct((3, HEADS, t, DH), F32),
        scratch_shapes=[pltpu.VMEM((t + HALO, LANE), F32)],
        compiler_params=_params(("arbitrary", "arbitrary")),
    )(pm, conv_w)


def _gdn_prep_bwd(pm, conv_w, dqkv):
    t = pm.shape[0]
    nj = WIDTH // LANE
    win = PREP_ROWS + 2 * HALO

    def body(x_ref, w_ref, d_ref, dx_ref, dw_ref, xp_ref, dp_ref):
        kind = pl.program_id(0)
        zeros = jnp.zeros((HALO, LANE), F32)
        for ref in (xp_ref, dp_ref):
            ref[0:HALO, :] = zeros
            ref[HALO + t:, :] = zeros
        xp_ref[HALO:HALO + t, :] = x_ref[...]
        dp_ref[HALO:HALO + t, 0:DH] = d_ref[0, 0]
        dp_ref[HALO:HALO + t, DH:] = d_ref[0, 1]
        w = w_ref[...]
        hs = _head_sum_matrix()
        rows = lax.broadcasted_iota(jnp.int32, (win, LANE), 0)
        in_tile = (rows >= HALO) & (rows < HALO + PREP_ROWS)

        def tile(r, dw):
            start = pl.multiple_of(r * PREP_ROWS, PREP_ROWS)
            xw = xp_ref[pl.ds(start, win), :]
            d = dp_ref[pl.ds(start, win), :]
            acc = _conv(xw, w)
            sg = _sigmoid(acc)
            y = acc * sg
            rn = lax.rsqrt(_group_sum(y * y, hs) + EPS)
            yn = y * rn
            dy = jnp.where(kind < 2, rn * (d - yn * _group_sum(d * yn, hs)), d)
            dacc = dy * sg * (1.0 + acc * (1.0 - sg))
            dx = w[3:4, :] * dacc
            for s in range(1, 4):
                dx += w[3 - s:4 - s, :] * _shift_up(dacc, s)
            dx_ref[_tile_rows(r), :] = dx[HALO:HALO + PREP_ROWS]
            dm = jnp.where(in_tile, dacc, 0.0)
            return tuple(dw[i] + jnp.sum(dm * (xw if i == 3 else _shift_down(xw, 3 - i)), axis=0, keepdims=True)
                         for i in range(4))

        dw = lax.fori_loop(0, t // PREP_ROWS, tile, tuple(jnp.zeros((1, LANE), F32) for _ in range(4)))
        for i in range(4):
            dw_ref[i:i + 1, :] = dw[i]

    return pl.pallas_call(
        body, name="gdn_prep_bwd", grid=(3, nj),
        in_specs=[pl.BlockSpec((t, LANE), lambda i, j: (0, i * nj + j)),
                  pl.BlockSpec((4, LANE), lambda i, j: (0, i * nj + j)),
                  pl.BlockSpec((1, 2, t, DH), lambda i, j: (i, j, 0, 0))],
        out_specs=[pl.BlockSpec((t, LANE), lambda i, j: (0, i * nj + j)),
                   pl.BlockSpec((4, LANE), lambda i, j: (0, i * nj + j))],
        out_shape=[jax.ShapeDtypeStruct((t, 3 * WIDTH), F32), jax.ShapeDtypeStruct((4, 3 * WIDTH), F32)],
        scratch_shapes=[pltpu.VMEM((t + 2 * HALO, LANE), F32), pltpu.VMEM((t + 2 * HALO, LANE), F32)],
        compiler_params=_params(("arbitrary", "arbitrary")),
    )(pm, conv_w, dqkv)


FOX_COL0 = 4 * WIDTH // LANE


def _fox_prep(pm, nw):
    t = pm.shape[0]
    nj = WIDTH // LANE

    def body(x_ref, w_ref, o_ref):
        kind = pl.program_id(0)
        hs = _head_sum_matrix()
        wk = w_ref[pl.ds(kind, 1), :]

        def tile(r, _):
            xv = x_ref[_tile_rows(r), :]
            ms = _group_sum(xv * xv, hs) * (1.0 / DH)
            out = jnp.where(kind < 2, xv * lax.rsqrt(ms + EPS) * wk, xv)
            o_ref[0, 0, _tile_rows(r), :] = out[:, :DH]
            o_ref[0, 1, _tile_rows(r), :] = out[:, DH:]
            return 0

        lax.fori_loop(0, t // PREP_ROWS, tile, 0)

    return pl.pallas_call(
        body, name="fox_prep", grid=(3, nj),
        in_specs=[pl.BlockSpec((t, LANE), lambda i, j: (0, FOX_COL0 + i * nj + j)),
                  pl.BlockSpec((3, LANE), lambda i, j: (0, 0))],
        out_specs=pl.BlockSpec((1, 2, t, DH), lambda i, j: (i, j, 0, 0)),
        out_shape=jax.ShapeDtypeStruct((3, HEADS, t, DH), F32),
        compiler_params=_params(("arbitrary", "arbitrary")),
    )(pm, nw)


def _fox_prep_bwd(pm, nw, dqkv):
    t = pm.shape[0]
    nj = WIDTH // LANE

    def body(x_ref, w_ref, d_ref, dx_ref, dw_ref):
        kind = pl.program_id(0)
        hs = _head_sum_matrix()
        wk = w_ref[pl.ds(kind, 1), :]

        def tile(r, dw):
            xv = x_ref[_tile_rows(r), :]
            rn = lax.rsqrt(_group_sum(xv * xv, hs) * (1.0 / DH) + EPS)
            xn = xv * rn
            d = jnp.concatenate([d_ref[0, 0, _tile_rows(r), :], d_ref[0, 1, _tile_rows(r), :]], axis=1)
            g = d * wk
            dxn = rn * (g - xn * _group_sum(g * xn, hs) * (1.0 / DH))
            dx_ref[_tile_rows(r), :] = jnp.where(kind < 2, dxn, d)
            return dw + jnp.sum(d * xn, axis=0, keepdims=True)

        dw_ref[0, 0] = lax.fori_loop(0, t // PREP_ROWS, tile, jnp.zeros((1, LANE), F32))

    return pl.pallas_call(
        body, name="fox_prep_bwd", grid=(3, nj),
        in_specs=[pl.BlockSpec((t, LANE), lambda i, j: (0, FOX_COL0 + i * nj + j)),
                  pl.BlockSpec((3, LANE), lambda i, j: (0, 0)),
                  pl.BlockSpec((1, 2, t, DH), lambda i, j: (i, j, 0, 0))],
        out_specs=[pl.BlockSpec((t, LANE), lambda i, j: (0, i * nj + j)),
                   pl.BlockSpec((1, 1, 1, LANE), lambda i, j: (i, j, 0, 0))],
        out_shape=[jax.ShapeDtypeStruct((t, 3 * WIDTH), F32), jax.ShapeDtypeStruct((3, nj, 1, LANE), F32)],
        compiler_params=_params(("arbitrary", "arbitrary")),
    )(pm, nw, dqkv)


SC = 256
CPS = SC // CHUNK
Q_SCALE = DH ** -0.5


def _sc_masks():
    ri = lax.broadcasted_iota(jnp.int32, (SC, SC), 0)
    ci = lax.broadcasted_iota(jnp.int32, (SC, SC), 1)
    same = (ri // CHUNK) == (ci // CHUNK)
    return same & (ri >= ci), same & (ri > ci), ri == ci


def _unit_lower_inverse(m, eye):
    inv = jnp.where(eye, 1.0, 0.0) + m
    for _ in range(5):
        m = _dot(m, m)
        inv = inv + _dot(inv, m)
    return inv


def _lane_col(blk, lane_idx):
    lane = lax.broadcasted_iota(jnp.int32, blk.shape, 1)
    return jnp.sum(jnp.where(lane == lane_idx, blk, 0.0), axis=1, keepdims=True)


def _to_lane(col, lane_idx):
    lane = lax.broadcasted_iota(jnp.int32, (col.shape[0], LANE), 1)
    return jnp.where(lane == lane_idx, col, 0.0)


def _gdn_columns(gates_ref, run_ref, tot_ref, runt_ref, rows, h):
    return (_lane_col(gates_ref[rows, :], h), _lane_col(run_ref[rows, :], HEADS + h),
            _lane_col(tot_ref[rows, :], HEADS + h), runt_ref[pl.ds(h, 1), rows])


def _gdn_local(q, k, beta, gc, gl, grow, causal):
    decay = jnp.exp(jnp.where(causal, gc - grow, -1e30))
    egc = jnp.exp(gc)
    ekd = jnp.exp(gl - gc)
    qs = q * Q_SCALE
    kb = k * beta
    kk = _dot_nt(kb, k)
    qk = _dot_nt(qs, k)
    return beta, gl, decay, egc, ekd, qs, kb, kk, qk, jnp.where(causal, qk * decay, 0.0)


def _chunk_rows(c):
    return pl.ds(pl.multiple_of(c * CHUNK, CHUNK), CHUNK)


def _sc_rows(b):
    return pl.ds(pl.multiple_of(b * SC, SC), SC)


def _gdn_fwd(qkv, gates, run, tot, run_t):
    t = qkv.shape[2]
    nc = t // CHUNK
    nsc = t // SC

    def body(qkv_ref, gates_ref, run_ref, tot_ref, runt_ref, o_ref, st_ref, inv_ref, kc_s, qc_s, g_s, au_s):
        h = pl.program_id(0)
        causal, strict, eye = _sc_masks()

        def local(b, _):
            rows = _sc_rows(b)
            q, k, v = qkv_ref[0, 0, rows, :], qkv_ref[1, 0, rows, :], qkv_ref[2, 0, rows, :]
            beta, _, decay, egc, ekd, qs, kb, kk, _, attn = _gdn_local(
                q, k, *_gdn_columns(gates_ref, run_ref, tot_ref, runt_ref, rows, h), causal)
            inv = _unit_lower_inverse(-jnp.where(strict, kk * decay, 0.0), eye)
            inv_ref[0, rows, :] = inv.astype(inv_ref.dtype)
            u = _dot(inv, v * beta)
            w = _dot(inv, kb * egc)
            g_s[rows, :] = qs * egc - _dot(attn, w)
            au_s[rows, :] = _dot(attn, u)
            kd = k * ekd
            for j in range(CPS):
                sl = slice(j * CHUNK, (j + 1) * CHUNK)
                kc_s[b * CPS + j] = _dot_tn(kd[sl], w[sl])
                qc_s[b * CPS + j] = _dot_tn(kd[sl], u[sl])
            return 0

        lax.fori_loop(0, nsc, local, 0)

        def step(c, s):
            rows = _chunk_rows(c)
            st_ref[0, c] = s
            o_ref[0, rows, :] = _dot(g_s[rows, :], s) + au_s[rows, :]
            egl = jnp.exp(_lane_col(tot_ref[pl.ds(c * CHUNK, 1), :], HEADS + h))
            return egl * s - _dot(kc_s[c], s) + qc_s[c]

        lax.fori_loop(0, nc, step, jnp.zeros((DH, DH), F32))

    whole = pl.BlockSpec((t, LANE), lambda h: (0, 0))
    return pl.pallas_call(
        body, name="gdn_fwd", grid=(HEADS,),
        in_specs=[pl.BlockSpec((3, 1, t, DH), lambda h: (0, h, 0, 0)), whole, whole, whole,
                  pl.BlockSpec((HEADS, t), lambda h: (1, 0))],
        out_specs=[pl.BlockSpec((1, t, DH), lambda h: (h, 0, 0)), pl.BlockSpec((1, nc, DH, DH), lambda h: (h, 0, 0, 0)),
                   pl.BlockSpec((1, t, SC), lambda h: (h, 0, 0))],
        out_shape=[jax.ShapeDtypeStruct((HEADS, t, DH), F32), jax.ShapeDtypeStruct((HEADS, nc, DH, DH), F32),
                   jax.ShapeDtypeStruct((HEADS, t, SC), MXU)],
        scratch_shapes=[pltpu.VMEM((nc, DH, DH), F32), pltpu.VMEM((nc, DH, DH), F32),
                        pltpu.VMEM((t, DH), F32), pltpu.VMEM((t, DH), F32)],
        compiler_params=_params(("arbitrary",)),
    )(qkv, gates, run, tot, run_t)


def _gdn_bwd(qkv, gates, run, tot, run_t, inv, states, do):
    t = qkv.shape[2]
    nc = t // CHUNK
    nsc = t // SC

    def body(qkv_ref, gates_ref, run_ref, tot_ref, runt_ref, inv_ref, st_ref, do_ref,
             dqkv_ref, dcol_ref, dtot_ref, drow_ref, u_s, w_s, kc_s, h_s, dsn_s):
        h = pl.program_id(0)
        causal, strict, _ = _sc_masks()

        @pl.when(h == 0)
        def _():
            dcol_ref[...] = jnp.zeros_like(dcol_ref)
            dtot_ref[...] = jnp.zeros_like(dtot_ref)

        def local(b, _):
            rows = _sc_rows(b)
            q, k, v = qkv_ref[0, 0, rows, :], qkv_ref[1, 0, rows, :], qkv_ref[2, 0, rows, :]
            beta, _, _, egc, ekd, qs, kb, _, _, attn = _gdn_local(
                q, k, *_gdn_columns(gates_ref, run_ref, tot_ref, runt_ref, rows, h), causal)
            inv_b = inv_ref[0, rows, :]
            u = _dot(inv_b, v * beta)
            w = _dot(inv_b, kb * egc)
            u_s[rows, :] = u
            w_s[rows, :] = w
            g = qs * egc - _dot(attn, w)
            kd = k * ekd
            dout = do_ref[0, rows, :]
            for j in range(CPS):
                sl = slice(j * CHUNK, (j + 1) * CHUNK)
                kc_s[b * CPS + j] = _dot_tn(kd[sl], w[sl])
                h_s[b * CPS + j] = _dot_tn(g[sl], dout[sl])
            return 0

        lax.fori_loop(0, nsc, local, 0)

        def step(i, ds):
            c = nc - 1 - i
            dsn_s[c] = ds
            egl = jnp.exp(_lane_col(tot_ref[pl.ds(c * CHUNK, 1), :], HEADS + h))
            return egl * ds - _dot_tn(kc_s[c], ds) + h_s[c]

        lax.fori_loop(0, nc, step, jnp.zeros((DH, DH), F32))

        def back(b, _):
            rows = _sc_rows(b)
            q, k, v = qkv_ref[0, 0, rows, :], qkv_ref[1, 0, rows, :], qkv_ref[2, 0, rows, :]
            beta, gl, decay, egc, ekd, qs, kb, kk, qk, attn = _gdn_local(
                q, k, *_gdn_columns(gates_ref, run_ref, tot_ref, runt_ref, rows, h), causal)
            u, w = u_s[rows, :], w_s[rows, :]
            kd = k * ekd
            vb = v * beta
            kbe = kb * egc
            dout = do_ref[0, rows, :]
            first = lax.broadcasted_iota(jnp.int32, (CHUNK, 1), 0) == 0
            dg_p, dkd_p, dw_p, du_p, dgl_p = [], [], [], [], []
            for j in range(CPS):
                sl = slice(j * CHUNK, (j + 1) * CHUNK)
                s = st_ref[0, b * CPS + j]
                dsn = dsn_s[b * CPS + j]
                dkc = -_dot_nt(dsn, s)
                dg_p.append(_dot_nt(dout[sl], s))
                dkd_p.append(_dot_nt(w[sl], dkc) + _dot_nt(u[sl], dsn))
                dw_p.append(_dot(kd[sl], dkc))
                du_p.append(_dot(kd[sl], dsn))
                degl = jnp.sum(jnp.sum(s * dsn, axis=1, keepdims=True), axis=0, keepdims=True)
                dgl_p.append(jnp.where(first, degl * jnp.exp(gl[j * CHUNK:j * CHUNK + 1, :]), 0.0))
            dg, dkd = jnp.concatenate(dg_p, axis=0), jnp.concatenate(dkd_p, axis=0)
            da = jnp.where(causal, _dot_nt(dout, u) - _dot_nt(dg, w), 0.0)
            at = _dot_tn(attn, jnp.concatenate([dout, dg], axis=1))
            du = at[:, :DH] + jnp.concatenate(du_p, axis=0)
            dw = jnp.concatenate(dw_p, axis=0) - at[:, DH:]
            inv_t = inv_ref[0, rows, :].astype(F32).T
            it = _dot(inv_t, jnp.concatenate([du, dw], axis=1))
            dvb, dkbe = it[:, :DH], it[:, DH:]
            dinv = _dot_nt(du, vb) + _dot_nt(dw, kbe)
            dl = jnp.where(strict, -_dot(_dot(inv_t, dinv), inv_t), 0.0)
            dlogd = (dl * kk + da * qk) * decay
            dkk = dl * decay
            dqk = da * decay
            dkb = _dot(dkk, k) + dkbe * egc
            dqs = _dot(dqk, k) + dg * egc
            dk = (_dot_tn(jnp.concatenate([dkk, dqk], axis=0), jnp.concatenate([kb, qs], axis=0))
                  + dkd * ekd + dkb * beta)
            dkd_kd = jnp.sum(dkd * kd, axis=1, keepdims=True)
            dgc = (jnp.sum(dlogd, axis=1, keepdims=True) + jnp.sum(dg * qs, axis=1, keepdims=True) * egc
                   + jnp.sum(dkbe * kbe, axis=1, keepdims=True) - dkd_kd)
            dgl = dkd_kd + jnp.concatenate(dgl_p, axis=0)
            dbeta = jnp.sum(dkb * k, axis=1, keepdims=True) + jnp.sum(dvb * v, axis=1, keepdims=True)
            dqkv_ref[0, 0, rows, :] = dqs * Q_SCALE
            dqkv_ref[1, 0, rows, :] = dk
            dqkv_ref[2, 0, rows, :] = dvb * beta
            dcol_ref[rows, :] += _to_lane(dbeta, h) + _to_lane(dgc, HEADS + h)
            dtot_ref[rows, :] += _to_lane(dgl, HEADS + h)
            drow_ref[pl.ds(h, 1), rows] = -jnp.sum(dlogd, axis=0, keepdims=True)
            return 0

        lax.fori_loop(0, nsc, back, 0)

    whole = pl.BlockSpec((t, LANE), lambda h: (0, 0))
    rowspec = pl.BlockSpec((HEADS, t), lambda h: (0, 0))
    sq = pltpu.VMEM((nc, DH, DH), F32)
    return pl.pallas_call(
        body, name="gdn_bwd", grid=(HEADS,),
        in_specs=[pl.BlockSpec((3, 1, t, DH), lambda h: (0, h, 0, 0)), whole, whole, whole,
                  pl.BlockSpec((HEADS, t), lambda h: (1, 0)),
                  pl.BlockSpec((1, t, SC), lambda h: (h, 0, 0)), pl.BlockSpec((1, nc, DH, DH), lambda h: (h, 0, 0, 0)),
                  pl.BlockSpec((1, t, DH), lambda h: (h, 0, 0))],
        out_specs=[pl.BlockSpec((3, 1, t, DH), lambda h: (0, h, 0, 0)), whole, whole, rowspec],
        out_shape=[jax.ShapeDtypeStruct((3, HEADS, t, DH), F32), jax.ShapeDtypeStruct((t, LANE), F32),
                   jax.ShapeDtypeStruct((t, LANE), F32), jax.ShapeDtypeStruct((HEADS, t), F32)],
        scratch_shapes=[pltpu.VMEM((t, DH), F32), pltpu.VMEM((t, DH), F32), sq, sq, sq],
        compiler_params=_params(("arbitrary",)),
    )(qkv, gates, run, tot, run_t, inv, states, do)


FOX_HP = 2


def _wide_t(a):
    r = a.shape[0]
    return jnp.concatenate([a, jnp.zeros((r, LANE - DH), F32)], axis=1).T[:DH]


def _tall_t(a):
    r = a.shape[1]
    return jnp.concatenate([a, jnp.zeros((LANE - DH, r), F32)], axis=0).T[:, :DH]


def _key_side_f(run_blk, head, qb):
    col = jnp.broadcast_to(_lane_col(run_blk, 2 * HEADS + head), (run_blk.shape[0], LANE))
    return jnp.concatenate([col] * (qb // LANE), axis=1)


def _diag_mask(qb):
    return lax.broadcasted_iota(jnp.int32, (qb, qb), 0) <= lax.broadcasted_iota(jnp.int32, (qb, qb), 1)


def _fox_fwd(qkv, run, run_t):
    t = qkv.shape[2]
    qb = min(QB, t)
    nq = t // qb
    hp_n = FOX_HP

    def body(q_ref, k_ref, v_ref, run_ref, runt_ref, o_ref, lse_ref, kb_s, vt_s):
        hp = pl.program_id(0)
        i = pl.program_id(1)

        @pl.when(i == 0)
        def _():
            for hh in range(hp_n):
                kb_s[hh] = k_ref[0, hh].astype(MXU)
                for b in range(nq):
                    rows = slice(b * qb, (b + 1) * qb)
                    vt_s[hh, :, rows] = _wide_t(v_ref[0, hh, rows, :]).astype(MXU)

        qrows = pl.ds(pl.multiple_of(i * qb, qb), qb)
        qs = [(q_ref[0, hh] * Q_SCALE).astype(MXU) for hh in range(hp_n)]
        fq = [runt_ref[pl.ds(hp * hp_n + hh, 1), qrows] for hh in range(hp_n)]

        def block_rows(j):
            return pl.ds(pl.multiple_of(j * qb, qb), qb)

        def scores(j):
            return tuple(_dot_nt(kb_s[hh, block_rows(j), :], qs[hh]) for hh in range(hp_n))

        def absorb(j, raw, state, diagonal):
            rows = block_rows(j)
            run_blk = run_ref[rows, :]
            stats, pv = [], []
            for hh in range(hp_n):
                m, l, _ = state[hh]
                st = raw[hh] + (fq[hh] - _key_side_f(run_blk, hp * hp_n + hh, qb))
                if diagonal:
                    st = jnp.where(_diag_mask(qb), st, -1e30)
                m_new = jnp.maximum(m, jnp.max(st, axis=0, keepdims=True))
                p = jnp.exp(st - m_new)
                alpha = jnp.exp(m - m_new)
                stats.append((m_new, alpha * l + jnp.sum(p, axis=0, keepdims=True), alpha))
                pv.append(_dot(vt_s[hh, :, rows], p))
            return tuple((stats[hh][0], stats[hh][1], stats[hh][2] * state[hh][2] + pv[hh]) for hh in range(hp_n))

        def kstep(j, carry):
            state, raw = carry
            ahead = scores(j + 1)
            return absorb(j, raw, state, False), ahead

        init = tuple((jnp.full((1, qb), -1e30, F32), jnp.zeros((1, qb), F32), jnp.zeros((DH, qb), F32))
                     for _ in range(hp_n))
        state, raw = lax.fori_loop(0, i, kstep, (init, scores(0)))
        state = absorb(i, raw, state, True)
        for hh in range(hp_n):
            m, l, acc = state[hh]
            o_ref[hh] = _tall_t(acc / l)
            lse_ref[pl.ds(hp * hp_n + hh, 1), qrows] = m + jnp.log(l)

    return pl.pallas_call(
        body, name="fox_fwd", grid=(HEADS // hp_n, nq),
        in_specs=[pl.BlockSpec((1, hp_n, qb, DH), lambda h, i: (0, h, i, 0)),
                  pl.BlockSpec((1, hp_n, t, DH), lambda h, i: (1, h, 0, 0)),
                  pl.BlockSpec((1, hp_n, t, DH), lambda h, i: (2, h, 0, 0)),
                  pl.BlockSpec((t, LANE), lambda h, i: (0, 0)),
                  pl.BlockSpec((HEADS, t), lambda h, i: (2, 0))],
        out_specs=[pl.BlockSpec((hp_n, qb, DH), lambda h, i: (h, i, 0)), pl.BlockSpec((HEADS, t), lambda h, i: (0, 0))],
        out_shape=[jax.ShapeDtypeStruct((HEADS, t, DH), F32), jax.ShapeDtypeStruct((HEADS, t), F32)],
        scratch_shapes=[pltpu.VMEM((hp_n, t, DH), MXU), pltpu.VMEM((hp_n, DH, t), MXU)],
        compiler_params=_params(("arbitrary", "arbitrary")),
    )(qkv, qkv, qkv, run, run_t)


def _fox_bwd(qkv, run, run_t, o, lse, do):
    t = qkv.shape[2]
    qb = min(QB, t)
    nq = t // qb
    hp_n = FOX_HP

    def body(q_ref, k_ref, v_ref, run_ref, runt_ref, o_ref, lse_ref, do_ref, dqkv_ref, dcol_ref, drow_ref, dqt_s):
        hp = pl.program_id(0)
        j = pl.program_id(1)

        @pl.when(j == 0)
        def _():
            dqt_s[...] = jnp.zeros_like(dqt_s)

        @pl.when((j == 0) & (hp == 0))
        def _():
            dcol_ref[...] = jnp.zeros_like(dcol_ref)
            drow_ref[...] = jnp.zeros_like(drow_ref)

        krows = pl.ds(pl.multiple_of(j * qb, qb), qb)
        run_blk = run_ref[krows, :]
        ones8 = jnp.ones((8, DH), MXU)
        kb, kt, vb, fk = [], [], [], []
        for hh in range(hp_n):
            kf = k_ref[0, hh]
            kb.append(kf.astype(MXU))
            kt.append(_wide_t(kf).astype(MXU))
            vb.append(v_ref[0, hh].astype(MXU))
            fk.append(_key_side_f(run_blk, hp * hp_n + hh, qb))

        def block_rows(i):
            return pl.ds(pl.multiple_of(i * qb, qb), qb)

        def products(i):
            rows = block_rows(i)
            out = []
            for hh in range(hp_n):
                dout = do_ref[hh, rows, :]
                x = dout * o_ref[hh, rows, :]
                x_hi = x.astype(MXU)
                out.append((_dot_nt(kb[hh], q_ref[0, hh, rows, :] * Q_SCALE), _dot_nt(vb[hh], dout),
                            (_dot_nt(ones8, x_hi) + _dot_nt(ones8, x - x_hi.astype(F32)))[0:1, :]))
            return tuple(out)

        def absorb(i, prods, acc, diagonal):
            rows = block_rows(i)
            pieces = []
            for hh in range(hp_n):
                head = hp * hp_n + hh
                raw, dpt, drow = prods[hh]
                off = runt_ref[pl.ds(head, 1), rows] - lse_ref[pl.ds(head, 1), rows]
                st = raw + (off - fk[hh])
                if diagonal:
                    st = jnp.where(_diag_mask(qb), st, -1e30)
                pt = jnp.exp(st)
                dst = pt * (dpt - drow)
                drow_ref[pl.ds(head, 1), rows] += jnp.sum(dst, axis=0, keepdims=True)
                folded = dst[:, 0:LANE]
                for c in range(1, qb // LANE):
                    folded = folded + dst[:, c * LANE:(c + 1) * LANE]
                pieces.append((_dot(dst, q_ref[0, hh, rows, :] * Q_SCALE), _dot(pt, do_ref[hh, rows, :]),
                               _dot(kt[hh], dst), folded))
            out = []
            for hh in range(hp_n):
                dqt_s[hh, :, rows] += pieces[hh][2]
                out.append((acc[hh][0] + pieces[hh][0], acc[hh][1] + pieces[hh][1], acc[hh][2] + pieces[hh][3]))
            return tuple(out)

        def qstep(i, carry):
            acc, prods = carry
            ahead = products(jnp.minimum(i + 1, nq - 1))
            return absorb(i, prods, acc, False), ahead

        init = tuple((jnp.zeros((qb, DH), F32), jnp.zeros((qb, DH), F32), jnp.zeros((qb, LANE), F32))
                     for _ in range(hp_n))
        first = products(j)
        ahead = products(jnp.minimum(j + 1, nq - 1))
        acc = absorb(j, first, init, True)
        acc, _ = lax.fori_loop(j + 1, nq, qstep, (acc, ahead))
        for hh in range(hp_n):
            dk, dv, ds_sum = acc[hh]
            dqkv_ref[1, hh, krows, :] = dk
            dqkv_ref[2, hh, krows, :] = dv
            dcol_ref[krows, :] += _to_lane(-jnp.sum(ds_sum, axis=1, keepdims=True), 2 * HEADS + hp * hp_n + hh)

        @pl.when(j == nq - 1)
        def _():
            for hh in range(hp_n):
                for b in range(nq):
                    rows = slice(b * qb, (b + 1) * qb)
                    dqkv_ref[0, hh, rows, :] = _tall_t(dqt_s[hh, :, rows]) * Q_SCALE

    full = pl.BlockSpec((hp_n, t, DH), lambda h, j: (h, 0, 0))
    rows8 = pl.BlockSpec((HEADS, t), lambda h, j: (0, 0))
    return pl.pallas_call(
        body, name="fox_bwd", grid=(HEADS // hp_n, nq),
        in_specs=[pl.BlockSpec((1, hp_n, t, DH), lambda h, j: (0, h, 0, 0)),
                  pl.BlockSpec((1, hp_n, qb, DH), lambda h, j: (1, h, j, 0)),
                  pl.BlockSpec((1, hp_n, qb, DH), lambda h, j: (2, h, j, 0)),
                  pl.BlockSpec((t, LANE), lambda h, j: (0, 0)), pl.BlockSpec((HEADS, t), lambda h, j: (2, 0)),
                  full, rows8, full],
        out_specs=[pl.BlockSpec((3, hp_n, t, DH), lambda h, j: (0, h, 0, 0)),
                   pl.BlockSpec((t, LANE), lambda h, j: (0, 0)), rows8],
        out_shape=[jax.ShapeDtypeStruct((3, HEADS, t, DH), F32), jax.ShapeDtypeStruct((t, LANE), F32),
                   jax.ShapeDtypeStruct((HEADS, t), F32)],
        scratch_shapes=[pltpu.VMEM((hp_n, DH, t), F32)],
        compiler_params=_params(("arbitrary", "arbitrary")),
    )(qkv, qkv, qkv, run, run_t, o, lse, do)


Z_COL0 = 3 * WIDTH // LANE
FGATE_COL0 = 7 * WIDTH // LANE


def _gdn_post(o, pm, onw):
    t = pm.shape[0]

    def body(o_ref, z_ref, w_ref, m_ref):
        z = z_ref[...]
        sz = z * _sigmoid(z)
        for hh in range(2):
            ov = o_ref[hh]
            n = ov * lax.rsqrt(jnp.mean(ov * ov, axis=-1, keepdims=True) + EPS) * w_ref[...]
            m_ref[:, hh * DH:(hh + 1) * DH] = (n * sz[:, hh * DH:(hh + 1) * DH]).astype(m_ref.dtype)

    return pl.pallas_call(
        body, name="gdn_post", grid=(WIDTH // LANE,),
        in_specs=[pl.BlockSpec((2, t, DH), lambda j: (j, 0, 0)), pl.BlockSpec((t, LANE), lambda j: (0, Z_COL0 + j)),
                  pl.BlockSpec((1, DH), lambda j: (0, 0))],
        out_specs=pl.BlockSpec((t, LANE), lambda j: (0, j)),
        out_shape=jax.ShapeDtypeStruct((t, WIDTH), MXU),
        compiler_params=_params(("arbitrary",)),
    )(o, pm, onw)


def _gdn_post_bwd(o, pm, onw, dmix):
    t = pm.shape[0]

    def body(o_ref, z_ref, w_ref, dm_ref, do_ref, dz_ref, dw_ref):
        @pl.when(pl.program_id(0) == 0)
        def _():
            dw_ref[...] = jnp.zeros_like(dw_ref)

        z = z_ref[...]
        sg = _sigmoid(z)
        sz = z * sg
        dsz = sg * (1.0 + z * (1.0 - sg))
        dm = dm_ref[...]
        for hh in range(2):
            cols = slice(hh * DH, (hh + 1) * DH)
            ov = o_ref[hh]
            r = lax.rsqrt(jnp.mean(ov * ov, axis=-1, keepdims=True) + EPS)
            xn = ov * r
            dmh = dm[:, cols]
            dn = dmh * sz[:, cols]
            dz_ref[:, cols] = dmh * (xn * w_ref[...]) * dsz[:, cols]
            dw_ref[...] += jnp.sum(dn * xn, axis=0, keepdims=True)
            g = dn * w_ref[...]
            do_ref[hh] = r * (g - xn * jnp.mean(g * xn, axis=-1, keepdims=True))

    return pl.pallas_call(
        body, name="gdn_post_bwd", grid=(WIDTH // LANE,),
        in_specs=[pl.BlockSpec((2, t, DH), lambda j: (j, 0, 0)), pl.BlockSpec((t, LANE), lambda j: (0, Z_COL0 + j)),
                  pl.BlockSpec((1, DH), lambda j: (0, 0)), pl.BlockSpec((t, LANE), lambda j: (0, j))],
        out_specs=[pl.BlockSpec((2, t, DH), lambda j: (j, 0, 0)), pl.BlockSpec((t, LANE), lambda j: (0, j)),
                   pl.BlockSpec((1, DH), lambda j: (0, 0))],
        out_shape=[jax.ShapeDtypeStruct((HEADS, t, DH), F32), jax.ShapeDtypeStruct((t, WIDTH), F32),
                   jax.ShapeDtypeStruct((1, DH), F32)],
        compiler_params=_params(("arbitrary",)),
    )(o, pm, onw, dmix)


def _fox_post(o, pm):
    t = pm.shape[0]

    def body(o_ref, g_ref, m_ref):
        sg = _sigmoid(g_ref[...])
        for hh in range(2):
            cols = slice(hh * DH, (hh + 1) * DH)
            m_ref[:, cols] = (o_ref[hh] * sg[:, cols]).astype(m_ref.dtype)

    return pl.pallas_call(
        body, name="fox_post", grid=(WIDTH // LANE,),
        in_specs=[pl.BlockSpec((2, t, DH), lambda j: (j, 0, 0)), pl.BlockSpec((t, LANE), lambda j: (0, FGATE_COL0 + j))],
        out_specs=pl.BlockSpec((t, LANE), lambda j: (0, j)),
        out_shape=jax.ShapeDtypeStruct((t, WIDTH), MXU),
        compiler_params=_params(("arbitrary",)),
    )(o, pm)


def _fox_post_bwd(o, pm, dmix):
    t = pm.shape[0]

    def body(o_ref, g_ref, dm_ref, do_ref, dg_ref):
        sg = _sigmoid(g_ref[...])
        dm = dm_ref[...]
        for hh in range(2):
            cols = slice(hh * DH, (hh + 1) * DH)
            do_ref[hh] = dm[:, cols] * sg[:, cols]
            dg_ref[:, cols] = dm[:, cols] * o_ref[hh] * (sg * (1.0 - sg))[:, cols]

    return pl.pallas_call(
        body, name="fox_post_bwd", grid=(WIDTH // LANE,),
        in_specs=[pl.BlockSpec((2, t, DH), lambda j: (j, 0, 0)), pl.BlockSpec((t, LANE), lambda j: (0, FGATE_COL0 + j)),
                  pl.BlockSpec((t, LANE), lambda j: (0, j))],
        out_specs=[pl.BlockSpec((2, t, DH), lambda j: (j, 0, 0)), pl.BlockSpec((t, LANE), lambda j: (0, j))],
        out_shape=[jax.ShapeDtypeStruct((HEADS, t, DH), F32), jax.ShapeDtypeStruct((t, WIDTH), F32)],
        compiler_params=_params(("arbitrary",)),
    )(o, pm, dmix)


def _tail(x, mixg, mixf, tgt, wo, n2w, wg, wu, wd, fw):
    t, d = x.shape
    dff = wg.shape[1]
    tb = min(TB, t)

    def body(x_ref, mg_ref, mf_ref, t_ref, wo_ref, n2_ref, wg_ref, wu_ref, wd_ref, fw_ref,
             h2_ref, act_ref, dgate_ref, dup_ref, dx3_ref, dx2_ref, dmg_ref, dmf_ref, dn2_ref, dfw_ref, loss_ref):
        @pl.when(pl.program_id(0) == 0)
        def _():
            dn2_ref[...] = jnp.zeros_like(dn2_ref)
            dfw_ref[...] = jnp.zeros_like(dfw_ref)
            loss_ref[...] = jnp.zeros_like(loss_ref)

        x2 = x_ref[...] + _dot(mg_ref[...], wo_ref[0:WIDTH, :]) + _dot(mf_ref[...], wo_ref[WIDTH:2 * WIDTH, :])
        r2 = lax.rsqrt(jnp.mean(x2 * x2, axis=-1, keepdims=True) + EPS)
        xn2 = x2 * r2
        h2 = (xn2 * n2_ref[...]).astype(MXU)
        h2_ref[...] = h2
        gate = _dot(h2, wg_ref[...])
        up = _dot(h2, wu_ref[...])
        sg = _sigmoid(gate)
        sl = gate * sg
        act = (sl * up).astype(MXU)
        act_ref[...] = act
        x3 = x2 + _dot(act, wd_ref[...])
        r3 = lax.rsqrt(jnp.mean(x3 * x3, axis=-1, keepdims=True) + EPS)
        xn3 = x3 * r3
        err = xn3 * fw_ref[...] - t_ref[...]
        loss_ref[...] += 0.5 * jnp.sum(jnp.mean(err * err, axis=-1, keepdims=True), axis=0, keepdims=True)
        dy = err * (1.0 / d)
        dfw_ref[...] += jnp.sum(dy * xn3, axis=0, keepdims=True)
        g3 = dy * fw_ref[...]
        dx3 = r3 * (g3 - xn3 * jnp.mean(g3 * xn3, axis=-1, keepdims=True))
        dx3_ref[...] = dx3.astype(MXU)
        dact = _dot_nt(dx3, wd_ref[...])
        dgate = (dact * up * (sg * (1.0 + gate * (1.0 - sg)))).astype(MXU)
        dup = (dact * sl).astype(MXU)
        dgate_ref[...] = dgate
        dup_ref[...] = dup
        dh2 = _dot_nt(dgate, wg_ref[...]) + _dot_nt(dup, wu_ref[...])
        dn2_ref[...] += jnp.sum(dh2 * xn2, axis=0, keepdims=True)
        g2 = dh2 * n2_ref[...]
        dx2 = dx3 + r2 * (g2 - xn2 * jnp.mean(g2 * xn2, axis=-1, keepdims=True))
        dx2_ref[...] = dx2
        dmg_ref[...] = _dot_nt(dx2, wo_ref[0:WIDTH, :])
        dmf_ref[...] = _dot_nt(dx2, wo_ref[WIDTH:2 * WIDTH, :])

    def tok(n):
        return pl.BlockSpec((tb, n), lambda i: (i, 0))

    acc = pl.BlockSpec((1, d), lambda i: (0, 0))
    sds = jax.ShapeDtypeStruct
    return pl.pallas_call(
        body, name="tail", grid=(t // tb,),
        in_specs=[tok(d), tok(WIDTH), tok(WIDTH), tok(d), _resident(wo.shape), _resident((1, d)),
                  _resident(wg.shape), _resident(wu.shape), _resident(wd.shape), _resident((1, d))],
        out_specs=[tok(d), tok(dff), tok(dff), tok(dff), tok(d), tok(d), tok(WIDTH), tok(WIDTH), acc, acc,
                   pl.BlockSpec((1, 1), lambda i: (0, 0))],
        out_shape=[sds((t, d), MXU), sds((t, dff), MXU), sds((t, dff), MXU), sds((t, dff), MXU), sds((t, d), MXU),
                   sds((t, d), F32), sds((t, WIDTH), F32), sds((t, WIDTH), F32), sds((1, d), F32), sds((1, d), F32),
                   sds((1, 1), F32)],
        compiler_params=_params(("arbitrary",)),
    )(x, mixg, mixf, tgt, wo, n2w, wg, wu, wd, fw)


def _wgrad(a, b, name):
    t, m = a.shape
    n = b.shape[1]
    bn = 256 if n % 256 == 0 else LANE

    def body(a_ref, b_ref, o_ref):
        o_ref[...] = _dot_tn(a_ref[...], b_ref[...]).astype(o_ref.dtype)

    return pl.pallas_call(
        body, name=name, grid=(n // bn,),
        in_specs=[_resident((t, m)), pl.BlockSpec((t, bn), lambda j: (0, j))],
        out_specs=pl.BlockSpec((m, bn), lambda j: (0, j)),
        out_shape=jax.ShapeDtypeStruct((m, n), WIRE),
        compiler_params=_params(("arbitrary",)),
    )(a, b)


def _split_w_in(w_in):
    a = 4 * WIDTH
    b = a + 2 * HEADS
    c = b + 4 * WIDTH
    main = jnp.concatenate([w_in[:, :a], w_in[:, b:c]], axis=1)
    small = jnp.concatenate([w_in[:, a:b], w_in[:, c:], jnp.zeros((w_in.shape[0], LANE - 3 * HEADS), w_in.dtype)], axis=1)
    return main, small


def _merge_dw_in(d_gdn, d_z, d_fox, d_fg, d_small):
    return jnp.concatenate([d_gdn, d_z, d_small[:, :2 * HEADS], d_fox, d_fg, d_small[:, 2 * HEADS:3 * HEADS]], axis=1)


def _lanes(*pieces):
    v = jnp.concatenate([p.reshape(-1).astype(F32) for p in pieces])
    return jnp.pad(v, (0, LANE - v.shape[0])).reshape(1, LANE)


def _vector_params(p):
    d = p["norm1_w"].size
    gparams = jnp.concatenate([_lanes(jnp.zeros(HEADS), p["gdn_dt_bias"], p["fox_f_bias"]),
                               _lanes(jnp.zeros(HEADS), p["gdn_A_log"]), jnp.zeros((6, LANE), F32)])
    fox_nw = jnp.stack([jnp.tile(p["fox_q_norm_w"].reshape(-1), 2), jnp.tile(p["fox_k_norm_w"].reshape(-1), 2),
                        jnp.ones((LANE,), F32)])
    return dict(n1w=p["norm1_w"].reshape(1, d), n2w=p["norm2_w"].reshape(1, d), fw=p["final_norm_w"].reshape(1, d),
                onw=p["gdn_out_norm_w"].reshape(1, DH), gparams=gparams, fox_nw=fox_nw)


def _mixer_forward(x, vp, w_main, w_small, conv_w):
    h1, pm, ps = _inproj(x, vp["n1w"], w_main, w_small)
    gates, run, tot, run_t = _gates(ps, vp["gparams"])
    gqkv = _gdn_prep(pm, conv_w)
    o_gdn, states, inv = _gdn_fwd(gqkv, gates, run, tot, run_t)
    mixg = _gdn_post(o_gdn, pm, vp["onw"])
    fqkv = _fox_prep(pm, vp["fox_nw"])
    o_fox, lse = _fox_fwd(fqkv, run, run_t)
    mixf = _fox_post(o_fox, pm)
    return dict(h1=h1, pm=pm, ps=ps, gates=gates, run=run, tot=tot, run_t=run_t, gqkv=gqkv, o_gdn=o_gdn, states=states,
                inv=inv, mixg=mixg, fqkv=fqkv, o_fox=o_fox, lse=lse, mixf=mixf)


def _mixer_backward(x, vp, w_main, w_small, conv_w, f, dx2, dmixg, dmixf, onw):
    pm = f["pm"]
    do_gdn, dz, donw = _gdn_post_bwd(f["o_gdn"], pm, onw, dmixg)
    dgqkv, dcol_g, dtot_g, drow_g = _gdn_bwd(f["gqkv"], f["gates"], f["run"], f["tot"], f["run_t"], f["inv"], f["states"], do_gdn)
    dgdn, dconv = _gdn_prep_bwd(pm, conv_w, dgqkv)
    do_fox, dfg = _fox_post_bwd(f["o_fox"], pm, dmixf)
    dfqkv, dcol_f, drow_f = _fox_bwd(f["fqkv"], f["run"], f["run_t"], f["o_fox"], f["lse"], do_fox)
    dfox, dfnw = _fox_prep_bwd(pm, vp["fox_nw"], dfqkv)
    dps, gsum = _gates_bwd(f["ps"], vp["gparams"], dcol_g, dtot_g, dcol_f, drow_g, drow_f)
    grad_x, dn1w = _inproj_bwd(x, vp["n1w"], dx2, dgdn, dz, dfox, dfg, dps, w_main, w_small)
    h1 = f["h1"]
    dw_in = _merge_dw_in(_wgrad(h1, dgdn, "dw_in_gdn"), _wgrad(h1, dz, "dw_in_z"), _wgrad(h1, dfox, "dw_in_fox"),
                         _wgrad(h1, dfg, "dw_in_fgate"), _wgrad(h1, dps, "dw_in_small"))
    return grad_x, dw_in, dconv, dict(dn1w=dn1w, gsum=gsum, donw=donw, dfnw=dfnw)


VECTORS = ("norm1_w", "norm2_w", "final_norm_w", "gdn_A_log", "gdn_dt_bias", "gdn_out_norm_w", "fox_f_bias",
           "fox_q_norm_w", "fox_k_norm_w")
VEC_ROWS = 16
LOSS_ROW = len(VECTORS)


def _pack_vectors(dn1w, dn2w, dfw, gsum, donw, dfnw, loss):
    d = dn1w.shape[1]

    def body(n1_ref, n2_ref, fw_ref, gs_ref, on_ref, fn_ref, loss_ref, o_ref):
        o_ref[...] = jnp.zeros_like(o_ref)
        o_ref[0:1, :] = n1_ref[...]
        o_ref[1:2, :] = n2_ref[...]
        o_ref[2:3, :] = fw_ref[...]
        o_ref[3:4, 0:HEADS] = gs_ref[0:1, 0:HEADS]
        o_ref[4:5, 0:HEADS] = gs_ref[1:2, 0:HEADS]
        o_ref[5:6, 0:DH] = on_ref[...]
        o_ref[6:7, 0:HEADS] = gs_ref[2:3, 0:HEADS]
        for kind in range(2):
            v = fn_ref[kind, 0]
            for j in range(1, fn_ref.shape[1]):
                v = v + fn_ref[kind, j]
            o_ref[7 + kind:8 + kind, 0:DH] = v[:, :DH] + v[:, DH:]
        o_ref[LOSS_ROW:LOSS_ROW + 1, 0:1] = loss_ref[...]

    return pl.pallas_call(body, name="pack_vectors", out_shape=jax.ShapeDtypeStruct((VEC_ROWS, d), F32),
                          compiler_params=_params())(dn1w, dn2w, dfw, gsum, donw, dfnw, loss)


def _local_step(x, tgt, p, w_in, conv_w, wo, wg, wu, wd):
    vp = _vector_params(p)
    w_main, w_small = _split_w_in(w_in)
    f = _mixer_forward(x, vp, w_main, w_small, conv_w)
    (h2, act, dgate, dup, dx3, dx2, dmixg, dmixf, dn2w, dfw, loss) = _tail(
        x, f["mixg"], f["mixf"], tgt, wo, vp["n2w"], wg, wu, wd, vp["fw"])
    grad_x, dw_in, dconv, small = _mixer_backward(x, vp, w_main, w_small, conv_w, f, dx2, dmixg, dmixf, vp["onw"])
    grads = {"w_in": dw_in, "gdn_conv_w": dconv,
             "w_out": jnp.concatenate([_wgrad(f["mixg"], dx2, "dw_out_gdn"), _wgrad(f["mixf"], dx2, "dw_out_fox")], axis=0),
             "w_ffn_gate": _wgrad(h2, dgate, "dw_gate"), "w_ffn_up": _wgrad(h2, dup, "dw_up"),
             "w_ffn_down": _wgrad(act, dx3, "dw_down")}
    vec = _pack_vectors(small["dn1w"], dn2w, dfw, small["gsum"], small["donw"], small["dfnw"], loss)
    return grad_x, grads, vec


def _my_place():
    return lax.axis_index("x"), lax.axis_index("y"), lax.axis_index("c")


def _peers():
    x, y, c = _my_place()
    peers = []
    for k in range(1, N_DEV):
        px = 1 - x if k & 4 else x
        py = 1 - y if k & 2 else y
        pc = 1 - c if k & 1 else c
        peers.append(((px, py, pc), 4 * px + 2 * py + pc))
    return 4 * x + 2 * y + c, peers


def _spread_copies(kind, srcs, lands, send_sems, recv_sems):
    me, peers = _peers()
    kinds = [kind] * len(srcs) if isinstance(kind, str) else kind
    remote, local = [], []
    for i, (kd, src, land) in enumerate(zip(kinds, srcs, lands)):
        for k, (dev, idx) in enumerate(peers):
            remote.append(pltpu.make_async_remote_copy(
                src_ref=src if kd == "gather" else src.at[idx], dst_ref=land.at[me],
                send_sem=send_sems.at[i * (N_DEV - 1) + k], recv_sem=recv_sems.at[i * (N_DEV - 1) + k],
                device_id=dev, device_id_type=MESH))
        local.append((src if kd == "gather" else src.at[me], land.at[me]))
    return remote, local


def _land_shape(kind, a):
    return (N_DEV,) + a.shape if kind == "gather" else a.shape


HBM = pl.BlockSpec(memory_space=pltpu.HBM)
SEM = pl.BlockSpec(memory_space=pltpu.SEMAPHORE)


def _hbm(a):
    return pltpu.with_memory_space_constraint(a, pltpu.HBM)


def _spread_start(arrays, kind, name):
    n = len(arrays)

    def body(*refs):
        srcs, lands = refs[:n], refs[n:2 * n]
        send_sems, recv_sems = refs[2 * n], refs[2 * n + 1]
        token = refs[4 * n + 2]
        local_sems = refs[4 * n + 3]
        remote, local = _spread_copies(kind, srcs, lands, send_sems, recv_sems)
        for cp in remote:
            cp.start()
        copies = [pltpu.make_async_copy(s, d, local_sems.at[i]) for i, (s, d) in enumerate(local)]
        for cp in copies:
            cp.start()
        for cp in copies:
            cp.wait()
        token[...] = jnp.zeros_like(token)

    sems = (pltpu.SemaphoreType.DMA((n * (N_DEV - 1),)),) * 2
    lands = [lax.empty(_land_shape(kind, a), a.dtype) for a in arrays]
    out = pl.pallas_call(
        body, name=name,
        out_shape=sems + tuple(pltpu.HBM(a.shape, a.dtype) for a in list(arrays) + lands)
        + (jax.ShapeDtypeStruct((8, LANE), F32),),
        in_specs=[HBM] * (2 * n), out_specs=tuple([SEM] * 2 + [HBM] * (2 * n) + [pl.BlockSpec(memory_space=pltpu.VMEM)]),
        input_output_aliases={j: 2 + j for j in range(2 * n)},
        scratch_shapes=[pltpu.SemaphoreType.DMA((n,))],
        compiler_params=pltpu.CompilerParams(has_side_effects=pltpu.SideEffectType.DATAFLOW_SIDE_EFFECTING),
    )(*[_hbm(a) for a in arrays], *[_hbm(a) for a in lands])
    return (list(out[2:2 + n]), list(out[2 + n:2 + 2 * n]), out[0], out[1]), out[-1]


def _spread_wait(state, kind, after, name):
    srcs, lands, send_sems, recv_sems = state
    n = len(srcs)

    def body(*refs):
        remote, _ = _spread_copies(kind, refs[:n], refs[n:2 * n], refs[2 * n], refs[2 * n + 1])
        for cp in remote:
            cp.wait_send()
        for cp in remote:
            cp.wait_recv()

    out = pl.pallas_call(
        body, name=name,
        out_shape=tuple(pltpu.HBM(a.shape, a.dtype) for a in srcs + lands),
        in_specs=[HBM] * (2 * n) + [SEM, SEM, pl.BlockSpec(memory_space=pl.ANY)], out_specs=tuple([HBM] * (2 * n)),
        input_output_aliases={j: j for j in range(2 * n)},
        compiler_params=pltpu.CompilerParams(has_side_effects=pltpu.SideEffectType.DATAFLOW_SIDE_EFFECTING),
    )(*srcs, *lands, send_sems, recv_sems, after)
    return list(out[n:])


def _spread_async(arrays, kind, name, collective_id):
    n = len(arrays)
    hbm = pltpu.MemorySpace.HBM
    src_refs = [jax.new_ref(a, memory_space=hbm) for a in arrays]
    land_refs = [jax.empty_ref(jax.ShapeDtypeStruct(_land_shape(kind, a), a.dtype), memory_space=hbm) for a in arrays]

    @pl.kernel(mesh=plsc.ScalarSubcoreMesh(axis_name="sequencer", num_cores=1), name=name,
               scratch_types=(pltpu.SemaphoreType.DMA((n * (N_DEV - 1),)), pltpu.SemaphoreType.DMA((n * (N_DEV - 1),)),
                              pltpu.SemaphoreType.DMA((n,))),
               compiler_params=pltpu.CompilerParams(collective_id=collective_id))
    def launch(send_sems, recv_sems, local_sems):
        _, peers = _peers()
        barrier = pltpu.get_barrier_semaphore()
        for dev, _ in peers:
            pl.semaphore_signal(barrier, inc=1, device_id=dev, device_id_type=MESH)
        pl.semaphore_wait(barrier, N_DEV - 1)
        remote, local = _spread_copies(kind, src_refs, land_refs, send_sems, recv_sems)
        copies = [pltpu.make_async_copy(s, d, local_sems.at[i]) for i, (s, d) in enumerate(local)]
        for cp in remote + copies:
            cp.start()
        for cp in remote:
            cp.wait_recv()
        for cp in remote:
            cp.wait_send()
        for cp in copies:
            cp.wait()

    launch()
    return [r[...] for r in land_refs]


def _exchange(scatter, gather, name):
    kinds = ["scatter"] * len(scatter) + ["gather"] * len(gather)
    arrays = list(scatter) + list(gather)
    n = len(arrays)

    def body(*refs):
        send_sems, recv_sems, local_sems = refs[2 * n:]
        remote, local = _spread_copies(kinds, refs[:n], refs[n:2 * n], send_sems, recv_sems)
        copies = [pltpu.make_async_copy(s, d, local_sems.at[i]) for i, (s, d) in enumerate(local)]
        for cp in remote + copies:
            cp.start()
        for cp in remote:
            cp.wait_recv()
        for cp in remote:
            cp.wait_send()
        for cp in copies:
            cp.wait()

    return pl.pallas_call(
        body, name=name,
        out_shape=[jax.ShapeDtypeStruct(_land_shape(k, a), a.dtype) for k, a in zip(kinds, arrays)],
        in_specs=[pl.BlockSpec(memory_space=pl.ANY)] * n, out_specs=[pl.BlockSpec(memory_space=pl.ANY)] * n,
        scratch_shapes=[pltpu.SemaphoreType.DMA((n * (N_DEV - 1),)), pltpu.SemaphoreType.DMA((n * (N_DEV - 1),)),
                        pltpu.SemaphoreType.DMA((n,))],
    )(*arrays)


ADAM_ROWS = 128


def _adam_math(g, w, m, v):
    nm = ADAM_B1 * m + (1.0 - ADAM_B1) * g
    nv = ADAM_B2 * v + (1.0 - ADAM_B2) * (g * g)
    m_hat = nm / (1.0 - ADAM_B1 ** ADAM_STEP)
    v_hat = nv / (1.0 - ADAM_B2 ** ADAM_STEP)
    return -ADAM_LR * (m_hat / (jnp.sqrt(v_hat) + ADAM_EPS) + ADAM_WD * w), nm, nv


def _sum_parts(p_ref):
    g = p_ref[0].astype(F32)
    for s in range(1, N_DEV):
        g = g + p_ref[s].astype(F32)
    return g


def _adam_matrix(parts, w, m, v, name):
    _, r, c = w.shape
    rb = ADAM_ROWS if r % ADAM_ROWS == 0 else r

    def body(p_ref, w_ref, m_ref, v_ref, g_ref, d_ref, nm_ref, nv_ref):
        g = _sum_parts(p_ref)
        g_ref[0] = g
        d_ref[0], nm_ref[0], nv_ref[0] = _adam_math(g, w_ref[0], m_ref[0], v_ref[0])

    blk = pl.BlockSpec((1, rb, c), lambda i: (0, i, 0))
    return pl.pallas_call(
        body, name=name, grid=(r // rb,),
        in_specs=[pl.BlockSpec((N_DEV, rb, c), lambda i: (0, i, 0)), blk, blk, blk],
        out_specs=[blk] * 4, out_shape=[jax.ShapeDtypeStruct(w.shape, F32)] * 4,
        compiler_params=_params(("arbitrary",)),
    )(parts, w, m, v)


def _adam_vectors(parts, ws, ms, vs):
    nv = len(ws)

    def body(*refs):
        p_ref = refs[0]
        w_refs, m_refs, v_refs = refs[1:1 + nv], refs[1 + nv:1 + 2 * nv], refs[1 + 2 * nv:1 + 3 * nv]
        outs = refs[1 + 3 * nv:]
        g_all = _sum_parts(p_ref)
        for i in range(nv):
            n = w_refs[i].shape[1]
            g = g_all[i:i + 1, 0:n]
            d, nm, nvv = _adam_math(g, w_refs[i][...], m_refs[i][...], v_refs[i][...])
            outs[i][...] = g
            outs[nv + i][...] = d
            outs[2 * nv + i][...] = nm
            outs[3 * nv + i][...] = nvv
        outs[4 * nv][...] = g_all[LOSS_ROW:LOSS_ROW + 1, 0:1]

    shapes = [jax.ShapeDtypeStruct(a.shape, F32) for a in ws]
    out = pl.pallas_call(body, name="adam_vectors", out_shape=shapes * 4 + [jax.ShapeDtypeStruct((1, 1), F32)],
                         compiler_params=_params())(parts, *ws, *ms, *vs)
    return out[:nv], out[nv:2 * nv], out[2 * nv:3 * nv], out[3 * nv:4 * nv], out[4 * nv]


MATRICES = (("w_in", 1), ("gdn_conv_w", 1), ("w_out", 0), ("w_ffn_gate", 1), ("w_ffn_up", 1), ("w_ffn_down", 0))
WEIGHTS = ("norm1_w", "w_in", "gdn_conv_w", "gdn_A_log", "gdn_dt_bias", "gdn_out_norm_w", "fox_f_bias", "fox_q_norm_w",
           "fox_k_norm_w", "w_out", "norm2_w", "w_ffn_gate", "w_ffn_up", "w_ffn_down", "final_norm_w")


def _join(blocks, axis):
    _, r, c = blocks.shape
    if axis == 0:
        return blocks.reshape(N_DEV * r, c)
    return blocks.transpose(1, 0, 2).reshape(r, N_DEV * c)


def _cut(full, axis):
    r, c = full.shape
    if axis == 0:
        return full.reshape(N_DEV, r // N_DEV, c)
    return full.reshape(r, N_DEV, c // N_DEV).transpose(1, 0, 2)


def kernel(x, norm1_w, w_in, gdn_conv_w, gdn_A_log, gdn_dt_bias, gdn_out_norm_w, fox_f_bias, fox_q_norm_w, fox_k_norm_w, w_out, norm2_w, w_ffn_gate, w_ffn_up, w_ffn_down, final_norm_w, loss_target, m_norm1_w, m_w_in, m_gdn_conv_w, m_gdn_A_log, m_gdn_dt_bias, m_gdn_out_norm_w, m_fox_f_bias, m_fox_q_norm_w, m_fox_k_norm_w, m_w_out, m_norm2_w, m_w_ffn_gate, m_w_ffn_up, m_w_ffn_down, m_final_norm_w, v_norm1_w, v_w_in, v_gdn_conv_w, v_gdn_A_log, v_gdn_dt_bias, v_gdn_out_norm_w, v_fox_f_bias, v_fox_q_norm_w, v_fox_k_norm_w, v_w_out, v_norm2_w, v_w_ffn_gate, v_w_ffn_up, v_w_ffn_down, v_final_norm_w):
    w = dict(norm1_w=norm1_w, w_in=w_in, gdn_conv_w=gdn_conv_w, gdn_A_log=gdn_A_log, gdn_dt_bias=gdn_dt_bias,
             gdn_out_norm_w=gdn_out_norm_w, fox_f_bias=fox_f_bias, fox_q_norm_w=fox_q_norm_w, fox_k_norm_w=fox_k_norm_w,
             w_out=w_out, norm2_w=norm2_w, w_ffn_gate=w_ffn_gate, w_ffn_up=w_ffn_up, w_ffn_down=w_ffn_down,
             final_norm_w=final_norm_w)
    m = dict(norm1_w=m_norm1_w, w_in=m_w_in, gdn_conv_w=m_gdn_conv_w, gdn_A_log=m_gdn_A_log, gdn_dt_bias=m_gdn_dt_bias,
             gdn_out_norm_w=m_gdn_out_norm_w, fox_f_bias=m_fox_f_bias, fox_q_norm_w=m_fox_q_norm_w,
             fox_k_norm_w=m_fox_k_norm_w, w_out=m_w_out, norm2_w=m_norm2_w, w_ffn_gate=m_w_ffn_gate,
             w_ffn_up=m_w_ffn_up, w_ffn_down=m_w_ffn_down, final_norm_w=m_final_norm_w)
    v = dict(norm1_w=v_norm1_w, w_in=v_w_in, gdn_conv_w=v_gdn_conv_w, gdn_A_log=v_gdn_A_log, gdn_dt_bias=v_gdn_dt_bias,
             gdn_out_norm_w=v_gdn_out_norm_w, fox_f_bias=v_fox_f_bias, fox_q_norm_w=v_fox_q_norm_w,
             fox_k_norm_w=v_fox_k_norm_w, w_out=v_w_out, norm2_w=v_norm2_w, w_ffn_gate=v_w_ffn_gate,
             w_ffn_up=v_w_ffn_up, w_ffn_down=v_w_ffn_down, final_norm_w=v_final_norm_w)
    axis_of = dict(MATRICES)
    late = ("w_out", "w_ffn_gate", "w_ffn_up", "w_ffn_down")
    xs, tgt = x[0], loss_target[0]
    vp = _vector_params({n: w[n] for n in VECTORS})

    d_model = xs.shape[1]
    late_shapes = [w[n].shape[1:] for n in late]

    def pack(blocks):
        return jnp.concatenate([b.reshape(b.shape[:-2] + (-1, d_model)) for b in blocks], axis=-2)

    def unpack(packed):
        out, off = [], 0
        for r, c in late_shapes:
            rows = r * c // d_model
            out.append(packed[..., off:off + rows, :].reshape(packed.shape[:-2] + (r, c)))
            off += rows
        return out

    state_in, token = _spread_start([w["w_in"][0].astype(WIRE), w["gdn_conv_w"][0]], "gather", "gather_in_start")
    state_late, token = _spread_start([pack([w[n][0].astype(WIRE) for n in late]) + token[0, 0].astype(WIRE)],
                                      "gather", "gather_late_start")
    w_in_blocks, conv_blocks = _spread_wait(state_in, "gather", token, "gather_in_wait")
    w_main, w_small = _split_w_in(_join(w_in_blocks, 1))
    conv_w = _join(conv_blocks, 1)
    f = _mixer_forward(xs, vp, w_main, w_small, conv_w)
    (late_packed,) = _spread_wait(state_late, "gather", f["mixf"], "gather_late_wait")
    full = {n: _join(b, axis_of[n]) for n, b in zip(late, unpack(late_packed))}

    (h2, act, dgate, dup, dx3, dx2, dmixg, dmixf, dn2w, dfw, loss) = _tail(
        xs, f["mixg"], f["mixf"], tgt, full["w_out"], vp["n2w"], full["w_ffn_gate"], full["w_ffn_up"],
        full["w_ffn_down"], vp["fw"])
    dlate = {"w_out": jnp.concatenate([_wgrad(f["mixg"], dx2, "dw_out_gdn"), _wgrad(f["mixf"], dx2, "dw_out_fox")], axis=0),
             "w_ffn_gate": _wgrad(h2, dgate, "dw_gate"), "w_ffn_up": _wgrad(h2, dup, "dw_up"),
             "w_ffn_down": _wgrad(act, dx3, "dw_down")}

    state_grads, token = _spread_start([pack([_cut(dlate[n], axis_of[n]) for n in late])], "scatter", "grads_late_start")
    grad_x, dw_in, dconv, small = _mixer_backward(xs, vp, w_main, w_small, conv_w, f, dx2, dmixg, dmixf,
                                                  vp["onw"] + token[0:1, 0:DH])
    vec = _pack_vectors(small["dn1w"], dn2w, dfw, small["gsum"], small["donw"], small["dfnw"], loss)
    parts = {}
    parts["w_in"], parts["gdn_conv_w"], parts_vec = _exchange([_cut(dw_in, 1), _cut(dconv, 1)], [vec], "exchange_last")
    (grads_packed,) = _spread_wait(state_grads, "scatter", parts_vec, "grads_late_wait")
    parts.update(zip(late, unpack(grads_packed)))

    results = [{}, {}, {}, {}]
    for n, _ in MATRICES:
        for d, a in zip(results, _adam_matrix(parts[n], w[n], m[n], v[n], "adam_" + n)):
            d[n] = a
    row = lambda a: a.reshape(1, -1)
    *vec_out, total_loss = _adam_vectors(parts_vec, [row(w[n]) for n in VECTORS], [row(m[n]) for n in VECTORS],
                                         [row(v[n]) for n in VECTORS])
    for d, arrs in zip(results, vec_out):
        for n, a in zip(VECTORS, arrs):
            d[n] = a.reshape(w[n].shape)
    return (total_loss[0, 0], grad_x[None], *[d[n] for d in results for n in WEIGHTS])
```

```python
import functools

import jax
import jax.numpy as jnp
from jax import lax
from jax.experimental import pallas as pl
from jax.experimental.pallas import tpu as pltpu
from jax.experimental.pallas import tpu_sc as plsc

F32 = jnp.float32
MXU = jnp.bfloat16
WIRE = jnp.bfloat16
HI = lax.Precision.HIGHEST
EPS = 1e-6

N_DEV = 8
HEADS = 8
DH = 64
WIDTH = HEADS * DH
CHUNK = 64
LANE = 128
ROW_ALIGN = 16
TB = 256
QB = 256
VMEM_LIMIT = 60 * 1024 * 1024

ADAM_LR = 0.001
ADAM_B1 = 0.9
ADAM_B2 = 0.999
ADAM_EPS = 1e-08
ADAM_WD = 0.01
ADAM_STEP = 10

MESH = pl.DeviceIdType.MESH


def _params(sem=None):
    return pltpu.CompilerParams(dimension_semantics=sem, vmem_limit_bytes=VMEM_LIMIT)


def _resident(shape):
    n = len(shape)
    return pl.BlockSpec(shape, lambda *_: (0,) * n, pipeline_mode=pl.Buffered(1))


def _dot(a, b):
    return jnp.dot(a.astype(MXU), b.astype(MXU), preferred_element_type=F32)


def _dot_nt(a, b):
    return lax.dot_general(a.astype(MXU), b.astype(MXU), (((1,), (1,)), ((), ())), preferred_element_type=F32)


def _dot_tn(a, b):
    return lax.dot_general(a.astype(MXU), b.astype(MXU), (((0,), (0,)), ((), ())), preferred_element_type=F32)


def _hdot(a, b):
    return jnp.dot(a, b, precision=HI, preferred_element_type=F32)


def _hdot_nt(a, b):
    return lax.dot_general(a, b, (((1,), (1,)), ((), ())), precision=HI, preferred_element_type=F32)


def _hdot_tn(a, b):
    return lax.dot_general(a, b, (((0,), (0,)), ((), ())), precision=HI, preferred_element_type=F32)


def _sigmoid(x):
    return 0.5 * jnp.tanh(0.5 * x) + 0.5


def _softplus(x):
    return jnp.maximum(x, 0.0) + jnp.log(1.0 + jnp.exp(-jnp.abs(x)))


def _head_sum_matrix():
    ri = lax.broadcasted_iota(jnp.int32, (LANE, LANE), 0) // DH
    ci = lax.broadcasted_iota(jnp.int32, (LANE, LANE), 1) // DH
    return (ri == ci).astype(F32)


def _group_sum(a, ones_matrix):
    hi = a.astype(jnp.bfloat16)
    lo = (a - hi.astype(F32)).astype(jnp.bfloat16)
    m = ones_matrix.astype(jnp.bfloat16)
    return jnp.dot(hi, m, preferred_element_type=F32) + jnp.dot(lo, m, preferred_element_type=F32)


def _shift_down(x, s):
    return pltpu.roll(x, s, 0)


def _shift_up(x, s):
    return pltpu.roll(x, x.shape[0] - s, 0)


def _inproj(x, n1w, w_main, w_small):
    t, d = x.shape
    nm = w_main.shape[1]
    tb = min(TB, t)

    def body(x_ref, nw_ref, wm_ref, ws_ref, h_ref, pm_ref, ps_ref):
        xv = x_ref[...]
        r = lax.rsqrt(jnp.mean(xv * xv, axis=-1, keepdims=True) + EPS)
        h = (xv * r * nw_ref[...]).astype(MXU)
        h_ref[...] = h
        pm_ref[...] = jnp.dot(h, wm_ref[...], preferred_element_type=F32)
        ps_ref[...] = jnp.dot(h, ws_ref[...], preferred_element_type=F32)

    return pl.pallas_call(
        body, name="inproj", grid=(t // tb,),
        in_specs=[pl.BlockSpec((tb, d), lambda i: (i, 0)), _resident((1, d)), _resident((d, nm)), _resident((d, LANE))],
        out_specs=[pl.BlockSpec((tb, d), lambda i: (i, 0)), pl.BlockSpec((tb, nm), lambda i: (i, 0)),
                   pl.BlockSpec((tb, LANE), lambda i: (i, 0))],
        out_shape=[jax.ShapeDtypeStruct((t, d), MXU), jax.ShapeDtypeStruct((t, nm), F32),
                   jax.ShapeDtypeStruct((t, LANE), F32)],
        compiler_params=_params(("arbitrary",)),
    )(x, n1w, w_main, w_small)


def _inproj_bwd(x, n1w, dx2, dgdn, dz, dfox, dfg, dps, w_main, w_small):
    t, d = x.shape
    tb = min(TB, t)
    w3 = 3 * WIDTH

    def body(x_ref, nw_ref, dx2_ref, dgdn_ref, dz_ref, dfox_ref, dfg_ref, dps_ref, wm_ref, ws_ref, gx_ref, dnw_ref):
        dh = _dot_nt(dgdn_ref[...], wm_ref[:, 0:w3])
        dh += _dot_nt(dz_ref[...], wm_ref[:, w3:w3 + WIDTH])
        dh += _dot_nt(dfox_ref[...], wm_ref[:, w3 + WIDTH:2 * w3 + WIDTH])
        dh += _dot_nt(dfg_ref[...], wm_ref[:, 2 * w3 + WIDTH:2 * w3 + 2 * WIDTH])
        dh += _dot_nt(dps_ref[...], ws_ref[...])
        xv = x_ref[...]
        r = lax.rsqrt(jnp.mean(xv * xv, axis=-1, keepdims=True) + EPS)
        xn = xv * r

        @pl.when(pl.program_id(0) == 0)
        def _():
            dnw_ref[...] = jnp.zeros_like(dnw_ref)

        dnw_ref[...] += jnp.sum(dh * xn, axis=0, keepdims=True)
        g = dh * nw_ref[...]
        gx_ref[...] = dx2_ref[...] + r * (g - xn * jnp.mean(g * xn, axis=-1, keepdims=True))

    def tok(n):
        return pl.BlockSpec((tb, n), lambda i: (i, 0))

    return pl.pallas_call(
        body, name="inproj_bwd", grid=(t // tb,),
        in_specs=[tok(d), _resident((1, d)), tok(d), tok(w3), tok(WIDTH), tok(w3), tok(WIDTH), tok(LANE),
                  _resident(w_main.shape), _resident(w_small.shape)],
        out_specs=[tok(d), pl.BlockSpec((1, d), lambda i: (0, 0))],
        out_shape=[jax.ShapeDtypeStruct((t, d), F32), jax.ShapeDtypeStruct((1, d), F32)],
        compiler_params=_params(("arbitrary",)),
    )(x, n1w, dx2, dgdn, dz, dfox, dfg, dps, w_main, w_small)


def _gate_lanes(shape):
    lane = lax.broadcasted_iota(jnp.int32, shape, 1)
    return lane < HEADS, (lane >= HEADS) & (lane < 2 * HEADS), (lane >= 2 * HEADS) & (lane < 3 * HEADS)


def _block_masks():
    ri = lax.broadcasted_iota(jnp.int32, (LANE, LANE), 0)
    ci = lax.broadcasted_iota(jnp.int32, (LANE, LANE), 1)
    same = (ri // CHUNK) == (ci // CHUNK)
    return ((ri >= ci).astype(F32), (ri <= ci).astype(F32), (same & (ri >= ci)).astype(F32),
            (same & (ri <= ci)).astype(F32), same.astype(F32))


def _gates(ps, gparams):
    t = ps.shape[0]
    nb = t // LANE

    def body(ps_ref, gp_ref, out_ref, run_ref, tot_ref, runt_ref):
        p = ps_ref[...]
        is_b, is_a, is_f = _gate_lanes(p.shape)
        z = p + gp_ref[0:1, :]
        neg_exp_a = -jnp.exp(gp_ref[1:2, :])
        glog = neg_exp_a * _softplus(z)
        logf = -_softplus(-z)
        out_ref[...] = jnp.where(is_b, _sigmoid(p), jnp.where(is_a, glog, jnp.where(is_f, logf, 0.0)))
        tril, _, tril_c, _, same_c = _block_masks()
        off = jnp.zeros((1, LANE), F32)
        for b in range(nb):
            rows = slice(b * LANE, (b + 1) * LANE)
            blk = out_ref[rows, :]
            ga = jnp.where(is_a[:LANE], blk, 0.0)
            fb = _hdot(tril, jnp.where(is_f[:LANE], blk, 0.0)) + off
            run = fb + _hdot(tril_c, ga)
            run_ref[rows, :] = run
            runt_ref[:, rows] = run.T
            tot_ref[rows, :] = _hdot(same_c, ga)
            off = fb[LANE - 1:LANE, :]

    return pl.pallas_call(
        body, name="gates",
        out_shape=[jax.ShapeDtypeStruct((t, LANE), F32)] * 3 + [jax.ShapeDtypeStruct((LANE, t), F32)],
        compiler_params=_params(),
    )(ps, gparams)


def _gates_bwd(ps, gparams, dcol_g, dtot_g, dcol_f, drow_g, drow_f):
    t = ps.shape[0]
    nb = t // LANE

    def body(ps_ref, gp_ref, dcg_ref, dtg_ref, dcf_ref, drg_ref, drf_ref, dps_ref, sums_ref, dl_ref, d0_ref, tr_ref):
        p = ps_ref[...]
        is_b, is_a, is_f = _gate_lanes(p.shape)
        _, triu, _, triu_c, same_c = _block_masks()
        tr_ref[...] = jnp.zeros_like(tr_ref)
        off = jnp.zeros((1, LANE), F32)
        for b in reversed(range(nb)):
            rows = slice(b * LANE, (b + 1) * LANE)
            tr_ref[HEADS:2 * HEADS, :] = drg_ref[:, rows]
            tr_ref[2 * HEADS:3 * HEADS, :] = drf_ref[:, rows]
            d = dcg_ref[rows, :] + dcf_ref[rows, :] + tr_ref[...].T
            d0_ref[rows, :] = d
            dlf = _hdot(triu, jnp.where(is_f[:LANE], d, 0.0)) + off
            dla = (_hdot(triu_c, jnp.where(is_a[:LANE], d, 0.0))
                   + _hdot(same_c, jnp.where(is_a[:LANE], dtg_ref[rows, :], 0.0)))
            dl_ref[rows, :] = dlf + dla
            off = dlf[0:1, :]
        z = p + gp_ref[0:1, :]
        neg_exp_a = -jnp.exp(gp_ref[1:2, :])
        sb = _sigmoid(p)
        glog = neg_exp_a * _softplus(z)
        dl = dl_ref[...]
        dp = jnp.where(is_b, d0_ref[...] * sb * (1.0 - sb),
                       jnp.where(is_a, dl * neg_exp_a * _sigmoid(z), jnp.where(is_f, dl * _sigmoid(-z), 0.0)))
        dps_ref[...] = dp
        s_a = jnp.sum(jnp.where(is_a, dl * glog, 0.0), axis=0, keepdims=True)
        s_p = jnp.sum(jnp.where(is_b, 0.0, dp), axis=0, keepdims=True)
        row = lax.broadcasted_iota(jnp.int32, (8, LANE), 0)
        from_a = pltpu.roll(jnp.where(row == 0, s_a, jnp.where(row == 1, s_p, 0.0)), LANE - HEADS, 1)
        from_f = pltpu.roll(jnp.where(row == 2, s_p, 0.0), LANE - 2 * HEADS, 1)
        lane = lax.broadcasted_iota(jnp.int32, (8, LANE), 1)
        sums_ref[...] = jnp.where(lane < HEADS, from_a + from_f, 0.0)

    return pl.pallas_call(
        body, name="gates_bwd",
        out_shape=[jax.ShapeDtypeStruct((t, LANE), F32), jax.ShapeDtypeStruct((8, LANE), F32)],
        scratch_shapes=[pltpu.VMEM((t, LANE), F32), pltpu.VMEM((t, LANE), F32), pltpu.VMEM((LANE, LANE), F32)],
        compiler_params=_params(),
    )(ps, gparams, dcol_g, dtot_g, dcol_f, drow_g, drow_f)


def _conv(xv, w):
    acc = w[3:4, :] * xv
    for s in range(1, 4):
        acc += w[3 - s:4 - s, :] * _shift_down(xv, s)
    return acc


PREP_ROWS = 256
HALO = 8


def _tile_rows(r):
    return pl.ds(pl.multiple_of(r * PREP_ROWS, PREP_ROWS), PREP_ROWS)


def _gdn_prep(pm, conv_w):
    t = pm.shape[0]
    nj = WIDTH // LANE
    win = PREP_ROWS + HALO

    def body(x_ref, w_ref, o_ref, xp_ref):
        kind = pl.program_id(0)
        xp_ref[0:HALO, :] = jnp.zeros((HALO, LANE), F32)
        xp_ref[HALO:, :] = x_ref[...]
        w = w_ref[...]
        hs = _head_sum_matrix()

        def tile(r, _):
            xw = xp_ref[pl.ds(pl.multiple_of(r * PREP_ROWS, PREP_ROWS), win), :]
            acc = _conv(xw, w)[HALO:]
            y = acc * _sigmoid(acc)
            out = jnp.where(kind < 2, y * lax.rsqrt(_group_sum(y * y, hs) + EPS), y)
            o_ref[0, 0, _tile_rows(r), :] = out[:, :DH]
            o_ref[0, 1, _tile_rows(r), :] = out[:, DH:]
            return 0

        lax.fori_loop(0, t // PREP_ROWS, tile, 0)

    return pl.pallas_call(
        body, name="gdn_prep", grid=(3, nj),
        in_specs=[pl.BlockSpec((t, LANE), lambda i, j: (0, i * nj + j)),
                  pl.BlockSpec((4, LANE), lambda i, j: (0, i * nj + j))],
        out_specs=pl.BlockSpec((1, 2, t, DH), lambda i, j: (i, j, 0, 0)),
        out_shape=jax.ShapeDtypeStruct((3, HEADS, t, DH), F32),
        scratch_shapes=[pltpu.VMEM((t + HALO, LANE), F32)],
        compiler_params=_params(("arbitrary", "arbitrary")),
    )(pm, conv_w)


def _gdn_prep_bwd(pm, conv_w, dqkv):
    t = pm.shape[0]
    nj = WIDTH // LANE
    win = PREP_ROWS + 2 * HALO

    def body(x_ref, w_ref, d_ref, dx_ref, dw_ref, xp_ref, dp_ref):
        kind = pl.program_id(0)
        zeros = jnp.zeros((HALO, LANE), F32)
        for ref in (xp_ref, dp_ref):
            ref[0:HALO, :] = zeros
            ref[HALO + t:, :] = zeros
        xp_ref[HALO:HALO + t, :] = x_ref[...]
        dp_ref[HALO:HALO + t, 0:DH] = d_ref[0, 0]
        dp_ref[HALO:HALO + t, DH:] = d_ref[0, 1]
        w = w_ref[...]
        hs = _head_sum_matrix()
        rows = lax.broadcasted_iota(jnp.int32, (win, LANE), 0)
        in_tile = (rows >= HALO) & (rows < HALO + PREP_ROWS)

        def tile(r, dw):
            start = pl.multiple_of(r * PREP_ROWS, PREP_ROWS)
            xw = xp_ref[pl.ds(start, win), :]
            d = dp_ref[pl.ds(start, win), :]
            acc = _conv(xw, w)
            sg = _sigmoid(acc)
            y = acc * sg
            rn = lax.rsqrt(_group_sum(y * y, hs) + EPS)
            yn = y * rn
            dy = jnp.where(kind < 2, rn * (d - yn * _group_sum(d * yn, hs)), d)
            dacc = dy * sg * (1.0 + acc * (1.0 - sg))
            dx = w[3:4, :] * dacc
            for s in range(1, 4):
                dx += w[3 - s:4 - s, :] * _shift_up(dacc, s)
            dx_ref[_tile_rows(r), :] = dx[HALO:HALO + PREP_ROWS]
            dm = jnp.where(in_tile, dacc, 0.0)
            return tuple(dw[i] + jnp.sum(dm * (xw if i == 3 else _shift_down(xw, 3 - i)), axis=0, keepdims=True)
                         for i in range(4))

        dw = lax.fori_loop(0, t // PREP_ROWS, tile, tuple(jnp.zeros((1, LANE), F32) for _ in range(4)))
        for i in range(4):
            dw_ref[i:i + 1, :] = dw[i]

    return pl.pallas_call(
        body, name="gdn_prep_bwd", grid=(3, nj),
        in_specs=[pl.BlockSpec((t, LANE), lambda i, j: (0, i * nj + j)),
                  pl.BlockSpec((4, LANE), lambda i, j: (0, i * nj + j)),
                  pl.BlockSpec((1, 2, t, DH), lambda i, j: (i, j, 0, 0))],
        out_specs=[pl.BlockSpec((t, LANE), lambda i, j: (0, i * nj + j)),
                   pl.BlockSpec((4, LANE), lambda i, j: (0, i * nj + j))],
        out_shape=[jax.ShapeDtypeStruct((t, 3 * WIDTH), F32), jax.ShapeDtypeStruct((4, 3 * WIDTH), F32)],
        scratch_shapes=[pltpu.VMEM((t + 2 * HALO, LANE), F32), pltpu.VMEM((t + 2 * HALO, LANE), F32)],
        compiler_params=_params(("arbitrary", "arbitrary")),
    )(pm, conv_w, dqkv)


FOX_COL0 = 4 * WIDTH // LANE


def _fox_prep(pm, nw):
    t = pm.shape[0]
    nj = WIDTH // LANE

    def body(x_ref, w_ref, o_ref):
        kind = pl.program_id(0)
        hs = _head_sum_matrix()
        wk = w_ref[pl.ds(kind, 1), :]

        def tile(r, _):
            xv = x_ref[_tile_rows(r), :]
            ms = _group_sum(xv * xv, hs) * (1.0 / DH)
            out = jnp.where(kind < 2, xv * lax.rsqrt(ms + EPS) * wk, xv)
            o_ref[0, 0, _tile_rows(r), :] = out[:, :DH]
            o_ref[0, 1, _tile_rows(r), :] = out[:, DH:]
            return 0

        lax.fori_loop(0, t // PREP_ROWS, tile, 0)

    return pl.pallas_call(
        body, name="fox_prep", grid=(3, nj),
        in_specs=[pl.BlockSpec((t, LANE), lambda i, j: (0, FOX_COL0 + i * nj + j)),
                  pl.BlockSpec((3, LANE), lambda i, j: (0, 0))],
        out_specs=pl.BlockSpec((1, 2, t, DH), lambda i, j: (i, j, 0, 0)),
        out_shape=jax.ShapeDtypeStruct((3, HEADS, t, DH), F32),
        compiler_params=_params(("arbitrary", "arbitrary")),
    )(pm, nw)


def _fox_prep_bwd(pm, nw, dqkv):
    t = pm.shape[0]
    nj = WIDTH // LANE

    def body(x_ref, w_ref, d_ref, dx_ref, dw_ref):
        kind = pl.program_id(0)
        hs = _head_sum_matrix()
        wk = w_ref[pl.ds(kind, 1), :]

        def tile(r, dw):
            xv = x_ref[_tile_rows(r), :]
            rn = lax.rsqrt(_group_sum(xv * xv, hs) * (1.0 / DH) + EPS)
            xn = xv * rn
            d = jnp.concatenate([d_ref[0, 0, _tile_rows(r), :], d_ref[0, 1, _tile_rows(r), :]], axis=1)
            g = d * wk
            dxn = rn * (g - xn * _group_sum(g * xn, hs) * (1.0 / DH))
            dx_ref[_tile_rows(r), :] = jnp.where(kind < 2, dxn, d)
            return dw + jnp.sum(d * xn, axis=0, keepdims=True)

        dw_ref[0, 0] = lax.fori_loop(0, t // PREP_ROWS, tile, jnp.zeros((1, LANE), F32))

    return pl.pallas_call(
        body, name="fox_prep_bwd", grid=(3, nj),
        in_specs=[pl.BlockSpec((t, LANE), lambda i, j: (0, FOX_COL0 + i * nj + j)),
                  pl.BlockSpec((3, LANE), lambda i, j: (0, 0)),
                  pl.BlockSpec((1, 2, t, DH), lambda i, j: (i, j, 0, 0))],
        out_specs=[pl.BlockSpec((t, LANE), lambda i, j: (0, i * nj + j)),
                   pl.BlockSpec((1, 1, 1, LANE), lambda i, j: (i, j, 0, 0))],
        out_shape=[jax.ShapeDtypeStruct((t, 3 * WIDTH), F32), jax.ShapeDtypeStruct((3, nj, 1, LANE), F32)],
        compiler_params=_params(("arbitrary", "arbitrary")),
    )(pm, nw, dqkv)


SC = 256
CPS = SC // CHUNK
GDN_HP = 2
Q_SCALE = DH ** -0.5


def _sc_masks():
    ri = lax.broadcasted_iota(jnp.int32, (SC, SC), 0)
    ci = lax.broadcasted_iota(jnp.int32, (SC, SC), 1)
    same = (ri // CHUNK) == (ci // CHUNK)
    return same & (ri >= ci), same & (ri > ci), ri == ci


def _unit_lower_inverses(ms, eye):
    invs = [jnp.where(eye, 1.0, 0.0) + m for m in ms]
    for _ in range(5):
        ms = [_dot(m, m) for m in ms]
        invs = [inv + _dot(inv, m) for inv, m in zip(invs, ms)]
    return invs


def _lane_col(blk, lane_idx):
    lane = lax.broadcasted_iota(jnp.int32, blk.shape, 1)
    return jnp.sum(jnp.where(lane == lane_idx, blk, 0.0), axis=1, keepdims=True)


def _to_lane(col, lane_idx):
    lane = lax.broadcasted_iota(jnp.int32, (col.shape[0], LANE), 1)
    return jnp.where(lane == lane_idx, col, 0.0)


def _gdn_columns(gates_ref, run_ref, tot_ref, runt_ref, rows, h):
    return (_lane_col(gates_ref[rows, :], h), _lane_col(run_ref[rows, :], HEADS + h),
            _lane_col(tot_ref[rows, :], HEADS + h), runt_ref[pl.ds(h, 1), rows])


def _gdn_local(q, k, beta, gc, gl, grow, causal):
    decay = jnp.exp(jnp.where(causal, gc - grow, -1e30))
    egc = jnp.exp(gc)
    ekd = jnp.exp(gl - gc)
    qs = q * Q_SCALE
    kb = k * beta
    kk = _dot_nt(kb, k)
    qk = _dot_nt(qs, k)
    return beta, gl, decay, egc, ekd, qs, kb, kk, qk, jnp.where(causal, qk * decay, 0.0)


def _chunk_rows(c):
    return pl.ds(pl.multiple_of(c * CHUNK, CHUNK), CHUNK)


def _sc_rows(b):
    return pl.ds(pl.multiple_of(b * SC, SC), SC)


def _gdn_fwd(qkv, gates, run, tot, run_t):
    t = qkv.shape[2]
    nc = t // CHUNK
    nsc = t // SC

    hp_n = GDN_HP
    heads = range(hp_n)

    def body(qkv_ref, gates_ref, run_ref, tot_ref, runt_ref, o_ref, st_ref, inv_ref, kc_s, qc_s, g_s, au_s):
        hp = pl.program_id(0)
        causal, strict, eye = _sc_masks()

        def local(b, _):
            rows = _sc_rows(b)
            loc = [_gdn_local(qkv_ref[0, hh, rows, :], qkv_ref[1, hh, rows, :],
                              *_gdn_columns(gates_ref, run_ref, tot_ref, runt_ref, rows, hp * hp_n + hh), causal)
                   for hh in heads]
            invs = _unit_lower_inverses([-jnp.where(strict, l[7] * l[2], 0.0) for l in loc], eye)
            us, ws = [], []
            for hh in heads:
                beta, _, _, egc, _, _, kb, _, _, _ = loc[hh]
                inv_ref[hh, rows, :] = invs[hh].astype(inv_ref.dtype)
                us.append(_dot(invs[hh], qkv_ref[2, hh, rows, :] * beta))
                ws.append(_dot(invs[hh], kb * egc))
            for hh in heads:
                _, _, _, egc, ekd, qs, _, _, _, attn = loc[hh]
                g_s[hh, rows, :] = qs * egc - _dot(attn, ws[hh])
                au_s[hh, rows, :] = _dot(attn, us[hh])
                kd = qkv_ref[1, hh, rows, :] * ekd
                for j in range(CPS):
                    sl = slice(j * CHUNK, (j + 1) * CHUNK)
                    kc_s[hh, b * CPS + j] = _dot_tn(kd[sl], ws[hh][sl])
                    qc_s[hh, b * CPS + j] = _dot_tn(kd[sl], us[hh][sl])
            return 0

        lax.fori_loop(0, nsc, local, 0)

        def step(c, states):
            rows = _chunk_rows(c)
            tot_row = tot_ref[pl.ds(c * CHUNK, 1), :]
            new = []
            for hh in heads:
                s = states[hh]
                st_ref[hh, c] = s
                o_ref[hh, rows, :] = _dot(g_s[hh, rows, :], s) + au_s[hh, rows, :]
                egl = jnp.exp(_lane_col(tot_row, HEADS + hp * hp_n + hh))
                new.append(egl * s - _dot(kc_s[hh, c], s) + qc_s[hh, c])
            return tuple(new)

        lax.fori_loop(0, nc, step, tuple(jnp.zeros((DH, DH), F32) for _ in heads))

    whole = pl.BlockSpec((t, LANE), lambda h: (0, 0))
    return pl.pallas_call(
        body, name="gdn_fwd", grid=(HEADS // hp_n,),
        in_specs=[pl.BlockSpec((3, hp_n, t, DH), lambda h: (0, h, 0, 0)), whole, whole, whole,
                  pl.BlockSpec((HEADS, t), lambda h: (1, 0))],
        out_specs=[pl.BlockSpec((hp_n, t, DH), lambda h: (h, 0, 0)), pl.BlockSpec((hp_n, nc, DH, DH), lambda h: (h, 0, 0, 0)),
                   pl.BlockSpec((hp_n, t, SC), lambda h: (h, 0, 0))],
        out_shape=[jax.ShapeDtypeStruct((HEADS, t, DH), F32), jax.ShapeDtypeStruct((HEADS, nc, DH, DH), F32),
                   jax.ShapeDtypeStruct((HEADS, t, SC), MXU)],
        scratch_shapes=[pltpu.VMEM((hp_n, nc, DH, DH), F32), pltpu.VMEM((hp_n, nc, DH, DH), F32),
                        pltpu.VMEM((hp_n, t, DH), F32), pltpu.VMEM((hp_n, t, DH), F32)],
        compiler_params=_params(("arbitrary",)),
    )(qkv, gates, run, tot, run_t)


def _gdn_bwd(qkv, gates, run, tot, run_t, inv, states, do):
    t = qkv.shape[2]
    nc = t // CHUNK
    nsc = t // SC

    hp_n = GDN_HP
    heads = range(hp_n)

    def body(qkv_ref, gates_ref, run_ref, tot_ref, runt_ref, inv_ref, st_ref, do_ref,
             dqkv_ref, dcol_ref, dtot_ref, drow_ref, u_s, w_s, kc_s, h_s, dsn_s):
        hp = pl.program_id(0)
        causal, strict, _ = _sc_masks()

        @pl.when(hp == 0)
        def _():
            dcol_ref[...] = jnp.zeros_like(dcol_ref)
            dtot_ref[...] = jnp.zeros_like(dtot_ref)

        def local_of(hh, rows):
            return _gdn_local(qkv_ref[0, hh, rows, :], qkv_ref[1, hh, rows, :],
                              *_gdn_columns(gates_ref, run_ref, tot_ref, runt_ref, rows, hp * hp_n + hh), causal)

        def local(b, _):
            rows = _sc_rows(b)
            loc = [local_of(hh, rows) for hh in heads]
            us, ws = [], []
            for hh in heads:
                beta, _, _, egc, _, _, kb, _, _, _ = loc[hh]
                inv_b = inv_ref[hh, rows, :]
                us.append(_dot(inv_b, qkv_ref[2, hh, rows, :] * beta))
                ws.append(_dot(inv_b, kb * egc))
            gs = [loc[hh][5] * loc[hh][3] - _dot(loc[hh][9], ws[hh]) for hh in heads]
            for hh in heads:
                u_s[hh, rows, :] = us[hh]
                w_s[hh, rows, :] = ws[hh]
                kd = qkv_ref[1, hh, rows, :] * loc[hh][4]
                dout = do_ref[hh, rows, :]
                for j in range(CPS):
                    sl = slice(j * CHUNK, (j + 1) * CHUNK)
                    kc_s[hh, b * CPS + j] = _dot_tn(kd[sl], ws[hh][sl])
                    h_s[hh, b * CPS + j] = _dot_tn(gs[hh][sl], dout[sl])
            return 0

        lax.fori_loop(0, nsc, local, 0)

        def step(i, dss):
            c = nc - 1 - i
            tot_row = tot_ref[pl.ds(c * CHUNK, 1), :]
            new = []
            for hh in heads:
                ds = dss[hh]
                dsn_s[hh, c] = ds
                egl = jnp.exp(_lane_col(tot_row, HEADS + hp * hp_n + hh))
                new.append(egl * ds - _dot_tn(kc_s[hh, c], ds) + h_s[hh, c])
            return tuple(new)

        lax.fori_loop(0, nc, step, tuple(jnp.zeros((DH, DH), F32) for _ in heads))

        def back(b, _):
            rows = _sc_rows(b)
            first = lax.broadcasted_iota(jnp.int32, (CHUNK, 1), 0) == 0
            loc = [local_of(hh, rows) for hh in heads]
            mid = []
            for hh in heads:
                beta, gl, decay, egc, ekd, qs, kb, kk, qk, attn = loc[hh]
                u, w = u_s[hh, rows, :], w_s[hh, rows, :]
                kd = qkv_ref[1, hh, rows, :] * ekd
                dout = do_ref[hh, rows, :]
                dg_p, dkd_p, dw_p, du_p, dgl_p = [], [], [], [], []
                for j in range(CPS):
                    sl = slice(j * CHUNK, (j + 1) * CHUNK)
                    s = st_ref[hh, b * CPS + j]
                    dsn = dsn_s[hh, b * CPS + j]
                    dkc = -_dot_nt(dsn, s)
                    dg_p.append(_dot_nt(dout[sl], s))
                    dkd_p.append(_dot_nt(w[sl], dkc) + _dot_nt(u[sl], dsn))
                    dw_p.append(_dot(kd[sl], dkc))
                    du_p.append(_dot(kd[sl], dsn))
                    degl = jnp.sum(jnp.sum(s * dsn, axis=1, keepdims=True), axis=0, keepdims=True)
                    dgl_p.append(jnp.where(first, degl * jnp.exp(gl[j * CHUNK:j * CHUNK + 1, :]), 0.0))
                dg, dkd = jnp.concatenate(dg_p, axis=0), jnp.concatenate(dkd_p, axis=0)
                da = jnp.where(causal, _dot_nt(dout, u) - _dot_nt(dg, w), 0.0)
                at = _dot_tn(attn, jnp.concatenate([dout, dg], axis=1))
                du = at[:, :DH] + jnp.concatenate(du_p, axis=0)
                dw = jnp.concatenate(dw_p, axis=0) - at[:, DH:]
                mid.append((kd, dg, dkd, da, du, dw, jnp.concatenate(dgl_p, axis=0)))
            inv_ts = [inv_ref[hh, rows, :].astype(F32).T for hh in heads]
            its = [_dot(inv_ts[hh], jnp.concatenate([mid[hh][4], mid[hh][5]], axis=1)) for hh in heads]
            dinvs = []
            for hh in heads:
                beta, _, _, egc, _, _, kb, _, _, _ = loc[hh]
                dinvs.append(_dot_nt(mid[hh][4], qkv_ref[2, hh, rows, :] * beta) + _dot_nt(mid[hh][5], kb * egc))
            half = [_dot(inv_ts[hh], dinvs[hh]) for hh in heads]
            dls = [jnp.where(strict, -_dot(half[hh], inv_ts[hh]), 0.0) for hh in heads]
            for hh in heads:
                head = hp * hp_n + hh
                beta, gl, decay, egc, ekd, qs, kb, kk, qk, attn = loc[hh]
                kd, dg, dkd, da, du, dw, dgl_first = mid[hh]
                k, v = qkv_ref[1, hh, rows, :], qkv_ref[2, hh, rows, :]
                dvb, dkbe = its[hh][:, :DH], its[hh][:, DH:]
                dl = dls[hh]
                dlogd = (dl * kk + da * qk) * decay
                dkk = dl * decay
                dqk = da * decay
                dkb = _dot(dkk, k) + dkbe * egc
                dqs = _dot(dqk, k) + dg * egc
                dk = (_dot_tn(jnp.concatenate([dkk, dqk], axis=0), jnp.concatenate([kb, qs], axis=0))
                      + dkd * ekd + dkb * beta)
                dkd_kd = jnp.sum(dkd * kd, axis=1, keepdims=True)
                dgc = (jnp.sum(dlogd, axis=1, keepdims=True) + jnp.sum(dg * qs, axis=1, keepdims=True) * egc
                       + jnp.sum(dkbe * (kb * egc), axis=1, keepdims=True) - dkd_kd)
                dbeta = jnp.sum(dkb * k, axis=1, keepdims=True) + jnp.sum(dvb * v, axis=1, keepdims=True)
                dqkv_ref[0, hh, rows, :] = dqs * Q_SCALE
                dqkv_ref[1, hh, rows, :] = dk
                dqkv_ref[2, hh, rows, :] = dvb * beta
                dcol_ref[rows, :] += _to_lane(dbeta, head) + _to_lane(dgc, HEADS + head)
                dtot_ref[rows, :] += _to_lane(dkd_kd + dgl_first, HEADS + head)
                drow_ref[pl.ds(head, 1), rows] = -jnp.sum(dlogd, axis=0, keepdims=True)
            return 0

        lax.fori_loop(0, nsc, back, 0)

    whole = pl.BlockSpec((t, LANE), lambda h: (0, 0))
    rowspec = pl.BlockSpec((HEADS, t), lambda h: (0, 0))
    sq = pltpu.VMEM((hp_n, nc, DH, DH), F32)
    per_head = pltpu.VMEM((hp_n, t, DH), F32)
    return pl.pallas_call(
        body, name="gdn_bwd", grid=(HEADS // hp_n,),
        in_specs=[pl.BlockSpec((3, hp_n, t, DH), lambda h: (0, h, 0, 0)), whole, whole, whole,
                  pl.BlockSpec((HEADS, t), lambda h: (1, 0)),
                  pl.BlockSpec((hp_n, t, SC), lambda h: (h, 0, 0)), pl.BlockSpec((hp_n, nc, DH, DH), lambda h: (h, 0, 0, 0)),
                  pl.BlockSpec((hp_n, t, DH), lambda h: (h, 0, 0))],
        out_specs=[pl.BlockSpec((3, hp_n, t, DH), lambda h: (0, h, 0, 0)), whole, whole, rowspec],
        out_shape=[jax.ShapeDtypeStruct((3, HEADS, t, DH), F32), jax.ShapeDtypeStruct((t, LANE), F32),
                   jax.ShapeDtypeStruct((t, LANE), F32), jax.ShapeDtypeStruct((HEADS, t), F32)],
        scratch_shapes=[per_head, per_head, sq, sq, sq],
        compiler_params=_params(("arbitrary",)),
    )(qkv, gates, run, tot, run_t, inv, states, do)


FOX_HP = 2


def _wide_t(a):
    r = a.shape[0]
    return jnp.concatenate([a, jnp.zeros((r, LANE - DH), F32)], axis=1).T[:DH]


def _tall_t(a):
    r = a.shape[1]
    return jnp.concatenate([a, jnp.zeros((LANE - DH, r), F32)], axis=0).T[:, :DH]


def _key_side_f(run_blk, head, qb):
    col = jnp.broadcast_to(_lane_col(run_blk, 2 * HEADS + head), (run_blk.shape[0], LANE))
    return jnp.concatenate([col] * (qb // LANE), axis=1)


def _diag_mask(qb):
    return lax.broadcasted_iota(jnp.int32, (qb, qb), 0) <= lax.broadcasted_iota(jnp.int32, (qb, qb), 1)


def _fox_fwd(qkv, run, run_t):
    t = qkv.shape[2]
    qb = min(QB, t)
    nq = t // qb
    hp_n = FOX_HP

    def body(q_ref, k_ref, v_ref, run_ref, runt_ref, o_ref, lse_ref, kb_s, vt_s):
        hp = pl.program_id(0)
        i = pl.program_id(1)

        @pl.when(i == 0)
        def _():
            for hh in range(hp_n):
                kb_s[hh] = k_ref[0, hh].astype(MXU)
                for b in range(nq):
                    rows = slice(b * qb, (b + 1) * qb)
                    vt_s[hh, :, rows] = _wide_t(v_ref[0, hh, rows, :]).astype(MXU)

        qrows = pl.ds(pl.multiple_of(i * qb, qb), qb)
        qs = [(q_ref[0, hh] * Q_SCALE).astype(MXU) for hh in range(hp_n)]
        fq = [runt_ref[pl.ds(hp * hp_n + hh, 1), qrows] for hh in range(hp_n)]

        def block_rows(j):
            return pl.ds(pl.multiple_of(j * qb, qb), qb)

        def scores(j):
            return tuple(_dot_nt(kb_s[hh, block_rows(j), :], qs[hh]) for hh in range(hp_n))

        def absorb(j, raw, state, diagonal):
            rows = block_rows(j)
            run_blk = run_ref[rows, :]
            stats, pv = [], []
            for hh in range(hp_n):
                m, l, _ = state[hh]
                st = raw[hh] + (fq[hh] - _key_side_f(run_blk, hp * hp_n + hh, qb))
                if diagonal:
                    st = jnp.where(_diag_mask(qb), st, -1e30)
                m_new = jnp.maximum(m, jnp.max(st, axis=0, keepdims=True))
                p = jnp.exp(st - m_new)
                alpha = jnp.exp(m - m_new)
                stats.append((m_new, alpha * l + jnp.sum(p, axis=0, keepdims=True), alpha))
                pv.append(_dot(vt_s[hh, :, rows], p))
            return tuple((stats[hh][0], stats[hh][1], stats[hh][2] * state[hh][2] + pv[hh]) for hh in range(hp_n))

        def kstep(j, carry):
            state, raw = carry
            ahead = scores(j + 1)
            return absorb(j, raw, state, False), ahead

        init = tuple((jnp.full((1, qb), -1e30, F32), jnp.zeros((1, qb), F32), jnp.zeros((DH, qb), F32))
                     for _ in range(hp_n))
        state, raw = lax.fori_loop(0, i, kstep, (init, scores(0)))
        state = absorb(i, raw, state, True)
        for hh in range(hp_n):
            m, l, acc = state[hh]
            o_ref[hh] = _tall_t(acc / l)
            lse_ref[pl.ds(hp * hp_n + hh, 1), qrows] = m + jnp.log(l)

    return pl.pallas_call(
        body, name="fox_fwd", grid=(HEADS // hp_n, nq),
        in_specs=[pl.BlockSpec((1, hp_n, qb, DH), lambda h, i: (0, h, i, 0)),
                  pl.BlockSpec((1, hp_n, t, DH), lambda h, i: (1, h, 0, 0)),
                  pl.BlockSpec((1, hp_n, t, DH), lambda h, i: (2, h, 0, 0)),
                  pl.BlockSpec((t, LANE), lambda h, i: (0, 0)),
                  pl.BlockSpec((HEADS, t), lambda h, i: (2, 0))],
        out_specs=[pl.BlockSpec((hp_n, qb, DH), lambda h, i: (h, i, 0)), pl.BlockSpec((HEADS, t), lambda h, i: (0, 0))],
        out_shape=[jax.ShapeDtypeStruct((HEADS, t, DH), F32), jax.ShapeDtypeStruct((HEADS, t), F32)],
        scratch_shapes=[pltpu.VMEM((hp_n, t, DH), MXU), pltpu.VMEM((hp_n, DH, t), MXU)],
        compiler_params=_params(("arbitrary", "arbitrary")),
    )(qkv, qkv, qkv, run, run_t)


def _fox_bwd(qkv, run, run_t, o, lse, do):
    t = qkv.shape[2]
    qb = min(QB, t)
    nq = t // qb
    hp_n = FOX_HP

    def body(q_ref, k_ref, v_ref, run_ref, runt_ref, o_ref, lse_ref, do_ref, dqkv_ref, dcol_ref, drow_ref, dqt_s):
        hp = pl.program_id(0)
        j = pl.program_id(1)

        @pl.when(j == 0)
        def _():
            dqt_s[...] = jnp.zeros_like(dqt_s)

        @pl.when((j == 0) & (hp == 0))
        def _():
            dcol_ref[...] = jnp.zeros_like(dcol_ref)
            drow_ref[...] = jnp.zeros_like(drow_ref)

        krows = pl.ds(pl.multiple_of(j * qb, qb), qb)
        run_blk = run_ref[krows, :]
        ones8 = jnp.ones((8, DH), MXU)
        kb, kt, vb, fk = [], [], [], []
        for hh in range(hp_n):
            kf = k_ref[0, hh]
            kb.append(kf.astype(MXU))
            kt.append(_wide_t(kf).astype(MXU))
            vb.append(v_ref[0, hh].astype(MXU))
            fk.append(_key_side_f(run_blk, hp * hp_n + hh, qb))

        def block_rows(i):
            return pl.ds(pl.multiple_of(i * qb, qb), qb)

        def products(i):
            rows = block_rows(i)
            out = []
            for hh in range(hp_n):
                dout = do_ref[hh, rows, :]
                x = dout * o_ref[hh, rows, :]
                x_hi = x.astype(MXU)
                out.append((_dot_nt(kb[hh], q_ref[0, hh, rows, :] * Q_SCALE), _dot_nt(vb[hh], dout),
                            (_dot_nt(ones8, x_hi) + _dot_nt(ones8, x - x_hi.astype(F32)))[0:1, :]))
            return tuple(out)

        def absorb(i, prods, acc, diagonal):
            rows = block_rows(i)
            pieces = []
            for hh in range(hp_n):
                head = hp * hp_n + hh
                raw, dpt, drow = prods[hh]
                off = runt_ref[pl.ds(head, 1), rows] - lse_ref[pl.ds(head, 1), rows]
                st = raw + (off - fk[hh])
                if diagonal:
                    st = jnp.where(_diag_mask(qb), st, -1e30)
                pt = jnp.exp(st)
                dst = pt * (dpt - drow)
                drow_ref[pl.ds(head, 1), rows] += jnp.sum(dst, axis=0, keepdims=True)
                folded = dst[:, 0:LANE]
                for c in range(1, qb // LANE):
                    folded = folded + dst[:, c * LANE:(c + 1) * LANE]
                pieces.append((_dot(dst, q_ref[0, hh, rows, :] * Q_SCALE), _dot(pt, do_ref[hh, rows, :]),
                               _dot(kt[hh], dst), folded))
            out = []
            for hh in range(hp_n):
                dqt_s[hh, :, rows] += pieces[hh][2]
                out.append((acc[hh][0] + pieces[hh][0], acc[hh][1] + pieces[hh][1], acc[hh][2] + pieces[hh][3]))
            return tuple(out)

        def qstep(i, carry):
            acc, prods = carry
            ahead = products(jnp.minimum(i + 1, nq - 1))
            return absorb(i, prods, acc, False), ahead

        init = tuple((jnp.zeros((qb, DH), F32), jnp.zeros((qb, DH), F32), jnp.zeros((qb, LANE), F32))
                     for _ in range(hp_n))
        first = products(j)
        ahead = products(jnp.minimum(j + 1, nq - 1))
        acc = absorb(j, first, init, True)
        acc, _ = lax.fori_loop(j + 1, nq, qstep, (acc, ahead))
        for hh in range(hp_n):
            dk, dv, ds_sum = acc[hh]
            dqkv_ref[1, hh, krows, :] = dk
            dqkv_ref[2, hh, krows, :] = dv
            dcol_ref[krows, :] += _to_lane(-jnp.sum(ds_sum, axis=1, keepdims=True), 2 * HEADS + hp * hp_n + hh)

        @pl.when(j == nq - 1)
        def _():
            for hh in range(hp_n):
                for b in range(nq):
                    rows = slice(b * qb, (b + 1) * qb)
                    dqkv_ref[0, hh, rows, :] = _tall_t(dqt_s[hh, :, rows]) * Q_SCALE

    full = pl.BlockSpec((hp_n, t, DH), lambda h, j: (h, 0, 0))
    rows8 = pl.BlockSpec((HEADS, t), lambda h, j: (0, 0))
    return pl.pallas_call(
        body, name="fox_bwd", grid=(HEADS // hp_n, nq),
        in_specs=[pl.BlockSpec((1, hp_n, t, DH), lambda h, j: (0, h, 0, 0)),
                  pl.BlockSpec((1, hp_n, qb, DH), lambda h, j: (1, h, j, 0)),
                  pl.BlockSpec((1, hp_n, qb, DH), lambda h, j: (2, h, j, 0)),
                  pl.BlockSpec((t, LANE), lambda h, j: (0, 0)), pl.BlockSpec((HEADS, t), lambda h, j: (2, 0)),
                  full, rows8, full],
        out_specs=[pl.BlockSpec((3, hp_n, t, DH), lambda h, j: (0, h, 0, 0)),
                   pl.BlockSpec((t, LANE), lambda h, j: (0, 0)), rows8],
        out_shape=[jax.ShapeDtypeStruct((3, HEADS, t, DH), F32), jax.ShapeDtypeStruct((t, LANE), F32),
                   jax.ShapeDtypeStruct((HEADS, t), F32)],
        scratch_shapes=[pltpu.VMEM((hp_n, DH, t), F32)],
        compiler_params=_params(("arbitrary", "arbitrary")),
    )(qkv, qkv, qkv, run, run_t, o, lse, do)


Z_COL0 = 3 * WIDTH // LANE
FGATE_COL0 = 7 * WIDTH // LANE


def _gdn_post(o, pm, onw):
    t = pm.shape[0]

    def body(o_ref, z_ref, w_ref, m_ref):
        z = z_ref[...]
        sz = z * _sigmoid(z)
        for hh in range(2):
            ov = o_ref[hh]
            n = ov * lax.rsqrt(jnp.mean(ov * ov, axis=-1, keepdims=True) + EPS) * w_ref[...]
            m_ref[:, hh * DH:(hh + 1) * DH] = (n * sz[:, hh * DH:(hh + 1) * DH]).astype(m_ref.dtype)

    return pl.pallas_call(
        body, name="gdn_post", grid=(WIDTH // LANE,),
        in_specs=[pl.BlockSpec((2, t, DH), lambda j: (j, 0, 0)), pl.BlockSpec((t, LANE), lambda j: (0, Z_COL0 + j)),
                  pl.BlockSpec((1, DH), lambda j: (0, 0))],
        out_specs=pl.BlockSpec((t, LANE), lambda j: (0, j)),
        out_shape=jax.ShapeDtypeStruct((t, WIDTH), MXU),
        compiler_params=_params(("arbitrary",)),
    )(o, pm, onw)


def _gdn_post_bwd(o, pm, onw, dmix):
    t = pm.shape[0]

    def body(o_ref, z_ref, w_ref, dm_ref, do_ref, dz_ref, dw_ref):
        @pl.when(pl.program_id(0) == 0)
        def _():
            dw_ref[...] = jnp.zeros_like(dw_ref)

        z = z_ref[...]
        sg = _sigmoid(z)
        sz = z * sg
        dsz = sg * (1.0 + z * (1.0 - sg))
        dm = dm_ref[...]
        for hh in range(2):
            cols = slice(hh * DH, (hh + 1) * DH)
            ov = o_ref[hh]
            r = lax.rsqrt(jnp.mean(ov * ov, axis=-1, keepdims=True) + EPS)
            xn = ov * r
            dmh = dm[:, cols]
            dn = dmh * sz[:, cols]
            dz_ref[:, cols] = dmh * (xn * w_ref[...]) * dsz[:, cols]
            dw_ref[...] += jnp.sum(dn * xn, axis=0, keepdims=True)
            g = dn * w_ref[...]
            do_ref[hh] = r * (g - xn * jnp.mean(g * xn, axis=-1, keepdims=True))

    return pl.pallas_call(
        body, name="gdn_post_bwd", grid=(WIDTH // LANE,),
        in_specs=[pl.BlockSpec((2, t, DH), lambda j: (j, 0, 0)), pl.BlockSpec((t, LANE), lambda j: (0, Z_COL0 + j)),
                  pl.BlockSpec((1, DH), lambda j: (0, 0)), pl.BlockSpec((t, LANE), lambda j: (0, j))],
        out_specs=[pl.BlockSpec((2, t, DH), lambda j: (j, 0, 0)), pl.BlockSpec((t, LANE), lambda j: (0, j)),
                   pl.BlockSpec((1, DH), lambda j: (0, 0))],
        out_shape=[jax.ShapeDtypeStruct((HEADS, t, DH), F32), jax.ShapeDtypeStruct((t, WIDTH), F32),
                   jax.ShapeDtypeStruct((1, DH), F32)],
        compiler_params=_params(("arbitrary",)),
    )(o, pm, onw, dmix)


def _fox_post(o, pm):
    t = pm.shape[0]

    def body(o_ref, g_ref, m_ref):
        sg = _sigmoid(g_ref[...])
        for hh in range(2):
            cols = slice(hh * DH, (hh + 1) * DH)
            m_ref[:, cols] = (o_ref[hh] * sg[:, cols]).astype(m_ref.dtype)

    return pl.pallas_call(
        body, name="fox_post", grid=(WIDTH // LANE,),
        in_specs=[pl.BlockSpec((2, t, DH), lambda j: (j, 0, 0)), pl.BlockSpec((t, LANE), lambda j: (0, FGATE_COL0 + j))],
        out_specs=pl.BlockSpec((t, LANE), lambda j: (0, j)),
        out_shape=jax.ShapeDtypeStruct((t, WIDTH), MXU),
        compiler_params=_params(("arbitrary",)),
    )(o, pm)


def _fox_post_bwd(o, pm, dmix):
    t = pm.shape[0]

    def body(o_ref, g_ref, dm_ref, do_ref, dg_ref):
        sg = _sigmoid(g_ref[...])
        dm = dm_ref[...]
        for hh in range(2):
            cols = slice(hh * DH, (hh + 1) * DH)
            do_ref[hh] = dm[:, cols] * sg[:, cols]
            dg_ref[:, cols] = dm[:, cols] * o_ref[hh] * (sg * (1.0 - sg))[:, cols]

    return pl.pallas_call(
        body, name="fox_post_bwd", grid=(WIDTH // LANE,),
        in_specs=[pl.BlockSpec((2, t, DH), lambda j: (j, 0, 0)), pl.BlockSpec((t, LANE), lambda j: (0, FGATE_COL0 + j)),
                  pl.BlockSpec((t, LANE), lambda j: (0, j))],
        out_specs=[pl.BlockSpec((2, t, DH), lambda j: (j, 0, 0)), pl.BlockSpec((t, LANE), lambda j: (0, j))],
        out_shape=[jax.ShapeDtypeStruct((HEADS, t, DH), F32), jax.ShapeDtypeStruct((t, WIDTH), F32)],
        compiler_params=_params(("arbitrary",)),
    )(o, pm, dmix)


def _tail(x, mixg, mixf, tgt, wo, n2w, wg, wu, wd, fw):
    t, d = x.shape
    dff = wg.shape[1]
    tb = min(TB, t)

    def body(x_ref, mg_ref, mf_ref, t_ref, wo_ref, n2_ref, wg_ref, wu_ref, wd_ref, fw_ref,
             h2_ref, act_ref, dgate_ref, dup_ref, dx3_ref, dx2_ref, dmg_ref, dmf_ref, dn2_ref, dfw_ref, loss_ref):
        @pl.when(pl.program_id(0) == 0)
        def _():
            dn2_ref[...] = jnp.zeros_like(dn2_ref)
            dfw_ref[...] = jnp.zeros_like(dfw_ref)
            loss_ref[...] = jnp.zeros_like(loss_ref)

        x2 = x_ref[...] + _dot(mg_ref[...], wo_ref[0:WIDTH, :]) + _dot(mf_ref[...], wo_ref[WIDTH:2 * WIDTH, :])
        r2 = lax.rsqrt(jnp.mean(x2 * x2, axis=-1, keepdims=True) + EPS)
        xn2 = x2 * r2
        h2 = (xn2 * n2_ref[...]).astype(MXU)
        h2_ref[...] = h2
        gate = _dot(h2, wg_ref[...])
        up = _dot(h2, wu_ref[...])
        sg = _sigmoid(gate)
        sl = gate * sg
        act = (sl * up).astype(MXU)
        act_ref[...] = act
        x3 = x2 + _dot(act, wd_ref[...])
        r3 = lax.rsqrt(jnp.mean(x3 * x3, axis=-1, keepdims=True) + EPS)
        xn3 = x3 * r3
        err = xn3 * fw_ref[...] - t_ref[...]
        loss_ref[...] += 0.5 * jnp.sum(jnp.mean(err * err, axis=-1, keepdims=True), axis=0, keepdims=True)
        dy = err * (1.0 / d)
        dfw_ref[...] += jnp.sum(dy * xn3, axis=0, keepdims=True)
        g3 = dy * fw_ref[...]
        dx3 = r3 * (g3 - xn3 * jnp.mean(g3 * xn3, axis=-1, keepdims=True))
        dx3_ref[...] = dx3.astype(MXU)
        dact = _dot_nt(dx3, wd_ref[...])
        dgate = (dact * up * (sg * (1.0 + gate * (1.0 - sg)))).astype(MXU)
        dup = (dact * sl).astype(MXU)
        dgate_ref[...] = dgate
        dup_ref[...] = dup
        dh2 = _dot_nt(dgate, wg_ref[...]) + _dot_nt(dup, wu_ref[...])
        dn2_ref[...] += jnp.sum(dh2 * xn2, axis=0, keepdims=True)
        g2 = dh2 * n2_ref[...]
        dx2 = dx3 + r2 * (g2 - xn2 * jnp.mean(g2 * xn2, axis=-1, keepdims=True))
        dx2_ref[...] = dx2
        dmg_ref[...] = _dot_nt(dx2, wo_ref[0:WIDTH, :])
        dmf_ref[...] = _dot_nt(dx2, wo_ref[WIDTH:2 * WIDTH, :])

    def tok(n):
        return pl.BlockSpec((tb, n), lambda i: (i, 0))

    acc = pl.BlockSpec((1, d), lambda i: (0, 0))
    sds = jax.ShapeDtypeStruct
    return pl.pallas_call(
        body, name="tail", grid=(t // tb,),
        in_specs=[tok(d), tok(WIDTH), tok(WIDTH), tok(d), _resident(wo.shape), _resident((1, d)),
                  _resident(wg.shape), _resident(wu.shape), _resident(wd.shape), _resident((1, d))],
        out_specs=[tok(d), tok(dff), tok(dff), tok(dff), tok(d), tok(d), tok(WIDTH), tok(WIDTH), acc, acc,
                   pl.BlockSpec((1, 1), lambda i: (0, 0))],
        out_shape=[sds((t, d), MXU), sds((t, dff), MXU), sds((t, dff), MXU), sds((t, dff), MXU), sds((t, d), MXU),
                   sds((t, d), F32), sds((t, WIDTH), F32), sds((t, WIDTH), F32), sds((1, d), F32), sds((1, d), F32),
                   sds((1, 1), F32)],
        compiler_params=_params(("arbitrary",)),
    )(x, mixg, mixf, tgt, wo, n2w, wg, wu, wd, fw)


def _wgrad(a, b, name):
    t, m = a.shape
    n = b.shape[1]
    bn = 256 if n % 256 == 0 else LANE

    def body(a_ref, b_ref, o_ref):
        o_ref[...] = _dot_tn(a_ref[...], b_ref[...]).astype(o_ref.dtype)

    return pl.pallas_call(
        body, name=name, grid=(n // bn,),
        in_specs=[_resident((t, m)), pl.BlockSpec((t, bn), lambda j: (0, j))],
        out_specs=pl.BlockSpec((m, bn), lambda j: (0, j)),
        out_shape=jax.ShapeDtypeStruct((m, n), WIRE),
        compiler_params=_params(("arbitrary",)),
    )(a, b)


def _split_w_in(w_in):
    a = 4 * WIDTH
    b = a + 2 * HEADS
    c = b + 4 * WIDTH
    main = jnp.concatenate([w_in[:, :a], w_in[:, b:c]], axis=1)
    small = jnp.concatenate([w_in[:, a:b], w_in[:, c:], jnp.zeros((w_in.shape[0], LANE - 3 * HEADS), w_in.dtype)], axis=1)
    return main, small


def _merge_dw_in(d_gdn, d_z, d_fox, d_fg, d_small):
    return jnp.concatenate([d_gdn, d_z, d_small[:, :2 * HEADS], d_fox, d_fg, d_small[:, 2 * HEADS:3 * HEADS]], axis=1)


def _lanes(*pieces):
    v = jnp.concatenate([p.reshape(-1).astype(F32) for p in pieces])
    return jnp.pad(v, (0, LANE - v.shape[0])).reshape(1, LANE)


def _vector_params(p):
    d = p["norm1_w"].size
    gparams = jnp.concatenate([_lanes(jnp.zeros(HEADS), p["gdn_dt_bias"], p["fox_f_bias"]),
                               _lanes(jnp.zeros(HEADS), p["gdn_A_log"]), jnp.zeros((6, LANE), F32)])
    fox_nw = jnp.stack([jnp.tile(p["fox_q_norm_w"].reshape(-1), 2), jnp.tile(p["fox_k_norm_w"].reshape(-1), 2),
                        jnp.ones((LANE,), F32)])
    return dict(n1w=p["norm1_w"].reshape(1, d), n2w=p["norm2_w"].reshape(1, d), fw=p["final_norm_w"].reshape(1, d),
                onw=p["gdn_out_norm_w"].reshape(1, DH), gparams=gparams, fox_nw=fox_nw)


def _mixer_forward(x, vp, w_main, w_small, conv_w):
    h1, pm, ps = _inproj(x, vp["n1w"], w_main, w_small)
    gates, run, tot, run_t = _gates(ps, vp["gparams"])
    gqkv = _gdn_prep(pm, conv_w)
    o_gdn, states, inv = _gdn_fwd(gqkv, gates, run, tot, run_t)
    mixg = _gdn_post(o_gdn, pm, vp["onw"])
    fqkv = _fox_prep(pm, vp["fox_nw"])
    o_fox, lse = _fox_fwd(fqkv, run, run_t)
    mixf = _fox_post(o_fox, pm)
    return dict(h1=h1, pm=pm, ps=ps, gates=gates, run=run, tot=tot, run_t=run_t, gqkv=gqkv, o_gdn=o_gdn, states=states,
                inv=inv, mixg=mixg, fqkv=fqkv, o_fox=o_fox, lse=lse, mixf=mixf)


def _mixer_backward(x, vp, w_main, w_small, conv_w, f, dx2, dmixg, dmixf, onw):
    pm = f["pm"]
    do_gdn, dz, donw = _gdn_post_bwd(f["o_gdn"], pm, onw, dmixg)
    dgqkv, dcol_g, dtot_g, drow_g = _gdn_bwd(f["gqkv"], f["gates"], f["run"], f["tot"], f["run_t"], f["inv"], f["states"], do_gdn)
    dgdn, dconv = _gdn_prep_bwd(pm, conv_w, dgqkv)
    do_fox, dfg = _fox_post_bwd(f["o_fox"], pm, dmixf)
    dfqkv, dcol_f, drow_f = _fox_bwd(f["fqkv"], f["run"], f["run_t"], f["o_fox"], f["lse"], do_fox)
    dfox, dfnw = _fox_prep_bwd(pm, vp["fox_nw"], dfqkv)
    dps, gsum = _gates_bwd(f["ps"], vp["gparams"], dcol_g, dtot_g, dcol_f, drow_g, drow_f)
    grad_x, dn1w = _inproj_bwd(x, vp["n1w"], dx2, dgdn, dz, dfox, dfg, dps, w_main, w_small)
    h1 = f["h1"]
    dw_in = _merge_dw_in(_wgrad(h1, dgdn, "dw_in_gdn"), _wgrad(h1, dz, "dw_in_z"), _wgrad(h1, dfox, "dw_in_fox"),
                         _wgrad(h1, dfg, "dw_in_fgate"), _wgrad(h1, dps, "dw_in_small"))
    return grad_x, dw_in, dconv, dict(dn1w=dn1w, gsum=gsum, donw=donw, dfnw=dfnw)


VECTORS = ("norm1_w", "norm2_w", "final_norm_w", "gdn_A_log", "gdn_dt_bias", "gdn_out_norm_w", "fox_f_bias",
           "fox_q_norm_w", "fox_k_norm_w")
VEC_ROWS = 16
LOSS_ROW = len(VECTORS)


def _pack_vectors(dn1w, dn2w, dfw, gsum, donw, dfnw, loss):
    d = dn1w.shape[1]

    def body(n1_ref, n2_ref, fw_ref, gs_ref, on_ref, fn_ref, loss_ref, o_ref):
        o_ref[...] = jnp.zeros_like(o_ref)
        o_ref[0:1, :] = n1_ref[...]
        o_ref[1:2, :] = n2_ref[...]
        o_ref[2:3, :] = fw_ref[...]
        o_ref[3:4, 0:HEADS] = gs_ref[0:1, 0:HEADS]
        o_ref[4:5, 0:HEADS] = gs_ref[1:2, 0:HEADS]
        o_ref[5:6, 0:DH] = on_ref[...]
        o_ref[6:7, 0:HEADS] = gs_ref[2:3, 0:HEADS]
        for kind in range(2):
            v = fn_ref[kind, 0]
            for j in range(1, fn_ref.shape[1]):
                v = v + fn_ref[kind, j]
            o_ref[7 + kind:8 + kind, 0:DH] = v[:, :DH] + v[:, DH:]
        o_ref[LOSS_ROW:LOSS_ROW + 1, 0:1] = loss_ref[...]

    return pl.pallas_call(body, name="pack_vectors", out_shape=jax.ShapeDtypeStruct((VEC_ROWS, d), F32),
                          compiler_params=_params())(dn1w, dn2w, dfw, gsum, donw, dfnw, loss)


def _local_step(x, tgt, p, w_in, conv_w, wo, wg, wu, wd):
    vp = _vector_params(p)
    w_main, w_small = _split_w_in(w_in)
    f = _mixer_forward(x, vp, w_main, w_small, conv_w)
    (h2, act, dgate, dup, dx3, dx2, dmixg, dmixf, dn2w, dfw, loss) = _tail(
        x, f["mixg"], f["mixf"], tgt, wo, vp["n2w"], wg, wu, wd, vp["fw"])
    grad_x, dw_in, dconv, small = _mixer_backward(x, vp, w_main, w_small, conv_w, f, dx2, dmixg, dmixf, vp["onw"])
    grads = {"w_in": dw_in, "gdn_conv_w": dconv,
             "w_out": jnp.concatenate([_wgrad(f["mixg"], dx2, "dw_out_gdn"), _wgrad(f["mixf"], dx2, "dw_out_fox")], axis=0),
             "w_ffn_gate": _wgrad(h2, dgate, "dw_gate"), "w_ffn_up": _wgrad(h2, dup, "dw_up"),
             "w_ffn_down": _wgrad(act, dx3, "dw_down")}
    vec = _pack_vectors(small["dn1w"], dn2w, dfw, small["gsum"], small["donw"], small["dfnw"], loss)
    return grad_x, grads, vec


def _my_place():
    return lax.axis_index("x"), lax.axis_index("y"), lax.axis_index("c")


def _peers():
    x, y, c = _my_place()
    peers = []
    for k in range(1, N_DEV):
        px = 1 - x if k & 4 else x
        py = 1 - y if k & 2 else y
        pc = 1 - c if k & 1 else c
        peers.append(((px, py, pc), 4 * px + 2 * py + pc))
    return 4 * x + 2 * y + c, peers


def _spread_copies(kind, srcs, lands, send_sems, recv_sems):
    me, peers = _peers()
    kinds = [kind] * len(srcs) if isinstance(kind, str) else kind
    remote, local = [], []
    for i, (kd, src, land) in enumerate(zip(kinds, srcs, lands)):
        for k, (dev, idx) in enumerate(peers):
            remote.append(pltpu.make_async_remote_copy(
                src_ref=src if kd == "gather" else src.at[idx], dst_ref=land.at[me],
                send_sem=send_sems.at[i * (N_DEV - 1) + k], recv_sem=recv_sems.at[i * (N_DEV - 1) + k],
                device_id=dev, device_id_type=MESH))
        local.append((src if kd == "gather" else src.at[me], land.at[me]))
    return remote, local


def _land_shape(kind, a):
    return (N_DEV,) + a.shape if kind == "gather" else a.shape


HBM = pl.BlockSpec(memory_space=pltpu.HBM)
SEM = pl.BlockSpec(memory_space=pltpu.SEMAPHORE)


def _hbm(a):
    return pltpu.with_memory_space_constraint(a, pltpu.HBM)


def _spread_start(arrays, kind, name):
    n = len(arrays)

    def body(*refs):
        srcs, lands = refs[:n], refs[n:2 * n]
        send_sems, recv_sems = refs[2 * n], refs[2 * n + 1]
        token = refs[4 * n + 2]
        local_sems = refs[4 * n + 3]
        remote, local = _spread_copies(kind, srcs, lands, send_sems, recv_sems)
        for cp in remote:
            cp.start()
        copies = [pltpu.make_async_copy(s, d, local_sems.at[i]) for i, (s, d) in enumerate(local)]
        for cp in copies:
            cp.start()
        for cp in copies:
            cp.wait()
        token[...] = jnp.zeros_like(token)

    sems = (pltpu.SemaphoreType.DMA((n * (N_DEV - 1),)),) * 2
    lands = [lax.empty(_land_shape(kind, a), a.dtype) for a in arrays]
    out = pl.pallas_call(
        body, name=name,
        out_shape=sems + tuple(pltpu.HBM(a.shape, a.dtype) for a in list(arrays) + lands)
        + (jax.ShapeDtypeStruct((8, LANE), F32),),
        in_specs=[HBM] * (2 * n), out_specs=tuple([SEM] * 2 + [HBM] * (2 * n) + [pl.BlockSpec(memory_space=pltpu.VMEM)]),
        input_output_aliases={j: 2 + j for j in range(2 * n)},
        scratch_shapes=[pltpu.SemaphoreType.DMA((n,))],
        compiler_params=pltpu.CompilerParams(has_side_effects=pltpu.SideEffectType.DATAFLOW_SIDE_EFFECTING),
    )(*[_hbm(a) for a in arrays], *[_hbm(a) for a in lands])
    return (list(out[2:2 + n]), list(out[2 + n:2 + 2 * n]), out[0], out[1]), out[-1]


def _spread_wait(state, kind, after, name):
    srcs, lands, send_sems, recv_sems = state
    n = len(srcs)

    def body(*refs):
        remote, _ = _spread_copies(kind, refs[:n], refs[n:2 * n], refs[2 * n], refs[2 * n + 1])
        for cp in remote:
            cp.wait_send()
        for cp in remote:
            cp.wait_recv()

    out = pl.pallas_call(
        body, name=name,
        out_shape=tuple(pltpu.HBM(a.shape, a.dtype) for a in srcs + lands),
        in_specs=[HBM] * (2 * n) + [SEM, SEM, pl.BlockSpec(memory_space=pl.ANY)], out_specs=tuple([HBM] * (2 * n)),
        input_output_aliases={j: j for j in range(2 * n)},
        compiler_params=pltpu.CompilerParams(has_side_effects=pltpu.SideEffectType.DATAFLOW_SIDE_EFFECTING),
    )(*srcs, *lands, send_sems, recv_sems, after)
    return list(out[n:])


def _spread_async(arrays, kind, name, collective_id):
    n = len(arrays)
    hbm = pltpu.MemorySpace.HBM
    src_refs = [jax.new_ref(a, memory_space=hbm) for a in arrays]
    land_refs = [jax.empty_ref(jax.ShapeDtypeStruct(_land_shape(kind, a), a.dtype), memory_space=hbm) for a in arrays]

    @pl.kernel(mesh=plsc.ScalarSubcoreMesh(axis_name="sequencer", num_cores=1), name=name,
               scratch_types=(pltpu.SemaphoreType.DMA((n * (N_DEV - 1),)), pltpu.SemaphoreType.DMA((n * (N_DEV - 1),)),
                              pltpu.SemaphoreType.DMA((n,))),
               compiler_params=pltpu.CompilerParams(collective_id=collective_id))
    def launch(send_sems, recv_sems, local_sems):
        _, peers = _peers()
        barrier = pltpu.get_barrier_semaphore()
        for dev, _ in peers:
            pl.semaphore_signal(barrier, inc=1, device_id=dev, device_id_type=MESH)
        pl.semaphore_wait(barrier, N_DEV - 1)
        remote, local = _spread_copies(kind, src_refs, land_refs, send_sems, recv_sems)
        copies = [pltpu.make_async_copy(s, d, local_sems.at[i]) for i, (s, d) in enumerate(local)]
        for cp in remote + copies:
            cp.start()
        for cp in remote:
            cp.wait_recv()
        for cp in remote:
            cp.wait_send()
        for cp in copies:
            cp.wait()

    launch()
    return [r[...] for r in land_refs]


def _exchange(scatter, gather, name):
    kinds = ["scatter"] * len(scatter) + ["gather"] * len(gather)
    arrays = list(scatter) + list(gather)
    n = len(arrays)

    def body(*refs):
        send_sems, recv_sems, local_sems = refs[2 * n:]
        remote, local = _spread_copies(kinds, refs[:n], refs[n:2 * n], send_sems, recv_sems)
        copies = [pltpu.make_async_copy(s, d, local_sems.at[i]) for i, (s, d) in enumerate(local)]
        for cp in remote + copies:
            cp.start()
        for cp in remote:
            cp.wait_recv()
        for cp in remote:
            cp.wait_send()
        for cp in copies:
            cp.wait()

    return pl.pallas_call(
        body, name=name,
        out_shape=[jax.ShapeDtypeStruct(_land_shape(k, a), a.dtype) for k, a in zip(kinds, arrays)],
        in_specs=[pl.BlockSpec(memory_space=pl.ANY)] * n, out_specs=[pl.BlockSpec(memory_space=pl.ANY)] * n,
        scratch_shapes=[pltpu.SemaphoreType.DMA((n * (N_DEV - 1),)), pltpu.SemaphoreType.DMA((n * (N_DEV - 1),)),
                        pltpu.SemaphoreType.DMA((n,))],
    )(*arrays)


ADAM_ROWS = 128


def _adam_math(g, w, m, v):
    nm = ADAM_B1 * m + (1.0 - ADAM_B1) * g
    nv = ADAM_B2 * v + (1.0 - ADAM_B2) * (g * g)
    m_hat = nm / (1.0 - ADAM_B1 ** ADAM_STEP)
    v_hat = nv / (1.0 - ADAM_B2 ** ADAM_STEP)
    return -ADAM_LR * (m_hat / (jnp.sqrt(v_hat) + ADAM_EPS) + ADAM_WD * w), nm, nv


def _sum_parts(p_ref):
    g = p_ref[0].astype(F32)
    for s in range(1, N_DEV):
        g = g + p_ref[s].astype(F32)
    return g


def _adam_matrix(parts, w, m, v, name):
    _, r, c = w.shape
    rb = ADAM_ROWS if r % ADAM_ROWS == 0 else r

    def body(p_ref, w_ref, m_ref, v_ref, g_ref, d_ref, nm_ref, nv_ref):
        g = _sum_parts(p_ref)
        g_ref[0] = g
        d_ref[0], nm_ref[0], nv_ref[0] = _adam_math(g, w_ref[0], m_ref[0], v_ref[0])

    blk = pl.BlockSpec((1, rb, c), lambda i: (0, i, 0))
    return pl.pallas_call(
        body, name=name, grid=(r // rb,),
        in_specs=[pl.BlockSpec((N_DEV, rb, c), lambda i: (0, i, 0)), blk, blk, blk],
        out_specs=[blk] * 4, out_shape=[jax.ShapeDtypeStruct(w.shape, F32)] * 4,
        compiler_params=_params(("arbitrary",)),
    )(parts, w, m, v)


def _adam_vectors(parts, ws, ms, vs):
    nv = len(ws)

    def body(*refs):
        p_ref = refs[0]
        w_refs, m_refs, v_refs = refs[1:1 + nv], refs[1 + nv:1 + 2 * nv], refs[1 + 2 * nv:1 + 3 * nv]
        outs = refs[1 + 3 * nv:]
        g_all = _sum_parts(p_ref)
        for i in range(nv):
            n = w_refs[i].shape[1]
            g = g_all[i:i + 1, 0:n]
            d, nm, nvv = _adam_math(g, w_refs[i][...], m_refs[i][...], v_refs[i][...])
            outs[i][...] = g
            outs[nv + i][...] = d
            outs[2 * nv + i][...] = nm
            outs[3 * nv + i][...] = nvv
        outs[4 * nv][...] = g_all[LOSS_ROW:LOSS_ROW + 1, 0:1]

    shapes = [jax.ShapeDtypeStruct(a.shape, F32) for a in ws]
    out = pl.pallas_call(body, name="adam_vectors", out_shape=shapes * 4 + [jax.ShapeDtypeStruct((1, 1), F32)],
                         compiler_params=_params())(parts, *ws, *ms, *vs)
    return out[:nv], out[nv:2 * nv], out[2 * nv:3 * nv], out[3 * nv:4 * nv], out[4 * nv]


MATRICES = (("w_in", 1), ("gdn_conv_w", 1), ("w_out", 0), ("w_ffn_gate", 1), ("w_ffn_up", 1), ("w_ffn_down", 0))
WEIGHTS = ("norm1_w", "w_in", "gdn_conv_w", "gdn_A_log", "gdn_dt_bias", "gdn_out_norm_w", "fox_f_bias", "fox_q_norm_w",
           "fox_k_norm_w", "w_out", "norm2_w", "w_ffn_gate", "w_ffn_up", "w_ffn_down", "final_norm_w")


def _join(blocks, axis):
    _, r, c = blocks.shape
    if axis == 0:
        return blocks.reshape(N_DEV * r, c)
    return blocks.transpose(1, 0, 2).reshape(r, N_DEV * c)


def _cut(full, axis):
    r, c = full.shape
    if axis == 0:
        return full.reshape(N_DEV, r // N_DEV, c)
    return full.reshape(r, N_DEV, c // N_DEV).transpose(1, 0, 2)


def kernel(x, norm1_w, w_in, gdn_conv_w, gdn_A_log, gdn_dt_bias, gdn_out_norm_w, fox_f_bias, fox_q_norm_w, fox_k_norm_w, w_out, norm2_w, w_ffn_gate, w_ffn_up, w_ffn_down, final_norm_w, loss_target, m_norm1_w, m_w_in, m_gdn_conv_w, m_gdn_A_log, m_gdn_dt_bias, m_gdn_out_norm_w, m_fox_f_bias, m_fox_q_norm_w, m_fox_k_norm_w, m_w_out, m_norm2_w, m_w_ffn_gate, m_w_ffn_up, m_w_ffn_down, m_final_norm_w, v_norm1_w, v_w_in, v_gdn_conv_w, v_gdn_A_log, v_gdn_dt_bias, v_gdn_out_norm_w, v_fox_f_bias, v_fox_q_norm_w, v_fox_k_norm_w, v_w_out, v_norm2_w, v_w_ffn_gate, v_w_ffn_up, v_w_ffn_down, v_final_norm_w):
    w = dict(norm1_w=norm1_w, w_in=w_in, gdn_conv_w=gdn_conv_w, gdn_A_log=gdn_A_log, gdn_dt_bias=gdn_dt_bias,
             gdn_out_norm_w=gdn_out_norm_w, fox_f_bias=fox_f_bias, fox_q_norm_w=fox_q_norm_w, fox_k_norm_w=fox_k_norm_w,
             w_out=w_out, norm2_w=norm2_w, w_ffn_gate=w_ffn_gate, w_ffn_up=w_ffn_up, w_ffn_down=w_ffn_down,
             final_norm_w=final_norm_w)
    m = dict(norm1_w=m_norm1_w, w_in=m_w_in, gdn_conv_w=m_gdn_conv_w, gdn_A_log=m_gdn_A_log, gdn_dt_bias=m_gdn_dt_bias,
             gdn_out_norm_w=m_gdn_out_norm_w, fox_f_bias=m_fox_f_bias, fox_q_norm_w=m_fox_q_norm_w,
             fox_k_norm_w=m_fox_k_norm_w, w_out=m_w_out, norm2_w=m_norm2_w, w_ffn_gate=m_w_ffn_gate,
             w_ffn_up=m_w_ffn_up, w_ffn_down=m_w_ffn_down, final_norm_w=m_final_norm_w)
    v = dict(norm1_w=v_norm1_w, w_in=v_w_in, gdn_conv_w=v_gdn_conv_w, gdn_A_log=v_gdn_A_log, gdn_dt_bias=v_gdn_dt_bias,
             gdn_out_norm_w=v_gdn_out_norm_w, fox_f_bias=v_fox_f_bias, fox_q_norm_w=v_fox_q_norm_w,
             fox_k_norm_w=v_fox_k_norm_w, w_out=v_w_out, norm2_w=v_norm2_w, w_ffn_gate=v_w_ffn_gate,
             w_ffn_up=v_w_ffn_up, w_ffn_down=v_w_ffn_down, final_norm_w=v_final_norm_w)
    axis_of = dict(MATRICES)
    late = ("w_out", "w_ffn_gate", "w_ffn_up", "w_ffn_down")
    xs, tgt = x[0], loss_target[0]
    vp = _vector_params({n: w[n] for n in VECTORS})

    state_in, token = _spread_start([w["w_in"][0].astype(WIRE), w["gdn_conv_w"][0]], "gather", "gather_in_start")
    first_late = w[late[0]][0].astype(WIRE) + token[0, 0].astype(WIRE)
    state_late, token = _spread_start([first_late] + [w[n][0].astype(WIRE) for n in late[1:]], "gather", "gather_late_start")
    w_in_blocks, conv_blocks = _spread_wait(state_in, "gather", token, "gather_in_wait")
    w_main, w_small = _split_w_in(_join(w_in_blocks, 1))
    conv_w = _join(conv_blocks, 1)
    f = _mixer_forward(xs, vp, w_main, w_small, conv_w)
    full = {n: _join(b, axis_of[n])
            for n, b in zip(late, _spread_wait(state_late, "gather", f["mixf"], "gather_late_wait"))}

    (h2, act, dgate, dup, dx3, dx2, dmixg, dmixf, dn2w, dfw, loss) = _tail(
        xs, f["mixg"], f["mixf"], tgt, full["w_out"], vp["n2w"], full["w_ffn_gate"], full["w_ffn_up"],
        full["w_ffn_down"], vp["fw"])
    dlate = {"w_out": jnp.concatenate([_wgrad(f["mixg"], dx2, "dw_out_gdn"), _wgrad(f["mixf"], dx2, "dw_out_fox")], axis=0),
             "w_ffn_gate": _wgrad(h2, dgate, "dw_gate"), "w_ffn_up": _wgrad(h2, dup, "dw_up"),
             "w_ffn_down": _wgrad(act, dx3, "dw_down")}

    state_grads, token = _spread_start([_cut(dlate[n], axis_of[n]) for n in late], "scatter", "grads_late_start")
    grad_x, dw_in, dconv, small = _mixer_backward(xs, vp, w_main, w_small, conv_w, f, dx2, dmixg, dmixf,
                                                  vp["onw"] + token[0:1, 0:DH])
    vec = _pack_vectors(small["dn1w"], dn2w, dfw, small["gsum"], small["donw"], small["dfnw"], loss)
    parts = {}
    parts["w_in"], parts["gdn_conv_w"], parts_vec = _exchange([_cut(dw_in, 1), _cut(dconv, 1)], [vec], "exchange_last")
    parts.update(zip(late, _spread_wait(state_grads, "scatter", parts_vec, "grads_late_wait")))

    results = [{}, {}, {}, {}]
    for n, _ in MATRICES:
        for d, a in zip(results, _adam_matrix(parts[n], w[n], m[n], v[n], "adam_" + n)):
            d[n] = a
    row = lambda a: a.reshape(1, -1)
    *vec_out, total_loss = _adam_vectors(parts_vec, [row(w[n]) for n in VECTORS], [row(m[n]) for n in VECTORS],
                                         [row(v[n]) for n in VECTORS])
    for d, arrs in zip(results, vec_out):
        for n, a in zip(VECTORS, arrs):
            d[n] = a.reshape(w[n].shape)
    return (total_loss[0, 0], grad_x[None], *[d[n] for d in results for n in WEIGHTS])
```

```python
import functools

import jax
import jax.numpy as jnp
from jax import lax
from jax.experimental import pallas as pl
from jax.experimental.pallas import tpu as pltpu

F32 = jnp.float32
MXU = jnp.bfloat16
WIRE = jnp.bfloat16
HI = lax.Precision.HIGHEST
EPS = 1e-6

N_DEV = 8
HEADS = 8
DH = 64
WIDTH = HEADS * DH
CHUNK = 64
LANE = 128
ROW_ALIGN = 16
TB = 256
QB = 256
VMEM_LIMIT = 60 * 1024 * 1024

ADAM_LR = 0.001
ADAM_B1 = 0.9
ADAM_B2 = 0.999
ADAM_EPS = 1e-08
ADAM_WD = 0.01
ADAM_STEP = 10

MESH = pl.DeviceIdType.MESH


def _params(sem=None):
    return pltpu.CompilerParams(dimension_semantics=sem, vmem_limit_bytes=VMEM_LIMIT)


def _resident(shape):
    n = len(shape)
    return pl.BlockSpec(shape, lambda *_: (0,) * n, pipeline_mode=pl.Buffered(1))


def _dot(a, b):
    return jnp.dot(a.astype(MXU), b.astype(MXU), preferred_element_type=F32)


def _dot_nt(a, b):
    return lax.dot_general(a.astype(MXU), b.astype(MXU), (((1,), (1,)), ((), ())), preferred_element_type=F32)


def _dot_tn(a, b):
    return lax.dot_general(a.astype(MXU), b.astype(MXU), (((0,), (0,)), ((), ())), preferred_element_type=F32)


def _hdot(a, b):
    return jnp.dot(a, b, precision=HI, preferred_element_type=F32)


def _hdot_nt(a, b):
    return lax.dot_general(a, b, (((1,), (1,)), ((), ())), precision=HI, preferred_element_type=F32)


def _hdot_tn(a, b):
    return lax.dot_general(a, b, (((0,), (0,)), ((), ())), precision=HI, preferred_element_type=F32)


def _sigmoid(x):
    return 0.5 * jnp.tanh(0.5 * x) + 0.5


def _softplus(x):
    return jnp.maximum(x, 0.0) + jnp.log(1.0 + jnp.exp(-jnp.abs(x)))


def _head_sum_matrix():
    ri = lax.broadcasted_iota(jnp.int32, (LANE, LANE), 0) // DH
    ci = lax.broadcasted_iota(jnp.int32, (LANE, LANE), 1) // DH
    return (ri == ci).astype(F32)


def _group_sum(a, ones_matrix):
    hi = a.astype(jnp.bfloat16)
    lo = (a - hi.astype(F32)).astype(jnp.bfloat16)
    m = ones_matrix.astype(jnp.bfloat16)
    return jnp.dot(hi, m, preferred_element_type=F32) + jnp.dot(lo, m, preferred_element_type=F32)


def _shift_down(x, s):
    return pltpu.roll(x, s, 0)


def _shift_up(x, s):
    return pltpu.roll(x, x.shape[0] - s, 0)


def _inproj(x, n1w, w_main, w_small):
    t, d = x.shape
    nm = w_main.shape[1]
    tb = min(TB, t)

    def body(x_ref, nw_ref, wm_ref, ws_ref, ht_ref, pm_ref, ps_ref):
        xv = x_ref[...]
        r = lax.rsqrt(jnp.mean(xv * xv, axis=-1, keepdims=True) + EPS)
        h32 = xv * r * nw_ref[...]
        h = h32.astype(MXU)
        ht_ref[...] = h32.T.astype(MXU)
        pm_ref[...] = jnp.dot(h, wm_ref[...], preferred_element_type=F32)
        ps_ref[...] = jnp.dot(h, ws_ref[...], preferred_element_type=F32)

    return pl.pallas_call(
        body, name="inproj", grid=(t // tb,),
        in_specs=[pl.BlockSpec((tb, d), lambda i: (i, 0)), _resident((1, d)), _resident((d, nm)), _resident((d, LANE))],
        out_specs=[pl.BlockSpec((d, tb), lambda i: (0, i)), pl.BlockSpec((tb, nm), lambda i: (i, 0)),
                   pl.BlockSpec((tb, LANE), lambda i: (i, 0))],
        out_shape=[jax.ShapeDtypeStruct((d, t), MXU), jax.ShapeDtypeStruct((t, nm), F32),
                   jax.ShapeDtypeStruct((t, LANE), F32)],
        compiler_params=_params(("arbitrary",)),
    )(x, n1w, w_main, w_small)


def _inproj_bwd(x, n1w, dx2, dgdn, dz, dfox, dfg, dps, w_main, w_small):
    t, d = x.shape
    tb = min(TB, t)
    w3 = 3 * WIDTH

    def body(x_ref, nw_ref, dx2_ref, dgdn_ref, dz_ref, dfox_ref, dfg_ref, dps_ref, wm_ref, ws_ref, gx_ref, dnw_ref):
        dh = _dot_nt(dgdn_ref[...], wm_ref[:, 0:w3])
        dh += _dot_nt(dz_ref[...], wm_ref[:, w3:w3 + WIDTH])
        dh += _dot_nt(dfox_ref[...], wm_ref[:, w3 + WIDTH:2 * w3 + WIDTH])
        dh += _dot_nt(dfg_ref[...], wm_ref[:, 2 * w3 + WIDTH:2 * w3 + 2 * WIDTH])
        dh += _dot_nt(dps_ref[...], ws_ref[...])
        xv = x_ref[...]
        r = lax.rsqrt(jnp.mean(xv * xv, axis=-1, keepdims=True) + EPS)
        xn = xv * r

        @pl.when(pl.program_id(0) == 0)
        def _():
            dnw_ref[...] = jnp.zeros_like(dnw_ref)

        dnw_ref[...] += jnp.sum(dh * xn, axis=0, keepdims=True)
        g = dh * nw_ref[...]
        gx_ref[...] = dx2_ref[...] + r * (g - xn * jnp.mean(g * xn, axis=-1, keepdims=True))

    def tok(n):
        return pl.BlockSpec((tb, n), lambda i: (i, 0))

    return pl.pallas_call(
        body, name="inproj_bwd", grid=(t // tb,),
        in_specs=[tok(d), _resident((1, d)), tok(d), tok(w3), tok(WIDTH), tok(w3), tok(WIDTH), tok(LANE),
                  _resident(w_main.shape), _resident(w_small.shape)],
        out_specs=[tok(d), pl.BlockSpec((1, d), lambda i: (0, 0))],
        out_shape=[jax.ShapeDtypeStruct((t, d), F32), jax.ShapeDtypeStruct((1, d), F32)],
        compiler_params=_params(("arbitrary",)),
    )(x, n1w, dx2, dgdn, dz, dfox, dfg, dps, w_main, w_small)


def _gate_lanes(shape):
    lane = lax.broadcasted_iota(jnp.int32, shape, 1)
    return lane < HEADS, (lane >= HEADS) & (lane < 2 * HEADS), (lane >= 2 * HEADS) & (lane < 3 * HEADS)


def _block_masks():
    ri = lax.broadcasted_iota(jnp.int32, (LANE, LANE), 0)
    ci = lax.broadcasted_iota(jnp.int32, (LANE, LANE), 1)
    same = (ri // CHUNK) == (ci // CHUNK)
    return ((ri >= ci).astype(F32), (ri <= ci).astype(F32), (same & (ri >= ci)).astype(F32),
            (same & (ri <= ci)).astype(F32), same.astype(F32))


def _gates(ps, gparams):
    t = ps.shape[0]
    nb = t // LANE

    def body(ps_ref, gp_ref, out_ref, run_ref, tot_ref, runt_ref):
        p = ps_ref[...]
        is_b, is_a, is_f = _gate_lanes(p.shape)
        z = p + gp_ref[0:1, :]
        neg_exp_a = -jnp.exp(gp_ref[1:2, :])
        glog = neg_exp_a * _softplus(z)
        logf = -_softplus(-z)
        out_ref[...] = jnp.where(is_b, _sigmoid(p), jnp.where(is_a, glog, jnp.where(is_f, logf, 0.0)))
        tril, _, tril_c, _, same_c = _block_masks()
        off = jnp.zeros((1, LANE), F32)
        for b in range(nb):
            rows = slice(b * LANE, (b + 1) * LANE)
            blk = out_ref[rows, :]
            ga = jnp.where(is_a[:LANE], blk, 0.0)
            fb = _hdot(tril, jnp.where(is_f[:LANE], blk, 0.0)) + off
            run = fb + _hdot(tril_c, ga)
            run_ref[rows, :] = run
            runt_ref[:, rows] = run.T
            tot_ref[rows, :] = _hdot(same_c, ga)
            off = fb[LANE - 1:LANE, :]

    return pl.pallas_call(
        body, name="gates",
        out_shape=[jax.ShapeDtypeStruct((t, LANE), F32)] * 3 + [jax.ShapeDtypeStruct((LANE, t), F32)],
        compiler_params=_params(),
    )(ps, gparams)


def _gates_bwd(ps, gparams, dcol_g, dtot_g, dcol_f, drow_g, drow_f):
    t = ps.shape[0]
    nb = t // LANE

    def body(ps_ref, gp_ref, dcg_ref, dtg_ref, dcf_ref, drg_ref, drf_ref, dps_ref, sums_ref, dl_ref, d0_ref, tr_ref):
        p = ps_ref[...]
        is_b, is_a, is_f = _gate_lanes(p.shape)
        _, triu, _, triu_c, same_c = _block_masks()
        tr_ref[...] = jnp.zeros_like(tr_ref)
        off = jnp.zeros((1, LANE), F32)
        for b in reversed(range(nb)):
            rows = slice(b * LANE, (b + 1) * LANE)
            tr_ref[HEADS:2 * HEADS, :] = drg_ref[:, rows]
            tr_ref[2 * HEADS:3 * HEADS, :] = drf_ref[:, rows]
            d = dcg_ref[rows, :] + dcf_ref[rows, :] + tr_ref[...].T
            d0_ref[rows, :] = d
            dlf = _hdot(triu, jnp.where(is_f[:LANE], d, 0.0)) + off
            dla = (_hdot(triu_c, jnp.where(is_a[:LANE], d, 0.0))
                   + _hdot(same_c, jnp.where(is_a[:LANE], dtg_ref[rows, :], 0.0)))
            dl_ref[rows, :] = dlf + dla
            off = dlf[0:1, :]
        z = p + gp_ref[0:1, :]
        neg_exp_a = -jnp.exp(gp_ref[1:2, :])
        sb = _sigmoid(p)
        glog = neg_exp_a * _softplus(z)
        dl = dl_ref[...]
        dp = jnp.where(is_b, d0_ref[...] * sb * (1.0 - sb),
                       jnp.where(is_a, dl * neg_exp_a * _sigmoid(z), jnp.where(is_f, dl * _sigmoid(-z), 0.0)))
        dps_ref[...] = dp
        s_a = jnp.sum(jnp.where(is_a, dl * glog, 0.0), axis=0, keepdims=True)
        s_p = jnp.sum(jnp.where(is_b, 0.0, dp), axis=0, keepdims=True)
        row = lax.broadcasted_iota(jnp.int32, (8, LANE), 0)
        from_a = pltpu.roll(jnp.where(row == 0, s_a, jnp.where(row == 1, s_p, 0.0)), LANE - HEADS, 1)
        from_f = pltpu.roll(jnp.where(row == 2, s_p, 0.0), LANE - 2 * HEADS, 1)
        lane = lax.broadcasted_iota(jnp.int32, (8, LANE), 1)
        sums_ref[...] = jnp.where(lane < HEADS, from_a + from_f, 0.0)

    return pl.pallas_call(
        body, name="gates_bwd",
        out_shape=[jax.ShapeDtypeStruct((t, LANE), F32), jax.ShapeDtypeStruct((8, LANE), F32)],
        scratch_shapes=[pltpu.VMEM((t, LANE), F32), pltpu.VMEM((t, LANE), F32), pltpu.VMEM((LANE, LANE), F32)],
        compiler_params=_params(),
    )(ps, gparams, dcol_g, dtot_g, dcol_f, drow_g, drow_f)


def _conv(xv, w):
    acc = w[3:4, :] * xv
    for s in range(1, 4):
        acc += w[3 - s:4 - s, :] * _shift_down(xv, s)
    return acc


PREP_ROWS = 256
HALO = 8


def _tile_rows(r):
    return pl.ds(pl.multiple_of(r * PREP_ROWS, PREP_ROWS), PREP_ROWS)


def _gdn_prep(pm, conv_w):
    t = pm.shape[0]
    nj = WIDTH // LANE
    win = PREP_ROWS + HALO

    def body(x_ref, w_ref, o_ref, xp_ref):
        kind = pl.program_id(0)
        xp_ref[0:HALO, :] = jnp.zeros((HALO, LANE), F32)
        xp_ref[HALO:, :] = x_ref[...]
        w = w_ref[...]
        hs = _head_sum_matrix()

        def tile(r, _):
            xw = xp_ref[pl.ds(pl.multiple_of(r * PREP_ROWS, PREP_ROWS), win), :]
            acc = _conv(xw, w)[HALO:]
            y = acc * _sigmoid(acc)
            out = jnp.where(kind < 2, y * lax.rsqrt(_group_sum(y * y, hs) + EPS), y)
            o_ref[0, 0, _tile_rows(r), :] = out[:, :DH]
            o_ref[0, 1, _tile_rows(r), :] = out[:, DH:]
            return 0

        lax.fori_loop(0, t // PREP_ROWS, tile, 0)

    return pl.pallas_call(
        body, name="gdn_prep", grid=(3, nj),
        in_specs=[pl.BlockSpec((t, LANE), lambda i, j: (0, i * nj + j)),
                  pl.BlockSpec((4, LANE), lambda i, j: (0, i * nj + j))],
        out_specs=pl.BlockSpec((1, 2, t, DH), lambda i, j: (i, j, 0, 0)),
        out_shape=jax.ShapeDtypeStruct((3, HEADS, t, DH), F32),
        scratch_shapes=[pltpu.VMEM((t + HALO, LANE), F32)],
        compiler_params=_params(("arbitrary", "arbitrary")),
    )(pm, conv_w)


def _gdn_prep_bwd(pm, conv_w, dqkv):
    t = pm.shape[0]
    nj = WIDTH // LANE
    win = PREP_ROWS + 2 * HALO

    def body(x_ref, w_ref, d_ref, dx_ref, dw_ref, xp_ref, dp_ref):
        kind = pl.program_id(0)
        zeros = jnp.zeros((HALO, LANE), F32)
        for ref in (xp_ref, dp_ref):
            ref[0:HALO, :] = zeros
            ref[HALO + t:, :] = zeros
        xp_ref[HALO:HALO + t, :] = x_ref[...]
        dp_ref[HALO:HALO + t, 0:DH] = d_ref[0, 0]
        dp_ref[HALO:HALO + t, DH:] = d_ref[0, 1]
        w = w_ref[...]
        hs = _head_sum_matrix()
        rows = lax.broadcasted_iota(jnp.int32, (win, LANE), 0)
        in_tile = (rows >= HALO) & (rows < HALO + PREP_ROWS)

        def tile(r, dw):
            start = pl.multiple_of(r * PREP_ROWS, PREP_ROWS)
            xw = xp_ref[pl.ds(start, win), :]
            d = dp_ref[pl.ds(start, win), :]
            acc = _conv(xw, w)
            sg = _sigmoid(acc)
            y = acc * sg
            rn = lax.rsqrt(_group_sum(y * y, hs) + EPS)
            yn = y * rn
            dy = jnp.where(kind < 2, rn * (d - yn * _group_sum(d * yn, hs)), d)
            dacc = dy * sg * (1.0 + acc * (1.0 - sg))
            dx = w[3:4, :] * dacc
            for s in range(1, 4):
                dx += w[3 - s:4 - s, :] * _shift_up(dacc, s)
            dx_ref[_tile_rows(r), :] = dx[HALO:HALO + PREP_ROWS]
            dm = jnp.where(in_tile, dacc, 0.0)
            return tuple(dw[i] + jnp.sum(dm * (xw if i == 3 else _shift_down(xw, 3 - i)), axis=0, keepdims=True)
                         for i in range(4))

        dw = lax.fori_loop(0, t // PREP_ROWS, tile, tuple(jnp.zeros((1, LANE), F32) for _ in range(4)))
        for i in range(4):
            dw_ref[i:i + 1, :] = dw[i]

    return pl.pallas_call(
        body, name="gdn_prep_bwd", grid=(3, nj),
        in_specs=[pl.BlockSpec((t, LANE), lambda i, j: (0, i * nj + j)),
                  pl.BlockSpec((4, LANE), lambda i, j: (0, i * nj + j)),
                  pl.BlockSpec((1, 2, t, DH), lambda i, j: (i, j, 0, 0))],
        out_specs=[pl.BlockSpec((t, LANE), lambda i, j: (0, i * nj + j)),
                   pl.BlockSpec((4, LANE), lambda i, j: (0, i * nj + j))],
        out_shape=[jax.ShapeDtypeStruct((t, 3 * WIDTH), F32), jax.ShapeDtypeStruct((4, 3 * WIDTH), F32)],
        scratch_shapes=[pltpu.VMEM((t + 2 * HALO, LANE), F32), pltpu.VMEM((t + 2 * HALO, LANE), F32)],
        compiler_params=_params(("arbitrary", "arbitrary")),
    )(pm, conv_w, dqkv)


FOX_COL0 = 4 * WIDTH // LANE


def _fox_prep(pm, nw):
    t = pm.shape[0]
    nj = WIDTH // LANE

    def body(x_ref, w_ref, o_ref):
        kind = pl.program_id(0)
        hs = _head_sum_matrix()
        wk = w_ref[pl.ds(kind, 1), :]

        def tile(r, _):
            xv = x_ref[_tile_rows(r), :]
            ms = _group_sum(xv * xv, hs) * (1.0 / DH)
            out = jnp.where(kind < 2, xv * lax.rsqrt(ms + EPS) * wk, xv)
            o_ref[0, 0, _tile_rows(r), :] = out[:, :DH]
            o_ref[0, 1, _tile_rows(r), :] = out[:, DH:]
            return 0

        lax.fori_loop(0, t // PREP_ROWS, tile, 0)

    return pl.pallas_call(
        body, name="fox_prep", grid=(3, nj),
        in_specs=[pl.BlockSpec((t, LANE), lambda i, j: (0, FOX_COL0 + i * nj + j)),
                  pl.BlockSpec((3, LANE), lambda i, j: (0, 0))],
        out_specs=pl.BlockSpec((1, 2, t, DH), lambda i, j: (i, j, 0, 0)),
        out_shape=jax.ShapeDtypeStruct((3, HEADS, t, DH), F32),
        compiler_params=_params(("arbitrary", "arbitrary")),
    )(pm, nw)


def _fox_prep_bwd(pm, nw, dqkv):
    t = pm.shape[0]
    nj = WIDTH // LANE

    def body(x_ref, w_ref, d_ref, dx_ref, dw_ref):
        kind = pl.program_id(0)
        hs = _head_sum_matrix()
        wk = w_ref[pl.ds(kind, 1), :]

        def tile(r, dw):
            xv = x_ref[_tile_rows(r), :]
            rn = lax.rsqrt(_group_sum(xv * xv, hs) * (1.0 / DH) + EPS)
            xn = xv * rn
            d = jnp.concatenate([d_ref[0, 0, _tile_rows(r), :], d_ref[0, 1, _tile_rows(r), :]], axis=1)
            g = d * wk
            dxn = rn * (g - xn * _group_sum(g * xn, hs) * (1.0 / DH))
            dx_ref[_tile_rows(r), :] = jnp.where(kind < 2, dxn, d)
            return dw + jnp.sum(d * xn, axis=0, keepdims=True)

        dw_ref[0, 0] = lax.fori_loop(0, t // PREP_ROWS, tile, jnp.zeros((1, LANE), F32))

    return pl.pallas_call(
        body, name="fox_prep_bwd", grid=(3, nj),
        in_specs=[pl.BlockSpec((t, LANE), lambda i, j: (0, FOX_COL0 + i * nj + j)),
                  pl.BlockSpec((3, LANE), lambda i, j: (0, 0)),
                  pl.BlockSpec((1, 2, t, DH), lambda i, j: (i, j, 0, 0))],
        out_specs=[pl.BlockSpec((t, LANE), lambda i, j: (0, i * nj + j)),
                   pl.BlockSpec((1, 1, 1, LANE), lambda i, j: (i, j, 0, 0))],
        out_shape=[jax.ShapeDtypeStruct((t, 3 * WIDTH), F32), jax.ShapeDtypeStruct((3, nj, 1, LANE), F32)],
        compiler_params=_params(("arbitrary", "arbitrary")),
    )(pm, nw, dqkv)


SC = 256
CPS = SC // CHUNK
GDN_HP = 2
Q_SCALE = DH ** -0.5


def _sc_masks():
    ri = lax.broadcasted_iota(jnp.int32, (SC, SC), 0)
    ci = lax.broadcasted_iota(jnp.int32, (SC, SC), 1)
    same = (ri // CHUNK) == (ci // CHUNK)
    return same & (ri >= ci), same & (ri > ci), ri == ci


def _unit_lower_inverses(ms, eye):
    invs = [jnp.where(eye, 1.0, 0.0) + m for m in ms]
    for _ in range(5):
        ms = [_dot(m, m) for m in ms]
        invs = [inv + _dot(inv, m) for inv, m in zip(invs, ms)]
    return invs


def _lane_col(blk, lane_idx):
    lane = lax.broadcasted_iota(jnp.int32, blk.shape, 1)
    return jnp.sum(jnp.where(lane == lane_idx, blk, 0.0), axis=1, keepdims=True)


def _to_lane(col, lane_idx):
    lane = lax.broadcasted_iota(jnp.int32, (col.shape[0], LANE), 1)
    return jnp.where(lane == lane_idx, col, 0.0)


def _gdn_columns(gates_ref, run_ref, tot_ref, runt_ref, rows, h):
    return (_lane_col(gates_ref[rows, :], h), _lane_col(run_ref[rows, :], HEADS + h),
            _lane_col(tot_ref[rows, :], HEADS + h), runt_ref[pl.ds(h, 1), rows])


def _gdn_local(q, k, beta, gc, gl, grow, causal):
    decay = jnp.exp(jnp.where(causal, gc - grow, -1e30))
    egc = jnp.exp(gc)
    ekd = jnp.exp(gl - gc)
    qs = q * Q_SCALE
    kb = k * beta
    kk = _dot_nt(kb, k)
    qk = _dot_nt(qs, k)
    return beta, gl, decay, egc, ekd, qs, kb, kk, qk, jnp.where(causal, qk * decay, 0.0)


def _chunk_rows(c):
    return pl.ds(pl.multiple_of(c * CHUNK, CHUNK), CHUNK)


def _sc_rows(b):
    return pl.ds(pl.multiple_of(b * SC, SC), SC)


def _gdn_fwd(qkv, gates, run, tot, run_t):
    t = qkv.shape[2]
    nc = t // CHUNK
    nsc = t // SC

    hp_n = GDN_HP
    heads = range(hp_n)

    def body(qkv_ref, gates_ref, run_ref, tot_ref, runt_ref, o_ref, st_ref, inv_ref, kc_s, qc_s, g_s, au_s):
        hp = pl.program_id(0)
        causal, strict, eye = _sc_masks()

        def local(b, _):
            rows = _sc_rows(b)
            loc = [_gdn_local(qkv_ref[0, hh, rows, :], qkv_ref[1, hh, rows, :],
                              *_gdn_columns(gates_ref, run_ref, tot_ref, runt_ref, rows, hp * hp_n + hh), causal)
                   for hh in heads]
            invs = _unit_lower_inverses([-jnp.where(strict, l[7] * l[2], 0.0) for l in loc], eye)
            us, ws = [], []
            for hh in heads:
                beta, _, _, egc, _, _, kb, _, _, _ = loc[hh]
                inv_ref[hh, rows, :] = invs[hh].astype(inv_ref.dtype)
                us.append(_dot(invs[hh], qkv_ref[2, hh, rows, :] * beta))
                ws.append(_dot(invs[hh], kb * egc))
            for hh in heads:
                _, _, _, egc, ekd, qs, _, _, _, attn = loc[hh]
                g_s[hh, rows, :] = qs * egc - _dot(attn, ws[hh])
                au_s[hh, rows, :] = _dot(attn, us[hh])
                kd = qkv_ref[1, hh, rows, :] * ekd
                for j in range(CPS):
                    sl = slice(j * CHUNK, (j + 1) * CHUNK)
                    kc_s[hh, b * CPS + j] = _dot_tn(kd[sl], ws[hh][sl])
                    qc_s[hh, b * CPS + j] = _dot_tn(kd[sl], us[hh][sl])
            return 0

        lax.fori_loop(0, nsc, local, 0)

        def step(c, states):
            rows = _chunk_rows(c)
            tot_row = tot_ref[pl.ds(c * CHUNK, 1), :]
            new = []
            for hh in heads:
                s = states[hh]
                st_ref[hh, c] = s
                o_ref[hh, rows, :] = _dot(g_s[hh, rows, :], s) + au_s[hh, rows, :]
                egl = jnp.exp(_lane_col(tot_row, HEADS + hp * hp_n + hh))
                new.append(egl * s - _dot(kc_s[hh, c], s) + qc_s[hh, c])
            return tuple(new)

        lax.fori_loop(0, nc, step, tuple(jnp.zeros((DH, DH), F32) for _ in heads))

    whole = pl.BlockSpec((t, LANE), lambda h: (0, 0))
    return pl.pallas_call(
        body, name="gdn_fwd", grid=(HEADS // hp_n,),
        in_specs=[pl.BlockSpec((3, hp_n, t, DH), lambda h: (0, h, 0, 0)), whole, whole, whole,
                  pl.BlockSpec((HEADS, t), lambda h: (1, 0))],
        out_specs=[pl.BlockSpec((hp_n, t, DH), lambda h: (h, 0, 0)), pl.BlockSpec((hp_n, nc, DH, DH), lambda h: (h, 0, 0, 0)),
                   pl.BlockSpec((hp_n, t, SC), lambda h: (h, 0, 0))],
        out_shape=[jax.ShapeDtypeStruct((HEADS, t, DH), F32), jax.ShapeDtypeStruct((HEADS, nc, DH, DH), F32),
                   jax.ShapeDtypeStruct((HEADS, t, SC), MXU)],
        scratch_shapes=[pltpu.VMEM((hp_n, nc, DH, DH), F32), pltpu.VMEM((hp_n, nc, DH, DH), F32),
                        pltpu.VMEM((hp_n, t, DH), F32), pltpu.VMEM((hp_n, t, DH), F32)],
        compiler_params=_params(("arbitrary",)),
    )(qkv, gates, run, tot, run_t)


def _gdn_bwd(qkv, gates, run, tot, run_t, inv, states, do):
    t = qkv.shape[2]
    nc = t // CHUNK
    nsc = t // SC

    hp_n = GDN_HP
    heads = range(hp_n)

    def body(qkv_ref, gates_ref, run_ref, tot_ref, runt_ref, inv_ref, st_ref, do_ref,
             dqkv_ref, dcol_ref, dtot_ref, drow_ref, u_s, w_s, kc_s, h_s, dsn_s):
        hp = pl.program_id(0)
        causal, strict, _ = _sc_masks()

        @pl.when(hp == 0)
        def _():
            dcol_ref[...] = jnp.zeros_like(dcol_ref)
            dtot_ref[...] = jnp.zeros_like(dtot_ref)

        def local_of(hh, rows):
            return _gdn_local(qkv_ref[0, hh, rows, :], qkv_ref[1, hh, rows, :],
                              *_gdn_columns(gates_ref, run_ref, tot_ref, runt_ref, rows, hp * hp_n + hh), causal)

        def local(b, _):
            rows = _sc_rows(b)
            loc = [local_of(hh, rows) for hh in heads]
            us, ws = [], []
            for hh in heads:
                beta, _, _, egc, _, _, kb, _, _, _ = loc[hh]
                inv_b = inv_ref[hh, rows, :]
                us.append(_dot(inv_b, qkv_ref[2, hh, rows, :] * beta))
                ws.append(_dot(inv_b, kb * egc))
            gs = [loc[hh][5] * loc[hh][3] - _dot(loc[hh][9], ws[hh]) for hh in heads]
            for hh in heads:
                u_s[hh, rows, :] = us[hh]
                w_s[hh, rows, :] = ws[hh]
                kd = qkv_ref[1, hh, rows, :] * loc[hh][4]
                dout = do_ref[hh, rows, :]
                for j in range(CPS):
                    sl = slice(j * CHUNK, (j + 1) * CHUNK)
                    kc_s[hh, b * CPS + j] = _dot_tn(kd[sl], ws[hh][sl])
                    h_s[hh, b * CPS + j] = _dot_tn(gs[hh][sl], dout[sl])
            return 0

        lax.fori_loop(0, nsc, local, 0)

        def step(i, dss):
            c = nc - 1 - i
            tot_row = tot_ref[pl.ds(c * CHUNK, 1), :]
            new = []
            for hh in heads:
                ds = dss[hh]
                dsn_s[hh, c] = ds
                egl = jnp.exp(_lane_col(tot_row, HEADS + hp * hp_n + hh))
                new.append(egl * ds - _dot_tn(kc_s[hh, c], ds) + h_s[hh, c])
            return tuple(new)

        lax.fori_loop(0, nc, step, tuple(jnp.zeros((DH, DH), F32) for _ in heads))

        def back(b, _):
            rows = _sc_rows(b)
            first = lax.broadcasted_iota(jnp.int32, (CHUNK, 1), 0) == 0
            loc = [local_of(hh, rows) for hh in heads]
            mid = []
            for hh in heads:
                beta, gl, decay, egc, ekd, qs, kb, kk, qk, attn = loc[hh]
                u, w = u_s[hh, rows, :], w_s[hh, rows, :]
                kd = qkv_ref[1, hh, rows, :] * ekd
                dout = do_ref[hh, rows, :]
                dg_p, dkd_p, dw_p, du_p, dgl_p = [], [], [], [], []
                for j in range(CPS):
                    sl = slice(j * CHUNK, (j + 1) * CHUNK)
                    s = st_ref[hh, b * CPS + j]
                    dsn = dsn_s[hh, b * CPS + j]
                    dkc = -_dot_nt(dsn, s)
                    dg_p.append(_dot_nt(dout[sl], s))
                    dkd_p.append(_dot_nt(w[sl], dkc) + _dot_nt(u[sl], dsn))
                    dw_p.append(_dot(kd[sl], dkc))
                    du_p.append(_dot(kd[sl], dsn))
                    degl = jnp.sum(jnp.sum(s * dsn, axis=1, keepdims=True), axis=0, keepdims=True)
                    dgl_p.append(jnp.where(first, degl * jnp.exp(gl[j * CHUNK:j * CHUNK + 1, :]), 0.0))
                dg, dkd = jnp.concatenate(dg_p, axis=0), jnp.concatenate(dkd_p, axis=0)
                da = jnp.where(causal, _dot_nt(dout, u) - _dot_nt(dg, w), 0.0)
                at = _dot_tn(attn, jnp.concatenate([dout, dg], axis=1))
                du = at[:, :DH] + jnp.concatenate(du_p, axis=0)
                dw = jnp.concatenate(dw_p, axis=0) - at[:, DH:]
                mid.append((kd, dg, dkd, da, du, dw, jnp.concatenate(dgl_p, axis=0)))
            inv_ts = [inv_ref[hh, rows, :].astype(F32).T for hh in heads]
            its = [_dot(inv_ts[hh], jnp.concatenate([mid[hh][4], mid[hh][5]], axis=1)) for hh in heads]
            dinvs = []
            for hh in heads:
                beta, _, _, egc, _, _, kb, _, _, _ = loc[hh]
                dinvs.append(_dot_nt(mid[hh][4], qkv_ref[2, hh, rows, :] * beta) + _dot_nt(mid[hh][5], kb * egc))
            half = [_dot(inv_ts[hh], dinvs[hh]) for hh in heads]
            dls = [jnp.where(strict, -_dot(half[hh], inv_ts[hh]), 0.0) for hh in heads]
            for hh in heads:
                head = hp * hp_n + hh
                beta, gl, decay, egc, ekd, qs, kb, kk, qk, attn = loc[hh]
                kd, dg, dkd, da, du, dw, dgl_first = mid[hh]
                k, v = qkv_ref[1, hh, rows, :], qkv_ref[2, hh, rows, :]
                dvb, dkbe = its[hh][:, :DH], its[hh][:, DH:]
                dl = dls[hh]
                dlogd = (dl * kk + da * qk) * decay
                dkk = dl * decay
                dqk = da * decay
                dkb = _dot(dkk, k) + dkbe * egc
                dqs = _dot(dqk, k) + dg * egc
                dk = (_dot_tn(jnp.concatenate([dkk, dqk], axis=0), jnp.concatenate([kb, qs], axis=0))
                      + dkd * ekd + dkb * beta)
                dkd_kd = jnp.sum(dkd * kd, axis=1, keepdims=True)
                dgc = (jnp.sum(dlogd, axis=1, keepdims=True) + jnp.sum(dg * qs, axis=1, keepdims=True) * egc
                       + jnp.sum(dkbe * (kb * egc), axis=1, keepdims=True) - dkd_kd)
                dbeta = jnp.sum(dkb * k, axis=1, keepdims=True) + jnp.sum(dvb * v, axis=1, keepdims=True)
                dqkv_ref[0, hh, rows, :] = dqs * Q_SCALE
                dqkv_ref[1, hh, rows, :] = dk
                dqkv_ref[2, hh, rows, :] = dvb * beta
                dcol_ref[rows, :] += _to_lane(dbeta, head) + _to_lane(dgc, HEADS + head)
                dtot_ref[rows, :] += _to_lane(dkd_kd + dgl_first, HEADS + head)
                drow_ref[pl.ds(head, 1), rows] = -jnp.sum(dlogd, axis=0, keepdims=True)
            return 0

        lax.fori_loop(0, nsc, back, 0)

    whole = pl.BlockSpec((t, LANE), lambda h: (0, 0))
    rowspec = pl.BlockSpec((HEADS, t), lambda h: (0, 0))
    sq = pltpu.VMEM((hp_n, nc, DH, DH), F32)
    per_head = pltpu.VMEM((hp_n, t, DH), F32)
    return pl.pallas_call(
        body, name="gdn_bwd", grid=(HEADS // hp_n,),
        in_specs=[pl.BlockSpec((3, hp_n, t, DH), lambda h: (0, h, 0, 0)), whole, whole, whole,
                  pl.BlockSpec((HEADS, t), lambda h: (1, 0)),
                  pl.BlockSpec((hp_n, t, SC), lambda h: (h, 0, 0)), pl.BlockSpec((hp_n, nc, DH, DH), lambda h: (h, 0, 0, 0)),
                  pl.BlockSpec((hp_n, t, DH), lambda h: (h, 0, 0))],
        out_specs=[pl.BlockSpec((3, hp_n, t, DH), lambda h: (0, h, 0, 0)), whole, whole, rowspec],
        out_shape=[jax.ShapeDtypeStruct((3, HEADS, t, DH), F32), jax.ShapeDtypeStruct((t, LANE), F32),
                   jax.ShapeDtypeStruct((t, LANE), F32), jax.ShapeDtypeStruct((HEADS, t), F32)],
        scratch_shapes=[per_head, per_head, sq, sq, sq],
        compiler_params=_params(("arbitrary",)),
    )(qkv, gates, run, tot, run_t, inv, states, do)


FOX_HP = 2


def _wide_t(a):
    r = a.shape[0]
    return jnp.concatenate([a, jnp.zeros((r, LANE - DH), F32)], axis=1).T[:DH]


def _tall_t(a):
    r = a.shape[1]
    return jnp.concatenate([a, jnp.zeros((LANE - DH, r), F32)], axis=0).T[:, :DH]


def _key_side_f(run_blk, head, qb):
    col = jnp.broadcast_to(_lane_col(run_blk, 2 * HEADS + head), (run_blk.shape[0], LANE))
    return jnp.concatenate([col] * (qb // LANE), axis=1)


def _diag_mask(qb):
    return lax.broadcasted_iota(jnp.int32, (qb, qb), 0) <= lax.broadcasted_iota(jnp.int32, (qb, qb), 1)


def _fox_fwd(qkv, run, run_t):
    t = qkv.shape[2]
    qb = min(QB, t)
    nq = t // qb
    hp_n = FOX_HP

    def body(q_ref, k_ref, v_ref, run_ref, runt_ref, o_ref, lse_ref, kb_s, vt_s):
        hp = pl.program_id(0)
        i = pl.program_id(1)

        @pl.when(i == 0)
        def _():
            for hh in range(hp_n):
                kb_s[hh] = k_ref[0, hh].astype(MXU)
                for b in range(nq):
                    rows = slice(b * qb, (b + 1) * qb)
                    vt_s[hh, :, rows] = _wide_t(v_ref[0, hh, rows, :]).astype(MXU)

        qrows = pl.ds(pl.multiple_of(i * qb, qb), qb)
        qs = [(q_ref[0, hh] * Q_SCALE).astype(MXU) for hh in range(hp_n)]
        fq = [runt_ref[pl.ds(hp * hp_n + hh, 1), qrows] for hh in range(hp_n)]

        def block_rows(j):
            return pl.ds(pl.multiple_of(j * qb, qb), qb)

        def scores(j):
            return tuple(_dot_nt(kb_s[hh, block_rows(j), :], qs[hh]) for hh in range(hp_n))

        def absorb(j, raw, state, diagonal):
            rows = block_rows(j)
            run_blk = run_ref[rows, :]
            stats, pv = [], []
            for hh in range(hp_n):
                m, l, _ = state[hh]
                st = raw[hh] + (fq[hh] - _key_side_f(run_blk, hp * hp_n + hh, qb))
                if diagonal:
                    st = jnp.where(_diag_mask(qb), st, -1e30)
                m_new = jnp.maximum(m, jnp.max(st, axis=0, keepdims=True))
                p = jnp.exp(st - m_new)
                alpha = jnp.exp(m - m_new)
                stats.append((m_new, alpha * l + jnp.sum(p, axis=0, keepdims=True), alpha))
                pv.append(_dot(vt_s[hh, :, rows], p))
            return tuple((stats[hh][0], stats[hh][1], stats[hh][2] * state[hh][2] + pv[hh]) for hh in range(hp_n))

        def kstep(j, carry):
            state, raw = carry
            ahead = scores(j + 1)
            return absorb(j, raw, state, False), ahead

        init = tuple((jnp.full((1, qb), -1e30, F32), jnp.zeros((1, qb), F32), jnp.zeros((DH, qb), F32))
                     for _ in range(hp_n))
        state, raw = lax.fori_loop(0, i, kstep, (init, scores(0)))
        state = absorb(i, raw, state, True)
        for hh in range(hp_n):
            m, l, acc = state[hh]
            o_ref[hh] = _tall_t(acc / l)
            lse_ref[pl.ds(hp * hp_n + hh, 1), qrows] = m + jnp.log(l)

    return pl.pallas_call(
        body, name="fox_fwd", grid=(HEADS // hp_n, nq),
        in_specs=[pl.BlockSpec((1, hp_n, qb, DH), lambda h, i: (0, h, i, 0)),
                  pl.BlockSpec((1, hp_n, t, DH), lambda h, i: (1, h, 0, 0)),
                  pl.BlockSpec((1, hp_n, t, DH), lambda h, i: (2, h, 0, 0)),
                  pl.BlockSpec((t, LANE), lambda h, i: (0, 0)),
                  pl.BlockSpec((HEADS, t), lambda h, i: (2, 0))],
        out_specs=[pl.BlockSpec((hp_n, qb, DH), lambda h, i: (h, i, 0)), pl.BlockSpec((HEADS, t), lambda h, i: (0, 0))],
        out_shape=[jax.ShapeDtypeStruct((HEADS, t, DH), F32), jax.ShapeDtypeStruct((HEADS, t), F32)],
        scratch_shapes=[pltpu.VMEM((hp_n, t, DH), MXU), pltpu.VMEM((hp_n, DH, t), MXU)],
        compiler_params=_params(("arbitrary", "arbitrary")),
    )(qkv, qkv, qkv, run, run_t)


def _fox_bwd(qkv, run, run_t, o, lse, do):
    t = qkv.shape[2]
    qb = min(QB, t)
    nq = t // qb
    hp_n = FOX_HP

    def body(q_ref, k_ref, v_ref, run_ref, runt_ref, o_ref, lse_ref, do_ref, dqkv_ref, dcol_ref, drow_ref, dqt_s):
        hp = pl.program_id(0)
        j = pl.program_id(1)

        @pl.when(j == 0)
        def _():
            dqt_s[...] = jnp.zeros_like(dqt_s)

        @pl.when((j == 0) & (hp == 0))
        def _():
            dcol_ref[...] = jnp.zeros_like(dcol_ref)
            drow_ref[...] = jnp.zeros_like(drow_ref)

        krows = pl.ds(pl.multiple_of(j * qb, qb), qb)
        run_blk = run_ref[krows, :]
        ones8 = jnp.ones((8, DH), MXU)
        kb, kt, vb, fk = [], [], [], []
        for hh in range(hp_n):
            kf = k_ref[0, hh]
            kb.append(kf.astype(MXU))
            kt.append(_wide_t(kf).astype(MXU))
            vb.append(v_ref[0, hh].astype(MXU))
            fk.append(_key_side_f(run_blk, hp * hp_n + hh, qb))

        def block_rows(i):
            return pl.ds(pl.multiple_of(i * qb, qb), qb)

        def products(i):
            rows = block_rows(i)
            out = []
            for hh in range(hp_n):
                dout = do_ref[hh, rows, :]
                x = dout * o_ref[hh, rows, :]
                x_hi = x.astype(MXU)
                out.append((_dot_nt(kb[hh], q_ref[0, hh, rows, :] * Q_SCALE), _dot_nt(vb[hh], dout),
                            (_dot_nt(ones8, x_hi) + _dot_nt(ones8, x - x_hi.astype(F32)))[0:1, :]))
            return tuple(out)

        def absorb(i, prods, acc, diagonal):
            rows = block_rows(i)
            pieces = []
            for hh in range(hp_n):
                head = hp * hp_n + hh
                raw, dpt, drow = prods[hh]
                off = runt_ref[pl.ds(head, 1), rows] - lse_ref[pl.ds(head, 1), rows]
                st = raw + (off - fk[hh])
                if diagonal:
                    st = jnp.where(_diag_mask(qb), st, -1e30)
                pt = jnp.exp(st)
                dst = pt * (dpt - drow)
                drow_ref[pl.ds(head, 1), rows] += jnp.sum(dst, axis=0, keepdims=True)
                folded = dst[:, 0:LANE]
                for c in range(1, qb // LANE):
                    folded = folded + dst[:, c * LANE:(c + 1) * LANE]
                pieces.append((_dot(dst, q_ref[0, hh, rows, :] * Q_SCALE), _dot(pt, do_ref[hh, rows, :]),
                               _dot(kt[hh], dst), folded))
            out = []
            for hh in range(hp_n):
                dqt_s[hh, :, rows] += pieces[hh][2]
                out.append((acc[hh][0] + pieces[hh][0], acc[hh][1] + pieces[hh][1], acc[hh][2] + pieces[hh][3]))
            return tuple(out)

        def qstep(i, carry):
            acc, prods = carry
            ahead = products(jnp.minimum(i + 1, nq - 1))
            return absorb(i, prods, acc, False), ahead

        init = tuple((jnp.zeros((qb, DH), F32), jnp.zeros((qb, DH), F32), jnp.zeros((qb, LANE), F32))
                     for _ in range(hp_n))
        first = products(j)
        ahead = products(jnp.minimum(j + 1, nq - 1))
        acc = absorb(j, first, init, True)
        acc, _ = lax.fori_loop(j + 1, nq, qstep, (acc, ahead))
        for hh in range(hp_n):
            dk, dv, ds_sum = acc[hh]
            dqkv_ref[1, hh, krows, :] = dk
            dqkv_ref[2, hh, krows, :] = dv
            dcol_ref[krows, :] += _to_lane(-jnp.sum(ds_sum, axis=1, keepdims=True), 2 * HEADS + hp * hp_n + hh)

        @pl.when(j == nq - 1)
        def _():
            for hh in range(hp_n):
                for b in range(nq):
                    rows = slice(b * qb, (b + 1) * qb)
                    dqkv_ref[0, hh, rows, :] = _tall_t(dqt_s[hh, :, rows]) * Q_SCALE

    full = pl.BlockSpec((hp_n, t, DH), lambda h, j: (h, 0, 0))
    rows8 = pl.BlockSpec((HEADS, t), lambda h, j: (0, 0))
    return pl.pallas_call(
        body, name="fox_bwd", grid=(HEADS // hp_n, nq),
        in_specs=[pl.BlockSpec((1, hp_n, t, DH), lambda h, j: (0, h, 0, 0)),
                  pl.BlockSpec((1, hp_n, qb, DH), lambda h, j: (1, h, j, 0)),
                  pl.BlockSpec((1, hp_n, qb, DH), lambda h, j: (2, h, j, 0)),
                  pl.BlockSpec((t, LANE), lambda h, j: (0, 0)), pl.BlockSpec((HEADS, t), lambda h, j: (2, 0)),
                  full, rows8, full],
        out_specs=[pl.BlockSpec((3, hp_n, t, DH), lambda h, j: (0, h, 0, 0)),
                   pl.BlockSpec((t, LANE), lambda h, j: (0, 0)), rows8],
        out_shape=[jax.ShapeDtypeStruct((3, HEADS, t, DH), F32), jax.ShapeDtypeStruct((t, LANE), F32),
                   jax.ShapeDtypeStruct((HEADS, t), F32)],
        scratch_shapes=[pltpu.VMEM((hp_n, DH, t), F32)],
        compiler_params=_params(("arbitrary", "arbitrary")),
    )(qkv, qkv, qkv, run, run_t, o, lse, do)


Z_COL0 = 3 * WIDTH // LANE
FGATE_COL0 = 7 * WIDTH // LANE


def _gdn_post(o, pm, onw):
    t = pm.shape[0]

    def body(o_ref, z_ref, w_ref, m_ref, mt_ref):
        z = z_ref[...]
        sz = z * _sigmoid(z)
        halves = []
        for hh in range(2):
            ov = o_ref[hh]
            n = ov * lax.rsqrt(jnp.mean(ov * ov, axis=-1, keepdims=True) + EPS) * w_ref[...]
            halves.append(n * sz[:, hh * DH:(hh + 1) * DH])
        m = jnp.concatenate(halves, axis=1)
        m_ref[...] = m.astype(m_ref.dtype)
        mt_ref[...] = m.T.astype(mt_ref.dtype)

    return pl.pallas_call(
        body, name="gdn_post", grid=(WIDTH // LANE,),
        in_specs=[pl.BlockSpec((2, t, DH), lambda j: (j, 0, 0)), pl.BlockSpec((t, LANE), lambda j: (0, Z_COL0 + j)),
                  pl.BlockSpec((1, DH), lambda j: (0, 0))],
        out_specs=[pl.BlockSpec((t, LANE), lambda j: (0, j)), pl.BlockSpec((LANE, t), lambda j: (j, 0))],
        out_shape=[jax.ShapeDtypeStruct((t, WIDTH), MXU), jax.ShapeDtypeStruct((WIDTH, t), MXU)],
        compiler_params=_params(("arbitrary",)),
    )(o, pm, onw)


def _gdn_post_bwd(o, pm, onw, dmix):
    t = pm.shape[0]

    def body(o_ref, z_ref, w_ref, dm_ref, do_ref, dz_ref, dw_ref):
        @pl.when(pl.program_id(0) == 0)
        def _():
            dw_ref[...] = jnp.zeros_like(dw_ref)

        z = z_ref[...]
        sg = _sigmoid(z)
        sz = z * sg
        dsz = sg * (1.0 + z * (1.0 - sg))
        dm = dm_ref[...]
        for hh in range(2):
            cols = slice(hh * DH, (hh + 1) * DH)
            ov = o_ref[hh]
            r = lax.rsqrt(jnp.mean(ov * ov, axis=-1, keepdims=True) + EPS)
            xn = ov * r
            dmh = dm[:, cols]
            dn = dmh * sz[:, cols]
            dz_ref[:, cols] = dmh * (xn * w_ref[...]) * dsz[:, cols]
            dw_ref[...] += jnp.sum(dn * xn, axis=0, keepdims=True)
            g = dn * w_ref[...]
            do_ref[hh] = r * (g - xn * jnp.mean(g * xn, axis=-1, keepdims=True))

    return pl.pallas_call(
        body, name="gdn_post_bwd", grid=(WIDTH // LANE,),
        in_specs=[pl.BlockSpec((2, t, DH), lambda j: (j, 0, 0)), pl.BlockSpec((t, LANE), lambda j: (0, Z_COL0 + j)),
                  pl.BlockSpec((1, DH), lambda j: (0, 0)), pl.BlockSpec((t, LANE), lambda j: (0, j))],
        out_specs=[pl.BlockSpec((2, t, DH), lambda j: (j, 0, 0)), pl.BlockSpec((t, LANE), lambda j: (0, j)),
                   pl.BlockSpec((1, DH), lambda j: (0, 0))],
        out_shape=[jax.ShapeDtypeStruct((HEADS, t, DH), F32), jax.ShapeDtypeStruct((t, WIDTH), F32),
                   jax.ShapeDtypeStruct((1, DH), F32)],
        compiler_params=_params(("arbitrary",)),
    )(o, pm, onw, dmix)


def _fox_post(o, pm):
    t = pm.shape[0]

    def body(o_ref, g_ref, m_ref, mt_ref):
        m = jnp.concatenate([o_ref[0], o_ref[1]], axis=1) * _sigmoid(g_ref[...])
        m_ref[...] = m.astype(m_ref.dtype)
        mt_ref[...] = m.T.astype(mt_ref.dtype)

    return pl.pallas_call(
        body, name="fox_post", grid=(WIDTH // LANE,),
        in_specs=[pl.BlockSpec((2, t, DH), lambda j: (j, 0, 0)), pl.BlockSpec((t, LANE), lambda j: (0, FGATE_COL0 + j))],
        out_specs=[pl.BlockSpec((t, LANE), lambda j: (0, j)), pl.BlockSpec((LANE, t), lambda j: (j, 0))],
        out_shape=[jax.ShapeDtypeStruct((t, WIDTH), MXU), jax.ShapeDtypeStruct((WIDTH, t), MXU)],
        compiler_params=_params(("arbitrary",)),
    )(o, pm)


def _fox_post_bwd(o, pm, dmix):
    t = pm.shape[0]

    def body(o_ref, g_ref, dm_ref, do_ref, dg_ref):
        sg = _sigmoid(g_ref[...])
        dm = dm_ref[...]
        for hh in range(2):
            cols = slice(hh * DH, (hh + 1) * DH)
            do_ref[hh] = dm[:, cols] * sg[:, cols]
            dg_ref[:, cols] = dm[:, cols] * o_ref[hh] * (sg * (1.0 - sg))[:, cols]

    return pl.pallas_call(
        body, name="fox_post_bwd", grid=(WIDTH // LANE,),
        in_specs=[pl.BlockSpec((2, t, DH), lambda j: (j, 0, 0)), pl.BlockSpec((t, LANE), lambda j: (0, FGATE_COL0 + j)),
                  pl.BlockSpec((t, LANE), lambda j: (0, j))],
        out_specs=[pl.BlockSpec((2, t, DH), lambda j: (j, 0, 0)), pl.BlockSpec((t, LANE), lambda j: (0, j))],
        out_shape=[jax.ShapeDtypeStruct((HEADS, t, DH), F32), jax.ShapeDtypeStruct((t, WIDTH), F32)],
        compiler_params=_params(("arbitrary",)),
    )(o, pm, dmix)


def _tail(x, mixg, mixf, tgt, wo, n2w, wg, wu, wd, fw):
    t, d = x.shape
    dff = wg.shape[1]
    tb = min(TB, t)

    def body(x_ref, mg_ref, mf_ref, t_ref, wo_ref, n2_ref, wg_ref, wu_ref, wd_ref, fw_ref,
             h2_ref, act_ref, dgate_ref, dup_ref, dx3_ref, dx2_ref, dmg_ref, dmf_ref, dn2_ref, dfw_ref, loss_ref):
        @pl.when(pl.program_id(0) == 0)
        def _():
            dn2_ref[...] = jnp.zeros_like(dn2_ref)
            dfw_ref[...] = jnp.zeros_like(dfw_ref)
            loss_ref[...] = jnp.zeros_like(loss_ref)

        x2 = x_ref[...] + _dot(mg_ref[...], wo_ref[0:WIDTH, :]) + _dot(mf_ref[...], wo_ref[WIDTH:2 * WIDTH, :])
        r2 = lax.rsqrt(jnp.mean(x2 * x2, axis=-1, keepdims=True) + EPS)
        xn2 = x2 * r2
        h2_32 = xn2 * n2_ref[...]
        h2 = h2_32.astype(MXU)
        h2_ref[...] = h2_32.T.astype(MXU)
        gate = _dot(h2, wg_ref[...])
        up = _dot(h2, wu_ref[...])
        sg = _sigmoid(gate)
        sl = gate * sg
        act_32 = sl * up
        act = act_32.astype(MXU)
        act_ref[...] = act_32.T.astype(MXU)
        x3 = x2 + _dot(act, wd_ref[...])
        r3 = lax.rsqrt(jnp.mean(x3 * x3, axis=-1, keepdims=True) + EPS)
        xn3 = x3 * r3
        err = xn3 * fw_ref[...] - t_ref[...]
        loss_ref[...] += 0.5 * jnp.sum(jnp.mean(err * err, axis=-1, keepdims=True), axis=0, keepdims=True)
        dy = err * (1.0 / d)
        dfw_ref[...] += jnp.sum(dy * xn3, axis=0, keepdims=True)
        g3 = dy * fw_ref[...]
        dx3 = r3 * (g3 - xn3 * jnp.mean(g3 * xn3, axis=-1, keepdims=True))
        dx3_ref[...] = dx3.astype(MXU)
        dact = _dot_nt(dx3, wd_ref[...])
        dgate = (dact * up * (sg * (1.0 + gate * (1.0 - sg)))).astype(MXU)
        dup = (dact * sl).astype(MXU)
        dgate_ref[...] = dgate
        dup_ref[...] = dup
        dh2 = _dot_nt(dgate, wg_ref[...]) + _dot_nt(dup, wu_ref[...])
        dn2_ref[...] += jnp.sum(dh2 * xn2, axis=0, keepdims=True)
        g2 = dh2 * n2_ref[...]
        dx2 = dx3 + r2 * (g2 - xn2 * jnp.mean(g2 * xn2, axis=-1, keepdims=True))
        dx2_ref[...] = dx2
        dmg_ref[...] = _dot_nt(dx2, wo_ref[0:WIDTH, :])
        dmf_ref[...] = _dot_nt(dx2, wo_ref[WIDTH:2 * WIDTH, :])

    def tok(n):
        return pl.BlockSpec((tb, n), lambda i: (i, 0))

    def tok_t(n):
        return pl.BlockSpec((n, tb), lambda i: (0, i))

    acc = pl.BlockSpec((1, d), lambda i: (0, 0))
    sds = jax.ShapeDtypeStruct
    return pl.pallas_call(
        body, name="tail", grid=(t // tb,),
        in_specs=[tok(d), tok(WIDTH), tok(WIDTH), tok(d), _resident(wo.shape), _resident((1, d)),
                  _resident(wg.shape), _resident(wu.shape), _resident(wd.shape), _resident((1, d))],
        out_specs=[tok_t(d), tok_t(dff), tok(dff), tok(dff), tok(d), tok(d), tok(WIDTH), tok(WIDTH), acc, acc,
                   pl.BlockSpec((1, 1), lambda i: (0, 0))],
        out_shape=[sds((d, t), MXU), sds((dff, t), MXU), sds((t, dff), MXU), sds((t, dff), MXU), sds((t, d), MXU),
                   sds((t, d), F32), sds((t, WIDTH), F32), sds((t, WIDTH), F32), sds((1, d), F32), sds((1, d), F32),
                   sds((1, 1), F32)],
        compiler_params=_params(("arbitrary",)),
    )(x, mixg, mixf, tgt, wo, n2w, wg, wu, wd, fw)


def _wgrad(at, b, name):
    m, t = at.shape
    n = b.shape[1]
    bn = 256 if n % 256 == 0 else LANE

    def body(a_ref, b_ref, o_ref):
        o_ref[...] = _dot(a_ref[...], b_ref[...]).astype(o_ref.dtype)

    return pl.pallas_call(
        body, name=name, grid=(n // bn,),
        in_specs=[_resident((m, t)), pl.BlockSpec((t, bn), lambda j: (0, j))],
        out_specs=pl.BlockSpec((m, bn), lambda j: (0, j)),
        out_shape=jax.ShapeDtypeStruct((m, n), WIRE),
        compiler_params=_params(("arbitrary",)),
    )(at, b)


def _split_w_in(w_in):
    a = 4 * WIDTH
    b = a + 2 * HEADS
    c = b + 4 * WIDTH
    main = jnp.concatenate([w_in[:, :a], w_in[:, b:c]], axis=1)
    small = jnp.concatenate([w_in[:, a:b], w_in[:, c:], jnp.zeros((w_in.shape[0], LANE - 3 * HEADS), w_in.dtype)], axis=1)
    return main, small


def _merge_dw_in(d_gdn, d_z, d_fox, d_fg, d_small):
    return jnp.concatenate([d_gdn, d_z, d_small[:, :2 * HEADS], d_fox, d_fg, d_small[:, 2 * HEADS:3 * HEADS]], axis=1)


def _lanes(*pieces):
    v = jnp.concatenate([p.reshape(-1).astype(F32) for p in pieces])
    return jnp.pad(v, (0, LANE - v.shape[0])).reshape(1, LANE)


def _vector_params(p):
    d = p["norm1_w"].size
    gparams = jnp.concatenate([_lanes(jnp.zeros(HEADS), p["gdn_dt_bias"], p["fox_f_bias"]),
                               _lanes(jnp.zeros(HEADS), p["gdn_A_log"]), jnp.zeros((6, LANE), F32)])
    fox_nw = jnp.stack([jnp.tile(p["fox_q_norm_w"].reshape(-1), 2), jnp.tile(p["fox_k_norm_w"].reshape(-1), 2),
                        jnp.ones((LANE,), F32)])
    return dict(n1w=p["norm1_w"].reshape(1, d), n2w=p["norm2_w"].reshape(1, d), fw=p["final_norm_w"].reshape(1, d),
                onw=p["gdn_out_norm_w"].reshape(1, DH), gparams=gparams, fox_nw=fox_nw)


def _mixer_forward(x, vp, w_main, w_small, conv_w):
    h1, pm, ps = _inproj(x, vp["n1w"], w_main, w_small)
    gates, run, tot, run_t = _gates(ps, vp["gparams"])
    gqkv = _gdn_prep(pm, conv_w)
    o_gdn, states, inv = _gdn_fwd(gqkv, gates, run, tot, run_t)
    mixg, mixg_t = _gdn_post(o_gdn, pm, vp["onw"])
    fqkv = _fox_prep(pm, vp["fox_nw"])
    o_fox, lse = _fox_fwd(fqkv, run, run_t)
    mixf, mixf_t = _fox_post(o_fox, pm)
    return dict(h1=h1, pm=pm, ps=ps, gates=gates, run=run, tot=tot, run_t=run_t, gqkv=gqkv, o_gdn=o_gdn, states=states,
                inv=inv, mixg=mixg, mixg_t=mixg_t, fqkv=fqkv, o_fox=o_fox, lse=lse, mixf=mixf, mixf_t=mixf_t)


def _mixer_backward(x, vp, w_main, w_small, conv_w, f, dx2, dmixg, dmixf, onw):
    pm = f["pm"]
    do_gdn, dz, donw = _gdn_post_bwd(f["o_gdn"], pm, onw, dmixg)
    dgqkv, dcol_g, dtot_g, drow_g = _gdn_bwd(f["gqkv"], f["gates"], f["run"], f["tot"], f["run_t"], f["inv"], f["states"], do_gdn)
    dgdn, dconv = _gdn_prep_bwd(pm, conv_w, dgqkv)
    do_fox, dfg = _fox_post_bwd(f["o_fox"], pm, dmixf)
    dfqkv, dcol_f, drow_f = _fox_bwd(f["fqkv"], f["run"], f["run_t"], f["o_fox"], f["lse"], do_fox)
    dfox, dfnw = _fox_prep_bwd(pm, vp["fox_nw"], dfqkv)
    dps, gsum = _gates_bwd(f["ps"], vp["gparams"], dcol_g, dtot_g, dcol_f, drow_g, drow_f)
    grad_x, dn1w = _inproj_bwd(x, vp["n1w"], dx2, dgdn, dz, dfox, dfg, dps, w_main, w_small)
    h1 = f["h1"]
    dw_in = _merge_dw_in(_wgrad(h1, dgdn, "dw_in_gdn"), _wgrad(h1, dz, "dw_in_z"), _wgrad(h1, dfox, "dw_in_fox"),
                         _wgrad(h1, dfg, "dw_in_fgate"), _wgrad(h1, dps, "dw_in_small"))
    return grad_x, dw_in, dconv, dict(dn1w=dn1w, gsum=gsum, donw=donw, dfnw=dfnw)


VECTORS = ("norm1_w", "norm2_w", "final_norm_w", "gdn_A_log", "gdn_dt_bias", "gdn_out_norm_w", "fox_f_bias",
           "fox_q_norm_w", "fox_k_norm_w")
VEC_ROWS = 16
LOSS_ROW = len(VECTORS)


def _pack_vectors(dn1w, dn2w, dfw, gsum, donw, dfnw, loss):
    d = dn1w.shape[1]

    def body(n1_ref, n2_ref, fw_ref, gs_ref, on_ref, fn_ref, loss_ref, o_ref):
        o_ref[...] = jnp.zeros_like(o_ref)
        o_ref[0:1, :] = n1_ref[...]
        o_ref[1:2, :] = n2_ref[...]
        o_ref[2:3, :] = fw_ref[...]
        o_ref[3:4, 0:HEADS] = gs_ref[0:1, 0:HEADS]
        o_ref[4:5, 0:HEADS] = gs_ref[1:2, 0:HEADS]
        o_ref[5:6, 0:DH] = on_ref[...]
        o_ref[6:7, 0:HEADS] = gs_ref[2:3, 0:HEADS]
        for kind in range(2):
            v = fn_ref[kind, 0]
            for j in range(1, fn_ref.shape[1]):
                v = v + fn_ref[kind, j]
            o_ref[7 + kind:8 + kind, 0:DH] = v[:, :DH] + v[:, DH:]
        o_ref[LOSS_ROW:LOSS_ROW + 1, 0:1] = loss_ref[...]

    return pl.pallas_call(body, name="pack_vectors", out_shape=jax.ShapeDtypeStruct((VEC_ROWS, d), F32),
                          compiler_params=_params())(dn1w, dn2w, dfw, gsum, donw, dfnw, loss)


def _late_grads(f, h2t, actt, dgate, dup, dx3, dx2):
    return {"w_out": jnp.concatenate([_wgrad(f["mixg_t"], dx2, "dw_out_gdn"), _wgrad(f["mixf_t"], dx2, "dw_out_fox")], axis=0),
            "w_ffn_gate": _wgrad(h2t, dgate, "dw_gate"), "w_ffn_up": _wgrad(h2t, dup, "dw_up"),
            "w_ffn_down": _wgrad(actt, dx3, "dw_down")}


def _local_step(x, tgt, p, w_in, conv_w, wo, wg, wu, wd):
    vp = _vector_params(p)
    w_main, w_small = _split_w_in(w_in)
    f = _mixer_forward(x, vp, w_main, w_small, conv_w)
    (h2t, actt, dgate, dup, dx3, dx2, dmixg, dmixf, dn2w, dfw, loss) = _tail(
        x, f["mixg"], f["mixf"], tgt, wo, vp["n2w"], wg, wu, wd, vp["fw"])
    grad_x, dw_in, dconv, small = _mixer_backward(x, vp, w_main, w_small, conv_w, f, dx2, dmixg, dmixf, vp["onw"])
    grads = {"w_in": dw_in, "gdn_conv_w": dconv, **_late_grads(f, h2t, actt, dgate, dup, dx3, dx2)}
    vec = _pack_vectors(small["dn1w"], dn2w, dfw, small["gsum"], small["donw"], small["dfnw"], loss)
    return grad_x, grads, vec


def _my_place():
    return lax.axis_index("x"), lax.axis_index("y"), lax.axis_index("c")


def _peers():
    x, y, c = _my_place()
    peers = []
    for k in range(1, N_DEV):
        px = 1 - x if k & 4 else x
        py = 1 - y if k & 2 else y
        pc = 1 - c if k & 1 else c
        peers.append(((px, py, pc), 4 * px + 2 * py + pc))
    return 4 * x + 2 * y + c, peers


def _spread_copies(kind, srcs, lands, send_sems, recv_sems):
    me, peers = _peers()
    kinds = [kind] * len(srcs) if isinstance(kind, str) else kind
    remote, local = [], []
    for i, (kd, src, land) in enumerate(zip(kinds, srcs, lands)):
        for k, (dev, idx) in enumerate(peers):
            remote.append(pltpu.make_async_remote_copy(
                src_ref=src if kd == "gather" else src.at[idx], dst_ref=land.at[me],
                send_sem=send_sems.at[i * (N_DEV - 1) + k], recv_sem=recv_sems.at[i * (N_DEV - 1) + k],
                device_id=dev, device_id_type=MESH))
        local.append((src if kd == "gather" else src.at[me], land.at[me]))
    return remote, local


def _land_shape(kind, a):
    return (N_DEV,) + a.shape if kind == "gather" else a.shape


HBM = pl.BlockSpec(memory_space=pltpu.HBM)
SEM = pl.BlockSpec(memory_space=pltpu.SEMAPHORE)


def _hbm(a):
    return pltpu.with_memory_space_constraint(a, pltpu.HBM)


def _spread_start(arrays, kind, name):
    n = len(arrays)

    def body(*refs):
        srcs, lands = refs[:n], refs[n:2 * n]
        send_sems, recv_sems = refs[2 * n], refs[2 * n + 1]
        token = refs[4 * n + 2]
        local_sems = refs[4 * n + 3]
        remote, local = _spread_copies(kind, srcs, lands, send_sems, recv_sems)
        for cp in remote:
            cp.start()
        copies = [pltpu.make_async_copy(s, d, local_sems.at[i]) for i, (s, d) in enumerate(local)]
        for cp in copies:
            cp.start()
        for cp in copies:
            cp.wait()
        token[...] = jnp.zeros_like(token)

    sems = (pltpu.SemaphoreType.DMA((n * (N_DEV - 1),)),) * 2
    kinds = [kind] * n if isinstance(kind, str) else kind
    lands = [lax.empty(_land_shape(kd, a), a.dtype) for kd, a in zip(kinds, arrays)]
    out = pl.pallas_call(
        body, name=name,
        out_shape=sems + tuple(pltpu.HBM(a.shape, a.dtype) for a in list(arrays) + lands)
        + (jax.ShapeDtypeStruct((8, LANE), F32),),
        in_specs=[HBM] * (2 * n), out_specs=tuple([SEM] * 2 + [HBM] * (2 * n) + [pl.BlockSpec(memory_space=pltpu.VMEM)]),
        input_output_aliases={j: 2 + j for j in range(2 * n)},
        scratch_shapes=[pltpu.SemaphoreType.DMA((n,))],
        compiler_params=pltpu.CompilerParams(has_side_effects=pltpu.SideEffectType.DATAFLOW_SIDE_EFFECTING),
    )(*[_hbm(a) for a in arrays], *[_hbm(a) for a in lands])
    return (list(out[2:2 + n]), list(out[2 + n:2 + 2 * n]), out[0], out[1]), out[-1]


def _spread_wait(state, kind, after, name):
    srcs, lands, send_sems, recv_sems = state
    n = len(srcs)

    def body(*refs):
        remote, _ = _spread_copies(kind, refs[:n], refs[n:2 * n], refs[2 * n], refs[2 * n + 1])
        for cp in remote:
            cp.wait_send()
        for cp in remote:
            cp.wait_recv()

    out = pl.pallas_call(
        body, name=name,
        out_shape=tuple(pltpu.HBM(a.shape, a.dtype) for a in srcs + lands),
        in_specs=[HBM] * (2 * n) + [SEM, SEM, pl.BlockSpec(memory_space=pl.ANY)], out_specs=tuple([HBM] * (2 * n)),
        input_output_aliases={j: j for j in range(2 * n)},
        compiler_params=pltpu.CompilerParams(has_side_effects=pltpu.SideEffectType.DATAFLOW_SIDE_EFFECTING),
    )(*srcs, *lands, send_sems, recv_sems, after)
    return list(out[n:])


ADAM_ROWS = 128


def _adam_math(g, w, m, v):
    nm = ADAM_B1 * m + (1.0 - ADAM_B1) * g
    nv = ADAM_B2 * v + (1.0 - ADAM_B2) * (g * g)
    m_hat = nm / (1.0 - ADAM_B1 ** ADAM_STEP)
    v_hat = nv / (1.0 - ADAM_B2 ** ADAM_STEP)
    return -ADAM_LR * (m_hat / (jnp.sqrt(v_hat) + ADAM_EPS) + ADAM_WD * w), nm, nv


def _sum_parts(p_ref):
    g = p_ref[0].astype(F32)
    for s in range(1, N_DEV):
        g = g + p_ref[s].astype(F32)
    return g


def _adam_matrix(parts, w, m, v, name):
    _, r, c = w.shape
    rb = ADAM_ROWS if r % ADAM_ROWS == 0 else r

    def body(p_ref, w_ref, m_ref, v_ref, g_ref, d_ref, nm_ref, nv_ref):
        g = _sum_parts(p_ref)
        g_ref[0] = g
        d_ref[0], nm_ref[0], nv_ref[0] = _adam_math(g, w_ref[0], m_ref[0], v_ref[0])

    blk = pl.BlockSpec((1, rb, c), lambda i: (0, i, 0))
    return pl.pallas_call(
        body, name=name, grid=(r // rb,),
        in_specs=[pl.BlockSpec((N_DEV, rb, c), lambda i: (0, i, 0)), blk, blk, blk],
        out_specs=[blk] * 4, out_shape=[jax.ShapeDtypeStruct(w.shape, F32)] * 4,
        compiler_params=_params(("arbitrary",)),
    )(parts, w, m, v)


def _adam_vectors(parts, ws, ms, vs):
    nv = len(ws)

    def body(*refs):
        p_ref = refs[0]
        w_refs, m_refs, v_refs = refs[1:1 + nv], refs[1 + nv:1 + 2 * nv], refs[1 + 2 * nv:1 + 3 * nv]
        outs = refs[1 + 3 * nv:]
        g_all = _sum_parts(p_ref)
        for i in range(nv):
            n = w_refs[i].shape[1]
            g = g_all[i:i + 1, 0:n]
            d, nm, nvv = _adam_math(g, w_refs[i][...], m_refs[i][...], v_refs[i][...])
            outs[i][...] = g
            outs[nv + i][...] = d
            outs[2 * nv + i][...] = nm
            outs[3 * nv + i][...] = nvv
        outs[4 * nv][...] = g_all[LOSS_ROW:LOSS_ROW + 1, 0:1]

    shapes = [jax.ShapeDtypeStruct(a.shape, F32) for a in ws]
    out = pl.pallas_call(body, name="adam_vectors", out_shape=shapes * 4 + [jax.ShapeDtypeStruct((1, 1), F32)],
                         compiler_params=_params())(parts, *ws, *ms, *vs)
    return out[:nv], out[nv:2 * nv], out[2 * nv:3 * nv], out[3 * nv:4 * nv], out[4 * nv]


MATRICES = (("w_in", 1), ("gdn_conv_w", 1), ("w_out", 0), ("w_ffn_gate", 1), ("w_ffn_up", 1), ("w_ffn_down", 0))
WEIGHTS = ("norm1_w", "w_in", "gdn_conv_w", "gdn_A_log", "gdn_dt_bias", "gdn_out_norm_w", "fox_f_bias", "fox_q_norm_w",
           "fox_k_norm_w", "w_out", "norm2_w", "w_ffn_gate", "w_ffn_up", "w_ffn_down", "final_norm_w")


def _join(blocks, axis):
    _, r, c = blocks.shape
    if axis == 0:
        return blocks.reshape(N_DEV * r, c)
    return blocks.transpose(1, 0, 2).reshape(r, N_DEV * c)


def _cut(full, axis):
    r, c = full.shape
    if axis == 0:
        return full.reshape(N_DEV, r // N_DEV, c)
    return full.reshape(r, N_DEV, c // N_DEV).transpose(1, 0, 2)


def kernel(x, norm1_w, w_in, gdn_conv_w, gdn_A_log, gdn_dt_bias, gdn_out_norm_w, fox_f_bias, fox_q_norm_w, fox_k_norm_w, w_out, norm2_w, w_ffn_gate, w_ffn_up, w_ffn_down, final_norm_w, loss_target, m_norm1_w, m_w_in, m_gdn_conv_w, m_gdn_A_log, m_gdn_dt_bias, m_gdn_out_norm_w, m_fox_f_bias, m_fox_q_norm_w, m_fox_k_norm_w, m_w_out, m_norm2_w, m_w_ffn_gate, m_w_ffn_up, m_w_ffn_down, m_final_norm_w, v_norm1_w, v_w_in, v_gdn_conv_w, v_gdn_A_log, v_gdn_dt_bias, v_gdn_out_norm_w, v_fox_f_bias, v_fox_q_norm_w, v_fox_k_norm_w, v_w_out, v_norm2_w, v_w_ffn_gate, v_w_ffn_up, v_w_ffn_down, v_final_norm_w):
    w = dict(norm1_w=norm1_w, w_in=w_in, gdn_conv_w=gdn_conv_w, gdn_A_log=gdn_A_log, gdn_dt_bias=gdn_dt_bias,
             gdn_out_norm_w=gdn_out_norm_w, fox_f_bias=fox_f_bias, fox_q_norm_w=fox_q_norm_w, fox_k_norm_w=fox_k_norm_w,
             w_out=w_out, norm2_w=norm2_w, w_ffn_gate=w_ffn_gate, w_ffn_up=w_ffn_up, w_ffn_down=w_ffn_down,
             final_norm_w=final_norm_w)
    m = dict(norm1_w=m_norm1_w, w_in=m_w_in, gdn_conv_w=m_gdn_conv_w, gdn_A_log=m_gdn_A_log, gdn_dt_bias=m_gdn_dt_bias,
             gdn_out_norm_w=m_gdn_out_norm_w, fox_f_bias=m_fox_f_bias, fox_q_norm_w=m_fox_q_norm_w,
             fox_k_norm_w=m_fox_k_norm_w, w_out=m_w_out, norm2_w=m_norm2_w, w_ffn_gate=m_w_ffn_gate,
             w_ffn_up=m_w_ffn_up, w_ffn_down=m_w_ffn_down, final_norm_w=m_final_norm_w)
    v = dict(norm1_w=v_norm1_w, w_in=v_w_in, gdn_conv_w=v_gdn_conv_w, gdn_A_log=v_gdn_A_log, gdn_dt_bias=v_gdn_dt_bias,
             gdn_out_norm_w=v_gdn_out_norm_w, fox_f_bias=v_fox_f_bias, fox_q_norm_w=v_fox_q_norm_w,
             fox_k_norm_w=v_fox_k_norm_w, w_out=v_w_out, norm2_w=v_norm2_w, w_ffn_gate=v_w_ffn_gate,
             w_ffn_up=v_w_ffn_up, w_ffn_down=v_w_ffn_down, final_norm_w=v_final_norm_w)
    axis_of = dict(MATRICES)
    late = ("w_out", "w_ffn_gate", "w_ffn_up", "w_ffn_down")
    xs, tgt = x[0], loss_target[0]
    vp = _vector_params({n: w[n] for n in VECTORS})

    state_in, token = _spread_start([w["w_in"][0].astype(WIRE), w["gdn_conv_w"][0]], "gather", "gather_in_start")
    first_late = w[late[0]][0].astype(WIRE) + token[0, 0].astype(WIRE)
    state_late, token = _spread_start([first_late] + [w[n][0].astype(WIRE) for n in late[1:]], "gather", "gather_late_start")
    w_in_blocks, conv_blocks = _spread_wait(state_in, "gather", token, "gather_in_wait")
    w_main, w_small = _split_w_in(_join(w_in_blocks, 1))
    conv_w = _join(conv_blocks, 1)
    f = _mixer_forward(xs, vp, w_main, w_small, conv_w)
    full = {n: _join(b, axis_of[n])
            for n, b in zip(late, _spread_wait(state_late, "gather", f["mixf"], "gather_late_wait"))}

    (h2t, actt, dgate, dup, dx3, dx2, dmixg, dmixf, dn2w, dfw, loss) = _tail(
        xs, f["mixg"], f["mixf"], tgt, full["w_out"], vp["n2w"], full["w_ffn_gate"], full["w_ffn_up"],
        full["w_ffn_down"], vp["fw"])
    dlate = _late_grads(f, h2t, actt, dgate, dup, dx3, dx2)

    state_grads, token = _spread_start([_cut(dlate[n], axis_of[n]) for n in late], "scatter", "grads_late_start")
    grad_x, dw_in, dconv, small = _mixer_backward(xs, vp, w_main, w_small, conv_w, f, dx2, dmixg, dmixf,
                                                  vp["onw"] + token[0:1, 0:DH])
    vec = _pack_vectors(small["dn1w"], dn2w, dfw, small["gsum"], small["donw"], small["dfnw"], loss)

    last_kinds = ["scatter", "scatter", "gather"]
    state_last, token = _spread_start([_cut(dw_in, 1), _cut(dconv, 1), vec], last_kinds, "last_start")
    parts = dict(zip(late, _spread_wait(state_grads, "scatter", token, "grads_late_wait")))
    results = [{}, {}, {}, {}]

    def update(n):
        for d, a in zip(results, _adam_matrix(parts[n], w[n], m[n], v[n], "adam_" + n)):
            d[n] = a

    for n in late:
        update(n)
    parts["w_in"], parts["gdn_conv_w"], parts_vec = _spread_wait(state_last, last_kinds, results[0][late[-1]], "last_wait")
    update("w_in")
    update("gdn_conv_w")
    row = lambda a: a.reshape(1, -1)
    *vec_out, total_loss = _adam_vectors(parts_vec, [row(w[n]) for n in VECTORS], [row(m[n]) for n in VECTORS],
                                         [row(v[n]) for n in VECTORS])
    for d, arrs in zip(results, vec_out):
        for n, a in zip(VECTORS, arrs):
            d[n] = a.reshape(w[n].shape)
    return (total_loss[0, 0], grad_x[None], *[d[n] for d in results for n in WEIGHTS])
```

```python
import functools

import jax
import jax.numpy as jnp
from jax import lax
from jax.experimental import pallas as pl
from jax.experimental.pallas import tpu as pltpu

F32 = jnp.float32
MXU = jnp.bfloat16
WIRE = jnp.bfloat16
HI = lax.Precision.HIGHEST
EPS = 1e-6

N_DEV = 8
HEADS = 8
DH = 64
WIDTH = HEADS * DH
CHUNK = 64
LANE = 128
ROW_ALIGN = 16
TB = 256
QB = 256
VMEM_LIMIT = 60 * 1024 * 1024

ADAM_LR = 0.001
ADAM_B1 = 0.9
ADAM_B2 = 0.999
ADAM_EPS = 1e-08
ADAM_WD = 0.01
ADAM_STEP = 10

MESH = pl.DeviceIdType.MESH


def _params(sem=None):
    return pltpu.CompilerParams(dimension_semantics=sem, vmem_limit_bytes=VMEM_LIMIT)


def _resident(shape):
    n = len(shape)
    return pl.BlockSpec(shape, lambda *_: (0,) * n, pipeline_mode=pl.Buffered(1))


def _dot(a, b):
    return jnp.dot(a.astype(MXU), b.astype(MXU), preferred_element_type=F32)


def _dot_nt(a, b):
    return lax.dot_general(a.astype(MXU), b.astype(MXU), (((1,), (1,)), ((), ())), preferred_element_type=F32)


def _dot_tn(a, b):
    return lax.dot_general(a.astype(MXU), b.astype(MXU), (((0,), (0,)), ((), ())), preferred_element_type=F32)


def _hdot(a, b):
    return jnp.dot(a, b, precision=HI, preferred_element_type=F32)


def _hdot_nt(a, b):
    return lax.dot_general(a, b, (((1,), (1,)), ((), ())), precision=HI, preferred_element_type=F32)


def _hdot_tn(a, b):
    return lax.dot_general(a, b, (((0,), (0,)), ((), ())), precision=HI, preferred_element_type=F32)


def _sigmoid(x):
    return 0.5 * jnp.tanh(0.5 * x) + 0.5


def _softplus(x):
    return jnp.maximum(x, 0.0) + jnp.log(1.0 + jnp.exp(-jnp.abs(x)))


def _head_sum_matrix():
    ri = lax.broadcasted_iota(jnp.int32, (LANE, LANE), 0) // DH
    ci = lax.broadcasted_iota(jnp.int32, (LANE, LANE), 1) // DH
    return (ri == ci).astype(F32)


def _group_sum(a, ones_matrix):
    hi = a.astype(jnp.bfloat16)
    lo = (a - hi.astype(F32)).astype(jnp.bfloat16)
    m = ones_matrix.astype(jnp.bfloat16)
    return jnp.dot(hi, m, preferred_element_type=F32) + jnp.dot(lo, m, preferred_element_type=F32)


def _shift_down(x, s):
    return pltpu.roll(x, s, 0)


def _shift_up(x, s):
    return pltpu.roll(x, x.shape[0] - s, 0)


def _inproj(x, n1w, w_main, w_small):
    t, d = x.shape
    nm = w_main.shape[1]
    tb = min(TB, t)

    def body(x_ref, nw_ref, wm_ref, ws_ref, ht_ref, pm_ref, ps_ref):
        xv = x_ref[...]
        r = lax.rsqrt(jnp.mean(xv * xv, axis=-1, keepdims=True) + EPS)
        h32 = xv * r * nw_ref[...]
        h = h32.astype(MXU)
        ht_ref[...] = h32.T.astype(MXU)
        pm_ref[...] = jnp.dot(h, wm_ref[...], preferred_element_type=F32)
        ps_ref[...] = jnp.dot(h, ws_ref[...], preferred_element_type=F32)

    return pl.pallas_call(
        body, name="inproj", grid=(t // tb,),
        in_specs=[pl.BlockSpec((tb, d), lambda i: (i, 0)), _resident((1, d)), _resident((d, nm)), _resident((d, LANE))],
        out_specs=[pl.BlockSpec((d, tb), lambda i: (0, i)), pl.BlockSpec((tb, nm), lambda i: (i, 0)),
                   pl.BlockSpec((tb, LANE), lambda i: (i, 0))],
        out_shape=[jax.ShapeDtypeStruct((d, t), MXU), jax.ShapeDtypeStruct((t, nm), F32),
                   jax.ShapeDtypeStruct((t, LANE), F32)],
        compiler_params=_params(("arbitrary",)),
    )(x, n1w, w_main, w_small)


def _inproj_bwd(x, n1w, dx2, dgdn, dz, dfox, dfg, dps, w_main, w_small):
    t, d = x.shape
    tb = min(TB, t)
    w3 = 3 * WIDTH

    def body(x_ref, nw_ref, dx2_ref, dgdn_ref, dz_ref, dfox_ref, dfg_ref, dps_ref, wm_ref, ws_ref, gx_ref, dnw_ref):
        dh = _dot_nt(dgdn_ref[...], wm_ref[:, 0:w3])
        dh += _dot_nt(dz_ref[...], wm_ref[:, w3:w3 + WIDTH])
        dh += _dot_nt(dfox_ref[...], wm_ref[:, w3 + WIDTH:2 * w3 + WIDTH])
        dh += _dot_nt(dfg_ref[...], wm_ref[:, 2 * w3 + WIDTH:2 * w3 + 2 * WIDTH])
        dh += _dot_nt(dps_ref[...], ws_ref[...])
        xv = x_ref[...]
        r = lax.rsqrt(jnp.mean(xv * xv, axis=-1, keepdims=True) + EPS)
        xn = xv * r

        @pl.when(pl.program_id(0) == 0)
        def _():
            dnw_ref[...] = jnp.zeros_like(dnw_ref)

        dnw_ref[...] += jnp.sum(dh * xn, axis=0, keepdims=True)
        g = dh * nw_ref[...]
        gx_ref[...] = dx2_ref[...] + r * (g - xn * jnp.mean(g * xn, axis=-1, keepdims=True))

    def tok(n):
        return pl.BlockSpec((tb, n), lambda i: (i, 0))

    return pl.pallas_call(
        body, name="inproj_bwd", grid=(t // tb,),
        in_specs=[tok(d), _resident((1, d)), tok(d), tok(w3), tok(WIDTH), tok(w3), tok(WIDTH), tok(LANE),
                  _resident(w_main.shape), _resident(w_small.shape)],
        out_specs=[tok(d), pl.BlockSpec((1, d), lambda i: (0, 0))],
        out_shape=[jax.ShapeDtypeStruct((t, d), F32), jax.ShapeDtypeStruct((1, d), F32)],
        compiler_params=_params(("arbitrary",)),
    )(x, n1w, dx2, dgdn, dz, dfox, dfg, dps, w_main, w_small)


def _gate_lanes(shape):
    lane = lax.broadcasted_iota(jnp.int32, shape, 1)
    return lane < HEADS, (lane >= HEADS) & (lane < 2 * HEADS), (lane >= 2 * HEADS) & (lane < 3 * HEADS)


def _block_masks():
    ri = lax.broadcasted_iota(jnp.int32, (LANE, LANE), 0)
    ci = lax.broadcasted_iota(jnp.int32, (LANE, LANE), 1)
    same = (ri // CHUNK) == (ci // CHUNK)
    return ((ri >= ci).astype(F32), (ri <= ci).astype(F32), (same & (ri >= ci)).astype(F32),
            (same & (ri <= ci)).astype(F32), same.astype(F32))


def _gates(ps, gparams):
    t = ps.shape[0]
    nb = t // LANE

    def body(ps_ref, gp_ref, out_ref, run_ref, tot_ref, runt_ref):
        p = ps_ref[...]
        is_b, is_a, is_f = _gate_lanes(p.shape)
        z = p + gp_ref[0:1, :]
        neg_exp_a = -jnp.exp(gp_ref[1:2, :])
        glog = neg_exp_a * _softplus(z)
        logf = -_softplus(-z)
        out_ref[...] = jnp.where(is_b, _sigmoid(p), jnp.where(is_a, glog, jnp.where(is_f, logf, 0.0)))
        tril, _, tril_c, _, same_c = _block_masks()
        off = jnp.zeros((1, LANE), F32)
        for b in range(nb):
            rows = slice(b * LANE, (b + 1) * LANE)
            blk = out_ref[rows, :]
            ga = jnp.where(is_a[:LANE], blk, 0.0)
            fb = _hdot(tril, jnp.where(is_f[:LANE], blk, 0.0)) + off
            run = fb + _hdot(tril_c, ga)
            run_ref[rows, :] = run
            runt_ref[:, rows] = run.T
            tot_ref[rows, :] = _hdot(same_c, ga)
            off = fb[LANE - 1:LANE, :]

    return pl.pallas_call(
        body, name="gates",
        out_shape=[jax.ShapeDtypeStruct((t, LANE), F32)] * 3 + [jax.ShapeDtypeStruct((LANE, t), F32)],
        compiler_params=_params(),
    )(ps, gparams)


def _gates_bwd(ps, gparams, dcol_g, dtot_g, dcol_f, drow_g, drow_f):
    t = ps.shape[0]
    nb = t // LANE

    def body(ps_ref, gp_ref, dcg_ref, dtg_ref, dcf_ref, drg_ref, drf_ref, dps_ref, sums_ref, dl_ref, d0_ref, tr_ref):
        p = ps_ref[...]
        is_b, is_a, is_f = _gate_lanes(p.shape)
        _, triu, _, triu_c, same_c = _block_masks()
        tr_ref[...] = jnp.zeros_like(tr_ref)
        off = jnp.zeros((1, LANE), F32)
        for b in reversed(range(nb)):
            rows = slice(b * LANE, (b + 1) * LANE)
            tr_ref[HEADS:2 * HEADS, :] = drg_ref[:, rows]
            tr_ref[2 * HEADS:3 * HEADS, :] = drf_ref[:, rows]
            d = dcg_ref[rows, :] + dcf_ref[rows, :] + tr_ref[...].T
            d0_ref[rows, :] = d
            dlf = _hdot(triu, jnp.where(is_f[:LANE], d, 0.0)) + off
            dla = (_hdot(triu_c, jnp.where(is_a[:LANE], d, 0.0))
                   + _hdot(same_c, jnp.where(is_a[:LANE], dtg_ref[rows, :], 0.0)))
            dl_ref[rows, :] = dlf + dla
            off = dlf[0:1, :]
        z = p + gp_ref[0:1, :]
        neg_exp_a = -jnp.exp(gp_ref[1:2, :])
        sb = _sigmoid(p)
        glog = neg_exp_a * _softplus(z)
        dl = dl_ref[...]
        dp = jnp.where(is_b, d0_ref[...] * sb * (1.0 - sb),
                       jnp.where(is_a, dl * neg_exp_a * _sigmoid(z), jnp.where(is_f, dl * _sigmoid(-z), 0.0)))
        dps_ref[...] = dp
        s_a = jnp.sum(jnp.where(is_a, dl * glog, 0.0), axis=0, keepdims=True)
        s_p = jnp.sum(jnp.where(is_b, 0.0, dp), axis=0, keepdims=True)
        row = lax.broadcasted_iota(jnp.int32, (8, LANE), 0)
        from_a = pltpu.roll(jnp.where(row == 0, s_a, jnp.where(row == 1, s_p, 0.0)), LANE - HEADS, 1)
        from_f = pltpu.roll(jnp.where(row == 2, s_p, 0.0), LANE - 2 * HEADS, 1)
        lane = lax.broadcasted_iota(jnp.int32, (8, LANE), 1)
        sums_ref[...] = jnp.where(lane < HEADS, from_a + from_f, 0.0)

    return pl.pallas_call(
        body, name="gates_bwd",
        out_shape=[jax.ShapeDtypeStruct((t, LANE), F32), jax.ShapeDtypeStruct((8, LANE), F32)],
        scratch_shapes=[pltpu.VMEM((t, LANE), F32), pltpu.VMEM((t, LANE), F32), pltpu.VMEM((LANE, LANE), F32)],
        compiler_params=_params(),
    )(ps, gparams, dcol_g, dtot_g, dcol_f, drow_g, drow_f)


def _conv(xv, w):
    acc = w[3:4, :] * xv
    for s in range(1, 4):
        acc += w[3 - s:4 - s, :] * _shift_down(xv, s)
    return acc


PREP_ROWS = 256
HALO = 8


def _tile_rows(r):
    return pl.ds(pl.multiple_of(r * PREP_ROWS, PREP_ROWS), PREP_ROWS)


def _gdn_prep(pm, conv_w):
    t = pm.shape[0]
    nj = WIDTH // LANE
    win = PREP_ROWS + HALO

    def body(x_ref, w_ref, o_ref, xp_ref):
        kind = pl.program_id(0)
        xp_ref[0:HALO, :] = jnp.zeros((HALO, LANE), F32)
        xp_ref[HALO:, :] = x_ref[...]
        w = w_ref[...]
        hs = _head_sum_matrix()

        def tile(r, _):
            xw = xp_ref[pl.ds(pl.multiple_of(r * PREP_ROWS, PREP_ROWS), win), :]
            acc = _conv(xw, w)[HALO:]
            y = acc * _sigmoid(acc)
            out = jnp.where(kind < 2, y * lax.rsqrt(_group_sum(y * y, hs) + EPS), y)
            o_ref[0, 0, _tile_rows(r), :] = out[:, :DH]
            o_ref[0, 1, _tile_rows(r), :] = out[:, DH:]
            return 0

        lax.fori_loop(0, t // PREP_ROWS, tile, 0)

    return pl.pallas_call(
        body, name="gdn_prep", grid=(3, nj),
        in_specs=[pl.BlockSpec((t, LANE), lambda i, j: (0, i * nj + j)),
                  pl.BlockSpec((4, LANE), lambda i, j: (0, i * nj + j))],
        out_specs=pl.BlockSpec((1, 2, t, DH), lambda i, j: (i, j, 0, 0)),
        out_shape=jax.ShapeDtypeStruct((3, HEADS, t, DH), F32),
        scratch_shapes=[pltpu.VMEM((t + HALO, LANE), F32)],
        compiler_params=_params(("arbitrary", "arbitrary")),
    )(pm, conv_w)


def _gdn_prep_bwd(pm, conv_w, dqkv):
    t = pm.shape[0]
    nj = WIDTH // LANE
    win = PREP_ROWS + 2 * HALO

    def body(x_ref, w_ref, d_ref, dx_ref, dw_ref, xp_ref, dp_ref):
        kind = pl.program_id(0)
        zeros = jnp.zeros((HALO, LANE), F32)
        for ref in (xp_ref, dp_ref):
            ref[0:HALO, :] = zeros
            ref[HALO + t:, :] = zeros
        xp_ref[HALO:HALO + t, :] = x_ref[...]
        dp_ref[HALO:HALO + t, 0:DH] = d_ref[0, 0]
        dp_ref[HALO:HALO + t, DH:] = d_ref[0, 1]
        w = w_ref[...]
        hs = _head_sum_matrix()
        rows = lax.broadcasted_iota(jnp.int32, (win, LANE), 0)
        in_tile = (rows >= HALO) & (rows < HALO + PREP_ROWS)

        def tile(r, dw):
            start = pl.multiple_of(r * PREP_ROWS, PREP_ROWS)
            xw = xp_ref[pl.ds(start, win), :]
            d = dp_ref[pl.ds(start, win), :]
            acc = _conv(xw, w)
            sg = _sigmoid(acc)
            y = acc * sg
            rn = lax.rsqrt(_group_sum(y * y, hs) + EPS)
            yn = y * rn
            dy = jnp.where(kind < 2, rn * (d - yn * _group_sum(d * yn, hs)), d)
            dacc = dy * sg * (1.0 + acc * (1.0 - sg))
            dx = w[3:4, :] * dacc
            for s in range(1, 4):
                dx += w[3 - s:4 - s, :] * _shift_up(dacc, s)
            dx_ref[_tile_rows(r), :] = dx[HALO:HALO + PREP_ROWS]
            dm = jnp.where(in_tile, dacc, 0.0)
            return tuple(dw[i] + jnp.sum(dm * (xw if i == 3 else _shift_down(xw, 3 - i)), axis=0, keepdims=True)
                         for i in range(4))

        dw = lax.fori_loop(0, t // PREP_ROWS, tile, tuple(jnp.zeros((1, LANE), F32) for _ in range(4)))
        for i in range(4):
            dw_ref[i:i + 1, :] = dw[i]

    return pl.pallas_call(
        body, name="gdn_prep_bwd", grid=(3, nj),
        in_specs=[pl.BlockSpec((t, LANE), lambda i, j: (0, i * nj + j)),
                  pl.BlockSpec((4, LANE), lambda i, j: (0, i * nj + j)),
                  pl.BlockSpec((1, 2, t, DH), lambda i, j: (i, j, 0, 0))],
        out_specs=[pl.BlockSpec((t, LANE), lambda i, j: (0, i * nj + j)),
                   pl.BlockSpec((4, LANE), lambda i, j: (0, i * nj + j))],
        out_shape=[jax.ShapeDtypeStruct((t, 3 * WIDTH), F32), jax.ShapeDtypeStruct((4, 3 * WIDTH), F32)],
        scratch_shapes=[pltpu.VMEM((t + 2 * HALO, LANE), F32), pltpu.VMEM((t + 2 * HALO, LANE), F32)],
        compiler_params=_params(("arbitrary", "arbitrary")),
    )(pm, conv_w, dqkv)


FOX_COL0 = 4 * WIDTH // LANE


def _fox_prep(pm, nw):
    t = pm.shape[0]
    nj = WIDTH // LANE

    def body(x_ref, w_ref, o_ref):
        kind = pl.program_id(0)
        hs = _head_sum_matrix()
        wk = w_ref[pl.ds(kind, 1), :]

        def tile(r, _):
            xv = x_ref[_tile_rows(r), :]
            ms = _group_sum(xv * xv, hs) * (1.0 / DH)
            out = jnp.where(kind < 2, xv * lax.rsqrt(ms + EPS) * wk, xv)
            o_ref[0, 0, _tile_rows(r), :] = out[:, :DH]
            o_ref[0, 1, _tile_rows(r), :] = out[:, DH:]
            return 0

        lax.fori_loop(0, t // PREP_ROWS, tile, 0)

    return pl.pallas_call(
        body, name="fox_prep", grid=(3, nj),
        in_specs=[pl.BlockSpec((t, LANE), lambda i, j: (0, FOX_COL0 + i * nj + j)),
                  pl.BlockSpec((3, LANE), lambda i, j: (0, 0))],
        out_specs=pl.BlockSpec((1, 2, t, DH), lambda i, j: (i, j, 0, 0)),
        out_shape=jax.ShapeDtypeStruct((3, HEADS, t, DH), F32),
        compiler_params=_params(("arbitrary", "arbitrary")),
    )(pm, nw)


def _fox_prep_bwd(pm, nw, dqkv):
    t = pm.shape[0]
    nj = WIDTH // LANE

    def body(x_ref, w_ref, d_ref, dx_ref, dw_ref):
        kind = pl.program_id(0)
        hs = _head_sum_matrix()
        wk = w_ref[pl.ds(kind, 1), :]

        def tile(r, dw):
            xv = x_ref[_tile_rows(r), :]
            rn = lax.rsqrt(_group_sum(xv * xv, hs) * (1.0 / DH) + EPS)
            xn = xv * rn
            d = jnp.concatenate([d_ref[0, 0, _tile_rows(r), :], d_ref[0, 1, _tile_rows(r), :]], axis=1)
            g = d * wk
            dxn = rn * (g - xn * _group_sum(g * xn, hs) * (1.0 / DH))
            dx_ref[_tile_rows(r), :] = jnp.where(kind < 2, dxn, d)
            return dw + jnp.sum(d * xn, axis=0, keepdims=True)

        dw_ref[0, 0] = lax.fori_loop(0, t // PREP_ROWS, tile, jnp.zeros((1, LANE), F32))

    return pl.pallas_call(
        body, name="fox_prep_bwd", grid=(3, nj),
        in_specs=[pl.BlockSpec((t, LANE), lambda i, j: (0, FOX_COL0 + i * nj + j)),
                  pl.BlockSpec((3, LANE), lambda i, j: (0, 0)),
                  pl.BlockSpec((1, 2, t, DH), lambda i, j: (i, j, 0, 0))],
        out_specs=[pl.BlockSpec((t, LANE), lambda i, j: (0, i * nj + j)),
                   pl.BlockSpec((1, 1, 1, LANE), lambda i, j: (i, j, 0, 0))],
        out_shape=[jax.ShapeDtypeStruct((t, 3 * WIDTH), F32), jax.ShapeDtypeStruct((3, nj, 1, LANE), F32)],
        compiler_params=_params(("arbitrary", "arbitrary")),
    )(pm, nw, dqkv)


SC = 256
CPS = SC // CHUNK
GDN_HP = 2
Q_SCALE = DH ** -0.5


def _sc_masks():
    ri = lax.broadcasted_iota(jnp.int32, (SC, SC), 0)
    ci = lax.broadcasted_iota(jnp.int32, (SC, SC), 1)
    same = (ri // CHUNK) == (ci // CHUNK)
    return same & (ri >= ci), same & (ri > ci), ri == ci


def _unit_lower_inverses(ms, eye):
    invs = [jnp.where(eye, 1.0, 0.0) + m for m in ms]
    for _ in range(5):
        ms = [_dot(m, m) for m in ms]
        invs = [inv + _dot(inv, m) for inv, m in zip(invs, ms)]
    return invs


def _lane_col(blk, lane_idx):
    lane = lax.broadcasted_iota(jnp.int32, blk.shape, 1)
    return jnp.sum(jnp.where(lane == lane_idx, blk, 0.0), axis=1, keepdims=True)


def _to_lane(col, lane_idx):
    lane = lax.broadcasted_iota(jnp.int32, (col.shape[0], LANE), 1)
    return jnp.where(lane == lane_idx, col, 0.0)


def _gdn_columns(gates_ref, run_ref, tot_ref, runt_ref, rows, h):
    return (_lane_col(gates_ref[rows, :], h), _lane_col(run_ref[rows, :], HEADS + h),
            _lane_col(tot_ref[rows, :], HEADS + h), runt_ref[pl.ds(h, 1), rows])


def _gdn_local(q, k, beta, gc, gl, grow, causal):
    decay = jnp.exp(jnp.where(causal, gc - grow, -1e30))
    egc = jnp.exp(gc)
    ekd = jnp.exp(gl - gc)
    qs = q * Q_SCALE
    kb = k * beta
    kk = _dot_nt(kb, k)
    qk = _dot_nt(qs, k)
    return beta, gl, decay, egc, ekd, qs, kb, kk, qk, jnp.where(causal, qk * decay, 0.0)


def _chunk_rows(c):
    return pl.ds(pl.multiple_of(c * CHUNK, CHUNK), CHUNK)


def _sc_rows(b):
    return pl.ds(pl.multiple_of(b * SC, SC), SC)


def _gdn_fwd(qkv, gates, run, tot, run_t):
    t = qkv.shape[2]
    nc = t // CHUNK
    nsc = t // SC

    hp_n = GDN_HP
    heads = range(hp_n)

    def body(qkv_ref, gates_ref, run_ref, tot_ref, runt_ref, o_ref, st_ref, inv_ref, kc_s, qc_s, g_s, au_s):
        hp = pl.program_id(0)
        causal, strict, eye = _sc_masks()

        def local(b, _):
            rows = _sc_rows(b)
            loc = [_gdn_local(qkv_ref[0, hh, rows, :], qkv_ref[1, hh, rows, :],
                              *_gdn_columns(gates_ref, run_ref, tot_ref, runt_ref, rows, hp * hp_n + hh), causal)
                   for hh in heads]
            invs = _unit_lower_inverses([-jnp.where(strict, l[7] * l[2], 0.0) for l in loc], eye)
            us, ws = [], []
            for hh in heads:
                beta, _, _, egc, _, _, kb, _, _, _ = loc[hh]
                inv_ref[hh, rows, :] = invs[hh].astype(inv_ref.dtype)
                us.append(_dot(invs[hh], qkv_ref[2, hh, rows, :] * beta))
                ws.append(_dot(invs[hh], kb * egc))
            for hh in heads:
                _, _, _, egc, ekd, qs, _, _, _, attn = loc[hh]
                g_s[hh, rows, :] = qs * egc - _dot(attn, ws[hh])
                au_s[hh, rows, :] = _dot(attn, us[hh])
                kd = qkv_ref[1, hh, rows, :] * ekd
                for j in range(CPS):
                    sl = slice(j * CHUNK, (j + 1) * CHUNK)
                    kc_s[hh, b * CPS + j] = _dot_tn(kd[sl], ws[hh][sl])
                    qc_s[hh, b * CPS + j] = _dot_tn(kd[sl], us[hh][sl])
            return 0

        lax.fori_loop(0, nsc, local, 0)

        def step(c, states):
            rows = _chunk_rows(c)
            tot_row = tot_ref[pl.ds(c * CHUNK, 1), :]
            new = []
            for hh in heads:
                s = states[hh]
                st_ref[hh, c] = s
                o_ref[hh, rows, :] = _dot(g_s[hh, rows, :], s) + au_s[hh, rows, :]
                egl = jnp.exp(_lane_col(tot_row, HEADS + hp * hp_n + hh))
                new.append(egl * s - _dot(kc_s[hh, c], s) + qc_s[hh, c])
            return tuple(new)

        lax.fori_loop(0, nc, step, tuple(jnp.zeros((DH, DH), F32) for _ in heads))

    whole = pl.BlockSpec((t, LANE), lambda h: (0, 0))
    return pl.pallas_call(
        body, name="gdn_fwd", grid=(HEADS // hp_n,),
        in_specs=[pl.BlockSpec((3, hp_n, t, DH), lambda h: (0, h, 0, 0)), whole, whole, whole,
                  pl.BlockSpec((HEADS, t), lambda h: (1, 0))],
        out_specs=[pl.BlockSpec((hp_n, t, DH), lambda h: (h, 0, 0)), pl.BlockSpec((hp_n, nc, DH, DH), lambda h: (h, 0, 0, 0)),
                   pl.BlockSpec((hp_n, t, SC), lambda h: (h, 0, 0))],
        out_shape=[jax.ShapeDtypeStruct((HEADS, t, DH), F32), jax.ShapeDtypeStruct((HEADS, nc, DH, DH), F32),
                   jax.ShapeDtypeStruct((HEADS, t, SC), MXU)],
        scratch_shapes=[pltpu.VMEM((hp_n, nc, DH, DH), F32), pltpu.VMEM((hp_n, nc, DH, DH), F32),
                        pltpu.VMEM((hp_n, t, DH), F32), pltpu.VMEM((hp_n, t, DH), F32)],
        compiler_params=_params(("arbitrary",)),
    )(qkv, gates, run, tot, run_t)


def _gdn_bwd(qkv, gates, run, tot, run_t, inv, states, do):
    t = qkv.shape[2]
    nc = t // CHUNK
    nsc = t // SC

    hp_n = GDN_HP
    heads = range(hp_n)

    def body(qkv_ref, gates_ref, run_ref, tot_ref, runt_ref, inv_ref, st_ref, do_ref,
             dqkv_ref, dcol_ref, dtot_ref, drow_ref, u_s, w_s, kc_s, h_s, dsn_s):
        hp = pl.program_id(0)
        causal, strict, _ = _sc_masks()

        @pl.when(hp == 0)
        def _():
            dcol_ref[...] = jnp.zeros_like(dcol_ref)
            dtot_ref[...] = jnp.zeros_like(dtot_ref)

        def local_of(hh, rows):
            return _gdn_local(qkv_ref[0, hh, rows, :], qkv_ref[1, hh, rows, :],
                              *_gdn_columns(gates_ref, run_ref, tot_ref, runt_ref, rows, hp * hp_n + hh), causal)

        def local(b, _):
            rows = _sc_rows(b)
            loc = [local_of(hh, rows) for hh in heads]
            us, ws = [], []
            for hh in heads:
                beta, _, _, egc, _, _, kb, _, _, _ = loc[hh]
                inv_b = inv_ref[hh, rows, :]
                us.append(_dot(inv_b, qkv_ref[2, hh, rows, :] * beta))
                ws.append(_dot(inv_b, kb * egc))
            gs = [loc[hh][5] * loc[hh][3] - _dot(loc[hh][9], ws[hh]) for hh in heads]
            for hh in heads:
                u_s[hh, rows, :] = us[hh]
                w_s[hh, rows, :] = ws[hh]
                kd = qkv_ref[1, hh, rows, :] * loc[hh][4]
                dout = do_ref[hh, rows, :]
                for j in range(CPS):
                    sl = slice(j * CHUNK, (j + 1) * CHUNK)
                    kc_s[hh, b * CPS + j] = _dot_tn(kd[sl], ws[hh][sl])
                    h_s[hh, b * CPS + j] = _dot_tn(gs[hh][sl], dout[sl])
            return 0

        lax.fori_loop(0, nsc, local, 0)

        def step(i, dss):
            c = nc - 1 - i
            tot_row = tot_ref[pl.ds(c * CHUNK, 1), :]
            new = []
            for hh in heads:
                ds = dss[hh]
                dsn_s[hh, c] = ds
                egl = jnp.exp(_lane_col(tot_row, HEADS + hp * hp_n + hh))
                new.append(egl * ds - _dot_tn(kc_s[hh, c], ds) + h_s[hh, c])
            return tuple(new)

        lax.fori_loop(0, nc, step, tuple(jnp.zeros((DH, DH), F32) for _ in heads))

        def back(b, _):
            rows = _sc_rows(b)
            first = lax.broadcasted_iota(jnp.int32, (CHUNK, 1), 0) == 0
            loc = [local_of(hh, rows) for hh in heads]
            mid = []
            for hh in heads:
                beta, gl, decay, egc, ekd, qs, kb, kk, qk, attn = loc[hh]
                u, w = u_s[hh, rows, :], w_s[hh, rows, :]
                kd = qkv_ref[1, hh, rows, :] * ekd
                dout = do_ref[hh, rows, :]
                dg_p, dkd_p, dw_p, du_p, dgl_p = [], [], [], [], []
                for j in range(CPS):
                    sl = slice(j * CHUNK, (j + 1) * CHUNK)
                    s = st_ref[hh, b * CPS + j]
                    dsn = dsn_s[hh, b * CPS + j]
                    dkc = -_dot_nt(dsn, s)
                    dg_p.append(_dot_nt(dout[sl], s))
                    dkd_p.append(_dot_nt(w[sl], dkc) + _dot_nt(u[sl], dsn))
                    dw_p.append(_dot(kd[sl], dkc))
                    du_p.append(_dot(kd[sl], dsn))
                    degl = jnp.sum(jnp.sum(s * dsn, axis=1, keepdims=True), axis=0, keepdims=True)
                    dgl_p.append(jnp.where(first, degl * jnp.exp(gl[j * CHUNK:j * CHUNK + 1, :]), 0.0))
                dg, dkd = jnp.concatenate(dg_p, axis=0), jnp.concatenate(dkd_p, axis=0)
                da = jnp.where(causal, _dot_nt(dout, u) - _dot_nt(dg, w), 0.0)
                at = _dot_tn(attn, jnp.concatenate([dout, dg], axis=1))
                du = at[:, :DH] + jnp.concatenate(du_p, axis=0)
                dw = jnp.concatenate(dw_p, axis=0) - at[:, DH:]
                mid.append((kd, dg, dkd, da, du, dw, jnp.concatenate(dgl_p, axis=0)))
            inv_ts = [inv_ref[hh, rows, :].astype(F32).T for hh in heads]
            its = [_dot(inv_ts[hh], jnp.concatenate([mid[hh][4], mid[hh][5]], axis=1)) for hh in heads]
            dinvs = []
            for hh in heads:
                beta, _, _, egc, _, _, kb, _, _, _ = loc[hh]
                dinvs.append(_dot_nt(mid[hh][4], qkv_ref[2, hh, rows, :] * beta) + _dot_nt(mid[hh][5], kb * egc))
            half = [_dot(inv_ts[hh], dinvs[hh]) for hh in heads]
            dls = [jnp.where(strict, -_dot(half[hh], inv_ts[hh]), 0.0) for hh in heads]
            for hh in heads:
                head = hp * hp_n + hh
                beta, gl, decay, egc, ekd, qs, kb, kk, qk, attn = loc[hh]
                kd, dg, dkd, da, du, dw, dgl_first = mid[hh]
                k, v = qkv_ref[1, hh, rows, :], qkv_ref[2, hh, rows, :]
                dvb, dkbe = its[hh][:, :DH], its[hh][:, DH:]
                dl = dls[hh]
                dlogd = (dl * kk + da * qk) * decay
                dkk = dl * decay
                dqk = da * decay
                dkb = _dot(dkk, k) + dkbe * egc
                dqs = _dot(dqk, k) + dg * egc
                dk = (_dot_tn(jnp.concatenate([dkk, dqk], axis=0), jnp.concatenate([kb, qs], axis=0))
                      + dkd * ekd + dkb * beta)
                dkd_kd = jnp.sum(dkd * kd, axis=1, keepdims=True)
                dgc = (jnp.sum(dlogd, axis=1, keepdims=True) + jnp.sum(dg * qs, axis=1, keepdims=True) * egc
                       + jnp.sum(dkbe * (kb * egc), axis=1, keepdims=True) - dkd_kd)
                dbeta = jnp.sum(dkb * k, axis=1, keepdims=True) + jnp.sum(dvb * v, axis=1, keepdims=True)
                dqkv_ref[0, hh, rows, :] = dqs * Q_SCALE
                dqkv_ref[1, hh, rows, :] = dk
                dqkv_ref[2, hh, rows, :] = dvb * beta
                dcol_ref[rows, :] += _to_lane(dbeta, head) + _to_lane(dgc, HEADS + head)
                dtot_ref[rows, :] += _to_lane(dkd_kd + dgl_first, HEADS + head)
                drow_ref[pl.ds(head, 1), rows] = -jnp.sum(dlogd, axis=0, keepdims=True)
            return 0

        lax.fori_loop(0, nsc, back, 0)

    whole = pl.BlockSpec((t, LANE), lambda h: (0, 0))
    rowspec = pl.BlockSpec((HEADS, t), lambda h: (0, 0))
    sq = pltpu.VMEM((hp_n, nc, DH, DH), F32)
    per_head = pltpu.VMEM((hp_n, t, DH), F32)
    return pl.pallas_call(
        body, name="gdn_bwd", grid=(HEADS // hp_n,),
        in_specs=[pl.BlockSpec((3, hp_n, t, DH), lambda h: (0, h, 0, 0)), whole, whole, whole,
                  pl.BlockSpec((HEADS, t), lambda h: (1, 0)),
                  pl.BlockSpec((hp_n, t, SC), lambda h: (h, 0, 0)), pl.BlockSpec((hp_n, nc, DH, DH), lambda h: (h, 0, 0, 0)),
                  pl.BlockSpec((hp_n, t, DH), lambda h: (h, 0, 0))],
        out_specs=[pl.BlockSpec((3, hp_n, t, DH), lambda h: (0, h, 0, 0)), whole, whole, rowspec],
        out_shape=[jax.ShapeDtypeStruct((3, HEADS, t, DH), F32), jax.ShapeDtypeStruct((t, LANE), F32),
                   jax.ShapeDtypeStruct((t, LANE), F32), jax.ShapeDtypeStruct((HEADS, t), F32)],
        scratch_shapes=[per_head, per_head, sq, sq, sq],
        compiler_params=_params(("arbitrary",)),
    )(qkv, gates, run, tot, run_t, inv, states, do)


FOX_HP = 2


def _wide_t(a):
    r = a.shape[0]
    return jnp.concatenate([a, jnp.zeros((r, LANE - DH), F32)], axis=1).T[:DH]


def _tall_t(a):
    r = a.shape[1]
    return jnp.concatenate([a, jnp.zeros((LANE - DH, r), F32)], axis=0).T[:, :DH]


def _key_side_f(run_blk, head, qb):
    col = jnp.broadcast_to(_lane_col(run_blk, 2 * HEADS + head), (run_blk.shape[0], LANE))
    return jnp.concatenate([col] * (qb // LANE), axis=1)


def _diag_mask(qb):
    return lax.broadcasted_iota(jnp.int32, (qb, qb), 0) <= lax.broadcasted_iota(jnp.int32, (qb, qb), 1)


def _fox_fwd(qkv, run, run_t, carry=None):
    t = qkv.shape[2]
    qb = min(QB, t)
    nq = t // qb
    hp_n = FOX_HP
    c_in, c_in_specs, c_out_shape, c_out_specs, c_sem = _carry_operands(*carry) if carry else ([], [], [], [], None)
    n_c = len(c_in)

    def body(*refs):
        q_ref, k_ref, v_ref, run_ref, runt_ref = refs[:5]
        o_ref, lse_ref = refs[5 + n_c:7 + n_c]
        kb_s, vt_s = refs[7 + n_c + len(c_out_shape):9 + n_c + len(c_out_shape)]
        hp = pl.program_id(0)
        i = pl.program_id(1)

        if carry:
            @pl.when((hp == 0) & (i == 0))
            def _():
                _carry_begin(carry[0], refs[5:5 + n_c], refs[7 + n_c], refs[8 + n_c], refs[-1])

        @pl.when(i == 0)
        def _():
            for hh in range(hp_n):
                kb_s[hh] = k_ref[0, hh].astype(MXU)
                for b in range(nq):
                    rows = slice(b * qb, (b + 1) * qb)
                    vt_s[hh, :, rows] = _wide_t(v_ref[0, hh, rows, :]).astype(MXU)

        qrows = pl.ds(pl.multiple_of(i * qb, qb), qb)
        qs = [(q_ref[0, hh] * Q_SCALE).astype(MXU) for hh in range(hp_n)]
        fq = [runt_ref[pl.ds(hp * hp_n + hh, 1), qrows] for hh in range(hp_n)]

        def block_rows(j):
            return pl.ds(pl.multiple_of(j * qb, qb), qb)

        def scores(j):
            return tuple(_dot_nt(kb_s[hh, block_rows(j), :], qs[hh]) for hh in range(hp_n))

        def absorb(j, raw, state, diagonal):
            rows = block_rows(j)
            run_blk = run_ref[rows, :]
            stats, pv = [], []
            for hh in range(hp_n):
                m, l, _ = state[hh]
                st = raw[hh] + (fq[hh] - _key_side_f(run_blk, hp * hp_n + hh, qb))
                if diagonal:
                    st = jnp.where(_diag_mask(qb), st, -1e30)
                m_new = jnp.maximum(m, jnp.max(st, axis=0, keepdims=True))
                p = jnp.exp(st - m_new)
                alpha = jnp.exp(m - m_new)
                stats.append((m_new, alpha * l + jnp.sum(p, axis=0, keepdims=True), alpha))
                pv.append(_dot(vt_s[hh, :, rows], p))
            return tuple((stats[hh][0], stats[hh][1], stats[hh][2] * state[hh][2] + pv[hh]) for hh in range(hp_n))

        def kstep(j, carry):
            state, raw = carry
            ahead = scores(j + 1)
            return absorb(j, raw, state, False), ahead

        init = tuple((jnp.full((1, qb), -1e30, F32), jnp.zeros((1, qb), F32), jnp.zeros((DH, qb), F32))
                     for _ in range(hp_n))
        state, raw = lax.fori_loop(0, i, kstep, (init, scores(0)))
        state = absorb(i, raw, state, True)
        for hh in range(hp_n):
            m, l, acc = state[hh]
            o_ref[hh] = _tall_t(acc / l)
            lse_ref[pl.ds(hp * hp_n + hh, 1), qrows] = m + jnp.log(l)

    out = pl.pallas_call(
        body, name="fox_fwd", grid=(HEADS // hp_n, nq),
        in_specs=[pl.BlockSpec((1, hp_n, qb, DH), lambda h, i: (0, h, i, 0)),
                  pl.BlockSpec((1, hp_n, t, DH), lambda h, i: (1, h, 0, 0)),
                  pl.BlockSpec((1, hp_n, t, DH), lambda h, i: (2, h, 0, 0)),
                  pl.BlockSpec((t, LANE), lambda h, i: (0, 0)),
                  pl.BlockSpec((HEADS, t), lambda h, i: (2, 0))] + c_in_specs,
        out_specs=[pl.BlockSpec((hp_n, qb, DH), lambda h, i: (h, i, 0)),
                   pl.BlockSpec((HEADS, t), lambda h, i: (0, 0))] + c_out_specs,
        out_shape=[jax.ShapeDtypeStruct((HEADS, t, DH), F32), jax.ShapeDtypeStruct((HEADS, t), F32)] + c_out_shape,
        input_output_aliases={5 + j: 4 + j for j in range(n_c)},
        scratch_shapes=[pltpu.VMEM((hp_n, t, DH), MXU), pltpu.VMEM((hp_n, DH, t), MXU)] + ([c_sem] if carry else []),
        compiler_params=_params(("arbitrary", "arbitrary")),
    )(qkv, qkv, qkv, run, run_t, *c_in)
    return (out[0], out[1], _carry_state(out[2:])) if carry else (out[0], out[1])


def _fox_bwd(qkv, run, run_t, o, lse, do, carry=None):
    t = qkv.shape[2]
    qb = min(QB, t)
    nq = t // qb
    hp_n = FOX_HP

    c_in, c_in_specs, c_out_shape, c_out_specs, c_sem = _carry_operands(*carry) if carry else ([], [], [], [], None)
    n_c = len(c_in)

    def body(*refs):
        q_ref, k_ref, v_ref, run_ref, runt_ref, o_ref, lse_ref, do_ref = refs[:8]
        dqkv_ref, dcol_ref, drow_ref = refs[8 + n_c:11 + n_c]
        dqt_s = refs[11 + n_c + len(c_out_shape)]
        hp = pl.program_id(0)
        j = pl.program_id(1)

        if carry:
            @pl.when((hp == 0) & (j == 0))
            def _():
                _carry_begin(carry[0], refs[8:8 + n_c], refs[11 + n_c], refs[12 + n_c], refs[-1])

        @pl.when(j == 0)
        def _():
            dqt_s[...] = jnp.zeros_like(dqt_s)

        @pl.when((j == 0) & (hp == 0))
        def _():
            dcol_ref[...] = jnp.zeros_like(dcol_ref)
            drow_ref[...] = jnp.zeros_like(drow_ref)

        krows = pl.ds(pl.multiple_of(j * qb, qb), qb)
        run_blk = run_ref[krows, :]
        ones8 = jnp.ones((8, DH), MXU)
        kb, kt, vb, fk = [], [], [], []
        for hh in range(hp_n):
            kf = k_ref[0, hh]
            kb.append(kf.astype(MXU))
            kt.append(_wide_t(kf).astype(MXU))
            vb.append(v_ref[0, hh].astype(MXU))
            fk.append(_key_side_f(run_blk, hp * hp_n + hh, qb))

        def block_rows(i):
            return pl.ds(pl.multiple_of(i * qb, qb), qb)

        def products(i):
            rows = block_rows(i)
            out = []
            for hh in range(hp_n):
                dout = do_ref[hh, rows, :]
                x = dout * o_ref[hh, rows, :]
                x_hi = x.astype(MXU)
                out.append((_dot_nt(kb[hh], q_ref[0, hh, rows, :] * Q_SCALE), _dot_nt(vb[hh], dout),
                            (_dot_nt(ones8, x_hi) + _dot_nt(ones8, x - x_hi.astype(F32)))[0:1, :]))
            return tuple(out)

        def absorb(i, prods, acc, diagonal):
            rows = block_rows(i)
            pieces = []
            for hh in range(hp_n):
                head = hp * hp_n + hh
                raw, dpt, drow = prods[hh]
                off = runt_ref[pl.ds(head, 1), rows] - lse_ref[pl.ds(head, 1), rows]
                st = raw + (off - fk[hh])
                if diagonal:
                    st = jnp.where(_diag_mask(qb), st, -1e30)
                pt = jnp.exp(st)
                dst = pt * (dpt - drow)
                drow_ref[pl.ds(head, 1), rows] += jnp.sum(dst, axis=0, keepdims=True)
                folded = dst[:, 0:LANE]
                for c in range(1, qb // LANE):
                    folded = folded + dst[:, c * LANE:(c + 1) * LANE]
                pieces.append((_dot(dst, q_ref[0, hh, rows, :] * Q_SCALE), _dot(pt, do_ref[hh, rows, :]),
                               _dot(kt[hh], dst), folded))
            out = []
            for hh in range(hp_n):
                dqt_s[hh, :, rows] += pieces[hh][2]
                out.append((acc[hh][0] + pieces[hh][0], acc[hh][1] + pieces[hh][1], acc[hh][2] + pieces[hh][3]))
            return tuple(out)

        def qstep(i, carry):
            acc, prods = carry
            ahead = products(jnp.minimum(i + 1, nq - 1))
            return absorb(i, prods, acc, False), ahead

        init = tuple((jnp.zeros((qb, DH), F32), jnp.zeros((qb, DH), F32), jnp.zeros((qb, LANE), F32))
                     for _ in range(hp_n))
        first = products(j)
        ahead = products(jnp.minimum(j + 1, nq - 1))
        acc = absorb(j, first, init, True)
        acc, _ = lax.fori_loop(j + 1, nq, qstep, (acc, ahead))
        for hh in range(hp_n):
            dk, dv, ds_sum = acc[hh]
            dqkv_ref[1, hh, krows, :] = dk
            dqkv_ref[2, hh, krows, :] = dv
            dcol_ref[krows, :] += _to_lane(-jnp.sum(ds_sum, axis=1, keepdims=True), 2 * HEADS + hp * hp_n + hh)

        @pl.when(j == nq - 1)
        def _():
            for hh in range(hp_n):
                for b in range(nq):
                    rows = slice(b * qb, (b + 1) * qb)
                    dqkv_ref[0, hh, rows, :] = _tall_t(dqt_s[hh, :, rows]) * Q_SCALE

    full = pl.BlockSpec((hp_n, t, DH), lambda h, j: (h, 0, 0))
    rows8 = pl.BlockSpec((HEADS, t), lambda h, j: (0, 0))
    out = pl.pallas_call(
        body, name="fox_bwd", grid=(HEADS // hp_n, nq),
        in_specs=[pl.BlockSpec((1, hp_n, t, DH), lambda h, j: (0, h, 0, 0)),
                  pl.BlockSpec((1, hp_n, qb, DH), lambda h, j: (1, h, j, 0)),
                  pl.BlockSpec((1, hp_n, qb, DH), lambda h, j: (2, h, j, 0)),
                  pl.BlockSpec((t, LANE), lambda h, j: (0, 0)), pl.BlockSpec((HEADS, t), lambda h, j: (2, 0)),
                  full, rows8, full] + c_in_specs,
        out_specs=[pl.BlockSpec((3, hp_n, t, DH), lambda h, j: (0, h, 0, 0)),
                   pl.BlockSpec((t, LANE), lambda h, j: (0, 0)), rows8] + c_out_specs,
        out_shape=[jax.ShapeDtypeStruct((3, HEADS, t, DH), F32), jax.ShapeDtypeStruct((t, LANE), F32),
                   jax.ShapeDtypeStruct((HEADS, t), F32)] + c_out_shape,
        input_output_aliases={8 + j: 5 + j for j in range(n_c)},
        scratch_shapes=[pltpu.VMEM((hp_n, DH, t), F32)] + ([c_sem] if carry else []),
        compiler_params=_params(("arbitrary", "arbitrary")),
    )(qkv, qkv, qkv, run, run_t, o, lse, do, *c_in)
    return (out[0], out[1], out[2], _carry_state(out[3:])) if carry else tuple(out)


Z_COL0 = 3 * WIDTH // LANE
FGATE_COL0 = 7 * WIDTH // LANE


def _gdn_post(o, pm, onw):
    t = pm.shape[0]

    def body(o_ref, z_ref, w_ref, m_ref, mt_ref):
        z = z_ref[...]
        sz = z * _sigmoid(z)
        halves = []
        for hh in range(2):
            ov = o_ref[hh]
            n = ov * lax.rsqrt(jnp.mean(ov * ov, axis=-1, keepdims=True) + EPS) * w_ref[...]
            halves.append(n * sz[:, hh * DH:(hh + 1) * DH])
        m = jnp.concatenate(halves, axis=1)
        m_ref[...] = m.astype(m_ref.dtype)
        mt_ref[...] = m.T.astype(mt_ref.dtype)

    return pl.pallas_call(
        body, name="gdn_post", grid=(WIDTH // LANE,),
        in_specs=[pl.BlockSpec((2, t, DH), lambda j: (j, 0, 0)), pl.BlockSpec((t, LANE), lambda j: (0, Z_COL0 + j)),
                  pl.BlockSpec((1, DH), lambda j: (0, 0))],
        out_specs=[pl.BlockSpec((t, LANE), lambda j: (0, j)), pl.BlockSpec((LANE, t), lambda j: (j, 0))],
        out_shape=[jax.ShapeDtypeStruct((t, WIDTH), MXU), jax.ShapeDtypeStruct((WIDTH, t), MXU)],
        compiler_params=_params(("arbitrary",)),
    )(o, pm, onw)


def _gdn_post_bwd(o, pm, onw, dmix):
    t = pm.shape[0]

    def body(o_ref, z_ref, w_ref, dm_ref, do_ref, dz_ref, dw_ref):
        @pl.when(pl.program_id(0) == 0)
        def _():
            dw_ref[...] = jnp.zeros_like(dw_ref)

        z = z_ref[...]
        sg = _sigmoid(z)
        sz = z * sg
        dsz = sg * (1.0 + z * (1.0 - sg))
        dm = dm_ref[...]
        for hh in range(2):
            cols = slice(hh * DH, (hh + 1) * DH)
            ov = o_ref[hh]
            r = lax.rsqrt(jnp.mean(ov * ov, axis=-1, keepdims=True) + EPS)
            xn = ov * r
            dmh = dm[:, cols]
            dn = dmh * sz[:, cols]
            dz_ref[:, cols] = dmh * (xn * w_ref[...]) * dsz[:, cols]
            dw_ref[...] += jnp.sum(dn * xn, axis=0, keepdims=True)
            g = dn * w_ref[...]
            do_ref[hh] = r * (g - xn * jnp.mean(g * xn, axis=-1, keepdims=True))

    return pl.pallas_call(
        body, name="gdn_post_bwd", grid=(WIDTH // LANE,),
        in_specs=[pl.BlockSpec((2, t, DH), lambda j: (j, 0, 0)), pl.BlockSpec((t, LANE), lambda j: (0, Z_COL0 + j)),
                  pl.BlockSpec((1, DH), lambda j: (0, 0)), pl.BlockSpec((t, LANE), lambda j: (0, j))],
        out_specs=[pl.BlockSpec((2, t, DH), lambda j: (j, 0, 0)), pl.BlockSpec((t, LANE), lambda j: (0, j)),
                   pl.BlockSpec((1, DH), lambda j: (0, 0))],
        out_shape=[jax.ShapeDtypeStruct((HEADS, t, DH), F32), jax.ShapeDtypeStruct((t, WIDTH), F32),
                   jax.ShapeDtypeStruct((1, DH), F32)],
        compiler_params=_params(("arbitrary",)),
    )(o, pm, onw, dmix)


def _fox_post(o, pm):
    t = pm.shape[0]

    def body(o_ref, g_ref, m_ref, mt_ref):
        m = jnp.concatenate([o_ref[0], o_ref[1]], axis=1) * _sigmoid(g_ref[...])
        m_ref[...] = m.astype(m_ref.dtype)
        mt_ref[...] = m.T.astype(mt_ref.dtype)

    return pl.pallas_call(
        body, name="fox_post", grid=(WIDTH // LANE,),
        in_specs=[pl.BlockSpec((2, t, DH), lambda j: (j, 0, 0)), pl.BlockSpec((t, LANE), lambda j: (0, FGATE_COL0 + j))],
        out_specs=[pl.BlockSpec((t, LANE), lambda j: (0, j)), pl.BlockSpec((LANE, t), lambda j: (j, 0))],
        out_shape=[jax.ShapeDtypeStruct((t, WIDTH), MXU), jax.ShapeDtypeStruct((WIDTH, t), MXU)],
        compiler_params=_params(("arbitrary",)),
    )(o, pm)


def _fox_post_bwd(o, pm, dmix):
    t = pm.shape[0]

    def body(o_ref, g_ref, dm_ref, do_ref, dg_ref):
        sg = _sigmoid(g_ref[...])
        dm = dm_ref[...]
        for hh in range(2):
            cols = slice(hh * DH, (hh + 1) * DH)
            do_ref[hh] = dm[:, cols] * sg[:, cols]
            dg_ref[:, cols] = dm[:, cols] * o_ref[hh] * (sg * (1.0 - sg))[:, cols]

    return pl.pallas_call(
        body, name="fox_post_bwd", grid=(WIDTH // LANE,),
        in_specs=[pl.BlockSpec((2, t, DH), lambda j: (j, 0, 0)), pl.BlockSpec((t, LANE), lambda j: (0, FGATE_COL0 + j)),
                  pl.BlockSpec((t, LANE), lambda j: (0, j))],
        out_specs=[pl.BlockSpec((2, t, DH), lambda j: (j, 0, 0)), pl.BlockSpec((t, LANE), lambda j: (0, j))],
        out_shape=[jax.ShapeDtypeStruct((HEADS, t, DH), F32), jax.ShapeDtypeStruct((t, WIDTH), F32)],
        compiler_params=_params(("arbitrary",)),
    )(o, pm, dmix)


def _tail(x, mixg, mixf, tgt, wo, n2w, wg, wu, wd, fw):
    t, d = x.shape
    dff = wg.shape[1]
    tb = min(TB, t)

    def body(x_ref, mg_ref, mf_ref, t_ref, wo_ref, n2_ref, wg_ref, wu_ref, wd_ref, fw_ref,
             h2_ref, act_ref, dgate_ref, dup_ref, dx3_ref, dx2_ref, dmg_ref, dmf_ref, dn2_ref, dfw_ref, loss_ref):
        @pl.when(pl.program_id(0) == 0)
        def _():
            dn2_ref[...] = jnp.zeros_like(dn2_ref)
            dfw_ref[...] = jnp.zeros_like(dfw_ref)
            loss_ref[...] = jnp.zeros_like(loss_ref)

        x2 = x_ref[...] + _dot(mg_ref[...], wo_ref[0:WIDTH, :]) + _dot(mf_ref[...], wo_ref[WIDTH:2 * WIDTH, :])
        r2 = lax.rsqrt(jnp.mean(x2 * x2, axis=-1, keepdims=True) + EPS)
        xn2 = x2 * r2
        h2_32 = xn2 * n2_ref[...]
        h2 = h2_32.astype(MXU)
        h2_ref[...] = h2_32.T.astype(MXU)
        gate = _dot(h2, wg_ref[...])
        up = _dot(h2, wu_ref[...])
        sg = _sigmoid(gate)
        sl = gate * sg
        act_32 = sl * up
        act = act_32.astype(MXU)
        act_ref[...] = act_32.T.astype(MXU)
        x3 = x2 + _dot(act, wd_ref[...])
        r3 = lax.rsqrt(jnp.mean(x3 * x3, axis=-1, keepdims=True) + EPS)
        xn3 = x3 * r3
        err = xn3 * fw_ref[...] - t_ref[...]
        loss_ref[...] += 0.5 * jnp.sum(jnp.mean(err * err, axis=-1, keepdims=True), axis=0, keepdims=True)
        dy = err * (1.0 / d)
        dfw_ref[...] += jnp.sum(dy * xn3, axis=0, keepdims=True)
        g3 = dy * fw_ref[...]
        dx3 = r3 * (g3 - xn3 * jnp.mean(g3 * xn3, axis=-1, keepdims=True))
        dx3_ref[...] = dx3.astype(MXU)
        dact = _dot_nt(dx3, wd_ref[...])
        dgate = (dact * up * (sg * (1.0 + gate * (1.0 - sg)))).astype(MXU)
        dup = (dact * sl).astype(MXU)
        dgate_ref[...] = dgate
        dup_ref[...] = dup
        dh2 = _dot_nt(dgate, wg_ref[...]) + _dot_nt(dup, wu_ref[...])
        dn2_ref[...] += jnp.sum(dh2 * xn2, axis=0, keepdims=True)
        g2 = dh2 * n2_ref[...]
        dx2 = dx3 + r2 * (g2 - xn2 * jnp.mean(g2 * xn2, axis=-1, keepdims=True))
        dx2_ref[...] = dx2
        dmg_ref[...] = _dot_nt(dx2, wo_ref[0:WIDTH, :])
        dmf_ref[...] = _dot_nt(dx2, wo_ref[WIDTH:2 * WIDTH, :])

    def tok(n):
        return pl.BlockSpec((tb, n), lambda i: (i, 0))

    def tok_t(n):
        return pl.BlockSpec((n, tb), lambda i: (0, i))

    acc = pl.BlockSpec((1, d), lambda i: (0, 0))
    sds = jax.ShapeDtypeStruct
    return pl.pallas_call(
        body, name="tail", grid=(t // tb,),
        in_specs=[tok(d), tok(WIDTH), tok(WIDTH), tok(d), _resident(wo.shape), _resident((1, d)),
                  _resident(wg.shape), _resident(wu.shape), _resident(wd.shape), _resident((1, d))],
        out_specs=[tok_t(d), tok_t(dff), tok(dff), tok(dff), tok(d), tok(d), tok(WIDTH), tok(WIDTH), acc, acc,
                   pl.BlockSpec((1, 1), lambda i: (0, 0))],
        out_shape=[sds((d, t), MXU), sds((dff, t), MXU), sds((t, dff), MXU), sds((t, dff), MXU), sds((t, d), MXU),
                   sds((t, d), F32), sds((t, WIDTH), F32), sds((t, WIDTH), F32), sds((1, d), F32), sds((1, d), F32),
                   sds((1, 1), F32)],
        compiler_params=_params(("arbitrary",)),
    )(x, mixg, mixf, tgt, wo, n2w, wg, wu, wd, fw)


def _wgrad(at, b, name):
    m, t = at.shape
    n = b.shape[1]
    bm = 256 if m % 256 == 0 else LANE
    cast = b.dtype != jnp.dtype(MXU)

    def body(a_ref, b_ref, o_ref, *scratch):
        if cast:
            @pl.when(pl.program_id(0) == 0)
            def _():
                scratch[0][...] = b_ref[...].astype(MXU)
        o_ref[...] = jnp.dot(a_ref[...], scratch[0][...] if cast else b_ref[...],
                             preferred_element_type=F32).astype(o_ref.dtype)

    return pl.pallas_call(
        body, name=name, grid=(m // bm,),
        in_specs=[pl.BlockSpec((bm, t), lambda i: (i, 0)), _resident((t, n))],
        out_specs=pl.BlockSpec((bm, n), lambda i: (i, 0)),
        out_shape=jax.ShapeDtypeStruct((m, n), WIRE),
        scratch_shapes=[pltpu.VMEM((t, n), MXU)] if cast else [],
        compiler_params=_params(("arbitrary",)),
    )(at, b)


def _split_w_in(w_in):
    a = 4 * WIDTH
    b = a + 2 * HEADS
    c = b + 4 * WIDTH
    main = jnp.concatenate([w_in[:, :a], w_in[:, b:c]], axis=1)
    small = jnp.concatenate([w_in[:, a:b], w_in[:, c:], jnp.zeros((w_in.shape[0], LANE - 3 * HEADS), w_in.dtype)], axis=1)
    return main, small


def _merge_dw_in(d_gdn, d_z, d_fox, d_fg, d_small):
    return jnp.concatenate([d_gdn, d_z, d_small[:, :2 * HEADS], d_fox, d_fg, d_small[:, 2 * HEADS:3 * HEADS]], axis=1)


def _lanes(*pieces):
    v = jnp.concatenate([p.reshape(-1).astype(F32) for p in pieces])
    return jnp.pad(v, (0, LANE - v.shape[0])).reshape(1, LANE)


def _vector_params(p):
    d = p["norm1_w"].size
    gparams = jnp.concatenate([_lanes(jnp.zeros(HEADS), p["gdn_dt_bias"], p["fox_f_bias"]),
                               _lanes(jnp.zeros(HEADS), p["gdn_A_log"]), jnp.zeros((6, LANE), F32)])
    fox_nw = jnp.stack([jnp.tile(p["fox_q_norm_w"].reshape(-1), 2), jnp.tile(p["fox_k_norm_w"].reshape(-1), 2),
                        jnp.ones((LANE,), F32)])
    return dict(n1w=p["norm1_w"].reshape(1, d), n2w=p["norm2_w"].reshape(1, d), fw=p["final_norm_w"].reshape(1, d),
                onw=p["gdn_out_norm_w"].reshape(1, DH), gparams=gparams, fox_nw=fox_nw)


def _mixer_forward(x, vp, w_main, w_small, conv_w, carry=None):
    h1, pm, ps = _inproj(x, vp["n1w"], w_main, w_small)
    gates, run, tot, run_t = _gates(ps, vp["gparams"])
    fqkv = _fox_prep(pm, vp["fox_nw"])
    o_fox, lse, *carried = _fox_fwd(fqkv, run, run_t, carry)
    mixf, mixf_t = _fox_post(o_fox, pm)
    gqkv = _gdn_prep(pm, conv_w)
    o_gdn, states, inv = _gdn_fwd(gqkv, gates, run, tot, run_t)
    mixg, mixg_t = _gdn_post(o_gdn, pm, vp["onw"])
    return dict(h1=h1, pm=pm, ps=ps, gates=gates, run=run, tot=tot, run_t=run_t, gqkv=gqkv, o_gdn=o_gdn, states=states,
                inv=inv, mixg=mixg, mixg_t=mixg_t, fqkv=fqkv, o_fox=o_fox, lse=lse, mixf=mixf, mixf_t=mixf_t,
                carried=carried[0] if carried else None)


def _mixer_backward(x, vp, w_main, w_small, conv_w, f, dx2, dmixg, dmixf, carry=None):
    pm = f["pm"]
    do_fox, dfg = _fox_post_bwd(f["o_fox"], pm, dmixf)
    dfqkv, dcol_f, drow_f, *carried = _fox_bwd(f["fqkv"], f["run"], f["run_t"], f["o_fox"], f["lse"], do_fox, carry)
    dfox, dfnw = _fox_prep_bwd(pm, vp["fox_nw"], dfqkv)
    do_gdn, dz, donw = _gdn_post_bwd(f["o_gdn"], pm, vp["onw"], dmixg)
    dgqkv, dcol_g, dtot_g, drow_g = _gdn_bwd(f["gqkv"], f["gates"], f["run"], f["tot"], f["run_t"], f["inv"], f["states"], do_gdn)
    dgdn, dconv = _gdn_prep_bwd(pm, conv_w, dgqkv)
    dps, gsum = _gates_bwd(f["ps"], vp["gparams"], dcol_g, dtot_g, dcol_f, drow_g, drow_f)
    grad_x, dn1w = _inproj_bwd(x, vp["n1w"], dx2, dgdn, dz, dfox, dfg, dps, w_main, w_small)
    h1 = f["h1"]
    dw_in = _merge_dw_in(_wgrad(h1, dgdn, "dw_in_gdn"), _wgrad(h1, dz, "dw_in_z"), _wgrad(h1, dfox, "dw_in_fox"),
                         _wgrad(h1, dfg, "dw_in_fgate"), _wgrad(h1, dps, "dw_in_small"))
    small = dict(dn1w=dn1w, gsum=gsum, donw=donw, dfnw=dfnw)
    return (grad_x, dw_in, dconv, small, carried[0]) if carried else (grad_x, dw_in, dconv, small)


VECTORS = ("norm1_w", "norm2_w", "final_norm_w", "gdn_A_log", "gdn_dt_bias", "gdn_out_norm_w", "fox_f_bias",
           "fox_q_norm_w", "fox_k_norm_w")
VEC_ROWS = 16
LOSS_ROW = len(VECTORS)


def _pack_vectors(dn1w, dn2w, dfw, gsum, donw, dfnw, loss):
    d = dn1w.shape[1]

    def body(n1_ref, n2_ref, fw_ref, gs_ref, on_ref, fn_ref, loss_ref, o_ref):
        o_ref[...] = jnp.zeros_like(o_ref)
        o_ref[0:1, :] = n1_ref[...]
        o_ref[1:2, :] = n2_ref[...]
        o_ref[2:3, :] = fw_ref[...]
        o_ref[3:4, 0:HEADS] = gs_ref[0:1, 0:HEADS]
        o_ref[4:5, 0:HEADS] = gs_ref[1:2, 0:HEADS]
        o_ref[5:6, 0:DH] = on_ref[...]
        o_ref[6:7, 0:HEADS] = gs_ref[2:3, 0:HEADS]
        for kind in range(2):
            v = fn_ref[kind, 0]
            for j in range(1, fn_ref.shape[1]):
                v = v + fn_ref[kind, j]
            o_ref[7 + kind:8 + kind, 0:DH] = v[:, :DH] + v[:, DH:]
        o_ref[LOSS_ROW:LOSS_ROW + 1, 0:1] = loss_ref[...]

    return pl.pallas_call(body, name="pack_vectors", out_shape=jax.ShapeDtypeStruct((VEC_ROWS, d), F32),
                          compiler_params=_params())(dn1w, dn2w, dfw, gsum, donw, dfnw, loss)


def _late_grads(f, h2t, actt, dgate, dup, dx3, dx2):
    return {"w_out": jnp.concatenate([_wgrad(f["mixg_t"], dx2, "dw_out_gdn"), _wgrad(f["mixf_t"], dx2, "dw_out_fox")], axis=0),
            "w_ffn_gate": _wgrad(h2t, dgate, "dw_gate"), "w_ffn_up": _wgrad(h2t, dup, "dw_up"),
            "w_ffn_down": _wgrad(actt, dx3, "dw_down")}


def _local_step(x, tgt, p, w_in, conv_w, wo, wg, wu, wd):
    vp = _vector_params(p)
    w_main, w_small = _split_w_in(w_in)
    f = _mixer_forward(x, vp, w_main, w_small, conv_w)
    (h2t, actt, dgate, dup, dx3, dx2, dmixg, dmixf, dn2w, dfw, loss) = _tail(
        x, f["mixg"], f["mixf"], tgt, wo, vp["n2w"], wg, wu, wd, vp["fw"])
    grad_x, dw_in, dconv, small = _mixer_backward(x, vp, w_main, w_small, conv_w, f, dx2, dmixg, dmixf)
    grads = {"w_in": dw_in, "gdn_conv_w": dconv, **_late_grads(f, h2t, actt, dgate, dup, dx3, dx2)}
    vec = _pack_vectors(small["dn1w"], dn2w, dfw, small["gsum"], small["donw"], small["dfnw"], loss)
    return grad_x, grads, vec


def _my_place():
    return lax.axis_index("x"), lax.axis_index("y"), lax.axis_index("c")


def _peers():
    x, y, c = _my_place()
    peers = []
    for k in range(1, N_DEV):
        px = 1 - x if k & 4 else x
        py = 1 - y if k & 2 else y
        pc = 1 - c if k & 1 else c
        peers.append(((px, py, pc), 4 * px + 2 * py + pc))
    return 4 * x + 2 * y + c, peers


def _spread_copies(kind, srcs, lands, send_sems, recv_sems):
    me, peers = _peers()
    kinds = [kind] * len(srcs) if isinstance(kind, str) else kind
    remote, local = [], []
    for i, (kd, src, land) in enumerate(zip(kinds, srcs, lands)):
        for k, (dev, idx) in enumerate(peers):
            remote.append(pltpu.make_async_remote_copy(
                src_ref=src if kd == "gather" else src.at[idx], dst_ref=land.at[me],
                send_sem=send_sems.at[i * (N_DEV - 1) + k], recv_sem=recv_sems.at[i * (N_DEV - 1) + k],
                device_id=dev, device_id_type=MESH))
        local.append((src if kd == "gather" else src.at[me], land.at[me]))
    return remote, local


def _land_shape(kind, a):
    return (N_DEV,) + a.shape if kind == "gather" else a.shape


HBM = pl.BlockSpec(memory_space=pltpu.HBM)
SEM = pl.BlockSpec(memory_space=pltpu.SEMAPHORE)


def _hbm(a):
    return pltpu.with_memory_space_constraint(a, pltpu.HBM)


def _carry_operands(kind, arrays):
    n = len(arrays)
    kinds = [kind] * n if isinstance(kind, str) else kind
    lands = [lax.empty(_land_shape(kd, a), a.dtype) for kd, a in zip(kinds, arrays)]
    sems = [pltpu.SemaphoreType.DMA((n * (N_DEV - 1),))] * 2
    return ([_hbm(a) for a in list(arrays) + lands], [HBM] * (2 * n),
            sems + [pltpu.HBM(a.shape, a.dtype) for a in list(arrays) + lands], [SEM] * 2 + [HBM] * (2 * n),
            pltpu.SemaphoreType.DMA((n,)))


def _carry_begin(kind, in_refs, send_sems, recv_sems, local_sems):
    n = len(in_refs) // 2
    remote, local = _spread_copies(kind, in_refs[:n], in_refs[n:], send_sems, recv_sems)
    for cp in remote:
        cp.start()
    copies = [pltpu.make_async_copy(s, d, local_sems.at[i]) for i, (s, d) in enumerate(local)]
    for cp in copies:
        cp.start()
    for cp in copies:
        cp.wait()


def _carry_state(extra_out):
    n = (len(extra_out) - 2) // 2
    return list(extra_out[2:2 + n]), list(extra_out[2 + n:]), extra_out[0], extra_out[1]


def _spread_start(arrays, kind, name):
    n = len(arrays)

    def body(*refs):
        srcs, lands = refs[:n], refs[n:2 * n]
        send_sems, recv_sems = refs[2 * n], refs[2 * n + 1]
        token = refs[4 * n + 2]
        local_sems = refs[4 * n + 3]
        remote, local = _spread_copies(kind, srcs, lands, send_sems, recv_sems)
        for cp in remote:
            cp.start()
        copies = [pltpu.make_async_copy(s, d, local_sems.at[i]) for i, (s, d) in enumerate(local)]
        for cp in copies:
            cp.start()
        for cp in copies:
            cp.wait()
        token[...] = jnp.zeros_like(token)

    sems = (pltpu.SemaphoreType.DMA((n * (N_DEV - 1),)),) * 2
    kinds = [kind] * n if isinstance(kind, str) else kind
    lands = [lax.empty(_land_shape(kd, a), a.dtype) for kd, a in zip(kinds, arrays)]
    out = pl.pallas_call(
        body, name=name,
        out_shape=sems + tuple(pltpu.HBM(a.shape, a.dtype) for a in list(arrays) + lands)
        + (jax.ShapeDtypeStruct((8, LANE), F32),),
        in_specs=[HBM] * (2 * n), out_specs=tuple([SEM] * 2 + [HBM] * (2 * n) + [pl.BlockSpec(memory_space=pltpu.VMEM)]),
        input_output_aliases={j: 2 + j for j in range(2 * n)},
        scratch_shapes=[pltpu.SemaphoreType.DMA((n,))],
        compiler_params=pltpu.CompilerParams(has_side_effects=pltpu.SideEffectType.DATAFLOW_SIDE_EFFECTING),
    )(*[_hbm(a) for a in arrays], *[_hbm(a) for a in lands])
    return (list(out[2:2 + n]), list(out[2 + n:2 + 2 * n]), out[0], out[1]), out[-1]


def _spread_wait(state, kind, after, name):
    srcs, lands, send_sems, recv_sems = state
    n = len(srcs)
    after = list(after) if isinstance(after, (list, tuple)) else [after]

    def body(*refs):
        remote, _ = _spread_copies(kind, refs[:n], refs[n:2 * n], refs[2 * n], refs[2 * n + 1])
        for cp in remote:
            cp.wait_send()
        for cp in remote:
            cp.wait_recv()

    out = pl.pallas_call(
        body, name=name,
        out_shape=tuple(pltpu.HBM(a.shape, a.dtype) for a in srcs + lands),
        in_specs=[HBM] * (2 * n) + [SEM, SEM] + [pl.BlockSpec(memory_space=pl.ANY)] * len(after),
        out_specs=tuple([HBM] * (2 * n)),
        input_output_aliases={j: j for j in range(2 * n)},
        compiler_params=pltpu.CompilerParams(has_side_effects=pltpu.SideEffectType.DATAFLOW_SIDE_EFFECTING),
    )(*srcs, *lands, send_sems, recv_sems, *after)
    return list(out[n:])


ADAM_ROWS = 128


def _adam_math(g, w, m, v):
    nm = ADAM_B1 * m + (1.0 - ADAM_B1) * g
    nv = ADAM_B2 * v + (1.0 - ADAM_B2) * (g * g)
    m_hat = nm / (1.0 - ADAM_B1 ** ADAM_STEP)
    v_hat = nv / (1.0 - ADAM_B2 ** ADAM_STEP)
    return -ADAM_LR * (m_hat / (jnp.sqrt(v_hat) + ADAM_EPS) + ADAM_WD * w), nm, nv


def _sum_parts(p_ref):
    g = p_ref[0].astype(F32)
    for s in range(1, N_DEV):
        g = g + p_ref[s].astype(F32)
    return g


def _adam_matrix(parts, w, m, v, name):
    _, r, c = w.shape
    rb = ADAM_ROWS if r % ADAM_ROWS == 0 else r

    def body(p_ref, w_ref, m_ref, v_ref, g_ref, d_ref, nm_ref, nv_ref):
        g = _sum_parts(p_ref)
        g_ref[0] = g
        d_ref[0], nm_ref[0], nv_ref[0] = _adam_math(g, w_ref[0], m_ref[0], v_ref[0])

    blk = pl.BlockSpec((1, rb, c), lambda i: (0, i, 0))
    return pl.pallas_call(
        body, name=name, grid=(r // rb,),
        in_specs=[pl.BlockSpec((N_DEV, rb, c), lambda i: (0, i, 0)), blk, blk, blk],
        out_specs=[blk] * 4, out_shape=[jax.ShapeDtypeStruct(w.shape, F32)] * 4,
        compiler_params=_params(("arbitrary",)),
    )(parts, w, m, v)


def _adam_vectors(parts, ws, ms, vs):
    nv = len(ws)

    def body(*refs):
        p_ref = refs[0]
        w_refs, m_refs, v_refs = refs[1:1 + nv], refs[1 + nv:1 + 2 * nv], refs[1 + 2 * nv:1 + 3 * nv]
        outs = refs[1 + 3 * nv:]
        g_all = _sum_parts(p_ref)
        for i in range(nv):
            n = w_refs[i].shape[1]
            g = g_all[i:i + 1, 0:n]
            d, nm, nvv = _adam_math(g, w_refs[i][...], m_refs[i][...], v_refs[i][...])
            outs[i][...] = g
            outs[nv + i][...] = d
            outs[2 * nv + i][...] = nm
            outs[3 * nv + i][...] = nvv
        outs[4 * nv][...] = g_all[LOSS_ROW:LOSS_ROW + 1, 0:1]

    shapes = [jax.ShapeDtypeStruct(a.shape, F32) for a in ws]
    out = pl.pallas_call(body, name="adam_vectors", out_shape=shapes * 4 + [jax.ShapeDtypeStruct((1, 1), F32)],
                         compiler_params=_params())(parts, *ws, *ms, *vs)
    return out[:nv], out[nv:2 * nv], out[2 * nv:3 * nv], out[3 * nv:4 * nv], out[4 * nv]


MATRICES = (("w_in", 1), ("gdn_conv_w", 1), ("w_out", 0), ("w_ffn_gate", 1), ("w_ffn_up", 1), ("w_ffn_down", 0))
WEIGHTS = ("norm1_w", "w_in", "gdn_conv_w", "gdn_A_log", "gdn_dt_bias", "gdn_out_norm_w", "fox_f_bias", "fox_q_norm_w",
           "fox_k_norm_w", "w_out", "norm2_w", "w_ffn_gate", "w_ffn_up", "w_ffn_down", "final_norm_w")


def _join(blocks, axis):
    _, r, c = blocks.shape
    if axis == 0:
        return blocks.reshape(N_DEV * r, c)
    return blocks.transpose(1, 0, 2).reshape(r, N_DEV * c)


def _cut(full, axis):
    r, c = full.shape
    if axis == 0:
        return full.reshape(N_DEV, r // N_DEV, c)
    return full.reshape(r, N_DEV, c // N_DEV).transpose(1, 0, 2)


def kernel(x, norm1_w, w_in, gdn_conv_w, gdn_A_log, gdn_dt_bias, gdn_out_norm_w, fox_f_bias, fox_q_norm_w, fox_k_norm_w, w_out, norm2_w, w_ffn_gate, w_ffn_up, w_ffn_down, final_norm_w, loss_target, m_norm1_w, m_w_in, m_gdn_conv_w, m_gdn_A_log, m_gdn_dt_bias, m_gdn_out_norm_w, m_fox_f_bias, m_fox_q_norm_w, m_fox_k_norm_w, m_w_out, m_norm2_w, m_w_ffn_gate, m_w_ffn_up, m_w_ffn_down, m_final_norm_w, v_norm1_w, v_w_in, v_gdn_conv_w, v_gdn_A_log, v_gdn_dt_bias, v_gdn_out_norm_w, v_fox_f_bias, v_fox_q_norm_w, v_fox_k_norm_w, v_w_out, v_norm2_w, v_w_ffn_gate, v_w_ffn_up, v_w_ffn_down, v_final_norm_w):
    w = dict(norm1_w=norm1_w, w_in=w_in, gdn_conv_w=gdn_conv_w, gdn_A_log=gdn_A_log, gdn_dt_bias=gdn_dt_bias,
             gdn_out_norm_w=gdn_out_norm_w, fox_f_bias=fox_f_bias, fox_q_norm_w=fox_q_norm_w, fox_k_norm_w=fox_k_norm_w,
             w_out=w_out, norm2_w=norm2_w, w_ffn_gate=w_ffn_gate, w_ffn_up=w_ffn_up, w_ffn_down=w_ffn_down,
             final_norm_w=final_norm_w)
    m = dict(norm1_w=m_norm1_w, w_in=m_w_in, gdn_conv_w=m_gdn_conv_w, gdn_A_log=m_gdn_A_log, gdn_dt_bias=m_gdn_dt_bias,
             gdn_out_norm_w=m_gdn_out_norm_w, fox_f_bias=m_fox_f_bias, fox_q_norm_w=m_fox_q_norm_w,
             fox_k_norm_w=m_fox_k_norm_w, w_out=m_w_out, norm2_w=m_norm2_w, w_ffn_gate=m_w_ffn_gate,
             w_ffn_up=m_w_ffn_up, w_ffn_down=m_w_ffn_down, final_norm_w=m_final_norm_w)
    v = dict(norm1_w=v_norm1_w, w_in=v_w_in, gdn_conv_w=v_gdn_conv_w, gdn_A_log=v_gdn_A_log, gdn_dt_bias=v_gdn_dt_bias,
             gdn_out_norm_w=v_gdn_out_norm_w, fox_f_bias=v_fox_f_bias, fox_q_norm_w=v_fox_q_norm_w,
             fox_k_norm_w=v_fox_k_norm_w, w_out=v_w_out, norm2_w=v_norm2_w, w_ffn_gate=v_w_ffn_gate,
             w_ffn_up=v_w_ffn_up, w_ffn_down=v_w_ffn_down, final_norm_w=v_final_norm_w)
    axis_of = dict(MATRICES)
    late = ("w_out", "w_ffn_gate", "w_ffn_up", "w_ffn_down")
    xs, tgt = x[0], loss_target[0]
    vp = _vector_params({n: w[n] for n in VECTORS})

    state_in, token = _spread_start([w["w_in"][0].astype(WIRE), w["gdn_conv_w"][0]], "gather", "gather_in_start")
    w_in_blocks, conv_blocks = _spread_wait(state_in, "gather", token, "gather_in_wait")
    w_main, w_small = _split_w_in(_join(w_in_blocks, 1))
    conv_w = _join(conv_blocks, 1)
    f = _mixer_forward(xs, vp, w_main, w_small, conv_w, ("gather", [w[n][0].astype(WIRE) for n in late]))
    full = {n: _join(b, axis_of[n])
            for n, b in zip(late, _spread_wait(f["carried"], "gather", f["mixg"], "gather_late_wait"))}

    (h2t, actt, dgate, dup, dx3, dx2, dmixg, dmixf, dn2w, dfw, loss) = _tail(
        xs, f["mixg"], f["mixf"], tgt, full["w_out"], vp["n2w"], full["w_ffn_gate"], full["w_ffn_up"],
        full["w_ffn_down"], vp["fw"])
    dlate = _late_grads(f, h2t, actt, dgate, dup, dx3, dx2)

    grad_x, dw_in, dconv, small, state_grads = _mixer_backward(
        xs, vp, w_main, w_small, conv_w, f, dx2, dmixg, dmixf, ("scatter", [_cut(dlate[n], axis_of[n]) for n in late]))
    vec = _pack_vectors(small["dn1w"], dn2w, dfw, small["gsum"], small["donw"], small["dfnw"], loss)

    last_kinds = ["scatter", "scatter", "gather"]
    state_last, token = _spread_start([_cut(dw_in, 1), _cut(dconv, 1), vec], last_kinds, "last_start")
    parts = dict(zip(late, _spread_wait(state_grads, "scatter", token, "grads_late_wait")))
    results = [{}, {}, {}, {}]

    def update(n):
        for d, a in zip(results, _adam_matrix(parts[n], w[n], m[n], v[n], "adam_" + n)):
            d[n] = a

    for n in late:
        update(n)
    parts["w_in"], parts["gdn_conv_w"], parts_vec = _spread_wait(state_last, last_kinds, [results[0][n] for n in late],
                                                                 "last_wait")
    update("w_in")
    update("gdn_conv_w")
    row = lambda a: a.reshape(1, -1)
    *vec_out, total_loss = _adam_vectors(parts_vec, [row(w[n]) for n in VECTORS], [row(m[n]) for n in VECTORS],
                                         [row(v[n]) for n in VECTORS])
    for d, arrs in zip(results, vec_out):
        for n, a in zip(VECTORS, arrs):
            d[n] = a.reshape(w[n].shape)
    return (total_loss[0, 0], grad_x[None], *[d[n] for d in results for n in WEIGHTS])
```

```python
import functools

import jax
import jax.numpy as jnp
from jax import lax
from jax.experimental import pallas as pl
from jax.experimental.pallas import tpu as pltpu

F32 = jnp.float32
MXU = jnp.bfloat16
WIRE = jnp.bfloat16
HI = lax.Precision.HIGHEST
EPS = 1e-6

N_DEV = 8
HEADS = 8
DH = 64
WIDTH = HEADS * DH
CHUNK = 64
LANE = 128
ROW_ALIGN = 16
TB = 256
QB = 256
VMEM_LIMIT = 60 * 1024 * 1024

ADAM_LR = 0.001
ADAM_B1 = 0.9
ADAM_B2 = 0.999
ADAM_EPS = 1e-08
ADAM_WD = 0.01
ADAM_STEP = 10

MESH = pl.DeviceIdType.MESH


def _params(sem=None):
    return pltpu.CompilerParams(dimension_semantics=sem, vmem_limit_bytes=VMEM_LIMIT)


def _resident(shape):
    n = len(shape)
    return pl.BlockSpec(shape, lambda *_: (0,) * n, pipeline_mode=pl.Buffered(1))


def _dot(a, b):
    return jnp.dot(a.astype(MXU), b.astype(MXU), preferred_element_type=F32)


def _dot_nt(a, b):
    return lax.dot_general(a.astype(MXU), b.astype(MXU), (((1,), (1,)), ((), ())), preferred_element_type=F32)


def _dot_tn(a, b):
    return lax.dot_general(a.astype(MXU), b.astype(MXU), (((0,), (0,)), ((), ())), preferred_element_type=F32)


def _hdot(a, b):
    return jnp.dot(a, b, precision=HI, preferred_element_type=F32)


def _hdot_nt(a, b):
    return lax.dot_general(a, b, (((1,), (1,)), ((), ())), precision=HI, preferred_element_type=F32)


def _hdot_tn(a, b):
    return lax.dot_general(a, b, (((0,), (0,)), ((), ())), precision=HI, preferred_element_type=F32)


def _sigmoid(x):
    return 0.5 * jnp.tanh(0.5 * x) + 0.5


def _softplus(x):
    return jnp.maximum(x, 0.0) + jnp.log(1.0 + jnp.exp(-jnp.abs(x)))


def _head_sum_matrix():
    ri = lax.broadcasted_iota(jnp.int32, (LANE, LANE), 0) // DH
    ci = lax.broadcasted_iota(jnp.int32, (LANE, LANE), 1) // DH
    return (ri == ci).astype(F32)


def _group_sum(a, ones_matrix):
    hi = a.astype(jnp.bfloat16)
    lo = (a - hi.astype(F32)).astype(jnp.bfloat16)
    m = ones_matrix.astype(jnp.bfloat16)
    return jnp.dot(hi, m, preferred_element_type=F32) + jnp.dot(lo, m, preferred_element_type=F32)


def _shift_down(x, s):
    return pltpu.roll(x, s, 0)


def _shift_up(x, s):
    return pltpu.roll(x, x.shape[0] - s, 0)


def _inproj(x, n1w, w_main, w_small):
    t, d = x.shape
    nm = w_main.shape[1]
    tb = min(TB, t)

    def body(x_ref, nw_ref, wm_ref, ws_ref, ht_ref, pm_ref, ps_ref):
        xv = x_ref[...]
        r = lax.rsqrt(jnp.mean(xv * xv, axis=-1, keepdims=True) + EPS)
        h32 = xv * r * nw_ref[...]
        h = h32.astype(MXU)
        ht_ref[...] = h32.T.astype(MXU)
        pm_ref[...] = jnp.dot(h, wm_ref[...], preferred_element_type=F32)
        ps_ref[...] = jnp.dot(h, ws_ref[...], preferred_element_type=F32)

    return pl.pallas_call(
        body, name="inproj", grid=(t // tb,),
        in_specs=[pl.BlockSpec((tb, d), lambda i: (i, 0)), _resident((1, d)), _resident((d, nm)), _resident((d, LANE))],
        out_specs=[pl.BlockSpec((d, tb), lambda i: (0, i)), pl.BlockSpec((tb, nm), lambda i: (i, 0)),
                   pl.BlockSpec((tb, LANE), lambda i: (i, 0))],
        out_shape=[jax.ShapeDtypeStruct((d, t), MXU), jax.ShapeDtypeStruct((t, nm), F32),
                   jax.ShapeDtypeStruct((t, LANE), F32)],
        compiler_params=_params(("arbitrary",)),
    )(x, n1w, w_main, w_small)


def _inproj_bwd(x, n1w, dx2, dgdn, dz, dfox, dfg, dps, w_main, w_small):
    t, d = x.shape
    tb = min(TB, t)
    w3 = 3 * WIDTH

    def body(x_ref, nw_ref, dx2_ref, dgdn_ref, dz_ref, dfox_ref, dfg_ref, dps_ref, wm_ref, ws_ref, gx_ref, dnw_ref):
        dh = _dot_nt(dgdn_ref[...], wm_ref[:, 0:w3])
        dh += _dot_nt(dz_ref[...], wm_ref[:, w3:w3 + WIDTH])
        dh += _dot_nt(dfox_ref[...], wm_ref[:, w3 + WIDTH:2 * w3 + WIDTH])
        dh += _dot_nt(dfg_ref[...], wm_ref[:, 2 * w3 + WIDTH:2 * w3 + 2 * WIDTH])
        dh += _dot_nt(dps_ref[...], ws_ref[...])
        xv = x_ref[...]
        r = lax.rsqrt(jnp.mean(xv * xv, axis=-1, keepdims=True) + EPS)
        xn = xv * r

        @pl.when(pl.program_id(0) == 0)
        def _():
            dnw_ref[...] = jnp.zeros_like(dnw_ref)

        dnw_ref[...] += jnp.sum(dh * xn, axis=0, keepdims=True)
        g = dh * nw_ref[...]
        gx_ref[...] = dx2_ref[...] + r * (g - xn * jnp.mean(g * xn, axis=-1, keepdims=True))

    def tok(n):
        return pl.BlockSpec((tb, n), lambda i: (i, 0))

    return pl.pallas_call(
        body, name="inproj_bwd", grid=(t // tb,),
        in_specs=[tok(d), _resident((1, d)), tok(d), tok(w3), tok(WIDTH), tok(w3), tok(WIDTH), tok(LANE),
                  _resident(w_main.shape), _resident(w_small.shape)],
        out_specs=[tok(d), pl.BlockSpec((1, d), lambda i: (0, 0))],
        out_shape=[jax.ShapeDtypeStruct((t, d), F32), jax.ShapeDtypeStruct((1, d), F32)],
        compiler_params=_params(("arbitrary",)),
    )(x, n1w, dx2, dgdn, dz, dfox, dfg, dps, w_main, w_small)


def _gate_lanes(shape):
    lane = lax.broadcasted_iota(jnp.int32, shape, 1)
    return lane < HEADS, (lane >= HEADS) & (lane < 2 * HEADS), (lane >= 2 * HEADS) & (lane < 3 * HEADS)


def _block_masks():
    ri = lax.broadcasted_iota(jnp.int32, (LANE, LANE), 0)
    ci = lax.broadcasted_iota(jnp.int32, (LANE, LANE), 1)
    same = (ri // CHUNK) == (ci // CHUNK)
    return ((ri >= ci).astype(F32), (ri <= ci).astype(F32), (same & (ri >= ci)).astype(F32),
            (same & (ri <= ci)).astype(F32), same.astype(F32))


def _gates(ps, gparams):
    t = ps.shape[0]
    nb = t // LANE

    def body(ps_ref, gp_ref, out_ref, run_ref, tot_ref, runt_ref):
        p = ps_ref[...]
        is_b, is_a, is_f = _gate_lanes(p.shape)
        z = p + gp_ref[0:1, :]
        neg_exp_a = -jnp.exp(gp_ref[1:2, :])
        glog = neg_exp_a * _softplus(z)
        logf = -_softplus(-z)
        out_ref[...] = jnp.where(is_b, _sigmoid(p), jnp.where(is_a, glog, jnp.where(is_f, logf, 0.0)))
        tril, _, tril_c, _, same_c = _block_masks()
        off = jnp.zeros((1, LANE), F32)
        for b in range(nb):
            rows = slice(b * LANE, (b + 1) * LANE)
            blk = out_ref[rows, :]
            ga = jnp.where(is_a[:LANE], blk, 0.0)
            fb = _hdot(tril, jnp.where(is_f[:LANE], blk, 0.0)) + off
            run = fb + _hdot(tril_c, ga)
            run_ref[rows, :] = run
            runt_ref[:, rows] = run.T
            tot_ref[rows, :] = _hdot(same_c, ga)
            off = fb[LANE - 1:LANE, :]

    return pl.pallas_call(
        body, name="gates",
        out_shape=[jax.ShapeDtypeStruct((t, LANE), F32)] * 3 + [jax.ShapeDtypeStruct((LANE, t), F32)],
        compiler_params=_params(),
    )(ps, gparams)


def _gates_bwd(ps, gparams, dcol_g, dtot_g, dcol_f, drow_g, drow_f):
    t = ps.shape[0]
    nb = t // LANE

    def body(ps_ref, gp_ref, dcg_ref, dtg_ref, dcf_ref, drg_ref, drf_ref, dps_ref, sums_ref, dl_ref, d0_ref, tr_ref):
        p = ps_ref[...]
        is_b, is_a, is_f = _gate_lanes(p.shape)
        _, triu, _, triu_c, same_c = _block_masks()
        tr_ref[...] = jnp.zeros_like(tr_ref)
        off = jnp.zeros((1, LANE), F32)
        for b in reversed(range(nb)):
            rows = slice(b * LANE, (b + 1) * LANE)
            tr_ref[HEADS:2 * HEADS, :] = drg_ref[:, rows]
            tr_ref[2 * HEADS:3 * HEADS, :] = drf_ref[:, rows]
            d = dcg_ref[rows, :] + dcf_ref[rows, :] + tr_ref[...].T
            d0_ref[rows, :] = d
            dlf = _hdot(triu, jnp.where(is_f[:LANE], d, 0.0)) + off
            dla = (_hdot(triu_c, jnp.where(is_a[:LANE], d, 0.0))
                   + _hdot(same_c, jnp.where(is_a[:LANE], dtg_ref[rows, :], 0.0)))
            dl_ref[rows, :] = dlf + dla
            off = dlf[0:1, :]
        z = p + gp_ref[0:1, :]
        neg_exp_a = -jnp.exp(gp_ref[1:2, :])
        sb = _sigmoid(p)
        glog = neg_exp_a * _softplus(z)
        dl = dl_ref[...]
        dp = jnp.where(is_b, d0_ref[...] * sb * (1.0 - sb),
                       jnp.where(is_a, dl * neg_exp_a * _sigmoid(z), jnp.where(is_f, dl * _sigmoid(-z), 0.0)))
        dps_ref[...] = dp
        s_a = jnp.sum(jnp.where(is_a, dl * glog, 0.0), axis=0, keepdims=True)
        s_p = jnp.sum(jnp.where(is_b, 0.0, dp), axis=0, keepdims=True)
        row = lax.broadcasted_iota(jnp.int32, (8, LANE), 0)
        from_a = pltpu.roll(jnp.where(row == 0, s_a, jnp.where(row == 1, s_p, 0.0)), LANE - HEADS, 1)
        from_f = pltpu.roll(jnp.where(row == 2, s_p, 0.0), LANE - 2 * HEADS, 1)
        lane = lax.broadcasted_iota(jnp.int32, (8, LANE), 1)
        sums_ref[...] = jnp.where(lane < HEADS, from_a + from_f, 0.0)

    return pl.pallas_call(
        body, name="gates_bwd",
        out_shape=[jax.ShapeDtypeStruct((t, LANE), F32), jax.ShapeDtypeStruct((8, LANE), F32)],
        scratch_shapes=[pltpu.VMEM((t, LANE), F32), pltpu.VMEM((t, LANE), F32), pltpu.VMEM((LANE, LANE), F32)],
        compiler_params=_params(),
    )(ps, gparams, dcol_g, dtot_g, dcol_f, drow_g, drow_f)


def _conv(xv, w):
    acc = w[3:4, :] * xv
    for s in range(1, 4):
        acc += w[3 - s:4 - s, :] * _shift_down(xv, s)
    return acc


PREP_ROWS = 256
HALO = 8


def _tile_rows(r):
    return pl.ds(pl.multiple_of(r * PREP_ROWS, PREP_ROWS), PREP_ROWS)


def _gdn_prep(pm, conv_w):
    t = pm.shape[0]
    nj = WIDTH // LANE
    win = PREP_ROWS + HALO

    def body(x_ref, w_ref, o_ref, xp_ref):
        kind = pl.program_id(0)
        xp_ref[0:HALO, :] = jnp.zeros((HALO, LANE), F32)
        xp_ref[HALO:, :] = x_ref[...]
        w = w_ref[...]
        hs = _head_sum_matrix()

        def tile(r, _):
            xw = xp_ref[pl.ds(pl.multiple_of(r * PREP_ROWS, PREP_ROWS), win), :]
            acc = _conv(xw, w)[HALO:]
            y = acc * _sigmoid(acc)
            out = jnp.where(kind < 2, y * lax.rsqrt(_group_sum(y * y, hs) + EPS), y)
            o_ref[0, 0, _tile_rows(r), :] = out[:, :DH]
            o_ref[0, 1, _tile_rows(r), :] = out[:, DH:]
            return 0

        lax.fori_loop(0, t // PREP_ROWS, tile, 0)

    return pl.pallas_call(
        body, name="gdn_prep", grid=(3, nj),
        in_specs=[pl.BlockSpec((t, LANE), lambda i, j: (0, i * nj + j)),
                  pl.BlockSpec((4, LANE), lambda i, j: (0, i * nj + j))],
        out_specs=pl.BlockSpec((1, 2, t, DH), lambda i, j: (i, j, 0, 0)),
        out_shape=jax.ShapeDtypeStruct((3, HEADS, t, DH), F32),
        scratch_shapes=[pltpu.VMEM((t + HALO, LANE), F32)],
        compiler_params=_params(("arbitrary", "arbitrary")),
    )(pm, conv_w)


def _gdn_prep_bwd(pm, conv_w, dqkv):
    t = pm.shape[0]
    nj = WIDTH // LANE
    win = PREP_ROWS + 2 * HALO

    def body(x_ref, w_ref, d_ref, dx_ref, dw_ref, xp_ref, dp_ref):
        kind = pl.program_id(0)
        zeros = jnp.zeros((HALO, LANE), F32)
        for ref in (xp_ref, dp_ref):
            ref[0:HALO, :] = zeros
            ref[HALO + t:, :] = zeros
        xp_ref[HALO:HALO + t, :] = x_ref[...]
        dp_ref[HALO:HALO + t, 0:DH] = d_ref[0, 0]
        dp_ref[HALO:HALO + t, DH:] = d_ref[0, 1]
        w = w_ref[...]
        hs = _head_sum_matrix()
        rows = lax.broadcasted_iota(jnp.int32, (win, LANE), 0)
        in_tile = (rows >= HALO) & (rows < HALO + PREP_ROWS)

        def tile(r, dw):
            start = pl.multiple_of(r * PREP_ROWS, PREP_ROWS)
            xw = xp_ref[pl.ds(start, win), :]
            d = dp_ref[pl.ds(start, win), :]
            acc = _conv(xw, w)
            sg = _sigmoid(acc)
            y = acc * sg
            rn = lax.rsqrt(_group_sum(y * y, hs) + EPS)
            yn = y * rn
            dy = jnp.where(kind < 2, rn * (d - yn * _group_sum(d * yn, hs)), d)
            dacc = dy * sg * (1.0 + acc * (1.0 - sg))
            dx = w[3:4, :] * dacc
            for s in range(1, 4):
                dx += w[3 - s:4 - s, :] * _shift_up(dacc, s)
            dx_ref[_tile_rows(r), :] = dx[HALO:HALO + PREP_ROWS]
            dm = jnp.where(in_tile, dacc, 0.0)
            return tuple(dw[i] + jnp.sum(dm * (xw if i == 3 else _shift_down(xw, 3 - i)), axis=0, keepdims=True)
                         for i in range(4))

        dw = lax.fori_loop(0, t // PREP_ROWS, tile, tuple(jnp.zeros((1, LANE), F32) for _ in range(4)))
        for i in range(4):
            dw_ref[i:i + 1, :] = dw[i]

    return pl.pallas_call(
        body, name="gdn_prep_bwd", grid=(3, nj),
        in_specs=[pl.BlockSpec((t, LANE), lambda i, j: (0, i * nj + j)),
                  pl.BlockSpec((4, LANE), lambda i, j: (0, i * nj + j)),
                  pl.BlockSpec((1, 2, t, DH), lambda i, j: (i, j, 0, 0))],
        out_specs=[pl.BlockSpec((t, LANE), lambda i, j: (0, i * nj + j)),
                   pl.BlockSpec((4, LANE), lambda i, j: (0, i * nj + j))],
        out_shape=[jax.ShapeDtypeStruct((t, 3 * WIDTH), F32), jax.ShapeDtypeStruct((4, 3 * WIDTH), F32)],
        scratch_shapes=[pltpu.VMEM((t + 2 * HALO, LANE), F32), pltpu.VMEM((t + 2 * HALO, LANE), F32)],
        compiler_params=_params(("arbitrary", "arbitrary")),
    )(pm, conv_w, dqkv)


FOX_COL0 = 4 * WIDTH // LANE


def _fox_prep(pm, nw):
    t = pm.shape[0]
    nj = WIDTH // LANE

    def body(x_ref, w_ref, o_ref):
        kind = pl.program_id(0)
        hs = _head_sum_matrix()
        wk = w_ref[pl.ds(kind, 1), :]

        def tile(r, _):
            xv = x_ref[_tile_rows(r), :]
            ms = _group_sum(xv * xv, hs) * (1.0 / DH)
            out = jnp.where(kind < 2, xv * lax.rsqrt(ms + EPS) * wk, xv)
            o_ref[0, 0, _tile_rows(r), :] = out[:, :DH]
            o_ref[0, 1, _tile_rows(r), :] = out[:, DH:]
            return 0

        lax.fori_loop(0, t // PREP_ROWS, tile, 0)

    return pl.pallas_call(
        body, name="fox_prep", grid=(3, nj),
        in_specs=[pl.BlockSpec((t, LANE), lambda i, j: (0, FOX_COL0 + i * nj + j)),
                  pl.BlockSpec((3, LANE), lambda i, j: (0, 0))],
        out_specs=pl.BlockSpec((1, 2, t, DH), lambda i, j: (i, j, 0, 0)),
        out_shape=jax.ShapeDtypeStruct((3, HEADS, t, DH), F32),
        compiler_params=_params(("arbitrary", "arbitrary")),
    )(pm, nw)


def _fox_prep_bwd(pm, nw, dqkv):
    t = pm.shape[0]
    nj = WIDTH // LANE

    def body(x_ref, w_ref, d_ref, dx_ref, dw_ref):
        kind = pl.program_id(0)
        hs = _head_sum_matrix()
        wk = w_ref[pl.ds(kind, 1), :]

        def tile(r, dw):
            xv = x_ref[_tile_rows(r), :]
            rn = lax.rsqrt(_group_sum(xv * xv, hs) * (1.0 / DH) + EPS)
            xn = xv * rn
            d = jnp.concatenate([d_ref[0, 0, _tile_rows(r), :], d_ref[0, 1, _tile_rows(r), :]], axis=1)
            g = d * wk
            dxn = rn * (g - xn * _group_sum(g * xn, hs) * (1.0 / DH))
            dx_ref[_tile_rows(r), :] = jnp.where(kind < 2, dxn, d)
            return dw + jnp.sum(d * xn, axis=0, keepdims=True)

        dw_ref[0, 0] = lax.fori_loop(0, t // PREP_ROWS, tile, jnp.zeros((1, LANE), F32))

    return pl.pallas_call(
        body, name="fox_prep_bwd", grid=(3, nj),
        in_specs=[pl.BlockSpec((t, LANE), lambda i, j: (0, FOX_COL0 + i * nj + j)),
                  pl.BlockSpec((3, LANE), lambda i, j: (0, 0)),
                  pl.BlockSpec((1, 2, t, DH), lambda i, j: (i, j, 0, 0))],
        out_specs=[pl.BlockSpec((t, LANE), lambda i, j: (0, i * nj + j)),
                   pl.BlockSpec((1, 1, 1, LANE), lambda i, j: (i, j, 0, 0))],
        out_shape=[jax.ShapeDtypeStruct((t, 3 * WIDTH), F32), jax.ShapeDtypeStruct((3, nj, 1, LANE), F32)],
        compiler_params=_params(("arbitrary", "arbitrary")),
    )(pm, nw, dqkv)


SC = 256
CPS = SC // CHUNK
GDN_HP = 2
Q_SCALE = DH ** -0.5


def _sc_masks():
    ri = lax.broadcasted_iota(jnp.int32, (SC, SC), 0)
    ci = lax.broadcasted_iota(jnp.int32, (SC, SC), 1)
    same = (ri // CHUNK) == (ci // CHUNK)
    return same & (ri >= ci), same & (ri > ci), ri == ci


def _unit_lower_inverses(ms, eye):
    invs = [jnp.where(eye, 1.0, 0.0) + m for m in ms]
    for _ in range(5):
        ms = [_dot(m, m) for m in ms]
        invs = [inv + _dot(inv, m) for inv, m in zip(invs, ms)]
    return invs


def _lane_col(blk, lane_idx):
    lane = lax.broadcasted_iota(jnp.int32, blk.shape, 1)
    return jnp.sum(jnp.where(lane == lane_idx, blk, 0.0), axis=1, keepdims=True)


def _to_lane(col, lane_idx):
    lane = lax.broadcasted_iota(jnp.int32, (col.shape[0], LANE), 1)
    return jnp.where(lane == lane_idx, col, 0.0)


def _gdn_columns(gates_ref, run_ref, tot_ref, runt_ref, rows, h):
    return (_lane_col(gates_ref[rows, :], h), _lane_col(run_ref[rows, :], HEADS + h),
            _lane_col(tot_ref[rows, :], HEADS + h), runt_ref[pl.ds(h, 1), rows])


def _gdn_local(q, k, beta, gc, gl, grow, causal):
    decay = jnp.exp(jnp.where(causal, gc - grow, -1e30))
    egc = jnp.exp(gc)
    ekd = jnp.exp(gl - gc)
    qs = q * Q_SCALE
    kb = k * beta
    kk = _dot_nt(kb, k)
    qk = _dot_nt(qs, k)
    return beta, gl, decay, egc, ekd, qs, kb, kk, qk, jnp.where(causal, qk * decay, 0.0)


def _chunk_rows(c):
    return pl.ds(pl.multiple_of(c * CHUNK, CHUNK), CHUNK)


def _sc_rows(b):
    return pl.ds(pl.multiple_of(b * SC, SC), SC)


def _gdn_fwd(qkv, gates, run, tot, run_t):
    t = qkv.shape[2]
    nc = t // CHUNK
    nsc = t // SC

    hp_n = GDN_HP
    heads = range(hp_n)

    def body(qkv_ref, gates_ref, run_ref, tot_ref, runt_ref, o_ref, st_ref, inv_ref, kc_s, qc_s, g_s, au_s):
        hp = pl.program_id(0)
        causal, strict, eye = _sc_masks()

        def local(b, _):
            rows = _sc_rows(b)
            loc = [_gdn_local(qkv_ref[0, hh, rows, :], qkv_ref[1, hh, rows, :],
                              *_gdn_columns(gates_ref, run_ref, tot_ref, runt_ref, rows, hp * hp_n + hh), causal)
                   for hh in heads]
            invs = _unit_lower_inverses([-jnp.where(strict, l[7] * l[2], 0.0) for l in loc], eye)
            us, ws = [], []
            for hh in heads:
                beta, _, _, egc, _, _, kb, _, _, _ = loc[hh]
                inv_ref[hh, rows, :] = invs[hh].astype(inv_ref.dtype)
                us.append(_dot(invs[hh], qkv_ref[2, hh, rows, :] * beta))
                ws.append(_dot(invs[hh], kb * egc))
            for hh in heads:
                _, _, _, egc, ekd, qs, _, _, _, attn = loc[hh]
                g_s[hh, rows, :] = qs * egc - _dot(attn, ws[hh])
                au_s[hh, rows, :] = _dot(attn, us[hh])
                kd = qkv_ref[1, hh, rows, :] * ekd
                for j in range(CPS):
                    sl = slice(j * CHUNK, (j + 1) * CHUNK)
                    kc_s[hh, b * CPS + j] = _dot_tn(kd[sl], ws[hh][sl])
                    qc_s[hh, b * CPS + j] = _dot_tn(kd[sl], us[hh][sl])
            return 0

        lax.fori_loop(0, nsc, local, 0)

        def step(c, states):
            rows = _chunk_rows(c)
            tot_row = tot_ref[pl.ds(c * CHUNK, 1), :]
            new = []
            for hh in heads:
                s = states[hh]
                st_ref[hh, c] = s
                o_ref[hh, rows, :] = _dot(g_s[hh, rows, :], s) + au_s[hh, rows, :]
                egl = jnp.exp(_lane_col(tot_row, HEADS + hp * hp_n + hh))
                new.append(egl * s - _dot(kc_s[hh, c], s) + qc_s[hh, c])
            return tuple(new)

        lax.fori_loop(0, nc, step, tuple(jnp.zeros((DH, DH), F32) for _ in heads))

    whole = pl.BlockSpec((t, LANE), lambda h: (0, 0))
    return pl.pallas_call(
        body, name="gdn_fwd", grid=(HEADS // hp_n,),
        in_specs=[pl.BlockSpec((3, hp_n, t, DH), lambda h: (0, h, 0, 0)), whole, whole, whole,
                  pl.BlockSpec((HEADS, t), lambda h: (1, 0))],
        out_specs=[pl.BlockSpec((hp_n, t, DH), lambda h: (h, 0, 0)), pl.BlockSpec((hp_n, nc, DH, DH), lambda h: (h, 0, 0, 0)),
                   pl.BlockSpec((hp_n, t, SC), lambda h: (h, 0, 0))],
        out_shape=[jax.ShapeDtypeStruct((HEADS, t, DH), F32), jax.ShapeDtypeStruct((HEADS, nc, DH, DH), F32),
                   jax.ShapeDtypeStruct((HEADS, t, SC), MXU)],
        scratch_shapes=[pltpu.VMEM((hp_n, nc, DH, DH), F32), pltpu.VMEM((hp_n, nc, DH, DH), F32),
                        pltpu.VMEM((hp_n, t, DH), F32), pltpu.VMEM((hp_n, t, DH), F32)],
        compiler_params=_params(("arbitrary",)),
    )(qkv, gates, run, tot, run_t)


def _gdn_bwd(qkv, gates, run, tot, run_t, inv, states, do):
    t = qkv.shape[2]
    nc = t // CHUNK
    nsc = t // SC

    hp_n = GDN_HP
    heads = range(hp_n)

    def body(qkv_ref, gates_ref, run_ref, tot_ref, runt_ref, inv_ref, st_ref, do_ref,
             dqkv_ref, dcol_ref, dtot_ref, drow_ref, u_s, w_s, kc_s, h_s, dsn_s):
        hp = pl.program_id(0)
        causal, strict, _ = _sc_masks()

        @pl.when(hp == 0)
        def _():
            dcol_ref[...] = jnp.zeros_like(dcol_ref)
            dtot_ref[...] = jnp.zeros_like(dtot_ref)

        def local_of(hh, rows):
            return _gdn_local(qkv_ref[0, hh, rows, :], qkv_ref[1, hh, rows, :],
                              *_gdn_columns(gates_ref, run_ref, tot_ref, runt_ref, rows, hp * hp_n + hh), causal)

        def local(b, _):
            rows = _sc_rows(b)
            loc = [local_of(hh, rows) for hh in heads]
            us, ws = [], []
            for hh in heads:
                beta, _, _, egc, _, _, kb, _, _, _ = loc[hh]
                inv_b = inv_ref[hh, rows, :]
                us.append(_dot(inv_b, qkv_ref[2, hh, rows, :] * beta))
                ws.append(_dot(inv_b, kb * egc))
            gs = [loc[hh][5] * loc[hh][3] - _dot(loc[hh][9], ws[hh]) for hh in heads]
            for hh in heads:
                u_s[hh, rows, :] = us[hh]
                w_s[hh, rows, :] = ws[hh]
                kd = qkv_ref[1, hh, rows, :] * loc[hh][4]
                dout = do_ref[hh, rows, :]
                for j in range(CPS):
                    sl = slice(j * CHUNK, (j + 1) * CHUNK)
                    kc_s[hh, b * CPS + j] = _dot_tn(kd[sl], ws[hh][sl])
                    h_s[hh, b * CPS + j] = _dot_tn(gs[hh][sl], dout[sl])
            return 0

        lax.fori_loop(0, nsc, local, 0)

        def step(i, dss):
            c = nc - 1 - i
            tot_row = tot_ref[pl.ds(c * CHUNK, 1), :]
            new = []
            for hh in heads:
                ds = dss[hh]
                dsn_s[hh, c] = ds
                egl = jnp.exp(_lane_col(tot_row, HEADS + hp * hp_n + hh))
                new.append(egl * ds - _dot_tn(kc_s[hh, c], ds) + h_s[hh, c])
            return tuple(new)

        lax.fori_loop(0, nc, step, tuple(jnp.zeros((DH, DH), F32) for _ in heads))

        def back(b, _):
            rows = _sc_rows(b)
            first = lax.broadcasted_iota(jnp.int32, (CHUNK, 1), 0) == 0
            loc = [local_of(hh, rows) for hh in heads]
            mid = []
            for hh in heads:
                beta, gl, decay, egc, ekd, qs, kb, kk, qk, attn = loc[hh]
                u, w = u_s[hh, rows, :], w_s[hh, rows, :]
                kd = qkv_ref[1, hh, rows, :] * ekd
                dout = do_ref[hh, rows, :]
                dg_p, dkd_p, dw_p, du_p, dgl_p = [], [], [], [], []
                for j in range(CPS):
                    sl = slice(j * CHUNK, (j + 1) * CHUNK)
                    s = st_ref[hh, b * CPS + j]
                    dsn = dsn_s[hh, b * CPS + j]
                    dkc = -_dot_nt(dsn, s)
                    dg_p.append(_dot_nt(dout[sl], s))
                    dkd_p.append(_dot_nt(w[sl], dkc) + _dot_nt(u[sl], dsn))
                    dw_p.append(_dot(kd[sl], dkc))
                    du_p.append(_dot(kd[sl], dsn))
                    degl = jnp.sum(jnp.sum(s * dsn, axis=1, keepdims=True), axis=0, keepdims=True)
                    dgl_p.append(jnp.where(first, degl * jnp.exp(gl[j * CHUNK:j * CHUNK + 1, :]), 0.0))
                dg, dkd = jnp.concatenate(dg_p, axis=0), jnp.concatenate(dkd_p, axis=0)
                da = jnp.where(causal, _dot_nt(dout, u) - _dot_nt(dg, w), 0.0)
                at = _dot_tn(attn, jnp.concatenate([dout, dg], axis=1))
                du = at[:, :DH] + jnp.concatenate(du_p, axis=0)
                dw = jnp.concatenate(dw_p, axis=0) - at[:, DH:]
                mid.append((kd, dg, dkd, da, du, dw, jnp.concatenate(dgl_p, axis=0)))
            inv_ts = [inv_ref[hh, rows, :].astype(F32).T for hh in heads]
            its = [_dot(inv_ts[hh], jnp.concatenate([mid[hh][4], mid[hh][5]], axis=1)) for hh in heads]
            dinvs = []
            for hh in heads:
                beta, _, _, egc, _, _, kb, _, _, _ = loc[hh]
                dinvs.append(_dot_nt(mid[hh][4], qkv_ref[2, hh, rows, :] * beta) + _dot_nt(mid[hh][5], kb * egc))
            half = [_dot(inv_ts[hh], dinvs[hh]) for hh in heads]
            dls = [jnp.where(strict, -_dot(half[hh], inv_ts[hh]), 0.0) for hh in heads]
            for hh in heads:
                head = hp * hp_n + hh
                beta, gl, decay, egc, ekd, qs, kb, kk, qk, attn = loc[hh]
                kd, dg, dkd, da, du, dw, dgl_first = mid[hh]
                k, v = qkv_ref[1, hh, rows, :], qkv_ref[2, hh, rows, :]
                dvb, dkbe = its[hh][:, :DH], its[hh][:, DH:]
                dl = dls[hh]
                dlogd = (dl * kk + da * qk) * decay
                dkk = dl * decay
                dqk = da * decay
                dkb = _dot(dkk, k) + dkbe * egc
                dqs = _dot(dqk, k) + dg * egc
                dk = (_dot_tn(jnp.concatenate([dkk, dqk], axis=0), jnp.concatenate([kb, qs], axis=0))
                      + dkd * ekd + dkb * beta)
                dkd_kd = jnp.sum(dkd * kd, axis=1, keepdims=True)
                dgc = (jnp.sum(dlogd, axis=1, keepdims=True) + jnp.sum(dg * qs, axis=1, keepdims=True) * egc
                       + jnp.sum(dkbe * (kb * egc), axis=1, keepdims=True) - dkd_kd)
                dbeta = jnp.sum(dkb * k, axis=1, keepdims=True) + jnp.sum(dvb * v, axis=1, keepdims=True)
                dqkv_ref[0, hh, rows, :] = dqs * Q_SCALE
                dqkv_ref[1, hh, rows, :] = dk
                dqkv_ref[2, hh, rows, :] = dvb * beta
                dcol_ref[rows, :] += _to_lane(dbeta, head) + _to_lane(dgc, HEADS + head)
                dtot_ref[rows, :] += _to_lane(dkd_kd + dgl_first, HEADS + head)
                drow_ref[pl.ds(head, 1), rows] = -jnp.sum(dlogd, axis=0, keepdims=True)
            return 0

        lax.fori_loop(0, nsc, back, 0)

    whole = pl.BlockSpec((t, LANE), lambda h: (0, 0))
    rowspec = pl.BlockSpec((HEADS, t), lambda h: (0, 0))
    sq = pltpu.VMEM((hp_n, nc, DH, DH), F32)
    per_head = pltpu.VMEM((hp_n, t, DH), F32)
    return pl.pallas_call(
        body, name="gdn_bwd", grid=(HEADS // hp_n,),
        in_specs=[pl.BlockSpec((3, hp_n, t, DH), lambda h: (0, h, 0, 0)), whole, whole, whole,
                  pl.BlockSpec((HEADS, t), lambda h: (1, 0)),
                  pl.BlockSpec((hp_n, t, SC), lambda h: (h, 0, 0)), pl.BlockSpec((hp_n, nc, DH, DH), lambda h: (h, 0, 0, 0)),
                  pl.BlockSpec((hp_n, t, DH), lambda h: (h, 0, 0))],
        out_specs=[pl.BlockSpec((3, hp_n, t, DH), lambda h: (0, h, 0, 0)), whole, whole, rowspec],
        out_shape=[jax.ShapeDtypeStruct((3, HEADS, t, DH), F32), jax.ShapeDtypeStruct((t, LANE), F32),
                   jax.ShapeDtypeStruct((t, LANE), F32), jax.ShapeDtypeStruct((HEADS, t), F32)],
        scratch_shapes=[per_head, per_head, sq, sq, sq],
        compiler_params=_params(("arbitrary",)),
    )(qkv, gates, run, tot, run_t, inv, states, do)


FOX_HP = 2


def _wide_t(a):
    r = a.shape[0]
    return jnp.concatenate([a, jnp.zeros((r, LANE - DH), F32)], axis=1).T[:DH]


def _tall_t(a):
    r = a.shape[1]
    return jnp.concatenate([a, jnp.zeros((LANE - DH, r), F32)], axis=0).T[:, :DH]


def _key_side_f(run_blk, head, qb):
    col = jnp.broadcast_to(_lane_col(run_blk, 2 * HEADS + head), (run_blk.shape[0], LANE))
    return jnp.concatenate([col] * (qb // LANE), axis=1)


def _diag_mask(qb):
    return lax.broadcasted_iota(jnp.int32, (qb, qb), 0) <= lax.broadcasted_iota(jnp.int32, (qb, qb), 1)


def _fox_fwd(qkv, run, run_t, carry=None):
    t = qkv.shape[2]
    qb = min(QB, t)
    nq = t // qb
    hp_n = FOX_HP
    c_in, c_in_specs, c_out_shape, c_out_specs, c_sem = _carry_operands(*carry) if carry else ([], [], [], [], None)
    n_c = len(c_in)

    def body(*refs):
        q_ref, k_ref, v_ref, run_ref, runt_ref = refs[:5]
        o_ref, lse_ref = refs[5 + n_c:7 + n_c]
        kb_s, vt_s = refs[7 + n_c + len(c_out_shape):9 + n_c + len(c_out_shape)]
        hp = pl.program_id(0)
        i = pl.program_id(1)

        if carry:
            _carry_step(carry[0], refs[5:5 + n_c], refs[7 + n_c], refs[8 + n_c], refs[-1], hp * nq + i,
                        (HEADS // hp_n) * nq)

        @pl.when(i == 0)
        def _():
            for hh in range(hp_n):
                kb_s[hh] = k_ref[0, hh].astype(MXU)
                for b in range(nq):
                    rows = slice(b * qb, (b + 1) * qb)
                    vt_s[hh, :, rows] = _wide_t(v_ref[0, hh, rows, :]).astype(MXU)

        qrows = pl.ds(pl.multiple_of(i * qb, qb), qb)
        qs = [(q_ref[0, hh] * Q_SCALE).astype(MXU) for hh in range(hp_n)]
        fq = [runt_ref[pl.ds(hp * hp_n + hh, 1), qrows] for hh in range(hp_n)]

        def block_rows(j):
            return pl.ds(pl.multiple_of(j * qb, qb), qb)

        def scores(j):
            return tuple(_dot_nt(kb_s[hh, block_rows(j), :], qs[hh]) for hh in range(hp_n))

        def absorb(j, raw, state, diagonal):
            rows = block_rows(j)
            run_blk = run_ref[rows, :]
            stats, pv = [], []
            for hh in range(hp_n):
                m, l, _ = state[hh]
                st = raw[hh] + (fq[hh] - _key_side_f(run_blk, hp * hp_n + hh, qb))
                if diagonal:
                    st = jnp.where(_diag_mask(qb), st, -1e30)
                m_new = jnp.maximum(m, jnp.max(st, axis=0, keepdims=True))
                p = jnp.exp(st - m_new)
                alpha = jnp.exp(m - m_new)
                stats.append((m_new, alpha * l + jnp.sum(p, axis=0, keepdims=True), alpha))
                pv.append(_dot(vt_s[hh, :, rows], p))
            return tuple((stats[hh][0], stats[hh][1], stats[hh][2] * state[hh][2] + pv[hh]) for hh in range(hp_n))

        def kstep(j, carry):
            state, raw = carry
            ahead = scores(j + 1)
            return absorb(j, raw, state, False), ahead

        init = tuple((jnp.full((1, qb), -1e30, F32), jnp.zeros((1, qb), F32), jnp.zeros((DH, qb), F32))
                     for _ in range(hp_n))
        state, raw = lax.fori_loop(0, i, kstep, (init, scores(0)))
        state = absorb(i, raw, state, True)
        for hh in range(hp_n):
            m, l, acc = state[hh]
            o_ref[hh] = _tall_t(acc / l)
            lse_ref[pl.ds(hp * hp_n + hh, 1), qrows] = m + jnp.log(l)

    out = pl.pallas_call(
        body, name="fox_fwd", grid=(HEADS // hp_n, nq),
        in_specs=[pl.BlockSpec((1, hp_n, qb, DH), lambda h, i: (0, h, i, 0)),
                  pl.BlockSpec((1, hp_n, t, DH), lambda h, i: (1, h, 0, 0)),
                  pl.BlockSpec((1, hp_n, t, DH), lambda h, i: (2, h, 0, 0)),
                  pl.BlockSpec((t, LANE), lambda h, i: (0, 0)),
                  pl.BlockSpec((HEADS, t), lambda h, i: (2, 0))] + c_in_specs,
        out_specs=[pl.BlockSpec((hp_n, qb, DH), lambda h, i: (h, i, 0)),
                   pl.BlockSpec((HEADS, t), lambda h, i: (0, 0))] + c_out_specs,
        out_shape=[jax.ShapeDtypeStruct((HEADS, t, DH), F32), jax.ShapeDtypeStruct((HEADS, t), F32)] + c_out_shape,
        input_output_aliases={5 + j: 4 + j for j in range(n_c)},
        scratch_shapes=[pltpu.VMEM((hp_n, t, DH), MXU), pltpu.VMEM((hp_n, DH, t), MXU)] + ([c_sem] if carry else []),
        compiler_params=_params(("arbitrary", "arbitrary")),
    )(qkv, qkv, qkv, run, run_t, *c_in)
    return (out[0], out[1], _carry_state(out[2:])) if carry else (out[0], out[1])


def _fox_bwd(qkv, run, run_t, o, lse, do, carry=None):
    t = qkv.shape[2]
    qb = min(QB, t)
    nq = t // qb
    hp_n = FOX_HP

    c_in, c_in_specs, c_out_shape, c_out_specs, c_sem = _carry_operands(*carry) if carry else ([], [], [], [], None)
    n_c = len(c_in)

    def body(*refs):
        q_ref, k_ref, v_ref, run_ref, runt_ref, o_ref, lse_ref, do_ref = refs[:8]
        dqkv_ref, dcol_ref, drow_ref = refs[8 + n_c:11 + n_c]
        dqt_s = refs[11 + n_c + len(c_out_shape)]
        hp = pl.program_id(0)
        j = pl.program_id(1)

        if carry:
            _carry_step(carry[0], refs[8:8 + n_c], refs[11 + n_c], refs[12 + n_c], refs[-1], hp * nq + j,
                        (HEADS // hp_n) * nq)

        @pl.when(j == 0)
        def _():
            dqt_s[...] = jnp.zeros_like(dqt_s)

        @pl.when((j == 0) & (hp == 0))
        def _():
            dcol_ref[...] = jnp.zeros_like(dcol_ref)
            drow_ref[...] = jnp.zeros_like(drow_ref)

        krows = pl.ds(pl.multiple_of(j * qb, qb), qb)
        run_blk = run_ref[krows, :]
        ones8 = jnp.ones((8, DH), MXU)
        kb, kt, vb, fk = [], [], [], []
        for hh in range(hp_n):
            kf = k_ref[0, hh]
            kb.append(kf.astype(MXU))
            kt.append(_wide_t(kf).astype(MXU))
            vb.append(v_ref[0, hh].astype(MXU))
            fk.append(_key_side_f(run_blk, hp * hp_n + hh, qb))

        def block_rows(i):
            return pl.ds(pl.multiple_of(i * qb, qb), qb)

        def products(i):
            rows = block_rows(i)
            out = []
            for hh in range(hp_n):
                dout = do_ref[hh, rows, :]
                x = dout * o_ref[hh, rows, :]
                x_hi = x.astype(MXU)
                out.append((_dot_nt(kb[hh], q_ref[0, hh, rows, :] * Q_SCALE), _dot_nt(vb[hh], dout),
                            (_dot_nt(ones8, x_hi) + _dot_nt(ones8, x - x_hi.astype(F32)))[0:1, :]))
            return tuple(out)

        def absorb(i, prods, acc, diagonal):
            rows = block_rows(i)
            pieces = []
            for hh in range(hp_n):
                head = hp * hp_n + hh
                raw, dpt, drow = prods[hh]
                off = runt_ref[pl.ds(head, 1), rows] - lse_ref[pl.ds(head, 1), rows]
                st = raw + (off - fk[hh])
                if diagonal:
                    st = jnp.where(_diag_mask(qb), st, -1e30)
                pt = jnp.exp(st)
                dst = pt * (dpt - drow)
                drow_ref[pl.ds(head, 1), rows] += jnp.sum(dst, axis=0, keepdims=True)
                folded = dst[:, 0:LANE]
                for c in range(1, qb // LANE):
                    folded = folded + dst[:, c * LANE:(c + 1) * LANE]
                pieces.append((_dot(dst, q_ref[0, hh, rows, :] * Q_SCALE), _dot(pt, do_ref[hh, rows, :]),
                               _dot(kt[hh], dst), folded))
            out = []
            for hh in range(hp_n):
                dqt_s[hh, :, rows] += pieces[hh][2]
                out.append((acc[hh][0] + pieces[hh][0], acc[hh][1] + pieces[hh][1], acc[hh][2] + pieces[hh][3]))
            return tuple(out)

        def qstep(i, carry):
            acc, prods = carry
            ahead = products(jnp.minimum(i + 1, nq - 1))
            return absorb(i, prods, acc, False), ahead

        init = tuple((jnp.zeros((qb, DH), F32), jnp.zeros((qb, DH), F32), jnp.zeros((qb, LANE), F32))
                     for _ in range(hp_n))
        first = products(j)
        ahead = products(jnp.minimum(j + 1, nq - 1))
        acc = absorb(j, first, init, True)
        acc, _ = lax.fori_loop(j + 1, nq, qstep, (acc, ahead))
        for hh in range(hp_n):
            dk, dv, ds_sum = acc[hh]
            dqkv_ref[1, hh, krows, :] = dk
            dqkv_ref[2, hh, krows, :] = dv
            dcol_ref[krows, :] += _to_lane(-jnp.sum(ds_sum, axis=1, keepdims=True), 2 * HEADS + hp * hp_n + hh)

        @pl.when(j == nq - 1)
        def _():
            for hh in range(hp_n):
                for b in range(nq):
                    rows = slice(b * qb, (b + 1) * qb)
                    dqkv_ref[0, hh, rows, :] = _tall_t(dqt_s[hh, :, rows]) * Q_SCALE

    full = pl.BlockSpec((hp_n, t, DH), lambda h, j: (h, 0, 0))
    rows8 = pl.BlockSpec((HEADS, t), lambda h, j: (0, 0))
    out = pl.pallas_call(
        body, name="fox_bwd", grid=(HEADS // hp_n, nq),
        in_specs=[pl.BlockSpec((1, hp_n, t, DH), lambda h, j: (0, h, 0, 0)),
                  pl.BlockSpec((1, hp_n, qb, DH), lambda h, j: (1, h, j, 0)),
                  pl.BlockSpec((1, hp_n, qb, DH), lambda h, j: (2, h, j, 0)),
                  pl.BlockSpec((t, LANE), lambda h, j: (0, 0)), pl.BlockSpec((HEADS, t), lambda h, j: (2, 0)),
                  full, rows8, full] + c_in_specs,
        out_specs=[pl.BlockSpec((3, hp_n, t, DH), lambda h, j: (0, h, 0, 0)),
                   pl.BlockSpec((t, LANE), lambda h, j: (0, 0)), rows8] + c_out_specs,
        out_shape=[jax.ShapeDtypeStruct((3, HEADS, t, DH), F32), jax.ShapeDtypeStruct((t, LANE), F32),
                   jax.ShapeDtypeStruct((HEADS, t), F32)] + c_out_shape,
        input_output_aliases={8 + j: 5 + j for j in range(n_c)},
        scratch_shapes=[pltpu.VMEM((hp_n, DH, t), F32)] + ([c_sem] if carry else []),
        compiler_params=_params(("arbitrary", "arbitrary")),
    )(qkv, qkv, qkv, run, run_t, o, lse, do, *c_in)
    return (out[0], out[1], out[2], _carry_state(out[3:])) if carry else tuple(out)


Z_COL0 = 3 * WIDTH // LANE
FGATE_COL0 = 7 * WIDTH // LANE


def _gdn_post(o, pm, onw):
    t = pm.shape[0]

    def body(o_ref, z_ref, w_ref, m_ref, mt_ref):
        z = z_ref[...]
        sz = z * _sigmoid(z)
        halves = []
        for hh in range(2):
            ov = o_ref[hh]
            n = ov * lax.rsqrt(jnp.mean(ov * ov, axis=-1, keepdims=True) + EPS) * w_ref[...]
            halves.append(n * sz[:, hh * DH:(hh + 1) * DH])
        m = jnp.concatenate(halves, axis=1)
        m_ref[...] = m.astype(m_ref.dtype)
        mt_ref[...] = m.T.astype(mt_ref.dtype)

    return pl.pallas_call(
        body, name="gdn_post", grid=(WIDTH // LANE,),
        in_specs=[pl.BlockSpec((2, t, DH), lambda j: (j, 0, 0)), pl.BlockSpec((t, LANE), lambda j: (0, Z_COL0 + j)),
                  pl.BlockSpec((1, DH), lambda j: (0, 0))],
        out_specs=[pl.BlockSpec((t, LANE), lambda j: (0, j)), pl.BlockSpec((LANE, t), lambda j: (j, 0))],
        out_shape=[jax.ShapeDtypeStruct((t, WIDTH), MXU), jax.ShapeDtypeStruct((WIDTH, t), MXU)],
        compiler_params=_params(("arbitrary",)),
    )(o, pm, onw)


def _gdn_post_bwd(o, pm, onw, dmix):
    t = pm.shape[0]

    def body(o_ref, z_ref, w_ref, dm_ref, do_ref, dz_ref, dw_ref):
        @pl.when(pl.program_id(0) == 0)
        def _():
            dw_ref[...] = jnp.zeros_like(dw_ref)

        z = z_ref[...]
        sg = _sigmoid(z)
        sz = z * sg
        dsz = sg * (1.0 + z * (1.0 - sg))
        dm = dm_ref[...]
        for hh in range(2):
            cols = slice(hh * DH, (hh + 1) * DH)
            ov = o_ref[hh]
            r = lax.rsqrt(jnp.mean(ov * ov, axis=-1, keepdims=True) + EPS)
            xn = ov * r
            dmh = dm[:, cols]
            dn = dmh * sz[:, cols]
            dz_ref[:, cols] = dmh * (xn * w_ref[...]) * dsz[:, cols]
            dw_ref[...] += jnp.sum(dn * xn, axis=0, keepdims=True)
            g = dn * w_ref[...]
            do_ref[hh] = r * (g - xn * jnp.mean(g * xn, axis=-1, keepdims=True))

    return pl.pallas_call(
        body, name="gdn_post_bwd", grid=(WIDTH // LANE,),
        in_specs=[pl.BlockSpec((2, t, DH), lambda j: (j, 0, 0)), pl.BlockSpec((t, LANE), lambda j: (0, Z_COL0 + j)),
                  pl.BlockSpec((1, DH), lambda j: (0, 0)), pl.BlockSpec((t, LANE), lambda j: (0, j))],
        out_specs=[pl.BlockSpec((2, t, DH), lambda j: (j, 0, 0)), pl.BlockSpec((t, LANE), lambda j: (0, j)),
                   pl.BlockSpec((1, DH), lambda j: (0, 0))],
        out_shape=[jax.ShapeDtypeStruct((HEADS, t, DH), F32), jax.ShapeDtypeStruct((t, WIDTH), F32),
                   jax.ShapeDtypeStruct((1, DH), F32)],
        compiler_params=_params(("arbitrary",)),
    )(o, pm, onw, dmix)


def _fox_post(o, pm):
    t = pm.shape[0]

    def body(o_ref, g_ref, m_ref, mt_ref):
        m = jnp.concatenate([o_ref[0], o_ref[1]], axis=1) * _sigmoid(g_ref[...])
        m_ref[...] = m.astype(m_ref.dtype)
        mt_ref[...] = m.T.astype(mt_ref.dtype)

    return pl.pallas_call(
        body, name="fox_post", grid=(WIDTH // LANE,),
        in_specs=[pl.BlockSpec((2, t, DH), lambda j: (j, 0, 0)), pl.BlockSpec((t, LANE), lambda j: (0, FGATE_COL0 + j))],
        out_specs=[pl.BlockSpec((t, LANE), lambda j: (0, j)), pl.BlockSpec((LANE, t), lambda j: (j, 0))],
        out_shape=[jax.ShapeDtypeStruct((t, WIDTH), MXU), jax.ShapeDtypeStruct((WIDTH, t), MXU)],
        compiler_params=_params(("arbitrary",)),
    )(o, pm)


def _fox_post_bwd(o, pm, dmix):
    t = pm.shape[0]

    def body(o_ref, g_ref, dm_ref, do_ref, dg_ref):
        sg = _sigmoid(g_ref[...])
        dm = dm_ref[...]
        for hh in range(2):
            cols = slice(hh * DH, (hh + 1) * DH)
            do_ref[hh] = dm[:, cols] * sg[:, cols]
            dg_ref[:, cols] = dm[:, cols] * o_ref[hh] * (sg * (1.0 - sg))[:, cols]

    return pl.pallas_call(
        body, name="fox_post_bwd", grid=(WIDTH // LANE,),
        in_specs=[pl.BlockSpec((2, t, DH), lambda j: (j, 0, 0)), pl.BlockSpec((t, LANE), lambda j: (0, FGATE_COL0 + j)),
                  pl.BlockSpec((t, LANE), lambda j: (0, j))],
        out_specs=[pl.BlockSpec((2, t, DH), lambda j: (j, 0, 0)), pl.BlockSpec((t, LANE), lambda j: (0, j))],
        out_shape=[jax.ShapeDtypeStruct((HEADS, t, DH), F32), jax.ShapeDtypeStruct((t, WIDTH), F32)],
        compiler_params=_params(("arbitrary",)),
    )(o, pm, dmix)


def _tail(x, mixg, mixf, tgt, wo, n2w, wg, wu, wd, fw):
    t, d = x.shape
    dff = wg.shape[1]
    tb = min(TB, t)

    def body(x_ref, mg_ref, mf_ref, t_ref, wo_ref, n2_ref, wg_ref, wu_ref, wd_ref, fw_ref,
             h2_ref, act_ref, dgate_ref, dup_ref, dx3_ref, dx2_ref, dmg_ref, dmf_ref, dn2_ref, dfw_ref, loss_ref):
        @pl.when(pl.program_id(0) == 0)
        def _():
            dn2_ref[...] = jnp.zeros_like(dn2_ref)
            dfw_ref[...] = jnp.zeros_like(dfw_ref)
            loss_ref[...] = jnp.zeros_like(loss_ref)

        x2 = x_ref[...] + _dot(mg_ref[...], wo_ref[0:WIDTH, :]) + _dot(mf_ref[...], wo_ref[WIDTH:2 * WIDTH, :])
        r2 = lax.rsqrt(jnp.mean(x2 * x2, axis=-1, keepdims=True) + EPS)
        xn2 = x2 * r2
        h2_32 = xn2 * n2_ref[...]
        h2 = h2_32.astype(MXU)
        h2_ref[...] = h2_32.T.astype(MXU)
        gate = _dot(h2, wg_ref[...])
        up = _dot(h2, wu_ref[...])
        sg = _sigmoid(gate)
        sl = gate * sg
        act_32 = sl * up
        act = act_32.astype(MXU)
        act_ref[...] = act_32.T.astype(MXU)
        x3 = x2 + _dot(act, wd_ref[...])
        r3 = lax.rsqrt(jnp.mean(x3 * x3, axis=-1, keepdims=True) + EPS)
        xn3 = x3 * r3
        err = xn3 * fw_ref[...] - t_ref[...]
        loss_ref[...] += 0.5 * jnp.sum(jnp.mean(err * err, axis=-1, keepdims=True), axis=0, keepdims=True)
        dy = err * (1.0 / d)
        dfw_ref[...] += jnp.sum(dy * xn3, axis=0, keepdims=True)
        g3 = dy * fw_ref[...]
        dx3 = r3 * (g3 - xn3 * jnp.mean(g3 * xn3, axis=-1, keepdims=True))
        dx3_ref[...] = dx3.astype(MXU)
        dact = _dot_nt(dx3, wd_ref[...])
        dgate = (dact * up * (sg * (1.0 + gate * (1.0 - sg)))).astype(MXU)
        dup = (dact * sl).astype(MXU)
        dgate_ref[...] = dgate
        dup_ref[...] = dup
        dh2 = _dot_nt(dgate, wg_ref[...]) + _dot_nt(dup, wu_ref[...])
        dn2_ref[...] += jnp.sum(dh2 * xn2, axis=0, keepdims=True)
        g2 = dh2 * n2_ref[...]
        dx2 = dx3 + r2 * (g2 - xn2 * jnp.mean(g2 * xn2, axis=-1, keepdims=True))
        dx2_ref[...] = dx2
        dmg_ref[...] = _dot_nt(dx2, wo_ref[0:WIDTH, :])
        dmf_ref[...] = _dot_nt(dx2, wo_ref[WIDTH:2 * WIDTH, :])

    def tok(n):
        return pl.BlockSpec((tb, n), lambda i: (i, 0))

    def tok_t(n):
        return pl.BlockSpec((n, tb), lambda i: (0, i))

    acc = pl.BlockSpec((1, d), lambda i: (0, 0))
    sds = jax.ShapeDtypeStruct
    return pl.pallas_call(
        body, name="tail", grid=(t // tb,),
        in_specs=[tok(d), tok(WIDTH), tok(WIDTH), tok(d), _resident(wo.shape), _resident((1, d)),
                  _resident(wg.shape), _resident(wu.shape), _resident(wd.shape), _resident((1, d))],
        out_specs=[tok_t(d), tok_t(dff), tok(dff), tok(dff), tok(d), tok(d), tok(WIDTH), tok(WIDTH), acc, acc,
                   pl.BlockSpec((1, 1), lambda i: (0, 0))],
        out_shape=[sds((d, t), MXU), sds((dff, t), MXU), sds((t, dff), MXU), sds((t, dff), MXU), sds((t, d), MXU),
                   sds((t, d), F32), sds((t, WIDTH), F32), sds((t, WIDTH), F32), sds((1, d), F32), sds((1, d), F32),
                   sds((1, 1), F32)],
        compiler_params=_params(("arbitrary",)),
    )(x, mixg, mixf, tgt, wo, n2w, wg, wu, wd, fw)


def _wgrad(at, b, name):
    m, t = at.shape
    n = b.shape[1]
    bm = 256 if m % 256 == 0 else LANE
    cast = b.dtype != jnp.dtype(MXU)

    def body(a_ref, b_ref, o_ref, *scratch):
        if cast:
            @pl.when(pl.program_id(0) == 0)
            def _():
                scratch[0][...] = b_ref[...].astype(MXU)
        o_ref[...] = jnp.dot(a_ref[...], scratch[0][...] if cast else b_ref[...],
                             preferred_element_type=F32).astype(o_ref.dtype)

    return pl.pallas_call(
        body, name=name, grid=(m // bm,),
        in_specs=[pl.BlockSpec((bm, t), lambda i: (i, 0)), _resident((t, n))],
        out_specs=pl.BlockSpec((bm, n), lambda i: (i, 0)),
        out_shape=jax.ShapeDtypeStruct((m, n), WIRE),
        scratch_shapes=[pltpu.VMEM((t, n), MXU)] if cast else [],
        compiler_params=_params(("arbitrary",)),
    )(at, b)


def _split_w_in(w_in):
    a = 4 * WIDTH
    b = a + 2 * HEADS
    c = b + 4 * WIDTH
    main = jnp.concatenate([w_in[:, :a], w_in[:, b:c]], axis=1)
    small = jnp.concatenate([w_in[:, a:b], w_in[:, c:], jnp.zeros((w_in.shape[0], LANE - 3 * HEADS), w_in.dtype)], axis=1)
    return main, small


def _merge_dw_in(d_gdn, d_z, d_fox, d_fg, d_small):
    return jnp.concatenate([d_gdn, d_z, d_small[:, :2 * HEADS], d_fox, d_fg, d_small[:, 2 * HEADS:3 * HEADS]], axis=1)


def _lanes(*pieces):
    v = jnp.concatenate([p.reshape(-1).astype(F32) for p in pieces])
    return jnp.pad(v, (0, LANE - v.shape[0])).reshape(1, LANE)


def _vector_params(p):
    d = p["norm1_w"].size
    gparams = jnp.concatenate([_lanes(jnp.zeros(HEADS), p["gdn_dt_bias"], p["fox_f_bias"]),
                               _lanes(jnp.zeros(HEADS), p["gdn_A_log"]), jnp.zeros((6, LANE), F32)])
    fox_nw = jnp.stack([jnp.tile(p["fox_q_norm_w"].reshape(-1), 2), jnp.tile(p["fox_k_norm_w"].reshape(-1), 2),
                        jnp.ones((LANE,), F32)])
    return dict(n1w=p["norm1_w"].reshape(1, d), n2w=p["norm2_w"].reshape(1, d), fw=p["final_norm_w"].reshape(1, d),
                onw=p["gdn_out_norm_w"].reshape(1, DH), gparams=gparams, fox_nw=fox_nw)


def _mixer_forward(x, vp, w_main, w_small, conv_w, carry=None):
    h1, pm, ps = _inproj(x, vp["n1w"], w_main, w_small)
    gates, run, tot, run_t = _gates(ps, vp["gparams"])
    fqkv = _fox_prep(pm, vp["fox_nw"])
    o_fox, lse, *carried = _fox_fwd(fqkv, run, run_t, carry)
    if carry:
        pm, o_fox = lax.optimization_barrier((pm, o_fox))
    mixf, mixf_t = _fox_post(o_fox, pm)
    gqkv = _gdn_prep(pm, conv_w)
    o_gdn, states, inv = _gdn_fwd(gqkv, gates, run, tot, run_t)
    mixg, mixg_t = _gdn_post(o_gdn, pm, vp["onw"])
    return dict(h1=h1, pm=pm, ps=ps, gates=gates, run=run, tot=tot, run_t=run_t, gqkv=gqkv, o_gdn=o_gdn, states=states,
                inv=inv, mixg=mixg, mixg_t=mixg_t, fqkv=fqkv, o_fox=o_fox, lse=lse, mixf=mixf, mixf_t=mixf_t,
                carried=carried[0] if carried else None)


def _mixer_backward(x, vp, w_main, w_small, conv_w, f, dx2, dmixg, dmixf, carry=None):
    pm = f["pm"]
    do_fox, dfg = _fox_post_bwd(f["o_fox"], pm, dmixf)
    dfqkv, dcol_f, drow_f, *carried = _fox_bwd(f["fqkv"], f["run"], f["run_t"], f["o_fox"], f["lse"], do_fox, carry)
    dfox, dfnw = _fox_prep_bwd(pm, vp["fox_nw"], dfqkv)
    do_gdn, dz, donw = _gdn_post_bwd(f["o_gdn"], pm, vp["onw"], dmixg)
    dgqkv, dcol_g, dtot_g, drow_g = _gdn_bwd(f["gqkv"], f["gates"], f["run"], f["tot"], f["run_t"], f["inv"], f["states"], do_gdn)
    dgdn, dconv = _gdn_prep_bwd(pm, conv_w, dgqkv)
    dps, gsum = _gates_bwd(f["ps"], vp["gparams"], dcol_g, dtot_g, dcol_f, drow_g, drow_f)
    grad_x, dn1w = _inproj_bwd(x, vp["n1w"], dx2, dgdn, dz, dfox, dfg, dps, w_main, w_small)
    h1 = f["h1"]
    dw_in = _merge_dw_in(_wgrad(h1, dgdn, "dw_in_gdn"), _wgrad(h1, dz, "dw_in_z"), _wgrad(h1, dfox, "dw_in_fox"),
                         _wgrad(h1, dfg, "dw_in_fgate"), _wgrad(h1, dps, "dw_in_small"))
    small = dict(dn1w=dn1w, gsum=gsum, donw=donw, dfnw=dfnw)
    return (grad_x, dw_in, dconv, small, carried[0]) if carried else (grad_x, dw_in, dconv, small)


VECTORS = ("norm1_w", "norm2_w", "final_norm_w", "gdn_A_log", "gdn_dt_bias", "gdn_out_norm_w", "fox_f_bias",
           "fox_q_norm_w", "fox_k_norm_w")
VEC_ROWS = 16
LOSS_ROW = len(VECTORS)


def _pack_vectors(dn1w, dn2w, dfw, gsum, donw, dfnw, loss):
    d = dn1w.shape[1]

    def body(n1_ref, n2_ref, fw_ref, gs_ref, on_ref, fn_ref, loss_ref, o_ref):
        o_ref[...] = jnp.zeros_like(o_ref)
        o_ref[0:1, :] = n1_ref[...]
        o_ref[1:2, :] = n2_ref[...]
        o_ref[2:3, :] = fw_ref[...]
        o_ref[3:4, 0:HEADS] = gs_ref[0:1, 0:HEADS]
        o_ref[4:5, 0:HEADS] = gs_ref[1:2, 0:HEADS]
        o_ref[5:6, 0:DH] = on_ref[...]
        o_ref[6:7, 0:HEADS] = gs_ref[2:3, 0:HEADS]
        for kind in range(2):
            v = fn_ref[kind, 0]
            for j in range(1, fn_ref.shape[1]):
                v = v + fn_ref[kind, j]
            o_ref[7 + kind:8 + kind, 0:DH] = v[:, :DH] + v[:, DH:]
        o_ref[LOSS_ROW:LOSS_ROW + 1, 0:1] = loss_ref[...]

    return pl.pallas_call(body, name="pack_vectors", out_shape=jax.ShapeDtypeStruct((VEC_ROWS, d), F32),
                          compiler_params=_params())(dn1w, dn2w, dfw, gsum, donw, dfnw, loss)


def _late_grads(f, h2t, actt, dgate, dup, dx3, dx2):
    return {"w_out": jnp.concatenate([_wgrad(f["mixg_t"], dx2, "dw_out_gdn"), _wgrad(f["mixf_t"], dx2, "dw_out_fox")], axis=0),
            "w_ffn_gate": _wgrad(h2t, dgate, "dw_gate"), "w_ffn_up": _wgrad(h2t, dup, "dw_up"),
            "w_ffn_down": _wgrad(actt, dx3, "dw_down")}


def _local_step(x, tgt, p, w_in, conv_w, wo, wg, wu, wd):
    vp = _vector_params(p)
    w_main, w_small = _split_w_in(w_in)
    f = _mixer_forward(x, vp, w_main, w_small, conv_w)
    (h2t, actt, dgate, dup, dx3, dx2, dmixg, dmixf, dn2w, dfw, loss) = _tail(
        x, f["mixg"], f["mixf"], tgt, wo, vp["n2w"], wg, wu, wd, vp["fw"])
    grad_x, dw_in, dconv, small = _mixer_backward(x, vp, w_main, w_small, conv_w, f, dx2, dmixg, dmixf)
    grads = {"w_in": dw_in, "gdn_conv_w": dconv, **_late_grads(f, h2t, actt, dgate, dup, dx3, dx2)}
    vec = _pack_vectors(small["dn1w"], dn2w, dfw, small["gsum"], small["donw"], small["dfnw"], loss)
    return grad_x, grads, vec


def _my_place():
    return lax.axis_index("x"), lax.axis_index("y"), lax.axis_index("c")


def _peers():
    x, y, c = _my_place()
    peers = []
    for k in range(1, N_DEV):
        px = 1 - x if k & 4 else x
        py = 1 - y if k & 2 else y
        pc = 1 - c if k & 1 else c
        peers.append(((px, py, pc), 4 * px + 2 * py + pc))
    return 4 * x + 2 * y + c, peers


def _spread_copies(kind, srcs, lands, send_sems, recv_sems):
    me, peers = _peers()
    kinds = [kind] * len(srcs) if isinstance(kind, str) else kind
    remote, local = [], []
    for i, (kd, src, land) in enumerate(zip(kinds, srcs, lands)):
        for k, (dev, idx) in enumerate(peers):
            remote.append(pltpu.make_async_remote_copy(
                src_ref=src if kd == "gather" else src.at[idx], dst_ref=land.at[me],
                send_sem=send_sems.at[i * (N_DEV - 1) + k], recv_sem=recv_sems.at[i * (N_DEV - 1) + k],
                device_id=dev, device_id_type=MESH))
        local.append((src if kd == "gather" else src.at[me], land.at[me]))
    return remote, local


def _land_shape(kind, a):
    return (N_DEV,) + a.shape if kind == "gather" else a.shape


HBM = pl.BlockSpec(memory_space=pltpu.HBM)
SEM = pl.BlockSpec(memory_space=pltpu.SEMAPHORE)


def _hbm(a):
    return pltpu.with_memory_space_constraint(a, pltpu.HBM)


def _carry_operands(kind, arrays):
    n = len(arrays)
    kinds = [kind] * n if isinstance(kind, str) else kind
    lands = [lax.empty(_land_shape(kd, a), a.dtype) for kd, a in zip(kinds, arrays)]
    sems = [pltpu.SemaphoreType.DMA((n * (N_DEV - 1),))] * 2
    return ([_hbm(a) for a in list(arrays) + lands], [HBM] * (2 * n),
            sems + [pltpu.HBM(a.shape, a.dtype) for a in list(arrays) + lands], [SEM] * 2 + [HBM] * (2 * n),
            pltpu.SemaphoreType.DMA((n,)))


def _carry_step(kind, in_refs, send_sems, recv_sems, local_sems, step, n_steps):
    n = len(in_refs) // 2
    remote, local = _spread_copies(kind, in_refs[:n], in_refs[n:], send_sems, recv_sems)
    copies = [pltpu.make_async_copy(s, d, local_sems.at[i]) for i, (s, d) in enumerate(local)]
    per_step = -(-len(remote) // n_steps)
    for s in range(-(-len(remote) // per_step)):
        @pl.when(step == s)
        def _():
            for cp in remote[s * per_step:(s + 1) * per_step]:
                cp.start()
            if s == 0:
                for cp in copies:
                    cp.start()

    @pl.when(step == n_steps - 1)
    def _():
        for cp in copies:
            cp.wait()


def _carry_state(extra_out):
    n = (len(extra_out) - 2) // 2
    return list(extra_out[2:2 + n]), list(extra_out[2 + n:]), extra_out[0], extra_out[1]


def _spread_start(arrays, kind, name):
    n = len(arrays)

    def body(*refs):
        srcs, lands = refs[:n], refs[n:2 * n]
        send_sems, recv_sems = refs[2 * n], refs[2 * n + 1]
        token = refs[4 * n + 2]
        local_sems = refs[4 * n + 3]
        remote, local = _spread_copies(kind, srcs, lands, send_sems, recv_sems)
        for cp in remote:
            cp.start()
        copies = [pltpu.make_async_copy(s, d, local_sems.at[i]) for i, (s, d) in enumerate(local)]
        for cp in copies:
            cp.start()
        for cp in copies:
            cp.wait()
        token[...] = jnp.zeros_like(token)

    sems = (pltpu.SemaphoreType.DMA((n * (N_DEV - 1),)),) * 2
    kinds = [kind] * n if isinstance(kind, str) else kind
    lands = [lax.empty(_land_shape(kd, a), a.dtype) for kd, a in zip(kinds, arrays)]
    out = pl.pallas_call(
        body, name=name,
        out_shape=sems + tuple(pltpu.HBM(a.shape, a.dtype) for a in list(arrays) + lands)
        + (jax.ShapeDtypeStruct((8, LANE), F32),),
        in_specs=[HBM] * (2 * n), out_specs=tuple([SEM] * 2 + [HBM] * (2 * n) + [pl.BlockSpec(memory_space=pltpu.VMEM)]),
        input_output_aliases={j: 2 + j for j in range(2 * n)},
        scratch_shapes=[pltpu.SemaphoreType.DMA((n,))],
        compiler_params=pltpu.CompilerParams(has_side_effects=pltpu.SideEffectType.DATAFLOW_SIDE_EFFECTING),
    )(*[_hbm(a) for a in arrays], *[_hbm(a) for a in lands])
    return (list(out[2:2 + n]), list(out[2 + n:2 + 2 * n]), out[0], out[1]), out[-1]


def _spread_wait(state, kind, after, name):
    srcs, lands, send_sems, recv_sems = state
    n = len(srcs)
    after = list(after) if isinstance(after, (list, tuple)) else [after]

    def body(*refs):
        remote, _ = _spread_copies(kind, refs[:n], refs[n:2 * n], refs[2 * n], refs[2 * n + 1])
        for cp in remote:
            cp.wait_send()
        for cp in remote:
            cp.wait_recv()

    out = pl.pallas_call(
        body, name=name,
        out_shape=tuple(pltpu.HBM(a.shape, a.dtype) for a in srcs + lands),
        in_specs=[HBM] * (2 * n) + [SEM, SEM] + [pl.BlockSpec(memory_space=pl.ANY)] * len(after),
        out_specs=tuple([HBM] * (2 * n)),
        input_output_aliases={j: j for j in range(2 * n)},
        compiler_params=pltpu.CompilerParams(has_side_effects=pltpu.SideEffectType.DATAFLOW_SIDE_EFFECTING),
    )(*srcs, *lands, send_sems, recv_sems, *after)
    return list(out[n:])


ADAM_ROWS = 128


def _adam_math(g, w, m, v):
    nm = ADAM_B1 * m + (1.0 - ADAM_B1) * g
    nv = ADAM_B2 * v + (1.0 - ADAM_B2) * (g * g)
    m_hat = nm / (1.0 - ADAM_B1 ** ADAM_STEP)
    v_hat = nv / (1.0 - ADAM_B2 ** ADAM_STEP)
    return -ADAM_LR * (m_hat / (jnp.sqrt(v_hat) + ADAM_EPS) + ADAM_WD * w), nm, nv


def _sum_parts(p_ref):
    g = p_ref[0].astype(F32)
    for s in range(1, N_DEV):
        g = g + p_ref[s].astype(F32)
    return g


def _adam_matrix(parts, w, m, v, name):
    _, r, c = w.shape
    rb = ADAM_ROWS if r % ADAM_ROWS == 0 else r

    def body(p_ref, w_ref, m_ref, v_ref, g_ref, d_ref, nm_ref, nv_ref):
        g = _sum_parts(p_ref)
        g_ref[0] = g
        d_ref[0], nm_ref[0], nv_ref[0] = _adam_math(g, w_ref[0], m_ref[0], v_ref[0])

    blk = pl.BlockSpec((1, rb, c), lambda i: (0, i, 0))
    return pl.pallas_call(
        body, name=name, grid=(r // rb,),
        in_specs=[pl.BlockSpec((N_DEV, rb, c), lambda i: (0, i, 0)), blk, blk, blk],
        out_specs=[blk] * 4, out_shape=[jax.ShapeDtypeStruct(w.shape, F32)] * 4,
        compiler_params=_params(("arbitrary",)),
    )(parts, w, m, v)


def _adam_vectors(parts, ws, ms, vs):
    nv = len(ws)

    def body(*refs):
        p_ref = refs[0]
        w_refs, m_refs, v_refs = refs[1:1 + nv], refs[1 + nv:1 + 2 * nv], refs[1 + 2 * nv:1 + 3 * nv]
        outs = refs[1 + 3 * nv:]
        g_all = _sum_parts(p_ref)
        for i in range(nv):
            n = w_refs[i].shape[1]
            g = g_all[i:i + 1, 0:n]
            d, nm, nvv = _adam_math(g, w_refs[i][...], m_refs[i][...], v_refs[i][...])
            outs[i][...] = g
            outs[nv + i][...] = d
            outs[2 * nv + i][...] = nm
            outs[3 * nv + i][...] = nvv
        outs[4 * nv][...] = g_all[LOSS_ROW:LOSS_ROW + 1, 0:1]

    shapes = [jax.ShapeDtypeStruct(a.shape, F32) for a in ws]
    out = pl.pallas_call(body, name="adam_vectors", out_shape=shapes * 4 + [jax.ShapeDtypeStruct((1, 1), F32)],
                         compiler_params=_params())(parts, *ws, *ms, *vs)
    return out[:nv], out[nv:2 * nv], out[2 * nv:3 * nv], out[3 * nv:4 * nv], out[4 * nv]


MATRICES = (("w_in", 1), ("gdn_conv_w", 1), ("w_out", 0), ("w_ffn_gate", 1), ("w_ffn_up", 1), ("w_ffn_down", 0))
WEIGHTS = ("norm1_w", "w_in", "gdn_conv_w", "gdn_A_log", "gdn_dt_bias", "gdn_out_norm_w", "fox_f_bias", "fox_q_norm_w",
           "fox_k_norm_w", "w_out", "norm2_w", "w_ffn_gate", "w_ffn_up", "w_ffn_down", "final_norm_w")


def _join(blocks, axis):
    _, r, c = blocks.shape
    if axis == 0:
        return blocks.reshape(N_DEV * r, c)
    return blocks.transpose(1, 0, 2).reshape(r, N_DEV * c)


def _cut(full, axis):
    r, c = full.shape
    if axis == 0:
        return full.reshape(N_DEV, r // N_DEV, c)
    return full.reshape(r, N_DEV, c // N_DEV).transpose(1, 0, 2)


def kernel(x, norm1_w, w_in, gdn_conv_w, gdn_A_log, gdn_dt_bias, gdn_out_norm_w, fox_f_bias, fox_q_norm_w, fox_k_norm_w, w_out, norm2_w, w_ffn_gate, w_ffn_up, w_ffn_down, final_norm_w, loss_target, m_norm1_w, m_w_in, m_gdn_conv_w, m_gdn_A_log, m_gdn_dt_bias, m_gdn_out_norm_w, m_fox_f_bias, m_fox_q_norm_w, m_fox_k_norm_w, m_w_out, m_norm2_w, m_w_ffn_gate, m_w_ffn_up, m_w_ffn_down, m_final_norm_w, v_norm1_w, v_w_in, v_gdn_conv_w, v_gdn_A_log, v_gdn_dt_bias, v_gdn_out_norm_w, v_fox_f_bias, v_fox_q_norm_w, v_fox_k_norm_w, v_w_out, v_norm2_w, v_w_ffn_gate, v_w_ffn_up, v_w_ffn_down, v_final_norm_w):
    w = dict(norm1_w=norm1_w, w_in=w_in, gdn_conv_w=gdn_conv_w, gdn_A_log=gdn_A_log, gdn_dt_bias=gdn_dt_bias,
             gdn_out_norm_w=gdn_out_norm_w, fox_f_bias=fox_f_bias, fox_q_norm_w=fox_q_norm_w, fox_k_norm_w=fox_k_norm_w,
             w_out=w_out, norm2_w=norm2_w, w_ffn_gate=w_ffn_gate, w_ffn_up=w_ffn_up, w_ffn_down=w_ffn_down,
             final_norm_w=final_norm_w)
    m = dict(norm1_w=m_norm1_w, w_in=m_w_in, gdn_conv_w=m_gdn_conv_w, gdn_A_log=m_gdn_A_log, gdn_dt_bias=m_gdn_dt_bias,
             gdn_out_norm_w=m_gdn_out_norm_w, fox_f_bias=m_fox_f_bias, fox_q_norm_w=m_fox_q_norm_w,
             fox_k_norm_w=m_fox_k_norm_w, w_out=m_w_out, norm2_w=m_norm2_w, w_ffn_gate=m_w_ffn_gate,
             w_ffn_up=m_w_ffn_up, w_ffn_down=m_w_ffn_down, final_norm_w=m_final_norm_w)
    v = dict(norm1_w=v_norm1_w, w_in=v_w_in, gdn_conv_w=v_gdn_conv_w, gdn_A_log=v_gdn_A_log, gdn_dt_bias=v_gdn_dt_bias,
             gdn_out_norm_w=v_gdn_out_norm_w, fox_f_bias=v_fox_f_bias, fox_q_norm_w=v_fox_q_norm_w,
             fox_k_norm_w=v_fox_k_norm_w, w_out=v_w_out, norm2_w=v_norm2_w, w_ffn_gate=v_w_ffn_gate,
             w_ffn_up=v_w_ffn_up, w_ffn_down=v_w_ffn_down, final_norm_w=v_final_norm_w)
    axis_of = dict(MATRICES)
    late = ("w_out", "w_ffn_gate", "w_ffn_up", "w_ffn_down")
    xs, tgt = x[0], loss_target[0]
    vp = _vector_params({n: w[n] for n in VECTORS})

    state_in, token = _spread_start([w["w_in"][0].astype(WIRE), w["gdn_conv_w"][0]], "gather", "gather_in_start")
    w_in_blocks, conv_blocks = _spread_wait(state_in, "gather", token, "gather_in_wait")
    w_main, w_small = _split_w_in(_join(w_in_blocks, 1))
    conv_w = _join(conv_blocks, 1)
    f = _mixer_forward(xs, vp, w_main, w_small, conv_w, ("gather", [w[n][0].astype(WIRE) for n in late]))
    full = {n: _join(b, axis_of[n])
            for n, b in zip(late, _spread_wait(f["carried"], "gather", f["mixg"], "gather_late_wait"))}

    (h2t, actt, dgate, dup, dx3, dx2, dmixg, dmixf, dn2w, dfw, loss) = _tail(
        xs, f["mixg"], f["mixf"], tgt, full["w_out"], vp["n2w"], full["w_ffn_gate"], full["w_ffn_up"],
        full["w_ffn_down"], vp["fw"])
    dlate = _late_grads(f, h2t, actt, dgate, dup, dx3, dx2)

    grad_x, dw_in, dconv, small, state_grads = _mixer_backward(
        xs, vp, w_main, w_small, conv_w, f, dx2, dmixg, dmixf, ("scatter", [_cut(dlate[n], axis_of[n]) for n in late]))
    vec = _pack_vectors(small["dn1w"], dn2w, dfw, small["gsum"], small["donw"], small["dfnw"], loss)

    last_kinds = ["scatter", "scatter", "gather"]
    state_last, token = _spread_start([_cut(dw_in, 1), _cut(dconv, 1), vec], last_kinds, "last_start")
    parts = dict(zip(late, _spread_wait(state_grads, "scatter", token, "grads_late_wait")))
    results = [{}, {}, {}, {}]

    def update(n):
        for d, a in zip(results, _adam_matrix(parts[n], w[n], m[n], v[n], "adam_" + n)):
            d[n] = a

    for n in late:
        update(n)
    parts["w_in"], parts["gdn_conv_w"], parts_vec = _spread_wait(state_last, last_kinds, [results[0][n] for n in late],
                                                                 "last_wait")
    update("w_in")
    update("gdn_conv_w")
    row = lambda a: a.reshape(1, -1)
    *vec_out, total_loss = _adam_vectors(parts_vec, [row(w[n]) for n in VECTORS], [row(m[n]) for n in VECTORS],
                                         [row(v[n]) for n in VECTORS])
    for d, arrs in zip(results, vec_out):
        for n, a in zip(VECTORS, arrs):
            d[n] = a.reshape(w[n].shape)
    return (total_loss[0, 0], grad_x[None], *[d[n] for d in results for n in WEIGHTS])
```

```python
import functools

import jax
import jax.numpy as jnp
from jax import lax
from jax.experimental import pallas as pl
from jax.experimental.pallas import tpu as pltpu

F32 = jnp.float32
MXU = jnp.bfloat16
WIRE = jnp.bfloat16
HI = lax.Precision.HIGHEST
EPS = 1e-6

N_DEV = 8
HEADS = 8
DH = 64
WIDTH = HEADS * DH
CHUNK = 64
LANE = 128
ROW_ALIGN = 16
TB = 256
QB = 256
VMEM_LIMIT = 60 * 1024 * 1024

ADAM_LR = 0.001
ADAM_B1 = 0.9
ADAM_B2 = 0.999
ADAM_EPS = 1e-08
ADAM_WD = 0.01
ADAM_STEP = 10

MESH = pl.DeviceIdType.MESH


def _params(sem=None):
    return pltpu.CompilerParams(dimension_semantics=sem, vmem_limit_bytes=VMEM_LIMIT)


def _resident(shape):
    n = len(shape)
    return pl.BlockSpec(shape, lambda *_: (0,) * n, pipeline_mode=pl.Buffered(1))


def _dot(a, b):
    return jnp.dot(a.astype(MXU), b.astype(MXU), preferred_element_type=F32)


def _dot_nt(a, b):
    return lax.dot_general(a.astype(MXU), b.astype(MXU), (((1,), (1,)), ((), ())), preferred_element_type=F32)


def _dot_tn(a, b):
    return lax.dot_general(a.astype(MXU), b.astype(MXU), (((0,), (0,)), ((), ())), preferred_element_type=F32)


def _hdot(a, b):
    return jnp.dot(a, b, precision=HI, preferred_element_type=F32)


def _hdot_nt(a, b):
    return lax.dot_general(a, b, (((1,), (1,)), ((), ())), precision=HI, preferred_element_type=F32)


def _hdot_tn(a, b):
    return lax.dot_general(a, b, (((0,), (0,)), ((), ())), precision=HI, preferred_element_type=F32)


def _sigmoid(x):
    return 0.5 * jnp.tanh(0.5 * x) + 0.5


def _softplus(x):
    return jnp.maximum(x, 0.0) + jnp.log(1.0 + jnp.exp(-jnp.abs(x)))


def _head_sum_matrix():
    ri = lax.broadcasted_iota(jnp.int32, (LANE, LANE), 0) // DH
    ci = lax.broadcasted_iota(jnp.int32, (LANE, LANE), 1) // DH
    return (ri == ci).astype(F32)


def _group_sum(a, ones_matrix):
    hi = a.astype(jnp.bfloat16)
    lo = (a - hi.astype(F32)).astype(jnp.bfloat16)
    m = ones_matrix.astype(jnp.bfloat16)
    return jnp.dot(hi, m, preferred_element_type=F32) + jnp.dot(lo, m, preferred_element_type=F32)


def _shift_down(x, s):
    return pltpu.roll(x, s, 0)


def _shift_up(x, s):
    return pltpu.roll(x, x.shape[0] - s, 0)


def _inproj(x, n1w, w_main, w_small):
    t, d = x.shape
    nm = w_main.shape[1]
    tb = min(TB, t)

    def body(x_ref, nw_ref, wm_ref, ws_ref, ht_ref, pm_ref, ps_ref):
        xv = x_ref[...]
        r = lax.rsqrt(jnp.mean(xv * xv, axis=-1, keepdims=True) + EPS)
        h32 = xv * r * nw_ref[...]
        h = h32.astype(MXU)
        ht_ref[...] = h32.T.astype(MXU)
        pm_ref[...] = jnp.dot(h, wm_ref[...], preferred_element_type=F32)
        ps_ref[...] = jnp.dot(h, ws_ref[...], preferred_element_type=F32)

    return pl.pallas_call(
        body, name="inproj", grid=(t // tb,),
        in_specs=[pl.BlockSpec((tb, d), lambda i: (i, 0)), _resident((1, d)), _resident((d, nm)), _resident((d, LANE))],
        out_specs=[pl.BlockSpec((d, tb), lambda i: (0, i)), pl.BlockSpec((tb, nm), lambda i: (i, 0)),
                   pl.BlockSpec((tb, LANE), lambda i: (i, 0))],
        out_shape=[jax.ShapeDtypeStruct((d, t), MXU), jax.ShapeDtypeStruct((t, nm), F32),
                   jax.ShapeDtypeStruct((t, LANE), F32)],
        compiler_params=_params(("arbitrary",)),
    )(x, n1w, w_main, w_small)


def _inproj_bwd(x, n1w, dx2, dgdn, dz, dfox, dfg, dps, w_main, w_small, carry=None):
    t, d = x.shape
    tb = min(TB, t)
    w3 = 3 * WIDTH
    c_in, c_in_specs, c_out_shape, c_out_specs, c_sem = _carry_operands(*carry) if carry else ([], [], [], [], None)
    n_c = len(c_in)

    def body(*refs):
        x_ref, nw_ref, dx2_ref, dgdn_ref, dz_ref, dfox_ref, dfg_ref, dps_ref, wm_ref, ws_ref = refs[:10]
        gx_ref, dnw_ref = refs[10 + n_c:12 + n_c]
        if carry:
            _carry_step(carry[0], refs[10:10 + n_c], refs[12 + n_c], refs[13 + n_c], refs[-1], pl.program_id(0), t // tb)
        dh = _dot_nt(dgdn_ref[...], wm_ref[:, 0:w3])
        dh += _dot_nt(dz_ref[...], wm_ref[:, w3:w3 + WIDTH])
        dh += _dot_nt(dfox_ref[...], wm_ref[:, w3 + WIDTH:2 * w3 + WIDTH])
        dh += _dot_nt(dfg_ref[...], wm_ref[:, 2 * w3 + WIDTH:2 * w3 + 2 * WIDTH])
        dh += _dot_nt(dps_ref[...], ws_ref[...])
        xv = x_ref[...]
        r = lax.rsqrt(jnp.mean(xv * xv, axis=-1, keepdims=True) + EPS)
        xn = xv * r

        @pl.when(pl.program_id(0) == 0)
        def _():
            dnw_ref[...] = jnp.zeros_like(dnw_ref)

        dnw_ref[...] += jnp.sum(dh * xn, axis=0, keepdims=True)
        g = dh * nw_ref[...]
        gx_ref[...] = dx2_ref[...] + r * (g - xn * jnp.mean(g * xn, axis=-1, keepdims=True))

    def tok(n):
        return pl.BlockSpec((tb, n), lambda i: (i, 0))

    out = pl.pallas_call(
        body, name="inproj_bwd", grid=(t // tb,),
        in_specs=[tok(d), _resident((1, d)), tok(d), tok(w3), tok(WIDTH), tok(w3), tok(WIDTH), tok(LANE),
                  _resident(w_main.shape), _resident(w_small.shape)] + c_in_specs,
        out_specs=[tok(d), pl.BlockSpec((1, d), lambda i: (0, 0))] + c_out_specs,
        out_shape=[jax.ShapeDtypeStruct((t, d), F32), jax.ShapeDtypeStruct((1, d), F32)] + c_out_shape,
        input_output_aliases={10 + j: 4 + j for j in range(n_c)},
        scratch_shapes=[c_sem] if carry else [],
        compiler_params=_params(("arbitrary",)),
    )(x, n1w, dx2, dgdn, dz, dfox, dfg, dps, w_main, w_small, *c_in)
    return (out[0], out[1], _carry_state(out[2:])) if carry else tuple(out)


def _gate_lanes(shape):
    lane = lax.broadcasted_iota(jnp.int32, shape, 1)
    return lane < HEADS, (lane >= HEADS) & (lane < 2 * HEADS), (lane >= 2 * HEADS) & (lane < 3 * HEADS)


def _block_masks():
    ri = lax.broadcasted_iota(jnp.int32, (LANE, LANE), 0)
    ci = lax.broadcasted_iota(jnp.int32, (LANE, LANE), 1)
    same = (ri // CHUNK) == (ci // CHUNK)
    return ((ri >= ci).astype(F32), (ri <= ci).astype(F32), (same & (ri >= ci)).astype(F32),
            (same & (ri <= ci)).astype(F32), same.astype(F32))


def _gates(ps, gparams):
    t = ps.shape[0]
    nb = t // LANE

    def body(ps_ref, gp_ref, out_ref, run_ref, tot_ref, runt_ref):
        p = ps_ref[...]
        is_b, is_a, is_f = _gate_lanes(p.shape)
        z = p + gp_ref[0:1, :]
        neg_exp_a = -jnp.exp(gp_ref[1:2, :])
        glog = neg_exp_a * _softplus(z)
        logf = -_softplus(-z)
        out_ref[...] = jnp.where(is_b, _sigmoid(p), jnp.where(is_a, glog, jnp.where(is_f, logf, 0.0)))
        tril, _, tril_c, _, same_c = _block_masks()
        off = jnp.zeros((1, LANE), F32)
        for b in range(nb):
            rows = slice(b * LANE, (b + 1) * LANE)
            blk = out_ref[rows, :]
            ga = jnp.where(is_a[:LANE], blk, 0.0)
            fb = _hdot(tril, jnp.where(is_f[:LANE], blk, 0.0)) + off
            run = fb + _hdot(tril_c, ga)
            run_ref[rows, :] = run
            runt_ref[:, rows] = run.T
            tot_ref[rows, :] = _hdot(same_c, ga)
            off = fb[LANE - 1:LANE, :]

    return pl.pallas_call(
        body, name="gates",
        out_shape=[jax.ShapeDtypeStruct((t, LANE), F32)] * 3 + [jax.ShapeDtypeStruct((LANE, t), F32)],
        compiler_params=_params(),
    )(ps, gparams)


def _gates_bwd(ps, gparams, dcol_g, dtot_g, dcol_f, drow_g, drow_f):
    t = ps.shape[0]
    nb = t // LANE

    def body(ps_ref, gp_ref, dcg_ref, dtg_ref, dcf_ref, drg_ref, drf_ref, dps_ref, sums_ref, dl_ref, d0_ref, tr_ref):
        p = ps_ref[...]
        is_b, is_a, is_f = _gate_lanes(p.shape)
        _, triu, _, triu_c, same_c = _block_masks()
        tr_ref[...] = jnp.zeros_like(tr_ref)
        off = jnp.zeros((1, LANE), F32)
        for b in reversed(range(nb)):
            rows = slice(b * LANE, (b + 1) * LANE)
            tr_ref[HEADS:2 * HEADS, :] = drg_ref[:, rows]
            tr_ref[2 * HEADS:3 * HEADS, :] = drf_ref[:, rows]
            d = dcg_ref[rows, :] + dcf_ref[rows, :] + tr_ref[...].T
            d0_ref[rows, :] = d
            dlf = _hdot(triu, jnp.where(is_f[:LANE], d, 0.0)) + off
            dla = (_hdot(triu_c, jnp.where(is_a[:LANE], d, 0.0))
                   + _hdot(same_c, jnp.where(is_a[:LANE], dtg_ref[rows, :], 0.0)))
            dl_ref[rows, :] = dlf + dla
            off = dlf[0:1, :]
        z = p + gp_ref[0:1, :]
        neg_exp_a = -jnp.exp(gp_ref[1:2, :])
        sb = _sigmoid(p)
        glog = neg_exp_a * _softplus(z)
        dl = dl_ref[...]
        dp = jnp.where(is_b, d0_ref[...] * sb * (1.0 - sb),
                       jnp.where(is_a, dl * neg_exp_a * _sigmoid(z), jnp.where(is_f, dl * _sigmoid(-z), 0.0)))
        dps_ref[...] = dp
        s_a = jnp.sum(jnp.where(is_a, dl * glog, 0.0), axis=0, keepdims=True)
        s_p = jnp.sum(jnp.where(is_b, 0.0, dp), axis=0, keepdims=True)
        row = lax.broadcasted_iota(jnp.int32, (8, LANE), 0)
        from_a = pltpu.roll(jnp.where(row == 0, s_a, jnp.where(row == 1, s_p, 0.0)), LANE - HEADS, 1)
        from_f = pltpu.roll(jnp.where(row == 2, s_p, 0.0), LANE - 2 * HEADS, 1)
        lane = lax.broadcasted_iota(jnp.int32, (8, LANE), 1)
        sums_ref[...] = jnp.where(lane < HEADS, from_a + from_f, 0.0)

    return pl.pallas_call(
        body, name="gates_bwd",
        out_shape=[jax.ShapeDtypeStruct((t, LANE), F32), jax.ShapeDtypeStruct((8, LANE), F32)],
        scratch_shapes=[pltpu.VMEM((t, LANE), F32), pltpu.VMEM((t, LANE), F32), pltpu.VMEM((LANE, LANE), F32)],
        compiler_params=_params(),
    )(ps, gparams, dcol_g, dtot_g, dcol_f, drow_g, drow_f)


def _conv(xv, w):
    acc = w[3:4, :] * xv
    for s in range(1, 4):
        acc += w[3 - s:4 - s, :] * _shift_down(xv, s)
    return acc


PREP_ROWS = 256
HALO = 8


def _tile_rows(r):
    return pl.ds(pl.multiple_of(r * PREP_ROWS, PREP_ROWS), PREP_ROWS)


def _gdn_prep(pm, conv_w):
    t = pm.shape[0]
    nj = WIDTH // LANE
    win = PREP_ROWS + HALO

    def body(x_ref, w_ref, o_ref, xp_ref):
        kind = pl.program_id(0)
        xp_ref[0:HALO, :] = jnp.zeros((HALO, LANE), F32)
        xp_ref[HALO:, :] = x_ref[...]
        w = w_ref[...]
        hs = _head_sum_matrix()

        def tile(r, _):
            xw = xp_ref[pl.ds(pl.multiple_of(r * PREP_ROWS, PREP_ROWS), win), :]
            acc = _conv(xw, w)[HALO:]
            y = acc * _sigmoid(acc)
            out = jnp.where(kind < 2, y * lax.rsqrt(_group_sum(y * y, hs) + EPS), y)
            o_ref[0, 0, _tile_rows(r), :] = out[:, :DH]
            o_ref[0, 1, _tile_rows(r), :] = out[:, DH:]
            return 0

        lax.fori_loop(0, t // PREP_ROWS, tile, 0)

    return pl.pallas_call(
        body, name="gdn_prep", grid=(3, nj),
        in_specs=[pl.BlockSpec((t, LANE), lambda i, j: (0, i * nj + j)),
                  pl.BlockSpec((4, LANE), lambda i, j: (0, i * nj + j))],
        out_specs=pl.BlockSpec((1, 2, t, DH), lambda i, j: (i, j, 0, 0)),
        out_shape=jax.ShapeDtypeStruct((3, HEADS, t, DH), F32),
        scratch_shapes=[pltpu.VMEM((t + HALO, LANE), F32)],
        compiler_params=_params(("arbitrary", "arbitrary")),
    )(pm, conv_w)


def _gdn_prep_bwd(pm, conv_w, dqkv):
    t = pm.shape[0]
    nj = WIDTH // LANE
    win = PREP_ROWS + 2 * HALO

    def body(x_ref, w_ref, d_ref, dx_ref, dw_ref, xp_ref, dp_ref):
        kind = pl.program_id(0)
        zeros = jnp.zeros((HALO, LANE), F32)
        for ref in (xp_ref, dp_ref):
            ref[0:HALO, :] = zeros
            ref[HALO + t:, :] = zeros
        xp_ref[HALO:HALO + t, :] = x_ref[...]
        dp_ref[HALO:HALO + t, 0:DH] = d_ref[0, 0]
        dp_ref[HALO:HALO + t, DH:] = d_ref[0, 1]
        w = w_ref[...]
        hs = _head_sum_matrix()
        rows = lax.broadcasted_iota(jnp.int32, (win, LANE), 0)
        in_tile = (rows >= HALO) & (rows < HALO + PREP_ROWS)

        def tile(r, dw):
            start = pl.multiple_of(r * PREP_ROWS, PREP_ROWS)
            xw = xp_ref[pl.ds(start, win), :]
            d = dp_ref[pl.ds(start, win), :]
            acc = _conv(xw, w)
            sg = _sigmoid(acc)
            y = acc * sg
            rn = lax.rsqrt(_group_sum(y * y, hs) + EPS)
            yn = y * rn
            dy = jnp.where(kind < 2, rn * (d - yn * _group_sum(d * yn, hs)), d)
            dacc = dy * sg * (1.0 + acc * (1.0 - sg))
            dx = w[3:4, :] * dacc
            for s in range(1, 4):
                dx += w[3 - s:4 - s, :] * _shift_up(dacc, s)
            dx_ref[_tile_rows(r), :] = dx[HALO:HALO + PREP_ROWS]
            dm = jnp.where(in_tile, dacc, 0.0)
            return tuple(dw[i] + jnp.sum(dm * (xw if i == 3 else _shift_down(xw, 3 - i)), axis=0, keepdims=True)
                         for i in range(4))

        dw = lax.fori_loop(0, t // PREP_ROWS, tile, tuple(jnp.zeros((1, LANE), F32) for _ in range(4)))
        for i in range(4):
            dw_ref[i:i + 1, :] = dw[i]

    return pl.pallas_call(
        body, name="gdn_prep_bwd", grid=(3, nj),
        in_specs=[pl.BlockSpec((t, LANE), lambda i, j: (0, i * nj + j)),
                  pl.BlockSpec((4, LANE), lambda i, j: (0, i * nj + j)),
                  pl.BlockSpec((1, 2, t, DH), lambda i, j: (i, j, 0, 0))],
        out_specs=[pl.BlockSpec((t, LANE), lambda i, j: (0, i * nj + j)),
                   pl.BlockSpec((4, LANE), lambda i, j: (0, i * nj + j))],
        out_shape=[jax.ShapeDtypeStruct((t, 3 * WIDTH), F32), jax.ShapeDtypeStruct((4, 3 * WIDTH), F32)],
        scratch_shapes=[pltpu.VMEM((t + 2 * HALO, LANE), F32), pltpu.VMEM((t + 2 * HALO, LANE), F32)],
        compiler_params=_params(("arbitrary", "arbitrary")),
    )(pm, conv_w, dqkv)


FOX_COL0 = 4 * WIDTH // LANE


def _fox_prep(pm, nw):
    t = pm.shape[0]
    nj = WIDTH // LANE

    def body(x_ref, w_ref, o_ref):
        kind = pl.program_id(0)
        hs = _head_sum_matrix()
        wk = w_ref[pl.ds(kind, 1), :]

        def tile(r, _):
            xv = x_ref[_tile_rows(r), :]
            ms = _group_sum(xv * xv, hs) * (1.0 / DH)
            out = jnp.where(kind < 2, xv * lax.rsqrt(ms + EPS) * wk, xv)
            o_ref[0, 0, _tile_rows(r), :] = out[:, :DH]
            o_ref[0, 1, _tile_rows(r), :] = out[:, DH:]
            return 0

        lax.fori_loop(0, t // PREP_ROWS, tile, 0)

    return pl.pallas_call(
        body, name="fox_prep", grid=(3, nj),
        in_specs=[pl.BlockSpec((t, LANE), lambda i, j: (0, FOX_COL0 + i * nj + j)),
                  pl.BlockSpec((3, LANE), lambda i, j: (0, 0))],
        out_specs=pl.BlockSpec((1, 2, t, DH), lambda i, j: (i, j, 0, 0)),
        out_shape=jax.ShapeDtypeStruct((3, HEADS, t, DH), F32),
        compiler_params=_params(("arbitrary", "arbitrary")),
    )(pm, nw)


def _fox_prep_bwd(pm, nw, dqkv):
    t = pm.shape[0]
    nj = WIDTH // LANE

    def body(x_ref, w_ref, d_ref, dx_ref, dw_ref):
        kind = pl.program_id(0)
        hs = _head_sum_matrix()
        wk = w_ref[pl.ds(kind, 1), :]

        def tile(r, dw):
            xv = x_ref[_tile_rows(r), :]
            rn = lax.rsqrt(_group_sum(xv * xv, hs) * (1.0 / DH) + EPS)
            xn = xv * rn
            d = jnp.concatenate([d_ref[0, 0, _tile_rows(r), :], d_ref[0, 1, _tile_rows(r), :]], axis=1)
            g = d * wk
            dxn = rn * (g - xn * _group_sum(g * xn, hs) * (1.0 / DH))
            dx_ref[_tile_rows(r), :] = jnp.where(kind < 2, dxn, d)
            return dw + jnp.sum(d * xn, axis=0, keepdims=True)

        dw_ref[0, 0] = lax.fori_loop(0, t // PREP_ROWS, tile, jnp.zeros((1, LANE), F32))

    return pl.pallas_call(
        body, name="fox_prep_bwd", grid=(3, nj),
        in_specs=[pl.BlockSpec((t, LANE), lambda i, j: (0, FOX_COL0 + i * nj + j)),
                  pl.BlockSpec((3, LANE), lambda i, j: (0, 0)),
                  pl.BlockSpec((1, 2, t, DH), lambda i, j: (i, j, 0, 0))],
        out_specs=[pl.BlockSpec((t, LANE), lambda i, j: (0, i * nj + j)),
                   pl.BlockSpec((1, 1, 1, LANE), lambda i, j: (i, j, 0, 0))],
        out_shape=[jax.ShapeDtypeStruct((t, 3 * WIDTH), F32), jax.ShapeDtypeStruct((3, nj, 1, LANE), F32)],
        compiler_params=_params(("arbitrary", "arbitrary")),
    )(pm, nw, dqkv)


SC = 256
CPS = SC // CHUNK
GDN_HP = 2
Q_SCALE = DH ** -0.5


def _sc_masks():
    ri = lax.broadcasted_iota(jnp.int32, (SC, SC), 0)
    ci = lax.broadcasted_iota(jnp.int32, (SC, SC), 1)
    same = (ri // CHUNK) == (ci // CHUNK)
    return same & (ri >= ci), same & (ri > ci), ri == ci


def _unit_lower_inverses(ms, eye):
    invs = [jnp.where(eye, 1.0, 0.0) + m for m in ms]
    for _ in range(5):
        ms = [_dot(m, m) for m in ms]
        invs = [inv + _dot(inv, m) for inv, m in zip(invs, ms)]
    return invs


def _lane_col(blk, lane_idx):
    lane = lax.broadcasted_iota(jnp.int32, blk.shape, 1)
    return jnp.sum(jnp.where(lane == lane_idx, blk, 0.0), axis=1, keepdims=True)


def _to_lane(col, lane_idx):
    lane = lax.broadcasted_iota(jnp.int32, (col.shape[0], LANE), 1)
    return jnp.where(lane == lane_idx, col, 0.0)


def _gdn_columns(gates_ref, run_ref, tot_ref, runt_ref, rows, h):
    return (_lane_col(gates_ref[rows, :], h), _lane_col(run_ref[rows, :], HEADS + h),
            _lane_col(tot_ref[rows, :], HEADS + h), runt_ref[pl.ds(h, 1), rows])


def _gdn_local(q, k, beta, gc, gl, grow, causal):
    decay = jnp.exp(jnp.where(causal, gc - grow, -1e30))
    egc = jnp.exp(gc)
    ekd = jnp.exp(gl - gc)
    qs = q * Q_SCALE
    kb = k * beta
    kk = _dot_nt(kb, k)
    qk = _dot_nt(qs, k)
    return beta, gl, decay, egc, ekd, qs, kb, kk, qk, jnp.where(causal, qk * decay, 0.0)


def _chunk_rows(c):
    return pl.ds(pl.multiple_of(c * CHUNK, CHUNK), CHUNK)


def _sc_rows(b):
    return pl.ds(pl.multiple_of(b * SC, SC), SC)


def _gdn_fwd(qkv, gates, run, tot, run_t):
    t = qkv.shape[2]
    nc = t // CHUNK
    nsc = t // SC

    hp_n = GDN_HP
    heads = range(hp_n)

    def body(qkv_ref, gates_ref, run_ref, tot_ref, runt_ref, o_ref, st_ref, inv_ref, kc_s, qc_s, g_s, au_s):
        hp = pl.program_id(0)
        causal, strict, eye = _sc_masks()

        def local(b, _):
            rows = _sc_rows(b)
            loc = [_gdn_local(qkv_ref[0, hh, rows, :], qkv_ref[1, hh, rows, :],
                              *_gdn_columns(gates_ref, run_ref, tot_ref, runt_ref, rows, hp * hp_n + hh), causal)
                   for hh in heads]
            invs = _unit_lower_inverses([-jnp.where(strict, l[7] * l[2], 0.0) for l in loc], eye)
            us, ws = [], []
            for hh in heads:
                beta, _, _, egc, _, _, kb, _, _, _ = loc[hh]
                inv_ref[hh, rows, :] = invs[hh].astype(inv_ref.dtype)
                us.append(_dot(invs[hh], qkv_ref[2, hh, rows, :] * beta))
                ws.append(_dot(invs[hh], kb * egc))
            for hh in heads:
                _, _, _, egc, ekd, qs, _, _, _, attn = loc[hh]
                g_s[hh, rows, :] = qs * egc - _dot(attn, ws[hh])
                au_s[hh, rows, :] = _dot(attn, us[hh])
                kd = qkv_ref[1, hh, rows, :] * ekd
                for j in range(CPS):
                    sl = slice(j * CHUNK, (j + 1) * CHUNK)
                    kc_s[hh, b * CPS + j] = _dot_tn(kd[sl], ws[hh][sl])
                    qc_s[hh, b * CPS + j] = _dot_tn(kd[sl], us[hh][sl])
            return 0

        lax.fori_loop(0, nsc, local, 0)

        def step(c, states):
            rows = _chunk_rows(c)
            tot_row = tot_ref[pl.ds(c * CHUNK, 1), :]
            new = []
            for hh in heads:
                s = states[hh]
                st_ref[hh, c] = s
                o_ref[hh, rows, :] = _dot(g_s[hh, rows, :], s) + au_s[hh, rows, :]
                egl = jnp.exp(_lane_col(tot_row, HEADS + hp * hp_n + hh))
                new.append(egl * s - _dot(kc_s[hh, c], s) + qc_s[hh, c])
            return tuple(new)

        lax.fori_loop(0, nc, step, tuple(jnp.zeros((DH, DH), F32) for _ in heads))

    whole = pl.BlockSpec((t, LANE), lambda h: (0, 0))
    return pl.pallas_call(
        body, name="gdn_fwd", grid=(HEADS // hp_n,),
        in_specs=[pl.BlockSpec((3, hp_n, t, DH), lambda h: (0, h, 0, 0)), whole, whole, whole,
                  pl.BlockSpec((HEADS, t), lambda h: (1, 0))],
        out_specs=[pl.BlockSpec((hp_n, t, DH), lambda h: (h, 0, 0)), pl.BlockSpec((hp_n, nc, DH, DH), lambda h: (h, 0, 0, 0)),
                   pl.BlockSpec((hp_n, t, SC), lambda h: (h, 0, 0))],
        out_shape=[jax.ShapeDtypeStruct((HEADS, t, DH), F32), jax.ShapeDtypeStruct((HEADS, nc, DH, DH), F32),
                   jax.ShapeDtypeStruct((HEADS, t, SC), MXU)],
        scratch_shapes=[pltpu.VMEM((hp_n, nc, DH, DH), F32), pltpu.VMEM((hp_n, nc, DH, DH), F32),
                        pltpu.VMEM((hp_n, t, DH), F32), pltpu.VMEM((hp_n, t, DH), F32)],
        compiler_params=_params(("arbitrary",)),
    )(qkv, gates, run, tot, run_t)


def _gdn_bwd(qkv, gates, run, tot, run_t, inv, states, do):
    t = qkv.shape[2]
    nc = t // CHUNK
    nsc = t // SC

    hp_n = GDN_HP
    heads = range(hp_n)

    def body(qkv_ref, gates_ref, run_ref, tot_ref, runt_ref, inv_ref, st_ref, do_ref,
             dqkv_ref, dcol_ref, dtot_ref, drow_ref, u_s, w_s, kc_s, h_s, dsn_s):
        hp = pl.program_id(0)
        causal, strict, _ = _sc_masks()

        @pl.when(hp == 0)
        def _():
            dcol_ref[...] = jnp.zeros_like(dcol_ref)
            dtot_ref[...] = jnp.zeros_like(dtot_ref)

        def local_of(hh, rows):
            return _gdn_local(qkv_ref[0, hh, rows, :], qkv_ref[1, hh, rows, :],
                              *_gdn_columns(gates_ref, run_ref, tot_ref, runt_ref, rows, hp * hp_n + hh), causal)

        def local(b, _):
            rows = _sc_rows(b)
            loc = [local_of(hh, rows) for hh in heads]
            us, ws = [], []
            for hh in heads:
                beta, _, _, egc, _, _, kb, _, _, _ = loc[hh]
                inv_b = inv_ref[hh, rows, :]
                us.append(_dot(inv_b, qkv_ref[2, hh, rows, :] * beta))
                ws.append(_dot(inv_b, kb * egc))
            gs = [loc[hh][5] * loc[hh][3] - _dot(loc[hh][9], ws[hh]) for hh in heads]
            for hh in heads:
                u_s[hh, rows, :] = us[hh]
                w_s[hh, rows, :] = ws[hh]
                kd = qkv_ref[1, hh, rows, :] * loc[hh][4]
                dout = do_ref[hh, rows, :]
                for j in range(CPS):
                    sl = slice(j * CHUNK, (j + 1) * CHUNK)
                    kc_s[hh, b * CPS + j] = _dot_tn(kd[sl], ws[hh][sl])
                    h_s[hh, b * CPS + j] = _dot_tn(gs[hh][sl], dout[sl])
            return 0

        lax.fori_loop(0, nsc, local, 0)

        def step(i, dss):
            c = nc - 1 - i
            tot_row = tot_ref[pl.ds(c * CHUNK, 1), :]
            new = []
            for hh in heads:
                ds = dss[hh]
                dsn_s[hh, c] = ds
                egl = jnp.exp(_lane_col(tot_row, HEADS + hp * hp_n + hh))
                new.append(egl * ds - _dot_tn(kc_s[hh, c], ds) + h_s[hh, c])
            return tuple(new)

        lax.fori_loop(0, nc, step, tuple(jnp.zeros((DH, DH), F32) for _ in heads))

        def back(b, _):
            rows = _sc_rows(b)
            first = lax.broadcasted_iota(jnp.int32, (CHUNK, 1), 0) == 0
            loc = [local_of(hh, rows) for hh in heads]
            mid = []
            for hh in heads:
                beta, gl, decay, egc, ekd, qs, kb, kk, qk, attn = loc[hh]
                u, w = u_s[hh, rows, :], w_s[hh, rows, :]
                kd = qkv_ref[1, hh, rows, :] * ekd
                dout = do_ref[hh, rows, :]
                dg_p, dkd_p, dw_p, du_p, dgl_p = [], [], [], [], []
                for j in range(CPS):
                    sl = slice(j * CHUNK, (j + 1) * CHUNK)
                    s = st_ref[hh, b * CPS + j]
                    dsn = dsn_s[hh, b * CPS + j]
                    dkc = -_dot_nt(dsn, s)
                    dg_p.append(_dot_nt(dout[sl], s))
                    dkd_p.append(_dot_nt(w[sl], dkc) + _dot_nt(u[sl], dsn))
                    dw_p.append(_dot(kd[sl], dkc))
                    du_p.append(_dot(kd[sl], dsn))
                    degl = jnp.sum(jnp.sum(s * dsn, axis=1, keepdims=True), axis=0, keepdims=True)
                    dgl_p.append(jnp.where(first, degl * jnp.exp(gl[j * CHUNK:j * CHUNK + 1, :]), 0.0))
                dg, dkd = jnp.concatenate(dg_p, axis=0), jnp.concatenate(dkd_p, axis=0)
                da = jnp.where(causal, _dot_nt(dout, u) - _dot_nt(dg, w), 0.0)
                at = _dot_tn(attn, jnp.concatenate([dout, dg], axis=1))
                du = at[:, :DH] + jnp.concatenate(du_p, axis=0)
                dw = jnp.concatenate(dw_p, axis=0) - at[:, DH:]
                mid.append((kd, dg, dkd, da, du, dw, jnp.concatenate(dgl_p, axis=0)))
            inv_ts = [inv_ref[hh, rows, :].astype(F32).T for hh in heads]
            its = [_dot(inv_ts[hh], jnp.concatenate([mid[hh][4], mid[hh][5]], axis=1)) for hh in heads]
            dinvs = []
            for hh in heads:
                beta, _, _, egc, _, _, kb, _, _, _ = loc[hh]
                dinvs.append(_dot_nt(mid[hh][4], qkv_ref[2, hh, rows, :] * beta) + _dot_nt(mid[hh][5], kb * egc))
            half = [_dot(inv_ts[hh], dinvs[hh]) for hh in heads]
            dls = [jnp.where(strict, -_dot(half[hh], inv_ts[hh]), 0.0) for hh in heads]
            for hh in heads:
                head = hp * hp_n + hh
                beta, gl, decay, egc, ekd, qs, kb, kk, qk, attn = loc[hh]
                kd, dg, dkd, da, du, dw, dgl_first = mid[hh]
                k, v = qkv_ref[1, hh, rows, :], qkv_ref[2, hh, rows, :]
                dvb, dkbe = its[hh][:, :DH], its[hh][:, DH:]
                dl = dls[hh]
                dlogd = (dl * kk + da * qk) * decay
                dkk = dl * decay
                dqk = da * decay
                dkb = _dot(dkk, k) + dkbe * egc
                dqs = _dot(dqk, k) + dg * egc
                dk = (_dot_tn(jnp.concatenate([dkk, dqk], axis=0), jnp.concatenate([kb, qs], axis=0))
                      + dkd * ekd + dkb * beta)
                dkd_kd = jnp.sum(dkd * kd, axis=1, keepdims=True)
                dgc = (jnp.sum(dlogd, axis=1, keepdims=True) + jnp.sum(dg * qs, axis=1, keepdims=True) * egc
                       + jnp.sum(dkbe * (kb * egc), axis=1, keepdims=True) - dkd_kd)
                dbeta = jnp.sum(dkb * k, axis=1, keepdims=True) + jnp.sum(dvb * v, axis=1, keepdims=True)
                dqkv_ref[0, hh, rows, :] = dqs * Q_SCALE
                dqkv_ref[1, hh, rows, :] = dk
                dqkv_ref[2, hh, rows, :] = dvb * beta
                dcol_ref[rows, :] += _to_lane(dbeta, head) + _to_lane(dgc, HEADS + head)
                dtot_ref[rows, :] += _to_lane(dkd_kd + dgl_first, HEADS + head)
                drow_ref[pl.ds(head, 1), rows] = -jnp.sum(dlogd, axis=0, keepdims=True)
            return 0

        lax.fori_loop(0, nsc, back, 0)

    whole = pl.BlockSpec((t, LANE), lambda h: (0, 0))
    rowspec = pl.BlockSpec((HEADS, t), lambda h: (0, 0))
    sq = pltpu.VMEM((hp_n, nc, DH, DH), F32)
    per_head = pltpu.VMEM((hp_n, t, DH), F32)
    return pl.pallas_call(
        body, name="gdn_bwd", grid=(HEADS // hp_n,),
        in_specs=[pl.BlockSpec((3, hp_n, t, DH), lambda h: (0, h, 0, 0)), whole, whole, whole,
                  pl.BlockSpec((HEADS, t), lambda h: (1, 0)),
                  pl.BlockSpec((hp_n, t, SC), lambda h: (h, 0, 0)), pl.BlockSpec((hp_n, nc, DH, DH), lambda h: (h, 0, 0, 0)),
                  pl.BlockSpec((hp_n, t, DH), lambda h: (h, 0, 0))],
        out_specs=[pl.BlockSpec((3, hp_n, t, DH), lambda h: (0, h, 0, 0)), whole, whole, rowspec],
        out_shape=[jax.ShapeDtypeStruct((3, HEADS, t, DH), F32), jax.ShapeDtypeStruct((t, LANE), F32),
                   jax.ShapeDtypeStruct((t, LANE), F32), jax.ShapeDtypeStruct((HEADS, t), F32)],
        scratch_shapes=[per_head, per_head, sq, sq, sq],
        compiler_params=_params(("arbitrary",)),
    )(qkv, gates, run, tot, run_t, inv, states, do)


FOX_HP_FWD = 4
FOX_HP_BWD = 2


def _wide_t(a):
    r = a.shape[0]
    return jnp.concatenate([a, jnp.zeros((r, LANE - DH), F32)], axis=1).T[:DH]


def _tall_t(a):
    r = a.shape[1]
    return jnp.concatenate([a, jnp.zeros((LANE - DH, r), F32)], axis=0).T[:, :DH]


def _key_side_f(run_blk, head, qb):
    col = jnp.broadcast_to(_lane_col(run_blk, 2 * HEADS + head), (run_blk.shape[0], LANE))
    return jnp.concatenate([col] * (qb // LANE), axis=1)


def _diag_mask(qb):
    return lax.broadcasted_iota(jnp.int32, (qb, qb), 0) <= lax.broadcasted_iota(jnp.int32, (qb, qb), 1)


def _fox_fwd(qkv, run, run_t, carry=None):
    t = qkv.shape[2]
    qb = min(QB, t)
    nq = t // qb
    hp_n = FOX_HP_FWD
    c_in, c_in_specs, c_out_shape, c_out_specs, c_sem = _carry_operands(*carry) if carry else ([], [], [], [], None)
    n_c = len(c_in)

    def body(*refs):
        q_ref, k_ref, v_ref, run_ref, runt_ref = refs[:5]
        o_ref, lse_ref = refs[5 + n_c:7 + n_c]
        kb_s, vt_s = refs[7 + n_c + len(c_out_shape):9 + n_c + len(c_out_shape)]
        hp = pl.program_id(0)
        i = pl.program_id(1)

        if carry:
            _carry_step(carry[0], refs[5:5 + n_c], refs[7 + n_c], refs[8 + n_c], refs[-1], hp * nq + i,
                        (HEADS // hp_n) * nq)

        @pl.when(i == 0)
        def _():
            for hh in range(hp_n):
                kb_s[hh] = k_ref[0, hh].astype(MXU)
                for b in range(nq):
                    rows = slice(b * qb, (b + 1) * qb)
                    vt_s[hh, :, rows] = _wide_t(v_ref[0, hh, rows, :]).astype(MXU)

        qrows = pl.ds(pl.multiple_of(i * qb, qb), qb)
        qs = [(q_ref[0, hh] * Q_SCALE).astype(MXU) for hh in range(hp_n)]
        fq = [runt_ref[pl.ds(hp * hp_n + hh, 1), qrows] for hh in range(hp_n)]

        def block_rows(j):
            return pl.ds(pl.multiple_of(j * qb, qb), qb)

        def scores(j):
            return tuple(_dot_nt(kb_s[hh, block_rows(j), :], qs[hh]) for hh in range(hp_n))

        def absorb(j, raw, state, diagonal):
            rows = block_rows(j)
            run_blk = run_ref[rows, :]
            stats, pv = [], []
            for hh in range(hp_n):
                m, l, _ = state[hh]
                st = raw[hh] + (fq[hh] - _key_side_f(run_blk, hp * hp_n + hh, qb))
                if diagonal:
                    st = jnp.where(_diag_mask(qb), st, -1e30)
                m_new = jnp.maximum(m, jnp.max(st, axis=0, keepdims=True))
                p = jnp.exp(st - m_new)
                alpha = jnp.exp(m - m_new)
                stats.append((m_new, alpha * l + jnp.sum(p, axis=0, keepdims=True), alpha))
                pv.append(_dot(vt_s[hh, :, rows], p))
            return tuple((stats[hh][0], stats[hh][1], stats[hh][2] * state[hh][2] + pv[hh]) for hh in range(hp_n))

        def kstep(j, carry):
            state, raw = carry
            ahead = scores(j + 1)
            return absorb(j, raw, state, False), ahead

        init = tuple((jnp.full((1, qb), -1e30, F32), jnp.zeros((1, qb), F32), jnp.zeros((DH, qb), F32))
                     for _ in range(hp_n))
        state, raw = lax.fori_loop(0, i, kstep, (init, scores(0)))
        state = absorb(i, raw, state, True)
        for hh in range(hp_n):
            m, l, acc = state[hh]
            o_ref[hh] = _tall_t(acc / l)
            lse_ref[pl.ds(hp * hp_n + hh, 1), qrows] = m + jnp.log(l)

    out = pl.pallas_call(
        body, name="fox_fwd", grid=(HEADS // hp_n, nq),
        in_specs=[pl.BlockSpec((1, hp_n, qb, DH), lambda h, i: (0, h, i, 0)),
                  pl.BlockSpec((1, hp_n, t, DH), lambda h, i: (1, h, 0, 0)),
                  pl.BlockSpec((1, hp_n, t, DH), lambda h, i: (2, h, 0, 0)),
                  pl.BlockSpec((t, LANE), lambda h, i: (0, 0)),
                  pl.BlockSpec((HEADS, t), lambda h, i: (2, 0))] + c_in_specs,
        out_specs=[pl.BlockSpec((hp_n, qb, DH), lambda h, i: (h, i, 0)),
                   pl.BlockSpec((HEADS, t), lambda h, i: (0, 0))] + c_out_specs,
        out_shape=[jax.ShapeDtypeStruct((HEADS, t, DH), F32), jax.ShapeDtypeStruct((HEADS, t), F32)] + c_out_shape,
        input_output_aliases={5 + j: 4 + j for j in range(n_c)},
        scratch_shapes=[pltpu.VMEM((hp_n, t, DH), MXU), pltpu.VMEM((hp_n, DH, t), MXU)] + ([c_sem] if carry else []),
        compiler_params=_params(("arbitrary", "arbitrary")),
    )(qkv, qkv, qkv, run, run_t, *c_in)
    return (out[0], out[1], _carry_state(out[2:])) if carry else (out[0], out[1])


def _fox_bwd(qkv, run, run_t, o, lse, do, carry=None):
    t = qkv.shape[2]
    qb = min(QB, t)
    nq = t // qb
    hp_n = FOX_HP_BWD
    c_in, c_in_specs, c_out_shape, c_out_specs, c_sem = _carry_operands(*carry) if carry else ([], [], [], [], None)
    n_c = len(c_in)

    def body(*refs):
        q_ref, k_ref, v_ref, run_ref, runt_ref, o_ref, lse_ref, do_ref = refs[:8]
        dqkv_ref, dcol_ref, drow_ref = refs[8 + n_c:11 + n_c]
        dqt_s = refs[11 + n_c + len(c_out_shape)]
        hp = pl.program_id(0)
        j = pl.program_id(1)

        if carry:
            _carry_step(carry[0], refs[8:8 + n_c], refs[11 + n_c], refs[12 + n_c], refs[-1], hp * nq + j,
                        (HEADS // hp_n) * nq)

        @pl.when(j == 0)
        def _():
            dqt_s[...] = jnp.zeros_like(dqt_s)

        @pl.when((j == 0) & (hp == 0))
        def _():
            dcol_ref[...] = jnp.zeros_like(dcol_ref)
            drow_ref[...] = jnp.zeros_like(drow_ref)

        krows = pl.ds(pl.multiple_of(j * qb, qb), qb)
        run_blk = run_ref[krows, :]
        ones8 = jnp.ones((8, DH), MXU)
        kb, kt, vb, fk = [], [], [], []
        for hh in range(hp_n):
            kf = k_ref[0, hh]
            kb.append(kf.astype(MXU))
            kt.append(_wide_t(kf).astype(MXU))
            vb.append(v_ref[0, hh].astype(MXU))
            fk.append(_key_side_f(run_blk, hp * hp_n + hh, qb))

        def block_rows(i):
            return pl.ds(pl.multiple_of(i * qb, qb), qb)

        def products(i):
            rows = block_rows(i)
            out = []
            for hh in range(hp_n):
                dout = do_ref[hh, rows, :]
                x = dout * o_ref[hh, rows, :]
                x_hi = x.astype(MXU)
                out.append((_dot_nt(kb[hh], q_ref[0, hh, rows, :] * Q_SCALE), _dot_nt(vb[hh], dout),
                            (_dot_nt(ones8, x_hi) + _dot_nt(ones8, x - x_hi.astype(F32)))[0:1, :]))
            return tuple(out)

        def absorb(i, prods, acc, diagonal):
            rows = block_rows(i)
            pieces = []
            for hh in range(hp_n):
                head = hp * hp_n + hh
                raw, dpt, drow = prods[hh]
                off = runt_ref[pl.ds(head, 1), rows] - lse_ref[pl.ds(head, 1), rows]
                st = raw + (off - fk[hh])
                if diagonal:
                    st = jnp.where(_diag_mask(qb), st, -1e30)
                pt = jnp.exp(st)
                dst = pt * (dpt - drow)
                drow_ref[pl.ds(head, 1), rows] += jnp.sum(dst, axis=0, keepdims=True)
                folded = dst[:, 0:LANE]
                for c in range(1, qb // LANE):
                    folded = folded + dst[:, c * LANE:(c + 1) * LANE]
                pieces.append((_dot(dst, q_ref[0, hh, rows, :] * Q_SCALE), _dot(pt, do_ref[hh, rows, :]),
                               _dot(kt[hh], dst), folded))
            out = []
            for hh in range(hp_n):
                dqt_s[hh, :, rows] += pieces[hh][2]
                out.append((acc[hh][0] + pieces[hh][0], acc[hh][1] + pieces[hh][1], acc[hh][2] + pieces[hh][3]))
            return tuple(out)

        def qstep(i, carry):
            acc, prods = carry
            ahead = products(jnp.minimum(i + 1, nq - 1))
            return absorb(i, prods, acc, False), ahead

        init = tuple((jnp.zeros((qb, DH), F32), jnp.zeros((qb, DH), F32), jnp.zeros((qb, LANE), F32))
                     for _ in range(hp_n))
        first = products(j)
        ahead = products(jnp.minimum(j + 1, nq - 1))
        acc = absorb(j, first, init, True)
        acc, _ = lax.fori_loop(j + 1, nq, qstep, (acc, ahead))
        for hh in range(hp_n):
            dk, dv, ds_sum = acc[hh]
            dqkv_ref[1, hh, krows, :] = dk
            dqkv_ref[2, hh, krows, :] = dv
            dcol_ref[krows, :] += _to_lane(-jnp.sum(ds_sum, axis=1, keepdims=True), 2 * HEADS + hp * hp_n + hh)

        @pl.when(j == nq - 1)
        def _():
            for hh in range(hp_n):
                for b in range(nq):
                    rows = slice(b * qb, (b + 1) * qb)
                    dqkv_ref[0, hh, rows, :] = _tall_t(dqt_s[hh, :, rows]) * Q_SCALE

    full = pl.BlockSpec((hp_n, t, DH), lambda h, j: (h, 0, 0))
    rows8 = pl.BlockSpec((HEADS, t), lambda h, j: (0, 0))
    out = pl.pallas_call(
        body, name="fox_bwd", grid=(HEADS // hp_n, nq),
        in_specs=[pl.BlockSpec((1, hp_n, t, DH), lambda h, j: (0, h, 0, 0)),
                  pl.BlockSpec((1, hp_n, qb, DH), lambda h, j: (1, h, j, 0)),
                  pl.BlockSpec((1, hp_n, qb, DH), lambda h, j: (2, h, j, 0)),
                  pl.BlockSpec((t, LANE), lambda h, j: (0, 0)), pl.BlockSpec((HEADS, t), lambda h, j: (2, 0)),
                  full, rows8, full] + c_in_specs,
        out_specs=[pl.BlockSpec((3, hp_n, t, DH), lambda h, j: (0, h, 0, 0)),
                   pl.BlockSpec((t, LANE), lambda h, j: (0, 0)), rows8] + c_out_specs,
        out_shape=[jax.ShapeDtypeStruct((3, HEADS, t, DH), F32), jax.ShapeDtypeStruct((t, LANE), F32),
                   jax.ShapeDtypeStruct((HEADS, t), F32)] + c_out_shape,
        input_output_aliases={8 + j: 5 + j for j in range(n_c)},
        scratch_shapes=[pltpu.VMEM((hp_n, DH, t), F32)] + ([c_sem] if carry else []),
        compiler_params=_params(("arbitrary", "arbitrary")),
    )(qkv, qkv, qkv, run, run_t, o, lse, do, *c_in)
    return (out[0], out[1], out[2], _carry_state(out[3:])) if carry else tuple(out)


Z_COL0 = 3 * WIDTH // LANE
FGATE_COL0 = 7 * WIDTH // LANE


def _gdn_post(o, pm, onw):
    t = pm.shape[0]

    def body(o_ref, z_ref, w_ref, m_ref, mt_ref):
        z = z_ref[...]
        sz = z * _sigmoid(z)
        halves = []
        for hh in range(2):
            ov = o_ref[hh]
            n = ov * lax.rsqrt(jnp.mean(ov * ov, axis=-1, keepdims=True) + EPS) * w_ref[...]
            halves.append(n * sz[:, hh * DH:(hh + 1) * DH])
        m = jnp.concatenate(halves, axis=1)
        m_ref[...] = m.astype(m_ref.dtype)
        mt_ref[...] = m.T.astype(mt_ref.dtype)

    return pl.pallas_call(
        body, name="gdn_post", grid=(WIDTH // LANE,),
        in_specs=[pl.BlockSpec((2, t, DH), lambda j: (j, 0, 0)), pl.BlockSpec((t, LANE), lambda j: (0, Z_COL0 + j)),
                  pl.BlockSpec((1, DH), lambda j: (0, 0))],
        out_specs=[pl.BlockSpec((t, LANE), lambda j: (0, j)), pl.BlockSpec((LANE, t), lambda j: (j, 0))],
        out_shape=[jax.ShapeDtypeStruct((t, WIDTH), MXU), jax.ShapeDtypeStruct((WIDTH, t), MXU)],
        compiler_params=_params(("arbitrary",)),
    )(o, pm, onw)


def _gdn_post_bwd(o, pm, onw, dmix):
    t = pm.shape[0]

    def body(o_ref, z_ref, w_ref, dm_ref, do_ref, dz_ref, dw_ref):
        @pl.when(pl.program_id(0) == 0)
        def _():
            dw_ref[...] = jnp.zeros_like(dw_ref)

        z = z_ref[...]
        sg = _sigmoid(z)
        sz = z * sg
        dsz = sg * (1.0 + z * (1.0 - sg))
        dm = dm_ref[...]
        for hh in range(2):
            cols = slice(hh * DH, (hh + 1) * DH)
            ov = o_ref[hh]
            r = lax.rsqrt(jnp.mean(ov * ov, axis=-1, keepdims=True) + EPS)
            xn = ov * r
            dmh = dm[:, cols]
            dn = dmh * sz[:, cols]
            dz_ref[:, cols] = dmh * (xn * w_ref[...]) * dsz[:, cols]
            dw_ref[...] += jnp.sum(dn * xn, axis=0, keepdims=True)
            g = dn * w_ref[...]
            do_ref[hh] = r * (g - xn * jnp.mean(g * xn, axis=-1, keepdims=True))

    return pl.pallas_call(
        body, name="gdn_post_bwd", grid=(WIDTH // LANE,),
        in_specs=[pl.BlockSpec((2, t, DH), lambda j: (j, 0, 0)), pl.BlockSpec((t, LANE), lambda j: (0, Z_COL0 + j)),
                  pl.BlockSpec((1, DH), lambda j: (0, 0)), pl.BlockSpec((t, LANE), lambda j: (0, j))],
        out_specs=[pl.BlockSpec((2, t, DH), lambda j: (j, 0, 0)), pl.BlockSpec((t, LANE), lambda j: (0, j)),
                   pl.BlockSpec((1, DH), lambda j: (0, 0))],
        out_shape=[jax.ShapeDtypeStruct((HEADS, t, DH), F32), jax.ShapeDtypeStruct((t, WIDTH), F32),
                   jax.ShapeDtypeStruct((1, DH), F32)],
        compiler_params=_params(("arbitrary",)),
    )(o, pm, onw, dmix)


def _fox_post(o, pm):
    t = pm.shape[0]

    def body(o_ref, g_ref, m_ref, mt_ref):
        m = jnp.concatenate([o_ref[0], o_ref[1]], axis=1) * _sigmoid(g_ref[...])
        m_ref[...] = m.astype(m_ref.dtype)
        mt_ref[...] = m.T.astype(mt_ref.dtype)

    return pl.pallas_call(
        body, name="fox_post", grid=(WIDTH // LANE,),
        in_specs=[pl.BlockSpec((2, t, DH), lambda j: (j, 0, 0)), pl.BlockSpec((t, LANE), lambda j: (0, FGATE_COL0 + j))],
        out_specs=[pl.BlockSpec((t, LANE), lambda j: (0, j)), pl.BlockSpec((LANE, t), lambda j: (j, 0))],
        out_shape=[jax.ShapeDtypeStruct((t, WIDTH), MXU), jax.ShapeDtypeStruct((WIDTH, t), MXU)],
        compiler_params=_params(("arbitrary",)),
    )(o, pm)


def _fox_post_bwd(o, pm, dmix):
    t = pm.shape[0]

    def body(o_ref, g_ref, dm_ref, do_ref, dg_ref):
        sg = _sigmoid(g_ref[...])
        dm = dm_ref[...]
        for hh in range(2):
            cols = slice(hh * DH, (hh + 1) * DH)
            do_ref[hh] = dm[:, cols] * sg[:, cols]
            dg_ref[:, cols] = dm[:, cols] * o_ref[hh] * (sg * (1.0 - sg))[:, cols]

    return pl.pallas_call(
        body, name="fox_post_bwd", grid=(WIDTH // LANE,),
        in_specs=[pl.BlockSpec((2, t, DH), lambda j: (j, 0, 0)), pl.BlockSpec((t, LANE), lambda j: (0, FGATE_COL0 + j)),
                  pl.BlockSpec((t, LANE), lambda j: (0, j))],
        out_specs=[pl.BlockSpec((2, t, DH), lambda j: (j, 0, 0)), pl.BlockSpec((t, LANE), lambda j: (0, j))],
        out_shape=[jax.ShapeDtypeStruct((HEADS, t, DH), F32), jax.ShapeDtypeStruct((t, WIDTH), F32)],
        compiler_params=_params(("arbitrary",)),
    )(o, pm, dmix)


def _tail(x, mixg, mixf, tgt, wo, n2w, wg, wu, wd, fw):
    t, d = x.shape
    dff = wg.shape[1]
    tb = min(TB, t)

    def body(x_ref, mg_ref, mf_ref, t_ref, wo_ref, n2_ref, wg_ref, wu_ref, wd_ref, fw_ref,
             h2_ref, act_ref, dgate_ref, dup_ref, dx3_ref, dx2_ref, dmg_ref, dmf_ref, dn2_ref, dfw_ref, loss_ref):
        @pl.when(pl.program_id(0) == 0)
        def _():
            dn2_ref[...] = jnp.zeros_like(dn2_ref)
            dfw_ref[...] = jnp.zeros_like(dfw_ref)
            loss_ref[...] = jnp.zeros_like(loss_ref)

        x2 = x_ref[...] + _dot(mg_ref[...], wo_ref[0:WIDTH, :]) + _dot(mf_ref[...], wo_ref[WIDTH:2 * WIDTH, :])
        r2 = lax.rsqrt(jnp.mean(x2 * x2, axis=-1, keepdims=True) + EPS)
        xn2 = x2 * r2
        h2_32 = xn2 * n2_ref[...]
        h2 = h2_32.astype(MXU)
        h2_ref[...] = h2_32.T.astype(MXU)
        gate = _dot(h2, wg_ref[...])
        up = _dot(h2, wu_ref[...])
        sg = _sigmoid(gate)
        sl = gate * sg
        act_32 = sl * up
        act = act_32.astype(MXU)
        act_ref[...] = act_32.T.astype(MXU)
        x3 = x2 + _dot(act, wd_ref[...])
        r3 = lax.rsqrt(jnp.mean(x3 * x3, axis=-1, keepdims=True) + EPS)
        xn3 = x3 * r3
        err = xn3 * fw_ref[...] - t_ref[...]
        loss_ref[...] += 0.5 * jnp.sum(jnp.mean(err * err, axis=-1, keepdims=True), axis=0, keepdims=True)
        dy = err * (1.0 / d)
        dfw_ref[...] += jnp.sum(dy * xn3, axis=0, keepdims=True)
        g3 = dy * fw_ref[...]
        dx3 = r3 * (g3 - xn3 * jnp.mean(g3 * xn3, axis=-1, keepdims=True))
        dx3_ref[...] = dx3.astype(MXU)
        dact = _dot_nt(dx3, wd_ref[...])
        dgate = (dact * up * (sg * (1.0 + gate * (1.0 - sg)))).astype(MXU)
        dup = (dact * sl).astype(MXU)
        dgate_ref[...] = dgate
        dup_ref[...] = dup
        dh2 = _dot_nt(dgate, wg_ref[...]) + _dot_nt(dup, wu_ref[...])
        dn2_ref[...] += jnp.sum(dh2 * xn2, axis=0, keepdims=True)
        g2 = dh2 * n2_ref[...]
        dx2 = dx3 + r2 * (g2 - xn2 * jnp.mean(g2 * xn2, axis=-1, keepdims=True))
        dx2_ref[...] = dx2
        dmg_ref[...] = _dot_nt(dx2, wo_ref[0:WIDTH, :])
        dmf_ref[...] = _dot_nt(dx2, wo_ref[WIDTH:2 * WIDTH, :])

    def tok(n):
        return pl.BlockSpec((tb, n), lambda i: (i, 0))

    def tok_t(n):
        return pl.BlockSpec((n, tb), lambda i: (0, i))

    acc = pl.BlockSpec((1, d), lambda i: (0, 0))
    sds = jax.ShapeDtypeStruct
    return pl.pallas_call(
        body, name="tail", grid=(t // tb,),
        in_specs=[tok(d), tok(WIDTH), tok(WIDTH), tok(d), _resident(wo.shape), _resident((1, d)),
                  _resident(wg.shape), _resident(wu.shape), _resident(wd.shape), _resident((1, d))],
        out_specs=[tok_t(d), tok_t(dff), tok(dff), tok(dff), tok(d), tok(d), tok(WIDTH), tok(WIDTH), acc, acc,
                   pl.BlockSpec((1, 1), lambda i: (0, 0))],
        out_shape=[sds((d, t), MXU), sds((dff, t), MXU), sds((t, dff), MXU), sds((t, dff), MXU), sds((t, d), MXU),
                   sds((t, d), F32), sds((t, WIDTH), F32), sds((t, WIDTH), F32), sds((1, d), F32), sds((1, d), F32),
                   sds((1, 1), F32)],
        compiler_params=_params(("arbitrary",)),
    )(x, mixg, mixf, tgt, wo, n2w, wg, wu, wd, fw)


def _wgrad(at, b, name):
    m, t = at.shape
    n = b.shape[1]
    bm = 256 if m % 256 == 0 else LANE
    cast = b.dtype != jnp.dtype(MXU)

    def body(a_ref, b_ref, o_ref, *scratch):
        if cast:
            @pl.when(pl.program_id(0) == 0)
            def _():
                scratch[0][...] = b_ref[...].astype(MXU)
        o_ref[...] = jnp.dot(a_ref[...], scratch[0][...] if cast else b_ref[...],
                             preferred_element_type=F32).astype(o_ref.dtype)

    return pl.pallas_call(
        body, name=name, grid=(m // bm,),
        in_specs=[pl.BlockSpec((bm, t), lambda i: (i, 0)), _resident((t, n))],
        out_specs=pl.BlockSpec((bm, n), lambda i: (i, 0)),
        out_shape=jax.ShapeDtypeStruct((m, n), WIRE),
        scratch_shapes=[pltpu.VMEM((t, n), MXU)] if cast else [],
        compiler_params=_params(("arbitrary",)),
    )(at, b)


def _split_w_in(w_in):
    a = 4 * WIDTH
    b = a + 2 * HEADS
    c = b + 4 * WIDTH
    main = jnp.concatenate([w_in[:, :a], w_in[:, b:c]], axis=1)
    small = jnp.concatenate([w_in[:, a:b], w_in[:, c:], jnp.zeros((w_in.shape[0], LANE - 3 * HEADS), w_in.dtype)], axis=1)
    return main, small


def _merge_dw_in(d_gdn, d_z, d_fox, d_fg, d_small):
    return jnp.concatenate([d_gdn, d_z, d_small[:, :2 * HEADS], d_fox, d_fg, d_small[:, 2 * HEADS:3 * HEADS]], axis=1)


def _lanes(*pieces):
    v = jnp.concatenate([p.reshape(-1).astype(F32) for p in pieces])
    return jnp.pad(v, (0, LANE - v.shape[0])).reshape(1, LANE)


def _vector_params(p):
    d = p["norm1_w"].size
    gparams = jnp.concatenate([_lanes(jnp.zeros(HEADS), p["gdn_dt_bias"], p["fox_f_bias"]),
                               _lanes(jnp.zeros(HEADS), p["gdn_A_log"]), jnp.zeros((6, LANE), F32)])
    fox_nw = jnp.stack([jnp.tile(p["fox_q_norm_w"].reshape(-1), 2), jnp.tile(p["fox_k_norm_w"].reshape(-1), 2),
                        jnp.ones((LANE,), F32)])
    return dict(n1w=p["norm1_w"].reshape(1, d), n2w=p["norm2_w"].reshape(1, d), fw=p["final_norm_w"].reshape(1, d),
                onw=p["gdn_out_norm_w"].reshape(1, DH), gparams=gparams, fox_nw=fox_nw)


def _mixer_forward(x, vp, w_main, w_small, conv_w, carry=None):
    h1, pm, ps = _inproj(x, vp["n1w"], w_main, w_small)
    gates, run, tot, run_t = _gates(ps, vp["gparams"])
    fqkv = _fox_prep(pm, vp["fox_nw"])
    o_fox, lse, *carried = _fox_fwd(fqkv, run, run_t, carry)
    if carry:
        pm, o_fox = lax.optimization_barrier((pm, o_fox))
    mixf, mixf_t = _fox_post(o_fox, pm)
    gqkv = _gdn_prep(pm, conv_w)
    o_gdn, states, inv = _gdn_fwd(gqkv, gates, run, tot, run_t)
    mixg, mixg_t = _gdn_post(o_gdn, pm, vp["onw"])
    return dict(h1=h1, pm=pm, ps=ps, gates=gates, run=run, tot=tot, run_t=run_t, gqkv=gqkv, o_gdn=o_gdn, states=states,
                inv=inv, mixg=mixg, mixg_t=mixg_t, fqkv=fqkv, o_fox=o_fox, lse=lse, mixf=mixf, mixf_t=mixf_t,
                carried=carried[0] if carried else None)


def _mixer_backward(x, vp, w_main, w_small, conv_w, f, dx2, dmixg, dmixf, carry=None, scatter_own=False):
    pm = f["pm"]
    do_fox, dfg = _fox_post_bwd(f["o_fox"], pm, dmixf)
    dfqkv, dcol_f, drow_f, *carried = _fox_bwd(f["fqkv"], f["run"], f["run_t"], f["o_fox"], f["lse"], do_fox, carry)
    dfox, dfnw = _fox_prep_bwd(pm, vp["fox_nw"], dfqkv)
    do_gdn, dz, donw = _gdn_post_bwd(f["o_gdn"], pm, vp["onw"], dmixg)
    dgqkv, dcol_g, dtot_g, drow_g = _gdn_bwd(f["gqkv"], f["gates"], f["run"], f["tot"], f["run_t"], f["inv"], f["states"], do_gdn)
    dgdn, dconv = _gdn_prep_bwd(pm, conv_w, dgqkv)
    dps, gsum = _gates_bwd(f["ps"], vp["gparams"], dcol_g, dtot_g, dcol_f, drow_g, drow_f)
    h1 = f["h1"]
    dw_in = _merge_dw_in(_wgrad(h1, dgdn, "dw_in_gdn"), _wgrad(h1, dz, "dw_in_z"), _wgrad(h1, dfox, "dw_in_fox"),
                         _wgrad(h1, dfg, "dw_in_fgate"), _wgrad(h1, dps, "dw_in_small"))
    own = ("scatter", [_cut(dw_in, 1), _cut(dconv, 1)]) if scatter_own else None
    grad_x, dn1w, *sent = _inproj_bwd(x, vp["n1w"], dx2, dgdn, dz, dfox, dfg, dps, w_main, w_small, own)
    small = dict(dn1w=dn1w, gsum=gsum, donw=donw, dfnw=dfnw)
    return (grad_x, dw_in, dconv, small, *carried, *sent)


VECTORS = ("norm1_w", "norm2_w", "final_norm_w", "gdn_A_log", "gdn_dt_bias", "gdn_out_norm_w", "fox_f_bias",
           "fox_q_norm_w", "fox_k_norm_w")
VEC_ROWS = 16
LOSS_ROW = len(VECTORS)


def _pack_vectors(dn1w, dn2w, dfw, gsum, donw, dfnw, loss):
    d = dn1w.shape[1]

    def body(n1_ref, n2_ref, fw_ref, gs_ref, on_ref, fn_ref, loss_ref, o_ref):
        o_ref[...] = jnp.zeros_like(o_ref)
        o_ref[0:1, :] = n1_ref[...]
        o_ref[1:2, :] = n2_ref[...]
        o_ref[2:3, :] = fw_ref[...]
        o_ref[3:4, 0:HEADS] = gs_ref[0:1, 0:HEADS]
        o_ref[4:5, 0:HEADS] = gs_ref[1:2, 0:HEADS]
        o_ref[5:6, 0:DH] = on_ref[...]
        o_ref[6:7, 0:HEADS] = gs_ref[2:3, 0:HEADS]
        for kind in range(2):
            v = fn_ref[kind, 0]
            for j in range(1, fn_ref.shape[1]):
                v = v + fn_ref[kind, j]
            o_ref[7 + kind:8 + kind, 0:DH] = v[:, :DH] + v[:, DH:]
        o_ref[LOSS_ROW:LOSS_ROW + 1, 0:1] = loss_ref[...]

    return pl.pallas_call(body, name="pack_vectors", out_shape=jax.ShapeDtypeStruct((VEC_ROWS, d), F32),
                          compiler_params=_params())(dn1w, dn2w, dfw, gsum, donw, dfnw, loss)


def _late_grads(f, h2t, actt, dgate, dup, dx3, dx2):
    return {"w_out": jnp.concatenate([_wgrad(f["mixg_t"], dx2, "dw_out_gdn"), _wgrad(f["mixf_t"], dx2, "dw_out_fox")], axis=0),
            "w_ffn_gate": _wgrad(h2t, dgate, "dw_gate"), "w_ffn_up": _wgrad(h2t, dup, "dw_up"),
            "w_ffn_down": _wgrad(actt, dx3, "dw_down")}


def _local_step(x, tgt, p, w_in, conv_w, wo, wg, wu, wd):
    vp = _vector_params(p)
    w_main, w_small = _split_w_in(w_in)
    f = _mixer_forward(x, vp, w_main, w_small, conv_w)
    (h2t, actt, dgate, dup, dx3, dx2, dmixg, dmixf, dn2w, dfw, loss) = _tail(
        x, f["mixg"], f["mixf"], tgt, wo, vp["n2w"], wg, wu, wd, vp["fw"])
    grad_x, dw_in, dconv, small = _mixer_backward(x, vp, w_main, w_small, conv_w, f, dx2, dmixg, dmixf)
    grads = {"w_in": dw_in, "gdn_conv_w": dconv, **_late_grads(f, h2t, actt, dgate, dup, dx3, dx2)}
    vec = _pack_vectors(small["dn1w"], dn2w, dfw, small["gsum"], small["donw"], small["dfnw"], loss)
    return grad_x, grads, vec


def _my_place():
    return lax.axis_index("x"), lax.axis_index("y"), lax.axis_index("c")


def _peers():
    x, y, c = _my_place()
    peers = []
    for k in range(1, N_DEV):
        px = 1 - x if k & 4 else x
        py = 1 - y if k & 2 else y
        pc = 1 - c if k & 1 else c
        peers.append(((px, py, pc), 4 * px + 2 * py + pc))
    return 4 * x + 2 * y + c, peers


def _spread_copies(kind, srcs, lands, send_sems, recv_sems):
    me, peers = _peers()
    kinds = [kind] * len(srcs) if isinstance(kind, str) else kind
    remote, local = [], []
    for i, (kd, src, land) in enumerate(zip(kinds, srcs, lands)):
        for k, (dev, idx) in enumerate(peers):
            remote.append(pltpu.make_async_remote_copy(
                src_ref=src if kd == "gather" else src.at[idx], dst_ref=land.at[me],
                send_sem=send_sems.at[i * (N_DEV - 1) + k], recv_sem=recv_sems.at[i * (N_DEV - 1) + k],
                device_id=dev, device_id_type=MESH))
        local.append((src if kd == "gather" else src.at[me], land.at[me]))
    return remote, local


def _land_shape(kind, a):
    return (N_DEV,) + a.shape if kind == "gather" else a.shape


HBM = pl.BlockSpec(memory_space=pltpu.HBM)
SEM = pl.BlockSpec(memory_space=pltpu.SEMAPHORE)


def _hbm(a):
    return pltpu.with_memory_space_constraint(a, pltpu.HBM)


def _carry_operands(kind, arrays):
    n = len(arrays)
    kinds = [kind] * n if isinstance(kind, str) else kind
    lands = [lax.empty(_land_shape(kd, a), a.dtype) for kd, a in zip(kinds, arrays)]
    sems = [pltpu.SemaphoreType.DMA((n * (N_DEV - 1),))] * 2
    return ([_hbm(a) for a in list(arrays) + lands], [HBM] * (2 * n),
            sems + [pltpu.HBM(a.shape, a.dtype) for a in list(arrays) + lands], [SEM] * 2 + [HBM] * (2 * n),
            pltpu.SemaphoreType.DMA((n,)))


def _carry_step(kind, in_refs, send_sems, recv_sems, local_sems, step, n_steps):
    n = len(in_refs) // 2
    remote, local = _spread_copies(kind, in_refs[:n], in_refs[n:], send_sems, recv_sems)
    copies = [pltpu.make_async_copy(s, d, local_sems.at[i]) for i, (s, d) in enumerate(local)]
    per_step = -(-len(remote) // n_steps)
    for s in range(-(-len(remote) // per_step)):
        @pl.when(step == s)
        def _():
            for cp in remote[s * per_step:(s + 1) * per_step]:
                cp.start()
            if s == 0:
                for cp in copies:
                    cp.start()

    @pl.when(step == n_steps - 1)
    def _():
        for cp in copies:
            cp.wait()


def _carry_state(extra_out):
    n = (len(extra_out) - 2) // 2
    return list(extra_out[2:2 + n]), list(extra_out[2 + n:]), extra_out[0], extra_out[1]


def _spread_start(arrays, kind, name):
    n = len(arrays)

    def body(*refs):
        srcs, lands = refs[:n], refs[n:2 * n]
        send_sems, recv_sems = refs[2 * n], refs[2 * n + 1]
        token = refs[4 * n + 2]
        local_sems = refs[4 * n + 3]
        remote, local = _spread_copies(kind, srcs, lands, send_sems, recv_sems)
        for cp in remote:
            cp.start()
        copies = [pltpu.make_async_copy(s, d, local_sems.at[i]) for i, (s, d) in enumerate(local)]
        for cp in copies:
            cp.start()
        for cp in copies:
            cp.wait()
        token[...] = jnp.zeros_like(token)

    sems = (pltpu.SemaphoreType.DMA((n * (N_DEV - 1),)),) * 2
    kinds = [kind] * n if isinstance(kind, str) else kind
    lands = [lax.empty(_land_shape(kd, a), a.dtype) for kd, a in zip(kinds, arrays)]
    out = pl.pallas_call(
        body, name=name,
        out_shape=sems + tuple(pltpu.HBM(a.shape, a.dtype) for a in list(arrays) + lands)
        + (jax.ShapeDtypeStruct((8, LANE), F32),),
        in_specs=[HBM] * (2 * n), out_specs=tuple([SEM] * 2 + [HBM] * (2 * n) + [pl.BlockSpec(memory_space=pltpu.VMEM)]),
        input_output_aliases={j: 2 + j for j in range(2 * n)},
        scratch_shapes=[pltpu.SemaphoreType.DMA((n,))],
        compiler_params=pltpu.CompilerParams(has_side_effects=pltpu.SideEffectType.DATAFLOW_SIDE_EFFECTING),
    )(*[_hbm(a) for a in arrays], *[_hbm(a) for a in lands])
    return (list(out[2:2 + n]), list(out[2 + n:2 + 2 * n]), out[0], out[1]), out[-1]


def _spread_wait(state, kind, after, name):
    srcs, lands, send_sems, recv_sems = state
    n = len(srcs)
    after = list(after) if isinstance(after, (list, tuple)) else [after]

    def body(*refs):
        remote, _ = _spread_copies(kind, refs[:n], refs[n:2 * n], refs[2 * n], refs[2 * n + 1])
        for cp in remote:
            cp.wait_send()
        for cp in remote:
            cp.wait_recv()

    out = pl.pallas_call(
        body, name=name,
        out_shape=tuple(pltpu.HBM(a.shape, a.dtype) for a in srcs + lands),
        in_specs=[HBM] * (2 * n) + [SEM, SEM] + [pl.BlockSpec(memory_space=pl.ANY)] * len(after),
        out_specs=tuple([HBM] * (2 * n)),
        input_output_aliases={j: j for j in range(2 * n)},
        compiler_params=pltpu.CompilerParams(has_side_effects=pltpu.SideEffectType.DATAFLOW_SIDE_EFFECTING),
    )(*srcs, *lands, send_sems, recv_sems, *after)
    return list(out[n:])


ADAM_ROWS = 128


def _adam_math(g, w, m, v):
    nm = ADAM_B1 * m + (1.0 - ADAM_B1) * g
    nv = ADAM_B2 * v + (1.0 - ADAM_B2) * (g * g)
    m_hat = nm / (1.0 - ADAM_B1 ** ADAM_STEP)
    v_hat = nv / (1.0 - ADAM_B2 ** ADAM_STEP)
    return -ADAM_LR * (m_hat / (jnp.sqrt(v_hat) + ADAM_EPS) + ADAM_WD * w), nm, nv


def _sum_parts(p_ref):
    g = p_ref[0].astype(F32)
    for s in range(1, N_DEV):
        g = g + p_ref[s].astype(F32)
    return g


def _adam_matrix(parts, w, m, v, name):
    _, r, c = w.shape
    rb = ADAM_ROWS if r % ADAM_ROWS == 0 else r

    def body(p_ref, w_ref, m_ref, v_ref, g_ref, d_ref, nm_ref, nv_ref):
        g = _sum_parts(p_ref)
        g_ref[0] = g
        d_ref[0], nm_ref[0], nv_ref[0] = _adam_math(g, w_ref[0], m_ref[0], v_ref[0])

    blk = pl.BlockSpec((1, rb, c), lambda i: (0, i, 0))
    return pl.pallas_call(
        body, name=name, grid=(r // rb,),
        in_specs=[pl.BlockSpec((N_DEV, rb, c), lambda i: (0, i, 0)), blk, blk, blk],
        out_specs=[blk] * 4, out_shape=[jax.ShapeDtypeStruct(w.shape, F32)] * 4,
        compiler_params=_params(("arbitrary",)),
    )(parts, w, m, v)


def _adam_vectors(parts, ws, ms, vs):
    nv = len(ws)

    def body(*refs):
        p_ref = refs[0]
        w_refs, m_refs, v_refs = refs[1:1 + nv], refs[1 + nv:1 + 2 * nv], refs[1 + 2 * nv:1 + 3 * nv]
        outs = refs[1 + 3 * nv:]
        g_all = _sum_parts(p_ref)
        for i in range(nv):
            n = w_refs[i].shape[1]
            g = g_all[i:i + 1, 0:n]
            d, nm, nvv = _adam_math(g, w_refs[i][...], m_refs[i][...], v_refs[i][...])
            outs[i][...] = g
            outs[nv + i][...] = d
            outs[2 * nv + i][...] = nm
            outs[3 * nv + i][...] = nvv
        outs[4 * nv][...] = g_all[LOSS_ROW:LOSS_ROW + 1, 0:1]

    shapes = [jax.ShapeDtypeStruct(a.shape, F32) for a in ws]
    out = pl.pallas_call(body, name="adam_vectors", out_shape=shapes * 4 + [jax.ShapeDtypeStruct((1, 1), F32)],
                         compiler_params=_params())(parts, *ws, *ms, *vs)
    return out[:nv], out[nv:2 * nv], out[2 * nv:3 * nv], out[3 * nv:4 * nv], out[4 * nv]


MATRICES = (("w_in", 1), ("gdn_conv_w", 1), ("w_out", 0), ("w_ffn_gate", 1), ("w_ffn_up", 1), ("w_ffn_down", 0))
WEIGHTS = ("norm1_w", "w_in", "gdn_conv_w", "gdn_A_log", "gdn_dt_bias", "gdn_out_norm_w", "fox_f_bias", "fox_q_norm_w",
           "fox_k_norm_w", "w_out", "norm2_w", "w_ffn_gate", "w_ffn_up", "w_ffn_down", "final_norm_w")


def _join(blocks, axis):
    _, r, c = blocks.shape
    if axis == 0:
        return blocks.reshape(N_DEV * r, c)
    return blocks.transpose(1, 0, 2).reshape(r, N_DEV * c)


def _cut(full, axis):
    r, c = full.shape
    if axis == 0:
        return full.reshape(N_DEV, r // N_DEV, c)
    return full.reshape(r, N_DEV, c // N_DEV).transpose(1, 0, 2)


def kernel(x, norm1_w, w_in, gdn_conv_w, gdn_A_log, gdn_dt_bias, gdn_out_norm_w, fox_f_bias, fox_q_norm_w, fox_k_norm_w, w_out, norm2_w, w_ffn_gate, w_ffn_up, w_ffn_down, final_norm_w, loss_target, m_norm1_w, m_w_in, m_gdn_conv_w, m_gdn_A_log, m_gdn_dt_bias, m_gdn_out_norm_w, m_fox_f_bias, m_fox_q_norm_w, m_fox_k_norm_w, m_w_out, m_norm2_w, m_w_ffn_gate, m_w_ffn_up, m_w_ffn_down, m_final_norm_w, v_norm1_w, v_w_in, v_gdn_conv_w, v_gdn_A_log, v_gdn_dt_bias, v_gdn_out_norm_w, v_fox_f_bias, v_fox_q_norm_w, v_fox_k_norm_w, v_w_out, v_norm2_w, v_w_ffn_gate, v_w_ffn_up, v_w_ffn_down, v_final_norm_w):
    w = dict(norm1_w=norm1_w, w_in=w_in, gdn_conv_w=gdn_conv_w, gdn_A_log=gdn_A_log, gdn_dt_bias=gdn_dt_bias,
             gdn_out_norm_w=gdn_out_norm_w, fox_f_bias=fox_f_bias, fox_q_norm_w=fox_q_norm_w, fox_k_norm_w=fox_k_norm_w,
             w_out=w_out, norm2_w=norm2_w, w_ffn_gate=w_ffn_gate, w_ffn_up=w_ffn_up, w_ffn_down=w_ffn_down,
             final_norm_w=final_norm_w)
    m = dict(norm1_w=m_norm1_w, w_in=m_w_in, gdn_conv_w=m_gdn_conv_w, gdn_A_log=m_gdn_A_log, gdn_dt_bias=m_gdn_dt_bias,
             gdn_out_norm_w=m_gdn_out_norm_w, fox_f_bias=m_fox_f_bias, fox_q_norm_w=m_fox_q_norm_w,
             fox_k_norm_w=m_fox_k_norm_w, w_out=m_w_out, norm2_w=m_norm2_w, w_ffn_gate=m_w_ffn_gate,
             w_ffn_up=m_w_ffn_up, w_ffn_down=m_w_ffn_down, final_norm_w=m_final_norm_w)
    v = dict(norm1_w=v_norm1_w, w_in=v_w_in, gdn_conv_w=v_gdn_conv_w, gdn_A_log=v_gdn_A_log, gdn_dt_bias=v_gdn_dt_bias,
             gdn_out_norm_w=v_gdn_out_norm_w, fox_f_bias=v_fox_f_bias, fox_q_norm_w=v_fox_q_norm_w,
             fox_k_norm_w=v_fox_k_norm_w, w_out=v_w_out, norm2_w=v_norm2_w, w_ffn_gate=v_w_ffn_gate,
             w_ffn_up=v_w_ffn_up, w_ffn_down=v_w_ffn_down, final_norm_w=v_final_norm_w)
    axis_of = dict(MATRICES)
    late = ("w_out", "w_ffn_gate", "w_ffn_up", "w_ffn_down")
    xs, tgt = x[0], loss_target[0]
    vp = _vector_params({n: w[n] for n in VECTORS})

    state_in, token = _spread_start([w["w_in"][0].astype(WIRE), w["gdn_conv_w"][0]], "gather", "gather_in_start")
    w_in_blocks, conv_blocks = _spread_wait(state_in, "gather", token, "gather_in_wait")
    w_main, w_small = _split_w_in(_join(w_in_blocks, 1))
    conv_w = _join(conv_blocks, 1)
    f = _mixer_forward(xs, vp, w_main, w_small, conv_w, ("gather", [w[n][0].astype(WIRE) for n in late]))
    full = {n: _join(b, axis_of[n])
            for n, b in zip(late, _spread_wait(f["carried"], "gather", f["mixg"], "gather_late_wait"))}

    (h2t, actt, dgate, dup, dx3, dx2, dmixg, dmixf, dn2w, dfw, loss) = _tail(
        xs, f["mixg"], f["mixf"], tgt, full["w_out"], vp["n2w"], full["w_ffn_gate"], full["w_ffn_up"],
        full["w_ffn_down"], vp["fw"])
    dlate = _late_grads(f, h2t, actt, dgate, dup, dx3, dx2)

    grad_x, dw_in, dconv, small, state_grads, state_own = _mixer_backward(
        xs, vp, w_main, w_small, conv_w, f, dx2, dmixg, dmixf, ("scatter", [_cut(dlate[n], axis_of[n]) for n in late]),
        scatter_own=True)
    vec = _pack_vectors(small["dn1w"], dn2w, dfw, small["gsum"], small["donw"], small["dfnw"], loss)

    state_vec, token = _spread_start([vec], "gather", "vectors_start")
    parts = dict(zip(late, _spread_wait(state_grads, "scatter", token, "grads_late_wait")))
    results = [{}, {}, {}, {}]

    def update(n):
        for d, a in zip(results, _adam_matrix(parts[n], w[n], m[n], v[n], "adam_" + n)):
            d[n] = a

    for n in late:
        update(n)
    parts["w_in"], parts["gdn_conv_w"] = _spread_wait(state_own, "scatter", [results[0][n] for n in late], "own_wait")
    (parts_vec,) = _spread_wait(state_vec, "gather", parts["w_in"], "vectors_wait")
    update("w_in")
    update("gdn_conv_w")
    row = lambda a: a.reshape(1, -1)
    *vec_out, total_loss = _adam_vectors(parts_vec, [row(w[n]) for n in VECTORS], [row(m[n]) for n in VECTORS],
                                         [row(v[n]) for n in VECTORS])
    for d, arrs in zip(results, vec_out):
        for n, a in zip(VECTORS, arrs):
            d[n] = a.reshape(w[n].shape)
    return (total_loss[0, 0], grad_x[None], *[d[n] for d in results for n in WEIGHTS])
```

```python
import functools

import jax
import jax.numpy as jnp
from jax import lax
from jax.experimental import pallas as pl
from jax.experimental.pallas import tpu as pltpu

F32 = jnp.float32
MXU = jnp.bfloat16
WIRE = jnp.bfloat16
HI = lax.Precision.HIGHEST
EPS = 1e-6

N_DEV = 8
HEADS = 8
DH = 64
WIDTH = HEADS * DH
CHUNK = 64
LANE = 128
ROW_ALIGN = 16
TB = 256
QB = 256
VMEM_LIMIT = 60 * 1024 * 1024

ADAM_LR = 0.001
ADAM_B1 = 0.9
ADAM_B2 = 0.999
ADAM_EPS = 1e-08
ADAM_WD = 0.01
ADAM_STEP = 10

MESH = pl.DeviceIdType.MESH


def _params(sem=None):
    return pltpu.CompilerParams(dimension_semantics=sem, vmem_limit_bytes=VMEM_LIMIT)


def _resident(shape):
    n = len(shape)
    return pl.BlockSpec(shape, lambda *_: (0,) * n, pipeline_mode=pl.Buffered(1))


def _dot(a, b):
    return jnp.dot(a.astype(MXU), b.astype(MXU), preferred_element_type=F32)


def _dot_nt(a, b):
    return lax.dot_general(a.astype(MXU), b.astype(MXU), (((1,), (1,)), ((), ())), preferred_element_type=F32)


def _dot_tn(a, b):
    return lax.dot_general(a.astype(MXU), b.astype(MXU), (((0,), (0,)), ((), ())), preferred_element_type=F32)


def _hdot(a, b):
    return jnp.dot(a, b, precision=HI, preferred_element_type=F32)


def _hdot_nt(a, b):
    return lax.dot_general(a, b, (((1,), (1,)), ((), ())), precision=HI, preferred_element_type=F32)


def _hdot_tn(a, b):
    return lax.dot_general(a, b, (((0,), (0,)), ((), ())), precision=HI, preferred_element_type=F32)


def _sigmoid(x):
    return 0.5 * jnp.tanh(0.5 * x) + 0.5


def _softplus(x):
    return jnp.maximum(x, 0.0) + jnp.log(1.0 + jnp.exp(-jnp.abs(x)))


def _head_sum_matrix():
    ri = lax.broadcasted_iota(jnp.int32, (LANE, LANE), 0) // DH
    ci = lax.broadcasted_iota(jnp.int32, (LANE, LANE), 1) // DH
    return (ri == ci).astype(F32)


def _group_sum(a, ones_matrix):
    hi = a.astype(jnp.bfloat16)
    lo = (a - hi.astype(F32)).astype(jnp.bfloat16)
    m = ones_matrix.astype(jnp.bfloat16)
    return jnp.dot(hi, m, preferred_element_type=F32) + jnp.dot(lo, m, preferred_element_type=F32)


def _shift_down(x, s):
    return pltpu.roll(x, s, 0)


def _shift_up(x, s):
    return pltpu.roll(x, x.shape[0] - s, 0)


def _inproj(x, n1w, w_main, w_small):
    t, d = x.shape
    nm = w_main.shape[1]
    tb = min(TB, t)

    def body(x_ref, nw_ref, wm_ref, ws_ref, ht_ref, pm_ref, ps_ref):
        xv = x_ref[...]
        r = lax.rsqrt(jnp.mean(xv * xv, axis=-1, keepdims=True) + EPS)
        h32 = xv * r * nw_ref[...]
        h = h32.astype(MXU)
        ht_ref[...] = h32.T.astype(MXU)
        pm_ref[...] = jnp.dot(h, wm_ref[...], preferred_element_type=F32)
        ps_ref[...] = jnp.dot(h, ws_ref[...], preferred_element_type=F32)

    return pl.pallas_call(
        body, name="inproj", grid=(t // tb,),
        in_specs=[pl.BlockSpec((tb, d), lambda i: (i, 0)), _resident((1, d)), _resident((d, nm)), _resident((d, LANE))],
        out_specs=[pl.BlockSpec((d, tb), lambda i: (0, i)), pl.BlockSpec((tb, nm), lambda i: (i, 0)),
                   pl.BlockSpec((tb, LANE), lambda i: (i, 0))],
        out_shape=[jax.ShapeDtypeStruct((d, t), MXU), jax.ShapeDtypeStruct((t, nm), F32),
                   jax.ShapeDtypeStruct((t, LANE), F32)],
        compiler_params=_params(("arbitrary",)),
    )(x, n1w, w_main, w_small)


def _inproj_bwd(x, n1w, dx2, dgdn, dz, dfox, dfg, dps, w_main, w_small, carry=None):
    t, d = x.shape
    tb = min(TB, t)
    w3 = 3 * WIDTH
    c_in, c_in_specs, c_out_shape, c_out_specs, c_sem = _carry_operands(*carry) if carry else ([], [], [], [], None)
    n_c = len(c_in)

    def body(*refs):
        x_ref, nw_ref, dx2_ref, dgdn_ref, dz_ref, dfox_ref, dfg_ref, dps_ref, wm_ref, ws_ref = refs[:10]
        gx_ref, dnw_ref = refs[10 + n_c:12 + n_c]
        if carry:
            _carry_step(carry[0], refs[10:10 + n_c], refs[12 + n_c], refs[13 + n_c], refs[-1], pl.program_id(0), t // tb)
        dh = _dot_nt(dgdn_ref[...], wm_ref[:, 0:w3])
        dh += _dot_nt(dz_ref[...], wm_ref[:, w3:w3 + WIDTH])
        dh += _dot_nt(dfox_ref[...], wm_ref[:, w3 + WIDTH:2 * w3 + WIDTH])
        dh += _dot_nt(dfg_ref[...], wm_ref[:, 2 * w3 + WIDTH:2 * w3 + 2 * WIDTH])
        dh += _dot_nt(dps_ref[...], ws_ref[...])
        xv = x_ref[...]
        r = lax.rsqrt(jnp.mean(xv * xv, axis=-1, keepdims=True) + EPS)
        xn = xv * r

        @pl.when(pl.program_id(0) == 0)
        def _():
            dnw_ref[...] = jnp.zeros_like(dnw_ref)

        dnw_ref[...] += jnp.sum(dh * xn, axis=0, keepdims=True)
        g = dh * nw_ref[...]
        gx_ref[...] = dx2_ref[...] + r * (g - xn * jnp.mean(g * xn, axis=-1, keepdims=True))

    def tok(n):
        return pl.BlockSpec((tb, n), lambda i: (i, 0))

    out = pl.pallas_call(
        body, name="inproj_bwd", grid=(t // tb,),
        in_specs=[tok(d), _resident((1, d)), tok(d), tok(w3), tok(WIDTH), tok(w3), tok(WIDTH), tok(LANE),
                  _resident(w_main.shape), _resident(w_small.shape)] + c_in_specs,
        out_specs=[tok(d), pl.BlockSpec((1, d), lambda i: (0, 0))] + c_out_specs,
        out_shape=[jax.ShapeDtypeStruct((t, d), F32), jax.ShapeDtypeStruct((1, d), F32)] + c_out_shape,
        input_output_aliases={10 + j: 4 + j for j in range(n_c)},
        scratch_shapes=[c_sem] if carry else [],
        compiler_params=_params(("arbitrary",)),
    )(x, n1w, dx2, dgdn, dz, dfox, dfg, dps, w_main, w_small, *c_in)
    return (out[0], out[1], _carry_state(out[2:])) if carry else tuple(out)


def _gate_lanes(shape):
    lane = lax.broadcasted_iota(jnp.int32, shape, 1)
    return lane < HEADS, (lane >= HEADS) & (lane < 2 * HEADS), (lane >= 2 * HEADS) & (lane < 3 * HEADS)


def _block_masks():
    ri = lax.broadcasted_iota(jnp.int32, (LANE, LANE), 0)
    ci = lax.broadcasted_iota(jnp.int32, (LANE, LANE), 1)
    same = (ri // CHUNK) == (ci // CHUNK)
    return ((ri >= ci).astype(F32), (ri <= ci).astype(F32), (same & (ri >= ci)).astype(F32),
            (same & (ri <= ci)).astype(F32), same.astype(F32))


def _gates(ps, gparams):
    t = ps.shape[0]
    nb = t // LANE

    def body(ps_ref, gp_ref, out_ref, run_ref, tot_ref, runt_ref):
        p = ps_ref[...]
        is_b, is_a, is_f = _gate_lanes(p.shape)
        z = p + gp_ref[0:1, :]
        neg_exp_a = -jnp.exp(gp_ref[1:2, :])
        glog = neg_exp_a * _softplus(z)
        logf = -_softplus(-z)
        out_ref[...] = jnp.where(is_b, _sigmoid(p), jnp.where(is_a, glog, jnp.where(is_f, logf, 0.0)))
        tril, _, tril_c, _, same_c = _block_masks()
        off = jnp.zeros((1, LANE), F32)
        for b in range(nb):
            rows = slice(b * LANE, (b + 1) * LANE)
            blk = out_ref[rows, :]
            ga = jnp.where(is_a[:LANE], blk, 0.0)
            fb = _hdot(tril, jnp.where(is_f[:LANE], blk, 0.0)) + off
            run = fb + _hdot(tril_c, ga)
            run_ref[rows, :] = run
            runt_ref[:, rows] = run.T
            tot_ref[rows, :] = _hdot(same_c, ga)
            off = fb[LANE - 1:LANE, :]

    return pl.pallas_call(
        body, name="gates",
        out_shape=[jax.ShapeDtypeStruct((t, LANE), F32)] * 3 + [jax.ShapeDtypeStruct((LANE, t), F32)],
        compiler_params=_params(),
    )(ps, gparams)


def _gates_bwd(ps, gparams, dcol_g, dtot_g, dcol_f, drow_g, drow_f):
    t = ps.shape[0]
    nb = t // LANE

    def body(ps_ref, gp_ref, dcg_ref, dtg_ref, dcf_ref, drg_ref, drf_ref, dps_ref, sums_ref, dl_ref, d0_ref, tr_ref):
        p = ps_ref[...]
        is_b, is_a, is_f = _gate_lanes(p.shape)
        _, triu, _, triu_c, same_c = _block_masks()
        tr_ref[...] = jnp.zeros_like(tr_ref)
        off = jnp.zeros((1, LANE), F32)
        for b in reversed(range(nb)):
            rows = slice(b * LANE, (b + 1) * LANE)
            tr_ref[HEADS:2 * HEADS, :] = drg_ref[:, rows]
            tr_ref[2 * HEADS:3 * HEADS, :] = drf_ref[:, rows]
            d = dcg_ref[rows, :] + dcf_ref[rows, :] + tr_ref[...].T
            d0_ref[rows, :] = d
            dlf = _hdot(triu, jnp.where(is_f[:LANE], d, 0.0)) + off
            dla = (_hdot(triu_c, jnp.where(is_a[:LANE], d, 0.0))
                   + _hdot(same_c, jnp.where(is_a[:LANE], dtg_ref[rows, :], 0.0)))
            dl_ref[rows, :] = dlf + dla
            off = dlf[0:1, :]
        z = p + gp_ref[0:1, :]
        neg_exp_a = -jnp.exp(gp_ref[1:2, :])
        sb = _sigmoid(p)
        glog = neg_exp_a * _softplus(z)
        dl = dl_ref[...]
        dp = jnp.where(is_b, d0_ref[...] * sb * (1.0 - sb),
                       jnp.where(is_a, dl * neg_exp_a * _sigmoid(z), jnp.where(is_f, dl * _sigmoid(-z), 0.0)))
        dps_ref[...] = dp
        s_a = jnp.sum(jnp.where(is_a, dl * glog, 0.0), axis=0, keepdims=True)
        s_p = jnp.sum(jnp.where(is_b, 0.0, dp), axis=0, keepdims=True)
        row = lax.broadcasted_iota(jnp.int32, (8, LANE), 0)
        from_a = pltpu.roll(jnp.where(row == 0, s_a, jnp.where(row == 1, s_p, 0.0)), LANE - HEADS, 1)
        from_f = pltpu.roll(jnp.where(row == 2, s_p, 0.0), LANE - 2 * HEADS, 1)
        lane = lax.broadcasted_iota(jnp.int32, (8, LANE), 1)
        sums_ref[...] = jnp.where(lane < HEADS, from_a + from_f, 0.0)

    return pl.pallas_call(
        body, name="gates_bwd",
        out_shape=[jax.ShapeDtypeStruct((t, LANE), F32), jax.ShapeDtypeStruct((8, LANE), F32)],
        scratch_shapes=[pltpu.VMEM((t, LANE), F32), pltpu.VMEM((t, LANE), F32), pltpu.VMEM((LANE, LANE), F32)],
        compiler_params=_params(),
    )(ps, gparams, dcol_g, dtot_g, dcol_f, drow_g, drow_f)


def _conv(xv, w):
    acc = w[3:4, :] * xv
    for s in range(1, 4):
        acc += w[3 - s:4 - s, :] * _shift_down(xv, s)
    return acc


PREP_ROWS = 256
HALO = 8


def _tile_rows(r):
    return pl.ds(pl.multiple_of(r * PREP_ROWS, PREP_ROWS), PREP_ROWS)


def _gdn_prep(pm, conv_w):
    t = pm.shape[0]
    nj = WIDTH // LANE
    win = PREP_ROWS + HALO

    def body(x_ref, w_ref, o_ref, xp_ref):
        kind = pl.program_id(0)
        xp_ref[0:HALO, :] = jnp.zeros((HALO, LANE), F32)
        xp_ref[HALO:, :] = x_ref[...]
        w = w_ref[...]
        hs = _head_sum_matrix()

        def tile(r, _):
            xw = xp_ref[pl.ds(pl.multiple_of(r * PREP_ROWS, PREP_ROWS), win), :]
            acc = _conv(xw, w)[HALO:]
            y = acc * _sigmoid(acc)
            out = jnp.where(kind < 2, y * lax.rsqrt(_group_sum(y * y, hs) + EPS), y)
            o_ref[0, 0, _tile_rows(r), :] = out[:, :DH]
            o_ref[0, 1, _tile_rows(r), :] = out[:, DH:]
            return 0

        lax.fori_loop(0, t // PREP_ROWS, tile, 0)

    return pl.pallas_call(
        body, name="gdn_prep", grid=(3, nj),
        in_specs=[pl.BlockSpec((t, LANE), lambda i, j: (0, i * nj + j)),
                  pl.BlockSpec((4, LANE), lambda i, j: (0, i * nj + j))],
        out_specs=pl.BlockSpec((1, 2, t, DH), lambda i, j: (i, j, 0, 0)),
        out_shape=jax.ShapeDtypeStruct((3, HEADS, t, DH), F32),
        scratch_shapes=[pltpu.VMEM((t + HALO, LANE), F32)],
        compiler_params=_params(("arbitrary", "arbitrary")),
    )(pm, conv_w)


def _gdn_prep_bwd(pm, conv_w, dqkv):
    t = pm.shape[0]
    nj = WIDTH // LANE
    win = PREP_ROWS + 2 * HALO

    def body(x_ref, w_ref, d_ref, dx_ref, dw_ref, xp_ref, dp_ref):
        kind = pl.program_id(0)
        zeros = jnp.zeros((HALO, LANE), F32)
        for ref in (xp_ref, dp_ref):
            ref[0:HALO, :] = zeros
            ref[HALO + t:, :] = zeros
        xp_ref[HALO:HALO + t, :] = x_ref[...]
        dp_ref[HALO:HALO + t, 0:DH] = d_ref[0, 0]
        dp_ref[HALO:HALO + t, DH:] = d_ref[0, 1]
        w = w_ref[...]
        hs = _head_sum_matrix()
        rows = lax.broadcasted_iota(jnp.int32, (win, LANE), 0)
        in_tile = (rows >= HALO) & (rows < HALO + PREP_ROWS)

        def tile(r, dw):
            start = pl.multiple_of(r * PREP_ROWS, PREP_ROWS)
            xw = xp_ref[pl.ds(start, win), :]
            d = dp_ref[pl.ds(start, win), :]
            acc = _conv(xw, w)
            sg = _sigmoid(acc)
            y = acc * sg
            rn = lax.rsqrt(_group_sum(y * y, hs) + EPS)
            yn = y * rn
            dy = jnp.where(kind < 2, rn * (d - yn * _group_sum(d * yn, hs)), d)
            dacc = dy * sg * (1.0 + acc * (1.0 - sg))
            dx = w[3:4, :] * dacc
            for s in range(1, 4):
                dx += w[3 - s:4 - s, :] * _shift_up(dacc, s)
            dx_ref[_tile_rows(r), :] = dx[HALO:HALO + PREP_ROWS]
            dm = jnp.where(in_tile, dacc, 0.0)
            return tuple(dw[i] + jnp.sum(dm * (xw if i == 3 else _shift_down(xw, 3 - i)), axis=0, keepdims=True)
                         for i in range(4))

        dw = lax.fori_loop(0, t // PREP_ROWS, tile, tuple(jnp.zeros((1, LANE), F32) for _ in range(4)))
        for i in range(4):
            dw_ref[i:i + 1, :] = dw[i]

    return pl.pallas_call(
        body, name="gdn_prep_bwd", grid=(3, nj),
        in_specs=[pl.BlockSpec((t, LANE), lambda i, j: (0, i * nj + j)),
                  pl.BlockSpec((4, LANE), lambda i, j: (0, i * nj + j)),
                  pl.BlockSpec((1, 2, t, DH), lambda i, j: (i, j, 0, 0))],
        out_specs=[pl.BlockSpec((t, LANE), lambda i, j: (0, i * nj + j)),
                   pl.BlockSpec((4, LANE), lambda i, j: (0, i * nj + j))],
        out_shape=[jax.ShapeDtypeStruct((t, 3 * WIDTH), F32), jax.ShapeDtypeStruct((4, 3 * WIDTH), F32)],
        scratch_shapes=[pltpu.VMEM((t + 2 * HALO, LANE), F32), pltpu.VMEM((t + 2 * HALO, LANE), F32)],
        compiler_params=_params(("arbitrary", "arbitrary")),
    )(pm, conv_w, dqkv)


FOX_COL0 = 4 * WIDTH // LANE


def _fox_prep(pm, nw):
    t = pm.shape[0]
    nj = WIDTH // LANE

    def body(x_ref, w_ref, o_ref):
        kind = pl.program_id(0)
        hs = _head_sum_matrix()
        wk = w_ref[pl.ds(kind, 1), :]

        def tile(r, _):
            xv = x_ref[_tile_rows(r), :]
            ms = _group_sum(xv * xv, hs) * (1.0 / DH)
            out = jnp.where(kind < 2, xv * lax.rsqrt(ms + EPS) * wk, xv)
            o_ref[0, 0, _tile_rows(r), :] = out[:, :DH]
            o_ref[0, 1, _tile_rows(r), :] = out[:, DH:]
            return 0

        lax.fori_loop(0, t // PREP_ROWS, tile, 0)

    return pl.pallas_call(
        body, name="fox_prep", grid=(3, nj),
        in_specs=[pl.BlockSpec((t, LANE), lambda i, j: (0, FOX_COL0 + i * nj + j)),
                  pl.BlockSpec((3, LANE), lambda i, j: (0, 0))],
        out_specs=pl.BlockSpec((1, 2, t, DH), lambda i, j: (i, j, 0, 0)),
        out_shape=jax.ShapeDtypeStruct((3, HEADS, t, DH), F32),
        compiler_params=_params(("arbitrary", "arbitrary")),
    )(pm, nw)


def _fox_prep_bwd(pm, nw, dqkv):
    t = pm.shape[0]
    nj = WIDTH // LANE

    def body(x_ref, w_ref, d_ref, dx_ref, dw_ref):
        kind = pl.program_id(0)
        hs = _head_sum_matrix()
        wk = w_ref[pl.ds(kind, 1), :]

        def tile(r, dw):
            xv = x_ref[_tile_rows(r), :]
            rn = lax.rsqrt(_group_sum(xv * xv, hs) * (1.0 / DH) + EPS)
            xn = xv * rn
            d = jnp.concatenate([d_ref[0, 0, _tile_rows(r), :], d_ref[0, 1, _tile_rows(r), :]], axis=1)
            g = d * wk
            dxn = rn * (g - xn * _group_sum(g * xn, hs) * (1.0 / DH))
            dx_ref[_tile_rows(r), :] = jnp.where(kind < 2, dxn, d)
            return dw + jnp.sum(d * xn, axis=0, keepdims=True)

        dw_ref[0, 0] = lax.fori_loop(0, t // PREP_ROWS, tile, jnp.zeros((1, LANE), F32))

    return pl.pallas_call(
        body, name="fox_prep_bwd", grid=(3, nj),
        in_specs=[pl.BlockSpec((t, LANE), lambda i, j: (0, FOX_COL0 + i * nj + j)),
                  pl.BlockSpec((3, LANE), lambda i, j: (0, 0)),
                  pl.BlockSpec((1, 2, t, DH), lambda i, j: (i, j, 0, 0))],
        out_specs=[pl.BlockSpec((t, LANE), lambda i, j: (0, i * nj + j)),
                   pl.BlockSpec((1, 1, 1, LANE), lambda i, j: (i, j, 0, 0))],
        out_shape=[jax.ShapeDtypeStruct((t, 3 * WIDTH), F32), jax.ShapeDtypeStruct((3, nj, 1, LANE), F32)],
        compiler_params=_params(("arbitrary", "arbitrary")),
    )(pm, nw, dqkv)


SC = 256
CPS = SC // CHUNK
GDN_HP = 2
Q_SCALE = DH ** -0.5


def _sc_masks():
    ri = lax.broadcasted_iota(jnp.int32, (SC, SC), 0)
    ci = lax.broadcasted_iota(jnp.int32, (SC, SC), 1)
    same = (ri // CHUNK) == (ci // CHUNK)
    return same & (ri >= ci), same & (ri > ci), ri == ci


def _unit_lower_inverses(ms, eye):
    invs = [jnp.where(eye, 1.0, 0.0) + m for m in ms]
    for _ in range(5):
        ms = [_dot(m, m) for m in ms]
        invs = [inv + _dot(inv, m) for inv, m in zip(invs, ms)]
    return invs


def _lane_col(blk, lane_idx):
    lane = lax.broadcasted_iota(jnp.int32, blk.shape, 1)
    return jnp.sum(jnp.where(lane == lane_idx, blk, 0.0), axis=1, keepdims=True)


def _to_lane(col, lane_idx):
    lane = lax.broadcasted_iota(jnp.int32, (col.shape[0], LANE), 1)
    return jnp.where(lane == lane_idx, col, 0.0)


def _gdn_columns(gates_ref, run_ref, tot_ref, runt_ref, rows, h):
    return (_lane_col(gates_ref[rows, :], h), _lane_col(run_ref[rows, :], HEADS + h),
            _lane_col(tot_ref[rows, :], HEADS + h), runt_ref[pl.ds(h, 1), rows])


def _gdn_local(q, k, beta, gc, gl, grow, causal):
    decay = jnp.exp(jnp.where(causal, gc - grow, -1e30))
    egc = jnp.exp(gc)
    ekd = jnp.exp(gl - gc)
    qs = q * Q_SCALE
    kb = k * beta
    kk = _dot_nt(kb, k)
    qk = _dot_nt(qs, k)
    return beta, gl, decay, egc, ekd, qs, kb, kk, qk, jnp.where(causal, qk * decay, 0.0)


def _chunk_rows(c):
    return pl.ds(pl.multiple_of(c * CHUNK, CHUNK), CHUNK)


def _sc_rows(b):
    return pl.ds(pl.multiple_of(b * SC, SC), SC)


def _gdn_fwd(qkv, gates, run, tot, run_t):
    t = qkv.shape[2]
    nc = t // CHUNK
    nsc = t // SC

    hp_n = GDN_HP
    heads = range(hp_n)

    def body(qkv_ref, gates_ref, run_ref, tot_ref, runt_ref, o_ref, st_ref, inv_ref, kc_s, qc_s, g_s, au_s):
        hp = pl.program_id(0)
        causal, strict, eye = _sc_masks()

        def local(b, _):
            rows = _sc_rows(b)
            loc = [_gdn_local(qkv_ref[0, hh, rows, :], qkv_ref[1, hh, rows, :],
                              *_gdn_columns(gates_ref, run_ref, tot_ref, runt_ref, rows, hp * hp_n + hh), causal)
                   for hh in heads]
            invs = _unit_lower_inverses([-jnp.where(strict, l[7] * l[2], 0.0) for l in loc], eye)
            us, ws = [], []
            for hh in heads:
                beta, _, _, egc, _, _, kb, _, _, _ = loc[hh]
                inv_ref[hh, rows, :] = invs[hh].astype(inv_ref.dtype)
                us.append(_dot(invs[hh], qkv_ref[2, hh, rows, :] * beta))
                ws.append(_dot(invs[hh], kb * egc))
            for hh in heads:
                _, _, _, egc, ekd, qs, _, _, _, attn = loc[hh]
                g_s[hh, rows, :] = qs * egc - _dot(attn, ws[hh])
                au_s[hh, rows, :] = _dot(attn, us[hh])
                kd = qkv_ref[1, hh, rows, :] * ekd
                for j in range(CPS):
                    sl = slice(j * CHUNK, (j + 1) * CHUNK)
                    kc_s[hh, b * CPS + j] = _dot_tn(kd[sl], ws[hh][sl])
                    qc_s[hh, b * CPS + j] = _dot_tn(kd[sl], us[hh][sl])
            return 0

        lax.fori_loop(0, nsc, local, 0)

        def step(c, states):
            rows = _chunk_rows(c)
            tot_row = tot_ref[pl.ds(c * CHUNK, 1), :]
            new = []
            for hh in heads:
                s = states[hh]
                st_ref[hh, c] = s
                o_ref[hh, rows, :] = _dot(g_s[hh, rows, :], s) + au_s[hh, rows, :]
                egl = jnp.exp(_lane_col(tot_row, HEADS + hp * hp_n + hh))
                new.append(egl * s - _dot(kc_s[hh, c], s) + qc_s[hh, c])
            return tuple(new)

        lax.fori_loop(0, nc, step, tuple(jnp.zeros((DH, DH), F32) for _ in heads))

    whole = pl.BlockSpec((t, LANE), lambda h: (0, 0))
    return pl.pallas_call(
        body, name="gdn_fwd", grid=(HEADS // hp_n,),
        in_specs=[pl.BlockSpec((3, hp_n, t, DH), lambda h: (0, h, 0, 0)), whole, whole, whole,
                  pl.BlockSpec((HEADS, t), lambda h: (1, 0))],
        out_specs=[pl.BlockSpec((hp_n, t, DH), lambda h: (h, 0, 0)), pl.BlockSpec((hp_n, nc, DH, DH), lambda h: (h, 0, 0, 0)),
                   pl.BlockSpec((hp_n, t, SC), lambda h: (h, 0, 0))],
        out_shape=[jax.ShapeDtypeStruct((HEADS, t, DH), F32), jax.ShapeDtypeStruct((HEADS, nc, DH, DH), F32),
                   jax.ShapeDtypeStruct((HEADS, t, SC), MXU)],
        scratch_shapes=[pltpu.VMEM((hp_n, nc, DH, DH), F32), pltpu.VMEM((hp_n, nc, DH, DH), F32),
                        pltpu.VMEM((hp_n, t, DH), F32), pltpu.VMEM((hp_n, t, DH), F32)],
        compiler_params=_params(("arbitrary",)),
    )(qkv, gates, run, tot, run_t)


def _gdn_bwd(qkv, gates, run, tot, run_t, inv, states, do):
    t = qkv.shape[2]
    nc = t // CHUNK
    nsc = t // SC

    hp_n = GDN_HP
    heads = range(hp_n)

    def body(qkv_ref, gates_ref, run_ref, tot_ref, runt_ref, inv_ref, st_ref, do_ref,
             dqkv_ref, dcol_ref, dtot_ref, drow_ref, u_s, w_s, kc_s, h_s, dsn_s):
        hp = pl.program_id(0)
        causal, strict, _ = _sc_masks()

        @pl.when(hp == 0)
        def _():
            dcol_ref[...] = jnp.zeros_like(dcol_ref)
            dtot_ref[...] = jnp.zeros_like(dtot_ref)

        def local_of(hh, rows):
            return _gdn_local(qkv_ref[0, hh, rows, :], qkv_ref[1, hh, rows, :],
                              *_gdn_columns(gates_ref, run_ref, tot_ref, runt_ref, rows, hp * hp_n + hh), causal)

        def local(b, _):
            rows = _sc_rows(b)
            loc = [local_of(hh, rows) for hh in heads]
            us, ws = [], []
            for hh in heads:
                beta, _, _, egc, _, _, kb, _, _, _ = loc[hh]
                inv_b = inv_ref[hh, rows, :]
                us.append(_dot(inv_b, qkv_ref[2, hh, rows, :] * beta))
                ws.append(_dot(inv_b, kb * egc))
            gs = [loc[hh][5] * loc[hh][3] - _dot(loc[hh][9], ws[hh]) for hh in heads]
            for hh in heads:
                u_s[hh, rows, :] = us[hh]
                w_s[hh, rows, :] = ws[hh]
                kd = qkv_ref[1, hh, rows, :] * loc[hh][4]
                dout = do_ref[hh, rows, :]
                for j in range(CPS):
                    sl = slice(j * CHUNK, (j + 1) * CHUNK)
                    kc_s[hh, b * CPS + j] = _dot_tn(kd[sl], ws[hh][sl])
                    h_s[hh, b * CPS + j] = _dot_tn(gs[hh][sl], dout[sl])
            return 0

        lax.fori_loop(0, nsc, local, 0)

        def step(i, dss):
            c = nc - 1 - i
            tot_row = tot_ref[pl.ds(c * CHUNK, 1), :]
            new = []
            for hh in heads:
                ds = dss[hh]
                dsn_s[hh, c] = ds
                egl = jnp.exp(_lane_col(tot_row, HEADS + hp * hp_n + hh))
                new.append(egl * ds - _dot_tn(kc_s[hh, c], ds) + h_s[hh, c])
            return tuple(new)

        lax.fori_loop(0, nc, step, tuple(jnp.zeros((DH, DH), F32) for _ in heads))

        def back(b, _):
            rows = _sc_rows(b)
            first = lax.broadcasted_iota(jnp.int32, (CHUNK, 1), 0) == 0
            loc = [local_of(hh, rows) for hh in heads]
            mid = []
            for hh in heads:
                beta, gl, decay, egc, ekd, qs, kb, kk, qk, attn = loc[hh]
                u, w = u_s[hh, rows, :], w_s[hh, rows, :]
                kd = qkv_ref[1, hh, rows, :] * ekd
                dout = do_ref[hh, rows, :]
                dg_p, dkd_p, dw_p, du_p, dgl_p = [], [], [], [], []
                for j in range(CPS):
                    sl = slice(j * CHUNK, (j + 1) * CHUNK)
                    s = st_ref[hh, b * CPS + j]
                    dsn = dsn_s[hh, b * CPS + j]
                    dkc = -_dot_nt(dsn, s)
                    dg_p.append(_dot_nt(dout[sl], s))
                    dkd_p.append(_dot_nt(w[sl], dkc) + _dot_nt(u[sl], dsn))
                    dw_p.append(_dot(kd[sl], dkc))
                    du_p.append(_dot(kd[sl], dsn))
                    degl = jnp.sum(jnp.sum(s * dsn, axis=1, keepdims=True), axis=0, keepdims=True)
                    dgl_p.append(jnp.where(first, degl * jnp.exp(gl[j * CHUNK:j * CHUNK + 1, :]), 0.0))
                dg, dkd = jnp.concatenate(dg_p, axis=0), jnp.concatenate(dkd_p, axis=0)
                da = jnp.where(causal, _dot_nt(dout, u) - _dot_nt(dg, w), 0.0)
                at = _dot_tn(attn, jnp.concatenate([dout, dg], axis=1))
                du = at[:, :DH] + jnp.concatenate(du_p, axis=0)
                dw = jnp.concatenate(dw_p, axis=0) - at[:, DH:]
                mid.append((kd, dg, dkd, da, du, dw, jnp.concatenate(dgl_p, axis=0)))
            inv_ts = [inv_ref[hh, rows, :].astype(F32).T for hh in heads]
            its = [_dot(inv_ts[hh], jnp.concatenate([mid[hh][4], mid[hh][5]], axis=1)) for hh in heads]
            dinvs = []
            for hh in heads:
                beta, _, _, egc, _, _, kb, _, _, _ = loc[hh]
                dinvs.append(_dot_nt(mid[hh][4], qkv_ref[2, hh, rows, :] * beta) + _dot_nt(mid[hh][5], kb * egc))
            half = [_dot(inv_ts[hh], dinvs[hh]) for hh in heads]
            dls = [jnp.where(strict, -_dot(half[hh], inv_ts[hh]), 0.0) for hh in heads]
            for hh in heads:
                head = hp * hp_n + hh
                beta, gl, decay, egc, ekd, qs, kb, kk, qk, attn = loc[hh]
                kd, dg, dkd, da, du, dw, dgl_first = mid[hh]
                k, v = qkv_ref[1, hh, rows, :], qkv_ref[2, hh, rows, :]
                dvb, dkbe = its[hh][:, :DH], its[hh][:, DH:]
                dl = dls[hh]
                dlogd = (dl * kk + da * qk) * decay
                dkk = dl * decay
                dqk = da * decay
                dkb = _dot(dkk, k) + dkbe * egc
                dqs = _dot(dqk, k) + dg * egc
                dk = (_dot_tn(jnp.concatenate([dkk, dqk], axis=0), jnp.concatenate([kb, qs], axis=0))
                      + dkd * ekd + dkb * beta)
                dkd_kd = jnp.sum(dkd * kd, axis=1, keepdims=True)
                dgc = (jnp.sum(dlogd, axis=1, keepdims=True) + jnp.sum(dg * qs, axis=1, keepdims=True) * egc
                       + jnp.sum(dkbe * (kb * egc), axis=1, keepdims=True) - dkd_kd)
                dbeta = jnp.sum(dkb * k, axis=1, keepdims=True) + jnp.sum(dvb * v, axis=1, keepdims=True)
                dqkv_ref[0, hh, rows, :] = dqs * Q_SCALE
                dqkv_ref[1, hh, rows, :] = dk
                dqkv_ref[2, hh, rows, :] = dvb * beta
                dcol_ref[rows, :] += _to_lane(dbeta, head) + _to_lane(dgc, HEADS + head)
                dtot_ref[rows, :] += _to_lane(dkd_kd + dgl_first, HEADS + head)
                drow_ref[pl.ds(head, 1), rows] = -jnp.sum(dlogd, axis=0, keepdims=True)
            return 0

        lax.fori_loop(0, nsc, back, 0)

    whole = pl.BlockSpec((t, LANE), lambda h: (0, 0))
    rowspec = pl.BlockSpec((HEADS, t), lambda h: (0, 0))
    sq = pltpu.VMEM((hp_n, nc, DH, DH), F32)
    per_head = pltpu.VMEM((hp_n, t, DH), F32)
    return pl.pallas_call(
        body, name="gdn_bwd", grid=(HEADS // hp_n,),
        in_specs=[pl.BlockSpec((3, hp_n, t, DH), lambda h: (0, h, 0, 0)), whole, whole, whole,
                  pl.BlockSpec((HEADS, t), lambda h: (1, 0)),
                  pl.BlockSpec((hp_n, t, SC), lambda h: (h, 0, 0)), pl.BlockSpec((hp_n, nc, DH, DH), lambda h: (h, 0, 0, 0)),
                  pl.BlockSpec((hp_n, t, DH), lambda h: (h, 0, 0))],
        out_specs=[pl.BlockSpec((3, hp_n, t, DH), lambda h: (0, h, 0, 0)), whole, whole, rowspec],
        out_shape=[jax.ShapeDtypeStruct((3, HEADS, t, DH), F32), jax.ShapeDtypeStruct((t, LANE), F32),
                   jax.ShapeDtypeStruct((t, LANE), F32), jax.ShapeDtypeStruct((HEADS, t), F32)],
        scratch_shapes=[per_head, per_head, sq, sq, sq],
        compiler_params=_params(("arbitrary",)),
    )(qkv, gates, run, tot, run_t, inv, states, do)


FOX_HP_FWD = 4
FOX_HP_BWD = 2


def _wide_t(a):
    r = a.shape[0]
    return jnp.concatenate([a, jnp.zeros((r, LANE - DH), F32)], axis=1).T[:DH]


def _tall_t(a):
    r = a.shape[1]
    return jnp.concatenate([a, jnp.zeros((LANE - DH, r), F32)], axis=0).T[:, :DH]


def _key_side_f(run_blk, head, qb):
    col = jnp.broadcast_to(_lane_col(run_blk, 2 * HEADS + head), (run_blk.shape[0], LANE))
    return jnp.concatenate([col] * (qb // LANE), axis=1)


def _diag_mask(qb):
    return lax.broadcasted_iota(jnp.int32, (qb, qb), 0) <= lax.broadcasted_iota(jnp.int32, (qb, qb), 1)


def _fox_fwd(qkv, run, run_t, carry=None):
    t = qkv.shape[2]
    qb = min(QB, t)
    nq = t // qb
    hp_n = FOX_HP_FWD
    c_in, c_in_specs, c_out_shape, c_out_specs, c_sem = _carry_operands(*carry) if carry else ([], [], [], [], None)
    n_c = len(c_in)

    def body(*refs):
        q_ref, k_ref, v_ref, run_ref, runt_ref = refs[:5]
        o_ref, lse_ref = refs[5 + n_c:7 + n_c]
        kb_s, vt_s = refs[7 + n_c + len(c_out_shape):9 + n_c + len(c_out_shape)]
        hp = pl.program_id(0)
        i = pl.program_id(1)

        if carry:
            _carry_step(carry[0], refs[5:5 + n_c], refs[7 + n_c], refs[8 + n_c], refs[-1], hp * nq + i,
                        (HEADS // hp_n) * nq)

        @pl.when(i == 0)
        def _():
            for hh in range(hp_n):
                kb_s[hh] = k_ref[0, hh].astype(MXU)
                for b in range(nq):
                    rows = slice(b * qb, (b + 1) * qb)
                    vt_s[hh, :, rows] = _wide_t(v_ref[0, hh, rows, :]).astype(MXU)

        qrows = pl.ds(pl.multiple_of(i * qb, qb), qb)
        qs = [(q_ref[0, hh] * Q_SCALE).astype(MXU) for hh in range(hp_n)]
        fq = [runt_ref[pl.ds(hp * hp_n + hh, 1), qrows] for hh in range(hp_n)]

        def block_rows(j):
            return pl.ds(pl.multiple_of(j * qb, qb), qb)

        def scores(j):
            return tuple(_dot_nt(kb_s[hh, block_rows(j), :], qs[hh]) for hh in range(hp_n))

        def absorb(j, raw, state, diagonal):
            rows = block_rows(j)
            run_blk = run_ref[rows, :]
            stats, pv = [], []
            for hh in range(hp_n):
                m, l, _ = state[hh]
                st = raw[hh] + (fq[hh] - _key_side_f(run_blk, hp * hp_n + hh, qb))
                if diagonal:
                    st = jnp.where(_diag_mask(qb), st, -1e30)
                m_new = jnp.maximum(m, jnp.max(st, axis=0, keepdims=True))
                p = jnp.exp(st - m_new)
                alpha = jnp.exp(m - m_new)
                stats.append((m_new, alpha * l + jnp.sum(p, axis=0, keepdims=True), alpha))
                pv.append(_dot(vt_s[hh, :, rows], p))
            return tuple((stats[hh][0], stats[hh][1], stats[hh][2] * state[hh][2] + pv[hh]) for hh in range(hp_n))

        def kstep(j, carry):
            state, raw = carry
            ahead = scores(j + 1)
            return absorb(j, raw, state, False), ahead

        init = tuple((jnp.full((1, qb), -1e30, F32), jnp.zeros((1, qb), F32), jnp.zeros((DH, qb), F32))
                     for _ in range(hp_n))
        state, raw = lax.fori_loop(0, i, kstep, (init, scores(0)))
        state = absorb(i, raw, state, True)
        for hh in range(hp_n):
            m, l, acc = state[hh]
            o_ref[hh] = _tall_t(acc / l)
            lse_ref[pl.ds(hp * hp_n + hh, 1), qrows] = m + jnp.log(l)

    out = pl.pallas_call(
        body, name="fox_fwd", grid=(HEADS // hp_n, nq),
        in_specs=[pl.BlockSpec((1, hp_n, qb, DH), lambda h, i: (0, h, i, 0)),
                  pl.BlockSpec((1, hp_n, t, DH), lambda h, i: (1, h, 0, 0)),
                  pl.BlockSpec((1, hp_n, t, DH), lambda h, i: (2, h, 0, 0)),
                  pl.BlockSpec((t, LANE), lambda h, i: (0, 0)),
                  pl.BlockSpec((HEADS, t), lambda h, i: (2, 0))] + c_in_specs,
        out_specs=[pl.BlockSpec((hp_n, qb, DH), lambda h, i: (h, i, 0)),
                   pl.BlockSpec((HEADS, t), lambda h, i: (0, 0))] + c_out_specs,
        out_shape=[jax.ShapeDtypeStruct((HEADS, t, DH), F32), jax.ShapeDtypeStruct((HEADS, t), F32)] + c_out_shape,
        input_output_aliases={5 + j: 4 + j for j in range(n_c)},
        scratch_shapes=[pltpu.VMEM((hp_n, t, DH), MXU), pltpu.VMEM((hp_n, DH, t), MXU)] + ([c_sem] if carry else []),
        compiler_params=_params(("arbitrary", "arbitrary")),
    )(qkv, qkv, qkv, run, run_t, *c_in)
    return (out[0], out[1], _carry_state(out[2:])) if carry else (out[0], out[1])


def _fox_bwd(qkv, run, run_t, o, lse, do, carry=None):
    t = qkv.shape[2]
    qb = min(QB, t)
    nq = t // qb
    hp_n = FOX_HP_BWD
    c_in, c_in_specs, c_out_shape, c_out_specs, c_sem = _carry_operands(*carry) if carry else ([], [], [], [], None)
    n_c = len(c_in)

    def body(*refs):
        q_ref, k_ref, v_ref, run_ref, runt_ref, o_ref, lse_ref, do_ref = refs[:8]
        dqkv_ref, dcol_ref, drow_ref = refs[8 + n_c:11 + n_c]
        dqt_s = refs[11 + n_c + len(c_out_shape)]
        hp = pl.program_id(0)
        j = pl.program_id(1)

        if carry:
            _carry_step(carry[0], refs[8:8 + n_c], refs[11 + n_c], refs[12 + n_c], refs[-1], hp * nq + j,
                        (HEADS // hp_n) * nq)

        @pl.when(j == 0)
        def _():
            dqt_s[...] = jnp.zeros_like(dqt_s)

        @pl.when((j == 0) & (hp == 0))
        def _():
            dcol_ref[...] = jnp.zeros_like(dcol_ref)
            drow_ref[...] = jnp.zeros_like(drow_ref)

        krows = pl.ds(pl.multiple_of(j * qb, qb), qb)
        run_blk = run_ref[krows, :]
        ones8 = jnp.ones((8, DH), MXU)
        kb, kt, vb, fk = [], [], [], []
        for hh in range(hp_n):
            kf = k_ref[0, hh]
            kb.append(kf.astype(MXU))
            kt.append(_wide_t(kf).astype(MXU))
            vb.append(v_ref[0, hh].astype(MXU))
            fk.append(_key_side_f(run_blk, hp * hp_n + hh, qb))

        def block_rows(i):
            return pl.ds(pl.multiple_of(i * qb, qb), qb)

        def products(i):
            rows = block_rows(i)
            out = []
            for hh in range(hp_n):
                dout = do_ref[hh, rows, :]
                x = dout * o_ref[hh, rows, :]
                x_hi = x.astype(MXU)
                out.append((_dot_nt(kb[hh], q_ref[0, hh, rows, :] * Q_SCALE), _dot_nt(vb[hh], dout),
                            (_dot_nt(ones8, x_hi) + _dot_nt(ones8, x - x_hi.astype(F32)))[0:1, :]))
            return tuple(out)

        def absorb(i, prods, acc, diagonal):
            rows = block_rows(i)
            pieces = []
            for hh in range(hp_n):
                head = hp * hp_n + hh
                raw, dpt, drow = prods[hh]
                off = runt_ref[pl.ds(head, 1), rows] - lse_ref[pl.ds(head, 1), rows]
                st = raw + (off - fk[hh])
                if diagonal:
                    st = jnp.where(_diag_mask(qb), st, -1e30)
                pt = jnp.exp(st)
                dst = pt * (dpt - drow)
                drow_ref[pl.ds(head, 1), rows] += jnp.sum(dst, axis=0, keepdims=True)
                folded = dst[:, 0:LANE]
                for c in range(1, qb // LANE):
                    folded = folded + dst[:, c * LANE:(c + 1) * LANE]
                pieces.append((_dot(dst, q_ref[0, hh, rows, :] * Q_SCALE), _dot(pt, do_ref[hh, rows, :]),
                               _dot(kt[hh], dst), folded))
            out = []
            for hh in range(hp_n):
                dqt_s[hh, :, rows] += pieces[hh][2]
                out.append((acc[hh][0] + pieces[hh][0], acc[hh][1] + pieces[hh][1], acc[hh][2] + pieces[hh][3]))
            return tuple(out)

        def qstep(i, carry):
            acc, prods = carry
            ahead = products(jnp.minimum(i + 1, nq - 1))
            return absorb(i, prods, acc, False), ahead

        init = tuple((jnp.zeros((qb, DH), F32), jnp.zeros((qb, DH), F32), jnp.zeros((qb, LANE), F32))
                     for _ in range(hp_n))
        first = products(j)
        ahead = products(jnp.minimum(j + 1, nq - 1))
        acc = absorb(j, first, init, True)
        acc, _ = lax.fori_loop(j + 1, nq, qstep, (acc, ahead))
        for hh in range(hp_n):
            dk, dv, ds_sum = acc[hh]
            dqkv_ref[1, hh, krows, :] = dk
            dqkv_ref[2, hh, krows, :] = dv
            dcol_ref[krows, :] += _to_lane(-jnp.sum(ds_sum, axis=1, keepdims=True), 2 * HEADS + hp * hp_n + hh)

        @pl.when(j == nq - 1)
        def _():
            for hh in range(hp_n):
                for b in range(nq):
                    rows = slice(b * qb, (b + 1) * qb)
                    dqkv_ref[0, hh, rows, :] = _tall_t(dqt_s[hh, :, rows]) * Q_SCALE

    full = pl.BlockSpec((hp_n, t, DH), lambda h, j: (h, 0, 0))
    rows8 = pl.BlockSpec((HEADS, t), lambda h, j: (0, 0))
    out = pl.pallas_call(
        body, name="fox_bwd", grid=(HEADS // hp_n, nq),
        in_specs=[pl.BlockSpec((1, hp_n, t, DH), lambda h, j: (0, h, 0, 0)),
                  pl.BlockSpec((1, hp_n, qb, DH), lambda h, j: (1, h, j, 0)),
                  pl.BlockSpec((1, hp_n, qb, DH), lambda h, j: (2, h, j, 0)),
                  pl.BlockSpec((t, LANE), lambda h, j: (0, 0)), pl.BlockSpec((HEADS, t), lambda h, j: (2, 0)),
                  full, rows8, full] + c_in_specs,
        out_specs=[pl.BlockSpec((3, hp_n, t, DH), lambda h, j: (0, h, 0, 0)),
                   pl.BlockSpec((t, LANE), lambda h, j: (0, 0)), rows8] + c_out_specs,
        out_shape=[jax.ShapeDtypeStruct((3, HEADS, t, DH), F32), jax.ShapeDtypeStruct((t, LANE), F32),
                   jax.ShapeDtypeStruct((HEADS, t), F32)] + c_out_shape,
        input_output_aliases={8 + j: 5 + j for j in range(n_c)},
        scratch_shapes=[pltpu.VMEM((hp_n, DH, t), F32)] + ([c_sem] if carry else []),
        compiler_params=_params(("arbitrary", "arbitrary")),
    )(qkv, qkv, qkv, run, run_t, o, lse, do, *c_in)
    return (out[0], out[1], out[2], _carry_state(out[3:])) if carry else tuple(out)


Z_COL0 = 3 * WIDTH // LANE
FGATE_COL0 = 7 * WIDTH // LANE


def _gdn_post(o, pm, onw):
    t = pm.shape[0]

    def body(o_ref, z_ref, w_ref, m_ref, mt_ref):
        z = z_ref[...]
        sz = z * _sigmoid(z)
        halves = []
        for hh in range(2):
            ov = o_ref[hh]
            n = ov * lax.rsqrt(jnp.mean(ov * ov, axis=-1, keepdims=True) + EPS) * w_ref[...]
            halves.append(n * sz[:, hh * DH:(hh + 1) * DH])
        m = jnp.concatenate(halves, axis=1)
        m_ref[...] = m.astype(m_ref.dtype)
        mt_ref[...] = m.T.astype(mt_ref.dtype)

    return pl.pallas_call(
        body, name="gdn_post", grid=(WIDTH // LANE,),
        in_specs=[pl.BlockSpec((2, t, DH), lambda j: (j, 0, 0)), pl.BlockSpec((t, LANE), lambda j: (0, Z_COL0 + j)),
                  pl.BlockSpec((1, DH), lambda j: (0, 0))],
        out_specs=[pl.BlockSpec((t, LANE), lambda j: (0, j)), pl.BlockSpec((LANE, t), lambda j: (j, 0))],
        out_shape=[jax.ShapeDtypeStruct((t, WIDTH), MXU), jax.ShapeDtypeStruct((WIDTH, t), MXU)],
        compiler_params=_params(("arbitrary",)),
    )(o, pm, onw)


def _gdn_post_bwd(o, pm, onw, dmix):
    t = pm.shape[0]

    def body(o_ref, z_ref, w_ref, dm_ref, do_ref, dz_ref, dw_ref):
        @pl.when(pl.program_id(0) == 0)
        def _():
            dw_ref[...] = jnp.zeros_like(dw_ref)

        z = z_ref[...]
        sg = _sigmoid(z)
        sz = z * sg
        dsz = sg * (1.0 + z * (1.0 - sg))
        dm = dm_ref[...]
        for hh in range(2):
            cols = slice(hh * DH, (hh + 1) * DH)
            ov = o_ref[hh]
            r = lax.rsqrt(jnp.mean(ov * ov, axis=-1, keepdims=True) + EPS)
            xn = ov * r
            dmh = dm[:, cols]
            dn = dmh * sz[:, cols]
            dz_ref[:, cols] = dmh * (xn * w_ref[...]) * dsz[:, cols]
            dw_ref[...] += jnp.sum(dn * xn, axis=0, keepdims=True)
            g = dn * w_ref[...]
            do_ref[hh] = r * (g - xn * jnp.mean(g * xn, axis=-1, keepdims=True))

    return pl.pallas_call(
        body, name="gdn_post_bwd", grid=(WIDTH // LANE,),
        in_specs=[pl.BlockSpec((2, t, DH), lambda j: (j, 0, 0)), pl.BlockSpec((t, LANE), lambda j: (0, Z_COL0 + j)),
                  pl.BlockSpec((1, DH), lambda j: (0, 0)), pl.BlockSpec((t, LANE), lambda j: (0, j))],
        out_specs=[pl.BlockSpec((2, t, DH), lambda j: (j, 0, 0)), pl.BlockSpec((t, LANE), lambda j: (0, j)),
                   pl.BlockSpec((1, DH), lambda j: (0, 0))],
        out_shape=[jax.ShapeDtypeStruct((HEADS, t, DH), F32), jax.ShapeDtypeStruct((t, WIDTH), F32),
                   jax.ShapeDtypeStruct((1, DH), F32)],
        compiler_params=_params(("arbitrary",)),
    )(o, pm, onw, dmix)


def _fox_post(o, pm):
    t = pm.shape[0]

    def body(o_ref, g_ref, m_ref, mt_ref):
        m = jnp.concatenate([o_ref[0], o_ref[1]], axis=1) * _sigmoid(g_ref[...])
        m_ref[...] = m.astype(m_ref.dtype)
        mt_ref[...] = m.T.astype(mt_ref.dtype)

    return pl.pallas_call(
        body, name="fox_post", grid=(WIDTH // LANE,),
        in_specs=[pl.BlockSpec((2, t, DH), lambda j: (j, 0, 0)), pl.BlockSpec((t, LANE), lambda j: (0, FGATE_COL0 + j))],
        out_specs=[pl.BlockSpec((t, LANE), lambda j: (0, j)), pl.BlockSpec((LANE, t), lambda j: (j, 0))],
        out_shape=[jax.ShapeDtypeStruct((t, WIDTH), MXU), jax.ShapeDtypeStruct((WIDTH, t), MXU)],
        compiler_params=_params(("arbitrary",)),
    )(o, pm)


def _fox_post_bwd(o, pm, dmix):
    t = pm.shape[0]

    def body(o_ref, g_ref, dm_ref, do_ref, dg_ref):
        sg = _sigmoid(g_ref[...])
        dm = dm_ref[...]
        for hh in range(2):
            cols = slice(hh * DH, (hh + 1) * DH)
            do_ref[hh] = dm[:, cols] * sg[:, cols]
            dg_ref[:, cols] = dm[:, cols] * o_ref[hh] * (sg * (1.0 - sg))[:, cols]

    return pl.pallas_call(
        body, name="fox_post_bwd", grid=(WIDTH // LANE,),
        in_specs=[pl.BlockSpec((2, t, DH), lambda j: (j, 0, 0)), pl.BlockSpec((t, LANE), lambda j: (0, FGATE_COL0 + j)),
                  pl.BlockSpec((t, LANE), lambda j: (0, j))],
        out_specs=[pl.BlockSpec((2, t, DH), lambda j: (j, 0, 0)), pl.BlockSpec((t, LANE), lambda j: (0, j))],
        out_shape=[jax.ShapeDtypeStruct((HEADS, t, DH), F32), jax.ShapeDtypeStruct((t, WIDTH), F32)],
        compiler_params=_params(("arbitrary",)),
    )(o, pm, dmix)


def _tail(x, mixg, mixf, tgt, wo, n2w, wg, wu, wd, fw):
    t, d = x.shape
    dff = wg.shape[1]
    tb = min(TB, t)

    def body(x_ref, mg_ref, mf_ref, t_ref, wo_ref, n2_ref, wg_ref, wu_ref, wd_ref, fw_ref,
             h2_ref, act_ref, dgate_ref, dup_ref, dx3_ref, dx2_ref, dmg_ref, dmf_ref, dn2_ref, dfw_ref, loss_ref):
        @pl.when(pl.program_id(0) == 0)
        def _():
            dn2_ref[...] = jnp.zeros_like(dn2_ref)
            dfw_ref[...] = jnp.zeros_like(dfw_ref)
            loss_ref[...] = jnp.zeros_like(loss_ref)

        x2 = x_ref[...] + _dot(mg_ref[...], wo_ref[0:WIDTH, :]) + _dot(mf_ref[...], wo_ref[WIDTH:2 * WIDTH, :])
        r2 = lax.rsqrt(jnp.mean(x2 * x2, axis=-1, keepdims=True) + EPS)
        xn2 = x2 * r2
        h2_32 = xn2 * n2_ref[...]
        h2 = h2_32.astype(MXU)
        h2_ref[...] = h2_32.T.astype(MXU)
        gate = _dot(h2, wg_ref[...])
        up = _dot(h2, wu_ref[...])
        sg = _sigmoid(gate)
        sl = gate * sg
        act_32 = sl * up
        act = act_32.astype(MXU)
        act_ref[...] = act_32.T.astype(MXU)
        x3 = x2 + _dot(act, wd_ref[...])
        r3 = lax.rsqrt(jnp.mean(x3 * x3, axis=-1, keepdims=True) + EPS)
        xn3 = x3 * r3
        err = xn3 * fw_ref[...] - t_ref[...]
        loss_ref[...] += 0.5 * jnp.sum(jnp.mean(err * err, axis=-1, keepdims=True), axis=0, keepdims=True)
        dy = err * (1.0 / d)
        dfw_ref[...] += jnp.sum(dy * xn3, axis=0, keepdims=True)
        g3 = dy * fw_ref[...]
        dx3 = r3 * (g3 - xn3 * jnp.mean(g3 * xn3, axis=-1, keepdims=True))
        dx3_ref[...] = dx3.astype(MXU)
        dact = _dot_nt(dx3, wd_ref[...])
        dgate = (dact * up * (sg * (1.0 + gate * (1.0 - sg)))).astype(MXU)
        dup = (dact * sl).astype(MXU)
        dgate_ref[...] = dgate
        dup_ref[...] = dup
        dh2 = _dot_nt(dgate, wg_ref[...]) + _dot_nt(dup, wu_ref[...])
        dn2_ref[...] += jnp.sum(dh2 * xn2, axis=0, keepdims=True)
        g2 = dh2 * n2_ref[...]
        dx2 = dx3 + r2 * (g2 - xn2 * jnp.mean(g2 * xn2, axis=-1, keepdims=True))
        dx2_ref[...] = dx2
        dmg_ref[...] = _dot_nt(dx2, wo_ref[0:WIDTH, :])
        dmf_ref[...] = _dot_nt(dx2, wo_ref[WIDTH:2 * WIDTH, :])

    def tok(n):
        return pl.BlockSpec((tb, n), lambda i: (i, 0))

    def tok_t(n):
        return pl.BlockSpec((n, tb), lambda i: (0, i))

    acc = pl.BlockSpec((1, d), lambda i: (0, 0))
    sds = jax.ShapeDtypeStruct
    return pl.pallas_call(
        body, name="tail", grid=(t // tb,),
        in_specs=[tok(d), tok(WIDTH), tok(WIDTH), tok(d), _resident(wo.shape), _resident((1, d)),
                  _resident(wg.shape), _resident(wu.shape), _resident(wd.shape), _resident((1, d))],
        out_specs=[tok_t(d), tok_t(dff), tok(dff), tok(dff), tok(d), tok(d), tok(WIDTH), tok(WIDTH), acc, acc,
                   pl.BlockSpec((1, 1), lambda i: (0, 0))],
        out_shape=[sds((d, t), MXU), sds((dff, t), MXU), sds((t, dff), MXU), sds((t, dff), MXU), sds((t, d), MXU),
                   sds((t, d), F32), sds((t, WIDTH), F32), sds((t, WIDTH), F32), sds((1, d), F32), sds((1, d), F32),
                   sds((1, 1), F32)],
        compiler_params=_params(("arbitrary",)),
    )(x, mixg, mixf, tgt, wo, n2w, wg, wu, wd, fw)


def _wgrad(at, b, name):
    m, t = at.shape
    n = b.shape[1]
    bm = 256 if m % 256 == 0 else LANE
    cast = b.dtype != jnp.dtype(MXU)

    def body(a_ref, b_ref, o_ref, *scratch):
        if cast:
            @pl.when(pl.program_id(0) == 0)
            def _():
                scratch[0][...] = b_ref[...].astype(MXU)
        o_ref[...] = jnp.dot(a_ref[...], scratch[0][...] if cast else b_ref[...],
                             preferred_element_type=F32).astype(o_ref.dtype)

    return pl.pallas_call(
        body, name=name, grid=(m // bm,),
        in_specs=[pl.BlockSpec((bm, t), lambda i: (i, 0)), _resident((t, n))],
        out_specs=pl.BlockSpec((bm, n), lambda i: (i, 0)),
        out_shape=jax.ShapeDtypeStruct((m, n), WIRE),
        scratch_shapes=[pltpu.VMEM((t, n), MXU)] if cast else [],
        compiler_params=_params(("arbitrary",)),
    )(at, b)


def _split_w_in(w_in):
    a = 4 * WIDTH
    b = a + 2 * HEADS
    c = b + 4 * WIDTH
    main = jnp.concatenate([w_in[:, :a], w_in[:, b:c]], axis=1)
    small = jnp.concatenate([w_in[:, a:b], w_in[:, c:], jnp.zeros((w_in.shape[0], LANE - 3 * HEADS), w_in.dtype)], axis=1)
    return main, small


def _merge_dw_in(d_gdn, d_z, d_fox, d_fg, d_small):
    return jnp.concatenate([d_gdn, d_z, d_small[:, :2 * HEADS], d_fox, d_fg, d_small[:, 2 * HEADS:3 * HEADS]], axis=1)


def _lanes(*pieces):
    v = jnp.concatenate([p.reshape(-1).astype(F32) for p in pieces])
    return jnp.pad(v, (0, LANE - v.shape[0])).reshape(1, LANE)


def _vector_params(p):
    d = p["norm1_w"].size
    gparams = jnp.concatenate([_lanes(jnp.zeros(HEADS), p["gdn_dt_bias"], p["fox_f_bias"]),
                               _lanes(jnp.zeros(HEADS), p["gdn_A_log"]), jnp.zeros((6, LANE), F32)])
    fox_nw = jnp.stack([jnp.tile(p["fox_q_norm_w"].reshape(-1), 2), jnp.tile(p["fox_k_norm_w"].reshape(-1), 2),
                        jnp.ones((LANE,), F32)])
    return dict(n1w=p["norm1_w"].reshape(1, d), n2w=p["norm2_w"].reshape(1, d), fw=p["final_norm_w"].reshape(1, d),
                onw=p["gdn_out_norm_w"].reshape(1, DH), gparams=gparams, fox_nw=fox_nw)


def _mixer_forward(x, vp, w_main, w_small, conv_w, carry=None):
    h1, pm, ps = _inproj(x, vp["n1w"], w_main, w_small)
    gates, run, tot, run_t = _gates(ps, vp["gparams"])
    fqkv = _fox_prep(pm, vp["fox_nw"])
    o_fox, lse, *carried = _fox_fwd(fqkv, run, run_t, carry)
    if carry:
        pm, o_fox = lax.optimization_barrier((pm, o_fox))
    mixf, mixf_t = _fox_post(o_fox, pm)
    gqkv = _gdn_prep(pm, conv_w)
    o_gdn, states, inv = _gdn_fwd(gqkv, gates, run, tot, run_t)
    mixg, mixg_t = _gdn_post(o_gdn, pm, vp["onw"])
    return dict(h1=h1, pm=pm, ps=ps, gates=gates, run=run, tot=tot, run_t=run_t, gqkv=gqkv, o_gdn=o_gdn, states=states,
                inv=inv, mixg=mixg, mixg_t=mixg_t, fqkv=fqkv, o_fox=o_fox, lse=lse, mixf=mixf, mixf_t=mixf_t,
                carried=carried[0] if carried else None)


def _mixer_backward(x, vp, w_main, w_small, conv_w, f, dx2, dmixg, dmixf, carry=None, scatter_own=False):
    pm = f["pm"]
    do_fox, dfg = _fox_post_bwd(f["o_fox"], pm, dmixf)
    dfqkv, dcol_f, drow_f, *carried = _fox_bwd(f["fqkv"], f["run"], f["run_t"], f["o_fox"], f["lse"], do_fox, carry)
    dfox, dfnw = _fox_prep_bwd(pm, vp["fox_nw"], dfqkv)
    do_gdn, dz, donw = _gdn_post_bwd(f["o_gdn"], pm, vp["onw"], dmixg)
    dgqkv, dcol_g, dtot_g, drow_g = _gdn_bwd(f["gqkv"], f["gates"], f["run"], f["tot"], f["run_t"], f["inv"], f["states"], do_gdn)
    dgdn, dconv = _gdn_prep_bwd(pm, conv_w, dgqkv)
    dps, gsum = _gates_bwd(f["ps"], vp["gparams"], dcol_g, dtot_g, dcol_f, drow_g, drow_f)
    h1 = f["h1"]
    dw_in = _merge_dw_in(_wgrad(h1, dgdn, "dw_in_gdn"), _wgrad(h1, dz, "dw_in_z"), _wgrad(h1, dfox, "dw_in_fox"),
                         _wgrad(h1, dfg, "dw_in_fgate"), _wgrad(h1, dps, "dw_in_small"))
    own = ("scatter", [_cut(dw_in, 1), _cut(dconv, 1)]) if scatter_own else None
    grad_x, dn1w, *sent = _inproj_bwd(x, vp["n1w"], dx2, dgdn, dz, dfox, dfg, dps, w_main, w_small, own)
    small = dict(dn1w=dn1w, gsum=gsum, donw=donw, dfnw=dfnw)
    return (grad_x, dw_in, dconv, small, *carried, *sent)


VECTORS = ("norm1_w", "norm2_w", "final_norm_w", "gdn_A_log", "gdn_dt_bias", "gdn_out_norm_w", "fox_f_bias",
           "fox_q_norm_w", "fox_k_norm_w")
VEC_ROWS = 16
LOSS_ROW = len(VECTORS)


def _pack_vectors(dn1w, dn2w, dfw, gsum, donw, dfnw, loss):
    d = dn1w.shape[1]

    def body(n1_ref, n2_ref, fw_ref, gs_ref, on_ref, fn_ref, loss_ref, o_ref):
        o_ref[...] = jnp.zeros_like(o_ref)
        o_ref[0:1, :] = n1_ref[...]
        o_ref[1:2, :] = n2_ref[...]
        o_ref[2:3, :] = fw_ref[...]
        o_ref[3:4, 0:HEADS] = gs_ref[0:1, 0:HEADS]
        o_ref[4:5, 0:HEADS] = gs_ref[1:2, 0:HEADS]
        o_ref[5:6, 0:DH] = on_ref[...]
        o_ref[6:7, 0:HEADS] = gs_ref[2:3, 0:HEADS]
        for kind in range(2):
            v = fn_ref[kind, 0]
            for j in range(1, fn_ref.shape[1]):
                v = v + fn_ref[kind, j]
            o_ref[7 + kind:8 + kind, 0:DH] = v[:, :DH] + v[:, DH:]
        o_ref[LOSS_ROW:LOSS_ROW + 1, 0:1] = loss_ref[...]

    return pl.pallas_call(body, name="pack_vectors", out_shape=jax.ShapeDtypeStruct((VEC_ROWS, d), F32),
                          compiler_params=_params())(dn1w, dn2w, dfw, gsum, donw, dfnw, loss)


def _late_grads(f, h2t, actt, dgate, dup, dx3, dx2):
    return {"w_out": jnp.concatenate([_wgrad(f["mixg_t"], dx2, "dw_out_gdn"), _wgrad(f["mixf_t"], dx2, "dw_out_fox")], axis=0),
            "w_ffn_gate": _wgrad(h2t, dgate, "dw_gate"), "w_ffn_up": _wgrad(h2t, dup, "dw_up"),
            "w_ffn_down": _wgrad(actt, dx3, "dw_down")}


def _local_step(x, tgt, p, w_in, conv_w, wo, wg, wu, wd):
    vp = _vector_params(p)
    w_main, w_small = _split_w_in(w_in)
    f = _mixer_forward(x, vp, w_main, w_small, conv_w)
    (h2t, actt, dgate, dup, dx3, dx2, dmixg, dmixf, dn2w, dfw, loss) = _tail(
        x, f["mixg"], f["mixf"], tgt, wo, vp["n2w"], wg, wu, wd, vp["fw"])
    grad_x, dw_in, dconv, small = _mixer_backward(x, vp, w_main, w_small, conv_w, f, dx2, dmixg, dmixf)
    grads = {"w_in": dw_in, "gdn_conv_w": dconv, **_late_grads(f, h2t, actt, dgate, dup, dx3, dx2)}
    vec = _pack_vectors(small["dn1w"], dn2w, dfw, small["gsum"], small["donw"], small["dfnw"], loss)
    return grad_x, grads, vec


def _my_place():
    return lax.axis_index("x"), lax.axis_index("y"), lax.axis_index("c")


def _peers():
    x, y, c = _my_place()
    peers = []
    for k in range(1, N_DEV):
        px = 1 - x if k & 4 else x
        py = 1 - y if k & 2 else y
        pc = 1 - c if k & 1 else c
        peers.append(((px, py, pc), 4 * px + 2 * py + pc))
    return 4 * x + 2 * y + c, peers


def _spread_copies(kind, srcs, lands, send_sems, recv_sems):
    me, peers = _peers()
    kinds = [kind] * len(srcs) if isinstance(kind, str) else kind
    remote, local = [], []
    for i, (kd, src, land) in enumerate(zip(kinds, srcs, lands)):
        for k, (dev, idx) in enumerate(peers):
            remote.append(pltpu.make_async_remote_copy(
                src_ref=src if kd == "gather" else src.at[idx], dst_ref=land.at[me],
                send_sem=send_sems.at[i * (N_DEV - 1) + k], recv_sem=recv_sems.at[i * (N_DEV - 1) + k],
                device_id=dev, device_id_type=MESH))
        local.append((src if kd == "gather" else src.at[me], land.at[me]))
    return remote, local


def _gather_two_level(arrays, name):
    n = len(arrays)

    def body(*refs):
        srcs, lands = refs[:n], refs[n:2 * n]
        send_sems, recv_sems, local_sems = refs[2 * n:]
        x, y, c = _my_place()
        me, sibling = (x, y, c), (x, y, 1 - c)
        chips = [(1 - x, y), (x, 1 - y), (1 - x, 1 - y)]

        def index(p):
            return 4 * p[0] + 2 * p[1] + p[2]

        def copy(i, k, block, to, src=None):
            blk = lands[i].at[index(block)]
            return pltpu.make_async_remote_copy(
                src_ref=blk if src is None else src, dst_ref=blk, send_sem=send_sems.at[7 * i + k],
                recv_sem=recv_sems.at[7 * i + k], device_id=to, device_id_type=MESH)

        mine = [pltpu.make_async_copy(srcs[i], lands[i].at[index(me)], local_sems.at[i]) for i in range(n)]
        for cp in mine:
            cp.start()
        first = []
        for i in range(n):
            first.append(copy(i, 0, me, sibling, src=srcs[i]))
            first += [copy(i, 1 + j, me, (*chip, c), src=srcs[i]) for j, chip in enumerate(chips)]
        for cp in first:
            cp.start()
        passed = []
        for i in range(n):
            for j, chip in enumerate(chips):
                copy(i, 1 + j, (*chip, c), me).wait_recv()
                passed.append(copy(i, 4 + j, (*chip, c), sibling))
                passed[-1].start()
        for i in range(n):
            copy(i, 0, sibling, me).wait_recv()
            for j, chip in enumerate(chips):
                copy(i, 4 + j, (*chip, 1 - c), me).wait_recv()
        for cp in first + passed:
            cp.wait_send()
        for cp in mine:
            cp.wait()

    return pl.pallas_call(
        body, name=name,
        out_shape=[jax.ShapeDtypeStruct((N_DEV,) + a.shape, a.dtype) for a in arrays],
        in_specs=[pl.BlockSpec(memory_space=pl.ANY)] * n, out_specs=[pl.BlockSpec(memory_space=pl.ANY)] * n,
        scratch_shapes=[pltpu.SemaphoreType.DMA((7 * n,)), pltpu.SemaphoreType.DMA((7 * n,)),
                        pltpu.SemaphoreType.DMA((n,))],
    )(*arrays)


def _land_shape(kind, a):
    return (N_DEV,) + a.shape if kind == "gather" else a.shape


HBM = pl.BlockSpec(memory_space=pltpu.HBM)
SEM = pl.BlockSpec(memory_space=pltpu.SEMAPHORE)


def _hbm(a):
    return pltpu.with_memory_space_constraint(a, pltpu.HBM)


def _carry_operands(kind, arrays):
    n = len(arrays)
    kinds = [kind] * n if isinstance(kind, str) else kind
    lands = [lax.empty(_land_shape(kd, a), a.dtype) for kd, a in zip(kinds, arrays)]
    sems = [pltpu.SemaphoreType.DMA((n * (N_DEV - 1),))] * 2
    return ([_hbm(a) for a in list(arrays) + lands], [HBM] * (2 * n),
            sems + [pltpu.HBM(a.shape, a.dtype) for a in list(arrays) + lands], [SEM] * 2 + [HBM] * (2 * n),
            pltpu.SemaphoreType.DMA((n,)))


def _carry_step(kind, in_refs, send_sems, recv_sems, local_sems, step, n_steps):
    n = len(in_refs) // 2
    remote, local = _spread_copies(kind, in_refs[:n], in_refs[n:], send_sems, recv_sems)
    copies = [pltpu.make_async_copy(s, d, local_sems.at[i]) for i, (s, d) in enumerate(local)]
    per_step = -(-len(remote) // n_steps)
    for s in range(-(-len(remote) // per_step)):
        @pl.when(step == s)
        def _():
            for cp in remote[s * per_step:(s + 1) * per_step]:
                cp.start()
            if s == 0:
                for cp in copies:
                    cp.start()

    @pl.when(step == n_steps - 1)
    def _():
        for cp in copies:
            cp.wait()


def _carry_state(extra_out):
    n = (len(extra_out) - 2) // 2
    return list(extra_out[2:2 + n]), list(extra_out[2 + n:]), extra_out[0], extra_out[1]


def _spread_start(arrays, kind, name):
    n = len(arrays)

    def body(*refs):
        srcs, lands = refs[:n], refs[n:2 * n]
        send_sems, recv_sems = refs[2 * n], refs[2 * n + 1]
        token = refs[4 * n + 2]
        local_sems = refs[4 * n + 3]
        remote, local = _spread_copies(kind, srcs, lands, send_sems, recv_sems)
        for cp in remote:
            cp.start()
        copies = [pltpu.make_async_copy(s, d, local_sems.at[i]) for i, (s, d) in enumerate(local)]
        for cp in copies:
            cp.start()
        for cp in copies:
            cp.wait()
        token[...] = jnp.zeros_like(token)

    sems = (pltpu.SemaphoreType.DMA((n * (N_DEV - 1),)),) * 2
    kinds = [kind] * n if isinstance(kind, str) else kind
    lands = [lax.empty(_land_shape(kd, a), a.dtype) for kd, a in zip(kinds, arrays)]
    out = pl.pallas_call(
        body, name=name,
        out_shape=sems + tuple(pltpu.HBM(a.shape, a.dtype) for a in list(arrays) + lands)
        + (jax.ShapeDtypeStruct((8, LANE), F32),),
        in_specs=[HBM] * (2 * n), out_specs=tuple([SEM] * 2 + [HBM] * (2 * n) + [pl.BlockSpec(memory_space=pltpu.VMEM)]),
        input_output_aliases={j: 2 + j for j in range(2 * n)},
        scratch_shapes=[pltpu.SemaphoreType.DMA((n,))],
        compiler_params=pltpu.CompilerParams(has_side_effects=pltpu.SideEffectType.DATAFLOW_SIDE_EFFECTING),
    )(*[_hbm(a) for a in arrays], *[_hbm(a) for a in lands])
    return (list(out[2:2 + n]), list(out[2 + n:2 + 2 * n]), out[0], out[1]), out[-1]


def _spread_wait(state, kind, after, name):
    srcs, lands, send_sems, recv_sems = state
    n = len(srcs)
    after = list(after) if isinstance(after, (list, tuple)) else [after]

    def body(*refs):
        remote, _ = _spread_copies(kind, refs[:n], refs[n:2 * n], refs[2 * n], refs[2 * n + 1])
        for cp in remote:
            cp.wait_send()
        for cp in remote:
            cp.wait_recv()

    out = pl.pallas_call(
        body, name=name,
        out_shape=tuple(pltpu.HBM(a.shape, a.dtype) for a in srcs + lands),
        in_specs=[HBM] * (2 * n) + [SEM, SEM] + [pl.BlockSpec(memory_space=pl.ANY)] * len(after),
        out_specs=tuple([HBM] * (2 * n)),
        input_output_aliases={j: j for j in range(2 * n)},
        compiler_params=pltpu.CompilerParams(has_side_effects=pltpu.SideEffectType.DATAFLOW_SIDE_EFFECTING),
    )(*srcs, *lands, send_sems, recv_sems, *after)
    return list(out[n:])


ADAM_ROWS = 128


def _adam_math(g, w, m, v):
    nm = ADAM_B1 * m + (1.0 - ADAM_B1) * g
    nv = ADAM_B2 * v + (1.0 - ADAM_B2) * (g * g)
    m_hat = nm / (1.0 - ADAM_B1 ** ADAM_STEP)
    v_hat = nv / (1.0 - ADAM_B2 ** ADAM_STEP)
    return -ADAM_LR * (m_hat / (jnp.sqrt(v_hat) + ADAM_EPS) + ADAM_WD * w), nm, nv


def _sum_parts(p_ref):
    g = p_ref[0].astype(F32)
    for s in range(1, N_DEV):
        g = g + p_ref[s].astype(F32)
    return g


def _adam_matrix(parts, w, m, v, name):
    _, r, c = w.shape
    rb = ADAM_ROWS if r % ADAM_ROWS == 0 else r

    def body(p_ref, w_ref, m_ref, v_ref, g_ref, d_ref, nm_ref, nv_ref):
        g = _sum_parts(p_ref)
        g_ref[0] = g
        d_ref[0], nm_ref[0], nv_ref[0] = _adam_math(g, w_ref[0], m_ref[0], v_ref[0])

    blk = pl.BlockSpec((1, rb, c), lambda i: (0, i, 0))
    return pl.pallas_call(
        body, name=name, grid=(r // rb,),
        in_specs=[pl.BlockSpec((N_DEV, rb, c), lambda i: (0, i, 0)), blk, blk, blk],
        out_specs=[blk] * 4, out_shape=[jax.ShapeDtypeStruct(w.shape, F32)] * 4,
        compiler_params=_params(("arbitrary",)),
    )(parts, w, m, v)


def _adam_vectors(parts, ws, ms, vs):
    nv = len(ws)

    def body(*refs):
        p_ref = refs[0]
        w_refs, m_refs, v_refs = refs[1:1 + nv], refs[1 + nv:1 + 2 * nv], refs[1 + 2 * nv:1 + 3 * nv]
        outs = refs[1 + 3 * nv:]
        g_all = _sum_parts(p_ref)
        for i in range(nv):
            n = w_refs[i].shape[1]
            g = g_all[i:i + 1, 0:n]
            d, nm, nvv = _adam_math(g, w_refs[i][...], m_refs[i][...], v_refs[i][...])
            outs[i][...] = g
            outs[nv + i][...] = d
            outs[2 * nv + i][...] = nm
            outs[3 * nv + i][...] = nvv
        outs[4 * nv][...] = g_all[LOSS_ROW:LOSS_ROW + 1, 0:1]

    shapes = [jax.ShapeDtypeStruct(a.shape, F32) for a in ws]
    out = pl.pallas_call(body, name="adam_vectors", out_shape=shapes * 4 + [jax.ShapeDtypeStruct((1, 1), F32)],
                         compiler_params=_params())(parts, *ws, *ms, *vs)
    return out[:nv], out[nv:2 * nv], out[2 * nv:3 * nv], out[3 * nv:4 * nv], out[4 * nv]


MATRICES = (("w_in", 1), ("gdn_conv_w", 1), ("w_out", 0), ("w_ffn_gate", 1), ("w_ffn_up", 1), ("w_ffn_down", 0))
WEIGHTS = ("norm1_w", "w_in", "gdn_conv_w", "gdn_A_log", "gdn_dt_bias", "gdn_out_norm_w", "fox_f_bias", "fox_q_norm_w",
           "fox_k_norm_w", "w_out", "norm2_w", "w_ffn_gate", "w_ffn_up", "w_ffn_down", "final_norm_w")


def _join(blocks, axis):
    _, r, c = blocks.shape
    if axis == 0:
        return blocks.reshape(N_DEV * r, c)
    return blocks.transpose(1, 0, 2).reshape(r, N_DEV * c)


def _cut(full, axis):
    r, c = full.shape
    if axis == 0:
        return full.reshape(N_DEV, r // N_DEV, c)
    return full.reshape(r, N_DEV, c // N_DEV).transpose(1, 0, 2)


def kernel(x, norm1_w, w_in, gdn_conv_w, gdn_A_log, gdn_dt_bias, gdn_out_norm_w, fox_f_bias, fox_q_norm_w, fox_k_norm_w, w_out, norm2_w, w_ffn_gate, w_ffn_up, w_ffn_down, final_norm_w, loss_target, m_norm1_w, m_w_in, m_gdn_conv_w, m_gdn_A_log, m_gdn_dt_bias, m_gdn_out_norm_w, m_fox_f_bias, m_fox_q_norm_w, m_fox_k_norm_w, m_w_out, m_norm2_w, m_w_ffn_gate, m_w_ffn_up, m_w_ffn_down, m_final_norm_w, v_norm1_w, v_w_in, v_gdn_conv_w, v_gdn_A_log, v_gdn_dt_bias, v_gdn_out_norm_w, v_fox_f_bias, v_fox_q_norm_w, v_fox_k_norm_w, v_w_out, v_norm2_w, v_w_ffn_gate, v_w_ffn_up, v_w_ffn_down, v_final_norm_w):
    w = dict(norm1_w=norm1_w, w_in=w_in, gdn_conv_w=gdn_conv_w, gdn_A_log=gdn_A_log, gdn_dt_bias=gdn_dt_bias,
             gdn_out_norm_w=gdn_out_norm_w, fox_f_bias=fox_f_bias, fox_q_norm_w=fox_q_norm_w, fox_k_norm_w=fox_k_norm_w,
             w_out=w_out, norm2_w=norm2_w, w_ffn_gate=w_ffn_gate, w_ffn_up=w_ffn_up, w_ffn_down=w_ffn_down,
             final_norm_w=final_norm_w)
    m = dict(norm1_w=m_norm1_w, w_in=m_w_in, gdn_conv_w=m_gdn_conv_w, gdn_A_log=m_gdn_A_log, gdn_dt_bias=m_gdn_dt_bias,
             gdn_out_norm_w=m_gdn_out_norm_w, fox_f_bias=m_fox_f_bias, fox_q_norm_w=m_fox_q_norm_w,
             fox_k_norm_w=m_fox_k_norm_w, w_out=m_w_out, norm2_w=m_norm2_w, w_ffn_gate=m_w_ffn_gate,
             w_ffn_up=m_w_ffn_up, w_ffn_down=m_w_ffn_down, final_norm_w=m_final_norm_w)
    v = dict(norm1_w=v_norm1_w, w_in=v_w_in, gdn_conv_w=v_gdn_conv_w, gdn_A_log=v_gdn_A_log, gdn_dt_bias=v_gdn_dt_bias,
             gdn_out_norm_w=v_gdn_out_norm_w, fox_f_bias=v_fox_f_bias, fox_q_norm_w=v_fox_q_norm_w,
             fox_k_norm_w=v_fox_k_norm_w, w_out=v_w_out, norm2_w=v_norm2_w, w_ffn_gate=v_w_ffn_gate,
             w_ffn_up=v_w_ffn_up, w_ffn_down=v_w_ffn_down, final_norm_w=v_final_norm_w)
    axis_of = dict(MATRICES)
    late = ("w_out", "w_ffn_gate", "w_ffn_up", "w_ffn_down")
    xs, tgt = x[0], loss_target[0]
    vp = _vector_params({n: w[n] for n in VECTORS})

    w_in_blocks, conv_blocks = _gather_two_level([w["w_in"][0].astype(WIRE), w["gdn_conv_w"][0]], "gather_in")
    w_main, w_small = _split_w_in(_join(w_in_blocks, 1))
    conv_w = _join(conv_blocks, 1)
    f = _mixer_forward(xs, vp, w_main, w_small, conv_w, ("gather", [w[n][0].astype(WIRE) for n in late]))
    full = {n: _join(b, axis_of[n])
            for n, b in zip(late, _spread_wait(f["carried"], "gather", f["mixg"], "gather_late_wait"))}

    (h2t, actt, dgate, dup, dx3, dx2, dmixg, dmixf, dn2w, dfw, loss) = _tail(
        xs, f["mixg"], f["mixf"], tgt, full["w_out"], vp["n2w"], full["w_ffn_gate"], full["w_ffn_up"],
        full["w_ffn_down"], vp["fw"])
    dlate = _late_grads(f, h2t, actt, dgate, dup, dx3, dx2)

    grad_x, dw_in, dconv, small, state_grads, state_own = _mixer_backward(
        xs, vp, w_main, w_small, conv_w, f, dx2, dmixg, dmixf, ("scatter", [_cut(dlate[n], axis_of[n]) for n in late]),
        scatter_own=True)
    vec = _pack_vectors(small["dn1w"], dn2w, dfw, small["gsum"], small["donw"], small["dfnw"], loss)

    state_vec, token = _spread_start([vec], "gather", "vectors_start")
    parts = dict(zip(late, _spread_wait(state_grads, "scatter", token, "grads_late_wait")))
    results = [{}, {}, {}, {}]

    def update(n):
        for d, a in zip(results, _adam_matrix(parts[n], w[n], m[n], v[n], "adam_" + n)):
            d[n] = a

    for n in late:
        update(n)
    parts["w_in"], parts["gdn_conv_w"] = _spread_wait(state_own, "scatter", [results[0][n] for n in late], "own_wait")
    (parts_vec,) = _spread_wait(state_vec, "gather", parts["w_in"], "vectors_wait")
    update("w_in")
    update("gdn_conv_w")
    row = lambda a: a.reshape(1, -1)
    *vec_out, total_loss = _adam_vectors(parts_vec, [row(w[n]) for n in VECTORS], [row(m[n]) for n in VECTORS],
                                         [row(v[n]) for n in VECTORS])
    for d, arrs in zip(results, vec_out):
        for n, a in zip(VECTORS, arrs):
            d[n] = a.reshape(w[n].shape)
    return (total_loss[0, 0], grad_x[None], *[d[n] for d in results for n in WEIGHTS])
```

```python
import functools

import jax
import jax.numpy as jnp
from jax import lax
from jax.experimental import pallas as pl
from jax.experimental.pallas import tpu as pltpu

F32 = jnp.float32
MXU = jnp.bfloat16
WIRE = jnp.bfloat16
HI = lax.Precision.HIGHEST
EPS = 1e-6

N_DEV = 8
HEADS = 8
DH = 64
WIDTH = HEADS * DH
CHUNK = 64
LANE = 128
ROW_ALIGN = 16
TB = 256
QB = 256
VMEM_LIMIT = 60 * 1024 * 1024

ADAM_LR = 0.001
ADAM_B1 = 0.9
ADAM_B2 = 0.999
ADAM_EPS = 1e-08
ADAM_WD = 0.01
ADAM_STEP = 10

MESH = pl.DeviceIdType.MESH


def _params(sem=None):
    return pltpu.CompilerParams(dimension_semantics=sem, vmem_limit_bytes=VMEM_LIMIT)


def _resident(shape):
    n = len(shape)
    return pl.BlockSpec(shape, lambda *_: (0,) * n, pipeline_mode=pl.Buffered(1))


def _dot(a, b):
    return jnp.dot(a.astype(MXU), b.astype(MXU), preferred_element_type=F32)


def _dot_nt(a, b):
    return lax.dot_general(a.astype(MXU), b.astype(MXU), (((1,), (1,)), ((), ())), preferred_element_type=F32)


def _dot_tn(a, b):
    return lax.dot_general(a.astype(MXU), b.astype(MXU), (((0,), (0,)), ((), ())), preferred_element_type=F32)


def _hdot(a, b):
    return jnp.dot(a, b, precision=HI, preferred_element_type=F32)


def _hdot_nt(a, b):
    return lax.dot_general(a, b, (((1,), (1,)), ((), ())), precision=HI, preferred_element_type=F32)


def _hdot_tn(a, b):
    return lax.dot_general(a, b, (((0,), (0,)), ((), ())), precision=HI, preferred_element_type=F32)


def _sigmoid(x):
    return 0.5 * jnp.tanh(0.5 * x) + 0.5


def _softplus(x):
    return jnp.maximum(x, 0.0) + jnp.log(1.0 + jnp.exp(-jnp.abs(x)))


def _head_sum_matrix():
    ri = lax.broadcasted_iota(jnp.int32, (LANE, LANE), 0) // DH
    ci = lax.broadcasted_iota(jnp.int32, (LANE, LANE), 1) // DH
    return (ri == ci).astype(F32)


def _group_sum(a, ones_matrix):
    hi = a.astype(jnp.bfloat16)
    lo = (a - hi.astype(F32)).astype(jnp.bfloat16)
    m = ones_matrix.astype(jnp.bfloat16)
    return jnp.dot(hi, m, preferred_element_type=F32) + jnp.dot(lo, m, preferred_element_type=F32)


def _shift_down(x, s):
    return pltpu.roll(x, s, 0)


def _shift_up(x, s):
    return pltpu.roll(x, x.shape[0] - s, 0)


ROWS_A = 4 * WIDTH
ROWS_B = ROWS_A + 2 * HEADS
ROWS_C = ROWS_B + 4 * WIDTH


def _small_rows(w_t):
    return jnp.concatenate([w_t[ROWS_A:ROWS_B], w_t[ROWS_C:], jnp.zeros((LANE - 3 * HEADS, w_t.shape[1]), w_t.dtype)])


def _inproj(x, n1w, w_t, ws_t):
    t, d = x.shape
    tb = min(TB, t)

    def body(x_ref, nw_ref, wt_ref, ws_ref, h_ref, pm_ref, ps_ref):
        xv = x_ref[...]
        r = lax.rsqrt(jnp.mean(xv * xv, axis=-1, keepdims=True) + EPS)
        h = (xv * r * nw_ref[...]).astype(MXU)
        h_ref[...] = h
        pm_ref[:, 0:ROWS_A] = _dot_nt(h, wt_ref[0:ROWS_A, :])
        pm_ref[:, ROWS_A:2 * ROWS_A] = _dot_nt(h, wt_ref[ROWS_B:ROWS_C, :])
        ps_ref[...] = _dot_nt(h, ws_ref[...])

    return pl.pallas_call(
        body, name="inproj", grid=(t // tb,),
        in_specs=[pl.BlockSpec((tb, d), lambda i: (i, 0)), _resident((1, d)), _resident(w_t.shape), _resident((LANE, d))],
        out_specs=[pl.BlockSpec((tb, d), lambda i: (i, 0)), pl.BlockSpec((tb, 2 * ROWS_A), lambda i: (i, 0)),
                   pl.BlockSpec((tb, LANE), lambda i: (i, 0))],
        out_shape=[jax.ShapeDtypeStruct((t, d), MXU), jax.ShapeDtypeStruct((t, 2 * ROWS_A), F32),
                   jax.ShapeDtypeStruct((t, LANE), F32)],
        compiler_params=_params(("arbitrary",)),
    )(x, n1w, w_t, ws_t)


def _inproj_bwd(x, n1w, dx2, dgdn, dz, dfox, dfg, dps, w_t, ws_t, carry=None):
    t, d = x.shape
    tb = min(TB, t)
    w3 = 3 * WIDTH
    c_in, c_in_specs, c_out_shape, c_out_specs, c_sem = _carry_operands(*carry) if carry else ([], [], [], [], None)
    n_c = len(c_in)

    def body(*refs):
        x_ref, nw_ref, dx2_ref, dgdn_ref, dz_ref, dfox_ref, dfg_ref, dps_ref, wm_ref, ws_ref = refs[:10]
        gx_ref, dnw_ref = refs[10 + n_c:12 + n_c]
        if carry:
            _carry_step(carry[0], refs[10:10 + n_c], refs[12 + n_c], refs[13 + n_c], refs[-1], pl.program_id(0), t // tb)
        dh = _dot(dgdn_ref[...], wm_ref[0:w3, :])
        dh += _dot(dz_ref[...], wm_ref[w3:ROWS_A, :])
        dh += _dot(dfox_ref[...], wm_ref[ROWS_B:ROWS_B + w3, :])
        dh += _dot(dfg_ref[...], wm_ref[ROWS_B + w3:ROWS_C, :])
        dh += _dot(dps_ref[...], ws_ref[...])
        xv = x_ref[...]
        r = lax.rsqrt(jnp.mean(xv * xv, axis=-1, keepdims=True) + EPS)
        xn = xv * r

        @pl.when(pl.program_id(0) == 0)
        def _():
            dnw_ref[...] = jnp.zeros_like(dnw_ref)

        dnw_ref[...] += jnp.sum(dh * xn, axis=0, keepdims=True)
        g = dh * nw_ref[...]
        gx_ref[...] = dx2_ref[...] + r * (g - xn * jnp.mean(g * xn, axis=-1, keepdims=True))

    def tok(n):
        return pl.BlockSpec((tb, n), lambda i: (i, 0))

    out = pl.pallas_call(
        body, name="inproj_bwd", grid=(t // tb,),
        in_specs=[tok(d), _resident((1, d)), tok(d), tok(w3), tok(WIDTH), tok(w3), tok(WIDTH), tok(LANE),
                  _resident(w_t.shape), _resident(ws_t.shape)] + c_in_specs,
        out_specs=[tok(d), pl.BlockSpec((1, d), lambda i: (0, 0))] + c_out_specs,
        out_shape=[jax.ShapeDtypeStruct((t, d), F32), jax.ShapeDtypeStruct((1, d), F32)] + c_out_shape,
        input_output_aliases={10 + j: 4 + j for j in range(n_c)},
        scratch_shapes=[c_sem] if carry else [],
        compiler_params=_params(("arbitrary",)),
    )(x, n1w, dx2, dgdn, dz, dfox, dfg, dps, w_t, ws_t, *c_in)
    return (out[0], out[1], _carry_state(out[2:])) if carry else tuple(out)


def _gate_lanes(shape):
    lane = lax.broadcasted_iota(jnp.int32, shape, 1)
    return lane < HEADS, (lane >= HEADS) & (lane < 2 * HEADS), (lane >= 2 * HEADS) & (lane < 3 * HEADS)


def _block_masks():
    ri = lax.broadcasted_iota(jnp.int32, (LANE, LANE), 0)
    ci = lax.broadcasted_iota(jnp.int32, (LANE, LANE), 1)
    same = (ri // CHUNK) == (ci // CHUNK)
    return ((ri >= ci).astype(F32), (ri <= ci).astype(F32), (same & (ri >= ci)).astype(F32),
            (same & (ri <= ci)).astype(F32), same.astype(F32))


def _gates(ps, gparams):
    t = ps.shape[0]
    nb = t // LANE

    def body(ps_ref, gp_ref, out_ref, run_ref, tot_ref, runt_ref):
        p = ps_ref[...]
        is_b, is_a, is_f = _gate_lanes(p.shape)
        z = p + gp_ref[0:1, :]
        neg_exp_a = -jnp.exp(gp_ref[1:2, :])
        glog = neg_exp_a * _softplus(z)
        logf = -_softplus(-z)
        out_ref[...] = jnp.where(is_b, _sigmoid(p), jnp.where(is_a, glog, jnp.where(is_f, logf, 0.0)))
        tril, _, tril_c, _, same_c = _block_masks()
        off = jnp.zeros((1, LANE), F32)
        for b in range(nb):
            rows = slice(b * LANE, (b + 1) * LANE)
            blk = out_ref[rows, :]
            ga = jnp.where(is_a[:LANE], blk, 0.0)
            fb = _hdot(tril, jnp.where(is_f[:LANE], blk, 0.0)) + off
            run = fb + _hdot(tril_c, ga)
            run_ref[rows, :] = run
            runt_ref[:, rows] = run.T
            tot_ref[rows, :] = _hdot(same_c, ga)
            off = fb[LANE - 1:LANE, :]

    return pl.pallas_call(
        body, name="gates",
        out_shape=[jax.ShapeDtypeStruct((t, LANE), F32)] * 3 + [jax.ShapeDtypeStruct((LANE, t), F32)],
        compiler_params=_params(),
    )(ps, gparams)


def _gates_bwd(ps, gparams, dcol_g, dtot_g, dcol_f, drow_g, drow_f):
    t = ps.shape[0]
    nb = t // LANE

    def body(ps_ref, gp_ref, dcg_ref, dtg_ref, dcf_ref, drg_ref, drf_ref, dps_ref, sums_ref, dl_ref, d0_ref, tr_ref):
        p = ps_ref[...]
        is_b, is_a, is_f = _gate_lanes(p.shape)
        _, triu, _, triu_c, same_c = _block_masks()
        tr_ref[...] = jnp.zeros_like(tr_ref)
        off = jnp.zeros((1, LANE), F32)
        for b in reversed(range(nb)):
            rows = slice(b * LANE, (b + 1) * LANE)
            tr_ref[HEADS:2 * HEADS, :] = drg_ref[:, rows]
            tr_ref[2 * HEADS:3 * HEADS, :] = drf_ref[:, rows]
            d = dcg_ref[rows, :] + dcf_ref[rows, :] + tr_ref[...].T
            d0_ref[rows, :] = d
            dlf = _hdot(triu, jnp.where(is_f[:LANE], d, 0.0)) + off
            dla = (_hdot(triu_c, jnp.where(is_a[:LANE], d, 0.0))
                   + _hdot(same_c, jnp.where(is_a[:LANE], dtg_ref[rows, :], 0.0)))
            dl_ref[rows, :] = dlf + dla
            off = dlf[0:1, :]
        z = p + gp_ref[0:1, :]
        neg_exp_a = -jnp.exp(gp_ref[1:2, :])
        sb = _sigmoid(p)
        glog = neg_exp_a * _softplus(z)
        dl = dl_ref[...]
        dp = jnp.where(is_b, d0_ref[...] * sb * (1.0 - sb),
                       jnp.where(is_a, dl * neg_exp_a * _sigmoid(z), jnp.where(is_f, dl * _sigmoid(-z), 0.0)))
        dps_ref[...] = dp
        s_a = jnp.sum(jnp.where(is_a, dl * glog, 0.0), axis=0, keepdims=True)
        s_p = jnp.sum(jnp.where(is_b, 0.0, dp), axis=0, keepdims=True)
        row = lax.broadcasted_iota(jnp.int32, (8, LANE), 0)
        from_a = pltpu.roll(jnp.where(row == 0, s_a, jnp.where(row == 1, s_p, 0.0)), LANE - HEADS, 1)
        from_f = pltpu.roll(jnp.where(row == 2, s_p, 0.0), LANE - 2 * HEADS, 1)
        lane = lax.broadcasted_iota(jnp.int32, (8, LANE), 1)
        sums_ref[...] = jnp.where(lane < HEADS, from_a + from_f, 0.0)

    return pl.pallas_call(
        body, name="gates_bwd",
        out_shape=[jax.ShapeDtypeStruct((t, LANE), F32), jax.ShapeDtypeStruct((8, LANE), F32)],
        scratch_shapes=[pltpu.VMEM((t, LANE), F32), pltpu.VMEM((t, LANE), F32), pltpu.VMEM((LANE, LANE), F32)],
        compiler_params=_params(),
    )(ps, gparams, dcol_g, dtot_g, dcol_f, drow_g, drow_f)


def _conv(xv, w):
    acc = w[3:4, :] * xv
    for s in range(1, 4):
        acc += w[3 - s:4 - s, :] * _shift_down(xv, s)
    return acc


PREP_ROWS = 256
HALO = 8


def _tile_rows(r):
    return pl.ds(pl.multiple_of(r * PREP_ROWS, PREP_ROWS), PREP_ROWS)


def _gdn_prep(pm, conv_w):
    t = pm.shape[0]
    nj = WIDTH // LANE
    win = PREP_ROWS + HALO

    def body(x_ref, w_ref, o_ref, xp_ref):
        kind = pl.program_id(0)
        xp_ref[0:HALO, :] = jnp.zeros((HALO, LANE), F32)
        xp_ref[HALO:, :] = x_ref[...]
        w = w_ref[...]
        hs = _head_sum_matrix()

        def tile(r, _):
            xw = xp_ref[pl.ds(pl.multiple_of(r * PREP_ROWS, PREP_ROWS), win), :]
            acc = _conv(xw, w)[HALO:]
            y = acc * _sigmoid(acc)
            out = jnp.where(kind < 2, y * lax.rsqrt(_group_sum(y * y, hs) + EPS), y)
            o_ref[0, 0, _tile_rows(r), :] = out[:, :DH]
            o_ref[0, 1, _tile_rows(r), :] = out[:, DH:]
            return 0

        lax.fori_loop(0, t // PREP_ROWS, tile, 0)

    return pl.pallas_call(
        body, name="gdn_prep", grid=(3, nj),
        in_specs=[pl.BlockSpec((t, LANE), lambda i, j: (0, i * nj + j)),
                  pl.BlockSpec((4, LANE), lambda i, j: (0, i * nj + j))],
        out_specs=pl.BlockSpec((1, 2, t, DH), lambda i, j: (i, j, 0, 0)),
        out_shape=jax.ShapeDtypeStruct((3, HEADS, t, DH), F32),
        scratch_shapes=[pltpu.VMEM((t + HALO, LANE), F32)],
        compiler_params=_params(("arbitrary", "arbitrary")),
    )(pm, conv_w)


def _gdn_prep_bwd(pm, conv_w, dqkv):
    t = pm.shape[0]
    nj = WIDTH // LANE
    win = PREP_ROWS + 2 * HALO

    def body(x_ref, w_ref, d_ref, dx_ref, dw_ref, xp_ref, dp_ref):
        kind = pl.program_id(0)
        zeros = jnp.zeros((HALO, LANE), F32)
        for ref in (xp_ref, dp_ref):
            ref[0:HALO, :] = zeros
            ref[HALO + t:, :] = zeros
        xp_ref[HALO:HALO + t, :] = x_ref[...]
        dp_ref[HALO:HALO + t, 0:DH] = d_ref[0, 0]
        dp_ref[HALO:HALO + t, DH:] = d_ref[0, 1]
        w = w_ref[...]
        hs = _head_sum_matrix()
        rows = lax.broadcasted_iota(jnp.int32, (win, LANE), 0)
        in_tile = (rows >= HALO) & (rows < HALO + PREP_ROWS)

        def tile(r, dw):
            start = pl.multiple_of(r * PREP_ROWS, PREP_ROWS)
            xw = xp_ref[pl.ds(start, win), :]
            d = dp_ref[pl.ds(start, win), :]
            acc = _conv(xw, w)
            sg = _sigmoid(acc)
            y = acc * sg
            rn = lax.rsqrt(_group_sum(y * y, hs) + EPS)
            yn = y * rn
            dy = jnp.where(kind < 2, rn * (d - yn * _group_sum(d * yn, hs)), d)
            dacc = dy * sg * (1.0 + acc * (1.0 - sg))
            dx = w[3:4, :] * dacc
            for s in range(1, 4):
                dx += w[3 - s:4 - s, :] * _shift_up(dacc, s)
            dx_ref[_tile_rows(r), :] = dx[HALO:HALO + PREP_ROWS]
            dm = jnp.where(in_tile, dacc, 0.0)
            return tuple(dw[i] + jnp.sum(dm * (xw if i == 3 else _shift_down(xw, 3 - i)), axis=0, keepdims=True)
                         for i in range(4))

        dw = lax.fori_loop(0, t // PREP_ROWS, tile, tuple(jnp.zeros((1, LANE), F32) for _ in range(4)))
        for i in range(4):
            dw_ref[i:i + 1, :] = dw[i]

    return pl.pallas_call(
        body, name="gdn_prep_bwd", grid=(3, nj),
        in_specs=[pl.BlockSpec((t, LANE), lambda i, j: (0, i * nj + j)),
                  pl.BlockSpec((4, LANE), lambda i, j: (0, i * nj + j)),
                  pl.BlockSpec((1, 2, t, DH), lambda i, j: (i, j, 0, 0))],
        out_specs=[pl.BlockSpec((t, LANE), lambda i, j: (0, i * nj + j)),
                   pl.BlockSpec((4, LANE), lambda i, j: (0, i * nj + j))],
        out_shape=[jax.ShapeDtypeStruct((t, 3 * WIDTH), F32), jax.ShapeDtypeStruct((4, 3 * WIDTH), F32)],
        scratch_shapes=[pltpu.VMEM((t + 2 * HALO, LANE), F32), pltpu.VMEM((t + 2 * HALO, LANE), F32)],
        compiler_params=_params(("arbitrary", "arbitrary")),
    )(pm, conv_w, dqkv)


FOX_COL0 = 4 * WIDTH // LANE


def _fox_prep(pm, nw):
    t = pm.shape[0]
    nj = WIDTH // LANE

    def body(x_ref, w_ref, o_ref):
        kind = pl.program_id(0)
        hs = _head_sum_matrix()
        wk = w_ref[pl.ds(kind, 1), :]

        def tile(r, _):
            xv = x_ref[_tile_rows(r), :]
            ms = _group_sum(xv * xv, hs) * (1.0 / DH)
            out = jnp.where(kind < 2, xv * lax.rsqrt(ms + EPS) * wk, xv)
            o_ref[0, 0, _tile_rows(r), :] = out[:, :DH]
            o_ref[0, 1, _tile_rows(r), :] = out[:, DH:]
            return 0

        lax.fori_loop(0, t // PREP_ROWS, tile, 0)

    return pl.pallas_call(
        body, name="fox_prep", grid=(3, nj),
        in_specs=[pl.BlockSpec((t, LANE), lambda i, j: (0, FOX_COL0 + i * nj + j)),
                  pl.BlockSpec((3, LANE), lambda i, j: (0, 0))],
        out_specs=pl.BlockSpec((1, 2, t, DH), lambda i, j: (i, j, 0, 0)),
        out_shape=jax.ShapeDtypeStruct((3, HEADS, t, DH), F32),
        compiler_params=_params(("arbitrary", "arbitrary")),
    )(pm, nw)


def _fox_prep_bwd(pm, nw, dqkv):
    t = pm.shape[0]
    nj = WIDTH // LANE

    def body(x_ref, w_ref, d_ref, dx_ref, dw_ref):
        kind = pl.program_id(0)
        hs = _head_sum_matrix()
        wk = w_ref[pl.ds(kind, 1), :]

        def tile(r, dw):
            xv = x_ref[_tile_rows(r), :]
            rn = lax.rsqrt(_group_sum(xv * xv, hs) * (1.0 / DH) + EPS)
            xn = xv * rn
            d = jnp.concatenate([d_ref[0, 0, _tile_rows(r), :], d_ref[0, 1, _tile_rows(r), :]], axis=1)
            g = d * wk
            dxn = rn * (g - xn * _group_sum(g * xn, hs) * (1.0 / DH))
            dx_ref[_tile_rows(r), :] = jnp.where(kind < 2, dxn, d)
            return dw + jnp.sum(d * xn, axis=0, keepdims=True)

        dw_ref[0, 0] = lax.fori_loop(0, t // PREP_ROWS, tile, jnp.zeros((1, LANE), F32))

    return pl.pallas_call(
        body, name="fox_prep_bwd", grid=(3, nj),
        in_specs=[pl.BlockSpec((t, LANE), lambda i, j: (0, FOX_COL0 + i * nj + j)),
                  pl.BlockSpec((3, LANE), lambda i, j: (0, 0)),
                  pl.BlockSpec((1, 2, t, DH), lambda i, j: (i, j, 0, 0))],
        out_specs=[pl.BlockSpec((t, LANE), lambda i, j: (0, i * nj + j)),
                   pl.BlockSpec((1, 1, 1, LANE), lambda i, j: (i, j, 0, 0))],
        out_shape=[jax.ShapeDtypeStruct((t, 3 * WIDTH), F32), jax.ShapeDtypeStruct((3, nj, 1, LANE), F32)],
        compiler_params=_params(("arbitrary", "arbitrary")),
    )(pm, nw, dqkv)


SC = 256
CPS = SC // CHUNK
GDN_HP = 2
Q_SCALE = DH ** -0.5


def _sc_masks():
    ri = lax.broadcasted_iota(jnp.int32, (SC, SC), 0)
    ci = lax.broadcasted_iota(jnp.int32, (SC, SC), 1)
    same = (ri // CHUNK) == (ci // CHUNK)
    return same & (ri >= ci), same & (ri > ci), ri == ci


def _unit_lower_inverses(ms, eye):
    invs = [jnp.where(eye, 1.0, 0.0) + m for m in ms]
    for _ in range(5):
        ms = [_dot(m, m) for m in ms]
        invs = [inv + _dot(inv, m) for inv, m in zip(invs, ms)]
    return invs


def _lane_col(blk, lane_idx):
    lane = lax.broadcasted_iota(jnp.int32, blk.shape, 1)
    return jnp.sum(jnp.where(lane == lane_idx, blk, 0.0), axis=1, keepdims=True)


def _to_lane(col, lane_idx):
    lane = lax.broadcasted_iota(jnp.int32, (col.shape[0], LANE), 1)
    return jnp.where(lane == lane_idx, col, 0.0)


def _gdn_columns(gates_ref, run_ref, tot_ref, runt_ref, rows, h):
    return (_lane_col(gates_ref[rows, :], h), _lane_col(run_ref[rows, :], HEADS + h),
            _lane_col(tot_ref[rows, :], HEADS + h), runt_ref[pl.ds(h, 1), rows])


def _gdn_local(q, k, beta, gc, gl, grow, causal):
    decay = jnp.exp(jnp.where(causal, gc - grow, -1e30))
    egc = jnp.exp(gc)
    ekd = jnp.exp(gl - gc)
    qs = q * Q_SCALE
    kb = k * beta
    kk = _dot_nt(kb, k)
    qk = _dot_nt(qs, k)
    return beta, gl, decay, egc, ekd, qs, kb, kk, qk, jnp.where(causal, qk * decay, 0.0)


def _chunk_rows(c):
    return pl.ds(pl.multiple_of(c * CHUNK, CHUNK), CHUNK)


def _sc_rows(b):
    return pl.ds(pl.multiple_of(b * SC, SC), SC)


def _gdn_fwd(qkv, gates, run, tot, run_t):
    t = qkv.shape[2]
    nc = t // CHUNK
    nsc = t // SC

    hp_n = GDN_HP
    heads = range(hp_n)

    def body(qkv_ref, gates_ref, run_ref, tot_ref, runt_ref, o_ref, st_ref, inv_ref, kc_s, qc_s, g_s, au_s):
        hp = pl.program_id(0)
        causal, strict, eye = _sc_masks()

        def local(b, _):
            rows = _sc_rows(b)
            loc = [_gdn_local(qkv_ref[0, hh, rows, :], qkv_ref[1, hh, rows, :],
                              *_gdn_columns(gates_ref, run_ref, tot_ref, runt_ref, rows, hp * hp_n + hh), causal)
                   for hh in heads]
            invs = _unit_lower_inverses([-jnp.where(strict, l[7] * l[2], 0.0) for l in loc], eye)
            us, ws = [], []
            for hh in heads:
                beta, _, _, egc, _, _, kb, _, _, _ = loc[hh]
                inv_ref[hh, rows, :] = invs[hh].astype(inv_ref.dtype)
                us.append(_dot(invs[hh], qkv_ref[2, hh, rows, :] * beta))
                ws.append(_dot(invs[hh], kb * egc))
            for hh in heads:
                _, _, _, egc, ekd, qs, _, _, _, attn = loc[hh]
                g_s[hh, rows, :] = qs * egc - _dot(attn, ws[hh])
                au_s[hh, rows, :] = _dot(attn, us[hh])
                kd = qkv_ref[1, hh, rows, :] * ekd
                for j in range(CPS):
                    sl = slice(j * CHUNK, (j + 1) * CHUNK)
                    kc_s[hh, b * CPS + j] = _dot_tn(kd[sl], ws[hh][sl])
                    qc_s[hh, b * CPS + j] = _dot_tn(kd[sl], us[hh][sl])
            return 0

        lax.fori_loop(0, nsc, local, 0)

        def step(c, states):
            rows = _chunk_rows(c)
            tot_row = tot_ref[pl.ds(c * CHUNK, 1), :]
            new = []
            for hh in heads:
                s = states[hh]
                st_ref[hh, c] = s
                o_ref[hh, rows, :] = _dot(g_s[hh, rows, :], s) + au_s[hh, rows, :]
                egl = jnp.exp(_lane_col(tot_row, HEADS + hp * hp_n + hh))
                new.append(egl * s - _dot(kc_s[hh, c], s) + qc_s[hh, c])
            return tuple(new)

        lax.fori_loop(0, nc, step, tuple(jnp.zeros((DH, DH), F32) for _ in heads))

    whole = pl.BlockSpec((t, LANE), lambda h: (0, 0))
    return pl.pallas_call(
        body, name="gdn_fwd", grid=(HEADS // hp_n,),
        in_specs=[pl.BlockSpec((3, hp_n, t, DH), lambda h: (0, h, 0, 0)), whole, whole, whole,
                  pl.BlockSpec((HEADS, t), lambda h: (1, 0))],
        out_specs=[pl.BlockSpec((hp_n, t, DH), lambda h: (h, 0, 0)), pl.BlockSpec((hp_n, nc, DH, DH), lambda h: (h, 0, 0, 0)),
                   pl.BlockSpec((hp_n, t, SC), lambda h: (h, 0, 0))],
        out_shape=[jax.ShapeDtypeStruct((HEADS, t, DH), F32), jax.ShapeDtypeStruct((HEADS, nc, DH, DH), F32),
                   jax.ShapeDtypeStruct((HEADS, t, SC), MXU)],
        scratch_shapes=[pltpu.VMEM((hp_n, nc, DH, DH), F32), pltpu.VMEM((hp_n, nc, DH, DH), F32),
                        pltpu.VMEM((hp_n, t, DH), F32), pltpu.VMEM((hp_n, t, DH), F32)],
        compiler_params=_params(("arbitrary",)),
    )(qkv, gates, run, tot, run_t)


def _gdn_bwd(qkv, gates, run, tot, run_t, inv, states, do):
    t = qkv.shape[2]
    nc = t // CHUNK
    nsc = t // SC

    hp_n = GDN_HP
    heads = range(hp_n)

    def body(qkv_ref, gates_ref, run_ref, tot_ref, runt_ref, inv_ref, st_ref, do_ref,
             dqkv_ref, dcol_ref, dtot_ref, drow_ref, u_s, w_s, kc_s, h_s, dsn_s):
        hp = pl.program_id(0)
        causal, strict, _ = _sc_masks()

        @pl.when(hp == 0)
        def _():
            dcol_ref[...] = jnp.zeros_like(dcol_ref)
            dtot_ref[...] = jnp.zeros_like(dtot_ref)

        def local_of(hh, rows):
            return _gdn_local(qkv_ref[0, hh, rows, :], qkv_ref[1, hh, rows, :],
                              *_gdn_columns(gates_ref, run_ref, tot_ref, runt_ref, rows, hp * hp_n + hh), causal)

        def local(b, _):
            rows = _sc_rows(b)
            loc = [local_of(hh, rows) for hh in heads]
            us, ws = [], []
            for hh in heads:
                beta, _, _, egc, _, _, kb, _, _, _ = loc[hh]
                inv_b = inv_ref[hh, rows, :]
                us.append(_dot(inv_b, qkv_ref[2, hh, rows, :] * beta))
                ws.append(_dot(inv_b, kb * egc))
            gs = [loc[hh][5] * loc[hh][3] - _dot(loc[hh][9], ws[hh]) for hh in heads]
            for hh in heads:
                u_s[hh, rows, :] = us[hh]
                w_s[hh, rows, :] = ws[hh]
                kd = qkv_ref[1, hh, rows, :] * loc[hh][4]
                dout = do_ref[hh, rows, :]
                for j in range(CPS):
                    sl = slice(j * CHUNK, (j + 1) * CHUNK)
                    kc_s[hh, b * CPS + j] = _dot_tn(kd[sl], ws[hh][sl])
                    h_s[hh, b * CPS + j] = _dot_tn(gs[hh][sl], dout[sl])
            return 0

        lax.fori_loop(0, nsc, local, 0)

        def step(i, dss):
            c = nc - 1 - i
            tot_row = tot_ref[pl.ds(c * CHUNK, 1), :]
            new = []
            for hh in heads:
                ds = dss[hh]
                dsn_s[hh, c] = ds
                egl = jnp.exp(_lane_col(tot_row, HEADS + hp * hp_n + hh))
                new.append(egl * ds - _dot_tn(kc_s[hh, c], ds) + h_s[hh, c])
            return tuple(new)

        lax.fori_loop(0, nc, step, tuple(jnp.zeros((DH, DH), F32) for _ in heads))

        def back(b, _):
            rows = _sc_rows(b)
            first = lax.broadcasted_iota(jnp.int32, (CHUNK, 1), 0) == 0
            loc = [local_of(hh, rows) for hh in heads]
            mid = []
            for hh in heads:
                beta, gl, decay, egc, ekd, qs, kb, kk, qk, attn = loc[hh]
                u, w = u_s[hh, rows, :], w_s[hh, rows, :]
                kd = qkv_ref[1, hh, rows, :] * ekd
                dout = do_ref[hh, rows, :]
                dg_p, dkd_p, dw_p, du_p, dgl_p = [], [], [], [], []
                for j in range(CPS):
                    sl = slice(j * CHUNK, (j + 1) * CHUNK)
                    s = st_ref[hh, b * CPS + j]
                    dsn = dsn_s[hh, b * CPS + j]
                    dkc = -_dot_nt(dsn, s)
                    dg_p.append(_dot_nt(dout[sl], s))
                    dkd_p.append(_dot_nt(w[sl], dkc) + _dot_nt(u[sl], dsn))
                    dw_p.append(_dot(kd[sl], dkc))
                    du_p.append(_dot(kd[sl], dsn))
                    degl = jnp.sum(jnp.sum(s * dsn, axis=1, keepdims=True), axis=0, keepdims=True)
                    dgl_p.append(jnp.where(first, degl * jnp.exp(gl[j * CHUNK:j * CHUNK + 1, :]), 0.0))
                dg, dkd = jnp.concatenate(dg_p, axis=0), jnp.concatenate(dkd_p, axis=0)
                da = jnp.where(causal, _dot_nt(dout, u) - _dot_nt(dg, w), 0.0)
                at = _dot_tn(attn, jnp.concatenate([dout, dg], axis=1))
                du = at[:, :DH] + jnp.concatenate(du_p, axis=0)
                dw = jnp.concatenate(dw_p, axis=0) - at[:, DH:]
                mid.append((kd, dg, dkd, da, du, dw, jnp.concatenate(dgl_p, axis=0)))
            inv_ts = [inv_ref[hh, rows, :].astype(F32).T for hh in heads]
            its = [_dot(inv_ts[hh], jnp.concatenate([mid[hh][4], mid[hh][5]], axis=1)) for hh in heads]
            dinvs = []
            for hh in heads:
                beta, _, _, egc, _, _, kb, _, _, _ = loc[hh]
                dinvs.append(_dot_nt(mid[hh][4], qkv_ref[2, hh, rows, :] * beta) + _dot_nt(mid[hh][5], kb * egc))
            half = [_dot(inv_ts[hh], dinvs[hh]) for hh in heads]
            dls = [jnp.where(strict, -_dot(half[hh], inv_ts[hh]), 0.0) for hh in heads]
            for hh in heads:
                head = hp * hp_n + hh
                beta, gl, decay, egc, ekd, qs, kb, kk, qk, attn = loc[hh]
                kd, dg, dkd, da, du, dw, dgl_first = mid[hh]
                k, v = qkv_ref[1, hh, rows, :], qkv_ref[2, hh, rows, :]
                dvb, dkbe = its[hh][:, :DH], its[hh][:, DH:]
                dl = dls[hh]
                dlogd = (dl * kk + da * qk) * decay
                dkk = dl * decay
                dqk = da * decay
                dkb = _dot(dkk, k) + dkbe * egc
                dqs = _dot(dqk, k) + dg * egc
                dk = (_dot_tn(jnp.concatenate([dkk, dqk], axis=0), jnp.concatenate([kb, qs], axis=0))
                      + dkd * ekd + dkb * beta)
                dkd_kd = jnp.sum(dkd * kd, axis=1, keepdims=True)
                dgc = (jnp.sum(dlogd, axis=1, keepdims=True) + jnp.sum(dg * qs, axis=1, keepdims=True) * egc
                       + jnp.sum(dkbe * (kb * egc), axis=1, keepdims=True) - dkd_kd)
                dbeta = jnp.sum(dkb * k, axis=1, keepdims=True) + jnp.sum(dvb * v, axis=1, keepdims=True)
                dqkv_ref[0, hh, rows, :] = dqs * Q_SCALE
                dqkv_ref[1, hh, rows, :] = dk
                dqkv_ref[2, hh, rows, :] = dvb * beta
                dcol_ref[rows, :] += _to_lane(dbeta, head) + _to_lane(dgc, HEADS + head)
                dtot_ref[rows, :] += _to_lane(dkd_kd + dgl_first, HEADS + head)
                drow_ref[pl.ds(head, 1), rows] = -jnp.sum(dlogd, axis=0, keepdims=True)
            return 0

        lax.fori_loop(0, nsc, back, 0)

    whole = pl.BlockSpec((t, LANE), lambda h: (0, 0))
    rowspec = pl.BlockSpec((HEADS, t), lambda h: (0, 0))
    sq = pltpu.VMEM((hp_n, nc, DH, DH), F32)
    per_head = pltpu.VMEM((hp_n, t, DH), F32)
    return pl.pallas_call(
        body, name="gdn_bwd", grid=(HEADS // hp_n,),
        in_specs=[pl.BlockSpec((3, hp_n, t, DH), lambda h: (0, h, 0, 0)), whole, whole, whole,
                  pl.BlockSpec((HEADS, t), lambda h: (1, 0)),
                  pl.BlockSpec((hp_n, t, SC), lambda h: (h, 0, 0)), pl.BlockSpec((hp_n, nc, DH, DH), lambda h: (h, 0, 0, 0)),
                  pl.BlockSpec((hp_n, t, DH), lambda h: (h, 0, 0))],
        out_specs=[pl.BlockSpec((3, hp_n, t, DH), lambda h: (0, h, 0, 0)), whole, whole, rowspec],
        out_shape=[jax.ShapeDtypeStruct((3, HEADS, t, DH), F32), jax.ShapeDtypeStruct((t, LANE), F32),
                   jax.ShapeDtypeStruct((t, LANE), F32), jax.ShapeDtypeStruct((HEADS, t), F32)],
        scratch_shapes=[per_head, per_head, sq, sq, sq],
        compiler_params=_params(("arbitrary",)),
    )(qkv, gates, run, tot, run_t, inv, states, do)


FOX_HP_FWD = 4
FOX_HP_BWD = 2


def _wide_t(a):
    r = a.shape[0]
    return jnp.concatenate([a, jnp.zeros((r, LANE - DH), F32)], axis=1).T[:DH]


def _tall_t(a):
    r = a.shape[1]
    return jnp.concatenate([a, jnp.zeros((LANE - DH, r), F32)], axis=0).T[:, :DH]


def _key_side_f(run_blk, head, qb):
    col = jnp.broadcast_to(_lane_col(run_blk, 2 * HEADS + head), (run_blk.shape[0], LANE))
    return jnp.concatenate([col] * (qb // LANE), axis=1)


def _diag_mask(qb):
    return lax.broadcasted_iota(jnp.int32, (qb, qb), 0) <= lax.broadcasted_iota(jnp.int32, (qb, qb), 1)


def _fox_fwd(qkv, run, run_t, carry=None):
    t = qkv.shape[2]
    qb = min(QB, t)
    nq = t // qb
    hp_n = FOX_HP_FWD
    c_in, c_in_specs, c_out_shape, c_out_specs, c_sem = _carry_operands(*carry) if carry else ([], [], [], [], None)
    n_c = len(c_in)

    def body(*refs):
        q_ref, k_ref, v_ref, run_ref, runt_ref = refs[:5]
        o_ref, lse_ref = refs[5 + n_c:7 + n_c]
        kb_s, vt_s = refs[7 + n_c + len(c_out_shape):9 + n_c + len(c_out_shape)]
        hp = pl.program_id(0)
        i = pl.program_id(1)

        if carry:
            _carry_step(carry[0], refs[5:5 + n_c], refs[7 + n_c], refs[8 + n_c], refs[-1], hp * nq + i,
                        (HEADS // hp_n) * nq)

        @pl.when(i == 0)
        def _():
            for hh in range(hp_n):
                kb_s[hh] = k_ref[0, hh].astype(MXU)
                for b in range(nq):
                    rows = slice(b * qb, (b + 1) * qb)
                    vt_s[hh, :, rows] = _wide_t(v_ref[0, hh, rows, :]).astype(MXU)

        qrows = pl.ds(pl.multiple_of(i * qb, qb), qb)
        qs = [(q_ref[0, hh] * Q_SCALE).astype(MXU) for hh in range(hp_n)]
        fq = [runt_ref[pl.ds(hp * hp_n + hh, 1), qrows] for hh in range(hp_n)]

        def block_rows(j):
            return pl.ds(pl.multiple_of(j * qb, qb), qb)

        def scores(j):
            return tuple(_dot_nt(kb_s[hh, block_rows(j), :], qs[hh]) for hh in range(hp_n))

        def absorb(j, raw, state, diagonal):
            rows = block_rows(j)
            run_blk = run_ref[rows, :]
            stats, pv = [], []
            for hh in range(hp_n):
                m, l, _ = state[hh]
                st = raw[hh] + (fq[hh] - _key_side_f(run_blk, hp * hp_n + hh, qb))
                if diagonal:
                    st = jnp.where(_diag_mask(qb), st, -1e30)
                m_new = jnp.maximum(m, jnp.max(st, axis=0, keepdims=True))
                p = jnp.exp(st - m_new)
                alpha = jnp.exp(m - m_new)
                stats.append((m_new, alpha * l + jnp.sum(p, axis=0, keepdims=True), alpha))
                pv.append(_dot(vt_s[hh, :, rows], p))
            return tuple((stats[hh][0], stats[hh][1], stats[hh][2] * state[hh][2] + pv[hh]) for hh in range(hp_n))

        def kstep(j, carry):
            state, raw = carry
            ahead = scores(j + 1)
            return absorb(j, raw, state, False), ahead

        init = tuple((jnp.full((1, qb), -1e30, F32), jnp.zeros((1, qb), F32), jnp.zeros((DH, qb), F32))
                     for _ in range(hp_n))
        state, raw = lax.fori_loop(0, i, kstep, (init, scores(0)))
        state = absorb(i, raw, state, True)
        for hh in range(hp_n):
            m, l, acc = state[hh]
            o_ref[hh] = _tall_t(acc / l)
            lse_ref[pl.ds(hp * hp_n + hh, 1), qrows] = m + jnp.log(l)

    out = pl.pallas_call(
        body, name="fox_fwd", grid=(HEADS // hp_n, nq),
        in_specs=[pl.BlockSpec((1, hp_n, qb, DH), lambda h, i: (0, h, i, 0)),
                  pl.BlockSpec((1, hp_n, t, DH), lambda h, i: (1, h, 0, 0)),
                  pl.BlockSpec((1, hp_n, t, DH), lambda h, i: (2, h, 0, 0)),
                  pl.BlockSpec((t, LANE), lambda h, i: (0, 0)),
                  pl.BlockSpec((HEADS, t), lambda h, i: (2, 0))] + c_in_specs,
        out_specs=[pl.BlockSpec((hp_n, qb, DH), lambda h, i: (h, i, 0)),
                   pl.BlockSpec((HEADS, t), lambda h, i: (0, 0))] + c_out_specs,
        out_shape=[jax.ShapeDtypeStruct((HEADS, t, DH), F32), jax.ShapeDtypeStruct((HEADS, t), F32)] + c_out_shape,
        input_output_aliases={5 + j: 4 + j for j in range(n_c)},
        scratch_shapes=[pltpu.VMEM((hp_n, t, DH), MXU), pltpu.VMEM((hp_n, DH, t), MXU)] + ([c_sem] if carry else []),
        compiler_params=_params(("arbitrary", "arbitrary")),
    )(qkv, qkv, qkv, run, run_t, *c_in)
    return (out[0], out[1], _carry_state(out[2:])) if carry else (out[0], out[1])


def _fox_bwd(qkv, run, run_t, o, lse, do, carry=None):
    t = qkv.shape[2]
    qb = min(QB, t)
    nq = t // qb
    hp_n = FOX_HP_BWD
    c_in, c_in_specs, c_out_shape, c_out_specs, c_sem = _carry_operands(*carry) if carry else ([], [], [], [], None)
    n_c = len(c_in)

    def body(*refs):
        q_ref, k_ref, v_ref, run_ref, runt_ref, o_ref, lse_ref, do_ref = refs[:8]
        dqkv_ref, dcol_ref, drow_ref = refs[8 + n_c:11 + n_c]
        dqt_s = refs[11 + n_c + len(c_out_shape)]
        hp = pl.program_id(0)
        j = pl.program_id(1)

        if carry:
            _carry_step(carry[0], refs[8:8 + n_c], refs[11 + n_c], refs[12 + n_c], refs[-1], hp * nq + j,
                        (HEADS // hp_n) * nq)

        @pl.when(j == 0)
        def _():
            dqt_s[...] = jnp.zeros_like(dqt_s)

        @pl.when((j == 0) & (hp == 0))
        def _():
            dcol_ref[...] = jnp.zeros_like(dcol_ref)
            drow_ref[...] = jnp.zeros_like(drow_ref)

        krows = pl.ds(pl.multiple_of(j * qb, qb), qb)
        run_blk = run_ref[krows, :]
        ones8 = jnp.ones((8, DH), MXU)
        kb, kt, vb, fk = [], [], [], []
        for hh in range(hp_n):
            kf = k_ref[0, hh]
            kb.append(kf.astype(MXU))
            kt.append(_wide_t(kf).astype(MXU))
            vb.append(v_ref[0, hh].astype(MXU))
            fk.append(_key_side_f(run_blk, hp * hp_n + hh, qb))

        def block_rows(i):
            return pl.ds(pl.multiple_of(i * qb, qb), qb)

        def products(i):
            rows = block_rows(i)
            out = []
            for hh in range(hp_n):
                dout = do_ref[hh, rows, :]
                x = dout * o_ref[hh, rows, :]
                x_hi = x.astype(MXU)
                out.append((_dot_nt(kb[hh], q_ref[0, hh, rows, :] * Q_SCALE), _dot_nt(vb[hh], dout),
                            (_dot_nt(ones8, x_hi) + _dot_nt(ones8, x - x_hi.astype(F32)))[0:1, :]))
            return tuple(out)

        def absorb(i, prods, acc, diagonal):
            rows = block_rows(i)
            pieces = []
            for hh in range(hp_n):
                head = hp * hp_n + hh
                raw, dpt, drow = prods[hh]
                off = runt_ref[pl.ds(head, 1), rows] - lse_ref[pl.ds(head, 1), rows]
                st = raw + (off - fk[hh])
                if diagonal:
                    st = jnp.where(_diag_mask(qb), st, -1e30)
                pt = jnp.exp(st)
                dst = pt * (dpt - drow)
                drow_ref[pl.ds(head, 1), rows] += jnp.sum(dst, axis=0, keepdims=True)
                folded = dst[:, 0:LANE]
                for c in range(1, qb // LANE):
                    folded = folded + dst[:, c * LANE:(c + 1) * LANE]
                pieces.append((_dot(dst, q_ref[0, hh, rows, :] * Q_SCALE), _dot(pt, do_ref[hh, rows, :]),
                               _dot(kt[hh], dst), folded))
            out = []
            for hh in range(hp_n):
                dqt_s[hh, :, rows] += pieces[hh][2]
                out.append((acc[hh][0] + pieces[hh][0], acc[hh][1] + pieces[hh][1], acc[hh][2] + pieces[hh][3]))
            return tuple(out)

        def qstep(i, carry):
            acc, prods = carry
            ahead = products(jnp.minimum(i + 1, nq - 1))
            return absorb(i, prods, acc, False), ahead

        init = tuple((jnp.zeros((qb, DH), F32), jnp.zeros((qb, DH), F32), jnp.zeros((qb, LANE), F32))
                     for _ in range(hp_n))
        first = products(j)
        ahead = products(jnp.minimum(j + 1, nq - 1))
        acc = absorb(j, first, init, True)
        acc, _ = lax.fori_loop(j + 1, nq, qstep, (acc, ahead))
        for hh in range(hp_n):
            dk, dv, ds_sum = acc[hh]
            dqkv_ref[1, hh, krows, :] = dk
            dqkv_ref[2, hh, krows, :] = dv
            dcol_ref[krows, :] += _to_lane(-jnp.sum(ds_sum, axis=1, keepdims=True), 2 * HEADS + hp * hp_n + hh)

        @pl.when(j == nq - 1)
        def _():
            for hh in range(hp_n):
                for b in range(nq):
                    rows = slice(b * qb, (b + 1) * qb)
                    dqkv_ref[0, hh, rows, :] = _tall_t(dqt_s[hh, :, rows]) * Q_SCALE

    full = pl.BlockSpec((hp_n, t, DH), lambda h, j: (h, 0, 0))
    rows8 = pl.BlockSpec((HEADS, t), lambda h, j: (0, 0))
    out = pl.pallas_call(
        body, name="fox_bwd", grid=(HEADS // hp_n, nq),
        in_specs=[pl.BlockSpec((1, hp_n, t, DH), lambda h, j: (0, h, 0, 0)),
                  pl.BlockSpec((1, hp_n, qb, DH), lambda h, j: (1, h, j, 0)),
                  pl.BlockSpec((1, hp_n, qb, DH), lambda h, j: (2, h, j, 0)),
                  pl.BlockSpec((t, LANE), lambda h, j: (0, 0)), pl.BlockSpec((HEADS, t), lambda h, j: (2, 0)),
                  full, rows8, full] + c_in_specs,
        out_specs=[pl.BlockSpec((3, hp_n, t, DH), lambda h, j: (0, h, 0, 0)),
                   pl.BlockSpec((t, LANE), lambda h, j: (0, 0)), rows8] + c_out_specs,
        out_shape=[jax.ShapeDtypeStruct((3, HEADS, t, DH), F32), jax.ShapeDtypeStruct((t, LANE), F32),
                   jax.ShapeDtypeStruct((HEADS, t), F32)] + c_out_shape,
        input_output_aliases={8 + j: 5 + j for j in range(n_c)},
        scratch_shapes=[pltpu.VMEM((hp_n, DH, t), F32)] + ([c_sem] if carry else []),
        compiler_params=_params(("arbitrary", "arbitrary")),
    )(qkv, qkv, qkv, run, run_t, o, lse, do, *c_in)
    return (out[0], out[1], out[2], _carry_state(out[3:])) if carry else tuple(out)


Z_COL0 = 3 * WIDTH // LANE
FGATE_COL0 = 7 * WIDTH // LANE


def _gdn_post(o, pm, onw):
    t = pm.shape[0]

    def body(o_ref, z_ref, w_ref, m_ref):
        z = z_ref[...]
        sz = z * _sigmoid(z)
        halves = []
        for hh in range(2):
            ov = o_ref[hh]
            n = ov * lax.rsqrt(jnp.mean(ov * ov, axis=-1, keepdims=True) + EPS) * w_ref[...]
            halves.append(n * sz[:, hh * DH:(hh + 1) * DH])
        m_ref[...] = jnp.concatenate(halves, axis=1).astype(m_ref.dtype)

    return pl.pallas_call(
        body, name="gdn_post", grid=(WIDTH // LANE,),
        in_specs=[pl.BlockSpec((2, t, DH), lambda j: (j, 0, 0)), pl.BlockSpec((t, LANE), lambda j: (0, Z_COL0 + j)),
                  pl.BlockSpec((1, DH), lambda j: (0, 0))],
        out_specs=pl.BlockSpec((t, LANE), lambda j: (0, j)),
        out_shape=jax.ShapeDtypeStruct((t, WIDTH), MXU),
        compiler_params=_params(("arbitrary",)),
    )(o, pm, onw)


def _gdn_post_bwd(o, pm, onw, dmix):
    t = pm.shape[0]

    def body(o_ref, z_ref, w_ref, dm_ref, do_ref, dz_ref, dw_ref):
        @pl.when(pl.program_id(0) == 0)
        def _():
            dw_ref[...] = jnp.zeros_like(dw_ref)

        z = z_ref[...]
        sg = _sigmoid(z)
        sz = z * sg
        dsz = sg * (1.0 + z * (1.0 - sg))
        dm = dm_ref[...]
        for hh in range(2):
            cols = slice(hh * DH, (hh + 1) * DH)
            ov = o_ref[hh]
            r = lax.rsqrt(jnp.mean(ov * ov, axis=-1, keepdims=True) + EPS)
            xn = ov * r
            dmh = dm[:, cols]
            dn = dmh * sz[:, cols]
            dz_ref[:, cols] = dmh * (xn * w_ref[...]) * dsz[:, cols]
            dw_ref[...] += jnp.sum(dn * xn, axis=0, keepdims=True)
            g = dn * w_ref[...]
            do_ref[hh] = r * (g - xn * jnp.mean(g * xn, axis=-1, keepdims=True))

    return pl.pallas_call(
        body, name="gdn_post_bwd", grid=(WIDTH // LANE,),
        in_specs=[pl.BlockSpec((2, t, DH), lambda j: (j, 0, 0)), pl.BlockSpec((t, LANE), lambda j: (0, Z_COL0 + j)),
                  pl.BlockSpec((1, DH), lambda j: (0, 0)), pl.BlockSpec((t, LANE), lambda j: (0, j))],
        out_specs=[pl.BlockSpec((2, t, DH), lambda j: (j, 0, 0)), pl.BlockSpec((t, LANE), lambda j: (0, j)),
                   pl.BlockSpec((1, DH), lambda j: (0, 0))],
        out_shape=[jax.ShapeDtypeStruct((HEADS, t, DH), F32), jax.ShapeDtypeStruct((t, WIDTH), F32),
                   jax.ShapeDtypeStruct((1, DH), F32)],
        compiler_params=_params(("arbitrary",)),
    )(o, pm, onw, dmix)


def _fox_post(o, pm):
    t = pm.shape[0]

    def body(o_ref, g_ref, m_ref):
        m_ref[...] = (jnp.concatenate([o_ref[0], o_ref[1]], axis=1) * _sigmoid(g_ref[...])).astype(m_ref.dtype)

    return pl.pallas_call(
        body, name="fox_post", grid=(WIDTH // LANE,),
        in_specs=[pl.BlockSpec((2, t, DH), lambda j: (j, 0, 0)), pl.BlockSpec((t, LANE), lambda j: (0, FGATE_COL0 + j))],
        out_specs=pl.BlockSpec((t, LANE), lambda j: (0, j)),
        out_shape=jax.ShapeDtypeStruct((t, WIDTH), MXU),
        compiler_params=_params(("arbitrary",)),
    )(o, pm)


def _fox_post_bwd(o, pm, dmix):
    t = pm.shape[0]

    def body(o_ref, g_ref, dm_ref, do_ref, dg_ref):
        sg = _sigmoid(g_ref[...])
        dm = dm_ref[...]
        for hh in range(2):
            cols = slice(hh * DH, (hh + 1) * DH)
            do_ref[hh] = dm[:, cols] * sg[:, cols]
            dg_ref[:, cols] = dm[:, cols] * o_ref[hh] * (sg * (1.0 - sg))[:, cols]

    return pl.pallas_call(
        body, name="fox_post_bwd", grid=(WIDTH // LANE,),
        in_specs=[pl.BlockSpec((2, t, DH), lambda j: (j, 0, 0)), pl.BlockSpec((t, LANE), lambda j: (0, FGATE_COL0 + j)),
                  pl.BlockSpec((t, LANE), lambda j: (0, j))],
        out_specs=[pl.BlockSpec((2, t, DH), lambda j: (j, 0, 0)), pl.BlockSpec((t, LANE), lambda j: (0, j))],
        out_shape=[jax.ShapeDtypeStruct((HEADS, t, DH), F32), jax.ShapeDtypeStruct((t, WIDTH), F32)],
        compiler_params=_params(("arbitrary",)),
    )(o, pm, dmix)


def _tail(x, mixg, mixf, tgt, wo, n2w, wg_t, wu_t, wd, fw):
    t, d = x.shape
    dff = wd.shape[0]
    tb = min(TB, t)

    def body(x_ref, mg_ref, mf_ref, t_ref, wo_ref, n2_ref, wg_ref, wu_ref, wd_ref, fw_ref,
             h2_ref, act_ref, dgate_ref, dup_ref, dx3_ref, dx2_ref, dmg_ref, dmf_ref, dn2_ref, dfw_ref, loss_ref):
        @pl.when(pl.program_id(0) == 0)
        def _():
            dn2_ref[...] = jnp.zeros_like(dn2_ref)
            dfw_ref[...] = jnp.zeros_like(dfw_ref)
            loss_ref[...] = jnp.zeros_like(loss_ref)

        x2 = x_ref[...] + _dot(mg_ref[...], wo_ref[0:WIDTH, :]) + _dot(mf_ref[...], wo_ref[WIDTH:2 * WIDTH, :])
        r2 = lax.rsqrt(jnp.mean(x2 * x2, axis=-1, keepdims=True) + EPS)
        xn2 = x2 * r2
        h2 = (xn2 * n2_ref[...]).astype(MXU)
        h2_ref[...] = h2
        gate = _dot_nt(h2, wg_ref[...])
        up = _dot_nt(h2, wu_ref[...])
        sg = _sigmoid(gate)
        sl = gate * sg
        act = (sl * up).astype(MXU)
        act_ref[...] = act
        x3 = x2 + _dot(act, wd_ref[...])
        r3 = lax.rsqrt(jnp.mean(x3 * x3, axis=-1, keepdims=True) + EPS)
        xn3 = x3 * r3
        err = xn3 * fw_ref[...] - t_ref[...]
        loss_ref[...] += 0.5 * jnp.sum(jnp.mean(err * err, axis=-1, keepdims=True), axis=0, keepdims=True)
        dy = err * (1.0 / d)
        dfw_ref[...] += jnp.sum(dy * xn3, axis=0, keepdims=True)
        g3 = dy * fw_ref[...]
        dx3 = r3 * (g3 - xn3 * jnp.mean(g3 * xn3, axis=-1, keepdims=True))
        dx3_ref[...] = dx3.astype(MXU)
        dact = _dot_nt(dx3, wd_ref[...])
        dgate = (dact * up * (sg * (1.0 + gate * (1.0 - sg)))).astype(MXU)
        dup = (dact * sl).astype(MXU)
        dgate_ref[...] = dgate
        dup_ref[...] = dup
        dh2 = _dot(dgate, wg_ref[...]) + _dot(dup, wu_ref[...])
        dn2_ref[...] += jnp.sum(dh2 * xn2, axis=0, keepdims=True)
        g2 = dh2 * n2_ref[...]
        dx2 = dx3 + r2 * (g2 - xn2 * jnp.mean(g2 * xn2, axis=-1, keepdims=True))
        dx2_ref[...] = dx2
        dmg_ref[...] = _dot_nt(dx2, wo_ref[0:WIDTH, :])
        dmf_ref[...] = _dot_nt(dx2, wo_ref[WIDTH:2 * WIDTH, :])

    def tok(n):
        return pl.BlockSpec((tb, n), lambda i: (i, 0))

    acc = pl.BlockSpec((1, d), lambda i: (0, 0))
    sds = jax.ShapeDtypeStruct
    return pl.pallas_call(
        body, name="tail", grid=(t // tb,),
        in_specs=[tok(d), tok(WIDTH), tok(WIDTH), tok(d), _resident(wo.shape), _resident((1, d)),
                  _resident(wg_t.shape), _resident(wu_t.shape), _resident(wd.shape), _resident((1, d))],
        out_specs=[tok(d), tok(dff), tok(dff), tok(dff), tok(d), tok(d), tok(WIDTH), tok(WIDTH), acc, acc,
                   pl.BlockSpec((1, 1), lambda i: (0, 0))],
        out_shape=[sds((t, d), MXU), sds((t, dff), MXU), sds((t, dff), MXU), sds((t, dff), MXU), sds((t, d), MXU),
                   sds((t, d), F32), sds((t, WIDTH), F32), sds((t, WIDTH), F32), sds((1, d), F32), sds((1, d), F32),
                   sds((1, 1), F32)],
        compiler_params=_params(("arbitrary",)),
    )(x, mixg, mixf, tgt, wo, n2w, wg_t, wu_t, wd, fw)


def _wgrad(a, b, name):
    t, m = a.shape
    n = b.shape[1]
    bm = 256 if m % 256 == 0 else LANE
    cast = b.dtype != jnp.dtype(MXU)

    def body(a_ref, b_ref, o_ref, *scratch):
        if cast:
            @pl.when(pl.program_id(0) == 0)
            def _():
                scratch[0][...] = b_ref[...].astype(MXU)
        o_ref[...] = _dot_tn(a_ref[...], scratch[0][...] if cast else b_ref[...]).astype(o_ref.dtype)

    return pl.pallas_call(
        body, name=name, grid=(m // bm,),
        in_specs=[pl.BlockSpec((t, bm), lambda i: (0, i)), _resident((t, n))],
        out_specs=pl.BlockSpec((bm, n), lambda i: (i, 0)),
        out_shape=jax.ShapeDtypeStruct((m, n), WIRE),
        scratch_shapes=[pltpu.VMEM((t, n), MXU)] if cast else [],
        compiler_params=_params(("arbitrary",)),
    )(a, b)


def _merge_dw_in(d_gdn, d_z, d_fox, d_fg, d_small):
    return jnp.concatenate([d_gdn, d_z, d_small[:2 * HEADS], d_fox, d_fg, d_small[2 * HEADS:3 * HEADS]], axis=0)


def _lanes(*pieces):
    v = jnp.concatenate([p.reshape(-1).astype(F32) for p in pieces])
    return jnp.pad(v, (0, LANE - v.shape[0])).reshape(1, LANE)


def _vector_params(p):
    d = p["norm1_w"].size
    gparams = jnp.concatenate([_lanes(jnp.zeros(HEADS), p["gdn_dt_bias"], p["fox_f_bias"]),
                               _lanes(jnp.zeros(HEADS), p["gdn_A_log"]), jnp.zeros((6, LANE), F32)])
    fox_nw = jnp.stack([jnp.tile(p["fox_q_norm_w"].reshape(-1), 2), jnp.tile(p["fox_k_norm_w"].reshape(-1), 2),
                        jnp.ones((LANE,), F32)])
    return dict(n1w=p["norm1_w"].reshape(1, d), n2w=p["norm2_w"].reshape(1, d), fw=p["final_norm_w"].reshape(1, d),
                onw=p["gdn_out_norm_w"].reshape(1, DH), gparams=gparams, fox_nw=fox_nw)


def _mixer_forward(x, vp, w_t, ws_t, conv_w, carry=None):
    h1, pm, ps = _inproj(x, vp["n1w"], w_t, ws_t)
    gates, run, tot, run_t = _gates(ps, vp["gparams"])
    fqkv = _fox_prep(pm, vp["fox_nw"])
    o_fox, lse, *carried = _fox_fwd(fqkv, run, run_t, carry)
    if carry:
        pm, o_fox = lax.optimization_barrier((pm, o_fox))
    mixf = _fox_post(o_fox, pm)
    gqkv = _gdn_prep(pm, conv_w)
    o_gdn, states, inv = _gdn_fwd(gqkv, gates, run, tot, run_t)
    mixg = _gdn_post(o_gdn, pm, vp["onw"])
    return dict(h1=h1, pm=pm, ps=ps, gates=gates, run=run, tot=tot, run_t=run_t, gqkv=gqkv, o_gdn=o_gdn, states=states,
                inv=inv, mixg=mixg, fqkv=fqkv, o_fox=o_fox, lse=lse, mixf=mixf, carried=carried[0] if carried else None)


def _mixer_backward(x, vp, w_t, ws_t, conv_w, f, dx2, dmixg, dmixf, carry=None, scatter_own=False):
    pm = f["pm"]
    do_fox, dfg = _fox_post_bwd(f["o_fox"], pm, dmixf)
    dfqkv, dcol_f, drow_f, *carried = _fox_bwd(f["fqkv"], f["run"], f["run_t"], f["o_fox"], f["lse"], do_fox, carry)
    dfox, dfnw = _fox_prep_bwd(pm, vp["fox_nw"], dfqkv)
    do_gdn, dz, donw = _gdn_post_bwd(f["o_gdn"], pm, vp["onw"], dmixg)
    dgqkv, dcol_g, dtot_g, drow_g = _gdn_bwd(f["gqkv"], f["gates"], f["run"], f["tot"], f["run_t"], f["inv"], f["states"], do_gdn)
    dgdn, dconv = _gdn_prep_bwd(pm, conv_w, dgqkv)
    dps, gsum = _gates_bwd(f["ps"], vp["gparams"], dcol_g, dtot_g, dcol_f, drow_g, drow_f)
    h1 = f["h1"]
    dw_in = _merge_dw_in(_wgrad(dgdn, h1, "dw_in_gdn"), _wgrad(dz, h1, "dw_in_z"), _wgrad(dfox, h1, "dw_in_fox"),
                         _wgrad(dfg, h1, "dw_in_fgate"), _wgrad(dps, h1, "dw_in_small"))
    own = ("scatter", [_cut(dw_in, 0), _cut(dconv, 1)]) if scatter_own else None
    grad_x, dn1w, *sent = _inproj_bwd(x, vp["n1w"], dx2, dgdn, dz, dfox, dfg, dps, w_t, ws_t, own)
    small = dict(dn1w=dn1w, gsum=gsum, donw=donw, dfnw=dfnw)
    return (grad_x, dw_in, dconv, small, *carried, *sent)


VECTORS = ("norm1_w", "norm2_w", "final_norm_w", "gdn_A_log", "gdn_dt_bias", "gdn_out_norm_w", "fox_f_bias",
           "fox_q_norm_w", "fox_k_norm_w")
VEC_ROWS = 16
LOSS_ROW = len(VECTORS)


def _pack_vectors(dn1w, dn2w, dfw, gsum, donw, dfnw, loss):
    d = dn1w.shape[1]

    def body(n1_ref, n2_ref, fw_ref, gs_ref, on_ref, fn_ref, loss_ref, o_ref):
        o_ref[...] = jnp.zeros_like(o_ref)
        o_ref[0:1, :] = n1_ref[...]
        o_ref[1:2, :] = n2_ref[...]
        o_ref[2:3, :] = fw_ref[...]
        o_ref[3:4, 0:HEADS] = gs_ref[0:1, 0:HEADS]
        o_ref[4:5, 0:HEADS] = gs_ref[1:2, 0:HEADS]
        o_ref[5:6, 0:DH] = on_ref[...]
        o_ref[6:7, 0:HEADS] = gs_ref[2:3, 0:HEADS]
        for kind in range(2):
            v = fn_ref[kind, 0]
            for j in range(1, fn_ref.shape[1]):
                v = v + fn_ref[kind, j]
            o_ref[7 + kind:8 + kind, 0:DH] = v[:, :DH] + v[:, DH:]
        o_ref[LOSS_ROW:LOSS_ROW + 1, 0:1] = loss_ref[...]

    return pl.pallas_call(body, name="pack_vectors", out_shape=jax.ShapeDtypeStruct((VEC_ROWS, d), F32),
                          compiler_params=_params())(dn1w, dn2w, dfw, gsum, donw, dfnw, loss)


def _late_grads(f, h2, act, dgate, dup, dx3, dx2):
    return {"w_out": jnp.concatenate([_wgrad(f["mixg"], dx2, "dw_out_gdn"), _wgrad(f["mixf"], dx2, "dw_out_fox")], axis=0),
            "w_ffn_gate": _wgrad(dgate, h2, "dw_gate"), "w_ffn_up": _wgrad(dup, h2, "dw_up"),
            "w_ffn_down": _wgrad(act, dx3, "dw_down")}


def _local_step(x, tgt, p, w_in_t, conv_w, wo, wg_t, wu_t, wd):
    vp = _vector_params(p)
    ws_t = _small_rows(w_in_t)
    f = _mixer_forward(x, vp, w_in_t, ws_t, conv_w)
    (h2, act, dgate, dup, dx3, dx2, dmixg, dmixf, dn2w, dfw, loss) = _tail(
        x, f["mixg"], f["mixf"], tgt, wo, vp["n2w"], wg_t, wu_t, wd, vp["fw"])
    grad_x, dw_in, dconv, small = _mixer_backward(x, vp, w_in_t, ws_t, conv_w, f, dx2, dmixg, dmixf)
    grads = {"w_in": dw_in, "gdn_conv_w": dconv, **_late_grads(f, h2, act, dgate, dup, dx3, dx2)}
    vec = _pack_vectors(small["dn1w"], dn2w, dfw, small["gsum"], small["donw"], small["dfnw"], loss)
    return grad_x, grads, vec


def _my_place():
    return lax.axis_index("x"), lax.axis_index("y"), lax.axis_index("c")


def _peers():
    x, y, c = _my_place()
    peers = []
    for k in range(1, N_DEV):
        px = 1 - x if k & 4 else x
        py = 1 - y if k & 2 else y
        pc = 1 - c if k & 1 else c
        peers.append(((px, py, pc), 4 * px + 2 * py + pc))
    return 4 * x + 2 * y + c, peers


def _spread_copies(kind, srcs, lands, send_sems, recv_sems):
    me, peers = _peers()
    kinds = [kind] * len(srcs) if isinstance(kind, str) else kind
    remote, local = [], []
    for i, (kd, src, land) in enumerate(zip(kinds, srcs, lands)):
        for k, (dev, idx) in enumerate(peers):
            remote.append(pltpu.make_async_remote_copy(
                src_ref=src if kd == "gather" else src.at[idx], dst_ref=land.at[me],
                send_sem=send_sems.at[i * (N_DEV - 1) + k], recv_sem=recv_sems.at[i * (N_DEV - 1) + k],
                device_id=dev, device_id_type=MESH))
        local.append((src if kd == "gather" else src.at[me], land.at[me]))
    return remote, local


def _gather_two_level(arrays, name):
    n = len(arrays)

    def body(*refs):
        srcs, lands = refs[:n], refs[n:2 * n]
        send_sems, recv_sems, local_sems = refs[2 * n:]
        x, y, c = _my_place()
        me, sibling = (x, y, c), (x, y, 1 - c)
        chips = [(1 - x, y), (x, 1 - y), (1 - x, 1 - y)]

        def index(p):
            return 4 * p[0] + 2 * p[1] + p[2]

        def copy(i, k, block, to, src=None):
            blk = lands[i].at[index(block)]
            return pltpu.make_async_remote_copy(
                src_ref=blk if src is None else src, dst_ref=blk, send_sem=send_sems.at[7 * i + k],
                recv_sem=recv_sems.at[7 * i + k], device_id=to, device_id_type=MESH)

        mine = [pltpu.make_async_copy(srcs[i], lands[i].at[index(me)], local_sems.at[i]) for i in range(n)]
        for cp in mine:
            cp.start()
        first = []
        for i in range(n):
            first.append(copy(i, 0, me, sibling, src=srcs[i]))
            first += [copy(i, 1 + j, me, (*chip, c), src=srcs[i]) for j, chip in enumerate(chips)]
        for cp in first:
            cp.start()
        passed = []
        for i in range(n):
            for j, chip in enumerate(chips):
                copy(i, 1 + j, (*chip, c), me).wait_recv()
                passed.append(copy(i, 4 + j, (*chip, c), sibling))
                passed[-1].start()
        for i in range(n):
            copy(i, 0, sibling, me).wait_recv()
            for j, chip in enumerate(chips):
                copy(i, 4 + j, (*chip, 1 - c), me).wait_recv()
        for cp in first + passed:
            cp.wait_send()
        for cp in mine:
            cp.wait()

    return pl.pallas_call(
        body, name=name,
        out_shape=[jax.ShapeDtypeStruct((N_DEV,) + a.shape, a.dtype) for a in arrays],
        in_specs=[pl.BlockSpec(memory_space=pl.ANY)] * n, out_specs=[pl.BlockSpec(memory_space=pl.ANY)] * n,
        scratch_shapes=[pltpu.SemaphoreType.DMA((7 * n,)), pltpu.SemaphoreType.DMA((7 * n,)),
                        pltpu.SemaphoreType.DMA((n,))],
    )(*arrays)


def _land_shape(kind, a):
    return (N_DEV,) + a.shape if kind == "gather" else a.shape


HBM = pl.BlockSpec(memory_space=pltpu.HBM)
SEM = pl.BlockSpec(memory_space=pltpu.SEMAPHORE)


def _hbm(a):
    return pltpu.with_memory_space_constraint(a, pltpu.HBM)


def _carry_operands(kind, arrays):
    n = len(arrays)
    kinds = [kind] * n if isinstance(kind, str) else kind
    lands = [lax.empty(_land_shape(kd, a), a.dtype) for kd, a in zip(kinds, arrays)]
    sems = [pltpu.SemaphoreType.DMA((n * (N_DEV - 1),))] * 2
    return ([_hbm(a) for a in list(arrays) + lands], [HBM] * (2 * n),
            sems + [pltpu.HBM(a.shape, a.dtype) for a in list(arrays) + lands], [SEM] * 2 + [HBM] * (2 * n),
            pltpu.SemaphoreType.DMA((n,)))


def _carry_step(kind, in_refs, send_sems, recv_sems, local_sems, step, n_steps):
    n = len(in_refs) // 2
    remote, local = _spread_copies(kind, in_refs[:n], in_refs[n:], send_sems, recv_sems)
    copies = [pltpu.make_async_copy(s, d, local_sems.at[i]) for i, (s, d) in enumerate(local)]
    per_step = -(-len(remote) // n_steps)
    for s in range(-(-len(remote) // per_step)):
        @pl.when(step == s)
        def _():
            for cp in remote[s * per_step:(s + 1) * per_step]:
                cp.start()
            if s == 0:
                for cp in copies:
                    cp.start()

    @pl.when(step == n_steps - 1)
    def _():
        for cp in copies:
            cp.wait()


def _carry_state(extra_out):
    n = (len(extra_out) - 2) // 2
    return list(extra_out[2:2 + n]), list(extra_out[2 + n:]), extra_out[0], extra_out[1]


def _spread_start(arrays, kind, name):
    n = len(arrays)

    def body(*refs):
        srcs, lands = refs[:n], refs[n:2 * n]
        send_sems, recv_sems = refs[2 * n], refs[2 * n + 1]
        token = refs[4 * n + 2]
        local_sems = refs[4 * n + 3]
        remote, local = _spread_copies(kind, srcs, lands, send_sems, recv_sems)
        for cp in remote:
            cp.start()
        copies = [pltpu.make_async_copy(s, d, local_sems.at[i]) for i, (s, d) in enumerate(local)]
        for cp in copies:
            cp.start()
        for cp in copies:
            cp.wait()
        token[...] = jnp.zeros_like(token)

    sems = (pltpu.SemaphoreType.DMA((n * (N_DEV - 1),)),) * 2
    kinds = [kind] * n if isinstance(kind, str) else kind
    lands = [lax.empty(_land_shape(kd, a), a.dtype) for kd, a in zip(kinds, arrays)]
    out = pl.pallas_call(
        body, name=name,
        out_shape=sems + tuple(pltpu.HBM(a.shape, a.dtype) for a in list(arrays) + lands)
        + (jax.ShapeDtypeStruct((8, LANE), F32),),
        in_specs=[HBM] * (2 * n), out_specs=tuple([SEM] * 2 + [HBM] * (2 * n) + [pl.BlockSpec(memory_space=pltpu.VMEM)]),
        input_output_aliases={j: 2 + j for j in range(2 * n)},
        scratch_shapes=[pltpu.SemaphoreType.DMA((n,))],
        compiler_params=pltpu.CompilerParams(has_side_effects=pltpu.SideEffectType.DATAFLOW_SIDE_EFFECTING),
    )(*[_hbm(a) for a in arrays], *[_hbm(a) for a in lands])
    return (list(out[2:2 + n]), list(out[2 + n:2 + 2 * n]), out[0], out[1]), out[-1]


def _spread_wait(state, kind, after, name):
    srcs, lands, send_sems, recv_sems = state
    n = len(srcs)
    after = list(after) if isinstance(after, (list, tuple)) else [after]

    def body(*refs):
        remote, _ = _spread_copies(kind, refs[:n], refs[n:2 * n], refs[2 * n], refs[2 * n + 1])
        for cp in remote:
            cp.wait_send()
        for cp in remote:
            cp.wait_recv()

    out = pl.pallas_call(
        body, name=name,
        out_shape=tuple(pltpu.HBM(a.shape, a.dtype) for a in srcs + lands),
        in_specs=[HBM] * (2 * n) + [SEM, SEM] + [pl.BlockSpec(memory_space=pl.ANY)] * len(after),
        out_specs=tuple([HBM] * (2 * n)),
        input_output_aliases={j: j for j in range(2 * n)},
        compiler_params=pltpu.CompilerParams(has_side_effects=pltpu.SideEffectType.DATAFLOW_SIDE_EFFECTING),
    )(*srcs, *lands, send_sems, recv_sems, *after)
    return list(out[n:])


ADAM_ROWS = 128
ADAM_COLS = 256


def _adam_math(g, w, m, v):
    nm = ADAM_B1 * m + (1.0 - ADAM_B1) * g
    nv = ADAM_B2 * v + (1.0 - ADAM_B2) * (g * g)
    m_hat = nm / (1.0 - ADAM_B1 ** ADAM_STEP)
    v_hat = nv / (1.0 - ADAM_B2 ** ADAM_STEP)
    return -ADAM_LR * (m_hat / (jnp.sqrt(v_hat) + ADAM_EPS) + ADAM_WD * w), nm, nv


def _sum_parts(p_ref):
    g = p_ref[0].astype(F32)
    for s in range(1, N_DEV):
        g = g + p_ref[s].astype(F32)
    return g


def _adam_matrix(parts, w, m, v, name):
    _, r, c = w.shape
    rb = ADAM_ROWS if r % ADAM_ROWS == 0 else r
    cb = ADAM_COLS if (rb == r and c % ADAM_COLS == 0) else c

    def body(p_ref, w_ref, m_ref, v_ref, g_ref, d_ref, nm_ref, nv_ref):
        g = _sum_parts(p_ref)
        g_ref[0] = g
        d_ref[0], nm_ref[0], nv_ref[0] = _adam_math(g, w_ref[0], m_ref[0], v_ref[0])

    blk = pl.BlockSpec((1, rb, cb), lambda i, j: (0, i, j))
    return pl.pallas_call(
        body, name=name, grid=(r // rb, c // cb),
        in_specs=[pl.BlockSpec((N_DEV, rb, cb), lambda i, j: (0, i, j)), blk, blk, blk],
        out_specs=[blk] * 4, out_shape=[jax.ShapeDtypeStruct(w.shape, F32)] * 4,
        compiler_params=_params(("arbitrary", "arbitrary")),
    )(parts, w, m, v)


def _adam_vectors(parts, ws, ms, vs):
    nv = len(ws)

    def body(*refs):
        p_ref = refs[0]
        w_refs, m_refs, v_refs = refs[1:1 + nv], refs[1 + nv:1 + 2 * nv], refs[1 + 2 * nv:1 + 3 * nv]
        outs = refs[1 + 3 * nv:]
        g_all = _sum_parts(p_ref)
        for i in range(nv):
            n = w_refs[i].shape[1]
            g = g_all[i:i + 1, 0:n]
            d, nm, nvv = _adam_math(g, w_refs[i][...], m_refs[i][...], v_refs[i][...])
            outs[i][...] = g
            outs[nv + i][...] = d
            outs[2 * nv + i][...] = nm
            outs[3 * nv + i][...] = nvv
        outs[4 * nv][...] = g_all[LOSS_ROW:LOSS_ROW + 1, 0:1]

    shapes = [jax.ShapeDtypeStruct(a.shape, F32) for a in ws]
    out = pl.pallas_call(body, name="adam_vectors", out_shape=shapes * 4 + [jax.ShapeDtypeStruct((1, 1), F32)],
                         compiler_params=_params())(parts, *ws, *ms, *vs)
    return out[:nv], out[nv:2 * nv], out[2 * nv:3 * nv], out[3 * nv:4 * nv], out[4 * nv]


MATRICES = (("w_in", 1), ("gdn_conv_w", 1), ("w_out", 0), ("w_ffn_gate", 1), ("w_ffn_up", 1), ("w_ffn_down", 0))
TRANSPOSED = ("w_in", "w_ffn_gate", "w_ffn_up")
WEIGHTS = ("norm1_w", "w_in", "gdn_conv_w", "gdn_A_log", "gdn_dt_bias", "gdn_out_norm_w", "fox_f_bias", "fox_q_norm_w",
           "fox_k_norm_w", "w_out", "norm2_w", "w_ffn_gate", "w_ffn_up", "w_ffn_down", "final_norm_w")


def _join(blocks, axis):
    _, r, c = blocks.shape
    if axis == 0:
        return blocks.reshape(N_DEV * r, c)
    return blocks.transpose(1, 0, 2).reshape(r, N_DEV * c)


def _cut(full, axis):
    r, c = full.shape
    if axis == 0:
        return full.reshape(N_DEV, r // N_DEV, c)
    return full.reshape(r, N_DEV, c // N_DEV).transpose(1, 0, 2)


def kernel(x, norm1_w, w_in, gdn_conv_w, gdn_A_log, gdn_dt_bias, gdn_out_norm_w, fox_f_bias, fox_q_norm_w, fox_k_norm_w, w_out, norm2_w, w_ffn_gate, w_ffn_up, w_ffn_down, final_norm_w, loss_target, m_norm1_w, m_w_in, m_gdn_conv_w, m_gdn_A_log, m_gdn_dt_bias, m_gdn_out_norm_w, m_fox_f_bias, m_fox_q_norm_w, m_fox_k_norm_w, m_w_out, m_norm2_w, m_w_ffn_gate, m_w_ffn_up, m_w_ffn_down, m_final_norm_w, v_norm1_w, v_w_in, v_gdn_conv_w, v_gdn_A_log, v_gdn_dt_bias, v_gdn_out_norm_w, v_fox_f_bias, v_fox_q_norm_w, v_fox_k_norm_w, v_w_out, v_norm2_w, v_w_ffn_gate, v_w_ffn_up, v_w_ffn_down, v_final_norm_w):
    w = dict(norm1_w=norm1_w, w_in=w_in, gdn_conv_w=gdn_conv_w, gdn_A_log=gdn_A_log, gdn_dt_bias=gdn_dt_bias,
             gdn_out_norm_w=gdn_out_norm_w, fox_f_bias=fox_f_bias, fox_q_norm_w=fox_q_norm_w, fox_k_norm_w=fox_k_norm_w,
             w_out=w_out, norm2_w=norm2_w, w_ffn_gate=w_ffn_gate, w_ffn_up=w_ffn_up, w_ffn_down=w_ffn_down,
             final_norm_w=final_norm_w)
    m = dict(norm1_w=m_norm1_w, w_in=m_w_in, gdn_conv_w=m_gdn_conv_w, gdn_A_log=m_gdn_A_log, gdn_dt_bias=m_gdn_dt_bias,
             gdn_out_norm_w=m_gdn_out_norm_w, fox_f_bias=m_fox_f_bias, fox_q_norm_w=m_fox_q_norm_w,
             fox_k_norm_w=m_fox_k_norm_w, w_out=m_w_out, norm2_w=m_norm2_w, w_ffn_gate=m_w_ffn_gate,
             w_ffn_up=m_w_ffn_up, w_ffn_down=m_w_ffn_down, final_norm_w=m_final_norm_w)
    v = dict(norm1_w=v_norm1_w, w_in=v_w_in, gdn_conv_w=v_gdn_conv_w, gdn_A_log=v_gdn_A_log, gdn_dt_bias=v_gdn_dt_bias,
             gdn_out_norm_w=v_gdn_out_norm_w, fox_f_bias=v_fox_f_bias, fox_q_norm_w=v_fox_q_norm_w,
             fox_k_norm_w=v_fox_k_norm_w, w_out=v_w_out, norm2_w=v_norm2_w, w_ffn_gate=v_w_ffn_gate,
             w_ffn_up=v_w_ffn_up, w_ffn_down=v_w_ffn_down, final_norm_w=v_final_norm_w)
    late = ("w_out", "w_ffn_gate", "w_ffn_up", "w_ffn_down")
    xs, tgt = x[0], loss_target[0]
    vp = _vector_params({n: w[n] for n in VECTORS})

    def rows_of(d, n):
        return d[n].transpose(0, 2, 1) if n in TRANSPOSED else d[n]

    wr, mr, vr = ({n: rows_of(d, n) for n, _ in MATRICES} for d in (w, m, v))

    w_in_blocks, conv_blocks = _gather_two_level([wr["w_in"][0].astype(WIRE), w["gdn_conv_w"][0]], "gather_in")
    w_t = _join(w_in_blocks, 0)
    conv_w = _join(conv_blocks, 1)
    f = _mixer_forward(xs, vp, w_t, _small_rows(w_t), conv_w, ("gather", [wr[n][0].astype(WIRE) for n in late]))
    full = {n: _join(b, 0) for n, b in zip(late, _spread_wait(f["carried"], "gather", f["mixg"], "gather_late_wait"))}

    (h2, act, dgate, dup, dx3, dx2, dmixg, dmixf, dn2w, dfw, loss) = _tail(
        xs, f["mixg"], f["mixf"], tgt, full["w_out"], vp["n2w"], full["w_ffn_gate"], full["w_ffn_up"],
        full["w_ffn_down"], vp["fw"])
    dlate = _late_grads(f, h2, act, dgate, dup, dx3, dx2)

    grad_x, dw_in, dconv, small, state_grads, state_own = _mixer_backward(
        xs, vp, w_t, _small_rows(w_t), conv_w, f, dx2, dmixg, dmixf, ("scatter", [_cut(dlate[n], 0) for n in late]),
        scatter_own=True)
    vec = _pack_vectors(small["dn1w"], dn2w, dfw, small["gsum"], small["donw"], small["dfnw"], loss)

    state_vec, token = _spread_start([vec], "gather", "vectors_start")
    parts = dict(zip(late, _spread_wait(state_grads, "scatter", token, "grads_late_wait")))
    results = [{}, {}, {}, {}]

    def update(n):
        for d, a in zip(results, _adam_matrix(parts[n], wr[n], mr[n], vr[n], "adam_" + n)):
            d[n] = a.transpose(0, 2, 1) if n in TRANSPOSED else a

    for n in late:
        update(n)
    parts["w_in"], parts["gdn_conv_w"] = _spread_wait(state_own, "scatter", [results[0][n] for n in late], "own_wait")
    (parts_vec,) = _spread_wait(state_vec, "gather", parts["w_in"], "vectors_wait")
    update("w_in")
    update("gdn_conv_w")
    row = lambda a: a.reshape(1, -1)
    *vec_out, total_loss = _adam_vectors(parts_vec, [row(w[n]) for n in VECTORS], [row(m[n]) for n in VECTORS],
                                         [row(v[n]) for n in VECTORS])
    for d, arrs in zip(results, vec_out):
        for n, a in zip(VECTORS, arrs):
            d[n] = a.reshape(w[n].shape)
    return (total_loss[0, 0], grad_x[None], *[d[n] for d in results for n in WEIGHTS])
```

```python
import functools

import jax
import jax.numpy as jnp
from jax import lax
from jax.experimental import pallas as pl
from jax.experimental.pallas import tpu as pltpu

F32 = jnp.float32
MXU = jnp.bfloat16
WIRE = jnp.bfloat16
HI = lax.Precision.HIGHEST
EPS = 1e-6

N_DEV = 8
HEADS = 8
DH = 64
WIDTH = HEADS * DH
CHUNK = 64
LANE = 128
ROW_ALIGN = 16
TB = 256
QB = 256
VMEM_LIMIT = 60 * 1024 * 1024

ADAM_LR = 0.001
ADAM_B1 = 0.9
ADAM_B2 = 0.999
ADAM_EPS = 1e-08
ADAM_WD = 0.01
ADAM_STEP = 10

MESH = pl.DeviceIdType.MESH


def _params(sem=None):
    return pltpu.CompilerParams(dimension_semantics=sem, vmem_limit_bytes=VMEM_LIMIT)


def _resident(shape):
    n = len(shape)
    return pl.BlockSpec(shape, lambda *_: (0,) * n, pipeline_mode=pl.Buffered(1))


def _dot(a, b):
    return jnp.dot(a.astype(MXU), b.astype(MXU), preferred_element_type=F32)


def _dot_nt(a, b):
    return lax.dot_general(a.astype(MXU), b.astype(MXU), (((1,), (1,)), ((), ())), preferred_element_type=F32)


def _dot_tn(a, b):
    return lax.dot_general(a.astype(MXU), b.astype(MXU), (((0,), (0,)), ((), ())), preferred_element_type=F32)


def _hdot(a, b):
    return jnp.dot(a, b, precision=HI, preferred_element_type=F32)


def _hdot_nt(a, b):
    return lax.dot_general(a, b, (((1,), (1,)), ((), ())), precision=HI, preferred_element_type=F32)


def _hdot_tn(a, b):
    return lax.dot_general(a, b, (((0,), (0,)), ((), ())), precision=HI, preferred_element_type=F32)


def _sigmoid(x):
    return 0.5 * jnp.tanh(0.5 * x) + 0.5


def _softplus(x):
    return jnp.maximum(x, 0.0) + jnp.log(1.0 + jnp.exp(-jnp.abs(x)))


def _head_sum_matrix():
    ri = lax.broadcasted_iota(jnp.int32, (LANE, LANE), 0) // DH
    ci = lax.broadcasted_iota(jnp.int32, (LANE, LANE), 1) // DH
    return (ri == ci).astype(F32)


def _group_sum(a, ones_matrix):
    hi = a.astype(jnp.bfloat16)
    lo = (a - hi.astype(F32)).astype(jnp.bfloat16)
    m = ones_matrix.astype(jnp.bfloat16)
    return jnp.dot(hi, m, preferred_element_type=F32) + jnp.dot(lo, m, preferred_element_type=F32)


def _shift_down(x, s):
    return pltpu.roll(x, s, 0)


def _shift_up(x, s):
    return pltpu.roll(x, x.shape[0] - s, 0)


ROWS_A = 4 * WIDTH
ROWS_B = ROWS_A + 2 * HEADS
ROWS_C = ROWS_B + 4 * WIDTH


def _small_rows(w_t):
    return jnp.concatenate([w_t[ROWS_A:ROWS_B], w_t[ROWS_C:], jnp.zeros((LANE - 3 * HEADS, w_t.shape[1]), w_t.dtype)])


def _inproj(x, n1w, w_t, ws_t):
    t, d = x.shape
    tb = min(TB, t)

    def body(x_ref, nw_ref, wt_ref, ws_ref, h_ref, pm_ref, ps_ref):
        xv = x_ref[...]
        r = lax.rsqrt(jnp.mean(xv * xv, axis=-1, keepdims=True) + EPS)
        h = (xv * r * nw_ref[...]).astype(MXU)
        h_ref[...] = h
        pm_ref[:, 0:ROWS_A] = _dot_nt(h, wt_ref[0:ROWS_A, :])
        pm_ref[:, ROWS_A:2 * ROWS_A] = _dot_nt(h, wt_ref[ROWS_B:ROWS_C, :])
        ps_ref[...] = _dot_nt(h, ws_ref[...])

    return pl.pallas_call(
        body, name="inproj", grid=(t // tb,),
        in_specs=[pl.BlockSpec((tb, d), lambda i: (i, 0)), _resident((1, d)), _resident(w_t.shape), _resident((LANE, d))],
        out_specs=[pl.BlockSpec((tb, d), lambda i: (i, 0)), pl.BlockSpec((tb, 2 * ROWS_A), lambda i: (i, 0)),
                   pl.BlockSpec((tb, LANE), lambda i: (i, 0))],
        out_shape=[jax.ShapeDtypeStruct((t, d), MXU), jax.ShapeDtypeStruct((t, 2 * ROWS_A), F32),
                   jax.ShapeDtypeStruct((t, LANE), F32)],
        compiler_params=_params(("arbitrary",)),
    )(x, n1w, w_t, ws_t)


def _inproj_bwd(x, n1w, dx2, dgdn, dz, dfox, dfg, dps, w_t, ws_t, carry=None):
    t, d = x.shape
    tb = min(TB, t)
    w3 = 3 * WIDTH
    c_in, c_in_specs, c_out_shape, c_out_specs, c_sem = _carry_operands(*carry) if carry else ([], [], [], [], None)
    n_c = len(c_in)

    def body(*refs):
        x_ref, nw_ref, dx2_ref, dgdn_ref, dz_ref, dfox_ref, dfg_ref, dps_ref, wm_ref, ws_ref = refs[:10]
        gx_ref, dnw_ref = refs[10 + n_c:12 + n_c]
        if carry:
            _carry_step(carry[0], refs[10:10 + n_c], refs[12 + n_c], refs[13 + n_c], refs[-1], pl.program_id(0), t // tb)
        dh = _dot(dgdn_ref[...], wm_ref[0:w3, :])
        dh += _dot(dz_ref[...], wm_ref[w3:ROWS_A, :])
        dh += _dot(dfox_ref[...], wm_ref[ROWS_B:ROWS_B + w3, :])
        dh += _dot(dfg_ref[...], wm_ref[ROWS_B + w3:ROWS_C, :])
        dh += _dot(dps_ref[...], ws_ref[...])
        xv = x_ref[...]
        r = lax.rsqrt(jnp.mean(xv * xv, axis=-1, keepdims=True) + EPS)
        xn = xv * r

        @pl.when(pl.program_id(0) == 0)
        def _():
            dnw_ref[...] = jnp.zeros_like(dnw_ref)

        dnw_ref[...] += jnp.sum(dh * xn, axis=0, keepdims=True)
        g = dh * nw_ref[...]
        gx_ref[...] = dx2_ref[...] + r * (g - xn * jnp.mean(g * xn, axis=-1, keepdims=True))

    def tok(n):
        return pl.BlockSpec((tb, n), lambda i: (i, 0))

    out = pl.pallas_call(
        body, name="inproj_bwd", grid=(t // tb,),
        in_specs=[tok(d), _resident((1, d)), tok(d), tok(w3), tok(WIDTH), tok(w3), tok(WIDTH), tok(LANE),
                  _resident(w_t.shape), _resident(ws_t.shape)] + c_in_specs,
        out_specs=[tok(d), pl.BlockSpec((1, d), lambda i: (0, 0))] + c_out_specs,
        out_shape=[jax.ShapeDtypeStruct((t, d), F32), jax.ShapeDtypeStruct((1, d), F32)] + c_out_shape,
        input_output_aliases={10 + j: 4 + j for j in range(n_c)},
        scratch_shapes=[c_sem] if carry else [],
        compiler_params=_params(("arbitrary",)),
    )(x, n1w, dx2, dgdn, dz, dfox, dfg, dps, w_t, ws_t, *c_in)
    return (out[0], out[1], _carry_state(out[2:])) if carry else tuple(out)


def _gate_lanes(shape):
    lane = lax.broadcasted_iota(jnp.int32, shape, 1)
    return lane < HEADS, (lane >= HEADS) & (lane < 2 * HEADS), (lane >= 2 * HEADS) & (lane < 3 * HEADS)


def _block_masks():
    ri = lax.broadcasted_iota(jnp.int32, (LANE, LANE), 0)
    ci = lax.broadcasted_iota(jnp.int32, (LANE, LANE), 1)
    same = (ri // CHUNK) == (ci // CHUNK)
    return ((ri >= ci).astype(F32), (ri <= ci).astype(F32), (same & (ri >= ci)).astype(F32),
            (same & (ri <= ci)).astype(F32), same.astype(F32))


def _gates(ps, gparams):
    t = ps.shape[0]
    nb = t // LANE

    def body(ps_ref, gp_ref, out_ref, run_ref, tot_ref, runt_ref):
        p = ps_ref[...]
        is_b, is_a, is_f = _gate_lanes(p.shape)
        z = p + gp_ref[0:1, :]
        neg_exp_a = -jnp.exp(gp_ref[1:2, :])
        glog = neg_exp_a * _softplus(z)
        logf = -_softplus(-z)
        out_ref[...] = jnp.where(is_b, _sigmoid(p), jnp.where(is_a, glog, jnp.where(is_f, logf, 0.0)))
        tril, _, tril_c, _, same_c = _block_masks()
        off = jnp.zeros((1, LANE), F32)
        for b in range(nb):
            rows = slice(b * LANE, (b + 1) * LANE)
            blk = out_ref[rows, :]
            ga = jnp.where(is_a[:LANE], blk, 0.0)
            fb = _hdot(tril, jnp.where(is_f[:LANE], blk, 0.0)) + off
            run = fb + _hdot(tril_c, ga)
            run_ref[rows, :] = run
            runt_ref[:, rows] = run.T
            tot_ref[rows, :] = _hdot(same_c, ga)
            off = fb[LANE - 1:LANE, :]

    return pl.pallas_call(
        body, name="gates",
        out_shape=[jax.ShapeDtypeStruct((t, LANE), F32)] * 3 + [jax.ShapeDtypeStruct((LANE, t), F32)],
        compiler_params=_params(),
    )(ps, gparams)


def _gates_bwd(ps, gparams, dcol_g, dtot_g, dcol_f, drow_g, drow_f):
    t = ps.shape[0]
    nb = t // LANE

    def body(ps_ref, gp_ref, dcg_ref, dtg_ref, dcf_ref, drg_ref, drf_ref, dps_ref, sums_ref, dl_ref, d0_ref, tr_ref):
        p = ps_ref[...]
        is_b, is_a, is_f = _gate_lanes(p.shape)
        _, triu, _, triu_c, same_c = _block_masks()
        tr_ref[...] = jnp.zeros_like(tr_ref)
        off = jnp.zeros((1, LANE), F32)
        for b in reversed(range(nb)):
            rows = slice(b * LANE, (b + 1) * LANE)
            tr_ref[HEADS:2 * HEADS, :] = drg_ref[:, rows]
            tr_ref[2 * HEADS:3 * HEADS, :] = drf_ref[:, rows]
            d = dcg_ref[rows, :] + dcf_ref[rows, :] + tr_ref[...].T
            d0_ref[rows, :] = d
            dlf = _hdot(triu, jnp.where(is_f[:LANE], d, 0.0)) + off
            dla = (_hdot(triu_c, jnp.where(is_a[:LANE], d, 0.0))
                   + _hdot(same_c, jnp.where(is_a[:LANE], dtg_ref[rows, :], 0.0)))
            dl_ref[rows, :] = dlf + dla
            off = dlf[0:1, :]
        z = p + gp_ref[0:1, :]
        neg_exp_a = -jnp.exp(gp_ref[1:2, :])
        sb = _sigmoid(p)
        glog = neg_exp_a * _softplus(z)
        dl = dl_ref[...]
        dp = jnp.where(is_b, d0_ref[...] * sb * (1.0 - sb),
                       jnp.where(is_a, dl * neg_exp_a * _sigmoid(z), jnp.where(is_f, dl * _sigmoid(-z), 0.0)))
        dps_ref[...] = dp
        s_a = jnp.sum(jnp.where(is_a, dl * glog, 0.0), axis=0, keepdims=True)
        s_p = jnp.sum(jnp.where(is_b, 0.0, dp), axis=0, keepdims=True)
        row = lax.broadcasted_iota(jnp.int32, (8, LANE), 0)
        from_a = pltpu.roll(jnp.where(row == 0, s_a, jnp.where(row == 1, s_p, 0.0)), LANE - HEADS, 1)
        from_f = pltpu.roll(jnp.where(row == 2, s_p, 0.0), LANE - 2 * HEADS, 1)
        lane = lax.broadcasted_iota(jnp.int32, (8, LANE), 1)
        sums_ref[...] = jnp.where(lane < HEADS, from_a + from_f, 0.0)

    return pl.pallas_call(
        body, name="gates_bwd",
        out_shape=[jax.ShapeDtypeStruct((t, LANE), F32), jax.ShapeDtypeStruct((8, LANE), F32)],
        scratch_shapes=[pltpu.VMEM((t, LANE), F32), pltpu.VMEM((t, LANE), F32), pltpu.VMEM((LANE, LANE), F32)],
        compiler_params=_params(),
    )(ps, gparams, dcol_g, dtot_g, dcol_f, drow_g, drow_f)


def _conv(xv, w):
    acc = w[3:4, :] * xv
    for s in range(1, 4):
        acc += w[3 - s:4 - s, :] * _shift_down(xv, s)
    return acc


PREP_ROWS = 256
HALO = 8
PREP_UNROLL = 2


def _tile_loop(n, tile, init):
    if n % PREP_UNROLL:
        return lax.fori_loop(0, n, tile, init)

    def trip(g, carry):
        for u in range(PREP_UNROLL):
            carry = tile(g * PREP_UNROLL + u, carry)
        return carry

    return lax.fori_loop(0, n // PREP_UNROLL, trip, init)


def _tile_rows(r):
    return pl.ds(pl.multiple_of(r * PREP_ROWS, PREP_ROWS), PREP_ROWS)


def _gdn_prep(pm, conv_w):
    t = pm.shape[0]
    nj = WIDTH // LANE
    win = PREP_ROWS + HALO

    def body(x_ref, w_ref, o_ref, xp_ref):
        kind = pl.program_id(0)
        xp_ref[0:HALO, :] = jnp.zeros((HALO, LANE), F32)
        xp_ref[HALO:, :] = x_ref[...]
        w = w_ref[...]
        hs = _head_sum_matrix()

        def tile(r, _):
            xw = xp_ref[pl.ds(pl.multiple_of(r * PREP_ROWS, PREP_ROWS), win), :]
            acc = _conv(xw, w)[HALO:]
            y = acc * _sigmoid(acc)
            out = jnp.where(kind < 2, y * lax.rsqrt(_group_sum(y * y, hs) + EPS), y)
            o_ref[0, 0, _tile_rows(r), :] = out[:, :DH]
            o_ref[0, 1, _tile_rows(r), :] = out[:, DH:]
            return 0

        _tile_loop(t // PREP_ROWS, tile, 0)

    return pl.pallas_call(
        body, name="gdn_prep", grid=(3, nj),
        in_specs=[pl.BlockSpec((t, LANE), lambda i, j: (0, i * nj + j)),
                  pl.BlockSpec((4, LANE), lambda i, j: (0, i * nj + j))],
        out_specs=pl.BlockSpec((1, 2, t, DH), lambda i, j: (i, j, 0, 0)),
        out_shape=jax.ShapeDtypeStruct((3, HEADS, t, DH), F32),
        scratch_shapes=[pltpu.VMEM((t + HALO, LANE), F32)],
        compiler_params=_params(("arbitrary", "arbitrary")),
    )(pm, conv_w)


def _gdn_prep_bwd(pm, conv_w, dqkv):
    t = pm.shape[0]
    nj = WIDTH // LANE
    win = PREP_ROWS + 2 * HALO

    def body(x_ref, w_ref, d_ref, dx_ref, dw_ref, xp_ref, dp_ref):
        kind = pl.program_id(0)
        zeros = jnp.zeros((HALO, LANE), F32)
        for ref in (xp_ref, dp_ref):
            ref[0:HALO, :] = zeros
            ref[HALO + t:, :] = zeros
        xp_ref[HALO:HALO + t, :] = x_ref[...]
        dp_ref[HALO:HALO + t, 0:DH] = d_ref[0, 0]
        dp_ref[HALO:HALO + t, DH:] = d_ref[0, 1]
        w = w_ref[...]
        hs = _head_sum_matrix()
        rows = lax.broadcasted_iota(jnp.int32, (win, LANE), 0)
        in_tile = (rows >= HALO) & (rows < HALO + PREP_ROWS)

        def tile(r, dw):
            start = pl.multiple_of(r * PREP_ROWS, PREP_ROWS)
            xw = xp_ref[pl.ds(start, win), :]
            d = dp_ref[pl.ds(start, win), :]
            acc = _conv(xw, w)
            sg = _sigmoid(acc)
            y = acc * sg
            rn = lax.rsqrt(_group_sum(y * y, hs) + EPS)
            yn = y * rn
            dy = jnp.where(kind < 2, rn * (d - yn * _group_sum(d * yn, hs)), d)
            dacc = dy * sg * (1.0 + acc * (1.0 - sg))
            dx = w[3:4, :] * dacc
            for s in range(1, 4):
                dx += w[3 - s:4 - s, :] * _shift_up(dacc, s)
            dx_ref[_tile_rows(r), :] = dx[HALO:HALO + PREP_ROWS]
            dm = jnp.where(in_tile, dacc, 0.0)
            return tuple(dw[i] + jnp.sum(dm * (xw if i == 3 else _shift_down(xw, 3 - i)), axis=0, keepdims=True)
                         for i in range(4))

        dw = _tile_loop(t // PREP_ROWS, tile, tuple(jnp.zeros((1, LANE), F32) for _ in range(4)))
        for i in range(4):
            dw_ref[i:i + 1, :] = dw[i]

    return pl.pallas_call(
        body, name="gdn_prep_bwd", grid=(3, nj),
        in_specs=[pl.BlockSpec((t, LANE), lambda i, j: (0, i * nj + j)),
                  pl.BlockSpec((4, LANE), lambda i, j: (0, i * nj + j)),
                  pl.BlockSpec((1, 2, t, DH), lambda i, j: (i, j, 0, 0))],
        out_specs=[pl.BlockSpec((t, LANE), lambda i, j: (0, i * nj + j)),
                   pl.BlockSpec((4, LANE), lambda i, j: (0, i * nj + j))],
        out_shape=[jax.ShapeDtypeStruct((t, 3 * WIDTH), F32), jax.ShapeDtypeStruct((4, 3 * WIDTH), F32)],
        scratch_shapes=[pltpu.VMEM((t + 2 * HALO, LANE), F32), pltpu.VMEM((t + 2 * HALO, LANE), F32)],
        compiler_params=_params(("arbitrary", "arbitrary")),
    )(pm, conv_w, dqkv)


FOX_COL0 = 4 * WIDTH // LANE


def _fox_prep(pm, nw):
    t = pm.shape[0]
    nj = WIDTH // LANE

    def body(x_ref, w_ref, o_ref):
        kind = pl.program_id(0)
        hs = _head_sum_matrix()
        wk = w_ref[pl.ds(kind, 1), :]

        def tile(r, _):
            xv = x_ref[_tile_rows(r), :]
            ms = _group_sum(xv * xv, hs) * (1.0 / DH)
            out = jnp.where(kind < 2, xv * lax.rsqrt(ms + EPS) * wk, xv)
            o_ref[0, 0, _tile_rows(r), :] = out[:, :DH]
            o_ref[0, 1, _tile_rows(r), :] = out[:, DH:]
            return 0

        _tile_loop(t // PREP_ROWS, tile, 0)

    return pl.pallas_call(
        body, name="fox_prep", grid=(3, nj),
        in_specs=[pl.BlockSpec((t, LANE), lambda i, j: (0, FOX_COL0 + i * nj + j)),
                  pl.BlockSpec((3, LANE), lambda i, j: (0, 0))],
        out_specs=pl.BlockSpec((1, 2, t, DH), lambda i, j: (i, j, 0, 0)),
        out_shape=jax.ShapeDtypeStruct((3, HEADS, t, DH), F32),
        compiler_params=_params(("arbitrary", "arbitrary")),
    )(pm, nw)


def _fox_prep_bwd(pm, nw, dqkv):
    t = pm.shape[0]
    nj = WIDTH // LANE

    def body(x_ref, w_ref, d_ref, dx_ref, dw_ref):
        kind = pl.program_id(0)
        hs = _head_sum_matrix()
        wk = w_ref[pl.ds(kind, 1), :]

        def tile(r, dw):
            xv = x_ref[_tile_rows(r), :]
            rn = lax.rsqrt(_group_sum(xv * xv, hs) * (1.0 / DH) + EPS)
            xn = xv * rn
            d = jnp.concatenate([d_ref[0, 0, _tile_rows(r), :], d_ref[0, 1, _tile_rows(r), :]], axis=1)
            g = d * wk
            dxn = rn * (g - xn * _group_sum(g * xn, hs) * (1.0 / DH))
            dx_ref[_tile_rows(r), :] = jnp.where(kind < 2, dxn, d)
            return dw + jnp.sum(d * xn, axis=0, keepdims=True)

        dw_ref[0, 0] = _tile_loop(t // PREP_ROWS, tile, jnp.zeros((1, LANE), F32))

    return pl.pallas_call(
        body, name="fox_prep_bwd", grid=(3, nj),
        in_specs=[pl.BlockSpec((t, LANE), lambda i, j: (0, FOX_COL0 + i * nj + j)),
                  pl.BlockSpec((3, LANE), lambda i, j: (0, 0)),
                  pl.BlockSpec((1, 2, t, DH), lambda i, j: (i, j, 0, 0))],
        out_specs=[pl.BlockSpec((t, LANE), lambda i, j: (0, i * nj + j)),
                   pl.BlockSpec((1, 1, 1, LANE), lambda i, j: (i, j, 0, 0))],
        out_shape=[jax.ShapeDtypeStruct((t, 3 * WIDTH), F32), jax.ShapeDtypeStruct((3, nj, 1, LANE), F32)],
        compiler_params=_params(("arbitrary", "arbitrary")),
    )(pm, nw, dqkv)


SC = 256
CPS = SC // CHUNK
GDN_HP = 2
Q_SCALE = DH ** -0.5


def _sc_masks():
    ri = lax.broadcasted_iota(jnp.int32, (SC, SC), 0)
    ci = lax.broadcasted_iota(jnp.int32, (SC, SC), 1)
    same = (ri // CHUNK) == (ci // CHUNK)
    return same & (ri >= ci), same & (ri > ci), ri == ci


def _unit_lower_inverses(ms, eye):
    invs = [jnp.where(eye, 1.0, 0.0) + m for m in ms]
    for _ in range(5):
        ms = [_dot(m, m) for m in ms]
        invs = [inv + _dot(inv, m) for inv, m in zip(invs, ms)]
    return invs


def _lane_col(blk, lane_idx):
    lane = lax.broadcasted_iota(jnp.int32, blk.shape, 1)
    return jnp.sum(jnp.where(lane == lane_idx, blk, 0.0), axis=1, keepdims=True)


def _to_lane(col, lane_idx):
    lane = lax.broadcasted_iota(jnp.int32, (col.shape[0], LANE), 1)
    return jnp.where(lane == lane_idx, col, 0.0)


def _gdn_columns(gates_ref, run_ref, tot_ref, runt_ref, rows, h):
    return (_lane_col(gates_ref[rows, :], h), _lane_col(run_ref[rows, :], HEADS + h),
            _lane_col(tot_ref[rows, :], HEADS + h), runt_ref[pl.ds(h, 1), rows])


def _gdn_local(q, k, beta, gc, gl, grow, causal):
    decay = jnp.exp(jnp.where(causal, gc - grow, -1e30))
    egc = jnp.exp(gc)
    ekd = jnp.exp(gl - gc)
    qs = q * Q_SCALE
    kb = k * beta
    kk = _dot_nt(kb, k)
    qk = _dot_nt(qs, k)
    return beta, gl, decay, egc, ekd, qs, kb, kk, qk, jnp.where(causal, qk * decay, 0.0)


def _chunk_rows(c):
    return pl.ds(pl.multiple_of(c * CHUNK, CHUNK), CHUNK)


def _sc_rows(b):
    return pl.ds(pl.multiple_of(b * SC, SC), SC)


def _gdn_fwd(qkv, gates, run, tot, run_t):
    t = qkv.shape[2]
    nc = t // CHUNK
    nsc = t // SC

    hp_n = GDN_HP
    heads = range(hp_n)

    def body(qkv_ref, gates_ref, run_ref, tot_ref, runt_ref, o_ref, st_ref, inv_ref, kc_s, qc_s, g_s, au_s):
        hp = pl.program_id(0)
        causal, strict, eye = _sc_masks()

        def local(b, _):
            rows = _sc_rows(b)
            loc = [_gdn_local(qkv_ref[0, hh, rows, :], qkv_ref[1, hh, rows, :],
                              *_gdn_columns(gates_ref, run_ref, tot_ref, runt_ref, rows, hp * hp_n + hh), causal)
                   for hh in heads]
            invs = _unit_lower_inverses([-jnp.where(strict, l[7] * l[2], 0.0) for l in loc], eye)
            us, ws = [], []
            for hh in heads:
                beta, _, _, egc, _, _, kb, _, _, _ = loc[hh]
                inv_ref[hh, rows, :] = invs[hh].astype(inv_ref.dtype)
                us.append(_dot(invs[hh], qkv_ref[2, hh, rows, :] * beta))
                ws.append(_dot(invs[hh], kb * egc))
            for hh in heads:
                _, _, _, egc, ekd, qs, _, _, _, attn = loc[hh]
                g_s[hh, rows, :] = qs * egc - _dot(attn, ws[hh])
                au_s[hh, rows, :] = _dot(attn, us[hh])
                kd = qkv_ref[1, hh, rows, :] * ekd
                for j in range(CPS):
                    sl = slice(j * CHUNK, (j + 1) * CHUNK)
                    kc_s[hh, b * CPS + j] = _dot_tn(kd[sl], ws[hh][sl])
                    qc_s[hh, b * CPS + j] = _dot_tn(kd[sl], us[hh][sl])
            return 0

        lax.fori_loop(0, nsc, local, 0)

        def step(c, states):
            rows = _chunk_rows(c)
            tot_row = tot_ref[pl.ds(c * CHUNK, 1), :]
            new = []
            for hh in heads:
                s = states[hh]
                st_ref[hh, c] = s
                o_ref[hh, rows, :] = _dot(g_s[hh, rows, :], s) + au_s[hh, rows, :]
                egl = jnp.exp(_lane_col(tot_row, HEADS + hp * hp_n + hh))
                new.append(egl * s - _dot(kc_s[hh, c], s) + qc_s[hh, c])
            return tuple(new)

        lax.fori_loop(0, nc, step, tuple(jnp.zeros((DH, DH), F32) for _ in heads))

    whole = pl.BlockSpec((t, LANE), lambda h: (0, 0))
    return pl.pallas_call(
        body, name="gdn_fwd", grid=(HEADS // hp_n,),
        in_specs=[pl.BlockSpec((3, hp_n, t, DH), lambda h: (0, h, 0, 0)), whole, whole, whole,
                  pl.BlockSpec((HEADS, t), lambda h: (1, 0))],
        out_specs=[pl.BlockSpec((hp_n, t, DH), lambda h: (h, 0, 0)), pl.BlockSpec((hp_n, nc, DH, DH), lambda h: (h, 0, 0, 0)),
                   pl.BlockSpec((hp_n, t, SC), lambda h: (h, 0, 0))],
        out_shape=[jax.ShapeDtypeStruct((HEADS, t, DH), F32), jax.ShapeDtypeStruct((HEADS, nc, DH, DH), F32),
                   jax.ShapeDtypeStruct((HEADS, t, SC), MXU)],
        scratch_shapes=[pltpu.VMEM((hp_n, nc, DH, DH), F32), pltpu.VMEM((hp_n, nc, DH, DH), F32),
                        pltpu.VMEM((hp_n, t, DH), F32), pltpu.VMEM((hp_n, t, DH), F32)],
        compiler_params=_params(("arbitrary",)),
    )(qkv, gates, run, tot, run_t)


def _gdn_bwd(qkv, gates, run, tot, run_t, inv, states, do):
    t = qkv.shape[2]
    nc = t // CHUNK
    nsc = t // SC

    hp_n = GDN_HP
    heads = range(hp_n)

    def body(qkv_ref, gates_ref, run_ref, tot_ref, runt_ref, inv_ref, st_ref, do_ref,
             dqkv_ref, dcol_ref, dtot_ref, drow_ref, u_s, w_s, kc_s, h_s, dsn_s):
        hp = pl.program_id(0)
        causal, strict, _ = _sc_masks()

        @pl.when(hp == 0)
        def _():
            dcol_ref[...] = jnp.zeros_like(dcol_ref)
            dtot_ref[...] = jnp.zeros_like(dtot_ref)

        def local_of(hh, rows):
            return _gdn_local(qkv_ref[0, hh, rows, :], qkv_ref[1, hh, rows, :],
                              *_gdn_columns(gates_ref, run_ref, tot_ref, runt_ref, rows, hp * hp_n + hh), causal)

        def local(b, _):
            rows = _sc_rows(b)
            loc = [local_of(hh, rows) for hh in heads]
            us, ws = [], []
            for hh in heads:
                beta, _, _, egc, _, _, kb, _, _, _ = loc[hh]
                inv_b = inv_ref[hh, rows, :]
                us.append(_dot(inv_b, qkv_ref[2, hh, rows, :] * beta))
                ws.append(_dot(inv_b, kb * egc))
            gs = [loc[hh][5] * loc[hh][3] - _dot(loc[hh][9], ws[hh]) for hh in heads]
            for hh in heads:
                u_s[hh, rows, :] = us[hh]
                w_s[hh, rows, :] = ws[hh]
                kd = qkv_ref[1, hh, rows, :] * loc[hh][4]
                dout = do_ref[hh, rows, :]
                for j in range(CPS):
                    sl = slice(j * CHUNK, (j + 1) * CHUNK)
                    kc_s[hh, b * CPS + j] = _dot_tn(kd[sl], ws[hh][sl])
                    h_s[hh, b * CPS + j] = _dot_tn(gs[hh][sl], dout[sl])
            return 0

        lax.fori_loop(0, nsc, local, 0)

        def step(i, dss):
            c = nc - 1 - i
            tot_row = tot_ref[pl.ds(c * CHUNK, 1), :]
            new = []
            for hh in heads:
                ds = dss[hh]
                dsn_s[hh, c] = ds
                egl = jnp.exp(_lane_col(tot_row, HEADS + hp * hp_n + hh))
                new.append(egl * ds - _dot_tn(kc_s[hh, c], ds) + h_s[hh, c])
            return tuple(new)

        lax.fori_loop(0, nc, step, tuple(jnp.zeros((DH, DH), F32) for _ in heads))

        def back(b, _):
            rows = _sc_rows(b)
            first = lax.broadcasted_iota(jnp.int32, (CHUNK, 1), 0) == 0
            loc = [local_of(hh, rows) for hh in heads]
            mid = []
            for hh in heads:
                beta, gl, decay, egc, ekd, qs, kb, kk, qk, attn = loc[hh]
                u, w = u_s[hh, rows, :], w_s[hh, rows, :]
                kd = qkv_ref[1, hh, rows, :] * ekd
                dout = do_ref[hh, rows, :]
                dg_p, dkd_p, dw_p, du_p, dgl_p = [], [], [], [], []
                for j in range(CPS):
                    sl = slice(j * CHUNK, (j + 1) * CHUNK)
                    s = st_ref[hh, b * CPS + j]
                    dsn = dsn_s[hh, b * CPS + j]
                    dkc = -_dot_nt(dsn, s)
                    dg_p.append(_dot_nt(dout[sl], s))
                    dkd_p.append(_dot_nt(w[sl], dkc) + _dot_nt(u[sl], dsn))
                    dw_p.append(_dot(kd[sl], dkc))
                    du_p.append(_dot(kd[sl], dsn))
                    degl = jnp.sum(jnp.sum(s * dsn, axis=1, keepdims=True), axis=0, keepdims=True)
                    dgl_p.append(jnp.where(first, degl * jnp.exp(gl[j * CHUNK:j * CHUNK + 1, :]), 0.0))
                dg, dkd = jnp.concatenate(dg_p, axis=0), jnp.concatenate(dkd_p, axis=0)
                da = jnp.where(causal, _dot_nt(dout, u) - _dot_nt(dg, w), 0.0)
                at = _dot_tn(attn, jnp.concatenate([dout, dg], axis=1))
                du = at[:, :DH] + jnp.concatenate(du_p, axis=0)
                dw = jnp.concatenate(dw_p, axis=0) - at[:, DH:]
                mid.append((kd, dg, dkd, da, du, dw, jnp.concatenate(dgl_p, axis=0)))
            inv_ts = [inv_ref[hh, rows, :].astype(F32).T for hh in heads]
            its = [_dot(inv_ts[hh], jnp.concatenate([mid[hh][4], mid[hh][5]], axis=1)) for hh in heads]
            dinvs = []
            for hh in heads:
                beta, _, _, egc, _, _, kb, _, _, _ = loc[hh]
                dinvs.append(_dot_nt(mid[hh][4], qkv_ref[2, hh, rows, :] * beta) + _dot_nt(mid[hh][5], kb * egc))
            half = [_dot(inv_ts[hh], dinvs[hh]) for hh in heads]
            dls = [jnp.where(strict, -_dot(half[hh], inv_ts[hh]), 0.0) for hh in heads]
            for hh in heads:
                head = hp * hp_n + hh
                beta, gl, decay, egc, ekd, qs, kb, kk, qk, attn = loc[hh]
                kd, dg, dkd, da, du, dw, dgl_first = mid[hh]
                k, v = qkv_ref[1, hh, rows, :], qkv_ref[2, hh, rows, :]
                dvb, dkbe = its[hh][:, :DH], its[hh][:, DH:]
                dl = dls[hh]
                dlogd = (dl * kk + da * qk) * decay
                dkk = dl * decay
                dqk = da * decay
                dkb = _dot(dkk, k) + dkbe * egc
                dqs = _dot(dqk, k) + dg * egc
                dk = (_dot_tn(jnp.concatenate([dkk, dqk], axis=0), jnp.concatenate([kb, qs], axis=0))
                      + dkd * ekd + dkb * beta)
                dkd_kd = jnp.sum(dkd * kd, axis=1, keepdims=True)
                dgc = (jnp.sum(dlogd, axis=1, keepdims=True) + jnp.sum(dg * qs, axis=1, keepdims=True) * egc
                       + jnp.sum(dkbe * (kb * egc), axis=1, keepdims=True) - dkd_kd)
                dbeta = jnp.sum(dkb * k, axis=1, keepdims=True) + jnp.sum(dvb * v, axis=1, keepdims=True)
                dqkv_ref[0, hh, rows, :] = dqs * Q_SCALE
                dqkv_ref[1, hh, rows, :] = dk
                dqkv_ref[2, hh, rows, :] = dvb * beta
                dcol_ref[rows, :] += _to_lane(dbeta, head) + _to_lane(dgc, HEADS + head)
                dtot_ref[rows, :] += _to_lane(dkd_kd + dgl_first, HEADS + head)
                drow_ref[pl.ds(head, 1), rows] = -jnp.sum(dlogd, axis=0, keepdims=True)
            return 0

        lax.fori_loop(0, nsc, back, 0)

    whole = pl.BlockSpec((t, LANE), lambda h: (0, 0))
    rowspec = pl.BlockSpec((HEADS, t), lambda h: (0, 0))
    sq = pltpu.VMEM((hp_n, nc, DH, DH), F32)
    per_head = pltpu.VMEM((hp_n, t, DH), F32)
    return pl.pallas_call(
        body, name="gdn_bwd", grid=(HEADS // hp_n,),
        in_specs=[pl.BlockSpec((3, hp_n, t, DH), lambda h: (0, h, 0, 0)), whole, whole, whole,
                  pl.BlockSpec((HEADS, t), lambda h: (1, 0)),
                  pl.BlockSpec((hp_n, t, SC), lambda h: (h, 0, 0)), pl.BlockSpec((hp_n, nc, DH, DH), lambda h: (h, 0, 0, 0)),
                  pl.BlockSpec((hp_n, t, DH), lambda h: (h, 0, 0))],
        out_specs=[pl.BlockSpec((3, hp_n, t, DH), lambda h: (0, h, 0, 0)), whole, whole, rowspec],
        out_shape=[jax.ShapeDtypeStruct((3, HEADS, t, DH), F32), jax.ShapeDtypeStruct((t, LANE), F32),
                   jax.ShapeDtypeStruct((t, LANE), F32), jax.ShapeDtypeStruct((HEADS, t), F32)],
        scratch_shapes=[per_head, per_head, sq, sq, sq],
        compiler_params=_params(("arbitrary",)),
    )(qkv, gates, run, tot, run_t, inv, states, do)


FOX_HP_FWD = 4
FOX_HP_BWD = 2


def _wide_t(a):
    r = a.shape[0]
    return jnp.concatenate([a, jnp.zeros((r, LANE - DH), F32)], axis=1).T[:DH]


def _tall_t(a):
    r = a.shape[1]
    return jnp.concatenate([a, jnp.zeros((LANE - DH, r), F32)], axis=0).T[:, :DH]


def _key_side_f(run_blk, head, qb):
    col = jnp.broadcast_to(_lane_col(run_blk, 2 * HEADS + head), (run_blk.shape[0], LANE))
    return jnp.concatenate([col] * (qb // LANE), axis=1)


def _diag_mask(qb):
    return lax.broadcasted_iota(jnp.int32, (qb, qb), 0) <= lax.broadcasted_iota(jnp.int32, (qb, qb), 1)


def _fox_fwd(qkv, run, run_t, carry=None):
    t = qkv.shape[2]
    qb = min(QB, t)
    nq = t // qb
    hp_n = FOX_HP_FWD
    c_in, c_in_specs, c_out_shape, c_out_specs, c_sem = _carry_operands(*carry) if carry else ([], [], [], [], None)
    n_c = len(c_in)

    def body(*refs):
        q_ref, k_ref, v_ref, run_ref, runt_ref = refs[:5]
        o_ref, lse_ref = refs[5 + n_c:7 + n_c]
        kb_s, vt_s = refs[7 + n_c + len(c_out_shape):9 + n_c + len(c_out_shape)]
        hp = pl.program_id(0)
        i = pl.program_id(1)

        if carry:
            _carry_step(carry[0], refs[5:5 + n_c], refs[7 + n_c], refs[8 + n_c], refs[-1], hp * nq + i,
                        (HEADS // hp_n) * nq)

        @pl.when(i == 0)
        def _():
            for hh in range(hp_n):
                kb_s[hh] = k_ref[0, hh].astype(MXU)
                for b in range(nq):
                    rows = slice(b * qb, (b + 1) * qb)
                    vt_s[hh, :, rows] = _wide_t(v_ref[0, hh, rows, :]).astype(MXU)

        qrows = pl.ds(pl.multiple_of(i * qb, qb), qb)
        qs = [(q_ref[0, hh] * Q_SCALE).astype(MXU) for hh in range(hp_n)]
        fq = [runt_ref[pl.ds(hp * hp_n + hh, 1), qrows] for hh in range(hp_n)]

        def block_rows(j):
            return pl.ds(pl.multiple_of(j * qb, qb), qb)

        def scores(j):
            return tuple(_dot_nt(kb_s[hh, block_rows(j), :], qs[hh]) for hh in range(hp_n))

        def absorb(j, raw, state, diagonal):
            rows = block_rows(j)
            run_blk = run_ref[rows, :]
            stats, pv = [], []
            for hh in range(hp_n):
                m, l, _ = state[hh]
                st = raw[hh] + (fq[hh] - _key_side_f(run_blk, hp * hp_n + hh, qb))
                if diagonal:
                    st = jnp.where(_diag_mask(qb), st, -1e30)
                m_new = jnp.maximum(m, jnp.max(st, axis=0, keepdims=True))
                p = jnp.exp(st - m_new)
                alpha = jnp.exp(m - m_new)
                stats.append((m_new, alpha * l + jnp.sum(p, axis=0, keepdims=True), alpha))
                pv.append(_dot(vt_s[hh, :, rows], p))
            return tuple((stats[hh][0], stats[hh][1], stats[hh][2] * state[hh][2] + pv[hh]) for hh in range(hp_n))

        def kstep(j, carry):
            state, raw = carry
            ahead = scores(j + 1)
            return absorb(j, raw, state, False), ahead

        init = tuple((jnp.full((1, qb), -1e30, F32), jnp.zeros((1, qb), F32), jnp.zeros((DH, qb), F32))
                     for _ in range(hp_n))
        state, raw = lax.fori_loop(0, i, kstep, (init, scores(0)))
        state = absorb(i, raw, state, True)
        for hh in range(hp_n):
            m, l, acc = state[hh]
            o_ref[hh] = _tall_t(acc / l)
            lse_ref[pl.ds(hp * hp_n + hh, 1), qrows] = m + jnp.log(l)

    out = pl.pallas_call(
        body, name="fox_fwd", grid=(HEADS // hp_n, nq),
        in_specs=[pl.BlockSpec((1, hp_n, qb, DH), lambda h, i: (0, h, i, 0)),
                  pl.BlockSpec((1, hp_n, t, DH), lambda h, i: (1, h, 0, 0)),
                  pl.BlockSpec((1, hp_n, t, DH), lambda h, i: (2, h, 0, 0)),
                  pl.BlockSpec((t, LANE), lambda h, i: (0, 0)),
                  pl.BlockSpec((HEADS, t), lambda h, i: (2, 0))] + c_in_specs,
        out_specs=[pl.BlockSpec((hp_n, qb, DH), lambda h, i: (h, i, 0)),
                   pl.BlockSpec((HEADS, t), lambda h, i: (0, 0))] + c_out_specs,
        out_shape=[jax.ShapeDtypeStruct((HEADS, t, DH), F32), jax.ShapeDtypeStruct((HEADS, t), F32)] + c_out_shape,
        input_output_aliases={5 + j: 4 + j for j in range(n_c)},
        scratch_shapes=[pltpu.VMEM((hp_n, t, DH), MXU), pltpu.VMEM((hp_n, DH, t), MXU)] + ([c_sem] if carry else []),
        compiler_params=_params(("arbitrary", "arbitrary")),
    )(qkv, qkv, qkv, run, run_t, *c_in)
    return (out[0], out[1], _carry_state(out[2:])) if carry else (out[0], out[1])


def _fox_bwd(qkv, run, run_t, o, lse, do, carry=None):
    t = qkv.shape[2]
    qb = min(QB, t)
    nq = t // qb
    hp_n = FOX_HP_BWD
    c_in, c_in_specs, c_out_shape, c_out_specs, c_sem = _carry_operands(*carry) if carry else ([], [], [], [], None)
    n_c = len(c_in)

    def body(*refs):
        q_ref, k_ref, v_ref, run_ref, runt_ref, o_ref, lse_ref, do_ref = refs[:8]
        dqkv_ref, dcol_ref, drow_ref = refs[8 + n_c:11 + n_c]
        dqt_s = refs[11 + n_c + len(c_out_shape)]
        hp = pl.program_id(0)
        j = pl.program_id(1)

        if carry:
            _carry_step(carry[0], refs[8:8 + n_c], refs[11 + n_c], refs[12 + n_c], refs[-1], hp * nq + j,
                        (HEADS // hp_n) * nq)

        @pl.when(j == 0)
        def _():
            dqt_s[...] = jnp.zeros_like(dqt_s)

        @pl.when((j == 0) & (hp == 0))
        def _():
            dcol_ref[...] = jnp.zeros_like(dcol_ref)
            drow_ref[...] = jnp.zeros_like(drow_ref)

        krows = pl.ds(pl.multiple_of(j * qb, qb), qb)
        run_blk = run_ref[krows, :]
        ones8 = jnp.ones((8, DH), MXU)
        kb, kt, vb, fk = [], [], [], []
        for hh in range(hp_n):
            kf = k_ref[0, hh]
            kb.append(kf.astype(MXU))
            kt.append(_wide_t(kf).astype(MXU))
            vb.append(v_ref[0, hh].astype(MXU))
            fk.append(_key_side_f(run_blk, hp * hp_n + hh, qb))

        def block_rows(i):
            return pl.ds(pl.multiple_of(i * qb, qb), qb)

        def products(i):
            rows = block_rows(i)
            out = []
            for hh in range(hp_n):
                dout = do_ref[hh, rows, :]
                x = dout * o_ref[hh, rows, :]
                x_hi = x.astype(MXU)
                out.append((_dot_nt(kb[hh], q_ref[0, hh, rows, :] * Q_SCALE), _dot_nt(vb[hh], dout),
                            (_dot_nt(ones8, x_hi) + _dot_nt(ones8, x - x_hi.astype(F32)))[0:1, :]))
            return tuple(out)

        def absorb(i, prods, acc, diagonal):
            rows = block_rows(i)
            pieces = []
            for hh in range(hp_n):
                head = hp * hp_n + hh
                raw, dpt, drow = prods[hh]
                off = runt_ref[pl.ds(head, 1), rows] - lse_ref[pl.ds(head, 1), rows]
                st = raw + (off - fk[hh])
                if diagonal:
                    st = jnp.where(_diag_mask(qb), st, -1e30)
                pt = jnp.exp(st)
                dst = pt * (dpt - drow)
                drow_ref[pl.ds(head, 1), rows] += jnp.sum(dst, axis=0, keepdims=True)
                folded = dst[:, 0:LANE]
                for c in range(1, qb // LANE):
                    folded = folded + dst[:, c * LANE:(c + 1) * LANE]
                pieces.append((_dot(dst, q_ref[0, hh, rows, :] * Q_SCALE), _dot(pt, do_ref[hh, rows, :]),
                               _dot(kt[hh], dst), folded))
            out = []
            for hh in range(hp_n):
                dqt_s[hh, :, rows] += pieces[hh][2]
                out.append((acc[hh][0] + pieces[hh][0], acc[hh][1] + pieces[hh][1], acc[hh][2] + pieces[hh][3]))
            return tuple(out)

        def qstep(i, carry):
            acc, prods = carry
            ahead = products(jnp.minimum(i + 1, nq - 1))
            return absorb(i, prods, acc, False), ahead

        init = tuple((jnp.zeros((qb, DH), F32), jnp.zeros((qb, DH), F32), jnp.zeros((qb, LANE), F32))
                     for _ in range(hp_n))
        first = products(j)
        ahead = products(jnp.minimum(j + 1, nq - 1))
        acc = absorb(j, first, init, True)
        acc, _ = lax.fori_loop(j + 1, nq, qstep, (acc, ahead))
        for hh in range(hp_n):
            dk, dv, ds_sum = acc[hh]
            dqkv_ref[1, hh, krows, :] = dk
            dqkv_ref[2, hh, krows, :] = dv
            dcol_ref[krows, :] += _to_lane(-jnp.sum(ds_sum, axis=1, keepdims=True), 2 * HEADS + hp * hp_n + hh)

        @pl.when(j == nq - 1)
        def _():
            for hh in range(hp_n):
                for b in range(nq):
                    rows = slice(b * qb, (b + 1) * qb)
                    dqkv_ref[0, hh, rows, :] = _tall_t(dqt_s[hh, :, rows]) * Q_SCALE

    full = pl.BlockSpec((hp_n, t, DH), lambda h, j: (h, 0, 0))
    rows8 = pl.BlockSpec((HEADS, t), lambda h, j: (0, 0))
    out = pl.pallas_call(
        body, name="fox_bwd", grid=(HEADS // hp_n, nq),
        in_specs=[pl.BlockSpec((1, hp_n, t, DH), lambda h, j: (0, h, 0, 0)),
                  pl.BlockSpec((1, hp_n, qb, DH), lambda h, j: (1, h, j, 0)),
                  pl.BlockSpec((1, hp_n, qb, DH), lambda h, j: (2, h, j, 0)),
                  pl.BlockSpec((t, LANE), lambda h, j: (0, 0)), pl.BlockSpec((HEADS, t), lambda h, j: (2, 0)),
                  full, rows8, full] + c_in_specs,
        out_specs=[pl.BlockSpec((3, hp_n, t, DH), lambda h, j: (0, h, 0, 0)),
                   pl.BlockSpec((t, LANE), lambda h, j: (0, 0)), rows8] + c_out_specs,
        out_shape=[jax.ShapeDtypeStruct((3, HEADS, t, DH), F32), jax.ShapeDtypeStruct((t, LANE), F32),
                   jax.ShapeDtypeStruct((HEADS, t), F32)] + c_out_shape,
        input_output_aliases={8 + j: 5 + j for j in range(n_c)},
        scratch_shapes=[pltpu.VMEM((hp_n, DH, t), F32)] + ([c_sem] if carry else []),
        compiler_params=_params(("arbitrary", "arbitrary")),
    )(qkv, qkv, qkv, run, run_t, o, lse, do, *c_in)
    return (out[0], out[1], out[2], _carry_state(out[3:])) if carry else tuple(out)


Z_COL0 = 3 * WIDTH // LANE
FGATE_COL0 = 7 * WIDTH // LANE


def _gdn_post(o, pm, onw):
    t = pm.shape[0]

    def body(o_ref, z_ref, w_ref, m_ref):
        z = z_ref[...]
        sz = z * _sigmoid(z)
        halves = []
        for hh in range(2):
            ov = o_ref[hh]
            n = ov * lax.rsqrt(jnp.mean(ov * ov, axis=-1, keepdims=True) + EPS) * w_ref[...]
            halves.append(n * sz[:, hh * DH:(hh + 1) * DH])
        m_ref[...] = jnp.concatenate(halves, axis=1).astype(m_ref.dtype)

    return pl.pallas_call(
        body, name="gdn_post", grid=(WIDTH // LANE,),
        in_specs=[pl.BlockSpec((2, t, DH), lambda j: (j, 0, 0)), pl.BlockSpec((t, LANE), lambda j: (0, Z_COL0 + j)),
                  pl.BlockSpec((1, DH), lambda j: (0, 0))],
        out_specs=pl.BlockSpec((t, LANE), lambda j: (0, j)),
        out_shape=jax.ShapeDtypeStruct((t, WIDTH), MXU),
        compiler_params=_params(("arbitrary",)),
    )(o, pm, onw)


def _gdn_post_bwd(o, pm, onw, dmix):
    t = pm.shape[0]

    def body(o_ref, z_ref, w_ref, dm_ref, do_ref, dz_ref, dw_ref):
        @pl.when(pl.program_id(0) == 0)
        def _():
            dw_ref[...] = jnp.zeros_like(dw_ref)

        z = z_ref[...]
        sg = _sigmoid(z)
        sz = z * sg
        dsz = sg * (1.0 + z * (1.0 - sg))
        dm = dm_ref[...]
        for hh in range(2):
            cols = slice(hh * DH, (hh + 1) * DH)
            ov = o_ref[hh]
            r = lax.rsqrt(jnp.mean(ov * ov, axis=-1, keepdims=True) + EPS)
            xn = ov * r
            dmh = dm[:, cols]
            dn = dmh * sz[:, cols]
            dz_ref[:, cols] = dmh * (xn * w_ref[...]) * dsz[:, cols]
            dw_ref[...] += jnp.sum(dn * xn, axis=0, keepdims=True)
            g = dn * w_ref[...]
            do_ref[hh] = r * (g - xn * jnp.mean(g * xn, axis=-1, keepdims=True))

    return pl.pallas_call(
        body, name="gdn_post_bwd", grid=(WIDTH // LANE,),
        in_specs=[pl.BlockSpec((2, t, DH), lambda j: (j, 0, 0)), pl.BlockSpec((t, LANE), lambda j: (0, Z_COL0 + j)),
                  pl.BlockSpec((1, DH), lambda j: (0, 0)), pl.BlockSpec((t, LANE), lambda j: (0, j))],
        out_specs=[pl.BlockSpec((2, t, DH), lambda j: (j, 0, 0)), pl.BlockSpec((t, LANE), lambda j: (0, j)),
                   pl.BlockSpec((1, DH), lambda j: (0, 0))],
        out_shape=[jax.ShapeDtypeStruct((HEADS, t, DH), F32), jax.ShapeDtypeStruct((t, WIDTH), F32),
                   jax.ShapeDtypeStruct((1, DH), F32)],
        compiler_params=_params(("arbitrary",)),
    )(o, pm, onw, dmix)


def _fox_post(o, pm):
    t = pm.shape[0]

    def body(o_ref, g_ref, m_ref):
        m_ref[...] = (jnp.concatenate([o_ref[0], o_ref[1]], axis=1) * _sigmoid(g_ref[...])).astype(m_ref.dtype)

    return pl.pallas_call(
        body, name="fox_post", grid=(WIDTH // LANE,),
        in_specs=[pl.BlockSpec((2, t, DH), lambda j: (j, 0, 0)), pl.BlockSpec((t, LANE), lambda j: (0, FGATE_COL0 + j))],
        out_specs=pl.BlockSpec((t, LANE), lambda j: (0, j)),
        out_shape=jax.ShapeDtypeStruct((t, WIDTH), MXU),
        compiler_params=_params(("arbitrary",)),
    )(o, pm)


def _fox_post_bwd(o, pm, dmix):
    t = pm.shape[0]

    def body(o_ref, g_ref, dm_ref, do_ref, dg_ref):
        sg = _sigmoid(g_ref[...])
        dm = dm_ref[...]
        for hh in range(2):
            cols = slice(hh * DH, (hh + 1) * DH)
            do_ref[hh] = dm[:, cols] * sg[:, cols]
            dg_ref[:, cols] = dm[:, cols] * o_ref[hh] * (sg * (1.0 - sg))[:, cols]

    return pl.pallas_call(
        body, name="fox_post_bwd", grid=(WIDTH // LANE,),
        in_specs=[pl.BlockSpec((2, t, DH), lambda j: (j, 0, 0)), pl.BlockSpec((t, LANE), lambda j: (0, FGATE_COL0 + j)),
                  pl.BlockSpec((t, LANE), lambda j: (0, j))],
        out_specs=[pl.BlockSpec((2, t, DH), lambda j: (j, 0, 0)), pl.BlockSpec((t, LANE), lambda j: (0, j))],
        out_shape=[jax.ShapeDtypeStruct((HEADS, t, DH), F32), jax.ShapeDtypeStruct((t, WIDTH), F32)],
        compiler_params=_params(("arbitrary",)),
    )(o, pm, dmix)


def _tail(x, mixg, mixf, tgt, wo, n2w, wg_t, wu_t, wd, fw):
    t, d = x.shape
    dff = wd.shape[0]
    tb = min(TB, t)

    def body(x_ref, mg_ref, mf_ref, t_ref, wo_ref, n2_ref, wg_ref, wu_ref, wd_ref, fw_ref,
             h2_ref, act_ref, dgate_ref, dup_ref, dx3_ref, dx2_ref, dmg_ref, dmf_ref, dn2_ref, dfw_ref, loss_ref):
        @pl.when(pl.program_id(0) == 0)
        def _():
            dn2_ref[...] = jnp.zeros_like(dn2_ref)
            dfw_ref[...] = jnp.zeros_like(dfw_ref)
            loss_ref[...] = jnp.zeros_like(loss_ref)

        x2 = x_ref[...] + _dot(mg_ref[...], wo_ref[0:WIDTH, :]) + _dot(mf_ref[...], wo_ref[WIDTH:2 * WIDTH, :])
        r2 = lax.rsqrt(jnp.mean(x2 * x2, axis=-1, keepdims=True) + EPS)
        xn2 = x2 * r2
        h2 = (xn2 * n2_ref[...]).astype(MXU)
        h2_ref[...] = h2
        gate = _dot_nt(h2, wg_ref[...])
        up = _dot_nt(h2, wu_ref[...])
        sg = _sigmoid(gate)
        sl = gate * sg
        act = (sl * up).astype(MXU)
        act_ref[...] = act
        x3 = x2 + _dot(act, wd_ref[...])
        r3 = lax.rsqrt(jnp.mean(x3 * x3, axis=-1, keepdims=True) + EPS)
        xn3 = x3 * r3
        err = xn3 * fw_ref[...] - t_ref[...]
        loss_ref[...] += 0.5 * jnp.sum(jnp.mean(err * err, axis=-1, keepdims=True), axis=0, keepdims=True)
        dy = err * (1.0 / d)
        dfw_ref[...] += jnp.sum(dy * xn3, axis=0, keepdims=True)
        g3 = dy * fw_ref[...]
        dx3 = r3 * (g3 - xn3 * jnp.mean(g3 * xn3, axis=-1, keepdims=True))
        dx3_ref[...] = dx3.astype(MXU)
        dact = _dot_nt(dx3, wd_ref[...])
        dgate = (dact * up * (sg * (1.0 + gate * (1.0 - sg)))).astype(MXU)
        dup = (dact * sl).astype(MXU)
        dgate_ref[...] = dgate
        dup_ref[...] = dup
        dh2 = _dot(dgate, wg_ref[...]) + _dot(dup, wu_ref[...])
        dn2_ref[...] += jnp.sum(dh2 * xn2, axis=0, keepdims=True)
        g2 = dh2 * n2_ref[...]
        dx2 = dx3 + r2 * (g2 - xn2 * jnp.mean(g2 * xn2, axis=-1, keepdims=True))
        dx2_ref[...] = dx2
        dmg_ref[...] = _dot_nt(dx2, wo_ref[0:WIDTH, :])
        dmf_ref[...] = _dot_nt(dx2, wo_ref[WIDTH:2 * WIDTH, :])

    def tok(n):
        return pl.BlockSpec((tb, n), lambda i: (i, 0))

    acc = pl.BlockSpec((1, d), lambda i: (0, 0))
    sds = jax.ShapeDtypeStruct
    return pl.pallas_call(
        body, name="tail", grid=(t // tb,),
        in_specs=[tok(d), tok(WIDTH), tok(WIDTH), tok(d), _resident(wo.shape), _resident((1, d)),
                  _resident(wg_t.shape), _resident(wu_t.shape), _resident(wd.shape), _resident((1, d))],
        out_specs=[tok(d), tok(dff), tok(dff), tok(dff), tok(d), tok(d), tok(WIDTH), tok(WIDTH), acc, acc,
                   pl.BlockSpec((1, 1), lambda i: (0, 0))],
        out_shape=[sds((t, d), MXU), sds((t, dff), MXU), sds((t, dff), MXU), sds((t, dff), MXU), sds((t, d), MXU),
                   sds((t, d), F32), sds((t, WIDTH), F32), sds((t, WIDTH), F32), sds((1, d), F32), sds((1, d), F32),
                   sds((1, 1), F32)],
        compiler_params=_params(("arbitrary",)),
    )(x, mixg, mixf, tgt, wo, n2w, wg_t, wu_t, wd, fw)


def _wgrad(a, b, name):
    t, m = a.shape
    n = b.shape[1]
    bm = 256 if m % 256 == 0 else LANE
    cast = b.dtype != jnp.dtype(MXU)

    def body(a_ref, b_ref, o_ref, *scratch):
        if cast:
            @pl.when(pl.program_id(0) == 0)
            def _():
                scratch[0][...] = b_ref[...].astype(MXU)
        o_ref[...] = _dot_tn(a_ref[...], scratch[0][...] if cast else b_ref[...]).astype(o_ref.dtype)

    return pl.pallas_call(
        body, name=name, grid=(m // bm,),
        in_specs=[pl.BlockSpec((t, bm), lambda i: (0, i)), _resident((t, n))],
        out_specs=pl.BlockSpec((bm, n), lambda i: (i, 0)),
        out_shape=jax.ShapeDtypeStruct((m, n), WIRE),
        scratch_shapes=[pltpu.VMEM((t, n), MXU)] if cast else [],
        compiler_params=_params(("arbitrary",)),
    )(a, b)


def _merge_dw_in(d_gdn, d_z, d_fox, d_fg, d_small):
    return jnp.concatenate([d_gdn, d_z, d_small[:2 * HEADS], d_fox, d_fg, d_small[2 * HEADS:3 * HEADS]], axis=0)


def _lanes(*pieces):
    v = jnp.concatenate([p.reshape(-1).astype(F32) for p in pieces])
    return jnp.pad(v, (0, LANE - v.shape[0])).reshape(1, LANE)


def _vector_params(p):
    d = p["norm1_w"].size
    gparams = jnp.concatenate([_lanes(jnp.zeros(HEADS), p["gdn_dt_bias"], p["fox_f_bias"]),
                               _lanes(jnp.zeros(HEADS), p["gdn_A_log"]), jnp.zeros((6, LANE), F32)])
    fox_nw = jnp.stack([jnp.tile(p["fox_q_norm_w"].reshape(-1), 2), jnp.tile(p["fox_k_norm_w"].reshape(-1), 2),
                        jnp.ones((LANE,), F32)])
    return dict(n1w=p["norm1_w"].reshape(1, d), n2w=p["norm2_w"].reshape(1, d), fw=p["final_norm_w"].reshape(1, d),
                onw=p["gdn_out_norm_w"].reshape(1, DH), gparams=gparams, fox_nw=fox_nw)


def _mixer_forward(x, vp, w_t, ws_t, conv_w, carry=None):
    h1, pm, ps = _inproj(x, vp["n1w"], w_t, ws_t)
    gates, run, tot, run_t = _gates(ps, vp["gparams"])
    fqkv = _fox_prep(pm, vp["fox_nw"])
    o_fox, lse, *carried = _fox_fwd(fqkv, run, run_t, carry)
    if carry:
        pm, o_fox = lax.optimization_barrier((pm, o_fox))
    mixf = _fox_post(o_fox, pm)
    gqkv = _gdn_prep(pm, conv_w)
    o_gdn, states, inv = _gdn_fwd(gqkv, gates, run, tot, run_t)
    mixg = _gdn_post(o_gdn, pm, vp["onw"])
    return dict(h1=h1, pm=pm, ps=ps, gates=gates, run=run, tot=tot, run_t=run_t, gqkv=gqkv, o_gdn=o_gdn, states=states,
                inv=inv, mixg=mixg, fqkv=fqkv, o_fox=o_fox, lse=lse, mixf=mixf, carried=carried[0] if carried else None)


def _mixer_backward(x, vp, w_t, ws_t, conv_w, f, dx2, dmixg, dmixf, carry=None, scatter_own=False):
    pm = f["pm"]
    do_fox, dfg = _fox_post_bwd(f["o_fox"], pm, dmixf)
    dfqkv, dcol_f, drow_f, *carried = _fox_bwd(f["fqkv"], f["run"], f["run_t"], f["o_fox"], f["lse"], do_fox, carry)
    dfox, dfnw = _fox_prep_bwd(pm, vp["fox_nw"], dfqkv)
    do_gdn, dz, donw = _gdn_post_bwd(f["o_gdn"], pm, vp["onw"], dmixg)
    dgqkv, dcol_g, dtot_g, drow_g = _gdn_bwd(f["gqkv"], f["gates"], f["run"], f["tot"], f["run_t"], f["inv"], f["states"], do_gdn)
    dgdn, dconv = _gdn_prep_bwd(pm, conv_w, dgqkv)
    dps, gsum = _gates_bwd(f["ps"], vp["gparams"], dcol_g, dtot_g, dcol_f, drow_g, drow_f)
    h1 = f["h1"]
    dw_in = _merge_dw_in(_wgrad(dgdn, h1, "dw_in_gdn"), _wgrad(dz, h1, "dw_in_z"), _wgrad(dfox, h1, "dw_in_fox"),
                         _wgrad(dfg, h1, "dw_in_fgate"), _wgrad(dps, h1, "dw_in_small"))
    own = ("scatter", [_cut(dw_in, 0), _cut(dconv, 1)]) if scatter_own else None
    grad_x, dn1w, *sent = _inproj_bwd(x, vp["n1w"], dx2, dgdn, dz, dfox, dfg, dps, w_t, ws_t, own)
    small = dict(dn1w=dn1w, gsum=gsum, donw=donw, dfnw=dfnw)
    return (grad_x, dw_in, dconv, small, *carried, *sent)


VECTORS = ("norm1_w", "norm2_w", "final_norm_w", "gdn_A_log", "gdn_dt_bias", "gdn_out_norm_w", "fox_f_bias",
           "fox_q_norm_w", "fox_k_norm_w")
VEC_ROWS = 16
LOSS_ROW = len(VECTORS)


def _pack_vectors(dn1w, dn2w, dfw, gsum, donw, dfnw, loss):
    d = dn1w.shape[1]

    def body(n1_ref, n2_ref, fw_ref, gs_ref, on_ref, fn_ref, loss_ref, o_ref):
        o_ref[...] = jnp.zeros_like(o_ref)
        o_ref[0:1, :] = n1_ref[...]
        o_ref[1:2, :] = n2_ref[...]
        o_ref[2:3, :] = fw_ref[...]
        o_ref[3:4, 0:HEADS] = gs_ref[0:1, 0:HEADS]
        o_ref[4:5, 0:HEADS] = gs_ref[1:2, 0:HEADS]
        o_ref[5:6, 0:DH] = on_ref[...]
        o_ref[6:7, 0:HEADS] = gs_ref[2:3, 0:HEADS]
        for kind in range(2):
            v = fn_ref[kind, 0]
            for j in range(1, fn_ref.shape[1]):
                v = v + fn_ref[kind, j]
            o_ref[7 + kind:8 + kind, 0:DH] = v[:, :DH] + v[:, DH:]
        o_ref[LOSS_ROW:LOSS_ROW + 1, 0:1] = loss_ref[...]

    return pl.pallas_call(body, name="pack_vectors", out_shape=jax.ShapeDtypeStruct((VEC_ROWS, d), F32),
                          compiler_params=_params())(dn1w, dn2w, dfw, gsum, donw, dfnw, loss)


def _late_grads(f, h2, act, dgate, dup, dx3, dx2):
    return {"w_out": jnp.concatenate([_wgrad(f["mixg"], dx2, "dw_out_gdn"), _wgrad(f["mixf"], dx2, "dw_out_fox")], axis=0),
            "w_ffn_gate": _wgrad(dgate, h2, "dw_gate"), "w_ffn_up": _wgrad(dup, h2, "dw_up"),
            "w_ffn_down": _wgrad(act, dx3, "dw_down")}


def _local_step(x, tgt, p, w_in_t, conv_w, wo, wg_t, wu_t, wd):
    vp = _vector_params(p)
    ws_t = _small_rows(w_in_t)
    f = _mixer_forward(x, vp, w_in_t, ws_t, conv_w)
    (h2, act, dgate, dup, dx3, dx2, dmixg, dmixf, dn2w, dfw, loss) = _tail(
        x, f["mixg"], f["mixf"], tgt, wo, vp["n2w"], wg_t, wu_t, wd, vp["fw"])
    grad_x, dw_in, dconv, small = _mixer_backward(x, vp, w_in_t, ws_t, conv_w, f, dx2, dmixg, dmixf)
    grads = {"w_in": dw_in, "gdn_conv_w": dconv, **_late_grads(f, h2, act, dgate, dup, dx3, dx2)}
    vec = _pack_vectors(small["dn1w"], dn2w, dfw, small["gsum"], small["donw"], small["dfnw"], loss)
    return grad_x, grads, vec


def _my_place():
    return lax.axis_index("x"), lax.axis_index("y"), lax.axis_index("c")


def _peers():
    x, y, c = _my_place()
    peers = []
    for k in range(1, N_DEV):
        px = 1 - x if k & 4 else x
        py = 1 - y if k & 2 else y
        pc = 1 - c if k & 1 else c
        peers.append(((px, py, pc), 4 * px + 2 * py + pc))
    return 4 * x + 2 * y + c, peers


def _spread_copies(kind, srcs, lands, send_sems, recv_sems):
    me, peers = _peers()
    kinds = [kind] * len(srcs) if isinstance(kind, str) else kind
    remote, local = [], []
    for i, (kd, src, land) in enumerate(zip(kinds, srcs, lands)):
        for k, (dev, idx) in enumerate(peers):
            remote.append(pltpu.make_async_remote_copy(
                src_ref=src if kd == "gather" else src.at[idx], dst_ref=land.at[me],
                send_sem=send_sems.at[i * (N_DEV - 1) + k], recv_sem=recv_sems.at[i * (N_DEV - 1) + k],
                device_id=dev, device_id_type=MESH))
        local.append((src if kd == "gather" else src.at[me], land.at[me]))
    return remote, local


def _gather_two_level(arrays, name):
    n = len(arrays)

    def body(*refs):
        srcs, lands = refs[:n], refs[n:2 * n]
        send_sems, recv_sems, local_sems = refs[2 * n:]
        x, y, c = _my_place()
        me, sibling = (x, y, c), (x, y, 1 - c)
        chips = [(1 - x, y), (x, 1 - y), (1 - x, 1 - y)]

        def index(p):
            return 4 * p[0] + 2 * p[1] + p[2]

        def copy(i, k, block, to, src=None):
            blk = lands[i].at[index(block)]
            return pltpu.make_async_remote_copy(
                src_ref=blk if src is None else src, dst_ref=blk, send_sem=send_sems.at[7 * i + k],
                recv_sem=recv_sems.at[7 * i + k], device_id=to, device_id_type=MESH)

        mine = [pltpu.make_async_copy(srcs[i], lands[i].at[index(me)], local_sems.at[i]) for i in range(n)]
        for cp in mine:
            cp.start()
        first = []
        for i in range(n):
            first.append(copy(i, 0, me, sibling, src=srcs[i]))
            first += [copy(i, 1 + j, me, (*chip, c), src=srcs[i]) for j, chip in enumerate(chips)]
        for cp in first:
            cp.start()
        passed = []
        for i in range(n):
            for j, chip in enumerate(chips):
                copy(i, 1 + j, (*chip, c), me).wait_recv()
                passed.append(copy(i, 4 + j, (*chip, c), sibling))
                passed[-1].start()
        for i in range(n):
            copy(i, 0, sibling, me).wait_recv()
            for j, chip in enumerate(chips):
                copy(i, 4 + j, (*chip, 1 - c), me).wait_recv()
        for cp in first + passed:
            cp.wait_send()
        for cp in mine:
            cp.wait()

    return pl.pallas_call(
        body, name=name,
        out_shape=[jax.ShapeDtypeStruct((N_DEV,) + a.shape, a.dtype) for a in arrays],
        in_specs=[pl.BlockSpec(memory_space=pl.ANY)] * n, out_specs=[pl.BlockSpec(memory_space=pl.ANY)] * n,
        scratch_shapes=[pltpu.SemaphoreType.DMA((7 * n,)), pltpu.SemaphoreType.DMA((7 * n,)),
                        pltpu.SemaphoreType.DMA((n,))],
    )(*arrays)


def _land_shape(kind, a):
    return (N_DEV,) + a.shape if kind == "gather" else a.shape


HBM = pl.BlockSpec(memory_space=pltpu.HBM)
SEM = pl.BlockSpec(memory_space=pltpu.SEMAPHORE)


def _hbm(a):
    return pltpu.with_memory_space_constraint(a, pltpu.HBM)


def _carry_operands(kind, arrays):
    n = len(arrays)
    kinds = [kind] * n if isinstance(kind, str) else kind
    lands = [lax.empty(_land_shape(kd, a), a.dtype) for kd, a in zip(kinds, arrays)]
    sems = [pltpu.SemaphoreType.DMA((n * (N_DEV - 1),))] * 2
    return ([_hbm(a) for a in list(arrays) + lands], [HBM] * (2 * n),
            sems + [pltpu.HBM(a.shape, a.dtype) for a in list(arrays) + lands], [SEM] * 2 + [HBM] * (2 * n),
            pltpu.SemaphoreType.DMA((n,)))


def _carry_step(kind, in_refs, send_sems, recv_sems, local_sems, step, n_steps):
    n = len(in_refs) // 2
    remote, local = _spread_copies(kind, in_refs[:n], in_refs[n:], send_sems, recv_sems)
    copies = [pltpu.make_async_copy(s, d, local_sems.at[i]) for i, (s, d) in enumerate(local)]
    per_step = -(-len(remote) // n_steps)
    for s in range(-(-len(remote) // per_step)):
        @pl.when(step == s)
        def _():
            for cp in remote[s * per_step:(s + 1) * per_step]:
                cp.start()
            if s == 0:
                for cp in copies:
                    cp.start()

    @pl.when(step == n_steps - 1)
    def _():
        for cp in copies:
            cp.wait()


def _carry_state(extra_out):
    n = (len(extra_out) - 2) // 2
    return list(extra_out[2:2 + n]), list(extra_out[2 + n:]), extra_out[0], extra_out[1]


def _spread_start(arrays, kind, name):
    n = len(arrays)

    def body(*refs):
        srcs, lands = refs[:n], refs[n:2 * n]
        send_sems, recv_sems = refs[2 * n], refs[2 * n + 1]
        token = refs[4 * n + 2]
        local_sems = refs[4 * n + 3]
        remote, local = _spread_copies(kind, srcs, lands, send_sems, recv_sems)
        for cp in remote:
            cp.start()
        copies = [pltpu.make_async_copy(s, d, local_sems.at[i]) for i, (s, d) in enumerate(local)]
        for cp in copies:
            cp.start()
        for cp in copies:
            cp.wait()
        token[...] = jnp.zeros_like(token)

    sems = (pltpu.SemaphoreType.DMA((n * (N_DEV - 1),)),) * 2
    kinds = [kind] * n if isinstance(kind, str) else kind
    lands = [lax.empty(_land_shape(kd, a), a.dtype) for kd, a in zip(kinds, arrays)]
    out = pl.pallas_call(
        body, name=name,
        out_shape=sems + tuple(pltpu.HBM(a.shape, a.dtype) for a in list(arrays) + lands)
        + (jax.ShapeDtypeStruct((8, LANE), F32),),
        in_specs=[HBM] * (2 * n), out_specs=tuple([SEM] * 2 + [HBM] * (2 * n) + [pl.BlockSpec(memory_space=pltpu.VMEM)]),
        input_output_aliases={j: 2 + j for j in range(2 * n)},
        scratch_shapes=[pltpu.SemaphoreType.DMA((n,))],
        compiler_params=pltpu.CompilerParams(has_side_effects=pltpu.SideEffectType.DATAFLOW_SIDE_EFFECTING),
    )(*[_hbm(a) for a in arrays], *[_hbm(a) for a in lands])
    return (list(out[2:2 + n]), list(out[2 + n:2 + 2 * n]), out[0], out[1]), out[-1]


def _spread_wait(state, kind, after, name):
    srcs, lands, send_sems, recv_sems = state
    n = len(srcs)
    after = list(after) if isinstance(after, (list, tuple)) else [after]

    def body(*refs):
        remote, _ = _spread_copies(kind, refs[:n], refs[n:2 * n], refs[2 * n], refs[2 * n + 1])
        for cp in remote:
            cp.wait_send()
        for cp in remote:
            cp.wait_recv()

    out = pl.pallas_call(
        body, name=name,
        out_shape=tuple(pltpu.HBM(a.shape, a.dtype) for a in srcs + lands),
        in_specs=[HBM] * (2 * n) + [SEM, SEM] + [pl.BlockSpec(memory_space=pl.ANY)] * len(after),
        out_specs=tuple([HBM] * (2 * n)),
        input_output_aliases={j: j for j in range(2 * n)},
        compiler_params=pltpu.CompilerParams(has_side_effects=pltpu.SideEffectType.DATAFLOW_SIDE_EFFECTING),
    )(*srcs, *lands, send_sems, recv_sems, *after)
    return list(out[n:])


ADAM_ROWS = 128
ADAM_COLS = 256


def _adam_math(g, w, m, v):
    nm = ADAM_B1 * m + (1.0 - ADAM_B1) * g
    nv = ADAM_B2 * v + (1.0 - ADAM_B2) * (g * g)
    m_hat = nm / (1.0 - ADAM_B1 ** ADAM_STEP)
    v_hat = nv / (1.0 - ADAM_B2 ** ADAM_STEP)
    return -ADAM_LR * (m_hat / (jnp.sqrt(v_hat) + ADAM_EPS) + ADAM_WD * w), nm, nv


def _sum_parts(p_ref):
    g = p_ref[0].astype(F32)
    for s in range(1, N_DEV):
        g = g + p_ref[s].astype(F32)
    return g


def _adam_matrix(parts, w, m, v, name):
    _, r, c = w.shape
    rb = ADAM_ROWS if r % ADAM_ROWS == 0 else r
    cb = ADAM_COLS if (rb == r and c % ADAM_COLS == 0) else c

    def body(p_ref, w_ref, m_ref, v_ref, g_ref, d_ref, nm_ref, nv_ref):
        g = _sum_parts(p_ref)
        g_ref[0] = g
        d_ref[0], nm_ref[0], nv_ref[0] = _adam_math(g, w_ref[0], m_ref[0], v_ref[0])

    blk = pl.BlockSpec((1, rb, cb), lambda i, j: (0, i, j))
    return pl.pallas_call(
        body, name=name, grid=(r // rb, c // cb),
        in_specs=[pl.BlockSpec((N_DEV, rb, cb), lambda i, j: (0, i, j)), blk, blk, blk],
        out_specs=[blk] * 4, out_shape=[jax.ShapeDtypeStruct(w.shape, F32)] * 4,
        compiler_params=_params(("arbitrary", "arbitrary")),
    )(parts, w, m, v)


def _adam_vectors(parts, ws, ms, vs):
    nv = len(ws)

    def body(*refs):
        p_ref = refs[0]
        w_refs, m_refs, v_refs = refs[1:1 + nv], refs[1 + nv:1 + 2 * nv], refs[1 + 2 * nv:1 + 3 * nv]
        outs = refs[1 + 3 * nv:]
        g_all = _sum_parts(p_ref)
        for i in range(nv):
            n = w_refs[i].shape[1]
            g = g_all[i:i + 1, 0:n]
            d, nm, nvv = _adam_math(g, w_refs[i][...], m_refs[i][...], v_refs[i][...])
            outs[i][...] = g
            outs[nv + i][...] = d
            outs[2 * nv + i][...] = nm
            outs[3 * nv + i][...] = nvv
        outs[4 * nv][...] = g_all[LOSS_ROW:LOSS_ROW + 1, 0:1]

    shapes = [jax.ShapeDtypeStruct(a.shape, F32) for a in ws]
    out = pl.pallas_call(body, name="adam_vectors", out_shape=shapes * 4 + [jax.ShapeDtypeStruct((1, 1), F32)],
                         compiler_params=_params())(parts, *ws, *ms, *vs)
    return out[:nv], out[nv:2 * nv], out[2 * nv:3 * nv], out[3 * nv:4 * nv], out[4 * nv]


MATRICES = (("w_in", 1), ("gdn_conv_w", 1), ("w_out", 0), ("w_ffn_gate", 1), ("w_ffn_up", 1), ("w_ffn_down", 0))
TRANSPOSED = ("w_in", "w_ffn_gate", "w_ffn_up")
WEIGHTS = ("norm1_w", "w_in", "gdn_conv_w", "gdn_A_log", "gdn_dt_bias", "gdn_out_norm_w", "fox_f_bias", "fox_q_norm_w",
           "fox_k_norm_w", "w_out", "norm2_w", "w_ffn_gate", "w_ffn_up", "w_ffn_down", "final_norm_w")


def _join(blocks, axis):
    _, r, c = blocks.shape
    if axis == 0:
        return blocks.reshape(N_DEV * r, c)
    return blocks.transpose(1, 0, 2).reshape(r, N_DEV * c)


def _cut(full, axis):
    r, c = full.shape
    if axis == 0:
        return full.reshape(N_DEV, r // N_DEV, c)
    return full.reshape(r, N_DEV, c // N_DEV).transpose(1, 0, 2)


def kernel(x, norm1_w, w_in, gdn_conv_w, gdn_A_log, gdn_dt_bias, gdn_out_norm_w, fox_f_bias, fox_q_norm_w, fox_k_norm_w, w_out, norm2_w, w_ffn_gate, w_ffn_up, w_ffn_down, final_norm_w, loss_target, m_norm1_w, m_w_in, m_gdn_conv_w, m_gdn_A_log, m_gdn_dt_bias, m_gdn_out_norm_w, m_fox_f_bias, m_fox_q_norm_w, m_fox_k_norm_w, m_w_out, m_norm2_w, m_w_ffn_gate, m_w_ffn_up, m_w_ffn_down, m_final_norm_w, v_norm1_w, v_w_in, v_gdn_conv_w, v_gdn_A_log, v_gdn_dt_bias, v_gdn_out_norm_w, v_fox_f_bias, v_fox_q_norm_w, v_fox_k_norm_w, v_w_out, v_norm2_w, v_w_ffn_gate, v_w_ffn_up, v_w_ffn_down, v_final_norm_w):
    w = dict(norm1_w=norm1_w, w_in=w_in, gdn_conv_w=gdn_conv_w, gdn_A_log=gdn_A_log, gdn_dt_bias=gdn_dt_bias,
             gdn_out_norm_w=gdn_out_norm_w, fox_f_bias=fox_f_bias, fox_q_norm_w=fox_q_norm_w, fox_k_norm_w=fox_k_norm_w,
             w_out=w_out, norm2_w=norm2_w, w_ffn_gate=w_ffn_gate, w_ffn_up=w_ffn_up, w_ffn_down=w_ffn_down,
             final_norm_w=final_norm_w)
    m = dict(norm1_w=m_norm1_w, w_in=m_w_in, gdn_conv_w=m_gdn_conv_w, gdn_A_log=m_gdn_A_log, gdn_dt_bias=m_gdn_dt_bias,
             gdn_out_norm_w=m_gdn_out_norm_w, fox_f_bias=m_fox_f_bias, fox_q_norm_w=m_fox_q_norm_w,
             fox_k_norm_w=m_fox_k_norm_w, w_out=m_w_out, norm2_w=m_norm2_w, w_ffn_gate=m_w_ffn_gate,
             w_ffn_up=m_w_ffn_up, w_ffn_down=m_w_ffn_down, final_norm_w=m_final_norm_w)
    v = dict(norm1_w=v_norm1_w, w_in=v_w_in, gdn_conv_w=v_gdn_conv_w, gdn_A_log=v_gdn_A_log, gdn_dt_bias=v_gdn_dt_bias,
             gdn_out_norm_w=v_gdn_out_norm_w, fox_f_bias=v_fox_f_bias, fox_q_norm_w=v_fox_q_norm_w,
             fox_k_norm_w=v_fox_k_norm_w, w_out=v_w_out, norm2_w=v_norm2_w, w_ffn_gate=v_w_ffn_gate,
             w_ffn_up=v_w_ffn_up, w_ffn_down=v_w_ffn_down, final_norm_w=v_final_norm_w)
    late = ("w_out", "w_ffn_gate", "w_ffn_up", "w_ffn_down")
    xs, tgt = x[0], loss_target[0]
    vp = _vector_params({n: w[n] for n in VECTORS})

    def rows_of(d, n):
        return d[n].transpose(0, 2, 1) if n in TRANSPOSED else d[n]

    wr, mr, vr = ({n: rows_of(d, n) for n, _ in MATRICES} for d in (w, m, v))

    w_in_blocks, conv_blocks = _gather_two_level([wr["w_in"][0].astype(WIRE), w["gdn_conv_w"][0]], "gather_in")
    w_t = _join(w_in_blocks, 0)
    conv_w = _join(conv_blocks, 1)
    f = _mixer_forward(xs, vp, w_t, _small_rows(w_t), conv_w, ("gather", [wr[n][0].astype(WIRE) for n in late]))
    full = {n: _join(b, 0) for n, b in zip(late, _spread_wait(f["carried"], "gather", f["mixg"], "gather_late_wait"))}

    (h2, act, dgate, dup, dx3, dx2, dmixg, dmixf, dn2w, dfw, loss) = _tail(
        xs, f["mixg"], f["mixf"], tgt, full["w_out"], vp["n2w"], full["w_ffn_gate"], full["w_ffn_up"],
        full["w_ffn_down"], vp["fw"])
    dlate = _late_grads(f, h2, act, dgate, dup, dx3, dx2)

    grad_x, dw_in, dconv, small, state_grads, state_own = _mixer_backward(
        xs, vp, w_t, _small_rows(w_t), conv_w, f, dx2, dmixg, dmixf, ("scatter", [_cut(dlate[n], 0) for n in late]),
        scatter_own=True)
    vec = _pack_vectors(small["dn1w"], dn2w, dfw, small["gsum"], small["donw"], small["dfnw"], loss)

    state_vec, token = _spread_start([vec], "gather", "vectors_start")
    parts = dict(zip(late, _spread_wait(state_grads, "scatter", token, "grads_late_wait")))
    results = [{}, {}, {}, {}]

    def update(n):
        for d, a in zip(results, _adam_matrix(parts[n], wr[n], mr[n], vr[n], "adam_" + n)):
            d[n] = a.transpose(0, 2, 1) if n in TRANSPOSED else a

    for n in late:
        update(n)
    parts["w_in"], parts["gdn_conv_w"] = _spread_wait(state_own, "scatter", [results[0][n] for n in late], "own_wait")
    (parts_vec,) = _spread_wait(state_vec, "gather", parts["w_in"], "vectors_wait")
    update("w_in")
    update("gdn_conv_w")
    row = lambda a: a.reshape(1, -1)
    *vec_out, total_loss = _adam_vectors(parts_vec, [row(w[n]) for n in VECTORS], [row(m[n]) for n in VECTORS],
                                         [row(v[n]) for n in VECTORS])
    for d, arrs in zip(results, vec_out):
        for n, a in zip(VECTORS, arrs):
            d[n] = a.reshape(w[n].shape)
    return (total_loss[0, 0], grad_x[None], *[d[n] for d in results for n in WEIGHTS])
```

```python
import functools

import jax
import jax.numpy as jnp
from jax import lax
from jax.experimental import pallas as pl
from jax.experimental.pallas import tpu as pltpu

F32 = jnp.float32
MXU = jnp.bfloat16
WIRE = jnp.bfloat16
HI = lax.Precision.HIGHEST
EPS = 1e-6

N_DEV = 8
HEADS = 8
DH = 64
WIDTH = HEADS * DH
CHUNK = 64
LANE = 128
ROW_ALIGN = 16
TB = 256
QB = 256
VMEM_LIMIT = 60 * 1024 * 1024

ADAM_LR = 0.001
ADAM_B1 = 0.9
ADAM_B2 = 0.999
ADAM_EPS = 1e-08
ADAM_WD = 0.01
ADAM_STEP = 10

MESH = pl.DeviceIdType.MESH


def _params(sem=None):
    return pltpu.CompilerParams(dimension_semantics=sem, vmem_limit_bytes=VMEM_LIMIT)


def _resident(shape):
    n = len(shape)
    return pl.BlockSpec(shape, lambda *_: (0,) * n, pipeline_mode=pl.Buffered(1))


def _dot(a, b):
    return jnp.dot(a.astype(MXU), b.astype(MXU), preferred_element_type=F32)


def _dot_nt(a, b):
    return lax.dot_general(a.astype(MXU), b.astype(MXU), (((1,), (1,)), ((), ())), preferred_element_type=F32)


def _dot_tn(a, b):
    return lax.dot_general(a.astype(MXU), b.astype(MXU), (((0,), (0,)), ((), ())), preferred_element_type=F32)


def _hdot(a, b):
    return jnp.dot(a, b, precision=HI, preferred_element_type=F32)


def _hdot_nt(a, b):
    return lax.dot_general(a, b, (((1,), (1,)), ((), ())), precision=HI, preferred_element_type=F32)


def _hdot_tn(a, b):
    return lax.dot_general(a, b, (((0,), (0,)), ((), ())), precision=HI, preferred_element_type=F32)


def _sigmoid(x):
    return 0.5 * jnp.tanh(0.5 * x) + 0.5


def _softplus(x):
    return jnp.maximum(x, 0.0) + jnp.log(1.0 + jnp.exp(-jnp.abs(x)))


def _head_sum_matrix():
    ri = lax.broadcasted_iota(jnp.int32, (LANE, LANE), 0) // DH
    ci = lax.broadcasted_iota(jnp.int32, (LANE, LANE), 1) // DH
    return (ri == ci).astype(F32)


def _group_sum(a, ones_matrix):
    hi = a.astype(jnp.bfloat16)
    lo = (a - hi.astype(F32)).astype(jnp.bfloat16)
    m = ones_matrix.astype(jnp.bfloat16)
    return jnp.dot(hi, m, preferred_element_type=F32) + jnp.dot(lo, m, preferred_element_type=F32)


def _shift_down(x, s):
    return pltpu.roll(x, s, 0)


def _shift_up(x, s):
    return pltpu.roll(x, x.shape[0] - s, 0)


ROWS_A = 4 * WIDTH
ROWS_B = ROWS_A + 2 * HEADS
ROWS_C = ROWS_B + 4 * WIDTH


def _small_rows(w_t):
    return jnp.concatenate([w_t[ROWS_A:ROWS_B], w_t[ROWS_C:], jnp.zeros((LANE - 3 * HEADS, w_t.shape[1]), w_t.dtype)])


def _inproj(x, n1w, w_t, ws_t):
    t, d = x.shape
    tb = min(TB, t)

    def body(x_ref, nw_ref, wt_ref, ws_ref, h_ref, pm_ref, ps_ref):
        xv = x_ref[...]
        r = lax.rsqrt(jnp.mean(xv * xv, axis=-1, keepdims=True) + EPS)
        h = (xv * r * nw_ref[...]).astype(MXU)
        h_ref[...] = h
        pm_ref[:, 0:ROWS_A] = _dot_nt(h, wt_ref[0:ROWS_A, :])
        pm_ref[:, ROWS_A:2 * ROWS_A] = _dot_nt(h, wt_ref[ROWS_B:ROWS_C, :])
        ps_ref[...] = _dot_nt(h, ws_ref[...])

    return pl.pallas_call(
        body, name="inproj", grid=(t // tb,),
        in_specs=[pl.BlockSpec((tb, d), lambda i: (i, 0)), _resident((1, d)), _resident(w_t.shape), _resident((LANE, d))],
        out_specs=[pl.BlockSpec((tb, d), lambda i: (i, 0)), pl.BlockSpec((tb, 2 * ROWS_A), lambda i: (i, 0)),
                   pl.BlockSpec((tb, LANE), lambda i: (i, 0))],
        out_shape=[jax.ShapeDtypeStruct((t, d), MXU), jax.ShapeDtypeStruct((t, 2 * ROWS_A), F32),
                   jax.ShapeDtypeStruct((t, LANE), F32)],
        compiler_params=_params(("arbitrary",)),
    )(x, n1w, w_t, ws_t)


def _inproj_bwd(x, n1w, dx2, dgdn, dz, dfox, dfg, dps, w_t, ws_t, carry=None):
    t, d = x.shape
    tb = min(TB, t)
    w3 = 3 * WIDTH
    c_in, c_in_specs, c_out_shape, c_out_specs, c_sem = _carry_operands(*carry) if carry else ([], [], [], [], None)
    n_c = len(c_in)

    def body(*refs):
        x_ref, nw_ref, dx2_ref, dgdn_ref, dz_ref, dfox_ref, dfg_ref, dps_ref, wm_ref, ws_ref = refs[:10]
        gx_ref, dnw_ref = refs[10 + n_c:12 + n_c]
        if carry:
            _carry_step(carry[0], refs[10:10 + n_c], refs[12 + n_c], refs[13 + n_c], refs[-1], pl.program_id(0), t // tb)
        dh = _dot(dgdn_ref[...], wm_ref[0:w3, :])
        dh += _dot(dz_ref[...], wm_ref[w3:ROWS_A, :])
        dh += _dot(dfox_ref[...], wm_ref[ROWS_B:ROWS_B + w3, :])
        dh += _dot(dfg_ref[...], wm_ref[ROWS_B + w3:ROWS_C, :])
        dh += _dot(dps_ref[...], ws_ref[...])
        xv = x_ref[...]
        r = lax.rsqrt(jnp.mean(xv * xv, axis=-1, keepdims=True) + EPS)
        xn = xv * r

        @pl.when(pl.program_id(0) == 0)
        def _():
            dnw_ref[...] = jnp.zeros_like(dnw_ref)

        dnw_ref[...] += jnp.sum(dh * xn, axis=0, keepdims=True)
        g = dh * nw_ref[...]
        gx_ref[...] = dx2_ref[...] + r * (g - xn * jnp.mean(g * xn, axis=-1, keepdims=True))

    def tok(n):
        return pl.BlockSpec((tb, n), lambda i: (i, 0))

    out = pl.pallas_call(
        body, name="inproj_bwd", grid=(t // tb,),
        in_specs=[tok(d), _resident((1, d)), tok(d), tok(w3), tok(WIDTH), tok(w3), tok(WIDTH), tok(LANE),
                  _resident(w_t.shape), _resident(ws_t.shape)] + c_in_specs,
        out_specs=[tok(d), pl.BlockSpec((1, d), lambda i: (0, 0))] + c_out_specs,
        out_shape=[jax.ShapeDtypeStruct((t, d), F32), jax.ShapeDtypeStruct((1, d), F32)] + c_out_shape,
        input_output_aliases={10 + j: 4 + j for j in range(n_c)},
        scratch_shapes=[c_sem] if carry else [],
        compiler_params=_params(("arbitrary",)),
    )(x, n1w, dx2, dgdn, dz, dfox, dfg, dps, w_t, ws_t, *c_in)
    return (out[0], out[1], _carry_state(out[2:])) if carry else tuple(out)


def _gate_lanes(shape):
    lane = lax.broadcasted_iota(jnp.int32, shape, 1)
    return lane < HEADS, (lane >= HEADS) & (lane < 2 * HEADS), (lane >= 2 * HEADS) & (lane < 3 * HEADS)


def _block_masks():
    ri = lax.broadcasted_iota(jnp.int32, (LANE, LANE), 0)
    ci = lax.broadcasted_iota(jnp.int32, (LANE, LANE), 1)
    same = (ri // CHUNK) == (ci // CHUNK)
    return ((ri >= ci).astype(F32), (ri <= ci).astype(F32), (same & (ri >= ci)).astype(F32),
            (same & (ri <= ci)).astype(F32), same.astype(F32))


def _gates(ps, gparams):
    t = ps.shape[0]
    nb = t // LANE

    def body(ps_ref, gp_ref, out_ref, run_ref, tot_ref, runt_ref):
        p = ps_ref[...]
        is_b, is_a, is_f = _gate_lanes(p.shape)
        z = p + gp_ref[0:1, :]
        neg_exp_a = -jnp.exp(gp_ref[1:2, :])
        glog = neg_exp_a * _softplus(z)
        logf = -_softplus(-z)
        out_ref[...] = jnp.where(is_b, _sigmoid(p), jnp.where(is_a, glog, jnp.where(is_f, logf, 0.0)))
        tril, _, tril_c, _, same_c = _block_masks()
        off = jnp.zeros((1, LANE), F32)
        for b in range(nb):
            rows = slice(b * LANE, (b + 1) * LANE)
            blk = out_ref[rows, :]
            ga = jnp.where(is_a[:LANE], blk, 0.0)
            fb = _hdot(tril, jnp.where(is_f[:LANE], blk, 0.0)) + off
            run = fb + _hdot(tril_c, ga)
            run_ref[rows, :] = run
            runt_ref[:, rows] = run.T
            tot_ref[rows, :] = _hdot(same_c, ga)
            off = fb[LANE - 1:LANE, :]

    return pl.pallas_call(
        body, name="gates",
        out_shape=[jax.ShapeDtypeStruct((t, LANE), F32)] * 3 + [jax.ShapeDtypeStruct((LANE, t), F32)],
        compiler_params=_params(),
    )(ps, gparams)


def _gates_bwd(ps, gparams, dcol_g, dtot_g, dcol_f, drow_g, drow_f):
    t = ps.shape[0]
    nb = t // LANE

    def body(ps_ref, gp_ref, dcg_ref, dtg_ref, dcf_ref, drg_ref, drf_ref, dps_ref, sums_ref, dl_ref, d0_ref, tr_ref):
        p = ps_ref[...]
        is_b, is_a, is_f = _gate_lanes(p.shape)
        _, triu, _, triu_c, same_c = _block_masks()
        tr_ref[...] = jnp.zeros_like(tr_ref)
        off = jnp.zeros((1, LANE), F32)
        for b in reversed(range(nb)):
            rows = slice(b * LANE, (b + 1) * LANE)
            tr_ref[HEADS:2 * HEADS, :] = drg_ref[:, rows]
            tr_ref[2 * HEADS:3 * HEADS, :] = drf_ref[:, rows]
            d = dcg_ref[rows, :] + dcf_ref[rows, :] + tr_ref[...].T
            d0_ref[rows, :] = d
            dlf = _hdot(triu, jnp.where(is_f[:LANE], d, 0.0)) + off
            dla = (_hdot(triu_c, jnp.where(is_a[:LANE], d, 0.0))
                   + _hdot(same_c, jnp.where(is_a[:LANE], dtg_ref[rows, :], 0.0)))
            dl_ref[rows, :] = dlf + dla
            off = dlf[0:1, :]
        z = p + gp_ref[0:1, :]
        neg_exp_a = -jnp.exp(gp_ref[1:2, :])
        sb = _sigmoid(p)
        glog = neg_exp_a * _softplus(z)
        dl = dl_ref[...]
        dp = jnp.where(is_b, d0_ref[...] * sb * (1.0 - sb),
                       jnp.where(is_a, dl * neg_exp_a * _sigmoid(z), jnp.where(is_f, dl * _sigmoid(-z), 0.0)))
        dps_ref[...] = dp
        s_a = jnp.sum(jnp.where(is_a, dl * glog, 0.0), axis=0, keepdims=True)
        s_p = jnp.sum(jnp.where(is_b, 0.0, dp), axis=0, keepdims=True)
        row = lax.broadcasted_iota(jnp.int32, (8, LANE), 0)
        from_a = pltpu.roll(jnp.where(row == 0, s_a, jnp.where(row == 1, s_p, 0.0)), LANE - HEADS, 1)
        from_f = pltpu.roll(jnp.where(row == 2, s_p, 0.0), LANE - 2 * HEADS, 1)
        lane = lax.broadcasted_iota(jnp.int32, (8, LANE), 1)
        sums_ref[...] = jnp.where(lane < HEADS, from_a + from_f, 0.0)

    return pl.pallas_call(
        body, name="gates_bwd",
        out_shape=[jax.ShapeDtypeStruct((t, LANE), F32), jax.ShapeDtypeStruct((8, LANE), F32)],
        scratch_shapes=[pltpu.VMEM((t, LANE), F32), pltpu.VMEM((t, LANE), F32), pltpu.VMEM((LANE, LANE), F32)],
        compiler_params=_params(),
    )(ps, gparams, dcol_g, dtot_g, dcol_f, drow_g, drow_f)


def _conv(xv, w):
    acc = w[3:4, :] * xv
    for s in range(1, 4):
        acc += w[3 - s:4 - s, :] * _shift_down(xv, s)
    return acc


PREP_ROWS = 256
HALO = 8
PREP_UNROLL = 2


def _tile_loop(n, tile, init):
    if n % PREP_UNROLL:
        return lax.fori_loop(0, n, tile, init)

    def trip(g, carry):
        for u in range(PREP_UNROLL):
            carry = tile(g * PREP_UNROLL + u, carry)
        return carry

    return lax.fori_loop(0, n // PREP_UNROLL, trip, init)


def _tile_rows(r):
    return pl.ds(pl.multiple_of(r * PREP_ROWS, PREP_ROWS), PREP_ROWS)


def _gdn_prep(pm, conv_w):
    t = pm.shape[0]
    nj = WIDTH // LANE
    win = PREP_ROWS + HALO

    def body(x_ref, w_ref, o_ref, xp_ref):
        kind = pl.program_id(0)
        xp_ref[0:HALO, :] = jnp.zeros((HALO, LANE), F32)
        xp_ref[HALO:, :] = x_ref[...]
        w = w_ref[...]
        hs = _head_sum_matrix()

        def tile(r, _):
            xw = xp_ref[pl.ds(pl.multiple_of(r * PREP_ROWS, PREP_ROWS), win), :]
            acc = _conv(xw, w)[HALO:]
            y = acc * _sigmoid(acc)
            out = jnp.where(kind < 2, y * lax.rsqrt(_group_sum(y * y, hs) + EPS), y)
            o_ref[0, 0, _tile_rows(r), :] = out[:, :DH]
            o_ref[0, 1, _tile_rows(r), :] = out[:, DH:]
            return 0

        _tile_loop(t // PREP_ROWS, tile, 0)

    return pl.pallas_call(
        body, name="gdn_prep", grid=(3, nj),
        in_specs=[pl.BlockSpec((t, LANE), lambda i, j: (0, i * nj + j)),
                  pl.BlockSpec((4, LANE), lambda i, j: (0, i * nj + j))],
        out_specs=pl.BlockSpec((1, 2, t, DH), lambda i, j: (i, j, 0, 0)),
        out_shape=jax.ShapeDtypeStruct((3, HEADS, t, DH), F32),
        scratch_shapes=[pltpu.VMEM((t + HALO, LANE), F32)],
        compiler_params=_params(("arbitrary", "arbitrary")),
    )(pm, conv_w)


def _gdn_prep_bwd(pm, conv_w, dqkv):
    t = pm.shape[0]
    nj = WIDTH // LANE
    win = PREP_ROWS + 2 * HALO

    def body(x_ref, w_ref, d_ref, dx_ref, dw_ref, xp_ref, dp_ref):
        kind = pl.program_id(0)
        zeros = jnp.zeros((HALO, LANE), F32)
        for ref in (xp_ref, dp_ref):
            ref[0:HALO, :] = zeros
            ref[HALO + t:, :] = zeros
        xp_ref[HALO:HALO + t, :] = x_ref[...]
        dp_ref[HALO:HALO + t, 0:DH] = d_ref[0, 0]
        dp_ref[HALO:HALO + t, DH:] = d_ref[0, 1]
        w = w_ref[...]
        hs = _head_sum_matrix()
        rows = lax.broadcasted_iota(jnp.int32, (win, LANE), 0)
        in_tile = (rows >= HALO) & (rows < HALO + PREP_ROWS)

        def tile(r, dw):
            start = pl.multiple_of(r * PREP_ROWS, PREP_ROWS)
            xw = xp_ref[pl.ds(start, win), :]
            d = dp_ref[pl.ds(start, win), :]
            acc = _conv(xw, w)
            sg = _sigmoid(acc)
            y = acc * sg
            rn = lax.rsqrt(_group_sum(y * y, hs) + EPS)
            yn = y * rn
            dy = jnp.where(kind < 2, rn * (d - yn * _group_sum(d * yn, hs)), d)
            dacc = dy * sg * (1.0 + acc * (1.0 - sg))
            dx = w[3:4, :] * dacc
            for s in range(1, 4):
                dx += w[3 - s:4 - s, :] * _shift_up(dacc, s)
            dx_ref[_tile_rows(r), :] = dx[HALO:HALO + PREP_ROWS]
            dm = jnp.where(in_tile, dacc, 0.0)
            return tuple(dw[i] + jnp.sum(dm * (xw if i == 3 else _shift_down(xw, 3 - i)), axis=0, keepdims=True)
                         for i in range(4))

        dw = _tile_loop(t // PREP_ROWS, tile, tuple(jnp.zeros((1, LANE), F32) for _ in range(4)))
        for i in range(4):
            dw_ref[i:i + 1, :] = dw[i]

    return pl.pallas_call(
        body, name="gdn_prep_bwd", grid=(3, nj),
        in_specs=[pl.BlockSpec((t, LANE), lambda i, j: (0, i * nj + j)),
                  pl.BlockSpec((4, LANE), lambda i, j: (0, i * nj + j)),
                  pl.BlockSpec((1, 2, t, DH), lambda i, j: (i, j, 0, 0))],
        out_specs=[pl.BlockSpec((t, LANE), lambda i, j: (0, i * nj + j)),
                   pl.BlockSpec((4, LANE), lambda i, j: (0, i * nj + j))],
        out_shape=[jax.ShapeDtypeStruct((t, 3 * WIDTH), F32), jax.ShapeDtypeStruct((4, 3 * WIDTH), F32)],
        scratch_shapes=[pltpu.VMEM((t + 2 * HALO, LANE), F32), pltpu.VMEM((t + 2 * HALO, LANE), F32)],
        compiler_params=_params(("arbitrary", "arbitrary")),
    )(pm, conv_w, dqkv)


FOX_COL0 = 4 * WIDTH // LANE


def _fox_prep(pm, nw):
    t = pm.shape[0]
    nj = WIDTH // LANE

    def body(x_ref, w_ref, o_ref):
        kind = pl.program_id(0)
        hs = _head_sum_matrix()
        wk = w_ref[pl.ds(kind, 1), :]

        def tile(r, _):
            xv = x_ref[_tile_rows(r), :]
            ms = _group_sum(xv * xv, hs) * (1.0 / DH)
            out = jnp.where(kind < 2, xv * lax.rsqrt(ms + EPS) * wk, xv)
            o_ref[0, 0, _tile_rows(r), :] = out[:, :DH]
            o_ref[0, 1, _tile_rows(r), :] = out[:, DH:]
            return 0

        _tile_loop(t // PREP_ROWS, tile, 0)

    return pl.pallas_call(
        body, name="fox_prep", grid=(3, nj),
        in_specs=[pl.BlockSpec((t, LANE), lambda i, j: (0, FOX_COL0 + i * nj + j)),
                  pl.BlockSpec((3, LANE), lambda i, j: (0, 0))],
        out_specs=pl.BlockSpec((1, 2, t, DH), lambda i, j: (i, j, 0, 0)),
        out_shape=jax.ShapeDtypeStruct((3, HEADS, t, DH), F32),
        compiler_params=_params(("arbitrary", "arbitrary")),
    )(pm, nw)


def _fox_prep_bwd(pm, nw, dqkv):
    t = pm.shape[0]
    nj = WIDTH // LANE

    def body(x_ref, w_ref, d_ref, dx_ref, dw_ref):
        kind = pl.program_id(0)
        hs = _head_sum_matrix()
        wk = w_ref[pl.ds(kind, 1), :]

        def tile(r, dw):
            xv = x_ref[_tile_rows(r), :]
            rn = lax.rsqrt(_group_sum(xv * xv, hs) * (1.0 / DH) + EPS)
            xn = xv * rn
            d = jnp.concatenate([d_ref[0, 0, _tile_rows(r), :], d_ref[0, 1, _tile_rows(r), :]], axis=1)
            g = d * wk
            dxn = rn * (g - xn * _group_sum(g * xn, hs) * (1.0 / DH))
            dx_ref[_tile_rows(r), :] = jnp.where(kind < 2, dxn, d)
            return dw + jnp.sum(d * xn, axis=0, keepdims=True)

        dw_ref[0, 0] = _tile_loop(t // PREP_ROWS, tile, jnp.zeros((1, LANE), F32))

    return pl.pallas_call(
        body, name="fox_prep_bwd", grid=(3, nj),
        in_specs=[pl.BlockSpec((t, LANE), lambda i, j: (0, FOX_COL0 + i * nj + j)),
                  pl.BlockSpec((3, LANE), lambda i, j: (0, 0)),
                  pl.BlockSpec((1, 2, t, DH), lambda i, j: (i, j, 0, 0))],
        out_specs=[pl.BlockSpec((t, LANE), lambda i, j: (0, i * nj + j)),
                   pl.BlockSpec((1, 1, 1, LANE), lambda i, j: (i, j, 0, 0))],
        out_shape=[jax.ShapeDtypeStruct((t, 3 * WIDTH), F32), jax.ShapeDtypeStruct((3, nj, 1, LANE), F32)],
        compiler_params=_params(("arbitrary", "arbitrary")),
    )(pm, nw, dqkv)


SC = 256
CPS = SC // CHUNK
GDN_HP_FWD = 4
GDN_HP_BWD = 2
Q_SCALE = DH ** -0.5


def _sc_masks():
    ri = lax.broadcasted_iota(jnp.int32, (SC, SC), 0)
    ci = lax.broadcasted_iota(jnp.int32, (SC, SC), 1)
    same = (ri // CHUNK) == (ci // CHUNK)
    return same & (ri >= ci), same & (ri > ci), ri == ci


def _unit_lower_inverses(ms, eye):
    invs = [jnp.where(eye, 1.0, 0.0) + m for m in ms]
    for _ in range(5):
        ms = [_dot(m, m) for m in ms]
        invs = [inv + _dot(inv, m) for inv, m in zip(invs, ms)]
    return invs


def _lane_col(blk, lane_idx):
    lane = lax.broadcasted_iota(jnp.int32, blk.shape, 1)
    return jnp.sum(jnp.where(lane == lane_idx, blk, 0.0), axis=1, keepdims=True)


def _to_lane(col, lane_idx):
    lane = lax.broadcasted_iota(jnp.int32, (col.shape[0], LANE), 1)
    return jnp.where(lane == lane_idx, col, 0.0)


def _gdn_columns(gates_ref, run_ref, tot_ref, runt_ref, rows, h):
    return (_lane_col(gates_ref[rows, :], h), _lane_col(run_ref[rows, :], HEADS + h),
            _lane_col(tot_ref[rows, :], HEADS + h), runt_ref[pl.ds(h, 1), rows])


def _gdn_local(q, k, beta, gc, gl, grow, causal):
    decay = jnp.exp(jnp.where(causal, gc - grow, -1e30))
    egc = jnp.exp(gc)
    ekd = jnp.exp(gl - gc)
    qs = q * Q_SCALE
    kb = k * beta
    kk = _dot_nt(kb, k)
    qk = _dot_nt(qs, k)
    return beta, gl, decay, egc, ekd, qs, kb, kk, qk, jnp.where(causal, qk * decay, 0.0)


def _chunk_rows(c):
    return pl.ds(pl.multiple_of(c * CHUNK, CHUNK), CHUNK)


def _sc_rows(b):
    return pl.ds(pl.multiple_of(b * SC, SC), SC)


def _gdn_fwd(qkv, gates, run, tot, run_t):
    t = qkv.shape[2]
    nc = t // CHUNK
    nsc = t // SC

    hp_n = GDN_HP_FWD
    heads = range(hp_n)

    def body(qkv_ref, gates_ref, run_ref, tot_ref, runt_ref, o_ref, st_ref, inv_ref, kc_s, qc_s, g_s, au_s):
        hp = pl.program_id(0)
        causal, strict, eye = _sc_masks()

        def local(b, _):
            rows = _sc_rows(b)
            loc = [_gdn_local(qkv_ref[0, hh, rows, :], qkv_ref[1, hh, rows, :],
                              *_gdn_columns(gates_ref, run_ref, tot_ref, runt_ref, rows, hp * hp_n + hh), causal)
                   for hh in heads]
            invs = _unit_lower_inverses([-jnp.where(strict, l[7] * l[2], 0.0) for l in loc], eye)
            us, ws = [], []
            for hh in heads:
                beta, _, _, egc, _, _, kb, _, _, _ = loc[hh]
                inv_ref[hh, rows, :] = invs[hh].astype(inv_ref.dtype)
                us.append(_dot(invs[hh], qkv_ref[2, hh, rows, :] * beta))
                ws.append(_dot(invs[hh], kb * egc))
            for hh in heads:
                _, _, _, egc, ekd, qs, _, _, _, attn = loc[hh]
                g_s[hh, rows, :] = qs * egc - _dot(attn, ws[hh])
                au_s[hh, rows, :] = _dot(attn, us[hh])
                kd = qkv_ref[1, hh, rows, :] * ekd
                for j in range(CPS):
                    sl = slice(j * CHUNK, (j + 1) * CHUNK)
                    kc_s[hh, b * CPS + j] = _dot_tn(kd[sl], ws[hh][sl])
                    qc_s[hh, b * CPS + j] = _dot_tn(kd[sl], us[hh][sl])
            return 0

        lax.fori_loop(0, nsc, local, 0)

        def step(c, states):
            rows = _chunk_rows(c)
            tot_row = tot_ref[pl.ds(c * CHUNK, 1), :]
            new = []
            for hh in heads:
                s = states[hh]
                st_ref[hh, c] = s
                o_ref[hh, rows, :] = _dot(g_s[hh, rows, :], s) + au_s[hh, rows, :]
                egl = jnp.exp(_lane_col(tot_row, HEADS + hp * hp_n + hh))
                new.append(egl * s - _dot(kc_s[hh, c], s) + qc_s[hh, c])
            return tuple(new)

        lax.fori_loop(0, nc, step, tuple(jnp.zeros((DH, DH), F32) for _ in heads))

    whole = pl.BlockSpec((t, LANE), lambda h: (0, 0))
    once = dict(pipeline_mode=pl.Buffered(1))
    return pl.pallas_call(
        body, name="gdn_fwd", grid=(HEADS // hp_n,),
        in_specs=[pl.BlockSpec((3, hp_n, t, DH), lambda h: (0, h, 0, 0), **once), whole, whole, whole,
                  pl.BlockSpec((HEADS, t), lambda h: (1, 0))],
        out_specs=[pl.BlockSpec((hp_n, t, DH), lambda h: (h, 0, 0), **once),
                   pl.BlockSpec((hp_n, nc, DH, DH), lambda h: (h, 0, 0, 0), **once),
                   pl.BlockSpec((hp_n, t, SC), lambda h: (h, 0, 0), **once)],
        out_shape=[jax.ShapeDtypeStruct((HEADS, t, DH), F32), jax.ShapeDtypeStruct((HEADS, nc, DH, DH), F32),
                   jax.ShapeDtypeStruct((HEADS, t, SC), MXU)],
        scratch_shapes=[pltpu.VMEM((hp_n, nc, DH, DH), F32), pltpu.VMEM((hp_n, nc, DH, DH), F32),
                        pltpu.VMEM((hp_n, t, DH), F32), pltpu.VMEM((hp_n, t, DH), F32)],
        compiler_params=_params(("arbitrary",)),
    )(qkv, gates, run, tot, run_t)


def _gdn_bwd(qkv, gates, run, tot, run_t, inv, states, do):
    t = qkv.shape[2]
    nc = t // CHUNK
    nsc = t // SC

    hp_n = GDN_HP_BWD
    heads = range(hp_n)

    def body(qkv_ref, gates_ref, run_ref, tot_ref, runt_ref, inv_ref, st_ref, do_ref,
             dqkv_ref, dcol_ref, dtot_ref, drow_ref, u_s, w_s, kc_s, h_s, dsn_s):
        hp = pl.program_id(0)
        causal, strict, _ = _sc_masks()

        @pl.when(hp == 0)
        def _():
            dcol_ref[...] = jnp.zeros_like(dcol_ref)
            dtot_ref[...] = jnp.zeros_like(dtot_ref)

        def local_of(hh, rows):
            return _gdn_local(qkv_ref[0, hh, rows, :], qkv_ref[1, hh, rows, :],
                              *_gdn_columns(gates_ref, run_ref, tot_ref, runt_ref, rows, hp * hp_n + hh), causal)

        def local(b, _):
            rows = _sc_rows(b)
            loc = [local_of(hh, rows) for hh in heads]
            us, ws = [], []
            for hh in heads:
                beta, _, _, egc, _, _, kb, _, _, _ = loc[hh]
                inv_b = inv_ref[hh, rows, :]
                us.append(_dot(inv_b, qkv_ref[2, hh, rows, :] * beta))
                ws.append(_dot(inv_b, kb * egc))
            gs = [loc[hh][5] * loc[hh][3] - _dot(loc[hh][9], ws[hh]) for hh in heads]
            for hh in heads:
                u_s[hh, rows, :] = us[hh]
                w_s[hh, rows, :] = ws[hh]
                kd = qkv_ref[1, hh, rows, :] * loc[hh][4]
                dout = do_ref[hh, rows, :]
                for j in range(CPS):
                    sl = slice(j * CHUNK, (j + 1) * CHUNK)
                    kc_s[hh, b * CPS + j] = _dot_tn(kd[sl], ws[hh][sl])
                    h_s[hh, b * CPS + j] = _dot_tn(gs[hh][sl], dout[sl])
            return 0

        lax.fori_loop(0, nsc, local, 0)

        def step(i, dss):
            c = nc - 1 - i
            tot_row = tot_ref[pl.ds(c * CHUNK, 1), :]
            new = []
            for hh in heads:
                ds = dss[hh]
                dsn_s[hh, c] = ds
                egl = jnp.exp(_lane_col(tot_row, HEADS + hp * hp_n + hh))
                new.append(egl * ds - _dot_tn(kc_s[hh, c], ds) + h_s[hh, c])
            return tuple(new)

        lax.fori_loop(0, nc, step, tuple(jnp.zeros((DH, DH), F32) for _ in heads))

        def back(b, _):
            rows = _sc_rows(b)
            first = lax.broadcasted_iota(jnp.int32, (CHUNK, 1), 0) == 0
            loc = [local_of(hh, rows) for hh in heads]
            mid = []
            for hh in heads:
                beta, gl, decay, egc, ekd, qs, kb, kk, qk, attn = loc[hh]
                u, w = u_s[hh, rows, :], w_s[hh, rows, :]
                kd = qkv_ref[1, hh, rows, :] * ekd
                dout = do_ref[hh, rows, :]
                dg_p, dkd_p, dw_p, du_p, dgl_p = [], [], [], [], []
                for j in range(CPS):
                    sl = slice(j * CHUNK, (j + 1) * CHUNK)
                    s = st_ref[hh, b * CPS + j]
                    dsn = dsn_s[hh, b * CPS + j]
                    dkc = -_dot_nt(dsn, s)
                    dg_p.append(_dot_nt(dout[sl], s))
                    dkd_p.append(_dot_nt(w[sl], dkc) + _dot_nt(u[sl], dsn))
                    dw_p.append(_dot(kd[sl], dkc))
                    du_p.append(_dot(kd[sl], dsn))
                    degl = jnp.sum(jnp.sum(s * dsn, axis=1, keepdims=True), axis=0, keepdims=True)
                    dgl_p.append(jnp.where(first, degl * jnp.exp(gl[j * CHUNK:j * CHUNK + 1, :]), 0.0))
                dg, dkd = jnp.concatenate(dg_p, axis=0), jnp.concatenate(dkd_p, axis=0)
                da = jnp.where(causal, _dot_nt(dout, u) - _dot_nt(dg, w), 0.0)
                at = _dot_tn(attn, jnp.concatenate([dout, dg], axis=1))
                du = at[:, :DH] + jnp.concatenate(du_p, axis=0)
                dw = jnp.concatenate(dw_p, axis=0) - at[:, DH:]
                mid.append((kd, dg, dkd, da, du, dw, jnp.concatenate(dgl_p, axis=0)))
            inv_ts = [inv_ref[hh, rows, :].astype(F32).T for hh in heads]
            its = [_dot(inv_ts[hh], jnp.concatenate([mid[hh][4], mid[hh][5]], axis=1)) for hh in heads]
            dinvs = []
            for hh in heads:
                beta, _, _, egc, _, _, kb, _, _, _ = loc[hh]
                dinvs.append(_dot_nt(mid[hh][4], qkv_ref[2, hh, rows, :] * beta) + _dot_nt(mid[hh][5], kb * egc))
            half = [_dot(inv_ts[hh], dinvs[hh]) for hh in heads]
            dls = [jnp.where(strict, -_dot(half[hh], inv_ts[hh]), 0.0) for hh in heads]
            for hh in heads:
                head = hp * hp_n + hh
                beta, gl, decay, egc, ekd, qs, kb, kk, qk, attn = loc[hh]
                kd, dg, dkd, da, du, dw, dgl_first = mid[hh]
                k, v = qkv_ref[1, hh, rows, :], qkv_ref[2, hh, rows, :]
                dvb, dkbe = its[hh][:, :DH], its[hh][:, DH:]
                dl = dls[hh]
                dlogd = (dl * kk + da * qk) * decay
                dkk = dl * decay
                dqk = da * decay
                dkb = _dot(dkk, k) + dkbe * egc
                dqs = _dot(dqk, k) + dg * egc
                dk = (_dot_tn(jnp.concatenate([dkk, dqk], axis=0), jnp.concatenate([kb, qs], axis=0))
                      + dkd * ekd + dkb * beta)
                dkd_kd = jnp.sum(dkd * kd, axis=1, keepdims=True)
                dgc = (jnp.sum(dlogd, axis=1, keepdims=True) + jnp.sum(dg * qs, axis=1, keepdims=True) * egc
                       + jnp.sum(dkbe * (kb * egc), axis=1, keepdims=True) - dkd_kd)
                dbeta = jnp.sum(dkb * k, axis=1, keepdims=True) + jnp.sum(dvb * v, axis=1, keepdims=True)
                dqkv_ref[0, hh, rows, :] = dqs * Q_SCALE
                dqkv_ref[1, hh, rows, :] = dk
                dqkv_ref[2, hh, rows, :] = dvb * beta
                dcol_ref[rows, :] += _to_lane(dbeta, head) + _to_lane(dgc, HEADS + head)
                dtot_ref[rows, :] += _to_lane(dkd_kd + dgl_first, HEADS + head)
                drow_ref[pl.ds(head, 1), rows] = -jnp.sum(dlogd, axis=0, keepdims=True)
            return 0

        lax.fori_loop(0, nsc, back, 0)

    whole = pl.BlockSpec((t, LANE), lambda h: (0, 0))
    rowspec = pl.BlockSpec((HEADS, t), lambda h: (0, 0))
    sq = pltpu.VMEM((hp_n, nc, DH, DH), F32)
    per_head = pltpu.VMEM((hp_n, t, DH), F32)
    return pl.pallas_call(
        body, name="gdn_bwd", grid=(HEADS // hp_n,),
        in_specs=[pl.BlockSpec((3, hp_n, t, DH), lambda h: (0, h, 0, 0)), whole, whole, whole,
                  pl.BlockSpec((HEADS, t), lambda h: (1, 0)),
                  pl.BlockSpec((hp_n, t, SC), lambda h: (h, 0, 0)), pl.BlockSpec((hp_n, nc, DH, DH), lambda h: (h, 0, 0, 0)),
                  pl.BlockSpec((hp_n, t, DH), lambda h: (h, 0, 0))],
        out_specs=[pl.BlockSpec((3, hp_n, t, DH), lambda h: (0, h, 0, 0)), whole, whole, rowspec],
        out_shape=[jax.ShapeDtypeStruct((3, HEADS, t, DH), F32), jax.ShapeDtypeStruct((t, LANE), F32),
                   jax.ShapeDtypeStruct((t, LANE), F32), jax.ShapeDtypeStruct((HEADS, t), F32)],
        scratch_shapes=[per_head, per_head, sq, sq, sq],
        compiler_params=_params(("arbitrary",)),
    )(qkv, gates, run, tot, run_t, inv, states, do)


FOX_HP_FWD = 4
FOX_HP_BWD = 4


def _wide_t(a):
    r = a.shape[0]
    return jnp.concatenate([a, jnp.zeros((r, LANE - DH), F32)], axis=1).T[:DH]


def _tall_t(a):
    r = a.shape[1]
    return jnp.concatenate([a, jnp.zeros((LANE - DH, r), F32)], axis=0).T[:, :DH]


def _key_side_f(run_blk, head, qb):
    col = jnp.broadcast_to(_lane_col(run_blk, 2 * HEADS + head), (run_blk.shape[0], LANE))
    return jnp.concatenate([col] * (qb // LANE), axis=1)


def _diag_mask(qb):
    return lax.broadcasted_iota(jnp.int32, (qb, qb), 0) <= lax.broadcasted_iota(jnp.int32, (qb, qb), 1)


def _fox_fwd(qkv, run, run_t, carry=None):
    t = qkv.shape[2]
    qb = min(QB, t)
    nq = t // qb
    hp_n = FOX_HP_FWD
    c_in, c_in_specs, c_out_shape, c_out_specs, c_sem = _carry_operands(*carry) if carry else ([], [], [], [], None)
    n_c = len(c_in)

    def body(*refs):
        q_ref, k_ref, v_ref, run_ref, runt_ref = refs[:5]
        o_ref, lse_ref = refs[5 + n_c:7 + n_c]
        kb_s, vt_s = refs[7 + n_c + len(c_out_shape):9 + n_c + len(c_out_shape)]
        hp = pl.program_id(0)
        i = pl.program_id(1)

        if carry:
            _carry_step(carry[0], refs[5:5 + n_c], refs[7 + n_c], refs[8 + n_c], refs[-1], hp * nq + i,
                        (HEADS // hp_n) * nq)

        @pl.when(i == 0)
        def _():
            for hh in range(hp_n):
                kb_s[hh] = k_ref[0, hh].astype(MXU)
                for b in range(nq):
                    rows = slice(b * qb, (b + 1) * qb)
                    vt_s[hh, :, rows] = _wide_t(v_ref[0, hh, rows, :]).astype(MXU)

        qrows = pl.ds(pl.multiple_of(i * qb, qb), qb)
        qs = [(q_ref[0, hh] * Q_SCALE).astype(MXU) for hh in range(hp_n)]
        fq = [runt_ref[pl.ds(hp * hp_n + hh, 1), qrows] for hh in range(hp_n)]

        def block_rows(j):
            return pl.ds(pl.multiple_of(j * qb, qb), qb)

        def scores(j):
            return tuple(_dot_nt(kb_s[hh, block_rows(j), :], qs[hh]) for hh in range(hp_n))

        def absorb(j, raw, state, diagonal):
            rows = block_rows(j)
            run_blk = run_ref[rows, :]
            stats, pv = [], []
            for hh in range(hp_n):
                m, l, _ = state[hh]
                st = raw[hh] + (fq[hh] - _key_side_f(run_blk, hp * hp_n + hh, qb))
                if diagonal:
                    st = jnp.where(_diag_mask(qb), st, -1e30)
                m_new = jnp.maximum(m, jnp.max(st, axis=0, keepdims=True))
                p = jnp.exp(st - m_new)
                alpha = jnp.exp(m - m_new)
                stats.append((m_new, alpha * l + jnp.sum(p, axis=0, keepdims=True), alpha))
                pv.append(_dot(vt_s[hh, :, rows], p))
            return tuple((stats[hh][0], stats[hh][1], stats[hh][2] * state[hh][2] + pv[hh]) for hh in range(hp_n))

        def kstep(j, carry):
            state, raw = carry
            ahead = scores(j + 1)
            return absorb(j, raw, state, False), ahead

        init = tuple((jnp.full((1, qb), -1e30, F32), jnp.zeros((1, qb), F32), jnp.zeros((DH, qb), F32))
                     for _ in range(hp_n))
        state, raw = lax.fori_loop(0, i, kstep, (init, scores(0)))
        state = absorb(i, raw, state, True)
        for hh in range(hp_n):
            m, l, acc = state[hh]
            o_ref[hh] = _tall_t(acc / l)
            lse_ref[pl.ds(hp * hp_n + hh, 1), qrows] = m + jnp.log(l)

    out = pl.pallas_call(
        body, name="fox_fwd", grid=(HEADS // hp_n, nq),
        in_specs=[pl.BlockSpec((1, hp_n, qb, DH), lambda h, i: (0, h, i, 0)),
                  pl.BlockSpec((1, hp_n, t, DH), lambda h, i: (1, h, 0, 0)),
                  pl.BlockSpec((1, hp_n, t, DH), lambda h, i: (2, h, 0, 0)),
                  pl.BlockSpec((t, LANE), lambda h, i: (0, 0)),
                  pl.BlockSpec((HEADS, t), lambda h, i: (2, 0))] + c_in_specs,
        out_specs=[pl.BlockSpec((hp_n, qb, DH), lambda h, i: (h, i, 0)),
                   pl.BlockSpec((HEADS, t), lambda h, i: (0, 0))] + c_out_specs,
        out_shape=[jax.ShapeDtypeStruct((HEADS, t, DH), F32), jax.ShapeDtypeStruct((HEADS, t), F32)] + c_out_shape,
        input_output_aliases={5 + j: 4 + j for j in range(n_c)},
        scratch_shapes=[pltpu.VMEM((hp_n, t, DH), MXU), pltpu.VMEM((hp_n, DH, t), MXU)] + ([c_sem] if carry else []),
        compiler_params=_params(("arbitrary", "arbitrary")),
    )(qkv, qkv, qkv, run, run_t, *c_in)
    return (out[0], out[1], _carry_state(out[2:])) if carry else (out[0], out[1])


def _fox_bwd(qkv, run, run_t, o, lse, do, carry=None):
    t = qkv.shape[2]
    qb = min(QB, t)
    nq = t // qb
    hp_n = FOX_HP_BWD
    c_in, c_in_specs, c_out_shape, c_out_specs, c_sem = _carry_operands(*carry) if carry else ([], [], [], [], None)
    n_c = len(c_in)

    def body(*refs):
        q_ref, k_ref, v_ref, run_ref, runt_ref, o_ref, lse_ref, do_ref = refs[:8]
        dqkv_ref, dcol_ref, drow_ref = refs[8 + n_c:11 + n_c]
        dqt_s = refs[11 + n_c + len(c_out_shape)]
        hp = pl.program_id(0)
        j = pl.program_id(1)

        if carry:
            _carry_step(carry[0], refs[8:8 + n_c], refs[11 + n_c], refs[12 + n_c], refs[-1], hp * nq + j,
                        (HEADS // hp_n) * nq)

        @pl.when(j == 0)
        def _():
            dqt_s[...] = jnp.zeros_like(dqt_s)

        @pl.when((j == 0) & (hp == 0))
        def _():
            dcol_ref[...] = jnp.zeros_like(dcol_ref)
            drow_ref[...] = jnp.zeros_like(drow_ref)

        krows = pl.ds(pl.multiple_of(j * qb, qb), qb)
        run_blk = run_ref[krows, :]
        ones8 = jnp.ones((8, DH), MXU)
        kb, kt, vb, fk = [], [], [], []
        for hh in range(hp_n):
            kf = k_ref[0, hh]
            kb.append(kf.astype(MXU))
            kt.append(_wide_t(kf).astype(MXU))
            vb.append(v_ref[0, hh].astype(MXU))
            fk.append(_key_side_f(run_blk, hp * hp_n + hh, qb))

        def block_rows(i):
            return pl.ds(pl.multiple_of(i * qb, qb), qb)

        def products(i):
            rows = block_rows(i)
            out = []
            for hh in range(hp_n):
                dout = do_ref[hh, rows, :]
                x = dout * o_ref[hh, rows, :]
                x_hi = x.astype(MXU)
                out.append((_dot_nt(kb[hh], q_ref[0, hh, rows, :] * Q_SCALE), _dot_nt(vb[hh], dout),
                            (_dot_nt(ones8, x_hi) + _dot_nt(ones8, x - x_hi.astype(F32)))[0:1, :]))
            return tuple(out)

        def absorb(i, prods, acc, diagonal):
            rows = block_rows(i)
            pieces = []
            for hh in range(hp_n):
                head = hp * hp_n + hh
                raw, dpt, drow = prods[hh]
                off = runt_ref[pl.ds(head, 1), rows] - lse_ref[pl.ds(head, 1), rows]
                st = raw + (off - fk[hh])
                if diagonal:
                    st = jnp.where(_diag_mask(qb), st, -1e30)
                pt = jnp.exp(st)
                dst = pt * (dpt - drow)
                drow_ref[pl.ds(head, 1), rows] += jnp.sum(dst, axis=0, keepdims=True)
                folded = dst[:, 0:LANE]
                for c in range(1, qb // LANE):
                    folded = folded + dst[:, c * LANE:(c + 1) * LANE]
                pieces.append((_dot(dst, q_ref[0, hh, rows, :] * Q_SCALE), _dot(pt, do_ref[hh, rows, :]),
                               _dot(kt[hh], dst), folded))
            out = []
            for hh in range(hp_n):
                dqt_s[hh, :, rows] += pieces[hh][2]
                out.append((acc[hh][0] + pieces[hh][0], acc[hh][1] + pieces[hh][1], acc[hh][2] + pieces[hh][3]))
            return tuple(out)

        def qstep(i, carry):
            acc, prods = carry
            ahead = products(jnp.minimum(i + 1, nq - 1))
            return absorb(i, prods, acc, False), ahead

        init = tuple((jnp.zeros((qb, DH), F32), jnp.zeros((qb, DH), F32), jnp.zeros((qb, LANE), F32))
                     for _ in range(hp_n))
        first = products(j)
        ahead = products(jnp.minimum(j + 1, nq - 1))
        acc = absorb(j, first, init, True)
        acc, _ = lax.fori_loop(j + 1, nq, qstep, (acc, ahead))
        for hh in range(hp_n):
            dk, dv, ds_sum = acc[hh]
            dqkv_ref[1, hh, krows, :] = dk
            dqkv_ref[2, hh, krows, :] = dv
            dcol_ref[krows, :] += _to_lane(-jnp.sum(ds_sum, axis=1, keepdims=True), 2 * HEADS + hp * hp_n + hh)

        @pl.when(j == nq - 1)
        def _():
            for hh in range(hp_n):
                for b in range(nq):
                    rows = slice(b * qb, (b + 1) * qb)
                    dqkv_ref[0, hh, rows, :] = _tall_t(dqt_s[hh, :, rows]) * Q_SCALE

    once = dict(pipeline_mode=pl.Buffered(1))
    full = pl.BlockSpec((hp_n, t, DH), lambda h, j: (h, 0, 0), **once)
    rows8 = pl.BlockSpec((HEADS, t), lambda h, j: (0, 0))
    out = pl.pallas_call(
        body, name="fox_bwd", grid=(HEADS // hp_n, nq),
        in_specs=[pl.BlockSpec((1, hp_n, t, DH), lambda h, j: (0, h, 0, 0), **once),
                  pl.BlockSpec((1, hp_n, qb, DH), lambda h, j: (1, h, j, 0)),
                  pl.BlockSpec((1, hp_n, qb, DH), lambda h, j: (2, h, j, 0)),
                  pl.BlockSpec((t, LANE), lambda h, j: (0, 0), **once), pl.BlockSpec((HEADS, t), lambda h, j: (2, 0)),
                  full, rows8, full] + c_in_specs,
        out_specs=[pl.BlockSpec((3, hp_n, t, DH), lambda h, j: (0, h, 0, 0), **once),
                   pl.BlockSpec((t, LANE), lambda h, j: (0, 0)), rows8] + c_out_specs,
        out_shape=[jax.ShapeDtypeStruct((3, HEADS, t, DH), F32), jax.ShapeDtypeStruct((t, LANE), F32),
                   jax.ShapeDtypeStruct((HEADS, t), F32)] + c_out_shape,
        input_output_aliases={8 + j: 5 + j for j in range(n_c)},
        scratch_shapes=[pltpu.VMEM((hp_n, DH, t), F32)] + ([c_sem] if carry else []),
        compiler_params=_params(("arbitrary", "arbitrary")),
    )(qkv, qkv, qkv, run, run_t, o, lse, do, *c_in)
    return (out[0], out[1], out[2], _carry_state(out[3:])) if carry else tuple(out)


Z_COL0 = 3 * WIDTH // LANE
FGATE_COL0 = 7 * WIDTH // LANE


def _gdn_post(o, pm, onw):
    t = pm.shape[0]

    def body(o_ref, z_ref, w_ref, m_ref):
        z = z_ref[...]
        sz = z * _sigmoid(z)
        halves = []
        for hh in range(2):
            ov = o_ref[hh]
            n = ov * lax.rsqrt(jnp.mean(ov * ov, axis=-1, keepdims=True) + EPS) * w_ref[...]
            halves.append(n * sz[:, hh * DH:(hh + 1) * DH])
        m_ref[...] = jnp.concatenate(halves, axis=1).astype(m_ref.dtype)

    return pl.pallas_call(
        body, name="gdn_post", grid=(WIDTH // LANE,),
        in_specs=[pl.BlockSpec((2, t, DH), lambda j: (j, 0, 0)), pl.BlockSpec((t, LANE), lambda j: (0, Z_COL0 + j)),
                  pl.BlockSpec((1, DH), lambda j: (0, 0))],
        out_specs=pl.BlockSpec((t, LANE), lambda j: (0, j)),
        out_shape=jax.ShapeDtypeStruct((t, WIDTH), MXU),
        compiler_params=_params(("arbitrary",)),
    )(o, pm, onw)


def _gdn_post_bwd(o, pm, onw, dmix):
    t = pm.shape[0]

    def body(o_ref, z_ref, w_ref, dm_ref, do_ref, dz_ref, dw_ref):
        @pl.when(pl.program_id(0) == 0)
        def _():
            dw_ref[...] = jnp.zeros_like(dw_ref)

        z = z_ref[...]
        sg = _sigmoid(z)
        sz = z * sg
        dsz = sg * (1.0 + z * (1.0 - sg))
        dm = dm_ref[...]
        for hh in range(2):
            cols = slice(hh * DH, (hh + 1) * DH)
            ov = o_ref[hh]
            r = lax.rsqrt(jnp.mean(ov * ov, axis=-1, keepdims=True) + EPS)
            xn = ov * r
            dmh = dm[:, cols]
            dn = dmh * sz[:, cols]
            dz_ref[:, cols] = dmh * (xn * w_ref[...]) * dsz[:, cols]
            dw_ref[...] += jnp.sum(dn * xn, axis=0, keepdims=True)
            g = dn * w_ref[...]
            do_ref[hh] = r * (g - xn * jnp.mean(g * xn, axis=-1, keepdims=True))

    return pl.pallas_call(
        body, name="gdn_post_bwd", grid=(WIDTH // LANE,),
        in_specs=[pl.BlockSpec((2, t, DH), lambda j: (j, 0, 0)), pl.BlockSpec((t, LANE), lambda j: (0, Z_COL0 + j)),
                  pl.BlockSpec((1, DH), lambda j: (0, 0)), pl.BlockSpec((t, LANE), lambda j: (0, j))],
        out_specs=[pl.BlockSpec((2, t, DH), lambda j: (j, 0, 0)), pl.BlockSpec((t, LANE), lambda j: (0, j)),
                   pl.BlockSpec((1, DH), lambda j: (0, 0))],
        out_shape=[jax.ShapeDtypeStruct((HEADS, t, DH), F32), jax.ShapeDtypeStruct((t, WIDTH), F32),
                   jax.ShapeDtypeStruct((1, DH), F32)],
        compiler_params=_params(("arbitrary",)),
    )(o, pm, onw, dmix)


def _fox_post(o, pm):
    t = pm.shape[0]

    def body(o_ref, g_ref, m_ref):
        m_ref[...] = (jnp.concatenate([o_ref[0], o_ref[1]], axis=1) * _sigmoid(g_ref[...])).astype(m_ref.dtype)

    return pl.pallas_call(
        body, name="fox_post", grid=(WIDTH // LANE,),
        in_specs=[pl.BlockSpec((2, t, DH), lambda j: (j, 0, 0)), pl.BlockSpec((t, LANE), lambda j: (0, FGATE_COL0 + j))],
        out_specs=pl.BlockSpec((t, LANE), lambda j: (0, j)),
        out_shape=jax.ShapeDtypeStruct((t, WIDTH), MXU),
        compiler_params=_params(("arbitrary",)),
    )(o, pm)


def _fox_post_bwd(o, pm, dmix):
    t = pm.shape[0]

    def body(o_ref, g_ref, dm_ref, do_ref, dg_ref):
        sg = _sigmoid(g_ref[...])
        dm = dm_ref[...]
        for hh in range(2):
            cols = slice(hh * DH, (hh + 1) * DH)
            do_ref[hh] = dm[:, cols] * sg[:, cols]
            dg_ref[:, cols] = dm[:, cols] * o_ref[hh] * (sg * (1.0 - sg))[:, cols]

    return pl.pallas_call(
        body, name="fox_post_bwd", grid=(WIDTH // LANE,),
        in_specs=[pl.BlockSpec((2, t, DH), lambda j: (j, 0, 0)), pl.BlockSpec((t, LANE), lambda j: (0, FGATE_COL0 + j)),
                  pl.BlockSpec((t, LANE), lambda j: (0, j))],
        out_specs=[pl.BlockSpec((2, t, DH), lambda j: (j, 0, 0)), pl.BlockSpec((t, LANE), lambda j: (0, j))],
        out_shape=[jax.ShapeDtypeStruct((HEADS, t, DH), F32), jax.ShapeDtypeStruct((t, WIDTH), F32)],
        compiler_params=_params(("arbitrary",)),
    )(o, pm, dmix)


def _tail(x, mixg, mixf, tgt, wo, n2w, wg_t, wu_t, wd, fw):
    t, d = x.shape
    dff = wd.shape[0]
    tb = min(TB, t)

    def body(x_ref, mg_ref, mf_ref, t_ref, wo_ref, n2_ref, wg_ref, wu_ref, wd_ref, fw_ref,
             h2_ref, act_ref, dgate_ref, dup_ref, dx3_ref, dx2_ref, dmg_ref, dmf_ref, dn2_ref, dfw_ref, loss_ref):
        @pl.when(pl.program_id(0) == 0)
        def _():
            dn2_ref[...] = jnp.zeros_like(dn2_ref)
            dfw_ref[...] = jnp.zeros_like(dfw_ref)
            loss_ref[...] = jnp.zeros_like(loss_ref)

        x2 = x_ref[...] + _dot(mg_ref[...], wo_ref[0:WIDTH, :]) + _dot(mf_ref[...], wo_ref[WIDTH:2 * WIDTH, :])
        r2 = lax.rsqrt(jnp.mean(x2 * x2, axis=-1, keepdims=True) + EPS)
        xn2 = x2 * r2
        h2 = (xn2 * n2_ref[...]).astype(MXU)
        h2_ref[...] = h2
        gate = _dot_nt(h2, wg_ref[...])
        up = _dot_nt(h2, wu_ref[...])
        sg = _sigmoid(gate)
        sl = gate * sg
        act = (sl * up).astype(MXU)
        act_ref[...] = act
        x3 = x2 + _dot(act, wd_ref[...])
        r3 = lax.rsqrt(jnp.mean(x3 * x3, axis=-1, keepdims=True) + EPS)
        xn3 = x3 * r3
        err = xn3 * fw_ref[...] - t_ref[...]
        loss_ref[...] += 0.5 * jnp.sum(jnp.mean(err * err, axis=-1, keepdims=True), axis=0, keepdims=True)
        dy = err * (1.0 / d)
        dfw_ref[...] += jnp.sum(dy * xn3, axis=0, keepdims=True)
        g3 = dy * fw_ref[...]
        dx3 = r3 * (g3 - xn3 * jnp.mean(g3 * xn3, axis=-1, keepdims=True))
        dx3_ref[...] = dx3.astype(MXU)
        dact = _dot_nt(dx3, wd_ref[...])
        dgate = (dact * up * (sg * (1.0 + gate * (1.0 - sg)))).astype(MXU)
        dup = (dact * sl).astype(MXU)
        dgate_ref[...] = dgate
        dup_ref[...] = dup
        dh2 = _dot(dgate, wg_ref[...]) + _dot(dup, wu_ref[...])
        dn2_ref[...] += jnp.sum(dh2 * xn2, axis=0, keepdims=True)
        g2 = dh2 * n2_ref[...]
        dx2 = dx3 + r2 * (g2 - xn2 * jnp.mean(g2 * xn2, axis=-1, keepdims=True))
        dx2_ref[...] = dx2
        dmg_ref[...] = _dot_nt(dx2, wo_ref[0:WIDTH, :])
        dmf_ref[...] = _dot_nt(dx2, wo_ref[WIDTH:2 * WIDTH, :])

    def tok(n):
        return pl.BlockSpec((tb, n), lambda i: (i, 0))

    acc = pl.BlockSpec((1, d), lambda i: (0, 0))
    sds = jax.ShapeDtypeStruct
    return pl.pallas_call(
        body, name="tail", grid=(t // tb,),
        in_specs=[tok(d), tok(WIDTH), tok(WIDTH), tok(d), _resident(wo.shape), _resident((1, d)),
                  _resident(wg_t.shape), _resident(wu_t.shape), _resident(wd.shape), _resident((1, d))],
        out_specs=[tok(d), tok(dff), tok(dff), tok(dff), tok(d), tok(d), tok(WIDTH), tok(WIDTH), acc, acc,
                   pl.BlockSpec((1, 1), lambda i: (0, 0))],
        out_shape=[sds((t, d), MXU), sds((t, dff), MXU), sds((t, dff), MXU), sds((t, dff), MXU), sds((t, d), MXU),
                   sds((t, d), F32), sds((t, WIDTH), F32), sds((t, WIDTH), F32), sds((1, d), F32), sds((1, d), F32),
                   sds((1, 1), F32)],
        compiler_params=_params(("arbitrary",)),
    )(x, mixg, mixf, tgt, wo, n2w, wg_t, wu_t, wd, fw)


def _wgrad(a, b, name):
    t, m = a.shape
    n = b.shape[1]
    bm = 256 if m % 256 == 0 else LANE
    cast = b.dtype != jnp.dtype(MXU)

    def body(a_ref, b_ref, o_ref, *scratch):
        if cast:
            @pl.when(pl.program_id(0) == 0)
            def _():
                scratch[0][...] = b_ref[...].astype(MXU)
        o_ref[...] = _dot_tn(a_ref[...], scratch[0][...] if cast else b_ref[...]).astype(o_ref.dtype)

    return pl.pallas_call(
        body, name=name, grid=(m // bm,),
        in_specs=[pl.BlockSpec((t, bm), lambda i: (0, i)), _resident((t, n))],
        out_specs=pl.BlockSpec((bm, n), lambda i: (i, 0)),
        out_shape=jax.ShapeDtypeStruct((m, n), WIRE),
        scratch_shapes=[pltpu.VMEM((t, n), MXU)] if cast else [],
        compiler_params=_params(("arbitrary",)),
    )(a, b)


def _merge_dw_in(d_gdn, d_z, d_fox, d_fg, d_small):
    return jnp.concatenate([d_gdn, d_z, d_small[:2 * HEADS], d_fox, d_fg, d_small[2 * HEADS:3 * HEADS]], axis=0)


def _lanes(*pieces):
    v = jnp.concatenate([p.reshape(-1).astype(F32) for p in pieces])
    return jnp.pad(v, (0, LANE - v.shape[0])).reshape(1, LANE)


def _vector_params(p):
    d = p["norm1_w"].size
    gparams = jnp.concatenate([_lanes(jnp.zeros(HEADS), p["gdn_dt_bias"], p["fox_f_bias"]),
                               _lanes(jnp.zeros(HEADS), p["gdn_A_log"]), jnp.zeros((6, LANE), F32)])
    fox_nw = jnp.stack([jnp.tile(p["fox_q_norm_w"].reshape(-1), 2), jnp.tile(p["fox_k_norm_w"].reshape(-1), 2),
                        jnp.ones((LANE,), F32)])
    return dict(n1w=p["norm1_w"].reshape(1, d), n2w=p["norm2_w"].reshape(1, d), fw=p["final_norm_w"].reshape(1, d),
                onw=p["gdn_out_norm_w"].reshape(1, DH), gparams=gparams, fox_nw=fox_nw)


def _mixer_forward(x, vp, w_t, ws_t, conv_w, carry=None):
    h1, pm, ps = _inproj(x, vp["n1w"], w_t, ws_t)
    gates, run, tot, run_t = _gates(ps, vp["gparams"])
    fqkv = _fox_prep(pm, vp["fox_nw"])
    o_fox, lse, *carried = _fox_fwd(fqkv, run, run_t, carry)
    if carry:
        pm, o_fox = lax.optimization_barrier((pm, o_fox))
    mixf = _fox_post(o_fox, pm)
    gqkv = _gdn_prep(pm, conv_w)
    o_gdn, states, inv = _gdn_fwd(gqkv, gates, run, tot, run_t)
    mixg = _gdn_post(o_gdn, pm, vp["onw"])
    return dict(h1=h1, pm=pm, ps=ps, gates=gates, run=run, tot=tot, run_t=run_t, gqkv=gqkv, o_gdn=o_gdn, states=states,
                inv=inv, mixg=mixg, fqkv=fqkv, o_fox=o_fox, lse=lse, mixf=mixf, carried=carried[0] if carried else None)


def _mixer_backward(x, vp, w_t, ws_t, conv_w, f, dx2, dmixg, dmixf, carry=None, scatter_own=False):
    pm = f["pm"]
    do_fox, dfg = _fox_post_bwd(f["o_fox"], pm, dmixf)
    dfqkv, dcol_f, drow_f, *carried = _fox_bwd(f["fqkv"], f["run"], f["run_t"], f["o_fox"], f["lse"], do_fox, carry)
    dfox, dfnw = _fox_prep_bwd(pm, vp["fox_nw"], dfqkv)
    do_gdn, dz, donw = _gdn_post_bwd(f["o_gdn"], pm, vp["onw"], dmixg)
    dgqkv, dcol_g, dtot_g, drow_g = _gdn_bwd(f["gqkv"], f["gates"], f["run"], f["tot"], f["run_t"], f["inv"], f["states"], do_gdn)
    dgdn, dconv = _gdn_prep_bwd(pm, conv_w, dgqkv)
    dps, gsum = _gates_bwd(f["ps"], vp["gparams"], dcol_g, dtot_g, dcol_f, drow_g, drow_f)
    h1 = f["h1"]
    dw_in = _merge_dw_in(_wgrad(dgdn, h1, "dw_in_gdn"), _wgrad(dz, h1, "dw_in_z"), _wgrad(dfox, h1, "dw_in_fox"),
                         _wgrad(dfg, h1, "dw_in_fgate"), _wgrad(dps, h1, "dw_in_small"))
    own = ("scatter", [_cut(dw_in, 0), _cut(dconv, 1)]) if scatter_own else None
    grad_x, dn1w, *sent = _inproj_bwd(x, vp["n1w"], dx2, dgdn, dz, dfox, dfg, dps, w_t, ws_t, own)
    small = dict(dn1w=dn1w, gsum=gsum, donw=donw, dfnw=dfnw)
    return (grad_x, dw_in, dconv, small, *carried, *sent)


VECTORS = ("norm1_w", "norm2_w", "final_norm_w", "gdn_A_log", "gdn_dt_bias", "gdn_out_norm_w", "fox_f_bias",
           "fox_q_norm_w", "fox_k_norm_w")
VEC_ROWS = 16
LOSS_ROW = len(VECTORS)


def _pack_vectors(dn1w, dn2w, dfw, gsum, donw, dfnw, loss):
    d = dn1w.shape[1]

    def body(n1_ref, n2_ref, fw_ref, gs_ref, on_ref, fn_ref, loss_ref, o_ref):
        o_ref[...] = jnp.zeros_like(o_ref)
        o_ref[0:1, :] = n1_ref[...]
        o_ref[1:2, :] = n2_ref[...]
        o_ref[2:3, :] = fw_ref[...]
        o_ref[3:4, 0:HEADS] = gs_ref[0:1, 0:HEADS]
        o_ref[4:5, 0:HEADS] = gs_ref[1:2, 0:HEADS]
        o_ref[5:6, 0:DH] = on_ref[...]
        o_ref[6:7, 0:HEADS] = gs_ref[2:3, 0:HEADS]
        for kind in range(2):
            v = fn_ref[kind, 0]
            for j in range(1, fn_ref.shape[1]):
                v = v + fn_ref[kind, j]
            o_ref[7 + kind:8 + kind, 0:DH] = v[:, :DH] + v[:, DH:]
        o_ref[LOSS_ROW:LOSS_ROW + 1, 0:1] = loss_ref[...]

    return pl.pallas_call(body, name="pack_vectors", out_shape=jax.ShapeDtypeStruct((VEC_ROWS, d), F32),
                          compiler_params=_params())(dn1w, dn2w, dfw, gsum, donw, dfnw, loss)


def _late_grads(f, h2, act, dgate, dup, dx3, dx2):
    return {"w_out": jnp.concatenate([_wgrad(f["mixg"], dx2, "dw_out_gdn"), _wgrad(f["mixf"], dx2, "dw_out_fox")], axis=0),
            "w_ffn_gate": _wgrad(dgate, h2, "dw_gate"), "w_ffn_up": _wgrad(dup, h2, "dw_up"),
            "w_ffn_down": _wgrad(act, dx3, "dw_down")}


def _local_step(x, tgt, p, w_in_t, conv_w, wo, wg_t, wu_t, wd):
    vp = _vector_params(p)
    ws_t = _small_rows(w_in_t)
    f = _mixer_forward(x, vp, w_in_t, ws_t, conv_w)
    (h2, act, dgate, dup, dx3, dx2, dmixg, dmixf, dn2w, dfw, loss) = _tail(
        x, f["mixg"], f["mixf"], tgt, wo, vp["n2w"], wg_t, wu_t, wd, vp["fw"])
    grad_x, dw_in, dconv, small = _mixer_backward(x, vp, w_in_t, ws_t, conv_w, f, dx2, dmixg, dmixf)
    grads = {"w_in": dw_in, "gdn_conv_w": dconv, **_late_grads(f, h2, act, dgate, dup, dx3, dx2)}
    vec = _pack_vectors(small["dn1w"], dn2w, dfw, small["gsum"], small["donw"], small["dfnw"], loss)
    return grad_x, grads, vec


def _my_place():
    return lax.axis_index("x"), lax.axis_index("y"), lax.axis_index("c")


def _peers():
    x, y, c = _my_place()
    peers = []
    for k in range(1, N_DEV):
        px = 1 - x if k & 4 else x
        py = 1 - y if k & 2 else y
        pc = 1 - c if k & 1 else c
        peers.append(((px, py, pc), 4 * px + 2 * py + pc))
    return 4 * x + 2 * y + c, peers


def _spread_copies(kind, srcs, lands, send_sems, recv_sems):
    me, peers = _peers()
    kinds = [kind] * len(srcs) if isinstance(kind, str) else kind
    remote, local = [], []
    for i, (kd, src, land) in enumerate(zip(kinds, srcs, lands)):
        for k, (dev, idx) in enumerate(peers):
            remote.append(pltpu.make_async_remote_copy(
                src_ref=src if kd == "gather" else src.at[idx], dst_ref=land.at[me],
                send_sem=send_sems.at[i * (N_DEV - 1) + k], recv_sem=recv_sems.at[i * (N_DEV - 1) + k],
                device_id=dev, device_id_type=MESH))
        local.append((src if kd == "gather" else src.at[me], land.at[me]))
    return remote, local


def _gather_two_level(arrays, name):
    n = len(arrays)

    def body(*refs):
        srcs, lands = refs[:n], refs[n:2 * n]
        send_sems, recv_sems, local_sems = refs[2 * n:]
        x, y, c = _my_place()
        me, sibling = (x, y, c), (x, y, 1 - c)
        chips = [(1 - x, y), (x, 1 - y), (1 - x, 1 - y)]

        def index(p):
            return 4 * p[0] + 2 * p[1] + p[2]

        def copy(i, k, block, to, src=None):
            blk = lands[i].at[index(block)]
            return pltpu.make_async_remote_copy(
                src_ref=blk if src is None else src, dst_ref=blk, send_sem=send_sems.at[7 * i + k],
                recv_sem=recv_sems.at[7 * i + k], device_id=to, device_id_type=MESH)

        mine = [pltpu.make_async_copy(srcs[i], lands[i].at[index(me)], local_sems.at[i]) for i in range(n)]
        for cp in mine:
            cp.start()
        first = []
        for i in range(n):
            first.append(copy(i, 0, me, sibling, src=srcs[i]))
            first += [copy(i, 1 + j, me, (*chip, c), src=srcs[i]) for j, chip in enumerate(chips)]
        for cp in first:
            cp.start()
        passed = []
        for i in range(n):
            for j, chip in enumerate(chips):
                copy(i, 1 + j, (*chip, c), me).wait_recv()
                passed.append(copy(i, 4 + j, (*chip, c), sibling))
                passed[-1].start()
        for i in range(n):
            copy(i, 0, sibling, me).wait_recv()
            for j, chip in enumerate(chips):
                copy(i, 4 + j, (*chip, 1 - c), me).wait_recv()
        for cp in first + passed:
            cp.wait_send()
        for cp in mine:
            cp.wait()

    return pl.pallas_call(
        body, name=name,
        out_shape=[jax.ShapeDtypeStruct((N_DEV,) + a.shape, a.dtype) for a in arrays],
        in_specs=[pl.BlockSpec(memory_space=pl.ANY)] * n, out_specs=[pl.BlockSpec(memory_space=pl.ANY)] * n,
        scratch_shapes=[pltpu.SemaphoreType.DMA((7 * n,)), pltpu.SemaphoreType.DMA((7 * n,)),
                        pltpu.SemaphoreType.DMA((n,))],
    )(*arrays)


def _land_shape(kind, a):
    return (N_DEV,) + a.shape if kind == "gather" else a.shape


HBM = pl.BlockSpec(memory_space=pltpu.HBM)
SEM = pl.BlockSpec(memory_space=pltpu.SEMAPHORE)


def _hbm(a):
    return pltpu.with_memory_space_constraint(a, pltpu.HBM)


def _carry_operands(kind, arrays):
    n = len(arrays)
    kinds = [kind] * n if isinstance(kind, str) else kind
    lands = [lax.empty(_land_shape(kd, a), a.dtype) for kd, a in zip(kinds, arrays)]
    sems = [pltpu.SemaphoreType.DMA((n * (N_DEV - 1),))] * 2
    return ([_hbm(a) for a in list(arrays) + lands], [HBM] * (2 * n),
            sems + [pltpu.HBM(a.shape, a.dtype) for a in list(arrays) + lands], [SEM] * 2 + [HBM] * (2 * n),
            pltpu.SemaphoreType.DMA((n,)))


def _carry_step(kind, in_refs, send_sems, recv_sems, local_sems, step, n_steps):
    n = len(in_refs) // 2
    remote, local = _spread_copies(kind, in_refs[:n], in_refs[n:], send_sems, recv_sems)
    copies = [pltpu.make_async_copy(s, d, local_sems.at[i]) for i, (s, d) in enumerate(local)]
    per_step = -(-len(remote) // n_steps)
    for s in range(-(-len(remote) // per_step)):
        @pl.when(step == s)
        def _():
            for cp in remote[s * per_step:(s + 1) * per_step]:
                cp.start()
            if s == 0:
                for cp in copies:
                    cp.start()

    @pl.when(step == n_steps - 1)
    def _():
        for cp in copies:
            cp.wait()


def _carry_state(extra_out):
    n = (len(extra_out) - 2) // 2
    return list(extra_out[2:2 + n]), list(extra_out[2 + n:]), extra_out[0], extra_out[1]


def _spread_start(arrays, kind, name):
    n = len(arrays)

    def body(*refs):
        srcs, lands = refs[:n], refs[n:2 * n]
        send_sems, recv_sems = refs[2 * n], refs[2 * n + 1]
        token = refs[4 * n + 2]
        local_sems = refs[4 * n + 3]
        remote, local = _spread_copies(kind, srcs, lands, send_sems, recv_sems)
        for cp in remote:
            cp.start()
        copies = [pltpu.make_async_copy(s, d, local_sems.at[i]) for i, (s, d) in enumerate(local)]
        for cp in copies:
            cp.start()
        for cp in copies:
            cp.wait()
        token[...] = jnp.zeros_like(token)

    sems = (pltpu.SemaphoreType.DMA((n * (N_DEV - 1),)),) * 2
    kinds = [kind] * n if isinstance(kind, str) else kind
    lands = [lax.empty(_land_shape(kd, a), a.dtype) for kd, a in zip(kinds, arrays)]
    out = pl.pallas_call(
        body, name=name,
        out_shape=sems + tuple(pltpu.HBM(a.shape, a.dtype) for a in list(arrays) + lands)
        + (jax.ShapeDtypeStruct((8, LANE), F32),),
        in_specs=[HBM] * (2 * n), out_specs=tuple([SEM] * 2 + [HBM] * (2 * n) + [pl.BlockSpec(memory_space=pltpu.VMEM)]),
        input_output_aliases={j: 2 + j for j in range(2 * n)},
        scratch_shapes=[pltpu.SemaphoreType.DMA((n,))],
        compiler_params=pltpu.CompilerParams(has_side_effects=pltpu.SideEffectType.DATAFLOW_SIDE_EFFECTING),
    )(*[_hbm(a) for a in arrays], *[_hbm(a) for a in lands])
    return (list(out[2:2 + n]), list(out[2 + n:2 + 2 * n]), out[0], out[1]), out[-1]


def _spread_wait(state, kind, after, name):
    srcs, lands, send_sems, recv_sems = state
    n = len(srcs)
    after = list(after) if isinstance(after, (list, tuple)) else [after]

    def body(*refs):
        remote, _ = _spread_copies(kind, refs[:n], refs[n:2 * n], refs[2 * n], refs[2 * n + 1])
        for cp in remote:
            cp.wait_send()
        for cp in remote:
            cp.wait_recv()

    out = pl.pallas_call(
        body, name=name,
        out_shape=tuple(pltpu.HBM(a.shape, a.dtype) for a in srcs + lands),
        in_specs=[HBM] * (2 * n) + [SEM, SEM] + [pl.BlockSpec(memory_space=pl.ANY)] * len(after),
        out_specs=tuple([HBM] * (2 * n)),
        input_output_aliases={j: j for j in range(2 * n)},
        compiler_params=pltpu.CompilerParams(has_side_effects=pltpu.SideEffectType.DATAFLOW_SIDE_EFFECTING),
    )(*srcs, *lands, send_sems, recv_sems, *after)
    return list(out[n:])


ADAM_ROWS = 128
ADAM_COLS = 256


def _adam_math(g, w, m, v):
    nm = ADAM_B1 * m + (1.0 - ADAM_B1) * g
    nv = ADAM_B2 * v + (1.0 - ADAM_B2) * (g * g)
    m_hat = nm / (1.0 - ADAM_B1 ** ADAM_STEP)
    v_hat = nv / (1.0 - ADAM_B2 ** ADAM_STEP)
    return -ADAM_LR * (m_hat / (jnp.sqrt(v_hat) + ADAM_EPS) + ADAM_WD * w), nm, nv


def _sum_parts(p_ref):
    g = p_ref[0].astype(F32)
    for s in range(1, N_DEV):
        g = g + p_ref[s].astype(F32)
    return g


def _adam_matrix(parts, w, m, v, name):
    _, r, c = w.shape
    rb = ADAM_ROWS if r % ADAM_ROWS == 0 else r
    cb = ADAM_COLS if (rb == r and c % ADAM_COLS == 0) else c

    def body(p_ref, w_ref, m_ref, v_ref, g_ref, d_ref, nm_ref, nv_ref):
        g = _sum_parts(p_ref)
        g_ref[0] = g
        d_ref[0], nm_ref[0], nv_ref[0] = _adam_math(g, w_ref[0], m_ref[0], v_ref[0])

    blk = pl.BlockSpec((1, rb, cb), lambda i, j: (0, i, j))
    return pl.pallas_call(
        body, name=name, grid=(r // rb, c // cb),
        in_specs=[pl.BlockSpec((N_DEV, rb, cb), lambda i, j: (0, i, j)), blk, blk, blk],
        out_specs=[blk] * 4, out_shape=[jax.ShapeDtypeStruct(w.shape, F32)] * 4,
        compiler_params=_params(("arbitrary", "arbitrary")),
    )(parts, w, m, v)


def _adam_vectors(parts, ws, ms, vs):
    nv = len(ws)

    def body(*refs):
        p_ref = refs[0]
        w_refs, m_refs, v_refs = refs[1:1 + nv], refs[1 + nv:1 + 2 * nv], refs[1 + 2 * nv:1 + 3 * nv]
        outs = refs[1 + 3 * nv:]
        g_all = _sum_parts(p_ref)
        for i in range(nv):
            n = w_refs[i].shape[1]
            g = g_all[i:i + 1, 0:n]
            d, nm, nvv = _adam_math(g, w_refs[i][...], m_refs[i][...], v_refs[i][...])
            outs[i][...] = g
            outs[nv + i][...] = d
            outs[2 * nv + i][...] = nm
            outs[3 * nv + i][...] = nvv
        outs[4 * nv][...] = g_all[LOSS_ROW:LOSS_ROW + 1, 0:1]

    shapes = [jax.ShapeDtypeStruct(a.shape, F32) for a in ws]
    out = pl.pallas_call(body, name="adam_vectors", out_shape=shapes * 4 + [jax.ShapeDtypeStruct((1, 1), F32)],
                         compiler_params=_params())(parts, *ws, *ms, *vs)
    return out[:nv], out[nv:2 * nv], out[2 * nv:3 * nv], out[3 * nv:4 * nv], out[4 * nv]


MATRICES = (("w_in", 1), ("gdn_conv_w", 1), ("w_out", 0), ("w_ffn_gate", 1), ("w_ffn_up", 1), ("w_ffn_down", 0))
TRANSPOSED = ("w_in", "w_ffn_gate", "w_ffn_up")
WEIGHTS = ("norm1_w", "w_in", "gdn_conv_w", "gdn_A_log", "gdn_dt_bias", "gdn_out_norm_w", "fox_f_bias", "fox_q_norm_w",
           "fox_k_norm_w", "w_out", "norm2_w", "w_ffn_gate", "w_ffn_up", "w_ffn_down", "final_norm_w")


def _join(blocks, axis):
    _, r, c = blocks.shape
    if axis == 0:
        return blocks.reshape(N_DEV * r, c)
    return blocks.transpose(1, 0, 2).reshape(r, N_DEV * c)


def _cut(full, axis):
    r, c = full.shape
    if axis == 0:
        return full.reshape(N_DEV, r // N_DEV, c)
    return full.reshape(r, N_DEV, c // N_DEV).transpose(1, 0, 2)


def kernel(x, norm1_w, w_in, gdn_conv_w, gdn_A_log, gdn_dt_bias, gdn_out_norm_w, fox_f_bias, fox_q_norm_w, fox_k_norm_w, w_out, norm2_w, w_ffn_gate, w_ffn_up, w_ffn_down, final_norm_w, loss_target, m_norm1_w, m_w_in, m_gdn_conv_w, m_gdn_A_log, m_gdn_dt_bias, m_gdn_out_norm_w, m_fox_f_bias, m_fox_q_norm_w, m_fox_k_norm_w, m_w_out, m_norm2_w, m_w_ffn_gate, m_w_ffn_up, m_w_ffn_down, m_final_norm_w, v_norm1_w, v_w_in, v_gdn_conv_w, v_gdn_A_log, v_gdn_dt_bias, v_gdn_out_norm_w, v_fox_f_bias, v_fox_q_norm_w, v_fox_k_norm_w, v_w_out, v_norm2_w, v_w_ffn_gate, v_w_ffn_up, v_w_ffn_down, v_final_norm_w):
    w = dict(norm1_w=norm1_w, w_in=w_in, gdn_conv_w=gdn_conv_w, gdn_A_log=gdn_A_log, gdn_dt_bias=gdn_dt_bias,
             gdn_out_norm_w=gdn_out_norm_w, fox_f_bias=fox_f_bias, fox_q_norm_w=fox_q_norm_w, fox_k_norm_w=fox_k_norm_w,
             w_out=w_out, norm2_w=norm2_w, w_ffn_gate=w_ffn_gate, w_ffn_up=w_ffn_up, w_ffn_down=w_ffn_down,
             final_norm_w=final_norm_w)
    m = dict(norm1_w=m_norm1_w, w_in=m_w_in, gdn_conv_w=m_gdn_conv_w, gdn_A_log=m_gdn_A_log, gdn_dt_bias=m_gdn_dt_bias,
             gdn_out_norm_w=m_gdn_out_norm_w, fox_f_bias=m_fox_f_bias, fox_q_norm_w=m_fox_q_norm_w,
             fox_k_norm_w=m_fox_k_norm_w, w_out=m_w_out, norm2_w=m_norm2_w, w_ffn_gate=m_w_ffn_gate,
             w_ffn_up=m_w_ffn_up, w_ffn_down=m_w_ffn_down, final_norm_w=m_final_norm_w)
    v = dict(norm1_w=v_norm1_w, w_in=v_w_in, gdn_conv_w=v_gdn_conv_w, gdn_A_log=v_gdn_A_log, gdn_dt_bias=v_gdn_dt_bias,
             gdn_out_norm_w=v_gdn_out_norm_w, fox_f_bias=v_fox_f_bias, fox_q_norm_w=v_fox_q_norm_w,
             fox_k_norm_w=v_fox_k_norm_w, w_out=v_w_out, norm2_w=v_norm2_w, w_ffn_gate=v_w_ffn_gate,
             w_ffn_up=v_w_ffn_up, w_ffn_down=v_w_ffn_down, final_norm_w=v_final_norm_w)
    late = ("w_out", "w_ffn_gate", "w_ffn_up", "w_ffn_down")
    xs, tgt = x[0], loss_target[0]
    vp = _vector_params({n: w[n] for n in VECTORS})

    def rows_of(d, n):
        return d[n].transpose(0, 2, 1) if n in TRANSPOSED else d[n]

    wr, mr, vr = ({n: rows_of(d, n) for n, _ in MATRICES} for d in (w, m, v))

    w_in_blocks, conv_blocks = _gather_two_level([wr["w_in"][0].astype(WIRE), w["gdn_conv_w"][0]], "gather_in")
    w_t = _join(w_in_blocks, 0)
    conv_w = _join(conv_blocks, 1)
    f = _mixer_forward(xs, vp, w_t, _small_rows(w_t), conv_w, ("gather", [wr[n][0].astype(WIRE) for n in late]))
    full = {n: _join(b, 0) for n, b in zip(late, _spread_wait(f["carried"], "gather", f["mixg"], "gather_late_wait"))}

    (h2, act, dgate, dup, dx3, dx2, dmixg, dmixf, dn2w, dfw, loss) = _tail(
        xs, f["mixg"], f["mixf"], tgt, full["w_out"], vp["n2w"], full["w_ffn_gate"], full["w_ffn_up"],
        full["w_ffn_down"], vp["fw"])
    dlate = _late_grads(f, h2, act, dgate, dup, dx3, dx2)

    grad_x, dw_in, dconv, small, state_grads, state_own = _mixer_backward(
        xs, vp, w_t, _small_rows(w_t), conv_w, f, dx2, dmixg, dmixf, ("scatter", [_cut(dlate[n], 0) for n in late]),
        scatter_own=True)
    vec = _pack_vectors(small["dn1w"], dn2w, dfw, small["gsum"], small["donw"], small["dfnw"], loss)

    state_vec, token = _spread_start([vec], "gather", "vectors_start")
    parts = dict(zip(late, _spread_wait(state_grads, "scatter", token, "grads_late_wait")))
    results = [{}, {}, {}, {}]

    def update(n):
        for d, a in zip(results, _adam_matrix(parts[n], wr[n], mr[n], vr[n], "adam_" + n)):
            d[n] = a.transpose(0, 2, 1) if n in TRANSPOSED else a

    for n in late:
        update(n)
    parts["w_in"], parts["gdn_conv_w"] = _spread_wait(state_own, "scatter", [results[0][n] for n in late], "own_wait")
    (parts_vec,) = _spread_wait(state_vec, "gather", parts["w_in"], "vectors_wait")
    update("w_in")
    update("gdn_conv_w")
    row = lambda a: a.reshape(1, -1)
    *vec_out, total_loss = _adam_vectors(parts_vec, [row(w[n]) for n in VECTORS], [row(m[n]) for n in VECTORS],
                                         [row(v[n]) for n in VECTORS])
    for d, arrs in zip(results, vec_out):
        for n, a in zip(VECTORS, arrs):
            d[n] = a.reshape(w[n].shape)
    return (total_loss[0, 0], grad_x[None], *[d[n] for d in results for n in WEIGHTS])
```

```python
import functools

import jax
import jax.numpy as jnp
from jax import lax
from jax.experimental import pallas as pl
from jax.experimental.pallas import tpu as pltpu

F32 = jnp.float32
MXU = jnp.bfloat16
WIRE = jnp.bfloat16
HI = lax.Precision.HIGHEST
EPS = 1e-6

N_DEV = 8
HEADS = 8
DH = 64
WIDTH = HEADS * DH
CHUNK = 64
LANE = 128
ROW_ALIGN = 16
TB = 256
QB = 256
VMEM_LIMIT = 60 * 1024 * 1024

ADAM_LR = 0.001
ADAM_B1 = 0.9
ADAM_B2 = 0.999
ADAM_EPS = 1e-08
ADAM_WD = 0.01
ADAM_STEP = 10

MESH = pl.DeviceIdType.MESH


def _params(sem=None):
    return pltpu.CompilerParams(dimension_semantics=sem, vmem_limit_bytes=VMEM_LIMIT)


def _resident(shape):
    n = len(shape)
    return pl.BlockSpec(shape, lambda *_: (0,) * n, pipeline_mode=pl.Buffered(1))


def _dot(a, b):
    return jnp.dot(a.astype(MXU), b.astype(MXU), preferred_element_type=F32)


def _dot_nt(a, b):
    return lax.dot_general(a.astype(MXU), b.astype(MXU), (((1,), (1,)), ((), ())), preferred_element_type=F32)


def _dot_tn(a, b):
    return lax.dot_general(a.astype(MXU), b.astype(MXU), (((0,), (0,)), ((), ())), preferred_element_type=F32)


def _hdot(a, b):
    return jnp.dot(a, b, precision=HI, preferred_element_type=F32)


def _hdot_nt(a, b):
    return lax.dot_general(a, b, (((1,), (1,)), ((), ())), precision=HI, preferred_element_type=F32)


def _hdot_tn(a, b):
    return lax.dot_general(a, b, (((0,), (0,)), ((), ())), precision=HI, preferred_element_type=F32)


def _sigmoid(x):
    return 0.5 * jnp.tanh(0.5 * x) + 0.5


def _softplus(x):
    return jnp.maximum(x, 0.0) + jnp.log(1.0 + jnp.exp(-jnp.abs(x)))


def _head_sum_matrix():
    ri = lax.broadcasted_iota(jnp.int32, (LANE, LANE), 0) // DH
    ci = lax.broadcasted_iota(jnp.int32, (LANE, LANE), 1) // DH
    return (ri == ci).astype(F32)


def _group_sum(a, ones_matrix):
    hi = a.astype(jnp.bfloat16)
    lo = (a - hi.astype(F32)).astype(jnp.bfloat16)
    m = ones_matrix.astype(jnp.bfloat16)
    return jnp.dot(hi, m, preferred_element_type=F32) + jnp.dot(lo, m, preferred_element_type=F32)


def _shift_down(x, s):
    return pltpu.roll(x, s, 0)


def _shift_up(x, s):
    return pltpu.roll(x, x.shape[0] - s, 0)


ROWS_A = 4 * WIDTH
ROWS_B = ROWS_A + 2 * HEADS
ROWS_C = ROWS_B + 4 * WIDTH


def _small_rows(w_t):
    return jnp.concatenate([w_t[ROWS_A:ROWS_B], w_t[ROWS_C:], jnp.zeros((LANE - 3 * HEADS, w_t.shape[1]), w_t.dtype)])


def _inproj(x, n1w, w_t, ws_t):
    t, d = x.shape
    tb = min(TB, t)

    def body(x_ref, nw_ref, wt_ref, ws_ref, h_ref, pm_ref, ps_ref):
        xv = x_ref[...]
        r = lax.rsqrt(jnp.mean(xv * xv, axis=-1, keepdims=True) + EPS)
        h = (xv * r * nw_ref[...]).astype(MXU)
        h_ref[...] = h
        pm_ref[:, 0:ROWS_A] = _dot_nt(h, wt_ref[0:ROWS_A, :])
        pm_ref[:, ROWS_A:2 * ROWS_A] = _dot_nt(h, wt_ref[ROWS_B:ROWS_C, :])
        ps_ref[...] = _dot_nt(h, ws_ref[...])

    return pl.pallas_call(
        body, name="inproj", grid=(t // tb,),
        in_specs=[pl.BlockSpec((tb, d), lambda i: (i, 0)), _resident((1, d)), _resident(w_t.shape), _resident((LANE, d))],
        out_specs=[pl.BlockSpec((tb, d), lambda i: (i, 0)), pl.BlockSpec((tb, 2 * ROWS_A), lambda i: (i, 0)),
                   pl.BlockSpec((tb, LANE), lambda i: (i, 0))],
        out_shape=[jax.ShapeDtypeStruct((t, d), MXU), jax.ShapeDtypeStruct((t, 2 * ROWS_A), F32),
                   jax.ShapeDtypeStruct((t, LANE), F32)],
        compiler_params=_params(("arbitrary",)),
    )(x, n1w, w_t, ws_t)


def _inproj_bwd(x, n1w, dx2, dgdn, dz, dfox, dfg, dps, w_t, ws_t, carry=None):
    t, d = x.shape
    tb = min(TB, t)
    w3 = 3 * WIDTH
    c_in, c_in_specs, c_out_shape, c_out_specs, c_sem = _carry_operands(*carry) if carry else ([], [], [], [], None)
    n_c = len(c_in)

    def body(*refs):
        x_ref, nw_ref, dx2_ref, dgdn_ref, dz_ref, dfox_ref, dfg_ref, dps_ref, wm_ref, ws_ref = refs[:10]
        gx_ref, dnw_ref = refs[10 + n_c:12 + n_c]
        if carry:
            _carry_step(carry[0], refs[10:10 + n_c], refs[12 + n_c], refs[13 + n_c], refs[-1], pl.program_id(0), t // tb)
        dh = _dot(dgdn_ref[...], wm_ref[0:w3, :])
        dh += _dot(dz_ref[...], wm_ref[w3:ROWS_A, :])
        dh += _dot(dfox_ref[...], wm_ref[ROWS_B:ROWS_B + w3, :])
        dh += _dot(dfg_ref[...], wm_ref[ROWS_B + w3:ROWS_C, :])
        dh += _dot(dps_ref[...], ws_ref[...])
        xv = x_ref[...]
        r = lax.rsqrt(jnp.mean(xv * xv, axis=-1, keepdims=True) + EPS)
        xn = xv * r

        @pl.when(pl.program_id(0) == 0)
        def _():
            dnw_ref[...] = jnp.zeros_like(dnw_ref)

        dnw_ref[...] += jnp.sum(dh * xn, axis=0, keepdims=True)
        g = dh * nw_ref[...]
        gx_ref[...] = dx2_ref[...] + r * (g - xn * jnp.mean(g * xn, axis=-1, keepdims=True))

    def tok(n):
        return pl.BlockSpec((tb, n), lambda i: (i, 0))

    out = pl.pallas_call(
        body, name="inproj_bwd", grid=(t // tb,),
        in_specs=[tok(d), _resident((1, d)), tok(d), tok(w3), tok(WIDTH), tok(w3), tok(WIDTH), tok(LANE),
                  _resident(w_t.shape), _resident(ws_t.shape)] + c_in_specs,
        out_specs=[tok(d), pl.BlockSpec((1, d), lambda i: (0, 0))] + c_out_specs,
        out_shape=[jax.ShapeDtypeStruct((t, d), F32), jax.ShapeDtypeStruct((1, d), F32)] + c_out_shape,
        input_output_aliases={10 + j: 4 + j for j in range(n_c)},
        scratch_shapes=[c_sem] if carry else [],
        compiler_params=_params(("arbitrary",)),
    )(x, n1w, dx2, dgdn, dz, dfox, dfg, dps, w_t, ws_t, *c_in)
    return (out[0], out[1], _carry_state(out[2:])) if carry else tuple(out)


def _gate_lanes(shape):
    lane = lax.broadcasted_iota(jnp.int32, shape, 1)
    return lane < HEADS, (lane >= HEADS) & (lane < 2 * HEADS), (lane >= 2 * HEADS) & (lane < 3 * HEADS)


def _block_masks():
    ri = lax.broadcasted_iota(jnp.int32, (LANE, LANE), 0)
    ci = lax.broadcasted_iota(jnp.int32, (LANE, LANE), 1)
    same = (ri // CHUNK) == (ci // CHUNK)
    return ((ri >= ci).astype(F32), (ri <= ci).astype(F32), (same & (ri >= ci)).astype(F32),
            (same & (ri <= ci)).astype(F32), same.astype(F32))


def _gates(ps, gparams):
    t = ps.shape[0]
    nb = t // LANE

    def body(ps_ref, gp_ref, out_ref, run_ref, tot_ref, runt_ref):
        p = ps_ref[...]
        is_b, is_a, is_f = _gate_lanes(p.shape)
        z = p + gp_ref[0:1, :]
        neg_exp_a = -jnp.exp(gp_ref[1:2, :])
        glog = neg_exp_a * _softplus(z)
        logf = -_softplus(-z)
        out_ref[...] = jnp.where(is_b, _sigmoid(p), jnp.where(is_a, glog, jnp.where(is_f, logf, 0.0)))
        tril, _, tril_c, _, same_c = _block_masks()
        off = jnp.zeros((1, LANE), F32)
        for b in range(nb):
            rows = slice(b * LANE, (b + 1) * LANE)
            blk = out_ref[rows, :]
            ga = jnp.where(is_a[:LANE], blk, 0.0)
            fb = _hdot(tril, jnp.where(is_f[:LANE], blk, 0.0)) + off
            run = fb + _hdot(tril_c, ga)
            run_ref[rows, :] = run
            runt_ref[:, rows] = run.T
            tot_ref[rows, :] = _hdot(same_c, ga)
            off = fb[LANE - 1:LANE, :]

    return pl.pallas_call(
        body, name="gates",
        out_shape=[jax.ShapeDtypeStruct((t, LANE), F32)] * 3 + [jax.ShapeDtypeStruct((LANE, t), F32)],
        compiler_params=_params(),
    )(ps, gparams)


def _gates_bwd(ps, gparams, dcol_g, dtot_g, dcol_f, drow_g, drow_f):
    t = ps.shape[0]
    nb = t // LANE

    def body(ps_ref, gp_ref, dcg_ref, dtg_ref, dcf_ref, drg_ref, drf_ref, dps_ref, sums_ref, dl_ref, d0_ref, tr_ref):
        p = ps_ref[...]
        is_b, is_a, is_f = _gate_lanes(p.shape)
        _, triu, _, triu_c, same_c = _block_masks()
        tr_ref[...] = jnp.zeros_like(tr_ref)
        off = jnp.zeros((1, LANE), F32)
        for b in reversed(range(nb)):
            rows = slice(b * LANE, (b + 1) * LANE)
            tr_ref[HEADS:2 * HEADS, :] = drg_ref[:, rows]
            tr_ref[2 * HEADS:3 * HEADS, :] = drf_ref[:, rows]
            d = dcg_ref[rows, :] + dcf_ref[rows, :] + tr_ref[...].T
            d0_ref[rows, :] = d
            dlf = _hdot(triu, jnp.where(is_f[:LANE], d, 0.0)) + off
            dla = (_hdot(triu_c, jnp.where(is_a[:LANE], d, 0.0))
                   + _hdot(same_c, jnp.where(is_a[:LANE], dtg_ref[rows, :], 0.0)))
            dl_ref[rows, :] = dlf + dla
            off = dlf[0:1, :]
        z = p + gp_ref[0:1, :]
        neg_exp_a = -jnp.exp(gp_ref[1:2, :])
        sb = _sigmoid(p)
        glog = neg_exp_a * _softplus(z)
        dl = dl_ref[...]
        dp = jnp.where(is_b, d0_ref[...] * sb * (1.0 - sb),
                       jnp.where(is_a, dl * neg_exp_a * _sigmoid(z), jnp.where(is_f, dl * _sigmoid(-z), 0.0)))
        dps_ref[...] = dp
        s_a = jnp.sum(jnp.where(is_a, dl * glog, 0.0), axis=0, keepdims=True)
        s_p = jnp.sum(jnp.where(is_b, 0.0, dp), axis=0, keepdims=True)
        row = lax.broadcasted_iota(jnp.int32, (8, LANE), 0)
        from_a = pltpu.roll(jnp.where(row == 0, s_a, jnp.where(row == 1, s_p, 0.0)), LANE - HEADS, 1)
        from_f = pltpu.roll(jnp.where(row == 2, s_p, 0.0), LANE - 2 * HEADS, 1)
        lane = lax.broadcasted_iota(jnp.int32, (8, LANE), 1)
        sums_ref[...] = jnp.where(lane < HEADS, from_a + from_f, 0.0)

    return pl.pallas_call(
        body, name="gates_bwd",
        out_shape=[jax.ShapeDtypeStruct((t, LANE), F32), jax.ShapeDtypeStruct((8, LANE), F32)],
        scratch_shapes=[pltpu.VMEM((t, LANE), F32), pltpu.VMEM((t, LANE), F32), pltpu.VMEM((LANE, LANE), F32)],
        compiler_params=_params(),
    )(ps, gparams, dcol_g, dtot_g, dcol_f, drow_g, drow_f)


def _conv(xv, w):
    acc = w[3:4, :] * xv
    for s in range(1, 4):
        acc += w[3 - s:4 - s, :] * _shift_down(xv, s)
    return acc


PREP_ROWS = 256
HALO = 8
PREP_UNROLL = 2


def _tile_loop(n, tile, init):
    if n % PREP_UNROLL:
        return lax.fori_loop(0, n, tile, init)

    def trip(g, carry):
        for u in range(PREP_UNROLL):
            carry = tile(g * PREP_UNROLL + u, carry)
        return carry

    return lax.fori_loop(0, n // PREP_UNROLL, trip, init)


def _tile_rows(r):
    return pl.ds(pl.multiple_of(r * PREP_ROWS, PREP_ROWS), PREP_ROWS)


def _gdn_prep(pm, conv_w):
    t = pm.shape[0]
    nj = WIDTH // LANE
    win = PREP_ROWS + HALO

    def body(x_ref, w_ref, o_ref, xp_ref):
        kind = pl.program_id(0)
        xp_ref[0:HALO, :] = jnp.zeros((HALO, LANE), F32)
        xp_ref[HALO:, :] = x_ref[...]
        w = w_ref[...]
        hs = _head_sum_matrix()

        def tile(r, _, normed):
            xw = xp_ref[pl.ds(pl.multiple_of(r * PREP_ROWS, PREP_ROWS), win), :]
            acc = _conv(xw, w)[HALO:]
            out = acc * _sigmoid(acc)
            if normed:
                out = out * lax.rsqrt(_group_sum(out * out, hs) + EPS)
            o_ref[0, 0, _tile_rows(r), :] = out[:, :DH]
            o_ref[0, 1, _tile_rows(r), :] = out[:, DH:]
            return 0

        @pl.when(kind < 2)
        def _():
            _tile_loop(t // PREP_ROWS, functools.partial(tile, normed=True), 0)

        @pl.when(kind == 2)
        def _():
            _tile_loop(t // PREP_ROWS, functools.partial(tile, normed=False), 0)

    return pl.pallas_call(
        body, name="gdn_prep", grid=(3, nj),
        in_specs=[pl.BlockSpec((t, LANE), lambda i, j: (0, i * nj + j)),
                  pl.BlockSpec((4, LANE), lambda i, j: (0, i * nj + j))],
        out_specs=pl.BlockSpec((1, 2, t, DH), lambda i, j: (i, j, 0, 0)),
        out_shape=jax.ShapeDtypeStruct((3, HEADS, t, DH), F32),
        scratch_shapes=[pltpu.VMEM((t + HALO, LANE), F32)],
        compiler_params=_params(("arbitrary", "arbitrary")),
    )(pm, conv_w)


def _gdn_prep_bwd(pm, conv_w, dqkv):
    t = pm.shape[0]
    nj = WIDTH // LANE
    win = PREP_ROWS + 2 * HALO

    def body(x_ref, w_ref, d_ref, dx_ref, dw_ref, xp_ref, dp_ref):
        kind = pl.program_id(0)
        zeros = jnp.zeros((HALO, LANE), F32)
        for ref in (xp_ref, dp_ref):
            ref[0:HALO, :] = zeros
            ref[HALO + t:, :] = zeros
        xp_ref[HALO:HALO + t, :] = x_ref[...]
        dp_ref[HALO:HALO + t, 0:DH] = d_ref[0, 0]
        dp_ref[HALO:HALO + t, DH:] = d_ref[0, 1]
        w = w_ref[...]
        hs = _head_sum_matrix()
        rows = lax.broadcasted_iota(jnp.int32, (win, LANE), 0)
        in_tile = (rows >= HALO) & (rows < HALO + PREP_ROWS)

        def tile(r, dw, normed):
            start = pl.multiple_of(r * PREP_ROWS, PREP_ROWS)
            xw = xp_ref[pl.ds(start, win), :]
            dy = dp_ref[pl.ds(start, win), :]
            acc = _conv(xw, w)
            sg = _sigmoid(acc)
            if normed:
                y = acc * sg
                rn = lax.rsqrt(_group_sum(y * y, hs) + EPS)
                yn = y * rn
                dy = rn * (dy - yn * _group_sum(dy * yn, hs))
            dacc = dy * sg * (1.0 + acc * (1.0 - sg))
            dx = w[3:4, :] * dacc
            for s in range(1, 4):
                dx += w[3 - s:4 - s, :] * _shift_up(dacc, s)
            dx_ref[_tile_rows(r), :] = dx[HALO:HALO + PREP_ROWS]
            dm = jnp.where(in_tile, dacc, 0.0)
            return tuple(dw[i] + jnp.sum(dm * (xw if i == 3 else _shift_down(xw, 3 - i)), axis=0, keepdims=True)
                         for i in range(4))

        def run(normed):
            dw = _tile_loop(t // PREP_ROWS, functools.partial(tile, normed=normed),
                            tuple(jnp.zeros((1, LANE), F32) for _ in range(4)))
            for i in range(4):
                dw_ref[i:i + 1, :] = dw[i]

        pl.when(kind < 2)(functools.partial(run, True))
        pl.when(kind == 2)(functools.partial(run, False))

    return pl.pallas_call(
        body, name="gdn_prep_bwd", grid=(3, nj),
        in_specs=[pl.BlockSpec((t, LANE), lambda i, j: (0, i * nj + j)),
                  pl.BlockSpec((4, LANE), lambda i, j: (0, i * nj + j)),
                  pl.BlockSpec((1, 2, t, DH), lambda i, j: (i, j, 0, 0))],
        out_specs=[pl.BlockSpec((t, LANE), lambda i, j: (0, i * nj + j)),
                   pl.BlockSpec((4, LANE), lambda i, j: (0, i * nj + j))],
        out_shape=[jax.ShapeDtypeStruct((t, 3 * WIDTH), F32), jax.ShapeDtypeStruct((4, 3 * WIDTH), F32)],
        scratch_shapes=[pltpu.VMEM((t + 2 * HALO, LANE), F32), pltpu.VMEM((t + 2 * HALO, LANE), F32)],
        compiler_params=_params(("arbitrary", "arbitrary")),
    )(pm, conv_w, dqkv)


FOX_COL0 = 4 * WIDTH // LANE


def _fox_prep(pm, nw):
    t = pm.shape[0]
    nj = WIDTH // LANE

    def body(x_ref, w_ref, o_ref):
        kind = pl.program_id(0)
        hs = _head_sum_matrix()
        wk = w_ref[pl.ds(kind, 1), :]

        def tile(r, _, normed):
            out = x_ref[_tile_rows(r), :]
            if normed:
                out = out * lax.rsqrt(_group_sum(out * out, hs) * (1.0 / DH) + EPS) * wk
            o_ref[0, 0, _tile_rows(r), :] = out[:, :DH]
            o_ref[0, 1, _tile_rows(r), :] = out[:, DH:]
            return 0

        @pl.when(kind < 2)
        def _():
            _tile_loop(t // PREP_ROWS, functools.partial(tile, normed=True), 0)

        @pl.when(kind == 2)
        def _():
            _tile_loop(t // PREP_ROWS, functools.partial(tile, normed=False), 0)

    return pl.pallas_call(
        body, name="fox_prep", grid=(3, nj),
        in_specs=[pl.BlockSpec((t, LANE), lambda i, j: (0, FOX_COL0 + i * nj + j)),
                  pl.BlockSpec((3, LANE), lambda i, j: (0, 0))],
        out_specs=pl.BlockSpec((1, 2, t, DH), lambda i, j: (i, j, 0, 0)),
        out_shape=jax.ShapeDtypeStruct((3, HEADS, t, DH), F32),
        compiler_params=_params(("arbitrary", "arbitrary")),
    )(pm, nw)


def _fox_prep_bwd(pm, nw, dqkv):
    t = pm.shape[0]
    nj = WIDTH // LANE

    def body(x_ref, w_ref, d_ref, dx_ref, dw_ref):
        kind = pl.program_id(0)
        hs = _head_sum_matrix()
        wk = w_ref[pl.ds(kind, 1), :]

        def tile(r, dw):
            xv = x_ref[_tile_rows(r), :]
            rn = lax.rsqrt(_group_sum(xv * xv, hs) * (1.0 / DH) + EPS)
            xn = xv * rn
            d = jnp.concatenate([d_ref[0, 0, _tile_rows(r), :], d_ref[0, 1, _tile_rows(r), :]], axis=1)
            g = d * wk
            dx_ref[_tile_rows(r), :] = rn * (g - xn * _group_sum(g * xn, hs) * (1.0 / DH))
            return dw + jnp.sum(d * xn, axis=0, keepdims=True)

        def copy_tile(r, _):
            dx_ref[_tile_rows(r), :] = jnp.concatenate([d_ref[0, 0, _tile_rows(r), :], d_ref[0, 1, _tile_rows(r), :]], axis=1)
            return 0

        @pl.when(kind < 2)
        def _():
            dw_ref[0, 0] = _tile_loop(t // PREP_ROWS, tile, jnp.zeros((1, LANE), F32))

        @pl.when(kind == 2)
        def _():
            _tile_loop(t // PREP_ROWS, copy_tile, 0)
            dw_ref[0, 0] = jnp.zeros((1, LANE), F32)

    return pl.pallas_call(
        body, name="fox_prep_bwd", grid=(3, nj),
        in_specs=[pl.BlockSpec((t, LANE), lambda i, j: (0, FOX_COL0 + i * nj + j)),
                  pl.BlockSpec((3, LANE), lambda i, j: (0, 0)),
                  pl.BlockSpec((1, 2, t, DH), lambda i, j: (i, j, 0, 0))],
        out_specs=[pl.BlockSpec((t, LANE), lambda i, j: (0, i * nj + j)),
                   pl.BlockSpec((1, 1, 1, LANE), lambda i, j: (i, j, 0, 0))],
        out_shape=[jax.ShapeDtypeStruct((t, 3 * WIDTH), F32), jax.ShapeDtypeStruct((3, nj, 1, LANE), F32)],
        compiler_params=_params(("arbitrary", "arbitrary")),
    )(pm, nw, dqkv)


SC = 256
CPS = SC // CHUNK
GDN_HP_FWD = 4
GDN_HP_BWD = 2
Q_SCALE = DH ** -0.5


def _sc_masks():
    ri = lax.broadcasted_iota(jnp.int32, (SC, SC), 0)
    ci = lax.broadcasted_iota(jnp.int32, (SC, SC), 1)
    same = (ri // CHUNK) == (ci // CHUNK)
    return same & (ri >= ci), same & (ri > ci), ri == ci


def _unit_lower_inverses(ms, eye):
    invs = [jnp.where(eye, 1.0, 0.0) + m for m in ms]
    for _ in range(5):
        ms = [_dot(m, m) for m in ms]
        invs = [inv + _dot(inv, m) for inv, m in zip(invs, ms)]
    return invs


def _lane_col(blk, lane_idx):
    lane = lax.broadcasted_iota(jnp.int32, blk.shape, 1)
    return jnp.sum(jnp.where(lane == lane_idx, blk, 0.0), axis=1, keepdims=True)


def _to_lane(col, lane_idx):
    lane = lax.broadcasted_iota(jnp.int32, (col.shape[0], LANE), 1)
    return jnp.where(lane == lane_idx, col, 0.0)


def _gdn_columns(gates_ref, run_ref, tot_ref, runt_ref, rows, h):
    return (_lane_col(gates_ref[rows, :], h), _lane_col(run_ref[rows, :], HEADS + h),
            _lane_col(tot_ref[rows, :], HEADS + h), runt_ref[pl.ds(h, 1), rows])


def _gdn_local(q, k, beta, gc, gl, grow, causal):
    decay = jnp.exp(jnp.where(causal, gc - grow, -1e30))
    egc = jnp.exp(gc)
    ekd = jnp.exp(gl - gc)
    qs = q * Q_SCALE
    kb = k * beta
    kk = _dot_nt(kb, k)
    qk = _dot_nt(qs, k)
    return beta, gl, decay, egc, ekd, qs, kb, kk, qk, jnp.where(causal, qk * decay, 0.0)


def _chunk_rows(c):
    return pl.ds(pl.multiple_of(c * CHUNK, CHUNK), CHUNK)


def _sc_rows(b):
    return pl.ds(pl.multiple_of(b * SC, SC), SC)


def _gdn_fwd(qkv, gates, run, tot, run_t):
    t = qkv.shape[2]
    nc = t // CHUNK
    nsc = t // SC

    hp_n = GDN_HP_FWD
    heads = range(hp_n)

    def body(qkv_ref, gates_ref, run_ref, tot_ref, runt_ref, o_ref, st_ref, inv_ref, kc_s, qc_s, g_s, au_s):
        hp = pl.program_id(0)
        causal, strict, eye = _sc_masks()

        def local(b, _):
            rows = _sc_rows(b)
            loc = [_gdn_local(qkv_ref[0, hh, rows, :], qkv_ref[1, hh, rows, :],
                              *_gdn_columns(gates_ref, run_ref, tot_ref, runt_ref, rows, hp * hp_n + hh), causal)
                   for hh in heads]
            invs = _unit_lower_inverses([-jnp.where(strict, l[7] * l[2], 0.0) for l in loc], eye)
            us, ws = [], []
            for hh in heads:
                beta, _, _, egc, _, _, kb, _, _, _ = loc[hh]
                inv_ref[hh, rows, :] = invs[hh].astype(inv_ref.dtype)
                us.append(_dot(invs[hh], qkv_ref[2, hh, rows, :] * beta))
                ws.append(_dot(invs[hh], kb * egc))
            for hh in heads:
                _, _, _, egc, ekd, qs, _, _, _, attn = loc[hh]
                g_s[hh, rows, :] = qs * egc - _dot(attn, ws[hh])
                au_s[hh, rows, :] = _dot(attn, us[hh])
                kd = qkv_ref[1, hh, rows, :] * ekd
                for j in range(CPS):
                    sl = slice(j * CHUNK, (j + 1) * CHUNK)
                    kc_s[hh, b * CPS + j] = _dot_tn(kd[sl], ws[hh][sl])
                    qc_s[hh, b * CPS + j] = _dot_tn(kd[sl], us[hh][sl])
            return 0

        lax.fori_loop(0, nsc, local, 0)

        def step(c, states):
            rows = _chunk_rows(c)
            tot_row = tot_ref[pl.ds(c * CHUNK, 1), :]
            new = []
            for hh in heads:
                s = states[hh]
                st_ref[hh, c] = s
                o_ref[hh, rows, :] = _dot(g_s[hh, rows, :], s) + au_s[hh, rows, :]
                egl = jnp.exp(_lane_col(tot_row, HEADS + hp * hp_n + hh))
                new.append(egl * s - _dot(kc_s[hh, c], s) + qc_s[hh, c])
            return tuple(new)

        lax.fori_loop(0, nc, step, tuple(jnp.zeros((DH, DH), F32) for _ in heads))

    whole = pl.BlockSpec((t, LANE), lambda h: (0, 0))
    once = dict(pipeline_mode=pl.Buffered(1))
    return pl.pallas_call(
        body, name="gdn_fwd", grid=(HEADS // hp_n,),
        in_specs=[pl.BlockSpec((3, hp_n, t, DH), lambda h: (0, h, 0, 0), **once), whole, whole, whole,
                  pl.BlockSpec((HEADS, t), lambda h: (1, 0))],
        out_specs=[pl.BlockSpec((hp_n, t, DH), lambda h: (h, 0, 0), **once),
                   pl.BlockSpec((hp_n, nc, DH, DH), lambda h: (h, 0, 0, 0), **once),
                   pl.BlockSpec((hp_n, t, SC), lambda h: (h, 0, 0), **once)],
        out_shape=[jax.ShapeDtypeStruct((HEADS, t, DH), F32), jax.ShapeDtypeStruct((HEADS, nc, DH, DH), F32),
                   jax.ShapeDtypeStruct((HEADS, t, SC), MXU)],
        scratch_shapes=[pltpu.VMEM((hp_n, nc, DH, DH), F32), pltpu.VMEM((hp_n, nc, DH, DH), F32),
                        pltpu.VMEM((hp_n, t, DH), F32), pltpu.VMEM((hp_n, t, DH), F32)],
        compiler_params=_params(("arbitrary",)),
    )(qkv, gates, run, tot, run_t)


def _gdn_bwd(qkv, gates, run, tot, run_t, inv, states, do):
    t = qkv.shape[2]
    nc = t // CHUNK
    nsc = t // SC

    hp_n = GDN_HP_BWD
    heads = range(hp_n)

    def body(qkv_ref, gates_ref, run_ref, tot_ref, runt_ref, inv_ref, st_ref, do_ref,
             dqkv_ref, dcol_ref, dtot_ref, drow_ref, u_s, w_s, kc_s, h_s, dsn_s):
        hp = pl.program_id(0)
        causal, strict, _ = _sc_masks()

        @pl.when(hp == 0)
        def _():
            dcol_ref[...] = jnp.zeros_like(dcol_ref)
            dtot_ref[...] = jnp.zeros_like(dtot_ref)

        def local_of(hh, rows):
            return _gdn_local(qkv_ref[0, hh, rows, :], qkv_ref[1, hh, rows, :],
                              *_gdn_columns(gates_ref, run_ref, tot_ref, runt_ref, rows, hp * hp_n + hh), causal)

        def local(b, _):
            rows = _sc_rows(b)
            loc = [local_of(hh, rows) for hh in heads]
            us, ws = [], []
            for hh in heads:
                beta, _, _, egc, _, _, kb, _, _, _ = loc[hh]
                inv_b = inv_ref[hh, rows, :]
                us.append(_dot(inv_b, qkv_ref[2, hh, rows, :] * beta))
                ws.append(_dot(inv_b, kb * egc))
            gs = [loc[hh][5] * loc[hh][3] - _dot(loc[hh][9], ws[hh]) for hh in heads]
            for hh in heads:
                u_s[hh, rows, :] = us[hh]
                w_s[hh, rows, :] = ws[hh]
                kd = qkv_ref[1, hh, rows, :] * loc[hh][4]
                dout = do_ref[hh, rows, :]
                for j in range(CPS):
                    sl = slice(j * CHUNK, (j + 1) * CHUNK)
                    kc_s[hh, b * CPS + j] = _dot_tn(kd[sl], ws[hh][sl])
                    h_s[hh, b * CPS + j] = _dot_tn(gs[hh][sl], dout[sl])
            return 0

        lax.fori_loop(0, nsc, local, 0)

        def step(i, dss):
            c = nc - 1 - i
            tot_row = tot_ref[pl.ds(c * CHUNK, 1), :]
            new = []
            for hh in heads:
                ds = dss[hh]
                dsn_s[hh, c] = ds
                egl = jnp.exp(_lane_col(tot_row, HEADS + hp * hp_n + hh))
                new.append(egl * ds - _dot_tn(kc_s[hh, c], ds) + h_s[hh, c])
            return tuple(new)

        lax.fori_loop(0, nc, step, tuple(jnp.zeros((DH, DH), F32) for _ in heads))

        def back(b, _):
            rows = _sc_rows(b)
            first = lax.broadcasted_iota(jnp.int32, (CHUNK, 1), 0) == 0
            loc = [local_of(hh, rows) for hh in heads]
            mid = []
            for hh in heads:
                beta, gl, decay, egc, ekd, qs, kb, kk, qk, attn = loc[hh]
                u, w = u_s[hh, rows, :], w_s[hh, rows, :]
                kd = qkv_ref[1, hh, rows, :] * ekd
                dout = do_ref[hh, rows, :]
                dg_p, dkd_p, dw_p, du_p, dgl_p = [], [], [], [], []
                for j in range(CPS):
                    sl = slice(j * CHUNK, (j + 1) * CHUNK)
                    s = st_ref[hh, b * CPS + j]
                    dsn = dsn_s[hh, b * CPS + j]
                    dkc = -_dot_nt(dsn, s)
                    dg_p.append(_dot_nt(dout[sl], s))
                    dkd_p.append(_dot_nt(w[sl], dkc) + _dot_nt(u[sl], dsn))
                    dw_p.append(_dot(kd[sl], dkc))
                    du_p.append(_dot(kd[sl], dsn))
                    degl = jnp.sum(jnp.sum(s * dsn, axis=1, keepdims=True), axis=0, keepdims=True)
                    dgl_p.append(jnp.where(first, degl * jnp.exp(gl[j * CHUNK:j * CHUNK + 1, :]), 0.0))
                dg, dkd = jnp.concatenate(dg_p, axis=0), jnp.concatenate(dkd_p, axis=0)
                da = jnp.where(causal, _dot_nt(dout, u) - _dot_nt(dg, w), 0.0)
                at = _dot_tn(attn, jnp.concatenate([dout, dg], axis=1))
                du = at[:, :DH] + jnp.concatenate(du_p, axis=0)
                dw = jnp.concatenate(dw_p, axis=0) - at[:, DH:]
                mid.append((kd, dg, dkd, da, du, dw, jnp.concatenate(dgl_p, axis=0)))
            inv_ts = [inv_ref[hh, rows, :].astype(F32).T for hh in heads]
            its = [_dot(inv_ts[hh], jnp.concatenate([mid[hh][4], mid[hh][5]], axis=1)) for hh in heads]
            dinvs = []
            for hh in heads:
                beta, _, _, egc, _, _, kb, _, _, _ = loc[hh]
                dinvs.append(_dot_nt(mid[hh][4], qkv_ref[2, hh, rows, :] * beta) + _dot_nt(mid[hh][5], kb * egc))
            half = [_dot(inv_ts[hh], dinvs[hh]) for hh in heads]
            dls = [jnp.where(strict, -_dot(half[hh], inv_ts[hh]), 0.0) for hh in heads]
            for hh in heads:
                head = hp * hp_n + hh
                beta, gl, decay, egc, ekd, qs, kb, kk, qk, attn = loc[hh]
                kd, dg, dkd, da, du, dw, dgl_first = mid[hh]
                k, v = qkv_ref[1, hh, rows, :], qkv_ref[2, hh, rows, :]
                dvb, dkbe = its[hh][:, :DH], its[hh][:, DH:]
                dl = dls[hh]
                dlogd = (dl * kk + da * qk) * decay
                dkk = dl * decay
                dqk = da * decay
                dkb = _dot(dkk, k) + dkbe * egc
                dqs = _dot(dqk, k) + dg * egc
                dk = (_dot_tn(jnp.concatenate([dkk, dqk], axis=0), jnp.concatenate([kb, qs], axis=0))
                      + dkd * ekd + dkb * beta)
                dkd_kd = jnp.sum(dkd * kd, axis=1, keepdims=True)
                dgc = (jnp.sum(dlogd, axis=1, keepdims=True) + jnp.sum(dg * qs, axis=1, keepdims=True) * egc
                       + jnp.sum(dkbe * (kb * egc), axis=1, keepdims=True) - dkd_kd)
                dbeta = jnp.sum(dkb * k, axis=1, keepdims=True) + jnp.sum(dvb * v, axis=1, keepdims=True)
                dqkv_ref[0, hh, rows, :] = dqs * Q_SCALE
                dqkv_ref[1, hh, rows, :] = dk
                dqkv_ref[2, hh, rows, :] = dvb * beta
                dcol_ref[rows, :] += _to_lane(dbeta, head) + _to_lane(dgc, HEADS + head)
                dtot_ref[rows, :] += _to_lane(dkd_kd + dgl_first, HEADS + head)
                drow_ref[pl.ds(head, 1), rows] = -jnp.sum(dlogd, axis=0, keepdims=True)
            return 0

        lax.fori_loop(0, nsc, back, 0)

    whole = pl.BlockSpec((t, LANE), lambda h: (0, 0))
    rowspec = pl.BlockSpec((HEADS, t), lambda h: (0, 0))
    sq = pltpu.VMEM((hp_n, nc, DH, DH), F32)
    per_head = pltpu.VMEM((hp_n, t, DH), F32)
    return pl.pallas_call(
        body, name="gdn_bwd", grid=(HEADS // hp_n,),
        in_specs=[pl.BlockSpec((3, hp_n, t, DH), lambda h: (0, h, 0, 0)), whole, whole, whole,
                  pl.BlockSpec((HEADS, t), lambda h: (1, 0)),
                  pl.BlockSpec((hp_n, t, SC), lambda h: (h, 0, 0)), pl.BlockSpec((hp_n, nc, DH, DH), lambda h: (h, 0, 0, 0)),
                  pl.BlockSpec((hp_n, t, DH), lambda h: (h, 0, 0))],
        out_specs=[pl.BlockSpec((3, hp_n, t, DH), lambda h: (0, h, 0, 0)), whole, whole, rowspec],
        out_shape=[jax.ShapeDtypeStruct((3, HEADS, t, DH), F32), jax.ShapeDtypeStruct((t, LANE), F32),
                   jax.ShapeDtypeStruct((t, LANE), F32), jax.ShapeDtypeStruct((HEADS, t), F32)],
        scratch_shapes=[per_head, per_head, sq, sq, sq],
        compiler_params=_params(("arbitrary",)),
    )(qkv, gates, run, tot, run_t, inv, states, do)


FOX_HP_FWD = 4
FOX_HP_BWD = 4


def _wide_t(a):
    r = a.shape[0]
    return jnp.concatenate([a, jnp.zeros((r, LANE - DH), F32)], axis=1).T[:DH]


def _tall_t(a):
    r = a.shape[1]
    return jnp.concatenate([a, jnp.zeros((LANE - DH, r), F32)], axis=0).T[:, :DH]


def _key_side_f(run_blk, head, qb):
    col = jnp.broadcast_to(_lane_col(run_blk, 2 * HEADS + head), (run_blk.shape[0], LANE))
    return jnp.concatenate([col] * (qb // LANE), axis=1)


def _diag_mask(qb):
    return lax.broadcasted_iota(jnp.int32, (qb, qb), 0) <= lax.broadcasted_iota(jnp.int32, (qb, qb), 1)


def _fox_fwd(qkv, run, run_t, carry=None):
    t = qkv.shape[2]
    qb = min(QB, t)
    nq = t // qb
    hp_n = FOX_HP_FWD
    c_in, c_in_specs, c_out_shape, c_out_specs, c_sem = _carry_operands(*carry) if carry else ([], [], [], [], None)
    n_c = len(c_in)

    def body(*refs):
        q_ref, k_ref, v_ref, run_ref, runt_ref = refs[:5]
        o_ref, lse_ref = refs[5 + n_c:7 + n_c]
        kb_s, vt_s = refs[7 + n_c + len(c_out_shape):9 + n_c + len(c_out_shape)]
        hp = pl.program_id(0)
        i = pl.program_id(1)

        if carry:
            _carry_step(carry[0], refs[5:5 + n_c], refs[7 + n_c], refs[8 + n_c], refs[-1], hp * nq + i,
                        (HEADS // hp_n) * nq)

        @pl.when(i == 0)
        def _():
            for hh in range(hp_n):
                kb_s[hh] = k_ref[0, hh].astype(MXU)
                for b in range(nq):
                    rows = slice(b * qb, (b + 1) * qb)
                    vt_s[hh, :, rows] = _wide_t(v_ref[0, hh, rows, :]).astype(MXU)

        qrows = pl.ds(pl.multiple_of(i * qb, qb), qb)
        qs = [(q_ref[0, hh] * Q_SCALE).astype(MXU) for hh in range(hp_n)]
        fq = [runt_ref[pl.ds(hp * hp_n + hh, 1), qrows] for hh in range(hp_n)]

        def block_rows(j):
            return pl.ds(pl.multiple_of(j * qb, qb), qb)

        def scores(j):
            return tuple(_dot_nt(kb_s[hh, block_rows(j), :], qs[hh]) for hh in range(hp_n))

        def absorb(j, raw, state, diagonal):
            rows = block_rows(j)
            run_blk = run_ref[rows, :]
            stats, pv = [], []
            for hh in range(hp_n):
                m, l, _ = state[hh]
                st = raw[hh] + (fq[hh] - _key_side_f(run_blk, hp * hp_n + hh, qb))
                if diagonal:
                    st = jnp.where(_diag_mask(qb), st, -1e30)
                m_new = jnp.maximum(m, jnp.max(st, axis=0, keepdims=True))
                p = jnp.exp(st - m_new)
                alpha = jnp.exp(m - m_new)
                stats.append((m_new, alpha * l + jnp.sum(p, axis=0, keepdims=True), alpha))
                pv.append(_dot(vt_s[hh, :, rows], p))
            return tuple((stats[hh][0], stats[hh][1], stats[hh][2] * state[hh][2] + pv[hh]) for hh in range(hp_n))

        def kstep(j, carry):
            state, raw = carry
            ahead = scores(j + 1)
            return absorb(j, raw, state, False), ahead

        init = tuple((jnp.full((1, qb), -1e30, F32), jnp.zeros((1, qb), F32), jnp.zeros((DH, qb), F32))
                     for _ in range(hp_n))
        state, raw = lax.fori_loop(0, i, kstep, (init, scores(0)))
        state = absorb(i, raw, state, True)
        for hh in range(hp_n):
            m, l, acc = state[hh]
            o_ref[hh] = _tall_t(acc / l)
            lse_ref[pl.ds(hp * hp_n + hh, 1), qrows] = m + jnp.log(l)

    out = pl.pallas_call(
        body, name="fox_fwd", grid=(HEADS // hp_n, nq),
        in_specs=[pl.BlockSpec((1, hp_n, qb, DH), lambda h, i: (0, h, i, 0)),
                  pl.BlockSpec((1, hp_n, t, DH), lambda h, i: (1, h, 0, 0)),
                  pl.BlockSpec((1, hp_n, t, DH), lambda h, i: (2, h, 0, 0)),
                  pl.BlockSpec((t, LANE), lambda h, i: (0, 0)),
                  pl.BlockSpec((HEADS, t), lambda h, i: (2, 0))] + c_in_specs,
        out_specs=[pl.BlockSpec((hp_n, qb, DH), lambda h, i: (h, i, 0)),
                   pl.BlockSpec((HEADS, t), lambda h, i: (0, 0))] + c_out_specs,
        out_shape=[jax.ShapeDtypeStruct((HEADS, t, DH), F32), jax.ShapeDtypeStruct((HEADS, t), F32)] + c_out_shape,
        input_output_aliases={5 + j: 4 + j for j in range(n_c)},
        scratch_shapes=[pltpu.VMEM((hp_n, t, DH), MXU), pltpu.VMEM((hp_n, DH, t), MXU)] + ([c_sem] if carry else []),
        compiler_params=_params(("arbitrary", "arbitrary")),
    )(qkv, qkv, qkv, run, run_t, *c_in)
    return (out[0], out[1], _carry_state(out[2:])) if carry else (out[0], out[1])


def _fox_bwd(qkv, run, run_t, o, lse, do, carry=None):
    t = qkv.shape[2]
    qb = min(QB, t)
    nq = t // qb
    hp_n = FOX_HP_BWD
    c_in, c_in_specs, c_out_shape, c_out_specs, c_sem = _carry_operands(*carry) if carry else ([], [], [], [], None)
    n_c = len(c_in)

    def body(*refs):
        q_ref, k_ref, v_ref, run_ref, runt_ref, o_ref, lse_ref, do_ref = refs[:8]
        dqkv_ref, dcol_ref, drow_ref = refs[8 + n_c:11 + n_c]
        dqt_s = refs[11 + n_c + len(c_out_shape)]
        hp = pl.program_id(0)
        j = pl.program_id(1)

        if carry:
            _carry_step(carry[0], refs[8:8 + n_c], refs[11 + n_c], refs[12 + n_c], refs[-1], hp * nq + j,
                        (HEADS // hp_n) * nq)

        @pl.when(j == 0)
        def _():
            dqt_s[...] = jnp.zeros_like(dqt_s)

        @pl.when((j == 0) & (hp == 0))
        def _():
            dcol_ref[...] = jnp.zeros_like(dcol_ref)
            drow_ref[...] = jnp.zeros_like(drow_ref)

        krows = pl.ds(pl.multiple_of(j * qb, qb), qb)
        run_blk = run_ref[krows, :]
        ones8 = jnp.ones((8, DH), MXU)
        kb, kt, vb, fk = [], [], [], []
        for hh in range(hp_n):
            kf = k_ref[0, hh]
            kb.append(kf.astype(MXU))
            kt.append(_wide_t(kf).astype(MXU))
            vb.append(v_ref[0, hh].astype(MXU))
            fk.append(_key_side_f(run_blk, hp * hp_n + hh, qb))

        def block_rows(i):
            return pl.ds(pl.multiple_of(i * qb, qb), qb)

        def products(i):
            rows = block_rows(i)
            out = []
            for hh in range(hp_n):
                dout = do_ref[hh, rows, :]
                x = dout * o_ref[hh, rows, :]
                x_hi = x.astype(MXU)
                out.append((_dot_nt(kb[hh], q_ref[0, hh, rows, :] * Q_SCALE), _dot_nt(vb[hh], dout),
                            (_dot_nt(ones8, x_hi) + _dot_nt(ones8, x - x_hi.astype(F32)))[0:1, :]))
            return tuple(out)

        def absorb(i, prods, acc, diagonal):
            rows = block_rows(i)
            pieces = []
            for hh in range(hp_n):
                head = hp * hp_n + hh
                raw, dpt, drow = prods[hh]
                off = runt_ref[pl.ds(head, 1), rows] - lse_ref[pl.ds(head, 1), rows]
                st = raw + (off - fk[hh])
                if diagonal:
                    st = jnp.where(_diag_mask(qb), st, -1e30)
                pt = jnp.exp(st)
                dst = pt * (dpt - drow)
                drow_ref[pl.ds(head, 1), rows] += jnp.sum(dst, axis=0, keepdims=True)
                folded = dst[:, 0:LANE]
                for c in range(1, qb // LANE):
                    folded = folded + dst[:, c * LANE:(c + 1) * LANE]
                pieces.append((_dot(dst, q_ref[0, hh, rows, :] * Q_SCALE), _dot(pt, do_ref[hh, rows, :]),
                               _dot(kt[hh], dst), folded))
            out = []
            for hh in range(hp_n):
                dqt_s[hh, :, rows] += pieces[hh][2]
                out.append((acc[hh][0] + pieces[hh][0], acc[hh][1] + pieces[hh][1], acc[hh][2] + pieces[hh][3]))
            return tuple(out)

        def qstep(i, carry):
            acc, prods = carry
            ahead = products(jnp.minimum(i + 1, nq - 1))
            return absorb(i, prods, acc, False), ahead

        init = tuple((jnp.zeros((qb, DH), F32), jnp.zeros((qb, DH), F32), jnp.zeros((qb, LANE), F32))
                     for _ in range(hp_n))
        first = products(j)
        ahead = products(jnp.minimum(j + 1, nq - 1))
        acc = absorb(j, first, init, True)
        acc, _ = lax.fori_loop(j + 1, nq, qstep, (acc, ahead))
        for hh in range(hp_n):
            dk, dv, ds_sum = acc[hh]
            dqkv_ref[1, hh, krows, :] = dk
            dqkv_ref[2, hh, krows, :] = dv
            dcol_ref[krows, :] += _to_lane(-jnp.sum(ds_sum, axis=1, keepdims=True), 2 * HEADS + hp * hp_n + hh)

        @pl.when(j == nq - 1)
        def _():
            for hh in range(hp_n):
                for b in range(nq):
                    rows = slice(b * qb, (b + 1) * qb)
                    dqkv_ref[0, hh, rows, :] = _tall_t(dqt_s[hh, :, rows]) * Q_SCALE

    once = dict(pipeline_mode=pl.Buffered(1))
    full = pl.BlockSpec((hp_n, t, DH), lambda h, j: (h, 0, 0), **once)
    rows8 = pl.BlockSpec((HEADS, t), lambda h, j: (0, 0))
    out = pl.pallas_call(
        body, name="fox_bwd", grid=(HEADS // hp_n, nq),
        in_specs=[pl.BlockSpec((1, hp_n, t, DH), lambda h, j: (0, h, 0, 0), **once),
                  pl.BlockSpec((1, hp_n, qb, DH), lambda h, j: (1, h, j, 0)),
                  pl.BlockSpec((1, hp_n, qb, DH), lambda h, j: (2, h, j, 0)),
                  pl.BlockSpec((t, LANE), lambda h, j: (0, 0), **once), pl.BlockSpec((HEADS, t), lambda h, j: (2, 0)),
                  full, rows8, full] + c_in_specs,
        out_specs=[pl.BlockSpec((3, hp_n, t, DH), lambda h, j: (0, h, 0, 0), **once),
                   pl.BlockSpec((t, LANE), lambda h, j: (0, 0)), rows8] + c_out_specs,
        out_shape=[jax.ShapeDtypeStruct((3, HEADS, t, DH), F32), jax.ShapeDtypeStruct((t, LANE), F32),
                   jax.ShapeDtypeStruct((HEADS, t), F32)] + c_out_shape,
        input_output_aliases={8 + j: 5 + j for j in range(n_c)},
        scratch_shapes=[pltpu.VMEM((hp_n, DH, t), F32)] + ([c_sem] if carry else []),
        compiler_params=_params(("arbitrary", "arbitrary")),
    )(qkv, qkv, qkv, run, run_t, o, lse, do, *c_in)
    return (out[0], out[1], out[2], _carry_state(out[3:])) if carry else tuple(out)


Z_COL0 = 3 * WIDTH // LANE
FGATE_COL0 = 7 * WIDTH // LANE


def _gdn_post(o, pm, onw):
    t = pm.shape[0]

    def body(o_ref, z_ref, w_ref, m_ref):
        z = z_ref[...]
        sz = z * _sigmoid(z)
        halves = []
        for hh in range(2):
            ov = o_ref[hh]
            n = ov * lax.rsqrt(jnp.mean(ov * ov, axis=-1, keepdims=True) + EPS) * w_ref[...]
            halves.append(n * sz[:, hh * DH:(hh + 1) * DH])
        m_ref[...] = jnp.concatenate(halves, axis=1).astype(m_ref.dtype)

    return pl.pallas_call(
        body, name="gdn_post", grid=(WIDTH // LANE,),
        in_specs=[pl.BlockSpec((2, t, DH), lambda j: (j, 0, 0)), pl.BlockSpec((t, LANE), lambda j: (0, Z_COL0 + j)),
                  pl.BlockSpec((1, DH), lambda j: (0, 0))],
        out_specs=pl.BlockSpec((t, LANE), lambda j: (0, j)),
        out_shape=jax.ShapeDtypeStruct((t, WIDTH), MXU),
        compiler_params=_params(("arbitrary",)),
    )(o, pm, onw)


def _gdn_post_bwd(o, pm, onw, dmix):
    t = pm.shape[0]

    def body(o_ref, z_ref, w_ref, dm_ref, do_ref, dz_ref, dw_ref):
        @pl.when(pl.program_id(0) == 0)
        def _():
            dw_ref[...] = jnp.zeros_like(dw_ref)

        z = z_ref[...]
        sg = _sigmoid(z)
        sz = z * sg
        dsz = sg * (1.0 + z * (1.0 - sg))
        dm = dm_ref[...]
        for hh in range(2):
            cols = slice(hh * DH, (hh + 1) * DH)
            ov = o_ref[hh]
            r = lax.rsqrt(jnp.mean(ov * ov, axis=-1, keepdims=True) + EPS)
            xn = ov * r
            dmh = dm[:, cols]
            dn = dmh * sz[:, cols]
            dz_ref[:, cols] = dmh * (xn * w_ref[...]) * dsz[:, cols]
            dw_ref[...] += jnp.sum(dn * xn, axis=0, keepdims=True)
            g = dn * w_ref[...]
            do_ref[hh] = r * (g - xn * jnp.mean(g * xn, axis=-1, keepdims=True))

    return pl.pallas_call(
        body, name="gdn_post_bwd", grid=(WIDTH // LANE,),
        in_specs=[pl.BlockSpec((2, t, DH), lambda j: (j, 0, 0)), pl.BlockSpec((t, LANE), lambda j: (0, Z_COL0 + j)),
                  pl.BlockSpec((1, DH), lambda j: (0, 0)), pl.BlockSpec((t, LANE), lambda j: (0, j))],
        out_specs=[pl.BlockSpec((2, t, DH), lambda j: (j, 0, 0)), pl.BlockSpec((t, LANE), lambda j: (0, j)),
                   pl.BlockSpec((1, DH), lambda j: (0, 0))],
        out_shape=[jax.ShapeDtypeStruct((HEADS, t, DH), F32), jax.ShapeDtypeStruct((t, WIDTH), F32),
                   jax.ShapeDtypeStruct((1, DH), F32)],
        compiler_params=_params(("arbitrary",)),
    )(o, pm, onw, dmix)


def _fox_post(o, pm):
    t = pm.shape[0]

    def body(o_ref, g_ref, m_ref):
        m_ref[...] = (jnp.concatenate([o_ref[0], o_ref[1]], axis=1) * _sigmoid(g_ref[...])).astype(m_ref.dtype)

    return pl.pallas_call(
        body, name="fox_post", grid=(WIDTH // LANE,),
        in_specs=[pl.BlockSpec((2, t, DH), lambda j: (j, 0, 0)), pl.BlockSpec((t, LANE), lambda j: (0, FGATE_COL0 + j))],
        out_specs=pl.BlockSpec((t, LANE), lambda j: (0, j)),
        out_shape=jax.ShapeDtypeStruct((t, WIDTH), MXU),
        compiler_params=_params(("arbitrary",)),
    )(o, pm)


def _fox_post_bwd(o, pm, dmix):
    t = pm.shape[0]

    def body(o_ref, g_ref, dm_ref, do_ref, dg_ref):
        sg = _sigmoid(g_ref[...])
        dm = dm_ref[...]
        for hh in range(2):
            cols = slice(hh * DH, (hh + 1) * DH)
            do_ref[hh] = dm[:, cols] * sg[:, cols]
            dg_ref[:, cols] = dm[:, cols] * o_ref[hh] * (sg * (1.0 - sg))[:, cols]

    return pl.pallas_call(
        body, name="fox_post_bwd", grid=(WIDTH // LANE,),
        in_specs=[pl.BlockSpec((2, t, DH), lambda j: (j, 0, 0)), pl.BlockSpec((t, LANE), lambda j: (0, FGATE_COL0 + j)),
                  pl.BlockSpec((t, LANE), lambda j: (0, j))],
        out_specs=[pl.BlockSpec((2, t, DH), lambda j: (j, 0, 0)), pl.BlockSpec((t, LANE), lambda j: (0, j))],
        out_shape=[jax.ShapeDtypeStruct((HEADS, t, DH), F32), jax.ShapeDtypeStruct((t, WIDTH), F32)],
        compiler_params=_params(("arbitrary",)),
    )(o, pm, dmix)


def _tail(x, mixg, mixf, tgt, wo, n2w, wg_t, wu_t, wd, fw):
    t, d = x.shape
    dff = wd.shape[0]
    tb = min(TB, t)

    def body(x_ref, mg_ref, mf_ref, t_ref, wo_ref, n2_ref, wg_ref, wu_ref, wd_ref, fw_ref,
             h2_ref, act_ref, dgate_ref, dup_ref, dx3_ref, dx2_ref, dmg_ref, dmf_ref, dn2_ref, dfw_ref, loss_ref):
        @pl.when(pl.program_id(0) == 0)
        def _():
            dn2_ref[...] = jnp.zeros_like(dn2_ref)
            dfw_ref[...] = jnp.zeros_like(dfw_ref)
            loss_ref[...] = jnp.zeros_like(loss_ref)

        x2 = x_ref[...] + _dot(mg_ref[...], wo_ref[0:WIDTH, :]) + _dot(mf_ref[...], wo_ref[WIDTH:2 * WIDTH, :])
        r2 = lax.rsqrt(jnp.mean(x2 * x2, axis=-1, keepdims=True) + EPS)
        xn2 = x2 * r2
        h2 = (xn2 * n2_ref[...]).astype(MXU)
        h2_ref[...] = h2
        gate = _dot_nt(h2, wg_ref[...])
        up = _dot_nt(h2, wu_ref[...])
        sg = _sigmoid(gate)
        sl = gate * sg
        act = (sl * up).astype(MXU)
        act_ref[...] = act
        x3 = x2 + _dot(act, wd_ref[...])
        r3 = lax.rsqrt(jnp.mean(x3 * x3, axis=-1, keepdims=True) + EPS)
        xn3 = x3 * r3
        err = xn3 * fw_ref[...] - t_ref[...]
        loss_ref[...] += 0.5 * jnp.sum(jnp.mean(err * err, axis=-1, keepdims=True), axis=0, keepdims=True)
        dy = err * (1.0 / d)
        dfw_ref[...] += jnp.sum(dy * xn3, axis=0, keepdims=True)
        g3 = dy * fw_ref[...]
        dx3 = r3 * (g3 - xn3 * jnp.mean(g3 * xn3, axis=-1, keepdims=True))
        dx3_ref[...] = dx3.astype(MXU)
        dact = _dot_nt(dx3, wd_ref[...])
        dgate = (dact * up * (sg * (1.0 + gate * (1.0 - sg)))).astype(MXU)
        dup = (dact * sl).astype(MXU)
        dgate_ref[...] = dgate
        dup_ref[...] = dup
        dh2 = _dot(dgate, wg_ref[...]) + _dot(dup, wu_ref[...])
        dn2_ref[...] += jnp.sum(dh2 * xn2, axis=0, keepdims=True)
        g2 = dh2 * n2_ref[...]
        dx2 = dx3 + r2 * (g2 - xn2 * jnp.mean(g2 * xn2, axis=-1, keepdims=True))
        dx2_ref[...] = dx2
        dmg_ref[...] = _dot_nt(dx2, wo_ref[0:WIDTH, :])
        dmf_ref[...] = _dot_nt(dx2, wo_ref[WIDTH:2 * WIDTH, :])

    def tok(n):
        return pl.BlockSpec((tb, n), lambda i: (i, 0))

    acc = pl.BlockSpec((1, d), lambda i: (0, 0))
    sds = jax.ShapeDtypeStruct
    return pl.pallas_call(
        body, name="tail", grid=(t // tb,),
        in_specs=[tok(d), tok(WIDTH), tok(WIDTH), tok(d), _resident(wo.shape), _resident((1, d)),
                  _resident(wg_t.shape), _resident(wu_t.shape), _resident(wd.shape), _resident((1, d))],
        out_specs=[tok(d), tok(dff), tok(dff), tok(dff), tok(d), tok(d), tok(WIDTH), tok(WIDTH), acc, acc,
                   pl.BlockSpec((1, 1), lambda i: (0, 0))],
        out_shape=[sds((t, d), MXU), sds((t, dff), MXU), sds((t, dff), MXU), sds((t, dff), MXU), sds((t, d), MXU),
                   sds((t, d), F32), sds((t, WIDTH), F32), sds((t, WIDTH), F32), sds((1, d), F32), sds((1, d), F32),
                   sds((1, 1), F32)],
        compiler_params=_params(("arbitrary",)),
    )(x, mixg, mixf, tgt, wo, n2w, wg_t, wu_t, wd, fw)


def _wgrads(a_list, b, name):
    t, n = b.shape
    ms = [a.shape[1] for a in a_list]
    bms = [256 if m % 256 == 0 else LANE for m in ms]
    nbs = [m // bm for m, bm in zip(ms, bms)]
    k = len(a_list)
    cast = b.dtype != jnp.dtype(MXU)

    def body(*refs):
        a_refs, b_ref, o_refs = refs[:k], refs[k], refs[k + 1:2 * k + 1]
        i = pl.program_id(0)
        if cast:
            @pl.when(i == 0)
            def _():
                refs[-1][...] = b_ref[...].astype(MXU)
        for a_ref, o_ref, nb in zip(a_refs, o_refs, nbs):
            @pl.when(i < nb)
            def _():
                o_ref[...] = _dot_tn(a_ref[...], refs[-1][...] if cast else b_ref[...]).astype(o_ref.dtype)

    def clamp(nb):
        return lambda i: jnp.minimum(i, nb - 1)

    return pl.pallas_call(
        body, name=name, grid=(max(nbs),),
        in_specs=[pl.BlockSpec((t, bm), lambda i, c=clamp(nb): (0, c(i))) for bm, nb in zip(bms, nbs)] + [_resident((t, n))],
        out_specs=[pl.BlockSpec((bm, n), lambda i, c=clamp(nb): (c(i), 0)) for bm, nb in zip(bms, nbs)],
        out_shape=[jax.ShapeDtypeStruct((m, n), WIRE) for m in ms],
        scratch_shapes=[pltpu.VMEM((t, n), MXU)] if cast else [],
        compiler_params=_params(("arbitrary",)),
    )(*a_list, b)


def _merge_dw_in(d_gdn, d_z, d_fox, d_fg, d_small):
    return jnp.concatenate([d_gdn, d_z, d_small[:2 * HEADS], d_fox, d_fg, d_small[2 * HEADS:3 * HEADS]], axis=0)


def _lanes(*pieces):
    v = jnp.concatenate([p.reshape(-1).astype(F32) for p in pieces])
    return jnp.pad(v, (0, LANE - v.shape[0])).reshape(1, LANE)


def _vector_params(p):
    d = p["norm1_w"].size
    gparams = jnp.concatenate([_lanes(jnp.zeros(HEADS), p["gdn_dt_bias"], p["fox_f_bias"]),
                               _lanes(jnp.zeros(HEADS), p["gdn_A_log"]), jnp.zeros((6, LANE), F32)])
    fox_nw = jnp.stack([jnp.tile(p["fox_q_norm_w"].reshape(-1), 2), jnp.tile(p["fox_k_norm_w"].reshape(-1), 2),
                        jnp.ones((LANE,), F32)])
    return dict(n1w=p["norm1_w"].reshape(1, d), n2w=p["norm2_w"].reshape(1, d), fw=p["final_norm_w"].reshape(1, d),
                onw=p["gdn_out_norm_w"].reshape(1, DH), gparams=gparams, fox_nw=fox_nw)


def _mixer_forward(x, vp, w_t, ws_t, conv_w, carry=None):
    h1, pm, ps = _inproj(x, vp["n1w"], w_t, ws_t)
    gates, run, tot, run_t = _gates(ps, vp["gparams"])
    fqkv = _fox_prep(pm, vp["fox_nw"])
    o_fox, lse, *carried = _fox_fwd(fqkv, run, run_t, carry)
    if carry:
        pm, o_fox = lax.optimization_barrier((pm, o_fox))
    mixf = _fox_post(o_fox, pm)
    gqkv = _gdn_prep(pm, conv_w)
    o_gdn, states, inv = _gdn_fwd(gqkv, gates, run, tot, run_t)
    mixg = _gdn_post(o_gdn, pm, vp["onw"])
    return dict(h1=h1, pm=pm, ps=ps, gates=gates, run=run, tot=tot, run_t=run_t, gqkv=gqkv, o_gdn=o_gdn, states=states,
                inv=inv, mixg=mixg, fqkv=fqkv, o_fox=o_fox, lse=lse, mixf=mixf, carried=carried[0] if carried else None)


def _mixer_backward(x, vp, w_t, ws_t, conv_w, f, dx2, dmixg, dmixf, carry=None, scatter_own=False):
    pm = f["pm"]
    do_fox, dfg = _fox_post_bwd(f["o_fox"], pm, dmixf)
    dfqkv, dcol_f, drow_f, *carried = _fox_bwd(f["fqkv"], f["run"], f["run_t"], f["o_fox"], f["lse"], do_fox, carry)
    dfox, dfnw = _fox_prep_bwd(pm, vp["fox_nw"], dfqkv)
    do_gdn, dz, donw = _gdn_post_bwd(f["o_gdn"], pm, vp["onw"], dmixg)
    dgqkv, dcol_g, dtot_g, drow_g = _gdn_bwd(f["gqkv"], f["gates"], f["run"], f["tot"], f["run_t"], f["inv"], f["states"], do_gdn)
    dgdn, dconv = _gdn_prep_bwd(pm, conv_w, dgqkv)
    dps, gsum = _gates_bwd(f["ps"], vp["gparams"], dcol_g, dtot_g, dcol_f, drow_g, drow_f)
    dw_in = _merge_dw_in(*_wgrads([dgdn, dz, dfox, dfg, dps], f["h1"], "dw_in"))
    own = ("scatter", [_cut(dw_in, 0), _cut(dconv, 1)]) if scatter_own else None
    grad_x, dn1w, *sent = _inproj_bwd(x, vp["n1w"], dx2, dgdn, dz, dfox, dfg, dps, w_t, ws_t, own)
    small = dict(dn1w=dn1w, gsum=gsum, donw=donw, dfnw=dfnw)
    return (grad_x, dw_in, dconv, small, *carried, *sent)


VECTORS = ("norm1_w", "norm2_w", "final_norm_w", "gdn_A_log", "gdn_dt_bias", "gdn_out_norm_w", "fox_f_bias",
           "fox_q_norm_w", "fox_k_norm_w")
VEC_ROWS = 16
LOSS_ROW = len(VECTORS)


def _pack_vectors(dn1w, dn2w, dfw, gsum, donw, dfnw, loss):
    d = dn1w.shape[1]

    def body(n1_ref, n2_ref, fw_ref, gs_ref, on_ref, fn_ref, loss_ref, o_ref):
        o_ref[...] = jnp.zeros_like(o_ref)
        o_ref[0:1, :] = n1_ref[...]
        o_ref[1:2, :] = n2_ref[...]
        o_ref[2:3, :] = fw_ref[...]
        o_ref[3:4, 0:HEADS] = gs_ref[0:1, 0:HEADS]
        o_ref[4:5, 0:HEADS] = gs_ref[1:2, 0:HEADS]
        o_ref[5:6, 0:DH] = on_ref[...]
        o_ref[6:7, 0:HEADS] = gs_ref[2:3, 0:HEADS]
        for kind in range(2):
            v = fn_ref[kind, 0]
            for j in range(1, fn_ref.shape[1]):
                v = v + fn_ref[kind, j]
            o_ref[7 + kind:8 + kind, 0:DH] = v[:, :DH] + v[:, DH:]
        o_ref[LOSS_ROW:LOSS_ROW + 1, 0:1] = loss_ref[...]

    return pl.pallas_call(body, name="pack_vectors", out_shape=jax.ShapeDtypeStruct((VEC_ROWS, d), F32),
                          compiler_params=_params())(dn1w, dn2w, dfw, gsum, donw, dfnw, loss)


def _late_grads(f, h2, act, dgate, dup, dx3, dx2):
    dw_gate, dw_up = _wgrads([dgate, dup], h2, "dw_gate_up")
    (dw_down,) = _wgrads([act], dx3, "dw_down")
    return {"w_out": jnp.concatenate(_wgrads([f["mixg"], f["mixf"]], dx2, "dw_out"), axis=0),
            "w_ffn_gate": dw_gate, "w_ffn_up": dw_up, "w_ffn_down": dw_down}


def _local_step(x, tgt, p, w_in_t, conv_w, wo, wg_t, wu_t, wd):
    vp = _vector_params(p)
    ws_t = _small_rows(w_in_t)
    f = _mixer_forward(x, vp, w_in_t, ws_t, conv_w)
    (h2, act, dgate, dup, dx3, dx2, dmixg, dmixf, dn2w, dfw, loss) = _tail(
        x, f["mixg"], f["mixf"], tgt, wo, vp["n2w"], wg_t, wu_t, wd, vp["fw"])
    grad_x, dw_in, dconv, small = _mixer_backward(x, vp, w_in_t, ws_t, conv_w, f, dx2, dmixg, dmixf)
    grads = {"w_in": dw_in, "gdn_conv_w": dconv, **_late_grads(f, h2, act, dgate, dup, dx3, dx2)}
    vec = _pack_vectors(small["dn1w"], dn2w, dfw, small["gsum"], small["donw"], small["dfnw"], loss)
    return grad_x, grads, vec


def _my_place():
    return lax.axis_index("x"), lax.axis_index("y"), lax.axis_index("c")


def _peers():
    x, y, c = _my_place()
    peers = []
    for k in range(1, N_DEV):
        px = 1 - x if k & 4 else x
        py = 1 - y if k & 2 else y
        pc = 1 - c if k & 1 else c
        peers.append(((px, py, pc), 4 * px + 2 * py + pc))
    return 4 * x + 2 * y + c, peers


def _spread_copies(kind, srcs, lands, send_sems, recv_sems):
    me, peers = _peers()
    kinds = [kind] * len(srcs) if isinstance(kind, str) else kind
    remote, local = [], []
    for i, (kd, src, land) in enumerate(zip(kinds, srcs, lands)):
        for k, (dev, idx) in enumerate(peers):
            remote.append(pltpu.make_async_remote_copy(
                src_ref=src if kd == "gather" else src.at[idx], dst_ref=land.at[me],
                send_sem=send_sems.at[i * (N_DEV - 1) + k], recv_sem=recv_sems.at[i * (N_DEV - 1) + k],
                device_id=dev, device_id_type=MESH))
        local.append((src if kd == "gather" else src.at[me], land.at[me]))
    return remote, local


def _gather_two_level(arrays, name):
    n = len(arrays)

    def body(*refs):
        srcs, lands = refs[:n], refs[n:2 * n]
        send_sems, recv_sems, local_sems = refs[2 * n:]
        x, y, c = _my_place()
        me, sibling = (x, y, c), (x, y, 1 - c)
        chips = [(1 - x, y), (x, 1 - y), (1 - x, 1 - y)]

        def index(p):
            return 4 * p[0] + 2 * p[1] + p[2]

        def copy(i, k, block, to, src=None):
            blk = lands[i].at[index(block)]
            return pltpu.make_async_remote_copy(
                src_ref=blk if src is None else src, dst_ref=blk, send_sem=send_sems.at[7 * i + k],
                recv_sem=recv_sems.at[7 * i + k], device_id=to, device_id_type=MESH)

        mine = [pltpu.make_async_copy(srcs[i], lands[i].at[index(me)], local_sems.at[i]) for i in range(n)]
        for cp in mine:
            cp.start()
        first = []
        for i in range(n):
            first.append(copy(i, 0, me, sibling, src=srcs[i]))
            first += [copy(i, 1 + j, me, (*chip, c), src=srcs[i]) for j, chip in enumerate(chips)]
        for cp in first:
            cp.start()
        passed = []
        for i in range(n):
            for j, chip in enumerate(chips):
                copy(i, 1 + j, (*chip, c), me).wait_recv()
                passed.append(copy(i, 4 + j, (*chip, c), sibling))
                passed[-1].start()
        for i in range(n):
            copy(i, 0, sibling, me).wait_recv()
            for j, chip in enumerate(chips):
                copy(i, 4 + j, (*chip, 1 - c), me).wait_recv()
        for cp in first + passed:
            cp.wait_send()
        for cp in mine:
            cp.wait()

    return pl.pallas_call(
        body, name=name,
        out_shape=[jax.ShapeDtypeStruct((N_DEV,) + a.shape, a.dtype) for a in arrays],
        in_specs=[pl.BlockSpec(memory_space=pl.ANY)] * n, out_specs=[pl.BlockSpec(memory_space=pl.ANY)] * n,
        scratch_shapes=[pltpu.SemaphoreType.DMA((7 * n,)), pltpu.SemaphoreType.DMA((7 * n,)),
                        pltpu.SemaphoreType.DMA((n,))],
    )(*arrays)


def _land_shape(kind, a):
    return (N_DEV,) + a.shape if kind == "gather" else a.shape


HBM = pl.BlockSpec(memory_space=pltpu.HBM)
SEM = pl.BlockSpec(memory_space=pltpu.SEMAPHORE)


def _hbm(a):
    return pltpu.with_memory_space_constraint(a, pltpu.HBM)


def _carry_operands(kind, arrays):
    n = len(arrays)
    kinds = [kind] * n if isinstance(kind, str) else kind
    lands = [lax.empty(_land_shape(kd, a), a.dtype) for kd, a in zip(kinds, arrays)]
    sems = [pltpu.SemaphoreType.DMA((n * (N_DEV - 1),))] * 2
    return ([_hbm(a) for a in list(arrays) + lands], [HBM] * (2 * n),
            sems + [pltpu.HBM(a.shape, a.dtype) for a in list(arrays) + lands], [SEM] * 2 + [HBM] * (2 * n),
            pltpu.SemaphoreType.DMA((n,)))


def _carry_step(kind, in_refs, send_sems, recv_sems, local_sems, step, n_steps):
    n = len(in_refs) // 2
    remote, local = _spread_copies(kind, in_refs[:n], in_refs[n:], send_sems, recv_sems)
    copies = [pltpu.make_async_copy(s, d, local_sems.at[i]) for i, (s, d) in enumerate(local)]
    per_step = -(-len(remote) // n_steps)
    for s in range(-(-len(remote) // per_step)):
        @pl.when(step == s)
        def _():
            for cp in remote[s * per_step:(s + 1) * per_step]:
                cp.start()
            if s == 0:
                for cp in copies:
                    cp.start()

    @pl.when(step == n_steps - 1)
    def _():
        for cp in copies:
            cp.wait()


def _carry_state(extra_out):
    n = (len(extra_out) - 2) // 2
    return list(extra_out[2:2 + n]), list(extra_out[2 + n:]), extra_out[0], extra_out[1]


def _spread_start(arrays, kind, name):
    n = len(arrays)

    def body(*refs):
        srcs, lands = refs[:n], refs[n:2 * n]
        send_sems, recv_sems = refs[2 * n], refs[2 * n + 1]
        token = refs[4 * n + 2]
        local_sems = refs[4 * n + 3]
        remote, local = _spread_copies(kind, srcs, lands, send_sems, recv_sems)
        for cp in remote:
            cp.start()
        copies = [pltpu.make_async_copy(s, d, local_sems.at[i]) for i, (s, d) in enumerate(local)]
        for cp in copies:
            cp.start()
        for cp in copies:
            cp.wait()
        token[...] = jnp.zeros_like(token)

    sems = (pltpu.SemaphoreType.DMA((n * (N_DEV - 1),)),) * 2
    kinds = [kind] * n if isinstance(kind, str) else kind
    lands = [lax.empty(_land_shape(kd, a), a.dtype) for kd, a in zip(kinds, arrays)]
    out = pl.pallas_call(
        body, name=name,
        out_shape=sems + tuple(pltpu.HBM(a.shape, a.dtype) for a in list(arrays) + lands)
        + (jax.ShapeDtypeStruct((8, LANE), F32),),
        in_specs=[HBM] * (2 * n), out_specs=tuple([SEM] * 2 + [HBM] * (2 * n) + [pl.BlockSpec(memory_space=pltpu.VMEM)]),
        input_output_aliases={j: 2 + j for j in range(2 * n)},
        scratch_shapes=[pltpu.SemaphoreType.DMA((n,))],
        compiler_params=pltpu.CompilerParams(has_side_effects=pltpu.SideEffectType.DATAFLOW_SIDE_EFFECTING),
    )(*[_hbm(a) for a in arrays], *[_hbm(a) for a in lands])
    return (list(out[2:2 + n]), list(out[2 + n:2 + 2 * n]), out[0], out[1]), out[-1]


def _spread_wait(state, kind, after, name):
    srcs, lands, send_sems, recv_sems = state
    n = len(srcs)
    after = list(after) if isinstance(after, (list, tuple)) else [after]

    def body(*refs):
        remote, _ = _spread_copies(kind, refs[:n], refs[n:2 * n], refs[2 * n], refs[2 * n + 1])
        for cp in remote:
            cp.wait_send()
        for cp in remote:
            cp.wait_recv()

    out = pl.pallas_call(
        body, name=name,
        out_shape=tuple(pltpu.HBM(a.shape, a.dtype) for a in srcs + lands),
        in_specs=[HBM] * (2 * n) + [SEM, SEM] + [pl.BlockSpec(memory_space=pl.ANY)] * len(after),
        out_specs=tuple([HBM] * (2 * n)),
        input_output_aliases={j: j for j in range(2 * n)},
        compiler_params=pltpu.CompilerParams(has_side_effects=pltpu.SideEffectType.DATAFLOW_SIDE_EFFECTING),
    )(*srcs, *lands, send_sems, recv_sems, *after)
    return list(out[n:])


ADAM_ROWS = 128
ADAM_COLS = 256


def _adam_math(g, w, m, v):
    nm = ADAM_B1 * m + (1.0 - ADAM_B1) * g
    nv = ADAM_B2 * v + (1.0 - ADAM_B2) * (g * g)
    m_hat = nm / (1.0 - ADAM_B1 ** ADAM_STEP)
    v_hat = nv / (1.0 - ADAM_B2 ** ADAM_STEP)
    return -ADAM_LR * (m_hat / (jnp.sqrt(v_hat) + ADAM_EPS) + ADAM_WD * w), nm, nv


def _sum_parts(p_ref):
    g = p_ref[0].astype(F32)
    for s in range(1, N_DEV):
        g = g + p_ref[s].astype(F32)
    return g


def _adam_matrix(parts, w, m, v, name):
    _, r, c = w.shape
    rb = ADAM_ROWS if r % ADAM_ROWS == 0 else r
    cb = ADAM_COLS if (rb == r and c % ADAM_COLS == 0) else c

    def body(p_ref, w_ref, m_ref, v_ref, g_ref, d_ref, nm_ref, nv_ref):
        g = _sum_parts(p_ref)
        g_ref[0] = g
        d_ref[0], nm_ref[0], nv_ref[0] = _adam_math(g, w_ref[0], m_ref[0], v_ref[0])

    blk = pl.BlockSpec((1, rb, cb), lambda i, j: (0, i, j))
    return pl.pallas_call(
        body, name=name, grid=(r // rb, c // cb),
        in_specs=[pl.BlockSpec((N_DEV, rb, cb), lambda i, j: (0, i, j)), blk, blk, blk],
        out_specs=[blk] * 4, out_shape=[jax.ShapeDtypeStruct(w.shape, F32)] * 4,
        compiler_params=_params(("arbitrary", "arbitrary")),
    )(parts, w, m, v)


def _adam_vectors(parts, ws, ms, vs):
    nv = len(ws)

    def body(*refs):
        p_ref = refs[0]
        w_refs, m_refs, v_refs = refs[1:1 + nv], refs[1 + nv:1 + 2 * nv], refs[1 + 2 * nv:1 + 3 * nv]
        outs = refs[1 + 3 * nv:]
        g_all = _sum_parts(p_ref)
        for i in range(nv):
            n = w_refs[i].shape[1]
            g = g_all[i:i + 1, 0:n]
            d, nm, nvv = _adam_math(g, w_refs[i][...], m_refs[i][...], v_refs[i][...])
            outs[i][...] = g
            outs[nv + i][...] = d
            outs[2 * nv + i][...] = nm
            outs[3 * nv + i][...] = nvv
        outs[4 * nv][...] = g_all[LOSS_ROW:LOSS_ROW + 1, 0:1]

    shapes = [jax.ShapeDtypeStruct(a.shape, F32) for a in ws]
    out = pl.pallas_call(body, name="adam_vectors", out_shape=shapes * 4 + [jax.ShapeDtypeStruct((1, 1), F32)],
                         compiler_params=_params())(parts, *ws, *ms, *vs)
    return out[:nv], out[nv:2 * nv], out[2 * nv:3 * nv], out[3 * nv:4 * nv], out[4 * nv]


MATRICES = (("w_in", 1), ("gdn_conv_w", 1), ("w_out", 0), ("w_ffn_gate", 1), ("w_ffn_up", 1), ("w_ffn_down", 0))
TRANSPOSED = ("w_in", "w_ffn_gate", "w_ffn_up")
WEIGHTS = ("norm1_w", "w_in", "gdn_conv_w", "gdn_A_log", "gdn_dt_bias", "gdn_out_norm_w", "fox_f_bias", "fox_q_norm_w",
           "fox_k_norm_w", "w_out", "norm2_w", "w_ffn_gate", "w_ffn_up", "w_ffn_down", "final_norm_w")


def _join(blocks, axis):
    _, r, c = blocks.shape
    if axis == 0:
        return blocks.reshape(N_DEV * r, c)
    return blocks.transpose(1, 0, 2).reshape(r, N_DEV * c)


def _cut(full, axis):
    r, c = full.shape
    if axis == 0:
        return full.reshape(N_DEV, r // N_DEV, c)
    return full.reshape(r, N_DEV, c // N_DEV).transpose(1, 0, 2)


def kernel(x, norm1_w, w_in, gdn_conv_w, gdn_A_log, gdn_dt_bias, gdn_out_norm_w, fox_f_bias, fox_q_norm_w, fox_k_norm_w, w_out, norm2_w, w_ffn_gate, w_ffn_up, w_ffn_down, final_norm_w, loss_target, m_norm1_w, m_w_in, m_gdn_conv_w, m_gdn_A_log, m_gdn_dt_bias, m_gdn_out_norm_w, m_fox_f_bias, m_fox_q_norm_w, m_fox_k_norm_w, m_w_out, m_norm2_w, m_w_ffn_gate, m_w_ffn_up, m_w_ffn_down, m_final_norm_w, v_norm1_w, v_w_in, v_gdn_conv_w, v_gdn_A_log, v_gdn_dt_bias, v_gdn_out_norm_w, v_fox_f_bias, v_fox_q_norm_w, v_fox_k_norm_w, v_w_out, v_norm2_w, v_w_ffn_gate, v_w_ffn_up, v_w_ffn_down, v_final_norm_w):
    w = dict(norm1_w=norm1_w, w_in=w_in, gdn_conv_w=gdn_conv_w, gdn_A_log=gdn_A_log, gdn_dt_bias=gdn_dt_bias,
             gdn_out_norm_w=gdn_out_norm_w, fox_f_bias=fox_f_bias, fox_q_norm_w=fox_q_norm_w, fox_k_norm_w=fox_k_norm_w,
             w_out=w_out, norm2_w=norm2_w, w_ffn_gate=w_ffn_gate, w_ffn_up=w_ffn_up, w_ffn_down=w_ffn_down,
             final_norm_w=final_norm_w)
    m = dict(norm1_w=m_norm1_w, w_in=m_w_in, gdn_conv_w=m_gdn_conv_w, gdn_A_log=m_gdn_A_log, gdn_dt_bias=m_gdn_dt_bias,
             gdn_out_norm_w=m_gdn_out_norm_w, fox_f_bias=m_fox_f_bias, fox_q_norm_w=m_fox_q_norm_w,
             fox_k_norm_w=m_fox_k_norm_w, w_out=m_w_out, norm2_w=m_norm2_w, w_ffn_gate=m_w_ffn_gate,
             w_ffn_up=m_w_ffn_up, w_ffn_down=m_w_ffn_down, final_norm_w=m_final_norm_w)
    v = dict(norm1_w=v_norm1_w, w_in=v_w_in, gdn_conv_w=v_gdn_conv_w, gdn_A_log=v_gdn_A_log, gdn_dt_bias=v_gdn_dt_bias,
             gdn_out_norm_w=v_gdn_out_norm_w, fox_f_bias=v_fox_f_bias, fox_q_norm_w=v_fox_q_norm_w,
             fox_k_norm_w=v_fox_k_norm_w, w_out=v_w_out, norm2_w=v_norm2_w, w_ffn_gate=v_w_ffn_gate,
             w_ffn_up=v_w_ffn_up, w_ffn_down=v_w_ffn_down, final_norm_w=v_final_norm_w)
    late = ("w_out", "w_ffn_gate", "w_ffn_up", "w_ffn_down")
    xs, tgt = x[0], loss_target[0]
    vp = _vector_params({n: w[n] for n in VECTORS})

    def rows_of(d, n):
        return d[n].transpose(0, 2, 1) if n in TRANSPOSED else d[n]

    wr, mr, vr = ({n: rows_of(d, n) for n, _ in MATRICES} for d in (w, m, v))

    w_in_blocks, conv_blocks = _gather_two_level([wr["w_in"][0].astype(WIRE), w["gdn_conv_w"][0]], "gather_in")
    w_t = _join(w_in_blocks, 0)
    conv_w = _join(conv_blocks, 1)
    f = _mixer_forward(xs, vp, w_t, _small_rows(w_t), conv_w, ("gather", [wr[n][0].astype(WIRE) for n in late]))
    full = {n: _join(b, 0) for n, b in zip(late, _spread_wait(f["carried"], "gather", f["mixg"], "gather_late_wait"))}

    (h2, act, dgate, dup, dx3, dx2, dmixg, dmixf, dn2w, dfw, loss) = _tail(
        xs, f["mixg"], f["mixf"], tgt, full["w_out"], vp["n2w"], full["w_ffn_gate"], full["w_ffn_up"],
        full["w_ffn_down"], vp["fw"])
    dlate = _late_grads(f, h2, act, dgate, dup, dx3, dx2)

    grad_x, dw_in, dconv, small, state_grads, state_own = _mixer_backward(
        xs, vp, w_t, _small_rows(w_t), conv_w, f, dx2, dmixg, dmixf, ("scatter", [_cut(dlate[n], 0) for n in late]),
        scatter_own=True)
    vec = _pack_vectors(small["dn1w"], dn2w, dfw, small["gsum"], small["donw"], small["dfnw"], loss)

    state_vec, token = _spread_start([vec], "gather", "vectors_start")
    parts = dict(zip(late, _spread_wait(state_grads, "scatter", token, "grads_late_wait")))
    results = [{}, {}, {}, {}]

    def update(n):
        for d, a in zip(results, _adam_matrix(parts[n], wr[n], mr[n], vr[n], "adam_" + n)):
            d[n] = a.transpose(0, 2, 1) if n in TRANSPOSED else a

    for n in late:
        update(n)
    parts["w_in"], parts["gdn_conv_w"] = _spread_wait(state_own, "scatter", [results[0][n] for n in late], "own_wait")
    (parts_vec,) = _spread_wait(state_vec, "gather", parts["w_in"], "vectors_wait")
    update("w_in")
    update("gdn_conv_w")
    row = lambda a: a.reshape(1, -1)
    *vec_out, total_loss = _adam_vectors(parts_vec, [row(w[n]) for n in VECTORS], [row(m[n]) for n in VECTORS],
                                         [row(v[n]) for n in VECTORS])
    for d, arrs in zip(results, vec_out):
        for n, a in zip(VECTORS, arrs):
            d[n] = a.reshape(w[n].shape)
    return (total_loss[0, 0], grad_x[None], *[d[n] for d in results for n in WEIGHTS])
```

```python
import functools

import jax
import jax.numpy as jnp
from jax import lax
from jax.experimental import pallas as pl
from jax.experimental.pallas import tpu as pltpu

F32 = jnp.float32
MXU = jnp.bfloat16
WIRE = jnp.bfloat16
HI = lax.Precision.HIGHEST
EPS = 1e-6

N_DEV = 8
HEADS = 8
DH = 64
WIDTH = HEADS * DH
CHUNK = 64
LANE = 128
ROW_ALIGN = 16
TB = 256
QB = 256
VMEM_LIMIT = 60 * 1024 * 1024

ADAM_LR = 0.001
ADAM_B1 = 0.9
ADAM_B2 = 0.999
ADAM_EPS = 1e-08
ADAM_WD = 0.01
ADAM_STEP = 10

MESH = pl.DeviceIdType.MESH


def _params(sem=None):
    return pltpu.CompilerParams(dimension_semantics=sem, vmem_limit_bytes=VMEM_LIMIT)


def _resident(shape):
    n = len(shape)
    return pl.BlockSpec(shape, lambda *_: (0,) * n, pipeline_mode=pl.Buffered(1))


def _dot(a, b):
    return jnp.dot(a.astype(MXU), b.astype(MXU), preferred_element_type=F32)


def _dot_nt(a, b):
    return lax.dot_general(a.astype(MXU), b.astype(MXU), (((1,), (1,)), ((), ())), preferred_element_type=F32)


def _dot_tn(a, b):
    return lax.dot_general(a.astype(MXU), b.astype(MXU), (((0,), (0,)), ((), ())), preferred_element_type=F32)


def _hdot(a, b):
    return jnp.dot(a, b, precision=HI, preferred_element_type=F32)


def _hdot_nt(a, b):
    return lax.dot_general(a, b, (((1,), (1,)), ((), ())), precision=HI, preferred_element_type=F32)


def _hdot_tn(a, b):
    return lax.dot_general(a, b, (((0,), (0,)), ((), ())), precision=HI, preferred_element_type=F32)


def _sigmoid(x):
    return 0.5 * jnp.tanh(0.5 * x) + 0.5


def _softplus(x):
    return jnp.maximum(x, 0.0) + jnp.log(1.0 + jnp.exp(-jnp.abs(x)))


def _head_sum_matrix():
    ri = lax.broadcasted_iota(jnp.int32, (LANE, LANE), 0) // DH
    ci = lax.broadcasted_iota(jnp.int32, (LANE, LANE), 1) // DH
    return (ri == ci).astype(F32)


def _group_sum(a, ones_matrix):
    hi = a.astype(jnp.bfloat16)
    lo = (a - hi.astype(F32)).astype(jnp.bfloat16)
    m = ones_matrix.astype(jnp.bfloat16)
    return jnp.dot(hi, m, preferred_element_type=F32) + jnp.dot(lo, m, preferred_element_type=F32)


def _shift_down(x, s):
    return pltpu.roll(x, s, 0)


def _shift_up(x, s):
    return pltpu.roll(x, x.shape[0] - s, 0)


ROWS_A = 4 * WIDTH
ROWS_B = ROWS_A + 2 * HEADS
ROWS_C = ROWS_B + 4 * WIDTH


def _small_rows(w_t):
    return jnp.concatenate([w_t[ROWS_A:ROWS_B], w_t[ROWS_C:], jnp.zeros((LANE - 3 * HEADS, w_t.shape[1]), w_t.dtype)])


def _inproj(x, n1w, w_t, ws_t):
    t, d = x.shape
    tb = min(TB, t)

    def body(x_ref, nw_ref, wt_ref, ws_ref, h_ref, pm_ref, ps_ref):
        xv = x_ref[...]
        r = lax.rsqrt(jnp.mean(xv * xv, axis=-1, keepdims=True) + EPS)
        h = (xv * r * nw_ref[...]).astype(MXU)
        h_ref[...] = h
        pm_ref[:, 0:ROWS_A] = _dot_nt(h, wt_ref[0:ROWS_A, :])
        pm_ref[:, ROWS_A:2 * ROWS_A] = _dot_nt(h, wt_ref[ROWS_B:ROWS_C, :])
        ps_ref[...] = _dot_nt(h, ws_ref[...])

    return pl.pallas_call(
        body, name="inproj", grid=(t // tb,),
        in_specs=[pl.BlockSpec((tb, d), lambda i: (i, 0)), _resident((1, d)), _resident(w_t.shape), _resident((LANE, d))],
        out_specs=[pl.BlockSpec((tb, d), lambda i: (i, 0)), pl.BlockSpec((tb, 2 * ROWS_A), lambda i: (i, 0)),
                   pl.BlockSpec((tb, LANE), lambda i: (i, 0))],
        out_shape=[jax.ShapeDtypeStruct((t, d), MXU), jax.ShapeDtypeStruct((t, 2 * ROWS_A), F32),
                   jax.ShapeDtypeStruct((t, LANE), F32)],
        compiler_params=_params(("arbitrary",)),
    )(x, n1w, w_t, ws_t)


def _inproj_bwd(x, n1w, dx2, dgdn, dz, dfox, dfg, dps, w_t, ws_t, carry=None):
    t, d = x.shape
    tb = min(TB, t)
    w3 = 3 * WIDTH
    c_in, c_in_specs, c_out_shape, c_out_specs, c_sem = _carry_operands(*carry) if carry else ([], [], [], [], None)
    n_c = len(c_in)

    def body(*refs):
        x_ref, nw_ref, dx2_ref, dgdn_ref, dz_ref, dfox_ref, dfg_ref, dps_ref, wm_ref, ws_ref = refs[:10]
        gx_ref, dnw_ref = refs[10 + n_c:12 + n_c]
        if carry:
            _carry_step(carry[0], refs[10:10 + n_c], refs[12 + n_c], refs[13 + n_c], refs[-1], pl.program_id(0), t // tb)
        dh = _dot(dgdn_ref[...], wm_ref[0:w3, :])
        dh += _dot(dz_ref[...], wm_ref[w3:ROWS_A, :])
        dh += _dot(dfox_ref[...], wm_ref[ROWS_B:ROWS_B + w3, :])
        dh += _dot(dfg_ref[...], wm_ref[ROWS_B + w3:ROWS_C, :])
        dh += _dot(dps_ref[...], ws_ref[...])
        xv = x_ref[...]
        r = lax.rsqrt(jnp.mean(xv * xv, axis=-1, keepdims=True) + EPS)
        xn = xv * r

        @pl.when(pl.program_id(0) == 0)
        def _():
            dnw_ref[...] = jnp.zeros_like(dnw_ref)

        dnw_ref[...] += jnp.sum(dh * xn, axis=0, keepdims=True)
        g = dh * nw_ref[...]
        gx_ref[...] = dx2_ref[...] + r * (g - xn * jnp.mean(g * xn, axis=-1, keepdims=True))

    def tok(n):
        return pl.BlockSpec((tb, n), lambda i: (i, 0))

    out = pl.pallas_call(
        body, name="inproj_bwd", grid=(t // tb,),
        in_specs=[tok(d), _resident((1, d)), tok(d), tok(w3), tok(WIDTH), tok(w3), tok(WIDTH), tok(LANE),
                  _resident(w_t.shape), _resident(ws_t.shape)] + c_in_specs,
        out_specs=[tok(d), pl.BlockSpec((1, d), lambda i: (0, 0))] + c_out_specs,
        out_shape=[jax.ShapeDtypeStruct((t, d), F32), jax.ShapeDtypeStruct((1, d), F32)] + c_out_shape,
        input_output_aliases={10 + j: 4 + j for j in range(n_c)},
        scratch_shapes=[c_sem] if carry else [],
        compiler_params=_params(("arbitrary",)),
    )(x, n1w, dx2, dgdn, dz, dfox, dfg, dps, w_t, ws_t, *c_in)
    return (out[0], out[1], _carry_state(out[2:])) if carry else tuple(out)


def _gate_lanes(shape):
    lane = lax.broadcasted_iota(jnp.int32, shape, 1)
    return lane < HEADS, (lane >= HEADS) & (lane < 2 * HEADS), (lane >= 2 * HEADS) & (lane < 3 * HEADS)


def _block_masks():
    ri = lax.broadcasted_iota(jnp.int32, (LANE, LANE), 0)
    ci = lax.broadcasted_iota(jnp.int32, (LANE, LANE), 1)
    same = (ri // CHUNK) == (ci // CHUNK)
    return ((ri >= ci).astype(F32), (ri <= ci).astype(F32), (same & (ri >= ci)).astype(F32),
            (same & (ri <= ci)).astype(F32), same.astype(F32))


def _gates(ps, gparams):
    t = ps.shape[0]
    nb = t // LANE

    def body(ps_ref, gp_ref, out_ref, run_ref, tot_ref, runt_ref):
        p = ps_ref[...]
        is_b, is_a, is_f = _gate_lanes(p.shape)
        z = p + gp_ref[0:1, :]
        neg_exp_a = -jnp.exp(gp_ref[1:2, :])
        glog = neg_exp_a * _softplus(z)
        logf = -_softplus(-z)
        out_ref[...] = jnp.where(is_b, _sigmoid(p), jnp.where(is_a, glog, jnp.where(is_f, logf, 0.0)))
        tril, _, tril_c, _, same_c = _block_masks()
        off = jnp.zeros((1, LANE), F32)
        for b in range(nb):
            rows = slice(b * LANE, (b + 1) * LANE)
            blk = out_ref[rows, :]
            ga = jnp.where(is_a[:LANE], blk, 0.0)
            fb = _hdot(tril, jnp.where(is_f[:LANE], blk, 0.0)) + off
            run = fb + _hdot(tril_c, ga)
            run_ref[rows, :] = run
            runt_ref[:, rows] = run.T
            tot_ref[rows, :] = _hdot(same_c, ga)
            off = fb[LANE - 1:LANE, :]

    return pl.pallas_call(
        body, name="gates",
        out_shape=[jax.ShapeDtypeStruct((t, LANE), F32)] * 3 + [jax.ShapeDtypeStruct((LANE, t), F32)],
        compiler_params=_params(),
    )(ps, gparams)


def _gates_bwd(ps, gparams, dcol_g, dtot_g, dcol_f, drow_g, drow_f):
    t = ps.shape[0]
    nb = t // LANE

    def body(ps_ref, gp_ref, dcg_ref, dtg_ref, dcf_ref, drg_ref, drf_ref, dps_ref, sums_ref, dl_ref, d0_ref, tr_ref):
        p = ps_ref[...]
        is_b, is_a, is_f = _gate_lanes(p.shape)
        _, triu, _, triu_c, same_c = _block_masks()
        tr_ref[...] = jnp.zeros_like(tr_ref)
        off = jnp.zeros((1, LANE), F32)
        for b in reversed(range(nb)):
            rows = slice(b * LANE, (b + 1) * LANE)
            tr_ref[HEADS:2 * HEADS, :] = drg_ref[:, rows]
            tr_ref[2 * HEADS:3 * HEADS, :] = drf_ref[:, rows]
            d = dcg_ref[rows, :] + dcf_ref[rows, :] + tr_ref[...].T
            d0_ref[rows, :] = d
            dlf = _hdot(triu, jnp.where(is_f[:LANE], d, 0.0)) + off
            dla = (_hdot(triu_c, jnp.where(is_a[:LANE], d, 0.0))
                   + _hdot(same_c, jnp.where(is_a[:LANE], dtg_ref[rows, :], 0.0)))
            dl_ref[rows, :] = dlf + dla
            off = dlf[0:1, :]
        z = p + gp_ref[0:1, :]
        neg_exp_a = -jnp.exp(gp_ref[1:2, :])
        sb = _sigmoid(p)
        glog = neg_exp_a * _softplus(z)
        dl = dl_ref[...]
        dp = jnp.where(is_b, d0_ref[...] * sb * (1.0 - sb),
                       jnp.where(is_a, dl * neg_exp_a * _sigmoid(z), jnp.where(is_f, dl * _sigmoid(-z), 0.0)))
        dps_ref[...] = dp
        s_a = jnp.sum(jnp.where(is_a, dl * glog, 0.0), axis=0, keepdims=True)
        s_p = jnp.sum(jnp.where(is_b, 0.0, dp), axis=0, keepdims=True)
        row = lax.broadcasted_iota(jnp.int32, (8, LANE), 0)
        from_a = pltpu.roll(jnp.where(row == 0, s_a, jnp.where(row == 1, s_p, 0.0)), LANE - HEADS, 1)
        from_f = pltpu.roll(jnp.where(row == 2, s_p, 0.0), LANE - 2 * HEADS, 1)
        lane = lax.broadcasted_iota(jnp.int32, (8, LANE), 1)
        sums_ref[...] = jnp.where(lane < HEADS, from_a + from_f, 0.0)

    return pl.pallas_call(
        body, name="gates_bwd",
        out_shape=[jax.ShapeDtypeStruct((t, LANE), F32), jax.ShapeDtypeStruct((8, LANE), F32)],
        scratch_shapes=[pltpu.VMEM((t, LANE), F32), pltpu.VMEM((t, LANE), F32), pltpu.VMEM((LANE, LANE), F32)],
        compiler_params=_params(),
    )(ps, gparams, dcol_g, dtot_g, dcol_f, drow_g, drow_f)


def _conv(xv, w):
    acc = w[3:4, :] * xv
    for s in range(1, 4):
        acc += w[3 - s:4 - s, :] * _shift_down(xv, s)
    return acc


PREP_ROWS = 256
HALO = 8
PREP_UNROLL = 2


def _tile_loop(n, tile, init):
    if n % PREP_UNROLL:
        return lax.fori_loop(0, n, tile, init)

    def trip(g, carry):
        for u in range(PREP_UNROLL):
            carry = tile(g * PREP_UNROLL + u, carry)
        return carry

    return lax.fori_loop(0, n // PREP_UNROLL, trip, init)


def _tile_rows(r):
    return pl.ds(pl.multiple_of(r * PREP_ROWS, PREP_ROWS), PREP_ROWS)


def _gdn_prep(pm, conv_w):
    t = pm.shape[0]
    nj = WIDTH // LANE
    win = PREP_ROWS + HALO

    def body(x_ref, w_ref, o_ref, xp_ref):
        kind = pl.program_id(0)
        xp_ref[0:HALO, :] = jnp.zeros((HALO, LANE), F32)
        xp_ref[HALO:, :] = x_ref[...]
        w = w_ref[...]
        hs = _head_sum_matrix()

        def tile(r, _, normed):
            xw = xp_ref[pl.ds(pl.multiple_of(r * PREP_ROWS, PREP_ROWS), win), :]
            acc = _conv(xw, w)[HALO:]
            out = acc * _sigmoid(acc)
            if normed:
                out = out * lax.rsqrt(_group_sum(out * out, hs) + EPS)
            o_ref[0, 0, _tile_rows(r), :] = out[:, :DH]
            o_ref[0, 1, _tile_rows(r), :] = out[:, DH:]
            return 0

        @pl.when(kind < 2)
        def _():
            _tile_loop(t // PREP_ROWS, functools.partial(tile, normed=True), 0)

        @pl.when(kind == 2)
        def _():
            _tile_loop(t // PREP_ROWS, functools.partial(tile, normed=False), 0)

    return pl.pallas_call(
        body, name="gdn_prep", grid=(3, nj),
        in_specs=[pl.BlockSpec((t, LANE), lambda i, j: (0, i * nj + j)),
                  pl.BlockSpec((4, LANE), lambda i, j: (0, i * nj + j))],
        out_specs=pl.BlockSpec((1, 2, t, DH), lambda i, j: (i, j, 0, 0)),
        out_shape=jax.ShapeDtypeStruct((3, HEADS, t, DH), F32),
        scratch_shapes=[pltpu.VMEM((t + HALO, LANE), F32)],
        compiler_params=_params(("arbitrary", "arbitrary")),
    )(pm, conv_w)


def _gdn_prep_bwd(pm, conv_w, dqkv):
    t = pm.shape[0]
    nj = WIDTH // LANE
    win = PREP_ROWS + 2 * HALO

    def body(x_ref, w_ref, d_ref, dx_ref, dw_ref, xp_ref, dp_ref):
        kind = pl.program_id(0)
        zeros = jnp.zeros((HALO, LANE), F32)
        for ref in (xp_ref, dp_ref):
            ref[0:HALO, :] = zeros
            ref[HALO + t:, :] = zeros
        xp_ref[HALO:HALO + t, :] = x_ref[...]
        dp_ref[HALO:HALO + t, 0:DH] = d_ref[0, 0]
        dp_ref[HALO:HALO + t, DH:] = d_ref[0, 1]
        w = w_ref[...]
        hs = _head_sum_matrix()
        rows = lax.broadcasted_iota(jnp.int32, (win, LANE), 0)
        in_tile = (rows >= HALO) & (rows < HALO + PREP_ROWS)

        def tile(r, dw, normed):
            start = pl.multiple_of(r * PREP_ROWS, PREP_ROWS)
            xw = xp_ref[pl.ds(start, win), :]
            dy = dp_ref[pl.ds(start, win), :]
            acc = _conv(xw, w)
            sg = _sigmoid(acc)
            if normed:
                y = acc * sg
                rn = lax.rsqrt(_group_sum(y * y, hs) + EPS)
                yn = y * rn
                dy = rn * (dy - yn * _group_sum(dy * yn, hs))
            dacc = dy * sg * (1.0 + acc * (1.0 - sg))
            dx = w[3:4, :] * dacc
            for s in range(1, 4):
                dx += w[3 - s:4 - s, :] * _shift_up(dacc, s)
            dx_ref[_tile_rows(r), :] = dx[HALO:HALO + PREP_ROWS]
            dm = jnp.where(in_tile, dacc, 0.0)
            return tuple(dw[i] + jnp.sum(dm * (xw if i == 3 else _shift_down(xw, 3 - i)), axis=0, keepdims=True)
                         for i in range(4))

        def run(normed):
            dw = _tile_loop(t // PREP_ROWS, functools.partial(tile, normed=normed),
                            tuple(jnp.zeros((1, LANE), F32) for _ in range(4)))
            for i in range(4):
                dw_ref[i:i + 1, :] = dw[i]

        pl.when(kind < 2)(functools.partial(run, True))
        pl.when(kind == 2)(functools.partial(run, False))

    return pl.pallas_call(
        body, name="gdn_prep_bwd", grid=(3, nj),
        in_specs=[pl.BlockSpec((t, LANE), lambda i, j: (0, i * nj + j)),
                  pl.BlockSpec((4, LANE), lambda i, j: (0, i * nj + j)),
                  pl.BlockSpec((1, 2, t, DH), lambda i, j: (i, j, 0, 0))],
        out_specs=[pl.BlockSpec((t, LANE), lambda i, j: (0, i * nj + j)),
                   pl.BlockSpec((4, LANE), lambda i, j: (0, i * nj + j))],
        out_shape=[jax.ShapeDtypeStruct((t, 3 * WIDTH), F32), jax.ShapeDtypeStruct((4, 3 * WIDTH), F32)],
        scratch_shapes=[pltpu.VMEM((t + 2 * HALO, LANE), F32), pltpu.VMEM((t + 2 * HALO, LANE), F32)],
        compiler_params=_params(("arbitrary", "arbitrary")),
    )(pm, conv_w, dqkv)


FOX_COL0 = 4 * WIDTH // LANE


def _fox_prep(pm, nw):
    t = pm.shape[0]
    nj = WIDTH // LANE

    def body(x_ref, w_ref, o_ref):
        kind = pl.program_id(0)
        hs = _head_sum_matrix()
        wk = w_ref[pl.ds(kind, 1), :]

        def tile(r, _, normed):
            out = x_ref[_tile_rows(r), :]
            if normed:
                out = out * lax.rsqrt(_group_sum(out * out, hs) * (1.0 / DH) + EPS) * wk
            o_ref[0, 0, _tile_rows(r), :] = out[:, :DH]
            o_ref[0, 1, _tile_rows(r), :] = out[:, DH:]
            return 0

        @pl.when(kind < 2)
        def _():
            _tile_loop(t // PREP_ROWS, functools.partial(tile, normed=True), 0)

        @pl.when(kind == 2)
        def _():
            _tile_loop(t // PREP_ROWS, functools.partial(tile, normed=False), 0)

    return pl.pallas_call(
        body, name="fox_prep", grid=(3, nj),
        in_specs=[pl.BlockSpec((t, LANE), lambda i, j: (0, FOX_COL0 + i * nj + j)),
                  pl.BlockSpec((3, LANE), lambda i, j: (0, 0))],
        out_specs=pl.BlockSpec((1, 2, t, DH), lambda i, j: (i, j, 0, 0)),
        out_shape=jax.ShapeDtypeStruct((3, HEADS, t, DH), F32),
        compiler_params=_params(("arbitrary", "arbitrary")),
    )(pm, nw)


def _fox_prep_bwd(pm, nw, dqkv):
    t = pm.shape[0]
    nj = WIDTH // LANE

    def body(x_ref, w_ref, d_ref, dx_ref, dw_ref):
        kind = pl.program_id(0)
        hs = _head_sum_matrix()
        wk = w_ref[pl.ds(kind, 1), :]

        def tile(r, dw):
            xv = x_ref[_tile_rows(r), :]
            rn = lax.rsqrt(_group_sum(xv * xv, hs) * (1.0 / DH) + EPS)
            xn = xv * rn
            d = jnp.concatenate([d_ref[0, 0, _tile_rows(r), :], d_ref[0, 1, _tile_rows(r), :]], axis=1)
            g = d * wk
            dx_ref[_tile_rows(r), :] = rn * (g - xn * _group_sum(g * xn, hs) * (1.0 / DH))
            return dw + jnp.sum(d * xn, axis=0, keepdims=True)

        def copy_tile(r, _):
            dx_ref[_tile_rows(r), :] = jnp.concatenate([d_ref[0, 0, _tile_rows(r), :], d_ref[0, 1, _tile_rows(r), :]], axis=1)
            return 0

        @pl.when(kind < 2)
        def _():
            dw_ref[0, 0] = _tile_loop(t // PREP_ROWS, tile, jnp.zeros((1, LANE), F32))

        @pl.when(kind == 2)
        def _():
            _tile_loop(t // PREP_ROWS, copy_tile, 0)
            dw_ref[0, 0] = jnp.zeros((1, LANE), F32)

    return pl.pallas_call(
        body, name="fox_prep_bwd", grid=(3, nj),
        in_specs=[pl.BlockSpec((t, LANE), lambda i, j: (0, FOX_COL0 + i * nj + j)),
                  pl.BlockSpec((3, LANE), lambda i, j: (0, 0)),
                  pl.BlockSpec((1, 2, t, DH), lambda i, j: (i, j, 0, 0))],
        out_specs=[pl.BlockSpec((t, LANE), lambda i, j: (0, i * nj + j)),
                   pl.BlockSpec((1, 1, 1, LANE), lambda i, j: (i, j, 0, 0))],
        out_shape=[jax.ShapeDtypeStruct((t, 3 * WIDTH), F32), jax.ShapeDtypeStruct((3, nj, 1, LANE), F32)],
        compiler_params=_params(("arbitrary", "arbitrary")),
    )(pm, nw, dqkv)


SC = 256
CPS = SC // CHUNK
GDN_HP_FWD = 4
GDN_HP_BWD = 2
Q_SCALE = DH ** -0.5


def _sc_masks():
    ri = lax.broadcasted_iota(jnp.int32, (SC, SC), 0)
    ci = lax.broadcasted_iota(jnp.int32, (SC, SC), 1)
    same = (ri // CHUNK) == (ci // CHUNK)
    return same & (ri >= ci), same & (ri > ci), ri == ci


def _unit_lower_inverses(ms, eye):
    invs = [jnp.where(eye, 1.0, 0.0) + m for m in ms]
    for _ in range(5):
        ms = [_dot(m, m) for m in ms]
        invs = [inv + _dot(inv, m) for inv, m in zip(invs, ms)]
    return invs


def _lane_col(blk, lane_idx):
    lane = lax.broadcasted_iota(jnp.int32, blk.shape, 1)
    return jnp.sum(jnp.where(lane == lane_idx, blk, 0.0), axis=1, keepdims=True)


def _to_lane(col, lane_idx):
    lane = lax.broadcasted_iota(jnp.int32, (col.shape[0], LANE), 1)
    return jnp.where(lane == lane_idx, col, 0.0)


def _gdn_columns(gates_ref, run_ref, tot_ref, runt_ref, rows, h):
    return (_lane_col(gates_ref[rows, :], h), _lane_col(run_ref[rows, :], HEADS + h),
            _lane_col(tot_ref[rows, :], HEADS + h), runt_ref[pl.ds(h, 1), rows])


def _gdn_local(q, k, beta, gc, gl, grow, causal):
    decay = jnp.exp(jnp.where(causal, gc - grow, -1e30))
    egc = jnp.exp(gc)
    ekd = jnp.exp(gl - gc)
    qs = q * Q_SCALE
    kb = k * beta
    kk = _dot_nt(kb, k)
    qk = _dot_nt(qs, k)
    return beta, gl, decay, egc, ekd, qs, kb, kk, qk, jnp.where(causal, qk * decay, 0.0)


def _chunk_rows(c):
    return pl.ds(pl.multiple_of(c * CHUNK, CHUNK), CHUNK)


def _sc_rows(b):
    return pl.ds(pl.multiple_of(b * SC, SC), SC)


def _gdn_fwd(qkv, gates, run, tot, run_t):
    t = qkv.shape[2]
    nc = t // CHUNK
    nsc = t // SC

    hp_n = GDN_HP_FWD
    heads = range(hp_n)

    def body(qkv_ref, gates_ref, run_ref, tot_ref, runt_ref, o_ref, st_ref, inv_ref, kc_s, qc_s, g_s, au_s):
        hp = pl.program_id(0)
        causal, strict, eye = _sc_masks()

        def local(b, _):
            rows = _sc_rows(b)
            loc = [_gdn_local(qkv_ref[0, hh, rows, :], qkv_ref[1, hh, rows, :],
                              *_gdn_columns(gates_ref, run_ref, tot_ref, runt_ref, rows, hp * hp_n + hh), causal)
                   for hh in heads]
            invs = _unit_lower_inverses([-jnp.where(strict, l[7] * l[2], 0.0) for l in loc], eye)
            us, ws = [], []
            for hh in heads:
                beta, _, _, egc, _, _, kb, _, _, _ = loc[hh]
                inv_ref[hh, rows, :] = invs[hh].astype(inv_ref.dtype)
                us.append(_dot(invs[hh], qkv_ref[2, hh, rows, :] * beta))
                ws.append(_dot(invs[hh], kb * egc))
            for hh in heads:
                _, _, _, egc, ekd, qs, _, _, _, attn = loc[hh]
                g_s[hh, rows, :] = qs * egc - _dot(attn, ws[hh])
                au_s[hh, rows, :] = _dot(attn, us[hh])
                kd = qkv_ref[1, hh, rows, :] * ekd
                for j in range(CPS):
                    sl = slice(j * CHUNK, (j + 1) * CHUNK)
                    kc_s[hh, b * CPS + j] = _dot_tn(kd[sl], ws[hh][sl])
                    qc_s[hh, b * CPS + j] = _dot_tn(kd[sl], us[hh][sl])
            return 0

        lax.fori_loop(0, nsc, local, 0)

        def step(c, states):
            rows = _chunk_rows(c)
            tot_row = tot_ref[pl.ds(c * CHUNK, 1), :]
            new = []
            for hh in heads:
                s = states[hh]
                st_ref[hh, c] = s
                o_ref[hh, rows, :] = _dot(g_s[hh, rows, :], s) + au_s[hh, rows, :]
                egl = jnp.exp(_lane_col(tot_row, HEADS + hp * hp_n + hh))
                new.append(egl * s - _dot(kc_s[hh, c], s) + qc_s[hh, c])
            return tuple(new)

        lax.fori_loop(0, nc, step, tuple(jnp.zeros((DH, DH), F32) for _ in heads))

    whole = pl.BlockSpec((t, LANE), lambda h: (0, 0))
    once = dict(pipeline_mode=pl.Buffered(1))
    return pl.pallas_call(
        body, name="gdn_fwd", grid=(HEADS // hp_n,),
        in_specs=[pl.BlockSpec((3, hp_n, t, DH), lambda h: (0, h, 0, 0), **once), whole, whole, whole,
                  pl.BlockSpec((HEADS, t), lambda h: (1, 0))],
        out_specs=[pl.BlockSpec((hp_n, t, DH), lambda h: (h, 0, 0), **once),
                   pl.BlockSpec((hp_n, nc, DH, DH), lambda h: (h, 0, 0, 0), **once),
                   pl.BlockSpec((hp_n, t, SC), lambda h: (h, 0, 0), **once)],
        out_shape=[jax.ShapeDtypeStruct((HEADS, t, DH), F32), jax.ShapeDtypeStruct((HEADS, nc, DH, DH), F32),
                   jax.ShapeDtypeStruct((HEADS, t, SC), MXU)],
        scratch_shapes=[pltpu.VMEM((hp_n, nc, DH, DH), F32), pltpu.VMEM((hp_n, nc, DH, DH), F32),
                        pltpu.VMEM((hp_n, t, DH), F32), pltpu.VMEM((hp_n, t, DH), F32)],
        compiler_params=_params(("arbitrary",)),
    )(qkv, gates, run, tot, run_t)


def _gdn_bwd(qkv, gates, run, tot, run_t, inv, states, do):
    t = qkv.shape[2]
    nc = t // CHUNK
    nsc = t // SC

    hp_n = GDN_HP_BWD
    heads = range(hp_n)

    def body(qkv_ref, gates_ref, run_ref, tot_ref, runt_ref, inv_ref, st_ref, do_ref,
             dqkv_ref, dcol_ref, dtot_ref, drow_ref, u_s, w_s, kc_s, h_s, dsn_s):
        hp = pl.program_id(0)
        causal, strict, _ = _sc_masks()

        @pl.when(hp == 0)
        def _():
            dcol_ref[...] = jnp.zeros_like(dcol_ref)
            dtot_ref[...] = jnp.zeros_like(dtot_ref)

        def local_of(hh, rows):
            return _gdn_local(qkv_ref[0, hh, rows, :], qkv_ref[1, hh, rows, :],
                              *_gdn_columns(gates_ref, run_ref, tot_ref, runt_ref, rows, hp * hp_n + hh), causal)

        def local(b, _):
            rows = _sc_rows(b)
            loc = [local_of(hh, rows) for hh in heads]
            us, ws = [], []
            for hh in heads:
                beta, _, _, egc, _, _, kb, _, _, _ = loc[hh]
                inv_b = inv_ref[hh, rows, :]
                us.append(_dot(inv_b, qkv_ref[2, hh, rows, :] * beta))
                ws.append(_dot(inv_b, kb * egc))
            gs = [loc[hh][5] * loc[hh][3] - _dot(loc[hh][9], ws[hh]) for hh in heads]
            for hh in heads:
                u_s[hh, rows, :] = us[hh]
                w_s[hh, rows, :] = ws[hh]
                kd = qkv_ref[1, hh, rows, :] * loc[hh][4]
                dout = do_ref[hh, rows, :]
                for j in range(CPS):
                    sl = slice(j * CHUNK, (j + 1) * CHUNK)
                    kc_s[hh, b * CPS + j] = _dot_tn(kd[sl], ws[hh][sl])
                    h_s[hh, b * CPS + j] = _dot_tn(gs[hh][sl], dout[sl])
            return 0

        lax.fori_loop(0, nsc, local, 0)

        def step(i, dss):
            c = nc - 1 - i
            tot_row = tot_ref[pl.ds(c * CHUNK, 1), :]
            new = []
            for hh in heads:
                ds = dss[hh]
                dsn_s[hh, c] = ds
                egl = jnp.exp(_lane_col(tot_row, HEADS + hp * hp_n + hh))
                new.append(egl * ds - _dot_tn(kc_s[hh, c], ds) + h_s[hh, c])
            return tuple(new)

        lax.fori_loop(0, nc, step, tuple(jnp.zeros((DH, DH), F32) for _ in heads))

        def back(b, _):
            rows = _sc_rows(b)
            first = lax.broadcasted_iota(jnp.int32, (CHUNK, 1), 0) == 0
            loc = [local_of(hh, rows) for hh in heads]
            mid = []
            for hh in heads:
                beta, gl, decay, egc, ekd, qs, kb, kk, qk, attn = loc[hh]
                u, w = u_s[hh, rows, :], w_s[hh, rows, :]
                kd = qkv_ref[1, hh, rows, :] * ekd
                dout = do_ref[hh, rows, :]
                dg_p, dkd_p, dw_p, du_p, dgl_p = [], [], [], [], []
                for j in range(CPS):
                    sl = slice(j * CHUNK, (j + 1) * CHUNK)
                    s = st_ref[hh, b * CPS + j]
                    dsn = dsn_s[hh, b * CPS + j]
                    dkc = -_dot_nt(dsn, s)
                    dg_p.append(_dot_nt(dout[sl], s))
                    dkd_p.append(_dot_nt(w[sl], dkc) + _dot_nt(u[sl], dsn))
                    dw_p.append(_dot(kd[sl], dkc))
                    du_p.append(_dot(kd[sl], dsn))
                    degl = jnp.sum(jnp.sum(s * dsn, axis=1, keepdims=True), axis=0, keepdims=True)
                    dgl_p.append(jnp.where(first, degl * jnp.exp(gl[j * CHUNK:j * CHUNK + 1, :]), 0.0))
                dg, dkd = jnp.concatenate(dg_p, axis=0), jnp.concatenate(dkd_p, axis=0)
                da = jnp.where(causal, _dot_nt(dout, u) - _dot_nt(dg, w), 0.0)
                at = _dot_tn(attn, jnp.concatenate([dout, dg], axis=1))
                du = at[:, :DH] + jnp.concatenate(du_p, axis=0)
                dw = jnp.concatenate(dw_p, axis=0) - at[:, DH:]
                mid.append((kd, dg, dkd, da, du, dw, jnp.concatenate(dgl_p, axis=0)))
            inv_ts = [inv_ref[hh, rows, :].astype(F32).T for hh in heads]
            its = [_dot(inv_ts[hh], jnp.concatenate([mid[hh][4], mid[hh][5]], axis=1)) for hh in heads]
            dinvs = []
            for hh in heads:
                beta, _, _, egc, _, _, kb, _, _, _ = loc[hh]
                dinvs.append(_dot_nt(mid[hh][4], qkv_ref[2, hh, rows, :] * beta) + _dot_nt(mid[hh][5], kb * egc))
            half = [_dot(inv_ts[hh], dinvs[hh]) for hh in heads]
            dls = [jnp.where(strict, -_dot(half[hh], inv_ts[hh]), 0.0) for hh in heads]
            for hh in heads:
                head = hp * hp_n + hh
                beta, gl, decay, egc, ekd, qs, kb, kk, qk, attn = loc[hh]
                kd, dg, dkd, da, du, dw, dgl_first = mid[hh]
                k, v = qkv_ref[1, hh, rows, :], qkv_ref[2, hh, rows, :]
                dvb, dkbe = its[hh][:, :DH], its[hh][:, DH:]
                dl = dls[hh]
                dlogd = (dl * kk + da * qk) * decay
                dkk = dl * decay
                dqk = da * decay
                dkb = _dot(dkk, k) + dkbe * egc
                dqs = _dot(dqk, k) + dg * egc
                dk = (_dot_tn(jnp.concatenate([dkk, dqk], axis=0), jnp.concatenate([kb, qs], axis=0))
                      + dkd * ekd + dkb * beta)
                dkd_kd = jnp.sum(dkd * kd, axis=1, keepdims=True)
                dgc = (jnp.sum(dlogd, axis=1, keepdims=True) + jnp.sum(dg * qs, axis=1, keepdims=True) * egc
                       + jnp.sum(dkbe * (kb * egc), axis=1, keepdims=True) - dkd_kd)
                dbeta = jnp.sum(dkb * k, axis=1, keepdims=True) + jnp.sum(dvb * v, axis=1, keepdims=True)
                dqkv_ref[0, hh, rows, :] = dqs * Q_SCALE
                dqkv_ref[1, hh, rows, :] = dk
                dqkv_ref[2, hh, rows, :] = dvb * beta
                dcol_ref[rows, :] += _to_lane(dbeta, head) + _to_lane(dgc, HEADS + head)
                dtot_ref[rows, :] += _to_lane(dkd_kd + dgl_first, HEADS + head)
                drow_ref[pl.ds(head, 1), rows] = -jnp.sum(dlogd, axis=0, keepdims=True)
            return 0

        lax.fori_loop(0, nsc, back, 0)

    whole = pl.BlockSpec((t, LANE), lambda h: (0, 0))
    rowspec = pl.BlockSpec((HEADS, t), lambda h: (0, 0))
    sq = pltpu.VMEM((hp_n, nc, DH, DH), F32)
    per_head = pltpu.VMEM((hp_n, t, DH), F32)
    return pl.pallas_call(
        body, name="gdn_bwd", grid=(HEADS // hp_n,),
        in_specs=[pl.BlockSpec((3, hp_n, t, DH), lambda h: (0, h, 0, 0)), whole, whole, whole,
                  pl.BlockSpec((HEADS, t), lambda h: (1, 0)),
                  pl.BlockSpec((hp_n, t, SC), lambda h: (h, 0, 0)), pl.BlockSpec((hp_n, nc, DH, DH), lambda h: (h, 0, 0, 0)),
                  pl.BlockSpec((hp_n, t, DH), lambda h: (h, 0, 0))],
        out_specs=[pl.BlockSpec((3, hp_n, t, DH), lambda h: (0, h, 0, 0)), whole, whole, rowspec],
        out_shape=[jax.ShapeDtypeStruct((3, HEADS, t, DH), F32), jax.ShapeDtypeStruct((t, LANE), F32),
                   jax.ShapeDtypeStruct((t, LANE), F32), jax.ShapeDtypeStruct((HEADS, t), F32)],
        scratch_shapes=[per_head, per_head, sq, sq, sq],
        compiler_params=_params(("arbitrary",)),
    )(qkv, gates, run, tot, run_t, inv, states, do)


FOX_HP_FWD = 4
FOX_HP_BWD = 4


def _wide_t(a):
    r = a.shape[0]
    return jnp.concatenate([a, jnp.zeros((r, LANE - DH), F32)], axis=1).T[:DH]


def _tall_t(a):
    r = a.shape[1]
    return jnp.concatenate([a, jnp.zeros((LANE - DH, r), F32)], axis=0).T[:, :DH]


def _key_side_f(run_blk, head, qb):
    col = jnp.broadcast_to(_lane_col(run_blk, 2 * HEADS + head), (run_blk.shape[0], LANE))
    return jnp.concatenate([col] * (qb // LANE), axis=1)


def _diag_mask(qb):
    return lax.broadcasted_iota(jnp.int32, (qb, qb), 0) <= lax.broadcasted_iota(jnp.int32, (qb, qb), 1)


def _fox_fwd(qkv, run, run_t, carry=None):
    t = qkv.shape[2]
    qb = min(QB, t)
    nq = t // qb
    hp_n = FOX_HP_FWD
    c_in, c_in_specs, c_out_shape, c_out_specs, c_sem = _carry_operands(*carry) if carry else ([], [], [], [], None)
    n_c = len(c_in)

    def body(*refs):
        q_ref, k_ref, v_ref, run_ref, runt_ref = refs[:5]
        o_ref, lse_ref = refs[5 + n_c:7 + n_c]
        s0 = 7 + n_c + len(c_out_shape)
        kb_s, vt_s, raw_s, m_s, l_s, acc_s = refs[s0:s0 + 6]
        hp = pl.program_id(0)
        i = pl.program_id(1)

        if carry:
            _carry_step(carry[0], refs[5:5 + n_c], refs[7 + n_c], refs[8 + n_c], refs[-1], hp * nq + i,
                        (HEADS // hp_n) * nq)

        @pl.when(i == 0)
        def _():
            for hh in range(hp_n):
                kb_s[hh] = k_ref[0, hh].astype(MXU)
                for b in range(nq):
                    rows = slice(b * qb, (b + 1) * qb)
                    vt_s[hh, :, rows] = _wide_t(v_ref[0, hh, rows, :]).astype(MXU)

        qrows = pl.ds(pl.multiple_of(i * qb, qb), qb)
        qs = [(q_ref[0, hh] * Q_SCALE).astype(MXU) for hh in range(hp_n)]
        fq = [runt_ref[pl.ds(hp * hp_n + hh, 1), qrows] for hh in range(hp_n)]

        def block_rows(j):
            return pl.ds(pl.multiple_of(j * qb, qb), qb)

        def scores(j):
            for hh in range(hp_n):
                raw_s[j % 2, hh] = _dot_nt(kb_s[hh, block_rows(j), :], qs[hh])

        def absorb(j, diagonal):
            rows = block_rows(j)
            run_blk = run_ref[rows, :]
            stats, pv = [], []
            for hh in range(hp_n):
                m, l = m_s[hh], l_s[hh]
                st = raw_s[j % 2, hh] + (fq[hh] - _key_side_f(run_blk, hp * hp_n + hh, qb))
                if diagonal:
                    st = jnp.where(_diag_mask(qb), st, -1e30)
                m_new = jnp.maximum(m, jnp.max(st, axis=0, keepdims=True))
                p = jnp.exp(st - m_new)
                alpha = jnp.exp(m - m_new)
                stats.append((m_new, alpha * l + jnp.sum(p, axis=0, keepdims=True), alpha))
                pv.append(_dot(vt_s[hh, :, rows], p))
            for hh in range(hp_n):
                m_s[hh], l_s[hh] = stats[hh][0], stats[hh][1]
                acc_s[hh] = stats[hh][2] * acc_s[hh] + pv[hh]

        def kstep(j, _):
            scores(j + 1)
            absorb(j, False)
            return 0

        for hh in range(hp_n):
            m_s[hh] = jnp.full((1, qb), -1e30, F32)
            l_s[hh] = jnp.zeros((1, qb), F32)
            acc_s[hh] = jnp.zeros((DH, qb), F32)
        scores(0)
        lax.fori_loop(0, i, kstep, 0)
        absorb(i, True)
        for hh in range(hp_n):
            l = l_s[hh]
            o_ref[hh] = _tall_t(acc_s[hh] / l)
            lse_ref[pl.ds(hp * hp_n + hh, 1), qrows] = m_s[hh] + jnp.log(l)

    out = pl.pallas_call(
        body, name="fox_fwd", grid=(HEADS // hp_n, nq),
        in_specs=[pl.BlockSpec((1, hp_n, qb, DH), lambda h, i: (0, h, i, 0)),
                  pl.BlockSpec((1, hp_n, t, DH), lambda h, i: (1, h, 0, 0)),
                  pl.BlockSpec((1, hp_n, t, DH), lambda h, i: (2, h, 0, 0)),
                  pl.BlockSpec((t, LANE), lambda h, i: (0, 0)),
                  pl.BlockSpec((HEADS, t), lambda h, i: (2, 0))] + c_in_specs,
        out_specs=[pl.BlockSpec((hp_n, qb, DH), lambda h, i: (h, i, 0)),
                   pl.BlockSpec((HEADS, t), lambda h, i: (0, 0))] + c_out_specs,
        out_shape=[jax.ShapeDtypeStruct((HEADS, t, DH), F32), jax.ShapeDtypeStruct((HEADS, t), F32)] + c_out_shape,
        input_output_aliases={5 + j: 4 + j for j in range(n_c)},
        scratch_shapes=[pltpu.VMEM((hp_n, t, DH), MXU), pltpu.VMEM((hp_n, DH, t), MXU), pltpu.VMEM((2, hp_n, qb, qb), F32),
                        pltpu.VMEM((hp_n, 1, qb), F32), pltpu.VMEM((hp_n, 1, qb), F32), pltpu.VMEM((hp_n, DH, qb), F32)]
        + ([c_sem] if carry else []),
        compiler_params=_params(("arbitrary", "arbitrary")),
    )(qkv, qkv, qkv, run, run_t, *c_in)
    return (out[0], out[1], _carry_state(out[2:])) if carry else (out[0], out[1])


def _fox_bwd(qkv, run, run_t, o, lse, do, carry=None):
    t = qkv.shape[2]
    qb = min(QB, t)
    nq = t // qb
    hp_n = FOX_HP_BWD
    c_in, c_in_specs, c_out_shape, c_out_specs, c_sem = _carry_operands(*carry) if carry else ([], [], [], [], None)
    n_c = len(c_in)

    def body(*refs):
        q_ref, k_ref, v_ref, run_ref, runt_ref, o_ref, lse_ref, do_ref = refs[:8]
        dqkv_ref, dcol_ref, drow_ref = refs[8 + n_c:11 + n_c]
        s0 = 11 + n_c + len(c_out_shape)
        dqt_s, raw_s, dpt_s, dro_s, dk_s, dv_s, dsum_s = refs[s0:s0 + 7]
        hp = pl.program_id(0)
        j = pl.program_id(1)

        if carry:
            _carry_step(carry[0], refs[8:8 + n_c], refs[11 + n_c], refs[12 + n_c], refs[-1], hp * nq + j,
                        (HEADS // hp_n) * nq)

        @pl.when(j == 0)
        def _():
            dqt_s[...] = jnp.zeros_like(dqt_s)

        @pl.when((j == 0) & (hp == 0))
        def _():
            dcol_ref[...] = jnp.zeros_like(dcol_ref)
            drow_ref[...] = jnp.zeros_like(drow_ref)

        krows = pl.ds(pl.multiple_of(j * qb, qb), qb)
        run_blk = run_ref[krows, :]
        ones8 = jnp.ones((8, DH), MXU)
        kb, kt, vb, fk = [], [], [], []
        for hh in range(hp_n):
            kf = k_ref[0, hh]
            kb.append(kf.astype(MXU))
            kt.append(_wide_t(kf).astype(MXU))
            vb.append(v_ref[0, hh].astype(MXU))
            fk.append(_key_side_f(run_blk, hp * hp_n + hh, qb))

        def block_rows(i):
            return pl.ds(pl.multiple_of(i * qb, qb), qb)

        def products(i):
            rows = block_rows(i)
            slot = i % 2
            for hh in range(hp_n):
                dout = do_ref[hh, rows, :]
                x = dout * o_ref[hh, rows, :]
                x_hi = x.astype(MXU)
                raw_s[slot, hh] = _dot_nt(kb[hh], q_ref[0, hh, rows, :] * Q_SCALE)
                dpt_s[slot, hh] = _dot_nt(vb[hh], dout)
                dro_s[slot, hh] = _dot_nt(ones8, x_hi) + _dot_nt(ones8, x - x_hi.astype(F32))

        def absorb(i, diagonal):
            rows = block_rows(i)
            slot = i % 2
            pieces = []
            for hh in range(hp_n):
                head = hp * hp_n + hh
                raw, dpt, drow = raw_s[slot, hh], dpt_s[slot, hh], dro_s[slot, hh, 0:1, :]
                off = runt_ref[pl.ds(head, 1), rows] - lse_ref[pl.ds(head, 1), rows]
                st = raw + (off - fk[hh])
                if diagonal:
                    st = jnp.where(_diag_mask(qb), st, -1e30)
                pt = jnp.exp(st)
                dst = pt * (dpt - drow)
                drow_ref[pl.ds(head, 1), rows] += jnp.sum(dst, axis=0, keepdims=True)
                folded = dst[:, 0:LANE]
                for c in range(1, qb // LANE):
                    folded = folded + dst[:, c * LANE:(c + 1) * LANE]
                pieces.append((_dot(dst, q_ref[0, hh, rows, :] * Q_SCALE), _dot(pt, do_ref[hh, rows, :]),
                               _dot(kt[hh], dst), folded))
            for hh in range(hp_n):
                dqt_s[hh, :, rows] += pieces[hh][2]
                if diagonal:
                    dk_s[hh], dv_s[hh], dsum_s[hh] = pieces[hh][0], pieces[hh][1], pieces[hh][3]
                else:
                    dk_s[hh] += pieces[hh][0]
                    dv_s[hh] += pieces[hh][1]
                    dsum_s[hh] += pieces[hh][3]

        def qstep(i, _):
            products(jnp.minimum(i + 1, nq - 1))
            absorb(i, False)
            return 0

        products(j)
        products(jnp.minimum(j + 1, nq - 1))
        absorb(j, True)
        lax.fori_loop(j + 1, nq, qstep, 0)
        for hh in range(hp_n):
            dqkv_ref[1, hh, krows, :] = dk_s[hh]
            dqkv_ref[2, hh, krows, :] = dv_s[hh]
            dcol_ref[krows, :] += _to_lane(-jnp.sum(dsum_s[hh], axis=1, keepdims=True), 2 * HEADS + hp * hp_n + hh)

        @pl.when(j == nq - 1)
        def _():
            for hh in range(hp_n):
                for b in range(nq):
                    rows = slice(b * qb, (b + 1) * qb)
                    dqkv_ref[0, hh, rows, :] = _tall_t(dqt_s[hh, :, rows]) * Q_SCALE

    once = dict(pipeline_mode=pl.Buffered(1))
    full = pl.BlockSpec((hp_n, t, DH), lambda h, j: (h, 0, 0), **once)
    rows8 = pl.BlockSpec((HEADS, t), lambda h, j: (0, 0))
    out = pl.pallas_call(
        body, name="fox_bwd", grid=(HEADS // hp_n, nq),
        in_specs=[pl.BlockSpec((1, hp_n, t, DH), lambda h, j: (0, h, 0, 0), **once),
                  pl.BlockSpec((1, hp_n, qb, DH), lambda h, j: (1, h, j, 0)),
                  pl.BlockSpec((1, hp_n, qb, DH), lambda h, j: (2, h, j, 0)),
                  pl.BlockSpec((t, LANE), lambda h, j: (0, 0), **once), pl.BlockSpec((HEADS, t), lambda h, j: (2, 0)),
                  full, rows8, full] + c_in_specs,
        out_specs=[pl.BlockSpec((3, hp_n, t, DH), lambda h, j: (0, h, 0, 0), **once),
                   pl.BlockSpec((t, LANE), lambda h, j: (0, 0)), rows8] + c_out_specs,
        out_shape=[jax.ShapeDtypeStruct((3, HEADS, t, DH), F32), jax.ShapeDtypeStruct((t, LANE), F32),
                   jax.ShapeDtypeStruct((HEADS, t), F32)] + c_out_shape,
        input_output_aliases={8 + j: 5 + j for j in range(n_c)},
        scratch_shapes=[pltpu.VMEM((hp_n, DH, t), F32), pltpu.VMEM((2, hp_n, qb, qb), F32), pltpu.VMEM((2, hp_n, qb, qb), F32),
                        pltpu.VMEM((2, hp_n, 8, qb), F32), pltpu.VMEM((hp_n, qb, DH), F32), pltpu.VMEM((hp_n, qb, DH), F32),
                        pltpu.VMEM((hp_n, qb, LANE), F32)] + ([c_sem] if carry else []),
        compiler_params=_params(("arbitrary", "arbitrary")),
    )(qkv, qkv, qkv, run, run_t, o, lse, do, *c_in)
    return (out[0], out[1], out[2], _carry_state(out[3:])) if carry else tuple(out)


Z_COL0 = 3 * WIDTH // LANE
FGATE_COL0 = 7 * WIDTH // LANE


def _gdn_post(o, pm, onw):
    t = pm.shape[0]

    def body(o_ref, z_ref, w_ref, m_ref):
        z = z_ref[...]
        sz = z * _sigmoid(z)
        halves = []
        for hh in range(2):
            ov = o_ref[hh]
            n = ov * lax.rsqrt(jnp.mean(ov * ov, axis=-1, keepdims=True) + EPS) * w_ref[...]
            halves.append(n * sz[:, hh * DH:(hh + 1) * DH])
        m_ref[...] = jnp.concatenate(halves, axis=1).astype(m_ref.dtype)

    return pl.pallas_call(
        body, name="gdn_post", grid=(WIDTH // LANE,),
        in_specs=[pl.BlockSpec((2, t, DH), lambda j: (j, 0, 0)), pl.BlockSpec((t, LANE), lambda j: (0, Z_COL0 + j)),
                  pl.BlockSpec((1, DH), lambda j: (0, 0))],
        out_specs=pl.BlockSpec((t, LANE), lambda j: (0, j)),
        out_shape=jax.ShapeDtypeStruct((t, WIDTH), MXU),
        compiler_params=_params(("arbitrary",)),
    )(o, pm, onw)


def _gdn_post_bwd(o, pm, onw, dmix):
    t = pm.shape[0]

    def body(o_ref, z_ref, w_ref, dm_ref, do_ref, dz_ref, dw_ref):
        @pl.when(pl.program_id(0) == 0)
        def _():
            dw_ref[...] = jnp.zeros_like(dw_ref)

        z = z_ref[...]
        sg = _sigmoid(z)
        sz = z * sg
        dsz = sg * (1.0 + z * (1.0 - sg))
        dm = dm_ref[...]
        for hh in range(2):
            cols = slice(hh * DH, (hh + 1) * DH)
            ov = o_ref[hh]
            r = lax.rsqrt(jnp.mean(ov * ov, axis=-1, keepdims=True) + EPS)
            xn = ov * r
            dmh = dm[:, cols]
            dn = dmh * sz[:, cols]
            dz_ref[:, cols] = dmh * (xn * w_ref[...]) * dsz[:, cols]
            dw_ref[...] += jnp.sum(dn * xn, axis=0, keepdims=True)
            g = dn * w_ref[...]
            do_ref[hh] = r * (g - xn * jnp.mean(g * xn, axis=-1, keepdims=True))

    return pl.pallas_call(
        body, name="gdn_post_bwd", grid=(WIDTH // LANE,),
        in_specs=[pl.BlockSpec((2, t, DH), lambda j: (j, 0, 0)), pl.BlockSpec((t, LANE), lambda j: (0, Z_COL0 + j)),
                  pl.BlockSpec((1, DH), lambda j: (0, 0)), pl.BlockSpec((t, LANE), lambda j: (0, j))],
        out_specs=[pl.BlockSpec((2, t, DH), lambda j: (j, 0, 0)), pl.BlockSpec((t, LANE), lambda j: (0, j)),
                   pl.BlockSpec((1, DH), lambda j: (0, 0))],
        out_shape=[jax.ShapeDtypeStruct((HEADS, t, DH), F32), jax.ShapeDtypeStruct((t, WIDTH), F32),
                   jax.ShapeDtypeStruct((1, DH), F32)],
        compiler_params=_params(("arbitrary",)),
    )(o, pm, onw, dmix)


def _fox_post(o, pm):
    t = pm.shape[0]

    def body(o_ref, g_ref, m_ref):
        m_ref[...] = (jnp.concatenate([o_ref[0], o_ref[1]], axis=1) * _sigmoid(g_ref[...])).astype(m_ref.dtype)

    return pl.pallas_call(
        body, name="fox_post", grid=(WIDTH // LANE,),
        in_specs=[pl.BlockSpec((2, t, DH), lambda j: (j, 0, 0)), pl.BlockSpec((t, LANE), lambda j: (0, FGATE_COL0 + j))],
        out_specs=pl.BlockSpec((t, LANE), lambda j: (0, j)),
        out_shape=jax.ShapeDtypeStruct((t, WIDTH), MXU),
        compiler_params=_params(("arbitrary",)),
    )(o, pm)


def _fox_post_bwd(o, pm, dmix):
    t = pm.shape[0]

    def body(o_ref, g_ref, dm_ref, do_ref, dg_ref):
        sg = _sigmoid(g_ref[...])
        dm = dm_ref[...]
        for hh in range(2):
            cols = slice(hh * DH, (hh + 1) * DH)
            do_ref[hh] = dm[:, cols] * sg[:, cols]
            dg_ref[:, cols] = dm[:, cols] * o_ref[hh] * (sg * (1.0 - sg))[:, cols]

    return pl.pallas_call(
        body, name="fox_post_bwd", grid=(WIDTH // LANE,),
        in_specs=[pl.BlockSpec((2, t, DH), lambda j: (j, 0, 0)), pl.BlockSpec((t, LANE), lambda j: (0, FGATE_COL0 + j)),
                  pl.BlockSpec((t, LANE), lambda j: (0, j))],
        out_specs=[pl.BlockSpec((2, t, DH), lambda j: (j, 0, 0)), pl.BlockSpec((t, LANE), lambda j: (0, j))],
        out_shape=[jax.ShapeDtypeStruct((HEADS, t, DH), F32), jax.ShapeDtypeStruct((t, WIDTH), F32)],
        compiler_params=_params(("arbitrary",)),
    )(o, pm, dmix)


def _tail(x, mixg, mixf, tgt, wo, n2w, wg_t, wu_t, wd, fw):
    t, d = x.shape
    dff = wd.shape[0]
    tb = min(TB, t)

    def body(x_ref, mg_ref, mf_ref, t_ref, wo_ref, n2_ref, wg_ref, wu_ref, wd_ref, fw_ref,
             h2_ref, act_ref, dgate_ref, dup_ref, dx3_ref, dx2_ref, dmg_ref, dmf_ref, dn2_ref, dfw_ref, loss_ref):
        @pl.when(pl.program_id(0) == 0)
        def _():
            dn2_ref[...] = jnp.zeros_like(dn2_ref)
            dfw_ref[...] = jnp.zeros_like(dfw_ref)
            loss_ref[...] = jnp.zeros_like(loss_ref)

        x2 = x_ref[...] + _dot(mg_ref[...], wo_ref[0:WIDTH, :]) + _dot(mf_ref[...], wo_ref[WIDTH:2 * WIDTH, :])
        r2 = lax.rsqrt(jnp.mean(x2 * x2, axis=-1, keepdims=True) + EPS)
        xn2 = x2 * r2
        h2 = (xn2 * n2_ref[...]).astype(MXU)
        h2_ref[...] = h2
        gate = _dot_nt(h2, wg_ref[...])
        up = _dot_nt(h2, wu_ref[...])
        sg = _sigmoid(gate)
        sl = gate * sg
        act = (sl * up).astype(MXU)
        act_ref[...] = act
        x3 = x2 + _dot(act, wd_ref[...])
        r3 = lax.rsqrt(jnp.mean(x3 * x3, axis=-1, keepdims=True) + EPS)
        xn3 = x3 * r3
        err = xn3 * fw_ref[...] - t_ref[...]
        loss_ref[...] += 0.5 * jnp.sum(jnp.mean(err * err, axis=-1, keepdims=True), axis=0, keepdims=True)
        dy = err * (1.0 / d)
        dfw_ref[...] += jnp.sum(dy * xn3, axis=0, keepdims=True)
        g3 = dy * fw_ref[...]
        dx3 = r3 * (g3 - xn3 * jnp.mean(g3 * xn3, axis=-1, keepdims=True))
        dx3_ref[...] = dx3.astype(MXU)
        dact = _dot_nt(dx3, wd_ref[...])
        dgate = (dact * up * (sg * (1.0 + gate * (1.0 - sg)))).astype(MXU)
        dup = (dact * sl).astype(MXU)
        dgate_ref[...] = dgate
        dup_ref[...] = dup
        dh2 = _dot(dgate, wg_ref[...]) + _dot(dup, wu_ref[...])
        dn2_ref[...] += jnp.sum(dh2 * xn2, axis=0, keepdims=True)
        g2 = dh2 * n2_ref[...]
        dx2 = dx3 + r2 * (g2 - xn2 * jnp.mean(g2 * xn2, axis=-1, keepdims=True))
        dx2_ref[...] = dx2
        dmg_ref[...] = _dot_nt(dx2, wo_ref[0:WIDTH, :])
        dmf_ref[...] = _dot_nt(dx2, wo_ref[WIDTH:2 * WIDTH, :])

    def tok(n):
        return pl.BlockSpec((tb, n), lambda i: (i, 0))

    acc = pl.BlockSpec((1, d), lambda i: (0, 0))
    sds = jax.ShapeDtypeStruct
    return pl.pallas_call(
        body, name="tail", grid=(t // tb,),
        in_specs=[tok(d), tok(WIDTH), tok(WIDTH), tok(d), _resident(wo.shape), _resident((1, d)),
                  _resident(wg_t.shape), _resident(wu_t.shape), _resident(wd.shape), _resident((1, d))],
        out_specs=[tok(d), tok(dff), tok(dff), tok(dff), tok(d), tok(d), tok(WIDTH), tok(WIDTH), acc, acc,
                   pl.BlockSpec((1, 1), lambda i: (0, 0))],
        out_shape=[sds((t, d), MXU), sds((t, dff), MXU), sds((t, dff), MXU), sds((t, dff), MXU), sds((t, d), MXU),
                   sds((t, d), F32), sds((t, WIDTH), F32), sds((t, WIDTH), F32), sds((1, d), F32), sds((1, d), F32),
                   sds((1, 1), F32)],
        compiler_params=_params(("arbitrary",)),
    )(x, mixg, mixf, tgt, wo, n2w, wg_t, wu_t, wd, fw)


def _wgrads(a_list, b, name):
    t, n = b.shape
    ms = [a.shape[1] for a in a_list]
    bms = [256 if m % 256 == 0 else LANE for m in ms]
    nbs = [m // bm for m, bm in zip(ms, bms)]
    k = len(a_list)
    cast = b.dtype != jnp.dtype(MXU)

    def body(*refs):
        a_refs, b_ref, o_refs = refs[:k], refs[k], refs[k + 1:2 * k + 1]
        i = pl.program_id(0)
        if cast:
            @pl.when(i == 0)
            def _():
                refs[-1][...] = b_ref[...].astype(MXU)
        for a_ref, o_ref, nb in zip(a_refs, o_refs, nbs):
            @pl.when(i < nb)
            def _():
                o_ref[...] = _dot_tn(a_ref[...], refs[-1][...] if cast else b_ref[...]).astype(o_ref.dtype)

    def clamp(nb):
        return lambda i: jnp.minimum(i, nb - 1)

    return pl.pallas_call(
        body, name=name, grid=(max(nbs),),
        in_specs=[pl.BlockSpec((t, bm), lambda i, c=clamp(nb): (0, c(i))) for bm, nb in zip(bms, nbs)] + [_resident((t, n))],
        out_specs=[pl.BlockSpec((bm, n), lambda i, c=clamp(nb): (c(i), 0)) for bm, nb in zip(bms, nbs)],
        out_shape=[jax.ShapeDtypeStruct((m, n), WIRE) for m in ms],
        scratch_shapes=[pltpu.VMEM((t, n), MXU)] if cast else [],
        compiler_params=_params(("arbitrary",)),
    )(*a_list, b)


def _merge_dw_in(d_gdn, d_z, d_fox, d_fg, d_small):
    return jnp.concatenate([d_gdn, d_z, d_small[:2 * HEADS], d_fox, d_fg, d_small[2 * HEADS:3 * HEADS]], axis=0)


def _lanes(*pieces):
    v = jnp.concatenate([p.reshape(-1).astype(F32) for p in pieces])
    return jnp.pad(v, (0, LANE - v.shape[0])).reshape(1, LANE)


def _vector_params(p):
    d = p["norm1_w"].size
    gparams = jnp.concatenate([_lanes(jnp.zeros(HEADS), p["gdn_dt_bias"], p["fox_f_bias"]),
                               _lanes(jnp.zeros(HEADS), p["gdn_A_log"]), jnp.zeros((6, LANE), F32)])
    fox_nw = jnp.stack([jnp.tile(p["fox_q_norm_w"].reshape(-1), 2), jnp.tile(p["fox_k_norm_w"].reshape(-1), 2),
                        jnp.ones((LANE,), F32)])
    return dict(n1w=p["norm1_w"].reshape(1, d), n2w=p["norm2_w"].reshape(1, d), fw=p["final_norm_w"].reshape(1, d),
                onw=p["gdn_out_norm_w"].reshape(1, DH), gparams=gparams, fox_nw=fox_nw)


def _mixer_forward(x, vp, w_t, ws_t, conv_w, carry=None):
    h1, pm, ps = _inproj(x, vp["n1w"], w_t, ws_t)
    gates, run, tot, run_t = _gates(ps, vp["gparams"])
    fqkv = _fox_prep(pm, vp["fox_nw"])
    o_fox, lse, *carried = _fox_fwd(fqkv, run, run_t, carry)
    if carry:
        pm, o_fox = lax.optimization_barrier((pm, o_fox))
    mixf = _fox_post(o_fox, pm)
    gqkv = _gdn_prep(pm, conv_w)
    o_gdn, states, inv = _gdn_fwd(gqkv, gates, run, tot, run_t)
    mixg = _gdn_post(o_gdn, pm, vp["onw"])
    return dict(h1=h1, pm=pm, ps=ps, gates=gates, run=run, tot=tot, run_t=run_t, gqkv=gqkv, o_gdn=o_gdn, states=states,
                inv=inv, mixg=mixg, fqkv=fqkv, o_fox=o_fox, lse=lse, mixf=mixf, carried=carried[0] if carried else None)


def _mixer_backward(x, vp, w_t, ws_t, conv_w, f, dx2, dmixg, dmixf, carry=None, scatter_own=False):
    pm = f["pm"]
    do_fox, dfg = _fox_post_bwd(f["o_fox"], pm, dmixf)
    dfqkv, dcol_f, drow_f, *carried = _fox_bwd(f["fqkv"], f["run"], f["run_t"], f["o_fox"], f["lse"], do_fox, carry)
    dfox, dfnw = _fox_prep_bwd(pm, vp["fox_nw"], dfqkv)
    do_gdn, dz, donw = _gdn_post_bwd(f["o_gdn"], pm, vp["onw"], dmixg)
    dgqkv, dcol_g, dtot_g, drow_g = _gdn_bwd(f["gqkv"], f["gates"], f["run"], f["tot"], f["run_t"], f["inv"], f["states"], do_gdn)
    dgdn, dconv = _gdn_prep_bwd(pm, conv_w, dgqkv)
    dps, gsum = _gates_bwd(f["ps"], vp["gparams"], dcol_g, dtot_g, dcol_f, drow_g, drow_f)
    dw_in = _merge_dw_in(*_wgrads([dgdn, dz, dfox, dfg, dps], f["h1"], "dw_in"))
    own = ("scatter", [_cut(dw_in, 0), _cut(dconv, 1)]) if scatter_own else None
    grad_x, dn1w, *sent = _inproj_bwd(x, vp["n1w"], dx2, dgdn, dz, dfox, dfg, dps, w_t, ws_t, own)
    small = dict(dn1w=dn1w, gsum=gsum, donw=donw, dfnw=dfnw)
    return (grad_x, dw_in, dconv, small, *carried, *sent)


VECTORS = ("norm1_w", "norm2_w", "final_norm_w", "gdn_A_log", "gdn_dt_bias", "gdn_out_norm_w", "fox_f_bias",
           "fox_q_norm_w", "fox_k_norm_w")
VEC_ROWS = 16
LOSS_ROW = len(VECTORS)


def _pack_vectors(dn1w, dn2w, dfw, gsum, donw, dfnw, loss):
    d = dn1w.shape[1]

    def body(n1_ref, n2_ref, fw_ref, gs_ref, on_ref, fn_ref, loss_ref, o_ref):
        o_ref[...] = jnp.zeros_like(o_ref)
        o_ref[0:1, :] = n1_ref[...]
        o_ref[1:2, :] = n2_ref[...]
        o_ref[2:3, :] = fw_ref[...]
        o_ref[3:4, 0:HEADS] = gs_ref[0:1, 0:HEADS]
        o_ref[4:5, 0:HEADS] = gs_ref[1:2, 0:HEADS]
        o_ref[5:6, 0:DH] = on_ref[...]
        o_ref[6:7, 0:HEADS] = gs_ref[2:3, 0:HEADS]
        for kind in range(2):
            v = fn_ref[kind, 0]
            for j in range(1, fn_ref.shape[1]):
                v = v + fn_ref[kind, j]
            o_ref[7 + kind:8 + kind, 0:DH] = v[:, :DH] + v[:, DH:]
        o_ref[LOSS_ROW:LOSS_ROW + 1, 0:1] = loss_ref[...]

    return pl.pallas_call(body, name="pack_vectors", out_shape=jax.ShapeDtypeStruct((VEC_ROWS, d), F32),
                          compiler_params=_params())(dn1w, dn2w, dfw, gsum, donw, dfnw, loss)


def _late_grads(f, h2, act, dgate, dup, dx3, dx2):
    dw_gate, dw_up = _wgrads([dgate, dup], h2, "dw_gate_up")
    (dw_down,) = _wgrads([act], dx3, "dw_down")
    return {"w_out": jnp.concatenate(_wgrads([f["mixg"], f["mixf"]], dx2, "dw_out"), axis=0),
            "w_ffn_gate": dw_gate, "w_ffn_up": dw_up, "w_ffn_down": dw_down}


def _local_step(x, tgt, p, w_in_t, conv_w, wo, wg_t, wu_t, wd):
    vp = _vector_params(p)
    ws_t = _small_rows(w_in_t)
    f = _mixer_forward(x, vp, w_in_t, ws_t, conv_w)
    (h2, act, dgate, dup, dx3, dx2, dmixg, dmixf, dn2w, dfw, loss) = _tail(
        x, f["mixg"], f["mixf"], tgt, wo, vp["n2w"], wg_t, wu_t, wd, vp["fw"])
    grad_x, dw_in, dconv, small = _mixer_backward(x, vp, w_in_t, ws_t, conv_w, f, dx2, dmixg, dmixf)
    grads = {"w_in": dw_in, "gdn_conv_w": dconv, **_late_grads(f, h2, act, dgate, dup, dx3, dx2)}
    vec = _pack_vectors(small["dn1w"], dn2w, dfw, small["gsum"], small["donw"], small["dfnw"], loss)
    return grad_x, grads, vec


def _my_place():
    return lax.axis_index("x"), lax.axis_index("y"), lax.axis_index("c")


def _peers():
    x, y, c = _my_place()
    peers = []
    for k in range(1, N_DEV):
        px = 1 - x if k & 4 else x
        py = 1 - y if k & 2 else y
        pc = 1 - c if k & 1 else c
        peers.append(((px, py, pc), 4 * px + 2 * py + pc))
    return 4 * x + 2 * y + c, peers


def _spread_copies(kind, srcs, lands, send_sems, recv_sems):
    me, peers = _peers()
    kinds = [kind] * len(srcs) if isinstance(kind, str) else kind
    remote, local = [], []
    for i, (kd, src, land) in enumerate(zip(kinds, srcs, lands)):
        for k, (dev, idx) in enumerate(peers):
            remote.append(pltpu.make_async_remote_copy(
                src_ref=src if kd == "gather" else src.at[idx], dst_ref=land.at[me],
                send_sem=send_sems.at[i * (N_DEV - 1) + k], recv_sem=recv_sems.at[i * (N_DEV - 1) + k],
                device_id=dev, device_id_type=MESH))
        local.append((src if kd == "gather" else src.at[me], land.at[me]))
    return remote, local


def _gather_two_level(arrays, name):
    n = len(arrays)

    def body(*refs):
        srcs, lands = refs[:n], refs[n:2 * n]
        send_sems, recv_sems, local_sems = refs[2 * n:]
        x, y, c = _my_place()
        me, sibling = (x, y, c), (x, y, 1 - c)
        chips = [(1 - x, y), (x, 1 - y), (1 - x, 1 - y)]

        def index(p):
            return 4 * p[0] + 2 * p[1] + p[2]

        def copy(i, k, block, to, src=None):
            blk = lands[i].at[index(block)]
            return pltpu.make_async_remote_copy(
                src_ref=blk if src is None else src, dst_ref=blk, send_sem=send_sems.at[7 * i + k],
                recv_sem=recv_sems.at[7 * i + k], device_id=to, device_id_type=MESH)

        mine = [pltpu.make_async_copy(srcs[i], lands[i].at[index(me)], local_sems.at[i]) for i in range(n)]
        for cp in mine:
            cp.start()
        first = []
        for i in range(n):
            first.append(copy(i, 0, me, sibling, src=srcs[i]))
            first += [copy(i, 1 + j, me, (*chip, c), src=srcs[i]) for j, chip in enumerate(chips)]
        for cp in first:
            cp.start()
        passed = []
        for i in range(n):
            for j, chip in enumerate(chips):
                copy(i, 1 + j, (*chip, c), me).wait_recv()
                passed.append(copy(i, 4 + j, (*chip, c), sibling))
                passed[-1].start()
        for i in range(n):
            copy(i, 0, sibling, me).wait_recv()
            for j, chip in enumerate(chips):
                copy(i, 4 + j, (*chip, 1 - c), me).wait_recv()
        for cp in first + passed:
            cp.wait_send()
        for cp in mine:
            cp.wait()

    return pl.pallas_call(
        body, name=name,
        out_shape=[jax.ShapeDtypeStruct((N_DEV,) + a.shape, a.dtype) for a in arrays],
        in_specs=[pl.BlockSpec(memory_space=pl.ANY)] * n, out_specs=[pl.BlockSpec(memory_space=pl.ANY)] * n,
        scratch_shapes=[pltpu.SemaphoreType.DMA((7 * n,)), pltpu.SemaphoreType.DMA((7 * n,)),
                        pltpu.SemaphoreType.DMA((n,))],
    )(*arrays)


def _land_shape(kind, a):
    return (N_DEV,) + a.shape if kind == "gather" else a.shape


HBM = pl.BlockSpec(memory_space=pltpu.HBM)
SEM = pl.BlockSpec(memory_space=pltpu.SEMAPHORE)


def _hbm(a):
    return pltpu.with_memory_space_constraint(a, pltpu.HBM)


def _carry_operands(kind, arrays):
    n = len(arrays)
    kinds = [kind] * n if isinstance(kind, str) else kind
    lands = [lax.empty(_land_shape(kd, a), a.dtype) for kd, a in zip(kinds, arrays)]
    sems = [pltpu.SemaphoreType.DMA((n * (N_DEV - 1),))] * 2
    return ([_hbm(a) for a in list(arrays) + lands], [HBM] * (2 * n),
            sems + [pltpu.HBM(a.shape, a.dtype) for a in list(arrays) + lands], [SEM] * 2 + [HBM] * (2 * n),
            pltpu.SemaphoreType.DMA((n,)))


def _carry_step(kind, in_refs, send_sems, recv_sems, local_sems, step, n_steps):
    n = len(in_refs) // 2
    remote, local = _spread_copies(kind, in_refs[:n], in_refs[n:], send_sems, recv_sems)
    copies = [pltpu.make_async_copy(s, d, local_sems.at[i]) for i, (s, d) in enumerate(local)]
    per_step = -(-len(remote) // n_steps)
    for s in range(-(-len(remote) // per_step)):
        @pl.when(step == s)
        def _():
            for cp in remote[s * per_step:(s + 1) * per_step]:
                cp.start()
            if s == 0:
                for cp in copies:
                    cp.start()

    @pl.when(step == n_steps - 1)
    def _():
        for cp in copies:
            cp.wait()


def _carry_state(extra_out):
    n = (len(extra_out) - 2) // 2
    return list(extra_out[2:2 + n]), list(extra_out[2 + n:]), extra_out[0], extra_out[1]


def _spread_start(arrays, kind, name):
    n = len(arrays)

    def body(*refs):
        srcs, lands = refs[:n], refs[n:2 * n]
        send_sems, recv_sems = refs[2 * n], refs[2 * n + 1]
        token = refs[4 * n + 2]
        local_sems = refs[4 * n + 3]
        remote, local = _spread_copies(kind, srcs, lands, send_sems, recv_sems)
        for cp in remote:
            cp.start()
        copies = [pltpu.make_async_copy(s, d, local_sems.at[i]) for i, (s, d) in enumerate(local)]
        for cp in copies:
            cp.start()
        for cp in copies:
            cp.wait()
        token[...] = jnp.zeros_like(token)

    sems = (pltpu.SemaphoreType.DMA((n * (N_DEV - 1),)),) * 2
    kinds = [kind] * n if isinstance(kind, str) else kind
    lands = [lax.empty(_land_shape(kd, a), a.dtype) for kd, a in zip(kinds, arrays)]
    out = pl.pallas_call(
        body, name=name,
        out_shape=sems + tuple(pltpu.HBM(a.shape, a.dtype) for a in list(arrays) + lands)
        + (jax.ShapeDtypeStruct((8, LANE), F32),),
        in_specs=[HBM] * (2 * n), out_specs=tuple([SEM] * 2 + [HBM] * (2 * n) + [pl.BlockSpec(memory_space=pltpu.VMEM)]),
        input_output_aliases={j: 2 + j for j in range(2 * n)},
        scratch_shapes=[pltpu.SemaphoreType.DMA((n,))],
        compiler_params=pltpu.CompilerParams(has_side_effects=pltpu.SideEffectType.DATAFLOW_SIDE_EFFECTING),
    )(*[_hbm(a) for a in arrays], *[_hbm(a) for a in lands])
    return (list(out[2:2 + n]), list(out[2 + n:2 + 2 * n]), out[0], out[1]), out[-1]


def _spread_wait(state, kind, after, name):
    srcs, lands, send_sems, recv_sems = state
    n = len(srcs)
    after = list(after) if isinstance(after, (list, tuple)) else [after]

    def body(*refs):
        remote, _ = _spread_copies(kind, refs[:n], refs[n:2 * n], refs[2 * n], refs[2 * n + 1])
        for cp in remote:
            cp.wait_send()
        for cp in remote:
            cp.wait_recv()

    out = pl.pallas_call(
        body, name=name,
        out_shape=tuple(pltpu.HBM(a.shape, a.dtype) for a in srcs + lands),
        in_specs=[HBM] * (2 * n) + [SEM, SEM] + [pl.BlockSpec(memory_space=pl.ANY)] * len(after),
        out_specs=tuple([HBM] * (2 * n)),
        input_output_aliases={j: j for j in range(2 * n)},
        compiler_params=pltpu.CompilerParams(has_side_effects=pltpu.SideEffectType.DATAFLOW_SIDE_EFFECTING),
    )(*srcs, *lands, send_sems, recv_sems, *after)
    return list(out[n:])


ADAM_ROWS = 128
ADAM_COLS = 256


def _adam_math(g, w, m, v):
    nm = ADAM_B1 * m + (1.0 - ADAM_B1) * g
    nv = ADAM_B2 * v + (1.0 - ADAM_B2) * (g * g)
    m_hat = nm / (1.0 - ADAM_B1 ** ADAM_STEP)
    v_hat = nv / (1.0 - ADAM_B2 ** ADAM_STEP)
    return -ADAM_LR * (m_hat / (jnp.sqrt(v_hat) + ADAM_EPS) + ADAM_WD * w), nm, nv


def _sum_parts(p_ref):
    g = p_ref[0].astype(F32)
    for s in range(1, N_DEV):
        g = g + p_ref[s].astype(F32)
    return g


def _adam_matrix(parts, w, m, v, name):
    _, r, c = w.shape
    rb = ADAM_ROWS if r % ADAM_ROWS == 0 else r
    cb = ADAM_COLS if (rb == r and c % ADAM_COLS == 0) else c

    def body(p_ref, w_ref, m_ref, v_ref, g_ref, d_ref, nm_ref, nv_ref):
        g = _sum_parts(p_ref)
        g_ref[0] = g
        d_ref[0], nm_ref[0], nv_ref[0] = _adam_math(g, w_ref[0], m_ref[0], v_ref[0])

    blk = pl.BlockSpec((1, rb, cb), lambda i, j: (0, i, j))
    return pl.pallas_call(
        body, name=name, grid=(r // rb, c // cb),
        in_specs=[pl.BlockSpec((N_DEV, rb, cb), lambda i, j: (0, i, j)), blk, blk, blk],
        out_specs=[blk] * 4, out_shape=[jax.ShapeDtypeStruct(w.shape, F32)] * 4,
        compiler_params=_params(("arbitrary", "arbitrary")),
    )(parts, w, m, v)


def _adam_vectors(parts, ws, ms, vs):
    nv = len(ws)

    def body(*refs):
        p_ref = refs[0]
        w_refs, m_refs, v_refs = refs[1:1 + nv], refs[1 + nv:1 + 2 * nv], refs[1 + 2 * nv:1 + 3 * nv]
        outs = refs[1 + 3 * nv:]
        g_all = _sum_parts(p_ref)
        for i in range(nv):
            n = w_refs[i].shape[1]
            g = g_all[i:i + 1, 0:n]
            d, nm, nvv = _adam_math(g, w_refs[i][...], m_refs[i][...], v_refs[i][...])
            outs[i][...] = g
            outs[nv + i][...] = d
            outs[2 * nv + i][...] = nm
            outs[3 * nv + i][...] = nvv
        outs[4 * nv][...] = g_all[LOSS_ROW:LOSS_ROW + 1, 0:1]

    shapes = [jax.ShapeDtypeStruct(a.shape, F32) for a in ws]
    out = pl.pallas_call(body, name="adam_vectors", out_shape=shapes * 4 + [jax.ShapeDtypeStruct((1, 1), F32)],
                         compiler_params=_params())(parts, *ws, *ms, *vs)
    return out[:nv], out[nv:2 * nv], out[2 * nv:3 * nv], out[3 * nv:4 * nv], out[4 * nv]


MATRICES = (("w_in", 1), ("gdn_conv_w", 1), ("w_out", 0), ("w_ffn_gate", 1), ("w_ffn_up", 1), ("w_ffn_down", 0))
TRANSPOSED = ("w_in", "w_ffn_gate", "w_ffn_up")
WEIGHTS = ("norm1_w", "w_in", "gdn_conv_w", "gdn_A_log", "gdn_dt_bias", "gdn_out_norm_w", "fox_f_bias", "fox_q_norm_w",
           "fox_k_norm_w", "w_out", "norm2_w", "w_ffn_gate", "w_ffn_up", "w_ffn_down", "final_norm_w")


def _join(blocks, axis):
    _, r, c = blocks.shape
    if axis == 0:
        return blocks.reshape(N_DEV * r, c)
    return blocks.transpose(1, 0, 2).reshape(r, N_DEV * c)


def _cut(full, axis):
    r, c = full.shape
    if axis == 0:
        return full.reshape(N_DEV, r // N_DEV, c)
    return full.reshape(r, N_DEV, c // N_DEV).transpose(1, 0, 2)


def kernel(x, norm1_w, w_in, gdn_conv_w, gdn_A_log, gdn_dt_bias, gdn_out_norm_w, fox_f_bias, fox_q_norm_w, fox_k_norm_w, w_out, norm2_w, w_ffn_gate, w_ffn_up, w_ffn_down, final_norm_w, loss_target, m_norm1_w, m_w_in, m_gdn_conv_w, m_gdn_A_log, m_gdn_dt_bias, m_gdn_out_norm_w, m_fox_f_bias, m_fox_q_norm_w, m_fox_k_norm_w, m_w_out, m_norm2_w, m_w_ffn_gate, m_w_ffn_up, m_w_ffn_down, m_final_norm_w, v_norm1_w, v_w_in, v_gdn_conv_w, v_gdn_A_log, v_gdn_dt_bias, v_gdn_out_norm_w, v_fox_f_bias, v_fox_q_norm_w, v_fox_k_norm_w, v_w_out, v_norm2_w, v_w_ffn_gate, v_w_ffn_up, v_w_ffn_down, v_final_norm_w):
    w = dict(norm1_w=norm1_w, w_in=w_in, gdn_conv_w=gdn_conv_w, gdn_A_log=gdn_A_log, gdn_dt_bias=gdn_dt_bias,
             gdn_out_norm_w=gdn_out_norm_w, fox_f_bias=fox_f_bias, fox_q_norm_w=fox_q_norm_w, fox_k_norm_w=fox_k_norm_w,
             w_out=w_out, norm2_w=norm2_w, w_ffn_gate=w_ffn_gate, w_ffn_up=w_ffn_up, w_ffn_down=w_ffn_down,
             final_norm_w=final_norm_w)
    m = dict(norm1_w=m_norm1_w, w_in=m_w_in, gdn_conv_w=m_gdn_conv_w, gdn_A_log=m_gdn_A_log, gdn_dt_bias=m_gdn_dt_bias,
             gdn_out_norm_w=m_gdn_out_norm_w, fox_f_bias=m_fox_f_bias, fox_q_norm_w=m_fox_q_norm_w,
             fox_k_norm_w=m_fox_k_norm_w, w_out=m_w_out, norm2_w=m_norm2_w, w_ffn_gate=m_w_ffn_gate,
             w_ffn_up=m_w_ffn_up, w_ffn_down=m_w_ffn_down, final_norm_w=m_final_norm_w)
    v = dict(norm1_w=v_norm1_w, w_in=v_w_in, gdn_conv_w=v_gdn_conv_w, gdn_A_log=v_gdn_A_log, gdn_dt_bias=v_gdn_dt_bias,
             gdn_out_norm_w=v_gdn_out_norm_w, fox_f_bias=v_fox_f_bias, fox_q_norm_w=v_fox_q_norm_w,
             fox_k_norm_w=v_fox_k_norm_w, w_out=v_w_out, norm2_w=v_norm2_w, w_ffn_gate=v_w_ffn_gate,
             w_ffn_up=v_w_ffn_up, w_ffn_down=v_w_ffn_down, final_norm_w=v_final_norm_w)
    late = ("w_out", "w_ffn_gate", "w_ffn_up", "w_ffn_down")
    xs, tgt = x[0], loss_target[0]
    vp = _vector_params({n: w[n] for n in VECTORS})

    def rows_of(d, n):
        return d[n].transpose(0, 2, 1) if n in TRANSPOSED else d[n]

    wr, mr, vr = ({n: rows_of(d, n) for n, _ in MATRICES} for d in (w, m, v))

    w_in_blocks, conv_blocks = _gather_two_level([wr["w_in"][0].astype(WIRE), w["gdn_conv_w"][0]], "gather_in")
    w_t = _join(w_in_blocks, 0)
    conv_w = _join(conv_blocks, 1)
    f = _mixer_forward(xs, vp, w_t, _small_rows(w_t), conv_w, ("gather", [wr[n][0].astype(WIRE) for n in late]))
    full = {n: _join(b, 0) for n, b in zip(late, _spread_wait(f["carried"], "gather", f["mixg"], "gather_late_wait"))}

    (h2, act, dgate, dup, dx3, dx2, dmixg, dmixf, dn2w, dfw, loss) = _tail(
        xs, f["mixg"], f["mixf"], tgt, full["w_out"], vp["n2w"], full["w_ffn_gate"], full["w_ffn_up"],
        full["w_ffn_down"], vp["fw"])
    dlate = _late_grads(f, h2, act, dgate, dup, dx3, dx2)

    grad_x, dw_in, dconv, small, state_grads, state_own = _mixer_backward(
        xs, vp, w_t, _small_rows(w_t), conv_w, f, dx2, dmixg, dmixf, ("scatter", [_cut(dlate[n], 0) for n in late]),
        scatter_own=True)
    vec = _pack_vectors(small["dn1w"], dn2w, dfw, small["gsum"], small["donw"], small["dfnw"], loss)

    state_vec, token = _spread_start([vec], "gather", "vectors_start")
    parts = dict(zip(late, _spread_wait(state_grads, "scatter", token, "grads_late_wait")))
    results = [{}, {}, {}, {}]

    def update(n):
        for d, a in zip(results, _adam_matrix(parts[n], wr[n], mr[n], vr[n], "adam_" + n)):
            d[n] = a.transpose(0, 2, 1) if n in TRANSPOSED else a

    for n in late:
        update(n)
    parts["w_in"], parts["gdn_conv_w"] = _spread_wait(state_own, "scatter", [results[0][n] for n in late], "own_wait")
    (parts_vec,) = _spread_wait(state_vec, "gather", parts["w_in"], "vectors_wait")
    update("w_in")
    update("gdn_conv_w")
    row = lambda a: a.reshape(1, -1)
    *vec_out, total_loss = _adam_vectors(parts_vec, [row(w[n]) for n in VECTORS], [row(m[n]) for n in VECTORS],
                                         [row(v[n]) for n in VECTORS])
    for d, arrs in zip(results, vec_out):
        for n, a in zip(VECTORS, arrs):
            d[n] = a.reshape(w[n].shape)
    return (total_loss[0, 0], grad_x[None], *[d[n] for d in results for n in WEIGHTS])
```

```python
import functools

import jax
import jax.numpy as jnp
from jax import lax
from jax.experimental import pallas as pl
from jax.experimental.pallas import tpu as pltpu

F32 = jnp.float32
MXU = jnp.bfloat16
WIRE = jnp.bfloat16
HI = lax.Precision.HIGHEST
EPS = 1e-6

N_DEV = 8
HEADS = 8
DH = 64
WIDTH = HEADS * DH
CHUNK = 64
LANE = 128
ROW_ALIGN = 16
TB = 256
QB = 256
VMEM_LIMIT = 60 * 1024 * 1024

ADAM_LR = 0.001
ADAM_B1 = 0.9
ADAM_B2 = 0.999
ADAM_EPS = 1e-08
ADAM_WD = 0.01
ADAM_STEP = 10

MESH = pl.DeviceIdType.MESH


def _params(sem=None):
    return pltpu.CompilerParams(dimension_semantics=sem, vmem_limit_bytes=VMEM_LIMIT)


def _resident(shape):
    n = len(shape)
    return pl.BlockSpec(shape, lambda *_: (0,) * n, pipeline_mode=pl.Buffered(1))


def _dot(a, b):
    return jnp.dot(a.astype(MXU), b.astype(MXU), preferred_element_type=F32)


def _dot_nt(a, b):
    return lax.dot_general(a.astype(MXU), b.astype(MXU), (((1,), (1,)), ((), ())), preferred_element_type=F32)


def _dot_tn(a, b):
    return lax.dot_general(a.astype(MXU), b.astype(MXU), (((0,), (0,)), ((), ())), preferred_element_type=F32)


def _hdot(a, b):
    return jnp.dot(a, b, precision=HI, preferred_element_type=F32)


def _hdot_nt(a, b):
    return lax.dot_general(a, b, (((1,), (1,)), ((), ())), precision=HI, preferred_element_type=F32)


def _hdot_tn(a, b):
    return lax.dot_general(a, b, (((0,), (0,)), ((), ())), precision=HI, preferred_element_type=F32)


def _sigmoid(x):
    return 0.5 * jnp.tanh(0.5 * x) + 0.5


def _softplus(x):
    return jnp.maximum(x, 0.0) + jnp.log(1.0 + jnp.exp(-jnp.abs(x)))


def _head_sum_matrix():
    ri = lax.broadcasted_iota(jnp.int32, (LANE, LANE), 0) // DH
    ci = lax.broadcasted_iota(jnp.int32, (LANE, LANE), 1) // DH
    return (ri == ci).astype(F32)


def _group_sum(a, ones_matrix):
    hi = a.astype(jnp.bfloat16)
    lo = (a - hi.astype(F32)).astype(jnp.bfloat16)
    m = ones_matrix.astype(jnp.bfloat16)
    return jnp.dot(hi, m, preferred_element_type=F32) + jnp.dot(lo, m, preferred_element_type=F32)


def _shift_down(x, s):
    return pltpu.roll(x, s, 0)


def _shift_up(x, s):
    return pltpu.roll(x, x.shape[0] - s, 0)


ROWS_A = 4 * WIDTH
ROWS_B = ROWS_A + 2 * HEADS
ROWS_C = ROWS_B + 4 * WIDTH


def _small_rows(w_t):
    return jnp.concatenate([w_t[ROWS_A:ROWS_B], w_t[ROWS_C:], jnp.zeros((LANE - 3 * HEADS, w_t.shape[1]), w_t.dtype)])


def _inproj(x, n1w, w_t, ws_t):
    t, d = x.shape
    tb = min(TB, t)

    def body(x_ref, nw_ref, wt_ref, ws_ref, h_ref, pm_ref, ps_ref):
        xv = x_ref[...]
        r = lax.rsqrt(jnp.mean(xv * xv, axis=-1, keepdims=True) + EPS)
        h = (xv * r * nw_ref[...]).astype(MXU)
        h_ref[...] = h
        pm_ref[:, 0:ROWS_A] = _dot_nt(h, wt_ref[0:ROWS_A, :])
        pm_ref[:, ROWS_A:2 * ROWS_A] = _dot_nt(h, wt_ref[ROWS_B:ROWS_C, :])
        ps_ref[...] = _dot_nt(h, ws_ref[...])

    return pl.pallas_call(
        body, name="inproj", grid=(t // tb,),
        in_specs=[pl.BlockSpec((tb, d), lambda i: (i, 0)), _resident((1, d)), _resident(w_t.shape), _resident((LANE, d))],
        out_specs=[pl.BlockSpec((tb, d), lambda i: (i, 0)), pl.BlockSpec((tb, 2 * ROWS_A), lambda i: (i, 0)),
                   pl.BlockSpec((tb, LANE), lambda i: (i, 0))],
        out_shape=[jax.ShapeDtypeStruct((t, d), MXU), jax.ShapeDtypeStruct((t, 2 * ROWS_A), F32),
                   jax.ShapeDtypeStruct((t, LANE), F32)],
        compiler_params=_params(("arbitrary",)),
    )(x, n1w, w_t, ws_t)


def _inproj_bwd(x, n1w, dx2, dgdn, dz, dfox, dfg, dps, w_t, ws_t, carry=None):
    t, d = x.shape
    tb = min(TB, t)
    w3 = 3 * WIDTH
    c_in, c_in_specs, c_out_shape, c_out_specs, c_sem = _carry_operands(*carry) if carry else ([], [], [], [], None)
    n_c = len(c_in)

    def body(*refs):
        x_ref, nw_ref, dx2_ref, dgdn_ref, dz_ref, dfox_ref, dfg_ref, dps_ref, wm_ref, ws_ref = refs[:10]
        gx_ref, dnw_ref = refs[10 + n_c:12 + n_c]
        if carry:
            _carry_step(carry[0], refs[10:10 + n_c], refs[12 + n_c], refs[13 + n_c], refs[-1], pl.program_id(0), t // tb)
        dh = _dot(dgdn_ref[...], wm_ref[0:w3, :])
        dh += _dot(dz_ref[...], wm_ref[w3:ROWS_A, :])
        dh += _dot(dfox_ref[...], wm_ref[ROWS_B:ROWS_B + w3, :])
        dh += _dot(dfg_ref[...], wm_ref[ROWS_B + w3:ROWS_C, :])
        dh += _dot(dps_ref[...], ws_ref[...])
        xv = x_ref[...]
        r = lax.rsqrt(jnp.mean(xv * xv, axis=-1, keepdims=True) + EPS)
        xn = xv * r

        @pl.when(pl.program_id(0) == 0)
        def _():
            dnw_ref[...] = jnp.zeros_like(dnw_ref)

        dnw_ref[...] += jnp.sum(dh * xn, axis=0, keepdims=True)
        g = dh * nw_ref[...]
        gx_ref[...] = dx2_ref[...] + r * (g - xn * jnp.mean(g * xn, axis=-1, keepdims=True))

    def tok(n):
        return pl.BlockSpec((tb, n), lambda i: (i, 0))

    out = pl.pallas_call(
        body, name="inproj_bwd", grid=(t // tb,),
        in_specs=[tok(d), _resident((1, d)), tok(d), tok(w3), tok(WIDTH), tok(w3), tok(WIDTH), tok(LANE),
                  _resident(w_t.shape), _resident(ws_t.shape)] + c_in_specs,
        out_specs=[tok(d), pl.BlockSpec((1, d), lambda i: (0, 0))] + c_out_specs,
        out_shape=[jax.ShapeDtypeStruct((t, d), F32), jax.ShapeDtypeStruct((1, d), F32)] + c_out_shape,
        input_output_aliases={10 + j: 4 + j for j in range(n_c)},
        scratch_shapes=[c_sem] if carry else [],
        compiler_params=_params(("arbitrary",)),
    )(x, n1w, dx2, dgdn, dz, dfox, dfg, dps, w_t, ws_t, *c_in)
    return (out[0], out[1], _carry_state(out[2:])) if carry else tuple(out)


def _gate_lanes(shape):
    lane = lax.broadcasted_iota(jnp.int32, shape, 1)
    return lane < HEADS, (lane >= HEADS) & (lane < 2 * HEADS), (lane >= 2 * HEADS) & (lane < 3 * HEADS)


def _block_masks():
    ri = lax.broadcasted_iota(jnp.int32, (LANE, LANE), 0)
    ci = lax.broadcasted_iota(jnp.int32, (LANE, LANE), 1)
    same = (ri // CHUNK) == (ci // CHUNK)
    return ((ri >= ci).astype(F32), (ri <= ci).astype(F32), (same & (ri >= ci)).astype(F32),
            (same & (ri <= ci)).astype(F32), same.astype(F32))


def _gates(ps, gparams):
    t = ps.shape[0]
    nb = t // LANE

    def body(ps_ref, gp_ref, out_ref, run_ref, tot_ref, runt_ref):
        p = ps_ref[...]
        is_b, is_a, is_f = _gate_lanes(p.shape)
        z = p + gp_ref[0:1, :]
        neg_exp_a = -jnp.exp(gp_ref[1:2, :])
        glog = neg_exp_a * _softplus(z)
        logf = -_softplus(-z)
        out_ref[...] = jnp.where(is_b, _sigmoid(p), jnp.where(is_a, glog, jnp.where(is_f, logf, 0.0)))
        tril, _, tril_c, _, same_c = _block_masks()
        off = jnp.zeros((1, LANE), F32)
        for b in range(nb):
            rows = slice(b * LANE, (b + 1) * LANE)
            blk = out_ref[rows, :]
            ga = jnp.where(is_a[:LANE], blk, 0.0)
            fb = _hdot(tril, jnp.where(is_f[:LANE], blk, 0.0)) + off
            run = fb + _hdot(tril_c, ga)
            run_ref[rows, :] = run
            runt_ref[:, rows] = run.T
            tot_ref[rows, :] = _hdot(same_c, ga)
            off = fb[LANE - 1:LANE, :]

    return pl.pallas_call(
        body, name="gates",
        out_shape=[jax.ShapeDtypeStruct((t, LANE), F32)] * 3 + [jax.ShapeDtypeStruct((LANE, t), F32)],
        compiler_params=_params(),
    )(ps, gparams)


def _gates_bwd(ps, gparams, dcol_g, dtot_g, dcol_f, drow_g, drow_f):
    t = ps.shape[0]
    nb = t // LANE

    def body(ps_ref, gp_ref, dcg_ref, dtg_ref, dcf_ref, drg_ref, drf_ref, dps_ref, sums_ref, dl_ref, d0_ref, tr_ref):
        p = ps_ref[...]
        is_b, is_a, is_f = _gate_lanes(p.shape)
        _, triu, _, triu_c, same_c = _block_masks()
        tr_ref[...] = jnp.zeros_like(tr_ref)
        off = jnp.zeros((1, LANE), F32)
        for b in reversed(range(nb)):
            rows = slice(b * LANE, (b + 1) * LANE)
            tr_ref[HEADS:2 * HEADS, :] = drg_ref[:, rows]
            tr_ref[2 * HEADS:3 * HEADS, :] = drf_ref[:, rows]
            d = dcg_ref[rows, :] + dcf_ref[rows, :] + tr_ref[...].T
            d0_ref[rows, :] = d
            dlf = _hdot(triu, jnp.where(is_f[:LANE], d, 0.0)) + off
            dla = (_hdot(triu_c, jnp.where(is_a[:LANE], d, 0.0))
                   + _hdot(same_c, jnp.where(is_a[:LANE], dtg_ref[rows, :], 0.0)))
            dl_ref[rows, :] = dlf + dla
            off = dlf[0:1, :]
        z = p + gp_ref[0:1, :]
        neg_exp_a = -jnp.exp(gp_ref[1:2, :])
        sb = _sigmoid(p)
        glog = neg_exp_a * _softplus(z)
        dl = dl_ref[...]
        dp = jnp.where(is_b, d0_ref[...] * sb * (1.0 - sb),
                       jnp.where(is_a, dl * neg_exp_a * _sigmoid(z), jnp.where(is_f, dl * _sigmoid(-z), 0.0)))
        dps_ref[...] = dp
        s_a = jnp.sum(jnp.where(is_a, dl * glog, 0.0), axis=0, keepdims=True)
        s_p = jnp.sum(jnp.where(is_b, 0.0, dp), axis=0, keepdims=True)
        row = lax.broadcasted_iota(jnp.int32, (8, LANE), 0)
        from_a = pltpu.roll(jnp.where(row == 0, s_a, jnp.where(row == 1, s_p, 0.0)), LANE - HEADS, 1)
        from_f = pltpu.roll(jnp.where(row == 2, s_p, 0.0), LANE - 2 * HEADS, 1)
        lane = lax.broadcasted_iota(jnp.int32, (8, LANE), 1)
        sums_ref[...] = jnp.where(lane < HEADS, from_a + from_f, 0.0)

    return pl.pallas_call(
        body, name="gates_bwd",
        out_shape=[jax.ShapeDtypeStruct((t, LANE), F32), jax.ShapeDtypeStruct((8, LANE), F32)],
        scratch_shapes=[pltpu.VMEM((t, LANE), F32), pltpu.VMEM((t, LANE), F32), pltpu.VMEM((LANE, LANE), F32)],
        compiler_params=_params(),
    )(ps, gparams, dcol_g, dtot_g, dcol_f, drow_g, drow_f)


def _conv(xv, w):
    acc = w[3:4, :] * xv
    for s in range(1, 4):
        acc += w[3 - s:4 - s, :] * _shift_down(xv, s)
    return acc


PREP_ROWS = 256
HALO = 8
PREP_UNROLL = 2


def _tile_loop(n, tile, init):
    if n % PREP_UNROLL:
        return lax.fori_loop(0, n, tile, init)

    def trip(g, carry):
        for u in range(PREP_UNROLL):
            carry = tile(g * PREP_UNROLL + u, carry)
        return carry

    return lax.fori_loop(0, n // PREP_UNROLL, trip, init)


def _tile_rows(r):
    return pl.ds(pl.multiple_of(r * PREP_ROWS, PREP_ROWS), PREP_ROWS)


def _gdn_prep(pm, conv_w):
    t = pm.shape[0]
    nj = WIDTH // LANE
    win = PREP_ROWS + HALO

    def body(x_ref, w_ref, o_ref, xp_ref):
        kind = pl.program_id(0)
        xp_ref[0:HALO, :] = jnp.zeros((HALO, LANE), F32)
        xp_ref[HALO:, :] = x_ref[...]
        w = w_ref[...]
        hs = _head_sum_matrix()

        def tile(r, _, normed):
            xw = xp_ref[pl.ds(pl.multiple_of(r * PREP_ROWS, PREP_ROWS), win), :]
            acc = _conv(xw, w)[HALO:]
            out = acc * _sigmoid(acc)
            if normed:
                out = out * lax.rsqrt(_group_sum(out * out, hs) + EPS)
            o_ref[0, 0, _tile_rows(r), :] = out[:, :DH]
            o_ref[0, 1, _tile_rows(r), :] = out[:, DH:]
            return 0

        @pl.when(kind < 2)
        def _():
            _tile_loop(t // PREP_ROWS, functools.partial(tile, normed=True), 0)

        @pl.when(kind == 2)
        def _():
            _tile_loop(t // PREP_ROWS, functools.partial(tile, normed=False), 0)

    return pl.pallas_call(
        body, name="gdn_prep", grid=(3, nj),
        in_specs=[pl.BlockSpec((t, LANE), lambda i, j: (0, i * nj + j)),
                  pl.BlockSpec((4, LANE), lambda i, j: (0, i * nj + j))],
        out_specs=pl.BlockSpec((1, 2, t, DH), lambda i, j: (i, j, 0, 0)),
        out_shape=jax.ShapeDtypeStruct((3, HEADS, t, DH), F32),
        scratch_shapes=[pltpu.VMEM((t + HALO, LANE), F32)],
        compiler_params=_params(("arbitrary", "arbitrary")),
    )(pm, conv_w)


def _gdn_prep_bwd(pm, conv_w, dqkv):
    t = pm.shape[0]
    nj = WIDTH // LANE
    win = PREP_ROWS + 2 * HALO

    def body(x_ref, w_ref, d_ref, dx_ref, dw_ref, xp_ref, dp_ref):
        kind = pl.program_id(0)
        zeros = jnp.zeros((HALO, LANE), F32)
        for ref in (xp_ref, dp_ref):
            ref[0:HALO, :] = zeros
            ref[HALO + t:, :] = zeros
        xp_ref[HALO:HALO + t, :] = x_ref[...]
        dp_ref[HALO:HALO + t, 0:DH] = d_ref[0, 0]
        dp_ref[HALO:HALO + t, DH:] = d_ref[0, 1]
        w = w_ref[...]
        hs = _head_sum_matrix()
        rows = lax.broadcasted_iota(jnp.int32, (win, LANE), 0)
        in_tile = (rows >= HALO) & (rows < HALO + PREP_ROWS)

        def tile(r, dw, normed):
            start = pl.multiple_of(r * PREP_ROWS, PREP_ROWS)
            xw = xp_ref[pl.ds(start, win), :]
            dy = dp_ref[pl.ds(start, win), :]
            acc = _conv(xw, w)
            sg = _sigmoid(acc)
            if normed:
                y = acc * sg
                rn = lax.rsqrt(_group_sum(y * y, hs) + EPS)
                yn = y * rn
                dy = rn * (dy - yn * _group_sum(dy * yn, hs))
            dacc = dy * sg * (1.0 + acc * (1.0 - sg))
            dx = w[3:4, :] * dacc
            for s in range(1, 4):
                dx += w[3 - s:4 - s, :] * _shift_up(dacc, s)
            dx_ref[_tile_rows(r), :] = dx[HALO:HALO + PREP_ROWS]
            dm = jnp.where(in_tile, dacc, 0.0)
            return tuple(dw[i] + jnp.sum(dm * (xw if i == 3 else _shift_down(xw, 3 - i)), axis=0, keepdims=True)
                         for i in range(4))

        def run(normed):
            dw = _tile_loop(t // PREP_ROWS, functools.partial(tile, normed=normed),
                            tuple(jnp.zeros((1, LANE), F32) for _ in range(4)))
            for i in range(4):
                dw_ref[i:i + 1, :] = dw[i]

        pl.when(kind < 2)(functools.partial(run, True))
        pl.when(kind == 2)(functools.partial(run, False))

    return pl.pallas_call(
        body, name="gdn_prep_bwd", grid=(3, nj),
        in_specs=[pl.BlockSpec((t, LANE), lambda i, j: (0, i * nj + j)),
                  pl.BlockSpec((4, LANE), lambda i, j: (0, i * nj + j)),
                  pl.BlockSpec((1, 2, t, DH), lambda i, j: (i, j, 0, 0))],
        out_specs=[pl.BlockSpec((t, LANE), lambda i, j: (0, i * nj + j)),
                   pl.BlockSpec((4, LANE), lambda i, j: (0, i * nj + j))],
        out_shape=[jax.ShapeDtypeStruct((t, 3 * WIDTH), F32), jax.ShapeDtypeStruct((4, 3 * WIDTH), F32)],
        scratch_shapes=[pltpu.VMEM((t + 2 * HALO, LANE), F32), pltpu.VMEM((t + 2 * HALO, LANE), F32)],
        compiler_params=_params(("arbitrary", "arbitrary")),
    )(pm, conv_w, dqkv)


FOX_COL0 = 4 * WIDTH // LANE


def _fox_prep(pm, nw):
    t = pm.shape[0]
    nj = WIDTH // LANE

    def body(x_ref, w_ref, o_ref):
        kind = pl.program_id(0)
        hs = _head_sum_matrix()
        wk = w_ref[pl.ds(kind, 1), :]

        def tile(r, _, normed):
            out = x_ref[_tile_rows(r), :]
            if normed:
                out = out * lax.rsqrt(_group_sum(out * out, hs) * (1.0 / DH) + EPS) * wk
            o_ref[0, 0, _tile_rows(r), :] = out[:, :DH]
            o_ref[0, 1, _tile_rows(r), :] = out[:, DH:]
            return 0

        @pl.when(kind < 2)
        def _():
            _tile_loop(t // PREP_ROWS, functools.partial(tile, normed=True), 0)

        @pl.when(kind == 2)
        def _():
            _tile_loop(t // PREP_ROWS, functools.partial(tile, normed=False), 0)

    return pl.pallas_call(
        body, name="fox_prep", grid=(3, nj),
        in_specs=[pl.BlockSpec((t, LANE), lambda i, j: (0, FOX_COL0 + i * nj + j)),
                  pl.BlockSpec((3, LANE), lambda i, j: (0, 0))],
        out_specs=pl.BlockSpec((1, 2, t, DH), lambda i, j: (i, j, 0, 0)),
        out_shape=jax.ShapeDtypeStruct((3, HEADS, t, DH), F32),
        compiler_params=_params(("arbitrary", "arbitrary")),
    )(pm, nw)


def _fox_prep_bwd(pm, nw, dqkv):
    t = pm.shape[0]
    nj = WIDTH // LANE

    def body(x_ref, w_ref, d_ref, dx_ref, dw_ref):
        kind = pl.program_id(0)
        hs = _head_sum_matrix()
        wk = w_ref[pl.ds(kind, 1), :]

        def tile(r, dw):
            xv = x_ref[_tile_rows(r), :]
            rn = lax.rsqrt(_group_sum(xv * xv, hs) * (1.0 / DH) + EPS)
            xn = xv * rn
            d = jnp.concatenate([d_ref[0, 0, _tile_rows(r), :], d_ref[0, 1, _tile_rows(r), :]], axis=1)
            g = d * wk
            dx_ref[_tile_rows(r), :] = rn * (g - xn * _group_sum(g * xn, hs) * (1.0 / DH))
            return dw + jnp.sum(d * xn, axis=0, keepdims=True)

        def copy_tile(r, _):
            dx_ref[_tile_rows(r), :] = jnp.concatenate([d_ref[0, 0, _tile_rows(r), :], d_ref[0, 1, _tile_rows(r), :]], axis=1)
            return 0

        @pl.when(kind < 2)
        def _():
            dw_ref[0, 0] = _tile_loop(t // PREP_ROWS, tile, jnp.zeros((1, LANE), F32))

        @pl.when(kind == 2)
        def _():
            _tile_loop(t // PREP_ROWS, copy_tile, 0)
            dw_ref[0, 0] = jnp.zeros((1, LANE), F32)

    return pl.pallas_call(
        body, name="fox_prep_bwd", grid=(3, nj),
        in_specs=[pl.BlockSpec((t, LANE), lambda i, j: (0, FOX_COL0 + i * nj + j)),
                  pl.BlockSpec((3, LANE), lambda i, j: (0, 0)),
                  pl.BlockSpec((1, 2, t, DH), lambda i, j: (i, j, 0, 0))],
        out_specs=[pl.BlockSpec((t, LANE), lambda i, j: (0, i * nj + j)),
                   pl.BlockSpec((1, 1, 1, LANE), lambda i, j: (i, j, 0, 0))],
        out_shape=[jax.ShapeDtypeStruct((t, 3 * WIDTH), F32), jax.ShapeDtypeStruct((3, nj, 1, LANE), F32)],
        compiler_params=_params(("arbitrary", "arbitrary")),
    )(pm, nw, dqkv)


SC = 256
CPS = SC // CHUNK
GDN_HP_FWD = 4
GDN_HP_BWD = 2
Q_SCALE = DH ** -0.5


def _sc_masks():
    ri = lax.broadcasted_iota(jnp.int32, (SC, SC), 0)
    ci = lax.broadcasted_iota(jnp.int32, (SC, SC), 1)
    same = (ri // CHUNK) == (ci // CHUNK)
    return same & (ri >= ci), same & (ri > ci), ri == ci


def _unit_lower_inverses(ms, eye):
    invs = [jnp.where(eye, 1.0, 0.0) + m for m in ms]
    for _ in range(5):
        ms = [_dot(m, m) for m in ms]
        invs = [inv + _dot(inv, m) for inv, m in zip(invs, ms)]
    return invs


def _lane_col(blk, lane_idx):
    lane = lax.broadcasted_iota(jnp.int32, blk.shape, 1)
    return jnp.sum(jnp.where(lane == lane_idx, blk, 0.0), axis=1, keepdims=True)


def _to_lane(col, lane_idx):
    lane = lax.broadcasted_iota(jnp.int32, (col.shape[0], LANE), 1)
    return jnp.where(lane == lane_idx, col, 0.0)


def _gdn_columns(gates_ref, run_ref, tot_ref, runt_ref, rows, h):
    return (_lane_col(gates_ref[rows, :], h), _lane_col(run_ref[rows, :], HEADS + h),
            _lane_col(tot_ref[rows, :], HEADS + h), runt_ref[pl.ds(h, 1), rows])


def _gdn_local(q, k, beta, gc, gl, grow, causal, with_kk=True):
    decay = jnp.exp(jnp.where(causal, gc - grow, -1e30))
    egc = jnp.exp(gc)
    ekd = jnp.exp(gl - gc)
    qs = q * Q_SCALE
    kb = k * beta
    if with_kk:
        both = _dot_nt(jnp.concatenate([kb, qs], axis=0), k)
        kk, qk = both[:SC], both[SC:]
    else:
        kk, qk = None, _dot_nt(qs, k)
    return beta, gl, decay, egc, ekd, qs, kb, kk, qk, jnp.where(causal, qk * decay, 0.0)


def _chunk_rows(c):
    return pl.ds(pl.multiple_of(c * CHUNK, CHUNK), CHUNK)


def _sc_rows(b):
    return pl.ds(pl.multiple_of(b * SC, SC), SC)


def _gdn_fwd(qkv, gates, run, tot, run_t):
    t = qkv.shape[2]
    nc = t // CHUNK
    nsc = t // SC

    hp_n = GDN_HP_FWD
    heads = range(hp_n)

    def body(qkv_ref, gates_ref, run_ref, tot_ref, runt_ref, o_ref, st_ref, inv_ref, kc_s, qc_s, g_s, au_s):
        hp = pl.program_id(0)
        causal, strict, eye = _sc_masks()

        def local(b, _):
            rows = _sc_rows(b)
            loc = [_gdn_local(qkv_ref[0, hh, rows, :], qkv_ref[1, hh, rows, :],
                              *_gdn_columns(gates_ref, run_ref, tot_ref, runt_ref, rows, hp * hp_n + hh), causal)
                   for hh in heads]
            invs = _unit_lower_inverses([-jnp.where(strict, l[7] * l[2], 0.0) for l in loc], eye)
            uws = []
            for hh in heads:
                beta, _, _, egc, _, _, kb, _, _, _ = loc[hh]
                inv_ref[hh, rows, :] = invs[hh].astype(inv_ref.dtype)
                uws.append(_dot(invs[hh], jnp.concatenate([qkv_ref[2, hh, rows, :] * beta, kb * egc], axis=1)))
            for hh in heads:
                _, _, _, egc, ekd, qs, _, _, _, attn = loc[hh]
                auw = _dot(attn, uws[hh])
                g_s[hh, rows, :] = qs * egc - auw[:, DH:]
                au_s[hh, rows, :] = auw[:, :DH]
                kd = qkv_ref[1, hh, rows, :] * ekd
                for j in range(CPS):
                    sl = slice(j * CHUNK, (j + 1) * CHUNK)
                    both = _dot_tn(kd[sl], uws[hh][sl])
                    kc_s[hh, b * CPS + j] = both[:, DH:]
                    qc_s[hh, b * CPS + j] = both[:, :DH]
            return 0

        lax.fori_loop(0, nsc, local, 0)

        def step(c, states):
            rows = _chunk_rows(c)
            tot_row = tot_ref[pl.ds(c * CHUNK, 1), :]
            new = []
            for hh in heads:
                s = states[hh]
                st_ref[hh, c] = s
                o_ref[hh, rows, :] = _dot(g_s[hh, rows, :], s) + au_s[hh, rows, :]
                egl = jnp.exp(_lane_col(tot_row, HEADS + hp * hp_n + hh))
                new.append(egl * s - _dot(kc_s[hh, c], s) + qc_s[hh, c])
            return tuple(new)

        lax.fori_loop(0, nc, step, tuple(jnp.zeros((DH, DH), F32) for _ in heads))

    whole = pl.BlockSpec((t, LANE), lambda h: (0, 0))
    once = dict(pipeline_mode=pl.Buffered(1))
    return pl.pallas_call(
        body, name="gdn_fwd", grid=(HEADS // hp_n,),
        in_specs=[pl.BlockSpec((3, hp_n, t, DH), lambda h: (0, h, 0, 0), **once), whole, whole, whole,
                  pl.BlockSpec((HEADS, t), lambda h: (1, 0))],
        out_specs=[pl.BlockSpec((hp_n, t, DH), lambda h: (h, 0, 0), **once),
                   pl.BlockSpec((hp_n, nc, DH, DH), lambda h: (h, 0, 0, 0), **once),
                   pl.BlockSpec((hp_n, t, SC), lambda h: (h, 0, 0), **once)],
        out_shape=[jax.ShapeDtypeStruct((HEADS, t, DH), F32), jax.ShapeDtypeStruct((HEADS, nc, DH, DH), F32),
                   jax.ShapeDtypeStruct((HEADS, t, SC), MXU)],
        scratch_shapes=[pltpu.VMEM((hp_n, nc, DH, DH), F32), pltpu.VMEM((hp_n, nc, DH, DH), F32),
                        pltpu.VMEM((hp_n, t, DH), F32), pltpu.VMEM((hp_n, t, DH), F32)],
        compiler_params=_params(("arbitrary",)),
    )(qkv, gates, run, tot, run_t)


def _gdn_bwd(qkv, gates, run, tot, run_t, inv, states, do):
    t = qkv.shape[2]
    nc = t // CHUNK
    nsc = t // SC

    hp_n = GDN_HP_BWD
    heads = range(hp_n)

    def body(qkv_ref, gates_ref, run_ref, tot_ref, runt_ref, inv_ref, st_ref, do_ref,
             dqkv_ref, dcol_ref, dtot_ref, drow_ref, uw_s, kc_s, h_s, dsn_s):
        hp = pl.program_id(0)
        causal, strict, _ = _sc_masks()

        @pl.when(hp == 0)
        def _():
            dcol_ref[...] = jnp.zeros_like(dcol_ref)
            dtot_ref[...] = jnp.zeros_like(dtot_ref)

        def local_of(hh, rows, with_kk=True):
            return _gdn_local(qkv_ref[0, hh, rows, :], qkv_ref[1, hh, rows, :],
                              *_gdn_columns(gates_ref, run_ref, tot_ref, runt_ref, rows, hp * hp_n + hh), causal, with_kk)

        def local(b, _):
            rows = _sc_rows(b)
            loc = [local_of(hh, rows, with_kk=False) for hh in heads]
            us, ws = [], []
            for hh in heads:
                beta, _, _, egc, _, _, kb, _, _, _ = loc[hh]
                inv_b = inv_ref[hh, rows, :]
                us.append(_dot(inv_b, qkv_ref[2, hh, rows, :] * beta))
                ws.append(_dot(inv_b, kb * egc))
            gs = [loc[hh][5] * loc[hh][3] - _dot(loc[hh][9], ws[hh]) for hh in heads]
            for hh in heads:
                uw_s[hh, rows, :] = jnp.concatenate([us[hh], ws[hh]], axis=1)
                kd = qkv_ref[1, hh, rows, :] * loc[hh][4]
                dout = do_ref[hh, rows, :]
                for j in range(CPS):
                    sl = slice(j * CHUNK, (j + 1) * CHUNK)
                    kc_s[hh, b * CPS + j] = _dot_tn(kd[sl], ws[hh][sl])
                    h_s[hh, b * CPS + j] = _dot_tn(gs[hh][sl], dout[sl])
            return 0

        lax.fori_loop(0, nsc, local, 0)

        def step(i, dss):
            c = nc - 1 - i
            tot_row = tot_ref[pl.ds(c * CHUNK, 1), :]
            new = []
            for hh in heads:
                ds = dss[hh]
                dsn_s[hh, c] = ds
                egl = jnp.exp(_lane_col(tot_row, HEADS + hp * hp_n + hh))
                new.append(egl * ds - _dot_tn(kc_s[hh, c], ds) + h_s[hh, c])
            return tuple(new)

        lax.fori_loop(0, nc, step, tuple(jnp.zeros((DH, DH), F32) for _ in heads))

        def back(b, _):
            rows = _sc_rows(b)
            first = lax.broadcasted_iota(jnp.int32, (CHUNK, 1), 0) == 0
            loc = [local_of(hh, rows) for hh in heads]
            sign = jnp.where(lax.broadcasted_iota(jnp.int32, (1, LANE), 1) < DH, 1.0, -1.0)
            mid = []
            for hh in heads:
                beta, gl, decay, egc, ekd, qs, kb, kk, qk, attn = loc[hh]
                uw = uw_s[hh, rows, :]
                kd = qkv_ref[1, hh, rows, :] * ekd
                dout = do_ref[hh, rows, :]
                dg_p, dkd_p, duw_p, dgl_p = [], [], [], []
                for j in range(CPS):
                    sl = slice(j * CHUNK, (j + 1) * CHUNK)
                    s = st_ref[hh, b * CPS + j]
                    dsn = dsn_s[hh, b * CPS + j]
                    both = _dot_nt(jnp.concatenate([dsn, dout[sl]], axis=0), s)
                    dg_p.append(both[CHUNK:])
                    dsc = jnp.concatenate([dsn, -both[:CHUNK]], axis=1)
                    dkd_p.append(_dot_nt(uw[sl], dsc))
                    duw_p.append(_dot(kd[sl], dsc))
                    degl = jnp.sum(jnp.sum(s * dsn, axis=1, keepdims=True), axis=0, keepdims=True)
                    dgl_p.append(jnp.where(first, degl * jnp.exp(gl[j * CHUNK:j * CHUNK + 1, :]), 0.0))
                dg, dkd = jnp.concatenate(dg_p, axis=0), jnp.concatenate(dkd_p, axis=0)
                dod = jnp.concatenate([dout, dg], axis=1)
                da = jnp.where(causal, _dot_nt(dod, uw * sign), 0.0)
                duw = jnp.concatenate(duw_p, axis=0) + _dot_tn(attn, dod) * sign
                mid.append((kd, dg, dkd, da, duw, jnp.concatenate(dgl_p, axis=0)))
            inv_ts = [inv_ref[hh, rows, :].astype(F32).T for hh in heads]
            its = [_dot(inv_ts[hh], mid[hh][4]) for hh in heads]
            dinvs = []
            for hh in heads:
                beta, _, _, egc, _, _, kb, _, _, _ = loc[hh]
                dinvs.append(_dot_nt(mid[hh][4], jnp.concatenate([qkv_ref[2, hh, rows, :] * beta, kb * egc], axis=1)))
            half = [_dot(inv_ts[hh], dinvs[hh]) for hh in heads]
            dls = [jnp.where(strict, -_dot(half[hh], inv_ts[hh]), 0.0) for hh in heads]
            for hh in heads:
                head = hp * hp_n + hh
                beta, gl, decay, egc, ekd, qs, kb, kk, qk, attn = loc[hh]
                kd, dg, dkd, da, _, dgl_first = mid[hh]
                k, v = qkv_ref[1, hh, rows, :], qkv_ref[2, hh, rows, :]
                dvb, dkbe = its[hh][:, :DH], its[hh][:, DH:]
                dl = dls[hh]
                dlogd = (dl * kk + da * qk) * decay
                dd = jnp.concatenate([dl * decay, da * decay], axis=0)
                ddk = _dot(dd, k)
                dkb = ddk[:SC] + dkbe * egc
                dqs = ddk[SC:] + dg * egc
                dk = _dot_tn(dd, jnp.concatenate([kb, qs], axis=0)) + dkd * ekd + dkb * beta
                dkd_kd = jnp.sum(dkd * kd, axis=1, keepdims=True)
                narrow = dg * (qs * egc) + dkbe * (kb * egc)
                wide = dlogd[:, :LANE] + dlogd[:, LANE:] + jnp.concatenate([narrow, jnp.zeros((SC, LANE - DH), F32)], axis=1)
                dgc = jnp.sum(wide, axis=1, keepdims=True) - dkd_kd
                dbeta = jnp.sum(dkb * k + dvb * v, axis=1, keepdims=True)
                dqkv_ref[0, hh, rows, :] = dqs * Q_SCALE
                dqkv_ref[1, hh, rows, :] = dk
                dqkv_ref[2, hh, rows, :] = dvb * beta
                dcol_ref[rows, :] += _to_lane(dbeta, head) + _to_lane(dgc, HEADS + head)
                dtot_ref[rows, :] += _to_lane(dkd_kd + dgl_first, HEADS + head)
                drow_ref[pl.ds(head, 1), rows] = -jnp.sum(dlogd, axis=0, keepdims=True)
            return 0

        lax.fori_loop(0, nsc, back, 0)

    whole = pl.BlockSpec((t, LANE), lambda h: (0, 0))
    rowspec = pl.BlockSpec((HEADS, t), lambda h: (0, 0))
    sq = pltpu.VMEM((hp_n, nc, DH, DH), F32)
    per_head = pltpu.VMEM((hp_n, t, 2 * DH), F32)
    return pl.pallas_call(
        body, name="gdn_bwd", grid=(HEADS // hp_n,),
        in_specs=[pl.BlockSpec((3, hp_n, t, DH), lambda h: (0, h, 0, 0)), whole, whole, whole,
                  pl.BlockSpec((HEADS, t), lambda h: (1, 0)),
                  pl.BlockSpec((hp_n, t, SC), lambda h: (h, 0, 0)), pl.BlockSpec((hp_n, nc, DH, DH), lambda h: (h, 0, 0, 0)),
                  pl.BlockSpec((hp_n, t, DH), lambda h: (h, 0, 0))],
        out_specs=[pl.BlockSpec((3, hp_n, t, DH), lambda h: (0, h, 0, 0)), whole, whole, rowspec],
        out_shape=[jax.ShapeDtypeStruct((3, HEADS, t, DH), F32), jax.ShapeDtypeStruct((t, LANE), F32),
                   jax.ShapeDtypeStruct((t, LANE), F32), jax.ShapeDtypeStruct((HEADS, t), F32)],
        scratch_shapes=[per_head, sq, sq, sq],
        compiler_params=_params(("arbitrary",)),
    )(qkv, gates, run, tot, run_t, inv, states, do)


FOX_HP_FWD = 4
FOX_HP_BWD = 4


def _wide_t(a):
    r = a.shape[0]
    return jnp.concatenate([a, jnp.zeros((r, LANE - DH), F32)], axis=1).T[:DH]


def _tall_t(a):
    r = a.shape[1]
    return jnp.concatenate([a, jnp.zeros((LANE - DH, r), F32)], axis=0).T[:, :DH]


def _key_side_f(run_blk, head, qb):
    col = jnp.broadcast_to(_lane_col(run_blk, 2 * HEADS + head), (run_blk.shape[0], LANE))
    return jnp.concatenate([col] * (qb // LANE), axis=1)


def _diag_mask(qb):
    return lax.broadcasted_iota(jnp.int32, (qb, qb), 0) <= lax.broadcasted_iota(jnp.int32, (qb, qb), 1)


def _fox_fwd(qkv, run, run_t, carry=None):
    t = qkv.shape[2]
    qb = min(QB, t)
    nq = t // qb
    hp_n = FOX_HP_FWD
    c_in, c_in_specs, c_out_shape, c_out_specs, c_sem = _carry_operands(*carry) if carry else ([], [], [], [], None)
    n_c = len(c_in)

    def body(*refs):
        q_ref, k_ref, v_ref, run_ref, runt_ref = refs[:5]
        o_ref, lse_ref = refs[5 + n_c:7 + n_c]
        s0 = 7 + n_c + len(c_out_shape)
        kb_s, vt_s, raw_s, m_s, l_s, acc_s = refs[s0:s0 + 6]
        hp = pl.program_id(0)
        i = pl.program_id(1)

        if carry:
            _carry_step(carry[0], refs[5:5 + n_c], refs[7 + n_c], refs[8 + n_c], refs[-1], hp * nq + i,
                        (HEADS // hp_n) * nq)

        @pl.when(i == 0)
        def _():
            for hh in range(hp_n):
                kb_s[hh] = k_ref[0, hh].astype(MXU)
                for b in range(nq):
                    rows = slice(b * qb, (b + 1) * qb)
                    vt_s[hh, :, rows] = _wide_t(v_ref[0, hh, rows, :]).astype(MXU)

        qrows = pl.ds(pl.multiple_of(i * qb, qb), qb)
        qs = [(q_ref[0, hh] * Q_SCALE).astype(MXU) for hh in range(hp_n)]
        fq = [runt_ref[pl.ds(hp * hp_n + hh, 1), qrows] for hh in range(hp_n)]

        def block_rows(j):
            return pl.ds(pl.multiple_of(j * qb, qb), qb)

        def scores(j):
            for hh in range(hp_n):
                raw_s[j % 2, hh] = _dot_nt(kb_s[hh, block_rows(j), :], qs[hh])

        def absorb(j, diagonal):
            rows = block_rows(j)
            run_blk = run_ref[rows, :]
            stats, pv = [], []
            for hh in range(hp_n):
                m, l = m_s[hh], l_s[hh]
                st = raw_s[j % 2, hh] + (fq[hh] - _key_side_f(run_blk, hp * hp_n + hh, qb))
                if diagonal:
                    st = jnp.where(_diag_mask(qb), st, -1e30)
                m_new = jnp.maximum(m, jnp.max(st, axis=0, keepdims=True))
                p = jnp.exp(st - m_new)
                alpha = jnp.exp(m - m_new)
                stats.append((m_new, alpha * l + jnp.sum(p, axis=0, keepdims=True), alpha))
                pv.append(_dot(vt_s[hh, :, rows], p))
            for hh in range(hp_n):
                m_s[hh], l_s[hh] = stats[hh][0], stats[hh][1]
                acc_s[hh] = stats[hh][2] * acc_s[hh] + pv[hh]

        def kstep(j, _):
            scores(j + 1)
            absorb(j, False)
            return 0

        for hh in range(hp_n):
            m_s[hh] = jnp.full((1, qb), -1e30, F32)
            l_s[hh] = jnp.zeros((1, qb), F32)
            acc_s[hh] = jnp.zeros((DH, qb), F32)
        scores(0)
        lax.fori_loop(0, i, kstep, 0)
        absorb(i, True)
        for hh in range(hp_n):
            l = l_s[hh]
            o_ref[hh] = _tall_t(acc_s[hh] / l)
            lse_ref[pl.ds(hp * hp_n + hh, 1), qrows] = m_s[hh] + jnp.log(l)

    out = pl.pallas_call(
        body, name="fox_fwd", grid=(HEADS // hp_n, nq),
        in_specs=[pl.BlockSpec((1, hp_n, qb, DH), lambda h, i: (0, h, i, 0)),
                  pl.BlockSpec((1, hp_n, t, DH), lambda h, i: (1, h, 0, 0)),
                  pl.BlockSpec((1, hp_n, t, DH), lambda h, i: (2, h, 0, 0)),
                  pl.BlockSpec((t, LANE), lambda h, i: (0, 0)),
                  pl.BlockSpec((HEADS, t), lambda h, i: (2, 0))] + c_in_specs,
        out_specs=[pl.BlockSpec((hp_n, qb, DH), lambda h, i: (h, i, 0)),
                   pl.BlockSpec((HEADS, t), lambda h, i: (0, 0))] + c_out_specs,
        out_shape=[jax.ShapeDtypeStruct((HEADS, t, DH), F32), jax.ShapeDtypeStruct((HEADS, t), F32)] + c_out_shape,
        input_output_aliases={5 + j: 4 + j for j in range(n_c)},
        scratch_shapes=[pltpu.VMEM((hp_n, t, DH), MXU), pltpu.VMEM((hp_n, DH, t), MXU), pltpu.VMEM((2, hp_n, qb, qb), F32),
                        pltpu.VMEM((hp_n, 1, qb), F32), pltpu.VMEM((hp_n, 1, qb), F32), pltpu.VMEM((hp_n, DH, qb), F32)]
        + ([c_sem] if carry else []),
        compiler_params=_params(("arbitrary", "arbitrary")),
    )(qkv, qkv, qkv, run, run_t, *c_in)
    return (out[0], out[1], _carry_state(out[2:])) if carry else (out[0], out[1])


def _fox_bwd(qkv, run, run_t, o, lse, do, carry=None):
    t = qkv.shape[2]
    qb = min(QB, t)
    nq = t // qb
    hp_n = FOX_HP_BWD
    c_in, c_in_specs, c_out_shape, c_out_specs, c_sem = _carry_operands(*carry) if carry else ([], [], [], [], None)
    n_c = len(c_in)

    def body(*refs):
        q_ref, k_ref, v_ref, run_ref, runt_ref, o_ref, lse_ref, do_ref = refs[:8]
        dqkv_ref, dcol_ref, drow_ref = refs[8 + n_c:11 + n_c]
        s0 = 11 + n_c + len(c_out_shape)
        dqt_s, raw_s, dpt_s, dro_s, dk_s, dv_s, dsum_s = refs[s0:s0 + 7]
        hp = pl.program_id(0)
        j = pl.program_id(1)

        if carry:
            _carry_step(carry[0], refs[8:8 + n_c], refs[11 + n_c], refs[12 + n_c], refs[-1], hp * nq + j,
                        (HEADS // hp_n) * nq)

        @pl.when(j == 0)
        def _():
            dqt_s[...] = jnp.zeros_like(dqt_s)

        @pl.when((j == 0) & (hp == 0))
        def _():
            dcol_ref[...] = jnp.zeros_like(dcol_ref)
            drow_ref[...] = jnp.zeros_like(drow_ref)

        krows = pl.ds(pl.multiple_of(j * qb, qb), qb)
        run_blk = run_ref[krows, :]
        ones8 = jnp.ones((8, DH), MXU)
        kb, kt, vb, fk = [], [], [], []
        for hh in range(hp_n):
            kf = k_ref[0, hh]
            kb.append(kf.astype(MXU))
            kt.append(_wide_t(kf).astype(MXU))
            vb.append(v_ref[0, hh].astype(MXU))
            fk.append(_key_side_f(run_blk, hp * hp_n + hh, qb))

        def block_rows(i):
            return pl.ds(pl.multiple_of(i * qb, qb), qb)

        def products(i):
            rows = block_rows(i)
            slot = i % 2
            for hh in range(hp_n):
                dout = do_ref[hh, rows, :]
                x = dout * o_ref[hh, rows, :]
                x_hi = x.astype(MXU)
                raw_s[slot, hh] = _dot_nt(kb[hh], q_ref[0, hh, rows, :] * Q_SCALE)
                dpt_s[slot, hh] = _dot_nt(vb[hh], dout)
                dro_s[slot, hh] = _dot_nt(ones8, x_hi) + _dot_nt(ones8, x - x_hi.astype(F32))

        def absorb(i, diagonal):
            rows = block_rows(i)
            slot = i % 2
            pieces = []
            for hh in range(hp_n):
                head = hp * hp_n + hh
                raw, dpt, drow = raw_s[slot, hh], dpt_s[slot, hh], dro_s[slot, hh, 0:1, :]
                off = runt_ref[pl.ds(head, 1), rows] - lse_ref[pl.ds(head, 1), rows]
                st = raw + (off - fk[hh])
                if diagonal:
                    st = jnp.where(_diag_mask(qb), st, -1e30)
                pt = jnp.exp(st)
                dst = pt * (dpt - drow)
                drow_ref[pl.ds(head, 1), rows] += jnp.sum(dst, axis=0, keepdims=True)
                folded = dst[:, 0:LANE]
                for c in range(1, qb // LANE):
                    folded = folded + dst[:, c * LANE:(c + 1) * LANE]
                pieces.append((_dot(dst, q_ref[0, hh, rows, :] * Q_SCALE), _dot(pt, do_ref[hh, rows, :]),
                               _dot(kt[hh], dst), folded))
            for hh in range(hp_n):
                dqt_s[hh, :, rows] += pieces[hh][2]
                if diagonal:
                    dk_s[hh], dv_s[hh], dsum_s[hh] = pieces[hh][0], pieces[hh][1], pieces[hh][3]
                else:
                    dk_s[hh] += pieces[hh][0]
                    dv_s[hh] += pieces[hh][1]
                    dsum_s[hh] += pieces[hh][3]

        def qstep(i, _):
            products(jnp.minimum(i + 1, nq - 1))
            absorb(i, False)
            return 0

        products(j)
        products(jnp.minimum(j + 1, nq - 1))
        absorb(j, True)
        lax.fori_loop(j + 1, nq, qstep, 0)
        for hh in range(hp_n):
            dqkv_ref[1, hh, krows, :] = dk_s[hh]
            dqkv_ref[2, hh, krows, :] = dv_s[hh]
            dcol_ref[krows, :] += _to_lane(-jnp.sum(dsum_s[hh], axis=1, keepdims=True), 2 * HEADS + hp * hp_n + hh)

        @pl.when(j == nq - 1)
        def _():
            for hh in range(hp_n):
                for b in range(nq):
                    rows = slice(b * qb, (b + 1) * qb)
                    dqkv_ref[0, hh, rows, :] = _tall_t(dqt_s[hh, :, rows]) * Q_SCALE

    once = dict(pipeline_mode=pl.Buffered(1))
    full = pl.BlockSpec((hp_n, t, DH), lambda h, j: (h, 0, 0), **once)
    rows8 = pl.BlockSpec((HEADS, t), lambda h, j: (0, 0))
    out = pl.pallas_call(
        body, name="fox_bwd", grid=(HEADS // hp_n, nq),
        in_specs=[pl.BlockSpec((1, hp_n, t, DH), lambda h, j: (0, h, 0, 0), **once),
                  pl.BlockSpec((1, hp_n, qb, DH), lambda h, j: (1, h, j, 0)),
                  pl.BlockSpec((1, hp_n, qb, DH), lambda h, j: (2, h, j, 0)),
                  pl.BlockSpec((t, LANE), lambda h, j: (0, 0), **once), pl.BlockSpec((HEADS, t), lambda h, j: (2, 0)),
                  full, rows8, full] + c_in_specs,
        out_specs=[pl.BlockSpec((3, hp_n, t, DH), lambda h, j: (0, h, 0, 0), **once),
                   pl.BlockSpec((t, LANE), lambda h, j: (0, 0)), rows8] + c_out_specs,
        out_shape=[jax.ShapeDtypeStruct((3, HEADS, t, DH), F32), jax.ShapeDtypeStruct((t, LANE), F32),
                   jax.ShapeDtypeStruct((HEADS, t), F32)] + c_out_shape,
        input_output_aliases={8 + j: 5 + j for j in range(n_c)},
        scratch_shapes=[pltpu.VMEM((hp_n, DH, t), F32), pltpu.VMEM((2, hp_n, qb, qb), F32), pltpu.VMEM((2, hp_n, qb, qb), F32),
                        pltpu.VMEM((2, hp_n, 8, qb), F32), pltpu.VMEM((hp_n, qb, DH), F32), pltpu.VMEM((hp_n, qb, DH), F32),
                        pltpu.VMEM((hp_n, qb, LANE), F32)] + ([c_sem] if carry else []),
        compiler_params=_params(("arbitrary", "arbitrary")),
    )(qkv, qkv, qkv, run, run_t, o, lse, do, *c_in)
    return (out[0], out[1], out[2], _carry_state(out[3:])) if carry else tuple(out)


Z_COL0 = 3 * WIDTH // LANE
FGATE_COL0 = 7 * WIDTH // LANE


def _gdn_post(o, pm, onw):
    t = pm.shape[0]

    def body(o_ref, z_ref, w_ref, m_ref):
        z = z_ref[...]
        sz = z * _sigmoid(z)
        halves = []
        for hh in range(2):
            ov = o_ref[hh]
            n = ov * lax.rsqrt(jnp.mean(ov * ov, axis=-1, keepdims=True) + EPS) * w_ref[...]
            halves.append(n * sz[:, hh * DH:(hh + 1) * DH])
        m_ref[...] = jnp.concatenate(halves, axis=1).astype(m_ref.dtype)

    return pl.pallas_call(
        body, name="gdn_post", grid=(WIDTH // LANE,),
        in_specs=[pl.BlockSpec((2, t, DH), lambda j: (j, 0, 0)), pl.BlockSpec((t, LANE), lambda j: (0, Z_COL0 + j)),
                  pl.BlockSpec((1, DH), lambda j: (0, 0))],
        out_specs=pl.BlockSpec((t, LANE), lambda j: (0, j)),
        out_shape=jax.ShapeDtypeStruct((t, WIDTH), MXU),
        compiler_params=_params(("arbitrary",)),
    )(o, pm, onw)


def _gdn_post_bwd(o, pm, onw, dmix):
    t = pm.shape[0]

    def body(o_ref, z_ref, w_ref, dm_ref, do_ref, dz_ref, dw_ref):
        @pl.when(pl.program_id(0) == 0)
        def _():
            dw_ref[...] = jnp.zeros_like(dw_ref)

        z = z_ref[...]
        sg = _sigmoid(z)
        sz = z * sg
        dsz = sg * (1.0 + z * (1.0 - sg))
        dm = dm_ref[...]
        for hh in range(2):
            cols = slice(hh * DH, (hh + 1) * DH)
            ov = o_ref[hh]
            r = lax.rsqrt(jnp.mean(ov * ov, axis=-1, keepdims=True) + EPS)
            xn = ov * r
            dmh = dm[:, cols]
            dn = dmh * sz[:, cols]
            dz_ref[:, cols] = dmh * (xn * w_ref[...]) * dsz[:, cols]
            dw_ref[...] += jnp.sum(dn * xn, axis=0, keepdims=True)
            g = dn * w_ref[...]
            do_ref[hh] = r * (g - xn * jnp.mean(g * xn, axis=-1, keepdims=True))

    return pl.pallas_call(
        body, name="gdn_post_bwd", grid=(WIDTH // LANE,),
        in_specs=[pl.BlockSpec((2, t, DH), lambda j: (j, 0, 0)), pl.BlockSpec((t, LANE), lambda j: (0, Z_COL0 + j)),
                  pl.BlockSpec((1, DH), lambda j: (0, 0)), pl.BlockSpec((t, LANE), lambda j: (0, j))],
        out_specs=[pl.BlockSpec((2, t, DH), lambda j: (j, 0, 0)), pl.BlockSpec((t, LANE), lambda j: (0, j)),
                   pl.BlockSpec((1, DH), lambda j: (0, 0))],
        out_shape=[jax.ShapeDtypeStruct((HEADS, t, DH), F32), jax.ShapeDtypeStruct((t, WIDTH), F32),
                   jax.ShapeDtypeStruct((1, DH), F32)],
        compiler_params=_params(("arbitrary",)),
    )(o, pm, onw, dmix)


def _fox_post(o, pm):
    t = pm.shape[0]

    def body(o_ref, g_ref, m_ref):
        m_ref[...] = (jnp.concatenate([o_ref[0], o_ref[1]], axis=1) * _sigmoid(g_ref[...])).astype(m_ref.dtype)

    return pl.pallas_call(
        body, name="fox_post", grid=(WIDTH // LANE,),
        in_specs=[pl.BlockSpec((2, t, DH), lambda j: (j, 0, 0)), pl.BlockSpec((t, LANE), lambda j: (0, FGATE_COL0 + j))],
        out_specs=pl.BlockSpec((t, LANE), lambda j: (0, j)),
        out_shape=jax.ShapeDtypeStruct((t, WIDTH), MXU),
        compiler_params=_params(("arbitrary",)),
    )(o, pm)


def _fox_post_bwd(o, pm, dmix):
    t = pm.shape[0]

    def body(o_ref, g_ref, dm_ref, do_ref, dg_ref):
        sg = _sigmoid(g_ref[...])
        dm = dm_ref[...]
        for hh in range(2):
            cols = slice(hh * DH, (hh + 1) * DH)
            do_ref[hh] = dm[:, cols] * sg[:, cols]
            dg_ref[:, cols] = dm[:, cols] * o_ref[hh] * (sg * (1.0 - sg))[:, cols]

    return pl.pallas_call(
        body, name="fox_post_bwd", grid=(WIDTH // LANE,),
        in_specs=[pl.BlockSpec((2, t, DH), lambda j: (j, 0, 0)), pl.BlockSpec((t, LANE), lambda j: (0, FGATE_COL0 + j)),
                  pl.BlockSpec((t, LANE), lambda j: (0, j))],
        out_specs=[pl.BlockSpec((2, t, DH), lambda j: (j, 0, 0)), pl.BlockSpec((t, LANE), lambda j: (0, j))],
        out_shape=[jax.ShapeDtypeStruct((HEADS, t, DH), F32), jax.ShapeDtypeStruct((t, WIDTH), F32)],
        compiler_params=_params(("arbitrary",)),
    )(o, pm, dmix)


def _tail(x, mixg, mixf, tgt, wo, n2w, wg_t, wu_t, wd, fw):
    t, d = x.shape
    dff = wd.shape[0]
    tb = min(TB, t)

    def body(x_ref, mg_ref, mf_ref, t_ref, wo_ref, n2_ref, wg_ref, wu_ref, wd_ref, fw_ref,
             h2_ref, act_ref, dgate_ref, dup_ref, dx3_ref, dx2_ref, dmg_ref, dmf_ref, dn2_ref, dfw_ref, loss_ref):
        @pl.when(pl.program_id(0) == 0)
        def _():
            dn2_ref[...] = jnp.zeros_like(dn2_ref)
            dfw_ref[...] = jnp.zeros_like(dfw_ref)
            loss_ref[...] = jnp.zeros_like(loss_ref)

        x2 = x_ref[...] + _dot(mg_ref[...], wo_ref[0:WIDTH, :]) + _dot(mf_ref[...], wo_ref[WIDTH:2 * WIDTH, :])
        r2 = lax.rsqrt(jnp.mean(x2 * x2, axis=-1, keepdims=True) + EPS)
        xn2 = x2 * r2
        h2 = (xn2 * n2_ref[...]).astype(MXU)
        h2_ref[...] = h2
        gate = _dot_nt(h2, wg_ref[...])
        up = _dot_nt(h2, wu_ref[...])
        sg = _sigmoid(gate)
        sl = gate * sg
        act = (sl * up).astype(MXU)
        act_ref[...] = act
        x3 = x2 + _dot(act, wd_ref[...])
        r3 = lax.rsqrt(jnp.mean(x3 * x3, axis=-1, keepdims=True) + EPS)
        xn3 = x3 * r3
        err = xn3 * fw_ref[...] - t_ref[...]
        loss_ref[...] += 0.5 * jnp.sum(jnp.mean(err * err, axis=-1, keepdims=True), axis=0, keepdims=True)
        dy = err * (1.0 / d)
        dfw_ref[...] += jnp.sum(dy * xn3, axis=0, keepdims=True)
        g3 = dy * fw_ref[...]
        dx3 = r3 * (g3 - xn3 * jnp.mean(g3 * xn3, axis=-1, keepdims=True))
        dx3_ref[...] = dx3.astype(MXU)
        dact = _dot_nt(dx3, wd_ref[...])
        dgate = (dact * up * (sg * (1.0 + gate * (1.0 - sg)))).astype(MXU)
        dup = (dact * sl).astype(MXU)
        dgate_ref[...] = dgate
        dup_ref[...] = dup
        dh2 = _dot(dgate, wg_ref[...]) + _dot(dup, wu_ref[...])
        dn2_ref[...] += jnp.sum(dh2 * xn2, axis=0, keepdims=True)
        g2 = dh2 * n2_ref[...]
        dx2 = dx3 + r2 * (g2 - xn2 * jnp.mean(g2 * xn2, axis=-1, keepdims=True))
        dx2_ref[...] = dx2
        dmg_ref[...] = _dot_nt(dx2, wo_ref[0:WIDTH, :])
        dmf_ref[...] = _dot_nt(dx2, wo_ref[WIDTH:2 * WIDTH, :])

    def tok(n):
        return pl.BlockSpec((tb, n), lambda i: (i, 0))

    acc = pl.BlockSpec((1, d), lambda i: (0, 0))
    sds = jax.ShapeDtypeStruct
    return pl.pallas_call(
        body, name="tail", grid=(t // tb,),
        in_specs=[tok(d), tok(WIDTH), tok(WIDTH), tok(d), _resident(wo.shape), _resident((1, d)),
                  _resident(wg_t.shape), _resident(wu_t.shape), _resident(wd.shape), _resident((1, d))],
        out_specs=[tok(d), tok(dff), tok(dff), tok(dff), tok(d), tok(d), tok(WIDTH), tok(WIDTH), acc, acc,
                   pl.BlockSpec((1, 1), lambda i: (0, 0))],
        out_shape=[sds((t, d), MXU), sds((t, dff), MXU), sds((t, dff), MXU), sds((t, dff), MXU), sds((t, d), MXU),
                   sds((t, d), F32), sds((t, WIDTH), F32), sds((t, WIDTH), F32), sds((1, d), F32), sds((1, d), F32),
                   sds((1, 1), F32)],
        compiler_params=_params(("arbitrary",)),
    )(x, mixg, mixf, tgt, wo, n2w, wg_t, wu_t, wd, fw)


def _wgrads(a_list, b, name):
    t, n = b.shape
    ms = [a.shape[1] for a in a_list]
    bms = [256 if m % 256 == 0 else LANE for m in ms]
    nbs = [m // bm for m, bm in zip(ms, bms)]
    k = len(a_list)
    cast = b.dtype != jnp.dtype(MXU)

    def body(*refs):
        a_refs, b_ref, o_refs = refs[:k], refs[k], refs[k + 1:2 * k + 1]
        i = pl.program_id(0)
        if cast:
            @pl.when(i == 0)
            def _():
                refs[-1][...] = b_ref[...].astype(MXU)
        for a_ref, o_ref, nb in zip(a_refs, o_refs, nbs):
            @pl.when(i < nb)
            def _():
                o_ref[...] = _dot_tn(a_ref[...], refs[-1][...] if cast else b_ref[...]).astype(o_ref.dtype)

    def clamp(nb):
        return lambda i: jnp.minimum(i, nb - 1)

    return pl.pallas_call(
        body, name=name, grid=(max(nbs),),
        in_specs=[pl.BlockSpec((t, bm), lambda i, c=clamp(nb): (0, c(i))) for bm, nb in zip(bms, nbs)] + [_resident((t, n))],
        out_specs=[pl.BlockSpec((bm, n), lambda i, c=clamp(nb): (c(i), 0)) for bm, nb in zip(bms, nbs)],
        out_shape=[jax.ShapeDtypeStruct((m, n), WIRE) for m in ms],
        scratch_shapes=[pltpu.VMEM((t, n), MXU)] if cast else [],
        compiler_params=_params(("arbitrary",)),
    )(*a_list, b)


def _merge_dw_in(d_gdn, d_z, d_fox, d_fg, d_small):
    return jnp.concatenate([d_gdn, d_z, d_small[:2 * HEADS], d_fox, d_fg, d_small[2 * HEADS:3 * HEADS]], axis=0)


def _lanes(*pieces):
    v = jnp.concatenate([p.reshape(-1).astype(F32) for p in pieces])
    return jnp.pad(v, (0, LANE - v.shape[0])).reshape(1, LANE)


def _vector_params(p):
    d = p["norm1_w"].size
    gparams = jnp.concatenate([_lanes(jnp.zeros(HEADS), p["gdn_dt_bias"], p["fox_f_bias"]),
                               _lanes(jnp.zeros(HEADS), p["gdn_A_log"]), jnp.zeros((6, LANE), F32)])
    fox_nw = jnp.stack([jnp.tile(p["fox_q_norm_w"].reshape(-1), 2), jnp.tile(p["fox_k_norm_w"].reshape(-1), 2),
                        jnp.ones((LANE,), F32)])
    return dict(n1w=p["norm1_w"].reshape(1, d), n2w=p["norm2_w"].reshape(1, d), fw=p["final_norm_w"].reshape(1, d),
                onw=p["gdn_out_norm_w"].reshape(1, DH), gparams=gparams, fox_nw=fox_nw)


def _mixer_forward(x, vp, w_t, ws_t, conv_w, carry=None):
    h1, pm, ps = _inproj(x, vp["n1w"], w_t, ws_t)
    gates, run, tot, run_t = _gates(ps, vp["gparams"])
    fqkv = _fox_prep(pm, vp["fox_nw"])
    o_fox, lse, *carried = _fox_fwd(fqkv, run, run_t, carry)
    if carry:
        pm, o_fox = lax.optimization_barrier((pm, o_fox))
    mixf = _fox_post(o_fox, pm)
    gqkv = _gdn_prep(pm, conv_w)
    o_gdn, states, inv = _gdn_fwd(gqkv, gates, run, tot, run_t)
    mixg = _gdn_post(o_gdn, pm, vp["onw"])
    return dict(h1=h1, pm=pm, ps=ps, gates=gates, run=run, tot=tot, run_t=run_t, gqkv=gqkv, o_gdn=o_gdn, states=states,
                inv=inv, mixg=mixg, fqkv=fqkv, o_fox=o_fox, lse=lse, mixf=mixf, carried=carried[0] if carried else None)


def _mixer_backward(x, vp, w_t, ws_t, conv_w, f, dx2, dmixg, dmixf, carry=None, scatter_own=False):
    pm = f["pm"]
    do_fox, dfg = _fox_post_bwd(f["o_fox"], pm, dmixf)
    dfqkv, dcol_f, drow_f, *carried = _fox_bwd(f["fqkv"], f["run"], f["run_t"], f["o_fox"], f["lse"], do_fox, carry)
    dfox, dfnw = _fox_prep_bwd(pm, vp["fox_nw"], dfqkv)
    do_gdn, dz, donw = _gdn_post_bwd(f["o_gdn"], pm, vp["onw"], dmixg)
    dgqkv, dcol_g, dtot_g, drow_g = _gdn_bwd(f["gqkv"], f["gates"], f["run"], f["tot"], f["run_t"], f["inv"], f["states"], do_gdn)
    dgdn, dconv = _gdn_prep_bwd(pm, conv_w, dgqkv)
    dps, gsum = _gates_bwd(f["ps"], vp["gparams"], dcol_g, dtot_g, dcol_f, drow_g, drow_f)
    dw_in = _merge_dw_in(*_wgrads([dgdn, dz, dfox, dfg, dps], f["h1"], "dw_in"))
    own = ("scatter", [_cut(dw_in, 0), _cut(dconv, 1)]) if scatter_own else None
    grad_x, dn1w, *sent = _inproj_bwd(x, vp["n1w"], dx2, dgdn, dz, dfox, dfg, dps, w_t, ws_t, own)
    small = dict(dn1w=dn1w, gsum=gsum, donw=donw, dfnw=dfnw)
    return (grad_x, dw_in, dconv, small, *carried, *sent)


VECTORS = ("norm1_w", "norm2_w", "final_norm_w", "gdn_A_log", "gdn_dt_bias", "gdn_out_norm_w", "fox_f_bias",
           "fox_q_norm_w", "fox_k_norm_w")
VEC_ROWS = 16
LOSS_ROW = len(VECTORS)


def _pack_vectors(dn1w, dn2w, dfw, gsum, donw, dfnw, loss):
    d = dn1w.shape[1]

    def body(n1_ref, n2_ref, fw_ref, gs_ref, on_ref, fn_ref, loss_ref, o_ref):
        o_ref[...] = jnp.zeros_like(o_ref)
        o_ref[0:1, :] = n1_ref[...]
        o_ref[1:2, :] = n2_ref[...]
        o_ref[2:3, :] = fw_ref[...]
        o_ref[3:4, 0:HEADS] = gs_ref[0:1, 0:HEADS]
        o_ref[4:5, 0:HEADS] = gs_ref[1:2, 0:HEADS]
        o_ref[5:6, 0:DH] = on_ref[...]
        o_ref[6:7, 0:HEADS] = gs_ref[2:3, 0:HEADS]
        for kind in range(2):
            v = fn_ref[kind, 0]
            for j in range(1, fn_ref.shape[1]):
                v = v + fn_ref[kind, j]
            o_ref[7 + kind:8 + kind, 0:DH] = v[:, :DH] + v[:, DH:]
        o_ref[LOSS_ROW:LOSS_ROW + 1, 0:1] = loss_ref[...]

    return pl.pallas_call(body, name="pack_vectors", out_shape=jax.ShapeDtypeStruct((VEC_ROWS, d), F32),
                          compiler_params=_params())(dn1w, dn2w, dfw, gsum, donw, dfnw, loss)


def _late_grads(f, h2, act, dgate, dup, dx3, dx2):
    dw_gate, dw_up = _wgrads([dgate, dup], h2, "dw_gate_up")
    (dw_down,) = _wgrads([act], dx3, "dw_down")
    return {"w_out": jnp.concatenate(_wgrads([f["mixg"], f["mixf"]], dx2, "dw_out"), axis=0),
            "w_ffn_gate": dw_gate, "w_ffn_up": dw_up, "w_ffn_down": dw_down}


def _local_step(x, tgt, p, w_in_t, conv_w, wo, wg_t, wu_t, wd):
    vp = _vector_params(p)
    ws_t = _small_rows(w_in_t)
    f = _mixer_forward(x, vp, w_in_t, ws_t, conv_w)
    (h2, act, dgate, dup, dx3, dx2, dmixg, dmixf, dn2w, dfw, loss) = _tail(
        x, f["mixg"], f["mixf"], tgt, wo, vp["n2w"], wg_t, wu_t, wd, vp["fw"])
    grad_x, dw_in, dconv, small = _mixer_backward(x, vp, w_in_t, ws_t, conv_w, f, dx2, dmixg, dmixf)
    grads = {"w_in": dw_in, "gdn_conv_w": dconv, **_late_grads(f, h2, act, dgate, dup, dx3, dx2)}
    vec = _pack_vectors(small["dn1w"], dn2w, dfw, small["gsum"], small["donw"], small["dfnw"], loss)
    return grad_x, grads, vec


def _my_place():
    return lax.axis_index("x"), lax.axis_index("y"), lax.axis_index("c")


def _peers():
    x, y, c = _my_place()
    peers = []
    for k in range(1, N_DEV):
        px = 1 - x if k & 4 else x
        py = 1 - y if k & 2 else y
        pc = 1 - c if k & 1 else c
        peers.append(((px, py, pc), 4 * px + 2 * py + pc))
    return 4 * x + 2 * y + c, peers


def _spread_copies(kind, srcs, lands, send_sems, recv_sems):
    me, peers = _peers()
    kinds = [kind] * len(srcs) if isinstance(kind, str) else kind
    remote, local = [], []
    for i, (kd, src, land) in enumerate(zip(kinds, srcs, lands)):
        for k, (dev, idx) in enumerate(peers):
            remote.append(pltpu.make_async_remote_copy(
                src_ref=src if kd == "gather" else src.at[idx], dst_ref=land.at[me],
                send_sem=send_sems.at[i * (N_DEV - 1) + k], recv_sem=recv_sems.at[i * (N_DEV - 1) + k],
                device_id=dev, device_id_type=MESH))
        local.append((src if kd == "gather" else src.at[me], land.at[me]))
    return remote, local


def _gather_two_level(arrays, name):
    n = len(arrays)

    def body(*refs):
        srcs, lands = refs[:n], refs[n:2 * n]
        send_sems, recv_sems, local_sems = refs[2 * n:]
        x, y, c = _my_place()
        me, sibling = (x, y, c), (x, y, 1 - c)
        chips = [(1 - x, y), (x, 1 - y), (1 - x, 1 - y)]

        def index(p):
            return 4 * p[0] + 2 * p[1] + p[2]

        def copy(i, k, block, to, src=None):
            blk = lands[i].at[index(block)]
            return pltpu.make_async_remote_copy(
                src_ref=blk if src is None else src, dst_ref=blk, send_sem=send_sems.at[7 * i + k],
                recv_sem=recv_sems.at[7 * i + k], device_id=to, device_id_type=MESH)

        mine = [pltpu.make_async_copy(srcs[i], lands[i].at[index(me)], local_sems.at[i]) for i in range(n)]
        for cp in mine:
            cp.start()
        first = []
        for i in range(n):
            first.append(copy(i, 0, me, sibling, src=srcs[i]))
            first += [copy(i, 1 + j, me, (*chip, c), src=srcs[i]) for j, chip in enumerate(chips)]
        for cp in first:
            cp.start()
        passed = []
        for i in range(n):
            for j, chip in enumerate(chips):
                copy(i, 1 + j, (*chip, c), me).wait_recv()
                passed.append(copy(i, 4 + j, (*chip, c), sibling))
                passed[-1].start()
        for i in range(n):
            copy(i, 0, sibling, me).wait_recv()
            for j, chip in enumerate(chips):
                copy(i, 4 + j, (*chip, 1 - c), me).wait_recv()
        for cp in first + passed:
            cp.wait_send()
        for cp in mine:
            cp.wait()

    return pl.pallas_call(
        body, name=name,
        out_shape=[jax.ShapeDtypeStruct((N_DEV,) + a.shape, a.dtype) for a in arrays],
        in_specs=[pl.BlockSpec(memory_space=pl.ANY)] * n, out_specs=[pl.BlockSpec(memory_space=pl.ANY)] * n,
        scratch_shapes=[pltpu.SemaphoreType.DMA((7 * n,)), pltpu.SemaphoreType.DMA((7 * n,)),
                        pltpu.SemaphoreType.DMA((n,))],
    )(*arrays)


def _land_shape(kind, a):
    return (N_DEV,) + a.shape if kind == "gather" else a.shape


HBM = pl.BlockSpec(memory_space=pltpu.HBM)
SEM = pl.BlockSpec(memory_space=pltpu.SEMAPHORE)


def _hbm(a):
    return pltpu.with_memory_space_constraint(a, pltpu.HBM)


def _carry_operands(kind, arrays):
    n = len(arrays)
    kinds = [kind] * n if isinstance(kind, str) else kind
    lands = [lax.empty(_land_shape(kd, a), a.dtype) for kd, a in zip(kinds, arrays)]
    sems = [pltpu.SemaphoreType.DMA((n * (N_DEV - 1),))] * 2
    return ([_hbm(a) for a in list(arrays) + lands], [HBM] * (2 * n),
            sems + [pltpu.HBM(a.shape, a.dtype) for a in list(arrays) + lands], [SEM] * 2 + [HBM] * (2 * n),
            pltpu.SemaphoreType.DMA((n,)))


def _carry_step(kind, in_refs, send_sems, recv_sems, local_sems, step, n_steps):
    n = len(in_refs) // 2
    remote, local = _spread_copies(kind, in_refs[:n], in_refs[n:], send_sems, recv_sems)
    copies = [pltpu.make_async_copy(s, d, local_sems.at[i]) for i, (s, d) in enumerate(local)]
    per_step = -(-len(remote) // n_steps)
    for s in range(-(-len(remote) // per_step)):
        @pl.when(step == s)
        def _():
            for cp in remote[s * per_step:(s + 1) * per_step]:
                cp.start()
            if s == 0:
                for cp in copies:
                    cp.start()

    @pl.when(step == n_steps - 1)
    def _():
        for cp in copies:
            cp.wait()


def _carry_state(extra_out):
    n = (len(extra_out) - 2) // 2
    return list(extra_out[2:2 + n]), list(extra_out[2 + n:]), extra_out[0], extra_out[1]


def _spread_start(arrays, kind, name):
    n = len(arrays)

    def body(*refs):
        srcs, lands = refs[:n], refs[n:2 * n]
        send_sems, recv_sems = refs[2 * n], refs[2 * n + 1]
        token = refs[4 * n + 2]
        local_sems = refs[4 * n + 3]
        remote, local = _spread_copies(kind, srcs, lands, send_sems, recv_sems)
        for cp in remote:
            cp.start()
        copies = [pltpu.make_async_copy(s, d, local_sems.at[i]) for i, (s, d) in enumerate(local)]
        for cp in copies:
            cp.start()
        for cp in copies:
            cp.wait()
        token[...] = jnp.zeros_like(token)

    sems = (pltpu.SemaphoreType.DMA((n * (N_DEV - 1),)),) * 2
    kinds = [kind] * n if isinstance(kind, str) else kind
    lands = [lax.empty(_land_shape(kd, a), a.dtype) for kd, a in zip(kinds, arrays)]
    out = pl.pallas_call(
        body, name=name,
        out_shape=sems + tuple(pltpu.HBM(a.shape, a.dtype) for a in list(arrays) + lands)
        + (jax.ShapeDtypeStruct((8, LANE), F32),),
        in_specs=[HBM] * (2 * n), out_specs=tuple([SEM] * 2 + [HBM] * (2 * n) + [pl.BlockSpec(memory_space=pltpu.VMEM)]),
        input_output_aliases={j: 2 + j for j in range(2 * n)},
        scratch_shapes=[pltpu.SemaphoreType.DMA((n,))],
        compiler_params=pltpu.CompilerParams(has_side_effects=pltpu.SideEffectType.DATAFLOW_SIDE_EFFECTING),
    )(*[_hbm(a) for a in arrays], *[_hbm(a) for a in lands])
    return (list(out[2:2 + n]), list(out[2 + n:2 + 2 * n]), out[0], out[1]), out[-1]


def _spread_wait(state, kind, after, name):
    srcs, lands, send_sems, recv_sems = state
    n = len(srcs)
    after = list(after) if isinstance(after, (list, tuple)) else [after]

    def body(*refs):
        remote, _ = _spread_copies(kind, refs[:n], refs[n:2 * n], refs[2 * n], refs[2 * n + 1])
        for cp in remote:
            cp.wait_send()
        for cp in remote:
            cp.wait_recv()

    out = pl.pallas_call(
        body, name=name,
        out_shape=tuple(pltpu.HBM(a.shape, a.dtype) for a in srcs + lands),
        in_specs=[HBM] * (2 * n) + [SEM, SEM] + [pl.BlockSpec(memory_space=pl.ANY)] * len(after),
        out_specs=tuple([HBM] * (2 * n)),
        input_output_aliases={j: j for j in range(2 * n)},
        compiler_params=pltpu.CompilerParams(has_side_effects=pltpu.SideEffectType.DATAFLOW_SIDE_EFFECTING),
    )(*srcs, *lands, send_sems, recv_sems, *after)
    return list(out[n:])


ADAM_ROWS = 128
ADAM_COLS = 256


def _adam_math(g, w, m, v):
    nm = ADAM_B1 * m + (1.0 - ADAM_B1) * g
    nv = ADAM_B2 * v + (1.0 - ADAM_B2) * (g * g)
    m_hat = nm / (1.0 - ADAM_B1 ** ADAM_STEP)
    v_hat = nv / (1.0 - ADAM_B2 ** ADAM_STEP)
    return -ADAM_LR * (m_hat / (jnp.sqrt(v_hat) + ADAM_EPS) + ADAM_WD * w), nm, nv


def _sum_parts(p_ref):
    g = p_ref[0].astype(F32)
    for s in range(1, N_DEV):
        g = g + p_ref[s].astype(F32)
    return g


def _adam_matrix(parts, w, m, v, name):
    _, r, c = w.shape
    rb = ADAM_ROWS if r % ADAM_ROWS == 0 else r
    cb = ADAM_COLS if (rb == r and c % ADAM_COLS == 0) else c

    def body(p_ref, w_ref, m_ref, v_ref, g_ref, d_ref, nm_ref, nv_ref):
        g = _sum_parts(p_ref)
        g_ref[0] = g
        d_ref[0], nm_ref[0], nv_ref[0] = _adam_math(g, w_ref[0], m_ref[0], v_ref[0])

    blk = pl.BlockSpec((1, rb, cb), lambda i, j: (0, i, j))
    return pl.pallas_call(
        body, name=name, grid=(r // rb, c // cb),
        in_specs=[pl.BlockSpec((N_DEV, rb, cb), lambda i, j: (0, i, j)), blk, blk, blk],
        out_specs=[blk] * 4, out_shape=[jax.ShapeDtypeStruct(w.shape, F32)] * 4,
        compiler_params=_params(("arbitrary", "arbitrary")),
    )(parts, w, m, v)


def _adam_vectors(parts, ws, ms, vs):
    nv = len(ws)

    def body(*refs):
        p_ref = refs[0]
        w_refs, m_refs, v_refs = refs[1:1 + nv], refs[1 + nv:1 + 2 * nv], refs[1 + 2 * nv:1 + 3 * nv]
        outs = refs[1 + 3 * nv:]
        g_all = _sum_parts(p_ref)
        for i in range(nv):
            n = w_refs[i].shape[1]
            g = g_all[i:i + 1, 0:n]
            d, nm, nvv = _adam_math(g, w_refs[i][...], m_refs[i][...], v_refs[i][...])
            outs[i][...] = g
            outs[nv + i][...] = d
            outs[2 * nv + i][...] = nm
            outs[3 * nv + i][...] = nvv
        outs[4 * nv][...] = g_all[LOSS_ROW:LOSS_ROW + 1, 0:1]

    shapes = [jax.ShapeDtypeStruct(a.shape, F32) for a in ws]
    out = pl.pallas_call(body, name="adam_vectors", out_shape=shapes * 4 + [jax.ShapeDtypeStruct((1, 1), F32)],
                         compiler_params=_params())(parts, *ws, *ms, *vs)
    return out[:nv], out[nv:2 * nv], out[2 * nv:3 * nv], out[3 * nv:4 * nv], out[4 * nv]


MATRICES = (("w_in", 1), ("gdn_conv_w", 1), ("w_out", 0), ("w_ffn_gate", 1), ("w_ffn_up", 1), ("w_ffn_down", 0))
TRANSPOSED = ("w_in", "w_ffn_gate", "w_ffn_up")
WEIGHTS = ("norm1_w", "w_in", "gdn_conv_w", "gdn_A_log", "gdn_dt_bias", "gdn_out_norm_w", "fox_f_bias", "fox_q_norm_w",
           "fox_k_norm_w", "w_out", "norm2_w", "w_ffn_gate", "w_ffn_up", "w_ffn_down", "final_norm_w")


def _join(blocks, axis):
    _, r, c = blocks.shape
    if axis == 0:
        return blocks.reshape(N_DEV * r, c)
    return blocks.transpose(1, 0, 2).reshape(r, N_DEV * c)


def _cut(full, axis):
    r, c = full.shape
    if axis == 0:
        return full.reshape(N_DEV, r // N_DEV, c)
    return full.reshape(r, N_DEV, c // N_DEV).transpose(1, 0, 2)


def kernel(x, norm1_w, w_in, gdn_conv_w, gdn_A_log, gdn_dt_bias, gdn_out_norm_w, fox_f_bias, fox_q_norm_w, fox_k_norm_w, w_out, norm2_w, w_ffn_gate, w_ffn_up, w_ffn_down, final_norm_w, loss_target, m_norm1_w, m_w_in, m_gdn_conv_w, m_gdn_A_log, m_gdn_dt_bias, m_gdn_out_norm_w, m_fox_f_bias, m_fox_q_norm_w, m_fox_k_norm_w, m_w_out, m_norm2_w, m_w_ffn_gate, m_w_ffn_up, m_w_ffn_down, m_final_norm_w, v_norm1_w, v_w_in, v_gdn_conv_w, v_gdn_A_log, v_gdn_dt_bias, v_gdn_out_norm_w, v_fox_f_bias, v_fox_q_norm_w, v_fox_k_norm_w, v_w_out, v_norm2_w, v_w_ffn_gate, v_w_ffn_up, v_w_ffn_down, v_final_norm_w):
    w = dict(norm1_w=norm1_w, w_in=w_in, gdn_conv_w=gdn_conv_w, gdn_A_log=gdn_A_log, gdn_dt_bias=gdn_dt_bias,
             gdn_out_norm_w=gdn_out_norm_w, fox_f_bias=fox_f_bias, fox_q_norm_w=fox_q_norm_w, fox_k_norm_w=fox_k_norm_w,
             w_out=w_out, norm2_w=norm2_w, w_ffn_gate=w_ffn_gate, w_ffn_up=w_ffn_up, w_ffn_down=w_ffn_down,
             final_norm_w=final_norm_w)
    m = dict(norm1_w=m_norm1_w, w_in=m_w_in, gdn_conv_w=m_gdn_conv_w, gdn_A_log=m_gdn_A_log, gdn_dt_bias=m_gdn_dt_bias,
             gdn_out_norm_w=m_gdn_out_norm_w, fox_f_bias=m_fox_f_bias, fox_q_norm_w=m_fox_q_norm_w,
             fox_k_norm_w=m_fox_k_norm_w, w_out=m_w_out, norm2_w=m_norm2_w, w_ffn_gate=m_w_ffn_gate,
             w_ffn_up=m_w_ffn_up, w_ffn_down=m_w_ffn_down, final_norm_w=m_final_norm_w)
    v = dict(norm1_w=v_norm1_w, w_in=v_w_in, gdn_conv_w=v_gdn_conv_w, gdn_A_log=v_gdn_A_log, gdn_dt_bias=v_gdn_dt_bias,
             gdn_out_norm_w=v_gdn_out_norm_w, fox_f_bias=v_fox_f_bias, fox_q_norm_w=v_fox_q_norm_w,
             fox_k_norm_w=v_fox_k_norm_w, w_out=v_w_out, norm2_w=v_norm2_w, w_ffn_gate=v_w_ffn_gate,
             w_ffn_up=v_w_ffn_up, w_ffn_down=v_w_ffn_down, final_norm_w=v_final_norm_w)
    late = ("w_out", "w_ffn_gate", "w_ffn_up", "w_ffn_down")
    xs, tgt = x[0], loss_target[0]
    vp = _vector_params({n: w[n] for n in VECTORS})

    def rows_of(d, n):
        return d[n].transpose(0, 2, 1) if n in TRANSPOSED else d[n]

    wr, mr, vr = ({n: rows_of(d, n) for n, _ in MATRICES} for d in (w, m, v))

    w_in_blocks, conv_blocks = _gather_two_level([wr["w_in"][0].astype(WIRE), w["gdn_conv_w"][0]], "gather_in")
    w_t = _join(w_in_blocks, 0)
    conv_w = _join(conv_blocks, 1)
    f = _mixer_forward(xs, vp, w_t, _small_rows(w_t), conv_w, ("gather", [wr[n][0].astype(WIRE) for n in late]))
    full = {n: _join(b, 0) for n, b in zip(late, _spread_wait(f["carried"], "gather", f["mixg"], "gather_late_wait"))}

    (h2, act, dgate, dup, dx3, dx2, dmixg, dmixf, dn2w, dfw, loss) = _tail(
        xs, f["mixg"], f["mixf"], tgt, full["w_out"], vp["n2w"], full["w_ffn_gate"], full["w_ffn_up"],
        full["w_ffn_down"], vp["fw"])
    dlate = _late_grads(f, h2, act, dgate, dup, dx3, dx2)

    grad_x, dw_in, dconv, small, state_grads, state_own = _mixer_backward(
        xs, vp, w_t, _small_rows(w_t), conv_w, f, dx2, dmixg, dmixf, ("scatter", [_cut(dlate[n], 0) for n in late]),
        scatter_own=True)
    vec = _pack_vectors(small["dn1w"], dn2w, dfw, small["gsum"], small["donw"], small["dfnw"], loss)

    state_vec, token = _spread_start([vec], "gather", "vectors_start")
    parts = dict(zip(late, _spread_wait(state_grads, "scatter", token, "grads_late_wait")))
    results = [{}, {}, {}, {}]

    updated = {}

    def update(n):
        out = _adam_matrix(parts[n], wr[n], mr[n], vr[n], "adam_" + n)
        updated[n] = out[0]
        for d, a in zip(results, out):
            d[n] = a.transpose(0, 2, 1) if n in TRANSPOSED else a

    for n in late:
        update(n)
    parts["w_in"], parts["gdn_conv_w"] = _spread_wait(state_own, "scatter", [updated[n] for n in late], "own_wait")
    (parts_vec,) = _spread_wait(state_vec, "gather", parts["w_in"], "vectors_wait")
    update("w_in")
    update("gdn_conv_w")
    row = lambda a: a.reshape(1, -1)
    *vec_out, total_loss = _adam_vectors(parts_vec, [row(w[n]) for n in VECTORS], [row(m[n]) for n in VECTORS],
                                         [row(v[n]) for n in VECTORS])
    for d, arrs in zip(results, vec_out):
        for n, a in zip(VECTORS, arrs):
            d[n] = a.reshape(w[n].shape)
    return (total_loss[0, 0], grad_x[None], *[d[n] for d in results for n in WEIGHTS])
```

```python
import functools

import jax
import jax.numpy as jnp
from jax import lax
from jax.experimental import pallas as pl
from jax.experimental.pallas import tpu as pltpu

F32 = jnp.float32
MXU = jnp.bfloat16
WIRE = jnp.bfloat16
HI = lax.Precision.HIGHEST
EPS = 1e-6

N_DEV = 8
HEADS = 8
DH = 64
WIDTH = HEADS * DH
CHUNK = 64
LANE = 128
ROW_ALIGN = 16
TB = 256
QB = 256
VMEM_LIMIT = 60 * 1024 * 1024

ADAM_LR = 0.001
ADAM_B1 = 0.9
ADAM_B2 = 0.999
ADAM_EPS = 1e-08
ADAM_WD = 0.01
ADAM_STEP = 10

MESH = pl.DeviceIdType.MESH


def _params(sem=None):
    return pltpu.CompilerParams(dimension_semantics=sem, vmem_limit_bytes=VMEM_LIMIT)


def _resident(shape):
    n = len(shape)
    return pl.BlockSpec(shape, lambda *_: (0,) * n, pipeline_mode=pl.Buffered(1))


def _dot(a, b):
    return jnp.dot(a.astype(MXU), b.astype(MXU), preferred_element_type=F32)


def _dot_nt(a, b):
    return lax.dot_general(a.astype(MXU), b.astype(MXU), (((1,), (1,)), ((), ())), preferred_element_type=F32)


def _dot_tn(a, b):
    return lax.dot_general(a.astype(MXU), b.astype(MXU), (((0,), (0,)), ((), ())), preferred_element_type=F32)


def _hdot(a, b):
    return jnp.dot(a, b, precision=HI, preferred_element_type=F32)


def _hdot_nt(a, b):
    return lax.dot_general(a, b, (((1,), (1,)), ((), ())), precision=HI, preferred_element_type=F32)


def _hdot_tn(a, b):
    return lax.dot_general(a, b, (((0,), (0,)), ((), ())), precision=HI, preferred_element_type=F32)


def _sigmoid(x):
    return 0.5 * jnp.tanh(0.5 * x) + 0.5


def _softplus(x):
    return jnp.maximum(x, 0.0) + jnp.log(1.0 + jnp.exp(-jnp.abs(x)))


def _head_sum_matrix():
    ri = lax.broadcasted_iota(jnp.int32, (LANE, LANE), 0) // DH
    ci = lax.broadcasted_iota(jnp.int32, (LANE, LANE), 1) // DH
    return (ri == ci).astype(F32)


def _group_sum(a, ones_matrix):
    hi = a.astype(jnp.bfloat16)
    lo = (a - hi.astype(F32)).astype(jnp.bfloat16)
    m = ones_matrix.astype(jnp.bfloat16)
    return jnp.dot(hi, m, preferred_element_type=F32) + jnp.dot(lo, m, preferred_element_type=F32)


def _shift_down(x, s):
    return pltpu.roll(x, s, 0)


def _shift_up(x, s):
    return pltpu.roll(x, x.shape[0] - s, 0)


ROWS_A = 4 * WIDTH
ROWS_B = ROWS_A + 2 * HEADS
ROWS_C = ROWS_B + 4 * WIDTH


def _small_rows(w_t):
    return jnp.concatenate([w_t[ROWS_A:ROWS_B], w_t[ROWS_C:], jnp.zeros((LANE - 3 * HEADS, w_t.shape[1]), w_t.dtype)])


def _inproj(x, n1w, w_t, ws_t):
    t, d = x.shape
    tb = min(TB, t)

    def body(x_ref, nw_ref, wt_ref, ws_ref, h_ref, pm_ref, ps_ref):
        xv = x_ref[...]
        r = lax.rsqrt(jnp.mean(xv * xv, axis=-1, keepdims=True) + EPS)
        h = (xv * r * nw_ref[...]).astype(MXU)
        h_ref[...] = h
        pm_ref[:, 0:ROWS_A] = _dot_nt(h, wt_ref[0:ROWS_A, :])
        pm_ref[:, ROWS_A:2 * ROWS_A] = _dot_nt(h, wt_ref[ROWS_B:ROWS_C, :])
        ps_ref[...] = _dot_nt(h, ws_ref[...])

    return pl.pallas_call(
        body, name="inproj", grid=(t // tb,),
        in_specs=[pl.BlockSpec((tb, d), lambda i: (i, 0)), _resident((1, d)), _resident(w_t.shape), _resident((LANE, d))],
        out_specs=[pl.BlockSpec((tb, d), lambda i: (i, 0)), pl.BlockSpec((tb, 2 * ROWS_A), lambda i: (i, 0)),
                   pl.BlockSpec((tb, LANE), lambda i: (i, 0))],
        out_shape=[jax.ShapeDtypeStruct((t, d), MXU), jax.ShapeDtypeStruct((t, 2 * ROWS_A), F32),
                   jax.ShapeDtypeStruct((t, LANE), F32)],
        compiler_params=_params(("arbitrary",)),
    )(x, n1w, w_t, ws_t)


def _inproj_bwd(x, n1w, dx2, dgdn, dz, dfox, dfg, dps, w_t, ws_t, carry=None):
    t, d = x.shape
    tb = min(TB, t)
    w3 = 3 * WIDTH
    c_in, c_in_specs, c_out_shape, c_out_specs, c_sem = _carry_operands(*carry) if carry else ([], [], [], [], None)
    n_c = len(c_in)

    def body(*refs):
        x_ref, nw_ref, dx2_ref, dgdn_ref, dz_ref, dfox_ref, dfg_ref, dps_ref, wm_ref, ws_ref = refs[:10]
        gx_ref, dnw_ref = refs[10 + n_c:12 + n_c]
        if carry:
            _carry_step(carry[0], refs[10:10 + n_c], refs[12 + n_c], refs[13 + n_c], refs[-1], pl.program_id(0), t // tb)
        dh = _dot(dgdn_ref[...], wm_ref[0:w3, :])
        dh += _dot(dz_ref[...], wm_ref[w3:ROWS_A, :])
        dh += _dot(dfox_ref[...], wm_ref[ROWS_B:ROWS_B + w3, :])
        dh += _dot(dfg_ref[...], wm_ref[ROWS_B + w3:ROWS_C, :])
        dh += _dot(dps_ref[...], ws_ref[...])
        xv = x_ref[...]
        r = lax.rsqrt(jnp.mean(xv * xv, axis=-1, keepdims=True) + EPS)
        xn = xv * r

        @pl.when(pl.program_id(0) == 0)
        def _():
            dnw_ref[...] = jnp.zeros_like(dnw_ref)

        dnw_ref[...] += jnp.sum(dh * xn, axis=0, keepdims=True)
        g = dh * nw_ref[...]
        gx_ref[...] = dx2_ref[...] + r * (g - xn * jnp.mean(g * xn, axis=-1, keepdims=True))

    def tok(n):
        return pl.BlockSpec((tb, n), lambda i: (i, 0))

    out = pl.pallas_call(
        body, name="inproj_bwd", grid=(t // tb,),
        in_specs=[tok(d), _resident((1, d)), tok(d), tok(w3), tok(WIDTH), tok(w3), tok(WIDTH), tok(LANE),
                  _resident(w_t.shape), _resident(ws_t.shape)] + c_in_specs,
        out_specs=[tok(d), pl.BlockSpec((1, d), lambda i: (0, 0))] + c_out_specs,
        out_shape=[jax.ShapeDtypeStruct((t, d), F32), jax.ShapeDtypeStruct((1, d), F32)] + c_out_shape,
        input_output_aliases={10 + j: 4 + j for j in range(n_c)},
        scratch_shapes=[c_sem] if carry else [],
        compiler_params=_params(("arbitrary",)),
    )(x, n1w, dx2, dgdn, dz, dfox, dfg, dps, w_t, ws_t, *c_in)
    return (out[0], out[1], _carry_state(out[2:])) if carry else tuple(out)


def _gate_lanes(shape):
    lane = lax.broadcasted_iota(jnp.int32, shape, 1)
    return lane < HEADS, (lane >= HEADS) & (lane < 2 * HEADS), (lane >= 2 * HEADS) & (lane < 3 * HEADS)


def _block_masks():
    ri = lax.broadcasted_iota(jnp.int32, (LANE, LANE), 0)
    ci = lax.broadcasted_iota(jnp.int32, (LANE, LANE), 1)
    same = (ri // CHUNK) == (ci // CHUNK)
    return ((ri >= ci).astype(F32), (ri <= ci).astype(F32), (same & (ri >= ci)).astype(F32),
            (same & (ri <= ci)).astype(F32), same.astype(F32))


def _gates(ps, gparams):
    t = ps.shape[0]
    nb = t // LANE

    def body(ps_ref, gp_ref, out_ref, run_ref, tot_ref, runt_ref):
        p = ps_ref[...]
        is_b, is_a, is_f = _gate_lanes(p.shape)
        z = p + gp_ref[0:1, :]
        neg_exp_a = -jnp.exp(gp_ref[1:2, :])
        glog = neg_exp_a * _softplus(z)
        logf = -_softplus(-z)
        out_ref[...] = jnp.where(is_b, _sigmoid(p), jnp.where(is_a, glog, jnp.where(is_f, logf, 0.0)))
        tril, _, tril_c, _, same_c = _block_masks()
        off = jnp.zeros((1, LANE), F32)
        for b in range(nb):
            rows = slice(b * LANE, (b + 1) * LANE)
            blk = out_ref[rows, :]
            ga = jnp.where(is_a[:LANE], blk, 0.0)
            fb = _hdot(tril, jnp.where(is_f[:LANE], blk, 0.0)) + off
            run = fb + _hdot(tril_c, ga)
            run_ref[rows, :] = run
            runt_ref[:, rows] = run.T
            tot_ref[rows, :] = _hdot(same_c, ga)
            off = fb[LANE - 1:LANE, :]

    return pl.pallas_call(
        body, name="gates",
        out_shape=[jax.ShapeDtypeStruct((t, LANE), F32)] * 3 + [jax.ShapeDtypeStruct((LANE, t), F32)],
        compiler_params=_params(),
    )(ps, gparams)


def _gates_bwd(ps, gparams, dcol_g, dtot_g, dcol_f, drow_g, drow_f):
    t = ps.shape[0]
    nb = t // LANE

    def body(ps_ref, gp_ref, dcg_ref, dtg_ref, dcf_ref, drg_ref, drf_ref, dps_ref, sums_ref, dl_ref, d0_ref, tr_ref):
        p = ps_ref[...]
        is_b, is_a, is_f = _gate_lanes(p.shape)
        _, triu, _, triu_c, same_c = _block_masks()
        tr_ref[...] = jnp.zeros_like(tr_ref)
        off = jnp.zeros((1, LANE), F32)
        for b in reversed(range(nb)):
            rows = slice(b * LANE, (b + 1) * LANE)
            tr_ref[HEADS:2 * HEADS, :] = drg_ref[:, rows]
            tr_ref[2 * HEADS:3 * HEADS, :] = drf_ref[:, rows]
            d = dcg_ref[rows, :] + dcf_ref[rows, :] + tr_ref[...].T
            d0_ref[rows, :] = d
            dlf = _hdot(triu, jnp.where(is_f[:LANE], d, 0.0)) + off
            dla = (_hdot(triu_c, jnp.where(is_a[:LANE], d, 0.0))
                   + _hdot(same_c, jnp.where(is_a[:LANE], dtg_ref[rows, :], 0.0)))
            dl_ref[rows, :] = dlf + dla
            off = dlf[0:1, :]
        z = p + gp_ref[0:1, :]
        neg_exp_a = -jnp.exp(gp_ref[1:2, :])
        sb = _sigmoid(p)
        glog = neg_exp_a * _softplus(z)
        dl = dl_ref[...]
        dp = jnp.where(is_b, d0_ref[...] * sb * (1.0 - sb),
                       jnp.where(is_a, dl * neg_exp_a * _sigmoid(z), jnp.where(is_f, dl * _sigmoid(-z), 0.0)))
        dps_ref[...] = dp
        s_a = jnp.sum(jnp.where(is_a, dl * glog, 0.0), axis=0, keepdims=True)
        s_p = jnp.sum(jnp.where(is_b, 0.0, dp), axis=0, keepdims=True)
        row = lax.broadcasted_iota(jnp.int32, (8, LANE), 0)
        from_a = pltpu.roll(jnp.where(row == 0, s_a, jnp.where(row == 1, s_p, 0.0)), LANE - HEADS, 1)
        from_f = pltpu.roll(jnp.where(row == 2, s_p, 0.0), LANE - 2 * HEADS, 1)
        lane = lax.broadcasted_iota(jnp.int32, (8, LANE), 1)
        sums_ref[...] = jnp.where(lane < HEADS, from_a + from_f, 0.0)

    return pl.pallas_call(
        body, name="gates_bwd",
        out_shape=[jax.ShapeDtypeStruct((t, LANE), F32), jax.ShapeDtypeStruct((8, LANE), F32)],
        scratch_shapes=[pltpu.VMEM((t, LANE), F32), pltpu.VMEM((t, LANE), F32), pltpu.VMEM((LANE, LANE), F32)],
        compiler_params=_params(),
    )(ps, gparams, dcol_g, dtot_g, dcol_f, drow_g, drow_f)


def _conv(xv, w):
    acc = w[3:4, :] * xv
    for s in range(1, 4):
        acc += w[3 - s:4 - s, :] * _shift_down(xv, s)
    return acc


PREP_ROWS = 256
HALO = 8
PREP_UNROLL = 2


def _tile_loop(n, tile, init):
    if n % PREP_UNROLL:
        return lax.fori_loop(0, n, tile, init)

    def trip(g, carry):
        for u in range(PREP_UNROLL):
            carry = tile(g * PREP_UNROLL + u, carry)
        return carry

    return lax.fori_loop(0, n // PREP_UNROLL, trip, init)


def _tile_rows(r):
    return pl.ds(pl.multiple_of(r * PREP_ROWS, PREP_ROWS), PREP_ROWS)


def _gdn_prep(pm, conv_w):
    t = pm.shape[0]
    nj = WIDTH // LANE
    win = PREP_ROWS + HALO

    def body(x_ref, w_ref, o_ref, xp_ref):
        kind = pl.program_id(0)
        xp_ref[0:HALO, :] = jnp.zeros((HALO, LANE), F32)
        xp_ref[HALO:, :] = x_ref[...]
        w = w_ref[...]
        hs = _head_sum_matrix()

        def tile(r, _, normed):
            xw = xp_ref[pl.ds(pl.multiple_of(r * PREP_ROWS, PREP_ROWS), win), :]
            acc = _conv(xw, w)[HALO:]
            out = acc * _sigmoid(acc)
            if normed:
                out = out * lax.rsqrt(_group_sum(out * out, hs) + EPS)
            o_ref[0, 0, _tile_rows(r), :] = out[:, :DH]
            o_ref[0, 1, _tile_rows(r), :] = out[:, DH:]
            return 0

        @pl.when(kind < 2)
        def _():
            _tile_loop(t // PREP_ROWS, functools.partial(tile, normed=True), 0)

        @pl.when(kind == 2)
        def _():
            _tile_loop(t // PREP_ROWS, functools.partial(tile, normed=False), 0)

    return pl.pallas_call(
        body, name="gdn_prep", grid=(3, nj),
        in_specs=[pl.BlockSpec((t, LANE), lambda i, j: (0, i * nj + j)),
                  pl.BlockSpec((4, LANE), lambda i, j: (0, i * nj + j))],
        out_specs=pl.BlockSpec((1, 2, t, DH), lambda i, j: (i, j, 0, 0)),
        out_shape=jax.ShapeDtypeStruct((3, HEADS, t, DH), F32),
        scratch_shapes=[pltpu.VMEM((t + HALO, LANE), F32)],
        compiler_params=_params(("arbitrary", "arbitrary")),
    )(pm, conv_w)


def _gdn_prep_bwd(pm, conv_w, dqkv):
    t = pm.shape[0]
    nj = WIDTH // LANE
    win = PREP_ROWS + 2 * HALO

    def body(x_ref, w_ref, d_ref, dx_ref, dw_ref, xp_ref, dp_ref):
        kind = pl.program_id(0)
        zeros = jnp.zeros((HALO, LANE), F32)
        for ref in (xp_ref, dp_ref):
            ref[0:HALO, :] = zeros
            ref[HALO + t:, :] = zeros
        xp_ref[HALO:HALO + t, :] = x_ref[...]
        dp_ref[HALO:HALO + t, 0:DH] = d_ref[0, 0]
        dp_ref[HALO:HALO + t, DH:] = d_ref[0, 1]
        w = w_ref[...]
        hs = _head_sum_matrix()
        rows = lax.broadcasted_iota(jnp.int32, (win, LANE), 0)
        in_tile = (rows >= HALO) & (rows < HALO + PREP_ROWS)

        def tile(r, dw, normed):
            start = pl.multiple_of(r * PREP_ROWS, PREP_ROWS)
            xw = xp_ref[pl.ds(start, win), :]
            dy = dp_ref[pl.ds(start, win), :]
            acc = _conv(xw, w)
            sg = _sigmoid(acc)
            if normed:
                y = acc * sg
                rn = lax.rsqrt(_group_sum(y * y, hs) + EPS)
                yn = y * rn
                dy = rn * (dy - yn * _group_sum(dy * yn, hs))
            dacc = dy * sg * (1.0 + acc * (1.0 - sg))
            dx = w[3:4, :] * dacc
            for s in range(1, 4):
                dx += w[3 - s:4 - s, :] * _shift_up(dacc, s)
            dx_ref[_tile_rows(r), :] = dx[HALO:HALO + PREP_ROWS]
            dm = jnp.where(in_tile, dacc, 0.0)
            return tuple(dw[i] + jnp.sum(dm * (xw if i == 3 else _shift_down(xw, 3 - i)), axis=0, keepdims=True)
                         for i in range(4))

        def run(normed):
            dw = _tile_loop(t // PREP_ROWS, functools.partial(tile, normed=normed),
                            tuple(jnp.zeros((1, LANE), F32) for _ in range(4)))
            for i in range(4):
                dw_ref[i:i + 1, :] = dw[i]

        pl.when(kind < 2)(functools.partial(run, True))
        pl.when(kind == 2)(functools.partial(run, False))

    return pl.pallas_call(
        body, name="gdn_prep_bwd", grid=(3, nj),
        in_specs=[pl.BlockSpec((t, LANE), lambda i, j: (0, i * nj + j)),
                  pl.BlockSpec((4, LANE), lambda i, j: (0, i * nj + j)),
                  pl.BlockSpec((1, 2, t, DH), lambda i, j: (i, j, 0, 0))],
        out_specs=[pl.BlockSpec((t, LANE), lambda i, j: (0, i * nj + j)),
                   pl.BlockSpec((4, LANE), lambda i, j: (0, i * nj + j))],
        out_shape=[jax.ShapeDtypeStruct((t, 3 * WIDTH), F32), jax.ShapeDtypeStruct((4, 3 * WIDTH), F32)],
        scratch_shapes=[pltpu.VMEM((t + 2 * HALO, LANE), F32), pltpu.VMEM((t + 2 * HALO, LANE), F32)],
        compiler_params=_params(("arbitrary", "arbitrary")),
    )(pm, conv_w, dqkv)


FOX_COL0 = 4 * WIDTH // LANE


def _fox_prep(pm, nw):
    t = pm.shape[0]
    nj = WIDTH // LANE

    def body(x_ref, w_ref, o_ref):
        kind = pl.program_id(0)
        hs = _head_sum_matrix()
        wk = w_ref[pl.ds(kind, 1), :]

        def tile(r, _, normed):
            out = x_ref[_tile_rows(r), :]
            if normed:
                out = out * lax.rsqrt(_group_sum(out * out, hs) * (1.0 / DH) + EPS) * wk
            o_ref[0, 0, _tile_rows(r), :] = out[:, :DH]
            o_ref[0, 1, _tile_rows(r), :] = out[:, DH:]
            return 0

        @pl.when(kind < 2)
        def _():
            _tile_loop(t // PREP_ROWS, functools.partial(tile, normed=True), 0)

        @pl.when(kind == 2)
        def _():
            _tile_loop(t // PREP_ROWS, functools.partial(tile, normed=False), 0)

    return pl.pallas_call(
        body, name="fox_prep", grid=(3, nj),
        in_specs=[pl.BlockSpec((t, LANE), lambda i, j: (0, FOX_COL0 + i * nj + j)),
                  pl.BlockSpec((3, LANE), lambda i, j: (0, 0))],
        out_specs=pl.BlockSpec((1, 2, t, DH), lambda i, j: (i, j, 0, 0)),
        out_shape=jax.ShapeDtypeStruct((3, HEADS, t, DH), F32),
        compiler_params=_params(("arbitrary", "arbitrary")),
    )(pm, nw)


def _fox_prep_bwd(pm, nw, dqkv):
    t = pm.shape[0]
    nj = WIDTH // LANE

    def body(x_ref, w_ref, d_ref, dx_ref, dw_ref):
        kind = pl.program_id(0)
        hs = _head_sum_matrix()
        wk = w_ref[pl.ds(kind, 1), :]

        def tile(r, dw):
            xv = x_ref[_tile_rows(r), :]
            rn = lax.rsqrt(_group_sum(xv * xv, hs) * (1.0 / DH) + EPS)
            xn = xv * rn
            d = jnp.concatenate([d_ref[0, 0, _tile_rows(r), :], d_ref[0, 1, _tile_rows(r), :]], axis=1)
            g = d * wk
            dx_ref[_tile_rows(r), :] = rn * (g - xn * _group_sum(g * xn, hs) * (1.0 / DH))
            return dw + jnp.sum(d * xn, axis=0, keepdims=True)

        def copy_tile(r, _):
            dx_ref[_tile_rows(r), :] = jnp.concatenate([d_ref[0, 0, _tile_rows(r), :], d_ref[0, 1, _tile_rows(r), :]], axis=1)
            return 0

        @pl.when(kind < 2)
        def _():
            dw_ref[0, 0] = _tile_loop(t // PREP_ROWS, tile, jnp.zeros((1, LANE), F32))

        @pl.when(kind == 2)
        def _():
            _tile_loop(t // PREP_ROWS, copy_tile, 0)
            dw_ref[0, 0] = jnp.zeros((1, LANE), F32)

    return pl.pallas_call(
        body, name="fox_prep_bwd", grid=(3, nj),
        in_specs=[pl.BlockSpec((t, LANE), lambda i, j: (0, FOX_COL0 + i * nj + j)),
                  pl.BlockSpec((3, LANE), lambda i, j: (0, 0)),
                  pl.BlockSpec((1, 2, t, DH), lambda i, j: (i, j, 0, 0))],
        out_specs=[pl.BlockSpec((t, LANE), lambda i, j: (0, i * nj + j)),
                   pl.BlockSpec((1, 1, 1, LANE), lambda i, j: (i, j, 0, 0))],
        out_shape=[jax.ShapeDtypeStruct((t, 3 * WIDTH), F32), jax.ShapeDtypeStruct((3, nj, 1, LANE), F32)],
        compiler_params=_params(("arbitrary", "arbitrary")),
    )(pm, nw, dqkv)


SC = 256
CPS = SC // CHUNK
GDN_HP_FWD = 4
GDN_HP_BWD = 2
Q_SCALE = DH ** -0.5


def _sc_masks():
    ri = lax.broadcasted_iota(jnp.int32, (SC, SC), 0)
    ci = lax.broadcasted_iota(jnp.int32, (SC, SC), 1)
    same = (ri // CHUNK) == (ci // CHUNK)
    return same & (ri >= ci), same & (ri > ci), ri == ci


def _unit_lower_inverses(ms, eye):
    invs = [jnp.where(eye, 1.0, 0.0) + m for m in ms]
    for _ in range(5):
        ms = [_dot(m, m) for m in ms]
        invs = [inv + _dot(inv, m) for inv, m in zip(invs, ms)]
    return invs


def _lane_col(blk, lane_idx):
    lane = lax.broadcasted_iota(jnp.int32, blk.shape, 1)
    return jnp.sum(jnp.where(lane == lane_idx, blk, 0.0), axis=1, keepdims=True)


def _to_lane(col, lane_idx):
    lane = lax.broadcasted_iota(jnp.int32, (col.shape[0], LANE), 1)
    return jnp.where(lane == lane_idx, col, 0.0)


def _gdn_columns(gates_ref, run_ref, tot_ref, runt_ref, rows, h):
    return (_lane_col(gates_ref[rows, :], h), _lane_col(run_ref[rows, :], HEADS + h),
            _lane_col(tot_ref[rows, :], HEADS + h), runt_ref[pl.ds(h, 1), rows])


def _gdn_local(q, k, beta, gc, gl, grow, causal, with_kk=True):
    decay = jnp.exp(jnp.where(causal, gc - grow, -1e30))
    egc = jnp.exp(gc)
    ekd = jnp.exp(gl - gc)
    qs = q * Q_SCALE
    kb = k * beta
    if with_kk:
        both = _dot_nt(jnp.concatenate([kb, qs], axis=0), k)
        kk, qk = both[:SC], both[SC:]
    else:
        kk, qk = None, _dot_nt(qs, k)
    return beta, gl, decay, egc, ekd, qs, kb, kk, qk, jnp.where(causal, qk * decay, 0.0)


def _chunk_rows(c):
    return pl.ds(pl.multiple_of(c * CHUNK, CHUNK), CHUNK)


def _sc_rows(b):
    return pl.ds(pl.multiple_of(b * SC, SC), SC)


def _gdn_fwd(qkv, gates, run, tot, run_t):
    t = qkv.shape[2]
    nc = t // CHUNK
    nsc = t // SC

    hp_n = GDN_HP_FWD
    heads = range(hp_n)

    def body(qkv_ref, gates_ref, run_ref, tot_ref, runt_ref, o_ref, st_ref, inv_ref, kc_s, qc_s, g_s, au_s):
        hp = pl.program_id(0)
        causal, strict, eye = _sc_masks()

        def local(b, _):
            rows = _sc_rows(b)
            loc = [_gdn_local(qkv_ref[0, hh, rows, :], qkv_ref[1, hh, rows, :],
                              *_gdn_columns(gates_ref, run_ref, tot_ref, runt_ref, rows, hp * hp_n + hh), causal)
                   for hh in heads]
            invs = _unit_lower_inverses([-jnp.where(strict, l[7] * l[2], 0.0) for l in loc], eye)
            uws = []
            for hh in heads:
                beta, _, _, egc, _, _, kb, _, _, _ = loc[hh]
                inv_ref[hh, rows, :] = invs[hh].astype(inv_ref.dtype)
                uws.append(_dot(invs[hh], jnp.concatenate([qkv_ref[2, hh, rows, :] * beta, kb * egc], axis=1)))
            for hh in heads:
                _, _, _, egc, ekd, qs, _, _, _, attn = loc[hh]
                auw = _dot(attn, uws[hh])
                g_s[hh, rows, :] = qs * egc - auw[:, DH:]
                au_s[hh, rows, :] = auw[:, :DH]
                kd = qkv_ref[1, hh, rows, :] * ekd
                for j in range(CPS):
                    sl = slice(j * CHUNK, (j + 1) * CHUNK)
                    both = _dot_tn(kd[sl], uws[hh][sl])
                    kc_s[hh, b * CPS + j] = both[:, DH:]
                    qc_s[hh, b * CPS + j] = both[:, :DH]
            return 0

        lax.fori_loop(0, nsc, local, 0)

        def step(c, states):
            rows = _chunk_rows(c)
            tot_row = tot_ref[pl.ds(c * CHUNK, 1), :]
            new = []
            for hh in heads:
                s = states[hh]
                st_ref[hh, c] = s
                o_ref[hh, rows, :] = _dot(g_s[hh, rows, :], s) + au_s[hh, rows, :]
                egl = jnp.exp(_lane_col(tot_row, HEADS + hp * hp_n + hh))
                new.append(egl * s - _dot(kc_s[hh, c], s) + qc_s[hh, c])
            return tuple(new)

        lax.fori_loop(0, nc, step, tuple(jnp.zeros((DH, DH), F32) for _ in heads))

    whole = pl.BlockSpec((t, LANE), lambda h: (0, 0))
    once = dict(pipeline_mode=pl.Buffered(1))
    return pl.pallas_call(
        body, name="gdn_fwd", grid=(HEADS // hp_n,),
        in_specs=[pl.BlockSpec((3, hp_n, t, DH), lambda h: (0, h, 0, 0), **once), whole, whole, whole,
                  pl.BlockSpec((HEADS, t), lambda h: (1, 0))],
        out_specs=[pl.BlockSpec((hp_n, t, DH), lambda h: (h, 0, 0), **once),
                   pl.BlockSpec((hp_n, nc, DH, DH), lambda h: (h, 0, 0, 0), **once),
                   pl.BlockSpec((hp_n, t, SC), lambda h: (h, 0, 0), **once)],
        out_shape=[jax.ShapeDtypeStruct((HEADS, t, DH), F32), jax.ShapeDtypeStruct((HEADS, nc, DH, DH), F32),
                   jax.ShapeDtypeStruct((HEADS, t, SC), MXU)],
        scratch_shapes=[pltpu.VMEM((hp_n, nc, DH, DH), F32), pltpu.VMEM((hp_n, nc, DH, DH), F32),
                        pltpu.VMEM((hp_n, t, DH), F32), pltpu.VMEM((hp_n, t, DH), F32)],
        compiler_params=_params(("arbitrary",)),
    )(qkv, gates, run, tot, run_t)


def _gdn_bwd(qkv, gates, run, tot, run_t, inv, states, do):
    t = qkv.shape[2]
    nc = t // CHUNK
    nsc = t // SC

    hp_n = GDN_HP_BWD
    heads = range(hp_n)

    def body(qkv_ref, gates_ref, run_ref, tot_ref, runt_ref, inv_ref, st_ref, do_ref,
             dqkv_ref, dcol_ref, dtot_ref, drow_ref, uw_s, kc_s, h_s, dsn_s):
        hp = pl.program_id(0)
        causal, strict, _ = _sc_masks()

        @pl.when(hp == 0)
        def _():
            dcol_ref[...] = jnp.zeros_like(dcol_ref)
            dtot_ref[...] = jnp.zeros_like(dtot_ref)

        def local_of(hh, rows, with_kk=True):
            return _gdn_local(qkv_ref[0, hh, rows, :], qkv_ref[1, hh, rows, :],
                              *_gdn_columns(gates_ref, run_ref, tot_ref, runt_ref, rows, hp * hp_n + hh), causal, with_kk)

        def local(b, _):
            rows = _sc_rows(b)
            loc = [local_of(hh, rows, with_kk=False) for hh in heads]
            us, ws = [], []
            for hh in heads:
                beta, _, _, egc, _, _, kb, _, _, _ = loc[hh]
                inv_b = inv_ref[hh, rows, :]
                us.append(_dot(inv_b, qkv_ref[2, hh, rows, :] * beta))
                ws.append(_dot(inv_b, kb * egc))
            gs = [loc[hh][5] * loc[hh][3] - _dot(loc[hh][9], ws[hh]) for hh in heads]
            for hh in heads:
                uw_s[hh, rows, :] = jnp.concatenate([us[hh], ws[hh]], axis=1)
                kd = qkv_ref[1, hh, rows, :] * loc[hh][4]
                dout = do_ref[hh, rows, :]
                for j in range(CPS):
                    sl = slice(j * CHUNK, (j + 1) * CHUNK)
                    kc_s[hh, b * CPS + j] = _dot_tn(kd[sl], ws[hh][sl])
                    h_s[hh, b * CPS + j] = _dot_tn(gs[hh][sl], dout[sl])
            return 0

        lax.fori_loop(0, nsc, local, 0)

        def step(i, dss):
            c = nc - 1 - i
            tot_row = tot_ref[pl.ds(c * CHUNK, 1), :]
            new = []
            for hh in heads:
                ds = dss[hh]
                dsn_s[hh, c] = ds
                egl = jnp.exp(_lane_col(tot_row, HEADS + hp * hp_n + hh))
                new.append(egl * ds - _dot_tn(kc_s[hh, c], ds) + h_s[hh, c])
            return tuple(new)

        lax.fori_loop(0, nc, step, tuple(jnp.zeros((DH, DH), F32) for _ in heads))

        def back(b, _):
            rows = _sc_rows(b)
            first = lax.broadcasted_iota(jnp.int32, (CHUNK, 1), 0) == 0
            loc = [local_of(hh, rows) for hh in heads]
            sign = jnp.where(lax.broadcasted_iota(jnp.int32, (1, LANE), 1) < DH, 1.0, -1.0)
            mid = []
            for hh in heads:
                beta, gl, decay, egc, ekd, qs, kb, kk, qk, attn = loc[hh]
                uw = uw_s[hh, rows, :]
                kd = qkv_ref[1, hh, rows, :] * ekd
                dout = do_ref[hh, rows, :]
                dg_p, dkd_p, duw_p, dgl_p = [], [], [], []
                for j in range(CPS):
                    sl = slice(j * CHUNK, (j + 1) * CHUNK)
                    s = st_ref[hh, b * CPS + j]
                    dsn = dsn_s[hh, b * CPS + j]
                    both = _dot_nt(jnp.concatenate([dsn, dout[sl]], axis=0), s)
                    dg_p.append(both[CHUNK:])
                    dsc = jnp.concatenate([dsn, -both[:CHUNK]], axis=1)
                    dkd_p.append(_dot_nt(uw[sl], dsc))
                    duw_p.append(_dot(kd[sl], dsc))
                    degl = jnp.sum(jnp.sum(s * dsn, axis=1, keepdims=True), axis=0, keepdims=True)
                    dgl_p.append(jnp.where(first, degl * jnp.exp(gl[j * CHUNK:j * CHUNK + 1, :]), 0.0))
                dg, dkd = jnp.concatenate(dg_p, axis=0), jnp.concatenate(dkd_p, axis=0)
                dod = jnp.concatenate([dout, dg], axis=1)
                da = jnp.where(causal, _dot_nt(dod, uw * sign), 0.0)
                duw = jnp.concatenate(duw_p, axis=0) + _dot_tn(attn, dod) * sign
                mid.append((kd, dg, dkd, da, duw, jnp.concatenate(dgl_p, axis=0)))
            inv_ts = [inv_ref[hh, rows, :].astype(F32).T for hh in heads]
            its = [_dot(inv_ts[hh], mid[hh][4]) for hh in heads]
            dinvs = []
            for hh in heads:
                beta, _, _, egc, _, _, kb, _, _, _ = loc[hh]
                dinvs.append(_dot_nt(mid[hh][4], jnp.concatenate([qkv_ref[2, hh, rows, :] * beta, kb * egc], axis=1)))
            half = [_dot(inv_ts[hh], dinvs[hh]) for hh in heads]
            dls = [jnp.where(strict, -_dot(half[hh], inv_ts[hh]), 0.0) for hh in heads]
            for hh in heads:
                head = hp * hp_n + hh
                beta, gl, decay, egc, ekd, qs, kb, kk, qk, attn = loc[hh]
                kd, dg, dkd, da, _, dgl_first = mid[hh]
                k, v = qkv_ref[1, hh, rows, :], qkv_ref[2, hh, rows, :]
                dvb, dkbe = its[hh][:, :DH], its[hh][:, DH:]
                dl = dls[hh]
                dlogd = (dl * kk + da * qk) * decay
                dd = jnp.concatenate([dl * decay, da * decay], axis=0)
                ddk = _dot(dd, k)
                dkb = ddk[:SC] + dkbe * egc
                dqs = ddk[SC:] + dg * egc
                dk = _dot_tn(dd, jnp.concatenate([kb, qs], axis=0)) + dkd * ekd + dkb * beta
                dkd_kd = jnp.sum(dkd * kd, axis=1, keepdims=True)
                narrow = dg * (qs * egc) + dkbe * (kb * egc)
                wide = dlogd[:, :LANE] + dlogd[:, LANE:] + jnp.concatenate([narrow, jnp.zeros((SC, LANE - DH), F32)], axis=1)
                dgc = jnp.sum(wide, axis=1, keepdims=True) - dkd_kd
                dbeta = jnp.sum(dkb * k + dvb * v, axis=1, keepdims=True)
                dqkv_ref[0, hh, rows, :] = dqs * Q_SCALE
                dqkv_ref[1, hh, rows, :] = dk
                dqkv_ref[2, hh, rows, :] = dvb * beta
                dcol_ref[rows, :] += _to_lane(dbeta, head) + _to_lane(dgc, HEADS + head)
                dtot_ref[rows, :] += _to_lane(dkd_kd + dgl_first, HEADS + head)
                drow_ref[pl.ds(head, 1), rows] = -jnp.sum(dlogd, axis=0, keepdims=True)
            return 0

        lax.fori_loop(0, nsc, back, 0)

    whole = pl.BlockSpec((t, LANE), lambda h: (0, 0))
    rowspec = pl.BlockSpec((HEADS, t), lambda h: (0, 0))
    sq = pltpu.VMEM((hp_n, nc, DH, DH), F32)
    per_head = pltpu.VMEM((hp_n, t, 2 * DH), F32)
    return pl.pallas_call(
        body, name="gdn_bwd", grid=(HEADS // hp_n,),
        in_specs=[pl.BlockSpec((3, hp_n, t, DH), lambda h: (0, h, 0, 0)), whole, whole, whole,
                  pl.BlockSpec((HEADS, t), lambda h: (1, 0)),
                  pl.BlockSpec((hp_n, t, SC), lambda h: (h, 0, 0)), pl.BlockSpec((hp_n, nc, DH, DH), lambda h: (h, 0, 0, 0)),
                  pl.BlockSpec((hp_n, t, DH), lambda h: (h, 0, 0))],
        out_specs=[pl.BlockSpec((3, hp_n, t, DH), lambda h: (0, h, 0, 0)), whole, whole, rowspec],
        out_shape=[jax.ShapeDtypeStruct((3, HEADS, t, DH), F32), jax.ShapeDtypeStruct((t, LANE), F32),
                   jax.ShapeDtypeStruct((t, LANE), F32), jax.ShapeDtypeStruct((HEADS, t), F32)],
        scratch_shapes=[per_head, sq, sq, sq],
        compiler_params=_params(("arbitrary",)),
    )(qkv, gates, run, tot, run_t, inv, states, do)


FOX_HP_FWD = 4
FOX_HP_BWD = 4


def _wide_t(a):
    r = a.shape[0]
    return jnp.concatenate([a, jnp.zeros((r, LANE - DH), F32)], axis=1).T[:DH]


def _tall_t(a):
    r = a.shape[1]
    return jnp.concatenate([a, jnp.zeros((LANE - DH, r), F32)], axis=0).T[:, :DH]


def _key_side_f(run_blk, head, qb):
    col = jnp.broadcast_to(_lane_col(run_blk, 2 * HEADS + head), (run_blk.shape[0], LANE))
    return jnp.concatenate([col] * (qb // LANE), axis=1)


def _diag_mask(qb):
    return lax.broadcasted_iota(jnp.int32, (qb, qb), 0) <= lax.broadcasted_iota(jnp.int32, (qb, qb), 1)


def _fox_fwd(qkv, run, run_t, carry=None):
    t = qkv.shape[2]
    qb = min(QB, t)
    nq = t // qb
    hp_n = FOX_HP_FWD
    c_in, c_in_specs, c_out_shape, c_out_specs, c_sem = _carry_operands(*carry) if carry else ([], [], [], [], None)
    n_c = len(c_in)

    def body(*refs):
        q_ref, k_ref, v_ref, run_ref, runt_ref = refs[:5]
        o_ref, lse_ref = refs[5 + n_c:7 + n_c]
        s0 = 7 + n_c + len(c_out_shape)
        kb_s, vt_s, raw_s, m_s, l_s, acc_s = refs[s0:s0 + 6]
        hp = pl.program_id(0)
        i = pl.program_id(1)

        if carry:
            _carry_step(carry[0], refs[5:5 + n_c], refs[7 + n_c], refs[8 + n_c], refs[-1], hp * nq + i,
                        (HEADS // hp_n) * nq)

        @pl.when(i == 0)
        def _():
            for hh in range(hp_n):
                kb_s[hh] = k_ref[0, hh].astype(MXU)
                for b in range(nq):
                    rows = slice(b * qb, (b + 1) * qb)
                    vt_s[hh, :, rows] = _wide_t(v_ref[0, hh, rows, :]).astype(MXU)

        qrows = pl.ds(pl.multiple_of(i * qb, qb), qb)
        qs = [(q_ref[0, hh] * Q_SCALE).astype(MXU) for hh in range(hp_n)]
        fq = [runt_ref[pl.ds(hp * hp_n + hh, 1), qrows] for hh in range(hp_n)]

        def block_rows(j):
            return pl.ds(pl.multiple_of(j * qb, qb), qb)

        def scores(j):
            for hh in range(hp_n):
                raw_s[j % 2, hh] = _dot_nt(kb_s[hh, block_rows(j), :], qs[hh])

        def absorb(j, diagonal):
            rows = block_rows(j)
            run_blk = run_ref[rows, :]
            stats, pv = [], []
            for hh in range(hp_n):
                m, l = m_s[hh], l_s[hh]
                st = raw_s[j % 2, hh] + (fq[hh] - _key_side_f(run_blk, hp * hp_n + hh, qb))
                if diagonal:
                    st = jnp.where(_diag_mask(qb), st, -1e30)
                m_new = jnp.maximum(m, jnp.max(st, axis=0, keepdims=True))
                p = jnp.exp(st - m_new)
                alpha = jnp.exp(m - m_new)
                stats.append((m_new, alpha * l + jnp.sum(p, axis=0, keepdims=True), alpha))
                pv.append(_dot(vt_s[hh, :, rows], p))
            for hh in range(hp_n):
                m_s[hh], l_s[hh] = stats[hh][0], stats[hh][1]
                acc_s[hh] = stats[hh][2] * acc_s[hh] + pv[hh]

        def kstep(j, _):
            scores(j + 1)
            absorb(j, False)
            return 0

        for hh in range(hp_n):
            m_s[hh] = jnp.full((1, qb), -1e30, F32)
            l_s[hh] = jnp.zeros((1, qb), F32)
            acc_s[hh] = jnp.zeros((DH, qb), F32)
        scores(0)
        lax.fori_loop(0, i, kstep, 0)
        absorb(i, True)
        for hh in range(hp_n):
            l = l_s[hh]
            o_ref[hh] = _tall_t(acc_s[hh] / l)
            lse_ref[pl.ds(hp * hp_n + hh, 1), qrows] = m_s[hh] + jnp.log(l)

    out = pl.pallas_call(
        body, name="fox_fwd", grid=(HEADS // hp_n, nq),
        in_specs=[pl.BlockSpec((1, hp_n, qb, DH), lambda h, i: (0, h, i, 0)),
                  pl.BlockSpec((1, hp_n, t, DH), lambda h, i: (1, h, 0, 0)),
                  pl.BlockSpec((1, hp_n, t, DH), lambda h, i: (2, h, 0, 0)),
                  pl.BlockSpec((t, LANE), lambda h, i: (0, 0)),
                  pl.BlockSpec((HEADS, t), lambda h, i: (2, 0))] + c_in_specs,
        out_specs=[pl.BlockSpec((hp_n, qb, DH), lambda h, i: (h, i, 0)),
                   pl.BlockSpec((HEADS, t), lambda h, i: (0, 0))] + c_out_specs,
        out_shape=[jax.ShapeDtypeStruct((HEADS, t, DH), F32), jax.ShapeDtypeStruct((HEADS, t), F32)] + c_out_shape,
        input_output_aliases={5 + j: 4 + j for j in range(n_c)},
        scratch_shapes=[pltpu.VMEM((hp_n, t, DH), MXU), pltpu.VMEM((hp_n, DH, t), MXU), pltpu.VMEM((2, hp_n, qb, qb), F32),
                        pltpu.VMEM((hp_n, 1, qb), F32), pltpu.VMEM((hp_n, 1, qb), F32), pltpu.VMEM((hp_n, DH, qb), F32)]
        + ([c_sem] if carry else []),
        compiler_params=_params(("arbitrary", "arbitrary")),
    )(qkv, qkv, qkv, run, run_t, *c_in)
    return (out[0], out[1], _carry_state(out[2:])) if carry else (out[0], out[1])


def _fox_bwd(qkv, run, run_t, o, lse, do, carry=None):
    t = qkv.shape[2]
    qb = min(QB, t)
    nq = t // qb
    hp_n = FOX_HP_BWD
    c_in, c_in_specs, c_out_shape, c_out_specs, c_sem = _carry_operands(*carry) if carry else ([], [], [], [], None)
    n_c = len(c_in)

    def body(*refs):
        q_ref, k_ref, v_ref, run_ref, runt_ref, o_ref, lse_ref, do_ref = refs[:8]
        dqkv_ref, dcol_ref, drow_ref = refs[8 + n_c:11 + n_c]
        s0 = 11 + n_c + len(c_out_shape)
        dqt_s, raw_s, dpt_s, dro_s, dk_s, dv_s, dsum_s = refs[s0:s0 + 7]
        hp = pl.program_id(0)
        j = pl.program_id(1)

        if carry:
            _carry_step(carry[0], refs[8:8 + n_c], refs[11 + n_c], refs[12 + n_c], refs[-1], hp * nq + j,
                        (HEADS // hp_n) * nq)

        @pl.when(j == 0)
        def _():
            dqt_s[...] = jnp.zeros_like(dqt_s)

        @pl.when((j == 0) & (hp == 0))
        def _():
            dcol_ref[...] = jnp.zeros_like(dcol_ref)
            drow_ref[...] = jnp.zeros_like(drow_ref)

        krows = pl.ds(pl.multiple_of(j * qb, qb), qb)
        run_blk = run_ref[krows, :]
        ones8 = jnp.ones((8, DH), MXU)
        kb, kt, vb, fk = [], [], [], []
        for hh in range(hp_n):
            kf = k_ref[0, hh]
            kb.append(kf.astype(MXU))
            kt.append(_wide_t(kf).astype(MXU))
            vb.append(v_ref[0, hh].astype(MXU))
            fk.append(_key_side_f(run_blk, hp * hp_n + hh, qb))

        def block_rows(i):
            return pl.ds(i * qb if isinstance(i, int) else pl.multiple_of(i * qb, qb), qb)

        def products(i):
            rows = block_rows(i)
            slot = i % 2
            for hh in range(hp_n):
                dout = do_ref[hh, rows, :]
                x = dout * o_ref[hh, rows, :]
                x_hi = x.astype(MXU)
                raw_s[slot, hh] = _dot_nt(kb[hh], q_ref[0, hh, rows, :] * Q_SCALE)
                dpt_s[slot, hh] = _dot_nt(vb[hh], dout)
                dro_s[slot, hh] = _dot_nt(ones8, x_hi) + _dot_nt(ones8, x - x_hi.astype(F32))

        def absorb(i, diagonal):
            rows = block_rows(i)
            slot = i % 2
            pieces = []
            for hh in range(hp_n):
                head = hp * hp_n + hh
                raw, dpt, drow = raw_s[slot, hh], dpt_s[slot, hh], dro_s[slot, hh, 0:1, :]
                off = runt_ref[pl.ds(head, 1), rows] - lse_ref[pl.ds(head, 1), rows]
                st = raw + (off - fk[hh])
                if diagonal:
                    st = jnp.where(_diag_mask(qb), st, -1e30)
                pt = jnp.exp(st)
                dst = pt * (dpt - drow)
                drow_ref[pl.ds(head, 1), rows] += jnp.sum(dst, axis=0, keepdims=True)
                folded = dst[:, 0:LANE]
                for c in range(1, qb // LANE):
                    folded = folded + dst[:, c * LANE:(c + 1) * LANE]
                pieces.append((_dot(dst, q_ref[0, hh, rows, :] * Q_SCALE), _dot(pt, do_ref[hh, rows, :]),
                               _dot(kt[hh], dst), folded))
            for hh in range(hp_n):
                dqt_s[hh, :, rows] += pieces[hh][2]
                if diagonal:
                    dk_s[hh], dv_s[hh], dsum_s[hh] = pieces[hh][0], pieces[hh][1], pieces[hh][3]
                else:
                    dk_s[hh] += pieces[hh][0]
                    dv_s[hh] += pieces[hh][1]
                    dsum_s[hh] += pieces[hh][3]

        def qstep(i, _):
            products(i + 1)
            absorb(i, False)
            return 0

        products(j)
        products(jnp.minimum(j + 1, nq - 1))
        absorb(j, True)
        lax.fori_loop(j + 1, nq - 1, qstep, 0)

        @pl.when(j < nq - 1)
        def _():
            absorb(nq - 1, False)
        for hh in range(hp_n):
            dqkv_ref[1, hh, krows, :] = dk_s[hh]
            dqkv_ref[2, hh, krows, :] = dv_s[hh]
            dcol_ref[krows, :] += _to_lane(-jnp.sum(dsum_s[hh], axis=1, keepdims=True), 2 * HEADS + hp * hp_n + hh)

        @pl.when(j == nq - 1)
        def _():
            for hh in range(hp_n):
                for b in range(nq):
                    rows = slice(b * qb, (b + 1) * qb)
                    dqkv_ref[0, hh, rows, :] = _tall_t(dqt_s[hh, :, rows]) * Q_SCALE

    once = dict(pipeline_mode=pl.Buffered(1))
    full = pl.BlockSpec((hp_n, t, DH), lambda h, j: (h, 0, 0), **once)
    rows8 = pl.BlockSpec((HEADS, t), lambda h, j: (0, 0))
    out = pl.pallas_call(
        body, name="fox_bwd", grid=(HEADS // hp_n, nq),
        in_specs=[pl.BlockSpec((1, hp_n, t, DH), lambda h, j: (0, h, 0, 0), **once),
                  pl.BlockSpec((1, hp_n, qb, DH), lambda h, j: (1, h, j, 0)),
                  pl.BlockSpec((1, hp_n, qb, DH), lambda h, j: (2, h, j, 0)),
                  pl.BlockSpec((t, LANE), lambda h, j: (0, 0), **once), pl.BlockSpec((HEADS, t), lambda h, j: (2, 0)),
                  full, rows8, full] + c_in_specs,
        out_specs=[pl.BlockSpec((3, hp_n, t, DH), lambda h, j: (0, h, 0, 0), **once),
                   pl.BlockSpec((t, LANE), lambda h, j: (0, 0)), rows8] + c_out_specs,
        out_shape=[jax.ShapeDtypeStruct((3, HEADS, t, DH), F32), jax.ShapeDtypeStruct((t, LANE), F32),
                   jax.ShapeDtypeStruct((HEADS, t), F32)] + c_out_shape,
        input_output_aliases={8 + j: 5 + j for j in range(n_c)},
        scratch_shapes=[pltpu.VMEM((hp_n, DH, t), F32), pltpu.VMEM((2, hp_n, qb, qb), F32), pltpu.VMEM((2, hp_n, qb, qb), F32),
                        pltpu.VMEM((2, hp_n, 8, qb), F32), pltpu.VMEM((hp_n, qb, DH), F32), pltpu.VMEM((hp_n, qb, DH), F32),
                        pltpu.VMEM((hp_n, qb, LANE), F32)] + ([c_sem] if carry else []),
        compiler_params=_params(("arbitrary", "arbitrary")),
    )(qkv, qkv, qkv, run, run_t, o, lse, do, *c_in)
    return (out[0], out[1], out[2], _carry_state(out[3:])) if carry else tuple(out)


Z_COL0 = 3 * WIDTH // LANE
FGATE_COL0 = 7 * WIDTH // LANE


def _gdn_post(o, pm, onw):
    t = pm.shape[0]

    def body(o_ref, z_ref, w_ref, m_ref):
        z = z_ref[...]
        sz = z * _sigmoid(z)
        halves = []
        for hh in range(2):
            ov = o_ref[hh]
            n = ov * lax.rsqrt(jnp.mean(ov * ov, axis=-1, keepdims=True) + EPS) * w_ref[...]
            halves.append(n * sz[:, hh * DH:(hh + 1) * DH])
        m_ref[...] = jnp.concatenate(halves, axis=1).astype(m_ref.dtype)

    return pl.pallas_call(
        body, name="gdn_post", grid=(WIDTH // LANE,),
        in_specs=[pl.BlockSpec((2, t, DH), lambda j: (j, 0, 0)), pl.BlockSpec((t, LANE), lambda j: (0, Z_COL0 + j)),
                  pl.BlockSpec((1, DH), lambda j: (0, 0))],
        out_specs=pl.BlockSpec((t, LANE), lambda j: (0, j)),
        out_shape=jax.ShapeDtypeStruct((t, WIDTH), MXU),
        compiler_params=_params(("arbitrary",)),
    )(o, pm, onw)


def _gdn_post_bwd(o, pm, onw, dmix):
    t = pm.shape[0]

    def body(o_ref, z_ref, w_ref, dm_ref, do_ref, dz_ref, dw_ref):
        @pl.when(pl.program_id(0) == 0)
        def _():
            dw_ref[...] = jnp.zeros_like(dw_ref)

        z = z_ref[...]
        sg = _sigmoid(z)
        sz = z * sg
        dsz = sg * (1.0 + z * (1.0 - sg))
        dm = dm_ref[...]
        for hh in range(2):
            cols = slice(hh * DH, (hh + 1) * DH)
            ov = o_ref[hh]
            r = lax.rsqrt(jnp.mean(ov * ov, axis=-1, keepdims=True) + EPS)
            xn = ov * r
            dmh = dm[:, cols]
            dn = dmh * sz[:, cols]
            dz_ref[:, cols] = dmh * (xn * w_ref[...]) * dsz[:, cols]
            dw_ref[...] += jnp.sum(dn * xn, axis=0, keepdims=True)
            g = dn * w_ref[...]
            do_ref[hh] = r * (g - xn * jnp.mean(g * xn, axis=-1, keepdims=True))

    return pl.pallas_call(
        body, name="gdn_post_bwd", grid=(WIDTH // LANE,),
        in_specs=[pl.BlockSpec((2, t, DH), lambda j: (j, 0, 0)), pl.BlockSpec((t, LANE), lambda j: (0, Z_COL0 + j)),
                  pl.BlockSpec((1, DH), lambda j: (0, 0)), pl.BlockSpec((t, LANE), lambda j: (0, j))],
        out_specs=[pl.BlockSpec((2, t, DH), lambda j: (j, 0, 0)), pl.BlockSpec((t, LANE), lambda j: (0, j)),
                   pl.BlockSpec((1, DH), lambda j: (0, 0))],
        out_shape=[jax.ShapeDtypeStruct((HEADS, t, DH), F32), jax.ShapeDtypeStruct((t, WIDTH), F32),
                   jax.ShapeDtypeStruct((1, DH), F32)],
        compiler_params=_params(("arbitrary",)),
    )(o, pm, onw, dmix)


def _fox_post(o, pm):
    t = pm.shape[0]

    def body(o_ref, g_ref, m_ref):
        m_ref[...] = (jnp.concatenate([o_ref[0], o_ref[1]], axis=1) * _sigmoid(g_ref[...])).astype(m_ref.dtype)

    return pl.pallas_call(
        body, name="fox_post", grid=(WIDTH // LANE,),
        in_specs=[pl.BlockSpec((2, t, DH), lambda j: (j, 0, 0)), pl.BlockSpec((t, LANE), lambda j: (0, FGATE_COL0 + j))],
        out_specs=pl.BlockSpec((t, LANE), lambda j: (0, j)),
        out_shape=jax.ShapeDtypeStruct((t, WIDTH), MXU),
        compiler_params=_params(("arbitrary",)),
    )(o, pm)


def _fox_post_bwd(o, pm, dmix):
    t = pm.shape[0]

    def body(o_ref, g_ref, dm_ref, do_ref, dg_ref):
        sg = _sigmoid(g_ref[...])
        dm = dm_ref[...]
        for hh in range(2):
            cols = slice(hh * DH, (hh + 1) * DH)
            do_ref[hh] = dm[:, cols] * sg[:, cols]
            dg_ref[:, cols] = dm[:, cols] * o_ref[hh] * (sg * (1.0 - sg))[:, cols]

    return pl.pallas_call(
        body, name="fox_post_bwd", grid=(WIDTH // LANE,),
        in_specs=[pl.BlockSpec((2, t, DH), lambda j: (j, 0, 0)), pl.BlockSpec((t, LANE), lambda j: (0, FGATE_COL0 + j)),
                  pl.BlockSpec((t, LANE), lambda j: (0, j))],
        out_specs=[pl.BlockSpec((2, t, DH), lambda j: (j, 0, 0)), pl.BlockSpec((t, LANE), lambda j: (0, j))],
        out_shape=[jax.ShapeDtypeStruct((HEADS, t, DH), F32), jax.ShapeDtypeStruct((t, WIDTH), F32)],
        compiler_params=_params(("arbitrary",)),
    )(o, pm, dmix)


def _tail(x, mixg, mixf, tgt, wo, n2w, wg_t, wu_t, wd, fw):
    t, d = x.shape
    dff = wd.shape[0]
    tb = min(TB, t)

    def body(x_ref, mg_ref, mf_ref, t_ref, wo_ref, n2_ref, wg_ref, wu_ref, wd_ref, fw_ref,
             h2_ref, act_ref, dgate_ref, dup_ref, dx3_ref, dx2_ref, dmg_ref, dmf_ref, dn2_ref, dfw_ref, loss_ref):
        @pl.when(pl.program_id(0) == 0)
        def _():
            dn2_ref[...] = jnp.zeros_like(dn2_ref)
            dfw_ref[...] = jnp.zeros_like(dfw_ref)
            loss_ref[...] = jnp.zeros_like(loss_ref)

        x2 = x_ref[...] + _dot(mg_ref[...], wo_ref[0:WIDTH, :]) + _dot(mf_ref[...], wo_ref[WIDTH:2 * WIDTH, :])
        r2 = lax.rsqrt(jnp.mean(x2 * x2, axis=-1, keepdims=True) + EPS)
        xn2 = x2 * r2
        h2 = (xn2 * n2_ref[...]).astype(MXU)
        h2_ref[...] = h2
        gate = _dot_nt(h2, wg_ref[...])
        up = _dot_nt(h2, wu_ref[...])
        sg = _sigmoid(gate)
        sl = gate * sg
        act = (sl * up).astype(MXU)
        act_ref[...] = act
        x3 = x2 + _dot(act, wd_ref[...])
        r3 = lax.rsqrt(jnp.mean(x3 * x3, axis=-1, keepdims=True) + EPS)
        xn3 = x3 * r3
        err = xn3 * fw_ref[...] - t_ref[...]
        loss_ref[...] += 0.5 * jnp.sum(jnp.mean(err * err, axis=-1, keepdims=True), axis=0, keepdims=True)
        dy = err * (1.0 / d)
        dfw_ref[...] += jnp.sum(dy * xn3, axis=0, keepdims=True)
        g3 = dy * fw_ref[...]
        dx3 = r3 * (g3 - xn3 * jnp.mean(g3 * xn3, axis=-1, keepdims=True))
        dx3_ref[...] = dx3.astype(MXU)
        dact = _dot_nt(dx3, wd_ref[...])
        dgate = (dact * up * (sg * (1.0 + gate * (1.0 - sg)))).astype(MXU)
        dup = (dact * sl).astype(MXU)
        dgate_ref[...] = dgate
        dup_ref[...] = dup
        dh2 = _dot(dgate, wg_ref[...]) + _dot(dup, wu_ref[...])
        dn2_ref[...] += jnp.sum(dh2 * xn2, axis=0, keepdims=True)
        g2 = dh2 * n2_ref[...]
        dx2 = dx3 + r2 * (g2 - xn2 * jnp.mean(g2 * xn2, axis=-1, keepdims=True))
        dx2_ref[...] = dx2
        dmg_ref[...] = _dot_nt(dx2, wo_ref[0:WIDTH, :])
        dmf_ref[...] = _dot_nt(dx2, wo_ref[WIDTH:2 * WIDTH, :])

    def tok(n):
        return pl.BlockSpec((tb, n), lambda i: (i, 0))

    acc = pl.BlockSpec((1, d), lambda i: (0, 0))
    sds = jax.ShapeDtypeStruct
    return pl.pallas_call(
        body, name="tail", grid=(t // tb,),
        in_specs=[tok(d), tok(WIDTH), tok(WIDTH), tok(d), _resident(wo.shape), _resident((1, d)),
                  _resident(wg_t.shape), _resident(wu_t.shape), _resident(wd.shape), _resident((1, d))],
        out_specs=[tok(d), tok(dff), tok(dff), tok(dff), tok(d), tok(d), tok(WIDTH), tok(WIDTH), acc, acc,
                   pl.BlockSpec((1, 1), lambda i: (0, 0))],
        out_shape=[sds((t, d), MXU), sds((t, dff), MXU), sds((t, dff), MXU), sds((t, dff), MXU), sds((t, d), MXU),
                   sds((t, d), F32), sds((t, WIDTH), F32), sds((t, WIDTH), F32), sds((1, d), F32), sds((1, d), F32),
                   sds((1, 1), F32)],
        compiler_params=_params(("arbitrary",)),
    )(x, mixg, mixf, tgt, wo, n2w, wg_t, wu_t, wd, fw)


def _wgrads(a_list, b, name):
    t, n = b.shape
    ms = [a.shape[1] for a in a_list]
    bms = [256 if m % 256 == 0 else LANE for m in ms]
    nbs = [m // bm for m, bm in zip(ms, bms)]
    k = len(a_list)
    cast = b.dtype != jnp.dtype(MXU)

    def body(*refs):
        a_refs, b_ref, o_refs = refs[:k], refs[k], refs[k + 1:2 * k + 1]
        i = pl.program_id(0)
        if cast:
            @pl.when(i == 0)
            def _():
                refs[-1][...] = b_ref[...].astype(MXU)
        for a_ref, o_ref, nb in zip(a_refs, o_refs, nbs):
            @pl.when(i < nb)
            def _():
                o_ref[...] = _dot_tn(a_ref[...], refs[-1][...] if cast else b_ref[...]).astype(o_ref.dtype)

    def clamp(nb):
        return lambda i: jnp.minimum(i, nb - 1)

    return pl.pallas_call(
        body, name=name, grid=(max(nbs),),
        in_specs=[pl.BlockSpec((t, bm), lambda i, c=clamp(nb): (0, c(i))) for bm, nb in zip(bms, nbs)] + [_resident((t, n))],
        out_specs=[pl.BlockSpec((bm, n), lambda i, c=clamp(nb): (c(i), 0)) for bm, nb in zip(bms, nbs)],
        out_shape=[jax.ShapeDtypeStruct((m, n), WIRE) for m in ms],
        scratch_shapes=[pltpu.VMEM((t, n), MXU)] if cast else [],
        compiler_params=_params(("arbitrary",)),
    )(*a_list, b)


def _merge_dw_in(d_gdn, d_z, d_fox, d_fg, d_small, shards=False):
    pieces = [d_gdn, d_z, d_small[:2 * HEADS], d_fox, d_fg, d_small[2 * HEADS:3 * HEADS]]
    if not shards:
        return jnp.concatenate(pieces, axis=0)
    n = sum(p.shape[0] for p in pieces) // N_DEV
    out = []
    for dev in range(N_DEV):
        parts, first = [], 0
        for p in pieces:
            lo, hi = max(dev * n, first), min((dev + 1) * n, first + p.shape[0])
            if lo < hi:
                parts.append(p[lo - first:hi - first])
            first += p.shape[0]
        out.append(jnp.concatenate(parts, axis=0))
    return jnp.stack(out)


def _lanes(*pieces):
    v = jnp.concatenate([p.reshape(-1).astype(F32) for p in pieces])
    return jnp.pad(v, (0, LANE - v.shape[0])).reshape(1, LANE)


def _vector_params(p):
    d = p["norm1_w"].size
    gparams = jnp.concatenate([_lanes(jnp.zeros(HEADS), p["gdn_dt_bias"], p["fox_f_bias"]),
                               _lanes(jnp.zeros(HEADS), p["gdn_A_log"]), jnp.zeros((6, LANE), F32)])
    fox_nw = jnp.stack([jnp.tile(p["fox_q_norm_w"].reshape(-1), 2), jnp.tile(p["fox_k_norm_w"].reshape(-1), 2),
                        jnp.ones((LANE,), F32)])
    return dict(n1w=p["norm1_w"].reshape(1, d), n2w=p["norm2_w"].reshape(1, d), fw=p["final_norm_w"].reshape(1, d),
                onw=p["gdn_out_norm_w"].reshape(1, DH), gparams=gparams, fox_nw=fox_nw)


def _mixer_forward(x, vp, w_t, ws_t, conv_w, carry=None):
    h1, pm, ps = _inproj(x, vp["n1w"], w_t, ws_t)
    gates, run, tot, run_t = _gates(ps, vp["gparams"])
    fqkv = _fox_prep(pm, vp["fox_nw"])
    o_fox, lse, *carried = _fox_fwd(fqkv, run, run_t, carry)
    if carry:
        pm, o_fox = lax.optimization_barrier((pm, o_fox))
    mixf = _fox_post(o_fox, pm)
    gqkv = _gdn_prep(pm, conv_w)
    o_gdn, states, inv = _gdn_fwd(gqkv, gates, run, tot, run_t)
    mixg = _gdn_post(o_gdn, pm, vp["onw"])
    return dict(h1=h1, pm=pm, ps=ps, gates=gates, run=run, tot=tot, run_t=run_t, gqkv=gqkv, o_gdn=o_gdn, states=states,
                inv=inv, mixg=mixg, fqkv=fqkv, o_fox=o_fox, lse=lse, mixf=mixf, carried=carried[0] if carried else None)


def _mixer_backward(x, vp, w_t, ws_t, conv_w, f, dx2, dmixg, dmixf, carry=None, scatter_own=False):
    pm = f["pm"]
    do_fox, dfg = _fox_post_bwd(f["o_fox"], pm, dmixf)
    dfqkv, dcol_f, drow_f, *carried = _fox_bwd(f["fqkv"], f["run"], f["run_t"], f["o_fox"], f["lse"], do_fox, carry)
    dfox, dfnw = _fox_prep_bwd(pm, vp["fox_nw"], dfqkv)
    do_gdn, dz, donw = _gdn_post_bwd(f["o_gdn"], pm, vp["onw"], dmixg)
    dgqkv, dcol_g, dtot_g, drow_g = _gdn_bwd(f["gqkv"], f["gates"], f["run"], f["tot"], f["run_t"], f["inv"], f["states"], do_gdn)
    dgdn, dconv = _gdn_prep_bwd(pm, conv_w, dgqkv)
    dps, gsum = _gates_bwd(f["ps"], vp["gparams"], dcol_g, dtot_g, dcol_f, drow_g, drow_f)
    dw_pieces = _wgrads([dgdn, dz, dfox, dfg, dps], f["h1"], "dw_in")
    dw_in = _merge_dw_in(*dw_pieces)
    own = ("scatter", [_merge_dw_in(*dw_pieces, shards=True), _cut(dconv, 1)]) if scatter_own else None
    grad_x, dn1w, *sent = _inproj_bwd(x, vp["n1w"], dx2, dgdn, dz, dfox, dfg, dps, w_t, ws_t, own)
    small = dict(dn1w=dn1w, gsum=gsum, donw=donw, dfnw=dfnw)
    return (grad_x, dw_in, dconv, small, *carried, *sent)


VECTORS = ("norm1_w", "norm2_w", "final_norm_w", "gdn_A_log", "gdn_dt_bias", "gdn_out_norm_w", "fox_f_bias",
           "fox_q_norm_w", "fox_k_norm_w")
VEC_ROWS = 16
LOSS_ROW = len(VECTORS)


def _pack_vectors(dn1w, dn2w, dfw, gsum, donw, dfnw, loss):
    d = dn1w.shape[1]

    def body(n1_ref, n2_ref, fw_ref, gs_ref, on_ref, fn_ref, loss_ref, o_ref):
        o_ref[...] = jnp.zeros_like(o_ref)
        o_ref[0:1, :] = n1_ref[...]
        o_ref[1:2, :] = n2_ref[...]
        o_ref[2:3, :] = fw_ref[...]
        o_ref[3:4, 0:HEADS] = gs_ref[0:1, 0:HEADS]
        o_ref[4:5, 0:HEADS] = gs_ref[1:2, 0:HEADS]
        o_ref[5:6, 0:DH] = on_ref[...]
        o_ref[6:7, 0:HEADS] = gs_ref[2:3, 0:HEADS]
        for kind in range(2):
            v = fn_ref[kind, 0]
            for j in range(1, fn_ref.shape[1]):
                v = v + fn_ref[kind, j]
            o_ref[7 + kind:8 + kind, 0:DH] = v[:, :DH] + v[:, DH:]
        o_ref[LOSS_ROW:LOSS_ROW + 1, 0:1] = loss_ref[...]

    return pl.pallas_call(body, name="pack_vectors", out_shape=jax.ShapeDtypeStruct((VEC_ROWS, d), F32),
                          compiler_params=_params())(dn1w, dn2w, dfw, gsum, donw, dfnw, loss)


def _late_grads(f, h2, act, dgate, dup, dx3, dx2):
    dw_gate, dw_up = _wgrads([dgate, dup], h2, "dw_gate_up")
    (dw_down,) = _wgrads([act], dx3, "dw_down")
    return {"w_out": jnp.concatenate(_wgrads([f["mixg"], f["mixf"]], dx2, "dw_out"), axis=0),
            "w_ffn_gate": dw_gate, "w_ffn_up": dw_up, "w_ffn_down": dw_down}


def _local_step(x, tgt, p, w_in_t, conv_w, wo, wg_t, wu_t, wd):
    vp = _vector_params(p)
    ws_t = _small_rows(w_in_t)
    f = _mixer_forward(x, vp, w_in_t, ws_t, conv_w)
    (h2, act, dgate, dup, dx3, dx2, dmixg, dmixf, dn2w, dfw, loss) = _tail(
        x, f["mixg"], f["mixf"], tgt, wo, vp["n2w"], wg_t, wu_t, wd, vp["fw"])
    grad_x, dw_in, dconv, small = _mixer_backward(x, vp, w_in_t, ws_t, conv_w, f, dx2, dmixg, dmixf)
    grads = {"w_in": dw_in, "gdn_conv_w": dconv, **_late_grads(f, h2, act, dgate, dup, dx3, dx2)}
    vec = _pack_vectors(small["dn1w"], dn2w, dfw, small["gsum"], small["donw"], small["dfnw"], loss)
    return grad_x, grads, vec


def _my_place():
    return lax.axis_index("x"), lax.axis_index("y"), lax.axis_index("c")


def _peers():
    x, y, c = _my_place()
    peers = []
    for k in range(1, N_DEV):
        px = 1 - x if k & 4 else x
        py = 1 - y if k & 2 else y
        pc = 1 - c if k & 1 else c
        peers.append(((px, py, pc), 4 * px + 2 * py + pc))
    return 4 * x + 2 * y + c, peers


def _spread_copies(kind, srcs, lands, send_sems, recv_sems):
    me, peers = _peers()
    kinds = [kind] * len(srcs) if isinstance(kind, str) else kind
    remote, local = [], []
    for i, (kd, src, land) in enumerate(zip(kinds, srcs, lands)):
        for k, (dev, idx) in enumerate(peers):
            remote.append(pltpu.make_async_remote_copy(
                src_ref=src if kd == "gather" else src.at[idx], dst_ref=land.at[me],
                send_sem=send_sems.at[i * (N_DEV - 1) + k], recv_sem=recv_sems.at[i * (N_DEV - 1) + k],
                device_id=dev, device_id_type=MESH))
        local.append((src if kd == "gather" else src.at[me], land.at[me]))
    return remote, local


def _gather_two_level(arrays, name):
    n = len(arrays)

    def body(*refs):
        srcs, lands = refs[:n], refs[n:2 * n]
        send_sems, recv_sems, local_sems = refs[2 * n:]
        x, y, c = _my_place()
        me, sibling = (x, y, c), (x, y, 1 - c)
        chips = [(1 - x, y), (x, 1 - y), (1 - x, 1 - y)]

        def index(p):
            return 4 * p[0] + 2 * p[1] + p[2]

        def copy(i, k, block, to, src=None):
            blk = lands[i].at[index(block)]
            return pltpu.make_async_remote_copy(
                src_ref=blk if src is None else src, dst_ref=blk, send_sem=send_sems.at[7 * i + k],
                recv_sem=recv_sems.at[7 * i + k], device_id=to, device_id_type=MESH)

        mine = [pltpu.make_async_copy(srcs[i], lands[i].at[index(me)], local_sems.at[i]) for i in range(n)]
        for cp in mine:
            cp.start()
        first = []
        for i in range(n):
            first.append(copy(i, 0, me, sibling, src=srcs[i]))
            first += [copy(i, 1 + j, me, (*chip, c), src=srcs[i]) for j, chip in enumerate(chips)]
        for cp in first:
            cp.start()
        passed = []
        for i in range(n):
            for j, chip in enumerate(chips):
                copy(i, 1 + j, (*chip, c), me).wait_recv()
                passed.append(copy(i, 4 + j, (*chip, c), sibling))
                passed[-1].start()
        for i in range(n):
            copy(i, 0, sibling, me).wait_recv()
            for j, chip in enumerate(chips):
                copy(i, 4 + j, (*chip, 1 - c), me).wait_recv()
        for cp in first + passed:
            cp.wait_send()
        for cp in mine:
            cp.wait()

    return pl.pallas_call(
        body, name=name,
        out_shape=[jax.ShapeDtypeStruct((N_DEV,) + a.shape, a.dtype) for a in arrays],
        in_specs=[pl.BlockSpec(memory_space=pl.ANY)] * n, out_specs=[pl.BlockSpec(memory_space=pl.ANY)] * n,
        scratch_shapes=[pltpu.SemaphoreType.DMA((7 * n,)), pltpu.SemaphoreType.DMA((7 * n,)),
                        pltpu.SemaphoreType.DMA((n,))],
    )(*arrays)


def _land_shape(kind, a):
    return (N_DEV,) + a.shape if kind == "gather" else a.shape


HBM = pl.BlockSpec(memory_space=pltpu.HBM)
SEM = pl.BlockSpec(memory_space=pltpu.SEMAPHORE)


def _hbm(a):
    return pltpu.with_memory_space_constraint(a, pltpu.HBM)


def _carry_operands(kind, arrays):
    n = len(arrays)
    kinds = [kind] * n if isinstance(kind, str) else kind
    lands = [lax.empty(_land_shape(kd, a), a.dtype) for kd, a in zip(kinds, arrays)]
    sems = [pltpu.SemaphoreType.DMA((n * (N_DEV - 1),))] * 2
    return ([_hbm(a) for a in list(arrays) + lands], [HBM] * (2 * n),
            sems + [pltpu.HBM(a.shape, a.dtype) for a in list(arrays) + lands], [SEM] * 2 + [HBM] * (2 * n),
            pltpu.SemaphoreType.DMA((n,)))


def _carry_step(kind, in_refs, send_sems, recv_sems, local_sems, step, n_steps):
    n = len(in_refs) // 2
    remote, local = _spread_copies(kind, in_refs[:n], in_refs[n:], send_sems, recv_sems)
    copies = [pltpu.make_async_copy(s, d, local_sems.at[i]) for i, (s, d) in enumerate(local)]
    per_step = -(-len(remote) // n_steps)
    for s in range(-(-len(remote) // per_step)):
        @pl.when(step == s)
        def _():
            for cp in remote[s * per_step:(s + 1) * per_step]:
                cp.start()
            if s == 0:
                for cp in copies:
                    cp.start()

    @pl.when(step == n_steps - 1)
    def _():
        for cp in copies:
            cp.wait()


def _carry_state(extra_out):
    n = (len(extra_out) - 2) // 2
    return list(extra_out[2:2 + n]), list(extra_out[2 + n:]), extra_out[0], extra_out[1]


def _spread_start(arrays, kind, name):
    n = len(arrays)

    def body(*refs):
        srcs, lands = refs[:n], refs[n:2 * n]
        send_sems, recv_sems = refs[2 * n], refs[2 * n + 1]
        token = refs[4 * n + 2]
        local_sems = refs[4 * n + 3]
        remote, local = _spread_copies(kind, srcs, lands, send_sems, recv_sems)
        for cp in remote:
            cp.start()
        copies = [pltpu.make_async_copy(s, d, local_sems.at[i]) for i, (s, d) in enumerate(local)]
        for cp in copies:
            cp.start()
        for cp in copies:
            cp.wait()
        token[...] = jnp.zeros_like(token)

    sems = (pltpu.SemaphoreType.DMA((n * (N_DEV - 1),)),) * 2
    kinds = [kind] * n if isinstance(kind, str) else kind
    lands = [lax.empty(_land_shape(kd, a), a.dtype) for kd, a in zip(kinds, arrays)]
    out = pl.pallas_call(
        body, name=name,
        out_shape=sems + tuple(pltpu.HBM(a.shape, a.dtype) for a in list(arrays) + lands)
        + (jax.ShapeDtypeStruct((8, LANE), F32),),
        in_specs=[HBM] * (2 * n), out_specs=tuple([SEM] * 2 + [HBM] * (2 * n) + [pl.BlockSpec(memory_space=pltpu.VMEM)]),
        input_output_aliases={j: 2 + j for j in range(2 * n)},
        scratch_shapes=[pltpu.SemaphoreType.DMA((n,))],
        compiler_params=pltpu.CompilerParams(has_side_effects=pltpu.SideEffectType.DATAFLOW_SIDE_EFFECTING),
    )(*[_hbm(a) for a in arrays], *[_hbm(a) for a in lands])
    return (list(out[2:2 + n]), list(out[2 + n:2 + 2 * n]), out[0], out[1]), out[-1]


def _spread_wait(state, kind, after, name):
    srcs, lands, send_sems, recv_sems = state
    n = len(srcs)
    after = list(after) if isinstance(after, (list, tuple)) else [after]

    def body(*refs):
        remote, _ = _spread_copies(kind, refs[:n], refs[n:2 * n], refs[2 * n], refs[2 * n + 1])
        for cp in remote:
            cp.wait_send()
        for cp in remote:
            cp.wait_recv()

    out = pl.pallas_call(
        body, name=name,
        out_shape=tuple(pltpu.HBM(a.shape, a.dtype) for a in srcs + lands),
        in_specs=[HBM] * (2 * n) + [SEM, SEM] + [pl.BlockSpec(memory_space=pl.ANY)] * len(after),
        out_specs=tuple([HBM] * (2 * n)),
        input_output_aliases={j: j for j in range(2 * n)},
        compiler_params=pltpu.CompilerParams(has_side_effects=pltpu.SideEffectType.DATAFLOW_SIDE_EFFECTING),
    )(*srcs, *lands, send_sems, recv_sems, *after)
    return list(out[n:])


ADAM_ROWS = 128
ADAM_COLS = 256


def _adam_math(g, w, m, v):
    nm = ADAM_B1 * m + (1.0 - ADAM_B1) * g
    nv = ADAM_B2 * v + (1.0 - ADAM_B2) * (g * g)
    m_hat = nm / (1.0 - ADAM_B1 ** ADAM_STEP)
    v_hat = nv / (1.0 - ADAM_B2 ** ADAM_STEP)
    return -ADAM_LR * (m_hat / (jnp.sqrt(v_hat) + ADAM_EPS) + ADAM_WD * w), nm, nv


def _sum_parts(p_ref):
    g = p_ref[0].astype(F32)
    for s in range(1, N_DEV):
        g = g + p_ref[s].astype(F32)
    return g


def _adam_matrix(parts, w, m, v, name):
    _, r, c = w.shape
    rb = ADAM_ROWS if r % ADAM_ROWS == 0 else r
    cb = ADAM_COLS if (rb == r and c % ADAM_COLS == 0) else c

    def body(p_ref, w_ref, m_ref, v_ref, g_ref, d_ref, nm_ref, nv_ref):
        g = _sum_parts(p_ref)
        g_ref[0] = g
        d_ref[0], nm_ref[0], nv_ref[0] = _adam_math(g, w_ref[0], m_ref[0], v_ref[0])

    blk = pl.BlockSpec((1, rb, cb), lambda i, j: (0, i, j))
    return pl.pallas_call(
        body, name=name, grid=(r // rb, c // cb),
        in_specs=[pl.BlockSpec((N_DEV, rb, cb), lambda i, j: (0, i, j)), blk, blk, blk],
        out_specs=[blk] * 4, out_shape=[jax.ShapeDtypeStruct(w.shape, F32)] * 4,
        compiler_params=_params(("arbitrary", "arbitrary")),
    )(parts, w, m, v)


def _adam_vectors(parts, ws, ms, vs):
    nv = len(ws)

    def body(*refs):
        p_ref = refs[0]
        w_refs, m_refs, v_refs = refs[1:1 + nv], refs[1 + nv:1 + 2 * nv], refs[1 + 2 * nv:1 + 3 * nv]
        outs = refs[1 + 3 * nv:]
        g_all = _sum_parts(p_ref)
        for i in range(nv):
            n = w_refs[i].shape[1]
            g = g_all[i:i + 1, 0:n]
            d, nm, nvv = _adam_math(g, w_refs[i][...], m_refs[i][...], v_refs[i][...])
            outs[i][...] = g
            outs[nv + i][...] = d
            outs[2 * nv + i][...] = nm
            outs[3 * nv + i][...] = nvv
        outs[4 * nv][...] = g_all[LOSS_ROW:LOSS_ROW + 1, 0:1]

    shapes = [jax.ShapeDtypeStruct(a.shape, F32) for a in ws]
    out = pl.pallas_call(body, name="adam_vectors", out_shape=shapes * 4 + [jax.ShapeDtypeStruct((1, 1), F32)],
                         compiler_params=_params())(parts, *ws, *ms, *vs)
    return out[:nv], out[nv:2 * nv], out[2 * nv:3 * nv], out[3 * nv:4 * nv], out[4 * nv]


MATRICES = (("w_in", 1), ("gdn_conv_w", 1), ("w_out", 0), ("w_ffn_gate", 1), ("w_ffn_up", 1), ("w_ffn_down", 0))
TRANSPOSED = ("w_in", "w_ffn_gate", "w_ffn_up")
WEIGHTS = ("norm1_w", "w_in", "gdn_conv_w", "gdn_A_log", "gdn_dt_bias", "gdn_out_norm_w", "fox_f_bias", "fox_q_norm_w",
           "fox_k_norm_w", "w_out", "norm2_w", "w_ffn_gate", "w_ffn_up", "w_ffn_down", "final_norm_w")


def _join(blocks, axis):
    _, r, c = blocks.shape
    if axis == 0:
        return blocks.reshape(N_DEV * r, c)
    return blocks.transpose(1, 0, 2).reshape(r, N_DEV * c)


def _cut(full, axis):
    r, c = full.shape
    if axis == 0:
        return full.reshape(N_DEV, r // N_DEV, c)
    return full.reshape(r, N_DEV, c // N_DEV).transpose(1, 0, 2)


def kernel(x, norm1_w, w_in, gdn_conv_w, gdn_A_log, gdn_dt_bias, gdn_out_norm_w, fox_f_bias, fox_q_norm_w, fox_k_norm_w, w_out, norm2_w, w_ffn_gate, w_ffn_up, w_ffn_down, final_norm_w, loss_target, m_norm1_w, m_w_in, m_gdn_conv_w, m_gdn_A_log, m_gdn_dt_bias, m_gdn_out_norm_w, m_fox_f_bias, m_fox_q_norm_w, m_fox_k_norm_w, m_w_out, m_norm2_w, m_w_ffn_gate, m_w_ffn_up, m_w_ffn_down, m_final_norm_w, v_norm1_w, v_w_in, v_gdn_conv_w, v_gdn_A_log, v_gdn_dt_bias, v_gdn_out_norm_w, v_fox_f_bias, v_fox_q_norm_w, v_fox_k_norm_w, v_w_out, v_norm2_w, v_w_ffn_gate, v_w_ffn_up, v_w_ffn_down, v_final_norm_w):
    w = dict(norm1_w=norm1_w, w_in=w_in, gdn_conv_w=gdn_conv_w, gdn_A_log=gdn_A_log, gdn_dt_bias=gdn_dt_bias,
             gdn_out_norm_w=gdn_out_norm_w, fox_f_bias=fox_f_bias, fox_q_norm_w=fox_q_norm_w, fox_k_norm_w=fox_k_norm_w,
             w_out=w_out, norm2_w=norm2_w, w_ffn_gate=w_ffn_gate, w_ffn_up=w_ffn_up, w_ffn_down=w_ffn_down,
             final_norm_w=final_norm_w)
    m = dict(norm1_w=m_norm1_w, w_in=m_w_in, gdn_conv_w=m_gdn_conv_w, gdn_A_log=m_gdn_A_log, gdn_dt_bias=m_gdn_dt_bias,
             gdn_out_norm_w=m_gdn_out_norm_w, fox_f_bias=m_fox_f_bias, fox_q_norm_w=m_fox_q_norm_w,
             fox_k_norm_w=m_fox_k_norm_w, w_out=m_w_out, norm2_w=m_norm2_w, w_ffn_gate=m_w_ffn_gate,
             w_ffn_up=m_w_ffn_up, w_ffn_down=m_w_ffn_down, final_norm_w=m_final_norm_w)
    v = dict(norm1_w=v_norm1_w, w_in=v_w_in, gdn_conv_w=v_gdn_conv_w, gdn_A_log=v_gdn_A_log, gdn_dt_bias=v_gdn_dt_bias,
             gdn_out_norm_w=v_gdn_out_norm_w, fox_f_bias=v_fox_f_bias, fox_q_norm_w=v_fox_q_norm_w,
             fox_k_norm_w=v_fox_k_norm_w, w_out=v_w_out, norm2_w=v_norm2_w, w_ffn_gate=v_w_ffn_gate,
             w_ffn_up=v_w_ffn_up, w_ffn_down=v_w_ffn_down, final_norm_w=v_final_norm_w)
    late = ("w_out", "w_ffn_gate", "w_ffn_up", "w_ffn_down")
    xs, tgt = x[0], loss_target[0]
    vp = _vector_params({n: w[n] for n in VECTORS})

    def rows_of(d, n):
        return d[n].transpose(0, 2, 1) if n in TRANSPOSED else d[n]

    wr, mr, vr = ({n: rows_of(d, n) for n, _ in MATRICES} for d in (w, m, v))

    w_in_blocks, conv_blocks = _gather_two_level([wr["w_in"][0].astype(WIRE), w["gdn_conv_w"][0]], "gather_in")
    w_t = _join(w_in_blocks, 0)
    conv_w = _join(conv_blocks, 1)
    f = _mixer_forward(xs, vp, w_t, _small_rows(w_t), conv_w, ("gather", [wr[n][0].astype(WIRE) for n in late]))
    full = {n: _join(b, 0) for n, b in zip(late, _spread_wait(f["carried"], "gather", f["mixg"], "gather_late_wait"))}

    (h2, act, dgate, dup, dx3, dx2, dmixg, dmixf, dn2w, dfw, loss) = _tail(
        xs, f["mixg"], f["mixf"], tgt, full["w_out"], vp["n2w"], full["w_ffn_gate"], full["w_ffn_up"],
        full["w_ffn_down"], vp["fw"])
    dlate = _late_grads(f, h2, act, dgate, dup, dx3, dx2)

    grad_x, dw_in, dconv, small, state_grads, state_own = _mixer_backward(
        xs, vp, w_t, _small_rows(w_t), conv_w, f, dx2, dmixg, dmixf, ("scatter", [_cut(dlate[n], 0) for n in late]),
        scatter_own=True)
    vec = _pack_vectors(small["dn1w"], dn2w, dfw, small["gsum"], small["donw"], small["dfnw"], loss)

    state_vec, token = _spread_start([vec], "gather", "vectors_start")
    parts = dict(zip(late, _spread_wait(state_grads, "scatter", token, "grads_late_wait")))
    results = [{}, {}, {}, {}]

    updated = {}

    def update(n):
        out = _adam_matrix(parts[n], wr[n], mr[n], vr[n], "adam_" + n)
        updated[n] = out[0]
        for d, a in zip(results, out):
            d[n] = a.transpose(0, 2, 1) if n in TRANSPOSED else a

    for n in late:
        update(n)
    parts["w_in"], parts["gdn_conv_w"] = _spread_wait(state_own, "scatter", [updated[n] for n in late], "own_wait")
    (parts_vec,) = _spread_wait(state_vec, "gather", parts["w_in"], "vectors_wait")
    update("w_in")
    update("gdn_conv_w")
    row = lambda a: a.reshape(1, -1)
    *vec_out, total_loss = _adam_vectors(parts_vec, [row(w[n]) for n in VECTORS], [row(m[n]) for n in VECTORS],
                                         [row(v[n]) for n in VECTORS])
    for d, arrs in zip(results, vec_out):
        for n, a in zip(VECTORS, arrs):
            d[n] = a.reshape(w[n].shape)
    return (total_loss[0, 0], grad_x[None], *[d[n] for d in results for n in WEIGHTS])
```

```python
import functools

import jax
import jax.numpy as jnp
from jax import lax
from jax.experimental import pallas as pl
from jax.experimental.pallas import tpu as pltpu

F32 = jnp.float32
MXU = jnp.bfloat16
WIRE = jnp.bfloat16
HI = lax.Precision.HIGHEST
EPS = 1e-6

N_DEV = 8
HEADS = 8
DH = 64
WIDTH = HEADS * DH
CHUNK = 64
LANE = 128
ROW_ALIGN = 16
TB = 256
QB = 256
VMEM_LIMIT = 60 * 1024 * 1024

ADAM_LR = 0.001
ADAM_B1 = 0.9
ADAM_B2 = 0.999
ADAM_EPS = 1e-08
ADAM_WD = 0.01
ADAM_STEP = 10

MESH = pl.DeviceIdType.MESH


def _params(sem=None):
    return pltpu.CompilerParams(dimension_semantics=sem, vmem_limit_bytes=VMEM_LIMIT)


def _resident(shape):
    n = len(shape)
    return pl.BlockSpec(shape, lambda *_: (0,) * n, pipeline_mode=pl.Buffered(1))


def _dot(a, b):
    return jnp.dot(a.astype(MXU), b.astype(MXU), preferred_element_type=F32)


def _dot_nt(a, b):
    return lax.dot_general(a.astype(MXU), b.astype(MXU), (((1,), (1,)), ((), ())), preferred_element_type=F32)


def _dot_tn(a, b):
    return lax.dot_general(a.astype(MXU), b.astype(MXU), (((0,), (0,)), ((), ())), preferred_element_type=F32)


def _hdot(a, b):
    return jnp.dot(a, b, precision=HI, preferred_element_type=F32)


def _hdot_nt(a, b):
    return lax.dot_general(a, b, (((1,), (1,)), ((), ())), precision=HI, preferred_element_type=F32)


def _hdot_tn(a, b):
    return lax.dot_general(a, b, (((0,), (0,)), ((), ())), precision=HI, preferred_element_type=F32)


def _sigmoid(x):
    return 0.5 * jnp.tanh(0.5 * x) + 0.5


def _softplus(x):
    return jnp.maximum(x, 0.0) + jnp.log(1.0 + jnp.exp(-jnp.abs(x)))


def _head_sum_matrix():
    ri = lax.broadcasted_iota(jnp.int32, (LANE, LANE), 0) // DH
    ci = lax.broadcasted_iota(jnp.int32, (LANE, LANE), 1) // DH
    return (ri == ci).astype(F32)


def _group_sum(a, ones_matrix):
    hi = a.astype(jnp.bfloat16)
    lo = (a - hi.astype(F32)).astype(jnp.bfloat16)
    m = ones_matrix.astype(jnp.bfloat16)
    return jnp.dot(hi, m, preferred_element_type=F32) + jnp.dot(lo, m, preferred_element_type=F32)


def _shift_down(x, s):
    return pltpu.roll(x, s, 0)


def _shift_up(x, s):
    return pltpu.roll(x, x.shape[0] - s, 0)


ROWS_A = 4 * WIDTH
ROWS_B = ROWS_A + 2 * HEADS
ROWS_C = ROWS_B + 4 * WIDTH


def _small_rows(w_t):
    return jnp.concatenate([w_t[ROWS_A:ROWS_B], w_t[ROWS_C:], jnp.zeros((LANE - 3 * HEADS, w_t.shape[1]), w_t.dtype)])


def _inproj(x, n1w, w_t, ws_t):
    t, d = x.shape
    tb = min(TB, t)

    def body(x_ref, nw_ref, wt_ref, ws_ref, h_ref, pm_ref, ps_ref):
        xv = x_ref[...]
        r = lax.rsqrt(jnp.mean(xv * xv, axis=-1, keepdims=True) + EPS)
        h = (xv * r * nw_ref[...]).astype(MXU)
        h_ref[...] = h
        pm_ref[:, 0:ROWS_A] = _dot_nt(h, wt_ref[0:ROWS_A, :])
        pm_ref[:, ROWS_A:2 * ROWS_A] = _dot_nt(h, wt_ref[ROWS_B:ROWS_C, :])
        ps_ref[...] = _dot_nt(h, ws_ref[...])

    return pl.pallas_call(
        body, name="inproj", grid=(t // tb,),
        in_specs=[pl.BlockSpec((tb, d), lambda i: (i, 0)), _resident((1, d)), _resident(w_t.shape), _resident((LANE, d))],
        out_specs=[pl.BlockSpec((tb, d), lambda i: (i, 0)), pl.BlockSpec((tb, 2 * ROWS_A), lambda i: (i, 0)),
                   pl.BlockSpec((tb, LANE), lambda i: (i, 0))],
        out_shape=[jax.ShapeDtypeStruct((t, d), MXU), jax.ShapeDtypeStruct((t, 2 * ROWS_A), F32),
                   jax.ShapeDtypeStruct((t, LANE), F32)],
        compiler_params=_params(("arbitrary",)),
    )(x, n1w, w_t, ws_t)


def _inproj_bwd(x, n1w, dx2, dgdn, dz, dfox, dfg, dps, w_t, ws_t, carry=None):
    t, d = x.shape
    tb = min(TB, t)
    w3 = 3 * WIDTH
    c_in, c_in_specs, c_out_shape, c_out_specs, c_sem = _carry_operands(*carry) if carry else ([], [], [], [], None)
    n_c = len(c_in)

    def body(*refs):
        x_ref, nw_ref, dx2_ref, dgdn_ref, dz_ref, dfox_ref, dfg_ref, dps_ref, wm_ref, ws_ref = refs[:10]
        gx_ref, dnw_ref = refs[10 + n_c:12 + n_c]
        if carry:
            _carry_step(carry[0], refs[10:10 + n_c], refs[12 + n_c], refs[13 + n_c], refs[-1], pl.program_id(0), t // tb)
        dh = _dot(dgdn_ref[...], wm_ref[0:w3, :])
        dh += _dot(dz_ref[...], wm_ref[w3:ROWS_A, :])
        dh += _dot(dfox_ref[...], wm_ref[ROWS_B:ROWS_B + w3, :])
        dh += _dot(dfg_ref[...], wm_ref[ROWS_B + w3:ROWS_C, :])
        dh += _dot(dps_ref[...], ws_ref[...])
        xv = x_ref[...]
        r = lax.rsqrt(jnp.mean(xv * xv, axis=-1, keepdims=True) + EPS)
        xn = xv * r

        @pl.when(pl.program_id(0) == 0)
        def _():
            dnw_ref[...] = jnp.zeros_like(dnw_ref)

        dnw_ref[...] += jnp.sum(dh * xn, axis=0, keepdims=True)
        g = dh * nw_ref[...]
        gx_ref[0] = dx2_ref[...] + r * (g - xn * jnp.mean(g * xn, axis=-1, keepdims=True))

    def tok(n):
        return pl.BlockSpec((tb, n), lambda i: (i, 0))

    out = pl.pallas_call(
        body, name="inproj_bwd", grid=(t // tb,),
        in_specs=[tok(d), _resident((1, d)), tok(d), tok(w3), tok(WIDTH), tok(w3), tok(WIDTH), tok(LANE),
                  _resident(w_t.shape), _resident(ws_t.shape)] + c_in_specs,
        out_specs=[pl.BlockSpec((1, tb, d), lambda i: (0, i, 0)), pl.BlockSpec((1, d), lambda i: (0, 0))] + c_out_specs,
        out_shape=[jax.ShapeDtypeStruct((1, t, d), F32), jax.ShapeDtypeStruct((1, d), F32)] + c_out_shape,
        input_output_aliases={10 + j: 4 + j for j in range(n_c)},
        scratch_shapes=[c_sem] if carry else [],
        compiler_params=_params(("arbitrary",)),
    )(x, n1w, dx2, dgdn, dz, dfox, dfg, dps, w_t, ws_t, *c_in)
    return (out[0], out[1], _carry_state(out[2:])) if carry else tuple(out)


def _gate_lanes(shape):
    lane = lax.broadcasted_iota(jnp.int32, shape, 1)
    return lane < HEADS, (lane >= HEADS) & (lane < 2 * HEADS), (lane >= 2 * HEADS) & (lane < 3 * HEADS)


def _block_masks():
    ri = lax.broadcasted_iota(jnp.int32, (LANE, LANE), 0)
    ci = lax.broadcasted_iota(jnp.int32, (LANE, LANE), 1)
    same = (ri // CHUNK) == (ci // CHUNK)
    return ((ri >= ci).astype(F32), (ri <= ci).astype(F32), (same & (ri >= ci)).astype(F32),
            (same & (ri <= ci)).astype(F32), same.astype(F32))


def _gates(ps, gparams):
    t = ps.shape[0]
    nb = t // LANE

    def body(ps_ref, gp_ref, out_ref, run_ref, tot_ref, runt_ref):
        p = ps_ref[...]
        is_b, is_a, is_f = _gate_lanes(p.shape)
        z = p + gp_ref[0:1, :]
        neg_exp_a = -jnp.exp(gp_ref[1:2, :])
        glog = neg_exp_a * _softplus(z)
        logf = -_softplus(-z)
        out_ref[...] = jnp.where(is_b, _sigmoid(p), jnp.where(is_a, glog, jnp.where(is_f, logf, 0.0)))
        tril, _, tril_c, _, same_c = _block_masks()
        off = jnp.zeros((1, LANE), F32)
        for b in range(nb):
            rows = slice(b * LANE, (b + 1) * LANE)
            blk = out_ref[rows, :]
            ga = jnp.where(is_a[:LANE], blk, 0.0)
            fb = _hdot(tril, jnp.where(is_f[:LANE], blk, 0.0)) + off
            run = fb + _hdot(tril_c, ga)
            run_ref[rows, :] = run
            runt_ref[:, rows] = run.T
            tot_ref[rows, :] = _hdot(same_c, ga)
            off = fb[LANE - 1:LANE, :]

    return pl.pallas_call(
        body, name="gates",
        out_shape=[jax.ShapeDtypeStruct((t, LANE), F32)] * 3 + [jax.ShapeDtypeStruct((LANE, t), F32)],
        compiler_params=_params(),
    )(ps, gparams)


def _gates_bwd(ps, gparams, dcol_g, dtot_g, dcol_f, drow_g, drow_f):
    t = ps.shape[0]
    nb = t // LANE

    def body(ps_ref, gp_ref, dcg_ref, dtg_ref, dcf_ref, drg_ref, drf_ref, dps_ref, sums_ref, dl_ref, d0_ref, tr_ref):
        p = ps_ref[...]
        is_b, is_a, is_f = _gate_lanes(p.shape)
        _, triu, _, triu_c, same_c = _block_masks()
        tr_ref[...] = jnp.zeros_like(tr_ref)
        off = jnp.zeros((1, LANE), F32)
        for b in reversed(range(nb)):
            rows = slice(b * LANE, (b + 1) * LANE)
            tr_ref[HEADS:2 * HEADS, :] = drg_ref[:, rows]
            tr_ref[2 * HEADS:3 * HEADS, :] = drf_ref[:, rows]
            d = dcg_ref[rows, :] + dcf_ref[rows, :] + tr_ref[...].T
            d0_ref[rows, :] = d
            dlf = _hdot(triu, jnp.where(is_f[:LANE], d, 0.0)) + off
            dla = (_hdot(triu_c, jnp.where(is_a[:LANE], d, 0.0))
                   + _hdot(same_c, jnp.where(is_a[:LANE], dtg_ref[rows, :], 0.0)))
            dl_ref[rows, :] = dlf + dla
            off = dlf[0:1, :]
        z = p + gp_ref[0:1, :]
        neg_exp_a = -jnp.exp(gp_ref[1:2, :])
        sb = _sigmoid(p)
        glog = neg_exp_a * _softplus(z)
        dl = dl_ref[...]
        dp = jnp.where(is_b, d0_ref[...] * sb * (1.0 - sb),
                       jnp.where(is_a, dl * neg_exp_a * _sigmoid(z), jnp.where(is_f, dl * _sigmoid(-z), 0.0)))
        dps_ref[...] = dp
        s_a = jnp.sum(jnp.where(is_a, dl * glog, 0.0), axis=0, keepdims=True)
        s_p = jnp.sum(jnp.where(is_b, 0.0, dp), axis=0, keepdims=True)
        row = lax.broadcasted_iota(jnp.int32, (8, LANE), 0)
        from_a = pltpu.roll(jnp.where(row == 0, s_a, jnp.where(row == 1, s_p, 0.0)), LANE - HEADS, 1)
        from_f = pltpu.roll(jnp.where(row == 2, s_p, 0.0), LANE - 2 * HEADS, 1)
        lane = lax.broadcasted_iota(jnp.int32, (8, LANE), 1)
        sums_ref[...] = jnp.where(lane < HEADS, from_a + from_f, 0.0)

    return pl.pallas_call(
        body, name="gates_bwd",
        out_shape=[jax.ShapeDtypeStruct((t, LANE), F32), jax.ShapeDtypeStruct((8, LANE), F32)],
        scratch_shapes=[pltpu.VMEM((t, LANE), F32), pltpu.VMEM((t, LANE), F32), pltpu.VMEM((LANE, LANE), F32)],
        compiler_params=_params(),
    )(ps, gparams, dcol_g, dtot_g, dcol_f, drow_g, drow_f)


def _conv(xv, w):
    acc = w[3:4, :] * xv
    for s in range(1, 4):
        acc += w[3 - s:4 - s, :] * _shift_down(xv, s)
    return acc


PREP_ROWS = 256
HALO = 8
PREP_UNROLL = 2


def _tile_loop(n, tile, init):
    if n % PREP_UNROLL:
        return lax.fori_loop(0, n, tile, init)

    def trip(g, carry):
        for u in range(PREP_UNROLL):
            carry = tile(g * PREP_UNROLL + u, carry)
        return carry

    return lax.fori_loop(0, n // PREP_UNROLL, trip, init)


def _tile_rows(r):
    return pl.ds(pl.multiple_of(r * PREP_ROWS, PREP_ROWS), PREP_ROWS)


def _gdn_prep(pm, conv_w):
    t = pm.shape[0]
    nj = WIDTH // LANE
    win = PREP_ROWS + HALO

    def body(x_ref, w_ref, o_ref, xp_ref):
        kind = pl.program_id(0)
        xp_ref[0:HALO, :] = jnp.zeros((HALO, LANE), F32)
        xp_ref[HALO:, :] = x_ref[...]
        w = w_ref[...]
        hs = _head_sum_matrix()

        def tile(r, _, normed):
            xw = xp_ref[pl.ds(pl.multiple_of(r * PREP_ROWS, PREP_ROWS), win), :]
            acc = _conv(xw, w)[HALO:]
            out = acc * _sigmoid(acc)
            if normed:
                out = out * lax.rsqrt(_group_sum(out * out, hs) + EPS)
            o_ref[0, 0, _tile_rows(r), :] = out[:, :DH]
            o_ref[0, 1, _tile_rows(r), :] = out[:, DH:]
            return 0

        @pl.when(kind < 2)
        def _():
            _tile_loop(t // PREP_ROWS, functools.partial(tile, normed=True), 0)

        @pl.when(kind == 2)
        def _():
            _tile_loop(t // PREP_ROWS, functools.partial(tile, normed=False), 0)

    return pl.pallas_call(
        body, name="gdn_prep", grid=(3, nj),
        in_specs=[pl.BlockSpec((t, LANE), lambda i, j: (0, i * nj + j)),
                  pl.BlockSpec((4, LANE), lambda i, j: (0, i * nj + j))],
        out_specs=pl.BlockSpec((1, 2, t, DH), lambda i, j: (i, j, 0, 0)),
        out_shape=jax.ShapeDtypeStruct((3, HEADS, t, DH), F32),
        scratch_shapes=[pltpu.VMEM((t + HALO, LANE), F32)],
        compiler_params=_params(("arbitrary", "arbitrary")),
    )(pm, conv_w)


def _gdn_prep_bwd(pm, conv_w, dqkv):
    t = pm.shape[0]
    nj = WIDTH // LANE
    win = PREP_ROWS + 2 * HALO

    def body(x_ref, w_ref, d_ref, dx_ref, dw_ref, xp_ref, dp_ref):
        kind = pl.program_id(0)
        zeros = jnp.zeros((HALO, LANE), F32)
        for ref in (xp_ref, dp_ref):
            ref[0:HALO, :] = zeros
            ref[HALO + t:, :] = zeros
        xp_ref[HALO:HALO + t, :] = x_ref[...]
        dp_ref[HALO:HALO + t, 0:DH] = d_ref[0, 0]
        dp_ref[HALO:HALO + t, DH:] = d_ref[0, 1]
        w = w_ref[...]
        hs = _head_sum_matrix()
        rows = lax.broadcasted_iota(jnp.int32, (win, LANE), 0)
        in_tile = (rows >= HALO) & (rows < HALO + PREP_ROWS)

        def tile(r, dw, normed):
            start = pl.multiple_of(r * PREP_ROWS, PREP_ROWS)
            xw = xp_ref[pl.ds(start, win), :]
            dy = dp_ref[pl.ds(start, win), :]
            acc = _conv(xw, w)
            sg = _sigmoid(acc)
            if normed:
                y = acc * sg
                rn = lax.rsqrt(_group_sum(y * y, hs) + EPS)
                yn = y * rn
                dy = rn * (dy - yn * _group_sum(dy * yn, hs))
            dacc = dy * sg * (1.0 + acc * (1.0 - sg))
            dx = w[3:4, :] * dacc
            for s in range(1, 4):
                dx += w[3 - s:4 - s, :] * _shift_up(dacc, s)
            dx_ref[_tile_rows(r), :] = dx[HALO:HALO + PREP_ROWS]
            dm = jnp.where(in_tile, dacc, 0.0)
            return tuple(dw[i] + jnp.sum(dm * (xw if i == 3 else _shift_down(xw, 3 - i)), axis=0, keepdims=True)
                         for i in range(4))

        def run(normed):
            dw = _tile_loop(t // PREP_ROWS, functools.partial(tile, normed=normed),
                            tuple(jnp.zeros((1, LANE), F32) for _ in range(4)))
            for i in range(4):
                dw_ref[i:i + 1, :] = dw[i]

        pl.when(kind < 2)(functools.partial(run, True))
        pl.when(kind == 2)(functools.partial(run, False))

    return pl.pallas_call(
        body, name="gdn_prep_bwd", grid=(3, nj),
        in_specs=[pl.BlockSpec((t, LANE), lambda i, j: (0, i * nj + j)),
                  pl.BlockSpec((4, LANE), lambda i, j: (0, i * nj + j)),
                  pl.BlockSpec((1, 2, t, DH), lambda i, j: (i, j, 0, 0))],
        out_specs=[pl.BlockSpec((t, LANE), lambda i, j: (0, i * nj + j)),
                   pl.BlockSpec((4, LANE), lambda i, j: (0, i * nj + j))],
        out_shape=[jax.ShapeDtypeStruct((t, 3 * WIDTH), F32), jax.ShapeDtypeStruct((4, 3 * WIDTH), F32)],
        scratch_shapes=[pltpu.VMEM((t + 2 * HALO, LANE), F32), pltpu.VMEM((t + 2 * HALO, LANE), F32)],
        compiler_params=_params(("arbitrary", "arbitrary")),
    )(pm, conv_w, dqkv)


FOX_COL0 = 4 * WIDTH // LANE


def _fox_prep(pm, nw):
    t = pm.shape[0]
    nj = WIDTH // LANE

    def body(x_ref, w_ref, o_ref):
        kind = pl.program_id(0)
        hs = _head_sum_matrix()
        wk = w_ref[pl.ds(kind, 1), :]

        def tile(r, _, normed):
            out = x_ref[_tile_rows(r), :]
            if normed:
                out = out * lax.rsqrt(_group_sum(out * out, hs) * (1.0 / DH) + EPS) * wk
            o_ref[0, 0, _tile_rows(r), :] = out[:, :DH]
            o_ref[0, 1, _tile_rows(r), :] = out[:, DH:]
            return 0

        @pl.when(kind < 2)
        def _():
            _tile_loop(t // PREP_ROWS, functools.partial(tile, normed=True), 0)

        @pl.when(kind == 2)
        def _():
            _tile_loop(t // PREP_ROWS, functools.partial(tile, normed=False), 0)

    return pl.pallas_call(
        body, name="fox_prep", grid=(3, nj),
        in_specs=[pl.BlockSpec((t, LANE), lambda i, j: (0, FOX_COL0 + i * nj + j)),
                  pl.BlockSpec((3, LANE), lambda i, j: (0, 0))],
        out_specs=pl.BlockSpec((1, 2, t, DH), lambda i, j: (i, j, 0, 0)),
        out_shape=jax.ShapeDtypeStruct((3, HEADS, t, DH), F32),
        compiler_params=_params(("arbitrary", "arbitrary")),
    )(pm, nw)


def _fox_prep_bwd(pm, nw, dqkv):
    t = pm.shape[0]
    nj = WIDTH // LANE

    def body(x_ref, w_ref, d_ref, dx_ref, dw_ref):
        kind = pl.program_id(0)
        hs = _head_sum_matrix()
        wk = w_ref[pl.ds(kind, 1), :]

        def tile(r, dw):
            xv = x_ref[_tile_rows(r), :]
            rn = lax.rsqrt(_group_sum(xv * xv, hs) * (1.0 / DH) + EPS)
            xn = xv * rn
            d = jnp.concatenate([d_ref[0, 0, _tile_rows(r), :], d_ref[0, 1, _tile_rows(r), :]], axis=1)
            g = d * wk
            dx_ref[_tile_rows(r), :] = rn * (g - xn * _group_sum(g * xn, hs) * (1.0 / DH))
            return dw + jnp.sum(d * xn, axis=0, keepdims=True)

        def copy_tile(r, _):
            dx_ref[_tile_rows(r), :] = jnp.concatenate([d_ref[0, 0, _tile_rows(r), :], d_ref[0, 1, _tile_rows(r), :]], axis=1)
            return 0

        @pl.when(kind < 2)
        def _():
            dw_ref[0, 0] = _tile_loop(t // PREP_ROWS, tile, jnp.zeros((1, LANE), F32))

        @pl.when(kind == 2)
        def _():
            _tile_loop(t // PREP_ROWS, copy_tile, 0)
            dw_ref[0, 0] = jnp.zeros((1, LANE), F32)

    return pl.pallas_call(
        body, name="fox_prep_bwd", grid=(3, nj),
        in_specs=[pl.BlockSpec((t, LANE), lambda i, j: (0, FOX_COL0 + i * nj + j)),
                  pl.BlockSpec((3, LANE), lambda i, j: (0, 0)),
                  pl.BlockSpec((1, 2, t, DH), lambda i, j: (i, j, 0, 0))],
        out_specs=[pl.BlockSpec((t, LANE), lambda i, j: (0, i * nj + j)),
                   pl.BlockSpec((1, 1, 1, LANE), lambda i, j: (i, j, 0, 0))],
        out_shape=[jax.ShapeDtypeStruct((t, 3 * WIDTH), F32), jax.ShapeDtypeStruct((3, nj, 1, LANE), F32)],
        compiler_params=_params(("arbitrary", "arbitrary")),
    )(pm, nw, dqkv)


SC = 256
CPS = SC // CHUNK
GDN_HP_FWD = 4
GDN_HP_BWD = 2
Q_SCALE = DH ** -0.5


def _sc_masks():
    ri = lax.broadcasted_iota(jnp.int32, (SC, SC), 0)
    ci = lax.broadcasted_iota(jnp.int32, (SC, SC), 1)
    same = (ri // CHUNK) == (ci // CHUNK)
    return same & (ri >= ci), same & (ri > ci), ri == ci


def _unit_lower_inverses(ms, eye):
    invs = [jnp.where(eye, 1.0, 0.0) + m for m in ms]
    for _ in range(5):
        ms = [_dot(m, m) for m in ms]
        invs = [inv + _dot(inv, m) for inv, m in zip(invs, ms)]
    return invs


def _lane_col(blk, lane_idx):
    lane = lax.broadcasted_iota(jnp.int32, blk.shape, 1)
    return jnp.sum(jnp.where(lane == lane_idx, blk, 0.0), axis=1, keepdims=True)


def _to_lane(col, lane_idx):
    lane = lax.broadcasted_iota(jnp.int32, (col.shape[0], LANE), 1)
    return jnp.where(lane == lane_idx, col, 0.0)


def _gdn_columns(gates_ref, run_ref, tot_ref, runt_ref, rows, h):
    return (_lane_col(gates_ref[rows, :], h), _lane_col(run_ref[rows, :], HEADS + h),
            _lane_col(tot_ref[rows, :], HEADS + h), runt_ref[pl.ds(h, 1), rows])


def _gdn_local(q, k, beta, gc, gl, grow, causal, with_kk=True):
    decay = jnp.exp(jnp.where(causal, gc - grow, -1e30))
    egc = jnp.exp(gc)
    ekd = jnp.exp(gl - gc)
    qs = q * Q_SCALE
    kb = k * beta
    if with_kk:
        both = _dot_nt(jnp.concatenate([kb, qs], axis=0), k)
        kk, qk = both[:SC], both[SC:]
    else:
        kk, qk = None, _dot_nt(qs, k)
    return beta, gl, decay, egc, ekd, qs, kb, kk, qk, jnp.where(causal, qk * decay, 0.0)


def _chunk_rows(c):
    return pl.ds(pl.multiple_of(c * CHUNK, CHUNK), CHUNK)


def _sc_rows(b):
    return pl.ds(pl.multiple_of(b * SC, SC), SC)


def _gdn_fwd(qkv, gates, run, tot, run_t):
    t = qkv.shape[2]
    nc = t // CHUNK
    nsc = t // SC

    hp_n = GDN_HP_FWD
    heads = range(hp_n)

    def body(qkv_ref, gates_ref, run_ref, tot_ref, runt_ref, o_ref, st_ref, inv_ref, kc_s, qc_s, g_s, au_s):
        hp = pl.program_id(0)
        causal, strict, eye = _sc_masks()

        def local(b, _):
            rows = _sc_rows(b)
            loc = [_gdn_local(qkv_ref[0, hh, rows, :], qkv_ref[1, hh, rows, :],
                              *_gdn_columns(gates_ref, run_ref, tot_ref, runt_ref, rows, hp * hp_n + hh), causal)
                   for hh in heads]
            invs = _unit_lower_inverses([-jnp.where(strict, l[7] * l[2], 0.0) for l in loc], eye)
            uws = []
            for hh in heads:
                beta, _, _, egc, _, _, kb, _, _, _ = loc[hh]
                inv_ref[hh, rows, :] = invs[hh].astype(inv_ref.dtype)
                uws.append(_dot(invs[hh], jnp.concatenate([qkv_ref[2, hh, rows, :] * beta, kb * egc], axis=1)))
            for hh in heads:
                _, _, _, egc, ekd, qs, _, _, _, attn = loc[hh]
                auw = _dot(attn, uws[hh])
                g_s[hh, rows, :] = qs * egc - auw[:, DH:]
                au_s[hh, rows, :] = auw[:, :DH]
                kd = qkv_ref[1, hh, rows, :] * ekd
                for j in range(CPS):
                    sl = slice(j * CHUNK, (j + 1) * CHUNK)
                    both = _dot_tn(kd[sl], uws[hh][sl])
                    kc_s[hh, b * CPS + j] = both[:, DH:]
                    qc_s[hh, b * CPS + j] = both[:, :DH]
            return 0

        lax.fori_loop(0, nsc, local, 0)

        def step(c, states):
            rows = _chunk_rows(c)
            tot_row = tot_ref[pl.ds(c * CHUNK, 1), :]
            new = []
            for hh in heads:
                s = states[hh]
                st_ref[hh, c] = s
                o_ref[hh, rows, :] = _dot(g_s[hh, rows, :], s) + au_s[hh, rows, :]
                egl = jnp.exp(_lane_col(tot_row, HEADS + hp * hp_n + hh))
                new.append(egl * s - _dot(kc_s[hh, c], s) + qc_s[hh, c])
            return tuple(new)

        lax.fori_loop(0, nc, step, tuple(jnp.zeros((DH, DH), F32) for _ in heads))

    whole = pl.BlockSpec((t, LANE), lambda h: (0, 0))
    once = dict(pipeline_mode=pl.Buffered(1))
    return pl.pallas_call(
        body, name="gdn_fwd", grid=(HEADS // hp_n,),
        in_specs=[pl.BlockSpec((3, hp_n, t, DH), lambda h: (0, h, 0, 0), **once), whole, whole, whole,
                  pl.BlockSpec((HEADS, t), lambda h: (1, 0))],
        out_specs=[pl.BlockSpec((hp_n, t, DH), lambda h: (h, 0, 0), **once),
                   pl.BlockSpec((hp_n, nc, DH, DH), lambda h: (h, 0, 0, 0), **once),
                   pl.BlockSpec((hp_n, t, SC), lambda h: (h, 0, 0), **once)],
        out_shape=[jax.ShapeDtypeStruct((HEADS, t, DH), F32), jax.ShapeDtypeStruct((HEADS, nc, DH, DH), F32),
                   jax.ShapeDtypeStruct((HEADS, t, SC), MXU)],
        scratch_shapes=[pltpu.VMEM((hp_n, nc, DH, DH), F32), pltpu.VMEM((hp_n, nc, DH, DH), F32),
                        pltpu.VMEM((hp_n, t, DH), F32), pltpu.VMEM((hp_n, t, DH), F32)],
        compiler_params=_params(("arbitrary",)),
    )(qkv, gates, run, tot, run_t)


def _gdn_bwd(qkv, gates, run, tot, run_t, inv, states, do):
    t = qkv.shape[2]
    nc = t // CHUNK
    nsc = t // SC

    hp_n = GDN_HP_BWD
    heads = range(hp_n)

    def body(qkv_ref, gates_ref, run_ref, tot_ref, runt_ref, inv_ref, st_ref, do_ref,
             dqkv_ref, dcol_ref, dtot_ref, drow_ref, uw_s, kc_s, h_s, dsn_s):
        hp = pl.program_id(0)
        causal, strict, _ = _sc_masks()

        @pl.when(hp == 0)
        def _():
            dcol_ref[...] = jnp.zeros_like(dcol_ref)
            dtot_ref[...] = jnp.zeros_like(dtot_ref)

        def local_of(hh, rows, with_kk=True):
            return _gdn_local(qkv_ref[0, hh, rows, :], qkv_ref[1, hh, rows, :],
                              *_gdn_columns(gates_ref, run_ref, tot_ref, runt_ref, rows, hp * hp_n + hh), causal, with_kk)

        def local(b, _):
            rows = _sc_rows(b)
            loc = [local_of(hh, rows, with_kk=False) for hh in heads]
            us, ws = [], []
            for hh in heads:
                beta, _, _, egc, _, _, kb, _, _, _ = loc[hh]
                inv_b = inv_ref[hh, rows, :]
                us.append(_dot(inv_b, qkv_ref[2, hh, rows, :] * beta))
                ws.append(_dot(inv_b, kb * egc))
            gs = [loc[hh][5] * loc[hh][3] - _dot(loc[hh][9], ws[hh]) for hh in heads]
            for hh in heads:
                uw_s[hh, rows, :] = jnp.concatenate([us[hh], ws[hh]], axis=1)
                kd = qkv_ref[1, hh, rows, :] * loc[hh][4]
                dout = do_ref[hh, rows, :]
                for j in range(CPS):
                    sl = slice(j * CHUNK, (j + 1) * CHUNK)
                    kc_s[hh, b * CPS + j] = _dot_tn(kd[sl], ws[hh][sl])
                    h_s[hh, b * CPS + j] = _dot_tn(gs[hh][sl], dout[sl])
            return 0

        lax.fori_loop(0, nsc, local, 0)

        def step(i, dss):
            c = nc - 1 - i
            tot_row = tot_ref[pl.ds(c * CHUNK, 1), :]
            new = []
            for hh in heads:
                ds = dss[hh]
                dsn_s[hh, c] = ds
                egl = jnp.exp(_lane_col(tot_row, HEADS + hp * hp_n + hh))
                new.append(egl * ds - _dot_tn(kc_s[hh, c], ds) + h_s[hh, c])
            return tuple(new)

        lax.fori_loop(0, nc, step, tuple(jnp.zeros((DH, DH), F32) for _ in heads))

        def back(b, _):
            rows = _sc_rows(b)
            first = lax.broadcasted_iota(jnp.int32, (CHUNK, 1), 0) == 0
            loc = [local_of(hh, rows) for hh in heads]
            sign = jnp.where(lax.broadcasted_iota(jnp.int32, (1, LANE), 1) < DH, 1.0, -1.0)
            mid = []
            for hh in heads:
                beta, gl, decay, egc, ekd, qs, kb, kk, qk, attn = loc[hh]
                uw = uw_s[hh, rows, :]
                kd = qkv_ref[1, hh, rows, :] * ekd
                dout = do_ref[hh, rows, :]
                dg_p, dkd_p, duw_p, dgl_p = [], [], [], []
                for j in range(CPS):
                    sl = slice(j * CHUNK, (j + 1) * CHUNK)
                    s = st_ref[hh, b * CPS + j]
                    dsn = dsn_s[hh, b * CPS + j]
                    both = _dot_nt(jnp.concatenate([dsn, dout[sl]], axis=0), s)
                    dg_p.append(both[CHUNK:])
                    dsc = jnp.concatenate([dsn, -both[:CHUNK]], axis=1)
                    dkd_p.append(_dot_nt(uw[sl], dsc))
                    duw_p.append(_dot(kd[sl], dsc))
                    degl = jnp.sum(jnp.sum(s * dsn, axis=1, keepdims=True), axis=0, keepdims=True)
                    dgl_p.append(jnp.where(first, degl * jnp.exp(gl[j * CHUNK:j * CHUNK + 1, :]), 0.0))
                dg, dkd = jnp.concatenate(dg_p, axis=0), jnp.concatenate(dkd_p, axis=0)
                dod = jnp.concatenate([dout, dg], axis=1)
                da = jnp.where(causal, _dot_nt(dod, uw * sign), 0.0)
                duw = jnp.concatenate(duw_p, axis=0) + _dot_tn(attn, dod) * sign
                mid.append((kd, dg, dkd, da, duw, jnp.concatenate(dgl_p, axis=0)))
            inv_ts = [inv_ref[hh, rows, :].astype(F32).T for hh in heads]
            its = [_dot(inv_ts[hh], mid[hh][4]) for hh in heads]
            dinvs = []
            for hh in heads:
                beta, _, _, egc, _, _, kb, _, _, _ = loc[hh]
                dinvs.append(_dot_nt(mid[hh][4], jnp.concatenate([qkv_ref[2, hh, rows, :] * beta, kb * egc], axis=1)))
            half = [_dot(inv_ts[hh], dinvs[hh]) for hh in heads]
            dls = [jnp.where(strict, -_dot(half[hh], inv_ts[hh]), 0.0) for hh in heads]
            for hh in heads:
                head = hp * hp_n + hh
                beta, gl, decay, egc, ekd, qs, kb, kk, qk, attn = loc[hh]
                kd, dg, dkd, da, _, dgl_first = mid[hh]
                k, v = qkv_ref[1, hh, rows, :], qkv_ref[2, hh, rows, :]
                dvb, dkbe = its[hh][:, :DH], its[hh][:, DH:]
                dl = dls[hh]
                dlogd = (dl * kk + da * qk) * decay
                dd = jnp.concatenate([dl * decay, da * decay], axis=0)
                ddk = _dot(dd, k)
                dkb = ddk[:SC] + dkbe * egc
                dqs = ddk[SC:] + dg * egc
                dk = _dot_tn(dd, jnp.concatenate([kb, qs], axis=0)) + dkd * ekd + dkb * beta
                dkd_kd = jnp.sum(dkd * kd, axis=1, keepdims=True)
                narrow = dg * (qs * egc) + dkbe * (kb * egc)
                wide = dlogd[:, :LANE] + dlogd[:, LANE:] + jnp.concatenate([narrow, jnp.zeros((SC, LANE - DH), F32)], axis=1)
                dgc = jnp.sum(wide, axis=1, keepdims=True) - dkd_kd
                dbeta = jnp.sum(dkb * k + dvb * v, axis=1, keepdims=True)
                dqkv_ref[0, hh, rows, :] = dqs * Q_SCALE
                dqkv_ref[1, hh, rows, :] = dk
                dqkv_ref[2, hh, rows, :] = dvb * beta
                dcol_ref[rows, :] += _to_lane(dbeta, head) + _to_lane(dgc, HEADS + head)
                dtot_ref[rows, :] += _to_lane(dkd_kd + dgl_first, HEADS + head)
                drow_ref[pl.ds(head, 1), rows] = -jnp.sum(dlogd, axis=0, keepdims=True)
            return 0

        lax.fori_loop(0, nsc, back, 0)

    whole = pl.BlockSpec((t, LANE), lambda h: (0, 0))
    rowspec = pl.BlockSpec((HEADS, t), lambda h: (0, 0))
    sq = pltpu.VMEM((hp_n, nc, DH, DH), F32)
    per_head = pltpu.VMEM((hp_n, t, 2 * DH), F32)
    return pl.pallas_call(
        body, name="gdn_bwd", grid=(HEADS // hp_n,),
        in_specs=[pl.BlockSpec((3, hp_n, t, DH), lambda h: (0, h, 0, 0)), whole, whole, whole,
                  pl.BlockSpec((HEADS, t), lambda h: (1, 0)),
                  pl.BlockSpec((hp_n, t, SC), lambda h: (h, 0, 0)), pl.BlockSpec((hp_n, nc, DH, DH), lambda h: (h, 0, 0, 0)),
                  pl.BlockSpec((hp_n, t, DH), lambda h: (h, 0, 0))],
        out_specs=[pl.BlockSpec((3, hp_n, t, DH), lambda h: (0, h, 0, 0)), whole, whole, rowspec],
        out_shape=[jax.ShapeDtypeStruct((3, HEADS, t, DH), F32), jax.ShapeDtypeStruct((t, LANE), F32),
                   jax.ShapeDtypeStruct((t, LANE), F32), jax.ShapeDtypeStruct((HEADS, t), F32)],
        scratch_shapes=[per_head, sq, sq, sq],
        compiler_params=_params(("arbitrary",)),
    )(qkv, gates, run, tot, run_t, inv, states, do)


FOX_HP_FWD = 4
FOX_HP_BWD = 4


def _wide_t(a):
    r = a.shape[0]
    return jnp.concatenate([a, jnp.zeros((r, LANE - DH), F32)], axis=1).T[:DH]


def _tall_t(a):
    r = a.shape[1]
    return jnp.concatenate([a, jnp.zeros((LANE - DH, r), F32)], axis=0).T[:, :DH]


def _key_side_f(run_blk, head, qb):
    col = jnp.broadcast_to(_lane_col(run_blk, 2 * HEADS + head), (run_blk.shape[0], LANE))
    return jnp.concatenate([col] * (qb // LANE), axis=1)


def _diag_mask(qb):
    return lax.broadcasted_iota(jnp.int32, (qb, qb), 0) <= lax.broadcasted_iota(jnp.int32, (qb, qb), 1)


def _fox_fwd(qkv, run, run_t, carry=None):
    t = qkv.shape[2]
    qb = min(QB, t)
    nq = t // qb
    hp_n = FOX_HP_FWD
    c_in, c_in_specs, c_out_shape, c_out_specs, c_sem = _carry_operands(*carry) if carry else ([], [], [], [], None)
    n_c = len(c_in)

    def body(*refs):
        q_ref, k_ref, v_ref, run_ref, runt_ref = refs[:5]
        o_ref, lse_ref = refs[5 + n_c:7 + n_c]
        s0 = 7 + n_c + len(c_out_shape)
        kb_s, vt_s, raw_s, m_s, l_s, acc_s = refs[s0:s0 + 6]
        hp = pl.program_id(0)
        i = pl.program_id(1)

        if carry:
            _carry_step(carry[0], refs[5:5 + n_c], refs[7 + n_c], refs[8 + n_c], refs[-1], hp * nq + i,
                        (HEADS // hp_n) * nq)

        @pl.when(i == 0)
        def _():
            for hh in range(hp_n):
                kb_s[hh] = k_ref[0, hh].astype(MXU)
                for b in range(nq):
                    rows = slice(b * qb, (b + 1) * qb)
                    vt_s[hh, :, rows] = _wide_t(v_ref[0, hh, rows, :]).astype(MXU)

        qrows = pl.ds(pl.multiple_of(i * qb, qb), qb)
        qs = [(q_ref[0, hh] * Q_SCALE).astype(MXU) for hh in range(hp_n)]
        fq = [runt_ref[pl.ds(hp * hp_n + hh, 1), qrows] for hh in range(hp_n)]

        def block_rows(j):
            return pl.ds(pl.multiple_of(j * qb, qb), qb)

        def scores(j):
            for hh in range(hp_n):
                raw_s[j % 2, hh] = _dot_nt(kb_s[hh, block_rows(j), :], qs[hh])

        def absorb(j, diagonal):
            rows = block_rows(j)
            run_blk = run_ref[rows, :]
            stats, pv = [], []
            for hh in range(hp_n):
                m, l = m_s[hh], l_s[hh]
                st = raw_s[j % 2, hh] + (fq[hh] - _key_side_f(run_blk, hp * hp_n + hh, qb))
                if diagonal:
                    st = jnp.where(_diag_mask(qb), st, -1e30)
                m_new = jnp.maximum(m, jnp.max(st, axis=0, keepdims=True))
                p = jnp.exp(st - m_new)
                alpha = jnp.exp(m - m_new)
                stats.append((m_new, alpha * l + jnp.sum(p, axis=0, keepdims=True), alpha))
                pv.append(_dot(vt_s[hh, :, rows], p))
            for hh in range(hp_n):
                m_s[hh], l_s[hh] = stats[hh][0], stats[hh][1]
                acc_s[hh] = stats[hh][2] * acc_s[hh] + pv[hh]

        def kstep(j, _):
            scores(j + 1)
            absorb(j, False)
            return 0

        for hh in range(hp_n):
            m_s[hh] = jnp.full((1, qb), -1e30, F32)
            l_s[hh] = jnp.zeros((1, qb), F32)
            acc_s[hh] = jnp.zeros((DH, qb), F32)
        scores(0)
        lax.fori_loop(0, i, kstep, 0)
        absorb(i, True)
        for hh in range(hp_n):
            l = l_s[hh]
            o_ref[hh] = _tall_t(acc_s[hh] / l)
            lse_ref[pl.ds(hp * hp_n + hh, 1), qrows] = m_s[hh] + jnp.log(l)

    out = pl.pallas_call(
        body, name="fox_fwd", grid=(HEADS // hp_n, nq),
        in_specs=[pl.BlockSpec((1, hp_n, qb, DH), lambda h, i: (0, h, i, 0)),
                  pl.BlockSpec((1, hp_n, t, DH), lambda h, i: (1, h, 0, 0)),
                  pl.BlockSpec((1, hp_n, t, DH), lambda h, i: (2, h, 0, 0)),
                  pl.BlockSpec((t, LANE), lambda h, i: (0, 0)),
                  pl.BlockSpec((HEADS, t), lambda h, i: (2, 0))] + c_in_specs,
        out_specs=[pl.BlockSpec((hp_n, qb, DH), lambda h, i: (h, i, 0)),
                   pl.BlockSpec((HEADS, t), lambda h, i: (0, 0))] + c_out_specs,
        out_shape=[jax.ShapeDtypeStruct((HEADS, t, DH), F32), jax.ShapeDtypeStruct((HEADS, t), F32)] + c_out_shape,
        input_output_aliases={5 + j: 4 + j for j in range(n_c)},
        scratch_shapes=[pltpu.VMEM((hp_n, t, DH), MXU), pltpu.VMEM((hp_n, DH, t), MXU), pltpu.VMEM((2, hp_n, qb, qb), F32),
                        pltpu.VMEM((hp_n, 1, qb), F32), pltpu.VMEM((hp_n, 1, qb), F32), pltpu.VMEM((hp_n, DH, qb), F32)]
        + ([c_sem] if carry else []),
        compiler_params=_params(("arbitrary", "arbitrary")),
    )(qkv, qkv, qkv, run, run_t, *c_in)
    return (out[0], out[1], _carry_state(out[2:])) if carry else (out[0], out[1])


def _fox_bwd(qkv, run, run_t, o, lse, do, carry=None):
    t = qkv.shape[2]
    qb = min(QB, t)
    nq = t // qb
    hp_n = FOX_HP_BWD
    c_in, c_in_specs, c_out_shape, c_out_specs, c_sem = _carry_operands(*carry) if carry else ([], [], [], [], None)
    n_c = len(c_in)

    def body(*refs):
        q_ref, k_ref, v_ref, run_ref, runt_ref, o_ref, lse_ref, do_ref = refs[:8]
        dqkv_ref, dcol_ref, drow_ref = refs[8 + n_c:11 + n_c]
        s0 = 11 + n_c + len(c_out_shape)
        dqt_s, raw_s, dpt_s, dro_s, dk_s, dv_s, dsum_s = refs[s0:s0 + 7]
        hp = pl.program_id(0)
        j = pl.program_id(1)

        if carry:
            _carry_step(carry[0], refs[8:8 + n_c], refs[11 + n_c], refs[12 + n_c], refs[-1], hp * nq + j,
                        (HEADS // hp_n) * nq)

        @pl.when(j == 0)
        def _():
            dqt_s[...] = jnp.zeros_like(dqt_s)

        @pl.when((j == 0) & (hp == 0))
        def _():
            dcol_ref[...] = jnp.zeros_like(dcol_ref)
            drow_ref[...] = jnp.zeros_like(drow_ref)

        krows = pl.ds(pl.multiple_of(j * qb, qb), qb)
        run_blk = run_ref[krows, :]
        ones8 = jnp.ones((8, DH), MXU)
        kb, kt, vb, fk = [], [], [], []
        for hh in range(hp_n):
            kf = k_ref[0, hh]
            kb.append(kf.astype(MXU))
            kt.append(_wide_t(kf).astype(MXU))
            vb.append(v_ref[0, hh].astype(MXU))
            fk.append(_key_side_f(run_blk, hp * hp_n + hh, qb))

        def block_rows(i):
            return pl.ds(i * qb if isinstance(i, int) else pl.multiple_of(i * qb, qb), qb)

        def products(i):
            rows = block_rows(i)
            slot = i % 2
            for hh in range(hp_n):
                dout = do_ref[hh, rows, :]
                x = dout * o_ref[hh, rows, :]
                x_hi = x.astype(MXU)
                raw_s[slot, hh] = _dot_nt(kb[hh], q_ref[0, hh, rows, :] * Q_SCALE)
                dpt_s[slot, hh] = _dot_nt(vb[hh], dout)
                dro_s[slot, hh] = _dot_nt(ones8, x_hi) + _dot_nt(ones8, x - x_hi.astype(F32))

        def absorb(i, diagonal):
            rows = block_rows(i)
            slot = i % 2
            pieces = []
            for hh in range(hp_n):
                head = hp * hp_n + hh
                raw, dpt, drow = raw_s[slot, hh], dpt_s[slot, hh], dro_s[slot, hh, 0:1, :]
                off = runt_ref[pl.ds(head, 1), rows] - lse_ref[pl.ds(head, 1), rows]
                st = raw + (off - fk[hh])
                if diagonal:
                    st = jnp.where(_diag_mask(qb), st, -1e30)
                pt = jnp.exp(st)
                dst = pt * (dpt - drow)
                drow_ref[pl.ds(head, 1), rows] += jnp.sum(dst, axis=0, keepdims=True)
                folded = dst[:, 0:LANE]
                for c in range(1, qb // LANE):
                    folded = folded + dst[:, c * LANE:(c + 1) * LANE]
                pieces.append((_dot(dst, q_ref[0, hh, rows, :] * Q_SCALE), _dot(pt, do_ref[hh, rows, :]),
                               _dot(kt[hh], dst), folded))
            for hh in range(hp_n):
                dqt_s[hh, :, rows] += pieces[hh][2]
                if diagonal:
                    dk_s[hh], dv_s[hh], dsum_s[hh] = pieces[hh][0], pieces[hh][1], pieces[hh][3]
                else:
                    dk_s[hh] += pieces[hh][0]
                    dv_s[hh] += pieces[hh][1]
                    dsum_s[hh] += pieces[hh][3]

        def qstep(i, _):
            products(i + 1)
            absorb(i, False)
            return 0

        products(j)
        products(jnp.minimum(j + 1, nq - 1))
        absorb(j, True)
        lax.fori_loop(j + 1, nq - 1, qstep, 0)

        @pl.when(j < nq - 1)
        def _():
            absorb(nq - 1, False)
        for hh in range(hp_n):
            dqkv_ref[1, hh, krows, :] = dk_s[hh]
            dqkv_ref[2, hh, krows, :] = dv_s[hh]
            dcol_ref[krows, :] += _to_lane(-jnp.sum(dsum_s[hh], axis=1, keepdims=True), 2 * HEADS + hp * hp_n + hh)

        @pl.when(j == nq - 1)
        def _():
            for hh in range(hp_n):
                for b in range(nq):
                    rows = slice(b * qb, (b + 1) * qb)
                    dqkv_ref[0, hh, rows, :] = _tall_t(dqt_s[hh, :, rows]) * Q_SCALE

    once = dict(pipeline_mode=pl.Buffered(1))
    full = pl.BlockSpec((hp_n, t, DH), lambda h, j: (h, 0, 0), **once)
    rows8 = pl.BlockSpec((HEADS, t), lambda h, j: (0, 0))
    out = pl.pallas_call(
        body, name="fox_bwd", grid=(HEADS // hp_n, nq),
        in_specs=[pl.BlockSpec((1, hp_n, t, DH), lambda h, j: (0, h, 0, 0), **once),
                  pl.BlockSpec((1, hp_n, qb, DH), lambda h, j: (1, h, j, 0)),
                  pl.BlockSpec((1, hp_n, qb, DH), lambda h, j: (2, h, j, 0)),
                  pl.BlockSpec((t, LANE), lambda h, j: (0, 0), **once), pl.BlockSpec((HEADS, t), lambda h, j: (2, 0)),
                  full, rows8, full] + c_in_specs,
        out_specs=[pl.BlockSpec((3, hp_n, t, DH), lambda h, j: (0, h, 0, 0), **once),
                   pl.BlockSpec((t, LANE), lambda h, j: (0, 0)), rows8] + c_out_specs,
        out_shape=[jax.ShapeDtypeStruct((3, HEADS, t, DH), F32), jax.ShapeDtypeStruct((t, LANE), F32),
                   jax.ShapeDtypeStruct((HEADS, t), F32)] + c_out_shape,
        input_output_aliases={8 + j: 5 + j for j in range(n_c)},
        scratch_shapes=[pltpu.VMEM((hp_n, DH, t), F32), pltpu.VMEM((2, hp_n, qb, qb), F32), pltpu.VMEM((2, hp_n, qb, qb), F32),
                        pltpu.VMEM((2, hp_n, 8, qb), F32), pltpu.VMEM((hp_n, qb, DH), F32), pltpu.VMEM((hp_n, qb, DH), F32),
                        pltpu.VMEM((hp_n, qb, LANE), F32)] + ([c_sem] if carry else []),
        compiler_params=_params(("arbitrary", "arbitrary")),
    )(qkv, qkv, qkv, run, run_t, o, lse, do, *c_in)
    return (out[0], out[1], out[2], _carry_state(out[3:])) if carry else tuple(out)


Z_COL0 = 3 * WIDTH // LANE
FGATE_COL0 = 7 * WIDTH // LANE


def _gdn_post(o, pm, onw):
    t = pm.shape[0]

    def body(o_ref, z_ref, w_ref, m_ref):
        z = z_ref[...]
        sz = z * _sigmoid(z)
        halves = []
        for hh in range(2):
            ov = o_ref[hh]
            n = ov * lax.rsqrt(jnp.mean(ov * ov, axis=-1, keepdims=True) + EPS) * w_ref[...]
            halves.append(n * sz[:, hh * DH:(hh + 1) * DH])
        m_ref[...] = jnp.concatenate(halves, axis=1).astype(m_ref.dtype)

    return pl.pallas_call(
        body, name="gdn_post", grid=(WIDTH // LANE,),
        in_specs=[pl.BlockSpec((2, t, DH), lambda j: (j, 0, 0)), pl.BlockSpec((t, LANE), lambda j: (0, Z_COL0 + j)),
                  pl.BlockSpec((1, DH), lambda j: (0, 0))],
        out_specs=pl.BlockSpec((t, LANE), lambda j: (0, j)),
        out_shape=jax.ShapeDtypeStruct((t, WIDTH), MXU),
        compiler_params=_params(("arbitrary",)),
    )(o, pm, onw)


def _gdn_post_bwd(o, pm, onw, dmix):
    t = pm.shape[0]

    def body(o_ref, z_ref, w_ref, dm_ref, do_ref, dz_ref, dw_ref):
        @pl.when(pl.program_id(0) == 0)
        def _():
            dw_ref[...] = jnp.zeros_like(dw_ref)

        z = z_ref[...]
        sg = _sigmoid(z)
        sz = z * sg
        dsz = sg * (1.0 + z * (1.0 - sg))
        dm = dm_ref[...]
        for hh in range(2):
            cols = slice(hh * DH, (hh + 1) * DH)
            ov = o_ref[hh]
            r = lax.rsqrt(jnp.mean(ov * ov, axis=-1, keepdims=True) + EPS)
            xn = ov * r
            dmh = dm[:, cols]
            dn = dmh * sz[:, cols]
            dz_ref[:, cols] = dmh * (xn * w_ref[...]) * dsz[:, cols]
            dw_ref[...] += jnp.sum(dn * xn, axis=0, keepdims=True)
            g = dn * w_ref[...]
            do_ref[hh] = r * (g - xn * jnp.mean(g * xn, axis=-1, keepdims=True))

    return pl.pallas_call(
        body, name="gdn_post_bwd", grid=(WIDTH // LANE,),
        in_specs=[pl.BlockSpec((2, t, DH), lambda j: (j, 0, 0)), pl.BlockSpec((t, LANE), lambda j: (0, Z_COL0 + j)),
                  pl.BlockSpec((1, DH), lambda j: (0, 0)), pl.BlockSpec((t, LANE), lambda j: (0, j))],
        out_specs=[pl.BlockSpec((2, t, DH), lambda j: (j, 0, 0)), pl.BlockSpec((t, LANE), lambda j: (0, j)),
                   pl.BlockSpec((1, DH), lambda j: (0, 0))],
        out_shape=[jax.ShapeDtypeStruct((HEADS, t, DH), F32), jax.ShapeDtypeStruct((t, WIDTH), F32),
                   jax.ShapeDtypeStruct((1, DH), F32)],
        compiler_params=_params(("arbitrary",)),
    )(o, pm, onw, dmix)


def _fox_post(o, pm):
    t = pm.shape[0]

    def body(o_ref, g_ref, m_ref):
        m_ref[...] = (jnp.concatenate([o_ref[0], o_ref[1]], axis=1) * _sigmoid(g_ref[...])).astype(m_ref.dtype)

    return pl.pallas_call(
        body, name="fox_post", grid=(WIDTH // LANE,),
        in_specs=[pl.BlockSpec((2, t, DH), lambda j: (j, 0, 0)), pl.BlockSpec((t, LANE), lambda j: (0, FGATE_COL0 + j))],
        out_specs=pl.BlockSpec((t, LANE), lambda j: (0, j)),
        out_shape=jax.ShapeDtypeStruct((t, WIDTH), MXU),
        compiler_params=_params(("arbitrary",)),
    )(o, pm)


def _fox_post_bwd(o, pm, dmix):
    t = pm.shape[0]

    def body(o_ref, g_ref, dm_ref, do_ref, dg_ref):
        sg = _sigmoid(g_ref[...])
        dm = dm_ref[...]
        for hh in range(2):
            cols = slice(hh * DH, (hh + 1) * DH)
            do_ref[hh] = dm[:, cols] * sg[:, cols]
            dg_ref[:, cols] = dm[:, cols] * o_ref[hh] * (sg * (1.0 - sg))[:, cols]

    return pl.pallas_call(
        body, name="fox_post_bwd", grid=(WIDTH // LANE,),
        in_specs=[pl.BlockSpec((2, t, DH), lambda j: (j, 0, 0)), pl.BlockSpec((t, LANE), lambda j: (0, FGATE_COL0 + j)),
                  pl.BlockSpec((t, LANE), lambda j: (0, j))],
        out_specs=[pl.BlockSpec((2, t, DH), lambda j: (j, 0, 0)), pl.BlockSpec((t, LANE), lambda j: (0, j))],
        out_shape=[jax.ShapeDtypeStruct((HEADS, t, DH), F32), jax.ShapeDtypeStruct((t, WIDTH), F32)],
        compiler_params=_params(("arbitrary",)),
    )(o, pm, dmix)


def _tail(x, mixg, mixf, tgt, wo, n2w, wg_t, wu_t, wd, fw):
    t, d = x.shape
    dff = wd.shape[0]
    tb = min(TB, t)

    def body(x_ref, mg_ref, mf_ref, t_ref, wo_ref, n2_ref, wg_ref, wu_ref, wd_ref, fw_ref,
             h2_ref, act_ref, dgate_ref, dup_ref, dx3_ref, dx2_ref, dmg_ref, dmf_ref, dn2_ref, dfw_ref, loss_ref):
        @pl.when(pl.program_id(0) == 0)
        def _():
            dn2_ref[...] = jnp.zeros_like(dn2_ref)
            dfw_ref[...] = jnp.zeros_like(dfw_ref)
            loss_ref[...] = jnp.zeros_like(loss_ref)

        x2 = x_ref[...] + _dot(mg_ref[...], wo_ref[0:WIDTH, :]) + _dot(mf_ref[...], wo_ref[WIDTH:2 * WIDTH, :])
        r2 = lax.rsqrt(jnp.mean(x2 * x2, axis=-1, keepdims=True) + EPS)
        xn2 = x2 * r2
        h2 = (xn2 * n2_ref[...]).astype(MXU)
        h2_ref[...] = h2
        gate = _dot_nt(h2, wg_ref[...])
        up = _dot_nt(h2, wu_ref[...])
        sg = _sigmoid(gate)
        sl = gate * sg
        act = (sl * up).astype(MXU)
        act_ref[...] = act
        x3 = x2 + _dot(act, wd_ref[...])
        r3 = lax.rsqrt(jnp.mean(x3 * x3, axis=-1, keepdims=True) + EPS)
        xn3 = x3 * r3
        err = xn3 * fw_ref[...] - t_ref[...]
        loss_ref[...] += 0.5 * jnp.sum(jnp.mean(err * err, axis=-1, keepdims=True), axis=0, keepdims=True)
        dy = err * (1.0 / d)
        dfw_ref[...] += jnp.sum(dy * xn3, axis=0, keepdims=True)
        g3 = dy * fw_ref[...]
        dx3 = r3 * (g3 - xn3 * jnp.mean(g3 * xn3, axis=-1, keepdims=True))
        dx3_ref[...] = dx3.astype(MXU)
        dact = _dot_nt(dx3, wd_ref[...])
        dgate = (dact * up * (sg * (1.0 + gate * (1.0 - sg)))).astype(MXU)
        dup = (dact * sl).astype(MXU)
        dgate_ref[...] = dgate
        dup_ref[...] = dup
        dh2 = _dot(dgate, wg_ref[...]) + _dot(dup, wu_ref[...])
        dn2_ref[...] += jnp.sum(dh2 * xn2, axis=0, keepdims=True)
        g2 = dh2 * n2_ref[...]
        dx2 = dx3 + r2 * (g2 - xn2 * jnp.mean(g2 * xn2, axis=-1, keepdims=True))
        dx2_ref[...] = dx2
        dmg_ref[...] = _dot_nt(dx2, wo_ref[0:WIDTH, :])
        dmf_ref[...] = _dot_nt(dx2, wo_ref[WIDTH:2 * WIDTH, :])

    def tok(n):
        return pl.BlockSpec((tb, n), lambda i: (i, 0))

    acc = pl.BlockSpec((1, d), lambda i: (0, 0))
    sds = jax.ShapeDtypeStruct
    return pl.pallas_call(
        body, name="tail", grid=(t // tb,),
        in_specs=[tok(d), tok(WIDTH), tok(WIDTH), tok(d), _resident(wo.shape), _resident((1, d)),
                  _resident(wg_t.shape), _resident(wu_t.shape), _resident(wd.shape), _resident((1, d))],
        out_specs=[tok(d), tok(dff), tok(dff), tok(dff), tok(d), tok(d), tok(WIDTH), tok(WIDTH), acc, acc,
                   pl.BlockSpec((1, 1), lambda i: (0, 0))],
        out_shape=[sds((t, d), MXU), sds((t, dff), MXU), sds((t, dff), MXU), sds((t, dff), MXU), sds((t, d), MXU),
                   sds((t, d), F32), sds((t, WIDTH), F32), sds((t, WIDTH), F32), sds((1, d), F32), sds((1, d), F32),
                   sds((1, 1), F32)],
        compiler_params=_params(("arbitrary",)),
    )(x, mixg, mixf, tgt, wo, n2w, wg_t, wu_t, wd, fw)


def _wgrads(a_list, b, name):
    t, n = b.shape
    ms = [a.shape[1] for a in a_list]
    bms = [256 if m % 256 == 0 else LANE for m in ms]
    nbs = [m // bm for m, bm in zip(ms, bms)]
    k = len(a_list)
    cast = b.dtype != jnp.dtype(MXU)

    def body(*refs):
        a_refs, b_ref, o_refs = refs[:k], refs[k], refs[k + 1:2 * k + 1]
        i = pl.program_id(0)
        if cast:
            @pl.when(i == 0)
            def _():
                refs[-1][...] = b_ref[...].astype(MXU)
        for a_ref, o_ref, nb in zip(a_refs, o_refs, nbs):
            @pl.when(i < nb)
            def _():
                o_ref[...] = _dot_tn(a_ref[...], refs[-1][...] if cast else b_ref[...]).astype(o_ref.dtype)

    def clamp(nb):
        return lambda i: jnp.minimum(i, nb - 1)

    return pl.pallas_call(
        body, name=name, grid=(max(nbs),),
        in_specs=[pl.BlockSpec((t, bm), lambda i, c=clamp(nb): (0, c(i))) for bm, nb in zip(bms, nbs)] + [_resident((t, n))],
        out_specs=[pl.BlockSpec((bm, n), lambda i, c=clamp(nb): (c(i), 0)) for bm, nb in zip(bms, nbs)],
        out_shape=[jax.ShapeDtypeStruct((m, n), WIRE) for m in ms],
        scratch_shapes=[pltpu.VMEM((t, n), MXU)] if cast else [],
        compiler_params=_params(("arbitrary",)),
    )(*a_list, b)


def _merge_dw_in(d_gdn, d_z, d_fox, d_fg, d_small, shards=False):
    pieces = [d_gdn, d_z, d_small[:2 * HEADS], d_fox, d_fg, d_small[2 * HEADS:3 * HEADS]]
    if not shards:
        return jnp.concatenate(pieces, axis=0)
    n = sum(p.shape[0] for p in pieces) // N_DEV
    out = []
    for dev in range(N_DEV):
        parts, first = [], 0
        for p in pieces:
            lo, hi = max(dev * n, first), min((dev + 1) * n, first + p.shape[0])
            if lo < hi:
                parts.append(p[lo - first:hi - first])
            first += p.shape[0]
        out.append(jnp.concatenate(parts, axis=0))
    return jnp.stack(out)


def _lanes(*pieces):
    v = jnp.concatenate([p.reshape(-1).astype(F32) for p in pieces])
    return jnp.pad(v, (0, LANE - v.shape[0])).reshape(1, LANE)


def _vector_params(p):
    d = p["norm1_w"].size
    gparams = jnp.concatenate([_lanes(jnp.zeros(HEADS), p["gdn_dt_bias"], p["fox_f_bias"]),
                               _lanes(jnp.zeros(HEADS), p["gdn_A_log"]), jnp.zeros((6, LANE), F32)])
    fox_nw = jnp.stack([jnp.tile(p["fox_q_norm_w"].reshape(-1), 2), jnp.tile(p["fox_k_norm_w"].reshape(-1), 2),
                        jnp.ones((LANE,), F32)])
    return dict(n1w=p["norm1_w"].reshape(1, d), n2w=p["norm2_w"].reshape(1, d), fw=p["final_norm_w"].reshape(1, d),
                onw=p["gdn_out_norm_w"].reshape(1, DH), gparams=gparams, fox_nw=fox_nw)


def _mixer_forward(x, vp, w_t, ws_t, conv_w, carry=None):
    h1, pm, ps = _inproj(x, vp["n1w"], w_t, ws_t)
    gates, run, tot, run_t = _gates(ps, vp["gparams"])
    fqkv = _fox_prep(pm, vp["fox_nw"])
    o_fox, lse, *carried = _fox_fwd(fqkv, run, run_t, carry)
    if carry:
        pm, o_fox = lax.optimization_barrier((pm, o_fox))
    mixf = _fox_post(o_fox, pm)
    gqkv = _gdn_prep(pm, conv_w)
    o_gdn, states, inv = _gdn_fwd(gqkv, gates, run, tot, run_t)
    mixg = _gdn_post(o_gdn, pm, vp["onw"])
    return dict(h1=h1, pm=pm, ps=ps, gates=gates, run=run, tot=tot, run_t=run_t, gqkv=gqkv, o_gdn=o_gdn, states=states,
                inv=inv, mixg=mixg, fqkv=fqkv, o_fox=o_fox, lse=lse, mixf=mixf, carried=carried[0] if carried else None)


def _mixer_backward(x, vp, w_t, ws_t, conv_w, f, dx2, dmixg, dmixf, carry=None, scatter_own=False):
    pm = f["pm"]
    do_fox, dfg = _fox_post_bwd(f["o_fox"], pm, dmixf)
    dfqkv, dcol_f, drow_f, *carried = _fox_bwd(f["fqkv"], f["run"], f["run_t"], f["o_fox"], f["lse"], do_fox, carry)
    dfox, dfnw = _fox_prep_bwd(pm, vp["fox_nw"], dfqkv)
    do_gdn, dz, donw = _gdn_post_bwd(f["o_gdn"], pm, vp["onw"], dmixg)
    dgqkv, dcol_g, dtot_g, drow_g = _gdn_bwd(f["gqkv"], f["gates"], f["run"], f["tot"], f["run_t"], f["inv"], f["states"], do_gdn)
    dgdn, dconv = _gdn_prep_bwd(pm, conv_w, dgqkv)
    dps, gsum = _gates_bwd(f["ps"], vp["gparams"], dcol_g, dtot_g, dcol_f, drow_g, drow_f)
    dw_pieces = _wgrads([dgdn, dz, dfox, dfg, dps], f["h1"], "dw_in")
    dw_in = _merge_dw_in(*dw_pieces)
    own = ("scatter", [_merge_dw_in(*dw_pieces, shards=True), _cut(dconv, 1)]) if scatter_own else None
    grad_x, dn1w, *sent = _inproj_bwd(x, vp["n1w"], dx2, dgdn, dz, dfox, dfg, dps, w_t, ws_t, own)
    small = dict(dn1w=dn1w, gsum=gsum, donw=donw, dfnw=dfnw)
    return (grad_x, dw_in, dconv, small, *carried, *sent)


VECTORS = ("norm1_w", "norm2_w", "final_norm_w", "gdn_A_log", "gdn_dt_bias", "gdn_out_norm_w", "fox_f_bias",
           "fox_q_norm_w", "fox_k_norm_w")
VEC_ROWS = 16
LOSS_ROW = len(VECTORS)


def _pack_vectors(dn1w, dn2w, dfw, gsum, donw, dfnw, loss):
    d = dn1w.shape[1]

    def body(n1_ref, n2_ref, fw_ref, gs_ref, on_ref, fn_ref, loss_ref, o_ref):
        o_ref[...] = jnp.zeros_like(o_ref)
        o_ref[0:1, :] = n1_ref[...]
        o_ref[1:2, :] = n2_ref[...]
        o_ref[2:3, :] = fw_ref[...]
        o_ref[3:4, 0:HEADS] = gs_ref[0:1, 0:HEADS]
        o_ref[4:5, 0:HEADS] = gs_ref[1:2, 0:HEADS]
        o_ref[5:6, 0:DH] = on_ref[...]
        o_ref[6:7, 0:HEADS] = gs_ref[2:3, 0:HEADS]
        for kind in range(2):
            v = fn_ref[kind, 0]
            for j in range(1, fn_ref.shape[1]):
                v = v + fn_ref[kind, j]
            o_ref[7 + kind:8 + kind, 0:DH] = v[:, :DH] + v[:, DH:]
        o_ref[LOSS_ROW:LOSS_ROW + 1, 0:1] = loss_ref[...]

    return pl.pallas_call(body, name="pack_vectors", out_shape=jax.ShapeDtypeStruct((VEC_ROWS, d), F32),
                          compiler_params=_params())(dn1w, dn2w, dfw, gsum, donw, dfnw, loss)


def _late_grads(f, h2, act, dgate, dup, dx3, dx2):
    dw_gate, dw_up = _wgrads([dgate, dup], h2, "dw_gate_up")
    (dw_down,) = _wgrads([act], dx3, "dw_down")
    return {"w_out": jnp.concatenate(_wgrads([f["mixg"], f["mixf"]], dx2, "dw_out"), axis=0),
            "w_ffn_gate": dw_gate, "w_ffn_up": dw_up, "w_ffn_down": dw_down}


def _local_step(x, tgt, p, w_in_t, conv_w, wo, wg_t, wu_t, wd):
    vp = _vector_params(p)
    ws_t = _small_rows(w_in_t)
    f = _mixer_forward(x, vp, w_in_t, ws_t, conv_w)
    (h2, act, dgate, dup, dx3, dx2, dmixg, dmixf, dn2w, dfw, loss) = _tail(
        x, f["mixg"], f["mixf"], tgt, wo, vp["n2w"], wg_t, wu_t, wd, vp["fw"])
    grad_x, dw_in, dconv, small = _mixer_backward(x, vp, w_in_t, ws_t, conv_w, f, dx2, dmixg, dmixf)
    grads = {"w_in": dw_in, "gdn_conv_w": dconv, **_late_grads(f, h2, act, dgate, dup, dx3, dx2)}
    vec = _pack_vectors(small["dn1w"], dn2w, dfw, small["gsum"], small["donw"], small["dfnw"], loss)
    return grad_x[0], grads, vec


def _my_place():
    return lax.axis_index("x"), lax.axis_index("y"), lax.axis_index("c")


def _peers():
    x, y, c = _my_place()
    peers = []
    for k in range(1, N_DEV):
        px = 1 - x if k & 4 else x
        py = 1 - y if k & 2 else y
        pc = 1 - c if k & 1 else c
        peers.append(((px, py, pc), 4 * px + 2 * py + pc))
    return 4 * x + 2 * y + c, peers


def _spread_copies(kind, srcs, lands, send_sems, recv_sems):
    me, peers = _peers()
    kinds = [kind] * len(srcs) if isinstance(kind, str) else kind
    remote, local = [], []
    for i, (kd, src, land) in enumerate(zip(kinds, srcs, lands)):
        for k, (dev, idx) in enumerate(peers):
            remote.append(pltpu.make_async_remote_copy(
                src_ref=src if kd == "gather" else src.at[idx], dst_ref=land.at[me],
                send_sem=send_sems.at[i * (N_DEV - 1) + k], recv_sem=recv_sems.at[i * (N_DEV - 1) + k],
                device_id=dev, device_id_type=MESH))
        local.append((src if kd == "gather" else src.at[me], land.at[me]))
    return remote, local


def _gather_two_level(arrays, name):
    n = len(arrays)

    def body(*refs):
        srcs, lands = refs[:n], refs[n:2 * n]
        send_sems, recv_sems, local_sems = refs[2 * n:]
        x, y, c = _my_place()
        me, sibling = (x, y, c), (x, y, 1 - c)
        chips = [(1 - x, y), (x, 1 - y), (1 - x, 1 - y)]

        def index(p):
            return 4 * p[0] + 2 * p[1] + p[2]

        def copy(i, k, block, to, src=None):
            blk = lands[i].at[index(block)]
            return pltpu.make_async_remote_copy(
                src_ref=blk if src is None else src, dst_ref=blk, send_sem=send_sems.at[7 * i + k],
                recv_sem=recv_sems.at[7 * i + k], device_id=to, device_id_type=MESH)

        mine = [pltpu.make_async_copy(srcs[i], lands[i].at[index(me)], local_sems.at[i]) for i in range(n)]
        for cp in mine:
            cp.start()
        first = []
        for i in range(n):
            first.append(copy(i, 0, me, sibling, src=srcs[i]))
            first += [copy(i, 1 + j, me, (*chip, c), src=srcs[i]) for j, chip in enumerate(chips)]
        for cp in first:
            cp.start()
        passed = []
        for i in range(n):
            for j, chip in enumerate(chips):
                copy(i, 1 + j, (*chip, c), me).wait_recv()
                passed.append(copy(i, 4 + j, (*chip, c), sibling))
                passed[-1].start()
        for i in range(n):
            copy(i, 0, sibling, me).wait_recv()
            for j, chip in enumerate(chips):
                copy(i, 4 + j, (*chip, 1 - c), me).wait_recv()
        for cp in first + passed:
            cp.wait_send()
        for cp in mine:
            cp.wait()

    return pl.pallas_call(
        body, name=name,
        out_shape=[jax.ShapeDtypeStruct((N_DEV,) + a.shape, a.dtype) for a in arrays],
        in_specs=[pl.BlockSpec(memory_space=pl.ANY)] * n, out_specs=[pl.BlockSpec(memory_space=pl.ANY)] * n,
        scratch_shapes=[pltpu.SemaphoreType.DMA((7 * n,)), pltpu.SemaphoreType.DMA((7 * n,)),
                        pltpu.SemaphoreType.DMA((n,))],
    )(*arrays)


def _land_shape(kind, a):
    return (N_DEV,) + a.shape if kind == "gather" else a.shape


HBM = pl.BlockSpec(memory_space=pltpu.HBM)
SEM = pl.BlockSpec(memory_space=pltpu.SEMAPHORE)


def _hbm(a):
    return pltpu.with_memory_space_constraint(a, pltpu.HBM)


def _carry_operands(kind, arrays):
    n = len(arrays)
    kinds = [kind] * n if isinstance(kind, str) else kind
    lands = [lax.empty(_land_shape(kd, a), a.dtype) for kd, a in zip(kinds, arrays)]
    sems = [pltpu.SemaphoreType.DMA((n * (N_DEV - 1),))] * 2
    return ([_hbm(a) for a in list(arrays) + lands], [HBM] * (2 * n),
            sems + [pltpu.HBM(a.shape, a.dtype) for a in list(arrays) + lands], [SEM] * 2 + [HBM] * (2 * n),
            pltpu.SemaphoreType.DMA((n,)))


def _carry_step(kind, in_refs, send_sems, recv_sems, local_sems, step, n_steps):
    n = len(in_refs) // 2
    remote, local = _spread_copies(kind, in_refs[:n], in_refs[n:], send_sems, recv_sems)
    copies = [pltpu.make_async_copy(s, d, local_sems.at[i]) for i, (s, d) in enumerate(local)]
    per_step = -(-len(remote) // n_steps)
    for s in range(-(-len(remote) // per_step)):
        @pl.when(step == s)
        def _():
            for cp in remote[s * per_step:(s + 1) * per_step]:
                cp.start()
            if s == 0:
                for cp in copies:
                    cp.start()

    @pl.when(step == n_steps - 1)
    def _():
        for cp in copies:
            cp.wait()


def _carry_state(extra_out):
    n = (len(extra_out) - 2) // 2
    return list(extra_out[2:2 + n]), list(extra_out[2 + n:]), extra_out[0], extra_out[1]


def _spread_start(arrays, kind, name):
    n = len(arrays)

    def body(*refs):
        srcs, lands = refs[:n], refs[n:2 * n]
        send_sems, recv_sems = refs[2 * n], refs[2 * n + 1]
        token = refs[4 * n + 2]
        local_sems = refs[4 * n + 3]
        remote, local = _spread_copies(kind, srcs, lands, send_sems, recv_sems)
        for cp in remote:
            cp.start()
        copies = [pltpu.make_async_copy(s, d, local_sems.at[i]) for i, (s, d) in enumerate(local)]
        for cp in copies:
            cp.start()
        for cp in copies:
            cp.wait()
        token[...] = jnp.zeros_like(token)

    sems = (pltpu.SemaphoreType.DMA((n * (N_DEV - 1),)),) * 2
    kinds = [kind] * n if isinstance(kind, str) else kind
    lands = [lax.empty(_land_shape(kd, a), a.dtype) for kd, a in zip(kinds, arrays)]
    out = pl.pallas_call(
        body, name=name,
        out_shape=sems + tuple(pltpu.HBM(a.shape, a.dtype) for a in list(arrays) + lands)
        + (jax.ShapeDtypeStruct((8, LANE), F32),),
        in_specs=[HBM] * (2 * n), out_specs=tuple([SEM] * 2 + [HBM] * (2 * n) + [pl.BlockSpec(memory_space=pltpu.VMEM)]),
        input_output_aliases={j: 2 + j for j in range(2 * n)},
        scratch_shapes=[pltpu.SemaphoreType.DMA((n,))],
        compiler_params=pltpu.CompilerParams(has_side_effects=pltpu.SideEffectType.DATAFLOW_SIDE_EFFECTING),
    )(*[_hbm(a) for a in arrays], *[_hbm(a) for a in lands])
    return (list(out[2:2 + n]), list(out[2 + n:2 + 2 * n]), out[0], out[1]), out[-1]


def _spread_wait(state, kind, after, name):
    srcs, lands, send_sems, recv_sems = state
    n = len(srcs)
    after = list(after) if isinstance(after, (list, tuple)) else [after]

    def body(*refs):
        remote, _ = _spread_copies(kind, refs[:n], refs[n:2 * n], refs[2 * n], refs[2 * n + 1])
        for cp in remote:
            cp.wait_send()
        for cp in remote:
            cp.wait_recv()

    out = pl.pallas_call(
        body, name=name,
        out_shape=tuple(pltpu.HBM(a.shape, a.dtype) for a in srcs + lands),
        in_specs=[HBM] * (2 * n) + [SEM, SEM] + [pl.BlockSpec(memory_space=pl.ANY)] * len(after),
        out_specs=tuple([HBM] * (2 * n)),
        input_output_aliases={j: j for j in range(2 * n)},
        compiler_params=pltpu.CompilerParams(has_side_effects=pltpu.SideEffectType.DATAFLOW_SIDE_EFFECTING),
    )(*srcs, *lands, send_sems, recv_sems, *after)
    return list(out[n:])


ADAM_ROWS = 128
ADAM_COLS = 256


def _adam_math(g, w, m, v):
    nm = ADAM_B1 * m + (1.0 - ADAM_B1) * g
    nv = ADAM_B2 * v + (1.0 - ADAM_B2) * (g * g)
    m_hat = nm / (1.0 - ADAM_B1 ** ADAM_STEP)
    v_hat = nv / (1.0 - ADAM_B2 ** ADAM_STEP)
    return -ADAM_LR * (m_hat / (jnp.sqrt(v_hat) + ADAM_EPS) + ADAM_WD * w), nm, nv


def _sum_parts(p_ref):
    g = p_ref[0].astype(F32)
    for s in range(1, N_DEV):
        g = g + p_ref[s].astype(F32)
    return g


def _adam_matrix(parts, w, m, v, name):
    _, r, c = w.shape
    rb = ADAM_ROWS if r % ADAM_ROWS == 0 else r
    cb = ADAM_COLS if (rb == r and c % ADAM_COLS == 0) else c

    def body(p_ref, w_ref, m_ref, v_ref, g_ref, d_ref, nm_ref, nv_ref):
        g = _sum_parts(p_ref)
        g_ref[0] = g
        d_ref[0], nm_ref[0], nv_ref[0] = _adam_math(g, w_ref[0], m_ref[0], v_ref[0])

    blk = pl.BlockSpec((1, rb, cb), lambda i, j: (0, i, j))
    return pl.pallas_call(
        body, name=name, grid=(r // rb, c // cb),
        in_specs=[pl.BlockSpec((N_DEV, rb, cb), lambda i, j: (0, i, j)), blk, blk, blk],
        out_specs=[blk] * 4, out_shape=[jax.ShapeDtypeStruct(w.shape, F32)] * 4,
        compiler_params=_params(("arbitrary", "arbitrary")),
    )(parts, w, m, v)


def _adam_vectors(parts, ws, ms, vs):
    nv = len(ws)

    def body(*refs):
        p_ref = refs[0]
        w_refs, m_refs, v_refs = refs[1:1 + nv], refs[1 + nv:1 + 2 * nv], refs[1 + 2 * nv:1 + 3 * nv]
        outs = refs[1 + 3 * nv:]
        g_all = _sum_parts(p_ref)
        for i in range(nv):
            n = w_refs[i].shape[1]
            g = g_all[i:i + 1, 0:n]
            d, nm, nvv = _adam_math(g, w_refs[i][...], m_refs[i][...], v_refs[i][...])
            outs[i][...] = g
            outs[nv + i][...] = d
            outs[2 * nv + i][...] = nm
            outs[3 * nv + i][...] = nvv
        outs[4 * nv][...] = g_all[LOSS_ROW:LOSS_ROW + 1, 0:1]

    shapes = [jax.ShapeDtypeStruct(a.shape, F32) for a in ws]
    out = pl.pallas_call(body, name="adam_vectors", out_shape=shapes * 4 + [jax.ShapeDtypeStruct((1, 1), F32)],
                         compiler_params=_params())(parts, *ws, *ms, *vs)
    return out[:nv], out[nv:2 * nv], out[2 * nv:3 * nv], out[3 * nv:4 * nv], out[4 * nv]


MATRICES = (("w_in", 1), ("gdn_conv_w", 1), ("w_out", 0), ("w_ffn_gate", 1), ("w_ffn_up", 1), ("w_ffn_down", 0))
TRANSPOSED = ("w_in", "w_ffn_gate", "w_ffn_up")
WEIGHTS = ("norm1_w", "w_in", "gdn_conv_w", "gdn_A_log", "gdn_dt_bias", "gdn_out_norm_w", "fox_f_bias", "fox_q_norm_w",
           "fox_k_norm_w", "w_out", "norm2_w", "w_ffn_gate", "w_ffn_up", "w_ffn_down", "final_norm_w")


def _join(blocks, axis):
    _, r, c = blocks.shape
    if axis == 0:
        return blocks.reshape(N_DEV * r, c)
    return blocks.transpose(1, 0, 2).reshape(r, N_DEV * c)


def _cut(full, axis):
    r, c = full.shape
    if axis == 0:
        return full.reshape(N_DEV, r // N_DEV, c)
    return full.reshape(r, N_DEV, c // N_DEV).transpose(1, 0, 2)


def kernel(x, norm1_w, w_in, gdn_conv_w, gdn_A_log, gdn_dt_bias, gdn_out_norm_w, fox_f_bias, fox_q_norm_w, fox_k_norm_w, w_out, norm2_w, w_ffn_gate, w_ffn_up, w_ffn_down, final_norm_w, loss_target, m_norm1_w, m_w_in, m_gdn_conv_w, m_gdn_A_log, m_gdn_dt_bias, m_gdn_out_norm_w, m_fox_f_bias, m_fox_q_norm_w, m_fox_k_norm_w, m_w_out, m_norm2_w, m_w_ffn_gate, m_w_ffn_up, m_w_ffn_down, m_final_norm_w, v_norm1_w, v_w_in, v_gdn_conv_w, v_gdn_A_log, v_gdn_dt_bias, v_gdn_out_norm_w, v_fox_f_bias, v_fox_q_norm_w, v_fox_k_norm_w, v_w_out, v_norm2_w, v_w_ffn_gate, v_w_ffn_up, v_w_ffn_down, v_final_norm_w):
    w = dict(norm1_w=norm1_w, w_in=w_in, gdn_conv_w=gdn_conv_w, gdn_A_log=gdn_A_log, gdn_dt_bias=gdn_dt_bias,
             gdn_out_norm_w=gdn_out_norm_w, fox_f_bias=fox_f_bias, fox_q_norm_w=fox_q_norm_w, fox_k_norm_w=fox_k_norm_w,
             w_out=w_out, norm2_w=norm2_w, w_ffn_gate=w_ffn_gate, w_ffn_up=w_ffn_up, w_ffn_down=w_ffn_down,
             final_norm_w=final_norm_w)
    m = dict(norm1_w=m_norm1_w, w_in=m_w_in, gdn_conv_w=m_gdn_conv_w, gdn_A_log=m_gdn_A_log, gdn_dt_bias=m_gdn_dt_bias,
             gdn_out_norm_w=m_gdn_out_norm_w, fox_f_bias=m_fox_f_bias, fox_q_norm_w=m_fox_q_norm_w,
             fox_k_norm_w=m_fox_k_norm_w, w_out=m_w_out, norm2_w=m_norm2_w, w_ffn_gate=m_w_ffn_gate,
             w_ffn_up=m_w_ffn_up, w_ffn_down=m_w_ffn_down, final_norm_w=m_final_norm_w)
    v = dict(norm1_w=v_norm1_w, w_in=v_w_in, gdn_conv_w=v_gdn_conv_w, gdn_A_log=v_gdn_A_log, gdn_dt_bias=v_gdn_dt_bias,
             gdn_out_norm_w=v_gdn_out_norm_w, fox_f_bias=v_fox_f_bias, fox_q_norm_w=v_fox_q_norm_w,
             fox_k_norm_w=v_fox_k_norm_w, w_out=v_w_out, norm2_w=v_norm2_w, w_ffn_gate=v_w_ffn_gate,
             w_ffn_up=v_w_ffn_up, w_ffn_down=v_w_ffn_down, final_norm_w=v_final_norm_w)
    late = ("w_out", "w_ffn_gate", "w_ffn_up", "w_ffn_down")
    xs, tgt = x[0], loss_target[0]
    vp = _vector_params({n: w[n] for n in VECTORS})

    def rows_of(d, n):
        return d[n].transpose(0, 2, 1) if n in TRANSPOSED else d[n]

    wr, mr, vr = ({n: rows_of(d, n) for n, _ in MATRICES} for d in (w, m, v))

    w_in_blocks, conv_blocks = _gather_two_level([wr["w_in"][0].astype(WIRE), w["gdn_conv_w"][0]], "gather_in")
    w_t = _join(w_in_blocks, 0)
    conv_w = _join(conv_blocks, 1)
    f = _mixer_forward(xs, vp, w_t, _small_rows(w_t), conv_w, ("gather", [wr[n][0].astype(WIRE) for n in late]))
    full = {n: _join(b, 0) for n, b in zip(late, _spread_wait(f["carried"], "gather", f["mixg"], "gather_late_wait"))}

    (h2, act, dgate, dup, dx3, dx2, dmixg, dmixf, dn2w, dfw, loss) = _tail(
        xs, f["mixg"], f["mixf"], tgt, full["w_out"], vp["n2w"], full["w_ffn_gate"], full["w_ffn_up"],
        full["w_ffn_down"], vp["fw"])
    dlate = _late_grads(f, h2, act, dgate, dup, dx3, dx2)

    grad_x, dw_in, dconv, small, state_grads, state_own = _mixer_backward(
        xs, vp, w_t, _small_rows(w_t), conv_w, f, dx2, dmixg, dmixf, ("scatter", [_cut(dlate[n], 0) for n in late]),
        scatter_own=True)
    vec = _pack_vectors(small["dn1w"], dn2w, dfw, small["gsum"], small["donw"], small["dfnw"], loss)

    state_vec, token = _spread_start([vec], "gather", "vectors_start")
    parts = dict(zip(late, _spread_wait(state_grads, "scatter", token, "grads_late_wait")))
    results = [{}, {}, {}, {}]

    updated = {}

    def update(n):
        out = _adam_matrix(parts[n], wr[n], mr[n], vr[n], "adam_" + n)
        updated[n] = out[0]
        for d, a in zip(results, out):
            d[n] = a.transpose(0, 2, 1) if n in TRANSPOSED else a

    for n in late:
        update(n)
    (parts_vec,) = _spread_wait(state_vec, "gather", [updated[n] for n in late], "vectors_wait")
    row = lambda a: a.reshape(1, -1)
    *vec_out, total_loss = _adam_vectors(parts_vec, [row(w[n]) for n in VECTORS], [row(m[n]) for n in VECTORS],
                                         [row(v[n]) for n in VECTORS])
    ready = [parts_vec] + [d["w_in"] for d in (wr, mr, vr)]
    parts["w_in"], parts["gdn_conv_w"] = _spread_wait(state_own, "scatter", ready, "own_wait")
    update("w_in")
    update("gdn_conv_w")
    for d, arrs in zip(results, vec_out):
        for n, a in zip(VECTORS, arrs):
            d[n] = a.reshape(w[n].shape)
    return (total_loss[0, 0], grad_x, *[d[n] for d in results for n in WEIGHTS])
```

```python
import functools

import jax
import jax.numpy as jnp
from jax import lax
from jax.experimental import pallas as pl
from jax.experimental.pallas import tpu as pltpu

F32 = jnp.float32
MXU = jnp.bfloat16
WIRE = jnp.bfloat16
HI = lax.Precision.HIGHEST
EPS = 1e-6

N_DEV = 8
HEADS = 8
DH = 64
WIDTH = HEADS * DH
CHUNK = 64
LANE = 128
ROW_ALIGN = 16
TB = 256
QB = 256
VMEM_LIMIT = 60 * 1024 * 1024

ADAM_LR = 0.001
ADAM_B1 = 0.9
ADAM_B2 = 0.999
ADAM_EPS = 1e-08
ADAM_WD = 0.01
ADAM_STEP = 10

MESH = pl.DeviceIdType.MESH


def _params(sem=None):
    return pltpu.CompilerParams(dimension_semantics=sem, vmem_limit_bytes=VMEM_LIMIT)


def _resident(shape):
    n = len(shape)
    return pl.BlockSpec(shape, lambda *_: (0,) * n, pipeline_mode=pl.Buffered(1))


def _dot(a, b):
    return jnp.dot(a.astype(MXU), b.astype(MXU), preferred_element_type=F32)


def _dot_nt(a, b):
    return lax.dot_general(a.astype(MXU), b.astype(MXU), (((1,), (1,)), ((), ())), preferred_element_type=F32)


def _dot_tn(a, b):
    return lax.dot_general(a.astype(MXU), b.astype(MXU), (((0,), (0,)), ((), ())), preferred_element_type=F32)


def _hdot(a, b):
    return jnp.dot(a, b, precision=HI, preferred_element_type=F32)


def _hdot_nt(a, b):
    return lax.dot_general(a, b, (((1,), (1,)), ((), ())), precision=HI, preferred_element_type=F32)


def _hdot_tn(a, b):
    return lax.dot_general(a, b, (((0,), (0,)), ((), ())), precision=HI, preferred_element_type=F32)


def _sigmoid(x):
    return 0.5 * jnp.tanh(0.5 * x) + 0.5


def _softplus(x):
    return jnp.maximum(x, 0.0) + jnp.log(1.0 + jnp.exp(-jnp.abs(x)))


def _head_sum_matrix():
    ri = lax.broadcasted_iota(jnp.int32, (LANE, LANE), 0) // DH
    ci = lax.broadcasted_iota(jnp.int32, (LANE, LANE), 1) // DH
    return (ri == ci).astype(F32)


def _group_sum(a, ones_matrix):
    hi = a.astype(jnp.bfloat16)
    lo = (a - hi.astype(F32)).astype(jnp.bfloat16)
    m = ones_matrix.astype(jnp.bfloat16)
    return jnp.dot(hi, m, preferred_element_type=F32) + jnp.dot(lo, m, preferred_element_type=F32)


def _shift_down(x, s):
    return pltpu.roll(x, s, 0)


def _shift_up(x, s):
    return pltpu.roll(x, x.shape[0] - s, 0)


ROWS_A = 4 * WIDTH
ROWS_B = ROWS_A + 2 * HEADS
ROWS_C = ROWS_B + 4 * WIDTH


def _small_rows(w_t):
    return jnp.concatenate([w_t[ROWS_A:ROWS_B], w_t[ROWS_C:], jnp.zeros((LANE - 3 * HEADS, w_t.shape[1]), w_t.dtype)])


def _inproj(x, n1w, w_t, ws_t):
    t, d = x.shape
    tb = min(TB, t)

    def body(x_ref, nw_ref, wt_ref, ws_ref, h_ref, pm_ref, ps_ref):
        xv = x_ref[...]
        r = lax.rsqrt(jnp.mean(xv * xv, axis=-1, keepdims=True) + EPS)
        h = (xv * r * nw_ref[...]).astype(MXU)
        h_ref[...] = h
        pm_ref[:, 0:ROWS_A] = _dot_nt(h, wt_ref[0:ROWS_A, :])
        pm_ref[:, ROWS_A:2 * ROWS_A] = _dot_nt(h, wt_ref[ROWS_B:ROWS_C, :])
        ps_ref[...] = _dot_nt(h, ws_ref[...])

    return pl.pallas_call(
        body, name="inproj", grid=(t // tb,),
        in_specs=[pl.BlockSpec((tb, d), lambda i: (i, 0)), _resident((1, d)), _resident(w_t.shape), _resident((LANE, d))],
        out_specs=[pl.BlockSpec((tb, d), lambda i: (i, 0)), pl.BlockSpec((tb, 2 * ROWS_A), lambda i: (i, 0)),
                   pl.BlockSpec((tb, LANE), lambda i: (i, 0))],
        out_shape=[jax.ShapeDtypeStruct((t, d), MXU), jax.ShapeDtypeStruct((t, 2 * ROWS_A), F32),
                   jax.ShapeDtypeStruct((t, LANE), F32)],
        compiler_params=_params(("arbitrary",)),
    )(x, n1w, w_t, ws_t)


def _inproj_bwd(x, n1w, dx2, dgdn, dz, dfox, dfg, dps, w_t, ws_t, carry=None):
    t, d = x.shape
    tb = min(TB, t)
    w3 = 3 * WIDTH
    c_in, c_in_specs, c_out_shape, c_out_specs, c_sem = _carry_operands(*carry) if carry else ([], [], [], [], None)
    n_c = len(c_in)

    def body(*refs):
        x_ref, nw_ref, dx2_ref, dgdn_ref, dz_ref, dfox_ref, dfg_ref, dps_ref, wm_ref, ws_ref = refs[:10]
        gx_ref, dnw_ref = refs[10 + n_c:12 + n_c]
        if carry:
            _carry_step(carry[0], refs[10:10 + n_c], refs[12 + n_c], refs[13 + n_c], refs[-1], pl.program_id(0), t // tb)
        dh = _dot(dgdn_ref[...], wm_ref[0:w3, :])
        dh += _dot(dz_ref[...], wm_ref[w3:ROWS_A, :])
        dh += _dot(dfox_ref[...], wm_ref[ROWS_B:ROWS_B + w3, :])
        dh += _dot(dfg_ref[...], wm_ref[ROWS_B + w3:ROWS_C, :])
        dh += _dot(dps_ref[...], ws_ref[...])
        xv = x_ref[...]
        r = lax.rsqrt(jnp.mean(xv * xv, axis=-1, keepdims=True) + EPS)
        xn = xv * r

        @pl.when(pl.program_id(0) == 0)
        def _():
            dnw_ref[...] = jnp.zeros_like(dnw_ref)

        dnw_ref[...] += jnp.sum(dh * xn, axis=0, keepdims=True)
        g = dh * nw_ref[...]
        gx_ref[0] = dx2_ref[...] + r * (g - xn * jnp.mean(g * xn, axis=-1, keepdims=True))

    def tok(n):
        return pl.BlockSpec((tb, n), lambda i: (i, 0))

    out = pl.pallas_call(
        body, name="inproj_bwd", grid=(t // tb,),
        in_specs=[tok(d), _resident((1, d)), tok(d), tok(w3), tok(WIDTH), tok(w3), tok(WIDTH), tok(LANE),
                  _resident(w_t.shape), _resident(ws_t.shape)] + c_in_specs,
        out_specs=[pl.BlockSpec((1, tb, d), lambda i: (0, i, 0)), pl.BlockSpec((1, d), lambda i: (0, 0))] + c_out_specs,
        out_shape=[jax.ShapeDtypeStruct((1, t, d), F32), jax.ShapeDtypeStruct((1, d), F32)] + c_out_shape,
        input_output_aliases={10 + j: 4 + j for j in range(n_c)},
        scratch_shapes=[c_sem] if carry else [],
        compiler_params=_params(("arbitrary",)),
    )(x, n1w, dx2, dgdn, dz, dfox, dfg, dps, w_t, ws_t, *c_in)
    return (out[0], out[1], _carry_state(out[2:])) if carry else tuple(out)


def _gate_lanes(shape):
    lane = lax.broadcasted_iota(jnp.int32, shape, 1)
    return lane < HEADS, (lane >= HEADS) & (lane < 2 * HEADS), (lane >= 2 * HEADS) & (lane < 3 * HEADS)


def _block_masks():
    ri = lax.broadcasted_iota(jnp.int32, (LANE, LANE), 0)
    ci = lax.broadcasted_iota(jnp.int32, (LANE, LANE), 1)
    same = (ri // CHUNK) == (ci // CHUNK)
    return ((ri >= ci).astype(F32), (ri <= ci).astype(F32), (same & (ri >= ci)).astype(F32),
            (same & (ri <= ci)).astype(F32), same.astype(F32))


def _gates(ps, gparams):
    t = ps.shape[0]
    nb = t // LANE

    def body(ps_ref, gp_ref, out_ref, run_ref, tot_ref, runt_ref):
        p = ps_ref[...]
        is_b, is_a, is_f = _gate_lanes(p.shape)
        z = p + gp_ref[0:1, :]
        neg_exp_a = -jnp.exp(gp_ref[1:2, :])
        glog = neg_exp_a * _softplus(z)
        logf = -_softplus(-z)
        out_ref[...] = jnp.where(is_b, _sigmoid(p), jnp.where(is_a, glog, jnp.where(is_f, logf, 0.0)))
        tril, _, tril_c, _, same_c = _block_masks()
        off = jnp.zeros((1, LANE), F32)
        for b in range(nb):
            rows = slice(b * LANE, (b + 1) * LANE)
            blk = out_ref[rows, :]
            ga = jnp.where(is_a[:LANE], blk, 0.0)
            fb = _hdot(tril, jnp.where(is_f[:LANE], blk, 0.0)) + off
            run = fb + _hdot(tril_c, ga)
            run_ref[rows, :] = run
            runt_ref[:, rows] = run.T
            tot_ref[rows, :] = _hdot(same_c, ga)
            off = fb[LANE - 1:LANE, :]

    return pl.pallas_call(
        body, name="gates",
        out_shape=[jax.ShapeDtypeStruct((t, LANE), F32)] * 3 + [jax.ShapeDtypeStruct((LANE, t), F32)],
        compiler_params=_params(),
    )(ps, gparams)


def _gates_bwd(ps, gparams, dcol_g, dtot_g, dcol_f, drow_g, drow_f):
    t = ps.shape[0]
    nb = t // LANE

    def body(ps_ref, gp_ref, dcg_ref, dtg_ref, dcf_ref, drg_ref, drf_ref, dps_ref, sums_ref, dl_ref, d0_ref, tr_ref):
        p = ps_ref[...]
        is_b, is_a, is_f = _gate_lanes(p.shape)
        _, triu, _, triu_c, same_c = _block_masks()
        tr_ref[...] = jnp.zeros_like(tr_ref)
        off = jnp.zeros((1, LANE), F32)
        for b in reversed(range(nb)):
            rows = slice(b * LANE, (b + 1) * LANE)
            tr_ref[HEADS:2 * HEADS, :] = drg_ref[:, rows]
            tr_ref[2 * HEADS:3 * HEADS, :] = drf_ref[:, rows]
            d = dcg_ref[rows, :] + dcf_ref[rows, :] + tr_ref[...].T
            d0_ref[rows, :] = d
            dlf = _hdot(triu, jnp.where(is_f[:LANE], d, 0.0)) + off
            dla = (_hdot(triu_c, jnp.where(is_a[:LANE], d, 0.0))
                   + _hdot(same_c, jnp.where(is_a[:LANE], dtg_ref[rows, :], 0.0)))
            dl_ref[rows, :] = dlf + dla
            off = dlf[0:1, :]
        z = p + gp_ref[0:1, :]
        neg_exp_a = -jnp.exp(gp_ref[1:2, :])
        sb = _sigmoid(p)
        glog = neg_exp_a * _softplus(z)
        dl = dl_ref[...]
        dp = jnp.where(is_b, d0_ref[...] * sb * (1.0 - sb),
                       jnp.where(is_a, dl * neg_exp_a * _sigmoid(z), jnp.where(is_f, dl * _sigmoid(-z), 0.0)))
        dps_ref[...] = dp
        s_a = jnp.sum(jnp.where(is_a, dl * glog, 0.0), axis=0, keepdims=True)
        s_p = jnp.sum(jnp.where(is_b, 0.0, dp), axis=0, keepdims=True)
        row = lax.broadcasted_iota(jnp.int32, (8, LANE), 0)
        from_a = pltpu.roll(jnp.where(row == 0, s_a, jnp.where(row == 1, s_p, 0.0)), LANE - HEADS, 1)
        from_f = pltpu.roll(jnp.where(row == 2, s_p, 0.0), LANE - 2 * HEADS, 1)
        lane = lax.broadcasted_iota(jnp.int32, (8, LANE), 1)
        sums_ref[...] = jnp.where(lane < HEADS, from_a + from_f, 0.0)

    return pl.pallas_call(
        body, name="gates_bwd",
        out_shape=[jax.ShapeDtypeStruct((t, LANE), F32), jax.ShapeDtypeStruct((8, LANE), F32)],
        scratch_shapes=[pltpu.VMEM((t, LANE), F32), pltpu.VMEM((t, LANE), F32), pltpu.VMEM((LANE, LANE), F32)],
        compiler_params=_params(),
    )(ps, gparams, dcol_g, dtot_g, dcol_f, drow_g, drow_f)


def _conv(xv, w):
    acc = w[3:4, :] * xv
    for s in range(1, 4):
        acc += w[3 - s:4 - s, :] * _shift_down(xv, s)
    return acc


PREP_ROWS = 256
HALO = 8
PREP_UNROLL = 2


def _tile_loop(n, tile, init):
    if n % PREP_UNROLL:
        return lax.fori_loop(0, n, tile, init)

    def trip(g, carry):
        for u in range(PREP_UNROLL):
            carry = tile(g * PREP_UNROLL + u, carry)
        return carry

    return lax.fori_loop(0, n // PREP_UNROLL, trip, init)


def _tile_rows(r):
    return pl.ds(pl.multiple_of(r * PREP_ROWS, PREP_ROWS), PREP_ROWS)


def _gdn_prep(pm, conv_w):
    t = pm.shape[0]
    nj = WIDTH // LANE
    win = PREP_ROWS + HALO

    def body(x_ref, w_ref, o_ref, xp_ref):
        kind = pl.program_id(0)
        xp_ref[0:HALO, :] = jnp.zeros((HALO, LANE), F32)
        xp_ref[HALO:, :] = x_ref[...]
        w = w_ref[...]
        hs = _head_sum_matrix()

        def tile(r, _, normed):
            xw = xp_ref[pl.ds(pl.multiple_of(r * PREP_ROWS, PREP_ROWS), win), :]
            acc = _conv(xw, w)[HALO:]
            out = acc * _sigmoid(acc)
            if normed:
                out = out * lax.rsqrt(_group_sum(out * out, hs) + EPS)
            o_ref[0, 0, _tile_rows(r), :] = out[:, :DH]
            o_ref[0, 1, _tile_rows(r), :] = out[:, DH:]
            return 0

        @pl.when(kind < 2)
        def _():
            _tile_loop(t // PREP_ROWS, functools.partial(tile, normed=True), 0)

        @pl.when(kind == 2)
        def _():
            _tile_loop(t // PREP_ROWS, functools.partial(tile, normed=False), 0)

    return pl.pallas_call(
        body, name="gdn_prep", grid=(3, nj),
        in_specs=[pl.BlockSpec((t, LANE), lambda i, j: (0, i * nj + j)),
                  pl.BlockSpec((4, LANE), lambda i, j: (0, i * nj + j))],
        out_specs=pl.BlockSpec((1, 2, t, DH), lambda i, j: (i, j, 0, 0)),
        out_shape=jax.ShapeDtypeStruct((3, HEADS, t, DH), F32),
        scratch_shapes=[pltpu.VMEM((t + HALO, LANE), F32)],
        compiler_params=_params(("arbitrary", "arbitrary")),
    )(pm, conv_w)


def _gdn_prep_bwd(pm, conv_w, dqkv):
    t = pm.shape[0]
    nj = WIDTH // LANE
    win = PREP_ROWS + 2 * HALO

    def body(x_ref, w_ref, d_ref, dx_ref, dw_ref, xp_ref, dp_ref):
        kind = pl.program_id(0)
        zeros = jnp.zeros((HALO, LANE), F32)
        for ref in (xp_ref, dp_ref):
            ref[0:HALO, :] = zeros
            ref[HALO + t:, :] = zeros
        xp_ref[HALO:HALO + t, :] = x_ref[...]
        dp_ref[HALO:HALO + t, 0:DH] = d_ref[0, 0]
        dp_ref[HALO:HALO + t, DH:] = d_ref[0, 1]
        w = w_ref[...]
        hs = _head_sum_matrix()
        rows = lax.broadcasted_iota(jnp.int32, (win, LANE), 0)
        in_tile = (rows >= HALO) & (rows < HALO + PREP_ROWS)

        def tile(r, dw, normed):
            start = pl.multiple_of(r * PREP_ROWS, PREP_ROWS)
            xw = xp_ref[pl.ds(start, win), :]
            dy = dp_ref[pl.ds(start, win), :]
            acc = _conv(xw, w)
            sg = _sigmoid(acc)
            if normed:
                y = acc * sg
                rn = lax.rsqrt(_group_sum(y * y, hs) + EPS)
                yn = y * rn
                dy = rn * (dy - yn * _group_sum(dy * yn, hs))
            dacc = dy * sg * (1.0 + acc * (1.0 - sg))
            dx = w[3:4, :] * dacc
            for s in range(1, 4):
                dx += w[3 - s:4 - s, :] * _shift_up(dacc, s)
            dx_ref[_tile_rows(r), :] = dx[HALO:HALO + PREP_ROWS]
            dm = jnp.where(in_tile, dacc, 0.0)
            return tuple(dw[i] + jnp.sum(dm * (xw if i == 3 else _shift_down(xw, 3 - i)), axis=0, keepdims=True)
                         for i in range(4))

        def run(normed):
            dw = _tile_loop(t // PREP_ROWS, functools.partial(tile, normed=normed),
                            tuple(jnp.zeros((1, LANE), F32) for _ in range(4)))
            for i in range(4):
                dw_ref[i:i + 1, :] = dw[i]

        pl.when(kind < 2)(functools.partial(run, True))
        pl.when(kind == 2)(functools.partial(run, False))

    return pl.pallas_call(
        body, name="gdn_prep_bwd", grid=(3, nj),
        in_specs=[pl.BlockSpec((t, LANE), lambda i, j: (0, i * nj + j)),
                  pl.BlockSpec((4, LANE), lambda i, j: (0, i * nj + j)),
                  pl.BlockSpec((1, 2, t, DH), lambda i, j: (i, j, 0, 0))],
        out_specs=[pl.BlockSpec((t, LANE), lambda i, j: (0, i * nj + j)),
                   pl.BlockSpec((4, LANE), lambda i, j: (0, i * nj + j))],
        out_shape=[jax.ShapeDtypeStruct((t, 3 * WIDTH), F32), jax.ShapeDtypeStruct((4, 3 * WIDTH), F32)],
        scratch_shapes=[pltpu.VMEM((t + 2 * HALO, LANE), F32), pltpu.VMEM((t + 2 * HALO, LANE), F32)],
        compiler_params=_params(("arbitrary", "arbitrary")),
    )(pm, conv_w, dqkv)


FOX_COL0 = 4 * WIDTH // LANE


def _fox_prep(pm, nw):
    t = pm.shape[0]
    nj = WIDTH // LANE

    def body(x_ref, w_ref, o_ref):
        kind = pl.program_id(0)
        hs = _head_sum_matrix()
        wk = w_ref[pl.ds(kind, 1), :]

        def tile(r, _, normed):
            out = x_ref[_tile_rows(r), :]
            if normed:
                out = out * lax.rsqrt(_group_sum(out * out, hs) * (1.0 / DH) + EPS) * wk
            o_ref[0, 0, _tile_rows(r), :] = out[:, :DH]
            o_ref[0, 1, _tile_rows(r), :] = out[:, DH:]
            return 0

        @pl.when(kind < 2)
        def _():
            _tile_loop(t // PREP_ROWS, functools.partial(tile, normed=True), 0)

        @pl.when(kind == 2)
        def _():
            _tile_loop(t // PREP_ROWS, functools.partial(tile, normed=False), 0)

    return pl.pallas_call(
        body, name="fox_prep", grid=(3, nj),
        in_specs=[pl.BlockSpec((t, LANE), lambda i, j: (0, FOX_COL0 + i * nj + j)),
                  pl.BlockSpec((3, LANE), lambda i, j: (0, 0))],
        out_specs=pl.BlockSpec((1, 2, t, DH), lambda i, j: (i, j, 0, 0)),
        out_shape=jax.ShapeDtypeStruct((3, HEADS, t, DH), F32),
        compiler_params=_params(("arbitrary", "arbitrary")),
    )(pm, nw)


def _fox_prep_bwd(pm, nw, dqkv):
    t = pm.shape[0]
    nj = WIDTH // LANE

    def body(x_ref, w_ref, d_ref, dx_ref, dw_ref):
        kind = pl.program_id(0)
        hs = _head_sum_matrix()
        wk = w_ref[pl.ds(kind, 1), :]

        def tile(r, dw):
            xv = x_ref[_tile_rows(r), :]
            rn = lax.rsqrt(_group_sum(xv * xv, hs) * (1.0 / DH) + EPS)
            xn = xv * rn
            d = jnp.concatenate([d_ref[0, 0, _tile_rows(r), :], d_ref[0, 1, _tile_rows(r), :]], axis=1)
            g = d * wk
            dx_ref[_tile_rows(r), :] = rn * (g - xn * _group_sum(g * xn, hs) * (1.0 / DH))
            return dw + jnp.sum(d * xn, axis=0, keepdims=True)

        def copy_tile(r, _):
            dx_ref[_tile_rows(r), :] = jnp.concatenate([d_ref[0, 0, _tile_rows(r), :], d_ref[0, 1, _tile_rows(r), :]], axis=1)
            return 0

        @pl.when(kind < 2)
        def _():
            dw_ref[0, 0] = _tile_loop(t // PREP_ROWS, tile, jnp.zeros((1, LANE), F32))

        @pl.when(kind == 2)
        def _():
            _tile_loop(t // PREP_ROWS, copy_tile, 0)
            dw_ref[0, 0] = jnp.zeros((1, LANE), F32)

    return pl.pallas_call(
        body, name="fox_prep_bwd", grid=(3, nj),
        in_specs=[pl.BlockSpec((t, LANE), lambda i, j: (0, FOX_COL0 + i * nj + j)),
                  pl.BlockSpec((3, LANE), lambda i, j: (0, 0)),
                  pl.BlockSpec((1, 2, t, DH), lambda i, j: (i, j, 0, 0))],
        out_specs=[pl.BlockSpec((t, LANE), lambda i, j: (0, i * nj + j)),
                   pl.BlockSpec((1, 1, 1, LANE), lambda i, j: (i, j, 0, 0))],
        out_shape=[jax.ShapeDtypeStruct((t, 3 * WIDTH), F32), jax.ShapeDtypeStruct((3, nj, 1, LANE), F32)],
        compiler_params=_params(("arbitrary", "arbitrary")),
    )(pm, nw, dqkv)


SC = 256
CPS = SC // CHUNK
GDN_HP_FWD = 4
GDN_HP_BWD = 2
Q_SCALE = DH ** -0.5


def _sc_masks():
    ri = lax.broadcasted_iota(jnp.int32, (SC, SC), 0)
    ci = lax.broadcasted_iota(jnp.int32, (SC, SC), 1)
    same = (ri // CHUNK) == (ci // CHUNK)
    return same & (ri >= ci), same & (ri > ci), ri == ci


def _unit_lower_inverses(ms, eye):
    invs = [jnp.where(eye, 1.0, 0.0) + m for m in ms]
    for _ in range(5):
        ms = [_dot(m, m) for m in ms]
        invs = [inv + _dot(inv, m) for inv, m in zip(invs, ms)]
    return invs


def _lane_col(blk, lane_idx):
    lane = lax.broadcasted_iota(jnp.int32, blk.shape, 1)
    return jnp.sum(jnp.where(lane == lane_idx, blk, 0.0), axis=1, keepdims=True)


def _to_lane(col, lane_idx):
    lane = lax.broadcasted_iota(jnp.int32, (col.shape[0], LANE), 1)
    return jnp.where(lane == lane_idx, col, 0.0)


def _gdn_columns(gates_ref, run_ref, tot_ref, runt_ref, rows, h):
    return (_lane_col(gates_ref[rows, :], h), _lane_col(run_ref[rows, :], HEADS + h),
            _lane_col(tot_ref[rows, :], HEADS + h), runt_ref[pl.ds(h, 1), rows])


def _gdn_local(q, k, beta, gc, gl, grow, causal, with_kk=True):
    decay = jnp.exp(jnp.where(causal, gc - grow, -1e30))
    egc = jnp.exp(gc)
    ekd = jnp.exp(gl - gc)
    qs = q * Q_SCALE
    kb = k * beta
    if with_kk:
        both = _dot_nt(jnp.concatenate([kb, qs], axis=0), k)
        kk, qk = both[:SC], both[SC:]
    else:
        kk, qk = None, _dot_nt(qs, k)
    return beta, gl, decay, egc, ekd, qs, kb, kk, qk, jnp.where(causal, qk * decay, 0.0)


def _chunk_rows(c):
    return pl.ds(pl.multiple_of(c * CHUNK, CHUNK), CHUNK)


def _sc_rows(b):
    return pl.ds(pl.multiple_of(b * SC, SC), SC)


def _gdn_fwd(qkv, gates, run, tot, run_t):
    t = qkv.shape[2]
    nc = t // CHUNK
    nsc = t // SC

    hp_n = GDN_HP_FWD
    heads = range(hp_n)

    def body(qkv_ref, gates_ref, run_ref, tot_ref, runt_ref, o_ref, st_ref, inv_ref, kc_s, qc_s, g_s, au_s):
        hp = pl.program_id(0)
        causal, strict, eye = _sc_masks()

        def local(b, _):
            rows = _sc_rows(b)
            loc = [_gdn_local(qkv_ref[0, hh, rows, :], qkv_ref[1, hh, rows, :],
                              *_gdn_columns(gates_ref, run_ref, tot_ref, runt_ref, rows, hp * hp_n + hh), causal)
                   for hh in heads]
            invs = _unit_lower_inverses([-jnp.where(strict, l[7] * l[2], 0.0) for l in loc], eye)
            uws = []
            for hh in heads:
                beta, _, _, egc, _, _, kb, _, _, _ = loc[hh]
                inv_ref[hh, rows, :] = invs[hh].astype(inv_ref.dtype)
                uws.append(_dot(invs[hh], jnp.concatenate([qkv_ref[2, hh, rows, :] * beta, kb * egc], axis=1)))
            for hh in heads:
                _, _, _, egc, ekd, qs, _, _, _, attn = loc[hh]
                auw = _dot(attn, uws[hh])
                g_s[hh, rows, :] = qs * egc - auw[:, DH:]
                au_s[hh, rows, :] = auw[:, :DH]
                kd = qkv_ref[1, hh, rows, :] * ekd
                for j in range(CPS):
                    sl = slice(j * CHUNK, (j + 1) * CHUNK)
                    both = _dot_tn(kd[sl], uws[hh][sl])
                    kc_s[hh, b * CPS + j] = both[:, DH:]
                    qc_s[hh, b * CPS + j] = both[:, :DH]
            return 0

        lax.fori_loop(0, nsc, local, 0)

        def step(c, states):
            rows = _chunk_rows(c)
            tot_row = tot_ref[pl.ds(c * CHUNK, 1), :]
            new = []
            for hh in heads:
                s = states[hh]
                st_ref[hh, c] = s
                o_ref[hh, rows, :] = _dot(g_s[hh, rows, :], s) + au_s[hh, rows, :]
                egl = jnp.exp(_lane_col(tot_row, HEADS + hp * hp_n + hh))
                new.append(egl * s - _dot(kc_s[hh, c], s) + qc_s[hh, c])
            return tuple(new)

        lax.fori_loop(0, nc, step, tuple(jnp.zeros((DH, DH), F32) for _ in heads))

    whole = pl.BlockSpec((t, LANE), lambda h: (0, 0))
    once = dict(pipeline_mode=pl.Buffered(1))
    return pl.pallas_call(
        body, name="gdn_fwd", grid=(HEADS // hp_n,),
        in_specs=[pl.BlockSpec((3, hp_n, t, DH), lambda h: (0, h, 0, 0), **once), whole, whole, whole,
                  pl.BlockSpec((HEADS, t), lambda h: (1, 0))],
        out_specs=[pl.BlockSpec((hp_n, t, DH), lambda h: (h, 0, 0), **once),
                   pl.BlockSpec((hp_n, nc, DH, DH), lambda h: (h, 0, 0, 0), **once),
                   pl.BlockSpec((hp_n, t, SC), lambda h: (h, 0, 0), **once)],
        out_shape=[jax.ShapeDtypeStruct((HEADS, t, DH), F32), jax.ShapeDtypeStruct((HEADS, nc, DH, DH), F32),
                   jax.ShapeDtypeStruct((HEADS, t, SC), MXU)],
        scratch_shapes=[pltpu.VMEM((hp_n, nc, DH, DH), F32), pltpu.VMEM((hp_n, nc, DH, DH), F32),
                        pltpu.VMEM((hp_n, t, DH), F32), pltpu.VMEM((hp_n, t, DH), F32)],
        compiler_params=_params(("arbitrary",)),
    )(qkv, gates, run, tot, run_t)


def _gdn_bwd(qkv, gates, run, tot, run_t, inv, states, do):
    t = qkv.shape[2]
    nc = t // CHUNK
    nsc = t // SC

    hp_n = GDN_HP_BWD
    heads = range(hp_n)

    def body(qkv_ref, gates_ref, run_ref, tot_ref, runt_ref, inv_ref, st_ref, do_ref,
             dqkv_ref, dcol_ref, dtot_ref, drow_ref, uw_s, kc_s, h_s, dsn_s):
        hp = pl.program_id(0)
        causal, strict, _ = _sc_masks()

        @pl.when(hp == 0)
        def _():
            dcol_ref[...] = jnp.zeros_like(dcol_ref)
            dtot_ref[...] = jnp.zeros_like(dtot_ref)

        def local_of(hh, rows, with_kk=True):
            return _gdn_local(qkv_ref[0, hh, rows, :], qkv_ref[1, hh, rows, :],
                              *_gdn_columns(gates_ref, run_ref, tot_ref, runt_ref, rows, hp * hp_n + hh), causal, with_kk)

        def local(b, _):
            rows = _sc_rows(b)
            loc = [local_of(hh, rows, with_kk=False) for hh in heads]
            us, ws = [], []
            for hh in heads:
                beta, _, _, egc, _, _, kb, _, _, _ = loc[hh]
                inv_b = inv_ref[hh, rows, :]
                us.append(_dot(inv_b, qkv_ref[2, hh, rows, :] * beta))
                ws.append(_dot(inv_b, kb * egc))
            gs = [loc[hh][5] * loc[hh][3] - _dot(loc[hh][9], ws[hh]) for hh in heads]
            for hh in heads:
                uw_s[hh, rows, :] = jnp.concatenate([us[hh], ws[hh]], axis=1)
                kd = qkv_ref[1, hh, rows, :] * loc[hh][4]
                dout = do_ref[hh, rows, :]
                for j in range(CPS):
                    sl = slice(j * CHUNK, (j + 1) * CHUNK)
                    kc_s[hh, b * CPS + j] = _dot_tn(kd[sl], ws[hh][sl])
                    h_s[hh, b * CPS + j] = _dot_tn(gs[hh][sl], dout[sl])
            return 0

        lax.fori_loop(0, nsc, local, 0)

        def step(i, dss):
            c = nc - 1 - i
            tot_row = tot_ref[pl.ds(c * CHUNK, 1), :]
            new = []
            for hh in heads:
                ds = dss[hh]
                dsn_s[hh, c] = ds
                egl = jnp.exp(_lane_col(tot_row, HEADS + hp * hp_n + hh))
                new.append(egl * ds - _dot_tn(kc_s[hh, c], ds) + h_s[hh, c])
            return tuple(new)

        lax.fori_loop(0, nc, step, tuple(jnp.zeros((DH, DH), F32) for _ in heads))

        def back(b, _):
            rows = _sc_rows(b)
            first = lax.broadcasted_iota(jnp.int32, (CHUNK, 1), 0) == 0
            loc = [local_of(hh, rows) for hh in heads]
            sign = jnp.where(lax.broadcasted_iota(jnp.int32, (1, LANE), 1) < DH, 1.0, -1.0)
            mid = []
            for hh in heads:
                beta, gl, decay, egc, ekd, qs, kb, kk, qk, attn = loc[hh]
                uw = uw_s[hh, rows, :]
                kd = qkv_ref[1, hh, rows, :] * ekd
                dout = do_ref[hh, rows, :]
                dg_p, dkd_p, duw_p, dgl_p = [], [], [], []
                for j in range(CPS):
                    sl = slice(j * CHUNK, (j + 1) * CHUNK)
                    s = st_ref[hh, b * CPS + j]
                    dsn = dsn_s[hh, b * CPS + j]
                    both = _dot_nt(jnp.concatenate([dsn, dout[sl]], axis=0), s)
                    dg_p.append(both[CHUNK:])
                    dsc = jnp.concatenate([dsn, -both[:CHUNK]], axis=1)
                    dkd_p.append(_dot_nt(uw[sl], dsc))
                    duw_p.append(_dot(kd[sl], dsc))
                    degl = jnp.sum(jnp.sum(s * dsn, axis=1, keepdims=True), axis=0, keepdims=True)
                    dgl_p.append(jnp.where(first, degl * jnp.exp(gl[j * CHUNK:j * CHUNK + 1, :]), 0.0))
                dg, dkd = jnp.concatenate(dg_p, axis=0), jnp.concatenate(dkd_p, axis=0)
                dod = jnp.concatenate([dout, dg], axis=1)
                da = jnp.where(causal, _dot_nt(dod, uw * sign), 0.0)
                duw = jnp.concatenate(duw_p, axis=0) + _dot_tn(attn, dod) * sign
                mid.append((kd, dg, dkd, da, duw, jnp.concatenate(dgl_p, axis=0)))
            inv_ts = [inv_ref[hh, rows, :].astype(F32).T for hh in heads]
            its = [_dot(inv_ts[hh], mid[hh][4]) for hh in heads]
            dinvs = []
            for hh in heads:
                beta, _, _, egc, _, _, kb, _, _, _ = loc[hh]
                dinvs.append(_dot_nt(mid[hh][4], jnp.concatenate([qkv_ref[2, hh, rows, :] * beta, kb * egc], axis=1)))
            half = [_dot(inv_ts[hh], dinvs[hh]) for hh in heads]
            dls = [jnp.where(strict, -_dot(half[hh], inv_ts[hh]), 0.0) for hh in heads]
            for hh in heads:
                head = hp * hp_n + hh
                beta, gl, decay, egc, ekd, qs, kb, kk, qk, attn = loc[hh]
                kd, dg, dkd, da, _, dgl_first = mid[hh]
                k, v = qkv_ref[1, hh, rows, :], qkv_ref[2, hh, rows, :]
                dvb, dkbe = its[hh][:, :DH], its[hh][:, DH:]
                dl = dls[hh]
                dlogd = (dl * kk + da * qk) * decay
                dd = jnp.concatenate([dl * decay, da * decay], axis=0)
                ddk = _dot(dd, k)
                dkb = ddk[:SC] + dkbe * egc
                dqs = ddk[SC:] + dg * egc
                dk = _dot_tn(dd, jnp.concatenate([kb, qs], axis=0)) + dkd * ekd + dkb * beta
                dkd_kd = jnp.sum(dkd * kd, axis=1, keepdims=True)
                narrow = dg * (qs * egc) + dkbe * (kb * egc)
                wide = dlogd[:, :LANE] + dlogd[:, LANE:] + jnp.concatenate([narrow, jnp.zeros((SC, LANE - DH), F32)], axis=1)
                dgc = jnp.sum(wide, axis=1, keepdims=True) - dkd_kd
                dbeta = jnp.sum(dkb * k + dvb * v, axis=1, keepdims=True)
                dqkv_ref[0, hh, rows, :] = dqs * Q_SCALE
                dqkv_ref[1, hh, rows, :] = dk
                dqkv_ref[2, hh, rows, :] = dvb * beta
                dcol_ref[rows, :] += _to_lane(dbeta, head) + _to_lane(dgc, HEADS + head)
                dtot_ref[rows, :] += _to_lane(dkd_kd + dgl_first, HEADS + head)
                drow_ref[pl.ds(head, 1), rows] = -jnp.sum(dlogd, axis=0, keepdims=True)
            return 0

        lax.fori_loop(0, nsc, back, 0)

    whole = pl.BlockSpec((t, LANE), lambda h: (0, 0))
    rowspec = pl.BlockSpec((HEADS, t), lambda h: (0, 0))
    sq = pltpu.VMEM((hp_n, nc, DH, DH), F32)
    per_head = pltpu.VMEM((hp_n, t, 2 * DH), F32)
    return pl.pallas_call(
        body, name="gdn_bwd", grid=(HEADS // hp_n,),
        in_specs=[pl.BlockSpec((3, hp_n, t, DH), lambda h: (0, h, 0, 0)), whole, whole, whole,
                  pl.BlockSpec((HEADS, t), lambda h: (1, 0)),
                  pl.BlockSpec((hp_n, t, SC), lambda h: (h, 0, 0)), pl.BlockSpec((hp_n, nc, DH, DH), lambda h: (h, 0, 0, 0)),
                  pl.BlockSpec((hp_n, t, DH), lambda h: (h, 0, 0))],
        out_specs=[pl.BlockSpec((3, hp_n, t, DH), lambda h: (0, h, 0, 0)), whole, whole, rowspec],
        out_shape=[jax.ShapeDtypeStruct((3, HEADS, t, DH), F32), jax.ShapeDtypeStruct((t, LANE), F32),
                   jax.ShapeDtypeStruct((t, LANE), F32), jax.ShapeDtypeStruct((HEADS, t), F32)],
        scratch_shapes=[per_head, sq, sq, sq],
        compiler_params=_params(("arbitrary",)),
    )(qkv, gates, run, tot, run_t, inv, states, do)


FOX_HP_FWD = 4
FOX_HP_BWD = 4


def _wide_t(a):
    r = a.shape[0]
    return jnp.concatenate([a, jnp.zeros((r, LANE - DH), F32)], axis=1).T[:DH]


def _tall_t(a):
    r = a.shape[1]
    return jnp.concatenate([a, jnp.zeros((LANE - DH, r), F32)], axis=0).T[:, :DH]


def _key_side_f(run_blk, head, qb):
    col = jnp.broadcast_to(_lane_col(run_blk, 2 * HEADS + head), (run_blk.shape[0], LANE))
    return jnp.concatenate([col] * (qb // LANE), axis=1)


def _diag_mask(qb):
    return lax.broadcasted_iota(jnp.int32, (qb, qb), 0) <= lax.broadcasted_iota(jnp.int32, (qb, qb), 1)


def _fox_fwd(qkv, run, run_t, carry=None):
    t = qkv.shape[2]
    qb = min(QB, t)
    nq = t // qb
    hp_n = FOX_HP_FWD
    c_in, c_in_specs, c_out_shape, c_out_specs, c_sem = _carry_operands(*carry) if carry else ([], [], [], [], None)
    n_c = len(c_in)

    def body(*refs):
        q_ref, k_ref, v_ref, run_ref, runt_ref = refs[:5]
        o_ref, lse_ref = refs[5 + n_c:7 + n_c]
        s0 = 7 + n_c + len(c_out_shape)
        kb_s, vt_s, raw_s, m_s, l_s, acc_s = refs[s0:s0 + 6]
        hp = pl.program_id(0)
        i = pl.program_id(1)

        if carry:
            _carry_step(carry[0], refs[5:5 + n_c], refs[7 + n_c], refs[8 + n_c], refs[-1], hp * nq + i,
                        (HEADS // hp_n) * nq)

        @pl.when(i == 0)
        def _():
            for hh in range(hp_n):
                kb_s[hh] = k_ref[0, hh].astype(MXU)
                for b in range(nq):
                    rows = slice(b * qb, (b + 1) * qb)
                    vt_s[hh, :, rows] = _wide_t(v_ref[0, hh, rows, :]).astype(MXU)

        qrows = pl.ds(pl.multiple_of(i * qb, qb), qb)
        qs = [(q_ref[0, hh] * Q_SCALE).astype(MXU) for hh in range(hp_n)]
        fq = [runt_ref[pl.ds(hp * hp_n + hh, 1), qrows] for hh in range(hp_n)]

        def block_rows(j):
            return pl.ds(pl.multiple_of(j * qb, qb), qb)

        def scores(j):
            for hh in range(hp_n):
                raw_s[j % 2, hh] = _dot_nt(kb_s[hh, block_rows(j), :], qs[hh])

        def absorb(j, diagonal):
            rows = block_rows(j)
            run_blk = run_ref[rows, :]
            stats, pv = [], []
            for hh in range(hp_n):
                m, l = m_s[hh], l_s[hh]
                st = raw_s[j % 2, hh] + (fq[hh] - _key_side_f(run_blk, hp * hp_n + hh, qb))
                if diagonal:
                    st = jnp.where(_diag_mask(qb), st, -1e30)
                m_new = jnp.maximum(m, jnp.max(st, axis=0, keepdims=True))
                p = jnp.exp(st - m_new)
                alpha = jnp.exp(m - m_new)
                stats.append((m_new, alpha * l + jnp.sum(p, axis=0, keepdims=True), alpha))
                pv.append(_dot(vt_s[hh, :, rows], p))
            for hh in range(hp_n):
                m_s[hh], l_s[hh] = stats[hh][0], stats[hh][1]
                acc_s[hh] = stats[hh][2] * acc_s[hh] + pv[hh]

        def kstep(j, _):
            scores(j + 1)
            absorb(j, False)
            return 0

        for hh in range(hp_n):
            m_s[hh] = jnp.full((1, qb), -1e30, F32)
            l_s[hh] = jnp.zeros((1, qb), F32)
            acc_s[hh] = jnp.zeros((DH, qb), F32)
        scores(0)
        lax.fori_loop(0, i, kstep, 0)
        absorb(i, True)
        for hh in range(hp_n):
            l = l_s[hh]
            o_ref[hh] = _tall_t(acc_s[hh] / l)
            lse_ref[pl.ds(hp * hp_n + hh, 1), qrows] = m_s[hh] + jnp.log(l)

    out = pl.pallas_call(
        body, name="fox_fwd", grid=(HEADS // hp_n, nq),
        in_specs=[pl.BlockSpec((1, hp_n, qb, DH), lambda h, i: (0, h, i, 0)),
                  pl.BlockSpec((1, hp_n, t, DH), lambda h, i: (1, h, 0, 0)),
                  pl.BlockSpec((1, hp_n, t, DH), lambda h, i: (2, h, 0, 0)),
                  pl.BlockSpec((t, LANE), lambda h, i: (0, 0)),
                  pl.BlockSpec((HEADS, t), lambda h, i: (2, 0))] + c_in_specs,
        out_specs=[pl.BlockSpec((hp_n, qb, DH), lambda h, i: (h, i, 0)),
                   pl.BlockSpec((HEADS, t), lambda h, i: (0, 0))] + c_out_specs,
        out_shape=[jax.ShapeDtypeStruct((HEADS, t, DH), F32), jax.ShapeDtypeStruct((HEADS, t), F32)] + c_out_shape,
        input_output_aliases={5 + j: 4 + j for j in range(n_c)},
        scratch_shapes=[pltpu.VMEM((hp_n, t, DH), MXU), pltpu.VMEM((hp_n, DH, t), MXU), pltpu.VMEM((2, hp_n, qb, qb), F32),
                        pltpu.VMEM((hp_n, 1, qb), F32), pltpu.VMEM((hp_n, 1, qb), F32), pltpu.VMEM((hp_n, DH, qb), F32)]
        + ([c_sem] if carry else []),
        compiler_params=_params(("arbitrary", "arbitrary")),
    )(qkv, qkv, qkv, run, run_t, *c_in)
    return (out[0], out[1], _carry_state(out[2:])) if carry else (out[0], out[1])


def _fox_bwd(qkv, run, run_t, o, lse, do, carry=None):
    t = qkv.shape[2]
    qb = min(QB, t)
    nq = t // qb
    hp_n = FOX_HP_BWD
    c_in, c_in_specs, c_out_shape, c_out_specs, c_sem = _carry_operands(*carry) if carry else ([], [], [], [], None)
    n_c = len(c_in)

    def body(*refs):
        q_ref, k_ref, v_ref, run_ref, runt_ref, o_ref, lse_ref, do_ref = refs[:8]
        dqkv_ref, dcol_ref, drow_ref = refs[8 + n_c:11 + n_c]
        s0 = 11 + n_c + len(c_out_shape)
        dqt_s, raw_s, dpt_s, dro_s, dk_s, dv_s, dsum_s = refs[s0:s0 + 7]
        hp = pl.program_id(0)
        j = pl.program_id(1)

        if carry:
            _carry_step(carry[0], refs[8:8 + n_c], refs[11 + n_c], refs[12 + n_c], refs[-1], hp * nq + j,
                        (HEADS // hp_n) * nq)

        @pl.when(j == 0)
        def _():
            dqt_s[...] = jnp.zeros_like(dqt_s)

        @pl.when((j == 0) & (hp == 0))
        def _():
            dcol_ref[...] = jnp.zeros_like(dcol_ref)
            drow_ref[...] = jnp.zeros_like(drow_ref)

        krows = pl.ds(pl.multiple_of(j * qb, qb), qb)
        run_blk = run_ref[krows, :]
        ones8 = jnp.ones((8, DH), MXU)
        kb, kt, vb, fk = [], [], [], []
        for hh in range(hp_n):
            kf = k_ref[0, hh]
            kb.append(kf.astype(MXU))
            kt.append(_wide_t(kf).astype(MXU))
            vb.append(v_ref[0, hh].astype(MXU))
            fk.append(_key_side_f(run_blk, hp * hp_n + hh, qb))

        def block_rows(i):
            return pl.ds(i * qb if isinstance(i, int) else pl.multiple_of(i * qb, qb), qb)

        def products(i):
            rows = block_rows(i)
            slot = i % 2
            for hh in range(hp_n):
                dout = do_ref[hh, rows, :]
                x = dout * o_ref[hh, rows, :]
                x_hi = x.astype(MXU)
                raw_s[slot, hh] = _dot_nt(kb[hh], q_ref[0, hh, rows, :] * Q_SCALE)
                dpt_s[slot, hh] = _dot_nt(vb[hh], dout)
                dro_s[slot, hh] = _dot_nt(ones8, x_hi) + _dot_nt(ones8, x - x_hi.astype(F32))

        def absorb(i, diagonal):
            rows = block_rows(i)
            slot = i % 2
            pieces = []
            for hh in range(hp_n):
                head = hp * hp_n + hh
                raw, dpt, drow = raw_s[slot, hh], dpt_s[slot, hh], dro_s[slot, hh, 0:1, :]
                off = runt_ref[pl.ds(head, 1), rows] - lse_ref[pl.ds(head, 1), rows]
                st = raw + (off - fk[hh])
                if diagonal:
                    st = jnp.where(_diag_mask(qb), st, -1e30)
                pt = jnp.exp(st)
                dst = pt * (dpt - drow)
                drow_ref[pl.ds(head, 1), rows] += jnp.sum(dst, axis=0, keepdims=True)
                folded = dst[:, 0:LANE]
                for c in range(1, qb // LANE):
                    folded = folded + dst[:, c * LANE:(c + 1) * LANE]
                pieces.append((_dot(dst, q_ref[0, hh, rows, :] * Q_SCALE), _dot(pt, do_ref[hh, rows, :]),
                               _dot(kt[hh], dst), folded))
            for hh in range(hp_n):
                dqt_s[hh, :, rows] += pieces[hh][2]
                if diagonal:
                    dk_s[hh], dv_s[hh], dsum_s[hh] = pieces[hh][0], pieces[hh][1], pieces[hh][3]
                else:
                    dk_s[hh] += pieces[hh][0]
                    dv_s[hh] += pieces[hh][1]
                    dsum_s[hh] += pieces[hh][3]

        def qstep(i, _):
            products(i + 1)
            absorb(i, False)
            return 0

        products(j)
        products(jnp.minimum(j + 1, nq - 1))
        absorb(j, True)
        lax.fori_loop(j + 1, nq - 1, qstep, 0)

        @pl.when(j < nq - 1)
        def _():
            absorb(nq - 1, False)
        for hh in range(hp_n):
            dqkv_ref[1, hh, krows, :] = dk_s[hh]
            dqkv_ref[2, hh, krows, :] = dv_s[hh]
            dcol_ref[krows, :] += _to_lane(-jnp.sum(dsum_s[hh], axis=1, keepdims=True), 2 * HEADS + hp * hp_n + hh)

        @pl.when(j == nq - 1)
        def _():
            for hh in range(hp_n):
                for b in range(nq):
                    rows = slice(b * qb, (b + 1) * qb)
                    dqkv_ref[0, hh, rows, :] = _tall_t(dqt_s[hh, :, rows]) * Q_SCALE

    once = dict(pipeline_mode=pl.Buffered(1))
    full = pl.BlockSpec((hp_n, t, DH), lambda h, j: (h, 0, 0), **once)
    rows8 = pl.BlockSpec((HEADS, t), lambda h, j: (0, 0))
    out = pl.pallas_call(
        body, name="fox_bwd", grid=(HEADS // hp_n, nq),
        in_specs=[pl.BlockSpec((1, hp_n, t, DH), lambda h, j: (0, h, 0, 0), **once),
                  pl.BlockSpec((1, hp_n, qb, DH), lambda h, j: (1, h, j, 0)),
                  pl.BlockSpec((1, hp_n, qb, DH), lambda h, j: (2, h, j, 0)),
                  pl.BlockSpec((t, LANE), lambda h, j: (0, 0), **once), pl.BlockSpec((HEADS, t), lambda h, j: (2, 0)),
                  full, rows8, full] + c_in_specs,
        out_specs=[pl.BlockSpec((3, hp_n, t, DH), lambda h, j: (0, h, 0, 0), **once),
                   pl.BlockSpec((t, LANE), lambda h, j: (0, 0)), rows8] + c_out_specs,
        out_shape=[jax.ShapeDtypeStruct((3, HEADS, t, DH), F32), jax.ShapeDtypeStruct((t, LANE), F32),
                   jax.ShapeDtypeStruct((HEADS, t), F32)] + c_out_shape,
        input_output_aliases={8 + j: 5 + j for j in range(n_c)},
        scratch_shapes=[pltpu.VMEM((hp_n, DH, t), F32), pltpu.VMEM((2, hp_n, qb, qb), F32), pltpu.VMEM((2, hp_n, qb, qb), F32),
                        pltpu.VMEM((2, hp_n, 8, qb), F32), pltpu.VMEM((hp_n, qb, DH), F32), pltpu.VMEM((hp_n, qb, DH), F32),
                        pltpu.VMEM((hp_n, qb, LANE), F32)] + ([c_sem] if carry else []),
        compiler_params=_params(("arbitrary", "arbitrary")),
    )(qkv, qkv, qkv, run, run_t, o, lse, do, *c_in)
    return (out[0], out[1], out[2], _carry_state(out[3:])) if carry else tuple(out)


Z_COL0 = 3 * WIDTH // LANE
FGATE_COL0 = 7 * WIDTH // LANE


def _gdn_post(o, pm, onw):
    t = pm.shape[0]

    def body(o_ref, z_ref, w_ref, m_ref):
        z = z_ref[...]
        sz = z * _sigmoid(z)
        halves = []
        for hh in range(2):
            ov = o_ref[hh]
            n = ov * lax.rsqrt(jnp.mean(ov * ov, axis=-1, keepdims=True) + EPS) * w_ref[...]
            halves.append(n * sz[:, hh * DH:(hh + 1) * DH])
        m_ref[...] = jnp.concatenate(halves, axis=1).astype(m_ref.dtype)

    return pl.pallas_call(
        body, name="gdn_post", grid=(WIDTH // LANE,),
        in_specs=[pl.BlockSpec((2, t, DH), lambda j: (j, 0, 0)), pl.BlockSpec((t, LANE), lambda j: (0, Z_COL0 + j)),
                  pl.BlockSpec((1, DH), lambda j: (0, 0))],
        out_specs=pl.BlockSpec((t, LANE), lambda j: (0, j)),
        out_shape=jax.ShapeDtypeStruct((t, WIDTH), MXU),
        compiler_params=_params(("arbitrary",)),
    )(o, pm, onw)


def _gdn_post_bwd(o, pm, onw, dmix):
    t = pm.shape[0]

    def body(o_ref, z_ref, w_ref, dm_ref, do_ref, dz_ref, dw_ref):
        @pl.when(pl.program_id(0) == 0)
        def _():
            dw_ref[...] = jnp.zeros_like(dw_ref)

        z = z_ref[...]
        sg = _sigmoid(z)
        sz = z * sg
        dsz = sg * (1.0 + z * (1.0 - sg))
        dm = dm_ref[...]
        for hh in range(2):
            cols = slice(hh * DH, (hh + 1) * DH)
            ov = o_ref[hh]
            r = lax.rsqrt(jnp.mean(ov * ov, axis=-1, keepdims=True) + EPS)
            xn = ov * r
            dmh = dm[:, cols]
            dn = dmh * sz[:, cols]
            dz_ref[:, cols] = dmh * (xn * w_ref[...]) * dsz[:, cols]
            dw_ref[...] += jnp.sum(dn * xn, axis=0, keepdims=True)
            g = dn * w_ref[...]
            do_ref[hh] = r * (g - xn * jnp.mean(g * xn, axis=-1, keepdims=True))

    return pl.pallas_call(
        body, name="gdn_post_bwd", grid=(WIDTH // LANE,),
        in_specs=[pl.BlockSpec((2, t, DH), lambda j: (j, 0, 0)), pl.BlockSpec((t, LANE), lambda j: (0, Z_COL0 + j)),
                  pl.BlockSpec((1, DH), lambda j: (0, 0)), pl.BlockSpec((t, LANE), lambda j: (0, j))],
        out_specs=[pl.BlockSpec((2, t, DH), lambda j: (j, 0, 0)), pl.BlockSpec((t, LANE), lambda j: (0, j)),
                   pl.BlockSpec((1, DH), lambda j: (0, 0))],
        out_shape=[jax.ShapeDtypeStruct((HEADS, t, DH), F32), jax.ShapeDtypeStruct((t, WIDTH), F32),
                   jax.ShapeDtypeStruct((1, DH), F32)],
        compiler_params=_params(("arbitrary",)),
    )(o, pm, onw, dmix)


def _fox_post(o, pm):
    t = pm.shape[0]

    def body(o_ref, g_ref, m_ref):
        m_ref[...] = (jnp.concatenate([o_ref[0], o_ref[1]], axis=1) * _sigmoid(g_ref[...])).astype(m_ref.dtype)

    return pl.pallas_call(
        body, name="fox_post", grid=(WIDTH // LANE,),
        in_specs=[pl.BlockSpec((2, t, DH), lambda j: (j, 0, 0)), pl.BlockSpec((t, LANE), lambda j: (0, FGATE_COL0 + j))],
        out_specs=pl.BlockSpec((t, LANE), lambda j: (0, j)),
        out_shape=jax.ShapeDtypeStruct((t, WIDTH), MXU),
        compiler_params=_params(("arbitrary",)),
    )(o, pm)


def _fox_post_bwd(o, pm, dmix):
    t = pm.shape[0]

    def body(o_ref, g_ref, dm_ref, do_ref, dg_ref):
        sg = _sigmoid(g_ref[...])
        dm = dm_ref[...]
        for hh in range(2):
            cols = slice(hh * DH, (hh + 1) * DH)
            do_ref[hh] = dm[:, cols] * sg[:, cols]
            dg_ref[:, cols] = dm[:, cols] * o_ref[hh] * (sg * (1.0 - sg))[:, cols]

    return pl.pallas_call(
        body, name="fox_post_bwd", grid=(WIDTH // LANE,),
        in_specs=[pl.BlockSpec((2, t, DH), lambda j: (j, 0, 0)), pl.BlockSpec((t, LANE), lambda j: (0, FGATE_COL0 + j)),
                  pl.BlockSpec((t, LANE), lambda j: (0, j))],
        out_specs=[pl.BlockSpec((2, t, DH), lambda j: (j, 0, 0)), pl.BlockSpec((t, LANE), lambda j: (0, j))],
        out_shape=[jax.ShapeDtypeStruct((HEADS, t, DH), F32), jax.ShapeDtypeStruct((t, WIDTH), F32)],
        compiler_params=_params(("arbitrary",)),
    )(o, pm, dmix)


def _tail(x, mixg, mixf, tgt, wo, n2w, wg_t, wu_t, wd, fw):
    t, d = x.shape
    dff = wd.shape[0]
    tb = min(TB, t)

    def body(x_ref, mg_ref, mf_ref, t_ref, wo_ref, n2_ref, wg_ref, wu_ref, wd_ref, fw_ref,
             h2_ref, act_ref, dgate_ref, dup_ref, dx3_ref, dx2_ref, dmg_ref, dmf_ref, dn2_ref, dfw_ref, loss_ref):
        @pl.when(pl.program_id(0) == 0)
        def _():
            dn2_ref[...] = jnp.zeros_like(dn2_ref)
            dfw_ref[...] = jnp.zeros_like(dfw_ref)
            loss_ref[...] = jnp.zeros_like(loss_ref)

        x2 = x_ref[...] + _dot(mg_ref[...], wo_ref[0:WIDTH, :]) + _dot(mf_ref[...], wo_ref[WIDTH:2 * WIDTH, :])
        r2 = lax.rsqrt(jnp.mean(x2 * x2, axis=-1, keepdims=True) + EPS)
        xn2 = x2 * r2
        h2 = (xn2 * n2_ref[...]).astype(MXU)
        h2_ref[...] = h2
        gate = _dot_nt(h2, wg_ref[...])
        up = _dot_nt(h2, wu_ref[...])
        sg = _sigmoid(gate)
        sl = gate * sg
        act = (sl * up).astype(MXU)
        act_ref[...] = act
        x3 = x2 + _dot(act, wd_ref[...])
        r3 = lax.rsqrt(jnp.mean(x3 * x3, axis=-1, keepdims=True) + EPS)
        xn3 = x3 * r3
        err = xn3 * fw_ref[...] - t_ref[...]
        loss_ref[...] += 0.5 * jnp.sum(jnp.mean(err * err, axis=-1, keepdims=True), axis=0, keepdims=True)
        dy = err * (1.0 / d)
        dfw_ref[...] += jnp.sum(dy * xn3, axis=0, keepdims=True)
        g3 = dy * fw_ref[...]
        dx3 = r3 * (g3 - xn3 * jnp.mean(g3 * xn3, axis=-1, keepdims=True))
        dx3_ref[...] = dx3.astype(MXU)
        dact = _dot_nt(dx3, wd_ref[...])
        dgate = (dact * up * (sg * (1.0 + gate * (1.0 - sg)))).astype(MXU)
        dup = (dact * sl).astype(MXU)
        dgate_ref[...] = dgate
        dup_ref[...] = dup
        dh2 = _dot(dgate, wg_ref[...]) + _dot(dup, wu_ref[...])
        dn2_ref[...] += jnp.sum(dh2 * xn2, axis=0, keepdims=True)
        g2 = dh2 * n2_ref[...]
        dx2 = dx3 + r2 * (g2 - xn2 * jnp.mean(g2 * xn2, axis=-1, keepdims=True))
        dx2_ref[...] = dx2
        dmg_ref[...] = _dot_nt(dx2, wo_ref[0:WIDTH, :])
        dmf_ref[...] = _dot_nt(dx2, wo_ref[WIDTH:2 * WIDTH, :])

    def tok(n):
        return pl.BlockSpec((tb, n), lambda i: (i, 0))

    acc = pl.BlockSpec((1, d), lambda i: (0, 0))
    sds = jax.ShapeDtypeStruct
    return pl.pallas_call(
        body, name="tail", grid=(t // tb,),
        in_specs=[tok(d), tok(WIDTH), tok(WIDTH), tok(d), _resident(wo.shape), _resident((1, d)),
                  _resident(wg_t.shape), _resident(wu_t.shape), _resident(wd.shape), _resident((1, d))],
        out_specs=[tok(d), tok(dff), tok(dff), tok(dff), tok(d), tok(d), tok(WIDTH), tok(WIDTH), acc, acc,
                   pl.BlockSpec((1, 1), lambda i: (0, 0))],
        out_shape=[sds((t, d), MXU), sds((t, dff), MXU), sds((t, dff), MXU), sds((t, dff), MXU), sds((t, d), MXU),
                   sds((t, d), F32), sds((t, WIDTH), F32), sds((t, WIDTH), F32), sds((1, d), F32), sds((1, d), F32),
                   sds((1, 1), F32)],
        compiler_params=_params(("arbitrary",)),
    )(x, mixg, mixf, tgt, wo, n2w, wg_t, wu_t, wd, fw)


def _wgrads(a_list, b, name):
    t, n = b.shape
    ms = [a.shape[1] for a in a_list]
    bms = [256 if m % 256 == 0 else LANE for m in ms]
    nbs = [m // bm for m, bm in zip(ms, bms)]
    k = len(a_list)
    cast = b.dtype != jnp.dtype(MXU)

    def body(*refs):
        a_refs, b_ref, o_refs = refs[:k], refs[k], refs[k + 1:2 * k + 1]
        i = pl.program_id(0)
        if cast:
            @pl.when(i == 0)
            def _():
                refs[-1][...] = b_ref[...].astype(MXU)
        for a_ref, o_ref, nb in zip(a_refs, o_refs, nbs):
            @pl.when(i < nb)
            def _():
                o_ref[...] = _dot_tn(a_ref[...], refs[-1][...] if cast else b_ref[...]).astype(o_ref.dtype)

    def clamp(nb):
        return lambda i: jnp.minimum(i, nb - 1)

    return pl.pallas_call(
        body, name=name, grid=(max(nbs),),
        in_specs=[pl.BlockSpec((t, bm), lambda i, c=clamp(nb): (0, c(i))) for bm, nb in zip(bms, nbs)] + [_resident((t, n))],
        out_specs=[pl.BlockSpec((bm, n), lambda i, c=clamp(nb): (c(i), 0)) for bm, nb in zip(bms, nbs)],
        out_shape=[jax.ShapeDtypeStruct((m, n), WIRE) for m in ms],
        scratch_shapes=[pltpu.VMEM((t, n), MXU)] if cast else [],
        compiler_params=_params(("arbitrary",)),
    )(*a_list, b)


def _merge_dw_in(d_gdn, d_z, d_fox, d_fg, d_small, shards=False):
    pieces = [d_gdn, d_z, d_small[:2 * HEADS], d_fox, d_fg, d_small[2 * HEADS:3 * HEADS]]
    if not shards:
        return jnp.concatenate(pieces, axis=0)
    n = sum(p.shape[0] for p in pieces) // N_DEV
    out = []
    for dev in range(N_DEV):
        parts, first = [], 0
        for p in pieces:
            lo, hi = max(dev * n, first), min((dev + 1) * n, first + p.shape[0])
            if lo < hi:
                parts.append(p[lo - first:hi - first])
            first += p.shape[0]
        out.append(jnp.concatenate(parts, axis=0))
    return jnp.stack(out)


def _lanes(*pieces):
    v = jnp.concatenate([p.reshape(-1).astype(F32) for p in pieces])
    return jnp.pad(v, (0, LANE - v.shape[0])).reshape(1, LANE)


def _vector_params(p):
    d = p["norm1_w"].size
    gparams = jnp.concatenate([_lanes(jnp.zeros(HEADS), p["gdn_dt_bias"], p["fox_f_bias"]),
                               _lanes(jnp.zeros(HEADS), p["gdn_A_log"]), jnp.zeros((6, LANE), F32)])
    fox_nw = jnp.stack([jnp.tile(p["fox_q_norm_w"].reshape(-1), 2), jnp.tile(p["fox_k_norm_w"].reshape(-1), 2),
                        jnp.ones((LANE,), F32)])
    return dict(n1w=p["norm1_w"].reshape(1, d), n2w=p["norm2_w"].reshape(1, d), fw=p["final_norm_w"].reshape(1, d),
                onw=p["gdn_out_norm_w"].reshape(1, DH), gparams=gparams, fox_nw=fox_nw)


def _mixer_forward(x, vp, w_t, ws_t, conv_w, carry=None):
    h1, pm, ps = _inproj(x, vp["n1w"], w_t, ws_t)
    gates, run, tot, run_t = _gates(ps, vp["gparams"])
    fqkv = _fox_prep(pm, vp["fox_nw"])
    o_fox, lse, *carried = _fox_fwd(fqkv, run, run_t, carry)
    if carry:
        pm, o_fox = lax.optimization_barrier((pm, o_fox))
    mixf = _fox_post(o_fox, pm)
    gqkv = _gdn_prep(pm, conv_w)
    o_gdn, states, inv = _gdn_fwd(gqkv, gates, run, tot, run_t)
    mixg = _gdn_post(o_gdn, pm, vp["onw"])
    return dict(h1=h1, pm=pm, ps=ps, gates=gates, run=run, tot=tot, run_t=run_t, gqkv=gqkv, o_gdn=o_gdn, states=states,
                inv=inv, mixg=mixg, fqkv=fqkv, o_fox=o_fox, lse=lse, mixf=mixf, carried=carried[0] if carried else None)


def _mixer_backward(x, vp, w_t, ws_t, conv_w, f, dx2, dmixg, dmixf, carry=None, scatter_own=False):
    pm = f["pm"]
    do_fox, dfg = _fox_post_bwd(f["o_fox"], pm, dmixf)
    dfqkv, dcol_f, drow_f, *carried = _fox_bwd(f["fqkv"], f["run"], f["run_t"], f["o_fox"], f["lse"], do_fox, carry)
    dfox, dfnw = _fox_prep_bwd(pm, vp["fox_nw"], dfqkv)
    do_gdn, dz, donw = _gdn_post_bwd(f["o_gdn"], pm, vp["onw"], dmixg)
    dgqkv, dcol_g, dtot_g, drow_g = _gdn_bwd(f["gqkv"], f["gates"], f["run"], f["tot"], f["run_t"], f["inv"], f["states"], do_gdn)
    dgdn, dconv = _gdn_prep_bwd(pm, conv_w, dgqkv)
    dps, gsum = _gates_bwd(f["ps"], vp["gparams"], dcol_g, dtot_g, dcol_f, drow_g, drow_f)
    dw_pieces = _wgrads([dgdn, dz, dfox, dfg, dps], f["h1"], "dw_in")
    dw_in = _merge_dw_in(*dw_pieces)
    own = ("scatter", [_merge_dw_in(*dw_pieces, shards=True), _cut(dconv, 1)]) if scatter_own else None
    grad_x, dn1w, *sent = _inproj_bwd(x, vp["n1w"], dx2, dgdn, dz, dfox, dfg, dps, w_t, ws_t, own)
    small = dict(dn1w=dn1w, gsum=gsum, donw=donw, dfnw=dfnw)
    return (grad_x, dw_in, dconv, small, *carried, *sent)


VECTORS = ("norm1_w", "norm2_w", "final_norm_w", "gdn_A_log", "gdn_dt_bias", "gdn_out_norm_w", "fox_f_bias",
           "fox_q_norm_w", "fox_k_norm_w")
VEC_ROWS = 16
LOSS_ROW = len(VECTORS)


def _pack_vectors(dn1w, dn2w, dfw, gsum, donw, dfnw, loss):
    d = dn1w.shape[1]

    def body(n1_ref, n2_ref, fw_ref, gs_ref, on_ref, fn_ref, loss_ref, o_ref):
        o_ref[...] = jnp.zeros_like(o_ref)
        o_ref[0:1, :] = n1_ref[...]
        o_ref[1:2, :] = n2_ref[...]
        o_ref[2:3, :] = fw_ref[...]
        o_ref[3:4, 0:HEADS] = gs_ref[0:1, 0:HEADS]
        o_ref[4:5, 0:HEADS] = gs_ref[1:2, 0:HEADS]
        o_ref[5:6, 0:DH] = on_ref[...]
        o_ref[6:7, 0:HEADS] = gs_ref[2:3, 0:HEADS]
        for kind in range(2):
            v = fn_ref[kind, 0]
            for j in range(1, fn_ref.shape[1]):
                v = v + fn_ref[kind, j]
            o_ref[7 + kind:8 + kind, 0:DH] = v[:, :DH] + v[:, DH:]
        o_ref[LOSS_ROW:LOSS_ROW + 1, 0:1] = loss_ref[...]

    return pl.pallas_call(body, name="pack_vectors", out_shape=jax.ShapeDtypeStruct((VEC_ROWS, d), F32),
                          compiler_params=_params())(dn1w, dn2w, dfw, gsum, donw, dfnw, loss)


def _late_grads(f, h2, act, dgate, dup, dx3, dx2):
    dw_gate, dw_up = _wgrads([dgate, dup], h2, "dw_gate_up")
    (dw_down,) = _wgrads([act], dx3, "dw_down")
    return {"w_out": jnp.concatenate(_wgrads([f["mixg"], f["mixf"]], dx2, "dw_out"), axis=0),
            "w_ffn_gate": dw_gate, "w_ffn_up": dw_up, "w_ffn_down": dw_down}


def _local_step(x, tgt, p, w_in_t, conv_w, wo, wg_t, wu_t, wd):
    vp = _vector_params(p)
    ws_t = _small_rows(w_in_t)
    f = _mixer_forward(x, vp, w_in_t, ws_t, conv_w)
    (h2, act, dgate, dup, dx3, dx2, dmixg, dmixf, dn2w, dfw, loss) = _tail(
        x, f["mixg"], f["mixf"], tgt, wo, vp["n2w"], wg_t, wu_t, wd, vp["fw"])
    grad_x, dw_in, dconv, small = _mixer_backward(x, vp, w_in_t, ws_t, conv_w, f, dx2, dmixg, dmixf)
    grads = {"w_in": dw_in, "gdn_conv_w": dconv, **_late_grads(f, h2, act, dgate, dup, dx3, dx2)}
    vec = _pack_vectors(small["dn1w"], dn2w, dfw, small["gsum"], small["donw"], small["dfnw"], loss)
    return grad_x[0], grads, vec


def _my_place():
    return lax.axis_index("x"), lax.axis_index("y"), lax.axis_index("c")


def _peers():
    x, y, c = _my_place()
    peers = []
    for k in range(1, N_DEV):
        px = 1 - x if k & 4 else x
        py = 1 - y if k & 2 else y
        pc = 1 - c if k & 1 else c
        peers.append(((px, py, pc), 4 * px + 2 * py + pc))
    return 4 * x + 2 * y + c, peers


def _spread_copies(kind, srcs, lands, send_sems, recv_sems):
    me, peers = _peers()
    kinds = [kind] * len(srcs) if isinstance(kind, str) else kind
    remote, local = [], []
    for i, (kd, src, land) in enumerate(zip(kinds, srcs, lands)):
        for k, (dev, idx) in enumerate(peers):
            remote.append(pltpu.make_async_remote_copy(
                src_ref=src if kd == "gather" else src.at[idx], dst_ref=land.at[me],
                send_sem=send_sems.at[i * (N_DEV - 1) + k], recv_sem=recv_sems.at[i * (N_DEV - 1) + k],
                device_id=dev, device_id_type=MESH))
        local.append((src if kd == "gather" else src.at[me], land.at[me]))
    return remote, local


def _gather_two_level(arrays, name):
    n = len(arrays)

    def body(*refs):
        srcs, lands = refs[:n], refs[n:2 * n]
        send_sems, recv_sems, local_sems = refs[2 * n:]
        x, y, c = _my_place()
        me, sibling = (x, y, c), (x, y, 1 - c)
        chips = [(1 - x, y), (x, 1 - y), (1 - x, 1 - y)]

        def index(p):
            return 4 * p[0] + 2 * p[1] + p[2]

        def copy(i, k, block, to, src=None):
            blk = lands[i].at[index(block)]
            return pltpu.make_async_remote_copy(
                src_ref=blk if src is None else src, dst_ref=blk, send_sem=send_sems.at[7 * i + k],
                recv_sem=recv_sems.at[7 * i + k], device_id=to, device_id_type=MESH)

        mine = [pltpu.make_async_copy(srcs[i], lands[i].at[index(me)], local_sems.at[i]) for i in range(n)]
        for cp in mine:
            cp.start()
        first = []
        for i in range(n):
            first.append(copy(i, 0, me, sibling, src=srcs[i]))
            first += [copy(i, 1 + j, me, (*chip, c), src=srcs[i]) for j, chip in enumerate(chips)]
        for cp in first:
            cp.start()
        passed = []
        for i in range(n):
            for j, chip in enumerate(chips):
                copy(i, 1 + j, (*chip, c), me).wait_recv()
                passed.append(copy(i, 4 + j, (*chip, c), sibling))
                passed[-1].start()
        for i in range(n):
            copy(i, 0, sibling, me).wait_recv()
            for j, chip in enumerate(chips):
                copy(i, 4 + j, (*chip, 1 - c), me).wait_recv()
        for cp in first + passed:
            cp.wait_send()
        for cp in mine:
            cp.wait()

    return pl.pallas_call(
        body, name=name,
        out_shape=[jax.ShapeDtypeStruct((N_DEV,) + a.shape, a.dtype) for a in arrays],
        in_specs=[pl.BlockSpec(memory_space=pl.ANY)] * n, out_specs=[pl.BlockSpec(memory_space=pl.ANY)] * n,
        scratch_shapes=[pltpu.SemaphoreType.DMA((7 * n,)), pltpu.SemaphoreType.DMA((7 * n,)),
                        pltpu.SemaphoreType.DMA((n,))],
    )(*arrays)


def _land_shape(kind, a):
    return (N_DEV,) + a.shape if kind == "gather" else a.shape


HBM = pl.BlockSpec(memory_space=pltpu.HBM)
SEM = pl.BlockSpec(memory_space=pltpu.SEMAPHORE)


def _hbm(a):
    return pltpu.with_memory_space_constraint(a, pltpu.HBM)


def _carry_operands(kind, arrays):
    n = len(arrays)
    kinds = [kind] * n if isinstance(kind, str) else kind
    lands = [lax.empty(_land_shape(kd, a), a.dtype) for kd, a in zip(kinds, arrays)]
    sems = [pltpu.SemaphoreType.DMA((n * (N_DEV - 1),))] * 2
    return ([_hbm(a) for a in list(arrays) + lands], [HBM] * (2 * n),
            sems + [pltpu.HBM(a.shape, a.dtype) for a in list(arrays) + lands], [SEM] * 2 + [HBM] * (2 * n),
            pltpu.SemaphoreType.DMA((n,)))


def _carry_step(kind, in_refs, send_sems, recv_sems, local_sems, step, n_steps):
    n = len(in_refs) // 2
    remote, local = _spread_copies(kind, in_refs[:n], in_refs[n:], send_sems, recv_sems)
    copies = [pltpu.make_async_copy(s, d, local_sems.at[i]) for i, (s, d) in enumerate(local)]
    per_step = -(-len(remote) // n_steps)
    for s in range(-(-len(remote) // per_step)):
        @pl.when(step == s)
        def _():
            for cp in remote[s * per_step:(s + 1) * per_step]:
                cp.start()
            if s == 0:
                for cp in copies:
                    cp.start()

    @pl.when(step == n_steps - 1)
    def _():
        for cp in copies:
            cp.wait()


def _carry_state(extra_out):
    n = (len(extra_out) - 2) // 2
    return list(extra_out[2:2 + n]), list(extra_out[2 + n:]), extra_out[0], extra_out[1]


def _spread_start(arrays, kind, name):
    n = len(arrays)

    def body(*refs):
        srcs, lands = refs[:n], refs[n:2 * n]
        send_sems, recv_sems = refs[2 * n], refs[2 * n + 1]
        token = refs[4 * n + 2]
        local_sems = refs[4 * n + 3]
        remote, local = _spread_copies(kind, srcs, lands, send_sems, recv_sems)
        for cp in remote:
            cp.start()
        copies = [pltpu.make_async_copy(s, d, local_sems.at[i]) for i, (s, d) in enumerate(local)]
        for cp in copies:
            cp.start()
        for cp in copies:
            cp.wait()
        token[...] = jnp.zeros_like(token)

    sems = (pltpu.SemaphoreType.DMA((n * (N_DEV - 1),)),) * 2
    kinds = [kind] * n if isinstance(kind, str) else kind
    lands = [lax.empty(_land_shape(kd, a), a.dtype) for kd, a in zip(kinds, arrays)]
    out = pl.pallas_call(
        body, name=name,
        out_shape=sems + tuple(pltpu.HBM(a.shape, a.dtype) for a in list(arrays) + lands)
        + (jax.ShapeDtypeStruct((8, LANE), F32),),
        in_specs=[HBM] * (2 * n), out_specs=tuple([SEM] * 2 + [HBM] * (2 * n) + [pl.BlockSpec(memory_space=pltpu.VMEM)]),
        input_output_aliases={j: 2 + j for j in range(2 * n)},
        scratch_shapes=[pltpu.SemaphoreType.DMA((n,))],
        compiler_params=pltpu.CompilerParams(has_side_effects=pltpu.SideEffectType.DATAFLOW_SIDE_EFFECTING),
    )(*[_hbm(a) for a in arrays], *[_hbm(a) for a in lands])
    return (list(out[2:2 + n]), list(out[2 + n:2 + 2 * n]), out[0], out[1]), out[-1]


def _spread_wait(state, kind, after, name):
    srcs, lands, send_sems, recv_sems = state
    n = len(srcs)
    after = list(after) if isinstance(after, (list, tuple)) else [after]

    def body(*refs):
        remote, _ = _spread_copies(kind, refs[:n], refs[n:2 * n], refs[2 * n], refs[2 * n + 1])
        for cp in remote:
            cp.wait_send()
        for cp in remote:
            cp.wait_recv()

    out = pl.pallas_call(
        body, name=name,
        out_shape=tuple(pltpu.HBM(a.shape, a.dtype) for a in srcs + lands),
        in_specs=[HBM] * (2 * n) + [SEM, SEM] + [pl.BlockSpec(memory_space=pl.ANY)] * len(after),
        out_specs=tuple([HBM] * (2 * n)),
        input_output_aliases={j: j for j in range(2 * n)},
        compiler_params=pltpu.CompilerParams(has_side_effects=pltpu.SideEffectType.DATAFLOW_SIDE_EFFECTING),
    )(*srcs, *lands, send_sems, recv_sems, *after)
    return list(out[n:])


ADAM_ROWS = 128
ADAM_COLS = 256


def _adam_math(g, w, m, v):
    nm = ADAM_B1 * m + (1.0 - ADAM_B1) * g
    nv = ADAM_B2 * v + (1.0 - ADAM_B2) * (g * g)
    m_hat = nm / (1.0 - ADAM_B1 ** ADAM_STEP)
    v_hat = nv / (1.0 - ADAM_B2 ** ADAM_STEP)
    return -ADAM_LR * (m_hat / (jnp.sqrt(v_hat) + ADAM_EPS) + ADAM_WD * w), nm, nv


def _sum_parts(p_ref):
    g = p_ref[0].astype(F32)
    for s in range(1, N_DEV):
        g = g + p_ref[s].astype(F32)
    return g


def _adam_matrix(parts, w, m, v, name):
    _, r, c = w.shape
    rb = ADAM_ROWS if r % ADAM_ROWS == 0 else r
    cb = ADAM_COLS if (rb == r and c % ADAM_COLS == 0) else c

    def body(p_ref, w_ref, m_ref, v_ref, g_ref, d_ref, nm_ref, nv_ref):
        g = _sum_parts(p_ref)
        g_ref[0] = g
        d_ref[0], nm_ref[0], nv_ref[0] = _adam_math(g, w_ref[0], m_ref[0], v_ref[0])

    blk = pl.BlockSpec((1, rb, cb), lambda i, j: (0, i, j))
    return pl.pallas_call(
        body, name=name, grid=(r // rb, c // cb),
        in_specs=[pl.BlockSpec((N_DEV, rb, cb), lambda i, j: (0, i, j)), blk, blk, blk],
        out_specs=[blk] * 4, out_shape=[jax.ShapeDtypeStruct(w.shape, F32)] * 4,
        compiler_params=_params(("arbitrary", "arbitrary")),
    )(parts, w, m, v)


def _copy_rows(a, name):
    _, r, c = a.shape
    rb = min(TB, r)

    def body(a_ref, o_ref):
        o_ref[...] = a_ref[...]

    blk = pl.BlockSpec((1, rb, c), lambda i: (0, i, 0))
    return pl.pallas_call(body, name=name, grid=(r // rb,), in_specs=[blk], out_specs=blk,
                          out_shape=jax.ShapeDtypeStruct(a.shape, a.dtype), compiler_params=_params(("arbitrary",)))(a)


def _adam_vectors(parts, ws, ms, vs):
    nv = len(ws)

    def body(*refs):
        p_ref = refs[0]
        w_refs, m_refs, v_refs = refs[1:1 + nv], refs[1 + nv:1 + 2 * nv], refs[1 + 2 * nv:1 + 3 * nv]
        outs = refs[1 + 3 * nv:]
        g_all = _sum_parts(p_ref)
        for i in range(nv):
            n = w_refs[i].shape[1]
            g = g_all[i:i + 1, 0:n]
            d, nm, nvv = _adam_math(g, w_refs[i][...], m_refs[i][...], v_refs[i][...])
            outs[i][...] = g
            outs[nv + i][...] = d
            outs[2 * nv + i][...] = nm
            outs[3 * nv + i][...] = nvv
        outs[4 * nv][...] = g_all[LOSS_ROW:LOSS_ROW + 1, 0:1]

    shapes = [jax.ShapeDtypeStruct(a.shape, F32) for a in ws]
    out = pl.pallas_call(body, name="adam_vectors", out_shape=shapes * 4 + [jax.ShapeDtypeStruct((1, 1), F32)],
                         compiler_params=_params())(parts, *ws, *ms, *vs)
    return out[:nv], out[nv:2 * nv], out[2 * nv:3 * nv], out[3 * nv:4 * nv], out[4 * nv]


MATRICES = (("w_in", 1), ("gdn_conv_w", 1), ("w_out", 0), ("w_ffn_gate", 1), ("w_ffn_up", 1), ("w_ffn_down", 0))
TRANSPOSED = ("w_in", "w_ffn_gate", "w_ffn_up")
WEIGHTS = ("norm1_w", "w_in", "gdn_conv_w", "gdn_A_log", "gdn_dt_bias", "gdn_out_norm_w", "fox_f_bias", "fox_q_norm_w",
           "fox_k_norm_w", "w_out", "norm2_w", "w_ffn_gate", "w_ffn_up", "w_ffn_down", "final_norm_w")


def _join(blocks, axis):
    _, r, c = blocks.shape
    if axis == 0:
        return blocks.reshape(N_DEV * r, c)
    return blocks.transpose(1, 0, 2).reshape(r, N_DEV * c)


def _cut(full, axis):
    r, c = full.shape
    if axis == 0:
        return full.reshape(N_DEV, r // N_DEV, c)
    return full.reshape(r, N_DEV, c // N_DEV).transpose(1, 0, 2)


def kernel(x, norm1_w, w_in, gdn_conv_w, gdn_A_log, gdn_dt_bias, gdn_out_norm_w, fox_f_bias, fox_q_norm_w, fox_k_norm_w, w_out, norm2_w, w_ffn_gate, w_ffn_up, w_ffn_down, final_norm_w, loss_target, m_norm1_w, m_w_in, m_gdn_conv_w, m_gdn_A_log, m_gdn_dt_bias, m_gdn_out_norm_w, m_fox_f_bias, m_fox_q_norm_w, m_fox_k_norm_w, m_w_out, m_norm2_w, m_w_ffn_gate, m_w_ffn_up, m_w_ffn_down, m_final_norm_w, v_norm1_w, v_w_in, v_gdn_conv_w, v_gdn_A_log, v_gdn_dt_bias, v_gdn_out_norm_w, v_fox_f_bias, v_fox_q_norm_w, v_fox_k_norm_w, v_w_out, v_norm2_w, v_w_ffn_gate, v_w_ffn_up, v_w_ffn_down, v_final_norm_w):
    w = dict(norm1_w=norm1_w, w_in=w_in, gdn_conv_w=gdn_conv_w, gdn_A_log=gdn_A_log, gdn_dt_bias=gdn_dt_bias,
             gdn_out_norm_w=gdn_out_norm_w, fox_f_bias=fox_f_bias, fox_q_norm_w=fox_q_norm_w, fox_k_norm_w=fox_k_norm_w,
             w_out=w_out, norm2_w=norm2_w, w_ffn_gate=w_ffn_gate, w_ffn_up=w_ffn_up, w_ffn_down=w_ffn_down,
             final_norm_w=final_norm_w)
    m = dict(norm1_w=m_norm1_w, w_in=m_w_in, gdn_conv_w=m_gdn_conv_w, gdn_A_log=m_gdn_A_log, gdn_dt_bias=m_gdn_dt_bias,
             gdn_out_norm_w=m_gdn_out_norm_w, fox_f_bias=m_fox_f_bias, fox_q_norm_w=m_fox_q_norm_w,
             fox_k_norm_w=m_fox_k_norm_w, w_out=m_w_out, norm2_w=m_norm2_w, w_ffn_gate=m_w_ffn_gate,
             w_ffn_up=m_w_ffn_up, w_ffn_down=m_w_ffn_down, final_norm_w=m_final_norm_w)
    v = dict(norm1_w=v_norm1_w, w_in=v_w_in, gdn_conv_w=v_gdn_conv_w, gdn_A_log=v_gdn_A_log, gdn_dt_bias=v_gdn_dt_bias,
             gdn_out_norm_w=v_gdn_out_norm_w, fox_f_bias=v_fox_f_bias, fox_q_norm_w=v_fox_q_norm_w,
             fox_k_norm_w=v_fox_k_norm_w, w_out=v_w_out, norm2_w=v_norm2_w, w_ffn_gate=v_w_ffn_gate,
             w_ffn_up=v_w_ffn_up, w_ffn_down=v_w_ffn_down, final_norm_w=v_final_norm_w)
    late = ("w_out", "w_ffn_gate", "w_ffn_up", "w_ffn_down")
    xs, tgt = x[0], loss_target[0]
    vp = _vector_params({n: w[n] for n in VECTORS})

    def rows_of(d, n):
        return d[n].transpose(0, 2, 1) if n in TRANSPOSED else d[n]

    wr, mr, vr = ({n: rows_of(d, n) for n, _ in MATRICES} for d in (w, m, v))

    w_in_blocks, conv_blocks = _gather_two_level([wr["w_in"][0].astype(WIRE), w["gdn_conv_w"][0]], "gather_in")
    w_t = _join(w_in_blocks, 0)
    conv_w = _join(conv_blocks, 1)
    f = _mixer_forward(xs, vp, w_t, _small_rows(w_t), conv_w, ("gather", [wr[n][0].astype(WIRE) for n in late]))
    full = {n: _join(b, 0) for n, b in zip(late, _spread_wait(f["carried"], "gather", f["mixg"], "gather_late_wait"))}

    (h2, act, dgate, dup, dx3, dx2, dmixg, dmixf, dn2w, dfw, loss) = _tail(
        xs, f["mixg"], f["mixf"], tgt, full["w_out"], vp["n2w"], full["w_ffn_gate"], full["w_ffn_up"],
        full["w_ffn_down"], vp["fw"])
    dlate = _late_grads(f, h2, act, dgate, dup, dx3, dx2)

    grad_x, dw_in, dconv, small, state_grads, state_own = _mixer_backward(
        xs, vp, w_t, _small_rows(w_t), conv_w, f, dx2, dmixg, dmixf, ("scatter", [_cut(dlate[n], 0) for n in late]),
        scatter_own=True)
    vec = _pack_vectors(small["dn1w"], dn2w, dfw, small["gsum"], small["donw"], small["dfnw"], loss)

    state_vec, token = _spread_start([vec], "gather", "vectors_start")
    parts = dict(zip(late, _spread_wait(state_grads, "scatter", token, "grads_late_wait")))
    results = [{}, {}, {}, {}]

    updated = {}

    def update(n):
        out = _adam_matrix(parts[n], wr[n], mr[n], vr[n], "adam_" + n)
        updated[n] = out[0]
        for d, a in zip(results, out):
            d[n] = a.transpose(0, 2, 1) if n in TRANSPOSED else a

    for n in late:
        update(n)
    grad_x = _copy_rows(grad_x, "grad_x_result")
    ready = [updated[n] for n in late] + [d["w_in"] for d in (wr, mr, vr)] + [grad_x]
    (parts_vec,) = _spread_wait(state_vec, "gather", ready, "vectors_wait")
    row = lambda a: a.reshape(1, -1)
    *vec_out, total_loss = _adam_vectors(parts_vec, [row(w[n]) for n in VECTORS], [row(m[n]) for n in VECTORS],
                                         [row(v[n]) for n in VECTORS])
    parts["w_in"], parts["gdn_conv_w"] = _spread_wait(state_own, "scatter", parts_vec, "own_wait")
    update("w_in")
    update("gdn_conv_w")
    for d, arrs in zip(results, vec_out):
        for n, a in zip(VECTORS, arrs):
            d[n] = a.reshape(w[n].shape)
    return (total_loss[0, 0], grad_x, *[d[n] for d in results for n in WEIGHTS])
```

```python
import functools

import jax
import jax.numpy as jnp
from jax import lax
from jax.experimental import pallas as pl
from jax.experimental.pallas import tpu as pltpu

F32 = jnp.float32
MXU = jnp.bfloat16
WIRE = jnp.bfloat16
HI = lax.Precision.HIGHEST
EPS = 1e-6

N_DEV = 8
HEADS = 8
DH = 64
WIDTH = HEADS * DH
CHUNK = 64
LANE = 128
ROW_ALIGN = 16
TB = 256
QB = 256
VMEM_LIMIT = 60 * 1024 * 1024

ADAM_LR = 0.001
ADAM_B1 = 0.9
ADAM_B2 = 0.999
ADAM_EPS = 1e-08
ADAM_WD = 0.01
ADAM_STEP = 10

MESH = pl.DeviceIdType.MESH


def _params(sem=None):
    return pltpu.CompilerParams(dimension_semantics=sem, vmem_limit_bytes=VMEM_LIMIT)


def _resident(shape):
    n = len(shape)
    return pl.BlockSpec(shape, lambda *_: (0,) * n, pipeline_mode=pl.Buffered(1))


def _dot(a, b):
    return jnp.dot(a.astype(MXU), b.astype(MXU), preferred_element_type=F32)


def _dot_nt(a, b):
    return lax.dot_general(a.astype(MXU), b.astype(MXU), (((1,), (1,)), ((), ())), preferred_element_type=F32)


def _dot_tn(a, b):
    return lax.dot_general(a.astype(MXU), b.astype(MXU), (((0,), (0,)), ((), ())), preferred_element_type=F32)


def _hdot(a, b):
    return jnp.dot(a, b, precision=HI, preferred_element_type=F32)


def _hdot_nt(a, b):
    return lax.dot_general(a, b, (((1,), (1,)), ((), ())), precision=HI, preferred_element_type=F32)


def _hdot_tn(a, b):
    return lax.dot_general(a, b, (((0,), (0,)), ((), ())), precision=HI, preferred_element_type=F32)


def _sigmoid(x):
    return 0.5 * jnp.tanh(0.5 * x) + 0.5


def _softplus(x):
    return jnp.maximum(x, 0.0) + jnp.log(1.0 + jnp.exp(-jnp.abs(x)))


def _head_sum_matrix():
    ri = lax.broadcasted_iota(jnp.int32, (LANE, LANE), 0) // DH
    ci = lax.broadcasted_iota(jnp.int32, (LANE, LANE), 1) // DH
    return (ri == ci).astype(F32)


def _group_sum(a, ones_matrix):
    hi = a.astype(jnp.bfloat16)
    lo = (a - hi.astype(F32)).astype(jnp.bfloat16)
    m = ones_matrix.astype(jnp.bfloat16)
    return jnp.dot(hi, m, preferred_element_type=F32) + jnp.dot(lo, m, preferred_element_type=F32)


def _shift_down(x, s):
    return pltpu.roll(x, s, 0)


def _shift_up(x, s):
    return pltpu.roll(x, x.shape[0] - s, 0)


ROWS_A = 4 * WIDTH
ROWS_B = ROWS_A + 2 * HEADS
ROWS_C = ROWS_B + 4 * WIDTH


def _small_rows(w_t):
    return jnp.concatenate([w_t[ROWS_A:ROWS_B], w_t[ROWS_C:], jnp.zeros((LANE - 3 * HEADS, w_t.shape[1]), w_t.dtype)])


def _inproj(x, n1w, w_t, ws_t):
    t, d = x.shape
    tb = min(TB, t)

    def body(x_ref, nw_ref, wt_ref, ws_ref, h_ref, pm_ref, ps_ref):
        xv = x_ref[...]
        r = lax.rsqrt(jnp.mean(xv * xv, axis=-1, keepdims=True) + EPS)
        h = (xv * r * nw_ref[...]).astype(MXU)
        h_ref[...] = h
        pm_ref[:, 0:ROWS_A] = _dot_nt(h, wt_ref[0:ROWS_A, :])
        pm_ref[:, ROWS_A:2 * ROWS_A] = _dot_nt(h, wt_ref[ROWS_B:ROWS_C, :])
        ps_ref[...] = _dot_nt(h, ws_ref[...])

    return pl.pallas_call(
        body, name="inproj", grid=(t // tb,),
        in_specs=[pl.BlockSpec((tb, d), lambda i: (i, 0)), _resident((1, d)), _resident(w_t.shape), _resident((LANE, d))],
        out_specs=[pl.BlockSpec((tb, d), lambda i: (i, 0)), pl.BlockSpec((tb, 2 * ROWS_A), lambda i: (i, 0)),
                   pl.BlockSpec((tb, LANE), lambda i: (i, 0))],
        out_shape=[jax.ShapeDtypeStruct((t, d), MXU), jax.ShapeDtypeStruct((t, 2 * ROWS_A), F32),
                   jax.ShapeDtypeStruct((t, LANE), F32)],
        compiler_params=_params(("arbitrary",)),
    )(x, n1w, w_t, ws_t)


def _inproj_bwd(x, n1w, dx2, dgdn, dz, dfox, dfg, dps, w_t, ws_t, carry=None):
    t, d = x.shape
    tb = min(TB, t)
    w3 = 3 * WIDTH
    c_in, c_in_specs, c_out_shape, c_out_specs, c_sem = _carry_operands(*carry) if carry else ([], [], [], [], None)
    n_c = len(c_in)

    def body(*refs):
        x_ref, nw_ref, dx2_ref, dgdn_ref, dz_ref, dfox_ref, dfg_ref, dps_ref, wm_ref, ws_ref = refs[:10]
        gx_ref, dnw_ref = refs[10 + n_c:12 + n_c]
        if carry:
            _carry_step(carry[0], refs[10:10 + n_c], refs[12 + n_c], refs[13 + n_c], refs[-1], pl.program_id(0), t // tb)
        dh = _dot(dgdn_ref[...], wm_ref[0:w3, :])
        dh += _dot(dz_ref[...], wm_ref[w3:ROWS_A, :])
        dh += _dot(dfox_ref[...], wm_ref[ROWS_B:ROWS_B + w3, :])
        dh += _dot(dfg_ref[...], wm_ref[ROWS_B + w3:ROWS_C, :])
        dh += _dot(dps_ref[...], ws_ref[...])
        xv = x_ref[...]
        r = lax.rsqrt(jnp.mean(xv * xv, axis=-1, keepdims=True) + EPS)
        xn = xv * r

        @pl.when(pl.program_id(0) == 0)
        def _():
            dnw_ref[...] = jnp.zeros_like(dnw_ref)

        dnw_ref[...] += jnp.sum(dh * xn, axis=0, keepdims=True)
        g = dh * nw_ref[...]
        gx_ref[0] = dx2_ref[...] + r * (g - xn * jnp.mean(g * xn, axis=-1, keepdims=True))

    def tok(n):
        return pl.BlockSpec((tb, n), lambda i: (i, 0))

    out = pl.pallas_call(
        body, name="inproj_bwd", grid=(t // tb,),
        in_specs=[tok(d), _resident((1, d)), tok(d), tok(w3), tok(WIDTH), tok(w3), tok(WIDTH), tok(LANE),
                  _resident(w_t.shape), _resident(ws_t.shape)] + c_in_specs,
        out_specs=[pl.BlockSpec((1, tb, d), lambda i: (0, i, 0)), pl.BlockSpec((1, d), lambda i: (0, 0))] + c_out_specs,
        out_shape=[jax.ShapeDtypeStruct((1, t, d), F32), jax.ShapeDtypeStruct((1, d), F32)] + c_out_shape,
        input_output_aliases={10 + j: 4 + j for j in range(n_c)},
        scratch_shapes=[c_sem] if carry else [],
        compiler_params=_params(("arbitrary",)),
    )(x, n1w, dx2, dgdn, dz, dfox, dfg, dps, w_t, ws_t, *c_in)
    return (out[0], out[1], _carry_state(out[2:])) if carry else tuple(out)


def _gate_lanes(shape):
    lane = lax.broadcasted_iota(jnp.int32, shape, 1)
    return lane < HEADS, (lane >= HEADS) & (lane < 2 * HEADS), (lane >= 2 * HEADS) & (lane < 3 * HEADS)


def _block_masks():
    ri = lax.broadcasted_iota(jnp.int32, (LANE, LANE), 0)
    ci = lax.broadcasted_iota(jnp.int32, (LANE, LANE), 1)
    same = (ri // CHUNK) == (ci // CHUNK)
    return ((ri >= ci).astype(F32), (ri <= ci).astype(F32), (same & (ri >= ci)).astype(F32),
            (same & (ri <= ci)).astype(F32), same.astype(F32))


def _gates(ps, gparams):
    t = ps.shape[0]
    nb = t // LANE

    def body(ps_ref, gp_ref, out_ref, run_ref, tot_ref, runt_ref):
        p = ps_ref[...]
        is_b, is_a, is_f = _gate_lanes(p.shape)
        z = p + gp_ref[0:1, :]
        neg_exp_a = -jnp.exp(gp_ref[1:2, :])
        glog = neg_exp_a * _softplus(z)
        logf = -_softplus(-z)
        out_ref[...] = jnp.where(is_b, _sigmoid(p), jnp.where(is_a, glog, jnp.where(is_f, logf, 0.0)))
        tril, _, tril_c, _, same_c = _block_masks()
        off = jnp.zeros((1, LANE), F32)
        for b in range(nb):
            rows = slice(b * LANE, (b + 1) * LANE)
            blk = out_ref[rows, :]
            ga = jnp.where(is_a[:LANE], blk, 0.0)
            fb = _hdot(tril, jnp.where(is_f[:LANE], blk, 0.0)) + off
            run = fb + _hdot(tril_c, ga)
            run_ref[rows, :] = run
            runt_ref[:, rows] = run.T
            tot_ref[rows, :] = _hdot(same_c, ga)
            off = fb[LANE - 1:LANE, :]

    return pl.pallas_call(
        body, name="gates",
        out_shape=[jax.ShapeDtypeStruct((t, LANE), F32)] * 3 + [jax.ShapeDtypeStruct((LANE, t), F32)],
        compiler_params=_params(),
    )(ps, gparams)


def _gates_bwd(ps, gparams, dcol_g, dtot_g, dcol_f, drow_g, drow_f):
    t = ps.shape[0]
    nb = t // LANE

    def body(ps_ref, gp_ref, dcg_ref, dtg_ref, dcf_ref, drg_ref, drf_ref, dps_ref, sums_ref, dl_ref, d0_ref, tr_ref):
        p = ps_ref[...]
        is_b, is_a, is_f = _gate_lanes(p.shape)
        _, triu, _, triu_c, same_c = _block_masks()
        tr_ref[...] = jnp.zeros_like(tr_ref)
        off = jnp.zeros((1, LANE), F32)
        for b in reversed(range(nb)):
            rows = slice(b * LANE, (b + 1) * LANE)
            tr_ref[HEADS:2 * HEADS, :] = drg_ref[:, rows]
            tr_ref[2 * HEADS:3 * HEADS, :] = drf_ref[:, rows]
            d = dcg_ref[rows, :] + dcf_ref[rows, :] + tr_ref[...].T
            d0_ref[rows, :] = d
            dlf = _hdot(triu, jnp.where(is_f[:LANE], d, 0.0)) + off
            dla = (_hdot(triu_c, jnp.where(is_a[:LANE], d, 0.0))
                   + _hdot(same_c, jnp.where(is_a[:LANE], dtg_ref[rows, :], 0.0)))
            dl_ref[rows, :] = dlf + dla
            off = dlf[0:1, :]
        z = p + gp_ref[0:1, :]
        neg_exp_a = -jnp.exp(gp_ref[1:2, :])
        sb = _sigmoid(p)
        glog = neg_exp_a * _softplus(z)
        dl = dl_ref[...]
        dp = jnp.where(is_b, d0_ref[...] * sb * (1.0 - sb),
                       jnp.where(is_a, dl * neg_exp_a * _sigmoid(z), jnp.where(is_f, dl * _sigmoid(-z), 0.0)))
        dps_ref[...] = dp
        s_a = jnp.sum(jnp.where(is_a, dl * glog, 0.0), axis=0, keepdims=True)
        s_p = jnp.sum(jnp.where(is_b, 0.0, dp), axis=0, keepdims=True)
        row = lax.broadcasted_iota(jnp.int32, (8, LANE), 0)
        from_a = pltpu.roll(jnp.where(row == 0, s_a, jnp.where(row == 1, s_p, 0.0)), LANE - HEADS, 1)
        from_f = pltpu.roll(jnp.where(row == 2, s_p, 0.0), LANE - 2 * HEADS, 1)
        lane = lax.broadcasted_iota(jnp.int32, (8, LANE), 1)
        sums_ref[...] = jnp.where(lane < HEADS, from_a + from_f, 0.0)

    return pl.pallas_call(
        body, name="gates_bwd",
        out_shape=[jax.ShapeDtypeStruct((t, LANE), F32), jax.ShapeDtypeStruct((8, LANE), F32)],
        scratch_shapes=[pltpu.VMEM((t, LANE), F32), pltpu.VMEM((t, LANE), F32), pltpu.VMEM((LANE, LANE), F32)],
        compiler_params=_params(),
    )(ps, gparams, dcol_g, dtot_g, dcol_f, drow_g, drow_f)


def _conv(xv, w):
    acc = w[3:4, :] * xv
    for s in range(1, 4):
        acc += w[3 - s:4 - s, :] * _shift_down(xv, s)
    return acc


PREP_ROWS = 256
HALO = 8
PREP_UNROLL = 2


def _tile_loop(n, tile, init):
    if n % PREP_UNROLL:
        return lax.fori_loop(0, n, tile, init)

    def trip(g, carry):
        for u in range(PREP_UNROLL):
            carry = tile(g * PREP_UNROLL + u, carry)
        return carry

    return lax.fori_loop(0, n // PREP_UNROLL, trip, init)


def _tile_rows(r):
    return pl.ds(pl.multiple_of(r * PREP_ROWS, PREP_ROWS), PREP_ROWS)


def _gdn_prep(pm, conv_w):
    t = pm.shape[0]
    nj = WIDTH // LANE
    win = PREP_ROWS + HALO

    def body(x_ref, w_ref, o_ref, xp_ref):
        kind = pl.program_id(0)
        xp_ref[0:HALO, :] = jnp.zeros((HALO, LANE), F32)
        xp_ref[HALO:, :] = x_ref[...]
        w = w_ref[...]
        hs = _head_sum_matrix()

        def tile(r, _, normed):
            xw = xp_ref[pl.ds(pl.multiple_of(r * PREP_ROWS, PREP_ROWS), win), :]
            acc = _conv(xw, w)[HALO:]
            out = acc * _sigmoid(acc)
            if normed:
                out = out * lax.rsqrt(_group_sum(out * out, hs) + EPS)
            o_ref[0, 0, _tile_rows(r), :] = out[:, :DH]
            o_ref[0, 1, _tile_rows(r), :] = out[:, DH:]
            return 0

        @pl.when(kind < 2)
        def _():
            _tile_loop(t // PREP_ROWS, functools.partial(tile, normed=True), 0)

        @pl.when(kind == 2)
        def _():
            _tile_loop(t // PREP_ROWS, functools.partial(tile, normed=False), 0)

    return pl.pallas_call(
        body, name="gdn_prep", grid=(3, nj),
        in_specs=[pl.BlockSpec((t, LANE), lambda i, j: (0, i * nj + j)),
                  pl.BlockSpec((4, LANE), lambda i, j: (0, i * nj + j))],
        out_specs=pl.BlockSpec((1, 2, t, DH), lambda i, j: (i, j, 0, 0)),
        out_shape=jax.ShapeDtypeStruct((3, HEADS, t, DH), F32),
        scratch_shapes=[pltpu.VMEM((t + HALO, LANE), F32)],
        compiler_params=_params(("arbitrary", "arbitrary")),
    )(pm, conv_w)


def _gdn_prep_bwd(pm, conv_w, dqkv):
    t = pm.shape[0]
    nj = WIDTH // LANE
    win = PREP_ROWS + 2 * HALO

    def body(x_ref, w_ref, d_ref, dx_ref, dw_ref, xp_ref, dp_ref):
        kind = pl.program_id(0)
        zeros = jnp.zeros((HALO, LANE), F32)
        for ref in (xp_ref, dp_ref):
            ref[0:HALO, :] = zeros
            ref[HALO + t:, :] = zeros
        xp_ref[HALO:HALO + t, :] = x_ref[...]
        dp_ref[HALO:HALO + t, 0:DH] = d_ref[0, 0]
        dp_ref[HALO:HALO + t, DH:] = d_ref[0, 1]
        w = w_ref[...]
        hs = _head_sum_matrix()
        rows = lax.broadcasted_iota(jnp.int32, (win, LANE), 0)
        in_tile = (rows >= HALO) & (rows < HALO + PREP_ROWS)

        def tile(r, dw, normed):
            start = pl.multiple_of(r * PREP_ROWS, PREP_ROWS)
            xw = xp_ref[pl.ds(start, win), :]
            dy = dp_ref[pl.ds(start, win), :]
            acc = _conv(xw, w)
            sg = _sigmoid(acc)
            if normed:
                y = acc * sg
                rn = lax.rsqrt(_group_sum(y * y, hs) + EPS)
                yn = y * rn
                dy = rn * (dy - yn * _group_sum(dy * yn, hs))
            dacc = dy * sg * (1.0 + acc * (1.0 - sg))
            dx = w[3:4, :] * dacc
            for s in range(1, 4):
                dx += w[3 - s:4 - s, :] * _shift_up(dacc, s)
            dx_ref[_tile_rows(r), :] = dx[HALO:HALO + PREP_ROWS]
            dm = jnp.where(in_tile, dacc, 0.0)
            return tuple(dw[i] + jnp.sum(dm * (xw if i == 3 else _shift_down(xw, 3 - i)), axis=0, keepdims=True)
                         for i in range(4))

        def run(normed):
            dw = _tile_loop(t // PREP_ROWS, functools.partial(tile, normed=normed),
                            tuple(jnp.zeros((1, LANE), F32) for _ in range(4)))
            for i in range(4):
                dw_ref[i:i + 1, :] = dw[i]

        pl.when(kind < 2)(functools.partial(run, True))
        pl.when(kind == 2)(functools.partial(run, False))

    return pl.pallas_call(
        body, name="gdn_prep_bwd", grid=(3, nj),
        in_specs=[pl.BlockSpec((t, LANE), lambda i, j: (0, i * nj + j)),
                  pl.BlockSpec((4, LANE), lambda i, j: (0, i * nj + j)),
                  pl.BlockSpec((1, 2, t, DH), lambda i, j: (i, j, 0, 0))],
        out_specs=[pl.BlockSpec((t, LANE), lambda i, j: (0, i * nj + j)),
                   pl.BlockSpec((4, LANE), lambda i, j: (0, i * nj + j))],
        out_shape=[jax.ShapeDtypeStruct((t, 3 * WIDTH), F32), jax.ShapeDtypeStruct((4, 3 * WIDTH), F32)],
        scratch_shapes=[pltpu.VMEM((t + 2 * HALO, LANE), F32), pltpu.VMEM((t + 2 * HALO, LANE), F32)],
        compiler_params=_params(("arbitrary", "arbitrary")),
    )(pm, conv_w, dqkv)


FOX_COL0 = 4 * WIDTH // LANE


def _fox_prep(pm, nw):
    t = pm.shape[0]
    nj = WIDTH // LANE

    def body(x_ref, w_ref, o_ref):
        kind = pl.program_id(0)
        hs = _head_sum_matrix()
        wk = w_ref[pl.ds(kind, 1), :]

        def tile(r, _, normed):
            out = x_ref[_tile_rows(r), :]
            if normed:
                out = out * lax.rsqrt(_group_sum(out * out, hs) * (1.0 / DH) + EPS) * wk
            o_ref[0, 0, _tile_rows(r), :] = out[:, :DH]
            o_ref[0, 1, _tile_rows(r), :] = out[:, DH:]
            return 0

        @pl.when(kind < 2)
        def _():
            _tile_loop(t // PREP_ROWS, functools.partial(tile, normed=True), 0)

        @pl.when(kind == 2)
        def _():
            _tile_loop(t // PREP_ROWS, functools.partial(tile, normed=False), 0)

    return pl.pallas_call(
        body, name="fox_prep", grid=(3, nj),
        in_specs=[pl.BlockSpec((t, LANE), lambda i, j: (0, FOX_COL0 + i * nj + j)),
                  pl.BlockSpec((3, LANE), lambda i, j: (0, 0))],
        out_specs=pl.BlockSpec((1, 2, t, DH), lambda i, j: (i, j, 0, 0)),
        out_shape=jax.ShapeDtypeStruct((3, HEADS, t, DH), F32),
        compiler_params=_params(("arbitrary", "arbitrary")),
    )(pm, nw)


def _fox_prep_bwd(pm, nw, dqkv):
    t = pm.shape[0]
    nj = WIDTH // LANE

    def body(x_ref, w_ref, d_ref, dx_ref, dw_ref):
        kind = pl.program_id(0)
        hs = _head_sum_matrix()
        wk = w_ref[pl.ds(kind, 1), :]

        def tile(r, dw):
            xv = x_ref[_tile_rows(r), :]
            rn = lax.rsqrt(_group_sum(xv * xv, hs) * (1.0 / DH) + EPS)
            xn = xv * rn
            d = jnp.concatenate([d_ref[0, 0, _tile_rows(r), :], d_ref[0, 1, _tile_rows(r), :]], axis=1)
            g = d * wk
            dx_ref[_tile_rows(r), :] = rn * (g - xn * _group_sum(g * xn, hs) * (1.0 / DH))
            return dw + jnp.sum(d * xn, axis=0, keepdims=True)

        def copy_tile(r, _):
            dx_ref[_tile_rows(r), :] = jnp.concatenate([d_ref[0, 0, _tile_rows(r), :], d_ref[0, 1, _tile_rows(r), :]], axis=1)
            return 0

        @pl.when(kind < 2)
        def _():
            dw_ref[0, 0] = _tile_loop(t // PREP_ROWS, tile, jnp.zeros((1, LANE), F32))

        @pl.when(kind == 2)
        def _():
            _tile_loop(t // PREP_ROWS, copy_tile, 0)
            dw_ref[0, 0] = jnp.zeros((1, LANE), F32)

    return pl.pallas_call(
        body, name="fox_prep_bwd", grid=(3, nj),
        in_specs=[pl.BlockSpec((t, LANE), lambda i, j: (0, FOX_COL0 + i * nj + j)),
                  pl.BlockSpec((3, LANE), lambda i, j: (0, 0)),
                  pl.BlockSpec((1, 2, t, DH), lambda i, j: (i, j, 0, 0))],
        out_specs=[pl.BlockSpec((t, LANE), lambda i, j: (0, i * nj + j)),
                   pl.BlockSpec((1, 1, 1, LANE), lambda i, j: (i, j, 0, 0))],
        out_shape=[jax.ShapeDtypeStruct((t, 3 * WIDTH), F32), jax.ShapeDtypeStruct((3, nj, 1, LANE), F32)],
        compiler_params=_params(("arbitrary", "arbitrary")),
    )(pm, nw, dqkv)


SC = 256
CPS = SC // CHUNK
GDN_HP_FWD = 4
GDN_HP_BWD = 2
Q_SCALE = DH ** -0.5


def _sc_masks():
    ri = lax.broadcasted_iota(jnp.int32, (SC, SC), 0)
    ci = lax.broadcasted_iota(jnp.int32, (SC, SC), 1)
    same = (ri // CHUNK) == (ci // CHUNK)
    return same & (ri >= ci), same & (ri > ci), ri == ci


def _unit_lower_inverses(ms, eye):
    invs = [jnp.where(eye, 1.0, 0.0) + m for m in ms]
    ms = [_dot(m, m) for m in ms]
    for _ in range(4):
        both = [_dot(jnp.concatenate([inv, m], axis=0), m) for inv, m in zip(invs, ms)]
        invs = [inv + b[:SC] for inv, b in zip(invs, both)]
        ms = [b[SC:] for b in both]
    return [inv + _dot(inv, m) for inv, m in zip(invs, ms)]


def _lane_col(blk, lane_idx):
    lane = lax.broadcasted_iota(jnp.int32, blk.shape, 1)
    return jnp.sum(jnp.where(lane == lane_idx, blk, 0.0), axis=1, keepdims=True)


def _to_lane(col, lane_idx):
    lane = lax.broadcasted_iota(jnp.int32, (col.shape[0], LANE), 1)
    return jnp.where(lane == lane_idx, col, 0.0)


def _gdn_columns(gates_ref, run_ref, tot_ref, runt_ref, rows, h):
    return (_lane_col(gates_ref[rows, :], h), _lane_col(run_ref[rows, :], HEADS + h),
            _lane_col(tot_ref[rows, :], HEADS + h), runt_ref[pl.ds(h, 1), rows])


def _gdn_local(q, k, beta, gc, gl, grow, causal, with_kk=True):
    decay = jnp.exp(jnp.where(causal, gc - grow, -1e30))
    egc = jnp.exp(gc)
    ekd = jnp.exp(gl - gc)
    qs = q * Q_SCALE
    kb = k * beta
    if with_kk:
        both = _dot_nt(jnp.concatenate([kb, qs], axis=0), k)
        kk, qk = both[:SC], both[SC:]
    else:
        kk, qk = None, _dot_nt(qs, k)
    return beta, gl, decay, egc, ekd, qs, kb, kk, qk, jnp.where(causal, qk * decay, 0.0)


def _chunk_rows(c):
    return pl.ds(c * CHUNK if isinstance(c, int) else pl.multiple_of(c * CHUNK, CHUNK), CHUNK)


def _sc_rows(b):
    return pl.ds(b * SC if isinstance(b, int) else pl.multiple_of(b * SC, SC), SC)


def _gdn_fwd(qkv, gates, run, tot, run_t):
    t = qkv.shape[2]
    nc = t // CHUNK
    nsc = t // SC

    hp_n = GDN_HP_FWD
    heads = range(hp_n)

    def body(qkv_ref, gates_ref, run_ref, tot_ref, runt_ref, o_ref, st_ref, inv_ref, kc_s, qc_s, g_s, au_s):
        hp = pl.program_id(0)
        causal, strict, eye = _sc_masks()

        def local(b, _):
            rows = _sc_rows(b)
            loc = [_gdn_local(qkv_ref[0, hh, rows, :], qkv_ref[1, hh, rows, :],
                              *_gdn_columns(gates_ref, run_ref, tot_ref, runt_ref, rows, hp * hp_n + hh), causal)
                   for hh in heads]
            invs = _unit_lower_inverses([-jnp.where(strict, l[7] * l[2], 0.0) for l in loc], eye)
            uws = []
            for hh in heads:
                beta, _, _, egc, _, _, kb, _, _, _ = loc[hh]
                inv_ref[hh, rows, :] = invs[hh].astype(inv_ref.dtype)
                uws.append(_dot(invs[hh], jnp.concatenate([qkv_ref[2, hh, rows, :] * beta, kb * egc], axis=1)))
            for hh in heads:
                _, _, _, egc, ekd, qs, _, _, _, attn = loc[hh]
                auw = _dot(attn, uws[hh])
                g_s[hh, rows, :] = qs * egc - auw[:, DH:]
                au_s[hh, rows, :] = auw[:, :DH]
                kd = qkv_ref[1, hh, rows, :] * ekd
                for j in range(CPS):
                    sl = slice(j * CHUNK, (j + 1) * CHUNK)
                    both = _dot_tn(kd[sl], uws[hh][sl])
                    kc_s[hh, b * CPS + j] = both[:, DH:]
                    qc_s[hh, b * CPS + j] = both[:, :DH]
            return 0

        def step(c, states):
            rows = _chunk_rows(c)
            tot_row = tot_ref[pl.ds(c * CHUNK, 1), :]
            new = []
            for hh in heads:
                s = states[hh]
                st_ref[hh, c] = s
                o_ref[hh, rows, :] = _dot(g_s[hh, rows, :], s) + au_s[hh, rows, :]
                egl = jnp.exp(_lane_col(tot_row, HEADS + hp * hp_n + hh))
                new.append(egl * s - _dot(kc_s[hh, c], s) + qc_s[hh, c])
            return tuple(new)

        def steps_of(b, states):
            for j in range(CPS):
                states = step(b * CPS + j, states)
            return states

        def fused(b, states):
            states = steps_of(b - 1, states)
            local(b, 0)
            return states

        local(0, 0)
        states = lax.fori_loop(1, nsc, fused, tuple(jnp.zeros((DH, DH), F32) for _ in heads))
        steps_of(nsc - 1, states)

    whole = pl.BlockSpec((t, LANE), lambda h: (0, 0))
    once = dict(pipeline_mode=pl.Buffered(1))
    return pl.pallas_call(
        body, name="gdn_fwd", grid=(HEADS // hp_n,),
        in_specs=[pl.BlockSpec((3, hp_n, t, DH), lambda h: (0, h, 0, 0), **once), whole, whole, whole,
                  pl.BlockSpec((HEADS, t), lambda h: (1, 0))],
        out_specs=[pl.BlockSpec((hp_n, t, DH), lambda h: (h, 0, 0), **once),
                   pl.BlockSpec((hp_n, nc, DH, DH), lambda h: (h, 0, 0, 0), **once),
                   pl.BlockSpec((hp_n, t, SC), lambda h: (h, 0, 0), **once)],
        out_shape=[jax.ShapeDtypeStruct((HEADS, t, DH), F32), jax.ShapeDtypeStruct((HEADS, nc, DH, DH), F32),
                   jax.ShapeDtypeStruct((HEADS, t, SC), MXU)],
        scratch_shapes=[pltpu.VMEM((hp_n, nc, DH, DH), F32), pltpu.VMEM((hp_n, nc, DH, DH), F32),
                        pltpu.VMEM((hp_n, t, DH), F32), pltpu.VMEM((hp_n, t, DH), F32)],
        compiler_params=_params(("arbitrary",)),
    )(qkv, gates, run, tot, run_t)


def _gdn_bwd(qkv, gates, run, tot, run_t, inv, states, do):
    t = qkv.shape[2]
    nc = t // CHUNK
    nsc = t // SC

    hp_n = GDN_HP_BWD
    heads = range(hp_n)

    def body(qkv_ref, gates_ref, run_ref, tot_ref, runt_ref, inv_ref, st_ref, do_ref,
             dqkv_ref, dcol_ref, dtot_ref, drow_ref, uw_s, kc_s, h_s, dsn_s):
        hp = pl.program_id(0)
        causal, strict, _ = _sc_masks()

        @pl.when(hp == 0)
        def _():
            dcol_ref[...] = jnp.zeros_like(dcol_ref)
            dtot_ref[...] = jnp.zeros_like(dtot_ref)

        def local_of(hh, rows, with_kk=True):
            return _gdn_local(qkv_ref[0, hh, rows, :], qkv_ref[1, hh, rows, :],
                              *_gdn_columns(gates_ref, run_ref, tot_ref, runt_ref, rows, hp * hp_n + hh), causal, with_kk)

        def local(b, _):
            rows = _sc_rows(b)
            loc = [local_of(hh, rows, with_kk=False) for hh in heads]
            us, ws = [], []
            for hh in heads:
                beta, _, _, egc, _, _, kb, _, _, _ = loc[hh]
                inv_b = inv_ref[hh, rows, :]
                us.append(_dot(inv_b, qkv_ref[2, hh, rows, :] * beta))
                ws.append(_dot(inv_b, kb * egc))
            gs = [loc[hh][5] * loc[hh][3] - _dot(loc[hh][9], ws[hh]) for hh in heads]
            for hh in heads:
                uw_s[hh, rows, :] = jnp.concatenate([us[hh], ws[hh]], axis=1)
                kd = qkv_ref[1, hh, rows, :] * loc[hh][4]
                dout = do_ref[hh, rows, :]
                for j in range(CPS):
                    sl = slice(j * CHUNK, (j + 1) * CHUNK)
                    kc_s[hh, b * CPS + j] = _dot_tn(kd[sl], ws[hh][sl])
                    h_s[hh, b * CPS + j] = _dot_tn(gs[hh][sl], dout[sl])
            return 0

        def step(c, dss):
            tot_row = tot_ref[pl.ds(c * CHUNK, 1), :]
            new = []
            for hh in heads:
                ds = dss[hh]
                dsn_s[hh, c] = ds
                egl = jnp.exp(_lane_col(tot_row, HEADS + hp * hp_n + hh))
                new.append(egl * ds - _dot_tn(kc_s[hh, c], ds) + h_s[hh, c])
            return tuple(new)

        def steps_of(b, dss):
            for j in reversed(range(CPS)):
                dss = step(b * CPS + j, dss)
            return dss

        def fused(i, dss):
            b = nsc - 2 - i
            dss = steps_of(b + 1, dss)
            local(b, 0)
            return dss

        local(nsc - 1, 0)
        dss = lax.fori_loop(0, nsc - 1, fused, tuple(jnp.zeros((DH, DH), F32) for _ in heads))
        steps_of(0, dss)

        def back(b, _):
            rows = _sc_rows(b)
            first = lax.broadcasted_iota(jnp.int32, (CHUNK, 1), 0) == 0
            loc = [local_of(hh, rows) for hh in heads]
            sign = jnp.where(lax.broadcasted_iota(jnp.int32, (1, LANE), 1) < DH, 1.0, -1.0)
            mid = []
            for hh in heads:
                beta, gl, decay, egc, ekd, qs, kb, kk, qk, attn = loc[hh]
                uw = uw_s[hh, rows, :]
                kd = qkv_ref[1, hh, rows, :] * ekd
                dout = do_ref[hh, rows, :]
                dg_p, dkd_p, duw_p, dgl_p = [], [], [], []
                for j in range(CPS):
                    sl = slice(j * CHUNK, (j + 1) * CHUNK)
                    s = st_ref[hh, b * CPS + j]
                    dsn = dsn_s[hh, b * CPS + j]
                    both = _dot_nt(jnp.concatenate([dsn, dout[sl]], axis=0), s)
                    dg_p.append(both[CHUNK:])
                    dsc = jnp.concatenate([dsn, -both[:CHUNK]], axis=1)
                    dkd_p.append(_dot_nt(uw[sl], dsc))
                    duw_p.append(_dot(kd[sl], dsc))
                    degl = jnp.sum(jnp.sum(s * dsn, axis=1, keepdims=True), axis=0, keepdims=True)
                    dgl_p.append(jnp.where(first, degl * jnp.exp(gl[j * CHUNK:j * CHUNK + 1, :]), 0.0))
                dg, dkd = jnp.concatenate(dg_p, axis=0), jnp.concatenate(dkd_p, axis=0)
                dod = jnp.concatenate([dout, dg], axis=1)
                da = jnp.where(causal, _dot_nt(dod, uw * sign), 0.0)
                duw = jnp.concatenate(duw_p, axis=0) + _dot_tn(attn, dod) * sign
                mid.append((kd, dg, dkd, da, duw, jnp.concatenate(dgl_p, axis=0)))
            inv_ts = [inv_ref[hh, rows, :].astype(F32).T for hh in heads]
            its = [_dot(inv_ts[hh], mid[hh][4]) for hh in heads]
            dinvs = []
            for hh in heads:
                beta, _, _, egc, _, _, kb, _, _, _ = loc[hh]
                dinvs.append(_dot_nt(mid[hh][4], jnp.concatenate([qkv_ref[2, hh, rows, :] * beta, kb * egc], axis=1)))
            half = [_dot(inv_ts[hh], dinvs[hh]) for hh in heads]
            dls = [jnp.where(strict, -_dot(half[hh], inv_ts[hh]), 0.0) for hh in heads]
            for hh in heads:
                head = hp * hp_n + hh
                beta, gl, decay, egc, ekd, qs, kb, kk, qk, attn = loc[hh]
                kd, dg, dkd, da, _, dgl_first = mid[hh]
                k, v = qkv_ref[1, hh, rows, :], qkv_ref[2, hh, rows, :]
                dvb, dkbe = its[hh][:, :DH], its[hh][:, DH:]
                dl = dls[hh]
                dlogd = (dl * kk + da * qk) * decay
                dd = jnp.concatenate([dl * decay, da * decay], axis=0)
                ddk = _dot(dd, k)
                dkb = ddk[:SC] + dkbe * egc
                dqs = ddk[SC:] + dg * egc
                dk = _dot_tn(dd, jnp.concatenate([kb, qs], axis=0)) + dkd * ekd + dkb * beta
                dkd_kd = jnp.sum(dkd * kd, axis=1, keepdims=True)
                narrow = dg * (qs * egc) + dkbe * (kb * egc)
                wide = dlogd[:, :LANE] + dlogd[:, LANE:] + jnp.concatenate([narrow, jnp.zeros((SC, LANE - DH), F32)], axis=1)
                dgc = jnp.sum(wide, axis=1, keepdims=True) - dkd_kd
                dbeta = jnp.sum(dkb * k + dvb * v, axis=1, keepdims=True)
                dqkv_ref[0, hh, rows, :] = dqs * Q_SCALE
                dqkv_ref[1, hh, rows, :] = dk
                dqkv_ref[2, hh, rows, :] = dvb * beta
                dcol_ref[rows, :] += _to_lane(dbeta, head) + _to_lane(dgc, HEADS + head)
                dtot_ref[rows, :] += _to_lane(dkd_kd + dgl_first, HEADS + head)
                drow_ref[pl.ds(head, 1), rows] = -jnp.sum(dlogd, axis=0, keepdims=True)
            return 0

        lax.fori_loop(0, nsc, back, 0)

    whole = pl.BlockSpec((t, LANE), lambda h: (0, 0))
    rowspec = pl.BlockSpec((HEADS, t), lambda h: (0, 0))
    sq = pltpu.VMEM((hp_n, nc, DH, DH), F32)
    per_head = pltpu.VMEM((hp_n, t, 2 * DH), F32)
    return pl.pallas_call(
        body, name="gdn_bwd", grid=(HEADS // hp_n,),
        in_specs=[pl.BlockSpec((3, hp_n, t, DH), lambda h: (0, h, 0, 0)), whole, whole, whole,
                  pl.BlockSpec((HEADS, t), lambda h: (1, 0)),
                  pl.BlockSpec((hp_n, t, SC), lambda h: (h, 0, 0)), pl.BlockSpec((hp_n, nc, DH, DH), lambda h: (h, 0, 0, 0)),
                  pl.BlockSpec((hp_n, t, DH), lambda h: (h, 0, 0))],
        out_specs=[pl.BlockSpec((3, hp_n, t, DH), lambda h: (0, h, 0, 0)), whole, whole, rowspec],
        out_shape=[jax.ShapeDtypeStruct((3, HEADS, t, DH), F32), jax.ShapeDtypeStruct((t, LANE), F32),
                   jax.ShapeDtypeStruct((t, LANE), F32), jax.ShapeDtypeStruct((HEADS, t), F32)],
        scratch_shapes=[per_head, sq, sq, sq],
        compiler_params=_params(("arbitrary",)),
    )(qkv, gates, run, tot, run_t, inv, states, do)


FOX_HP_FWD = 4
FOX_HP_BWD = 4


def _wide_t(a):
    r = a.shape[0]
    return jnp.concatenate([a, jnp.zeros((r, LANE - DH), F32)], axis=1).T[:DH]


def _tall_t(a):
    r = a.shape[1]
    return jnp.concatenate([a, jnp.zeros((LANE - DH, r), F32)], axis=0).T[:, :DH]


def _key_side_f(run_blk, head, qb):
    col = jnp.broadcast_to(_lane_col(run_blk, 2 * HEADS + head), (run_blk.shape[0], LANE))
    return jnp.concatenate([col] * (qb // LANE), axis=1)


def _diag_mask(qb):
    return lax.broadcasted_iota(jnp.int32, (qb, qb), 0) <= lax.broadcasted_iota(jnp.int32, (qb, qb), 1)


def _fox_fwd(qkv, run, run_t, carry=None):
    t = qkv.shape[2]
    qb = min(QB, t)
    nq = t // qb
    hp_n = FOX_HP_FWD
    c_in, c_in_specs, c_out_shape, c_out_specs, c_sem = _carry_operands(*carry) if carry else ([], [], [], [], None)
    n_c = len(c_in)

    def body(*refs):
        q_ref, k_ref, v_ref, run_ref, runt_ref = refs[:5]
        o_ref, lse_ref = refs[5 + n_c:7 + n_c]
        s0 = 7 + n_c + len(c_out_shape)
        kb_s, vt_s, raw_s, m_s, l_s, acc_s = refs[s0:s0 + 6]
        hp = pl.program_id(0)
        i = pl.program_id(1)

        if carry:
            _carry_step(carry[0], refs[5:5 + n_c], refs[7 + n_c], refs[8 + n_c], refs[-1], hp * nq + i,
                        (HEADS // hp_n) * nq)

        @pl.when(i == 0)
        def _():
            for hh in range(hp_n):
                kb_s[hh] = k_ref[0, hh].astype(MXU)
                for b in range(nq):
                    rows = slice(b * qb, (b + 1) * qb)
                    vt_s[hh, :, rows] = _wide_t(v_ref[0, hh, rows, :]).astype(MXU)

        qrows = pl.ds(pl.multiple_of(i * qb, qb), qb)
        qs = [(q_ref[0, hh] * Q_SCALE).astype(MXU) for hh in range(hp_n)]
        fq = [runt_ref[pl.ds(hp * hp_n + hh, 1), qrows] for hh in range(hp_n)]

        def block_rows(j):
            return pl.ds(pl.multiple_of(j * qb, qb), qb)

        def scores(j):
            for hh in range(hp_n):
                raw_s[j % 2, hh] = _dot_nt(kb_s[hh, block_rows(j), :], qs[hh])

        def absorb(j, diagonal):
            rows = block_rows(j)
            run_blk = run_ref[rows, :]
            stats, pv = [], []
            for hh in range(hp_n):
                m, l = m_s[hh], l_s[hh]
                st = raw_s[j % 2, hh] + (fq[hh] - _key_side_f(run_blk, hp * hp_n + hh, qb))
                if diagonal:
                    st = jnp.where(_diag_mask(qb), st, -1e30)
                m_new = jnp.maximum(m, jnp.max(st, axis=0, keepdims=True))
                p = jnp.exp(st - m_new)
                alpha = jnp.exp(m - m_new)
                stats.append((m_new, alpha * l + jnp.sum(p, axis=0, keepdims=True), alpha))
                pv.append(_dot(vt_s[hh, :, rows], p))
            for hh in range(hp_n):
                m_s[hh], l_s[hh] = stats[hh][0], stats[hh][1]
                acc_s[hh] = stats[hh][2] * acc_s[hh] + pv[hh]

        def kstep(j, _):
            scores(j + 1)
            absorb(j, False)
            return 0

        for hh in range(hp_n):
            m_s[hh] = jnp.full((1, qb), -1e30, F32)
            l_s[hh] = jnp.zeros((1, qb), F32)
            acc_s[hh] = jnp.zeros((DH, qb), F32)
        scores(0)
        lax.fori_loop(0, i, kstep, 0)
        absorb(i, True)
        for hh in range(hp_n):
            l = l_s[hh]
            o_ref[hh] = _tall_t(acc_s[hh] / l)
            lse_ref[pl.ds(hp * hp_n + hh, 1), qrows] = m_s[hh] + jnp.log(l)

    out = pl.pallas_call(
        body, name="fox_fwd", grid=(HEADS // hp_n, nq),
        in_specs=[pl.BlockSpec((1, hp_n, qb, DH), lambda h, i: (0, h, i, 0)),
                  pl.BlockSpec((1, hp_n, t, DH), lambda h, i: (1, h, 0, 0)),
                  pl.BlockSpec((1, hp_n, t, DH), lambda h, i: (2, h, 0, 0)),
                  pl.BlockSpec((t, LANE), lambda h, i: (0, 0)),
                  pl.BlockSpec((HEADS, t), lambda h, i: (2, 0))] + c_in_specs,
        out_specs=[pl.BlockSpec((hp_n, qb, DH), lambda h, i: (h, i, 0)),
                   pl.BlockSpec((HEADS, t), lambda h, i: (0, 0))] + c_out_specs,
        out_shape=[jax.ShapeDtypeStruct((HEADS, t, DH), F32), jax.ShapeDtypeStruct((HEADS, t), F32)] + c_out_shape,
        input_output_aliases={5 + j: 4 + j for j in range(n_c)},
        scratch_shapes=[pltpu.VMEM((hp_n, t, DH), MXU), pltpu.VMEM((hp_n, DH, t), MXU), pltpu.VMEM((2, hp_n, qb, qb), F32),
                        pltpu.VMEM((hp_n, 1, qb), F32), pltpu.VMEM((hp_n, 1, qb), F32), pltpu.VMEM((hp_n, DH, qb), F32)]
        + ([c_sem] if carry else []),
        compiler_params=_params(("arbitrary", "arbitrary")),
    )(qkv, qkv, qkv, run, run_t, *c_in)
    return (out[0], out[1], _carry_state(out[2:])) if carry else (out[0], out[1])


def _fox_bwd(qkv, run, run_t, o, lse, do, carry=None):
    t = qkv.shape[2]
    qb = min(QB, t)
    nq = t // qb
    hp_n = FOX_HP_BWD
    c_in, c_in_specs, c_out_shape, c_out_specs, c_sem = _carry_operands(*carry) if carry else ([], [], [], [], None)
    n_c = len(c_in)

    def body(*refs):
        q_ref, k_ref, v_ref, run_ref, runt_ref, o_ref, lse_ref, do_ref = refs[:8]
        dqkv_ref, dcol_ref, drow_ref = refs[8 + n_c:11 + n_c]
        s0 = 11 + n_c + len(c_out_shape)
        dqt_s, raw_s, dpt_s, dro_s, dk_s, dv_s, dsum_s = refs[s0:s0 + 7]
        hp = pl.program_id(0)
        j = pl.program_id(1)

        if carry:
            _carry_step(carry[0], refs[8:8 + n_c], refs[11 + n_c], refs[12 + n_c], refs[-1], hp * nq + j,
                        (HEADS // hp_n) * nq)

        @pl.when(j == 0)
        def _():
            dqt_s[...] = jnp.zeros_like(dqt_s)

        @pl.when((j == 0) & (hp == 0))
        def _():
            dcol_ref[...] = jnp.zeros_like(dcol_ref)
            drow_ref[...] = jnp.zeros_like(drow_ref)

        krows = pl.ds(pl.multiple_of(j * qb, qb), qb)
        run_blk = run_ref[krows, :]
        ones8 = jnp.ones((8, DH), MXU)
        kb, kt, vb, fk = [], [], [], []
        for hh in range(hp_n):
            kf = k_ref[0, hh]
            kb.append(kf.astype(MXU))
            kt.append(_wide_t(kf).astype(MXU))
            vb.append(v_ref[0, hh].astype(MXU))
            fk.append(_key_side_f(run_blk, hp * hp_n + hh, qb))

        def block_rows(i):
            return pl.ds(i * qb if isinstance(i, int) else pl.multiple_of(i * qb, qb), qb)

        def products(i):
            rows = block_rows(i)
            slot = i % 2
            for hh in range(hp_n):
                dout = do_ref[hh, rows, :]
                x = dout * o_ref[hh, rows, :]
                x_hi = x.astype(MXU)
                raw_s[slot, hh] = _dot_nt(kb[hh], q_ref[0, hh, rows, :] * Q_SCALE)
                dpt_s[slot, hh] = _dot_nt(vb[hh], dout)
                dro_s[slot, hh] = _dot_nt(ones8, x_hi) + _dot_nt(ones8, x - x_hi.astype(F32))

        def absorb(i, diagonal):
            rows = block_rows(i)
            slot = i % 2
            pieces = []
            for hh in range(hp_n):
                head = hp * hp_n + hh
                raw, dpt, drow = raw_s[slot, hh], dpt_s[slot, hh], dro_s[slot, hh, 0:1, :]
                off = runt_ref[pl.ds(head, 1), rows] - lse_ref[pl.ds(head, 1), rows]
                st = raw + (off - fk[hh])
                if diagonal:
                    st = jnp.where(_diag_mask(qb), st, -1e30)
                pt = jnp.exp(st)
                dst = pt * (dpt - drow)
                drow_ref[pl.ds(head, 1), rows] += jnp.sum(dst, axis=0, keepdims=True)
                folded = dst[:, 0:LANE]
                for c in range(1, qb // LANE):
                    folded = folded + dst[:, c * LANE:(c + 1) * LANE]
                pieces.append((_dot(dst, q_ref[0, hh, rows, :] * Q_SCALE), _dot(pt, do_ref[hh, rows, :]),
                               _dot(kt[hh], dst), folded))
            for hh in range(hp_n):
                dqt_s[hh, :, rows] += pieces[hh][2]
                if diagonal:
                    dk_s[hh], dv_s[hh], dsum_s[hh] = pieces[hh][0], pieces[hh][1], pieces[hh][3]
                else:
                    dk_s[hh] += pieces[hh][0]
                    dv_s[hh] += pieces[hh][1]
                    dsum_s[hh] += pieces[hh][3]

        def qstep(i, _):
            products(i + 1)
            absorb(i, False)
            return 0

        products(j)
        products(jnp.minimum(j + 1, nq - 1))
        absorb(j, True)
        lax.fori_loop(j + 1, nq - 1, qstep, 0)

        @pl.when(j < nq - 1)
        def _():
            absorb(nq - 1, False)
        for hh in range(hp_n):
            dqkv_ref[1, hh, krows, :] = dk_s[hh]
            dqkv_ref[2, hh, krows, :] = dv_s[hh]
            dcol_ref[krows, :] += _to_lane(-jnp.sum(dsum_s[hh], axis=1, keepdims=True), 2 * HEADS + hp * hp_n + hh)

        @pl.when(j == nq - 1)
        def _():
            for hh in range(hp_n):
                for b in range(nq):
                    rows = slice(b * qb, (b + 1) * qb)
                    dqkv_ref[0, hh, rows, :] = _tall_t(dqt_s[hh, :, rows]) * Q_SCALE

    once = dict(pipeline_mode=pl.Buffered(1))
    full = pl.BlockSpec((hp_n, t, DH), lambda h, j: (h, 0, 0), **once)
    rows8 = pl.BlockSpec((HEADS, t), lambda h, j: (0, 0))
    out = pl.pallas_call(
        body, name="fox_bwd", grid=(HEADS // hp_n, nq),
        in_specs=[pl.BlockSpec((1, hp_n, t, DH), lambda h, j: (0, h, 0, 0), **once),
                  pl.BlockSpec((1, hp_n, qb, DH), lambda h, j: (1, h, j, 0)),
                  pl.BlockSpec((1, hp_n, qb, DH), lambda h, j: (2, h, j, 0)),
                  pl.BlockSpec((t, LANE), lambda h, j: (0, 0), **once), pl.BlockSpec((HEADS, t), lambda h, j: (2, 0)),
                  full, rows8, full] + c_in_specs,
        out_specs=[pl.BlockSpec((3, hp_n, t, DH), lambda h, j: (0, h, 0, 0), **once),
                   pl.BlockSpec((t, LANE), lambda h, j: (0, 0)), rows8] + c_out_specs,
        out_shape=[jax.ShapeDtypeStruct((3, HEADS, t, DH), F32), jax.ShapeDtypeStruct((t, LANE), F32),
                   jax.ShapeDtypeStruct((HEADS, t), F32)] + c_out_shape,
        input_output_aliases={8 + j: 5 + j for j in range(n_c)},
        scratch_shapes=[pltpu.VMEM((hp_n, DH, t), F32), pltpu.VMEM((2, hp_n, qb, qb), F32), pltpu.VMEM((2, hp_n, qb, qb), F32),
                        pltpu.VMEM((2, hp_n, 8, qb), F32), pltpu.VMEM((hp_n, qb, DH), F32), pltpu.VMEM((hp_n, qb, DH), F32),
                        pltpu.VMEM((hp_n, qb, LANE), F32)] + ([c_sem] if carry else []),
        compiler_params=_params(("arbitrary", "arbitrary")),
    )(qkv, qkv, qkv, run, run_t, o, lse, do, *c_in)
    return (out[0], out[1], out[2], _carry_state(out[3:])) if carry else tuple(out)


Z_COL0 = 3 * WIDTH // LANE
FGATE_COL0 = 7 * WIDTH // LANE


def _gdn_post(o, pm, onw):
    t = pm.shape[0]

    def body(o_ref, z_ref, w_ref, m_ref):
        z = z_ref[...]
        sz = z * _sigmoid(z)
        halves = []
        for hh in range(2):
            ov = o_ref[hh]
            n = ov * lax.rsqrt(jnp.mean(ov * ov, axis=-1, keepdims=True) + EPS) * w_ref[...]
            halves.append(n * sz[:, hh * DH:(hh + 1) * DH])
        m_ref[...] = jnp.concatenate(halves, axis=1).astype(m_ref.dtype)

    return pl.pallas_call(
        body, name="gdn_post", grid=(WIDTH // LANE,),
        in_specs=[pl.BlockSpec((2, t, DH), lambda j: (j, 0, 0)), pl.BlockSpec((t, LANE), lambda j: (0, Z_COL0 + j)),
                  pl.BlockSpec((1, DH), lambda j: (0, 0))],
        out_specs=pl.BlockSpec((t, LANE), lambda j: (0, j)),
        out_shape=jax.ShapeDtypeStruct((t, WIDTH), MXU),
        compiler_params=_params(("arbitrary",)),
    )(o, pm, onw)


def _gdn_post_bwd(o, pm, onw, dmix):
    t = pm.shape[0]

    def body(o_ref, z_ref, w_ref, dm_ref, do_ref, dz_ref, dw_ref):
        @pl.when(pl.program_id(0) == 0)
        def _():
            dw_ref[...] = jnp.zeros_like(dw_ref)

        z = z_ref[...]
        sg = _sigmoid(z)
        sz = z * sg
        dsz = sg * (1.0 + z * (1.0 - sg))
        dm = dm_ref[...]
        for hh in range(2):
            cols = slice(hh * DH, (hh + 1) * DH)
            ov = o_ref[hh]
            r = lax.rsqrt(jnp.mean(ov * ov, axis=-1, keepdims=True) + EPS)
            xn = ov * r
            dmh = dm[:, cols]
            dn = dmh * sz[:, cols]
            dz_ref[:, cols] = dmh * (xn * w_ref[...]) * dsz[:, cols]
            dw_ref[...] += jnp.sum(dn * xn, axis=0, keepdims=True)
            g = dn * w_ref[...]
            do_ref[hh] = r * (g - xn * jnp.mean(g * xn, axis=-1, keepdims=True))

    return pl.pallas_call(
        body, name="gdn_post_bwd", grid=(WIDTH // LANE,),
        in_specs=[pl.BlockSpec((2, t, DH), lambda j: (j, 0, 0)), pl.BlockSpec((t, LANE), lambda j: (0, Z_COL0 + j)),
                  pl.BlockSpec((1, DH), lambda j: (0, 0)), pl.BlockSpec((t, LANE), lambda j: (0, j))],
        out_specs=[pl.BlockSpec((2, t, DH), lambda j: (j, 0, 0)), pl.BlockSpec((t, LANE), lambda j: (0, j)),
                   pl.BlockSpec((1, DH), lambda j: (0, 0))],
        out_shape=[jax.ShapeDtypeStruct((HEADS, t, DH), F32), jax.ShapeDtypeStruct((t, WIDTH), F32),
                   jax.ShapeDtypeStruct((1, DH), F32)],
        compiler_params=_params(("arbitrary",)),
    )(o, pm, onw, dmix)


def _fox_post(o, pm):
    t = pm.shape[0]

    def body(o_ref, g_ref, m_ref):
        m_ref[...] = (jnp.concatenate([o_ref[0], o_ref[1]], axis=1) * _sigmoid(g_ref[...])).astype(m_ref.dtype)

    return pl.pallas_call(
        body, name="fox_post", grid=(WIDTH // LANE,),
        in_specs=[pl.BlockSpec((2, t, DH), lambda j: (j, 0, 0)), pl.BlockSpec((t, LANE), lambda j: (0, FGATE_COL0 + j))],
        out_specs=pl.BlockSpec((t, LANE), lambda j: (0, j)),
        out_shape=jax.ShapeDtypeStruct((t, WIDTH), MXU),
        compiler_params=_params(("arbitrary",)),
    )(o, pm)


def _fox_post_bwd(o, pm, dmix):
    t = pm.shape[0]

    def body(o_ref, g_ref, dm_ref, do_ref, dg_ref):
        sg = _sigmoid(g_ref[...])
        dm = dm_ref[...]
        for hh in range(2):
            cols = slice(hh * DH, (hh + 1) * DH)
            do_ref[hh] = dm[:, cols] * sg[:, cols]
            dg_ref[:, cols] = dm[:, cols] * o_ref[hh] * (sg * (1.0 - sg))[:, cols]

    return pl.pallas_call(
        body, name="fox_post_bwd", grid=(WIDTH // LANE,),
        in_specs=[pl.BlockSpec((2, t, DH), lambda j: (j, 0, 0)), pl.BlockSpec((t, LANE), lambda j: (0, FGATE_COL0 + j)),
                  pl.BlockSpec((t, LANE), lambda j: (0, j))],
        out_specs=[pl.BlockSpec((2, t, DH), lambda j: (j, 0, 0)), pl.BlockSpec((t, LANE), lambda j: (0, j))],
        out_shape=[jax.ShapeDtypeStruct((HEADS, t, DH), F32), jax.ShapeDtypeStruct((t, WIDTH), F32)],
        compiler_params=_params(("arbitrary",)),
    )(o, pm, dmix)


def _tail(x, mixg, mixf, tgt, wo, n2w, wg_t, wu_t, wd, fw):
    t, d = x.shape
    dff = wd.shape[0]
    tb = min(TB, t)

    def body(x_ref, mg_ref, mf_ref, t_ref, wo_ref, n2_ref, wg_ref, wu_ref, wd_ref, fw_ref,
             h2_ref, act_ref, dgate_ref, dup_ref, dx3_ref, dx2_ref, dmg_ref, dmf_ref, dn2_ref, dfw_ref, loss_ref):
        @pl.when(pl.program_id(0) == 0)
        def _():
            dn2_ref[...] = jnp.zeros_like(dn2_ref)
            dfw_ref[...] = jnp.zeros_like(dfw_ref)
            loss_ref[...] = jnp.zeros_like(loss_ref)

        x2 = x_ref[...] + _dot(mg_ref[...], wo_ref[0:WIDTH, :]) + _dot(mf_ref[...], wo_ref[WIDTH:2 * WIDTH, :])
        r2 = lax.rsqrt(jnp.mean(x2 * x2, axis=-1, keepdims=True) + EPS)
        xn2 = x2 * r2
        h2 = (xn2 * n2_ref[...]).astype(MXU)
        h2_ref[...] = h2
        gate = _dot_nt(h2, wg_ref[...])
        up = _dot_nt(h2, wu_ref[...])
        sg = _sigmoid(gate)
        sl = gate * sg
        act = (sl * up).astype(MXU)
        act_ref[...] = act
        x3 = x2 + _dot(act, wd_ref[...])
        r3 = lax.rsqrt(jnp.mean(x3 * x3, axis=-1, keepdims=True) + EPS)
        xn3 = x3 * r3
        err = xn3 * fw_ref[...] - t_ref[...]
        loss_ref[...] += 0.5 * jnp.sum(jnp.mean(err * err, axis=-1, keepdims=True), axis=0, keepdims=True)
        dy = err * (1.0 / d)
        dfw_ref[...] += jnp.sum(dy * xn3, axis=0, keepdims=True)
        g3 = dy * fw_ref[...]
        dx3 = r3 * (g3 - xn3 * jnp.mean(g3 * xn3, axis=-1, keepdims=True))
        dx3_ref[...] = dx3.astype(MXU)
        dact = _dot_nt(dx3, wd_ref[...])
        dgate = (dact * up * (sg * (1.0 + gate * (1.0 - sg)))).astype(MXU)
        dup = (dact * sl).astype(MXU)
        dgate_ref[...] = dgate
        dup_ref[...] = dup
        dh2 = _dot(dgate, wg_ref[...]) + _dot(dup, wu_ref[...])
        dn2_ref[...] += jnp.sum(dh2 * xn2, axis=0, keepdims=True)
        g2 = dh2 * n2_ref[...]
        dx2 = dx3 + r2 * (g2 - xn2 * jnp.mean(g2 * xn2, axis=-1, keepdims=True))
        dx2_ref[...] = dx2
        dmg_ref[...] = _dot_nt(dx2, wo_ref[0:WIDTH, :])
        dmf_ref[...] = _dot_nt(dx2, wo_ref[WIDTH:2 * WIDTH, :])

    def tok(n):
        return pl.BlockSpec((tb, n), lambda i: (i, 0))

    acc = pl.BlockSpec((1, d), lambda i: (0, 0))
    sds = jax.ShapeDtypeStruct
    return pl.pallas_call(
        body, name="tail", grid=(t // tb,),
        in_specs=[tok(d), tok(WIDTH), tok(WIDTH), tok(d), _resident(wo.shape), _resident((1, d)),
                  _resident(wg_t.shape), _resident(wu_t.shape), _resident(wd.shape), _resident((1, d))],
        out_specs=[tok(d), tok(dff), tok(dff), tok(dff), tok(d), tok(d), tok(WIDTH), tok(WIDTH), acc, acc,
                   pl.BlockSpec((1, 1), lambda i: (0, 0))],
        out_shape=[sds((t, d), MXU), sds((t, dff), MXU), sds((t, dff), MXU), sds((t, dff), MXU), sds((t, d), MXU),
                   sds((t, d), F32), sds((t, WIDTH), F32), sds((t, WIDTH), F32), sds((1, d), F32), sds((1, d), F32),
                   sds((1, 1), F32)],
        compiler_params=_params(("arbitrary",)),
    )(x, mixg, mixf, tgt, wo, n2w, wg_t, wu_t, wd, fw)


def _wgrads(a_list, b, name):
    t, n = b.shape
    ms = [a.shape[1] for a in a_list]
    bms = [256 if m % 256 == 0 else LANE for m in ms]
    nbs = [m // bm for m, bm in zip(ms, bms)]
    k = len(a_list)
    cast = b.dtype != jnp.dtype(MXU)

    def body(*refs):
        a_refs, b_ref, o_refs = refs[:k], refs[k], refs[k + 1:2 * k + 1]
        i = pl.program_id(0)
        if cast:
            @pl.when(i == 0)
            def _():
                refs[-1][...] = b_ref[...].astype(MXU)
        for a_ref, o_ref, nb in zip(a_refs, o_refs, nbs):
            @pl.when(i < nb)
            def _():
                o_ref[...] = _dot_tn(a_ref[...], refs[-1][...] if cast else b_ref[...]).astype(o_ref.dtype)

    def clamp(nb):
        return lambda i: jnp.minimum(i, nb - 1)

    return pl.pallas_call(
        body, name=name, grid=(max(nbs),),
        in_specs=[pl.BlockSpec((t, bm), lambda i, c=clamp(nb): (0, c(i))) for bm, nb in zip(bms, nbs)] + [_resident((t, n))],
        out_specs=[pl.BlockSpec((bm, n), lambda i, c=clamp(nb): (c(i), 0)) for bm, nb in zip(bms, nbs)],
        out_shape=[jax.ShapeDtypeStruct((m, n), WIRE) for m in ms],
        scratch_shapes=[pltpu.VMEM((t, n), MXU)] if cast else [],
        compiler_params=_params(("arbitrary",)),
    )(*a_list, b)


def _merge_dw_in(d_gdn, d_z, d_fox, d_fg, d_small, shards=False):
    pieces = [d_gdn, d_z, d_small[:2 * HEADS], d_fox, d_fg, d_small[2 * HEADS:3 * HEADS]]
    if not shards:
        return jnp.concatenate(pieces, axis=0)
    n = sum(p.shape[0] for p in pieces) // N_DEV
    out = []
    for dev in range(N_DEV):
        parts, first = [], 0
        for p in pieces:
            lo, hi = max(dev * n, first), min((dev + 1) * n, first + p.shape[0])
            if lo < hi:
                parts.append(p[lo - first:hi - first])
            first += p.shape[0]
        out.append(jnp.concatenate(parts, axis=0))
    return jnp.stack(out)


def _lanes(*pieces):
    v = jnp.concatenate([p.reshape(-1).astype(F32) for p in pieces])
    return jnp.pad(v, (0, LANE - v.shape[0])).reshape(1, LANE)


def _vector_params(p):
    d = p["norm1_w"].size
    gparams = jnp.concatenate([_lanes(jnp.zeros(HEADS), p["gdn_dt_bias"], p["fox_f_bias"]),
                               _lanes(jnp.zeros(HEADS), p["gdn_A_log"]), jnp.zeros((6, LANE), F32)])
    fox_nw = jnp.stack([jnp.tile(p["fox_q_norm_w"].reshape(-1), 2), jnp.tile(p["fox_k_norm_w"].reshape(-1), 2),
                        jnp.ones((LANE,), F32)])
    return dict(n1w=p["norm1_w"].reshape(1, d), n2w=p["norm2_w"].reshape(1, d), fw=p["final_norm_w"].reshape(1, d),
                onw=p["gdn_out_norm_w"].reshape(1, DH), gparams=gparams, fox_nw=fox_nw)


def _mixer_forward(x, vp, w_t, ws_t, conv_w, carry=None):
    h1, pm, ps = _inproj(x, vp["n1w"], w_t, ws_t)
    gates, run, tot, run_t = _gates(ps, vp["gparams"])
    fqkv = _fox_prep(pm, vp["fox_nw"])
    o_fox, lse, *carried = _fox_fwd(fqkv, run, run_t, carry)
    if carry:
        pm, o_fox = lax.optimization_barrier((pm, o_fox))
    mixf = _fox_post(o_fox, pm)
    gqkv = _gdn_prep(pm, conv_w)
    o_gdn, states, inv = _gdn_fwd(gqkv, gates, run, tot, run_t)
    mixg = _gdn_post(o_gdn, pm, vp["onw"])
    return dict(h1=h1, pm=pm, ps=ps, gates=gates, run=run, tot=tot, run_t=run_t, gqkv=gqkv, o_gdn=o_gdn, states=states,
                inv=inv, mixg=mixg, fqkv=fqkv, o_fox=o_fox, lse=lse, mixf=mixf, carried=carried[0] if carried else None)


def _mixer_backward(x, vp, w_t, ws_t, conv_w, f, dx2, dmixg, dmixf, carry=None, scatter_own=False):
    pm = f["pm"]
    do_fox, dfg = _fox_post_bwd(f["o_fox"], pm, dmixf)
    dfqkv, dcol_f, drow_f, *carried = _fox_bwd(f["fqkv"], f["run"], f["run_t"], f["o_fox"], f["lse"], do_fox, carry)
    dfox, dfnw = _fox_prep_bwd(pm, vp["fox_nw"], dfqkv)
    do_gdn, dz, donw = _gdn_post_bwd(f["o_gdn"], pm, vp["onw"], dmixg)
    dgqkv, dcol_g, dtot_g, drow_g = _gdn_bwd(f["gqkv"], f["gates"], f["run"], f["tot"], f["run_t"], f["inv"], f["states"], do_gdn)
    dgdn, dconv = _gdn_prep_bwd(pm, conv_w, dgqkv)
    dps, gsum = _gates_bwd(f["ps"], vp["gparams"], dcol_g, dtot_g, dcol_f, drow_g, drow_f)
    dw_pieces = _wgrads([dgdn, dz, dfox, dfg, dps], f["h1"], "dw_in")
    dw_in = _merge_dw_in(*dw_pieces)
    own = ("scatter", [_merge_dw_in(*dw_pieces, shards=True), _cut(dconv, 1)]) if scatter_own else None
    grad_x, dn1w, *sent = _inproj_bwd(x, vp["n1w"], dx2, dgdn, dz, dfox, dfg, dps, w_t, ws_t, own)
    small = dict(dn1w=dn1w, gsum=gsum, donw=donw, dfnw=dfnw)
    return (grad_x, dw_in, dconv, small, *carried, *sent)


VECTORS = ("norm1_w", "norm2_w", "final_norm_w", "gdn_A_log", "gdn_dt_bias", "gdn_out_norm_w", "fox_f_bias",
           "fox_q_norm_w", "fox_k_norm_w")
VEC_ROWS = 16
LOSS_ROW = len(VECTORS)


def _pack_vectors(dn1w, dn2w, dfw, gsum, donw, dfnw, loss):
    d = dn1w.shape[1]

    def body(n1_ref, n2_ref, fw_ref, gs_ref, on_ref, fn_ref, loss_ref, o_ref):
        o_ref[...] = jnp.zeros_like(o_ref)
        o_ref[0:1, :] = n1_ref[...]
        o_ref[1:2, :] = n2_ref[...]
        o_ref[2:3, :] = fw_ref[...]
        o_ref[3:4, 0:HEADS] = gs_ref[0:1, 0:HEADS]
        o_ref[4:5, 0:HEADS] = gs_ref[1:2, 0:HEADS]
        o_ref[5:6, 0:DH] = on_ref[...]
        o_ref[6:7, 0:HEADS] = gs_ref[2:3, 0:HEADS]
        for kind in range(2):
            v = fn_ref[kind, 0]
            for j in range(1, fn_ref.shape[1]):
                v = v + fn_ref[kind, j]
            o_ref[7 + kind:8 + kind, 0:DH] = v[:, :DH] + v[:, DH:]
        o_ref[LOSS_ROW:LOSS_ROW + 1, 0:1] = loss_ref[...]

    return pl.pallas_call(body, name="pack_vectors", out_shape=jax.ShapeDtypeStruct((VEC_ROWS, d), F32),
                          compiler_params=_params())(dn1w, dn2w, dfw, gsum, donw, dfnw, loss)


def _late_grads(f, h2, act, dgate, dup, dx3, dx2):
    dw_gate, dw_up = _wgrads([dgate, dup], h2, "dw_gate_up")
    (dw_down,) = _wgrads([act], dx3, "dw_down")
    return {"w_out": jnp.concatenate(_wgrads([f["mixg"], f["mixf"]], dx2, "dw_out"), axis=0),
            "w_ffn_gate": dw_gate, "w_ffn_up": dw_up, "w_ffn_down": dw_down}


def _local_step(x, tgt, p, w_in_t, conv_w, wo, wg_t, wu_t, wd):
    vp = _vector_params(p)
    ws_t = _small_rows(w_in_t)
    f = _mixer_forward(x, vp, w_in_t, ws_t, conv_w)
    (h2, act, dgate, dup, dx3, dx2, dmixg, dmixf, dn2w, dfw, loss) = _tail(
        x, f["mixg"], f["mixf"], tgt, wo, vp["n2w"], wg_t, wu_t, wd, vp["fw"])
    grad_x, dw_in, dconv, small = _mixer_backward(x, vp, w_in_t, ws_t, conv_w, f, dx2, dmixg, dmixf)
    grads = {"w_in": dw_in, "gdn_conv_w": dconv, **_late_grads(f, h2, act, dgate, dup, dx3, dx2)}
    vec = _pack_vectors(small["dn1w"], dn2w, dfw, small["gsum"], small["donw"], small["dfnw"], loss)
    return grad_x[0], grads, vec


def _my_place():
    return lax.axis_index("x"), lax.axis_index("y"), lax.axis_index("c")


def _peers():
    x, y, c = _my_place()
    peers = []
    for k in range(1, N_DEV):
        px = 1 - x if k & 4 else x
        py = 1 - y if k & 2 else y
        pc = 1 - c if k & 1 else c
        peers.append(((px, py, pc), 4 * px + 2 * py + pc))
    return 4 * x + 2 * y + c, peers


def _spread_copies(kind, srcs, lands, send_sems, recv_sems):
    me, peers = _peers()
    kinds = [kind] * len(srcs) if isinstance(kind, str) else kind
    remote, local = [], []
    for i, (kd, src, land) in enumerate(zip(kinds, srcs, lands)):
        for k, (dev, idx) in enumerate(peers):
            remote.append(pltpu.make_async_remote_copy(
                src_ref=src if kd == "gather" else src.at[idx], dst_ref=land.at[me],
                send_sem=send_sems.at[i * (N_DEV - 1) + k], recv_sem=recv_sems.at[i * (N_DEV - 1) + k],
                device_id=dev, device_id_type=MESH))
        local.append((src if kd == "gather" else src.at[me], land.at[me]))
    return remote, local


def _gather_two_level(arrays, name):
    n = len(arrays)

    def body(*refs):
        srcs, lands = refs[:n], refs[n:2 * n]
        send_sems, recv_sems, local_sems = refs[2 * n:]
        x, y, c = _my_place()
        me, sibling = (x, y, c), (x, y, 1 - c)
        chips = [(1 - x, y), (x, 1 - y), (1 - x, 1 - y)]

        def index(p):
            return 4 * p[0] + 2 * p[1] + p[2]

        def copy(i, k, block, to, src=None):
            blk = lands[i].at[index(block)]
            return pltpu.make_async_remote_copy(
                src_ref=blk if src is None else src, dst_ref=blk, send_sem=send_sems.at[7 * i + k],
                recv_sem=recv_sems.at[7 * i + k], device_id=to, device_id_type=MESH)

        mine = [pltpu.make_async_copy(srcs[i], lands[i].at[index(me)], local_sems.at[i]) for i in range(n)]
        for cp in mine:
            cp.start()
        first = []
        for i in range(n):
            first.append(copy(i, 0, me, sibling, src=srcs[i]))
            first += [copy(i, 1 + j, me, (*chip, c), src=srcs[i]) for j, chip in enumerate(chips)]
        for cp in first:
            cp.start()
        passed = []
        for i in range(n):
            for j, chip in enumerate(chips):
                copy(i, 1 + j, (*chip, c), me).wait_recv()
                passed.append(copy(i, 4 + j, (*chip, c), sibling))
                passed[-1].start()
        for i in range(n):
            copy(i, 0, sibling, me).wait_recv()
            for j, chip in enumerate(chips):
                copy(i, 4 + j, (*chip, 1 - c), me).wait_recv()
        for cp in first + passed:
            cp.wait_send()
        for cp in mine:
            cp.wait()

    return pl.pallas_call(
        body, name=name,
        out_shape=[jax.ShapeDtypeStruct((N_DEV,) + a.shape, a.dtype) for a in arrays],
        in_specs=[pl.BlockSpec(memory_space=pl.ANY)] * n, out_specs=[pl.BlockSpec(memory_space=pl.ANY)] * n,
        scratch_shapes=[pltpu.SemaphoreType.DMA((7 * n,)), pltpu.SemaphoreType.DMA((7 * n,)),
                        pltpu.SemaphoreType.DMA((n,))],
    )(*arrays)


def _land_shape(kind, a):
    return (N_DEV,) + a.shape if kind == "gather" else a.shape


HBM = pl.BlockSpec(memory_space=pltpu.HBM)
SEM = pl.BlockSpec(memory_space=pltpu.SEMAPHORE)


def _hbm(a):
    return pltpu.with_memory_space_constraint(a, pltpu.HBM)


def _carry_operands(kind, arrays):
    n = len(arrays)
    kinds = [kind] * n if isinstance(kind, str) else kind
    lands = [lax.empty(_land_shape(kd, a), a.dtype) for kd, a in zip(kinds, arrays)]
    sems = [pltpu.SemaphoreType.DMA((n * (N_DEV - 1),))] * 2
    return ([_hbm(a) for a in list(arrays) + lands], [HBM] * (2 * n),
            sems + [pltpu.HBM(a.shape, a.dtype) for a in list(arrays) + lands], [SEM] * 2 + [HBM] * (2 * n),
            pltpu.SemaphoreType.DMA((n,)))


def _carry_step(kind, in_refs, send_sems, recv_sems, local_sems, step, n_steps):
    n = len(in_refs) // 2
    remote, local = _spread_copies(kind, in_refs[:n], in_refs[n:], send_sems, recv_sems)
    copies = [pltpu.make_async_copy(s, d, local_sems.at[i]) for i, (s, d) in enumerate(local)]
    per_step = -(-len(remote) // n_steps)
    for s in range(-(-len(remote) // per_step)):
        @pl.when(step == s)
        def _():
            for cp in remote[s * per_step:(s + 1) * per_step]:
                cp.start()
            if s == 0:
                for cp in copies:
                    cp.start()

    @pl.when(step == n_steps - 1)
    def _():
        for cp in copies:
            cp.wait()


def _carry_state(extra_out):
    n = (len(extra_out) - 2) // 2
    return list(extra_out[2:2 + n]), list(extra_out[2 + n:]), extra_out[0], extra_out[1]


def _spread_start(arrays, kind, name):
    n = len(arrays)

    def body(*refs):
        srcs, lands = refs[:n], refs[n:2 * n]
        send_sems, recv_sems = refs[2 * n], refs[2 * n + 1]
        token = refs[4 * n + 2]
        local_sems = refs[4 * n + 3]
        remote, local = _spread_copies(kind, srcs, lands, send_sems, recv_sems)
        for cp in remote:
            cp.start()
        copies = [pltpu.make_async_copy(s, d, local_sems.at[i]) for i, (s, d) in enumerate(local)]
        for cp in copies:
            cp.start()
        for cp in copies:
            cp.wait()
        token[...] = jnp.zeros_like(token)

    sems = (pltpu.SemaphoreType.DMA((n * (N_DEV - 1),)),) * 2
    kinds = [kind] * n if isinstance(kind, str) else kind
    lands = [lax.empty(_land_shape(kd, a), a.dtype) for kd, a in zip(kinds, arrays)]
    out = pl.pallas_call(
        body, name=name,
        out_shape=sems + tuple(pltpu.HBM(a.shape, a.dtype) for a in list(arrays) + lands)
        + (jax.ShapeDtypeStruct((8, LANE), F32),),
        in_specs=[HBM] * (2 * n), out_specs=tuple([SEM] * 2 + [HBM] * (2 * n) + [pl.BlockSpec(memory_space=pltpu.VMEM)]),
        input_output_aliases={j: 2 + j for j in range(2 * n)},
        scratch_shapes=[pltpu.SemaphoreType.DMA((n,))],
        compiler_params=pltpu.CompilerParams(has_side_effects=pltpu.SideEffectType.DATAFLOW_SIDE_EFFECTING),
    )(*[_hbm(a) for a in arrays], *[_hbm(a) for a in lands])
    return (list(out[2:2 + n]), list(out[2 + n:2 + 2 * n]), out[0], out[1]), out[-1]


def _spread_wait(state, kind, after, name):
    srcs, lands, send_sems, recv_sems = state
    n = len(srcs)
    after = list(after) if isinstance(after, (list, tuple)) else [after]

    def body(*refs):
        remote, _ = _spread_copies(kind, refs[:n], refs[n:2 * n], refs[2 * n], refs[2 * n + 1])
        for cp in remote:
            cp.wait_send()
        for cp in remote:
            cp.wait_recv()

    out = pl.pallas_call(
        body, name=name,
        out_shape=tuple(pltpu.HBM(a.shape, a.dtype) for a in srcs + lands),
        in_specs=[HBM] * (2 * n) + [SEM, SEM] + [pl.BlockSpec(memory_space=pl.ANY)] * len(after),
        out_specs=tuple([HBM] * (2 * n)),
        input_output_aliases={j: j for j in range(2 * n)},
        compiler_params=pltpu.CompilerParams(has_side_effects=pltpu.SideEffectType.DATAFLOW_SIDE_EFFECTING),
    )(*srcs, *lands, send_sems, recv_sems, *after)
    return list(out[n:])


ADAM_ROWS = 128
ADAM_COLS = 256


def _adam_math(g, w, m, v):
    nm = ADAM_B1 * m + (1.0 - ADAM_B1) * g
    nv = ADAM_B2 * v + (1.0 - ADAM_B2) * (g * g)
    m_hat = nm / (1.0 - ADAM_B1 ** ADAM_STEP)
    v_hat = nv / (1.0 - ADAM_B2 ** ADAM_STEP)
    return -ADAM_LR * (m_hat / (jnp.sqrt(v_hat) + ADAM_EPS) + ADAM_WD * w), nm, nv


def _sum_parts(p_ref):
    g = p_ref[0].astype(F32)
    for s in range(1, N_DEV):
        g = g + p_ref[s].astype(F32)
    return g


def _adam_matrix(parts, w, m, v, name):
    _, r, c = w.shape
    rb = ADAM_ROWS if r % ADAM_ROWS == 0 else r
    cb = ADAM_COLS if (rb == r and c % ADAM_COLS == 0) else c

    def body(p_ref, w_ref, m_ref, v_ref, g_ref, d_ref, nm_ref, nv_ref):
        g = _sum_parts(p_ref)
        g_ref[0] = g
        d_ref[0], nm_ref[0], nv_ref[0] = _adam_math(g, w_ref[0], m_ref[0], v_ref[0])

    blk = pl.BlockSpec((1, rb, cb), lambda i, j: (0, i, j))
    return pl.pallas_call(
        body, name=name, grid=(r // rb, c // cb),
        in_specs=[pl.BlockSpec((N_DEV, rb, cb), lambda i, j: (0, i, j)), blk, blk, blk],
        out_specs=[blk] * 4, out_shape=[jax.ShapeDtypeStruct(w.shape, F32)] * 4,
        compiler_params=_params(("arbitrary", "arbitrary")),
    )(parts, w, m, v)


def _copy_rows(a, name):
    _, r, c = a.shape
    rb = min(TB, r)

    def body(a_ref, o_ref):
        o_ref[...] = a_ref[...]

    blk = pl.BlockSpec((1, rb, c), lambda i: (0, i, 0))
    return pl.pallas_call(body, name=name, grid=(r // rb,), in_specs=[blk], out_specs=blk,
                          out_shape=jax.ShapeDtypeStruct(a.shape, a.dtype), compiler_params=_params(("arbitrary",)))(a)


def _adam_vectors(parts, ws, ms, vs):
    nv = len(ws)

    def body(*refs):
        p_ref = refs[0]
        w_refs, m_refs, v_refs = refs[1:1 + nv], refs[1 + nv:1 + 2 * nv], refs[1 + 2 * nv:1 + 3 * nv]
        outs = refs[1 + 3 * nv:]
        g_all = _sum_parts(p_ref)
        for i in range(nv):
            n = w_refs[i].shape[1]
            g = g_all[i:i + 1, 0:n]
            d, nm, nvv = _adam_math(g, w_refs[i][...], m_refs[i][...], v_refs[i][...])
            outs[i][...] = g
            outs[nv + i][...] = d
            outs[2 * nv + i][...] = nm
            outs[3 * nv + i][...] = nvv
        outs[4 * nv][...] = g_all[LOSS_ROW:LOSS_ROW + 1, 0:1]

    shapes = [jax.ShapeDtypeStruct(a.shape, F32) for a in ws]
    out = pl.pallas_call(body, name="adam_vectors", out_shape=shapes * 4 + [jax.ShapeDtypeStruct((1, 1), F32)],
                         compiler_params=_params())(parts, *ws, *ms, *vs)
    return out[:nv], out[nv:2 * nv], out[2 * nv:3 * nv], out[3 * nv:4 * nv], out[4 * nv]


MATRICES = (("w_in", 1), ("gdn_conv_w", 1), ("w_out", 0), ("w_ffn_gate", 1), ("w_ffn_up", 1), ("w_ffn_down", 0))
TRANSPOSED = ("w_in", "w_ffn_gate", "w_ffn_up")
WEIGHTS = ("norm1_w", "w_in", "gdn_conv_w", "gdn_A_log", "gdn_dt_bias", "gdn_out_norm_w", "fox_f_bias", "fox_q_norm_w",
           "fox_k_norm_w", "w_out", "norm2_w", "w_ffn_gate", "w_ffn_up", "w_ffn_down", "final_norm_w")


def _join(blocks, axis):
    _, r, c = blocks.shape
    if axis == 0:
        return blocks.reshape(N_DEV * r, c)
    return blocks.transpose(1, 0, 2).reshape(r, N_DEV * c)


def _cut(full, axis):
    r, c = full.shape
    if axis == 0:
        return full.reshape(N_DEV, r // N_DEV, c)
    return full.reshape(r, N_DEV, c // N_DEV).transpose(1, 0, 2)


def kernel(x, norm1_w, w_in, gdn_conv_w, gdn_A_log, gdn_dt_bias, gdn_out_norm_w, fox_f_bias, fox_q_norm_w, fox_k_norm_w, w_out, norm2_w, w_ffn_gate, w_ffn_up, w_ffn_down, final_norm_w, loss_target, m_norm1_w, m_w_in, m_gdn_conv_w, m_gdn_A_log, m_gdn_dt_bias, m_gdn_out_norm_w, m_fox_f_bias, m_fox_q_norm_w, m_fox_k_norm_w, m_w_out, m_norm2_w, m_w_ffn_gate, m_w_ffn_up, m_w_ffn_down, m_final_norm_w, v_norm1_w, v_w_in, v_gdn_conv_w, v_gdn_A_log, v_gdn_dt_bias, v_gdn_out_norm_w, v_fox_f_bias, v_fox_q_norm_w, v_fox_k_norm_w, v_w_out, v_norm2_w, v_w_ffn_gate, v_w_ffn_up, v_w_ffn_down, v_final_norm_w):
    w = dict(norm1_w=norm1_w, w_in=w_in, gdn_conv_w=gdn_conv_w, gdn_A_log=gdn_A_log, gdn_dt_bias=gdn_dt_bias,
             gdn_out_norm_w=gdn_out_norm_w, fox_f_bias=fox_f_bias, fox_q_norm_w=fox_q_norm_w, fox_k_norm_w=fox_k_norm_w,
             w_out=w_out, norm2_w=norm2_w, w_ffn_gate=w_ffn_gate, w_ffn_up=w_ffn_up, w_ffn_down=w_ffn_down,
             final_norm_w=final_norm_w)
    m = dict(norm1_w=m_norm1_w, w_in=m_w_in, gdn_conv_w=m_gdn_conv_w, gdn_A_log=m_gdn_A_log, gdn_dt_bias=m_gdn_dt_bias,
             gdn_out_norm_w=m_gdn_out_norm_w, fox_f_bias=m_fox_f_bias, fox_q_norm_w=m_fox_q_norm_w,
             fox_k_norm_w=m_fox_k_norm_w, w_out=m_w_out, norm2_w=m_norm2_w, w_ffn_gate=m_w_ffn_gate,
             w_ffn_up=m_w_ffn_up, w_ffn_down=m_w_ffn_down, final_norm_w=m_final_norm_w)
    v = dict(norm1_w=v_norm1_w, w_in=v_w_in, gdn_conv_w=v_gdn_conv_w, gdn_A_log=v_gdn_A_log, gdn_dt_bias=v_gdn_dt_bias,
             gdn_out_norm_w=v_gdn_out_norm_w, fox_f_bias=v_fox_f_bias, fox_q_norm_w=v_fox_q_norm_w,
             fox_k_norm_w=v_fox_k_norm_w, w_out=v_w_out, norm2_w=v_norm2_w, w_ffn_gate=v_w_ffn_gate,
             w_ffn_up=v_w_ffn_up, w_ffn_down=v_w_ffn_down, final_norm_w=v_final_norm_w)
    late = ("w_out", "w_ffn_gate", "w_ffn_up", "w_ffn_down")
    xs, tgt = x[0], loss_target[0]
    vp = _vector_params({n: w[n] for n in VECTORS})

    def rows_of(d, n):
        return d[n].transpose(0, 2, 1) if n in TRANSPOSED else d[n]

    wr, mr, vr = ({n: rows_of(d, n) for n, _ in MATRICES} for d in (w, m, v))

    w_in_blocks, conv_blocks = _gather_two_level([wr["w_in"][0].astype(WIRE), w["gdn_conv_w"][0]], "gather_in")
    w_t = _join(w_in_blocks, 0)
    conv_w = _join(conv_blocks, 1)
    f = _mixer_forward(xs, vp, w_t, _small_rows(w_t), conv_w, ("gather", [wr[n][0].astype(WIRE) for n in late]))
    full = {n: _join(b, 0) for n, b in zip(late, _spread_wait(f["carried"], "gather", f["mixg"], "gather_late_wait"))}

    (h2, act, dgate, dup, dx3, dx2, dmixg, dmixf, dn2w, dfw, loss) = _tail(
        xs, f["mixg"], f["mixf"], tgt, full["w_out"], vp["n2w"], full["w_ffn_gate"], full["w_ffn_up"],
        full["w_ffn_down"], vp["fw"])
    dlate = _late_grads(f, h2, act, dgate, dup, dx3, dx2)

    grad_x, dw_in, dconv, small, state_grads, state_own = _mixer_backward(
        xs, vp, w_t, _small_rows(w_t), conv_w, f, dx2, dmixg, dmixf, ("scatter", [_cut(dlate[n], 0) for n in late]),
        scatter_own=True)
    vec = _pack_vectors(small["dn1w"], dn2w, dfw, small["gsum"], small["donw"], small["dfnw"], loss)

    state_vec, token = _spread_start([vec], "gather", "vectors_start")
    parts = dict(zip(late, _spread_wait(state_grads, "scatter", token, "grads_late_wait")))
    results = [{}, {}, {}, {}]

    updated = {}

    def update(n):
        out = _adam_matrix(parts[n], wr[n], mr[n], vr[n], "adam_" + n)
        updated[n] = out[0]
        for d, a in zip(results, out):
            d[n] = a.transpose(0, 2, 1) if n in TRANSPOSED else a

    for n in late:
        update(n)
    grad_x = _copy_rows(grad_x, "grad_x_result")
    ready = [updated[n] for n in late] + [d["w_in"] for d in (wr, mr, vr)] + [grad_x]
    (parts_vec,) = _spread_wait(state_vec, "gather", ready, "vectors_wait")
    row = lambda a: a.reshape(1, -1)
    *vec_out, total_loss = _adam_vectors(parts_vec, [row(w[n]) for n in VECTORS], [row(m[n]) for n in VECTORS],
                                         [row(v[n]) for n in VECTORS])
    parts["w_in"], parts["gdn_conv_w"] = _spread_wait(state_own, "scatter", parts_vec, "own_wait")
    update("w_in")
    update("gdn_conv_w")
    for d, arrs in zip(results, vec_out):
        for n, a in zip(VECTORS, arrs):
            d[n] = a.reshape(w[n].shape)
    return (total_loss[0, 0], grad_x, *[d[n] for d in results for n in WEIGHTS])
```

```python
import functools

import jax
import jax.numpy as jnp
from jax import lax
from jax.experimental import pallas as pl
from jax.experimental.pallas import tpu as pltpu

F32 = jnp.float32
MXU = jnp.bfloat16
WIRE = jnp.bfloat16
HI = lax.Precision.HIGHEST
EPS = 1e-6

N_DEV = 8
HEADS = 8
DH = 64
WIDTH = HEADS * DH
CHUNK = 64
LANE = 128
ROW_ALIGN = 16
TB = 256
QB = 256
VMEM_LIMIT = 60 * 1024 * 1024

ADAM_LR = 0.001
ADAM_B1 = 0.9
ADAM_B2 = 0.999
ADAM_EPS = 1e-08
ADAM_WD = 0.01
ADAM_STEP = 10

MESH = pl.DeviceIdType.MESH


def _params(sem=None):
    return pltpu.CompilerParams(dimension_semantics=sem, vmem_limit_bytes=VMEM_LIMIT)


def _resident(shape):
    n = len(shape)
    return pl.BlockSpec(shape, lambda *_: (0,) * n, pipeline_mode=pl.Buffered(1))


def _dot(a, b):
    return jnp.dot(a.astype(MXU), b.astype(MXU), preferred_element_type=F32)


def _dot_nt(a, b):
    return lax.dot_general(a.astype(MXU), b.astype(MXU), (((1,), (1,)), ((), ())), preferred_element_type=F32)


def _dot_tn(a, b):
    return lax.dot_general(a.astype(MXU), b.astype(MXU), (((0,), (0,)), ((), ())), preferred_element_type=F32)


def _hdot(a, b):
    return jnp.dot(a, b, precision=HI, preferred_element_type=F32)


def _hdot_nt(a, b):
    return lax.dot_general(a, b, (((1,), (1,)), ((), ())), precision=HI, preferred_element_type=F32)


def _hdot_tn(a, b):
    return lax.dot_general(a, b, (((0,), (0,)), ((), ())), precision=HI, preferred_element_type=F32)


def _sigmoid(x):
    return 0.5 * jnp.tanh(0.5 * x) + 0.5


def _softplus(x):
    return jnp.maximum(x, 0.0) + jnp.log(1.0 + jnp.exp(-jnp.abs(x)))


def _head_sum_matrix():
    ri = lax.broadcasted_iota(jnp.int32, (LANE, LANE), 0) // DH
    ci = lax.broadcasted_iota(jnp.int32, (LANE, LANE), 1) // DH
    return (ri == ci).astype(F32)


def _group_sum(a, ones_matrix):
    hi = a.astype(jnp.bfloat16)
    lo = (a - hi.astype(F32)).astype(jnp.bfloat16)
    m = ones_matrix.astype(jnp.bfloat16)
    return jnp.dot(hi, m, preferred_element_type=F32) + jnp.dot(lo, m, preferred_element_type=F32)


def _shift_down(x, s):
    return pltpu.roll(x, s, 0)


def _shift_up(x, s):
    return pltpu.roll(x, x.shape[0] - s, 0)


ROWS_A = 4 * WIDTH
ROWS_B = ROWS_A + 2 * HEADS
ROWS_C = ROWS_B + 4 * WIDTH


def _small_rows(w_t):
    return jnp.concatenate([w_t[ROWS_A:ROWS_B], w_t[ROWS_C:], jnp.zeros((LANE - 3 * HEADS, w_t.shape[1]), w_t.dtype)])


def _inproj(x, n1w, w_t, ws_t):
    t, d = x.shape
    tb = min(TB, t)

    def body(x_ref, nw_ref, wt_ref, ws_ref, h_ref, pm_ref, ps_ref):
        xv = x_ref[...]
        r = lax.rsqrt(jnp.mean(xv * xv, axis=-1, keepdims=True) + EPS)
        h = (xv * r * nw_ref[...]).astype(MXU)
        h_ref[...] = h
        pm_ref[:, 0:ROWS_A] = _dot_nt(h, wt_ref[0:ROWS_A, :])
        pm_ref[:, ROWS_A:2 * ROWS_A] = _dot_nt(h, wt_ref[ROWS_B:ROWS_C, :])
        ps_ref[...] = _dot_nt(h, ws_ref[...])

    return pl.pallas_call(
        body, name="inproj", grid=(t // tb,),
        in_specs=[pl.BlockSpec((tb, d), lambda i: (i, 0)), _resident((1, d)), _resident(w_t.shape), _resident((LANE, d))],
        out_specs=[pl.BlockSpec((tb, d), lambda i: (i, 0)), pl.BlockSpec((tb, 2 * ROWS_A), lambda i: (i, 0)),
                   pl.BlockSpec((tb, LANE), lambda i: (i, 0))],
        out_shape=[jax.ShapeDtypeStruct((t, d), MXU), jax.ShapeDtypeStruct((t, 2 * ROWS_A), F32),
                   jax.ShapeDtypeStruct((t, LANE), F32)],
        compiler_params=_params(("arbitrary",)),
    )(x, n1w, w_t, ws_t)


def _inproj_bwd(x, n1w, dx2, dgdn, dz, dfox, dfg, dps, w_t, ws_t, carry=None):
    t, d = x.shape
    tb = min(TB, t)
    w3 = 3 * WIDTH
    c_in, c_in_specs, c_out_shape, c_out_specs, c_sem = _carry_operands(*carry) if carry else ([], [], [], [], None)
    n_c = len(c_in)

    def body(*refs):
        x_ref, nw_ref, dx2_ref, dgdn_ref, dz_ref, dfox_ref, dfg_ref, dps_ref, wm_ref, ws_ref = refs[:10]
        gx_ref, dnw_ref = refs[10 + n_c:12 + n_c]
        if carry:
            _carry_step(carry[0], refs[10:10 + n_c], refs[12 + n_c], refs[13 + n_c], refs[-1], pl.program_id(0), t // tb)
        dh = _dot(dgdn_ref[...], wm_ref[0:w3, :])
        dh += _dot(dz_ref[...], wm_ref[w3:ROWS_A, :])
        dh += _dot(dfox_ref[...], wm_ref[ROWS_B:ROWS_B + w3, :])
        dh += _dot(dfg_ref[...], wm_ref[ROWS_B + w3:ROWS_C, :])
        dh += _dot(dps_ref[...], ws_ref[...])
        xv = x_ref[...]
        r = lax.rsqrt(jnp.mean(xv * xv, axis=-1, keepdims=True) + EPS)
        xn = xv * r

        @pl.when(pl.program_id(0) == 0)
        def _():
            dnw_ref[...] = jnp.zeros_like(dnw_ref)

        dnw_ref[...] += jnp.sum(dh * xn, axis=0, keepdims=True)
        g = dh * nw_ref[...]
        gx_ref[0] = dx2_ref[...] + r * (g - xn * jnp.mean(g * xn, axis=-1, keepdims=True))

    def tok(n):
        return pl.BlockSpec((tb, n), lambda i: (i, 0))

    out = pl.pallas_call(
        body, name="inproj_bwd", grid=(t // tb,),
        in_specs=[tok(d), _resident((1, d)), tok(d), tok(w3), tok(WIDTH), tok(w3), tok(WIDTH), tok(LANE),
                  _resident(w_t.shape), _resident(ws_t.shape)] + c_in_specs,
        out_specs=[pl.BlockSpec((1, tb, d), lambda i: (0, i, 0)), pl.BlockSpec((1, d), lambda i: (0, 0))] + c_out_specs,
        out_shape=[jax.ShapeDtypeStruct((1, t, d), F32), jax.ShapeDtypeStruct((1, d), F32)] + c_out_shape,
        input_output_aliases={10 + j: 4 + j for j in range(n_c)},
        scratch_shapes=[c_sem] if carry else [],
        compiler_params=_params(("arbitrary",)),
    )(x, n1w, dx2, dgdn, dz, dfox, dfg, dps, w_t, ws_t, *c_in)
    return (out[0], out[1], _carry_state(out[2:])) if carry else tuple(out)


def _gate_lanes(shape):
    lane = lax.broadcasted_iota(jnp.int32, shape, 1)
    return lane < HEADS, (lane >= HEADS) & (lane < 2 * HEADS), (lane >= 2 * HEADS) & (lane < 3 * HEADS)


def _block_masks():
    ri = lax.broadcasted_iota(jnp.int32, (LANE, LANE), 0)
    ci = lax.broadcasted_iota(jnp.int32, (LANE, LANE), 1)
    same = (ri // CHUNK) == (ci // CHUNK)
    return ((ri >= ci).astype(F32), (ri <= ci).astype(F32), (same & (ri >= ci)).astype(F32),
            (same & (ri <= ci)).astype(F32), same.astype(F32))


def _gates(ps, gparams):
    t = ps.shape[0]
    nb = t // LANE

    def body(ps_ref, gp_ref, out_ref, run_ref, tot_ref, runt_ref):
        p = ps_ref[...]
        is_b, is_a, is_f = _gate_lanes(p.shape)
        z = p + gp_ref[0:1, :]
        neg_exp_a = -jnp.exp(gp_ref[1:2, :])
        glog = neg_exp_a * _softplus(z)
        logf = -_softplus(-z)
        out_ref[...] = jnp.where(is_b, _sigmoid(p), jnp.where(is_a, glog, jnp.where(is_f, logf, 0.0)))
        tril, _, tril_c, _, same_c = _block_masks()
        off = jnp.zeros((1, LANE), F32)
        for b in range(nb):
            rows = slice(b * LANE, (b + 1) * LANE)
            blk = out_ref[rows, :]
            ga = jnp.where(is_a[:LANE], blk, 0.0)
            fb = _hdot(tril, jnp.where(is_f[:LANE], blk, 0.0)) + off
            run = fb + _hdot(tril_c, ga)
            run_ref[rows, :] = run
            runt_ref[:, rows] = run.T
            tot_ref[rows, :] = _hdot(same_c, ga)
            off = fb[LANE - 1:LANE, :]

    return pl.pallas_call(
        body, name="gates",
        out_shape=[jax.ShapeDtypeStruct((t, LANE), F32)] * 3 + [jax.ShapeDtypeStruct((LANE, t), F32)],
        compiler_params=_params(),
    )(ps, gparams)


def _gates_bwd(ps, gparams, dcol_g, dtot_g, dcol_f, drow_g, drow_f):
    t = ps.shape[0]
    nb = t // LANE

    def body(ps_ref, gp_ref, dcg_ref, dtg_ref, dcf_ref, drg_ref, drf_ref, dps_ref, sums_ref, dl_ref, d0_ref, tr_ref):
        p = ps_ref[...]
        is_b, is_a, is_f = _gate_lanes(p.shape)
        _, triu, _, triu_c, same_c = _block_masks()
        tr_ref[...] = jnp.zeros_like(tr_ref)
        off = jnp.zeros((1, LANE), F32)
        for b in reversed(range(nb)):
            rows = slice(b * LANE, (b + 1) * LANE)
            tr_ref[HEADS:2 * HEADS, :] = drg_ref[:, rows]
            tr_ref[2 * HEADS:3 * HEADS, :] = drf_ref[:, rows]
            d = dcg_ref[rows, :] + dcf_ref[rows, :] + tr_ref[...].T
            d0_ref[rows, :] = d
            dlf = _hdot(triu, jnp.where(is_f[:LANE], d, 0.0)) + off
            dla = (_hdot(triu_c, jnp.where(is_a[:LANE], d, 0.0))
                   + _hdot(same_c, jnp.where(is_a[:LANE], dtg_ref[rows, :], 0.0)))
            dl_ref[rows, :] = dlf + dla
            off = dlf[0:1, :]
        z = p + gp_ref[0:1, :]
        neg_exp_a = -jnp.exp(gp_ref[1:2, :])
        sb = _sigmoid(p)
        glog = neg_exp_a * _softplus(z)
        dl = dl_ref[...]
        dp = jnp.where(is_b, d0_ref[...] * sb * (1.0 - sb),
                       jnp.where(is_a, dl * neg_exp_a * _sigmoid(z), jnp.where(is_f, dl * _sigmoid(-z), 0.0)))
        dps_ref[...] = dp
        s_a = jnp.sum(jnp.where(is_a, dl * glog, 0.0), axis=0, keepdims=True)
        s_p = jnp.sum(jnp.where(is_b, 0.0, dp), axis=0, keepdims=True)
        row = lax.broadcasted_iota(jnp.int32, (8, LANE), 0)
        from_a = pltpu.roll(jnp.where(row == 0, s_a, jnp.where(row == 1, s_p, 0.0)), LANE - HEADS, 1)
        from_f = pltpu.roll(jnp.where(row == 2, s_p, 0.0), LANE - 2 * HEADS, 1)
        lane = lax.broadcasted_iota(jnp.int32, (8, LANE), 1)
        sums_ref[...] = jnp.where(lane < HEADS, from_a + from_f, 0.0)

    return pl.pallas_call(
        body, name="gates_bwd",
        out_shape=[jax.ShapeDtypeStruct((t, LANE), F32), jax.ShapeDtypeStruct((8, LANE), F32)],
        scratch_shapes=[pltpu.VMEM((t, LANE), F32), pltpu.VMEM((t, LANE), F32), pltpu.VMEM((LANE, LANE), F32)],
        compiler_params=_params(),
    )(ps, gparams, dcol_g, dtot_g, dcol_f, drow_g, drow_f)


def _conv(xv, w):
    acc = w[3:4, :] * xv
    for s in range(1, 4):
        acc += w[3 - s:4 - s, :] * _shift_down(xv, s)
    return acc


PREP_ROWS = 256
HALO = 8
PREP_UNROLL = 2


def _tile_loop(n, tile, init):
    if n % PREP_UNROLL:
        return lax.fori_loop(0, n, tile, init)

    def trip(g, carry):
        for u in range(PREP_UNROLL):
            carry = tile(g * PREP_UNROLL + u, carry)
        return carry

    return lax.fori_loop(0, n // PREP_UNROLL, trip, init)


def _tile_rows(r):
    return pl.ds(pl.multiple_of(r * PREP_ROWS, PREP_ROWS), PREP_ROWS)


def _gdn_prep(pm, conv_w):
    t = pm.shape[0]
    nj = WIDTH // LANE
    win = PREP_ROWS + HALO

    def body(x_ref, w_ref, o_ref, xp_ref):
        kind = pl.program_id(0)
        xp_ref[0:HALO, :] = jnp.zeros((HALO, LANE), F32)
        xp_ref[HALO:, :] = x_ref[...]
        w = w_ref[...]
        hs = _head_sum_matrix()

        def tile(r, _, normed):
            xw = xp_ref[pl.ds(pl.multiple_of(r * PREP_ROWS, PREP_ROWS), win), :]
            acc = _conv(xw, w)[HALO:]
            out = acc * _sigmoid(acc)
            if normed:
                out = out * lax.rsqrt(_group_sum(out * out, hs) + EPS)
            o_ref[0, 0, _tile_rows(r), :] = out[:, :DH]
            o_ref[0, 1, _tile_rows(r), :] = out[:, DH:]
            return 0

        @pl.when(kind < 2)
        def _():
            _tile_loop(t // PREP_ROWS, functools.partial(tile, normed=True), 0)

        @pl.when(kind == 2)
        def _():
            _tile_loop(t // PREP_ROWS, functools.partial(tile, normed=False), 0)

    return pl.pallas_call(
        body, name="gdn_prep", grid=(3, nj),
        in_specs=[pl.BlockSpec((t, LANE), lambda i, j: (0, i * nj + j)),
                  pl.BlockSpec((4, LANE), lambda i, j: (0, i * nj + j))],
        out_specs=pl.BlockSpec((1, 2, t, DH), lambda i, j: (i, j, 0, 0)),
        out_shape=jax.ShapeDtypeStruct((3, HEADS, t, DH), F32),
        scratch_shapes=[pltpu.VMEM((t + HALO, LANE), F32)],
        compiler_params=_params(("arbitrary", "arbitrary")),
    )(pm, conv_w)


def _gdn_prep_bwd(pm, conv_w, dqkv):
    t = pm.shape[0]
    nj = WIDTH // LANE
    win = PREP_ROWS + 2 * HALO

    def body(x_ref, w_ref, d_ref, dx_ref, dw_ref, xp_ref, dp_ref):
        kind = pl.program_id(0)
        zeros = jnp.zeros((HALO, LANE), F32)
        for ref in (xp_ref, dp_ref):
            ref[0:HALO, :] = zeros
            ref[HALO + t:, :] = zeros
        xp_ref[HALO:HALO + t, :] = x_ref[...]
        dp_ref[HALO:HALO + t, 0:DH] = d_ref[0, 0]
        dp_ref[HALO:HALO + t, DH:] = d_ref[0, 1]
        w = w_ref[...]
        hs = _head_sum_matrix()
        rows = lax.broadcasted_iota(jnp.int32, (win, LANE), 0)
        in_tile = (rows >= HALO) & (rows < HALO + PREP_ROWS)

        def tile(r, dw, normed):
            start = pl.multiple_of(r * PREP_ROWS, PREP_ROWS)
            xw = xp_ref[pl.ds(start, win), :]
            dy = dp_ref[pl.ds(start, win), :]
            acc = _conv(xw, w)
            sg = _sigmoid(acc)
            if normed:
                y = acc * sg
                rn = lax.rsqrt(_group_sum(y * y, hs) + EPS)
                yn = y * rn
                dy = rn * (dy - yn * _group_sum(dy * yn, hs))
            dacc = dy * sg * (1.0 + acc * (1.0 - sg))
            dx = w[3:4, :] * dacc
            for s in range(1, 4):
                dx += w[3 - s:4 - s, :] * _shift_up(dacc, s)
            dx_ref[_tile_rows(r), :] = dx[HALO:HALO + PREP_ROWS]
            dm = jnp.where(in_tile, dacc, 0.0)
            return tuple(dw[i] + jnp.sum(dm * (xw if i == 3 else _shift_down(xw, 3 - i)), axis=0, keepdims=True)
                         for i in range(4))

        def run(normed):
            dw = _tile_loop(t // PREP_ROWS, functools.partial(tile, normed=normed),
                            tuple(jnp.zeros((1, LANE), F32) for _ in range(4)))
            for i in range(4):
                dw_ref[i:i + 1, :] = dw[i]

        pl.when(kind < 2)(functools.partial(run, True))
        pl.when(kind == 2)(functools.partial(run, False))

    return pl.pallas_call(
        body, name="gdn_prep_bwd", grid=(3, nj),
        in_specs=[pl.BlockSpec((t, LANE), lambda i, j: (0, i * nj + j)),
                  pl.BlockSpec((4, LANE), lambda i, j: (0, i * nj + j)),
                  pl.BlockSpec((1, 2, t, DH), lambda i, j: (i, j, 0, 0))],
        out_specs=[pl.BlockSpec((t, LANE), lambda i, j: (0, i * nj + j)),
                   pl.BlockSpec((4, LANE), lambda i, j: (0, i * nj + j))],
        out_shape=[jax.ShapeDtypeStruct((t, 3 * WIDTH), F32), jax.ShapeDtypeStruct((4, 3 * WIDTH), F32)],
        scratch_shapes=[pltpu.VMEM((t + 2 * HALO, LANE), F32), pltpu.VMEM((t + 2 * HALO, LANE), F32)],
        compiler_params=_params(("arbitrary", "arbitrary")),
    )(pm, conv_w, dqkv)


FOX_COL0 = 4 * WIDTH // LANE


def _fox_prep(pm, nw):
    t = pm.shape[0]
    nj = WIDTH // LANE

    def body(x_ref, w_ref, o_ref):
        kind = pl.program_id(0)
        hs = _head_sum_matrix()
        wk = w_ref[pl.ds(kind, 1), :]

        def tile(r, _, normed):
            out = x_ref[_tile_rows(r), :]
            if normed:
                out = out * lax.rsqrt(_group_sum(out * out, hs) * (1.0 / DH) + EPS) * wk
            o_ref[0, 0, _tile_rows(r), :] = out[:, :DH]
            o_ref[0, 1, _tile_rows(r), :] = out[:, DH:]
            return 0

        @pl.when(kind < 2)
        def _():
            _tile_loop(t // PREP_ROWS, functools.partial(tile, normed=True), 0)

        @pl.when(kind == 2)
        def _():
            _tile_loop(t // PREP_ROWS, functools.partial(tile, normed=False), 0)

    return pl.pallas_call(
        body, name="fox_prep", grid=(3, nj),
        in_specs=[pl.BlockSpec((t, LANE), lambda i, j: (0, FOX_COL0 + i * nj + j)),
                  pl.BlockSpec((3, LANE), lambda i, j: (0, 0))],
        out_specs=pl.BlockSpec((1, 2, t, DH), lambda i, j: (i, j, 0, 0)),
        out_shape=jax.ShapeDtypeStruct((3, HEADS, t, DH), F32),
        compiler_params=_params(("arbitrary", "arbitrary")),
    )(pm, nw)


def _fox_prep_bwd(pm, nw, dqkv):
    t = pm.shape[0]
    nj = WIDTH // LANE

    def body(x_ref, w_ref, d_ref, dx_ref, dw_ref):
        kind = pl.program_id(0)
        hs = _head_sum_matrix()
        wk = w_ref[pl.ds(kind, 1), :]

        def tile(r, dw):
            xv = x_ref[_tile_rows(r), :]
            rn = lax.rsqrt(_group_sum(xv * xv, hs) * (1.0 / DH) + EPS)
            xn = xv * rn
            d = jnp.concatenate([d_ref[0, 0, _tile_rows(r), :], d_ref[0, 1, _tile_rows(r), :]], axis=1)
            g = d * wk
            dx_ref[_tile_rows(r), :] = rn * (g - xn * _group_sum(g * xn, hs) * (1.0 / DH))
            return dw + jnp.sum(d * xn, axis=0, keepdims=True)

        def copy_tile(r, _):
            dx_ref[_tile_rows(r), :] = jnp.concatenate([d_ref[0, 0, _tile_rows(r), :], d_ref[0, 1, _tile_rows(r), :]], axis=1)
            return 0

        @pl.when(kind < 2)
        def _():
            dw_ref[0, 0] = _tile_loop(t // PREP_ROWS, tile, jnp.zeros((1, LANE), F32))

        @pl.when(kind == 2)
        def _():
            _tile_loop(t // PREP_ROWS, copy_tile, 0)
            dw_ref[0, 0] = jnp.zeros((1, LANE), F32)

    return pl.pallas_call(
        body, name="fox_prep_bwd", grid=(3, nj),
        in_specs=[pl.BlockSpec((t, LANE), lambda i, j: (0, FOX_COL0 + i * nj + j)),
                  pl.BlockSpec((3, LANE), lambda i, j: (0, 0)),
                  pl.BlockSpec((1, 2, t, DH), lambda i, j: (i, j, 0, 0))],
        out_specs=[pl.BlockSpec((t, LANE), lambda i, j: (0, i * nj + j)),
                   pl.BlockSpec((1, 1, 1, LANE), lambda i, j: (i, j, 0, 0))],
        out_shape=[jax.ShapeDtypeStruct((t, 3 * WIDTH), F32), jax.ShapeDtypeStruct((3, nj, 1, LANE), F32)],
        compiler_params=_params(("arbitrary", "arbitrary")),
    )(pm, nw, dqkv)


SC = 256
CPS = SC // CHUNK
GDN_HP_FWD = 4
GDN_HP_BWD = 2
Q_SCALE = DH ** -0.5


def _sc_masks():
    ri = lax.broadcasted_iota(jnp.int32, (SC, SC), 0)
    ci = lax.broadcasted_iota(jnp.int32, (SC, SC), 1)
    same = (ri // CHUNK) == (ci // CHUNK)
    return same & (ri >= ci), same & (ri > ci), ri == ci


def _unit_lower_inverses(ms, eye):
    invs = [jnp.where(eye, 1.0, 0.0) + m for m in ms]
    ms = [_dot(m, m) for m in ms]
    for _ in range(4):
        both = [_dot(jnp.concatenate([inv, m], axis=0), m) for inv, m in zip(invs, ms)]
        invs = [inv + b[:SC] for inv, b in zip(invs, both)]
        ms = [b[SC:] for b in both]
    return [inv + _dot(inv, m) for inv, m in zip(invs, ms)]


def _lane_col(blk, lane_idx):
    lane = lax.broadcasted_iota(jnp.int32, blk.shape, 1)
    return jnp.sum(jnp.where(lane == lane_idx, blk, 0.0), axis=1, keepdims=True)


def _to_lane(col, lane_idx):
    lane = lax.broadcasted_iota(jnp.int32, (col.shape[0], LANE), 1)
    return jnp.where(lane == lane_idx, col, 0.0)


def _gdn_columns(gates_ref, run_ref, tot_ref, runt_ref, rows, h):
    return (_lane_col(gates_ref[rows, :], h), _lane_col(run_ref[rows, :], HEADS + h),
            _lane_col(tot_ref[rows, :], HEADS + h), runt_ref[pl.ds(h, 1), rows])


def _gdn_local(q, k, beta, gc, gl, grow, causal, with_kk=True):
    decay = jnp.exp(jnp.where(causal, gc - grow, -1e30))
    egc = jnp.exp(gc)
    ekd = jnp.exp(gl - gc)
    qs = q * Q_SCALE
    kb = k * beta
    if with_kk:
        both = _dot_nt(jnp.concatenate([kb, qs], axis=0), k)
        kk, qk = both[:SC], both[SC:]
    else:
        kk, qk = None, _dot_nt(qs, k)
    return beta, gl, decay, egc, ekd, qs, kb, kk, qk, jnp.where(causal, qk * decay, 0.0)


def _chunk_rows(c):
    return pl.ds(c * CHUNK if isinstance(c, int) else pl.multiple_of(c * CHUNK, CHUNK), CHUNK)


def _sc_rows(b):
    return pl.ds(b * SC if isinstance(b, int) else pl.multiple_of(b * SC, SC), SC)


def _gdn_fwd(qkv, gates, run, tot, run_t):
    t = qkv.shape[2]
    nc = t // CHUNK
    nsc = t // SC

    hp_n = GDN_HP_FWD
    heads = range(hp_n)

    def body(qkv_ref, gates_ref, run_ref, tot_ref, runt_ref, o_ref, st_ref, inv_ref, kc_s, qc_s, g_s, au_s):
        hp = pl.program_id(0)
        causal, strict, eye = _sc_masks()

        def local(b, _):
            rows = _sc_rows(b)
            loc = [_gdn_local(qkv_ref[0, hh, rows, :], qkv_ref[1, hh, rows, :],
                              *_gdn_columns(gates_ref, run_ref, tot_ref, runt_ref, rows, hp * hp_n + hh), causal)
                   for hh in heads]
            invs = _unit_lower_inverses([-jnp.where(strict, l[7] * l[2], 0.0) for l in loc], eye)
            uws = []
            for hh in heads:
                beta, _, _, egc, _, _, kb, _, _, _ = loc[hh]
                inv_ref[hh, rows, :] = invs[hh].astype(inv_ref.dtype)
                uws.append(_dot(invs[hh], jnp.concatenate([qkv_ref[2, hh, rows, :] * beta, kb * egc], axis=1)))
            for hh in heads:
                _, _, _, egc, ekd, qs, _, _, _, attn = loc[hh]
                auw = _dot(attn, uws[hh])
                g_s[hh, rows, :] = qs * egc - auw[:, DH:]
                au_s[hh, rows, :] = auw[:, :DH]
                kd = qkv_ref[1, hh, rows, :] * ekd
                for j in range(CPS):
                    sl = slice(j * CHUNK, (j + 1) * CHUNK)
                    both = _dot_tn(kd[sl], uws[hh][sl])
                    kc_s[hh, b * CPS + j] = both[:, DH:]
                    qc_s[hh, b * CPS + j] = both[:, :DH]
            return 0

        def step(c, states):
            rows = _chunk_rows(c)
            tot_row = tot_ref[pl.ds(c * CHUNK, 1), :]
            new = []
            for hh in heads:
                s = states[hh]
                st_ref[hh, c] = s
                o_ref[hh, rows, :] = _dot(g_s[hh, rows, :], s) + au_s[hh, rows, :]
                egl = jnp.exp(_lane_col(tot_row, HEADS + hp * hp_n + hh))
                new.append(egl * s - _dot(kc_s[hh, c], s) + qc_s[hh, c])
            return tuple(new)

        def steps_of(b, states):
            for j in range(CPS):
                states = step(b * CPS + j, states)
            return states

        def fused(b, states):
            states = steps_of(b - 1, states)
            local(b, 0)
            return states

        local(0, 0)
        states = lax.fori_loop(1, nsc, fused, tuple(jnp.zeros((DH, DH), F32) for _ in heads))
        steps_of(nsc - 1, states)

    whole = pl.BlockSpec((t, LANE), lambda h: (0, 0))
    once = dict(pipeline_mode=pl.Buffered(1))
    return pl.pallas_call(
        body, name="gdn_fwd", grid=(HEADS // hp_n,),
        in_specs=[pl.BlockSpec((3, hp_n, t, DH), lambda h: (0, h, 0, 0), **once), whole, whole, whole,
                  pl.BlockSpec((HEADS, t), lambda h: (1, 0))],
        out_specs=[pl.BlockSpec((hp_n, t, DH), lambda h: (h, 0, 0), **once),
                   pl.BlockSpec((hp_n, nc, DH, DH), lambda h: (h, 0, 0, 0), **once),
                   pl.BlockSpec((hp_n, t, SC), lambda h: (h, 0, 0), **once)],
        out_shape=[jax.ShapeDtypeStruct((HEADS, t, DH), F32), jax.ShapeDtypeStruct((HEADS, nc, DH, DH), F32),
                   jax.ShapeDtypeStruct((HEADS, t, SC), MXU)],
        scratch_shapes=[pltpu.VMEM((hp_n, nc, DH, DH), F32), pltpu.VMEM((hp_n, nc, DH, DH), F32),
                        pltpu.VMEM((hp_n, t, DH), F32), pltpu.VMEM((hp_n, t, DH), F32)],
        compiler_params=_params(("arbitrary",)),
    )(qkv, gates, run, tot, run_t)


def _gdn_bwd(qkv, gates, run, tot, run_t, inv, states, do):
    t = qkv.shape[2]
    nc = t // CHUNK
    nsc = t // SC

    hp_n = GDN_HP_BWD
    heads = range(hp_n)

    def body(qkv_ref, gates_ref, run_ref, tot_ref, runt_ref, inv_ref, st_ref, do_ref,
             dqkv_ref, dcol_ref, dtot_ref, drow_ref, uw_s, kc_s, h_s, dsn_s):
        hp = pl.program_id(0)
        causal, strict, _ = _sc_masks()

        @pl.when(hp == 0)
        def _():
            dcol_ref[...] = jnp.zeros_like(dcol_ref)
            dtot_ref[...] = jnp.zeros_like(dtot_ref)

        def local_of(hh, rows, with_kk=True):
            return _gdn_local(qkv_ref[0, hh, rows, :], qkv_ref[1, hh, rows, :],
                              *_gdn_columns(gates_ref, run_ref, tot_ref, runt_ref, rows, hp * hp_n + hh), causal, with_kk)

        def local(b, _):
            rows = _sc_rows(b)
            loc = [local_of(hh, rows, with_kk=False) for hh in heads]
            us, ws = [], []
            for hh in heads:
                beta, _, _, egc, _, _, kb, _, _, _ = loc[hh]
                inv_b = inv_ref[hh, rows, :]
                us.append(_dot(inv_b, qkv_ref[2, hh, rows, :] * beta))
                ws.append(_dot(inv_b, kb * egc))
            gs = [loc[hh][5] * loc[hh][3] - _dot(loc[hh][9], ws[hh]) for hh in heads]
            for hh in heads:
                uw_s[hh, rows, :] = jnp.concatenate([us[hh], ws[hh]], axis=1)
                kd = qkv_ref[1, hh, rows, :] * loc[hh][4]
                dout = do_ref[hh, rows, :]
                for j in range(CPS):
                    sl = slice(j * CHUNK, (j + 1) * CHUNK)
                    kc_s[hh, b * CPS + j] = _dot_tn(kd[sl], ws[hh][sl])
                    h_s[hh, b * CPS + j] = _dot_tn(gs[hh][sl], dout[sl])
            return 0

        def step(c, dss):
            tot_row = tot_ref[pl.ds(c * CHUNK, 1), :]
            new = []
            for hh in heads:
                ds = dss[hh]
                dsn_s[hh, c] = ds
                egl = jnp.exp(_lane_col(tot_row, HEADS + hp * hp_n + hh))
                new.append(egl * ds - _dot_tn(kc_s[hh, c], ds) + h_s[hh, c])
            return tuple(new)

        def steps_of(b, dss):
            for j in reversed(range(CPS)):
                dss = step(b * CPS + j, dss)
            return dss

        def fused(i, dss):
            b = nsc - 2 - i
            dss = steps_of(b + 1, dss)
            local(b, 0)
            return dss

        local(nsc - 1, 0)
        dss = lax.fori_loop(0, nsc - 1, fused, tuple(jnp.zeros((DH, DH), F32) for _ in heads))
        steps_of(0, dss)

        def back(b, _):
            rows = _sc_rows(b)
            first = lax.broadcasted_iota(jnp.int32, (CHUNK, 1), 0) == 0
            loc = [local_of(hh, rows) for hh in heads]
            sign = jnp.where(lax.broadcasted_iota(jnp.int32, (1, LANE), 1) < DH, 1.0, -1.0)
            mid = []
            for hh in heads:
                beta, gl, decay, egc, ekd, qs, kb, kk, qk, attn = loc[hh]
                uw = uw_s[hh, rows, :]
                kd = qkv_ref[1, hh, rows, :] * ekd
                dout = do_ref[hh, rows, :]
                dg_p, dkd_p, duw_p, dgl_p = [], [], [], []
                for j in range(CPS):
                    sl = slice(j * CHUNK, (j + 1) * CHUNK)
                    s = st_ref[hh, b * CPS + j]
                    dsn = dsn_s[hh, b * CPS + j]
                    both = _dot_nt(jnp.concatenate([dsn, dout[sl]], axis=0), s)
                    dg_p.append(both[CHUNK:])
                    dsc = jnp.concatenate([dsn, -both[:CHUNK]], axis=1)
                    dkd_p.append(_dot_nt(uw[sl], dsc))
                    duw_p.append(_dot(kd[sl], dsc))
                    degl = jnp.sum(jnp.sum(s * dsn, axis=1, keepdims=True), axis=0, keepdims=True)
                    dgl_p.append(jnp.where(first, degl * jnp.exp(gl[j * CHUNK:j * CHUNK + 1, :]), 0.0))
                dg, dkd = jnp.concatenate(dg_p, axis=0), jnp.concatenate(dkd_p, axis=0)
                dod = jnp.concatenate([dout, dg], axis=1)
                da = jnp.where(causal, _dot_nt(dod, uw * sign), 0.0)
                duw = jnp.concatenate(duw_p, axis=0) + _dot_tn(attn, dod) * sign
                mid.append((kd, dg, dkd, da, duw, jnp.concatenate(dgl_p, axis=0)))
            inv_ts = [inv_ref[hh, rows, :].astype(F32).T for hh in heads]
            its = [_dot(inv_ts[hh], mid[hh][4]) for hh in heads]
            dinvs = []
            for hh in heads:
                beta, _, _, egc, _, _, kb, _, _, _ = loc[hh]
                dinvs.append(_dot_nt(mid[hh][4], jnp.concatenate([qkv_ref[2, hh, rows, :] * beta, kb * egc], axis=1)))
            half = [_dot(inv_ts[hh], dinvs[hh]) for hh in heads]
            dls = [jnp.where(strict, -_dot(half[hh], inv_ts[hh]), 0.0) for hh in heads]
            for hh in heads:
                head = hp * hp_n + hh
                beta, gl, decay, egc, ekd, qs, kb, kk, qk, attn = loc[hh]
                kd, dg, dkd, da, _, dgl_first = mid[hh]
                k, v = qkv_ref[1, hh, rows, :], qkv_ref[2, hh, rows, :]
                dvb, dkbe = its[hh][:, :DH], its[hh][:, DH:]
                dl = dls[hh]
                dlogd = (dl * kk + da * qk) * decay
                dd = jnp.concatenate([dl * decay, da * decay], axis=0)
                ddk = _dot(dd, k)
                dkb = ddk[:SC] + dkbe * egc
                dqs = ddk[SC:] + dg * egc
                dk = _dot_tn(dd, jnp.concatenate([kb, qs], axis=0)) + dkd * ekd + dkb * beta
                dkd_kd = jnp.sum(dkd * kd, axis=1, keepdims=True)
                narrow = dg * (qs * egc) + dkbe * (kb * egc)
                wide = dlogd[:, :LANE] + dlogd[:, LANE:] + jnp.concatenate([narrow, jnp.zeros((SC, LANE - DH), F32)], axis=1)
                dgc = jnp.sum(wide, axis=1, keepdims=True) - dkd_kd
                dbeta = jnp.sum(dkb * k + dvb * v, axis=1, keepdims=True)
                dqkv_ref[0, hh, rows, :] = dqs * Q_SCALE
                dqkv_ref[1, hh, rows, :] = dk
                dqkv_ref[2, hh, rows, :] = dvb * beta
                dcol_ref[rows, :] += _to_lane(dbeta, head) + _to_lane(dgc, HEADS + head)
                dtot_ref[rows, :] += _to_lane(dkd_kd + dgl_first, HEADS + head)
                drow_ref[pl.ds(head, 1), rows] = -jnp.sum(dlogd, axis=0, keepdims=True)
            return 0

        lax.fori_loop(0, nsc, back, 0)

    whole = pl.BlockSpec((t, LANE), lambda h: (0, 0))
    rowspec = pl.BlockSpec((HEADS, t), lambda h: (0, 0))
    sq = pltpu.VMEM((hp_n, nc, DH, DH), F32)
    per_head = pltpu.VMEM((hp_n, t, 2 * DH), F32)
    return pl.pallas_call(
        body, name="gdn_bwd", grid=(HEADS // hp_n,),
        in_specs=[pl.BlockSpec((3, hp_n, t, DH), lambda h: (0, h, 0, 0)), whole, whole, whole,
                  pl.BlockSpec((HEADS, t), lambda h: (1, 0)),
                  pl.BlockSpec((hp_n, t, SC), lambda h: (h, 0, 0)), pl.BlockSpec((hp_n, nc, DH, DH), lambda h: (h, 0, 0, 0)),
                  pl.BlockSpec((hp_n, t, DH), lambda h: (h, 0, 0))],
        out_specs=[pl.BlockSpec((3, hp_n, t, DH), lambda h: (0, h, 0, 0)), whole, whole, rowspec],
        out_shape=[jax.ShapeDtypeStruct((3, HEADS, t, DH), F32), jax.ShapeDtypeStruct((t, LANE), F32),
                   jax.ShapeDtypeStruct((t, LANE), F32), jax.ShapeDtypeStruct((HEADS, t), F32)],
        scratch_shapes=[per_head, sq, sq, sq],
        compiler_params=_params(("arbitrary",)),
    )(qkv, gates, run, tot, run_t, inv, states, do)


FOX_HP_FWD = 4
FOX_HP_BWD = 4


def _wide_t(a):
    r = a.shape[0]
    return jnp.concatenate([a, jnp.zeros((r, LANE - DH), F32)], axis=1).T[:DH]


def _tall_t(a):
    r = a.shape[1]
    return jnp.concatenate([a, jnp.zeros((LANE - DH, r), F32)], axis=0).T[:, :DH]


def _key_side_f(run_blk, head, qb):
    col = jnp.broadcast_to(_lane_col(run_blk, 2 * HEADS + head), (run_blk.shape[0], LANE))
    return jnp.concatenate([col] * (qb // LANE), axis=1)


def _diag_mask(qb):
    return lax.broadcasted_iota(jnp.int32, (qb, qb), 0) <= lax.broadcasted_iota(jnp.int32, (qb, qb), 1)


def _fox_fwd(qkv, run, run_t, carry=None):
    t = qkv.shape[2]
    qb = min(QB, t)
    nq = t // qb
    hp_n = FOX_HP_FWD
    c_in, c_in_specs, c_out_shape, c_out_specs, c_sem = _carry_operands(*carry) if carry else ([], [], [], [], None)
    n_c = len(c_in)

    def body(*refs):
        q_ref, k_ref, v_ref, run_ref, runt_ref = refs[:5]
        o_ref, lse_ref = refs[5 + n_c:7 + n_c]
        s0 = 7 + n_c + len(c_out_shape)
        kb_s, vt_s, raw_s, m_s, l_s, acc_s = refs[s0:s0 + 6]
        hp = pl.program_id(0)
        i = pl.program_id(1)

        if carry:
            _carry_step(carry[0], refs[5:5 + n_c], refs[7 + n_c], refs[8 + n_c], refs[-1], hp * nq + i,
                        (HEADS // hp_n) * nq)

        @pl.when(i == 0)
        def _():
            for hh in range(hp_n):
                kb_s[hh] = k_ref[0, hh].astype(MXU)
                for b in range(nq):
                    rows = slice(b * qb, (b + 1) * qb)
                    vt_s[hh, :, rows] = _wide_t(v_ref[0, hh, rows, :]).astype(MXU)

        qrows = pl.ds(pl.multiple_of(i * qb, qb), qb)
        qs = [(q_ref[0, hh] * Q_SCALE).astype(MXU) for hh in range(hp_n)]
        fq = [runt_ref[pl.ds(hp * hp_n + hh, 1), qrows] for hh in range(hp_n)]

        def block_rows(j):
            return pl.ds(pl.multiple_of(j * qb, qb), qb)

        def scores(j):
            for hh in range(hp_n):
                raw_s[j % 2, hh] = _dot_nt(kb_s[hh, block_rows(j), :], qs[hh])

        def absorb(j, diagonal):
            rows = block_rows(j)
            run_blk = run_ref[rows, :]
            stats, pv = [], []
            for hh in range(hp_n):
                m, l = m_s[hh], l_s[hh]
                st = raw_s[j % 2, hh] + (fq[hh] - _key_side_f(run_blk, hp * hp_n + hh, qb))
                if diagonal:
                    st = jnp.where(_diag_mask(qb), st, -1e30)
                m_new = jnp.maximum(m, jnp.max(st, axis=0, keepdims=True))
                p = jnp.exp(st - m_new)
                alpha = jnp.exp(m - m_new)
                stats.append((m_new, alpha * l + jnp.sum(p, axis=0, keepdims=True), alpha))
                pv.append(_dot(vt_s[hh, :, rows], p))
            for hh in range(hp_n):
                m_s[hh], l_s[hh] = stats[hh][0], stats[hh][1]
                acc_s[hh] = stats[hh][2] * acc_s[hh] + pv[hh]

        def kstep(j, _):
            scores(j + 1)
            absorb(j, False)
            return 0

        for hh in range(hp_n):
            m_s[hh] = jnp.full((1, qb), -1e30, F32)
            l_s[hh] = jnp.zeros((1, qb), F32)
            acc_s[hh] = jnp.zeros((DH, qb), F32)
        scores(0)
        lax.fori_loop(0, i, kstep, 0)
        absorb(i, True)
        for hh in range(hp_n):
            l = l_s[hh]
            o_ref[hh] = _tall_t(acc_s[hh] / l)
            lse_ref[pl.ds(hp * hp_n + hh, 1), qrows] = m_s[hh] + jnp.log(l)

    out = pl.pallas_call(
        body, name="fox_fwd", grid=(HEADS // hp_n, nq),
        in_specs=[pl.BlockSpec((1, hp_n, qb, DH), lambda h, i: (0, h, i, 0)),
                  pl.BlockSpec((1, hp_n, t, DH), lambda h, i: (1, h, 0, 0)),
                  pl.BlockSpec((1, hp_n, t, DH), lambda h, i: (2, h, 0, 0)),
                  pl.BlockSpec((t, LANE), lambda h, i: (0, 0)),
                  pl.BlockSpec((HEADS, t), lambda h, i: (2, 0))] + c_in_specs,
        out_specs=[pl.BlockSpec((hp_n, qb, DH), lambda h, i: (h, i, 0)),
                   pl.BlockSpec((HEADS, t), lambda h, i: (0, 0))] + c_out_specs,
        out_shape=[jax.ShapeDtypeStruct((HEADS, t, DH), F32), jax.ShapeDtypeStruct((HEADS, t), F32)] + c_out_shape,
        input_output_aliases={5 + j: 4 + j for j in range(n_c)},
        scratch_shapes=[pltpu.VMEM((hp_n, t, DH), MXU), pltpu.VMEM((hp_n, DH, t), MXU), pltpu.VMEM((2, hp_n, qb, qb), F32),
                        pltpu.VMEM((hp_n, 1, qb), F32), pltpu.VMEM((hp_n, 1, qb), F32), pltpu.VMEM((hp_n, DH, qb), F32)]
        + ([c_sem] if carry else []),
        compiler_params=_params(("arbitrary", "arbitrary")),
    )(qkv, qkv, qkv, run, run_t, *c_in)
    return (out[0], out[1], _carry_state(out[2:])) if carry else (out[0], out[1])


def _fox_bwd(qkv, run, run_t, o, lse, do, carry=None):
    t = qkv.shape[2]
    qb = min(QB, t)
    nq = t // qb
    hp_n = FOX_HP_BWD
    c_in, c_in_specs, c_out_shape, c_out_specs, c_sem = _carry_operands(*carry) if carry else ([], [], [], [], None)
    n_c = len(c_in)

    def body(*refs):
        q_ref, k_ref, v_ref, run_ref, runt_ref, o_ref, lse_ref, do_ref = refs[:8]
        dqkv_ref, dcol_ref, drow_ref = refs[8 + n_c:11 + n_c]
        s0 = 11 + n_c + len(c_out_shape)
        dqt_s, raw_s, dpt_s, dro_s, dk_s, dv_s, dsum_s = refs[s0:s0 + 7]
        hp = pl.program_id(0)
        j = pl.program_id(1)

        if carry:
            _carry_step(carry[0], refs[8:8 + n_c], refs[11 + n_c], refs[12 + n_c], refs[-1], hp * nq + j,
                        (HEADS // hp_n) * nq)

        @pl.when(j == 0)
        def _():
            dqt_s[...] = jnp.zeros_like(dqt_s)

        @pl.when((j == 0) & (hp == 0))
        def _():
            dcol_ref[...] = jnp.zeros_like(dcol_ref)
            drow_ref[...] = jnp.zeros_like(drow_ref)

        krows = pl.ds(pl.multiple_of(j * qb, qb), qb)
        run_blk = run_ref[krows, :]
        ones8 = jnp.ones((8, DH), MXU)
        kb, kt, vb, fk = [], [], [], []
        for hh in range(hp_n):
            kf = k_ref[0, hh]
            kb.append(kf.astype(MXU))
            kt.append(_wide_t(kf).astype(MXU))
            vb.append(v_ref[0, hh].astype(MXU))
            fk.append(_key_side_f(run_blk, hp * hp_n + hh, qb))

        def block_rows(i):
            return pl.ds(i * qb if isinstance(i, int) else pl.multiple_of(i * qb, qb), qb)

        def products(i):
            rows = block_rows(i)
            slot = i % 2
            for hh in range(hp_n):
                dout = do_ref[hh, rows, :]
                x = dout * o_ref[hh, rows, :]
                x_hi = x.astype(MXU)
                raw_s[slot, hh] = _dot_nt(kb[hh], q_ref[0, hh, rows, :] * Q_SCALE)
                dpt_s[slot, hh] = _dot_nt(vb[hh], dout)
                dro_s[slot, hh] = _dot_nt(ones8, x_hi) + _dot_nt(ones8, x - x_hi.astype(F32))

        def absorb(i, diagonal):
            rows = block_rows(i)
            slot = i % 2
            pieces = []
            for hh in range(hp_n):
                head = hp * hp_n + hh
                raw, dpt, drow = raw_s[slot, hh], dpt_s[slot, hh], dro_s[slot, hh, 0:1, :]
                off = runt_ref[pl.ds(head, 1), rows] - lse_ref[pl.ds(head, 1), rows]
                st = raw + (off - fk[hh])
                if diagonal:
                    st = jnp.where(_diag_mask(qb), st, -1e30)
                pt = jnp.exp(st)
                dst = pt * (dpt - drow)
                drow_ref[pl.ds(head, 1), rows] += jnp.sum(dst, axis=0, keepdims=True)
                folded = dst[:, 0:LANE]
                for c in range(1, qb // LANE):
                    folded = folded + dst[:, c * LANE:(c + 1) * LANE]
                pieces.append((_dot(dst, q_ref[0, hh, rows, :] * Q_SCALE), _dot(pt, do_ref[hh, rows, :]),
                               _dot(kt[hh], dst), folded))
            for hh in range(hp_n):
                dqt_s[hh, :, rows] += pieces[hh][2]
                if diagonal:
                    dk_s[hh], dv_s[hh], dsum_s[hh] = pieces[hh][0], pieces[hh][1], pieces[hh][3]
                else:
                    dk_s[hh] += pieces[hh][0]
                    dv_s[hh] += pieces[hh][1]
                    dsum_s[hh] += pieces[hh][3]

        def qstep(i, _):
            products(i + 1)
            absorb(i, False)
            return 0

        products(j)
        products(jnp.minimum(j + 1, nq - 1))
        absorb(j, True)
        lax.fori_loop(j + 1, nq - 1, qstep, 0)

        @pl.when(j < nq - 1)
        def _():
            absorb(nq - 1, False)
        for hh in range(hp_n):
            dqkv_ref[1, hh, krows, :] = dk_s[hh]
            dqkv_ref[2, hh, krows, :] = dv_s[hh]
            dcol_ref[krows, :] += _to_lane(-jnp.sum(dsum_s[hh], axis=1, keepdims=True), 2 * HEADS + hp * hp_n + hh)

        @pl.when(j == nq - 1)
        def _():
            for hh in range(hp_n):
                for b in range(nq):
                    rows = slice(b * qb, (b + 1) * qb)
                    dqkv_ref[0, hh, rows, :] = _tall_t(dqt_s[hh, :, rows]) * Q_SCALE

    once = dict(pipeline_mode=pl.Buffered(1))
    full = pl.BlockSpec((hp_n, t, DH), lambda h, j: (h, 0, 0), **once)
    rows8 = pl.BlockSpec((HEADS, t), lambda h, j: (0, 0))
    out = pl.pallas_call(
        body, name="fox_bwd", grid=(HEADS // hp_n, nq),
        in_specs=[pl.BlockSpec((1, hp_n, t, DH), lambda h, j: (0, h, 0, 0), **once),
                  pl.BlockSpec((1, hp_n, qb, DH), lambda h, j: (1, h, j, 0)),
                  pl.BlockSpec((1, hp_n, qb, DH), lambda h, j: (2, h, j, 0)),
                  pl.BlockSpec((t, LANE), lambda h, j: (0, 0), **once), pl.BlockSpec((HEADS, t), lambda h, j: (2, 0)),
                  full, rows8, full] + c_in_specs,
        out_specs=[pl.BlockSpec((3, hp_n, t, DH), lambda h, j: (0, h, 0, 0), **once),
                   pl.BlockSpec((t, LANE), lambda h, j: (0, 0)), rows8] + c_out_specs,
        out_shape=[jax.ShapeDtypeStruct((3, HEADS, t, DH), F32), jax.ShapeDtypeStruct((t, LANE), F32),
                   jax.ShapeDtypeStruct((HEADS, t), F32)] + c_out_shape,
        input_output_aliases={8 + j: 5 + j for j in range(n_c)},
        scratch_shapes=[pltpu.VMEM((hp_n, DH, t), F32), pltpu.VMEM((2, hp_n, qb, qb), F32), pltpu.VMEM((2, hp_n, qb, qb), F32),
                        pltpu.VMEM((2, hp_n, 8, qb), F32), pltpu.VMEM((hp_n, qb, DH), F32), pltpu.VMEM((hp_n, qb, DH), F32),
                        pltpu.VMEM((hp_n, qb, LANE), F32)] + ([c_sem] if carry else []),
        compiler_params=_params(("arbitrary", "arbitrary")),
    )(qkv, qkv, qkv, run, run_t, o, lse, do, *c_in)
    return (out[0], out[1], out[2], _carry_state(out[3:])) if carry else tuple(out)


Z_COL0 = 3 * WIDTH // LANE
FGATE_COL0 = 7 * WIDTH // LANE


def _gdn_post(o, pm, onw):
    t = pm.shape[0]

    def body(o_ref, z_ref, w_ref, m_ref):
        z = z_ref[...]
        sz = z * _sigmoid(z)
        halves = []
        for hh in range(2):
            ov = o_ref[hh]
            n = ov * lax.rsqrt(jnp.mean(ov * ov, axis=-1, keepdims=True) + EPS) * w_ref[...]
            halves.append(n * sz[:, hh * DH:(hh + 1) * DH])
        m_ref[...] = jnp.concatenate(halves, axis=1).astype(m_ref.dtype)

    return pl.pallas_call(
        body, name="gdn_post", grid=(WIDTH // LANE,),
        in_specs=[pl.BlockSpec((2, t, DH), lambda j: (j, 0, 0)), pl.BlockSpec((t, LANE), lambda j: (0, Z_COL0 + j)),
                  pl.BlockSpec((1, DH), lambda j: (0, 0))],
        out_specs=pl.BlockSpec((t, LANE), lambda j: (0, j)),
        out_shape=jax.ShapeDtypeStruct((t, WIDTH), MXU),
        compiler_params=_params(("arbitrary",)),
    )(o, pm, onw)


def _gdn_post_bwd(o, pm, onw, dmix):
    t = pm.shape[0]

    def body(o_ref, z_ref, w_ref, dm_ref, do_ref, dz_ref, dw_ref):
        @pl.when(pl.program_id(0) == 0)
        def _():
            dw_ref[...] = jnp.zeros_like(dw_ref)

        z = z_ref[...]
        sg = _sigmoid(z)
        sz = z * sg
        dsz = sg * (1.0 + z * (1.0 - sg))
        dm = dm_ref[...]
        for hh in range(2):
            cols = slice(hh * DH, (hh + 1) * DH)
            ov = o_ref[hh]
            r = lax.rsqrt(jnp.mean(ov * ov, axis=-1, keepdims=True) + EPS)
            xn = ov * r
            dmh = dm[:, cols]
            dn = dmh * sz[:, cols]
            dz_ref[:, cols] = dmh * (xn * w_ref[...]) * dsz[:, cols]
            dw_ref[...] += jnp.sum(dn * xn, axis=0, keepdims=True)
            g = dn * w_ref[...]
            do_ref[hh] = r * (g - xn * jnp.mean(g * xn, axis=-1, keepdims=True))

    return pl.pallas_call(
        body, name="gdn_post_bwd", grid=(WIDTH // LANE,),
        in_specs=[pl.BlockSpec((2, t, DH), lambda j: (j, 0, 0)), pl.BlockSpec((t, LANE), lambda j: (0, Z_COL0 + j)),
                  pl.BlockSpec((1, DH), lambda j: (0, 0)), pl.BlockSpec((t, LANE), lambda j: (0, j))],
        out_specs=[pl.BlockSpec((2, t, DH), lambda j: (j, 0, 0)), pl.BlockSpec((t, LANE), lambda j: (0, j)),
                   pl.BlockSpec((1, DH), lambda j: (0, 0))],
        out_shape=[jax.ShapeDtypeStruct((HEADS, t, DH), F32), jax.ShapeDtypeStruct((t, WIDTH), F32),
                   jax.ShapeDtypeStruct((1, DH), F32)],
        compiler_params=_params(("arbitrary",)),
    )(o, pm, onw, dmix)


def _fox_post(o, pm):
    t = pm.shape[0]

    def body(o_ref, g_ref, m_ref):
        m_ref[...] = (jnp.concatenate([o_ref[0], o_ref[1]], axis=1) * _sigmoid(g_ref[...])).astype(m_ref.dtype)

    return pl.pallas_call(
        body, name="fox_post", grid=(WIDTH // LANE,),
        in_specs=[pl.BlockSpec((2, t, DH), lambda j: (j, 0, 0)), pl.BlockSpec((t, LANE), lambda j: (0, FGATE_COL0 + j))],
        out_specs=pl.BlockSpec((t, LANE), lambda j: (0, j)),
        out_shape=jax.ShapeDtypeStruct((t, WIDTH), MXU),
        compiler_params=_params(("arbitrary",)),
    )(o, pm)


def _fox_post_bwd(o, pm, dmix):
    t = pm.shape[0]

    def body(o_ref, g_ref, dm_ref, do_ref, dg_ref):
        sg = _sigmoid(g_ref[...])
        dm = dm_ref[...]
        for hh in range(2):
            cols = slice(hh * DH, (hh + 1) * DH)
            do_ref[hh] = dm[:, cols] * sg[:, cols]
            dg_ref[:, cols] = dm[:, cols] * o_ref[hh] * (sg * (1.0 - sg))[:, cols]

    return pl.pallas_call(
        body, name="fox_post_bwd", grid=(WIDTH // LANE,),
        in_specs=[pl.BlockSpec((2, t, DH), lambda j: (j, 0, 0)), pl.BlockSpec((t, LANE), lambda j: (0, FGATE_COL0 + j)),
                  pl.BlockSpec((t, LANE), lambda j: (0, j))],
        out_specs=[pl.BlockSpec((2, t, DH), lambda j: (j, 0, 0)), pl.BlockSpec((t, LANE), lambda j: (0, j))],
        out_shape=[jax.ShapeDtypeStruct((HEADS, t, DH), F32), jax.ShapeDtypeStruct((t, WIDTH), F32)],
        compiler_params=_params(("arbitrary",)),
    )(o, pm, dmix)


def _tail(x, mixg, mixf, tgt, wo, n2w, wg_t, wu_t, wd, fw):
    t, d = x.shape
    dff = wd.shape[0]
    tb = min(TB, t)

    def body(x_ref, mg_ref, mf_ref, t_ref, wo_ref, n2_ref, wg_ref, wu_ref, wd_ref, fw_ref,
             h2_ref, act_ref, dgate_ref, dup_ref, dx3_ref, dx2_ref, dmg_ref, dmf_ref, dn2_ref, dfw_ref, loss_ref):
        @pl.when(pl.program_id(0) == 0)
        def _():
            dn2_ref[...] = jnp.zeros_like(dn2_ref)
            dfw_ref[...] = jnp.zeros_like(dfw_ref)
            loss_ref[...] = jnp.zeros_like(loss_ref)

        x2 = x_ref[...] + _dot(mg_ref[...], wo_ref[0:WIDTH, :]) + _dot(mf_ref[...], wo_ref[WIDTH:2 * WIDTH, :])
        r2 = lax.rsqrt(jnp.mean(x2 * x2, axis=-1, keepdims=True) + EPS)
        xn2 = x2 * r2
        h2 = (xn2 * n2_ref[...]).astype(MXU)
        h2_ref[...] = h2
        gate = _dot_nt(h2, wg_ref[...])
        up = _dot_nt(h2, wu_ref[...])
        sg = _sigmoid(gate)
        sl = gate * sg
        act = (sl * up).astype(MXU)
        act_ref[...] = act
        x3 = x2 + _dot(act, wd_ref[...])
        r3 = lax.rsqrt(jnp.mean(x3 * x3, axis=-1, keepdims=True) + EPS)
        xn3 = x3 * r3
        err = xn3 * fw_ref[...] - t_ref[...]
        loss_ref[...] += 0.5 * jnp.sum(jnp.mean(err * err, axis=-1, keepdims=True), axis=0, keepdims=True)
        dy = err * (1.0 / d)
        dfw_ref[...] += jnp.sum(dy * xn3, axis=0, keepdims=True)
        g3 = dy * fw_ref[...]
        dx3 = r3 * (g3 - xn3 * jnp.mean(g3 * xn3, axis=-1, keepdims=True))
        dx3_ref[...] = dx3.astype(MXU)
        dact = _dot_nt(dx3, wd_ref[...])
        dgate = (dact * up * (sg * (1.0 + gate * (1.0 - sg)))).astype(MXU)
        dup = (dact * sl).astype(MXU)
        dgate_ref[...] = dgate
        dup_ref[...] = dup
        dh2 = _dot(dgate, wg_ref[...]) + _dot(dup, wu_ref[...])
        dn2_ref[...] += jnp.sum(dh2 * xn2, axis=0, keepdims=True)
        g2 = dh2 * n2_ref[...]
        dx2 = dx3 + r2 * (g2 - xn2 * jnp.mean(g2 * xn2, axis=-1, keepdims=True))
        dx2_ref[...] = dx2
        dmg_ref[...] = _dot_nt(dx2, wo_ref[0:WIDTH, :])
        dmf_ref[...] = _dot_nt(dx2, wo_ref[WIDTH:2 * WIDTH, :])

    def tok(n):
        return pl.BlockSpec((tb, n), lambda i: (i, 0))

    acc = pl.BlockSpec((1, d), lambda i: (0, 0))
    sds = jax.ShapeDtypeStruct
    return pl.pallas_call(
        body, name="tail", grid=(t // tb,),
        in_specs=[tok(d), tok(WIDTH), tok(WIDTH), tok(d), _resident(wo.shape), _resident((1, d)),
                  _resident(wg_t.shape), _resident(wu_t.shape), _resident(wd.shape), _resident((1, d))],
        out_specs=[tok(d), tok(dff), tok(dff), tok(dff), tok(d), tok(d), tok(WIDTH), tok(WIDTH), acc, acc,
                   pl.BlockSpec((1, 1), lambda i: (0, 0))],
        out_shape=[sds((t, d), MXU), sds((t, dff), MXU), sds((t, dff), MXU), sds((t, dff), MXU), sds((t, d), MXU),
                   sds((t, d), F32), sds((t, WIDTH), F32), sds((t, WIDTH), F32), sds((1, d), F32), sds((1, d), F32),
                   sds((1, 1), F32)],
        compiler_params=_params(("arbitrary",)),
    )(x, mixg, mixf, tgt, wo, n2w, wg_t, wu_t, wd, fw)


WGRAD_BLOCK = 6 * 2 ** 20


def _wgrads(a_list, b, name, block_bytes=WGRAD_BLOCK):
    t, n = b.shape
    ms = [a.shape[1] for a in a_list]
    bms = [max(bm for bm in range(LANE, m + 1, LANE) if m % bm == 0 and (bm == LANE or t * bm * a.dtype.itemsize <= block_bytes))
           for a, m in zip(a_list, ms)]
    nbs = [m // bm for m, bm in zip(ms, bms)]
    k = len(a_list)
    cast = b.dtype != jnp.dtype(MXU)

    def body(*refs):
        a_refs, b_ref, o_refs = refs[:k], refs[k], refs[k + 1:2 * k + 1]
        i = pl.program_id(0)
        if cast:
            @pl.when(i == 0)
            def _():
                refs[-1][...] = b_ref[...].astype(MXU)
        for a_ref, o_ref, nb in zip(a_refs, o_refs, nbs):
            @pl.when(i < nb)
            def _():
                o_ref[...] = _dot_tn(a_ref[...], refs[-1][...] if cast else b_ref[...]).astype(o_ref.dtype)

    def clamp(nb):
        return lambda i: jnp.minimum(i, nb - 1)

    return pl.pallas_call(
        body, name=name, grid=(max(nbs),),
        in_specs=[pl.BlockSpec((t, bm), lambda i, c=clamp(nb): (0, c(i))) for bm, nb in zip(bms, nbs)] + [_resident((t, n))],
        out_specs=[pl.BlockSpec((bm, n), lambda i, c=clamp(nb): (c(i), 0)) for bm, nb in zip(bms, nbs)],
        out_shape=[jax.ShapeDtypeStruct((m, n), WIRE) for m in ms],
        scratch_shapes=[pltpu.VMEM((t, n), MXU)] if cast else [],
        compiler_params=_params(("arbitrary",)),
    )(*a_list, b)


def _merge_dw_in(d_gdn, d_z, d_fox, d_fg, d_small, shards=False):
    pieces = [d_gdn, d_z, d_small[:2 * HEADS], d_fox, d_fg, d_small[2 * HEADS:3 * HEADS]]
    if not shards:
        return jnp.concatenate(pieces, axis=0)
    n = sum(p.shape[0] for p in pieces) // N_DEV
    out = []
    for dev in range(N_DEV):
        parts, first = [], 0
        for p in pieces:
            lo, hi = max(dev * n, first), min((dev + 1) * n, first + p.shape[0])
            if lo < hi:
                parts.append(p[lo - first:hi - first])
            first += p.shape[0]
        out.append(jnp.concatenate(parts, axis=0))
    return jnp.stack(out)


def _lanes(*pieces):
    v = jnp.concatenate([p.reshape(-1).astype(F32) for p in pieces])
    return jnp.pad(v, (0, LANE - v.shape[0])).reshape(1, LANE)


def _vector_params(p):
    d = p["norm1_w"].size
    gparams = jnp.concatenate([_lanes(jnp.zeros(HEADS), p["gdn_dt_bias"], p["fox_f_bias"]),
                               _lanes(jnp.zeros(HEADS), p["gdn_A_log"]), jnp.zeros((6, LANE), F32)])
    fox_nw = jnp.stack([jnp.tile(p["fox_q_norm_w"].reshape(-1), 2), jnp.tile(p["fox_k_norm_w"].reshape(-1), 2),
                        jnp.ones((LANE,), F32)])
    return dict(n1w=p["norm1_w"].reshape(1, d), n2w=p["norm2_w"].reshape(1, d), fw=p["final_norm_w"].reshape(1, d),
                onw=p["gdn_out_norm_w"].reshape(1, DH), gparams=gparams, fox_nw=fox_nw)


def _mixer_forward(x, vp, w_t, ws_t, conv_w, carry=None):
    h1, pm, ps = _inproj(x, vp["n1w"], w_t, ws_t)
    gates, run, tot, run_t = _gates(ps, vp["gparams"])
    fqkv = _fox_prep(pm, vp["fox_nw"])
    o_fox, lse, *carried = _fox_fwd(fqkv, run, run_t, carry)
    if carry:
        pm, o_fox = lax.optimization_barrier((pm, o_fox))
    mixf = _fox_post(o_fox, pm)
    gqkv = _gdn_prep(pm, conv_w)
    o_gdn, states, inv = _gdn_fwd(gqkv, gates, run, tot, run_t)
    mixg = _gdn_post(o_gdn, pm, vp["onw"])
    return dict(h1=h1, pm=pm, ps=ps, gates=gates, run=run, tot=tot, run_t=run_t, gqkv=gqkv, o_gdn=o_gdn, states=states,
                inv=inv, mixg=mixg, fqkv=fqkv, o_fox=o_fox, lse=lse, mixf=mixf, carried=carried[0] if carried else None)


def _mixer_backward(x, vp, w_t, ws_t, conv_w, f, dx2, dmixg, dmixf, carry=None, scatter_own=False):
    pm = f["pm"]
    do_fox, dfg = _fox_post_bwd(f["o_fox"], pm, dmixf)
    dfqkv, dcol_f, drow_f, *carried = _fox_bwd(f["fqkv"], f["run"], f["run_t"], f["o_fox"], f["lse"], do_fox, carry)
    dfox, dfnw = _fox_prep_bwd(pm, vp["fox_nw"], dfqkv)
    do_gdn, dz, donw = _gdn_post_bwd(f["o_gdn"], pm, vp["onw"], dmixg)
    dgqkv, dcol_g, dtot_g, drow_g = _gdn_bwd(f["gqkv"], f["gates"], f["run"], f["tot"], f["run_t"], f["inv"], f["states"], do_gdn)
    dgdn, dconv = _gdn_prep_bwd(pm, conv_w, dgqkv)
    dps, gsum = _gates_bwd(f["ps"], vp["gparams"], dcol_g, dtot_g, dcol_f, drow_g, drow_f)
    dw_pieces = _wgrads([dgdn, dz, dfox, dfg, dps], f["h1"], "dw_in", block_bytes=WGRAD_BLOCK * 2 // 3)
    dw_in = _merge_dw_in(*dw_pieces)
    own = ("scatter", [_merge_dw_in(*dw_pieces, shards=True), _cut(dconv, 1)]) if scatter_own else None
    grad_x, dn1w, *sent = _inproj_bwd(x, vp["n1w"], dx2, dgdn, dz, dfox, dfg, dps, w_t, ws_t, own)
    small = dict(dn1w=dn1w, gsum=gsum, donw=donw, dfnw=dfnw)
    return (grad_x, dw_in, dconv, small, *carried, *sent)


VECTORS = ("norm1_w", "norm2_w", "final_norm_w", "gdn_A_log", "gdn_dt_bias", "gdn_out_norm_w", "fox_f_bias",
           "fox_q_norm_w", "fox_k_norm_w")
VEC_ROWS = 16
LOSS_ROW = len(VECTORS)


def _pack_vectors(dn1w, dn2w, dfw, gsum, donw, dfnw, loss):
    d = dn1w.shape[1]

    def body(n1_ref, n2_ref, fw_ref, gs_ref, on_ref, fn_ref, loss_ref, o_ref):
        o_ref[...] = jnp.zeros_like(o_ref)
        o_ref[0:1, :] = n1_ref[...]
        o_ref[1:2, :] = n2_ref[...]
        o_ref[2:3, :] = fw_ref[...]
        o_ref[3:4, 0:HEADS] = gs_ref[0:1, 0:HEADS]
        o_ref[4:5, 0:HEADS] = gs_ref[1:2, 0:HEADS]
        o_ref[5:6, 0:DH] = on_ref[...]
        o_ref[6:7, 0:HEADS] = gs_ref[2:3, 0:HEADS]
        for kind in range(2):
            v = fn_ref[kind, 0]
            for j in range(1, fn_ref.shape[1]):
                v = v + fn_ref[kind, j]
            o_ref[7 + kind:8 + kind, 0:DH] = v[:, :DH] + v[:, DH:]
        o_ref[LOSS_ROW:LOSS_ROW + 1, 0:1] = loss_ref[...]

    return pl.pallas_call(body, name="pack_vectors", out_shape=jax.ShapeDtypeStruct((VEC_ROWS, d), F32),
                          compiler_params=_params())(dn1w, dn2w, dfw, gsum, donw, dfnw, loss)


def _late_grads(f, h2, act, dgate, dup, dx3, dx2):
    dw_gate, dw_up = _wgrads([dgate, dup], h2, "dw_gate_up")
    (dw_down,) = _wgrads([act], dx3, "dw_down")
    return {"w_out": jnp.concatenate(_wgrads([f["mixg"], f["mixf"]], dx2, "dw_out"), axis=0),
            "w_ffn_gate": dw_gate, "w_ffn_up": dw_up, "w_ffn_down": dw_down}


def _local_step(x, tgt, p, w_in_t, conv_w, wo, wg_t, wu_t, wd):
    vp = _vector_params(p)
    ws_t = _small_rows(w_in_t)
    f = _mixer_forward(x, vp, w_in_t, ws_t, conv_w)
    (h2, act, dgate, dup, dx3, dx2, dmixg, dmixf, dn2w, dfw, loss) = _tail(
        x, f["mixg"], f["mixf"], tgt, wo, vp["n2w"], wg_t, wu_t, wd, vp["fw"])
    grad_x, dw_in, dconv, small = _mixer_backward(x, vp, w_in_t, ws_t, conv_w, f, dx2, dmixg, dmixf)
    grads = {"w_in": dw_in, "gdn_conv_w": dconv, **_late_grads(f, h2, act, dgate, dup, dx3, dx2)}
    vec = _pack_vectors(small["dn1w"], dn2w, dfw, small["gsum"], small["donw"], small["dfnw"], loss)
    return grad_x[0], grads, vec


def _my_place():
    return lax.axis_index("x"), lax.axis_index("y"), lax.axis_index("c")


def _peers():
    x, y, c = _my_place()
    peers = []
    for k in range(1, N_DEV):
        px = 1 - x if k & 4 else x
        py = 1 - y if k & 2 else y
        pc = 1 - c if k & 1 else c
        peers.append(((px, py, pc), 4 * px + 2 * py + pc))
    return 4 * x + 2 * y + c, peers


def _spread_copies(kind, srcs, lands, send_sems, recv_sems):
    me, peers = _peers()
    kinds = [kind] * len(srcs) if isinstance(kind, str) else kind
    remote, local = [], []
    for i, (kd, src, land) in enumerate(zip(kinds, srcs, lands)):
        for k, (dev, idx) in enumerate(peers):
            remote.append(pltpu.make_async_remote_copy(
                src_ref=src if kd == "gather" else src.at[idx], dst_ref=land.at[me],
                send_sem=send_sems.at[i * (N_DEV - 1) + k], recv_sem=recv_sems.at[i * (N_DEV - 1) + k],
                device_id=dev, device_id_type=MESH))
        local.append((src if kd == "gather" else src.at[me], land.at[me]))
    return remote, local


def _gather_two_level(arrays, name):
    n = len(arrays)

    def body(*refs):
        srcs, lands = refs[:n], refs[n:2 * n]
        send_sems, recv_sems, local_sems = refs[2 * n:]
        x, y, c = _my_place()
        me, sibling = (x, y, c), (x, y, 1 - c)
        chips = [(1 - x, y), (x, 1 - y), (1 - x, 1 - y)]

        def index(p):
            return 4 * p[0] + 2 * p[1] + p[2]

        def copy(i, k, block, to, src=None):
            blk = lands[i].at[index(block)]
            return pltpu.make_async_remote_copy(
                src_ref=blk if src is None else src, dst_ref=blk, send_sem=send_sems.at[7 * i + k],
                recv_sem=recv_sems.at[7 * i + k], device_id=to, device_id_type=MESH)

        mine = [pltpu.make_async_copy(srcs[i], lands[i].at[index(me)], local_sems.at[i]) for i in range(n)]
        for cp in mine:
            cp.start()
        first = []
        for i in range(n):
            first.append(copy(i, 0, me, sibling, src=srcs[i]))
            first += [copy(i, 1 + j, me, (*chip, c), src=srcs[i]) for j, chip in enumerate(chips)]
        for cp in first:
            cp.start()
        passed = []
        for i in range(n):
            for j, chip in enumerate(chips):
                copy(i, 1 + j, (*chip, c), me).wait_recv()
                passed.append(copy(i, 4 + j, (*chip, c), sibling))
                passed[-1].start()
        for i in range(n):
            copy(i, 0, sibling, me).wait_recv()
            for j, chip in enumerate(chips):
                copy(i, 4 + j, (*chip, 1 - c), me).wait_recv()
        for cp in first + passed:
            cp.wait_send()
        for cp in mine:
            cp.wait()

    return pl.pallas_call(
        body, name=name,
        out_shape=[jax.ShapeDtypeStruct((N_DEV,) + a.shape, a.dtype) for a in arrays],
        in_specs=[pl.BlockSpec(memory_space=pl.ANY)] * n, out_specs=[pl.BlockSpec(memory_space=pl.ANY)] * n,
        scratch_shapes=[pltpu.SemaphoreType.DMA((7 * n,)), pltpu.SemaphoreType.DMA((7 * n,)),
                        pltpu.SemaphoreType.DMA((n,))],
    )(*arrays)


def _land_shape(kind, a):
    return (N_DEV,) + a.shape if kind == "gather" else a.shape


HBM = pl.BlockSpec(memory_space=pltpu.HBM)
SEM = pl.BlockSpec(memory_space=pltpu.SEMAPHORE)


def _hbm(a):
    return pltpu.with_memory_space_constraint(a, pltpu.HBM)


def _carry_operands(kind, arrays):
    n = len(arrays)
    kinds = [kind] * n if isinstance(kind, str) else kind
    lands = [lax.empty(_land_shape(kd, a), a.dtype) for kd, a in zip(kinds, arrays)]
    sems = [pltpu.SemaphoreType.DMA((n * (N_DEV - 1),))] * 2
    return ([_hbm(a) for a in list(arrays) + lands], [HBM] * (2 * n),
            sems + [pltpu.HBM(a.shape, a.dtype) for a in list(arrays) + lands], [SEM] * 2 + [HBM] * (2 * n),
            pltpu.SemaphoreType.DMA((n,)))


def _carry_step(kind, in_refs, send_sems, recv_sems, local_sems, step, n_steps):
    n = len(in_refs) // 2
    remote, local = _spread_copies(kind, in_refs[:n], in_refs[n:], send_sems, recv_sems)
    copies = [pltpu.make_async_copy(s, d, local_sems.at[i]) for i, (s, d) in enumerate(local)]
    per_step = -(-len(remote) // n_steps)
    for s in range(-(-len(remote) // per_step)):
        @pl.when(step == s)
        def _():
            for cp in remote[s * per_step:(s + 1) * per_step]:
                cp.start()
            if s == 0:
                for cp in copies:
                    cp.start()

    @pl.when(step == n_steps - 1)
    def _():
        for cp in copies:
            cp.wait()


def _carry_state(extra_out):
    n = (len(extra_out) - 2) // 2
    return list(extra_out[2:2 + n]), list(extra_out[2 + n:]), extra_out[0], extra_out[1]


def _spread_start(arrays, kind, name):
    n = len(arrays)

    def body(*refs):
        srcs, lands = refs[:n], refs[n:2 * n]
        send_sems, recv_sems = refs[2 * n], refs[2 * n + 1]
        token = refs[4 * n + 2]
        local_sems = refs[4 * n + 3]
        remote, local = _spread_copies(kind, srcs, lands, send_sems, recv_sems)
        for cp in remote:
            cp.start()
        copies = [pltpu.make_async_copy(s, d, local_sems.at[i]) for i, (s, d) in enumerate(local)]
        for cp in copies:
            cp.start()
        for cp in copies:
            cp.wait()
        token[...] = jnp.zeros_like(token)

    sems = (pltpu.SemaphoreType.DMA((n * (N_DEV - 1),)),) * 2
    kinds = [kind] * n if isinstance(kind, str) else kind
    lands = [lax.empty(_land_shape(kd, a), a.dtype) for kd, a in zip(kinds, arrays)]
    out = pl.pallas_call(
        body, name=name,
        out_shape=sems + tuple(pltpu.HBM(a.shape, a.dtype) for a in list(arrays) + lands)
        + (jax.ShapeDtypeStruct((8, LANE), F32),),
        in_specs=[HBM] * (2 * n), out_specs=tuple([SEM] * 2 + [HBM] * (2 * n) + [pl.BlockSpec(memory_space=pltpu.VMEM)]),
        input_output_aliases={j: 2 + j for j in range(2 * n)},
        scratch_shapes=[pltpu.SemaphoreType.DMA((n,))],
        compiler_params=pltpu.CompilerParams(has_side_effects=pltpu.SideEffectType.DATAFLOW_SIDE_EFFECTING),
    )(*[_hbm(a) for a in arrays], *[_hbm(a) for a in lands])
    return (list(out[2:2 + n]), list(out[2 + n:2 + 2 * n]), out[0], out[1]), out[-1]


def _spread_wait(state, kind, after, name):
    srcs, lands, send_sems, recv_sems = state
    n = len(srcs)
    after = list(after) if isinstance(after, (list, tuple)) else [after]

    def body(*refs):
        remote, _ = _spread_copies(kind, refs[:n], refs[n:2 * n], refs[2 * n], refs[2 * n + 1])
        for cp in remote:
            cp.wait_send()
        for cp in remote:
            cp.wait_recv()

    out = pl.pallas_call(
        body, name=name,
        out_shape=tuple(pltpu.HBM(a.shape, a.dtype) for a in srcs + lands),
        in_specs=[HBM] * (2 * n) + [SEM, SEM] + [pl.BlockSpec(memory_space=pl.ANY)] * len(after),
        out_specs=tuple([HBM] * (2 * n)),
        input_output_aliases={j: j for j in range(2 * n)},
        compiler_params=pltpu.CompilerParams(has_side_effects=pltpu.SideEffectType.DATAFLOW_SIDE_EFFECTING),
    )(*srcs, *lands, send_sems, recv_sems, *after)
    return list(out[n:])


ADAM_ROWS = 128
ADAM_COLS = 256


def _adam_math(g, w, m, v):
    nm = ADAM_B1 * m + (1.0 - ADAM_B1) * g
    nv = ADAM_B2 * v + (1.0 - ADAM_B2) * (g * g)
    m_hat = nm / (1.0 - ADAM_B1 ** ADAM_STEP)
    v_hat = nv / (1.0 - ADAM_B2 ** ADAM_STEP)
    return -ADAM_LR * (m_hat / (jnp.sqrt(v_hat) + ADAM_EPS) + ADAM_WD * w), nm, nv


def _sum_parts(p_ref):
    g = p_ref[0].astype(F32)
    for s in range(1, N_DEV):
        g = g + p_ref[s].astype(F32)
    return g


def _adam_matrix(parts, w, m, v, name):
    _, r, c = w.shape
    rb = ADAM_ROWS if r % ADAM_ROWS == 0 else r
    cb = ADAM_COLS if (rb == r and c % ADAM_COLS == 0) else c

    def body(p_ref, w_ref, m_ref, v_ref, g_ref, d_ref, nm_ref, nv_ref):
        g = _sum_parts(p_ref)
        g_ref[0] = g
        d_ref[0], nm_ref[0], nv_ref[0] = _adam_math(g, w_ref[0], m_ref[0], v_ref[0])

    blk = pl.BlockSpec((1, rb, cb), lambda i, j: (0, i, j))
    return pl.pallas_call(
        body, name=name, grid=(r // rb, c // cb),
        in_specs=[pl.BlockSpec((N_DEV, rb, cb), lambda i, j: (0, i, j)), blk, blk, blk],
        out_specs=[blk] * 4, out_shape=[jax.ShapeDtypeStruct(w.shape, F32)] * 4,
        compiler_params=_params(("arbitrary", "arbitrary")),
    )(parts, w, m, v)


def _copy_rows(a, name):
    _, r, c = a.shape
    rb = min(TB, r)

    def body(a_ref, o_ref):
        o_ref[...] = a_ref[...]

    blk = pl.BlockSpec((1, rb, c), lambda i: (0, i, 0))
    return pl.pallas_call(body, name=name, grid=(r // rb,), in_specs=[blk], out_specs=blk,
                          out_shape=jax.ShapeDtypeStruct(a.shape, a.dtype), compiler_params=_params(("arbitrary",)))(a)


def _adam_vectors(parts, ws, ms, vs):
    nv = len(ws)

    def body(*refs):
        p_ref = refs[0]
        w_refs, m_refs, v_refs = refs[1:1 + nv], refs[1 + nv:1 + 2 * nv], refs[1 + 2 * nv:1 + 3 * nv]
        outs = refs[1 + 3 * nv:]
        g_all = _sum_parts(p_ref)
        for i in range(nv):
            n = w_refs[i].shape[1]
            g = g_all[i:i + 1, 0:n]
            d, nm, nvv = _adam_math(g, w_refs[i][...], m_refs[i][...], v_refs[i][...])
            outs[i][...] = g
            outs[nv + i][...] = d
            outs[2 * nv + i][...] = nm
            outs[3 * nv + i][...] = nvv
        outs[4 * nv][...] = g_all[LOSS_ROW:LOSS_ROW + 1, 0:1]

    shapes = [jax.ShapeDtypeStruct(a.shape, F32) for a in ws]
    out = pl.pallas_call(body, name="adam_vectors", out_shape=shapes * 4 + [jax.ShapeDtypeStruct((1, 1), F32)],
                         compiler_params=_params())(parts, *ws, *ms, *vs)
    return out[:nv], out[nv:2 * nv], out[2 * nv:3 * nv], out[3 * nv:4 * nv], out[4 * nv]


MATRICES = (("w_in", 1), ("gdn_conv_w", 1), ("w_out", 0), ("w_ffn_gate", 1), ("w_ffn_up", 1), ("w_ffn_down", 0))
TRANSPOSED = ("w_in", "w_ffn_gate", "w_ffn_up")
WEIGHTS = ("norm1_w", "w_in", "gdn_conv_w", "gdn_A_log", "gdn_dt_bias", "gdn_out_norm_w", "fox_f_bias", "fox_q_norm_w",
           "fox_k_norm_w", "w_out", "norm2_w", "w_ffn_gate", "w_ffn_up", "w_ffn_down", "final_norm_w")


def _join(blocks, axis):
    _, r, c = blocks.shape
    if axis == 0:
        return blocks.reshape(N_DEV * r, c)
    return blocks.transpose(1, 0, 2).reshape(r, N_DEV * c)


def _cut(full, axis):
    r, c = full.shape
    if axis == 0:
        return full.reshape(N_DEV, r // N_DEV, c)
    return full.reshape(r, N_DEV, c // N_DEV).transpose(1, 0, 2)


def kernel(x, norm1_w, w_in, gdn_conv_w, gdn_A_log, gdn_dt_bias, gdn_out_norm_w, fox_f_bias, fox_q_norm_w, fox_k_norm_w, w_out, norm2_w, w_ffn_gate, w_ffn_up, w_ffn_down, final_norm_w, loss_target, m_norm1_w, m_w_in, m_gdn_conv_w, m_gdn_A_log, m_gdn_dt_bias, m_gdn_out_norm_w, m_fox_f_bias, m_fox_q_norm_w, m_fox_k_norm_w, m_w_out, m_norm2_w, m_w_ffn_gate, m_w_ffn_up, m_w_ffn_down, m_final_norm_w, v_norm1_w, v_w_in, v_gdn_conv_w, v_gdn_A_log, v_gdn_dt_bias, v_gdn_out_norm_w, v_fox_f_bias, v_fox_q_norm_w, v_fox_k_norm_w, v_w_out, v_norm2_w, v_w_ffn_gate, v_w_ffn_up, v_w_ffn_down, v_final_norm_w):
    w = dict(norm1_w=norm1_w, w_in=w_in, gdn_conv_w=gdn_conv_w, gdn_A_log=gdn_A_log, gdn_dt_bias=gdn_dt_bias,
             gdn_out_norm_w=gdn_out_norm_w, fox_f_bias=fox_f_bias, fox_q_norm_w=fox_q_norm_w, fox_k_norm_w=fox_k_norm_w,
             w_out=w_out, norm2_w=norm2_w, w_ffn_gate=w_ffn_gate, w_ffn_up=w_ffn_up, w_ffn_down=w_ffn_down,
             final_norm_w=final_norm_w)
    m = dict(norm1_w=m_norm1_w, w_in=m_w_in, gdn_conv_w=m_gdn_conv_w, gdn_A_log=m_gdn_A_log, gdn_dt_bias=m_gdn_dt_bias,
             gdn_out_norm_w=m_gdn_out_norm_w, fox_f_bias=m_fox_f_bias, fox_q_norm_w=m_fox_q_norm_w,
             fox_k_norm_w=m_fox_k_norm_w, w_out=m_w_out, norm2_w=m_norm2_w, w_ffn_gate=m_w_ffn_gate,
             w_ffn_up=m_w_ffn_up, w_ffn_down=m_w_ffn_down, final_norm_w=m_final_norm_w)
    v = dict(norm1_w=v_norm1_w, w_in=v_w_in, gdn_conv_w=v_gdn_conv_w, gdn_A_log=v_gdn_A_log, gdn_dt_bias=v_gdn_dt_bias,
             gdn_out_norm_w=v_gdn_out_norm_w, fox_f_bias=v_fox_f_bias, fox_q_norm_w=v_fox_q_norm_w,
             fox_k_norm_w=v_fox_k_norm_w, w_out=v_w_out, norm2_w=v_norm2_w, w_ffn_gate=v_w_ffn_gate,
             w_ffn_up=v_w_ffn_up, w_ffn_down=v_w_ffn_down, final_norm_w=v_final_norm_w)
    late = ("w_out", "w_ffn_gate", "w_ffn_up", "w_ffn_down")
    xs, tgt = x[0], loss_target[0]
    vp = _vector_params({n: w[n] for n in VECTORS})

    def rows_of(d, n):
        return d[n].transpose(0, 2, 1) if n in TRANSPOSED else d[n]

    wr, mr, vr = ({n: rows_of(d, n) for n, _ in MATRICES} for d in (w, m, v))

    w_in_blocks, conv_blocks = _gather_two_level([wr["w_in"][0].astype(WIRE), w["gdn_conv_w"][0]], "gather_in")
    w_t = _join(w_in_blocks, 0)
    conv_w = _join(conv_blocks, 1)
    f = _mixer_forward(xs, vp, w_t, _small_rows(w_t), conv_w, ("gather", [wr[n][0].astype(WIRE) for n in late]))
    full = {n: _join(b, 0) for n, b in zip(late, _spread_wait(f["carried"], "gather", f["mixg"], "gather_late_wait"))}

    (h2, act, dgate, dup, dx3, dx2, dmixg, dmixf, dn2w, dfw, loss) = _tail(
        xs, f["mixg"], f["mixf"], tgt, full["w_out"], vp["n2w"], full["w_ffn_gate"], full["w_ffn_up"],
        full["w_ffn_down"], vp["fw"])
    dlate = _late_grads(f, h2, act, dgate, dup, dx3, dx2)

    grad_x, dw_in, dconv, small, state_grads, state_own = _mixer_backward(
        xs, vp, w_t, _small_rows(w_t), conv_w, f, dx2, dmixg, dmixf, ("scatter", [_cut(dlate[n], 0) for n in late]),
        scatter_own=True)
    vec = _pack_vectors(small["dn1w"], dn2w, dfw, small["gsum"], small["donw"], small["dfnw"], loss)

    state_vec, token = _spread_start([vec], "gather", "vectors_start")
    parts = dict(zip(late, _spread_wait(state_grads, "scatter", token, "grads_late_wait")))
    results = [{}, {}, {}, {}]

    updated = {}

    def update(n):
        out = _adam_matrix(parts[n], wr[n], mr[n], vr[n], "adam_" + n)
        updated[n] = out[0]
        for d, a in zip(results, out):
            d[n] = a.transpose(0, 2, 1) if n in TRANSPOSED else a

    for n in late:
        update(n)
    grad_x = _copy_rows(grad_x, "grad_x_result")
    ready = [updated[n] for n in late] + [d["w_in"] for d in (wr, mr, vr)] + [grad_x]
    (parts_vec,) = _spread_wait(state_vec, "gather", ready, "vectors_wait")
    row = lambda a: a.reshape(1, -1)
    *vec_out, total_loss = _adam_vectors(parts_vec, [row(w[n]) for n in VECTORS], [row(m[n]) for n in VECTORS],
                                         [row(v[n]) for n in VECTORS])
    parts["w_in"], parts["gdn_conv_w"] = _spread_wait(state_own, "scatter", parts_vec, "own_wait")
    update("w_in")
    update("gdn_conv_w")
    for d, arrs in zip(results, vec_out):
        for n, a in zip(VECTORS, arrs):
            d[n] = a.reshape(w[n].shape)
    return (total_loss[0, 0], grad_x, *[d[n] for d in results for n in WEIGHTS])
```

```python
import functools

import jax
import jax.numpy as jnp
from jax import lax
from jax.experimental import pallas as pl
from jax.experimental.pallas import tpu as pltpu

F32 = jnp.float32
MXU = jnp.bfloat16
WIRE = jnp.bfloat16
HI = lax.Precision.HIGHEST
EPS = 1e-6

N_DEV = 8
HEADS = 8
DH = 64
WIDTH = HEADS * DH
CHUNK = 64
LANE = 128
ROW_ALIGN = 16
TB = 256
QB = 256
VMEM_LIMIT = 60 * 1024 * 1024

ADAM_LR = 0.001
ADAM_B1 = 0.9
ADAM_B2 = 0.999
ADAM_EPS = 1e-08
ADAM_WD = 0.01
ADAM_STEP = 10

MESH = pl.DeviceIdType.MESH


def _params(sem=None):
    return pltpu.CompilerParams(dimension_semantics=sem, vmem_limit_bytes=VMEM_LIMIT)


def _resident(shape):
    n = len(shape)
    return pl.BlockSpec(shape, lambda *_: (0,) * n, pipeline_mode=pl.Buffered(1))


def _dot(a, b):
    return jnp.dot(a.astype(MXU), b.astype(MXU), preferred_element_type=F32)


def _dot_nt(a, b):
    return lax.dot_general(a.astype(MXU), b.astype(MXU), (((1,), (1,)), ((), ())), preferred_element_type=F32)


def _dot_tn(a, b):
    return lax.dot_general(a.astype(MXU), b.astype(MXU), (((0,), (0,)), ((), ())), preferred_element_type=F32)


def _hdot(a, b):
    return jnp.dot(a, b, precision=HI, preferred_element_type=F32)


def _hdot_nt(a, b):
    return lax.dot_general(a, b, (((1,), (1,)), ((), ())), precision=HI, preferred_element_type=F32)


def _hdot_tn(a, b):
    return lax.dot_general(a, b, (((0,), (0,)), ((), ())), precision=HI, preferred_element_type=F32)


def _sigmoid(x):
    return 0.5 * jnp.tanh(0.5 * x) + 0.5


def _softplus(x):
    return jnp.maximum(x, 0.0) + jnp.log(1.0 + jnp.exp(-jnp.abs(x)))


def _head_sum_matrix():
    ri = lax.broadcasted_iota(jnp.int32, (LANE, LANE), 0) // DH
    ci = lax.broadcasted_iota(jnp.int32, (LANE, LANE), 1) // DH
    return (ri == ci).astype(F32)


def _group_sum(a, ones_matrix):
    hi = a.astype(jnp.bfloat16)
    lo = (a - hi.astype(F32)).astype(jnp.bfloat16)
    m = ones_matrix.astype(jnp.bfloat16)
    return jnp.dot(hi, m, preferred_element_type=F32) + jnp.dot(lo, m, preferred_element_type=F32)


def _shift_down(x, s):
    return pltpu.roll(x, s, 0)


def _shift_up(x, s):
    return pltpu.roll(x, x.shape[0] - s, 0)


ROWS_A = 4 * WIDTH
ROWS_B = ROWS_A + 2 * HEADS
ROWS_C = ROWS_B + 4 * WIDTH


def _small_rows(w_t):
    return jnp.concatenate([w_t[ROWS_A:ROWS_B], w_t[ROWS_C:], jnp.zeros((LANE - 3 * HEADS, w_t.shape[1]), w_t.dtype)])


def _inproj(x, n1w, w_t, ws_t):
    t, d = x.shape
    tb = min(TB, t)

    def body(x_ref, nw_ref, wt_ref, ws_ref, h_ref, pm_ref, ps_ref):
        xv = x_ref[...]
        r = lax.rsqrt(jnp.mean(xv * xv, axis=-1, keepdims=True) + EPS)
        h = (xv * r * nw_ref[...]).astype(MXU)
        h_ref[...] = h
        pm_ref[:, 0:ROWS_A] = _dot_nt(h, wt_ref[0:ROWS_A, :])
        pm_ref[:, ROWS_A:2 * ROWS_A] = _dot_nt(h, wt_ref[ROWS_B:ROWS_C, :])
        ps_ref[...] = _dot_nt(h, ws_ref[...])

    return pl.pallas_call(
        body, name="inproj", grid=(t // tb,),
        in_specs=[pl.BlockSpec((tb, d), lambda i: (i, 0)), _resident((1, d)), _resident(w_t.shape), _resident((LANE, d))],
        out_specs=[pl.BlockSpec((tb, d), lambda i: (i, 0)), pl.BlockSpec((tb, 2 * ROWS_A), lambda i: (i, 0)),
                   pl.BlockSpec((tb, LANE), lambda i: (i, 0))],
        out_shape=[jax.ShapeDtypeStruct((t, d), MXU), jax.ShapeDtypeStruct((t, 2 * ROWS_A), F32),
                   jax.ShapeDtypeStruct((t, LANE), F32)],
        compiler_params=_params(("arbitrary",)),
    )(x, n1w, w_t, ws_t)


def _inproj_bwd(x, n1w, dx2, dgdn, dz, dfox, dfg, dps, w_t, ws_t, carry=None):
    t, d = x.shape
    tb = min(TB, t)
    w3 = 3 * WIDTH
    c_in, c_in_specs, c_out_shape, c_out_specs, c_sem = _carry_operands(*carry) if carry else ([], [], [], [], None)
    n_c = len(c_in)

    def body(*refs):
        x_ref, nw_ref, dx2_ref, dgdn_ref, dz_ref, dfox_ref, dfg_ref, dps_ref, wm_ref, ws_ref = refs[:10]
        gx_ref, dnw_ref = refs[10 + n_c:12 + n_c]
        if carry:
            _carry_step(carry[0], refs[10:10 + n_c], refs[12 + n_c], refs[13 + n_c], refs[-1], pl.program_id(0), t // tb)
        dh = _dot(dgdn_ref[...], wm_ref[0:w3, :])
        dh += _dot(dz_ref[...], wm_ref[w3:ROWS_A, :])
        dh += _dot(dfox_ref[...], wm_ref[ROWS_B:ROWS_B + w3, :])
        dh += _dot(dfg_ref[...], wm_ref[ROWS_B + w3:ROWS_C, :])
        dh += _dot(dps_ref[...], ws_ref[...])
        xv = x_ref[...]
        r = lax.rsqrt(jnp.mean(xv * xv, axis=-1, keepdims=True) + EPS)
        xn = xv * r

        @pl.when(pl.program_id(0) == 0)
        def _():
            dnw_ref[...] = jnp.zeros_like(dnw_ref)

        dnw_ref[...] += jnp.sum(dh * xn, axis=0, keepdims=True)
        g = dh * nw_ref[...]
        gx_ref[0] = dx2_ref[...] + r * (g - xn * jnp.mean(g * xn, axis=-1, keepdims=True))

    def tok(n):
        return pl.BlockSpec((tb, n), lambda i: (i, 0))

    out = pl.pallas_call(
        body, name="inproj_bwd", grid=(t // tb,),
        in_specs=[tok(d), _resident((1, d)), tok(d), tok(w3), tok(WIDTH), tok(w3), tok(WIDTH), tok(LANE),
                  _resident(w_t.shape), _resident(ws_t.shape)] + c_in_specs,
        out_specs=[pl.BlockSpec((1, tb, d), lambda i: (0, i, 0)), pl.BlockSpec((1, d), lambda i: (0, 0))] + c_out_specs,
        out_shape=[jax.ShapeDtypeStruct((1, t, d), F32), jax.ShapeDtypeStruct((1, d), F32)] + c_out_shape,
        input_output_aliases={10 + j: 4 + j for j in range(n_c)},
        scratch_shapes=[c_sem] if carry else [],
        compiler_params=_params(("arbitrary",)),
    )(x, n1w, dx2, dgdn, dz, dfox, dfg, dps, w_t, ws_t, *c_in)
    return (out[0], out[1], _carry_state(out[2:])) if carry else tuple(out)


def _gate_lanes(shape):
    lane = lax.broadcasted_iota(jnp.int32, shape, 1)
    return lane < HEADS, (lane >= HEADS) & (lane < 2 * HEADS), (lane >= 2 * HEADS) & (lane < 3 * HEADS)


def _block_masks():
    ri = lax.broadcasted_iota(jnp.int32, (LANE, LANE), 0)
    ci = lax.broadcasted_iota(jnp.int32, (LANE, LANE), 1)
    same = (ri // CHUNK) == (ci // CHUNK)
    return ((ri >= ci).astype(F32), (ri <= ci).astype(F32), (same & (ri >= ci)).astype(F32),
            (same & (ri <= ci)).astype(F32), same.astype(F32))


def _gates(ps, gparams):
    t = ps.shape[0]
    nb = t // LANE

    def body(ps_ref, gp_ref, out_ref, run_ref, tot_ref, runt_ref):
        p = ps_ref[...]
        is_b, is_a, is_f = _gate_lanes(p.shape)
        z = p + gp_ref[0:1, :]
        neg_exp_a = -jnp.exp(gp_ref[1:2, :])
        glog = neg_exp_a * _softplus(z)
        logf = -_softplus(-z)
        out_ref[...] = jnp.where(is_b, _sigmoid(p), jnp.where(is_a, glog, jnp.where(is_f, logf, 0.0)))
        tril, _, tril_c, _, same_c = _block_masks()
        off = jnp.zeros((1, LANE), F32)
        for b in range(nb):
            rows = slice(b * LANE, (b + 1) * LANE)
            blk = out_ref[rows, :]
            ga = jnp.where(is_a[:LANE], blk, 0.0)
            fb = _hdot(tril, jnp.where(is_f[:LANE], blk, 0.0)) + off
            run = fb + _hdot(tril_c, ga)
            run_ref[rows, :] = run
            runt_ref[:, rows] = run.T
            tot_ref[rows, :] = _hdot(same_c, ga)
            off = fb[LANE - 1:LANE, :]

    return pl.pallas_call(
        body, name="gates",
        out_shape=[jax.ShapeDtypeStruct((t, LANE), F32)] * 3 + [jax.ShapeDtypeStruct((LANE, t), F32)],
        compiler_params=_params(),
    )(ps, gparams)


def _gates_bwd(ps, gparams, dcol_g, dtot_g, dcol_f, drow_g, drow_f):
    t = ps.shape[0]
    nb = t // LANE

    def body(ps_ref, gp_ref, dcg_ref, dtg_ref, dcf_ref, drg_ref, drf_ref, dps_ref, sums_ref, dl_ref, d0_ref, tr_ref):
        p = ps_ref[...]
        is_b, is_a, is_f = _gate_lanes(p.shape)
        _, triu, _, triu_c, same_c = _block_masks()
        tr_ref[...] = jnp.zeros_like(tr_ref)
        off = jnp.zeros((1, LANE), F32)
        for b in reversed(range(nb)):
            rows = slice(b * LANE, (b + 1) * LANE)
            tr_ref[HEADS:2 * HEADS, :] = drg_ref[:, rows]
            tr_ref[2 * HEADS:3 * HEADS, :] = drf_ref[:, rows]
            d = dcg_ref[rows, :] + dcf_ref[rows, :] + tr_ref[...].T
            d0_ref[rows, :] = d
            dlf = _hdot(triu, jnp.where(is_f[:LANE], d, 0.0)) + off
            dla = (_hdot(triu_c, jnp.where(is_a[:LANE], d, 0.0))
                   + _hdot(same_c, jnp.where(is_a[:LANE], dtg_ref[rows, :], 0.0)))
            dl_ref[rows, :] = dlf + dla
            off = dlf[0:1, :]
        z = p + gp_ref[0:1, :]
        neg_exp_a = -jnp.exp(gp_ref[1:2, :])
        sb = _sigmoid(p)
        glog = neg_exp_a * _softplus(z)
        dl = dl_ref[...]
        dp = jnp.where(is_b, d0_ref[...] * sb * (1.0 - sb),
                       jnp.where(is_a, dl * neg_exp_a * _sigmoid(z), jnp.where(is_f, dl * _sigmoid(-z), 0.0)))
        dps_ref[...] = dp
        s_a = jnp.sum(jnp.where(is_a, dl * glog, 0.0), axis=0, keepdims=True)
        s_p = jnp.sum(jnp.where(is_b, 0.0, dp), axis=0, keepdims=True)
        row = lax.broadcasted_iota(jnp.int32, (8, LANE), 0)
        from_a = pltpu.roll(jnp.where(row == 0, s_a, jnp.where(row == 1, s_p, 0.0)), LANE - HEADS, 1)
        from_f = pltpu.roll(jnp.where(row == 2, s_p, 0.0), LANE - 2 * HEADS, 1)
        lane = lax.broadcasted_iota(jnp.int32, (8, LANE), 1)
        sums_ref[...] = jnp.where(lane < HEADS, from_a + from_f, 0.0)

    return pl.pallas_call(
        body, name="gates_bwd",
        out_shape=[jax.ShapeDtypeStruct((t, LANE), F32), jax.ShapeDtypeStruct((8, LANE), F32)],
        scratch_shapes=[pltpu.VMEM((t, LANE), F32), pltpu.VMEM((t, LANE), F32), pltpu.VMEM((LANE, LANE), F32)],
        compiler_params=_params(),
    )(ps, gparams, dcol_g, dtot_g, dcol_f, drow_g, drow_f)


def _conv(xv, w):
    acc = w[3:4, :] * xv
    for s in range(1, 4):
        acc += w[3 - s:4 - s, :] * _shift_down(xv, s)
    return acc


PREP_ROWS = 256
HALO = 8
PREP_UNROLL = 2


def _tile_loop(n, tile, init):
    if n % PREP_UNROLL:
        return lax.fori_loop(0, n, tile, init)

    def trip(g, carry):
        for u in range(PREP_UNROLL):
            carry = tile(g * PREP_UNROLL + u, carry)
        return carry

    return lax.fori_loop(0, n // PREP_UNROLL, trip, init)


def _tile_rows(r):
    return pl.ds(pl.multiple_of(r * PREP_ROWS, PREP_ROWS), PREP_ROWS)


def _gdn_prep(pm, conv_w):
    t = pm.shape[0]
    nj = WIDTH // LANE
    win = PREP_ROWS + HALO

    def body(x_ref, w_ref, o_ref, xp_ref):
        kind = pl.program_id(0)
        xp_ref[0:HALO, :] = jnp.zeros((HALO, LANE), F32)
        xp_ref[HALO:, :] = x_ref[...]
        w = w_ref[...]
        hs = _head_sum_matrix()

        def tile(r, _, normed):
            xw = xp_ref[pl.ds(pl.multiple_of(r * PREP_ROWS, PREP_ROWS), win), :]
            acc = _conv(xw, w)[HALO:]
            out = acc * _sigmoid(acc)
            if normed:
                out = out * lax.rsqrt(_group_sum(out * out, hs) + EPS)
            o_ref[0, 0, _tile_rows(r), :] = out[:, :DH]
            o_ref[0, 1, _tile_rows(r), :] = out[:, DH:]
            return 0

        @pl.when(kind < 2)
        def _():
            _tile_loop(t // PREP_ROWS, functools.partial(tile, normed=True), 0)

        @pl.when(kind == 2)
        def _():
            _tile_loop(t // PREP_ROWS, functools.partial(tile, normed=False), 0)

    return pl.pallas_call(
        body, name="gdn_prep", grid=(3, nj),
        in_specs=[pl.BlockSpec((t, LANE), lambda i, j: (0, i * nj + j)),
                  pl.BlockSpec((4, LANE), lambda i, j: (0, i * nj + j))],
        out_specs=pl.BlockSpec((1, 2, t, DH), lambda i, j: (i, j, 0, 0)),
        out_shape=jax.ShapeDtypeStruct((3, HEADS, t, DH), F32),
        scratch_shapes=[pltpu.VMEM((t + HALO, LANE), F32)],
        compiler_params=_params(("arbitrary", "arbitrary")),
    )(pm, conv_w)


def _gdn_prep_bwd(pm, conv_w, dqkv):
    t = pm.shape[0]
    nj = WIDTH // LANE
    win = PREP_ROWS + 2 * HALO

    def body(x_ref, w_ref, d_ref, dx_ref, dw_ref, xp_ref, dp_ref):
        kind = pl.program_id(0)
        zeros = jnp.zeros((HALO, LANE), F32)
        for ref in (xp_ref, dp_ref):
            ref[0:HALO, :] = zeros
            ref[HALO + t:, :] = zeros
        xp_ref[HALO:HALO + t, :] = x_ref[...]
        dp_ref[HALO:HALO + t, 0:DH] = d_ref[0, 0]
        dp_ref[HALO:HALO + t, DH:] = d_ref[0, 1]
        w = w_ref[...]
        hs = _head_sum_matrix()
        rows = lax.broadcasted_iota(jnp.int32, (win, LANE), 0)
        in_tile = (rows >= HALO) & (rows < HALO + PREP_ROWS)

        def tile(r, dw, normed):
            start = pl.multiple_of(r * PREP_ROWS, PREP_ROWS)
            xw = xp_ref[pl.ds(start, win), :]
            dy = dp_ref[pl.ds(start, win), :]
            acc = _conv(xw, w)
            sg = _sigmoid(acc)
            if normed:
                y = acc * sg
                rn = lax.rsqrt(_group_sum(y * y, hs) + EPS)
                yn = y * rn
                dy = rn * (dy - yn * _group_sum(dy * yn, hs))
            dacc = dy * sg * (1.0 + acc * (1.0 - sg))
            dx = w[3:4, :] * dacc
            for s in range(1, 4):
                dx += w[3 - s:4 - s, :] * _shift_up(dacc, s)
            dx_ref[_tile_rows(r), :] = dx[HALO:HALO + PREP_ROWS].astype(dx_ref.dtype)
            dm = jnp.where(in_tile, dacc, 0.0)
            return tuple(dw[i] + jnp.sum(dm * (xw if i == 3 else _shift_down(xw, 3 - i)), axis=0, keepdims=True)
                         for i in range(4))

        def run(normed):
            dw = _tile_loop(t // PREP_ROWS, functools.partial(tile, normed=normed),
                            tuple(jnp.zeros((1, LANE), F32) for _ in range(4)))
            for i in range(4):
                dw_ref[i:i + 1, :] = dw[i]

        pl.when(kind < 2)(functools.partial(run, True))
        pl.when(kind == 2)(functools.partial(run, False))

    return pl.pallas_call(
        body, name="gdn_prep_bwd", grid=(3, nj),
        in_specs=[pl.BlockSpec((t, LANE), lambda i, j: (0, i * nj + j)),
                  pl.BlockSpec((4, LANE), lambda i, j: (0, i * nj + j)),
                  pl.BlockSpec((1, 2, t, DH), lambda i, j: (i, j, 0, 0))],
        out_specs=[pl.BlockSpec((t, LANE), lambda i, j: (0, i * nj + j)),
                   pl.BlockSpec((4, LANE), lambda i, j: (0, i * nj + j))],
        out_shape=[jax.ShapeDtypeStruct((t, 3 * WIDTH), MXU), jax.ShapeDtypeStruct((4, 3 * WIDTH), F32)],
        scratch_shapes=[pltpu.VMEM((t + 2 * HALO, LANE), F32), pltpu.VMEM((t + 2 * HALO, LANE), F32)],
        compiler_params=_params(("arbitrary", "arbitrary")),
    )(pm, conv_w, dqkv)


FOX_COL0 = 4 * WIDTH // LANE


def _fox_prep(pm, nw):
    t = pm.shape[0]
    nj = WIDTH // LANE

    def body(x_ref, w_ref, o_ref):
        kind = pl.program_id(0)
        hs = _head_sum_matrix()
        wk = w_ref[pl.ds(kind, 1), :]

        def tile(r, _, normed):
            out = x_ref[_tile_rows(r), :]
            if normed:
                out = out * lax.rsqrt(_group_sum(out * out, hs) * (1.0 / DH) + EPS) * wk
            o_ref[0, 0, _tile_rows(r), :] = out[:, :DH]
            o_ref[0, 1, _tile_rows(r), :] = out[:, DH:]
            return 0

        @pl.when(kind < 2)
        def _():
            _tile_loop(t // PREP_ROWS, functools.partial(tile, normed=True), 0)

        @pl.when(kind == 2)
        def _():
            _tile_loop(t // PREP_ROWS, functools.partial(tile, normed=False), 0)

    return pl.pallas_call(
        body, name="fox_prep", grid=(3, nj),
        in_specs=[pl.BlockSpec((t, LANE), lambda i, j: (0, FOX_COL0 + i * nj + j)),
                  pl.BlockSpec((3, LANE), lambda i, j: (0, 0))],
        out_specs=pl.BlockSpec((1, 2, t, DH), lambda i, j: (i, j, 0, 0)),
        out_shape=jax.ShapeDtypeStruct((3, HEADS, t, DH), F32),
        compiler_params=_params(("arbitrary", "arbitrary")),
    )(pm, nw)


def _fox_prep_bwd(pm, nw, dqkv):
    t = pm.shape[0]
    nj = WIDTH // LANE

    def body(x_ref, w_ref, d_ref, dx_ref, dw_ref):
        kind = pl.program_id(0)
        hs = _head_sum_matrix()
        wk = w_ref[pl.ds(kind, 1), :]

        def tile(r, dw):
            xv = x_ref[_tile_rows(r), :]
            rn = lax.rsqrt(_group_sum(xv * xv, hs) * (1.0 / DH) + EPS)
            xn = xv * rn
            d = jnp.concatenate([d_ref[0, 0, _tile_rows(r), :], d_ref[0, 1, _tile_rows(r), :]], axis=1)
            g = d * wk
            dx_ref[_tile_rows(r), :] = (rn * (g - xn * _group_sum(g * xn, hs) * (1.0 / DH))).astype(dx_ref.dtype)
            return dw + jnp.sum(d * xn, axis=0, keepdims=True)

        def copy_tile(r, _):
            dx_ref[_tile_rows(r), :] = jnp.concatenate([d_ref[0, 0, _tile_rows(r), :], d_ref[0, 1, _tile_rows(r), :]],
                                                       axis=1).astype(dx_ref.dtype)
            return 0

        @pl.when(kind < 2)
        def _():
            dw_ref[0, 0] = _tile_loop(t // PREP_ROWS, tile, jnp.zeros((1, LANE), F32))

        @pl.when(kind == 2)
        def _():
            _tile_loop(t // PREP_ROWS, copy_tile, 0)
            dw_ref[0, 0] = jnp.zeros((1, LANE), F32)

    return pl.pallas_call(
        body, name="fox_prep_bwd", grid=(3, nj),
        in_specs=[pl.BlockSpec((t, LANE), lambda i, j: (0, FOX_COL0 + i * nj + j)),
                  pl.BlockSpec((3, LANE), lambda i, j: (0, 0)),
                  pl.BlockSpec((1, 2, t, DH), lambda i, j: (i, j, 0, 0))],
        out_specs=[pl.BlockSpec((t, LANE), lambda i, j: (0, i * nj + j)),
                   pl.BlockSpec((1, 1, 1, LANE), lambda i, j: (i, j, 0, 0))],
        out_shape=[jax.ShapeDtypeStruct((t, 3 * WIDTH), MXU), jax.ShapeDtypeStruct((3, nj, 1, LANE), F32)],
        compiler_params=_params(("arbitrary", "arbitrary")),
    )(pm, nw, dqkv)


SC = 256
CPS = SC // CHUNK
GDN_HP_FWD = 4
GDN_HP_BWD = 2
Q_SCALE = DH ** -0.5


def _sc_masks():
    ri = lax.broadcasted_iota(jnp.int32, (SC, SC), 0)
    ci = lax.broadcasted_iota(jnp.int32, (SC, SC), 1)
    same = (ri // CHUNK) == (ci // CHUNK)
    return same & (ri >= ci), same & (ri > ci), ri == ci


def _unit_lower_inverses(ms, eye):
    invs = [jnp.where(eye, 1.0, 0.0) + m for m in ms]
    ms = [_dot(m, m) for m in ms]
    for _ in range(4):
        both = [_dot(jnp.concatenate([inv, m], axis=0), m) for inv, m in zip(invs, ms)]
        invs = [inv + b[:SC] for inv, b in zip(invs, both)]
        ms = [b[SC:] for b in both]
    return [inv + _dot(inv, m) for inv, m in zip(invs, ms)]


def _lane_col(blk, lane_idx):
    lane = lax.broadcasted_iota(jnp.int32, blk.shape, 1)
    return jnp.sum(jnp.where(lane == lane_idx, blk, 0.0), axis=1, keepdims=True)


def _to_lane(col, lane_idx):
    lane = lax.broadcasted_iota(jnp.int32, (col.shape[0], LANE), 1)
    return jnp.where(lane == lane_idx, col, 0.0)


def _gdn_columns(gates_ref, run_ref, tot_ref, runt_ref, rows, h):
    return (_lane_col(gates_ref[rows, :], h), _lane_col(run_ref[rows, :], HEADS + h),
            _lane_col(tot_ref[rows, :], HEADS + h), runt_ref[pl.ds(h, 1), rows])


def _gdn_local(q, k, beta, gc, gl, grow, causal, with_kk=True):
    decay = jnp.exp(jnp.where(causal, gc - grow, -1e30))
    egc = jnp.exp(gc)
    ekd = jnp.exp(gl - gc)
    qs = q * Q_SCALE
    kb = k * beta
    if with_kk:
        both = _dot_nt(jnp.concatenate([kb, qs], axis=0), k)
        kk, qk = both[:SC], both[SC:]
    else:
        kk, qk = None, _dot_nt(qs, k)
    return beta, gl, decay, egc, ekd, qs, kb, kk, qk, jnp.where(causal, qk * decay, 0.0)


def _chunk_rows(c):
    return pl.ds(c * CHUNK if isinstance(c, int) else pl.multiple_of(c * CHUNK, CHUNK), CHUNK)


def _sc_rows(b):
    return pl.ds(b * SC if isinstance(b, int) else pl.multiple_of(b * SC, SC), SC)


def _gdn_fwd(qkv, gates, run, tot, run_t):
    t = qkv.shape[2]
    nc = t // CHUNK
    nsc = t // SC

    hp_n = GDN_HP_FWD
    heads = range(hp_n)

    def body(qkv_ref, gates_ref, run_ref, tot_ref, runt_ref, o_ref, st_ref, inv_ref, kc_s, qc_s, g_s, au_s):
        hp = pl.program_id(0)
        causal, strict, eye = _sc_masks()

        def local(b, _):
            rows = _sc_rows(b)
            loc = [_gdn_local(qkv_ref[0, hh, rows, :], qkv_ref[1, hh, rows, :],
                              *_gdn_columns(gates_ref, run_ref, tot_ref, runt_ref, rows, hp * hp_n + hh), causal)
                   for hh in heads]
            invs = _unit_lower_inverses([-jnp.where(strict, l[7] * l[2], 0.0) for l in loc], eye)
            uws = []
            for hh in heads:
                beta, _, _, egc, _, _, kb, _, _, _ = loc[hh]
                inv_ref[hh, rows, :] = invs[hh].astype(inv_ref.dtype)
                uws.append(_dot(invs[hh], jnp.concatenate([qkv_ref[2, hh, rows, :] * beta, kb * egc], axis=1)))
            for hh in heads:
                _, _, _, egc, ekd, qs, _, _, _, attn = loc[hh]
                auw = _dot(attn, uws[hh])
                g_s[hh, rows, :] = qs * egc - auw[:, DH:]
                au_s[hh, rows, :] = auw[:, :DH]
                kd = qkv_ref[1, hh, rows, :] * ekd
                for j in range(CPS):
                    sl = slice(j * CHUNK, (j + 1) * CHUNK)
                    both = _dot_tn(kd[sl], uws[hh][sl])
                    kc_s[hh, b * CPS + j] = both[:, DH:]
                    qc_s[hh, b * CPS + j] = both[:, :DH]
            return 0

        def step(c, states):
            rows = _chunk_rows(c)
            tot_row = tot_ref[pl.ds(c * CHUNK, 1), :]
            new = []
            for hh in heads:
                s = states[hh]
                st_ref[hh, c] = s
                o_ref[hh, rows, :] = _dot(g_s[hh, rows, :], s) + au_s[hh, rows, :]
                egl = jnp.exp(_lane_col(tot_row, HEADS + hp * hp_n + hh))
                new.append(egl * s - _dot(kc_s[hh, c], s) + qc_s[hh, c])
            return tuple(new)

        def steps_of(b, states):
            for j in range(CPS):
                states = step(b * CPS + j, states)
            return states

        def fused(b, states):
            states = steps_of(b - 1, states)
            local(b, 0)
            return states

        local(0, 0)
        states = lax.fori_loop(1, nsc, fused, tuple(jnp.zeros((DH, DH), F32) for _ in heads))
        steps_of(nsc - 1, states)

    whole = pl.BlockSpec((t, LANE), lambda h: (0, 0))
    once = dict(pipeline_mode=pl.Buffered(1))
    return pl.pallas_call(
        body, name="gdn_fwd", grid=(HEADS // hp_n,),
        in_specs=[pl.BlockSpec((3, hp_n, t, DH), lambda h: (0, h, 0, 0), **once), whole, whole, whole,
                  pl.BlockSpec((HEADS, t), lambda h: (1, 0))],
        out_specs=[pl.BlockSpec((hp_n, t, DH), lambda h: (h, 0, 0), **once),
                   pl.BlockSpec((hp_n, nc, DH, DH), lambda h: (h, 0, 0, 0), **once),
                   pl.BlockSpec((hp_n, t, SC), lambda h: (h, 0, 0), **once)],
        out_shape=[jax.ShapeDtypeStruct((HEADS, t, DH), F32), jax.ShapeDtypeStruct((HEADS, nc, DH, DH), F32),
                   jax.ShapeDtypeStruct((HEADS, t, SC), MXU)],
        scratch_shapes=[pltpu.VMEM((hp_n, nc, DH, DH), F32), pltpu.VMEM((hp_n, nc, DH, DH), F32),
                        pltpu.VMEM((hp_n, t, DH), F32), pltpu.VMEM((hp_n, t, DH), F32)],
        compiler_params=_params(("arbitrary",)),
    )(qkv, gates, run, tot, run_t)


def _gdn_bwd(qkv, gates, run, tot, run_t, inv, states, do):
    t = qkv.shape[2]
    nc = t // CHUNK
    nsc = t // SC

    hp_n = GDN_HP_BWD
    heads = range(hp_n)

    def body(qkv_ref, gates_ref, run_ref, tot_ref, runt_ref, inv_ref, st_ref, do_ref,
             dqkv_ref, dcol_ref, dtot_ref, drow_ref, uw_s, kc_s, h_s, dsn_s):
        hp = pl.program_id(0)
        causal, strict, _ = _sc_masks()

        @pl.when(hp == 0)
        def _():
            dcol_ref[...] = jnp.zeros_like(dcol_ref)
            dtot_ref[...] = jnp.zeros_like(dtot_ref)

        def local_of(hh, rows, with_kk=True):
            return _gdn_local(qkv_ref[0, hh, rows, :], qkv_ref[1, hh, rows, :],
                              *_gdn_columns(gates_ref, run_ref, tot_ref, runt_ref, rows, hp * hp_n + hh), causal, with_kk)

        def local(b, _):
            rows = _sc_rows(b)
            loc = [local_of(hh, rows, with_kk=False) for hh in heads]
            us, ws = [], []
            for hh in heads:
                beta, _, _, egc, _, _, kb, _, _, _ = loc[hh]
                inv_b = inv_ref[hh, rows, :]
                us.append(_dot(inv_b, qkv_ref[2, hh, rows, :] * beta))
                ws.append(_dot(inv_b, kb * egc))
            gs = [loc[hh][5] * loc[hh][3] - _dot(loc[hh][9], ws[hh]) for hh in heads]
            for hh in heads:
                uw_s[hh, rows, :] = jnp.concatenate([us[hh], ws[hh]], axis=1)
                kd = qkv_ref[1, hh, rows, :] * loc[hh][4]
                dout = do_ref[hh, rows, :]
                for j in range(CPS):
                    sl = slice(j * CHUNK, (j + 1) * CHUNK)
                    kc_s[hh, b * CPS + j] = _dot_tn(kd[sl], ws[hh][sl])
                    h_s[hh, b * CPS + j] = _dot_tn(gs[hh][sl], dout[sl])
            return 0

        def step(c, dss):
            tot_row = tot_ref[pl.ds(c * CHUNK, 1), :]
            new = []
            for hh in heads:
                ds = dss[hh]
                dsn_s[hh, c] = ds
                egl = jnp.exp(_lane_col(tot_row, HEADS + hp * hp_n + hh))
                new.append(egl * ds - _dot_tn(kc_s[hh, c], ds) + h_s[hh, c])
            return tuple(new)

        def steps_of(b, dss):
            for j in reversed(range(CPS)):
                dss = step(b * CPS + j, dss)
            return dss

        def fused(i, dss):
            b = nsc - 2 - i
            dss = steps_of(b + 1, dss)
            local(b, 0)
            return dss

        local(nsc - 1, 0)
        dss = lax.fori_loop(0, nsc - 1, fused, tuple(jnp.zeros((DH, DH), F32) for _ in heads))
        steps_of(0, dss)

        def back(b, _):
            rows = _sc_rows(b)
            first = lax.broadcasted_iota(jnp.int32, (CHUNK, 1), 0) == 0
            loc = [local_of(hh, rows) for hh in heads]
            sign = jnp.where(lax.broadcasted_iota(jnp.int32, (1, LANE), 1) < DH, 1.0, -1.0)
            mid = []
            for hh in heads:
                beta, gl, decay, egc, ekd, qs, kb, kk, qk, attn = loc[hh]
                uw = uw_s[hh, rows, :]
                kd = qkv_ref[1, hh, rows, :] * ekd
                dout = do_ref[hh, rows, :]
                dg_p, dkd_p, duw_p, dgl_p = [], [], [], []
                for j in range(CPS):
                    sl = slice(j * CHUNK, (j + 1) * CHUNK)
                    s = st_ref[hh, b * CPS + j]
                    dsn = dsn_s[hh, b * CPS + j]
                    both = _dot_nt(jnp.concatenate([dsn, dout[sl]], axis=0), s)
                    dg_p.append(both[CHUNK:])
                    dsc = jnp.concatenate([dsn, -both[:CHUNK]], axis=1)
                    dkd_p.append(_dot_nt(uw[sl], dsc))
                    duw_p.append(_dot(kd[sl], dsc))
                    degl = jnp.sum(jnp.sum(s * dsn, axis=1, keepdims=True), axis=0, keepdims=True)
                    dgl_p.append(jnp.where(first, degl * jnp.exp(gl[j * CHUNK:j * CHUNK + 1, :]), 0.0))
                dg, dkd = jnp.concatenate(dg_p, axis=0), jnp.concatenate(dkd_p, axis=0)
                dod = jnp.concatenate([dout, dg], axis=1)
                da = jnp.where(causal, _dot_nt(dod, uw * sign), 0.0)
                duw = jnp.concatenate(duw_p, axis=0) + _dot_tn(attn, dod) * sign
                mid.append((kd, dg, dkd, da, duw, jnp.concatenate(dgl_p, axis=0)))
            inv_ts = [inv_ref[hh, rows, :].astype(F32).T for hh in heads]
            its = [_dot(inv_ts[hh], mid[hh][4]) for hh in heads]
            dinvs = []
            for hh in heads:
                beta, _, _, egc, _, _, kb, _, _, _ = loc[hh]
                dinvs.append(_dot_nt(mid[hh][4], jnp.concatenate([qkv_ref[2, hh, rows, :] * beta, kb * egc], axis=1)))
            half = [_dot(inv_ts[hh], dinvs[hh]) for hh in heads]
            dls = [jnp.where(strict, -_dot(half[hh], inv_ts[hh]), 0.0) for hh in heads]
            for hh in heads:
                head = hp * hp_n + hh
                beta, gl, decay, egc, ekd, qs, kb, kk, qk, attn = loc[hh]
                kd, dg, dkd, da, _, dgl_first = mid[hh]
                k, v = qkv_ref[1, hh, rows, :], qkv_ref[2, hh, rows, :]
                dvb, dkbe = its[hh][:, :DH], its[hh][:, DH:]
                dl = dls[hh]
                dlogd = (dl * kk + da * qk) * decay
                dd = jnp.concatenate([dl * decay, da * decay], axis=0)
                ddk = _dot(dd, k)
                dkb = ddk[:SC] + dkbe * egc
                dqs = ddk[SC:] + dg * egc
                dk = _dot_tn(dd, jnp.concatenate([kb, qs], axis=0)) + dkd * ekd + dkb * beta
                dkd_kd = jnp.sum(dkd * kd, axis=1, keepdims=True)
                narrow = dg * (qs * egc) + dkbe * (kb * egc)
                wide = dlogd[:, :LANE] + dlogd[:, LANE:] + jnp.concatenate([narrow, jnp.zeros((SC, LANE - DH), F32)], axis=1)
                dgc = jnp.sum(wide, axis=1, keepdims=True) - dkd_kd
                dbeta = jnp.sum(dkb * k + dvb * v, axis=1, keepdims=True)
                dqkv_ref[0, hh, rows, :] = dqs * Q_SCALE
                dqkv_ref[1, hh, rows, :] = dk
                dqkv_ref[2, hh, rows, :] = dvb * beta
                dcol_ref[rows, :] += _to_lane(dbeta, head) + _to_lane(dgc, HEADS + head)
                dtot_ref[rows, :] += _to_lane(dkd_kd + dgl_first, HEADS + head)
                drow_ref[pl.ds(head, 1), rows] = -jnp.sum(dlogd, axis=0, keepdims=True)
            return 0

        lax.fori_loop(0, nsc, back, 0)

    whole = pl.BlockSpec((t, LANE), lambda h: (0, 0))
    rowspec = pl.BlockSpec((HEADS, t), lambda h: (0, 0))
    sq = pltpu.VMEM((hp_n, nc, DH, DH), F32)
    per_head = pltpu.VMEM((hp_n, t, 2 * DH), F32)
    return pl.pallas_call(
        body, name="gdn_bwd", grid=(HEADS // hp_n,),
        in_specs=[pl.BlockSpec((3, hp_n, t, DH), lambda h: (0, h, 0, 0)), whole, whole, whole,
                  pl.BlockSpec((HEADS, t), lambda h: (1, 0)),
                  pl.BlockSpec((hp_n, t, SC), lambda h: (h, 0, 0)), pl.BlockSpec((hp_n, nc, DH, DH), lambda h: (h, 0, 0, 0)),
                  pl.BlockSpec((hp_n, t, DH), lambda h: (h, 0, 0))],
        out_specs=[pl.BlockSpec((3, hp_n, t, DH), lambda h: (0, h, 0, 0)), whole, whole, rowspec],
        out_shape=[jax.ShapeDtypeStruct((3, HEADS, t, DH), F32), jax.ShapeDtypeStruct((t, LANE), F32),
                   jax.ShapeDtypeStruct((t, LANE), F32), jax.ShapeDtypeStruct((HEADS, t), F32)],
        scratch_shapes=[per_head, sq, sq, sq],
        compiler_params=_params(("arbitrary",)),
    )(qkv, gates, run, tot, run_t, inv, states, do)


FOX_HP_FWD = 4
FOX_HP_BWD = 4


def _wide_t(a):
    r = a.shape[0]
    return jnp.concatenate([a, jnp.zeros((r, LANE - DH), F32)], axis=1).T[:DH]


def _tall_t(a):
    r = a.shape[1]
    return jnp.concatenate([a, jnp.zeros((LANE - DH, r), F32)], axis=0).T[:, :DH]


def _key_side_f(run_blk, head, qb):
    col = jnp.broadcast_to(_lane_col(run_blk, 2 * HEADS + head), (run_blk.shape[0], LANE))
    return jnp.concatenate([col] * (qb // LANE), axis=1)


def _diag_mask(qb):
    return lax.broadcasted_iota(jnp.int32, (qb, qb), 0) <= lax.broadcasted_iota(jnp.int32, (qb, qb), 1)


def _fox_fwd(qkv, run, run_t, carry=None):
    t = qkv.shape[2]
    qb = min(QB, t)
    nq = t // qb
    hp_n = FOX_HP_FWD
    c_in, c_in_specs, c_out_shape, c_out_specs, c_sem = _carry_operands(*carry) if carry else ([], [], [], [], None)
    n_c = len(c_in)

    def body(*refs):
        q_ref, k_ref, v_ref, run_ref, runt_ref = refs[:5]
        o_ref, lse_ref = refs[5 + n_c:7 + n_c]
        s0 = 7 + n_c + len(c_out_shape)
        kb_s, vt_s, raw_s, m_s, l_s, acc_s = refs[s0:s0 + 6]
        hp = pl.program_id(0)
        i = pl.program_id(1)

        if carry:
            _carry_step(carry[0], refs[5:5 + n_c], refs[7 + n_c], refs[8 + n_c], refs[-1], hp * nq + i,
                        (HEADS // hp_n) * nq)

        @pl.when(i == 0)
        def _():
            for hh in range(hp_n):
                kb_s[hh] = k_ref[0, hh].astype(MXU)
                for b in range(nq):
                    rows = slice(b * qb, (b + 1) * qb)
                    vt_s[hh, :, rows] = _wide_t(v_ref[0, hh, rows, :]).astype(MXU)

        qrows = pl.ds(pl.multiple_of(i * qb, qb), qb)
        qs = [(q_ref[0, hh] * Q_SCALE).astype(MXU) for hh in range(hp_n)]
        fq = [runt_ref[pl.ds(hp * hp_n + hh, 1), qrows] for hh in range(hp_n)]

        def block_rows(j):
            return pl.ds(pl.multiple_of(j * qb, qb), qb)

        def scores(j):
            for hh in range(hp_n):
                raw_s[j % 2, hh] = _dot_nt(kb_s[hh, block_rows(j), :], qs[hh])

        def absorb(j, diagonal):
            rows = block_rows(j)
            run_blk = run_ref[rows, :]
            stats, pv = [], []
            for hh in range(hp_n):
                m, l = m_s[hh], l_s[hh]
                st = raw_s[j % 2, hh] + (fq[hh] - _key_side_f(run_blk, hp * hp_n + hh, qb))
                if diagonal:
                    st = jnp.where(_diag_mask(qb), st, -1e30)
                m_new = jnp.maximum(m, jnp.max(st, axis=0, keepdims=True))
                p = jnp.exp(st - m_new)
                alpha = jnp.exp(m - m_new)
                stats.append((m_new, alpha * l + jnp.sum(p, axis=0, keepdims=True), alpha))
                pv.append(_dot(vt_s[hh, :, rows], p))
            for hh in range(hp_n):
                m_s[hh], l_s[hh] = stats[hh][0], stats[hh][1]
                acc_s[hh] = stats[hh][2] * acc_s[hh] + pv[hh]

        def kstep(j, _):
            scores(j + 1)
            absorb(j, False)
            return 0

        for hh in range(hp_n):
            m_s[hh] = jnp.full((1, qb), -1e30, F32)
            l_s[hh] = jnp.zeros((1, qb), F32)
            acc_s[hh] = jnp.zeros((DH, qb), F32)
        scores(0)
        lax.fori_loop(0, i, kstep, 0)
        absorb(i, True)
        for hh in range(hp_n):
            l = l_s[hh]
            o_ref[hh] = _tall_t(acc_s[hh] / l)
            lse_ref[pl.ds(hp * hp_n + hh, 1), qrows] = m_s[hh] + jnp.log(l)

    out = pl.pallas_call(
        body, name="fox_fwd", grid=(HEADS // hp_n, nq),
        in_specs=[pl.BlockSpec((1, hp_n, qb, DH), lambda h, i: (0, h, i, 0)),
                  pl.BlockSpec((1, hp_n, t, DH), lambda h, i: (1, h, 0, 0)),
                  pl.BlockSpec((1, hp_n, t, DH), lambda h, i: (2, h, 0, 0)),
                  pl.BlockSpec((t, LANE), lambda h, i: (0, 0)),
                  pl.BlockSpec((HEADS, t), lambda h, i: (2, 0))] + c_in_specs,
        out_specs=[pl.BlockSpec((hp_n, qb, DH), lambda h, i: (h, i, 0)),
                   pl.BlockSpec((HEADS, t), lambda h, i: (0, 0))] + c_out_specs,
        out_shape=[jax.ShapeDtypeStruct((HEADS, t, DH), F32), jax.ShapeDtypeStruct((HEADS, t), F32)] + c_out_shape,
        input_output_aliases={5 + j: 4 + j for j in range(n_c)},
        scratch_shapes=[pltpu.VMEM((hp_n, t, DH), MXU), pltpu.VMEM((hp_n, DH, t), MXU), pltpu.VMEM((2, hp_n, qb, qb), F32),
                        pltpu.VMEM((hp_n, 1, qb), F32), pltpu.VMEM((hp_n, 1, qb), F32), pltpu.VMEM((hp_n, DH, qb), F32)]
        + ([c_sem] if carry else []),
        compiler_params=_params(("arbitrary", "arbitrary")),
    )(qkv, qkv, qkv, run, run_t, *c_in)
    return (out[0], out[1], _carry_state(out[2:])) if carry else (out[0], out[1])


def _fox_bwd(qkv, run, run_t, o, lse, do, carry=None):
    t = qkv.shape[2]
    qb = min(QB, t)
    nq = t // qb
    hp_n = FOX_HP_BWD
    c_in, c_in_specs, c_out_shape, c_out_specs, c_sem = _carry_operands(*carry) if carry else ([], [], [], [], None)
    n_c = len(c_in)

    def body(*refs):
        q_ref, k_ref, v_ref, run_ref, runt_ref, o_ref, lse_ref, do_ref = refs[:8]
        dqkv_ref, dcol_ref, drow_ref = refs[8 + n_c:11 + n_c]
        s0 = 11 + n_c + len(c_out_shape)
        dqt_s, raw_s, dpt_s, dro_s, dk_s, dv_s, dsum_s = refs[s0:s0 + 7]
        hp = pl.program_id(0)
        j = pl.program_id(1)

        if carry:
            _carry_step(carry[0], refs[8:8 + n_c], refs[11 + n_c], refs[12 + n_c], refs[-1], hp * nq + j,
                        (HEADS // hp_n) * nq)

        @pl.when(j == 0)
        def _():
            dqt_s[...] = jnp.zeros_like(dqt_s)

        @pl.when((j == 0) & (hp == 0))
        def _():
            dcol_ref[...] = jnp.zeros_like(dcol_ref)
            drow_ref[...] = jnp.zeros_like(drow_ref)

        krows = pl.ds(pl.multiple_of(j * qb, qb), qb)
        run_blk = run_ref[krows, :]
        ones8 = jnp.ones((8, DH), MXU)
        kb, kt, vb, fk = [], [], [], []
        for hh in range(hp_n):
            kf = k_ref[0, hh]
            kb.append(kf.astype(MXU))
            kt.append(_wide_t(kf).astype(MXU))
            vb.append(v_ref[0, hh].astype(MXU))
            fk.append(_key_side_f(run_blk, hp * hp_n + hh, qb))

        def block_rows(i):
            return pl.ds(i * qb if isinstance(i, int) else pl.multiple_of(i * qb, qb), qb)

        def products(i):
            rows = block_rows(i)
            slot = i % 2
            for hh in range(hp_n):
                dout = do_ref[hh, rows, :]
                x = dout * o_ref[hh, rows, :]
                x_hi = x.astype(MXU)
                raw_s[slot, hh] = _dot_nt(kb[hh], q_ref[0, hh, rows, :] * Q_SCALE)
                dpt_s[slot, hh] = _dot_nt(vb[hh], dout)
                dro_s[slot, hh] = _dot_nt(ones8, x_hi) + _dot_nt(ones8, x - x_hi.astype(F32))

        def absorb(i, diagonal):
            rows = block_rows(i)
            slot = i % 2
            pieces = []
            for hh in range(hp_n):
                head = hp * hp_n + hh
                raw, dpt, drow = raw_s[slot, hh], dpt_s[slot, hh], dro_s[slot, hh, 0:1, :]
                off = runt_ref[pl.ds(head, 1), rows] - lse_ref[pl.ds(head, 1), rows]
                st = raw + (off - fk[hh])
                if diagonal:
                    st = jnp.where(_diag_mask(qb), st, -1e30)
                pt = jnp.exp(st)
                dst = pt * (dpt - drow)
                drow_ref[pl.ds(head, 1), rows] += jnp.sum(dst, axis=0, keepdims=True)
                folded = dst[:, 0:LANE]
                for c in range(1, qb // LANE):
                    folded = folded + dst[:, c * LANE:(c + 1) * LANE]
                pieces.append((_dot(dst, q_ref[0, hh, rows, :] * Q_SCALE), _dot(pt, do_ref[hh, rows, :]),
                               _dot(kt[hh], dst), folded))
            for hh in range(hp_n):
                dqt_s[hh, :, rows] += pieces[hh][2]
                if diagonal:
                    dk_s[hh], dv_s[hh], dsum_s[hh] = pieces[hh][0], pieces[hh][1], pieces[hh][3]
                else:
                    dk_s[hh] += pieces[hh][0]
                    dv_s[hh] += pieces[hh][1]
                    dsum_s[hh] += pieces[hh][3]

        def qstep(i, _):
            products(i + 1)
            absorb(i, False)
            return 0

        products(j)
        products(jnp.minimum(j + 1, nq - 1))
        absorb(j, True)
        lax.fori_loop(j + 1, nq - 1, qstep, 0)

        @pl.when(j < nq - 1)
        def _():
            absorb(nq - 1, False)
        for hh in range(hp_n):
            dqkv_ref[1, hh, krows, :] = dk_s[hh]
            dqkv_ref[2, hh, krows, :] = dv_s[hh]
            dcol_ref[krows, :] += _to_lane(-jnp.sum(dsum_s[hh], axis=1, keepdims=True), 2 * HEADS + hp * hp_n + hh)

        @pl.when(j == nq - 1)
        def _():
            for hh in range(hp_n):
                for b in range(nq):
                    rows = slice(b * qb, (b + 1) * qb)
                    dqkv_ref[0, hh, rows, :] = _tall_t(dqt_s[hh, :, rows]) * Q_SCALE

    once = dict(pipeline_mode=pl.Buffered(1))
    full = pl.BlockSpec((hp_n, t, DH), lambda h, j: (h, 0, 0), **once)
    rows8 = pl.BlockSpec((HEADS, t), lambda h, j: (0, 0))
    out = pl.pallas_call(
        body, name="fox_bwd", grid=(HEADS // hp_n, nq),
        in_specs=[pl.BlockSpec((1, hp_n, t, DH), lambda h, j: (0, h, 0, 0), **once),
                  pl.BlockSpec((1, hp_n, qb, DH), lambda h, j: (1, h, j, 0)),
                  pl.BlockSpec((1, hp_n, qb, DH), lambda h, j: (2, h, j, 0)),
                  pl.BlockSpec((t, LANE), lambda h, j: (0, 0), **once), pl.BlockSpec((HEADS, t), lambda h, j: (2, 0)),
                  full, rows8, full] + c_in_specs,
        out_specs=[pl.BlockSpec((3, hp_n, t, DH), lambda h, j: (0, h, 0, 0), **once),
                   pl.BlockSpec((t, LANE), lambda h, j: (0, 0)), rows8] + c_out_specs,
        out_shape=[jax.ShapeDtypeStruct((3, HEADS, t, DH), F32), jax.ShapeDtypeStruct((t, LANE), F32),
                   jax.ShapeDtypeStruct((HEADS, t), F32)] + c_out_shape,
        input_output_aliases={8 + j: 5 + j for j in range(n_c)},
        scratch_shapes=[pltpu.VMEM((hp_n, DH, t), F32), pltpu.VMEM((2, hp_n, qb, qb), F32), pltpu.VMEM((2, hp_n, qb, qb), F32),
                        pltpu.VMEM((2, hp_n, 8, qb), F32), pltpu.VMEM((hp_n, qb, DH), F32), pltpu.VMEM((hp_n, qb, DH), F32),
                        pltpu.VMEM((hp_n, qb, LANE), F32)] + ([c_sem] if carry else []),
        compiler_params=_params(("arbitrary", "arbitrary")),
    )(qkv, qkv, qkv, run, run_t, o, lse, do, *c_in)
    return (out[0], out[1], out[2], _carry_state(out[3:])) if carry else tuple(out)


Z_COL0 = 3 * WIDTH // LANE
FGATE_COL0 = 7 * WIDTH // LANE


def _gdn_post(o, pm, onw):
    t = pm.shape[0]

    def body(o_ref, z_ref, w_ref, m_ref):
        z = z_ref[...]
        sz = z * _sigmoid(z)
        halves = []
        for hh in range(2):
            ov = o_ref[hh]
            n = ov * lax.rsqrt(jnp.mean(ov * ov, axis=-1, keepdims=True) + EPS) * w_ref[...]
            halves.append(n * sz[:, hh * DH:(hh + 1) * DH])
        m_ref[...] = jnp.concatenate(halves, axis=1).astype(m_ref.dtype)

    return pl.pallas_call(
        body, name="gdn_post", grid=(WIDTH // LANE,),
        in_specs=[pl.BlockSpec((2, t, DH), lambda j: (j, 0, 0)), pl.BlockSpec((t, LANE), lambda j: (0, Z_COL0 + j)),
                  pl.BlockSpec((1, DH), lambda j: (0, 0))],
        out_specs=pl.BlockSpec((t, LANE), lambda j: (0, j)),
        out_shape=jax.ShapeDtypeStruct((t, WIDTH), MXU),
        compiler_params=_params(("arbitrary",)),
    )(o, pm, onw)


def _gdn_post_bwd(o, pm, onw, dmix):
    t = pm.shape[0]

    def body(o_ref, z_ref, w_ref, dm_ref, do_ref, dz_ref, dw_ref):
        @pl.when(pl.program_id(0) == 0)
        def _():
            dw_ref[...] = jnp.zeros_like(dw_ref)

        z = z_ref[...]
        sg = _sigmoid(z)
        sz = z * sg
        dsz = sg * (1.0 + z * (1.0 - sg))
        dm = dm_ref[...]
        for hh in range(2):
            cols = slice(hh * DH, (hh + 1) * DH)
            ov = o_ref[hh]
            r = lax.rsqrt(jnp.mean(ov * ov, axis=-1, keepdims=True) + EPS)
            xn = ov * r
            dmh = dm[:, cols]
            dn = dmh * sz[:, cols]
            dz_ref[:, cols] = (dmh * (xn * w_ref[...]) * dsz[:, cols]).astype(dz_ref.dtype)
            dw_ref[...] += jnp.sum(dn * xn, axis=0, keepdims=True)
            g = dn * w_ref[...]
            do_ref[hh] = r * (g - xn * jnp.mean(g * xn, axis=-1, keepdims=True))

    return pl.pallas_call(
        body, name="gdn_post_bwd", grid=(WIDTH // LANE,),
        in_specs=[pl.BlockSpec((2, t, DH), lambda j: (j, 0, 0)), pl.BlockSpec((t, LANE), lambda j: (0, Z_COL0 + j)),
                  pl.BlockSpec((1, DH), lambda j: (0, 0)), pl.BlockSpec((t, LANE), lambda j: (0, j))],
        out_specs=[pl.BlockSpec((2, t, DH), lambda j: (j, 0, 0)), pl.BlockSpec((t, LANE), lambda j: (0, j)),
                   pl.BlockSpec((1, DH), lambda j: (0, 0))],
        out_shape=[jax.ShapeDtypeStruct((HEADS, t, DH), F32), jax.ShapeDtypeStruct((t, WIDTH), MXU),
                   jax.ShapeDtypeStruct((1, DH), F32)],
        compiler_params=_params(("arbitrary",)),
    )(o, pm, onw, dmix)


def _fox_post(o, pm):
    t = pm.shape[0]

    def body(o_ref, g_ref, m_ref):
        m_ref[...] = (jnp.concatenate([o_ref[0], o_ref[1]], axis=1) * _sigmoid(g_ref[...])).astype(m_ref.dtype)

    return pl.pallas_call(
        body, name="fox_post", grid=(WIDTH // LANE,),
        in_specs=[pl.BlockSpec((2, t, DH), lambda j: (j, 0, 0)), pl.BlockSpec((t, LANE), lambda j: (0, FGATE_COL0 + j))],
        out_specs=pl.BlockSpec((t, LANE), lambda j: (0, j)),
        out_shape=jax.ShapeDtypeStruct((t, WIDTH), MXU),
        compiler_params=_params(("arbitrary",)),
    )(o, pm)


def _fox_post_bwd(o, pm, dmix):
    t = pm.shape[0]

    def body(o_ref, g_ref, dm_ref, do_ref, dg_ref):
        sg = _sigmoid(g_ref[...])
        dm = dm_ref[...]
        for hh in range(2):
            cols = slice(hh * DH, (hh + 1) * DH)
            do_ref[hh] = dm[:, cols] * sg[:, cols]
            dg_ref[:, cols] = (dm[:, cols] * o_ref[hh] * (sg * (1.0 - sg))[:, cols]).astype(dg_ref.dtype)

    return pl.pallas_call(
        body, name="fox_post_bwd", grid=(WIDTH // LANE,),
        in_specs=[pl.BlockSpec((2, t, DH), lambda j: (j, 0, 0)), pl.BlockSpec((t, LANE), lambda j: (0, FGATE_COL0 + j)),
                  pl.BlockSpec((t, LANE), lambda j: (0, j))],
        out_specs=[pl.BlockSpec((2, t, DH), lambda j: (j, 0, 0)), pl.BlockSpec((t, LANE), lambda j: (0, j))],
        out_shape=[jax.ShapeDtypeStruct((HEADS, t, DH), F32), jax.ShapeDtypeStruct((t, WIDTH), MXU)],
        compiler_params=_params(("arbitrary",)),
    )(o, pm, dmix)


def _tail(x, mixg, mixf, tgt, wo, n2w, wg_t, wu_t, wd, fw):
    t, d = x.shape
    dff = wd.shape[0]
    tb = min(TB, t)

    def body(x_ref, mg_ref, mf_ref, t_ref, wo_ref, n2_ref, wg_ref, wu_ref, wd_ref, fw_ref,
             h2_ref, act_ref, dgate_ref, dup_ref, dx3_ref, dx2_ref, dmg_ref, dmf_ref, dn2_ref, dfw_ref, loss_ref):
        @pl.when(pl.program_id(0) == 0)
        def _():
            dn2_ref[...] = jnp.zeros_like(dn2_ref)
            dfw_ref[...] = jnp.zeros_like(dfw_ref)
            loss_ref[...] = jnp.zeros_like(loss_ref)

        x2 = x_ref[...] + _dot(mg_ref[...], wo_ref[0:WIDTH, :]) + _dot(mf_ref[...], wo_ref[WIDTH:2 * WIDTH, :])
        r2 = lax.rsqrt(jnp.mean(x2 * x2, axis=-1, keepdims=True) + EPS)
        xn2 = x2 * r2
        h2 = (xn2 * n2_ref[...]).astype(MXU)
        h2_ref[...] = h2
        gate = _dot_nt(h2, wg_ref[...])
        up = _dot_nt(h2, wu_ref[...])
        sg = _sigmoid(gate)
        sl = gate * sg
        act = (sl * up).astype(MXU)
        act_ref[...] = act
        x3 = x2 + _dot(act, wd_ref[...])
        r3 = lax.rsqrt(jnp.mean(x3 * x3, axis=-1, keepdims=True) + EPS)
        xn3 = x3 * r3
        err = xn3 * fw_ref[...] - t_ref[...]
        loss_ref[...] += 0.5 * jnp.sum(jnp.mean(err * err, axis=-1, keepdims=True), axis=0, keepdims=True)
        dy = err * (1.0 / d)
        dfw_ref[...] += jnp.sum(dy * xn3, axis=0, keepdims=True)
        g3 = dy * fw_ref[...]
        dx3 = r3 * (g3 - xn3 * jnp.mean(g3 * xn3, axis=-1, keepdims=True))
        dx3_ref[...] = dx3.astype(MXU)
        dact = _dot_nt(dx3, wd_ref[...])
        dgate = (dact * up * (sg * (1.0 + gate * (1.0 - sg)))).astype(MXU)
        dup = (dact * sl).astype(MXU)
        dgate_ref[...] = dgate
        dup_ref[...] = dup
        dh2 = _dot(dgate, wg_ref[...]) + _dot(dup, wu_ref[...])
        dn2_ref[...] += jnp.sum(dh2 * xn2, axis=0, keepdims=True)
        g2 = dh2 * n2_ref[...]
        dx2 = dx3 + r2 * (g2 - xn2 * jnp.mean(g2 * xn2, axis=-1, keepdims=True))
        dx2_ref[...] = dx2
        dmg_ref[...] = _dot_nt(dx2, wo_ref[0:WIDTH, :])
        dmf_ref[...] = _dot_nt(dx2, wo_ref[WIDTH:2 * WIDTH, :])

    def tok(n):
        return pl.BlockSpec((tb, n), lambda i: (i, 0))

    acc = pl.BlockSpec((1, d), lambda i: (0, 0))
    sds = jax.ShapeDtypeStruct
    return pl.pallas_call(
        body, name="tail", grid=(t // tb,),
        in_specs=[tok(d), tok(WIDTH), tok(WIDTH), tok(d), _resident(wo.shape), _resident((1, d)),
                  _resident(wg_t.shape), _resident(wu_t.shape), _resident(wd.shape), _resident((1, d))],
        out_specs=[tok(d), tok(dff), tok(dff), tok(dff), tok(d), tok(d), tok(WIDTH), tok(WIDTH), acc, acc,
                   pl.BlockSpec((1, 1), lambda i: (0, 0))],
        out_shape=[sds((t, d), MXU), sds((t, dff), MXU), sds((t, dff), MXU), sds((t, dff), MXU), sds((t, d), MXU),
                   sds((t, d), F32), sds((t, WIDTH), F32), sds((t, WIDTH), F32), sds((1, d), F32), sds((1, d), F32),
                   sds((1, 1), F32)],
        compiler_params=_params(("arbitrary",)),
    )(x, mixg, mixf, tgt, wo, n2w, wg_t, wu_t, wd, fw)


def _wgrads(a_list, b, name):
    t, n = b.shape
    ms = [a.shape[1] for a in a_list]
    bms = [256 if m % 256 == 0 else LANE for m in ms]
    nbs = [m // bm for m, bm in zip(ms, bms)]
    k = len(a_list)
    cast = b.dtype != jnp.dtype(MXU)

    def body(*refs):
        a_refs, b_ref, o_refs = refs[:k], refs[k], refs[k + 1:2 * k + 1]
        i = pl.program_id(0)
        if cast:
            @pl.when(i == 0)
            def _():
                refs[-1][...] = b_ref[...].astype(MXU)
        for a_ref, o_ref, nb in zip(a_refs, o_refs, nbs):
            @pl.when(i < nb)
            def _():
                o_ref[...] = _dot_tn(a_ref[...], refs[-1][...] if cast else b_ref[...]).astype(o_ref.dtype)

    def clamp(nb):
        return lambda i: jnp.minimum(i, nb - 1)

    return pl.pallas_call(
        body, name=name, grid=(max(nbs),),
        in_specs=[pl.BlockSpec((t, bm), lambda i, c=clamp(nb): (0, c(i))) for bm, nb in zip(bms, nbs)] + [_resident((t, n))],
        out_specs=[pl.BlockSpec((bm, n), lambda i, c=clamp(nb): (c(i), 0)) for bm, nb in zip(bms, nbs)],
        out_shape=[jax.ShapeDtypeStruct((m, n), WIRE) for m in ms],
        scratch_shapes=[pltpu.VMEM((t, n), MXU)] if cast else [],
        compiler_params=_params(("arbitrary",)),
    )(*a_list, b)


def _merge_dw_in(d_gdn, d_z, d_fox, d_fg, d_small, shards=False):
    pieces = [d_gdn, d_z, d_small[:2 * HEADS], d_fox, d_fg, d_small[2 * HEADS:3 * HEADS]]
    if not shards:
        return jnp.concatenate(pieces, axis=0)
    n = sum(p.shape[0] for p in pieces) // N_DEV
    out = []
    for dev in range(N_DEV):
        parts, first = [], 0
        for p in pieces:
            lo, hi = max(dev * n, first), min((dev + 1) * n, first + p.shape[0])
            if lo < hi:
                parts.append(p[lo - first:hi - first])
            first += p.shape[0]
        out.append(jnp.concatenate(parts, axis=0))
    return jnp.stack(out)


def _lanes(*pieces):
    v = jnp.concatenate([p.reshape(-1).astype(F32) for p in pieces])
    return jnp.pad(v, (0, LANE - v.shape[0])).reshape(1, LANE)


def _vector_params(p):
    d = p["norm1_w"].size
    gparams = jnp.concatenate([_lanes(jnp.zeros(HEADS), p["gdn_dt_bias"], p["fox_f_bias"]),
                               _lanes(jnp.zeros(HEADS), p["gdn_A_log"]), jnp.zeros((6, LANE), F32)])
    fox_nw = jnp.stack([jnp.tile(p["fox_q_norm_w"].reshape(-1), 2), jnp.tile(p["fox_k_norm_w"].reshape(-1), 2),
                        jnp.ones((LANE,), F32)])
    return dict(n1w=p["norm1_w"].reshape(1, d), n2w=p["norm2_w"].reshape(1, d), fw=p["final_norm_w"].reshape(1, d),
                onw=p["gdn_out_norm_w"].reshape(1, DH), gparams=gparams, fox_nw=fox_nw)


def _mixer_forward(x, vp, w_t, ws_t, conv_w, carry=None):
    h1, pm, ps = _inproj(x, vp["n1w"], w_t, ws_t)
    gates, run, tot, run_t = _gates(ps, vp["gparams"])
    fqkv = _fox_prep(pm, vp["fox_nw"])
    o_fox, lse, *carried = _fox_fwd(fqkv, run, run_t, carry)
    if carry:
        pm, o_fox = lax.optimization_barrier((pm, o_fox))
    mixf = _fox_post(o_fox, pm)
    gqkv = _gdn_prep(pm, conv_w)
    o_gdn, states, inv = _gdn_fwd(gqkv, gates, run, tot, run_t)
    mixg = _gdn_post(o_gdn, pm, vp["onw"])
    return dict(h1=h1, pm=pm, ps=ps, gates=gates, run=run, tot=tot, run_t=run_t, gqkv=gqkv, o_gdn=o_gdn, states=states,
                inv=inv, mixg=mixg, fqkv=fqkv, o_fox=o_fox, lse=lse, mixf=mixf, carried=carried[0] if carried else None)


def _mixer_backward(x, vp, w_t, ws_t, conv_w, f, dx2, dmixg, dmixf, carry=None, scatter_own=False):
    pm = f["pm"]
    do_fox, dfg = _fox_post_bwd(f["o_fox"], pm, dmixf)
    dfqkv, dcol_f, drow_f, *carried = _fox_bwd(f["fqkv"], f["run"], f["run_t"], f["o_fox"], f["lse"], do_fox, carry)
    dfox, dfnw = _fox_prep_bwd(pm, vp["fox_nw"], dfqkv)
    do_gdn, dz, donw = _gdn_post_bwd(f["o_gdn"], pm, vp["onw"], dmixg)
    dgqkv, dcol_g, dtot_g, drow_g = _gdn_bwd(f["gqkv"], f["gates"], f["run"], f["tot"], f["run_t"], f["inv"], f["states"], do_gdn)
    dgdn, dconv = _gdn_prep_bwd(pm, conv_w, dgqkv)
    dps, gsum = _gates_bwd(f["ps"], vp["gparams"], dcol_g, dtot_g, dcol_f, drow_g, drow_f)
    dw_pieces = _wgrads([dgdn, dz, dfox, dfg, dps], f["h1"], "dw_in")
    dw_in = _merge_dw_in(*dw_pieces)
    own = ("scatter", [_merge_dw_in(*dw_pieces, shards=True), _cut(dconv, 1)]) if scatter_own else None
    grad_x, dn1w, *sent = _inproj_bwd(x, vp["n1w"], dx2, dgdn, dz, dfox, dfg, dps, w_t, ws_t, own)
    small = dict(dn1w=dn1w, gsum=gsum, donw=donw, dfnw=dfnw)
    return (grad_x, dw_in, dconv, small, *carried, *sent)


VECTORS = ("norm1_w", "norm2_w", "final_norm_w", "gdn_A_log", "gdn_dt_bias", "gdn_out_norm_w", "fox_f_bias",
           "fox_q_norm_w", "fox_k_norm_w")
VEC_ROWS = 16
LOSS_ROW = len(VECTORS)


def _pack_vectors(dn1w, dn2w, dfw, gsum, donw, dfnw, loss):
    d = dn1w.shape[1]

    def body(n1_ref, n2_ref, fw_ref, gs_ref, on_ref, fn_ref, loss_ref, o_ref):
        o_ref[...] = jnp.zeros_like(o_ref)
        o_ref[0:1, :] = n1_ref[...]
        o_ref[1:2, :] = n2_ref[...]
        o_ref[2:3, :] = fw_ref[...]
        o_ref[3:4, 0:HEADS] = gs_ref[0:1, 0:HEADS]
        o_ref[4:5, 0:HEADS] = gs_ref[1:2, 0:HEADS]
        o_ref[5:6, 0:DH] = on_ref[...]
        o_ref[6:7, 0:HEADS] = gs_ref[2:3, 0:HEADS]
        for kind in range(2):
            v = fn_ref[kind, 0]
            for j in range(1, fn_ref.shape[1]):
                v = v + fn_ref[kind, j]
            o_ref[7 + kind:8 + kind, 0:DH] = v[:, :DH] + v[:, DH:]
        o_ref[LOSS_ROW:LOSS_ROW + 1, 0:1] = loss_ref[...]

    return pl.pallas_call(body, name="pack_vectors", out_shape=jax.ShapeDtypeStruct((VEC_ROWS, d), F32),
                          compiler_params=_params())(dn1w, dn2w, dfw, gsum, donw, dfnw, loss)


def _late_grads(f, h2, act, dgate, dup, dx3, dx2):
    dw_gate, dw_up = _wgrads([dgate, dup], h2, "dw_gate_up")
    (dw_down,) = _wgrads([act], dx3, "dw_down")
    return {"w_out": jnp.concatenate(_wgrads([f["mixg"], f["mixf"]], dx2, "dw_out"), axis=0),
            "w_ffn_gate": dw_gate, "w_ffn_up": dw_up, "w_ffn_down": dw_down}


def _local_step(x, tgt, p, w_in_t, conv_w, wo, wg_t, wu_t, wd):
    vp = _vector_params(p)
    ws_t = _small_rows(w_in_t)
    f = _mixer_forward(x, vp, w_in_t, ws_t, conv_w)
    (h2, act, dgate, dup, dx3, dx2, dmixg, dmixf, dn2w, dfw, loss) = _tail(
        x, f["mixg"], f["mixf"], tgt, wo, vp["n2w"], wg_t, wu_t, wd, vp["fw"])
    grad_x, dw_in, dconv, small = _mixer_backward(x, vp, w_in_t, ws_t, conv_w, f, dx2, dmixg, dmixf)
    grads = {"w_in": dw_in, "gdn_conv_w": dconv, **_late_grads(f, h2, act, dgate, dup, dx3, dx2)}
    vec = _pack_vectors(small["dn1w"], dn2w, dfw, small["gsum"], small["donw"], small["dfnw"], loss)
    return grad_x[0], grads, vec


def _my_place():
    return lax.axis_index("x"), lax.axis_index("y"), lax.axis_index("c")


def _peers():
    x, y, c = _my_place()
    peers = []
    for k in range(1, N_DEV):
        px = 1 - x if k & 4 else x
        py = 1 - y if k & 2 else y
        pc = 1 - c if k & 1 else c
        peers.append(((px, py, pc), 4 * px + 2 * py + pc))
    return 4 * x + 2 * y + c, peers


def _spread_copies(kind, srcs, lands, send_sems, recv_sems):
    me, peers = _peers()
    kinds = [kind] * len(srcs) if isinstance(kind, str) else kind
    remote, local = [], []
    for i, (kd, src, land) in enumerate(zip(kinds, srcs, lands)):
        for k, (dev, idx) in enumerate(peers):
            remote.append(pltpu.make_async_remote_copy(
                src_ref=src if kd == "gather" else src.at[idx], dst_ref=land.at[me],
                send_sem=send_sems.at[i * (N_DEV - 1) + k], recv_sem=recv_sems.at[i * (N_DEV - 1) + k],
                device_id=dev, device_id_type=MESH))
        local.append((src if kd == "gather" else src.at[me], land.at[me]))
    return remote, local


def _gather_two_level(arrays, name):
    n = len(arrays)

    def body(*refs):
        srcs, lands = refs[:n], refs[n:2 * n]
        send_sems, recv_sems, local_sems = refs[2 * n:]
        x, y, c = _my_place()
        me, sibling = (x, y, c), (x, y, 1 - c)
        chips = [(1 - x, y), (x, 1 - y), (1 - x, 1 - y)]

        def index(p):
            return 4 * p[0] + 2 * p[1] + p[2]

        def copy(i, k, block, to, src=None):
            blk = lands[i].at[index(block)]
            return pltpu.make_async_remote_copy(
                src_ref=blk if src is None else src, dst_ref=blk, send_sem=send_sems.at[7 * i + k],
                recv_sem=recv_sems.at[7 * i + k], device_id=to, device_id_type=MESH)

        mine = [pltpu.make_async_copy(srcs[i], lands[i].at[index(me)], local_sems.at[i]) for i in range(n)]
        for cp in mine:
            cp.start()
        first = []
        for i in range(n):
            first.append(copy(i, 0, me, sibling, src=srcs[i]))
            first += [copy(i, 1 + j, me, (*chip, c), src=srcs[i]) for j, chip in enumerate(chips)]
        for cp in first:
            cp.start()
        passed = []
        for i in range(n):
            for j, chip in enumerate(chips):
                copy(i, 1 + j, (*chip, c), me).wait_recv()
                passed.append(copy(i, 4 + j, (*chip, c), sibling))
                passed[-1].start()
        for i in range(n):
            copy(i, 0, sibling, me).wait_recv()
            for j, chip in enumerate(chips):
                copy(i, 4 + j, (*chip, 1 - c), me).wait_recv()
        for cp in first + passed:
            cp.wait_send()
        for cp in mine:
            cp.wait()

    return pl.pallas_call(
        body, name=name,
        out_shape=[jax.ShapeDtypeStruct((N_DEV,) + a.shape, a.dtype) for a in arrays],
        in_specs=[pl.BlockSpec(memory_space=pl.ANY)] * n, out_specs=[pl.BlockSpec(memory_space=pl.ANY)] * n,
        scratch_shapes=[pltpu.SemaphoreType.DMA((7 * n,)), pltpu.SemaphoreType.DMA((7 * n,)),
                        pltpu.SemaphoreType.DMA((n,))],
    )(*arrays)


def _land_shape(kind, a):
    return (N_DEV,) + a.shape if kind == "gather" else a.shape


HBM = pl.BlockSpec(memory_space=pltpu.HBM)
SEM = pl.BlockSpec(memory_space=pltpu.SEMAPHORE)


def _hbm(a):
    return pltpu.with_memory_space_constraint(a, pltpu.HBM)


def _carry_operands(kind, arrays):
    n = len(arrays)
    kinds = [kind] * n if isinstance(kind, str) else kind
    lands = [lax.empty(_land_shape(kd, a), a.dtype) for kd, a in zip(kinds, arrays)]
    sems = [pltpu.SemaphoreType.DMA((n * (N_DEV - 1),))] * 2
    return ([_hbm(a) for a in list(arrays) + lands], [HBM] * (2 * n),
            sems + [pltpu.HBM(a.shape, a.dtype) for a in list(arrays) + lands], [SEM] * 2 + [HBM] * (2 * n),
            pltpu.SemaphoreType.DMA((n,)))


def _carry_step(kind, in_refs, send_sems, recv_sems, local_sems, step, n_steps):
    n = len(in_refs) // 2
    remote, local = _spread_copies(kind, in_refs[:n], in_refs[n:], send_sems, recv_sems)
    copies = [pltpu.make_async_copy(s, d, local_sems.at[i]) for i, (s, d) in enumerate(local)]
    per_step = -(-len(remote) // n_steps)
    for s in range(-(-len(remote) // per_step)):
        @pl.when(step == s)
        def _():
            for cp in remote[s * per_step:(s + 1) * per_step]:
                cp.start()
            if s == 0:
                for cp in copies:
                    cp.start()

    @pl.when(step == n_steps - 1)
    def _():
        for cp in copies:
            cp.wait()


def _carry_state(extra_out):
    n = (len(extra_out) - 2) // 2
    return list(extra_out[2:2 + n]), list(extra_out[2 + n:]), extra_out[0], extra_out[1]


def _spread_start(arrays, kind, name):
    n = len(arrays)

    def body(*refs):
        srcs, lands = refs[:n], refs[n:2 * n]
        send_sems, recv_sems = refs[2 * n], refs[2 * n + 1]
        token = refs[4 * n + 2]
        local_sems = refs[4 * n + 3]
        remote, local = _spread_copies(kind, srcs, lands, send_sems, recv_sems)
        for cp in remote:
            cp.start()
        copies = [pltpu.make_async_copy(s, d, local_sems.at[i]) for i, (s, d) in enumerate(local)]
        for cp in copies:
            cp.start()
        for cp in copies:
            cp.wait()
        token[...] = jnp.zeros_like(token)

    sems = (pltpu.SemaphoreType.DMA((n * (N_DEV - 1),)),) * 2
    kinds = [kind] * n if isinstance(kind, str) else kind
    lands = [lax.empty(_land_shape(kd, a), a.dtype) for kd, a in zip(kinds, arrays)]
    out = pl.pallas_call(
        body, name=name,
        out_shape=sems + tuple(pltpu.HBM(a.shape, a.dtype) for a in list(arrays) + lands)
        + (jax.ShapeDtypeStruct((8, LANE), F32),),
        in_specs=[HBM] * (2 * n), out_specs=tuple([SEM] * 2 + [HBM] * (2 * n) + [pl.BlockSpec(memory_space=pltpu.VMEM)]),
        input_output_aliases={j: 2 + j for j in range(2 * n)},
        scratch_shapes=[pltpu.SemaphoreType.DMA((n,))],
        compiler_params=pltpu.CompilerParams(has_side_effects=pltpu.SideEffectType.DATAFLOW_SIDE_EFFECTING),
    )(*[_hbm(a) for a in arrays], *[_hbm(a) for a in lands])
    return (list(out[2:2 + n]), list(out[2 + n:2 + 2 * n]), out[0], out[1]), out[-1]


def _spread_wait(state, kind, after, name):
    srcs, lands, send_sems, recv_sems = state
    n = len(srcs)
    after = list(after) if isinstance(after, (list, tuple)) else [after]

    def body(*refs):
        remote, _ = _spread_copies(kind, refs[:n], refs[n:2 * n], refs[2 * n], refs[2 * n + 1])
        for cp in remote:
            cp.wait_send()
        for cp in remote:
            cp.wait_recv()

    out = pl.pallas_call(
        body, name=name,
        out_shape=tuple(pltpu.HBM(a.shape, a.dtype) for a in srcs + lands),
        in_specs=[HBM] * (2 * n) + [SEM, SEM] + [pl.BlockSpec(memory_space=pl.ANY)] * len(after),
        out_specs=tuple([HBM] * (2 * n)),
        input_output_aliases={j: j for j in range(2 * n)},
        compiler_params=pltpu.CompilerParams(has_side_effects=pltpu.SideEffectType.DATAFLOW_SIDE_EFFECTING),
    )(*srcs, *lands, send_sems, recv_sems, *after)
    return list(out[n:])


ADAM_ROWS = 128
ADAM_COLS = 256


def _adam_math(g, w, m, v):
    nm = ADAM_B1 * m + (1.0 - ADAM_B1) * g
    nv = ADAM_B2 * v + (1.0 - ADAM_B2) * (g * g)
    m_hat = nm / (1.0 - ADAM_B1 ** ADAM_STEP)
    v_hat = nv / (1.0 - ADAM_B2 ** ADAM_STEP)
    return -ADAM_LR * (m_hat / (jnp.sqrt(v_hat) + ADAM_EPS) + ADAM_WD * w), nm, nv


def _sum_parts(p_ref):
    g = p_ref[0].astype(F32)
    for s in range(1, N_DEV):
        g = g + p_ref[s].astype(F32)
    return g


def _adam_matrix(parts, w, m, v, name):
    _, r, c = w.shape
    rb = ADAM_ROWS if r % ADAM_ROWS == 0 else r
    cb = ADAM_COLS if (rb == r and c % ADAM_COLS == 0) else c

    def body(p_ref, w_ref, m_ref, v_ref, g_ref, d_ref, nm_ref, nv_ref):
        g = _sum_parts(p_ref)
        g_ref[0] = g
        d_ref[0], nm_ref[0], nv_ref[0] = _adam_math(g, w_ref[0], m_ref[0], v_ref[0])

    blk = pl.BlockSpec((1, rb, cb), lambda i, j: (0, i, j))
    return pl.pallas_call(
        body, name=name, grid=(r // rb, c // cb),
        in_specs=[pl.BlockSpec((N_DEV, rb, cb), lambda i, j: (0, i, j)), blk, blk, blk],
        out_specs=[blk] * 4, out_shape=[jax.ShapeDtypeStruct(w.shape, F32)] * 4,
        compiler_params=_params(("arbitrary", "arbitrary")),
    )(parts, w, m, v)


def _copy_rows(a, name):
    _, r, c = a.shape
    rb = min(TB, r)

    def body(a_ref, o_ref):
        o_ref[...] = a_ref[...]

    blk = pl.BlockSpec((1, rb, c), lambda i: (0, i, 0))
    return pl.pallas_call(body, name=name, grid=(r // rb,), in_specs=[blk], out_specs=blk,
                          out_shape=jax.ShapeDtypeStruct(a.shape, a.dtype), compiler_params=_params(("arbitrary",)))(a)


def _adam_vectors(parts, ws, ms, vs):
    nv = len(ws)

    def body(*refs):
        p_ref = refs[0]
        w_refs, m_refs, v_refs = refs[1:1 + nv], refs[1 + nv:1 + 2 * nv], refs[1 + 2 * nv:1 + 3 * nv]
        outs = refs[1 + 3 * nv:]
        g_all = _sum_parts(p_ref)
        for i in range(nv):
            n = w_refs[i].shape[1]
            g = g_all[i:i + 1, 0:n]
            d, nm, nvv = _adam_math(g, w_refs[i][...], m_refs[i][...], v_refs[i][...])
            outs[i][...] = g
            outs[nv + i][...] = d
            outs[2 * nv + i][...] = nm
            outs[3 * nv + i][...] = nvv
        outs[4 * nv][...] = g_all[LOSS_ROW:LOSS_ROW + 1, 0:1]

    shapes = [jax.ShapeDtypeStruct(a.shape, F32) for a in ws]
    out = pl.pallas_call(body, name="adam_vectors", out_shape=shapes * 4 + [jax.ShapeDtypeStruct((1, 1), F32)],
                         compiler_params=_params())(parts, *ws, *ms, *vs)
    return out[:nv], out[nv:2 * nv], out[2 * nv:3 * nv], out[3 * nv:4 * nv], out[4 * nv]


MATRICES = (("w_in", 1), ("gdn_conv_w", 1), ("w_out", 0), ("w_ffn_gate", 1), ("w_ffn_up", 1), ("w_ffn_down", 0))
TRANSPOSED = ("w_in", "w_ffn_gate", "w_ffn_up")
WEIGHTS = ("norm1_w", "w_in", "gdn_conv_w", "gdn_A_log", "gdn_dt_bias", "gdn_out_norm_w", "fox_f_bias", "fox_q_norm_w",
           "fox_k_norm_w", "w_out", "norm2_w", "w_ffn_gate", "w_ffn_up", "w_ffn_down", "final_norm_w")


def _join(blocks, axis):
    _, r, c = blocks.shape
    if axis == 0:
        return blocks.reshape(N_DEV * r, c)
    return blocks.transpose(1, 0, 2).reshape(r, N_DEV * c)


def _cut(full, axis):
    r, c = full.shape
    if axis == 0:
        return full.reshape(N_DEV, r // N_DEV, c)
    return full.reshape(r, N_DEV, c // N_DEV).transpose(1, 0, 2)


def kernel(x, norm1_w, w_in, gdn_conv_w, gdn_A_log, gdn_dt_bias, gdn_out_norm_w, fox_f_bias, fox_q_norm_w, fox_k_norm_w, w_out, norm2_w, w_ffn_gate, w_ffn_up, w_ffn_down, final_norm_w, loss_target, m_norm1_w, m_w_in, m_gdn_conv_w, m_gdn_A_log, m_gdn_dt_bias, m_gdn_out_norm_w, m_fox_f_bias, m_fox_q_norm_w, m_fox_k_norm_w, m_w_out, m_norm2_w, m_w_ffn_gate, m_w_ffn_up, m_w_ffn_down, m_final_norm_w, v_norm1_w, v_w_in, v_gdn_conv_w, v_gdn_A_log, v_gdn_dt_bias, v_gdn_out_norm_w, v_fox_f_bias, v_fox_q_norm_w, v_fox_k_norm_w, v_w_out, v_norm2_w, v_w_ffn_gate, v_w_ffn_up, v_w_ffn_down, v_final_norm_w):
    w = dict(norm1_w=norm1_w, w_in=w_in, gdn_conv_w=gdn_conv_w, gdn_A_log=gdn_A_log, gdn_dt_bias=gdn_dt_bias,
             gdn_out_norm_w=gdn_out_norm_w, fox_f_bias=fox_f_bias, fox_q_norm_w=fox_q_norm_w, fox_k_norm_w=fox_k_norm_w,
             w_out=w_out, norm2_w=norm2_w, w_ffn_gate=w_ffn_gate, w_ffn_up=w_ffn_up, w_ffn_down=w_ffn_down,
             final_norm_w=final_norm_w)
    m = dict(norm1_w=m_norm1_w, w_in=m_w_in, gdn_conv_w=m_gdn_conv_w, gdn_A_log=m_gdn_A_log, gdn_dt_bias=m_gdn_dt_bias,
             gdn_out_norm_w=m_gdn_out_norm_w, fox_f_bias=m_fox_f_bias, fox_q_norm_w=m_fox_q_norm_w,
             fox_k_norm_w=m_fox_k_norm_w, w_out=m_w_out, norm2_w=m_norm2_w, w_ffn_gate=m_w_ffn_gate,
             w_ffn_up=m_w_ffn_up, w_ffn_down=m_w_ffn_down, final_norm_w=m_final_norm_w)
    v = dict(norm1_w=v_norm1_w, w_in=v_w_in, gdn_conv_w=v_gdn_conv_w, gdn_A_log=v_gdn_A_log, gdn_dt_bias=v_gdn_dt_bias,
             gdn_out_norm_w=v_gdn_out_norm_w, fox_f_bias=v_fox_f_bias, fox_q_norm_w=v_fox_q_norm_w,
             fox_k_norm_w=v_fox_k_norm_w, w_out=v_w_out, norm2_w=v_norm2_w, w_ffn_gate=v_w_ffn_gate,
             w_ffn_up=v_w_ffn_up, w_ffn_down=v_w_ffn_down, final_norm_w=v_final_norm_w)
    late = ("w_out", "w_ffn_gate", "w_ffn_up", "w_ffn_down")
    xs, tgt = x[0], loss_target[0]
    vp = _vector_params({n: w[n] for n in VECTORS})

    def rows_of(d, n):
        return d[n].transpose(0, 2, 1) if n in TRANSPOSED else d[n]

    wr, mr, vr = ({n: rows_of(d, n) for n, _ in MATRICES} for d in (w, m, v))

    w_in_blocks, conv_blocks = _gather_two_level([wr["w_in"][0].astype(WIRE), w["gdn_conv_w"][0]], "gather_in")
    w_t = _join(w_in_blocks, 0)
    conv_w = _join(conv_blocks, 1)
    f = _mixer_forward(xs, vp, w_t, _small_rows(w_t), conv_w, ("gather", [wr[n][0].astype(WIRE) for n in late]))
    full = {n: _join(b, 0) for n, b in zip(late, _spread_wait(f["carried"], "gather", f["mixg"], "gather_late_wait"))}

    (h2, act, dgate, dup, dx3, dx2, dmixg, dmixf, dn2w, dfw, loss) = _tail(
        xs, f["mixg"], f["mixf"], tgt, full["w_out"], vp["n2w"], full["w_ffn_gate"], full["w_ffn_up"],
        full["w_ffn_down"], vp["fw"])
    dlate = _late_grads(f, h2, act, dgate, dup, dx3, dx2)

    grad_x, dw_in, dconv, small, state_grads, state_own = _mixer_backward(
        xs, vp, w_t, _small_rows(w_t), conv_w, f, dx2, dmixg, dmixf, ("scatter", [_cut(dlate[n], 0) for n in late]),
        scatter_own=True)
    vec = _pack_vectors(small["dn1w"], dn2w, dfw, small["gsum"], small["donw"], small["dfnw"], loss)

    state_vec, token = _spread_start([vec], "gather", "vectors_start")
    parts = dict(zip(late, _spread_wait(state_grads, "scatter", token, "grads_late_wait")))
    results = [{}, {}, {}, {}]

    updated = {}

    def update(n):
        out = _adam_matrix(parts[n], wr[n], mr[n], vr[n], "adam_" + n)
        updated[n] = out[0]
        for d, a in zip(results, out):
            d[n] = a.transpose(0, 2, 1) if n in TRANSPOSED else a

    for n in late:
        update(n)
    grad_x = _copy_rows(grad_x, "grad_x_result")
    ready = [updated[n] for n in late] + [d["w_in"] for d in (wr, mr, vr)] + [grad_x]
    (parts_vec,) = _spread_wait(state_vec, "gather", ready, "vectors_wait")
    row = lambda a: a.reshape(1, -1)
    *vec_out, total_loss = _adam_vectors(parts_vec, [row(w[n]) for n in VECTORS], [row(m[n]) for n in VECTORS],
                                         [row(v[n]) for n in VECTORS])
    parts["w_in"], parts["gdn_conv_w"] = _spread_wait(state_own, "scatter", parts_vec, "own_wait")
    update("w_in")
    update("gdn_conv_w")
    for d, arrs in zip(results, vec_out):
        for n, a in zip(VECTORS, arrs):
            d[n] = a.reshape(w[n].shape)
    return (total_loss[0, 0], grad_x, *[d[n] for d in results for n in WEIGHTS])
```

```python
import functools

import jax
import jax.numpy as jnp
from jax import lax
from jax.experimental import pallas as pl
from jax.experimental.pallas import tpu as pltpu

F32 = jnp.float32
MXU = jnp.bfloat16
WIRE = jnp.bfloat16
HI = lax.Precision.HIGHEST
EPS = 1e-6

N_DEV = 8
HEADS = 8
DH = 64
WIDTH = HEADS * DH
CHUNK = 64
LANE = 128
ROW_ALIGN = 16
TB = 256
QB = 256
VMEM_LIMIT = 60 * 1024 * 1024

ADAM_LR = 0.001
ADAM_B1 = 0.9
ADAM_B2 = 0.999
ADAM_EPS = 1e-08
ADAM_WD = 0.01
ADAM_STEP = 10

MESH = pl.DeviceIdType.MESH


def _params(sem=None):
    return pltpu.CompilerParams(dimension_semantics=sem, vmem_limit_bytes=VMEM_LIMIT)


def _resident(shape):
    n = len(shape)
    return pl.BlockSpec(shape, lambda *_: (0,) * n, pipeline_mode=pl.Buffered(1))


def _dot(a, b):
    return jnp.dot(a.astype(MXU), b.astype(MXU), preferred_element_type=F32)


def _dot_nt(a, b):
    return lax.dot_general(a.astype(MXU), b.astype(MXU), (((1,), (1,)), ((), ())), preferred_element_type=F32)


def _dot_tn(a, b):
    return lax.dot_general(a.astype(MXU), b.astype(MXU), (((0,), (0,)), ((), ())), preferred_element_type=F32)


def _hdot(a, b):
    return jnp.dot(a, b, precision=HI, preferred_element_type=F32)


def _hdot_nt(a, b):
    return lax.dot_general(a, b, (((1,), (1,)), ((), ())), precision=HI, preferred_element_type=F32)


def _hdot_tn(a, b):
    return lax.dot_general(a, b, (((0,), (0,)), ((), ())), precision=HI, preferred_element_type=F32)


def _sigmoid(x):
    return 0.5 * jnp.tanh(0.5 * x) + 0.5


def _softplus(x):
    return jnp.maximum(x, 0.0) + jnp.log(1.0 + jnp.exp(-jnp.abs(x)))


def _head_sum_matrix():
    ri = lax.broadcasted_iota(jnp.int32, (LANE, LANE), 0) // DH
    ci = lax.broadcasted_iota(jnp.int32, (LANE, LANE), 1) // DH
    return (ri == ci).astype(F32)


def _group_sum(a, ones_matrix):
    hi = a.astype(jnp.bfloat16)
    lo = (a - hi.astype(F32)).astype(jnp.bfloat16)
    m = ones_matrix.astype(jnp.bfloat16)
    return jnp.dot(hi, m, preferred_element_type=F32) + jnp.dot(lo, m, preferred_element_type=F32)


def _shift_down(x, s):
    return pltpu.roll(x, s, 0)


def _shift_up(x, s):
    return pltpu.roll(x, x.shape[0] - s, 0)


ROWS_A = 4 * WIDTH
ROWS_B = ROWS_A + 2 * HEADS
ROWS_C = ROWS_B + 4 * WIDTH


def _small_rows(w_t):
    return jnp.concatenate([w_t[ROWS_A:ROWS_B], w_t[ROWS_C:], jnp.zeros((LANE - 3 * HEADS, w_t.shape[1]), w_t.dtype)])


def _inproj(x, n1w, w_t, ws_t, carry=None):
    t, d = x.shape
    tb = min(TB, t)
    c_in, c_in_specs, c_out_shape, c_out_specs, c_sem = _carry_operands(*carry) if carry else ([], [], [], [], None)
    n_c = len(c_in)

    def body(*refs):
        x_ref, nw_ref, wt_ref, ws_ref = refs[:4]
        h_ref, pm_ref, ps_ref = refs[4 + n_c:7 + n_c]
        if carry:
            _carry_step(carry[0], refs[4:4 + n_c], refs[7 + n_c], refs[8 + n_c], refs[-1], pl.program_id(0), t // tb)
        xv = x_ref[...]
        r = lax.rsqrt(jnp.mean(xv * xv, axis=-1, keepdims=True) + EPS)
        h = (xv * r * nw_ref[...]).astype(MXU)
        h_ref[...] = h
        pm_ref[:, 0:ROWS_A] = _dot_nt(h, wt_ref[0:ROWS_A, :])
        pm_ref[:, ROWS_A:2 * ROWS_A] = _dot_nt(h, wt_ref[ROWS_B:ROWS_C, :])
        ps_ref[...] = _dot_nt(h, ws_ref[...])

    out = pl.pallas_call(
        body, name="inproj", grid=(t // tb,),
        in_specs=[pl.BlockSpec((tb, d), lambda i: (i, 0)), _resident((1, d)), _resident(w_t.shape), _resident((LANE, d))]
        + c_in_specs,
        out_specs=[pl.BlockSpec((tb, d), lambda i: (i, 0)), pl.BlockSpec((tb, 2 * ROWS_A), lambda i: (i, 0)),
                   pl.BlockSpec((tb, LANE), lambda i: (i, 0))] + c_out_specs,
        out_shape=[jax.ShapeDtypeStruct((t, d), MXU), jax.ShapeDtypeStruct((t, 2 * ROWS_A), F32),
                   jax.ShapeDtypeStruct((t, LANE), F32)] + c_out_shape,
        input_output_aliases={4 + j: 5 + j for j in range(n_c)},
        scratch_shapes=[c_sem] if carry else [],
        compiler_params=_params(("arbitrary",)),
    )(x, n1w, w_t, ws_t, *c_in)
    return (out[0], out[1], out[2], _carry_state(out[3:])) if carry else tuple(out)


def _inproj_bwd(x, n1w, dx2, dgdn, dz, dfox, dfg, dps, w_t, ws_t, carry=None):
    t, d = x.shape
    tb = min(TB, t)
    w3 = 3 * WIDTH
    c_in, c_in_specs, c_out_shape, c_out_specs, c_sem = _carry_operands(*carry) if carry else ([], [], [], [], None)
    n_c = len(c_in)

    def body(*refs):
        x_ref, nw_ref, dx2_ref, dgdn_ref, dz_ref, dfox_ref, dfg_ref, dps_ref, wm_ref, ws_ref = refs[:10]
        gx_ref, dnw_ref = refs[10 + n_c:12 + n_c]
        if carry:
            _carry_step(carry[0], refs[10:10 + n_c], refs[12 + n_c], refs[13 + n_c], refs[-1], pl.program_id(0), t // tb)
        dh = _dot(dgdn_ref[...], wm_ref[0:w3, :])
        dh += _dot(dz_ref[...], wm_ref[w3:ROWS_A, :])
        dh += _dot(dfox_ref[...], wm_ref[ROWS_B:ROWS_B + w3, :])
        dh += _dot(dfg_ref[...], wm_ref[ROWS_B + w3:ROWS_C, :])
        dh += _dot(dps_ref[...], ws_ref[...])
        xv = x_ref[...]
        r = lax.rsqrt(jnp.mean(xv * xv, axis=-1, keepdims=True) + EPS)
        xn = xv * r

        @pl.when(pl.program_id(0) == 0)
        def _():
            dnw_ref[...] = jnp.zeros_like(dnw_ref)

        dnw_ref[...] += jnp.sum(dh * xn, axis=0, keepdims=True)
        g = dh * nw_ref[...]
        gx_ref[0] = dx2_ref[...] + r * (g - xn * jnp.mean(g * xn, axis=-1, keepdims=True))

    def tok(n):
        return pl.BlockSpec((tb, n), lambda i: (i, 0))

    out = pl.pallas_call(
        body, name="inproj_bwd", grid=(t // tb,),
        in_specs=[tok(d), _resident((1, d)), tok(d), tok(w3), tok(WIDTH), tok(w3), tok(WIDTH), tok(LANE),
                  _resident(w_t.shape), _resident(ws_t.shape)] + c_in_specs,
        out_specs=[pl.BlockSpec((1, tb, d), lambda i: (0, i, 0)), pl.BlockSpec((1, d), lambda i: (0, 0))] + c_out_specs,
        out_shape=[jax.ShapeDtypeStruct((1, t, d), F32), jax.ShapeDtypeStruct((1, d), F32)] + c_out_shape,
        input_output_aliases={10 + j: 4 + j for j in range(n_c)},
        scratch_shapes=[c_sem] if carry else [],
        compiler_params=_params(("arbitrary",)),
    )(x, n1w, dx2, dgdn, dz, dfox, dfg, dps, w_t, ws_t, *c_in)
    return (out[0], out[1], _carry_state(out[2:])) if carry else tuple(out)


def _gate_lanes(shape):
    lane = lax.broadcasted_iota(jnp.int32, shape, 1)
    return lane < HEADS, (lane >= HEADS) & (lane < 2 * HEADS), (lane >= 2 * HEADS) & (lane < 3 * HEADS)


def _block_masks():
    ri = lax.broadcasted_iota(jnp.int32, (LANE, LANE), 0)
    ci = lax.broadcasted_iota(jnp.int32, (LANE, LANE), 1)
    same = (ri // CHUNK) == (ci // CHUNK)
    return ((ri >= ci).astype(F32), (ri <= ci).astype(F32), (same & (ri >= ci)).astype(F32),
            (same & (ri <= ci)).astype(F32), same.astype(F32))


def _gates(ps, gparams):
    t = ps.shape[0]
    nb = t // LANE

    def body(ps_ref, gp_ref, out_ref, run_ref, tot_ref, runt_ref):
        p = ps_ref[...]
        is_b, is_a, is_f = _gate_lanes(p.shape)
        z = p + gp_ref[0:1, :]
        neg_exp_a = -jnp.exp(gp_ref[1:2, :])
        glog = neg_exp_a * _softplus(z)
        logf = -_softplus(-z)
        out_ref[...] = jnp.where(is_b, _sigmoid(p), jnp.where(is_a, glog, jnp.where(is_f, logf, 0.0)))
        tril, _, tril_c, _, same_c = _block_masks()
        off = jnp.zeros((1, LANE), F32)
        for b in range(nb):
            rows = slice(b * LANE, (b + 1) * LANE)
            blk = out_ref[rows, :]
            ga = jnp.where(is_a[:LANE], blk, 0.0)
            fb = _hdot(tril, jnp.where(is_f[:LANE], blk, 0.0)) + off
            run = fb + _hdot(tril_c, ga)
            run_ref[rows, :] = run
            runt_ref[:, rows] = run.T
            tot_ref[rows, :] = _hdot(same_c, ga)
            off = fb[LANE - 1:LANE, :]

    return pl.pallas_call(
        body, name="gates",
        out_shape=[jax.ShapeDtypeStruct((t, LANE), F32)] * 3 + [jax.ShapeDtypeStruct((LANE, t), F32)],
        compiler_params=_params(),
    )(ps, gparams)


def _gates_bwd(ps, gparams, dcol_g, dtot_g, dcol_f, drow_g, drow_f):
    t = ps.shape[0]
    nb = t // LANE

    def body(ps_ref, gp_ref, dcg_ref, dtg_ref, dcf_ref, drg_ref, drf_ref, dps_ref, sums_ref, dl_ref, d0_ref, tr_ref):
        p = ps_ref[...]
        is_b, is_a, is_f = _gate_lanes(p.shape)
        _, triu, _, triu_c, same_c = _block_masks()
        tr_ref[...] = jnp.zeros_like(tr_ref)
        off = jnp.zeros((1, LANE), F32)
        for b in reversed(range(nb)):
            rows = slice(b * LANE, (b + 1) * LANE)
            tr_ref[HEADS:2 * HEADS, :] = drg_ref[:, rows]
            tr_ref[2 * HEADS:3 * HEADS, :] = drf_ref[:, rows]
            d = dcg_ref[rows, :] + dcf_ref[rows, :] + tr_ref[...].T
            d0_ref[rows, :] = d
            dlf = _hdot(triu, jnp.where(is_f[:LANE], d, 0.0)) + off
            dla = (_hdot(triu_c, jnp.where(is_a[:LANE], d, 0.0))
                   + _hdot(same_c, jnp.where(is_a[:LANE], dtg_ref[rows, :], 0.0)))
            dl_ref[rows, :] = dlf + dla
            off = dlf[0:1, :]
        z = p + gp_ref[0:1, :]
        neg_exp_a = -jnp.exp(gp_ref[1:2, :])
        sb = _sigmoid(p)
        glog = neg_exp_a * _softplus(z)
        dl = dl_ref[...]
        dp = jnp.where(is_b, d0_ref[...] * sb * (1.0 - sb),
                       jnp.where(is_a, dl * neg_exp_a * _sigmoid(z), jnp.where(is_f, dl * _sigmoid(-z), 0.0)))
        dps_ref[...] = dp
        s_a = jnp.sum(jnp.where(is_a, dl * glog, 0.0), axis=0, keepdims=True)
        s_p = jnp.sum(jnp.where(is_b, 0.0, dp), axis=0, keepdims=True)
        row = lax.broadcasted_iota(jnp.int32, (8, LANE), 0)
        from_a = pltpu.roll(jnp.where(row == 0, s_a, jnp.where(row == 1, s_p, 0.0)), LANE - HEADS, 1)
        from_f = pltpu.roll(jnp.where(row == 2, s_p, 0.0), LANE - 2 * HEADS, 1)
        lane = lax.broadcasted_iota(jnp.int32, (8, LANE), 1)
        sums_ref[...] = jnp.where(lane < HEADS, from_a + from_f, 0.0)

    return pl.pallas_call(
        body, name="gates_bwd",
        out_shape=[jax.ShapeDtypeStruct((t, LANE), F32), jax.ShapeDtypeStruct((8, LANE), F32)],
        scratch_shapes=[pltpu.VMEM((t, LANE), F32), pltpu.VMEM((t, LANE), F32), pltpu.VMEM((LANE, LANE), F32)],
        compiler_params=_params(),
    )(ps, gparams, dcol_g, dtot_g, dcol_f, drow_g, drow_f)


def _conv(xv, w):
    acc = w[3:4, :] * xv
    for s in range(1, 4):
        acc += w[3 - s:4 - s, :] * _shift_down(xv, s)
    return acc


PREP_ROWS = 256
HALO = 8
PREP_UNROLL = 2


def _tile_loop(n, tile, init):
    if n % PREP_UNROLL:
        return lax.fori_loop(0, n, tile, init)

    def trip(g, carry):
        for u in range(PREP_UNROLL):
            carry = tile(g * PREP_UNROLL + u, carry)
        return carry

    return lax.fori_loop(0, n // PREP_UNROLL, trip, init)


def _tile_rows(r):
    return pl.ds(pl.multiple_of(r * PREP_ROWS, PREP_ROWS), PREP_ROWS)


def _gdn_prep(pm, conv_w):
    t = pm.shape[0]
    nj = WIDTH // LANE
    win = PREP_ROWS + HALO

    def body(x_ref, w_ref, o_ref, xp_ref):
        kind = pl.program_id(0)
        xp_ref[0:HALO, :] = jnp.zeros((HALO, LANE), F32)
        xp_ref[HALO:, :] = x_ref[...]
        w = w_ref[...]
        hs = _head_sum_matrix()

        def tile(r, _, normed):
            xw = xp_ref[pl.ds(pl.multiple_of(r * PREP_ROWS, PREP_ROWS), win), :]
            acc = _conv(xw, w)[HALO:]
            out = acc * _sigmoid(acc)
            if normed:
                out = out * lax.rsqrt(_group_sum(out * out, hs) + EPS)
            o_ref[0, 0, _tile_rows(r), :] = out[:, :DH]
            o_ref[0, 1, _tile_rows(r), :] = out[:, DH:]
            return 0

        @pl.when(kind < 2)
        def _():
            _tile_loop(t // PREP_ROWS, functools.partial(tile, normed=True), 0)

        @pl.when(kind == 2)
        def _():
            _tile_loop(t // PREP_ROWS, functools.partial(tile, normed=False), 0)

    return pl.pallas_call(
        body, name="gdn_prep", grid=(3, nj),
        in_specs=[pl.BlockSpec((t, LANE), lambda i, j: (0, i * nj + j)),
                  pl.BlockSpec((4, LANE), lambda i, j: (0, i * nj + j))],
        out_specs=pl.BlockSpec((1, 2, t, DH), lambda i, j: (i, j, 0, 0)),
        out_shape=jax.ShapeDtypeStruct((3, HEADS, t, DH), F32),
        scratch_shapes=[pltpu.VMEM((t + HALO, LANE), F32)],
        compiler_params=_params(("arbitrary", "arbitrary")),
    )(pm, conv_w)


def _gdn_prep_bwd(pm, conv_w, dqkv):
    t = pm.shape[0]
    nj = WIDTH // LANE
    win = PREP_ROWS + 2 * HALO

    def body(x_ref, w_ref, d_ref, dx_ref, dw_ref, xp_ref, dp_ref):
        kind = pl.program_id(0)
        zeros = jnp.zeros((HALO, LANE), F32)
        for ref in (xp_ref, dp_ref):
            ref[0:HALO, :] = zeros
            ref[HALO + t:, :] = zeros
        xp_ref[HALO:HALO + t, :] = x_ref[...]
        dp_ref[HALO:HALO + t, 0:DH] = d_ref[0, 0]
        dp_ref[HALO:HALO + t, DH:] = d_ref[0, 1]
        w = w_ref[...]
        hs = _head_sum_matrix()
        rows = lax.broadcasted_iota(jnp.int32, (win, LANE), 0)
        in_tile = (rows >= HALO) & (rows < HALO + PREP_ROWS)

        def tile(r, dw, normed):
            start = pl.multiple_of(r * PREP_ROWS, PREP_ROWS)
            xw = xp_ref[pl.ds(start, win), :]
            dy = dp_ref[pl.ds(start, win), :]
            acc = _conv(xw, w)
            sg = _sigmoid(acc)
            if normed:
                y = acc * sg
                rn = lax.rsqrt(_group_sum(y * y, hs) + EPS)
                yn = y * rn
                dy = rn * (dy - yn * _group_sum(dy * yn, hs))
            dacc = dy * sg * (1.0 + acc * (1.0 - sg))
            dx = w[3:4, :] * dacc
            for s in range(1, 4):
                dx += w[3 - s:4 - s, :] * _shift_up(dacc, s)
            dx_ref[_tile_rows(r), :] = dx[HALO:HALO + PREP_ROWS].astype(dx_ref.dtype)
            dm = jnp.where(in_tile, dacc, 0.0)
            return tuple(dw[i] + jnp.sum(dm * (xw if i == 3 else _shift_down(xw, 3 - i)), axis=0, keepdims=True)
                         for i in range(4))

        def run(normed):
            dw = _tile_loop(t // PREP_ROWS, functools.partial(tile, normed=normed),
                            tuple(jnp.zeros((1, LANE), F32) for _ in range(4)))
            for i in range(4):
                dw_ref[i:i + 1, :] = dw[i]

        pl.when(kind < 2)(functools.partial(run, True))
        pl.when(kind == 2)(functools.partial(run, False))

    return pl.pallas_call(
        body, name="gdn_prep_bwd", grid=(3, nj),
        in_specs=[pl.BlockSpec((t, LANE), lambda i, j: (0, i * nj + j)),
                  pl.BlockSpec((4, LANE), lambda i, j: (0, i * nj + j)),
                  pl.BlockSpec((1, 2, t, DH), lambda i, j: (i, j, 0, 0))],
        out_specs=[pl.BlockSpec((t, LANE), lambda i, j: (0, i * nj + j)),
                   pl.BlockSpec((4, LANE), lambda i, j: (0, i * nj + j))],
        out_shape=[jax.ShapeDtypeStruct((t, 3 * WIDTH), MXU), jax.ShapeDtypeStruct((4, 3 * WIDTH), F32)],
        scratch_shapes=[pltpu.VMEM((t + 2 * HALO, LANE), F32), pltpu.VMEM((t + 2 * HALO, LANE), F32)],
        compiler_params=_params(("arbitrary", "arbitrary")),
    )(pm, conv_w, dqkv)


FOX_COL0 = 4 * WIDTH // LANE


def _fox_prep(pm, nw):
    t = pm.shape[0]
    nj = WIDTH // LANE

    def body(x_ref, w_ref, o_ref):
        kind = pl.program_id(0)
        hs = _head_sum_matrix()
        wk = w_ref[pl.ds(kind, 1), :]

        def tile(r, _, normed):
            out = x_ref[_tile_rows(r), :]
            if normed:
                out = out * lax.rsqrt(_group_sum(out * out, hs) * (1.0 / DH) + EPS) * wk
            o_ref[0, 0, _tile_rows(r), :] = out[:, :DH]
            o_ref[0, 1, _tile_rows(r), :] = out[:, DH:]
            return 0

        @pl.when(kind < 2)
        def _():
            _tile_loop(t // PREP_ROWS, functools.partial(tile, normed=True), 0)

        @pl.when(kind == 2)
        def _():
            _tile_loop(t // PREP_ROWS, functools.partial(tile, normed=False), 0)

    return pl.pallas_call(
        body, name="fox_prep", grid=(3, nj),
        in_specs=[pl.BlockSpec((t, LANE), lambda i, j: (0, FOX_COL0 + i * nj + j)),
                  pl.BlockSpec((3, LANE), lambda i, j: (0, 0))],
        out_specs=pl.BlockSpec((1, 2, t, DH), lambda i, j: (i, j, 0, 0)),
        out_shape=jax.ShapeDtypeStruct((3, HEADS, t, DH), F32),
        compiler_params=_params(("arbitrary", "arbitrary")),
    )(pm, nw)


def _fox_prep_bwd(pm, nw, dqkv):
    t = pm.shape[0]
    nj = WIDTH // LANE

    def body(x_ref, w_ref, d_ref, dx_ref, dw_ref):
        kind = pl.program_id(0)
        hs = _head_sum_matrix()
        wk = w_ref[pl.ds(kind, 1), :]

        def tile(r, dw):
            xv = x_ref[_tile_rows(r), :]
            rn = lax.rsqrt(_group_sum(xv * xv, hs) * (1.0 / DH) + EPS)
            xn = xv * rn
            d = jnp.concatenate([d_ref[0, 0, _tile_rows(r), :], d_ref[0, 1, _tile_rows(r), :]], axis=1)
            g = d * wk
            dx_ref[_tile_rows(r), :] = (rn * (g - xn * _group_sum(g * xn, hs) * (1.0 / DH))).astype(dx_ref.dtype)
            return dw + jnp.sum(d * xn, axis=0, keepdims=True)

        def copy_tile(r, _):
            dx_ref[_tile_rows(r), :] = jnp.concatenate([d_ref[0, 0, _tile_rows(r), :], d_ref[0, 1, _tile_rows(r), :]],
                                                       axis=1).astype(dx_ref.dtype)
            return 0

        @pl.when(kind < 2)
        def _():
            dw_ref[0, 0] = _tile_loop(t // PREP_ROWS, tile, jnp.zeros((1, LANE), F32))

        @pl.when(kind == 2)
        def _():
            _tile_loop(t // PREP_ROWS, copy_tile, 0)
            dw_ref[0, 0] = jnp.zeros((1, LANE), F32)

    return pl.pallas_call(
        body, name="fox_prep_bwd", grid=(3, nj),
        in_specs=[pl.BlockSpec((t, LANE), lambda i, j: (0, FOX_COL0 + i * nj + j)),
                  pl.BlockSpec((3, LANE), lambda i, j: (0, 0)),
                  pl.BlockSpec((1, 2, t, DH), lambda i, j: (i, j, 0, 0))],
        out_specs=[pl.BlockSpec((t, LANE), lambda i, j: (0, i * nj + j)),
                   pl.BlockSpec((1, 1, 1, LANE), lambda i, j: (i, j, 0, 0))],
        out_shape=[jax.ShapeDtypeStruct((t, 3 * WIDTH), MXU), jax.ShapeDtypeStruct((3, nj, 1, LANE), F32)],
        compiler_params=_params(("arbitrary", "arbitrary")),
    )(pm, nw, dqkv)


SC = 256
CPS = SC // CHUNK
GDN_HP_FWD = 4
GDN_HP_BWD = 2
Q_SCALE = DH ** -0.5


def _sc_masks():
    ri = lax.broadcasted_iota(jnp.int32, (SC, SC), 0)
    ci = lax.broadcasted_iota(jnp.int32, (SC, SC), 1)
    same = (ri // CHUNK) == (ci // CHUNK)
    return same & (ri >= ci), same & (ri > ci), ri == ci


def _unit_lower_inverses(ms, eye):
    invs = [jnp.where(eye, 1.0, 0.0) + m for m in ms]
    ms = [_dot(m, m) for m in ms]
    for _ in range(4):
        both = [_dot(jnp.concatenate([inv, m], axis=0), m) for inv, m in zip(invs, ms)]
        invs = [inv + b[:SC] for inv, b in zip(invs, both)]
        ms = [b[SC:] for b in both]
    return [inv + _dot(inv, m) for inv, m in zip(invs, ms)]


def _lane_col(blk, lane_idx):
    lane = lax.broadcasted_iota(jnp.int32, blk.shape, 1)
    return jnp.sum(jnp.where(lane == lane_idx, blk, 0.0), axis=1, keepdims=True)


def _to_lane(col, lane_idx):
    lane = lax.broadcasted_iota(jnp.int32, (col.shape[0], LANE), 1)
    return jnp.where(lane == lane_idx, col, 0.0)


def _gdn_columns(gates_ref, run_ref, tot_ref, runt_ref, rows, h):
    return (_lane_col(gates_ref[rows, :], h), _lane_col(run_ref[rows, :], HEADS + h),
            _lane_col(tot_ref[rows, :], HEADS + h), runt_ref[pl.ds(h, 1), rows])


def _gdn_local(q, k, beta, gc, gl, grow, causal, with_kk=True):
    decay = jnp.exp(jnp.where(causal, gc - grow, -1e30))
    egc = jnp.exp(gc)
    ekd = jnp.exp(gl - gc)
    qs = q * Q_SCALE
    kb = k * beta
    if with_kk:
        both = _dot_nt(jnp.concatenate([kb, qs], axis=0), k)
        kk, qk = both[:SC], both[SC:]
    else:
        kk, qk = None, _dot_nt(qs, k)
    return beta, gl, decay, egc, ekd, qs, kb, kk, qk, jnp.where(causal, qk * decay, 0.0)


def _chunk_rows(c):
    return pl.ds(c * CHUNK if isinstance(c, int) else pl.multiple_of(c * CHUNK, CHUNK), CHUNK)


def _sc_rows(b):
    return pl.ds(b * SC if isinstance(b, int) else pl.multiple_of(b * SC, SC), SC)


def _gdn_fwd(qkv, gates, run, tot, run_t):
    t = qkv.shape[2]
    nc = t // CHUNK
    nsc = t // SC

    hp_n = GDN_HP_FWD
    heads = range(hp_n)

    def body(qkv_ref, gates_ref, run_ref, tot_ref, runt_ref, o_ref, st_ref, inv_ref, kc_s, qc_s, g_s, au_s):
        hp = pl.program_id(0)
        causal, strict, eye = _sc_masks()

        def local(b, _):
            rows = _sc_rows(b)
            loc = [_gdn_local(qkv_ref[0, hh, rows, :], qkv_ref[1, hh, rows, :],
                              *_gdn_columns(gates_ref, run_ref, tot_ref, runt_ref, rows, hp * hp_n + hh), causal)
                   for hh in heads]
            invs = _unit_lower_inverses([-jnp.where(strict, l[7] * l[2], 0.0) for l in loc], eye)
            uws = []
            for hh in heads:
                beta, _, _, egc, _, _, kb, _, _, _ = loc[hh]
                inv_ref[hh, rows, :] = invs[hh].astype(inv_ref.dtype)
                uws.append(_dot(invs[hh], jnp.concatenate([qkv_ref[2, hh, rows, :] * beta, kb * egc], axis=1)))
            for hh in heads:
                _, _, _, egc, ekd, qs, _, _, _, attn = loc[hh]
                auw = _dot(attn, uws[hh])
                g_s[hh, rows, :] = qs * egc - auw[:, DH:]
                au_s[hh, rows, :] = auw[:, :DH]
                kd = qkv_ref[1, hh, rows, :] * ekd
                for j in range(CPS):
                    sl = slice(j * CHUNK, (j + 1) * CHUNK)
                    both = _dot_tn(kd[sl], uws[hh][sl])
                    kc_s[hh, b * CPS + j] = both[:, DH:]
                    qc_s[hh, b * CPS + j] = both[:, :DH]
            return 0

        def step(c, states):
            rows = _chunk_rows(c)
            tot_row = tot_ref[pl.ds(c * CHUNK, 1), :]
            new = []
            for hh in heads:
                s = states[hh]
                st_ref[hh, c] = s
                o_ref[hh, rows, :] = _dot(g_s[hh, rows, :], s) + au_s[hh, rows, :]
                egl = jnp.exp(_lane_col(tot_row, HEADS + hp * hp_n + hh))
                new.append(egl * s - _dot(kc_s[hh, c], s) + qc_s[hh, c])
            return tuple(new)

        def steps_of(b, states):
            for j in range(CPS):
                states = step(b * CPS + j, states)
            return states

        def fused(b, states):
            states = steps_of(b - 1, states)
            local(b, 0)
            return states

        local(0, 0)
        states = lax.fori_loop(1, nsc, fused, tuple(jnp.zeros((DH, DH), F32) for _ in heads))
        steps_of(nsc - 1, states)

    whole = pl.BlockSpec((t, LANE), lambda h: (0, 0))
    once = dict(pipeline_mode=pl.Buffered(1))
    return pl.pallas_call(
        body, name="gdn_fwd", grid=(HEADS // hp_n,),
        in_specs=[pl.BlockSpec((3, hp_n, t, DH), lambda h: (0, h, 0, 0), **once), whole, whole, whole,
                  pl.BlockSpec((HEADS, t), lambda h: (1, 0))],
        out_specs=[pl.BlockSpec((hp_n, t, DH), lambda h: (h, 0, 0), **once),
                   pl.BlockSpec((hp_n, nc, DH, DH), lambda h: (h, 0, 0, 0), **once),
                   pl.BlockSpec((hp_n, t, SC), lambda h: (h, 0, 0), **once)],
        out_shape=[jax.ShapeDtypeStruct((HEADS, t, DH), F32), jax.ShapeDtypeStruct((HEADS, nc, DH, DH), F32),
                   jax.ShapeDtypeStruct((HEADS, t, SC), MXU)],
        scratch_shapes=[pltpu.VMEM((hp_n, nc, DH, DH), F32), pltpu.VMEM((hp_n, nc, DH, DH), F32),
                        pltpu.VMEM((hp_n, t, DH), F32), pltpu.VMEM((hp_n, t, DH), F32)],
        compiler_params=_params(("arbitrary",)),
    )(qkv, gates, run, tot, run_t)


def _gdn_bwd(qkv, gates, run, tot, run_t, inv, states, do):
    t = qkv.shape[2]
    nc = t // CHUNK
    nsc = t // SC

    hp_n = GDN_HP_BWD
    heads = range(hp_n)

    def body(qkv_ref, gates_ref, run_ref, tot_ref, runt_ref, inv_ref, st_ref, do_ref,
             dqkv_ref, dcol_ref, dtot_ref, drow_ref, uw_s, kc_s, h_s, dsn_s):
        hp = pl.program_id(0)
        causal, strict, _ = _sc_masks()

        @pl.when(hp == 0)
        def _():
            dcol_ref[...] = jnp.zeros_like(dcol_ref)
            dtot_ref[...] = jnp.zeros_like(dtot_ref)

        def local_of(hh, rows, with_kk=True):
            return _gdn_local(qkv_ref[0, hh, rows, :], qkv_ref[1, hh, rows, :],
                              *_gdn_columns(gates_ref, run_ref, tot_ref, runt_ref, rows, hp * hp_n + hh), causal, with_kk)

        def local(b, _):
            rows = _sc_rows(b)
            loc = [local_of(hh, rows, with_kk=False) for hh in heads]
            us, ws = [], []
            for hh in heads:
                beta, _, _, egc, _, _, kb, _, _, _ = loc[hh]
                inv_b = inv_ref[hh, rows, :]
                us.append(_dot(inv_b, qkv_ref[2, hh, rows, :] * beta))
                ws.append(_dot(inv_b, kb * egc))
            gs = [loc[hh][5] * loc[hh][3] - _dot(loc[hh][9], ws[hh]) for hh in heads]
            for hh in heads:
                uw_s[hh, rows, :] = jnp.concatenate([us[hh], ws[hh]], axis=1)
                kd = qkv_ref[1, hh, rows, :] * loc[hh][4]
                dout = do_ref[hh, rows, :]
                for j in range(CPS):
                    sl = slice(j * CHUNK, (j + 1) * CHUNK)
                    kc_s[hh, b * CPS + j] = _dot_tn(kd[sl], ws[hh][sl])
                    h_s[hh, b * CPS + j] = _dot_tn(gs[hh][sl], dout[sl])
            return 0

        def step(c, dss):
            tot_row = tot_ref[pl.ds(c * CHUNK, 1), :]
            new = []
            for hh in heads:
                ds = dss[hh]
                dsn_s[hh, c] = ds
                egl = jnp.exp(_lane_col(tot_row, HEADS + hp * hp_n + hh))
                new.append(egl * ds - _dot_tn(kc_s[hh, c], ds) + h_s[hh, c])
            return tuple(new)

        def steps_of(b, dss):
            for j in reversed(range(CPS)):
                dss = step(b * CPS + j, dss)
            return dss

        def fused(i, dss):
            b = nsc - 2 - i
            dss = steps_of(b + 1, dss)
            local(b, 0)
            return dss

        local(nsc - 1, 0)
        dss = lax.fori_loop(0, nsc - 1, fused, tuple(jnp.zeros((DH, DH), F32) for _ in heads))
        steps_of(0, dss)

        def back(b, _):
            rows = _sc_rows(b)
            first = lax.broadcasted_iota(jnp.int32, (CHUNK, 1), 0) == 0
            loc = [local_of(hh, rows) for hh in heads]
            sign = jnp.where(lax.broadcasted_iota(jnp.int32, (1, LANE), 1) < DH, 1.0, -1.0)
            mid = []
            for hh in heads:
                beta, gl, decay, egc, ekd, qs, kb, kk, qk, attn = loc[hh]
                uw = uw_s[hh, rows, :]
                kd = qkv_ref[1, hh, rows, :] * ekd
                dout = do_ref[hh, rows, :]
                dg_p, dkd_p, duw_p, dgl_p = [], [], [], []
                for j in range(CPS):
                    sl = slice(j * CHUNK, (j + 1) * CHUNK)
                    s = st_ref[hh, b * CPS + j]
                    dsn = dsn_s[hh, b * CPS + j]
                    both = _dot_nt(jnp.concatenate([dsn, dout[sl]], axis=0), s)
                    dg_p.append(both[CHUNK:])
                    dsc = jnp.concatenate([dsn, -both[:CHUNK]], axis=1)
                    dkd_p.append(_dot_nt(uw[sl], dsc))
                    duw_p.append(_dot(kd[sl], dsc))
                    degl = jnp.sum(jnp.sum(s * dsn, axis=1, keepdims=True), axis=0, keepdims=True)
                    dgl_p.append(jnp.where(first, degl * jnp.exp(gl[j * CHUNK:j * CHUNK + 1, :]), 0.0))
                dg, dkd = jnp.concatenate(dg_p, axis=0), jnp.concatenate(dkd_p, axis=0)
                dod = jnp.concatenate([dout, dg], axis=1)
                da = jnp.where(causal, _dot_nt(dod, uw * sign), 0.0)
                duw = jnp.concatenate(duw_p, axis=0) + _dot_tn(attn, dod) * sign
                mid.append((kd, dg, dkd, da, duw, jnp.concatenate(dgl_p, axis=0)))
            inv_ts = [inv_ref[hh, rows, :].astype(F32).T for hh in heads]
            its = [_dot(inv_ts[hh], mid[hh][4]) for hh in heads]
            dinvs = []
            for hh in heads:
                beta, _, _, egc, _, _, kb, _, _, _ = loc[hh]
                dinvs.append(_dot_nt(mid[hh][4], jnp.concatenate([qkv_ref[2, hh, rows, :] * beta, kb * egc], axis=1)))
            half = [_dot(inv_ts[hh], dinvs[hh]) for hh in heads]
            dls = [jnp.where(strict, -_dot(half[hh], inv_ts[hh]), 0.0) for hh in heads]
            for hh in heads:
                head = hp * hp_n + hh
                beta, gl, decay, egc, ekd, qs, kb, kk, qk, attn = loc[hh]
                kd, dg, dkd, da, _, dgl_first = mid[hh]
                k, v = qkv_ref[1, hh, rows, :], qkv_ref[2, hh, rows, :]
                dvb, dkbe = its[hh][:, :DH], its[hh][:, DH:]
                dl = dls[hh]
                dlogd = (dl * kk + da * qk) * decay
                dd = jnp.concatenate([dl * decay, da * decay], axis=0)
                ddk = _dot(dd, k)
                dkb = ddk[:SC] + dkbe * egc
                dqs = ddk[SC:] + dg * egc
                dk = _dot_tn(dd, jnp.concatenate([kb, qs], axis=0)) + dkd * ekd + dkb * beta
                dkd_kd = jnp.sum(dkd * kd, axis=1, keepdims=True)
                narrow = dg * (qs * egc) + dkbe * (kb * egc)
                wide = dlogd[:, :LANE] + dlogd[:, LANE:] + jnp.concatenate([narrow, jnp.zeros((SC, LANE - DH), F32)], axis=1)
                dgc = jnp.sum(wide, axis=1, keepdims=True) - dkd_kd
                dbeta = jnp.sum(dkb * k + dvb * v, axis=1, keepdims=True)
                dqkv_ref[0, hh, rows, :] = dqs * Q_SCALE
                dqkv_ref[1, hh, rows, :] = dk
                dqkv_ref[2, hh, rows, :] = dvb * beta
                dcol_ref[rows, :] += _to_lane(dbeta, head) + _to_lane(dgc, HEADS + head)
                dtot_ref[rows, :] += _to_lane(dkd_kd + dgl_first, HEADS + head)
                drow_ref[pl.ds(head, 1), rows] = -jnp.sum(dlogd, axis=0, keepdims=True)
            return 0

        lax.fori_loop(0, nsc, back, 0)

    whole = pl.BlockSpec((t, LANE), lambda h: (0, 0))
    rowspec = pl.BlockSpec((HEADS, t), lambda h: (0, 0))
    sq = pltpu.VMEM((hp_n, nc, DH, DH), F32)
    per_head = pltpu.VMEM((hp_n, t, 2 * DH), F32)
    return pl.pallas_call(
        body, name="gdn_bwd", grid=(HEADS // hp_n,),
        in_specs=[pl.BlockSpec((3, hp_n, t, DH), lambda h: (0, h, 0, 0)), whole, whole, whole,
                  pl.BlockSpec((HEADS, t), lambda h: (1, 0)),
                  pl.BlockSpec((hp_n, t, SC), lambda h: (h, 0, 0)), pl.BlockSpec((hp_n, nc, DH, DH), lambda h: (h, 0, 0, 0)),
                  pl.BlockSpec((hp_n, t, DH), lambda h: (h, 0, 0))],
        out_specs=[pl.BlockSpec((3, hp_n, t, DH), lambda h: (0, h, 0, 0)), whole, whole, rowspec],
        out_shape=[jax.ShapeDtypeStruct((3, HEADS, t, DH), F32), jax.ShapeDtypeStruct((t, LANE), F32),
                   jax.ShapeDtypeStruct((t, LANE), F32), jax.ShapeDtypeStruct((HEADS, t), F32)],
        scratch_shapes=[per_head, sq, sq, sq],
        compiler_params=_params(("arbitrary",)),
    )(qkv, gates, run, tot, run_t, inv, states, do)


FOX_HP_FWD = 4
FOX_HP_BWD = 4


def _wide_t(a):
    r = a.shape[0]
    return jnp.concatenate([a, jnp.zeros((r, LANE - DH), F32)], axis=1).T[:DH]


def _tall_t(a):
    r = a.shape[1]
    return jnp.concatenate([a, jnp.zeros((LANE - DH, r), F32)], axis=0).T[:, :DH]


def _key_side_f(run_blk, head, qb):
    col = jnp.broadcast_to(_lane_col(run_blk, 2 * HEADS + head), (run_blk.shape[0], LANE))
    return jnp.concatenate([col] * (qb // LANE), axis=1)


def _diag_mask(qb):
    return lax.broadcasted_iota(jnp.int32, (qb, qb), 0) <= lax.broadcasted_iota(jnp.int32, (qb, qb), 1)


def _fox_fwd(qkv, run, run_t, carry=None):
    t = qkv.shape[2]
    qb = min(QB, t)
    nq = t // qb
    hp_n = FOX_HP_FWD
    c_in, c_in_specs, c_out_shape, c_out_specs, c_sem = _carry_operands(*carry) if carry else ([], [], [], [], None)
    n_c = len(c_in)

    def body(*refs):
        q_ref, k_ref, v_ref, run_ref, runt_ref = refs[:5]
        o_ref, lse_ref = refs[5 + n_c:7 + n_c]
        s0 = 7 + n_c + len(c_out_shape)
        kb_s, vt_s, raw_s, m_s, l_s, acc_s = refs[s0:s0 + 6]
        hp = pl.program_id(0)
        i = pl.program_id(1)

        if carry:
            _carry_step(carry[0], refs[5:5 + n_c], refs[7 + n_c], refs[8 + n_c], refs[-1], hp * nq + i,
                        (HEADS // hp_n) * nq)

        @pl.when(i == 0)
        def _():
            for hh in range(hp_n):
                kb_s[hh] = k_ref[0, hh].astype(MXU)
                for b in range(nq):
                    rows = slice(b * qb, (b + 1) * qb)
                    vt_s[hh, :, rows] = _wide_t(v_ref[0, hh, rows, :]).astype(MXU)

        qrows = pl.ds(pl.multiple_of(i * qb, qb), qb)
        qs = [(q_ref[0, hh] * Q_SCALE).astype(MXU) for hh in range(hp_n)]
        fq = [runt_ref[pl.ds(hp * hp_n + hh, 1), qrows] for hh in range(hp_n)]

        def block_rows(j):
            return pl.ds(pl.multiple_of(j * qb, qb), qb)

        def scores(j):
            for hh in range(hp_n):
                raw_s[j % 2, hh] = _dot_nt(kb_s[hh, block_rows(j), :], qs[hh])

        def absorb(j, diagonal):
            rows = block_rows(j)
            run_blk = run_ref[rows, :]
            stats, pv = [], []
            for hh in range(hp_n):
                m, l = m_s[hh], l_s[hh]
                st = raw_s[j % 2, hh] + (fq[hh] - _key_side_f(run_blk, hp * hp_n + hh, qb))
                if diagonal:
                    st = jnp.where(_diag_mask(qb), st, -1e30)
                m_new = jnp.maximum(m, jnp.max(st, axis=0, keepdims=True))
                p = jnp.exp(st - m_new)
                alpha = jnp.exp(m - m_new)
                stats.append((m_new, alpha * l + jnp.sum(p, axis=0, keepdims=True), alpha))
                pv.append(_dot(vt_s[hh, :, rows], p))
            for hh in range(hp_n):
                m_s[hh], l_s[hh] = stats[hh][0], stats[hh][1]
                acc_s[hh] = stats[hh][2] * acc_s[hh] + pv[hh]

        def kstep(j, _):
            scores(j + 1)
            absorb(j, False)
            return 0

        for hh in range(hp_n):
            m_s[hh] = jnp.full((1, qb), -1e30, F32)
            l_s[hh] = jnp.zeros((1, qb), F32)
            acc_s[hh] = jnp.zeros((DH, qb), F32)
        scores(0)
        lax.fori_loop(0, i, kstep, 0)
        absorb(i, True)
        for hh in range(hp_n):
            l = l_s[hh]
            o_ref[hh] = _tall_t(acc_s[hh] / l)
            lse_ref[pl.ds(hp * hp_n + hh, 1), qrows] = m_s[hh] + jnp.log(l)

    out = pl.pallas_call(
        body, name="fox_fwd", grid=(HEADS // hp_n, nq),
        in_specs=[pl.BlockSpec((1, hp_n, qb, DH), lambda h, i: (0, h, i, 0)),
                  pl.BlockSpec((1, hp_n, t, DH), lambda h, i: (1, h, 0, 0)),
                  pl.BlockSpec((1, hp_n, t, DH), lambda h, i: (2, h, 0, 0)),
                  pl.BlockSpec((t, LANE), lambda h, i: (0, 0)),
                  pl.BlockSpec((HEADS, t), lambda h, i: (2, 0))] + c_in_specs,
        out_specs=[pl.BlockSpec((hp_n, qb, DH), lambda h, i: (h, i, 0)),
                   pl.BlockSpec((HEADS, t), lambda h, i: (0, 0))] + c_out_specs,
        out_shape=[jax.ShapeDtypeStruct((HEADS, t, DH), F32), jax.ShapeDtypeStruct((HEADS, t), F32)] + c_out_shape,
        input_output_aliases={5 + j: 4 + j for j in range(n_c)},
        scratch_shapes=[pltpu.VMEM((hp_n, t, DH), MXU), pltpu.VMEM((hp_n, DH, t), MXU), pltpu.VMEM((2, hp_n, qb, qb), F32),
                        pltpu.VMEM((hp_n, 1, qb), F32), pltpu.VMEM((hp_n, 1, qb), F32), pltpu.VMEM((hp_n, DH, qb), F32)]
        + ([c_sem] if carry else []),
        compiler_params=_params(("arbitrary", "arbitrary")),
    )(qkv, qkv, qkv, run, run_t, *c_in)
    return (out[0], out[1], _carry_state(out[2:])) if carry else (out[0], out[1])


def _fox_bwd(qkv, run, run_t, o, lse, do, carry=None):
    t = qkv.shape[2]
    qb = min(QB, t)
    nq = t // qb
    hp_n = FOX_HP_BWD
    c_in, c_in_specs, c_out_shape, c_out_specs, c_sem = _carry_operands(*carry) if carry else ([], [], [], [], None)
    n_c = len(c_in)

    def body(*refs):
        q_ref, k_ref, v_ref, run_ref, runt_ref, o_ref, lse_ref, do_ref = refs[:8]
        dqkv_ref, dcol_ref, drow_ref = refs[8 + n_c:11 + n_c]
        s0 = 11 + n_c + len(c_out_shape)
        dqt_s, raw_s, dpt_s, dro_s, dk_s, dv_s, dsum_s = refs[s0:s0 + 7]
        hp = pl.program_id(0)
        j = pl.program_id(1)

        if carry:
            _carry_step(carry[0], refs[8:8 + n_c], refs[11 + n_c], refs[12 + n_c], refs[-1], hp * nq + j,
                        (HEADS // hp_n) * nq)

        @pl.when(j == 0)
        def _():
            dqt_s[...] = jnp.zeros_like(dqt_s)

        @pl.when((j == 0) & (hp == 0))
        def _():
            dcol_ref[...] = jnp.zeros_like(dcol_ref)
            drow_ref[...] = jnp.zeros_like(drow_ref)

        krows = pl.ds(pl.multiple_of(j * qb, qb), qb)
        run_blk = run_ref[krows, :]
        ones8 = jnp.ones((8, DH), MXU)
        kb, kt, vb, fk = [], [], [], []
        for hh in range(hp_n):
            kf = k_ref[0, hh]
            kb.append(kf.astype(MXU))
            kt.append(_wide_t(kf).astype(MXU))
            vb.append(v_ref[0, hh].astype(MXU))
            fk.append(_key_side_f(run_blk, hp * hp_n + hh, qb))

        def block_rows(i):
            return pl.ds(i * qb if isinstance(i, int) else pl.multiple_of(i * qb, qb), qb)

        def products(i):
            rows = block_rows(i)
            slot = i % 2
            for hh in range(hp_n):
                dout = do_ref[hh, rows, :]
                x = dout * o_ref[hh, rows, :]
                x_hi = x.astype(MXU)
                raw_s[slot, hh] = _dot_nt(kb[hh], q_ref[0, hh, rows, :] * Q_SCALE)
                dpt_s[slot, hh] = _dot_nt(vb[hh], dout)
                dro_s[slot, hh] = _dot_nt(ones8, x_hi) + _dot_nt(ones8, x - x_hi.astype(F32))

        def absorb(i, diagonal):
            rows = block_rows(i)
            slot = i % 2
            pieces = []
            for hh in range(hp_n):
                head = hp * hp_n + hh
                raw, dpt, drow = raw_s[slot, hh], dpt_s[slot, hh], dro_s[slot, hh, 0:1, :]
                off = runt_ref[pl.ds(head, 1), rows] - lse_ref[pl.ds(head, 1), rows]
                st = raw + (off - fk[hh])
                if diagonal:
                    st = jnp.where(_diag_mask(qb), st, -1e30)
                pt = jnp.exp(st)
                dst = pt * (dpt - drow)
                drow_ref[pl.ds(head, 1), rows] += jnp.sum(dst, axis=0, keepdims=True)
                folded = dst[:, 0:LANE]
                for c in range(1, qb // LANE):
                    folded = folded + dst[:, c * LANE:(c + 1) * LANE]
                pieces.append((_dot(dst, q_ref[0, hh, rows, :] * Q_SCALE), _dot(pt, do_ref[hh, rows, :]),
                               _dot(kt[hh], dst), folded))
            for hh in range(hp_n):
                dqt_s[hh, :, rows] += pieces[hh][2]
                if diagonal:
                    dk_s[hh], dv_s[hh], dsum_s[hh] = pieces[hh][0], pieces[hh][1], pieces[hh][3]
                else:
                    dk_s[hh] += pieces[hh][0]
                    dv_s[hh] += pieces[hh][1]
                    dsum_s[hh] += pieces[hh][3]

        def qstep(i, _):
            products(i + 1)
            absorb(i, False)
            return 0

        products(j)
        products(jnp.minimum(j + 1, nq - 1))
        absorb(j, True)
        lax.fori_loop(j + 1, nq - 1, qstep, 0)

        @pl.when(j < nq - 1)
        def _():
            absorb(nq - 1, False)
        for hh in range(hp_n):
            dqkv_ref[1, hh, krows, :] = dk_s[hh]
            dqkv_ref[2, hh, krows, :] = dv_s[hh]
            dcol_ref[krows, :] += _to_lane(-jnp.sum(dsum_s[hh], axis=1, keepdims=True), 2 * HEADS + hp * hp_n + hh)

        @pl.when(j == nq - 1)
        def _():
            for hh in range(hp_n):
                for b in range(nq):
                    rows = slice(b * qb, (b + 1) * qb)
                    dqkv_ref[0, hh, rows, :] = _tall_t(dqt_s[hh, :, rows]) * Q_SCALE

    once = dict(pipeline_mode=pl.Buffered(1))
    full = pl.BlockSpec((hp_n, t, DH), lambda h, j: (h, 0, 0), **once)
    rows8 = pl.BlockSpec((HEADS, t), lambda h, j: (0, 0))
    out = pl.pallas_call(
        body, name="fox_bwd", grid=(HEADS // hp_n, nq),
        in_specs=[pl.BlockSpec((1, hp_n, t, DH), lambda h, j: (0, h, 0, 0), **once),
                  pl.BlockSpec((1, hp_n, qb, DH), lambda h, j: (1, h, j, 0)),
                  pl.BlockSpec((1, hp_n, qb, DH), lambda h, j: (2, h, j, 0)),
                  pl.BlockSpec((t, LANE), lambda h, j: (0, 0), **once), pl.BlockSpec((HEADS, t), lambda h, j: (2, 0)),
                  full, rows8, full] + c_in_specs,
        out_specs=[pl.BlockSpec((3, hp_n, t, DH), lambda h, j: (0, h, 0, 0), **once),
                   pl.BlockSpec((t, LANE), lambda h, j: (0, 0)), rows8] + c_out_specs,
        out_shape=[jax.ShapeDtypeStruct((3, HEADS, t, DH), F32), jax.ShapeDtypeStruct((t, LANE), F32),
                   jax.ShapeDtypeStruct((HEADS, t), F32)] + c_out_shape,
        input_output_aliases={8 + j: 5 + j for j in range(n_c)},
        scratch_shapes=[pltpu.VMEM((hp_n, DH, t), F32), pltpu.VMEM((2, hp_n, qb, qb), F32), pltpu.VMEM((2, hp_n, qb, qb), F32),
                        pltpu.VMEM((2, hp_n, 8, qb), F32), pltpu.VMEM((hp_n, qb, DH), F32), pltpu.VMEM((hp_n, qb, DH), F32),
                        pltpu.VMEM((hp_n, qb, LANE), F32)] + ([c_sem] if carry else []),
        compiler_params=_params(("arbitrary", "arbitrary")),
    )(qkv, qkv, qkv, run, run_t, o, lse, do, *c_in)
    return (out[0], out[1], out[2], _carry_state(out[3:])) if carry else tuple(out)


Z_COL0 = 3 * WIDTH // LANE
FGATE_COL0 = 7 * WIDTH // LANE


def _gdn_post(o, pm, onw):
    t = pm.shape[0]

    def body(o_ref, z_ref, w_ref, m_ref):
        z = z_ref[...]
        sz = z * _sigmoid(z)
        halves = []
        for hh in range(2):
            ov = o_ref[hh]
            n = ov * lax.rsqrt(jnp.mean(ov * ov, axis=-1, keepdims=True) + EPS) * w_ref[...]
            halves.append(n * sz[:, hh * DH:(hh + 1) * DH])
        m_ref[...] = jnp.concatenate(halves, axis=1).astype(m_ref.dtype)

    return pl.pallas_call(
        body, name="gdn_post", grid=(WIDTH // LANE,),
        in_specs=[pl.BlockSpec((2, t, DH), lambda j: (j, 0, 0)), pl.BlockSpec((t, LANE), lambda j: (0, Z_COL0 + j)),
                  pl.BlockSpec((1, DH), lambda j: (0, 0))],
        out_specs=pl.BlockSpec((t, LANE), lambda j: (0, j)),
        out_shape=jax.ShapeDtypeStruct((t, WIDTH), MXU),
        compiler_params=_params(("arbitrary",)),
    )(o, pm, onw)


def _gdn_post_bwd(o, pm, onw, dmix):
    t = pm.shape[0]

    def body(o_ref, z_ref, w_ref, dm_ref, do_ref, dz_ref, dw_ref):
        @pl.when(pl.program_id(0) == 0)
        def _():
            dw_ref[...] = jnp.zeros_like(dw_ref)

        z = z_ref[...]
        sg = _sigmoid(z)
        sz = z * sg
        dsz = sg * (1.0 + z * (1.0 - sg))
        dm = dm_ref[...]
        for hh in range(2):
            cols = slice(hh * DH, (hh + 1) * DH)
            ov = o_ref[hh]
            r = lax.rsqrt(jnp.mean(ov * ov, axis=-1, keepdims=True) + EPS)
            xn = ov * r
            dmh = dm[:, cols]
            dn = dmh * sz[:, cols]
            dz_ref[:, cols] = (dmh * (xn * w_ref[...]) * dsz[:, cols]).astype(dz_ref.dtype)
            dw_ref[...] += jnp.sum(dn * xn, axis=0, keepdims=True)
            g = dn * w_ref[...]
            do_ref[hh] = r * (g - xn * jnp.mean(g * xn, axis=-1, keepdims=True))

    return pl.pallas_call(
        body, name="gdn_post_bwd", grid=(WIDTH // LANE,),
        in_specs=[pl.BlockSpec((2, t, DH), lambda j: (j, 0, 0)), pl.BlockSpec((t, LANE), lambda j: (0, Z_COL0 + j)),
                  pl.BlockSpec((1, DH), lambda j: (0, 0)), pl.BlockSpec((t, LANE), lambda j: (0, j))],
        out_specs=[pl.BlockSpec((2, t, DH), lambda j: (j, 0, 0)), pl.BlockSpec((t, LANE), lambda j: (0, j)),
                   pl.BlockSpec((1, DH), lambda j: (0, 0))],
        out_shape=[jax.ShapeDtypeStruct((HEADS, t, DH), F32), jax.ShapeDtypeStruct((t, WIDTH), MXU),
                   jax.ShapeDtypeStruct((1, DH), F32)],
        compiler_params=_params(("arbitrary",)),
    )(o, pm, onw, dmix)


def _fox_post(o, pm):
    t = pm.shape[0]

    def body(o_ref, g_ref, m_ref):
        m_ref[...] = (jnp.concatenate([o_ref[0], o_ref[1]], axis=1) * _sigmoid(g_ref[...])).astype(m_ref.dtype)

    return pl.pallas_call(
        body, name="fox_post", grid=(WIDTH // LANE,),
        in_specs=[pl.BlockSpec((2, t, DH), lambda j: (j, 0, 0)), pl.BlockSpec((t, LANE), lambda j: (0, FGATE_COL0 + j))],
        out_specs=pl.BlockSpec((t, LANE), lambda j: (0, j)),
        out_shape=jax.ShapeDtypeStruct((t, WIDTH), MXU),
        compiler_params=_params(("arbitrary",)),
    )(o, pm)


def _fox_post_bwd(o, pm, dmix):
    t = pm.shape[0]

    def body(o_ref, g_ref, dm_ref, do_ref, dg_ref):
        sg = _sigmoid(g_ref[...])
        dm = dm_ref[...]
        for hh in range(2):
            cols = slice(hh * DH, (hh + 1) * DH)
            do_ref[hh] = dm[:, cols] * sg[:, cols]
            dg_ref[:, cols] = (dm[:, cols] * o_ref[hh] * (sg * (1.0 - sg))[:, cols]).astype(dg_ref.dtype)

    return pl.pallas_call(
        body, name="fox_post_bwd", grid=(WIDTH // LANE,),
        in_specs=[pl.BlockSpec((2, t, DH), lambda j: (j, 0, 0)), pl.BlockSpec((t, LANE), lambda j: (0, FGATE_COL0 + j)),
                  pl.BlockSpec((t, LANE), lambda j: (0, j))],
        out_specs=[pl.BlockSpec((2, t, DH), lambda j: (j, 0, 0)), pl.BlockSpec((t, LANE), lambda j: (0, j))],
        out_shape=[jax.ShapeDtypeStruct((HEADS, t, DH), F32), jax.ShapeDtypeStruct((t, WIDTH), MXU)],
        compiler_params=_params(("arbitrary",)),
    )(o, pm, dmix)


def _tail(x, mixg, mixf, tgt, wo, n2w, wg_t, wu_t, wd, fw):
    t, d = x.shape
    dff = wd.shape[0]
    tb = min(TB, t)

    def body(x_ref, mg_ref, mf_ref, t_ref, wo_ref, n2_ref, wg_ref, wu_ref, wd_ref, fw_ref,
             h2_ref, act_ref, dgate_ref, dup_ref, dx3_ref, dx2_ref, dmg_ref, dmf_ref, dn2_ref, dfw_ref, loss_ref):
        @pl.when(pl.program_id(0) == 0)
        def _():
            dn2_ref[...] = jnp.zeros_like(dn2_ref)
            dfw_ref[...] = jnp.zeros_like(dfw_ref)
            loss_ref[...] = jnp.zeros_like(loss_ref)

        x2 = x_ref[...] + _dot(mg_ref[...], wo_ref[0:WIDTH, :]) + _dot(mf_ref[...], wo_ref[WIDTH:2 * WIDTH, :])
        r2 = lax.rsqrt(jnp.mean(x2 * x2, axis=-1, keepdims=True) + EPS)
        xn2 = x2 * r2
        h2 = (xn2 * n2_ref[...]).astype(MXU)
        h2_ref[...] = h2
        gate = _dot_nt(h2, wg_ref[...])
        up = _dot_nt(h2, wu_ref[...])
        sg = _sigmoid(gate)
        sl = gate * sg
        act = (sl * up).astype(MXU)
        act_ref[...] = act
        x3 = x2 + _dot(act, wd_ref[...])
        r3 = lax.rsqrt(jnp.mean(x3 * x3, axis=-1, keepdims=True) + EPS)
        xn3 = x3 * r3
        err = xn3 * fw_ref[...] - t_ref[...]
        loss_ref[...] += 0.5 * jnp.sum(jnp.mean(err * err, axis=-1, keepdims=True), axis=0, keepdims=True)
        dy = err * (1.0 / d)
        dfw_ref[...] += jnp.sum(dy * xn3, axis=0, keepdims=True)
        g3 = dy * fw_ref[...]
        dx3 = r3 * (g3 - xn3 * jnp.mean(g3 * xn3, axis=-1, keepdims=True))
        dx3_ref[...] = dx3.astype(MXU)
        dact = _dot_nt(dx3, wd_ref[...])
        dgate = (dact * up * (sg * (1.0 + gate * (1.0 - sg)))).astype(MXU)
        dup = (dact * sl).astype(MXU)
        dgate_ref[...] = dgate
        dup_ref[...] = dup
        dh2 = _dot(dgate, wg_ref[...]) + _dot(dup, wu_ref[...])
        dn2_ref[...] += jnp.sum(dh2 * xn2, axis=0, keepdims=True)
        g2 = dh2 * n2_ref[...]
        dx2 = dx3 + r2 * (g2 - xn2 * jnp.mean(g2 * xn2, axis=-1, keepdims=True))
        dx2_ref[...] = dx2
        dmg_ref[...] = _dot_nt(dx2, wo_ref[0:WIDTH, :])
        dmf_ref[...] = _dot_nt(dx2, wo_ref[WIDTH:2 * WIDTH, :])

    def tok(n):
        return pl.BlockSpec((tb, n), lambda i: (i, 0))

    acc = pl.BlockSpec((1, d), lambda i: (0, 0))
    sds = jax.ShapeDtypeStruct
    return pl.pallas_call(
        body, name="tail", grid=(t // tb,),
        in_specs=[tok(d), tok(WIDTH), tok(WIDTH), tok(d), _resident(wo.shape), _resident((1, d)),
                  _resident(wg_t.shape), _resident(wu_t.shape), _resident(wd.shape), _resident((1, d))],
        out_specs=[tok(d), tok(dff), tok(dff), tok(dff), tok(d), tok(d), tok(WIDTH), tok(WIDTH), acc, acc,
                   pl.BlockSpec((1, 1), lambda i: (0, 0))],
        out_shape=[sds((t, d), MXU), sds((t, dff), MXU), sds((t, dff), MXU), sds((t, dff), MXU), sds((t, d), MXU),
                   sds((t, d), F32), sds((t, WIDTH), F32), sds((t, WIDTH), F32), sds((1, d), F32), sds((1, d), F32),
                   sds((1, 1), F32)],
        compiler_params=_params(("arbitrary",)),
    )(x, mixg, mixf, tgt, wo, n2w, wg_t, wu_t, wd, fw)


def _wgrads(a_list, b, name):
    t, n = b.shape
    ms = [a.shape[1] for a in a_list]
    bms = [256 if m % 256 == 0 else LANE for m in ms]
    nbs = [m // bm for m, bm in zip(ms, bms)]
    k = len(a_list)
    cast = b.dtype != jnp.dtype(MXU)

    def body(*refs):
        a_refs, b_ref, o_refs = refs[:k], refs[k], refs[k + 1:2 * k + 1]
        i = pl.program_id(0)
        if cast:
            @pl.when(i == 0)
            def _():
                refs[-1][...] = b_ref[...].astype(MXU)
        for a_ref, o_ref, nb in zip(a_refs, o_refs, nbs):
            @pl.when(i < nb)
            def _():
                o_ref[...] = _dot_tn(a_ref[...], refs[-1][...] if cast else b_ref[...]).astype(o_ref.dtype)

    def clamp(nb):
        return lambda i: jnp.minimum(i, nb - 1)

    return pl.pallas_call(
        body, name=name, grid=(max(nbs),),
        in_specs=[pl.BlockSpec((t, bm), lambda i, c=clamp(nb): (0, c(i))) for bm, nb in zip(bms, nbs)] + [_resident((t, n))],
        out_specs=[pl.BlockSpec((bm, n), lambda i, c=clamp(nb): (c(i), 0)) for bm, nb in zip(bms, nbs)],
        out_shape=[jax.ShapeDtypeStruct((m, n), WIRE) for m in ms],
        scratch_shapes=[pltpu.VMEM((t, n), MXU)] if cast else [],
        compiler_params=_params(("arbitrary",)),
    )(*a_list, b)


def _merge_dw_in(d_gdn, d_z, d_fox, d_fg, d_small, shards=False):
    pieces = [d_gdn, d_z, d_small[:2 * HEADS], d_fox, d_fg, d_small[2 * HEADS:3 * HEADS]]
    if not shards:
        return jnp.concatenate(pieces, axis=0)
    n = sum(p.shape[0] for p in pieces) // N_DEV
    out = []
    for dev in range(N_DEV):
        parts, first = [], 0
        for p in pieces:
            lo, hi = max(dev * n, first), min((dev + 1) * n, first + p.shape[0])
            if lo < hi:
                parts.append(p[lo - first:hi - first])
            first += p.shape[0]
        out.append(jnp.concatenate(parts, axis=0))
    return jnp.stack(out)


def _lanes(*pieces):
    v = jnp.concatenate([p.reshape(-1).astype(F32) for p in pieces])
    return jnp.pad(v, (0, LANE - v.shape[0])).reshape(1, LANE)


def _vector_params(p):
    d = p["norm1_w"].size
    gparams = jnp.concatenate([_lanes(jnp.zeros(HEADS), p["gdn_dt_bias"], p["fox_f_bias"]),
                               _lanes(jnp.zeros(HEADS), p["gdn_A_log"]), jnp.zeros((6, LANE), F32)])
    fox_nw = jnp.stack([jnp.tile(p["fox_q_norm_w"].reshape(-1), 2), jnp.tile(p["fox_k_norm_w"].reshape(-1), 2),
                        jnp.ones((LANE,), F32)])
    return dict(n1w=p["norm1_w"].reshape(1, d), n2w=p["norm2_w"].reshape(1, d), fw=p["final_norm_w"].reshape(1, d),
                onw=p["gdn_out_norm_w"].reshape(1, DH), gparams=gparams, fox_nw=fox_nw)


def _mixer_forward(x, vp, w_t, ws_t, conv_w, carry=None, carry_first=None):
    h1, pm, ps, *first = _inproj(x, vp["n1w"], w_t, ws_t, carry_first)
    gates, run, tot, run_t = _gates(ps, vp["gparams"])
    fqkv = _fox_prep(pm, vp["fox_nw"])
    o_fox, lse, *carried = _fox_fwd(fqkv, run, run_t, carry)
    if carry:
        pm, o_fox = lax.optimization_barrier((pm, o_fox))
    mixf = _fox_post(o_fox, pm)
    gqkv = _gdn_prep(pm, conv_w)
    o_gdn, states, inv = _gdn_fwd(gqkv, gates, run, tot, run_t)
    mixg = _gdn_post(o_gdn, pm, vp["onw"])
    return dict(h1=h1, pm=pm, ps=ps, gates=gates, run=run, tot=tot, run_t=run_t, gqkv=gqkv, o_gdn=o_gdn, states=states,
                inv=inv, mixg=mixg, fqkv=fqkv, o_fox=o_fox, lse=lse, mixf=mixf, carried=carried[0] if carried else None,
                carried_first=first[0] if first else None)


def _mixer_backward(x, vp, w_t, ws_t, conv_w, f, dx2, dmixg, dmixf, carry=None, scatter_own=False):
    pm = f["pm"]
    do_fox, dfg = _fox_post_bwd(f["o_fox"], pm, dmixf)
    dfqkv, dcol_f, drow_f, *carried = _fox_bwd(f["fqkv"], f["run"], f["run_t"], f["o_fox"], f["lse"], do_fox, carry)
    dfox, dfnw = _fox_prep_bwd(pm, vp["fox_nw"], dfqkv)
    do_gdn, dz, donw = _gdn_post_bwd(f["o_gdn"], pm, vp["onw"], dmixg)
    dgqkv, dcol_g, dtot_g, drow_g = _gdn_bwd(f["gqkv"], f["gates"], f["run"], f["tot"], f["run_t"], f["inv"], f["states"], do_gdn)
    dgdn, dconv = _gdn_prep_bwd(pm, conv_w, dgqkv)
    dps, gsum = _gates_bwd(f["ps"], vp["gparams"], dcol_g, dtot_g, dcol_f, drow_g, drow_f)
    dw_pieces = _wgrads([dgdn, dz, dfox, dfg, dps], f["h1"], "dw_in")
    dw_in = _merge_dw_in(*dw_pieces)
    own = ("scatter", [_merge_dw_in(*dw_pieces, shards=True), _cut(dconv, 1)]) if scatter_own else None
    grad_x, dn1w, *sent = _inproj_bwd(x, vp["n1w"], dx2, dgdn, dz, dfox, dfg, dps, w_t, ws_t, own)
    small = dict(dn1w=dn1w, gsum=gsum, donw=donw, dfnw=dfnw)
    return (grad_x, dw_in, dconv, small, *carried, *sent)


VECTORS = ("norm1_w", "norm2_w", "final_norm_w", "gdn_A_log", "gdn_dt_bias", "gdn_out_norm_w", "fox_f_bias",
           "fox_q_norm_w", "fox_k_norm_w")
VEC_ROWS = 16
LOSS_ROW = len(VECTORS)


def _pack_vectors(dn1w, dn2w, dfw, gsum, donw, dfnw, loss):
    d = dn1w.shape[1]

    def body(n1_ref, n2_ref, fw_ref, gs_ref, on_ref, fn_ref, loss_ref, o_ref):
        o_ref[...] = jnp.zeros_like(o_ref)
        o_ref[0:1, :] = n1_ref[...]
        o_ref[1:2, :] = n2_ref[...]
        o_ref[2:3, :] = fw_ref[...]
        o_ref[3:4, 0:HEADS] = gs_ref[0:1, 0:HEADS]
        o_ref[4:5, 0:HEADS] = gs_ref[1:2, 0:HEADS]
        o_ref[5:6, 0:DH] = on_ref[...]
        o_ref[6:7, 0:HEADS] = gs_ref[2:3, 0:HEADS]
        for kind in range(2):
            v = fn_ref[kind, 0]
            for j in range(1, fn_ref.shape[1]):
                v = v + fn_ref[kind, j]
            o_ref[7 + kind:8 + kind, 0:DH] = v[:, :DH] + v[:, DH:]
        o_ref[LOSS_ROW:LOSS_ROW + 1, 0:1] = loss_ref[...]

    return pl.pallas_call(body, name="pack_vectors", out_shape=jax.ShapeDtypeStruct((VEC_ROWS, d), F32),
                          compiler_params=_params())(dn1w, dn2w, dfw, gsum, donw, dfnw, loss)


def _late_grads(f, h2, act, dgate, dup, dx3, dx2):
    dw_gate, dw_up = _wgrads([dgate, dup], h2, "dw_gate_up")
    (dw_down,) = _wgrads([act], dx3, "dw_down")
    return {"w_out": jnp.concatenate(_wgrads([f["mixg"], f["mixf"]], dx2, "dw_out"), axis=0),
            "w_ffn_gate": dw_gate, "w_ffn_up": dw_up, "w_ffn_down": dw_down}


def _local_step(x, tgt, p, w_in_t, conv_w, wo, wg_t, wu_t, wd):
    vp = _vector_params(p)
    ws_t = _small_rows(w_in_t)
    f = _mixer_forward(x, vp, w_in_t, ws_t, conv_w)
    (h2, act, dgate, dup, dx3, dx2, dmixg, dmixf, dn2w, dfw, loss) = _tail(
        x, f["mixg"], f["mixf"], tgt, wo, vp["n2w"], wg_t, wu_t, wd, vp["fw"])
    grad_x, dw_in, dconv, small = _mixer_backward(x, vp, w_in_t, ws_t, conv_w, f, dx2, dmixg, dmixf)
    grads = {"w_in": dw_in, "gdn_conv_w": dconv, **_late_grads(f, h2, act, dgate, dup, dx3, dx2)}
    vec = _pack_vectors(small["dn1w"], dn2w, dfw, small["gsum"], small["donw"], small["dfnw"], loss)
    return grad_x[0], grads, vec


def _my_place():
    return lax.axis_index("x"), lax.axis_index("y"), lax.axis_index("c")


def _peers():
    x, y, c = _my_place()
    peers = []
    for k in range(1, N_DEV):
        px = 1 - x if k & 4 else x
        py = 1 - y if k & 2 else y
        pc = 1 - c if k & 1 else c
        peers.append(((px, py, pc), 4 * px + 2 * py + pc))
    return 4 * x + 2 * y + c, peers


def _spread_copies(kind, srcs, lands, send_sems, recv_sems):
    me, peers = _peers()
    kinds = [kind] * len(srcs) if isinstance(kind, str) else kind
    remote, local = [], []
    for i, (kd, src, land) in enumerate(zip(kinds, srcs, lands)):
        for k, (dev, idx) in enumerate(peers):
            remote.append(pltpu.make_async_remote_copy(
                src_ref=src if kd == "gather" else src.at[idx], dst_ref=land.at[me],
                send_sem=send_sems.at[i * (N_DEV - 1) + k], recv_sem=recv_sems.at[i * (N_DEV - 1) + k],
                device_id=dev, device_id_type=MESH))
        local.append((src if kd == "gather" else src.at[me], land.at[me]))
    return remote, local


def _gather_two_level(arrays, name):
    n = len(arrays)

    def body(*refs):
        srcs, lands = refs[:n], refs[n:2 * n]
        send_sems, recv_sems, local_sems = refs[2 * n:]
        x, y, c = _my_place()
        me, sibling = (x, y, c), (x, y, 1 - c)
        chips = [(1 - x, y), (x, 1 - y), (1 - x, 1 - y)]

        def index(p):
            return 4 * p[0] + 2 * p[1] + p[2]

        def copy(i, k, block, to, src=None):
            blk = lands[i].at[index(block)]
            return pltpu.make_async_remote_copy(
                src_ref=blk if src is None else src, dst_ref=blk, send_sem=send_sems.at[7 * i + k],
                recv_sem=recv_sems.at[7 * i + k], device_id=to, device_id_type=MESH)

        mine = [pltpu.make_async_copy(srcs[i], lands[i].at[index(me)], local_sems.at[i]) for i in range(n)]
        for cp in mine:
            cp.start()
        first = []
        for i in range(n):
            first.append(copy(i, 0, me, sibling, src=srcs[i]))
            first += [copy(i, 1 + j, me, (*chip, c), src=srcs[i]) for j, chip in enumerate(chips)]
        for cp in first:
            cp.start()
        passed = []
        for i in range(n):
            for j, chip in enumerate(chips):
                copy(i, 1 + j, (*chip, c), me).wait_recv()
                passed.append(copy(i, 4 + j, (*chip, c), sibling))
                passed[-1].start()
        for i in range(n):
            copy(i, 0, sibling, me).wait_recv()
            for j, chip in enumerate(chips):
                copy(i, 4 + j, (*chip, 1 - c), me).wait_recv()
        for cp in first + passed:
            cp.wait_send()
        for cp in mine:
            cp.wait()

    return pl.pallas_call(
        body, name=name,
        out_shape=[jax.ShapeDtypeStruct((N_DEV,) + a.shape, a.dtype) for a in arrays],
        in_specs=[pl.BlockSpec(memory_space=pl.ANY)] * n, out_specs=[pl.BlockSpec(memory_space=pl.ANY)] * n,
        scratch_shapes=[pltpu.SemaphoreType.DMA((7 * n,)), pltpu.SemaphoreType.DMA((7 * n,)),
                        pltpu.SemaphoreType.DMA((n,))],
    )(*arrays)


def _land_shape(kind, a):
    return (N_DEV,) + a.shape if kind == "gather" else a.shape


HBM = pl.BlockSpec(memory_space=pltpu.HBM)
SEM = pl.BlockSpec(memory_space=pltpu.SEMAPHORE)


def _hbm(a):
    return pltpu.with_memory_space_constraint(a, pltpu.HBM)


def _carry_operands(kind, arrays):
    n = len(arrays)
    kinds = [kind] * n if isinstance(kind, str) else kind
    lands = [lax.empty(_land_shape(kd, a), a.dtype) for kd, a in zip(kinds, arrays)]
    sems = [pltpu.SemaphoreType.DMA((n * (N_DEV - 1),))] * 2
    return ([_hbm(a) for a in list(arrays) + lands], [HBM] * (2 * n),
            sems + [pltpu.HBM(a.shape, a.dtype) for a in list(arrays) + lands], [SEM] * 2 + [HBM] * (2 * n),
            pltpu.SemaphoreType.DMA((n,)))


def _carry_step(kind, in_refs, send_sems, recv_sems, local_sems, step, n_steps):
    n = len(in_refs) // 2
    remote, local = _spread_copies(kind, in_refs[:n], in_refs[n:], send_sems, recv_sems)
    copies = [pltpu.make_async_copy(s, d, local_sems.at[i]) for i, (s, d) in enumerate(local)]
    per_step = -(-len(remote) // n_steps)
    for s in range(-(-len(remote) // per_step)):
        @pl.when(step == s)
        def _():
            for cp in remote[s * per_step:(s + 1) * per_step]:
                cp.start()
            if s == 0:
                for cp in copies:
                    cp.start()

    @pl.when(step == n_steps - 1)
    def _():
        for cp in copies:
            cp.wait()


def _carry_state(extra_out):
    n = (len(extra_out) - 2) // 2
    return list(extra_out[2:2 + n]), list(extra_out[2 + n:]), extra_out[0], extra_out[1]


def _spread_start(arrays, kind, name):
    n = len(arrays)

    def body(*refs):
        srcs, lands = refs[:n], refs[n:2 * n]
        send_sems, recv_sems = refs[2 * n], refs[2 * n + 1]
        token = refs[4 * n + 2]
        local_sems = refs[4 * n + 3]
        remote, local = _spread_copies(kind, srcs, lands, send_sems, recv_sems)
        for cp in remote:
            cp.start()
        copies = [pltpu.make_async_copy(s, d, local_sems.at[i]) for i, (s, d) in enumerate(local)]
        for cp in copies:
            cp.start()
        for cp in copies:
            cp.wait()
        token[...] = jnp.zeros_like(token)

    sems = (pltpu.SemaphoreType.DMA((n * (N_DEV - 1),)),) * 2
    kinds = [kind] * n if isinstance(kind, str) else kind
    lands = [lax.empty(_land_shape(kd, a), a.dtype) for kd, a in zip(kinds, arrays)]
    out = pl.pallas_call(
        body, name=name,
        out_shape=sems + tuple(pltpu.HBM(a.shape, a.dtype) for a in list(arrays) + lands)
        + (jax.ShapeDtypeStruct((8, LANE), F32),),
        in_specs=[HBM] * (2 * n), out_specs=tuple([SEM] * 2 + [HBM] * (2 * n) + [pl.BlockSpec(memory_space=pltpu.VMEM)]),
        input_output_aliases={j: 2 + j for j in range(2 * n)},
        scratch_shapes=[pltpu.SemaphoreType.DMA((n,))],
        compiler_params=pltpu.CompilerParams(has_side_effects=pltpu.SideEffectType.DATAFLOW_SIDE_EFFECTING),
    )(*[_hbm(a) for a in arrays], *[_hbm(a) for a in lands])
    return (list(out[2:2 + n]), list(out[2 + n:2 + 2 * n]), out[0], out[1]), out[-1]


def _spread_wait(state, kind, after, name):
    srcs, lands, send_sems, recv_sems = state
    n = len(srcs)
    after = list(after) if isinstance(after, (list, tuple)) else [after]

    def body(*refs):
        remote, _ = _spread_copies(kind, refs[:n], refs[n:2 * n], refs[2 * n], refs[2 * n + 1])
        for cp in remote:
            cp.wait_send()
        for cp in remote:
            cp.wait_recv()

    out = pl.pallas_call(
        body, name=name,
        out_shape=tuple(pltpu.HBM(a.shape, a.dtype) for a in srcs + lands),
        in_specs=[HBM] * (2 * n) + [SEM, SEM] + [pl.BlockSpec(memory_space=pl.ANY)] * len(after),
        out_specs=tuple([HBM] * (2 * n)),
        input_output_aliases={j: j for j in range(2 * n)},
        compiler_params=pltpu.CompilerParams(has_side_effects=pltpu.SideEffectType.DATAFLOW_SIDE_EFFECTING),
    )(*srcs, *lands, send_sems, recv_sems, *after)
    return list(out[n:])


ADAM_ROWS = 128
ADAM_COLS = 256


def _adam_math(g, w, m, v):
    nm = ADAM_B1 * m + (1.0 - ADAM_B1) * g
    nv = ADAM_B2 * v + (1.0 - ADAM_B2) * (g * g)
    m_hat = nm / (1.0 - ADAM_B1 ** ADAM_STEP)
    v_hat = nv / (1.0 - ADAM_B2 ** ADAM_STEP)
    return -ADAM_LR * (m_hat / (jnp.sqrt(v_hat) + ADAM_EPS) + ADAM_WD * w), nm, nv


def _sum_parts(p_ref):
    g = p_ref[0].astype(F32)
    for s in range(1, N_DEV):
        g = g + p_ref[s].astype(F32)
    return g


def _adam_matrix(parts, w, m, v, name):
    _, r, c = w.shape
    rb = ADAM_ROWS if r % ADAM_ROWS == 0 else r
    cb = ADAM_COLS if (rb == r and c % ADAM_COLS == 0) else c

    def body(p_ref, w_ref, m_ref, v_ref, g_ref, d_ref, nm_ref, nv_ref):
        g = _sum_parts(p_ref)
        g_ref[0] = g
        d_ref[0], nm_ref[0], nv_ref[0] = _adam_math(g, w_ref[0], m_ref[0], v_ref[0])

    blk = pl.BlockSpec((1, rb, cb), lambda i, j: (0, i, j))
    return pl.pallas_call(
        body, name=name, grid=(r // rb, c // cb),
        in_specs=[pl.BlockSpec((N_DEV, rb, cb), lambda i, j: (0, i, j)), blk, blk, blk],
        out_specs=[blk] * 4, out_shape=[jax.ShapeDtypeStruct(w.shape, F32)] * 4,
        compiler_params=_params(("arbitrary", "arbitrary")),
    )(parts, w, m, v)


def _copy_rows(a, name):
    _, r, c = a.shape
    rb = min(TB, r)

    def body(a_ref, o_ref):
        o_ref[...] = a_ref[...]

    blk = pl.BlockSpec((1, rb, c), lambda i: (0, i, 0))
    return pl.pallas_call(body, name=name, grid=(r // rb,), in_specs=[blk], out_specs=blk,
                          out_shape=jax.ShapeDtypeStruct(a.shape, a.dtype), compiler_params=_params(("arbitrary",)))(a)


def _adam_vectors(parts, ws, ms, vs):
    nv = len(ws)

    def body(*refs):
        p_ref = refs[0]
        w_refs, m_refs, v_refs = refs[1:1 + nv], refs[1 + nv:1 + 2 * nv], refs[1 + 2 * nv:1 + 3 * nv]
        outs = refs[1 + 3 * nv:]
        g_all = _sum_parts(p_ref)
        for i in range(nv):
            n = w_refs[i].shape[1]
            g = g_all[i:i + 1, 0:n]
            d, nm, nvv = _adam_math(g, w_refs[i][...], m_refs[i][...], v_refs[i][...])
            outs[i][...] = g
            outs[nv + i][...] = d
            outs[2 * nv + i][...] = nm
            outs[3 * nv + i][...] = nvv
        outs[4 * nv][...] = g_all[LOSS_ROW:LOSS_ROW + 1, 0:1]

    shapes = [jax.ShapeDtypeStruct(a.shape, F32) for a in ws]
    out = pl.pallas_call(body, name="adam_vectors", out_shape=shapes * 4 + [jax.ShapeDtypeStruct((1, 1), F32)],
                         compiler_params=_params())(parts, *ws, *ms, *vs)
    return out[:nv], out[nv:2 * nv], out[2 * nv:3 * nv], out[3 * nv:4 * nv], out[4 * nv]


MATRICES = (("w_in", 1), ("gdn_conv_w", 1), ("w_out", 0), ("w_ffn_gate", 1), ("w_ffn_up", 1), ("w_ffn_down", 0))
TRANSPOSED = ("w_in", "w_ffn_gate", "w_ffn_up")
WEIGHTS = ("norm1_w", "w_in", "gdn_conv_w", "gdn_A_log", "gdn_dt_bias", "gdn_out_norm_w", "fox_f_bias", "fox_q_norm_w",
           "fox_k_norm_w", "w_out", "norm2_w", "w_ffn_gate", "w_ffn_up", "w_ffn_down", "final_norm_w")


def _join(blocks, axis):
    _, r, c = blocks.shape
    if axis == 0:
        return blocks.reshape(N_DEV * r, c)
    return blocks.transpose(1, 0, 2).reshape(r, N_DEV * c)


def _cut(full, axis):
    r, c = full.shape
    if axis == 0:
        return full.reshape(N_DEV, r // N_DEV, c)
    return full.reshape(r, N_DEV, c // N_DEV).transpose(1, 0, 2)


def kernel(x, norm1_w, w_in, gdn_conv_w, gdn_A_log, gdn_dt_bias, gdn_out_norm_w, fox_f_bias, fox_q_norm_w, fox_k_norm_w, w_out, norm2_w, w_ffn_gate, w_ffn_up, w_ffn_down, final_norm_w, loss_target, m_norm1_w, m_w_in, m_gdn_conv_w, m_gdn_A_log, m_gdn_dt_bias, m_gdn_out_norm_w, m_fox_f_bias, m_fox_q_norm_w, m_fox_k_norm_w, m_w_out, m_norm2_w, m_w_ffn_gate, m_w_ffn_up, m_w_ffn_down, m_final_norm_w, v_norm1_w, v_w_in, v_gdn_conv_w, v_gdn_A_log, v_gdn_dt_bias, v_gdn_out_norm_w, v_fox_f_bias, v_fox_q_norm_w, v_fox_k_norm_w, v_w_out, v_norm2_w, v_w_ffn_gate, v_w_ffn_up, v_w_ffn_down, v_final_norm_w):
    w = dict(norm1_w=norm1_w, w_in=w_in, gdn_conv_w=gdn_conv_w, gdn_A_log=gdn_A_log, gdn_dt_bias=gdn_dt_bias,
             gdn_out_norm_w=gdn_out_norm_w, fox_f_bias=fox_f_bias, fox_q_norm_w=fox_q_norm_w, fox_k_norm_w=fox_k_norm_w,
             w_out=w_out, norm2_w=norm2_w, w_ffn_gate=w_ffn_gate, w_ffn_up=w_ffn_up, w_ffn_down=w_ffn_down,
             final_norm_w=final_norm_w)
    m = dict(norm1_w=m_norm1_w, w_in=m_w_in, gdn_conv_w=m_gdn_conv_w, gdn_A_log=m_gdn_A_log, gdn_dt_bias=m_gdn_dt_bias,
             gdn_out_norm_w=m_gdn_out_norm_w, fox_f_bias=m_fox_f_bias, fox_q_norm_w=m_fox_q_norm_w,
             fox_k_norm_w=m_fox_k_norm_w, w_out=m_w_out, norm2_w=m_norm2_w, w_ffn_gate=m_w_ffn_gate,
             w_ffn_up=m_w_ffn_up, w_ffn_down=m_w_ffn_down, final_norm_w=m_final_norm_w)
    v = dict(norm1_w=v_norm1_w, w_in=v_w_in, gdn_conv_w=v_gdn_conv_w, gdn_A_log=v_gdn_A_log, gdn_dt_bias=v_gdn_dt_bias,
             gdn_out_norm_w=v_gdn_out_norm_w, fox_f_bias=v_fox_f_bias, fox_q_norm_w=v_fox_q_norm_w,
             fox_k_norm_w=v_fox_k_norm_w, w_out=v_w_out, norm2_w=v_norm2_w, w_ffn_gate=v_w_ffn_gate,
             w_ffn_up=v_w_ffn_up, w_ffn_down=v_w_ffn_down, final_norm_w=v_final_norm_w)
    late = ("w_out", "w_ffn_gate", "w_ffn_up", "w_ffn_down")
    xs, tgt = x[0], loss_target[0]
    vp = _vector_params({n: w[n] for n in VECTORS})

    def rows_of(d, n):
        return d[n].transpose(0, 2, 1) if n in TRANSPOSED else d[n]

    wr, mr, vr = ({n: rows_of(d, n) for n, _ in MATRICES} for d in (w, m, v))

    w_in_blocks, conv_blocks = _gather_two_level([wr["w_in"][0].astype(WIRE), w["gdn_conv_w"][0]], "gather_in")
    w_t = _join(w_in_blocks, 0)
    conv_w = _join(conv_blocks, 1)
    early = ("w_ffn_gate",)
    rest = tuple(n for n in late if n not in early)
    f = _mixer_forward(xs, vp, w_t, _small_rows(w_t), conv_w, ("gather", [wr[n][0].astype(WIRE) for n in rest]),
                       ("gather", [wr[n][0].astype(WIRE) for n in early]))
    full = {n: _join(b, 0) for n, b in zip(early, _spread_wait(f["carried_first"], "gather", f["mixg"], "gather_early_wait"))}
    full.update({n: _join(b, 0)
                 for n, b in zip(rest, _spread_wait(f["carried"], "gather", full[early[0]], "gather_late_wait"))})

    (h2, act, dgate, dup, dx3, dx2, dmixg, dmixf, dn2w, dfw, loss) = _tail(
        xs, f["mixg"], f["mixf"], tgt, full["w_out"], vp["n2w"], full["w_ffn_gate"], full["w_ffn_up"],
        full["w_ffn_down"], vp["fw"])
    dlate = _late_grads(f, h2, act, dgate, dup, dx3, dx2)

    grad_x, dw_in, dconv, small, state_grads, state_own = _mixer_backward(
        xs, vp, w_t, _small_rows(w_t), conv_w, f, dx2, dmixg, dmixf, ("scatter", [_cut(dlate[n], 0) for n in late]),
        scatter_own=True)
    vec = _pack_vectors(small["dn1w"], dn2w, dfw, small["gsum"], small["donw"], small["dfnw"], loss)

    state_vec, token = _spread_start([vec], "gather", "vectors_start")
    parts = dict(zip(late, _spread_wait(state_grads, "scatter", token, "grads_late_wait")))
    results = [{}, {}, {}, {}]

    updated = {}

    def update(n):
        out = _adam_matrix(parts[n], wr[n], mr[n], vr[n], "adam_" + n)
        updated[n] = out[0]
        for d, a in zip(results, out):
            d[n] = a.transpose(0, 2, 1) if n in TRANSPOSED else a

    for n in late:
        update(n)
    grad_x = _copy_rows(grad_x, "grad_x_result")
    ready = [updated[n] for n in late] + [d["w_in"] for d in (wr, mr, vr)] + [grad_x]
    (parts_vec,) = _spread_wait(state_vec, "gather", ready, "vectors_wait")
    row = lambda a: a.reshape(1, -1)
    *vec_out, total_loss = _adam_vectors(parts_vec, [row(w[n]) for n in VECTORS], [row(m[n]) for n in VECTORS],
                                         [row(v[n]) for n in VECTORS])
    parts["w_in"], parts["gdn_conv_w"] = _spread_wait(state_own, "scatter", parts_vec, "own_wait")
    update("w_in")
    update("gdn_conv_w")
    for d, arrs in zip(results, vec_out):
        for n, a in zip(VECTORS, arrs):
            d[n] = a.reshape(w[n].shape)
    return (total_loss[0, 0], grad_x, *[d[n] for d in results for n in WEIGHTS])
```

```python
import functools

import jax
import jax.numpy as jnp
from jax import lax
from jax.experimental import pallas as pl
from jax.experimental.pallas import tpu as pltpu

F32 = jnp.float32
MXU = jnp.bfloat16
WIRE = jnp.bfloat16
HI = lax.Precision.HIGHEST
EPS = 1e-6

N_DEV = 8
HEADS = 8
DH = 64
WIDTH = HEADS * DH
CHUNK = 64
LANE = 128
ROW_ALIGN = 16
TB = 256
QB = 256
VMEM_LIMIT = 60 * 1024 * 1024

ADAM_LR = 0.001
ADAM_B1 = 0.9
ADAM_B2 = 0.999
ADAM_EPS = 1e-08
ADAM_WD = 0.01
ADAM_STEP = 10

MESH = pl.DeviceIdType.MESH


def _params(sem=None):
    return pltpu.CompilerParams(dimension_semantics=sem, vmem_limit_bytes=VMEM_LIMIT)


def _resident(shape):
    n = len(shape)
    return pl.BlockSpec(shape, lambda *_: (0,) * n, pipeline_mode=pl.Buffered(1))


def _dot(a, b):
    return jnp.dot(a.astype(MXU), b.astype(MXU), preferred_element_type=F32)


def _dot_nt(a, b):
    return lax.dot_general(a.astype(MXU), b.astype(MXU), (((1,), (1,)), ((), ())), preferred_element_type=F32)


def _dot_tn(a, b):
    return lax.dot_general(a.astype(MXU), b.astype(MXU), (((0,), (0,)), ((), ())), preferred_element_type=F32)


def _hdot(a, b):
    return jnp.dot(a, b, precision=HI, preferred_element_type=F32)


def _hdot_nt(a, b):
    return lax.dot_general(a, b, (((1,), (1,)), ((), ())), precision=HI, preferred_element_type=F32)


def _hdot_tn(a, b):
    return lax.dot_general(a, b, (((0,), (0,)), ((), ())), precision=HI, preferred_element_type=F32)


def _sigmoid(x):
    return 0.5 * jnp.tanh(0.5 * x) + 0.5


def _softplus(x):
    return jnp.maximum(x, 0.0) + jnp.log(1.0 + jnp.exp(-jnp.abs(x)))


def _head_sum_matrix():
    ri = lax.broadcasted_iota(jnp.int32, (LANE, LANE), 0) // DH
    ci = lax.broadcasted_iota(jnp.int32, (LANE, LANE), 1) // DH
    return (ri == ci).astype(F32)


def _group_sum(a, ones_matrix):
    hi = a.astype(jnp.bfloat16)
    lo = (a - hi.astype(F32)).astype(jnp.bfloat16)
    m = ones_matrix.astype(jnp.bfloat16)
    return jnp.dot(hi, m, preferred_element_type=F32) + jnp.dot(lo, m, preferred_element_type=F32)


def _shift_down(x, s):
    return pltpu.roll(x, s, 0)


def _shift_up(x, s):
    return pltpu.roll(x, x.shape[0] - s, 0)


ROWS_A = 4 * WIDTH
ROWS_B = ROWS_A + 2 * HEADS
ROWS_C = ROWS_B + 4 * WIDTH


def _small_rows(w_t):
    return jnp.concatenate([w_t[ROWS_A:ROWS_B], w_t[ROWS_C:], jnp.zeros((LANE - 3 * HEADS, w_t.shape[1]), w_t.dtype)])


def _inproj(x, n1w, w_t, ws_t, carry=None):
    t, d = x.shape
    tb = min(TB, t)
    c_in, c_in_specs, c_out_shape, c_out_specs, c_sem = _carry_operands(*carry) if carry else ([], [], [], [], None)
    n_c = len(c_in)

    def body(*refs):
        x_ref, nw_ref, wt_ref, ws_ref = refs[:4]
        h_ref, pm_ref, ps_ref = refs[4 + n_c:7 + n_c]
        if carry:
            _carry_step(carry[0], refs[4:4 + n_c], refs[7 + n_c], refs[8 + n_c], refs[-1], pl.program_id(0), t // tb)
        xv = x_ref[...]
        r = lax.rsqrt(jnp.mean(xv * xv, axis=-1, keepdims=True) + EPS)
        h = (xv * r * nw_ref[...]).astype(MXU)
        h_ref[...] = h
        pm_ref[:, 0:ROWS_A] = _dot_nt(h, wt_ref[0:ROWS_A, :])
        pm_ref[:, ROWS_A:2 * ROWS_A] = _dot_nt(h, wt_ref[ROWS_B:ROWS_C, :])
        ps_ref[...] = _dot_nt(h, ws_ref[...])

    out = pl.pallas_call(
        body, name="inproj", grid=(t // tb,),
        in_specs=[pl.BlockSpec((tb, d), lambda i: (i, 0)), _resident((1, d)), _resident(w_t.shape), _resident((LANE, d))]
        + c_in_specs,
        out_specs=[pl.BlockSpec((tb, d), lambda i: (i, 0)), pl.BlockSpec((tb, 2 * ROWS_A), lambda i: (i, 0)),
                   pl.BlockSpec((tb, LANE), lambda i: (i, 0))] + c_out_specs,
        out_shape=[jax.ShapeDtypeStruct((t, d), MXU), jax.ShapeDtypeStruct((t, 2 * ROWS_A), F32),
                   jax.ShapeDtypeStruct((t, LANE), F32)] + c_out_shape,
        input_output_aliases={4 + j: 5 + j for j in range(n_c)},
        scratch_shapes=[c_sem] if carry else [],
        compiler_params=_params(("arbitrary",)),
    )(x, n1w, w_t, ws_t, *c_in)
    return (out[0], out[1], out[2], _carry_state(out[3:])) if carry else tuple(out)


def _inproj_bwd(x, n1w, dx2, dgdn, dz, dfox, dfg, dps, w_t, ws_t, carry=None):
    t, d = x.shape
    tb = min(TB, t)
    w3 = 3 * WIDTH
    c_in, c_in_specs, c_out_shape, c_out_specs, c_sem = _carry_operands(*carry) if carry else ([], [], [], [], None)
    n_c = len(c_in)

    def body(*refs):
        x_ref, nw_ref, dx2_ref, dgdn_ref, dz_ref, dfox_ref, dfg_ref, dps_ref, wm_ref, ws_ref = refs[:10]
        gx_ref, dnw_ref = refs[10 + n_c:12 + n_c]
        if carry:
            _carry_step(carry[0], refs[10:10 + n_c], refs[12 + n_c], refs[13 + n_c], refs[-1], pl.program_id(0), t // tb)
        dh = _dot(dgdn_ref[...], wm_ref[0:w3, :])
        dh += _dot(dz_ref[...], wm_ref[w3:ROWS_A, :])
        dh += _dot(dfox_ref[...], wm_ref[ROWS_B:ROWS_B + w3, :])
        dh += _dot(dfg_ref[...], wm_ref[ROWS_B + w3:ROWS_C, :])
        dh += _dot(dps_ref[...], ws_ref[...])
        xv = x_ref[...]
        r = lax.rsqrt(jnp.mean(xv * xv, axis=-1, keepdims=True) + EPS)
        xn = xv * r

        @pl.when(pl.program_id(0) == 0)
        def _():
            dnw_ref[...] = jnp.zeros_like(dnw_ref)

        dnw_ref[...] += jnp.sum(dh * xn, axis=0, keepdims=True)
        g = dh * nw_ref[...]
        gx_ref[0] = dx2_ref[...] + r * (g - xn * jnp.mean(g * xn, axis=-1, keepdims=True))

    def tok(n):
        return pl.BlockSpec((tb, n), lambda i: (i, 0))

    out = pl.pallas_call(
        body, name="inproj_bwd", grid=(t // tb,),
        in_specs=[tok(d), _resident((1, d)), tok(d), tok(w3), tok(WIDTH), tok(w3), tok(WIDTH), tok(LANE),
                  _resident(w_t.shape), _resident(ws_t.shape)] + c_in_specs,
        out_specs=[pl.BlockSpec((1, tb, d), lambda i: (0, i, 0)), pl.BlockSpec((1, d), lambda i: (0, 0))] + c_out_specs,
        out_shape=[jax.ShapeDtypeStruct((1, t, d), F32), jax.ShapeDtypeStruct((1, d), F32)] + c_out_shape,
        input_output_aliases={10 + j: 4 + j for j in range(n_c)},
        scratch_shapes=[c_sem] if carry else [],
        compiler_params=_params(("arbitrary",)),
    )(x, n1w, dx2, dgdn, dz, dfox, dfg, dps, w_t, ws_t, *c_in)
    return (out[0], out[1], _carry_state(out[2:])) if carry else tuple(out)


def _gate_lanes(shape):
    lane = lax.broadcasted_iota(jnp.int32, shape, 1)
    return lane < HEADS, (lane >= HEADS) & (lane < 2 * HEADS), (lane >= 2 * HEADS) & (lane < 3 * HEADS)


def _block_masks():
    ri = lax.broadcasted_iota(jnp.int32, (LANE, LANE), 0)
    ci = lax.broadcasted_iota(jnp.int32, (LANE, LANE), 1)
    same = (ri // CHUNK) == (ci // CHUNK)
    return ((ri >= ci).astype(F32), (ri <= ci).astype(F32), (same & (ri >= ci)).astype(F32),
            (same & (ri <= ci)).astype(F32), same.astype(F32))


def _gates(ps, gparams):
    t = ps.shape[0]
    nb = t // LANE

    def body(ps_ref, gp_ref, out_ref, run_ref, tot_ref, runt_ref):
        p = ps_ref[...]
        is_b, is_a, is_f = _gate_lanes(p.shape)
        z = p + gp_ref[0:1, :]
        neg_exp_a = -jnp.exp(gp_ref[1:2, :])
        glog = neg_exp_a * _softplus(z)
        logf = -_softplus(-z)
        out_ref[...] = jnp.where(is_b, _sigmoid(p), jnp.where(is_a, glog, jnp.where(is_f, logf, 0.0)))
        tril, _, tril_c, _, same_c = _block_masks()
        off = jnp.zeros((1, LANE), F32)
        for b in range(nb):
            rows = slice(b * LANE, (b + 1) * LANE)
            blk = out_ref[rows, :]
            ga = jnp.where(is_a[:LANE], blk, 0.0)
            fb = _hdot(tril, jnp.where(is_f[:LANE], blk, 0.0)) + off
            run = fb + _hdot(tril_c, ga)
            run_ref[rows, :] = run
            runt_ref[:, rows] = run.T
            tot_ref[rows, :] = _hdot(same_c, ga)
            off = fb[LANE - 1:LANE, :]

    return pl.pallas_call(
        body, name="gates",
        out_shape=[jax.ShapeDtypeStruct((t, LANE), F32)] * 3 + [jax.ShapeDtypeStruct((LANE, t), F32)],
        compiler_params=_params(),
    )(ps, gparams)


def _gates_bwd(ps, gparams, dcol_g, dtot_g, dcol_f, drow_g, drow_f):
    t = ps.shape[0]
    nb = t // LANE

    def body(ps_ref, gp_ref, dcg_ref, dtg_ref, dcf_ref, drg_ref, drf_ref, dps_ref, sums_ref, dl_ref, d0_ref, tr_ref):
        p = ps_ref[...]
        is_b, is_a, is_f = _gate_lanes(p.shape)
        _, triu, _, triu_c, same_c = _block_masks()
        tr_ref[...] = jnp.zeros_like(tr_ref)
        off = jnp.zeros((1, LANE), F32)
        for b in reversed(range(nb)):
            rows = slice(b * LANE, (b + 1) * LANE)
            tr_ref[HEADS:2 * HEADS, :] = drg_ref[:, rows]
            tr_ref[2 * HEADS:3 * HEADS, :] = drf_ref[:, rows]
            d = dcg_ref[rows, :] + dcf_ref[rows, :] + tr_ref[...].T
            d0_ref[rows, :] = d
            dlf = _hdot(triu, jnp.where(is_f[:LANE], d, 0.0)) + off
            dla = (_hdot(triu_c, jnp.where(is_a[:LANE], d, 0.0))
                   + _hdot(same_c, jnp.where(is_a[:LANE], dtg_ref[rows, :], 0.0)))
            dl_ref[rows, :] = dlf + dla
            off = dlf[0:1, :]
        z = p + gp_ref[0:1, :]
        neg_exp_a = -jnp.exp(gp_ref[1:2, :])
        sb = _sigmoid(p)
        glog = neg_exp_a * _softplus(z)
        dl = dl_ref[...]
        dp = jnp.where(is_b, d0_ref[...] * sb * (1.0 - sb),
                       jnp.where(is_a, dl * neg_exp_a * _sigmoid(z), jnp.where(is_f, dl * _sigmoid(-z), 0.0)))
        dps_ref[...] = dp
        s_a = jnp.sum(jnp.where(is_a, dl * glog, 0.0), axis=0, keepdims=True)
        s_p = jnp.sum(jnp.where(is_b, 0.0, dp), axis=0, keepdims=True)
        row = lax.broadcasted_iota(jnp.int32, (8, LANE), 0)
        from_a = pltpu.roll(jnp.where(row == 0, s_a, jnp.where(row == 1, s_p, 0.0)), LANE - HEADS, 1)
        from_f = pltpu.roll(jnp.where(row == 2, s_p, 0.0), LANE - 2 * HEADS, 1)
        lane = lax.broadcasted_iota(jnp.int32, (8, LANE), 1)
        sums_ref[...] = jnp.where(lane < HEADS, from_a + from_f, 0.0)

    return pl.pallas_call(
        body, name="gates_bwd",
        out_shape=[jax.ShapeDtypeStruct((t, LANE), F32), jax.ShapeDtypeStruct((8, LANE), F32)],
        scratch_shapes=[pltpu.VMEM((t, LANE), F32), pltpu.VMEM((t, LANE), F32), pltpu.VMEM((LANE, LANE), F32)],
        compiler_params=_params(),
    )(ps, gparams, dcol_g, dtot_g, dcol_f, drow_g, drow_f)


def _conv(xv, w):
    acc = w[3:4, :] * xv
    for s in range(1, 4):
        acc += w[3 - s:4 - s, :] * _shift_down(xv, s)
    return acc


PREP_ROWS = 256
HALO = 8
PREP_UNROLL = 2


def _tile_loop(n, tile, init):
    if n % PREP_UNROLL:
        return lax.fori_loop(0, n, tile, init)

    def trip(g, carry):
        for u in range(PREP_UNROLL):
            carry = tile(g * PREP_UNROLL + u, carry)
        return carry

    return lax.fori_loop(0, n // PREP_UNROLL, trip, init)


def _tile_rows(r):
    return pl.ds(pl.multiple_of(r * PREP_ROWS, PREP_ROWS), PREP_ROWS)


def _gdn_prep(pm, conv_w):
    t = pm.shape[0]
    nj = WIDTH // LANE
    win = PREP_ROWS + HALO

    def body(x_ref, w_ref, o_ref, xp_ref):
        kind = pl.program_id(0)
        xp_ref[0:HALO, :] = jnp.zeros((HALO, LANE), F32)
        xp_ref[HALO:, :] = x_ref[...]
        w = w_ref[...]
        hs = _head_sum_matrix()

        def tile(r, _, normed):
            xw = xp_ref[pl.ds(pl.multiple_of(r * PREP_ROWS, PREP_ROWS), win), :]
            acc = _conv(xw, w)[HALO:]
            out = acc * _sigmoid(acc)
            if normed:
                out = out * lax.rsqrt(_group_sum(out * out, hs) + EPS)
            o_ref[0, 0, _tile_rows(r), :] = out[:, :DH]
            o_ref[0, 1, _tile_rows(r), :] = out[:, DH:]
            return 0

        @pl.when(kind < 2)
        def _():
            _tile_loop(t // PREP_ROWS, functools.partial(tile, normed=True), 0)

        @pl.when(kind == 2)
        def _():
            _tile_loop(t // PREP_ROWS, functools.partial(tile, normed=False), 0)

    return pl.pallas_call(
        body, name="gdn_prep", grid=(3, nj),
        in_specs=[pl.BlockSpec((t, LANE), lambda i, j: (0, i * nj + j)),
                  pl.BlockSpec((4, LANE), lambda i, j: (0, i * nj + j))],
        out_specs=pl.BlockSpec((1, 2, t, DH), lambda i, j: (i, j, 0, 0)),
        out_shape=jax.ShapeDtypeStruct((3, HEADS, t, DH), F32),
        scratch_shapes=[pltpu.VMEM((t + HALO, LANE), F32)],
        compiler_params=_params(("arbitrary", "arbitrary")),
    )(pm, conv_w)


def _gdn_prep_bwd(pm, conv_w, dqkv):
    t = pm.shape[0]
    nj = WIDTH // LANE
    win = PREP_ROWS + 2 * HALO

    def body(x_ref, w_ref, d_ref, dx_ref, dw_ref, xp_ref, dp_ref):
        kind = pl.program_id(0)
        zeros = jnp.zeros((HALO, LANE), F32)
        for ref in (xp_ref, dp_ref):
            ref[0:HALO, :] = zeros
            ref[HALO + t:, :] = zeros
        xp_ref[HALO:HALO + t, :] = x_ref[...]
        dp_ref[HALO:HALO + t, 0:DH] = d_ref[0, 0]
        dp_ref[HALO:HALO + t, DH:] = d_ref[0, 1]
        w = w_ref[...]
        hs = _head_sum_matrix()
        rows = lax.broadcasted_iota(jnp.int32, (win, LANE), 0)
        in_tile = (rows >= HALO) & (rows < HALO + PREP_ROWS)

        def tile(r, dw, normed):
            start = pl.multiple_of(r * PREP_ROWS, PREP_ROWS)
            xw = xp_ref[pl.ds(start, win), :]
            dy = dp_ref[pl.ds(start, win), :]
            acc = _conv(xw, w)
            sg = _sigmoid(acc)
            if normed:
                y = acc * sg
                rn = lax.rsqrt(_group_sum(y * y, hs) + EPS)
                yn = y * rn
                dy = rn * (dy - yn * _group_sum(dy * yn, hs))
            dacc = dy * sg * (1.0 + acc * (1.0 - sg))
            dx = w[3:4, :] * dacc
            for s in range(1, 4):
                dx += w[3 - s:4 - s, :] * _shift_up(dacc, s)
            dx_ref[_tile_rows(r), :] = dx[HALO:HALO + PREP_ROWS].astype(dx_ref.dtype)
            dm = jnp.where(in_tile, dacc, 0.0)
            return tuple(dw[i] + jnp.sum(dm * (xw if i == 3 else _shift_down(xw, 3 - i)), axis=0, keepdims=True)
                         for i in range(4))

        def run(normed):
            dw = _tile_loop(t // PREP_ROWS, functools.partial(tile, normed=normed),
                            tuple(jnp.zeros((1, LANE), F32) for _ in range(4)))
            for i in range(4):
                dw_ref[i:i + 1, :] = dw[i]

        pl.when(kind < 2)(functools.partial(run, True))
        pl.when(kind == 2)(functools.partial(run, False))

    return pl.pallas_call(
        body, name="gdn_prep_bwd", grid=(3, nj),
        in_specs=[pl.BlockSpec((t, LANE), lambda i, j: (0, i * nj + j)),
                  pl.BlockSpec((4, LANE), lambda i, j: (0, i * nj + j)),
                  pl.BlockSpec((1, 2, t, DH), lambda i, j: (i, j, 0, 0))],
        out_specs=[pl.BlockSpec((t, LANE), lambda i, j: (0, i * nj + j)),
                   pl.BlockSpec((4, LANE), lambda i, j: (0, i * nj + j))],
        out_shape=[jax.ShapeDtypeStruct((t, 3 * WIDTH), MXU), jax.ShapeDtypeStruct((4, 3 * WIDTH), F32)],
        scratch_shapes=[pltpu.VMEM((t + 2 * HALO, LANE), F32), pltpu.VMEM((t + 2 * HALO, LANE), F32)],
        compiler_params=_params(("arbitrary", "arbitrary")),
    )(pm, conv_w, dqkv)


FOX_COL0 = 4 * WIDTH // LANE


def _fox_prep(pm, nw):
    t = pm.shape[0]
    nj = WIDTH // LANE

    def body(x_ref, w_ref, o_ref):
        kind = pl.program_id(0)
        hs = _head_sum_matrix()
        wk = w_ref[pl.ds(kind, 1), :]

        def tile(r, _, normed):
            out = x_ref[_tile_rows(r), :]
            if normed:
                out = out * lax.rsqrt(_group_sum(out * out, hs) * (1.0 / DH) + EPS) * wk
            o_ref[0, 0, _tile_rows(r), :] = out[:, :DH]
            o_ref[0, 1, _tile_rows(r), :] = out[:, DH:]
            return 0

        @pl.when(kind < 2)
        def _():
            _tile_loop(t // PREP_ROWS, functools.partial(tile, normed=True), 0)

        @pl.when(kind == 2)
        def _():
            _tile_loop(t // PREP_ROWS, functools.partial(tile, normed=False), 0)

    return pl.pallas_call(
        body, name="fox_prep", grid=(3, nj),
        in_specs=[pl.BlockSpec((t, LANE), lambda i, j: (0, FOX_COL0 + i * nj + j)),
                  pl.BlockSpec((3, LANE), lambda i, j: (0, 0))],
        out_specs=pl.BlockSpec((1, 2, t, DH), lambda i, j: (i, j, 0, 0)),
        out_shape=jax.ShapeDtypeStruct((3, HEADS, t, DH), F32),
        compiler_params=_params(("arbitrary", "arbitrary")),
    )(pm, nw)


def _fox_prep_bwd(pm, nw, dqkv):
    t = pm.shape[0]
    nj = WIDTH // LANE

    def body(x_ref, w_ref, d_ref, dx_ref, dw_ref):
        kind = pl.program_id(0)
        hs = _head_sum_matrix()
        wk = w_ref[pl.ds(kind, 1), :]

        def tile(r, dw):
            xv = x_ref[_tile_rows(r), :]
            rn = lax.rsqrt(_group_sum(xv * xv, hs) * (1.0 / DH) + EPS)
            xn = xv * rn
            d = jnp.concatenate([d_ref[0, 0, _tile_rows(r), :], d_ref[0, 1, _tile_rows(r), :]], axis=1)
            g = d * wk
            dx_ref[_tile_rows(r), :] = (rn * (g - xn * _group_sum(g * xn, hs) * (1.0 / DH))).astype(dx_ref.dtype)
            return dw + jnp.sum(d * xn, axis=0, keepdims=True)

        def copy_tile(r, _):
            dx_ref[_tile_rows(r), :] = jnp.concatenate([d_ref[0, 0, _tile_rows(r), :], d_ref[0, 1, _tile_rows(r), :]],
                                                       axis=1).astype(dx_ref.dtype)
            return 0

        @pl.when(kind < 2)
        def _():
            dw_ref[0, 0] = _tile_loop(t // PREP_ROWS, tile, jnp.zeros((1, LANE), F32))

        @pl.when(kind == 2)
        def _():
            _tile_loop(t // PREP_ROWS, copy_tile, 0)
            dw_ref[0, 0] = jnp.zeros((1, LANE), F32)

    return pl.pallas_call(
        body, name="fox_prep_bwd", grid=(3, nj),
        in_specs=[pl.BlockSpec((t, LANE), lambda i, j: (0, FOX_COL0 + i * nj + j)),
                  pl.BlockSpec((3, LANE), lambda i, j: (0, 0)),
                  pl.BlockSpec((1, 2, t, DH), lambda i, j: (i, j, 0, 0))],
        out_specs=[pl.BlockSpec((t, LANE), lambda i, j: (0, i * nj + j)),
                   pl.BlockSpec((1, 1, 1, LANE), lambda i, j: (i, j, 0, 0))],
        out_shape=[jax.ShapeDtypeStruct((t, 3 * WIDTH), MXU), jax.ShapeDtypeStruct((3, nj, 1, LANE), F32)],
        compiler_params=_params(("arbitrary", "arbitrary")),
    )(pm, nw, dqkv)


SC = 256
CPS = SC // CHUNK
GDN_HP_FWD = 4
GDN_HP_BWD = 2
Q_SCALE = DH ** -0.5


def _sc_masks():
    ri = lax.broadcasted_iota(jnp.int32, (SC, SC), 0)
    ci = lax.broadcasted_iota(jnp.int32, (SC, SC), 1)
    same = (ri // CHUNK) == (ci // CHUNK)
    return same & (ri >= ci), same & (ri > ci), ri == ci


def _unit_lower_inverses(ms, eye):
    invs = [jnp.where(eye, 1.0, 0.0) + m for m in ms]
    ms = [_dot(m, m) for m in ms]
    for _ in range(4):
        both = [_dot(jnp.concatenate([inv, m], axis=0), m) for inv, m in zip(invs, ms)]
        invs = [inv + b[:SC] for inv, b in zip(invs, both)]
        ms = [b[SC:] for b in both]
    return [inv + _dot(inv, m) for inv, m in zip(invs, ms)]


def _lane_col(blk, lane_idx):
    lane = lax.broadcasted_iota(jnp.int32, blk.shape, 1)
    return jnp.sum(jnp.where(lane == lane_idx, blk, 0.0), axis=1, keepdims=True)


def _to_lane(col, lane_idx):
    lane = lax.broadcasted_iota(jnp.int32, (col.shape[0], LANE), 1)
    return jnp.where(lane == lane_idx, col, 0.0)


def _gdn_columns(gates_ref, run_ref, tot_ref, runt_ref, rows, h):
    return (_lane_col(gates_ref[rows, :], h), _lane_col(run_ref[rows, :], HEADS + h),
            _lane_col(tot_ref[rows, :], HEADS + h), runt_ref[pl.ds(h, 1), rows])


def _gdn_local(q, k, beta, gc, gl, grow, causal, with_kk=True):
    decay = jnp.exp(jnp.where(causal, gc - grow, -1e30))
    egc = jnp.exp(gc)
    ekd = jnp.exp(gl - gc)
    qs = q * Q_SCALE
    kb = k * beta
    if with_kk:
        both = _dot_nt(jnp.concatenate([kb, qs], axis=0), k)
        kk, qk = both[:SC], both[SC:]
    else:
        kk, qk = None, _dot_nt(qs, k)
    return beta, gl, decay, egc, ekd, qs, kb, kk, qk, jnp.where(causal, qk * decay, 0.0)


def _chunk_rows(c):
    return pl.ds(c * CHUNK if isinstance(c, int) else pl.multiple_of(c * CHUNK, CHUNK), CHUNK)


def _sc_rows(b):
    return pl.ds(b * SC if isinstance(b, int) else pl.multiple_of(b * SC, SC), SC)


def _gdn_fwd(qkv, gates, run, tot, run_t):
    t = qkv.shape[2]
    nc = t // CHUNK
    nsc = t // SC

    hp_n = GDN_HP_FWD
    heads = range(hp_n)

    def body(qkv_ref, gates_ref, run_ref, tot_ref, runt_ref, o_ref, st_ref, inv_ref, kc_s, qc_s, g_s, au_s):
        hp = pl.program_id(0)
        causal, strict, eye = _sc_masks()

        def local(b, _):
            rows = _sc_rows(b)
            loc = [_gdn_local(qkv_ref[0, hh, rows, :], qkv_ref[1, hh, rows, :],
                              *_gdn_columns(gates_ref, run_ref, tot_ref, runt_ref, rows, hp * hp_n + hh), causal)
                   for hh in heads]
            invs = _unit_lower_inverses([-jnp.where(strict, l[7] * l[2], 0.0) for l in loc], eye)
            uws = []
            for hh in heads:
                beta, _, _, egc, _, _, kb, _, _, _ = loc[hh]
                inv_ref[hh, rows, :] = invs[hh].astype(inv_ref.dtype)
                uws.append(_dot(invs[hh], jnp.concatenate([qkv_ref[2, hh, rows, :] * beta, kb * egc], axis=1)))
            for hh in heads:
                _, _, _, egc, ekd, qs, _, _, _, attn = loc[hh]
                auw = _dot(attn, uws[hh])
                g_s[hh, rows, :] = qs * egc - auw[:, DH:]
                au_s[hh, rows, :] = auw[:, :DH]
                kd = qkv_ref[1, hh, rows, :] * ekd
                for j in range(CPS):
                    sl = slice(j * CHUNK, (j + 1) * CHUNK)
                    both = _dot_tn(kd[sl], uws[hh][sl])
                    kc_s[hh, b * CPS + j] = both[:, DH:]
                    qc_s[hh, b * CPS + j] = both[:, :DH]
            return 0

        def step(c, states):
            rows = _chunk_rows(c)
            tot_row = tot_ref[pl.ds(c * CHUNK, 1), :]
            new = []
            for hh in heads:
                s = states[hh]
                st_ref[hh, c] = s
                o_ref[hh, rows, :] = _dot(g_s[hh, rows, :], s) + au_s[hh, rows, :]
                egl = jnp.exp(_lane_col(tot_row, HEADS + hp * hp_n + hh))
                new.append(egl * s - _dot(kc_s[hh, c], s) + qc_s[hh, c])
            return tuple(new)

        def steps_of(b, states):
            for j in range(CPS):
                states = step(b * CPS + j, states)
            return states

        def fused(b, states):
            states = steps_of(b - 1, states)
            local(b, 0)
            return states

        local(0, 0)
        states = lax.fori_loop(1, nsc, fused, tuple(jnp.zeros((DH, DH), F32) for _ in heads))
        steps_of(nsc - 1, states)

    whole = pl.BlockSpec((t, LANE), lambda h: (0, 0))
    once = dict(pipeline_mode=pl.Buffered(1))
    return pl.pallas_call(
        body, name="gdn_fwd", grid=(HEADS // hp_n,),
        in_specs=[pl.BlockSpec((3, hp_n, t, DH), lambda h: (0, h, 0, 0), **once), whole, whole, whole,
                  pl.BlockSpec((HEADS, t), lambda h: (1, 0))],
        out_specs=[pl.BlockSpec((hp_n, t, DH), lambda h: (h, 0, 0), **once),
                   pl.BlockSpec((hp_n, nc, DH, DH), lambda h: (h, 0, 0, 0), **once),
                   pl.BlockSpec((hp_n, t, SC), lambda h: (h, 0, 0), **once)],
        out_shape=[jax.ShapeDtypeStruct((HEADS, t, DH), F32), jax.ShapeDtypeStruct((HEADS, nc, DH, DH), F32),
                   jax.ShapeDtypeStruct((HEADS, t, SC), MXU)],
        scratch_shapes=[pltpu.VMEM((hp_n, nc, DH, DH), F32), pltpu.VMEM((hp_n, nc, DH, DH), F32),
                        pltpu.VMEM((hp_n, t, DH), F32), pltpu.VMEM((hp_n, t, DH), F32)],
        compiler_params=_params(("arbitrary",)),
    )(qkv, gates, run, tot, run_t)


def _gdn_bwd(qkv, gates, run, tot, run_t, inv, states, do):
    t = qkv.shape[2]
    nc = t // CHUNK
    nsc = t // SC

    hp_n = GDN_HP_BWD
    heads = range(hp_n)

    def body(qkv_ref, gates_ref, run_ref, tot_ref, runt_ref, inv_ref, st_ref, do_ref,
             dqkv_ref, dcol_ref, dtot_ref, drow_ref, uw_s, kc_s, h_s, dsn_s):
        hp = pl.program_id(0)
        causal, strict, _ = _sc_masks()

        @pl.when(hp == 0)
        def _():
            dcol_ref[...] = jnp.zeros_like(dcol_ref)
            dtot_ref[...] = jnp.zeros_like(dtot_ref)

        def local_of(hh, rows, with_kk=True):
            return _gdn_local(qkv_ref[0, hh, rows, :], qkv_ref[1, hh, rows, :],
                              *_gdn_columns(gates_ref, run_ref, tot_ref, runt_ref, rows, hp * hp_n + hh), causal, with_kk)

        def local(b, _):
            rows = _sc_rows(b)
            loc = [local_of(hh, rows, with_kk=False) for hh in heads]
            us, ws = [], []
            for hh in heads:
                beta, _, _, egc, _, _, kb, _, _, _ = loc[hh]
                inv_b = inv_ref[hh, rows, :]
                us.append(_dot(inv_b, qkv_ref[2, hh, rows, :] * beta))
                ws.append(_dot(inv_b, kb * egc))
            gs = [loc[hh][5] * loc[hh][3] - _dot(loc[hh][9], ws[hh]) for hh in heads]
            for hh in heads:
                uw_s[hh, rows, :] = jnp.concatenate([us[hh], ws[hh]], axis=1)
                kd = qkv_ref[1, hh, rows, :] * loc[hh][4]
                dout = do_ref[hh, rows, :]
                for j in range(CPS):
                    sl = slice(j * CHUNK, (j + 1) * CHUNK)
                    kc_s[hh, b * CPS + j] = _dot_tn(kd[sl], ws[hh][sl])
                    h_s[hh, b * CPS + j] = _dot_tn(gs[hh][sl], dout[sl])
            return 0

        def step(c, dss):
            tot_row = tot_ref[pl.ds(c * CHUNK, 1), :]
            new = []
            for hh in heads:
                ds = dss[hh]
                dsn_s[hh, c] = ds
                egl = jnp.exp(_lane_col(tot_row, HEADS + hp * hp_n + hh))
                new.append(egl * ds - _dot_tn(kc_s[hh, c], ds) + h_s[hh, c])
            return tuple(new)

        def steps_of(b, dss):
            for j in reversed(range(CPS)):
                dss = step(b * CPS + j, dss)
            return dss

        def fused(i, dss):
            b = nsc - 2 - i
            dss = steps_of(b + 1, dss)
            local(b, 0)
            return dss

        local(nsc - 1, 0)
        dss = lax.fori_loop(0, nsc - 1, fused, tuple(jnp.zeros((DH, DH), F32) for _ in heads))
        steps_of(0, dss)

        def back(b, _):
            rows = _sc_rows(b)
            first = lax.broadcasted_iota(jnp.int32, (CHUNK, 1), 0) == 0
            loc = [local_of(hh, rows) for hh in heads]
            sign = jnp.where(lax.broadcasted_iota(jnp.int32, (1, LANE), 1) < DH, 1.0, -1.0)
            mid = []
            for hh in heads:
                beta, gl, decay, egc, ekd, qs, kb, kk, qk, attn = loc[hh]
                uw = uw_s[hh, rows, :]
                kd = qkv_ref[1, hh, rows, :] * ekd
                dout = do_ref[hh, rows, :]
                dg_p, dkd_p, duw_p, dgl_p = [], [], [], []
                for j in range(CPS):
                    sl = slice(j * CHUNK, (j + 1) * CHUNK)
                    s = st_ref[hh, b * CPS + j]
                    dsn = dsn_s[hh, b * CPS + j]
                    both = _dot_nt(jnp.concatenate([dsn, dout[sl]], axis=0), s)
                    dg_p.append(both[CHUNK:])
                    dsc = jnp.concatenate([dsn, -both[:CHUNK]], axis=1)
                    dkd_p.append(_dot_nt(uw[sl], dsc))
                    duw_p.append(_dot(kd[sl], dsc))
                    degl = jnp.sum(jnp.sum(s * dsn, axis=1, keepdims=True), axis=0, keepdims=True)
                    dgl_p.append(jnp.where(first, degl * jnp.exp(gl[j * CHUNK:j * CHUNK + 1, :]), 0.0))
                dg, dkd = jnp.concatenate(dg_p, axis=0), jnp.concatenate(dkd_p, axis=0)
                dod = jnp.concatenate([dout, dg], axis=1)
                da = jnp.where(causal, _dot_nt(dod, uw * sign), 0.0)
                duw = jnp.concatenate(duw_p, axis=0) + _dot_tn(attn, dod) * sign
                mid.append((kd, dg, dkd, da, duw, jnp.concatenate(dgl_p, axis=0)))
            inv_ts = [inv_ref[hh, rows, :].astype(F32).T for hh in heads]
            its = [_dot(inv_ts[hh], mid[hh][4]) for hh in heads]
            dinvs = []
            for hh in heads:
                beta, _, _, egc, _, _, kb, _, _, _ = loc[hh]
                dinvs.append(_dot_nt(mid[hh][4], jnp.concatenate([qkv_ref[2, hh, rows, :] * beta, kb * egc], axis=1)))
            half = [_dot(inv_ts[hh], dinvs[hh]) for hh in heads]
            dls = [jnp.where(strict, -_dot(half[hh], inv_ts[hh]), 0.0) for hh in heads]
            for hh in heads:
                head = hp * hp_n + hh
                beta, gl, decay, egc, ekd, qs, kb, kk, qk, attn = loc[hh]
                kd, dg, dkd, da, _, dgl_first = mid[hh]
                k, v = qkv_ref[1, hh, rows, :], qkv_ref[2, hh, rows, :]
                dvb, dkbe = its[hh][:, :DH], its[hh][:, DH:]
                dl = dls[hh]
                dlogd = (dl * kk + da * qk) * decay
                dd = jnp.concatenate([dl * decay, da * decay], axis=0)
                ddk = _dot(dd, k)
                dkb = ddk[:SC] + dkbe * egc
                dqs = ddk[SC:] + dg * egc
                dk = _dot_tn(dd, jnp.concatenate([kb, qs], axis=0)) + dkd * ekd + dkb * beta
                dkd_kd = jnp.sum(dkd * kd, axis=1, keepdims=True)
                narrow = dg * (qs * egc) + dkbe * (kb * egc)
                wide = dlogd[:, :LANE] + dlogd[:, LANE:] + jnp.concatenate([narrow, jnp.zeros((SC, LANE - DH), F32)], axis=1)
                dgc = jnp.sum(wide, axis=1, keepdims=True) - dkd_kd
                dbeta = jnp.sum(dkb * k + dvb * v, axis=1, keepdims=True)
                dqkv_ref[0, hh, rows, :] = dqs * Q_SCALE
                dqkv_ref[1, hh, rows, :] = dk
                dqkv_ref[2, hh, rows, :] = dvb * beta
                dcol_ref[rows, :] += _to_lane(dbeta, head) + _to_lane(dgc, HEADS + head)
                dtot_ref[rows, :] += _to_lane(dkd_kd + dgl_first, HEADS + head)
                drow_ref[pl.ds(head, 1), rows] = -jnp.sum(dlogd, axis=0, keepdims=True)
            return 0

        lax.fori_loop(0, nsc, back, 0)

    whole = pl.BlockSpec((t, LANE), lambda h: (0, 0))
    rowspec = pl.BlockSpec((HEADS, t), lambda h: (0, 0))
    sq = pltpu.VMEM((hp_n, nc, DH, DH), F32)
    per_head = pltpu.VMEM((hp_n, t, 2 * DH), F32)
    return pl.pallas_call(
        body, name="gdn_bwd", grid=(HEADS // hp_n,),
        in_specs=[pl.BlockSpec((3, hp_n, t, DH), lambda h: (0, h, 0, 0)), whole, whole, whole,
                  pl.BlockSpec((HEADS, t), lambda h: (1, 0)),
                  pl.BlockSpec((hp_n, t, SC), lambda h: (h, 0, 0)), pl.BlockSpec((hp_n, nc, DH, DH), lambda h: (h, 0, 0, 0)),
                  pl.BlockSpec((hp_n, t, DH), lambda h: (h, 0, 0))],
        out_specs=[pl.BlockSpec((3, hp_n, t, DH), lambda h: (0, h, 0, 0)), whole, whole, rowspec],
        out_shape=[jax.ShapeDtypeStruct((3, HEADS, t, DH), F32), jax.ShapeDtypeStruct((t, LANE), F32),
                   jax.ShapeDtypeStruct((t, LANE), F32), jax.ShapeDtypeStruct((HEADS, t), F32)],
        scratch_shapes=[per_head, sq, sq, sq],
        compiler_params=_params(("arbitrary",)),
    )(qkv, gates, run, tot, run_t, inv, states, do)


FOX_HP_FWD = 4
FOX_HP_BWD = 4


def _wide_t(a):
    r = a.shape[0]
    return jnp.concatenate([a, jnp.zeros((r, LANE - DH), F32)], axis=1).T[:DH]


def _tall_t(a):
    r = a.shape[1]
    return jnp.concatenate([a, jnp.zeros((LANE - DH, r), F32)], axis=0).T[:, :DH]


def _key_side_f(run_blk, head, qb):
    col = jnp.broadcast_to(_lane_col(run_blk, 2 * HEADS + head), (run_blk.shape[0], LANE))
    return jnp.concatenate([col] * (qb // LANE), axis=1)


def _diag_mask(qb):
    return lax.broadcasted_iota(jnp.int32, (qb, qb), 0) <= lax.broadcasted_iota(jnp.int32, (qb, qb), 1)


def _fox_fwd(qkv, run, run_t, carry=None):
    t = qkv.shape[2]
    qb = min(QB, t)
    nq = t // qb
    hp_n = FOX_HP_FWD
    c_in, c_in_specs, c_out_shape, c_out_specs, c_sem = _carry_operands(*carry) if carry else ([], [], [], [], None)
    n_c = len(c_in)

    def body(*refs):
        q_ref, k_ref, v_ref, run_ref, runt_ref = refs[:5]
        o_ref, lse_ref = refs[5 + n_c:7 + n_c]
        s0 = 7 + n_c + len(c_out_shape)
        kb_s, vt_s, raw_s, m_s, l_s, acc_s = refs[s0:s0 + 6]
        hp = pl.program_id(0)
        i = pl.program_id(1)

        if carry:
            _carry_step(carry[0], refs[5:5 + n_c], refs[7 + n_c], refs[8 + n_c], refs[-1], hp * nq + i,
                        (HEADS // hp_n) * nq)

        @pl.when(i == 0)
        def _():
            for hh in range(hp_n):
                kb_s[hh] = k_ref[0, hh].astype(MXU)
                for b in range(nq):
                    rows = slice(b * qb, (b + 1) * qb)
                    vt_s[hh, :, rows] = _wide_t(v_ref[0, hh, rows, :]).astype(MXU)

        qrows = pl.ds(pl.multiple_of(i * qb, qb), qb)
        qs = [(q_ref[0, hh] * Q_SCALE).astype(MXU) for hh in range(hp_n)]
        fq = [runt_ref[pl.ds(hp * hp_n + hh, 1), qrows] for hh in range(hp_n)]

        def block_rows(j):
            return pl.ds(pl.multiple_of(j * qb, qb), qb)

        def scores(j):
            for hh in range(hp_n):
                raw_s[j % 2, hh] = _dot_nt(kb_s[hh, block_rows(j), :], qs[hh])

        def absorb(j, diagonal):
            rows = block_rows(j)
            run_blk = run_ref[rows, :]
            stats, pv = [], []
            for hh in range(hp_n):
                m, l = m_s[hh], l_s[hh]
                st = raw_s[j % 2, hh] + (fq[hh] - _key_side_f(run_blk, hp * hp_n + hh, qb))
                if diagonal:
                    st = jnp.where(_diag_mask(qb), st, -1e30)
                m_new = jnp.maximum(m, jnp.max(st, axis=0, keepdims=True))
                p = jnp.exp(st - m_new)
                alpha = jnp.exp(m - m_new)
                stats.append((m_new, alpha * l + jnp.sum(p, axis=0, keepdims=True), alpha))
                pv.append(_dot(vt_s[hh, :, rows], p))
            for hh in range(hp_n):
                m_s[hh], l_s[hh] = stats[hh][0], stats[hh][1]
                acc_s[hh] = stats[hh][2] * acc_s[hh] + pv[hh]

        def kstep(j, _):
            scores(j + 1)
            absorb(j, False)
            return 0

        for hh in range(hp_n):
            m_s[hh] = jnp.full((1, qb), -1e30, F32)
            l_s[hh] = jnp.zeros((1, qb), F32)
            acc_s[hh] = jnp.zeros((DH, qb), F32)
        scores(0)
        lax.fori_loop(0, i, kstep, 0)
        absorb(i, True)
        for hh in range(hp_n):
            l = l_s[hh]
            o_ref[hh] = _tall_t(acc_s[hh] / l)
            lse_ref[pl.ds(hp * hp_n + hh, 1), qrows] = m_s[hh] + jnp.log(l)

    out = pl.pallas_call(
        body, name="fox_fwd", grid=(HEADS // hp_n, nq),
        in_specs=[pl.BlockSpec((1, hp_n, qb, DH), lambda h, i: (0, h, i, 0)),
                  pl.BlockSpec((1, hp_n, t, DH), lambda h, i: (1, h, 0, 0)),
                  pl.BlockSpec((1, hp_n, t, DH), lambda h, i: (2, h, 0, 0)),
                  pl.BlockSpec((t, LANE), lambda h, i: (0, 0)),
                  pl.BlockSpec((HEADS, t), lambda h, i: (2, 0))] + c_in_specs,
        out_specs=[pl.BlockSpec((hp_n, qb, DH), lambda h, i: (h, i, 0)),
                   pl.BlockSpec((HEADS, t), lambda h, i: (0, 0))] + c_out_specs,
        out_shape=[jax.ShapeDtypeStruct((HEADS, t, DH), F32), jax.ShapeDtypeStruct((HEADS, t), F32)] + c_out_shape,
        input_output_aliases={5 + j: 4 + j for j in range(n_c)},
        scratch_shapes=[pltpu.VMEM((hp_n, t, DH), MXU), pltpu.VMEM((hp_n, DH, t), MXU), pltpu.VMEM((2, hp_n, qb, qb), F32),
                        pltpu.VMEM((hp_n, 1, qb), F32), pltpu.VMEM((hp_n, 1, qb), F32), pltpu.VMEM((hp_n, DH, qb), F32)]
        + ([c_sem] if carry else []),
        compiler_params=_params(("arbitrary", "arbitrary")),
    )(qkv, qkv, qkv, run, run_t, *c_in)
    return (out[0], out[1], _carry_state(out[2:])) if carry else (out[0], out[1])


def _fox_bwd(qkv, run, run_t, o, lse, do, carry=None):
    t = qkv.shape[2]
    qb = min(QB, t)
    nq = t // qb
    hp_n = FOX_HP_BWD
    c_in, c_in_specs, c_out_shape, c_out_specs, c_sem = _carry_operands(*carry) if carry else ([], [], [], [], None)
    n_c = len(c_in)

    def body(*refs):
        q_ref, k_ref, v_ref, run_ref, runt_ref, o_ref, lse_ref, do_ref = refs[:8]
        dqkv_ref, dcol_ref, drow_ref = refs[8 + n_c:11 + n_c]
        s0 = 11 + n_c + len(c_out_shape)
        dqt_s, raw_s, dpt_s, dro_s, dk_s, dv_s, dsum_s = refs[s0:s0 + 7]
        hp = pl.program_id(0)
        j = pl.program_id(1)

        if carry:
            _carry_step(carry[0], refs[8:8 + n_c], refs[11 + n_c], refs[12 + n_c], refs[-1], hp * nq + j,
                        (HEADS // hp_n) * nq)

        @pl.when(j == 0)
        def _():
            dqt_s[...] = jnp.zeros_like(dqt_s)

        @pl.when((j == 0) & (hp == 0))
        def _():
            dcol_ref[...] = jnp.zeros_like(dcol_ref)
            drow_ref[...] = jnp.zeros_like(drow_ref)

        krows = pl.ds(pl.multiple_of(j * qb, qb), qb)
        run_blk = run_ref[krows, :]
        ones8 = jnp.ones((8, DH), MXU)
        kb, kt, vb, fk = [], [], [], []
        for hh in range(hp_n):
            kf = k_ref[0, hh]
            kb.append(kf.astype(MXU))
            kt.append(_wide_t(kf).astype(MXU))
            vb.append(v_ref[0, hh].astype(MXU))
            fk.append(_key_side_f(run_blk, hp * hp_n + hh, qb))

        def block_rows(i):
            return pl.ds(i * qb if isinstance(i, int) else pl.multiple_of(i * qb, qb), qb)

        def products(i):
            rows = block_rows(i)
            slot = i % 2
            for hh in range(hp_n):
                dout = do_ref[hh, rows, :]
                x = dout * o_ref[hh, rows, :]
                x_hi = x.astype(MXU)
                raw_s[slot, hh] = _dot_nt(kb[hh], q_ref[0, hh, rows, :] * Q_SCALE)
                dpt_s[slot, hh] = _dot_nt(vb[hh], dout)
                dro_s[slot, hh] = _dot_nt(ones8, x_hi) + _dot_nt(ones8, x - x_hi.astype(F32))

        def absorb(i, diagonal):
            rows = block_rows(i)
            slot = i % 2
            pieces = []
            for hh in range(hp_n):
                head = hp * hp_n + hh
                raw, dpt, drow = raw_s[slot, hh], dpt_s[slot, hh], dro_s[slot, hh, 0:1, :]
                off = runt_ref[pl.ds(head, 1), rows] - lse_ref[pl.ds(head, 1), rows]
                st = raw + (off - fk[hh])
                if diagonal:
                    st = jnp.where(_diag_mask(qb), st, -1e30)
                pt = jnp.exp(st)
                dst = pt * (dpt - drow)
                drow_ref[pl.ds(head, 1), rows] += jnp.sum(dst, axis=0, keepdims=True)
                folded = dst[:, 0:LANE]
                for c in range(1, qb // LANE):
                    folded = folded + dst[:, c * LANE:(c + 1) * LANE]
                pieces.append((_dot(dst, q_ref[0, hh, rows, :] * Q_SCALE), _dot(pt, do_ref[hh, rows, :]),
                               _dot(kt[hh], dst), folded))
            for hh in range(hp_n):
                dqt_s[hh, :, rows] += pieces[hh][2]
                if diagonal:
                    dk_s[hh], dv_s[hh], dsum_s[hh] = pieces[hh][0], pieces[hh][1], pieces[hh][3]
                else:
                    dk_s[hh] += pieces[hh][0]
                    dv_s[hh] += pieces[hh][1]
                    dsum_s[hh] += pieces[hh][3]

        def qstep(i, _):
            products(i + 1)
            absorb(i, False)
            return 0

        products(j)
        products(jnp.minimum(j + 1, nq - 1))
        absorb(j, True)
        lax.fori_loop(j + 1, nq - 1, qstep, 0)

        @pl.when(j < nq - 1)
        def _():
            absorb(nq - 1, False)
        for hh in range(hp_n):
            dqkv_ref[1, hh, krows, :] = dk_s[hh]
            dqkv_ref[2, hh, krows, :] = dv_s[hh]
            dcol_ref[krows, :] += _to_lane(-jnp.sum(dsum_s[hh], axis=1, keepdims=True), 2 * HEADS + hp * hp_n + hh)

        @pl.when(j == nq - 1)
        def _():
            for hh in range(hp_n):
                for b in range(nq):
                    rows = slice(b * qb, (b + 1) * qb)
                    dqkv_ref[0, hh, rows, :] = _tall_t(dqt_s[hh, :, rows]) * Q_SCALE

    once = dict(pipeline_mode=pl.Buffered(1))
    full = pl.BlockSpec((hp_n, t, DH), lambda h, j: (h, 0, 0), **once)
    rows8 = pl.BlockSpec((HEADS, t), lambda h, j: (0, 0))
    out = pl.pallas_call(
        body, name="fox_bwd", grid=(HEADS // hp_n, nq),
        in_specs=[pl.BlockSpec((1, hp_n, t, DH), lambda h, j: (0, h, 0, 0), **once),
                  pl.BlockSpec((1, hp_n, qb, DH), lambda h, j: (1, h, j, 0)),
                  pl.BlockSpec((1, hp_n, qb, DH), lambda h, j: (2, h, j, 0)),
                  pl.BlockSpec((t, LANE), lambda h, j: (0, 0), **once), pl.BlockSpec((HEADS, t), lambda h, j: (2, 0)),
                  full, rows8, full] + c_in_specs,
        out_specs=[pl.BlockSpec((3, hp_n, t, DH), lambda h, j: (0, h, 0, 0), **once),
                   pl.BlockSpec((t, LANE), lambda h, j: (0, 0)), rows8] + c_out_specs,
        out_shape=[jax.ShapeDtypeStruct((3, HEADS, t, DH), F32), jax.ShapeDtypeStruct((t, LANE), F32),
                   jax.ShapeDtypeStruct((HEADS, t), F32)] + c_out_shape,
        input_output_aliases={8 + j: 5 + j for j in range(n_c)},
        scratch_shapes=[pltpu.VMEM((hp_n, DH, t), F32), pltpu.VMEM((2, hp_n, qb, qb), F32), pltpu.VMEM((2, hp_n, qb, qb), F32),
                        pltpu.VMEM((2, hp_n, 8, qb), F32), pltpu.VMEM((hp_n, qb, DH), F32), pltpu.VMEM((hp_n, qb, DH), F32),
                        pltpu.VMEM((hp_n, qb, LANE), F32)] + ([c_sem] if carry else []),
        compiler_params=_params(("arbitrary", "arbitrary")),
    )(qkv, qkv, qkv, run, run_t, o, lse, do, *c_in)
    return (out[0], out[1], out[2], _carry_state(out[3:])) if carry else tuple(out)


Z_COL0 = 3 * WIDTH // LANE
FGATE_COL0 = 7 * WIDTH // LANE


def _gdn_post(o, pm, onw):
    t = pm.shape[0]

    def body(o_ref, z_ref, w_ref, m_ref):
        z = z_ref[...]
        sz = z * _sigmoid(z)
        halves = []
        for hh in range(2):
            ov = o_ref[hh]
            n = ov * lax.rsqrt(jnp.mean(ov * ov, axis=-1, keepdims=True) + EPS) * w_ref[...]
            halves.append(n * sz[:, hh * DH:(hh + 1) * DH])
        m_ref[...] = jnp.concatenate(halves, axis=1).astype(m_ref.dtype)

    return pl.pallas_call(
        body, name="gdn_post", grid=(WIDTH // LANE,),
        in_specs=[pl.BlockSpec((2, t, DH), lambda j: (j, 0, 0)), pl.BlockSpec((t, LANE), lambda j: (0, Z_COL0 + j)),
                  pl.BlockSpec((1, DH), lambda j: (0, 0))],
        out_specs=pl.BlockSpec((t, LANE), lambda j: (0, j)),
        out_shape=jax.ShapeDtypeStruct((t, WIDTH), MXU),
        compiler_params=_params(("arbitrary",)),
    )(o, pm, onw)


def _gdn_post_bwd(o, pm, onw, dmix):
    t = pm.shape[0]

    def body(o_ref, z_ref, w_ref, dm_ref, do_ref, dz_ref, dw_ref):
        @pl.when(pl.program_id(0) == 0)
        def _():
            dw_ref[...] = jnp.zeros_like(dw_ref)

        z = z_ref[...]
        sg = _sigmoid(z)
        sz = z * sg
        dsz = sg * (1.0 + z * (1.0 - sg))
        dm = dm_ref[...]
        for hh in range(2):
            cols = slice(hh * DH, (hh + 1) * DH)
            ov = o_ref[hh]
            r = lax.rsqrt(jnp.mean(ov * ov, axis=-1, keepdims=True) + EPS)
            xn = ov * r
            dmh = dm[:, cols]
            dn = dmh * sz[:, cols]
            dz_ref[:, cols] = (dmh * (xn * w_ref[...]) * dsz[:, cols]).astype(dz_ref.dtype)
            dw_ref[...] += jnp.sum(dn * xn, axis=0, keepdims=True)
            g = dn * w_ref[...]
            do_ref[hh] = r * (g - xn * jnp.mean(g * xn, axis=-1, keepdims=True))

    return pl.pallas_call(
        body, name="gdn_post_bwd", grid=(WIDTH // LANE,),
        in_specs=[pl.BlockSpec((2, t, DH), lambda j: (j, 0, 0)), pl.BlockSpec((t, LANE), lambda j: (0, Z_COL0 + j)),
                  pl.BlockSpec((1, DH), lambda j: (0, 0)), pl.BlockSpec((t, LANE), lambda j: (0, j))],
        out_specs=[pl.BlockSpec((2, t, DH), lambda j: (j, 0, 0)), pl.BlockSpec((t, LANE), lambda j: (0, j)),
                   pl.BlockSpec((1, DH), lambda j: (0, 0))],
        out_shape=[jax.ShapeDtypeStruct((HEADS, t, DH), F32), jax.ShapeDtypeStruct((t, WIDTH), MXU),
                   jax.ShapeDtypeStruct((1, DH), F32)],
        compiler_params=_params(("arbitrary",)),
    )(o, pm, onw, dmix)


def _fox_post(o, pm):
    t = pm.shape[0]

    def body(o_ref, g_ref, m_ref):
        m_ref[...] = (jnp.concatenate([o_ref[0], o_ref[1]], axis=1) * _sigmoid(g_ref[...])).astype(m_ref.dtype)

    return pl.pallas_call(
        body, name="fox_post", grid=(WIDTH // LANE,),
        in_specs=[pl.BlockSpec((2, t, DH), lambda j: (j, 0, 0)), pl.BlockSpec((t, LANE), lambda j: (0, FGATE_COL0 + j))],
        out_specs=pl.BlockSpec((t, LANE), lambda j: (0, j)),
        out_shape=jax.ShapeDtypeStruct((t, WIDTH), MXU),
        compiler_params=_params(("arbitrary",)),
    )(o, pm)


def _fox_post_bwd(o, pm, dmix):
    t = pm.shape[0]

    def body(o_ref, g_ref, dm_ref, do_ref, dg_ref):
        sg = _sigmoid(g_ref[...])
        dm = dm_ref[...]
        for hh in range(2):
            cols = slice(hh * DH, (hh + 1) * DH)
            do_ref[hh] = dm[:, cols] * sg[:, cols]
            dg_ref[:, cols] = (dm[:, cols] * o_ref[hh] * (sg * (1.0 - sg))[:, cols]).astype(dg_ref.dtype)

    return pl.pallas_call(
        body, name="fox_post_bwd", grid=(WIDTH // LANE,),
        in_specs=[pl.BlockSpec((2, t, DH), lambda j: (j, 0, 0)), pl.BlockSpec((t, LANE), lambda j: (0, FGATE_COL0 + j)),
                  pl.BlockSpec((t, LANE), lambda j: (0, j))],
        out_specs=[pl.BlockSpec((2, t, DH), lambda j: (j, 0, 0)), pl.BlockSpec((t, LANE), lambda j: (0, j))],
        out_shape=[jax.ShapeDtypeStruct((HEADS, t, DH), F32), jax.ShapeDtypeStruct((t, WIDTH), MXU)],
        compiler_params=_params(("arbitrary",)),
    )(o, pm, dmix)


def _tail(x, mixg, mixf, tgt, wo, n2w, wg_t, wu_t, wd, fw):
    t, d = x.shape
    dff = wd.shape[0]
    tb = min(TB, t)

    def body(x_ref, mg_ref, mf_ref, t_ref, wo_ref, n2_ref, wg_ref, wu_ref, wd_ref, fw_ref,
             h2_ref, act_ref, dgate_ref, dup_ref, dx3_ref, dx2_ref, dmg_ref, dmf_ref, dn2_ref, dfw_ref, loss_ref):
        @pl.when(pl.program_id(0) == 0)
        def _():
            dn2_ref[...] = jnp.zeros_like(dn2_ref)
            dfw_ref[...] = jnp.zeros_like(dfw_ref)
            loss_ref[...] = jnp.zeros_like(loss_ref)

        x2 = x_ref[...] + _dot(mg_ref[...], wo_ref[0:WIDTH, :]) + _dot(mf_ref[...], wo_ref[WIDTH:2 * WIDTH, :])
        r2 = lax.rsqrt(jnp.mean(x2 * x2, axis=-1, keepdims=True) + EPS)
        xn2 = x2 * r2
        h2 = (xn2 * n2_ref[...]).astype(MXU)
        h2_ref[...] = h2
        gate = _dot_nt(h2, wg_ref[...])
        up = _dot_nt(h2, wu_ref[...])
        sg = _sigmoid(gate)
        sl = gate * sg
        act = (sl * up).astype(MXU)
        act_ref[...] = act
        x3 = x2 + _dot(act, wd_ref[...])
        r3 = lax.rsqrt(jnp.mean(x3 * x3, axis=-1, keepdims=True) + EPS)
        xn3 = x3 * r3
        err = xn3 * fw_ref[...] - t_ref[...]
        loss_ref[...] += 0.5 * jnp.sum(jnp.mean(err * err, axis=-1, keepdims=True), axis=0, keepdims=True)
        dy = err * (1.0 / d)
        dfw_ref[...] += jnp.sum(dy * xn3, axis=0, keepdims=True)
        g3 = dy * fw_ref[...]
        dx3 = r3 * (g3 - xn3 * jnp.mean(g3 * xn3, axis=-1, keepdims=True))
        dx3_ref[...] = dx3.astype(MXU)
        dact = _dot_nt(dx3, wd_ref[...])
        dgate = (dact * up * (sg * (1.0 + gate * (1.0 - sg)))).astype(MXU)
        dup = (dact * sl).astype(MXU)
        dgate_ref[...] = dgate
        dup_ref[...] = dup
        dh2 = _dot(dgate, wg_ref[...]) + _dot(dup, wu_ref[...])
        dn2_ref[...] += jnp.sum(dh2 * xn2, axis=0, keepdims=True)
        g2 = dh2 * n2_ref[...]
        dx2 = dx3 + r2 * (g2 - xn2 * jnp.mean(g2 * xn2, axis=-1, keepdims=True))
        dx2_ref[...] = dx2
        dmg_ref[...] = _dot_nt(dx2, wo_ref[0:WIDTH, :])
        dmf_ref[...] = _dot_nt(dx2, wo_ref[WIDTH:2 * WIDTH, :])

    def tok(n):
        return pl.BlockSpec((tb, n), lambda i: (i, 0))

    acc = pl.BlockSpec((1, d), lambda i: (0, 0))
    sds = jax.ShapeDtypeStruct
    return pl.pallas_call(
        body, name="tail", grid=(t // tb,),
        in_specs=[tok(d), tok(WIDTH), tok(WIDTH), tok(d), _resident(wo.shape), _resident((1, d)),
                  _resident(wg_t.shape), _resident(wu_t.shape), _resident(wd.shape), _resident((1, d))],
        out_specs=[tok(d), tok(dff), tok(dff), tok(dff), tok(d), tok(d), tok(WIDTH), tok(WIDTH), acc, acc,
                   pl.BlockSpec((1, 1), lambda i: (0, 0))],
        out_shape=[sds((t, d), MXU), sds((t, dff), MXU), sds((t, dff), MXU), sds((t, dff), MXU), sds((t, d), MXU),
                   sds((t, d), F32), sds((t, WIDTH), F32), sds((t, WIDTH), F32), sds((1, d), F32), sds((1, d), F32),
                   sds((1, 1), F32)],
        compiler_params=_params(("arbitrary",)),
    )(x, mixg, mixf, tgt, wo, n2w, wg_t, wu_t, wd, fw)


def _wgrads(a_list, b, name):
    t, n = b.shape
    ms = [a.shape[1] for a in a_list]
    bms = [256 if m % 256 == 0 else LANE for m in ms]
    nbs = [m // bm for m, bm in zip(ms, bms)]
    k = len(a_list)
    cast = b.dtype != jnp.dtype(MXU)

    def body(*refs):
        a_refs, b_ref, o_refs = refs[:k], refs[k], refs[k + 1:2 * k + 1]
        i = pl.program_id(0)
        if cast:
            @pl.when(i == 0)
            def _():
                refs[-1][...] = b_ref[...].astype(MXU)
        for a_ref, o_ref, nb in zip(a_refs, o_refs, nbs):
            @pl.when(i < nb)
            def _():
                o_ref[...] = _dot_tn(a_ref[...], refs[-1][...] if cast else b_ref[...]).astype(o_ref.dtype)

    def clamp(nb):
        return lambda i: jnp.minimum(i, nb - 1)

    return pl.pallas_call(
        body, name=name, grid=(max(nbs),),
        in_specs=[pl.BlockSpec((t, bm), lambda i, c=clamp(nb): (0, c(i))) for bm, nb in zip(bms, nbs)] + [_resident((t, n))],
        out_specs=[pl.BlockSpec((bm, n), lambda i, c=clamp(nb): (c(i), 0)) for bm, nb in zip(bms, nbs)],
        out_shape=[jax.ShapeDtypeStruct((m, n), WIRE) for m in ms],
        scratch_shapes=[pltpu.VMEM((t, n), MXU)] if cast else [],
        compiler_params=_params(("arbitrary",)),
    )(*a_list, b)


def _merge_dw_in(d_gdn, d_z, d_fox, d_fg, d_small, shards=False):
    pieces = [d_gdn, d_z, d_small[:2 * HEADS], d_fox, d_fg, d_small[2 * HEADS:3 * HEADS]]
    if not shards:
        return jnp.concatenate(pieces, axis=0)
    n = sum(p.shape[0] for p in pieces) // N_DEV
    out = []
    for dev in range(N_DEV):
        parts, first = [], 0
        for p in pieces:
            lo, hi = max(dev * n, first), min((dev + 1) * n, first + p.shape[0])
            if lo < hi:
                parts.append(p[lo - first:hi - first])
            first += p.shape[0]
        out.append(jnp.concatenate(parts, axis=0))
    return jnp.stack(out)


def _lanes(*pieces):
    v = jnp.concatenate([p.reshape(-1).astype(F32) for p in pieces])
    return jnp.pad(v, (0, LANE - v.shape[0])).reshape(1, LANE)


def _vector_params(p):
    d = p["norm1_w"].size
    gparams = jnp.concatenate([_lanes(jnp.zeros(HEADS), p["gdn_dt_bias"], p["fox_f_bias"]),
                               _lanes(jnp.zeros(HEADS), p["gdn_A_log"]), jnp.zeros((6, LANE), F32)])
    fox_nw = jnp.stack([jnp.tile(p["fox_q_norm_w"].reshape(-1), 2), jnp.tile(p["fox_k_norm_w"].reshape(-1), 2),
                        jnp.ones((LANE,), F32)])
    return dict(n1w=p["norm1_w"].reshape(1, d), n2w=p["norm2_w"].reshape(1, d), fw=p["final_norm_w"].reshape(1, d),
                onw=p["gdn_out_norm_w"].reshape(1, DH), gparams=gparams, fox_nw=fox_nw)


def _mixer_forward(x, vp, w_t, ws_t, conv_w, carry=None, carry_first=None):
    h1, pm, ps, *first = _inproj(x, vp["n1w"], w_t, ws_t, carry_first)
    gates, run, tot, run_t = _gates(ps, vp["gparams"])
    fqkv = _fox_prep(pm, vp["fox_nw"])
    o_fox, lse, *carried = _fox_fwd(fqkv, run, run_t, carry)
    if carry:
        pm, o_fox = lax.optimization_barrier((pm, o_fox))
    mixf = _fox_post(o_fox, pm)
    gqkv = _gdn_prep(pm, conv_w)
    o_gdn, states, inv = _gdn_fwd(gqkv, gates, run, tot, run_t)
    mixg = _gdn_post(o_gdn, pm, vp["onw"])
    return dict(h1=h1, pm=pm, ps=ps, gates=gates, run=run, tot=tot, run_t=run_t, gqkv=gqkv, o_gdn=o_gdn, states=states,
                inv=inv, mixg=mixg, fqkv=fqkv, o_fox=o_fox, lse=lse, mixf=mixf, carried=carried[0] if carried else None,
                carried_first=first[0] if first else None)


def _mixer_backward(x, vp, w_t, ws_t, conv_w, f, dx2, dmixg, dmixf, carry=None, scatter_own=False):
    pm = f["pm"]
    do_fox, dfg = _fox_post_bwd(f["o_fox"], pm, dmixf)
    dfqkv, dcol_f, drow_f, *carried = _fox_bwd(f["fqkv"], f["run"], f["run_t"], f["o_fox"], f["lse"], do_fox, carry)
    dfox, dfnw = _fox_prep_bwd(pm, vp["fox_nw"], dfqkv)
    do_gdn, dz, donw = _gdn_post_bwd(f["o_gdn"], pm, vp["onw"], dmixg)
    dgqkv, dcol_g, dtot_g, drow_g = _gdn_bwd(f["gqkv"], f["gates"], f["run"], f["tot"], f["run_t"], f["inv"], f["states"], do_gdn)
    dgdn, dconv = _gdn_prep_bwd(pm, conv_w, dgqkv)
    dps, gsum = _gates_bwd(f["ps"], vp["gparams"], dcol_g, dtot_g, dcol_f, drow_g, drow_f)
    dw_pieces = _wgrads([dgdn, dz, dfox, dfg, dps], f["h1"], "dw_in")
    dw_in = _merge_dw_in(*dw_pieces)
    own = ("scatter", [_merge_dw_in(*dw_pieces, shards=True), _cut(dconv, 1)]) if scatter_own else None
    grad_x, dn1w, *sent = _inproj_bwd(x, vp["n1w"], dx2, dgdn, dz, dfox, dfg, dps, w_t, ws_t, own)
    small = dict(dn1w=dn1w, gsum=gsum, donw=donw, dfnw=dfnw)
    return (grad_x, dw_in, dconv, small, *carried, *sent)


VECTORS = ("norm1_w", "norm2_w", "final_norm_w", "gdn_A_log", "gdn_dt_bias", "gdn_out_norm_w", "fox_f_bias",
           "fox_q_norm_w", "fox_k_norm_w")
VEC_ROWS = 16
LOSS_ROW = len(VECTORS)


def _pack_vectors(dn1w, dn2w, dfw, gsum, donw, dfnw, loss):
    d = dn1w.shape[1]

    def body(n1_ref, n2_ref, fw_ref, gs_ref, on_ref, fn_ref, loss_ref, o_ref):
        o_ref[...] = jnp.zeros_like(o_ref)
        o_ref[0:1, :] = n1_ref[...]
        o_ref[1:2, :] = n2_ref[...]
        o_ref[2:3, :] = fw_ref[...]
        o_ref[3:4, 0:HEADS] = gs_ref[0:1, 0:HEADS]
        o_ref[4:5, 0:HEADS] = gs_ref[1:2, 0:HEADS]
        o_ref[5:6, 0:DH] = on_ref[...]
        o_ref[6:7, 0:HEADS] = gs_ref[2:3, 0:HEADS]
        for kind in range(2):
            v = fn_ref[kind, 0]
            for j in range(1, fn_ref.shape[1]):
                v = v + fn_ref[kind, j]
            o_ref[7 + kind:8 + kind, 0:DH] = v[:, :DH] + v[:, DH:]
        o_ref[LOSS_ROW:LOSS_ROW + 1, 0:1] = loss_ref[...]

    return pl.pallas_call(body, name="pack_vectors", out_shape=jax.ShapeDtypeStruct((VEC_ROWS, d), F32),
                          compiler_params=_params())(dn1w, dn2w, dfw, gsum, donw, dfnw, loss)


def _late_grads(f, h2, act, dgate, dup, dx3, dx2):
    dw_gate, dw_up = _wgrads([dgate, dup], h2, "dw_gate_up")
    (dw_down,) = _wgrads([act], dx3, "dw_down")
    return {"w_out": jnp.concatenate(_wgrads([f["mixg"], f["mixf"]], dx2, "dw_out"), axis=0),
            "w_ffn_gate": dw_gate, "w_ffn_up": dw_up, "w_ffn_down": dw_down}


def _local_step(x, tgt, p, w_in_t, conv_w, wo, wg_t, wu_t, wd):
    vp = _vector_params(p)
    ws_t = _small_rows(w_in_t)
    f = _mixer_forward(x, vp, w_in_t, ws_t, conv_w)
    (h2, act, dgate, dup, dx3, dx2, dmixg, dmixf, dn2w, dfw, loss) = _tail(
        x, f["mixg"], f["mixf"], tgt, wo, vp["n2w"], wg_t, wu_t, wd, vp["fw"])
    grad_x, dw_in, dconv, small = _mixer_backward(x, vp, w_in_t, ws_t, conv_w, f, dx2, dmixg, dmixf)
    grads = {"w_in": dw_in, "gdn_conv_w": dconv, **_late_grads(f, h2, act, dgate, dup, dx3, dx2)}
    vec = _pack_vectors(small["dn1w"], dn2w, dfw, small["gsum"], small["donw"], small["dfnw"], loss)
    return grad_x[0], grads, vec


def _my_place():
    return lax.axis_index("x"), lax.axis_index("y"), lax.axis_index("c")


def _peers():
    x, y, c = _my_place()
    peers = []
    for k in range(1, N_DEV):
        px = 1 - x if k & 4 else x
        py = 1 - y if k & 2 else y
        pc = 1 - c if k & 1 else c
        peers.append(((px, py, pc), 4 * px + 2 * py + pc))
    return 4 * x + 2 * y + c, peers


def _spread_copies(kind, srcs, lands, send_sems, recv_sems):
    me, peers = _peers()
    kinds = [kind] * len(srcs) if isinstance(kind, str) else kind
    remote, local = [], []
    for i, (kd, src, land) in enumerate(zip(kinds, srcs, lands)):
        for k, (dev, idx) in enumerate(peers):
            remote.append(pltpu.make_async_remote_copy(
                src_ref=src if kd == "gather" else src.at[idx], dst_ref=land.at[me],
                send_sem=send_sems.at[i * (N_DEV - 1) + k], recv_sem=recv_sems.at[i * (N_DEV - 1) + k],
                device_id=dev, device_id_type=MESH))
        local.append((src if kd == "gather" else src.at[me], land.at[me]))
    return remote, local


def _gather_two_level(arrays, name):
    n = len(arrays)

    def body(*refs):
        srcs, lands = refs[:n], refs[n:2 * n]
        send_sems, recv_sems, local_sems = refs[2 * n:]
        x, y, c = _my_place()
        me, sibling = (x, y, c), (x, y, 1 - c)
        chips = [(1 - x, y), (x, 1 - y), (1 - x, 1 - y)]

        def index(p):
            return 4 * p[0] + 2 * p[1] + p[2]

        def copy(i, k, block, to, src=None):
            blk = lands[i].at[index(block)]
            return pltpu.make_async_remote_copy(
                src_ref=blk if src is None else src, dst_ref=blk, send_sem=send_sems.at[7 * i + k],
                recv_sem=recv_sems.at[7 * i + k], device_id=to, device_id_type=MESH)

        mine = [pltpu.make_async_copy(srcs[i], lands[i].at[index(me)], local_sems.at[i]) for i in range(n)]
        for cp in mine:
            cp.start()
        first = []
        for i in range(n):
            first.append(copy(i, 0, me, sibling, src=srcs[i]))
            first += [copy(i, 1 + j, me, (*chip, c), src=srcs[i]) for j, chip in enumerate(chips)]
        for cp in first:
            cp.start()
        passed = []
        for i in range(n):
            for j, chip in enumerate(chips):
                copy(i, 1 + j, (*chip, c), me).wait_recv()
                passed.append(copy(i, 4 + j, (*chip, c), sibling))
                passed[-1].start()
        for i in range(n):
            copy(i, 0, sibling, me).wait_recv()
            for j, chip in enumerate(chips):
                copy(i, 4 + j, (*chip, 1 - c), me).wait_recv()
        for cp in first + passed:
            cp.wait_send()
        for cp in mine:
            cp.wait()

    return pl.pallas_call(
        body, name=name,
        out_shape=[jax.ShapeDtypeStruct((N_DEV,) + a.shape, a.dtype) for a in arrays],
        in_specs=[pl.BlockSpec(memory_space=pl.ANY)] * n, out_specs=[pl.BlockSpec(memory_space=pl.ANY)] * n,
        scratch_shapes=[pltpu.SemaphoreType.DMA((7 * n,)), pltpu.SemaphoreType.DMA((7 * n,)),
                        pltpu.SemaphoreType.DMA((n,))],
    )(*arrays)


def _land_shape(kind, a):
    return (N_DEV,) + a.shape if kind == "gather" else a.shape


HBM = pl.BlockSpec(memory_space=pltpu.HBM)
SEM = pl.BlockSpec(memory_space=pltpu.SEMAPHORE)


def _hbm(a):
    return pltpu.with_memory_space_constraint(a, pltpu.HBM)


def _carry_operands(kind, arrays):
    n = len(arrays)
    kinds = [kind] * n if isinstance(kind, str) else kind
    lands = [lax.empty(_land_shape(kd, a), a.dtype) for kd, a in zip(kinds, arrays)]
    sems = [pltpu.SemaphoreType.DMA((n * (N_DEV - 1),))] * 2
    return ([_hbm(a) for a in list(arrays) + lands], [HBM] * (2 * n),
            sems + [pltpu.HBM(a.shape, a.dtype) for a in list(arrays) + lands], [SEM] * 2 + [HBM] * (2 * n),
            pltpu.SemaphoreType.DMA((n,)))


def _carry_step(kind, in_refs, send_sems, recv_sems, local_sems, step, n_steps):
    n = len(in_refs) // 2
    remote, local = _spread_copies(kind, in_refs[:n], in_refs[n:], send_sems, recv_sems)
    copies = [pltpu.make_async_copy(s, d, local_sems.at[i]) for i, (s, d) in enumerate(local)]
    per_step = -(-len(remote) // n_steps)
    for s in range(-(-len(remote) // per_step)):
        @pl.when(step == s)
        def _():
            for cp in remote[s * per_step:(s + 1) * per_step]:
                cp.start()
            if s == 0:
                for cp in copies:
                    cp.start()

    @pl.when(step == n_steps - 1)
    def _():
        for cp in copies:
            cp.wait()


def _carry_state(extra_out):
    n = (len(extra_out) - 2) // 2
    return list(extra_out[2:2 + n]), list(extra_out[2 + n:]), extra_out[0], extra_out[1]


def _spread_start(arrays, kind, name):
    n = len(arrays)

    def body(*refs):
        srcs, lands = refs[:n], refs[n:2 * n]
        send_sems, recv_sems = refs[2 * n], refs[2 * n + 1]
        token = refs[4 * n + 2]
        local_sems = refs[4 * n + 3]
        remote, local = _spread_copies(kind, srcs, lands, send_sems, recv_sems)
        for cp in remote:
            cp.start()
        copies = [pltpu.make_async_copy(s, d, local_sems.at[i]) for i, (s, d) in enumerate(local)]
        for cp in copies:
            cp.start()
        for cp in copies:
            cp.wait()
        token[...] = jnp.zeros_like(token)

    sems = (pltpu.SemaphoreType.DMA((n * (N_DEV - 1),)),) * 2
    kinds = [kind] * n if isinstance(kind, str) else kind
    lands = [lax.empty(_land_shape(kd, a), a.dtype) for kd, a in zip(kinds, arrays)]
    out = pl.pallas_call(
        body, name=name,
        out_shape=sems + tuple(pltpu.HBM(a.shape, a.dtype) for a in list(arrays) + lands)
        + (jax.ShapeDtypeStruct((8, LANE), F32),),
        in_specs=[HBM] * (2 * n), out_specs=tuple([SEM] * 2 + [HBM] * (2 * n) + [pl.BlockSpec(memory_space=pltpu.VMEM)]),
        input_output_aliases={j: 2 + j for j in range(2 * n)},
        scratch_shapes=[pltpu.SemaphoreType.DMA((n,))],
        compiler_params=pltpu.CompilerParams(has_side_effects=pltpu.SideEffectType.DATAFLOW_SIDE_EFFECTING),
    )(*[_hbm(a) for a in arrays], *[_hbm(a) for a in lands])
    return (list(out[2:2 + n]), list(out[2 + n:2 + 2 * n]), out[0], out[1]), out[-1]


def _spread_wait(state, kind, after, name):
    srcs, lands, send_sems, recv_sems = state
    n = len(srcs)
    after = list(after) if isinstance(after, (list, tuple)) else [after]

    def body(*refs):
        remote, _ = _spread_copies(kind, refs[:n], refs[n:2 * n], refs[2 * n], refs[2 * n + 1])
        for cp in remote:
            cp.wait_send()
        for cp in remote:
            cp.wait_recv()

    out = pl.pallas_call(
        body, name=name,
        out_shape=tuple(pltpu.HBM(a.shape, a.dtype) for a in srcs + lands),
        in_specs=[HBM] * (2 * n) + [SEM, SEM] + [pl.BlockSpec(memory_space=pl.ANY)] * len(after),
        out_specs=tuple([HBM] * (2 * n)),
        input_output_aliases={j: j for j in range(2 * n)},
        compiler_params=pltpu.CompilerParams(has_side_effects=pltpu.SideEffectType.DATAFLOW_SIDE_EFFECTING),
    )(*srcs, *lands, send_sems, recv_sems, *after)
    return list(out[n:])


ADAM_ROWS = 128
ADAM_COLS = 256


def _adam_math(g, w, m, v):
    nm = ADAM_B1 * m + (1.0 - ADAM_B1) * g
    nv = ADAM_B2 * v + (1.0 - ADAM_B2) * (g * g)
    m_hat = nm / (1.0 - ADAM_B1 ** ADAM_STEP)
    v_hat = nv / (1.0 - ADAM_B2 ** ADAM_STEP)
    return -ADAM_LR * (m_hat / (jnp.sqrt(v_hat) + ADAM_EPS) + ADAM_WD * w), nm, nv


def _sum_parts(p_ref):
    g = p_ref[0].astype(F32)
    for s in range(1, N_DEV):
        g = g + p_ref[s].astype(F32)
    return g


def _adam_matrix(parts, w, m, v, name):
    _, r, c = w.shape
    rb = ADAM_ROWS if r % ADAM_ROWS == 0 else r
    cb = ADAM_COLS if (rb == r and c % ADAM_COLS == 0) else c

    def body(p_ref, w_ref, m_ref, v_ref, g_ref, d_ref, nm_ref, nv_ref):
        g = _sum_parts(p_ref)
        g_ref[0] = g
        d_ref[0], nm_ref[0], nv_ref[0] = _adam_math(g, w_ref[0], m_ref[0], v_ref[0])

    blk = pl.BlockSpec((1, rb, cb), lambda i, j: (0, i, j))
    return pl.pallas_call(
        body, name=name, grid=(r // rb, c // cb),
        in_specs=[pl.BlockSpec((N_DEV, rb, cb), lambda i, j: (0, i, j)), blk, blk, blk],
        out_specs=[blk] * 4, out_shape=[jax.ShapeDtypeStruct(w.shape, F32)] * 4,
        compiler_params=_params(("arbitrary", "arbitrary")),
    )(parts, w, m, v)


def _copy_rows(a, name):
    _, r, c = a.shape
    rb = min(TB, r)

    def body(a_ref, o_ref):
        o_ref[...] = a_ref[...]

    blk = pl.BlockSpec((1, rb, c), lambda i: (0, i, 0))
    return pl.pallas_call(body, name=name, grid=(r // rb,), in_specs=[blk], out_specs=blk,
                          out_shape=jax.ShapeDtypeStruct(a.shape, a.dtype), compiler_params=_params(("arbitrary",)))(a)


def _adam_vectors(parts, ws, ms, vs):
    nv = len(ws)

    def body(*refs):
        p_ref = refs[0]
        w_refs, m_refs, v_refs = refs[1:1 + nv], refs[1 + nv:1 + 2 * nv], refs[1 + 2 * nv:1 + 3 * nv]
        outs = refs[1 + 3 * nv:]
        g_all = _sum_parts(p_ref)
        for i in range(nv):
            n = w_refs[i].shape[1]
            g = g_all[i:i + 1, 0:n]
            d, nm, nvv = _adam_math(g, w_refs[i][...], m_refs[i][...], v_refs[i][...])
            outs[i][...] = g
            outs[nv + i][...] = d
            outs[2 * nv + i][...] = nm
            outs[3 * nv + i][...] = nvv
        outs[4 * nv][...] = g_all[LOSS_ROW:LOSS_ROW + 1, 0:1]

    shapes = [jax.ShapeDtypeStruct(a.shape, F32) for a in ws]
    out = pl.pallas_call(body, name="adam_vectors", out_shape=shapes * 4 + [jax.ShapeDtypeStruct((1, 1), F32)],
                         compiler_params=_params())(parts, *ws, *ms, *vs)
    return out[:nv], out[nv:2 * nv], out[2 * nv:3 * nv], out[3 * nv:4 * nv], out[4 * nv]


MATRICES = (("w_in", 1), ("gdn_conv_w", 1), ("w_out", 0), ("w_ffn_gate", 1), ("w_ffn_up", 1), ("w_ffn_down", 0))
TRANSPOSED = ("w_in", "w_ffn_gate", "w_ffn_up")
WEIGHTS = ("norm1_w", "w_in", "gdn_conv_w", "gdn_A_log", "gdn_dt_bias", "gdn_out_norm_w", "fox_f_bias", "fox_q_norm_w",
           "fox_k_norm_w", "w_out", "norm2_w", "w_ffn_gate", "w_ffn_up", "w_ffn_down", "final_norm_w")


def _join(blocks, axis):
    _, r, c = blocks.shape
    if axis == 0:
        return blocks.reshape(N_DEV * r, c)
    return blocks.transpose(1, 0, 2).reshape(r, N_DEV * c)


def _cut(full, axis):
    r, c = full.shape
    if axis == 0:
        return full.reshape(N_DEV, r // N_DEV, c)
    return full.reshape(r, N_DEV, c // N_DEV).transpose(1, 0, 2)


def kernel(x, norm1_w, w_in, gdn_conv_w, gdn_A_log, gdn_dt_bias, gdn_out_norm_w, fox_f_bias, fox_q_norm_w, fox_k_norm_w, w_out, norm2_w, w_ffn_gate, w_ffn_up, w_ffn_down, final_norm_w, loss_target, m_norm1_w, m_w_in, m_gdn_conv_w, m_gdn_A_log, m_gdn_dt_bias, m_gdn_out_norm_w, m_fox_f_bias, m_fox_q_norm_w, m_fox_k_norm_w, m_w_out, m_norm2_w, m_w_ffn_gate, m_w_ffn_up, m_w_ffn_down, m_final_norm_w, v_norm1_w, v_w_in, v_gdn_conv_w, v_gdn_A_log, v_gdn_dt_bias, v_gdn_out_norm_w, v_fox_f_bias, v_fox_q_norm_w, v_fox_k_norm_w, v_w_out, v_norm2_w, v_w_ffn_gate, v_w_ffn_up, v_w_ffn_down, v_final_norm_w):
    w = dict(norm1_w=norm1_w, w_in=w_in, gdn_conv_w=gdn_conv_w, gdn_A_log=gdn_A_log, gdn_dt_bias=gdn_dt_bias,
             gdn_out_norm_w=gdn_out_norm_w, fox_f_bias=fox_f_bias, fox_q_norm_w=fox_q_norm_w, fox_k_norm_w=fox_k_norm_w,
             w_out=w_out, norm2_w=norm2_w, w_ffn_gate=w_ffn_gate, w_ffn_up=w_ffn_up, w_ffn_down=w_ffn_down,
             final_norm_w=final_norm_w)
    m = dict(norm1_w=m_norm1_w, w_in=m_w_in, gdn_conv_w=m_gdn_conv_w, gdn_A_log=m_gdn_A_log, gdn_dt_bias=m_gdn_dt_bias,
             gdn_out_norm_w=m_gdn_out_norm_w, fox_f_bias=m_fox_f_bias, fox_q_norm_w=m_fox_q_norm_w,
             fox_k_norm_w=m_fox_k_norm_w, w_out=m_w_out, norm2_w=m_norm2_w, w_ffn_gate=m_w_ffn_gate,
             w_ffn_up=m_w_ffn_up, w_ffn_down=m_w_ffn_down, final_norm_w=m_final_norm_w)
    v = dict(norm1_w=v_norm1_w, w_in=v_w_in, gdn_conv_w=v_gdn_conv_w, gdn_A_log=v_gdn_A_log, gdn_dt_bias=v_gdn_dt_bias,
             gdn_out_norm_w=v_gdn_out_norm_w, fox_f_bias=v_fox_f_bias, fox_q_norm_w=v_fox_q_norm_w,
             fox_k_norm_w=v_fox_k_norm_w, w_out=v_w_out, norm2_w=v_norm2_w, w_ffn_gate=v_w_ffn_gate,
             w_ffn_up=v_w_ffn_up, w_ffn_down=v_w_ffn_down, final_norm_w=v_final_norm_w)
    late = ("w_out", "w_ffn_gate", "w_ffn_up", "w_ffn_down")
    xs, tgt = x[0], loss_target[0]
    vp = _vector_params({n: w[n] for n in VECTORS})

    def rows_of(d, n):
        return d[n].transpose(0, 2, 1) if n in TRANSPOSED else d[n]

    wr, mr, vr = ({n: rows_of(d, n) for n, _ in MATRICES} for d in (w, m, v))

    w_in_blocks, conv_blocks = _gather_two_level([wr["w_in"][0].astype(WIRE), w["gdn_conv_w"][0]], "gather_in")
    w_t = _join(w_in_blocks, 0)
    conv_w = _join(conv_blocks, 1)
    early = ("w_ffn_gate", "w_ffn_up")
    rest = tuple(n for n in late if n not in early)
    f = _mixer_forward(xs, vp, w_t, _small_rows(w_t), conv_w, ("gather", [wr[n][0].astype(WIRE) for n in rest]),
                       ("gather", [wr[n][0].astype(WIRE) for n in early]))
    full = {n: _join(b, 0) for n, b in zip(early, _spread_wait(f["carried_first"], "gather", f["mixg"], "gather_early_wait"))}
    full.update({n: _join(b, 0)
                 for n, b in zip(rest, _spread_wait(f["carried"], "gather", full[early[0]], "gather_late_wait"))})

    (h2, act, dgate, dup, dx3, dx2, dmixg, dmixf, dn2w, dfw, loss) = _tail(
        xs, f["mixg"], f["mixf"], tgt, full["w_out"], vp["n2w"], full["w_ffn_gate"], full["w_ffn_up"],
        full["w_ffn_down"], vp["fw"])
    dlate = _late_grads(f, h2, act, dgate, dup, dx3, dx2)

    grad_x, dw_in, dconv, small, state_grads, state_own = _mixer_backward(
        xs, vp, w_t, _small_rows(w_t), conv_w, f, dx2, dmixg, dmixf, ("scatter", [_cut(dlate[n], 0) for n in late]),
        scatter_own=True)
    vec = _pack_vectors(small["dn1w"], dn2w, dfw, small["gsum"], small["donw"], small["dfnw"], loss)

    state_vec, token = _spread_start([vec], "gather", "vectors_start")
    parts = dict(zip(late, _spread_wait(state_grads, "scatter", token, "grads_late_wait")))
    results = [{}, {}, {}, {}]

    updated = {}

    def update(n):
        out = _adam_matrix(parts[n], wr[n], mr[n], vr[n], "adam_" + n)
        updated[n] = out[0]
        for d, a in zip(results, out):
            d[n] = a.transpose(0, 2, 1) if n in TRANSPOSED else a

    for n in late:
        update(n)
    grad_x = _copy_rows(grad_x, "grad_x_result")
    ready = [updated[n] for n in late] + [d["w_in"] for d in (wr, mr, vr)] + [grad_x]
    (parts_vec,) = _spread_wait(state_vec, "gather", ready, "vectors_wait")
    row = lambda a: a.reshape(1, -1)
    *vec_out, total_loss = _adam_vectors(parts_vec, [row(w[n]) for n in VECTORS], [row(m[n]) for n in VECTORS],
                                         [row(v[n]) for n in VECTORS])
    parts["w_in"], parts["gdn_conv_w"] = _spread_wait(state_own, "scatter", parts_vec, "own_wait")
    update("w_in")
    update("gdn_conv_w")
    for d, arrs in zip(results, vec_out):
        for n, a in zip(VECTORS, arrs):
            d[n] = a.reshape(w[n].shape)
    return (total_loss[0, 0], grad_x, *[d[n] for d in results for n in WEIGHTS])
```

```python
import functools

import jax
import jax.numpy as jnp
from jax import lax
from jax.experimental import pallas as pl
from jax.experimental.pallas import tpu as pltpu

F32 = jnp.float32
MXU = jnp.bfloat16
WIRE = jnp.bfloat16
HI = lax.Precision.HIGHEST
EPS = 1e-6

N_DEV = 8
HEADS = 8
DH = 64
WIDTH = HEADS * DH
CHUNK = 64
LANE = 128
ROW_ALIGN = 16
TB = 256
QB = 256
VMEM_LIMIT = 60 * 1024 * 1024

ADAM_LR = 0.001
ADAM_B1 = 0.9
ADAM_B2 = 0.999
ADAM_EPS = 1e-08
ADAM_WD = 0.01
ADAM_STEP = 10

MESH = pl.DeviceIdType.MESH


def _params(sem=None):
    return pltpu.CompilerParams(dimension_semantics=sem, vmem_limit_bytes=VMEM_LIMIT)


def _resident(shape):
    n = len(shape)
    return pl.BlockSpec(shape, lambda *_: (0,) * n, pipeline_mode=pl.Buffered(1))


def _dot(a, b):
    return jnp.dot(a.astype(MXU), b.astype(MXU), preferred_element_type=F32)


def _dot_nt(a, b):
    return lax.dot_general(a.astype(MXU), b.astype(MXU), (((1,), (1,)), ((), ())), preferred_element_type=F32)


def _dot_tn(a, b):
    return lax.dot_general(a.astype(MXU), b.astype(MXU), (((0,), (0,)), ((), ())), preferred_element_type=F32)


def _hdot(a, b):
    return jnp.dot(a, b, precision=HI, preferred_element_type=F32)


def _hdot_nt(a, b):
    return lax.dot_general(a, b, (((1,), (1,)), ((), ())), precision=HI, preferred_element_type=F32)


def _hdot_tn(a, b):
    return lax.dot_general(a, b, (((0,), (0,)), ((), ())), precision=HI, preferred_element_type=F32)


def _sigmoid(x):
    return 0.5 * jnp.tanh(0.5 * x) + 0.5


def _softplus(x):
    return jnp.maximum(x, 0.0) + jnp.log(1.0 + jnp.exp(-jnp.abs(x)))


def _head_sum_matrix():
    ri = lax.broadcasted_iota(jnp.int32, (LANE, LANE), 0) // DH
    ci = lax.broadcasted_iota(jnp.int32, (LANE, LANE), 1) // DH
    return (ri == ci).astype(F32)


def _group_sum(a, ones_matrix):
    hi = a.astype(jnp.bfloat16)
    lo = (a - hi.astype(F32)).astype(jnp.bfloat16)
    m = ones_matrix.astype(jnp.bfloat16)
    return jnp.dot(hi, m, preferred_element_type=F32) + jnp.dot(lo, m, preferred_element_type=F32)


def _shift_down(x, s):
    return pltpu.roll(x, s, 0)


def _shift_up(x, s):
    return pltpu.roll(x, x.shape[0] - s, 0)


ROWS_A = 4 * WIDTH
ROWS_B = ROWS_A + 2 * HEADS
ROWS_C = ROWS_B + 4 * WIDTH


def _small_rows(w_t):
    return jnp.concatenate([w_t[ROWS_A:ROWS_B], w_t[ROWS_C:], jnp.zeros((LANE - 3 * HEADS, w_t.shape[1]), w_t.dtype)])


def _inproj(x, n1w, w_t, ws_t, carry=None):
    t, d = x.shape
    tb = min(TB, t)
    c_in, c_in_specs, c_out_shape, c_out_specs, c_sem = _carry_operands(*carry) if carry else ([], [], [], [], None)
    n_c = len(c_in)

    def body(*refs):
        x_ref, nw_ref, wt_ref, ws_ref = refs[:4]
        h_ref, pm_ref, ps_ref = refs[4 + n_c:7 + n_c]
        if carry:
            _carry_step(carry[0], refs[4:4 + n_c], refs[7 + n_c], refs[8 + n_c], refs[-1], pl.program_id(0), t // tb)
        xv = x_ref[...]
        r = lax.rsqrt(jnp.mean(xv * xv, axis=-1, keepdims=True) + EPS)
        h = (xv * r * nw_ref[...]).astype(MXU)
        h_ref[...] = h
        pm_ref[:, 0:ROWS_A] = _dot_nt(h, wt_ref[0:ROWS_A, :])
        pm_ref[:, ROWS_A:2 * ROWS_A] = _dot_nt(h, wt_ref[ROWS_B:ROWS_C, :])
        ps_ref[...] = _dot_nt(h, ws_ref[...])

    out = pl.pallas_call(
        body, name="inproj", grid=(t // tb,),
        in_specs=[pl.BlockSpec((tb, d), lambda i: (i, 0)), _resident((1, d)), _resident(w_t.shape), _resident((LANE, d))]
        + c_in_specs,
        out_specs=[pl.BlockSpec((tb, d), lambda i: (i, 0)), pl.BlockSpec((tb, 2 * ROWS_A), lambda i: (i, 0)),
                   pl.BlockSpec((tb, LANE), lambda i: (i, 0))] + c_out_specs,
        out_shape=[jax.ShapeDtypeStruct((t, d), MXU), jax.ShapeDtypeStruct((t, 2 * ROWS_A), F32),
                   jax.ShapeDtypeStruct((t, LANE), F32)] + c_out_shape,
        input_output_aliases={4 + j: 5 + j for j in range(n_c)},
        scratch_shapes=[c_sem] if carry else [],
        compiler_params=_params(("arbitrary",)),
    )(x, n1w, w_t, ws_t, *c_in)
    return (out[0], out[1], out[2], _carry_state(out[3:])) if carry else tuple(out)


def _inproj_bwd(x, n1w, dx2, dgdn, dz, dfox, dfg, dps, w_t, ws_t, carry=None):
    t, d = x.shape
    tb = min(TB, t)
    w3 = 3 * WIDTH
    c_in, c_in_specs, c_out_shape, c_out_specs, c_sem = _carry_operands(*carry) if carry else ([], [], [], [], None)
    n_c = len(c_in)

    def body(*refs):
        x_ref, nw_ref, dx2_ref, dgdn_ref, dz_ref, dfox_ref, dfg_ref, dps_ref, wm_ref, ws_ref = refs[:10]
        gx_ref, dnw_ref = refs[10 + n_c:12 + n_c]
        if carry:
            _carry_step(carry[0], refs[10:10 + n_c], refs[12 + n_c], refs[13 + n_c], refs[-1], pl.program_id(0), t // tb)
        dh = _dot(dgdn_ref[...], wm_ref[0:w3, :])
        dh += _dot(dz_ref[...], wm_ref[w3:ROWS_A, :])
        dh += _dot(dfox_ref[...], wm_ref[ROWS_B:ROWS_B + w3, :])
        dh += _dot(dfg_ref[...], wm_ref[ROWS_B + w3:ROWS_C, :])
        dh += _dot(dps_ref[...], ws_ref[...])
        xv = x_ref[...]
        r = lax.rsqrt(jnp.mean(xv * xv, axis=-1, keepdims=True) + EPS)
        xn = xv * r

        @pl.when(pl.program_id(0) == 0)
        def _():
            dnw_ref[...] = jnp.zeros_like(dnw_ref)

        dnw_ref[...] += jnp.sum(dh * xn, axis=0, keepdims=True)
        g = dh * nw_ref[...]
        gx_ref[0] = dx2_ref[...] + r * (g - xn * jnp.mean(g * xn, axis=-1, keepdims=True))

    def tok(n):
        return pl.BlockSpec((tb, n), lambda i: (i, 0))

    out = pl.pallas_call(
        body, name="inproj_bwd", grid=(t // tb,),
        in_specs=[tok(d), _resident((1, d)), tok(d), tok(w3), tok(WIDTH), tok(w3), tok(WIDTH), tok(LANE),
                  _resident(w_t.shape), _resident(ws_t.shape)] + c_in_specs,
        out_specs=[pl.BlockSpec((1, tb, d), lambda i: (0, i, 0)), pl.BlockSpec((1, d), lambda i: (0, 0))] + c_out_specs,
        out_shape=[jax.ShapeDtypeStruct((1, t, d), F32), jax.ShapeDtypeStruct((1, d), F32)] + c_out_shape,
        input_output_aliases={10 + j: 4 + j for j in range(n_c)},
        scratch_shapes=[c_sem] if carry else [],
        compiler_params=_params(("arbitrary",)),
    )(x, n1w, dx2, dgdn, dz, dfox, dfg, dps, w_t, ws_t, *c_in)
    return (out[0], out[1], _carry_state(out[2:])) if carry else tuple(out)


def _gate_lanes(shape):
    lane = lax.broadcasted_iota(jnp.int32, shape, 1)
    return lane < HEADS, (lane >= HEADS) & (lane < 2 * HEADS), (lane >= 2 * HEADS) & (lane < 3 * HEADS)


def _block_masks():
    ri = lax.broadcasted_iota(jnp.int32, (LANE, LANE), 0)
    ci = lax.broadcasted_iota(jnp.int32, (LANE, LANE), 1)
    same = (ri // CHUNK) == (ci // CHUNK)
    return ((ri >= ci).astype(F32), (ri <= ci).astype(F32), (same & (ri >= ci)).astype(F32),
            (same & (ri <= ci)).astype(F32), same.astype(F32))


def _gates(ps, gparams):
    t = ps.shape[0]
    nb = t // LANE

    def body(ps_ref, gp_ref, out_ref, run_ref, tot_ref, runt_ref):
        p = ps_ref[...]
        is_b, is_a, is_f = _gate_lanes(p.shape)
        z = p + gp_ref[0:1, :]
        neg_exp_a = -jnp.exp(gp_ref[1:2, :])
        glog = neg_exp_a * _softplus(z)
        logf = -_softplus(-z)
        out_ref[...] = jnp.where(is_b, _sigmoid(p), jnp.where(is_a, glog, jnp.where(is_f, logf, 0.0)))
        tril, _, tril_c, _, same_c = _block_masks()
        off = jnp.zeros((1, LANE), F32)
        for b in range(nb):
            rows = slice(b * LANE, (b + 1) * LANE)
            blk = out_ref[rows, :]
            ga = jnp.where(is_a[:LANE], blk, 0.0)
            fb = _hdot(tril, jnp.where(is_f[:LANE], blk, 0.0)) + off
            run = fb + _hdot(tril_c, ga)
            run_ref[rows, :] = run
            runt_ref[:, rows] = run.T
            tot_ref[rows, :] = _hdot(same_c, ga)
            off = fb[LANE - 1:LANE, :]

    return pl.pallas_call(
        body, name="gates",
        out_shape=[jax.ShapeDtypeStruct((t, LANE), F32)] * 3 + [jax.ShapeDtypeStruct((LANE, t), F32)],
        compiler_params=_params(),
    )(ps, gparams)


def _gates_bwd(ps, gparams, dcol_g, dtot_g, dcol_f, drow_g, drow_f):
    t = ps.shape[0]
    nb = t // LANE

    def body(ps_ref, gp_ref, dcg_ref, dtg_ref, dcf_ref, drg_ref, drf_ref, dps_ref, sums_ref, dl_ref, d0_ref, tr_ref):
        p = ps_ref[...]
        is_b, is_a, is_f = _gate_lanes(p.shape)
        _, triu, _, triu_c, same_c = _block_masks()
        tr_ref[...] = jnp.zeros_like(tr_ref)
        off = jnp.zeros((1, LANE), F32)
        for b in reversed(range(nb)):
            rows = slice(b * LANE, (b + 1) * LANE)
            tr_ref[HEADS:2 * HEADS, :] = drg_ref[:, rows]
            tr_ref[2 * HEADS:3 * HEADS, :] = drf_ref[:, rows]
            d = dcg_ref[rows, :] + dcf_ref[rows, :] + tr_ref[...].T
            d0_ref[rows, :] = d
            dlf = _hdot(triu, jnp.where(is_f[:LANE], d, 0.0)) + off
            dla = (_hdot(triu_c, jnp.where(is_a[:LANE], d, 0.0))
                   + _hdot(same_c, jnp.where(is_a[:LANE], dtg_ref[rows, :], 0.0)))
            dl_ref[rows, :] = dlf + dla
            off = dlf[0:1, :]
        z = p + gp_ref[0:1, :]
        neg_exp_a = -jnp.exp(gp_ref[1:2, :])
        sb = _sigmoid(p)
        glog = neg_exp_a * _softplus(z)
        dl = dl_ref[...]
        dp = jnp.where(is_b, d0_ref[...] * sb * (1.0 - sb),
                       jnp.where(is_a, dl * neg_exp_a * _sigmoid(z), jnp.where(is_f, dl * _sigmoid(-z), 0.0)))
        dps_ref[...] = dp
        s_a = jnp.sum(jnp.where(is_a, dl * glog, 0.0), axis=0, keepdims=True)
        s_p = jnp.sum(jnp.where(is_b, 0.0, dp), axis=0, keepdims=True)
        row = lax.broadcasted_iota(jnp.int32, (8, LANE), 0)
        from_a = pltpu.roll(jnp.where(row == 0, s_a, jnp.where(row == 1, s_p, 0.0)), LANE - HEADS, 1)
        from_f = pltpu.roll(jnp.where(row == 2, s_p, 0.0), LANE - 2 * HEADS, 1)
        lane = lax.broadcasted_iota(jnp.int32, (8, LANE), 1)
        sums_ref[...] = jnp.where(lane < HEADS, from_a + from_f, 0.0)

    return pl.pallas_call(
        body, name="gates_bwd",
        out_shape=[jax.ShapeDtypeStruct((t, LANE), F32), jax.ShapeDtypeStruct((8, LANE), F32)],
        scratch_shapes=[pltpu.VMEM((t, LANE), F32), pltpu.VMEM((t, LANE), F32), pltpu.VMEM((LANE, LANE), F32)],
        compiler_params=_params(),
    )(ps, gparams, dcol_g, dtot_g, dcol_f, drow_g, drow_f)


def _conv(xv, w):
    acc = w[3:4, :] * xv
    for s in range(1, 4):
        acc += w[3 - s:4 - s, :] * _shift_down(xv, s)
    return acc


PREP_ROWS = 256
HALO = 8
PREP_UNROLL = 4


def _tile_loop(n, tile, init):
    if n % PREP_UNROLL:
        return lax.fori_loop(0, n, tile, init)

    def trip(g, carry):
        for u in range(PREP_UNROLL):
            carry = tile(g * PREP_UNROLL + u, carry)
        return carry

    return lax.fori_loop(0, n // PREP_UNROLL, trip, init)


def _tile_rows(r):
    return pl.ds(pl.multiple_of(r * PREP_ROWS, PREP_ROWS), PREP_ROWS)


def _gdn_prep(pm, conv_w):
    t = pm.shape[0]
    nj = WIDTH // LANE
    win = PREP_ROWS + HALO

    def body(x_ref, w_ref, o_ref, xp_ref):
        kind = pl.program_id(0)
        xp_ref[0:HALO, :] = jnp.zeros((HALO, LANE), F32)
        xp_ref[HALO:, :] = x_ref[...]
        w = w_ref[...]
        hs = _head_sum_matrix()

        def tile(r, _, normed):
            xw = xp_ref[pl.ds(pl.multiple_of(r * PREP_ROWS, PREP_ROWS), win), :]
            acc = _conv(xw, w)[HALO:]
            out = acc * _sigmoid(acc)
            if normed:
                out = out * lax.rsqrt(_group_sum(out * out, hs) + EPS)
            o_ref[0, 0, _tile_rows(r), :] = out[:, :DH]
            o_ref[0, 1, _tile_rows(r), :] = out[:, DH:]
            return 0

        @pl.when(kind < 2)
        def _():
            _tile_loop(t // PREP_ROWS, functools.partial(tile, normed=True), 0)

        @pl.when(kind == 2)
        def _():
            _tile_loop(t // PREP_ROWS, functools.partial(tile, normed=False), 0)

    return pl.pallas_call(
        body, name="gdn_prep", grid=(3, nj),
        in_specs=[pl.BlockSpec((t, LANE), lambda i, j: (0, i * nj + j)),
                  pl.BlockSpec((4, LANE), lambda i, j: (0, i * nj + j))],
        out_specs=pl.BlockSpec((1, 2, t, DH), lambda i, j: (i, j, 0, 0)),
        out_shape=jax.ShapeDtypeStruct((3, HEADS, t, DH), F32),
        scratch_shapes=[pltpu.VMEM((t + HALO, LANE), F32)],
        compiler_params=_params(("arbitrary", "arbitrary")),
    )(pm, conv_w)


def _gdn_prep_bwd(pm, conv_w, dqkv):
    t = pm.shape[0]
    nj = WIDTH // LANE
    win = PREP_ROWS + 2 * HALO

    def body(x_ref, w_ref, d_ref, dx_ref, dw_ref, xp_ref, dp_ref):
        kind = pl.program_id(0)
        zeros = jnp.zeros((HALO, LANE), F32)
        for ref in (xp_ref, dp_ref):
            ref[0:HALO, :] = zeros
            ref[HALO + t:, :] = zeros
        xp_ref[HALO:HALO + t, :] = x_ref[...]
        dp_ref[HALO:HALO + t, 0:DH] = d_ref[0, 0]
        dp_ref[HALO:HALO + t, DH:] = d_ref[0, 1]
        w = w_ref[...]
        hs = _head_sum_matrix()
        rows = lax.broadcasted_iota(jnp.int32, (win, LANE), 0)
        in_tile = (rows >= HALO) & (rows < HALO + PREP_ROWS)

        def tile(r, dw, normed):
            start = pl.multiple_of(r * PREP_ROWS, PREP_ROWS)
            xw = xp_ref[pl.ds(start, win), :]
            dy = dp_ref[pl.ds(start, win), :]
            acc = _conv(xw, w)
            sg = _sigmoid(acc)
            if normed:
                y = acc * sg
                rn = lax.rsqrt(_group_sum(y * y, hs) + EPS)
                yn = y * rn
                dy = rn * (dy - yn * _group_sum(dy * yn, hs))
            dacc = dy * sg * (1.0 + acc * (1.0 - sg))
            dx = w[3:4, :] * dacc
            for s in range(1, 4):
                dx += w[3 - s:4 - s, :] * _shift_up(dacc, s)
            dx_ref[_tile_rows(r), :] = dx[HALO:HALO + PREP_ROWS].astype(dx_ref.dtype)
            dm = jnp.where(in_tile, dacc, 0.0)
            return tuple(dw[i] + jnp.sum(dm * (xw if i == 3 else _shift_down(xw, 3 - i)), axis=0, keepdims=True)
                         for i in range(4))

        def run(normed):
            dw = _tile_loop(t // PREP_ROWS, functools.partial(tile, normed=normed),
                            tuple(jnp.zeros((1, LANE), F32) for _ in range(4)))
            for i in range(4):
                dw_ref[i:i + 1, :] = dw[i]

        pl.when(kind < 2)(functools.partial(run, True))
        pl.when(kind == 2)(functools.partial(run, False))

    return pl.pallas_call(
        body, name="gdn_prep_bwd", grid=(3, nj),
        in_specs=[pl.BlockSpec((t, LANE), lambda i, j: (0, i * nj + j)),
                  pl.BlockSpec((4, LANE), lambda i, j: (0, i * nj + j)),
                  pl.BlockSpec((1, 2, t, DH), lambda i, j: (i, j, 0, 0))],
        out_specs=[pl.BlockSpec((t, LANE), lambda i, j: (0, i * nj + j)),
                   pl.BlockSpec((4, LANE), lambda i, j: (0, i * nj + j))],
        out_shape=[jax.ShapeDtypeStruct((t, 3 * WIDTH), MXU), jax.ShapeDtypeStruct((4, 3 * WIDTH), F32)],
        scratch_shapes=[pltpu.VMEM((t + 2 * HALO, LANE), F32), pltpu.VMEM((t + 2 * HALO, LANE), F32)],
        compiler_params=_params(("arbitrary", "arbitrary")),
    )(pm, conv_w, dqkv)


FOX_COL0 = 4 * WIDTH // LANE


def _fox_prep(pm, nw):
    t = pm.shape[0]
    nj = WIDTH // LANE

    def body(x_ref, w_ref, o_ref):
        kind = pl.program_id(0)
        hs = _head_sum_matrix()
        wk = w_ref[pl.ds(kind, 1), :]

        def tile(r, _, normed):
            out = x_ref[_tile_rows(r), :]
            if normed:
                out = out * lax.rsqrt(_group_sum(out * out, hs) * (1.0 / DH) + EPS) * wk
            o_ref[0, 0, _tile_rows(r), :] = out[:, :DH]
            o_ref[0, 1, _tile_rows(r), :] = out[:, DH:]
            return 0

        @pl.when(kind < 2)
        def _():
            _tile_loop(t // PREP_ROWS, functools.partial(tile, normed=True), 0)

        @pl.when(kind == 2)
        def _():
            _tile_loop(t // PREP_ROWS, functools.partial(tile, normed=False), 0)

    return pl.pallas_call(
        body, name="fox_prep", grid=(3, nj),
        in_specs=[pl.BlockSpec((t, LANE), lambda i, j: (0, FOX_COL0 + i * nj + j)),
                  pl.BlockSpec((3, LANE), lambda i, j: (0, 0))],
        out_specs=pl.BlockSpec((1, 2, t, DH), lambda i, j: (i, j, 0, 0)),
        out_shape=jax.ShapeDtypeStruct((3, HEADS, t, DH), F32),
        compiler_params=_params(("arbitrary", "arbitrary")),
    )(pm, nw)


def _fox_prep_bwd(pm, nw, dqkv):
    t = pm.shape[0]
    nj = WIDTH // LANE

    def body(x_ref, w_ref, d_ref, dx_ref, dw_ref):
        kind = pl.program_id(0)
        hs = _head_sum_matrix()
        wk = w_ref[pl.ds(kind, 1), :]

        def tile(r, dw):
            xv = x_ref[_tile_rows(r), :]
            rn = lax.rsqrt(_group_sum(xv * xv, hs) * (1.0 / DH) + EPS)
            xn = xv * rn
            d = jnp.concatenate([d_ref[0, 0, _tile_rows(r), :], d_ref[0, 1, _tile_rows(r), :]], axis=1)
            g = d * wk
            dx_ref[_tile_rows(r), :] = (rn * (g - xn * _group_sum(g * xn, hs) * (1.0 / DH))).astype(dx_ref.dtype)
            return dw + jnp.sum(d * xn, axis=0, keepdims=True)

        def copy_tile(r, _):
            dx_ref[_tile_rows(r), :] = jnp.concatenate([d_ref[0, 0, _tile_rows(r), :], d_ref[0, 1, _tile_rows(r), :]],
                                                       axis=1).astype(dx_ref.dtype)
            return 0

        @pl.when(kind < 2)
        def _():
            dw_ref[0, 0] = _tile_loop(t // PREP_ROWS, tile, jnp.zeros((1, LANE), F32))

        @pl.when(kind == 2)
        def _():
            _tile_loop(t // PREP_ROWS, copy_tile, 0)
            dw_ref[0, 0] = jnp.zeros((1, LANE), F32)

    return pl.pallas_call(
        body, name="fox_prep_bwd", grid=(3, nj),
        in_specs=[pl.BlockSpec((t, LANE), lambda i, j: (0, FOX_COL0 + i * nj + j)),
                  pl.BlockSpec((3, LANE), lambda i, j: (0, 0)),
                  pl.BlockSpec((1, 2, t, DH), lambda i, j: (i, j, 0, 0))],
        out_specs=[pl.BlockSpec((t, LANE), lambda i, j: (0, i * nj + j)),
                   pl.BlockSpec((1, 1, 1, LANE), lambda i, j: (i, j, 0, 0))],
        out_shape=[jax.ShapeDtypeStruct((t, 3 * WIDTH), MXU), jax.ShapeDtypeStruct((3, nj, 1, LANE), F32)],
        compiler_params=_params(("arbitrary", "arbitrary")),
    )(pm, nw, dqkv)


SC = 256
CPS = SC // CHUNK
GDN_HP_FWD = 4
GDN_HP_BWD = 2
Q_SCALE = DH ** -0.5


def _sc_masks():
    ri = lax.broadcasted_iota(jnp.int32, (SC, SC), 0)
    ci = lax.broadcasted_iota(jnp.int32, (SC, SC), 1)
    same = (ri // CHUNK) == (ci // CHUNK)
    return same & (ri >= ci), same & (ri > ci), ri == ci


def _unit_lower_inverses(ms, eye):
    invs = [jnp.where(eye, 1.0, 0.0) + m for m in ms]
    ms = [_dot(m, m) for m in ms]
    for _ in range(4):
        both = [_dot(jnp.concatenate([inv, m], axis=0), m) for inv, m in zip(invs, ms)]
        invs = [inv + b[:SC] for inv, b in zip(invs, both)]
        ms = [b[SC:] for b in both]
    return [inv + _dot(inv, m) for inv, m in zip(invs, ms)]


def _lane_col(blk, lane_idx):
    lane = lax.broadcasted_iota(jnp.int32, blk.shape, 1)
    return jnp.sum(jnp.where(lane == lane_idx, blk, 0.0), axis=1, keepdims=True)


def _to_lane(col, lane_idx):
    lane = lax.broadcasted_iota(jnp.int32, (col.shape[0], LANE), 1)
    return jnp.where(lane == lane_idx, col, 0.0)


def _gdn_columns(gates_ref, run_ref, tot_ref, runt_ref, rows, h):
    return (_lane_col(gates_ref[rows, :], h), _lane_col(run_ref[rows, :], HEADS + h),
            _lane_col(tot_ref[rows, :], HEADS + h), runt_ref[pl.ds(h, 1), rows])


def _gdn_local(q, k, beta, gc, gl, grow, causal, with_kk=True):
    decay = jnp.exp(jnp.where(causal, gc - grow, -1e30))
    egc = jnp.exp(gc)
    ekd = jnp.exp(gl - gc)
    qs = q * Q_SCALE
    kb = k * beta
    if with_kk:
        both = _dot_nt(jnp.concatenate([kb, qs], axis=0), k)
        kk, qk = both[:SC], both[SC:]
    else:
        kk, qk = None, _dot_nt(qs, k)
    return beta, gl, decay, egc, ekd, qs, kb, kk, qk, jnp.where(causal, qk * decay, 0.0)


def _chunk_rows(c):
    return pl.ds(c * CHUNK if isinstance(c, int) else pl.multiple_of(c * CHUNK, CHUNK), CHUNK)


def _sc_rows(b):
    return pl.ds(b * SC if isinstance(b, int) else pl.multiple_of(b * SC, SC), SC)


def _gdn_fwd(qkv, gates, run, tot, run_t):
    t = qkv.shape[2]
    nc = t // CHUNK
    nsc = t // SC

    hp_n = GDN_HP_FWD
    heads = range(hp_n)

    def body(qkv_ref, gates_ref, run_ref, tot_ref, runt_ref, o_ref, st_ref, inv_ref, kc_s, qc_s, g_s, au_s):
        hp = pl.program_id(0)
        causal, strict, eye = _sc_masks()

        def local(b, _):
            rows = _sc_rows(b)
            loc = [_gdn_local(qkv_ref[0, hh, rows, :], qkv_ref[1, hh, rows, :],
                              *_gdn_columns(gates_ref, run_ref, tot_ref, runt_ref, rows, hp * hp_n + hh), causal)
                   for hh in heads]
            invs = _unit_lower_inverses([-jnp.where(strict, l[7] * l[2], 0.0) for l in loc], eye)
            uws = []
            for hh in heads:
                beta, _, _, egc, _, _, kb, _, _, _ = loc[hh]
                inv_ref[hh, rows, :] = invs[hh].astype(inv_ref.dtype)
                uws.append(_dot(invs[hh], jnp.concatenate([qkv_ref[2, hh, rows, :] * beta, kb * egc], axis=1)))
            for hh in heads:
                _, _, _, egc, ekd, qs, _, _, _, attn = loc[hh]
                auw = _dot(attn, uws[hh])
                g_s[hh, rows, :] = qs * egc - auw[:, DH:]
                au_s[hh, rows, :] = auw[:, :DH]
                kd = qkv_ref[1, hh, rows, :] * ekd
                for j in range(CPS):
                    sl = slice(j * CHUNK, (j + 1) * CHUNK)
                    both = _dot_tn(kd[sl], uws[hh][sl])
                    kc_s[hh, b * CPS + j] = both[:, DH:]
                    qc_s[hh, b * CPS + j] = both[:, :DH]
            return 0

        def step(c, states):
            rows = _chunk_rows(c)
            tot_row = tot_ref[pl.ds(c * CHUNK, 1), :]
            new = []
            for hh in heads:
                s = states[hh]
                st_ref[hh, c] = s
                o_ref[hh, rows, :] = _dot(g_s[hh, rows, :], s) + au_s[hh, rows, :]
                egl = jnp.exp(_lane_col(tot_row, HEADS + hp * hp_n + hh))
                new.append(egl * s - _dot(kc_s[hh, c], s) + qc_s[hh, c])
            return tuple(new)

        def steps_of(b, states):
            for j in range(CPS):
                states = step(b * CPS + j, states)
            return states

        def fused(b, states):
            states = steps_of(b - 1, states)
            local(b, 0)
            return states

        local(0, 0)
        states = lax.fori_loop(1, nsc, fused, tuple(jnp.zeros((DH, DH), F32) for _ in heads))
        steps_of(nsc - 1, states)

    whole = pl.BlockSpec((t, LANE), lambda h: (0, 0))
    once = dict(pipeline_mode=pl.Buffered(1))
    return pl.pallas_call(
        body, name="gdn_fwd", grid=(HEADS // hp_n,),
        in_specs=[pl.BlockSpec((3, hp_n, t, DH), lambda h: (0, h, 0, 0), **once), whole, whole, whole,
                  pl.BlockSpec((HEADS, t), lambda h: (1, 0))],
        out_specs=[pl.BlockSpec((hp_n, t, DH), lambda h: (h, 0, 0), **once),
                   pl.BlockSpec((hp_n, nc, DH, DH), lambda h: (h, 0, 0, 0), **once),
                   pl.BlockSpec((hp_n, t, SC), lambda h: (h, 0, 0), **once)],
        out_shape=[jax.ShapeDtypeStruct((HEADS, t, DH), F32), jax.ShapeDtypeStruct((HEADS, nc, DH, DH), F32),
                   jax.ShapeDtypeStruct((HEADS, t, SC), MXU)],
        scratch_shapes=[pltpu.VMEM((hp_n, nc, DH, DH), F32), pltpu.VMEM((hp_n, nc, DH, DH), F32),
                        pltpu.VMEM((hp_n, t, DH), F32), pltpu.VMEM((hp_n, t, DH), F32)],
        compiler_params=_params(("arbitrary",)),
    )(qkv, gates, run, tot, run_t)


def _gdn_bwd(qkv, gates, run, tot, run_t, inv, states, do):
    t = qkv.shape[2]
    nc = t // CHUNK
    nsc = t // SC

    hp_n = GDN_HP_BWD
    heads = range(hp_n)

    def body(qkv_ref, gates_ref, run_ref, tot_ref, runt_ref, inv_ref, st_ref, do_ref,
             dqkv_ref, dcol_ref, dtot_ref, drow_ref, uw_s, kc_s, h_s, dsn_s):
        hp = pl.program_id(0)
        causal, strict, _ = _sc_masks()

        @pl.when(hp == 0)
        def _():
            dcol_ref[...] = jnp.zeros_like(dcol_ref)
            dtot_ref[...] = jnp.zeros_like(dtot_ref)

        def local_of(hh, rows, with_kk=True):
            return _gdn_local(qkv_ref[0, hh, rows, :], qkv_ref[1, hh, rows, :],
                              *_gdn_columns(gates_ref, run_ref, tot_ref, runt_ref, rows, hp * hp_n + hh), causal, with_kk)

        def local(b, _):
            rows = _sc_rows(b)
            loc = [local_of(hh, rows, with_kk=False) for hh in heads]
            us, ws = [], []
            for hh in heads:
                beta, _, _, egc, _, _, kb, _, _, _ = loc[hh]
                inv_b = inv_ref[hh, rows, :]
                us.append(_dot(inv_b, qkv_ref[2, hh, rows, :] * beta))
                ws.append(_dot(inv_b, kb * egc))
            gs = [loc[hh][5] * loc[hh][3] - _dot(loc[hh][9], ws[hh]) for hh in heads]
            for hh in heads:
                uw_s[hh, rows, :] = jnp.concatenate([us[hh], ws[hh]], axis=1)
                kd = qkv_ref[1, hh, rows, :] * loc[hh][4]
                dout = do_ref[hh, rows, :]
                for j in range(CPS):
                    sl = slice(j * CHUNK, (j + 1) * CHUNK)
                    kc_s[hh, b * CPS + j] = _dot_tn(kd[sl], ws[hh][sl])
                    h_s[hh, b * CPS + j] = _dot_tn(gs[hh][sl], dout[sl])
            return 0

        def step(c, dss):
            tot_row = tot_ref[pl.ds(c * CHUNK, 1), :]
            new = []
            for hh in heads:
                ds = dss[hh]
                dsn_s[hh, c] = ds
                egl = jnp.exp(_lane_col(tot_row, HEADS + hp * hp_n + hh))
                new.append(egl * ds - _dot_tn(kc_s[hh, c], ds) + h_s[hh, c])
            return tuple(new)

        def steps_of(b, dss):
            for j in reversed(range(CPS)):
                dss = step(b * CPS + j, dss)
            return dss

        def fused(i, dss):
            b = nsc - 2 - i
            dss = steps_of(b + 1, dss)
            local(b, 0)
            return dss

        local(nsc - 1, 0)
        dss = lax.fori_loop(0, nsc - 1, fused, tuple(jnp.zeros((DH, DH), F32) for _ in heads))
        steps_of(0, dss)

        def back(b, _):
            rows = _sc_rows(b)
            first = lax.broadcasted_iota(jnp.int32, (CHUNK, 1), 0) == 0
            loc = [local_of(hh, rows) for hh in heads]
            sign = jnp.where(lax.broadcasted_iota(jnp.int32, (1, LANE), 1) < DH, 1.0, -1.0)
            mid = []
            for hh in heads:
                beta, gl, decay, egc, ekd, qs, kb, kk, qk, attn = loc[hh]
                uw = uw_s[hh, rows, :]
                kd = qkv_ref[1, hh, rows, :] * ekd
                dout = do_ref[hh, rows, :]
                dg_p, dkd_p, duw_p, dgl_p = [], [], [], []
                for j in range(CPS):
                    sl = slice(j * CHUNK, (j + 1) * CHUNK)
                    s = st_ref[hh, b * CPS + j]
                    dsn = dsn_s[hh, b * CPS + j]
                    both = _dot_nt(jnp.concatenate([dsn, dout[sl]], axis=0), s)
                    dg_p.append(both[CHUNK:])
                    dsc = jnp.concatenate([dsn, -both[:CHUNK]], axis=1)
                    dkd_p.append(_dot_nt(uw[sl], dsc))
                    duw_p.append(_dot(kd[sl], dsc))
                    degl = jnp.sum(jnp.sum(s * dsn, axis=1, keepdims=True), axis=0, keepdims=True)
                    dgl_p.append(jnp.where(first, degl * jnp.exp(gl[j * CHUNK:j * CHUNK + 1, :]), 0.0))
                dg, dkd = jnp.concatenate(dg_p, axis=0), jnp.concatenate(dkd_p, axis=0)
                dod = jnp.concatenate([dout, dg], axis=1)
                da = jnp.where(causal, _dot_nt(dod, uw * sign), 0.0)
                duw = jnp.concatenate(duw_p, axis=0) + _dot_tn(attn, dod) * sign
                mid.append((kd, dg, dkd, da, duw, jnp.concatenate(dgl_p, axis=0)))
            inv_ts = [inv_ref[hh, rows, :].astype(F32).T for hh in heads]
            its = [_dot(inv_ts[hh], mid[hh][4]) for hh in heads]
            dinvs = []
            for hh in heads:
                beta, _, _, egc, _, _, kb, _, _, _ = loc[hh]
                dinvs.append(_dot_nt(mid[hh][4], jnp.concatenate([qkv_ref[2, hh, rows, :] * beta, kb * egc], axis=1)))
            half = [_dot(inv_ts[hh], dinvs[hh]) for hh in heads]
            dls = [jnp.where(strict, -_dot(half[hh], inv_ts[hh]), 0.0) for hh in heads]
            for hh in heads:
                head = hp * hp_n + hh
                beta, gl, decay, egc, ekd, qs, kb, kk, qk, attn = loc[hh]
                kd, dg, dkd, da, _, dgl_first = mid[hh]
                k, v = qkv_ref[1, hh, rows, :], qkv_ref[2, hh, rows, :]
                dvb, dkbe = its[hh][:, :DH], its[hh][:, DH:]
                dl = dls[hh]
                dlogd = (dl * kk + da * qk) * decay
                dd = jnp.concatenate([dl * decay, da * decay], axis=0)
                ddk = _dot(dd, k)
                dkb = ddk[:SC] + dkbe * egc
                dqs = ddk[SC:] + dg * egc
                dk = _dot_tn(dd, jnp.concatenate([kb, qs], axis=0)) + dkd * ekd + dkb * beta
                dkd_kd = jnp.sum(dkd * kd, axis=1, keepdims=True)
                narrow = dg * (qs * egc) + dkbe * (kb * egc)
                wide = dlogd[:, :LANE] + dlogd[:, LANE:] + jnp.concatenate([narrow, jnp.zeros((SC, LANE - DH), F32)], axis=1)
                dgc = jnp.sum(wide, axis=1, keepdims=True) - dkd_kd
                dbeta = jnp.sum(dkb * k + dvb * v, axis=1, keepdims=True)
                dqkv_ref[0, hh, rows, :] = dqs * Q_SCALE
                dqkv_ref[1, hh, rows, :] = dk
                dqkv_ref[2, hh, rows, :] = dvb * beta
                dcol_ref[rows, :] += _to_lane(dbeta, head) + _to_lane(dgc, HEADS + head)
                dtot_ref[rows, :] += _to_lane(dkd_kd + dgl_first, HEADS + head)
                drow_ref[pl.ds(head, 1), rows] = -jnp.sum(dlogd, axis=0, keepdims=True)
            return 0

        lax.fori_loop(0, nsc, back, 0)

    whole = pl.BlockSpec((t, LANE), lambda h: (0, 0))
    rowspec = pl.BlockSpec((HEADS, t), lambda h: (0, 0))
    sq = pltpu.VMEM((hp_n, nc, DH, DH), F32)
    per_head = pltpu.VMEM((hp_n, t, 2 * DH), F32)
    return pl.pallas_call(
        body, name="gdn_bwd", grid=(HEADS // hp_n,),
        in_specs=[pl.BlockSpec((3, hp_n, t, DH), lambda h: (0, h, 0, 0)), whole, whole, whole,
                  pl.BlockSpec((HEADS, t), lambda h: (1, 0)),
                  pl.BlockSpec((hp_n, t, SC), lambda h: (h, 0, 0)), pl.BlockSpec((hp_n, nc, DH, DH), lambda h: (h, 0, 0, 0)),
                  pl.BlockSpec((hp_n, t, DH), lambda h: (h, 0, 0))],
        out_specs=[pl.BlockSpec((3, hp_n, t, DH), lambda h: (0, h, 0, 0)), whole, whole, rowspec],
        out_shape=[jax.ShapeDtypeStruct((3, HEADS, t, DH), F32), jax.ShapeDtypeStruct((t, LANE), F32),
                   jax.ShapeDtypeStruct((t, LANE), F32), jax.ShapeDtypeStruct((HEADS, t), F32)],
        scratch_shapes=[per_head, sq, sq, sq],
        compiler_params=_params(("arbitrary",)),
    )(qkv, gates, run, tot, run_t, inv, states, do)


FOX_HP_FWD = 4
FOX_HP_BWD = 4


def _wide_t(a):
    r = a.shape[0]
    return jnp.concatenate([a, jnp.zeros((r, LANE - DH), F32)], axis=1).T[:DH]


def _tall_t(a):
    r = a.shape[1]
    return jnp.concatenate([a, jnp.zeros((LANE - DH, r), F32)], axis=0).T[:, :DH]


def _key_side_f(run_blk, head, qb):
    col = jnp.broadcast_to(_lane_col(run_blk, 2 * HEADS + head), (run_blk.shape[0], LANE))
    return jnp.concatenate([col] * (qb // LANE), axis=1)


def _diag_mask(qb):
    return lax.broadcasted_iota(jnp.int32, (qb, qb), 0) <= lax.broadcasted_iota(jnp.int32, (qb, qb), 1)


def _fox_fwd(qkv, run, run_t, carry=None):
    t = qkv.shape[2]
    qb = min(QB, t)
    nq = t // qb
    hp_n = FOX_HP_FWD
    c_in, c_in_specs, c_out_shape, c_out_specs, c_sem = _carry_operands(*carry) if carry else ([], [], [], [], None)
    n_c = len(c_in)

    def body(*refs):
        q_ref, k_ref, v_ref, run_ref, runt_ref = refs[:5]
        o_ref, lse_ref = refs[5 + n_c:7 + n_c]
        s0 = 7 + n_c + len(c_out_shape)
        kb_s, vt_s, raw_s, m_s, l_s, acc_s = refs[s0:s0 + 6]
        hp = pl.program_id(0)
        i = pl.program_id(1)

        if carry:
            _carry_step(carry[0], refs[5:5 + n_c], refs[7 + n_c], refs[8 + n_c], refs[-1], hp * nq + i,
                        (HEADS // hp_n) * nq)

        @pl.when(i == 0)
        def _():
            for hh in range(hp_n):
                kb_s[hh] = k_ref[0, hh].astype(MXU)
                for b in range(nq):
                    rows = slice(b * qb, (b + 1) * qb)
                    vt_s[hh, :, rows] = _wide_t(v_ref[0, hh, rows, :]).astype(MXU)

        qrows = pl.ds(pl.multiple_of(i * qb, qb), qb)
        qs = [(q_ref[0, hh] * Q_SCALE).astype(MXU) for hh in range(hp_n)]
        fq = [runt_ref[pl.ds(hp * hp_n + hh, 1), qrows] for hh in range(hp_n)]

        def block_rows(j):
            return pl.ds(pl.multiple_of(j * qb, qb), qb)

        def scores(j):
            for hh in range(hp_n):
                raw_s[j % 2, hh] = _dot_nt(kb_s[hh, block_rows(j), :], qs[hh])

        def absorb(j, diagonal):
            rows = block_rows(j)
            run_blk = run_ref[rows, :]
            stats, pv = [], []
            for hh in range(hp_n):
                m, l = m_s[hh], l_s[hh]
                st = raw_s[j % 2, hh] + (fq[hh] - _key_side_f(run_blk, hp * hp_n + hh, qb))
                if diagonal:
                    st = jnp.where(_diag_mask(qb), st, -1e30)
                m_new = jnp.maximum(m, jnp.max(st, axis=0, keepdims=True))
                p = jnp.exp(st - m_new)
                alpha = jnp.exp(m - m_new)
                stats.append((m_new, alpha * l + jnp.sum(p, axis=0, keepdims=True), alpha))
                pv.append(_dot(vt_s[hh, :, rows], p))
            for hh in range(hp_n):
                m_s[hh], l_s[hh] = stats[hh][0], stats[hh][1]
                acc_s[hh] = stats[hh][2] * acc_s[hh] + pv[hh]

        def kstep(j, _):
            scores(j + 1)
            absorb(j, False)
            return 0

        for hh in range(hp_n):
            m_s[hh] = jnp.full((1, qb), -1e30, F32)
            l_s[hh] = jnp.zeros((1, qb), F32)
            acc_s[hh] = jnp.zeros((DH, qb), F32)
        scores(0)
        lax.fori_loop(0, i, kstep, 0)
        absorb(i, True)
        for hh in range(hp_n):
            l = l_s[hh]
            o_ref[hh] = _tall_t(acc_s[hh] / l)
            lse_ref[pl.ds(hp * hp_n + hh, 1), qrows] = m_s[hh] + jnp.log(l)

    out = pl.pallas_call(
        body, name="fox_fwd", grid=(HEADS // hp_n, nq),
        in_specs=[pl.BlockSpec((1, hp_n, qb, DH), lambda h, i: (0, h, i, 0)),
                  pl.BlockSpec((1, hp_n, t, DH), lambda h, i: (1, h, 0, 0)),
                  pl.BlockSpec((1, hp_n, t, DH), lambda h, i: (2, h, 0, 0)),
                  pl.BlockSpec((t, LANE), lambda h, i: (0, 0)),
                  pl.BlockSpec((HEADS, t), lambda h, i: (2, 0))] + c_in_specs,
        out_specs=[pl.BlockSpec((hp_n, qb, DH), lambda h, i: (h, i, 0)),
                   pl.BlockSpec((HEADS, t), lambda h, i: (0, 0))] + c_out_specs,
        out_shape=[jax.ShapeDtypeStruct((HEADS, t, DH), F32), jax.ShapeDtypeStruct((HEADS, t), F32)] + c_out_shape,
        input_output_aliases={5 + j: 4 + j for j in range(n_c)},
        scratch_shapes=[pltpu.VMEM((hp_n, t, DH), MXU), pltpu.VMEM((hp_n, DH, t), MXU), pltpu.VMEM((2, hp_n, qb, qb), F32),
                        pltpu.VMEM((hp_n, 1, qb), F32), pltpu.VMEM((hp_n, 1, qb), F32), pltpu.VMEM((hp_n, DH, qb), F32)]
        + ([c_sem] if carry else []),
        compiler_params=_params(("arbitrary", "arbitrary")),
    )(qkv, qkv, qkv, run, run_t, *c_in)
    return (out[0], out[1], _carry_state(out[2:])) if carry else (out[0], out[1])


def _fox_bwd(qkv, run, run_t, o, lse, do, carry=None):
    t = qkv.shape[2]
    qb = min(QB, t)
    nq = t // qb
    hp_n = FOX_HP_BWD
    c_in, c_in_specs, c_out_shape, c_out_specs, c_sem = _carry_operands(*carry) if carry else ([], [], [], [], None)
    n_c = len(c_in)

    def body(*refs):
        q_ref, k_ref, v_ref, run_ref, runt_ref, o_ref, lse_ref, do_ref = refs[:8]
        dqkv_ref, dcol_ref, drow_ref = refs[8 + n_c:11 + n_c]
        s0 = 11 + n_c + len(c_out_shape)
        dqt_s, raw_s, dpt_s, dro_s, dk_s, dv_s, dsum_s = refs[s0:s0 + 7]
        hp = pl.program_id(0)
        j = pl.program_id(1)

        if carry:
            _carry_step(carry[0], refs[8:8 + n_c], refs[11 + n_c], refs[12 + n_c], refs[-1], hp * nq + j,
                        (HEADS // hp_n) * nq)

        @pl.when(j == 0)
        def _():
            dqt_s[...] = jnp.zeros_like(dqt_s)

        @pl.when((j == 0) & (hp == 0))
        def _():
            dcol_ref[...] = jnp.zeros_like(dcol_ref)
            drow_ref[...] = jnp.zeros_like(drow_ref)

        krows = pl.ds(pl.multiple_of(j * qb, qb), qb)
        run_blk = run_ref[krows, :]
        ones8 = jnp.ones((8, DH), MXU)
        kb, kt, vb, fk = [], [], [], []
        for hh in range(hp_n):
            kf = k_ref[0, hh]
            kb.append(kf.astype(MXU))
            kt.append(_wide_t(kf).astype(MXU))
            vb.append(v_ref[0, hh].astype(MXU))
            fk.append(_key_side_f(run_blk, hp * hp_n + hh, qb))

        def block_rows(i):
            return pl.ds(i * qb if isinstance(i, int) else pl.multiple_of(i * qb, qb), qb)

        def products(i):
            rows = block_rows(i)
            slot = i % 2
            for hh in range(hp_n):
                dout = do_ref[hh, rows, :]
                x = dout * o_ref[hh, rows, :]
                x_hi = x.astype(MXU)
                raw_s[slot, hh] = _dot_nt(kb[hh], q_ref[0, hh, rows, :] * Q_SCALE)
                dpt_s[slot, hh] = _dot_nt(vb[hh], dout)
                dro_s[slot, hh] = _dot_nt(ones8, x_hi) + _dot_nt(ones8, x - x_hi.astype(F32))

        def absorb(i, diagonal):
            rows = block_rows(i)
            slot = i % 2
            pieces = []
            for hh in range(hp_n):
                head = hp * hp_n + hh
                raw, dpt, drow = raw_s[slot, hh], dpt_s[slot, hh], dro_s[slot, hh, 0:1, :]
                off = runt_ref[pl.ds(head, 1), rows] - lse_ref[pl.ds(head, 1), rows]
                st = raw + (off - fk[hh])
                if diagonal:
                    st = jnp.where(_diag_mask(qb), st, -1e30)
                pt = jnp.exp(st)
                dst = pt * (dpt - drow)
                drow_ref[pl.ds(head, 1), rows] += jnp.sum(dst, axis=0, keepdims=True)
                folded = dst[:, 0:LANE]
                for c in range(1, qb // LANE):
                    folded = folded + dst[:, c * LANE:(c + 1) * LANE]
                pieces.append((_dot(dst, q_ref[0, hh, rows, :] * Q_SCALE), _dot(pt, do_ref[hh, rows, :]),
                               _dot(kt[hh], dst), folded))
            for hh in range(hp_n):
                dqt_s[hh, :, rows] += pieces[hh][2]
                if diagonal:
                    dk_s[hh], dv_s[hh], dsum_s[hh] = pieces[hh][0], pieces[hh][1], pieces[hh][3]
                else:
                    dk_s[hh] += pieces[hh][0]
                    dv_s[hh] += pieces[hh][1]
                    dsum_s[hh] += pieces[hh][3]

        def qstep(i, _):
            products(i + 1)
            absorb(i, False)
            return 0

        products(j)
        products(jnp.minimum(j + 1, nq - 1))
        absorb(j, True)
        lax.fori_loop(j + 1, nq - 1, qstep, 0)

        @pl.when(j < nq - 1)
        def _():
            absorb(nq - 1, False)
        for hh in range(hp_n):
            dqkv_ref[1, hh, krows, :] = dk_s[hh]
            dqkv_ref[2, hh, krows, :] = dv_s[hh]
            dcol_ref[krows, :] += _to_lane(-jnp.sum(dsum_s[hh], axis=1, keepdims=True), 2 * HEADS + hp * hp_n + hh)

        @pl.when(j == nq - 1)
        def _():
            for hh in range(hp_n):
                for b in range(nq):
                    rows = slice(b * qb, (b + 1) * qb)
                    dqkv_ref[0, hh, rows, :] = _tall_t(dqt_s[hh, :, rows]) * Q_SCALE

    once = dict(pipeline_mode=pl.Buffered(1))
    full = pl.BlockSpec((hp_n, t, DH), lambda h, j: (h, 0, 0), **once)
    rows8 = pl.BlockSpec((HEADS, t), lambda h, j: (0, 0))
    out = pl.pallas_call(
        body, name="fox_bwd", grid=(HEADS // hp_n, nq),
        in_specs=[pl.BlockSpec((1, hp_n, t, DH), lambda h, j: (0, h, 0, 0), **once),
                  pl.BlockSpec((1, hp_n, qb, DH), lambda h, j: (1, h, j, 0)),
                  pl.BlockSpec((1, hp_n, qb, DH), lambda h, j: (2, h, j, 0)),
                  pl.BlockSpec((t, LANE), lambda h, j: (0, 0), **once), pl.BlockSpec((HEADS, t), lambda h, j: (2, 0)),
                  full, rows8, full] + c_in_specs,
        out_specs=[pl.BlockSpec((3, hp_n, t, DH), lambda h, j: (0, h, 0, 0), **once),
                   pl.BlockSpec((t, LANE), lambda h, j: (0, 0)), rows8] + c_out_specs,
        out_shape=[jax.ShapeDtypeStruct((3, HEADS, t, DH), F32), jax.ShapeDtypeStruct((t, LANE), F32),
                   jax.ShapeDtypeStruct((HEADS, t), F32)] + c_out_shape,
        input_output_aliases={8 + j: 5 + j for j in range(n_c)},
        scratch_shapes=[pltpu.VMEM((hp_n, DH, t), F32), pltpu.VMEM((2, hp_n, qb, qb), F32), pltpu.VMEM((2, hp_n, qb, qb), F32),
                        pltpu.VMEM((2, hp_n, 8, qb), F32), pltpu.VMEM((hp_n, qb, DH), F32), pltpu.VMEM((hp_n, qb, DH), F32),
                        pltpu.VMEM((hp_n, qb, LANE), F32)] + ([c_sem] if carry else []),
        compiler_params=_params(("arbitrary", "arbitrary")),
    )(qkv, qkv, qkv, run, run_t, o, lse, do, *c_in)
    return (out[0], out[1], out[2], _carry_state(out[3:])) if carry else tuple(out)


Z_COL0 = 3 * WIDTH // LANE
FGATE_COL0 = 7 * WIDTH // LANE


def _gdn_post(o, pm, onw):
    t = pm.shape[0]

    def body(o_ref, z_ref, w_ref, m_ref):
        z = z_ref[...]
        sz = z * _sigmoid(z)
        halves = []
        for hh in range(2):
            ov = o_ref[hh]
            n = ov * lax.rsqrt(jnp.mean(ov * ov, axis=-1, keepdims=True) + EPS) * w_ref[...]
            halves.append(n * sz[:, hh * DH:(hh + 1) * DH])
        m_ref[...] = jnp.concatenate(halves, axis=1).astype(m_ref.dtype)

    return pl.pallas_call(
        body, name="gdn_post", grid=(WIDTH // LANE,),
        in_specs=[pl.BlockSpec((2, t, DH), lambda j: (j, 0, 0)), pl.BlockSpec((t, LANE), lambda j: (0, Z_COL0 + j)),
                  pl.BlockSpec((1, DH), lambda j: (0, 0))],
        out_specs=pl.BlockSpec((t, LANE), lambda j: (0, j)),
        out_shape=jax.ShapeDtypeStruct((t, WIDTH), MXU),
        compiler_params=_params(("arbitrary",)),
    )(o, pm, onw)


def _gdn_post_bwd(o, pm, onw, dmix):
    t = pm.shape[0]

    def body(o_ref, z_ref, w_ref, dm_ref, do_ref, dz_ref, dw_ref):
        @pl.when(pl.program_id(0) == 0)
        def _():
            dw_ref[...] = jnp.zeros_like(dw_ref)

        z = z_ref[...]
        sg = _sigmoid(z)
        sz = z * sg
        dsz = sg * (1.0 + z * (1.0 - sg))
        dm = dm_ref[...]
        for hh in range(2):
            cols = slice(hh * DH, (hh + 1) * DH)
            ov = o_ref[hh]
            r = lax.rsqrt(jnp.mean(ov * ov, axis=-1, keepdims=True) + EPS)
            xn = ov * r
            dmh = dm[:, cols]
            dn = dmh * sz[:, cols]
            dz_ref[:, cols] = (dmh * (xn * w_ref[...]) * dsz[:, cols]).astype(dz_ref.dtype)
            dw_ref[...] += jnp.sum(dn * xn, axis=0, keepdims=True)
            g = dn * w_ref[...]
            do_ref[hh] = r * (g - xn * jnp.mean(g * xn, axis=-1, keepdims=True))

    return pl.pallas_call(
        body, name="gdn_post_bwd", grid=(WIDTH // LANE,),
        in_specs=[pl.BlockSpec((2, t, DH), lambda j: (j, 0, 0)), pl.BlockSpec((t, LANE), lambda j: (0, Z_COL0 + j)),
                  pl.BlockSpec((1, DH), lambda j: (0, 0)), pl.BlockSpec((t, LANE), lambda j: (0, j))],
        out_specs=[pl.BlockSpec((2, t, DH), lambda j: (j, 0, 0)), pl.BlockSpec((t, LANE), lambda j: (0, j)),
                   pl.BlockSpec((1, DH), lambda j: (0, 0))],
        out_shape=[jax.ShapeDtypeStruct((HEADS, t, DH), F32), jax.ShapeDtypeStruct((t, WIDTH), MXU),
                   jax.ShapeDtypeStruct((1, DH), F32)],
        compiler_params=_params(("arbitrary",)),
    )(o, pm, onw, dmix)


def _fox_post(o, pm):
    t = pm.shape[0]

    def body(o_ref, g_ref, m_ref):
        m_ref[...] = (jnp.concatenate([o_ref[0], o_ref[1]], axis=1) * _sigmoid(g_ref[...])).astype(m_ref.dtype)

    return pl.pallas_call(
        body, name="fox_post", grid=(WIDTH // LANE,),
        in_specs=[pl.BlockSpec((2, t, DH), lambda j: (j, 0, 0)), pl.BlockSpec((t, LANE), lambda j: (0, FGATE_COL0 + j))],
        out_specs=pl.BlockSpec((t, LANE), lambda j: (0, j)),
        out_shape=jax.ShapeDtypeStruct((t, WIDTH), MXU),
        compiler_params=_params(("arbitrary",)),
    )(o, pm)


def _fox_post_bwd(o, pm, dmix):
    t = pm.shape[0]

    def body(o_ref, g_ref, dm_ref, do_ref, dg_ref):
        sg = _sigmoid(g_ref[...])
        dm = dm_ref[...]
        for hh in range(2):
            cols = slice(hh * DH, (hh + 1) * DH)
            do_ref[hh] = dm[:, cols] * sg[:, cols]
            dg_ref[:, cols] = (dm[:, cols] * o_ref[hh] * (sg * (1.0 - sg))[:, cols]).astype(dg_ref.dtype)

    return pl.pallas_call(
        body, name="fox_post_bwd", grid=(WIDTH // LANE,),
        in_specs=[pl.BlockSpec((2, t, DH), lambda j: (j, 0, 0)), pl.BlockSpec((t, LANE), lambda j: (0, FGATE_COL0 + j)),
                  pl.BlockSpec((t, LANE), lambda j: (0, j))],
        out_specs=[pl.BlockSpec((2, t, DH), lambda j: (j, 0, 0)), pl.BlockSpec((t, LANE), lambda j: (0, j))],
        out_shape=[jax.ShapeDtypeStruct((HEADS, t, DH), F32), jax.ShapeDtypeStruct((t, WIDTH), MXU)],
        compiler_params=_params(("arbitrary",)),
    )(o, pm, dmix)


def _tail(x, mixg, mixf, tgt, wo, n2w, wg_t, wu_t, wd, fw):
    t, d = x.shape
    dff = wd.shape[0]
    tb = min(TB, t)

    def body(x_ref, mg_ref, mf_ref, t_ref, wo_ref, n2_ref, wg_ref, wu_ref, wd_ref, fw_ref,
             h2_ref, act_ref, dgate_ref, dup_ref, dx3_ref, dx2_ref, dmg_ref, dmf_ref, dn2_ref, dfw_ref, loss_ref):
        @pl.when(pl.program_id(0) == 0)
        def _():
            dn2_ref[...] = jnp.zeros_like(dn2_ref)
            dfw_ref[...] = jnp.zeros_like(dfw_ref)
            loss_ref[...] = jnp.zeros_like(loss_ref)

        x2 = x_ref[...] + _dot(mg_ref[...], wo_ref[0:WIDTH, :]) + _dot(mf_ref[...], wo_ref[WIDTH:2 * WIDTH, :])
        r2 = lax.rsqrt(jnp.mean(x2 * x2, axis=-1, keepdims=True) + EPS)
        xn2 = x2 * r2
        h2 = (xn2 * n2_ref[...]).astype(MXU)
        h2_ref[...] = h2
        gate = _dot_nt(h2, wg_ref[...])
        up = _dot_nt(h2, wu_ref[...])
        sg = _sigmoid(gate)
        sl = gate * sg
        act = (sl * up).astype(MXU)
        act_ref[...] = act
        x3 = x2 + _dot(act, wd_ref[...])
        r3 = lax.rsqrt(jnp.mean(x3 * x3, axis=-1, keepdims=True) + EPS)
        xn3 = x3 * r3
        err = xn3 * fw_ref[...] - t_ref[...]
        loss_ref[...] += 0.5 * jnp.sum(jnp.mean(err * err, axis=-1, keepdims=True), axis=0, keepdims=True)
        dy = err * (1.0 / d)
        dfw_ref[...] += jnp.sum(dy * xn3, axis=0, keepdims=True)
        g3 = dy * fw_ref[...]
        dx3 = r3 * (g3 - xn3 * jnp.mean(g3 * xn3, axis=-1, keepdims=True))
        dx3_ref[...] = dx3.astype(MXU)
        dact = _dot_nt(dx3, wd_ref[...])
        dgate = (dact * up * (sg * (1.0 + gate * (1.0 - sg)))).astype(MXU)
        dup = (dact * sl).astype(MXU)
        dgate_ref[...] = dgate
        dup_ref[...] = dup
        dh2 = _dot(dgate, wg_ref[...]) + _dot(dup, wu_ref[...])
        dn2_ref[...] += jnp.sum(dh2 * xn2, axis=0, keepdims=True)
        g2 = dh2 * n2_ref[...]
        dx2 = dx3 + r2 * (g2 - xn2 * jnp.mean(g2 * xn2, axis=-1, keepdims=True))
        dx2_ref[...] = dx2
        dmg_ref[...] = _dot_nt(dx2, wo_ref[0:WIDTH, :])
        dmf_ref[...] = _dot_nt(dx2, wo_ref[WIDTH:2 * WIDTH, :])

    def tok(n):
        return pl.BlockSpec((tb, n), lambda i: (i, 0))

    acc = pl.BlockSpec((1, d), lambda i: (0, 0))
    sds = jax.ShapeDtypeStruct
    return pl.pallas_call(
        body, name="tail", grid=(t // tb,),
        in_specs=[tok(d), tok(WIDTH), tok(WIDTH), tok(d), _resident(wo.shape), _resident((1, d)),
                  _resident(wg_t.shape), _resident(wu_t.shape), _resident(wd.shape), _resident((1, d))],
        out_specs=[tok(d), tok(dff), tok(dff), tok(dff), tok(d), tok(d), tok(WIDTH), tok(WIDTH), acc, acc,
                   pl.BlockSpec((1, 1), lambda i: (0, 0))],
        out_shape=[sds((t, d), MXU), sds((t, dff), MXU), sds((t, dff), MXU), sds((t, dff), MXU), sds((t, d), MXU),
                   sds((t, d), F32), sds((t, WIDTH), F32), sds((t, WIDTH), F32), sds((1, d), F32), sds((1, d), F32),
                   sds((1, 1), F32)],
        compiler_params=_params(("arbitrary",)),
    )(x, mixg, mixf, tgt, wo, n2w, wg_t, wu_t, wd, fw)


def _wgrads(a_list, b, name):
    t, n = b.shape
    ms = [a.shape[1] for a in a_list]
    bms = [256 if m % 256 == 0 else LANE for m in ms]
    nbs = [m // bm for m, bm in zip(ms, bms)]
    k = len(a_list)
    cast = b.dtype != jnp.dtype(MXU)

    def body(*refs):
        a_refs, b_ref, o_refs = refs[:k], refs[k], refs[k + 1:2 * k + 1]
        i = pl.program_id(0)
        if cast:
            @pl.when(i == 0)
            def _():
                refs[-1][...] = b_ref[...].astype(MXU)
        for a_ref, o_ref, nb in zip(a_refs, o_refs, nbs):
            @pl.when(i < nb)
            def _():
                o_ref[...] = _dot_tn(a_ref[...], refs[-1][...] if cast else b_ref[...]).astype(o_ref.dtype)

    def clamp(nb):
        return lambda i: jnp.minimum(i, nb - 1)

    return pl.pallas_call(
        body, name=name, grid=(max(nbs),),
        in_specs=[pl.BlockSpec((t, bm), lambda i, c=clamp(nb): (0, c(i))) for bm, nb in zip(bms, nbs)] + [_resident((t, n))],
        out_specs=[pl.BlockSpec((bm, n), lambda i, c=clamp(nb): (c(i), 0)) for bm, nb in zip(bms, nbs)],
        out_shape=[jax.ShapeDtypeStruct((m, n), WIRE) for m in ms],
        scratch_shapes=[pltpu.VMEM((t, n), MXU)] if cast else [],
        compiler_params=_params(("arbitrary",)),
    )(*a_list, b)


def _merge_dw_in(d_gdn, d_z, d_fox, d_fg, d_small, shards=False):
    pieces = [d_gdn, d_z, d_small[:2 * HEADS], d_fox, d_fg, d_small[2 * HEADS:3 * HEADS]]
    if not shards:
        return jnp.concatenate(pieces, axis=0)
    n = sum(p.shape[0] for p in pieces) // N_DEV
    out = []
    for dev in range(N_DEV):
        parts, first = [], 0
        for p in pieces:
            lo, hi = max(dev * n, first), min((dev + 1) * n, first + p.shape[0])
            if lo < hi:
                parts.append(p[lo - first:hi - first])
            first += p.shape[0]
        out.append(jnp.concatenate(parts, axis=0))
    return jnp.stack(out)


def _lanes(*pieces):
    v = jnp.concatenate([p.reshape(-1).astype(F32) for p in pieces])
    return jnp.pad(v, (0, LANE - v.shape[0])).reshape(1, LANE)


def _vector_params(p):
    d = p["norm1_w"].size
    gparams = jnp.concatenate([_lanes(jnp.zeros(HEADS), p["gdn_dt_bias"], p["fox_f_bias"]),
                               _lanes(jnp.zeros(HEADS), p["gdn_A_log"]), jnp.zeros((6, LANE), F32)])
    fox_nw = jnp.stack([jnp.tile(p["fox_q_norm_w"].reshape(-1), 2), jnp.tile(p["fox_k_norm_w"].reshape(-1), 2),
                        jnp.ones((LANE,), F32)])
    return dict(n1w=p["norm1_w"].reshape(1, d), n2w=p["norm2_w"].reshape(1, d), fw=p["final_norm_w"].reshape(1, d),
                onw=p["gdn_out_norm_w"].reshape(1, DH), gparams=gparams, fox_nw=fox_nw)


def _mixer_forward(x, vp, w_t, ws_t, conv_w, carry=None, carry_first=None):
    h1, pm, ps, *first = _inproj(x, vp["n1w"], w_t, ws_t, carry_first)
    gates, run, tot, run_t = _gates(ps, vp["gparams"])
    fqkv = _fox_prep(pm, vp["fox_nw"])
    o_fox, lse, *carried = _fox_fwd(fqkv, run, run_t, carry)
    if carry:
        pm, o_fox = lax.optimization_barrier((pm, o_fox))
    mixf = _fox_post(o_fox, pm)
    gqkv = _gdn_prep(pm, conv_w)
    o_gdn, states, inv = _gdn_fwd(gqkv, gates, run, tot, run_t)
    mixg = _gdn_post(o_gdn, pm, vp["onw"])
    return dict(h1=h1, pm=pm, ps=ps, gates=gates, run=run, tot=tot, run_t=run_t, gqkv=gqkv, o_gdn=o_gdn, states=states,
                inv=inv, mixg=mixg, fqkv=fqkv, o_fox=o_fox, lse=lse, mixf=mixf, carried=carried[0] if carried else None,
                carried_first=first[0] if first else None)


def _mixer_backward(x, vp, w_t, ws_t, conv_w, f, dx2, dmixg, dmixf, carry=None, scatter_own=False):
    pm = f["pm"]
    do_fox, dfg = _fox_post_bwd(f["o_fox"], pm, dmixf)
    dfqkv, dcol_f, drow_f, *carried = _fox_bwd(f["fqkv"], f["run"], f["run_t"], f["o_fox"], f["lse"], do_fox, carry)
    dfox, dfnw = _fox_prep_bwd(pm, vp["fox_nw"], dfqkv)
    do_gdn, dz, donw = _gdn_post_bwd(f["o_gdn"], pm, vp["onw"], dmixg)
    dgqkv, dcol_g, dtot_g, drow_g = _gdn_bwd(f["gqkv"], f["gates"], f["run"], f["tot"], f["run_t"], f["inv"], f["states"], do_gdn)
    dgdn, dconv = _gdn_prep_bwd(pm, conv_w, dgqkv)
    dps, gsum = _gates_bwd(f["ps"], vp["gparams"], dcol_g, dtot_g, dcol_f, drow_g, drow_f)
    dw_pieces = _wgrads([dgdn, dz, dfox, dfg, dps], f["h1"], "dw_in")
    dw_in = _merge_dw_in(*dw_pieces)
    own = ("scatter", [_merge_dw_in(*dw_pieces, shards=True), _cut(dconv, 1)]) if scatter_own else None
    grad_x, dn1w, *sent = _inproj_bwd(x, vp["n1w"], dx2, dgdn, dz, dfox, dfg, dps, w_t, ws_t, own)
    small = dict(dn1w=dn1w, gsum=gsum, donw=donw, dfnw=dfnw)
    return (grad_x, dw_in, dconv, small, *carried, *sent)


VECTORS = ("norm1_w", "norm2_w", "final_norm_w", "gdn_A_log", "gdn_dt_bias", "gdn_out_norm_w", "fox_f_bias",
           "fox_q_norm_w", "fox_k_norm_w")
VEC_ROWS = 16
LOSS_ROW = len(VECTORS)


def _pack_vectors(dn1w, dn2w, dfw, gsum, donw, dfnw, loss):
    d = dn1w.shape[1]

    def body(n1_ref, n2_ref, fw_ref, gs_ref, on_ref, fn_ref, loss_ref, o_ref):
        o_ref[...] = jnp.zeros_like(o_ref)
        o_ref[0:1, :] = n1_ref[...]
        o_ref[1:2, :] = n2_ref[...]
        o_ref[2:3, :] = fw_ref[...]
        o_ref[3:4, 0:HEADS] = gs_ref[0:1, 0:HEADS]
        o_ref[4:5, 0:HEADS] = gs_ref[1:2, 0:HEADS]
        o_ref[5:6, 0:DH] = on_ref[...]
        o_ref[6:7, 0:HEADS] = gs_ref[2:3, 0:HEADS]
        for kind in range(2):
            v = fn_ref[kind, 0]
            for j in range(1, fn_ref.shape[1]):
                v = v + fn_ref[kind, j]
            o_ref[7 + kind:8 + kind, 0:DH] = v[:, :DH] + v[:, DH:]
        o_ref[LOSS_ROW:LOSS_ROW + 1, 0:1] = loss_ref[...]

    return pl.pallas_call(body, name="pack_vectors", out_shape=jax.ShapeDtypeStruct((VEC_ROWS, d), F32),
                          compiler_params=_params())(dn1w, dn2w, dfw, gsum, donw, dfnw, loss)


def _late_grads(f, h2, act, dgate, dup, dx3, dx2):
    dw_gate, dw_up = _wgrads([dgate, dup], h2, "dw_gate_up")
    (dw_down,) = _wgrads([act], dx3, "dw_down")
    return {"w_out": jnp.concatenate(_wgrads([f["mixg"], f["mixf"]], dx2, "dw_out"), axis=0),
            "w_ffn_gate": dw_gate, "w_ffn_up": dw_up, "w_ffn_down": dw_down}


def _local_step(x, tgt, p, w_in_t, conv_w, wo, wg_t, wu_t, wd):
    vp = _vector_params(p)
    ws_t = _small_rows(w_in_t)
    f = _mixer_forward(x, vp, w_in_t, ws_t, conv_w)
    (h2, act, dgate, dup, dx3, dx2, dmixg, dmixf, dn2w, dfw, loss) = _tail(
        x, f["mixg"], f["mixf"], tgt, wo, vp["n2w"], wg_t, wu_t, wd, vp["fw"])
    grad_x, dw_in, dconv, small = _mixer_backward(x, vp, w_in_t, ws_t, conv_w, f, dx2, dmixg, dmixf)
    grads = {"w_in": dw_in, "gdn_conv_w": dconv, **_late_grads(f, h2, act, dgate, dup, dx3, dx2)}
    vec = _pack_vectors(small["dn1w"], dn2w, dfw, small["gsum"], small["donw"], small["dfnw"], loss)
    return grad_x[0], grads, vec


def _my_place():
    return lax.axis_index("x"), lax.axis_index("y"), lax.axis_index("c")


def _peers():
    x, y, c = _my_place()
    peers = []
    for k in range(1, N_DEV):
        px = 1 - x if k & 4 else x
        py = 1 - y if k & 2 else y
        pc = 1 - c if k & 1 else c
        peers.append(((px, py, pc), 4 * px + 2 * py + pc))
    return 4 * x + 2 * y + c, peers


def _spread_copies(kind, srcs, lands, send_sems, recv_sems):
    me, peers = _peers()
    kinds = [kind] * len(srcs) if isinstance(kind, str) else kind
    remote, local = [], []
    for i, (kd, src, land) in enumerate(zip(kinds, srcs, lands)):
        for k, (dev, idx) in enumerate(peers):
            remote.append(pltpu.make_async_remote_copy(
                src_ref=src if kd == "gather" else src.at[idx], dst_ref=land.at[me],
                send_sem=send_sems.at[i * (N_DEV - 1) + k], recv_sem=recv_sems.at[i * (N_DEV - 1) + k],
                device_id=dev, device_id_type=MESH))
        local.append((src if kd == "gather" else src.at[me], land.at[me]))
    return remote, local


def _gather_two_level(arrays, name):
    n = len(arrays)

    def body(*refs):
        srcs, lands = refs[:n], refs[n:2 * n]
        send_sems, recv_sems, local_sems = refs[2 * n:]
        x, y, c = _my_place()
        me, sibling = (x, y, c), (x, y, 1 - c)
        chips = [(1 - x, y), (x, 1 - y), (1 - x, 1 - y)]

        def index(p):
            return 4 * p[0] + 2 * p[1] + p[2]

        def copy(i, k, block, to, src=None):
            blk = lands[i].at[index(block)]
            return pltpu.make_async_remote_copy(
                src_ref=blk if src is None else src, dst_ref=blk, send_sem=send_sems.at[7 * i + k],
                recv_sem=recv_sems.at[7 * i + k], device_id=to, device_id_type=MESH)

        mine = [pltpu.make_async_copy(srcs[i], lands[i].at[index(me)], local_sems.at[i]) for i in range(n)]
        for cp in mine:
            cp.start()
        first = []
        for i in range(n):
            first.append(copy(i, 0, me, sibling, src=srcs[i]))
            first += [copy(i, 1 + j, me, (*chip, c), src=srcs[i]) for j, chip in enumerate(chips)]
        for cp in first:
            cp.start()
        passed = []
        for i in range(n):
            for j, chip in enumerate(chips):
                copy(i, 1 + j, (*chip, c), me).wait_recv()
                passed.append(copy(i, 4 + j, (*chip, c), sibling))
                passed[-1].start()
        for i in range(n):
            copy(i, 0, sibling, me).wait_recv()
            for j, chip in enumerate(chips):
                copy(i, 4 + j, (*chip, 1 - c), me).wait_recv()
        for cp in first + passed:
            cp.wait_send()
        for cp in mine:
            cp.wait()

    return pl.pallas_call(
        body, name=name,
        out_shape=[jax.ShapeDtypeStruct((N_DEV,) + a.shape, a.dtype) for a in arrays],
        in_specs=[pl.BlockSpec(memory_space=pl.ANY)] * n, out_specs=[pl.BlockSpec(memory_space=pl.ANY)] * n,
        scratch_shapes=[pltpu.SemaphoreType.DMA((7 * n,)), pltpu.SemaphoreType.DMA((7 * n,)),
                        pltpu.SemaphoreType.DMA((n,))],
    )(*arrays)


def _land_shape(kind, a):
    return (N_DEV,) + a.shape if kind == "gather" else a.shape


HBM = pl.BlockSpec(memory_space=pltpu.HBM)
SEM = pl.BlockSpec(memory_space=pltpu.SEMAPHORE)


def _hbm(a):
    return pltpu.with_memory_space_constraint(a, pltpu.HBM)


def _carry_operands(kind, arrays):
    n = len(arrays)
    kinds = [kind] * n if isinstance(kind, str) else kind
    lands = [lax.empty(_land_shape(kd, a), a.dtype) for kd, a in zip(kinds, arrays)]
    sems = [pltpu.SemaphoreType.DMA((n * (N_DEV - 1),))] * 2
    return ([_hbm(a) for a in list(arrays) + lands], [HBM] * (2 * n),
            sems + [pltpu.HBM(a.shape, a.dtype) for a in list(arrays) + lands], [SEM] * 2 + [HBM] * (2 * n),
            pltpu.SemaphoreType.DMA((n,)))


def _carry_step(kind, in_refs, send_sems, recv_sems, local_sems, step, n_steps):
    n = len(in_refs) // 2
    remote, local = _spread_copies(kind, in_refs[:n], in_refs[n:], send_sems, recv_sems)
    copies = [pltpu.make_async_copy(s, d, local_sems.at[i]) for i, (s, d) in enumerate(local)]
    per_step = -(-len(remote) // n_steps)
    for s in range(-(-len(remote) // per_step)):
        @pl.when(step == s)
        def _():
            for cp in remote[s * per_step:(s + 1) * per_step]:
                cp.start()
            if s == 0:
                for cp in copies:
                    cp.start()

    @pl.when(step == n_steps - 1)
    def _():
        for cp in copies:
            cp.wait()


def _carry_state(extra_out):
    n = (len(extra_out) - 2) // 2
    return list(extra_out[2:2 + n]), list(extra_out[2 + n:]), extra_out[0], extra_out[1]


def _spread_start(arrays, kind, name):
    n = len(arrays)

    def body(*refs):
        srcs, lands = refs[:n], refs[n:2 * n]
        send_sems, recv_sems = refs[2 * n], refs[2 * n + 1]
        token = refs[4 * n + 2]
        local_sems = refs[4 * n + 3]
        remote, local = _spread_copies(kind, srcs, lands, send_sems, recv_sems)
        for cp in remote:
            cp.start()
        copies = [pltpu.make_async_copy(s, d, local_sems.at[i]) for i, (s, d) in enumerate(local)]
        for cp in copies:
            cp.start()
        for cp in copies:
            cp.wait()
        token[...] = jnp.zeros_like(token)

    sems = (pltpu.SemaphoreType.DMA((n * (N_DEV - 1),)),) * 2
    kinds = [kind] * n if isinstance(kind, str) else kind
    lands = [lax.empty(_land_shape(kd, a), a.dtype) for kd, a in zip(kinds, arrays)]
    out = pl.pallas_call(
        body, name=name,
        out_shape=sems + tuple(pltpu.HBM(a.shape, a.dtype) for a in list(arrays) + lands)
        + (jax.ShapeDtypeStruct((8, LANE), F32),),
        in_specs=[HBM] * (2 * n), out_specs=tuple([SEM] * 2 + [HBM] * (2 * n) + [pl.BlockSpec(memory_space=pltpu.VMEM)]),
        input_output_aliases={j: 2 + j for j in range(2 * n)},
        scratch_shapes=[pltpu.SemaphoreType.DMA((n,))],
        compiler_params=pltpu.CompilerParams(has_side_effects=pltpu.SideEffectType.DATAFLOW_SIDE_EFFECTING),
    )(*[_hbm(a) for a in arrays], *[_hbm(a) for a in lands])
    return (list(out[2:2 + n]), list(out[2 + n:2 + 2 * n]), out[0], out[1]), out[-1]


def _spread_wait(state, kind, after, name):
    srcs, lands, send_sems, recv_sems = state
    n = len(srcs)
    after = list(after) if isinstance(after, (list, tuple)) else [after]

    def body(*refs):
        remote, _ = _spread_copies(kind, refs[:n], refs[n:2 * n], refs[2 * n], refs[2 * n + 1])
        for cp in remote:
            cp.wait_send()
        for cp in remote:
            cp.wait_recv()

    out = pl.pallas_call(
        body, name=name,
        out_shape=tuple(pltpu.HBM(a.shape, a.dtype) for a in srcs + lands),
        in_specs=[HBM] * (2 * n) + [SEM, SEM] + [pl.BlockSpec(memory_space=pl.ANY)] * len(after),
        out_specs=tuple([HBM] * (2 * n)),
        input_output_aliases={j: j for j in range(2 * n)},
        compiler_params=pltpu.CompilerParams(has_side_effects=pltpu.SideEffectType.DATAFLOW_SIDE_EFFECTING),
    )(*srcs, *lands, send_sems, recv_sems, *after)
    return list(out[n:])


ADAM_ROWS = 128
ADAM_COLS = 256


def _adam_math(g, w, m, v):
    nm = ADAM_B1 * m + (1.0 - ADAM_B1) * g
    nv = ADAM_B2 * v + (1.0 - ADAM_B2) * (g * g)
    m_hat = nm / (1.0 - ADAM_B1 ** ADAM_STEP)
    v_hat = nv / (1.0 - ADAM_B2 ** ADAM_STEP)
    return -ADAM_LR * (m_hat / (jnp.sqrt(v_hat) + ADAM_EPS) + ADAM_WD * w), nm, nv


def _sum_parts(p_ref):
    g = p_ref[0].astype(F32)
    for s in range(1, N_DEV):
        g = g + p_ref[s].astype(F32)
    return g


def _adam_matrix(parts, w, m, v, name):
    _, r, c = w.shape
    rb = ADAM_ROWS if r % ADAM_ROWS == 0 else r
    cb = ADAM_COLS if (rb == r and c % ADAM_COLS == 0) else c

    def body(p_ref, w_ref, m_ref, v_ref, g_ref, d_ref, nm_ref, nv_ref):
        g = _sum_parts(p_ref)
        g_ref[0] = g
        d_ref[0], nm_ref[0], nv_ref[0] = _adam_math(g, w_ref[0], m_ref[0], v_ref[0])

    blk = pl.BlockSpec((1, rb, cb), lambda i, j: (0, i, j))
    return pl.pallas_call(
        body, name=name, grid=(r // rb, c // cb),
        in_specs=[pl.BlockSpec((N_DEV, rb, cb), lambda i, j: (0, i, j)), blk, blk, blk],
        out_specs=[blk] * 4, out_shape=[jax.ShapeDtypeStruct(w.shape, F32)] * 4,
        compiler_params=_params(("arbitrary", "arbitrary")),
    )(parts, w, m, v)


def _copy_rows(a, name):
    _, r, c = a.shape
    rb = min(TB, r)

    def body(a_ref, o_ref):
        o_ref[...] = a_ref[...]

    blk = pl.BlockSpec((1, rb, c), lambda i: (0, i, 0))
    return pl.pallas_call(body, name=name, grid=(r // rb,), in_specs=[blk], out_specs=blk,
                          out_shape=jax.ShapeDtypeStruct(a.shape, a.dtype), compiler_params=_params(("arbitrary",)))(a)


def _adam_vectors(parts, ws, ms, vs):
    nv = len(ws)

    def body(*refs):
        p_ref = refs[0]
        w_refs, m_refs, v_refs = refs[1:1 + nv], refs[1 + nv:1 + 2 * nv], refs[1 + 2 * nv:1 + 3 * nv]
        outs = refs[1 + 3 * nv:]
        g_all = _sum_parts(p_ref)
        for i in range(nv):
            n = w_refs[i].shape[1]
            g = g_all[i:i + 1, 0:n]
            d, nm, nvv = _adam_math(g, w_refs[i][...], m_refs[i][...], v_refs[i][...])
            outs[i][...] = g
            outs[nv + i][...] = d
            outs[2 * nv + i][...] = nm
            outs[3 * nv + i][...] = nvv
        outs[4 * nv][...] = g_all[LOSS_ROW:LOSS_ROW + 1, 0:1]

    shapes = [jax.ShapeDtypeStruct(a.shape, F32) for a in ws]
    out = pl.pallas_call(body, name="adam_vectors", out_shape=shapes * 4 + [jax.ShapeDtypeStruct((1, 1), F32)],
                         compiler_params=_params())(parts, *ws, *ms, *vs)
    return out[:nv], out[nv:2 * nv], out[2 * nv:3 * nv], out[3 * nv:4 * nv], out[4 * nv]


MATRICES = (("w_in", 1), ("gdn_conv_w", 1), ("w_out", 0), ("w_ffn_gate", 1), ("w_ffn_up", 1), ("w_ffn_down", 0))
TRANSPOSED = ("w_in", "w_ffn_gate", "w_ffn_up")
WEIGHTS = ("norm1_w", "w_in", "gdn_conv_w", "gdn_A_log", "gdn_dt_bias", "gdn_out_norm_w", "fox_f_bias", "fox_q_norm_w",
           "fox_k_norm_w", "w_out", "norm2_w", "w_ffn_gate", "w_ffn_up", "w_ffn_down", "final_norm_w")


def _join(blocks, axis):
    _, r, c = blocks.shape
    if axis == 0:
        return blocks.reshape(N_DEV * r, c)
    return blocks.transpose(1, 0, 2).reshape(r, N_DEV * c)


def _cut(full, axis):
    r, c = full.shape
    if axis == 0:
        return full.reshape(N_DEV, r // N_DEV, c)
    return full.reshape(r, N_DEV, c // N_DEV).transpose(1, 0, 2)


def kernel(x, norm1_w, w_in, gdn_conv_w, gdn_A_log, gdn_dt_bias, gdn_out_norm_w, fox_f_bias, fox_q_norm_w, fox_k_norm_w, w_out, norm2_w, w_ffn_gate, w_ffn_up, w_ffn_down, final_norm_w, loss_target, m_norm1_w, m_w_in, m_gdn_conv_w, m_gdn_A_log, m_gdn_dt_bias, m_gdn_out_norm_w, m_fox_f_bias, m_fox_q_norm_w, m_fox_k_norm_w, m_w_out, m_norm2_w, m_w_ffn_gate, m_w_ffn_up, m_w_ffn_down, m_final_norm_w, v_norm1_w, v_w_in, v_gdn_conv_w, v_gdn_A_log, v_gdn_dt_bias, v_gdn_out_norm_w, v_fox_f_bias, v_fox_q_norm_w, v_fox_k_norm_w, v_w_out, v_norm2_w, v_w_ffn_gate, v_w_ffn_up, v_w_ffn_down, v_final_norm_w):
    w = dict(norm1_w=norm1_w, w_in=w_in, gdn_conv_w=gdn_conv_w, gdn_A_log=gdn_A_log, gdn_dt_bias=gdn_dt_bias,
             gdn_out_norm_w=gdn_out_norm_w, fox_f_bias=fox_f_bias, fox_q_norm_w=fox_q_norm_w, fox_k_norm_w=fox_k_norm_w,
             w_out=w_out, norm2_w=norm2_w, w_ffn_gate=w_ffn_gate, w_ffn_up=w_ffn_up, w_ffn_down=w_ffn_down,
             final_norm_w=final_norm_w)
    m = dict(norm1_w=m_norm1_w, w_in=m_w_in, gdn_conv_w=m_gdn_conv_w, gdn_A_log=m_gdn_A_log, gdn_dt_bias=m_gdn_dt_bias,
             gdn_out_norm_w=m_gdn_out_norm_w, fox_f_bias=m_fox_f_bias, fox_q_norm_w=m_fox_q_norm_w,
             fox_k_norm_w=m_fox_k_norm_w, w_out=m_w_out, norm2_w=m_norm2_w, w_ffn_gate=m_w_ffn_gate,
             w_ffn_up=m_w_ffn_up, w_ffn_down=m_w_ffn_down, final_norm_w=m_final_norm_w)
    v = dict(norm1_w=v_norm1_w, w_in=v_w_in, gdn_conv_w=v_gdn_conv_w, gdn_A_log=v_gdn_A_log, gdn_dt_bias=v_gdn_dt_bias,
             gdn_out_norm_w=v_gdn_out_norm_w, fox_f_bias=v_fox_f_bias, fox_q_norm_w=v_fox_q_norm_w,
             fox_k_norm_w=v_fox_k_norm_w, w_out=v_w_out, norm2_w=v_norm2_w, w_ffn_gate=v_w_ffn_gate,
             w_ffn_up=v_w_ffn_up, w_ffn_down=v_w_ffn_down, final_norm_w=v_final_norm_w)
    late = ("w_out", "w_ffn_gate", "w_ffn_up", "w_ffn_down")
    xs, tgt = x[0], loss_target[0]
    vp = _vector_params({n: w[n] for n in VECTORS})

    def rows_of(d, n):
        return d[n].transpose(0, 2, 1) if n in TRANSPOSED else d[n]

    wr, mr, vr = ({n: rows_of(d, n) for n, _ in MATRICES} for d in (w, m, v))

    w_in_blocks, conv_blocks = _gather_two_level([wr["w_in"][0].astype(WIRE), w["gdn_conv_w"][0]], "gather_in")
    w_t = _join(w_in_blocks, 0)
    conv_w = _join(conv_blocks, 1)
    early = ("w_ffn_gate",)
    rest = tuple(n for n in late if n not in early)
    f = _mixer_forward(xs, vp, w_t, _small_rows(w_t), conv_w, ("gather", [wr[n][0].astype(WIRE) for n in rest]),
                       ("gather", [wr[n][0].astype(WIRE) for n in early]))
    full = {n: _join(b, 0) for n, b in zip(early, _spread_wait(f["carried_first"], "gather", f["mixg"], "gather_early_wait"))}
    full.update({n: _join(b, 0)
                 for n, b in zip(rest, _spread_wait(f["carried"], "gather", full[early[0]], "gather_late_wait"))})

    (h2, act, dgate, dup, dx3, dx2, dmixg, dmixf, dn2w, dfw, loss) = _tail(
        xs, f["mixg"], f["mixf"], tgt, full["w_out"], vp["n2w"], full["w_ffn_gate"], full["w_ffn_up"],
        full["w_ffn_down"], vp["fw"])
    dlate = _late_grads(f, h2, act, dgate, dup, dx3, dx2)

    grad_x, dw_in, dconv, small, state_grads, state_own = _mixer_backward(
        xs, vp, w_t, _small_rows(w_t), conv_w, f, dx2, dmixg, dmixf, ("scatter", [_cut(dlate[n], 0) for n in late]),
        scatter_own=True)
    vec = _pack_vectors(small["dn1w"], dn2w, dfw, small["gsum"], small["donw"], small["dfnw"], loss)

    state_vec, token = _spread_start([vec], "gather", "vectors_start")
    parts = dict(zip(late, _spread_wait(state_grads, "scatter", token, "grads_late_wait")))
    results = [{}, {}, {}, {}]

    updated = {}

    def update(n):
        out = _adam_matrix(parts[n], wr[n], mr[n], vr[n], "adam_" + n)
        updated[n] = out[0]
        for d, a in zip(results, out):
            d[n] = a.transpose(0, 2, 1) if n in TRANSPOSED else a

    for n in late:
        update(n)
    grad_x = _copy_rows(grad_x, "grad_x_result")
    ready = [updated[n] for n in late] + [d["w_in"] for d in (wr, mr, vr)] + [grad_x]
    (parts_vec,) = _spread_wait(state_vec, "gather", ready, "vectors_wait")
    row = lambda a: a.reshape(1, -1)
    *vec_out, total_loss = _adam_vectors(parts_vec, [row(w[n]) for n in VECTORS], [row(m[n]) for n in VECTORS],
                                         [row(v[n]) for n in VECTORS])
    parts["w_in"], parts["gdn_conv_w"] = _spread_wait(state_own, "scatter", parts_vec, "own_wait")
    update("w_in")
    update("gdn_conv_w")
    for d, arrs in zip(results, vec_out):
        for n, a in zip(VECTORS, arrs):
            d[n] = a.reshape(w[n].shape)
    return (total_loss[0, 0], grad_x, *[d[n] for d in results for n in WEIGHTS])
```

```python
import functools

import jax
import jax.numpy as jnp
from jax import lax
from jax.experimental import pallas as pl
from jax.experimental.pallas import tpu as pltpu

F32 = jnp.float32
MXU = jnp.bfloat16
WIRE = jnp.bfloat16
HI = lax.Precision.HIGHEST
EPS = 1e-6

N_DEV = 8
HEADS = 8
DH = 64
WIDTH = HEADS * DH
CHUNK = 64
LANE = 128
ROW_ALIGN = 16
TB = 256
QB = 256
VMEM_LIMIT = 60 * 1024 * 1024

ADAM_LR = 0.001
ADAM_B1 = 0.9
ADAM_B2 = 0.999
ADAM_EPS = 1e-08
ADAM_WD = 0.01
ADAM_STEP = 10

MESH = pl.DeviceIdType.MESH


def _params(sem=None):
    return pltpu.CompilerParams(dimension_semantics=sem, vmem_limit_bytes=VMEM_LIMIT)


def _resident(shape):
    n = len(shape)
    return pl.BlockSpec(shape, lambda *_: (0,) * n, pipeline_mode=pl.Buffered(1))


def _dot(a, b):
    return jnp.dot(a.astype(MXU), b.astype(MXU), preferred_element_type=F32)


def _dot_nt(a, b):
    return lax.dot_general(a.astype(MXU), b.astype(MXU), (((1,), (1,)), ((), ())), preferred_element_type=F32)


def _dot_tn(a, b):
    return lax.dot_general(a.astype(MXU), b.astype(MXU), (((0,), (0,)), ((), ())), preferred_element_type=F32)


def _hdot(a, b):
    return jnp.dot(a, b, precision=HI, preferred_element_type=F32)


def _hdot_nt(a, b):
    return lax.dot_general(a, b, (((1,), (1,)), ((), ())), precision=HI, preferred_element_type=F32)


def _hdot_tn(a, b):
    return lax.dot_general(a, b, (((0,), (0,)), ((), ())), precision=HI, preferred_element_type=F32)


def _sigmoid(x):
    return 0.5 * jnp.tanh(0.5 * x) + 0.5


def _softplus(x):
    return jnp.maximum(x, 0.0) + jnp.log(1.0 + jnp.exp(-jnp.abs(x)))


def _head_sum_matrix():
    ri = lax.broadcasted_iota(jnp.int32, (LANE, LANE), 0) // DH
    ci = lax.broadcasted_iota(jnp.int32, (LANE, LANE), 1) // DH
    return (ri == ci).astype(F32)


def _group_sum(a, ones_matrix):
    hi = a.astype(jnp.bfloat16)
    lo = (a - hi.astype(F32)).astype(jnp.bfloat16)
    m = ones_matrix.astype(jnp.bfloat16)
    return jnp.dot(hi, m, preferred_element_type=F32) + jnp.dot(lo, m, preferred_element_type=F32)


def _shift_down(x, s):
    return pltpu.roll(x, s, 0)


def _shift_up(x, s):
    return pltpu.roll(x, x.shape[0] - s, 0)


ROWS_A = 4 * WIDTH
ROWS_B = ROWS_A + 2 * HEADS
ROWS_C = ROWS_B + 4 * WIDTH


def _small_rows(w_t):
    return jnp.concatenate([w_t[ROWS_A:ROWS_B], w_t[ROWS_C:], jnp.zeros((LANE - 3 * HEADS, w_t.shape[1]), w_t.dtype)])


def _inproj(x, n1w, w_t, ws_t, carry=None):
    t, d = x.shape
    tb = min(TB, t)
    c_in, c_in_specs, c_out_shape, c_out_specs, c_sem = _carry_operands(*carry) if carry else ([], [], [], [], None)
    n_c = len(c_in)

    def body(*refs):
        x_ref, nw_ref, wt_ref, ws_ref = refs[:4]
        h_ref, pm_ref, ps_ref = refs[4 + n_c:7 + n_c]
        if carry:
            _carry_step(carry[0], refs[4:4 + n_c], refs[7 + n_c], refs[8 + n_c], refs[-1], pl.program_id(0), t // tb)
        xv = x_ref[...]
        r = lax.rsqrt(jnp.mean(xv * xv, axis=-1, keepdims=True) + EPS)
        h = (xv * r * nw_ref[...]).astype(MXU)
        h_ref[...] = h
        pm_ref[:, 0:ROWS_A] = _dot_nt(h, wt_ref[0:ROWS_A, :])
        pm_ref[:, ROWS_A:2 * ROWS_A] = _dot_nt(h, wt_ref[ROWS_B:ROWS_C, :])
        ps_ref[...] = _dot_nt(h, ws_ref[...])

    out = pl.pallas_call(
        body, name="inproj", grid=(t // tb,),
        in_specs=[pl.BlockSpec((tb, d), lambda i: (i, 0)), _resident((1, d)), _resident(w_t.shape), _resident((LANE, d))]
        + c_in_specs,
        out_specs=[pl.BlockSpec((tb, d), lambda i: (i, 0)), pl.BlockSpec((tb, 2 * ROWS_A), lambda i: (i, 0)),
                   pl.BlockSpec((tb, LANE), lambda i: (i, 0))] + c_out_specs,
        out_shape=[jax.ShapeDtypeStruct((t, d), MXU), jax.ShapeDtypeStruct((t, 2 * ROWS_A), F32),
                   jax.ShapeDtypeStruct((t, LANE), F32)] + c_out_shape,
        input_output_aliases={4 + j: 5 + j for j in range(n_c)},
        scratch_shapes=[c_sem] if carry else [],
        compiler_params=_params(("arbitrary",)),
    )(x, n1w, w_t, ws_t, *c_in)
    return (out[0], out[1], out[2], _carry_state(out[3:])) if carry else tuple(out)


def _inproj_bwd(x, n1w, dx2, dgdn, dz, dfox, dfg, dps, w_t, ws_t, carry=None):
    t, d = x.shape
    tb = min(TB, t)
    w3 = 3 * WIDTH
    c_in, c_in_specs, c_out_shape, c_out_specs, c_sem = _carry_operands(*carry) if carry else ([], [], [], [], None)
    n_c = len(c_in)

    def body(*refs):
        x_ref, nw_ref, dx2_ref, dgdn_ref, dz_ref, dfox_ref, dfg_ref, dps_ref, wm_ref, ws_ref = refs[:10]
        gx_ref, dnw_ref = refs[10 + n_c:12 + n_c]
        if carry:
            _carry_step(carry[0], refs[10:10 + n_c], refs[12 + n_c], refs[13 + n_c], refs[-1], pl.program_id(0), t // tb)
        dh = _dot(dgdn_ref[...], wm_ref[0:w3, :])
        dh += _dot(dz_ref[...], wm_ref[w3:ROWS_A, :])
        dh += _dot(dfox_ref[...], wm_ref[ROWS_B:ROWS_B + w3, :])
        dh += _dot(dfg_ref[...], wm_ref[ROWS_B + w3:ROWS_C, :])
        dh += _dot(dps_ref[...], ws_ref[...])
        xv = x_ref[...]
        r = lax.rsqrt(jnp.mean(xv * xv, axis=-1, keepdims=True) + EPS)
        xn = xv * r

        @pl.when(pl.program_id(0) == 0)
        def _():
            dnw_ref[...] = jnp.zeros_like(dnw_ref)

        dnw_ref[...] += jnp.sum(dh * xn, axis=0, keepdims=True)
        g = dh * nw_ref[...]
        gx_ref[0] = dx2_ref[...] + r * (g - xn * jnp.mean(g * xn, axis=-1, keepdims=True))

    def tok(n):
        return pl.BlockSpec((tb, n), lambda i: (i, 0))

    out = pl.pallas_call(
        body, name="inproj_bwd", grid=(t // tb,),
        in_specs=[tok(d), _resident((1, d)), tok(d), tok(w3), tok(WIDTH), tok(w3), tok(WIDTH), tok(LANE),
                  _resident(w_t.shape), _resident(ws_t.shape)] + c_in_specs,
        out_specs=[pl.BlockSpec((1, tb, d), lambda i: (0, i, 0)), pl.BlockSpec((1, d), lambda i: (0, 0))] + c_out_specs,
        out_shape=[jax.ShapeDtypeStruct((1, t, d), F32), jax.ShapeDtypeStruct((1, d), F32)] + c_out_shape,
        input_output_aliases={10 + j: 4 + j for j in range(n_c)},
        scratch_shapes=[c_sem] if carry else [],
        compiler_params=_params(("arbitrary",)),
    )(x, n1w, dx2, dgdn, dz, dfox, dfg, dps, w_t, ws_t, *c_in)
    return (out[0], out[1], _carry_state(out[2:])) if carry else tuple(out)


def _gate_lanes(shape):
    lane = lax.broadcasted_iota(jnp.int32, shape, 1)
    return lane < HEADS, (lane >= HEADS) & (lane < 2 * HEADS), (lane >= 2 * HEADS) & (lane < 3 * HEADS)


def _block_masks():
    ri = lax.broadcasted_iota(jnp.int32, (LANE, LANE), 0)
    ci = lax.broadcasted_iota(jnp.int32, (LANE, LANE), 1)
    same = (ri // CHUNK) == (ci // CHUNK)
    return ((ri >= ci).astype(F32), (ri <= ci).astype(F32), (same & (ri >= ci)).astype(F32),
            (same & (ri <= ci)).astype(F32), same.astype(F32))


def _gates(ps, gparams):
    t = ps.shape[0]
    nb = t // LANE

    def body(ps_ref, gp_ref, out_ref, run_ref, tot_ref, runt_ref):
        p = ps_ref[...]
        is_b, is_a, is_f = _gate_lanes(p.shape)
        z = p + gp_ref[0:1, :]
        neg_exp_a = -jnp.exp(gp_ref[1:2, :])
        glog = neg_exp_a * _softplus(z)
        logf = -_softplus(-z)
        out_ref[...] = jnp.where(is_b, _sigmoid(p), jnp.where(is_a, glog, jnp.where(is_f, logf, 0.0)))
        tril, _, tril_c, _, same_c = _block_masks()
        off = jnp.zeros((1, LANE), F32)
        for b in range(nb):
            rows = slice(b * LANE, (b + 1) * LANE)
            blk = out_ref[rows, :]
            ga = jnp.where(is_a[:LANE], blk, 0.0)
            fb = _hdot(tril, jnp.where(is_f[:LANE], blk, 0.0)) + off
            run = fb + _hdot(tril_c, ga)
            run_ref[rows, :] = run
            runt_ref[:, rows] = run.T
            tot_ref[rows, :] = _hdot(same_c, ga)
            off = fb[LANE - 1:LANE, :]

    return pl.pallas_call(
        body, name="gates",
        out_shape=[jax.ShapeDtypeStruct((t, LANE), F32)] * 3 + [jax.ShapeDtypeStruct((LANE, t), F32)],
        compiler_params=_params(),
    )(ps, gparams)


def _gates_bwd(ps, gparams, dcol_g, dtot_g, dcol_f, drow_g, drow_f):
    t = ps.shape[0]
    nb = t // LANE

    def body(ps_ref, gp_ref, dcg_ref, dtg_ref, dcf_ref, drg_ref, drf_ref, dps_ref, sums_ref, dl_ref, d0_ref, tr_ref):
        p = ps_ref[...]
        is_b, is_a, is_f = _gate_lanes(p.shape)
        _, triu, _, triu_c, same_c = _block_masks()
        tr_ref[...] = jnp.zeros_like(tr_ref)
        off = jnp.zeros((1, LANE), F32)
        for b in reversed(range(nb)):
            rows = slice(b * LANE, (b + 1) * LANE)
            tr_ref[HEADS:2 * HEADS, :] = drg_ref[:, rows]
            tr_ref[2 * HEADS:3 * HEADS, :] = drf_ref[:, rows]
            d = dcg_ref[rows, :] + dcf_ref[rows, :] + tr_ref[...].T
            d0_ref[rows, :] = d
            dlf = _hdot(triu, jnp.where(is_f[:LANE], d, 0.0)) + off
            dla = (_hdot(triu_c, jnp.where(is_a[:LANE], d, 0.0))
                   + _hdot(same_c, jnp.where(is_a[:LANE], dtg_ref[rows, :], 0.0)))
            dl_ref[rows, :] = dlf + dla
            off = dlf[0:1, :]
        z = p + gp_ref[0:1, :]
        neg_exp_a = -jnp.exp(gp_ref[1:2, :])
        sb = _sigmoid(p)
        glog = neg_exp_a * _softplus(z)
        dl = dl_ref[...]
        dp = jnp.where(is_b, d0_ref[...] * sb * (1.0 - sb),
                       jnp.where(is_a, dl * neg_exp_a * _sigmoid(z), jnp.where(is_f, dl * _sigmoid(-z), 0.0)))
        dps_ref[...] = dp
        s_a = jnp.sum(jnp.where(is_a, dl * glog, 0.0), axis=0, keepdims=True)
        s_p = jnp.sum(jnp.where(is_b, 0.0, dp), axis=0, keepdims=True)
        row = lax.broadcasted_iota(jnp.int32, (8, LANE), 0)
        from_a = pltpu.roll(jnp.where(row == 0, s_a, jnp.where(row == 1, s_p, 0.0)), LANE - HEADS, 1)
        from_f = pltpu.roll(jnp.where(row == 2, s_p, 0.0), LANE - 2 * HEADS, 1)
        lane = lax.broadcasted_iota(jnp.int32, (8, LANE), 1)
        sums_ref[...] = jnp.where(lane < HEADS, from_a + from_f, 0.0)

    return pl.pallas_call(
        body, name="gates_bwd",
        out_shape=[jax.ShapeDtypeStruct((t, LANE), F32), jax.ShapeDtypeStruct((8, LANE), F32)],
        scratch_shapes=[pltpu.VMEM((t, LANE), F32), pltpu.VMEM((t, LANE), F32), pltpu.VMEM((LANE, LANE), F32)],
        compiler_params=_params(),
    )(ps, gparams, dcol_g, dtot_g, dcol_f, drow_g, drow_f)


def _conv(xv, w):
    acc = w[3:4, :] * xv
    for s in range(1, 4):
        acc += w[3 - s:4 - s, :] * _shift_down(xv, s)
    return acc


PREP_ROWS = 256
HALO = 8
PREP_UNROLL = 8


def _tile_loop(n, tile, init):
    if n % PREP_UNROLL:
        return lax.fori_loop(0, n, tile, init)

    def trip(g, carry):
        for u in range(PREP_UNROLL):
            carry = tile(g * PREP_UNROLL + u, carry)
        return carry

    return lax.fori_loop(0, n // PREP_UNROLL, trip, init)


def _tile_rows(r):
    return pl.ds(pl.multiple_of(r * PREP_ROWS, PREP_ROWS), PREP_ROWS)


def _gdn_prep(pm, conv_w):
    t = pm.shape[0]
    nj = WIDTH // LANE
    win = PREP_ROWS + HALO

    def body(x_ref, w_ref, o_ref, xp_ref):
        kind = pl.program_id(0)
        xp_ref[0:HALO, :] = jnp.zeros((HALO, LANE), F32)
        xp_ref[HALO:, :] = x_ref[...]
        w = w_ref[...]
        hs = _head_sum_matrix()

        def tile(r, _, normed):
            xw = xp_ref[pl.ds(pl.multiple_of(r * PREP_ROWS, PREP_ROWS), win), :]
            acc = _conv(xw, w)[HALO:]
            out = acc * _sigmoid(acc)
            if normed:
                out = out * lax.rsqrt(_group_sum(out * out, hs) + EPS)
            o_ref[0, 0, _tile_rows(r), :] = out[:, :DH]
            o_ref[0, 1, _tile_rows(r), :] = out[:, DH:]
            return 0

        @pl.when(kind < 2)
        def _():
            _tile_loop(t // PREP_ROWS, functools.partial(tile, normed=True), 0)

        @pl.when(kind == 2)
        def _():
            _tile_loop(t // PREP_ROWS, functools.partial(tile, normed=False), 0)

    return pl.pallas_call(
        body, name="gdn_prep", grid=(3, nj),
        in_specs=[pl.BlockSpec((t, LANE), lambda i, j: (0, i * nj + j)),
                  pl.BlockSpec((4, LANE), lambda i, j: (0, i * nj + j))],
        out_specs=pl.BlockSpec((1, 2, t, DH), lambda i, j: (i, j, 0, 0)),
        out_shape=jax.ShapeDtypeStruct((3, HEADS, t, DH), F32),
        scratch_shapes=[pltpu.VMEM((t + HALO, LANE), F32)],
        compiler_params=_params(("arbitrary", "arbitrary")),
    )(pm, conv_w)


def _gdn_prep_bwd(pm, conv_w, dqkv):
    t = pm.shape[0]
    nj = WIDTH // LANE
    win = PREP_ROWS + 2 * HALO

    def body(x_ref, w_ref, d_ref, dx_ref, dw_ref, xp_ref, dp_ref):
        kind = pl.program_id(0)
        zeros = jnp.zeros((HALO, LANE), F32)
        for ref in (xp_ref, dp_ref):
            ref[0:HALO, :] = zeros
            ref[HALO + t:, :] = zeros
        xp_ref[HALO:HALO + t, :] = x_ref[...]
        dp_ref[HALO:HALO + t, 0:DH] = d_ref[0, 0]
        dp_ref[HALO:HALO + t, DH:] = d_ref[0, 1]
        w = w_ref[...]
        hs = _head_sum_matrix()
        rows = lax.broadcasted_iota(jnp.int32, (win, LANE), 0)
        in_tile = (rows >= HALO) & (rows < HALO + PREP_ROWS)

        def tile(r, dw, normed):
            start = pl.multiple_of(r * PREP_ROWS, PREP_ROWS)
            xw = xp_ref[pl.ds(start, win), :]
            dy = dp_ref[pl.ds(start, win), :]
            acc = _conv(xw, w)
            sg = _sigmoid(acc)
            if normed:
                y = acc * sg
                rn = lax.rsqrt(_group_sum(y * y, hs) + EPS)
                yn = y * rn
                dy = rn * (dy - yn * _group_sum(dy * yn, hs))
            dacc = dy * sg * (1.0 + acc * (1.0 - sg))
            dx = w[3:4, :] * dacc
            for s in range(1, 4):
                dx += w[3 - s:4 - s, :] * _shift_up(dacc, s)
            dx_ref[_tile_rows(r), :] = dx[HALO:HALO + PREP_ROWS].astype(dx_ref.dtype)
            dm = jnp.where(in_tile, dacc, 0.0)
            return tuple(dw[i] + jnp.sum(dm * (xw if i == 3 else _shift_down(xw, 3 - i)), axis=0, keepdims=True)
                         for i in range(4))

        def run(normed):
            dw = _tile_loop(t // PREP_ROWS, functools.partial(tile, normed=normed),
                            tuple(jnp.zeros((1, LANE), F32) for _ in range(4)))
            for i in range(4):
                dw_ref[i:i + 1, :] = dw[i]

        pl.when(kind < 2)(functools.partial(run, True))
        pl.when(kind == 2)(functools.partial(run, False))

    return pl.pallas_call(
        body, name="gdn_prep_bwd", grid=(3, nj),
        in_specs=[pl.BlockSpec((t, LANE), lambda i, j: (0, i * nj + j)),
                  pl.BlockSpec((4, LANE), lambda i, j: (0, i * nj + j)),
                  pl.BlockSpec((1, 2, t, DH), lambda i, j: (i, j, 0, 0))],
        out_specs=[pl.BlockSpec((t, LANE), lambda i, j: (0, i * nj + j)),
                   pl.BlockSpec((4, LANE), lambda i, j: (0, i * nj + j))],
        out_shape=[jax.ShapeDtypeStruct((t, 3 * WIDTH), MXU), jax.ShapeDtypeStruct((4, 3 * WIDTH), F32)],
        scratch_shapes=[pltpu.VMEM((t + 2 * HALO, LANE), F32), pltpu.VMEM((t + 2 * HALO, LANE), F32)],
        compiler_params=_params(("arbitrary", "arbitrary")),
    )(pm, conv_w, dqkv)


FOX_COL0 = 4 * WIDTH // LANE


def _fox_prep(pm, nw):
    t = pm.shape[0]
    nj = WIDTH // LANE

    def body(x_ref, w_ref, o_ref):
        kind = pl.program_id(0)
        hs = _head_sum_matrix()
        wk = w_ref[pl.ds(kind, 1), :]

        def tile(r, _, normed):
            out = x_ref[_tile_rows(r), :]
            if normed:
                out = out * lax.rsqrt(_group_sum(out * out, hs) * (1.0 / DH) + EPS) * wk
            o_ref[0, 0, _tile_rows(r), :] = out[:, :DH]
            o_ref[0, 1, _tile_rows(r), :] = out[:, DH:]
            return 0

        @pl.when(kind < 2)
        def _():
            _tile_loop(t // PREP_ROWS, functools.partial(tile, normed=True), 0)

        @pl.when(kind == 2)
        def _():
            _tile_loop(t // PREP_ROWS, functools.partial(tile, normed=False), 0)

    return pl.pallas_call(
        body, name="fox_prep", grid=(3, nj),
        in_specs=[pl.BlockSpec((t, LANE), lambda i, j: (0, FOX_COL0 + i * nj + j)),
                  pl.BlockSpec((3, LANE), lambda i, j: (0, 0))],
        out_specs=pl.BlockSpec((1, 2, t, DH), lambda i, j: (i, j, 0, 0)),
        out_shape=jax.ShapeDtypeStruct((3, HEADS, t, DH), F32),
        compiler_params=_params(("arbitrary", "arbitrary")),
    )(pm, nw)


def _fox_prep_bwd(pm, nw, dqkv):
    t = pm.shape[0]
    nj = WIDTH // LANE

    def body(x_ref, w_ref, d_ref, dx_ref, dw_ref):
        kind = pl.program_id(0)
        hs = _head_sum_matrix()
        wk = w_ref[pl.ds(kind, 1), :]

        def tile(r, dw):
            xv = x_ref[_tile_rows(r), :]
            rn = lax.rsqrt(_group_sum(xv * xv, hs) * (1.0 / DH) + EPS)
            xn = xv * rn
            d = jnp.concatenate([d_ref[0, 0, _tile_rows(r), :], d_ref[0, 1, _tile_rows(r), :]], axis=1)
            g = d * wk
            dx_ref[_tile_rows(r), :] = (rn * (g - xn * _group_sum(g * xn, hs) * (1.0 / DH))).astype(dx_ref.dtype)
            return dw + jnp.sum(d * xn, axis=0, keepdims=True)

        def copy_tile(r, _):
            dx_ref[_tile_rows(r), :] = jnp.concatenate([d_ref[0, 0, _tile_rows(r), :], d_ref[0, 1, _tile_rows(r), :]],
                                                       axis=1).astype(dx_ref.dtype)
            return 0

        @pl.when(kind < 2)
        def _():
            dw_ref[0, 0] = _tile_loop(t // PREP_ROWS, tile, jnp.zeros((1, LANE), F32))

        @pl.when(kind == 2)
        def _():
            _tile_loop(t // PREP_ROWS, copy_tile, 0)
            dw_ref[0, 0] = jnp.zeros((1, LANE), F32)

    return pl.pallas_call(
        body, name="fox_prep_bwd", grid=(3, nj),
        in_specs=[pl.BlockSpec((t, LANE), lambda i, j: (0, FOX_COL0 + i * nj + j)),
                  pl.BlockSpec((3, LANE), lambda i, j: (0, 0)),
                  pl.BlockSpec((1, 2, t, DH), lambda i, j: (i, j, 0, 0))],
        out_specs=[pl.BlockSpec((t, LANE), lambda i, j: (0, i * nj + j)),
                   pl.BlockSpec((1, 1, 1, LANE), lambda i, j: (i, j, 0, 0))],
        out_shape=[jax.ShapeDtypeStruct((t, 3 * WIDTH), MXU), jax.ShapeDtypeStruct((3, nj, 1, LANE), F32)],
        compiler_params=_params(("arbitrary", "arbitrary")),
    )(pm, nw, dqkv)


SC = 256
CPS = SC // CHUNK
GDN_HP_FWD = 4
GDN_HP_BWD = 2
Q_SCALE = DH ** -0.5


def _sc_masks():
    ri = lax.broadcasted_iota(jnp.int32, (SC, SC), 0)
    ci = lax.broadcasted_iota(jnp.int32, (SC, SC), 1)
    same = (ri // CHUNK) == (ci // CHUNK)
    return same & (ri >= ci), same & (ri > ci), ri == ci


def _unit_lower_inverses(ms, eye):
    invs = [jnp.where(eye, 1.0, 0.0) + m for m in ms]
    ms = [_dot(m, m) for m in ms]
    for _ in range(4):
        both = [_dot(jnp.concatenate([inv, m], axis=0), m) for inv, m in zip(invs, ms)]
        invs = [inv + b[:SC] for inv, b in zip(invs, both)]
        ms = [b[SC:] for b in both]
    return [inv + _dot(inv, m) for inv, m in zip(invs, ms)]


def _lane_col(blk, lane_idx):
    lane = lax.broadcasted_iota(jnp.int32, blk.shape, 1)
    return jnp.sum(jnp.where(lane == lane_idx, blk, 0.0), axis=1, keepdims=True)


def _to_lane(col, lane_idx):
    lane = lax.broadcasted_iota(jnp.int32, (col.shape[0], LANE), 1)
    return jnp.where(lane == lane_idx, col, 0.0)


def _gdn_columns(gates_ref, run_ref, tot_ref, runt_ref, rows, h):
    return (_lane_col(gates_ref[rows, :], h), _lane_col(run_ref[rows, :], HEADS + h),
            _lane_col(tot_ref[rows, :], HEADS + h), runt_ref[pl.ds(h, 1), rows])


def _gdn_local(q, k, beta, gc, gl, grow, causal, with_kk=True):
    decay = jnp.exp(jnp.where(causal, gc - grow, -1e30))
    egc = jnp.exp(gc)
    ekd = jnp.exp(gl - gc)
    qs = q * Q_SCALE
    kb = k * beta
    if with_kk:
        both = _dot_nt(jnp.concatenate([kb, qs], axis=0), k)
        kk, qk = both[:SC], both[SC:]
    else:
        kk, qk = None, _dot_nt(qs, k)
    return beta, gl, decay, egc, ekd, qs, kb, kk, qk, jnp.where(causal, qk * decay, 0.0)


def _chunk_rows(c):
    return pl.ds(c * CHUNK if isinstance(c, int) else pl.multiple_of(c * CHUNK, CHUNK), CHUNK)


def _sc_rows(b):
    return pl.ds(b * SC if isinstance(b, int) else pl.multiple_of(b * SC, SC), SC)


def _gdn_fwd(qkv, gates, run, tot, run_t):
    t = qkv.shape[2]
    nc = t // CHUNK
    nsc = t // SC

    hp_n = GDN_HP_FWD
    heads = range(hp_n)

    def body(qkv_ref, gates_ref, run_ref, tot_ref, runt_ref, o_ref, st_ref, inv_ref, kc_s, qc_s, g_s, au_s):
        hp = pl.program_id(0)
        causal, strict, eye = _sc_masks()

        def local(b, _):
            rows = _sc_rows(b)
            loc = [_gdn_local(qkv_ref[0, hh, rows, :], qkv_ref[1, hh, rows, :],
                              *_gdn_columns(gates_ref, run_ref, tot_ref, runt_ref, rows, hp * hp_n + hh), causal)
                   for hh in heads]
            invs = _unit_lower_inverses([-jnp.where(strict, l[7] * l[2], 0.0) for l in loc], eye)
            uws = []
            for hh in heads:
                beta, _, _, egc, _, _, kb, _, _, _ = loc[hh]
                inv_ref[hh, rows, :] = invs[hh].astype(inv_ref.dtype)
                uws.append(_dot(invs[hh], jnp.concatenate([qkv_ref[2, hh, rows, :] * beta, kb * egc], axis=1)))
            for hh in heads:
                _, _, _, egc, ekd, qs, _, _, _, attn = loc[hh]
                auw = _dot(attn, uws[hh])
                g_s[hh, rows, :] = qs * egc - auw[:, DH:]
                au_s[hh, rows, :] = auw[:, :DH]
                kd = qkv_ref[1, hh, rows, :] * ekd
                for j in range(CPS):
                    sl = slice(j * CHUNK, (j + 1) * CHUNK)
                    both = _dot_tn(kd[sl], uws[hh][sl])
                    kc_s[hh, b * CPS + j] = both[:, DH:]
                    qc_s[hh, b * CPS + j] = both[:, :DH]
            return 0

        def step(c, states):
            rows = _chunk_rows(c)
            tot_row = tot_ref[pl.ds(c * CHUNK, 1), :]
            new = []
            for hh in heads:
                s = states[hh]
                st_ref[hh, c] = s
                o_ref[hh, rows, :] = _dot(g_s[hh, rows, :], s) + au_s[hh, rows, :]
                egl = jnp.exp(_lane_col(tot_row, HEADS + hp * hp_n + hh))
                new.append(egl * s - _dot(kc_s[hh, c], s) + qc_s[hh, c])
            return tuple(new)

        def steps_of(b, states):
            for j in range(CPS):
                states = step(b * CPS + j, states)
            return states

        def fused(b, states):
            states = steps_of(b - 1, states)
            local(b, 0)
            return states

        local(0, 0)
        states = lax.fori_loop(1, nsc, fused, tuple(jnp.zeros((DH, DH), F32) for _ in heads))
        steps_of(nsc - 1, states)

    whole = pl.BlockSpec((t, LANE), lambda h: (0, 0))
    once = dict(pipeline_mode=pl.Buffered(1))
    return pl.pallas_call(
        body, name="gdn_fwd", grid=(HEADS // hp_n,),
        in_specs=[pl.BlockSpec((3, hp_n, t, DH), lambda h: (0, h, 0, 0), **once), whole, whole, whole,
                  pl.BlockSpec((HEADS, t), lambda h: (1, 0))],
        out_specs=[pl.BlockSpec((hp_n, t, DH), lambda h: (h, 0, 0), **once),
                   pl.BlockSpec((hp_n, nc, DH, DH), lambda h: (h, 0, 0, 0), **once),
                   pl.BlockSpec((hp_n, t, SC), lambda h: (h, 0, 0), **once)],
        out_shape=[jax.ShapeDtypeStruct((HEADS, t, DH), F32), jax.ShapeDtypeStruct((HEADS, nc, DH, DH), F32),
                   jax.ShapeDtypeStruct((HEADS, t, SC), MXU)],
        scratch_shapes=[pltpu.VMEM((hp_n, nc, DH, DH), F32), pltpu.VMEM((hp_n, nc, DH, DH), F32),
                        pltpu.VMEM((hp_n, t, DH), F32), pltpu.VMEM((hp_n, t, DH), F32)],
        compiler_params=_params(("arbitrary",)),
    )(qkv, gates, run, tot, run_t)


def _gdn_bwd(qkv, gates, run, tot, run_t, inv, states, do):
    t = qkv.shape[2]
    nc = t // CHUNK
    nsc = t // SC

    hp_n = GDN_HP_BWD
    heads = range(hp_n)

    def body(qkv_ref, gates_ref, run_ref, tot_ref, runt_ref, inv_ref, st_ref, do_ref,
             dqkv_ref, dcol_ref, dtot_ref, drow_ref, uw_s, kc_s, h_s, dsn_s):
        hp = pl.program_id(0)
        causal, strict, _ = _sc_masks()

        @pl.when(hp == 0)
        def _():
            dcol_ref[...] = jnp.zeros_like(dcol_ref)
            dtot_ref[...] = jnp.zeros_like(dtot_ref)

        def local_of(hh, rows, with_kk=True):
            return _gdn_local(qkv_ref[0, hh, rows, :], qkv_ref[1, hh, rows, :],
                              *_gdn_columns(gates_ref, run_ref, tot_ref, runt_ref, rows, hp * hp_n + hh), causal, with_kk)

        def local(b, _):
            rows = _sc_rows(b)
            loc = [local_of(hh, rows, with_kk=False) for hh in heads]
            us, ws = [], []
            for hh in heads:
                beta, _, _, egc, _, _, kb, _, _, _ = loc[hh]
                inv_b = inv_ref[hh, rows, :]
                us.append(_dot(inv_b, qkv_ref[2, hh, rows, :] * beta))
                ws.append(_dot(inv_b, kb * egc))
            gs = [loc[hh][5] * loc[hh][3] - _dot(loc[hh][9], ws[hh]) for hh in heads]
            for hh in heads:
                uw_s[hh, rows, :] = jnp.concatenate([us[hh], ws[hh]], axis=1)
                kd = qkv_ref[1, hh, rows, :] * loc[hh][4]
                dout = do_ref[hh, rows, :]
                for j in range(CPS):
                    sl = slice(j * CHUNK, (j + 1) * CHUNK)
                    kc_s[hh, b * CPS + j] = _dot_tn(kd[sl], ws[hh][sl])
                    h_s[hh, b * CPS + j] = _dot_tn(gs[hh][sl], dout[sl])
            return 0

        def step(c, dss):
            tot_row = tot_ref[pl.ds(c * CHUNK, 1), :]
            new = []
            for hh in heads:
                ds = dss[hh]
                dsn_s[hh, c] = ds
                egl = jnp.exp(_lane_col(tot_row, HEADS + hp * hp_n + hh))
                new.append(egl * ds - _dot_tn(kc_s[hh, c], ds) + h_s[hh, c])
            return tuple(new)

        def steps_of(b, dss):
            for j in reversed(range(CPS)):
                dss = step(b * CPS + j, dss)
            return dss

        def fused(i, dss):
            b = nsc - 2 - i
            dss = steps_of(b + 1, dss)
            local(b, 0)
            return dss

        local(nsc - 1, 0)
        dss = lax.fori_loop(0, nsc - 1, fused, tuple(jnp.zeros((DH, DH), F32) for _ in heads))
        steps_of(0, dss)

        def back(b, _):
            rows = _sc_rows(b)
            first = lax.broadcasted_iota(jnp.int32, (CHUNK, 1), 0) == 0
            loc = [local_of(hh, rows) for hh in heads]
            sign = jnp.where(lax.broadcasted_iota(jnp.int32, (1, LANE), 1) < DH, 1.0, -1.0)
            mid = []
            for hh in heads:
                beta, gl, decay, egc, ekd, qs, kb, kk, qk, attn = loc[hh]
                uw = uw_s[hh, rows, :]
                kd = qkv_ref[1, hh, rows, :] * ekd
                dout = do_ref[hh, rows, :]
                dg_p, dkd_p, duw_p, dgl_p = [], [], [], []
                for j in range(CPS):
                    sl = slice(j * CHUNK, (j + 1) * CHUNK)
                    s = st_ref[hh, b * CPS + j]
                    dsn = dsn_s[hh, b * CPS + j]
                    both = _dot_nt(jnp.concatenate([dsn, dout[sl]], axis=0), s)
                    dg_p.append(both[CHUNK:])
                    dsc = jnp.concatenate([dsn, -both[:CHUNK]], axis=1)
                    dkd_p.append(_dot_nt(uw[sl], dsc))
                    duw_p.append(_dot(kd[sl], dsc))
                    degl = jnp.sum(jnp.sum(s * dsn, axis=1, keepdims=True), axis=0, keepdims=True)
                    dgl_p.append(jnp.where(first, degl * jnp.exp(gl[j * CHUNK:j * CHUNK + 1, :]), 0.0))
                dg, dkd = jnp.concatenate(dg_p, axis=0), jnp.concatenate(dkd_p, axis=0)
                dod = jnp.concatenate([dout, dg], axis=1)
                da = jnp.where(causal, _dot_nt(dod, uw * sign), 0.0)
                duw = jnp.concatenate(duw_p, axis=0) + _dot_tn(attn, dod) * sign
                mid.append((kd, dg, dkd, da, duw, jnp.concatenate(dgl_p, axis=0)))
            inv_ts = [inv_ref[hh, rows, :].astype(F32).T for hh in heads]
            its = [_dot(inv_ts[hh], mid[hh][4]) for hh in heads]
            dinvs = []
            for hh in heads:
                beta, _, _, egc, _, _, kb, _, _, _ = loc[hh]
                dinvs.append(_dot_nt(mid[hh][4], jnp.concatenate([qkv_ref[2, hh, rows, :] * beta, kb * egc], axis=1)))
            half = [_dot(inv_ts[hh], dinvs[hh]) for hh in heads]
            dls = [jnp.where(strict, -_dot(half[hh], inv_ts[hh]), 0.0) for hh in heads]
            for hh in heads:
                head = hp * hp_n + hh
                beta, gl, decay, egc, ekd, qs, kb, kk, qk, attn = loc[hh]
                kd, dg, dkd, da, _, dgl_first = mid[hh]
                k, v = qkv_ref[1, hh, rows, :], qkv_ref[2, hh, rows, :]
                dvb, dkbe = its[hh][:, :DH], its[hh][:, DH:]
                dl = dls[hh]
                dlogd = (dl * kk + da * qk) * decay
                dd = jnp.concatenate([dl * decay, da * decay], axis=0)
                ddk = _dot(dd, k)
                dkb = ddk[:SC] + dkbe * egc
                dqs = ddk[SC:] + dg * egc
                dk = _dot_tn(dd, jnp.concatenate([kb, qs], axis=0)) + dkd * ekd + dkb * beta
                dkd_kd = jnp.sum(dkd * kd, axis=1, keepdims=True)
                narrow = dg * (qs * egc) + dkbe * (kb * egc)
                wide = dlogd[:, :LANE] + dlogd[:, LANE:] + jnp.concatenate([narrow, jnp.zeros((SC, LANE - DH), F32)], axis=1)
                dgc = jnp.sum(wide, axis=1, keepdims=True) - dkd_kd
                dbeta = jnp.sum(dkb * k + dvb * v, axis=1, keepdims=True)
                dqkv_ref[0, hh, rows, :] = dqs * Q_SCALE
                dqkv_ref[1, hh, rows, :] = dk
                dqkv_ref[2, hh, rows, :] = dvb * beta
                dcol_ref[rows, :] += _to_lane(dbeta, head) + _to_lane(dgc, HEADS + head)
                dtot_ref[rows, :] += _to_lane(dkd_kd + dgl_first, HEADS + head)
                drow_ref[pl.ds(head, 1), rows] = -jnp.sum(dlogd, axis=0, keepdims=True)
            return 0

        lax.fori_loop(0, nsc, back, 0)

    whole = pl.BlockSpec((t, LANE), lambda h: (0, 0))
    rowspec = pl.BlockSpec((HEADS, t), lambda h: (0, 0))
    sq = pltpu.VMEM((hp_n, nc, DH, DH), F32)
    per_head = pltpu.VMEM((hp_n, t, 2 * DH), F32)
    return pl.pallas_call(
        body, name="gdn_bwd", grid=(HEADS // hp_n,),
        in_specs=[pl.BlockSpec((3, hp_n, t, DH), lambda h: (0, h, 0, 0)), whole, whole, whole,
                  pl.BlockSpec((HEADS, t), lambda h: (1, 0)),
                  pl.BlockSpec((hp_n, t, SC), lambda h: (h, 0, 0)), pl.BlockSpec((hp_n, nc, DH, DH), lambda h: (h, 0, 0, 0)),
                  pl.BlockSpec((hp_n, t, DH), lambda h: (h, 0, 0))],
        out_specs=[pl.BlockSpec((3, hp_n, t, DH), lambda h: (0, h, 0, 0)), whole, whole, rowspec],
        out_shape=[jax.ShapeDtypeStruct((3, HEADS, t, DH), F32), jax.ShapeDtypeStruct((t, LANE), F32),
                   jax.ShapeDtypeStruct((t, LANE), F32), jax.ShapeDtypeStruct((HEADS, t), F32)],
        scratch_shapes=[per_head, sq, sq, sq],
        compiler_params=_params(("arbitrary",)),
    )(qkv, gates, run, tot, run_t, inv, states, do)


FOX_HP_FWD = 4
FOX_HP_BWD = 4


def _wide_t(a):
    r = a.shape[0]
    return jnp.concatenate([a, jnp.zeros((r, LANE - DH), F32)], axis=1).T[:DH]


def _tall_t(a):
    r = a.shape[1]
    return jnp.concatenate([a, jnp.zeros((LANE - DH, r), F32)], axis=0).T[:, :DH]


def _key_side_f(run_blk, head, qb):
    col = jnp.broadcast_to(_lane_col(run_blk, 2 * HEADS + head), (run_blk.shape[0], LANE))
    return jnp.concatenate([col] * (qb // LANE), axis=1)


def _diag_mask(qb):
    return lax.broadcasted_iota(jnp.int32, (qb, qb), 0) <= lax.broadcasted_iota(jnp.int32, (qb, qb), 1)


def _fox_fwd(qkv, run, run_t, carry=None):
    t = qkv.shape[2]
    qb = min(QB, t)
    nq = t // qb
    hp_n = FOX_HP_FWD
    c_in, c_in_specs, c_out_shape, c_out_specs, c_sem = _carry_operands(*carry) if carry else ([], [], [], [], None)
    n_c = len(c_in)

    def body(*refs):
        q_ref, k_ref, v_ref, run_ref, runt_ref = refs[:5]
        o_ref, lse_ref = refs[5 + n_c:7 + n_c]
        s0 = 7 + n_c + len(c_out_shape)
        kb_s, vt_s, raw_s, m_s, l_s, acc_s = refs[s0:s0 + 6]
        hp = pl.program_id(0)
        i = pl.program_id(1)

        if carry:
            _carry_step(carry[0], refs[5:5 + n_c], refs[7 + n_c], refs[8 + n_c], refs[-1], hp * nq + i,
                        (HEADS // hp_n) * nq)

        @pl.when(i == 0)
        def _():
            for hh in range(hp_n):
                kb_s[hh] = k_ref[0, hh].astype(MXU)
                for b in range(nq):
                    rows = slice(b * qb, (b + 1) * qb)
                    vt_s[hh, :, rows] = _wide_t(v_ref[0, hh, rows, :]).astype(MXU)

        qrows = pl.ds(pl.multiple_of(i * qb, qb), qb)
        qs = [(q_ref[0, hh] * Q_SCALE).astype(MXU) for hh in range(hp_n)]
        fq = [runt_ref[pl.ds(hp * hp_n + hh, 1), qrows] for hh in range(hp_n)]

        def block_rows(j):
            return pl.ds(pl.multiple_of(j * qb, qb), qb)

        def scores(j):
            for hh in range(hp_n):
                raw_s[j % 2, hh] = _dot_nt(kb_s[hh, block_rows(j), :], qs[hh])

        def absorb(j, diagonal):
            rows = block_rows(j)
            run_blk = run_ref[rows, :]
            stats, pv = [], []
            for hh in range(hp_n):
                m, l = m_s[hh], l_s[hh]
                st = raw_s[j % 2, hh] + (fq[hh] - _key_side_f(run_blk, hp * hp_n + hh, qb))
                if diagonal:
                    st = jnp.where(_diag_mask(qb), st, -1e30)
                m_new = jnp.maximum(m, jnp.max(st, axis=0, keepdims=True))
                p = jnp.exp(st - m_new)
                alpha = jnp.exp(m - m_new)
                stats.append((m_new, alpha * l + jnp.sum(p, axis=0, keepdims=True), alpha))
                pv.append(_dot(vt_s[hh, :, rows], p))
            for hh in range(hp_n):
                m_s[hh], l_s[hh] = stats[hh][0], stats[hh][1]
                acc_s[hh] = stats[hh][2] * acc_s[hh] + pv[hh]

        def kstep(j, _):
            scores(j + 1)
            absorb(j, False)
            return 0

        for hh in range(hp_n):
            m_s[hh] = jnp.full((1, qb), -1e30, F32)
            l_s[hh] = jnp.zeros((1, qb), F32)
            acc_s[hh] = jnp.zeros((DH, qb), F32)
        scores(0)
        lax.fori_loop(0, i, kstep, 0)
        absorb(i, True)
        for hh in range(hp_n):
            l = l_s[hh]
            o_ref[hh] = _tall_t(acc_s[hh] / l)
            lse_ref[pl.ds(hp * hp_n + hh, 1), qrows] = m_s[hh] + jnp.log(l)

    out = pl.pallas_call(
        body, name="fox_fwd", grid=(HEADS // hp_n, nq),
        in_specs=[pl.BlockSpec((1, hp_n, qb, DH), lambda h, i: (0, h, i, 0)),
                  pl.BlockSpec((1, hp_n, t, DH), lambda h, i: (1, h, 0, 0)),
                  pl.BlockSpec((1, hp_n, t, DH), lambda h, i: (2, h, 0, 0)),
                  pl.BlockSpec((t, LANE), lambda h, i: (0, 0)),
                  pl.BlockSpec((HEADS, t), lambda h, i: (2, 0))] + c_in_specs,
        out_specs=[pl.BlockSpec((hp_n, qb, DH), lambda h, i: (h, i, 0)),
                   pl.BlockSpec((HEADS, t), lambda h, i: (0, 0))] + c_out_specs,
        out_shape=[jax.ShapeDtypeStruct((HEADS, t, DH), F32), jax.ShapeDtypeStruct((HEADS, t), F32)] + c_out_shape,
        input_output_aliases={5 + j: 4 + j for j in range(n_c)},
        scratch_shapes=[pltpu.VMEM((hp_n, t, DH), MXU), pltpu.VMEM((hp_n, DH, t), MXU), pltpu.VMEM((2, hp_n, qb, qb), F32),
                        pltpu.VMEM((hp_n, 1, qb), F32), pltpu.VMEM((hp_n, 1, qb), F32), pltpu.VMEM((hp_n, DH, qb), F32)]
        + ([c_sem] if carry else []),
        compiler_params=_params(("arbitrary", "arbitrary")),
    )(qkv, qkv, qkv, run, run_t, *c_in)
    return (out[0], out[1], _carry_state(out[2:])) if carry else (out[0], out[1])


def _fox_bwd(qkv, run, run_t, o, lse, do, carry=None):
    t = qkv.shape[2]
    qb = min(QB, t)
    nq = t // qb
    hp_n = FOX_HP_BWD
    c_in, c_in_specs, c_out_shape, c_out_specs, c_sem = _carry_operands(*carry) if carry else ([], [], [], [], None)
    n_c = len(c_in)

    def body(*refs):
        q_ref, k_ref, v_ref, run_ref, runt_ref, o_ref, lse_ref, do_ref = refs[:8]
        dqkv_ref, dcol_ref, drow_ref = refs[8 + n_c:11 + n_c]
        s0 = 11 + n_c + len(c_out_shape)
        dqt_s, raw_s, dpt_s, dro_s, dk_s, dv_s, dsum_s = refs[s0:s0 + 7]
        hp = pl.program_id(0)
        j = pl.program_id(1)

        if carry:
            _carry_step(carry[0], refs[8:8 + n_c], refs[11 + n_c], refs[12 + n_c], refs[-1], hp * nq + j,
                        (HEADS // hp_n) * nq)

        @pl.when(j == 0)
        def _():
            dqt_s[...] = jnp.zeros_like(dqt_s)

        @pl.when((j == 0) & (hp == 0))
        def _():
            dcol_ref[...] = jnp.zeros_like(dcol_ref)
            drow_ref[...] = jnp.zeros_like(drow_ref)

        krows = pl.ds(pl.multiple_of(j * qb, qb), qb)
        run_blk = run_ref[krows, :]
        ones8 = jnp.ones((8, DH), MXU)
        kb, kt, vb, fk = [], [], [], []
        for hh in range(hp_n):
            kf = k_ref[0, hh]
            kb.append(kf.astype(MXU))
            kt.append(_wide_t(kf).astype(MXU))
            vb.append(v_ref[0, hh].astype(MXU))
            fk.append(_key_side_f(run_blk, hp * hp_n + hh, qb))

        def block_rows(i):
            return pl.ds(i * qb if isinstance(i, int) else pl.multiple_of(i * qb, qb), qb)

        def products(i):
            rows = block_rows(i)
            slot = i % 2
            for hh in range(hp_n):
                dout = do_ref[hh, rows, :]
                x = dout * o_ref[hh, rows, :]
                x_hi = x.astype(MXU)
                raw_s[slot, hh] = _dot_nt(kb[hh], q_ref[0, hh, rows, :] * Q_SCALE)
                dpt_s[slot, hh] = _dot_nt(vb[hh], dout)
                dro_s[slot, hh] = _dot_nt(ones8, x_hi) + _dot_nt(ones8, x - x_hi.astype(F32))

        def absorb(i, diagonal):
            rows = block_rows(i)
            slot = i % 2
            pieces = []
            for hh in range(hp_n):
                head = hp * hp_n + hh
                raw, dpt, drow = raw_s[slot, hh], dpt_s[slot, hh], dro_s[slot, hh, 0:1, :]
                off = runt_ref[pl.ds(head, 1), rows] - lse_ref[pl.ds(head, 1), rows]
                st = raw + (off - fk[hh])
                if diagonal:
                    st = jnp.where(_diag_mask(qb), st, -1e30)
                pt = jnp.exp(st)
                dst = pt * (dpt - drow)
                drow_ref[pl.ds(head, 1), rows] += jnp.sum(dst, axis=0, keepdims=True)
                folded = dst[:, 0:LANE]
                for c in range(1, qb // LANE):
                    folded = folded + dst[:, c * LANE:(c + 1) * LANE]
                pieces.append((_dot(dst, q_ref[0, hh, rows, :] * Q_SCALE), _dot(pt, do_ref[hh, rows, :]),
                               _dot(kt[hh], dst), folded))
            for hh in range(hp_n):
                dqt_s[hh, :, rows] += pieces[hh][2]
                if diagonal:
                    dk_s[hh], dv_s[hh], dsum_s[hh] = pieces[hh][0], pieces[hh][1], pieces[hh][3]
                else:
                    dk_s[hh] += pieces[hh][0]
                    dv_s[hh] += pieces[hh][1]
                    dsum_s[hh] += pieces[hh][3]

        def qstep(i, _):
            products(i + 1)
            absorb(i, False)
            return 0

        products(j)
        products(jnp.minimum(j + 1, nq - 1))
        absorb(j, True)
        lax.fori_loop(j + 1, nq - 1, qstep, 0)

        @pl.when(j < nq - 1)
        def _():
            absorb(nq - 1, False)
        for hh in range(hp_n):
            dqkv_ref[1, hh, krows, :] = dk_s[hh]
            dqkv_ref[2, hh, krows, :] = dv_s[hh]
            dcol_ref[krows, :] += _to_lane(-jnp.sum(dsum_s[hh], axis=1, keepdims=True), 2 * HEADS + hp * hp_n + hh)

        @pl.when(j == nq - 1)
        def _():
            for hh in range(hp_n):
                for b in range(nq):
                    rows = slice(b * qb, (b + 1) * qb)
                    dqkv_ref[0, hh, rows, :] = _tall_t(dqt_s[hh, :, rows]) * Q_SCALE

    once = dict(pipeline_mode=pl.Buffered(1))
    full = pl.BlockSpec((hp_n, t, DH), lambda h, j: (h, 0, 0), **once)
    rows8 = pl.BlockSpec((HEADS, t), lambda h, j: (0, 0))
    out = pl.pallas_call(
        body, name="fox_bwd", grid=(HEADS // hp_n, nq),
        in_specs=[pl.BlockSpec((1, hp_n, t, DH), lambda h, j: (0, h, 0, 0), **once),
                  pl.BlockSpec((1, hp_n, qb, DH), lambda h, j: (1, h, j, 0)),
                  pl.BlockSpec((1, hp_n, qb, DH), lambda h, j: (2, h, j, 0)),
                  pl.BlockSpec((t, LANE), lambda h, j: (0, 0), **once), pl.BlockSpec((HEADS, t), lambda h, j: (2, 0)),
                  full, rows8, full] + c_in_specs,
        out_specs=[pl.BlockSpec((3, hp_n, t, DH), lambda h, j: (0, h, 0, 0), **once),
                   pl.BlockSpec((t, LANE), lambda h, j: (0, 0)), rows8] + c_out_specs,
        out_shape=[jax.ShapeDtypeStruct((3, HEADS, t, DH), F32), jax.ShapeDtypeStruct((t, LANE), F32),
                   jax.ShapeDtypeStruct((HEADS, t), F32)] + c_out_shape,
        input_output_aliases={8 + j: 5 + j for j in range(n_c)},
        scratch_shapes=[pltpu.VMEM((hp_n, DH, t), F32), pltpu.VMEM((2, hp_n, qb, qb), F32), pltpu.VMEM((2, hp_n, qb, qb), F32),
                        pltpu.VMEM((2, hp_n, 8, qb), F32), pltpu.VMEM((hp_n, qb, DH), F32), pltpu.VMEM((hp_n, qb, DH), F32),
                        pltpu.VMEM((hp_n, qb, LANE), F32)] + ([c_sem] if carry else []),
        compiler_params=_params(("arbitrary", "arbitrary")),
    )(qkv, qkv, qkv, run, run_t, o, lse, do, *c_in)
    return (out[0], out[1], out[2], _carry_state(out[3:])) if carry else tuple(out)


Z_COL0 = 3 * WIDTH // LANE
FGATE_COL0 = 7 * WIDTH // LANE


def _gdn_post(o, pm, onw):
    t = pm.shape[0]

    def body(o_ref, z_ref, w_ref, m_ref):
        z = z_ref[...]
        sz = z * _sigmoid(z)
        halves = []
        for hh in range(2):
            ov = o_ref[hh]
            n = ov * lax.rsqrt(jnp.mean(ov * ov, axis=-1, keepdims=True) + EPS) * w_ref[...]
            halves.append(n * sz[:, hh * DH:(hh + 1) * DH])
        m_ref[...] = jnp.concatenate(halves, axis=1).astype(m_ref.dtype)

    return pl.pallas_call(
        body, name="gdn_post", grid=(WIDTH // LANE,),
        in_specs=[pl.BlockSpec((2, t, DH), lambda j: (j, 0, 0)), pl.BlockSpec((t, LANE), lambda j: (0, Z_COL0 + j)),
                  pl.BlockSpec((1, DH), lambda j: (0, 0))],
        out_specs=pl.BlockSpec((t, LANE), lambda j: (0, j)),
        out_shape=jax.ShapeDtypeStruct((t, WIDTH), MXU),
        compiler_params=_params(("arbitrary",)),
    )(o, pm, onw)


def _gdn_post_bwd(o, pm, onw, dmix):
    t = pm.shape[0]

    def body(o_ref, z_ref, w_ref, dm_ref, do_ref, dz_ref, dw_ref):
        @pl.when(pl.program_id(0) == 0)
        def _():
            dw_ref[...] = jnp.zeros_like(dw_ref)

        z = z_ref[...]
        sg = _sigmoid(z)
        sz = z * sg
        dsz = sg * (1.0 + z * (1.0 - sg))
        dm = dm_ref[...]
        for hh in range(2):
            cols = slice(hh * DH, (hh + 1) * DH)
            ov = o_ref[hh]
            r = lax.rsqrt(jnp.mean(ov * ov, axis=-1, keepdims=True) + EPS)
            xn = ov * r
            dmh = dm[:, cols]
            dn = dmh * sz[:, cols]
            dz_ref[:, cols] = (dmh * (xn * w_ref[...]) * dsz[:, cols]).astype(dz_ref.dtype)
            dw_ref[...] += jnp.sum(dn * xn, axis=0, keepdims=True)
            g = dn * w_ref[...]
            do_ref[hh] = r * (g - xn * jnp.mean(g * xn, axis=-1, keepdims=True))

    return pl.pallas_call(
        body, name="gdn_post_bwd", grid=(WIDTH // LANE,),
        in_specs=[pl.BlockSpec((2, t, DH), lambda j: (j, 0, 0)), pl.BlockSpec((t, LANE), lambda j: (0, Z_COL0 + j)),
                  pl.BlockSpec((1, DH), lambda j: (0, 0)), pl.BlockSpec((t, LANE), lambda j: (0, j))],
        out_specs=[pl.BlockSpec((2, t, DH), lambda j: (j, 0, 0)), pl.BlockSpec((t, LANE), lambda j: (0, j)),
                   pl.BlockSpec((1, DH), lambda j: (0, 0))],
        out_shape=[jax.ShapeDtypeStruct((HEADS, t, DH), F32), jax.ShapeDtypeStruct((t, WIDTH), MXU),
                   jax.ShapeDtypeStruct((1, DH), F32)],
        compiler_params=_params(("arbitrary",)),
    )(o, pm, onw, dmix)


def _fox_post(o, pm):
    t = pm.shape[0]

    def body(o_ref, g_ref, m_ref):
        m_ref[...] = (jnp.concatenate([o_ref[0], o_ref[1]], axis=1) * _sigmoid(g_ref[...])).astype(m_ref.dtype)

    return pl.pallas_call(
        body, name="fox_post", grid=(WIDTH // LANE,),
        in_specs=[pl.BlockSpec((2, t, DH), lambda j: (j, 0, 0)), pl.BlockSpec((t, LANE), lambda j: (0, FGATE_COL0 + j))],
        out_specs=pl.BlockSpec((t, LANE), lambda j: (0, j)),
        out_shape=jax.ShapeDtypeStruct((t, WIDTH), MXU),
        compiler_params=_params(("arbitrary",)),
    )(o, pm)


def _fox_post_bwd(o, pm, dmix):
    t = pm.shape[0]

    def body(o_ref, g_ref, dm_ref, do_ref, dg_ref):
        sg = _sigmoid(g_ref[...])
        dm = dm_ref[...]
        for hh in range(2):
            cols = slice(hh * DH, (hh + 1) * DH)
            do_ref[hh] = dm[:, cols] * sg[:, cols]
            dg_ref[:, cols] = (dm[:, cols] * o_ref[hh] * (sg * (1.0 - sg))[:, cols]).astype(dg_ref.dtype)

    return pl.pallas_call(
        body, name="fox_post_bwd", grid=(WIDTH // LANE,),
        in_specs=[pl.BlockSpec((2, t, DH), lambda j: (j, 0, 0)), pl.BlockSpec((t, LANE), lambda j: (0, FGATE_COL0 + j)),
                  pl.BlockSpec((t, LANE), lambda j: (0, j))],
        out_specs=[pl.BlockSpec((2, t, DH), lambda j: (j, 0, 0)), pl.BlockSpec((t, LANE), lambda j: (0, j))],
        out_shape=[jax.ShapeDtypeStruct((HEADS, t, DH), F32), jax.ShapeDtypeStruct((t, WIDTH), MXU)],
        compiler_params=_params(("arbitrary",)),
    )(o, pm, dmix)


def _tail(x, mixg, mixf, tgt, wo, n2w, wg_t, wu_t, wd, fw):
    t, d = x.shape
    dff = wd.shape[0]
    tb = min(TB, t)

    def body(x_ref, mg_ref, mf_ref, t_ref, wo_ref, n2_ref, wg_ref, wu_ref, wd_ref, fw_ref,
             h2_ref, act_ref, dgate_ref, dup_ref, dx3_ref, dx2_ref, dmg_ref, dmf_ref, dn2_ref, dfw_ref, loss_ref):
        @pl.when(pl.program_id(0) == 0)
        def _():
            dn2_ref[...] = jnp.zeros_like(dn2_ref)
            dfw_ref[...] = jnp.zeros_like(dfw_ref)
            loss_ref[...] = jnp.zeros_like(loss_ref)

        x2 = x_ref[...] + _dot(mg_ref[...], wo_ref[0:WIDTH, :]) + _dot(mf_ref[...], wo_ref[WIDTH:2 * WIDTH, :])
        r2 = lax.rsqrt(jnp.mean(x2 * x2, axis=-1, keepdims=True) + EPS)
        xn2 = x2 * r2
        h2 = (xn2 * n2_ref[...]).astype(MXU)
        h2_ref[...] = h2
        gate = _dot_nt(h2, wg_ref[...])
        up = _dot_nt(h2, wu_ref[...])
        sg = _sigmoid(gate)
        sl = gate * sg
        act = (sl * up).astype(MXU)
        act_ref[...] = act
        x3 = x2 + _dot(act, wd_ref[...])
        r3 = lax.rsqrt(jnp.mean(x3 * x3, axis=-1, keepdims=True) + EPS)
        xn3 = x3 * r3
        err = xn3 * fw_ref[...] - t_ref[...]
        loss_ref[...] += 0.5 * jnp.sum(jnp.mean(err * err, axis=-1, keepdims=True), axis=0, keepdims=True)
        dy = err * (1.0 / d)
        dfw_ref[...] += jnp.sum(dy * xn3, axis=0, keepdims=True)
        g3 = dy * fw_ref[...]
        dx3 = r3 * (g3 - xn3 * jnp.mean(g3 * xn3, axis=-1, keepdims=True))
        dx3_ref[...] = dx3.astype(MXU)
        dact = _dot_nt(dx3, wd_ref[...])
        dgate = (dact * up * (sg * (1.0 + gate * (1.0 - sg)))).astype(MXU)
        dup = (dact * sl).astype(MXU)
        dgate_ref[...] = dgate
        dup_ref[...] = dup
        dh2 = _dot(dgate, wg_ref[...]) + _dot(dup, wu_ref[...])
        dn2_ref[...] += jnp.sum(dh2 * xn2, axis=0, keepdims=True)
        g2 = dh2 * n2_ref[...]
        dx2 = dx3 + r2 * (g2 - xn2 * jnp.mean(g2 * xn2, axis=-1, keepdims=True))
        dx2_ref[...] = dx2
        dmg_ref[...] = _dot_nt(dx2, wo_ref[0:WIDTH, :])
        dmf_ref[...] = _dot_nt(dx2, wo_ref[WIDTH:2 * WIDTH, :])

    def tok(n):
        return pl.BlockSpec((tb, n), lambda i: (i, 0))

    acc = pl.BlockSpec((1, d), lambda i: (0, 0))
    sds = jax.ShapeDtypeStruct
    return pl.pallas_call(
        body, name="tail", grid=(t // tb,),
        in_specs=[tok(d), tok(WIDTH), tok(WIDTH), tok(d), _resident(wo.shape), _resident((1, d)),
                  _resident(wg_t.shape), _resident(wu_t.shape), _resident(wd.shape), _resident((1, d))],
        out_specs=[tok(d), tok(dff), tok(dff), tok(dff), tok(d), tok(d), tok(WIDTH), tok(WIDTH), acc, acc,
                   pl.BlockSpec((1, 1), lambda i: (0, 0))],
        out_shape=[sds((t, d), MXU), sds((t, dff), MXU), sds((t, dff), MXU), sds((t, dff), MXU), sds((t, d), MXU),
                   sds((t, d), F32), sds((t, WIDTH), F32), sds((t, WIDTH), F32), sds((1, d), F32), sds((1, d), F32),
                   sds((1, 1), F32)],
        compiler_params=_params(("arbitrary",)),
    )(x, mixg, mixf, tgt, wo, n2w, wg_t, wu_t, wd, fw)


def _wgrads(a_list, b, name):
    t, n = b.shape
    ms = [a.shape[1] for a in a_list]
    bms = [256 if m % 256 == 0 else LANE for m in ms]
    nbs = [m // bm for m, bm in zip(ms, bms)]
    k = len(a_list)
    cast = b.dtype != jnp.dtype(MXU)

    def body(*refs):
        a_refs, b_ref, o_refs = refs[:k], refs[k], refs[k + 1:2 * k + 1]
        i = pl.program_id(0)
        if cast:
            @pl.when(i == 0)
            def _():
                refs[-1][...] = b_ref[...].astype(MXU)
        for a_ref, o_ref, nb in zip(a_refs, o_refs, nbs):
            @pl.when(i < nb)
            def _():
                o_ref[...] = _dot_tn(a_ref[...], refs[-1][...] if cast else b_ref[...]).astype(o_ref.dtype)

    def clamp(nb):
        return lambda i: jnp.minimum(i, nb - 1)

    return pl.pallas_call(
        body, name=name, grid=(max(nbs),),
        in_specs=[pl.BlockSpec((t, bm), lambda i, c=clamp(nb): (0, c(i))) for bm, nb in zip(bms, nbs)] + [_resident((t, n))],
        out_specs=[pl.BlockSpec((bm, n), lambda i, c=clamp(nb): (c(i), 0)) for bm, nb in zip(bms, nbs)],
        out_shape=[jax.ShapeDtypeStruct((m, n), WIRE) for m in ms],
        scratch_shapes=[pltpu.VMEM((t, n), MXU)] if cast else [],
        compiler_params=_params(("arbitrary",)),
    )(*a_list, b)


def _merge_dw_in(d_gdn, d_z, d_fox, d_fg, d_small, shards=False):
    pieces = [d_gdn, d_z, d_small[:2 * HEADS], d_fox, d_fg, d_small[2 * HEADS:3 * HEADS]]
    if not shards:
        return jnp.concatenate(pieces, axis=0)
    n = sum(p.shape[0] for p in pieces) // N_DEV
    out = []
    for dev in range(N_DEV):
        parts, first = [], 0
        for p in pieces:
            lo, hi = max(dev * n, first), min((dev + 1) * n, first + p.shape[0])
            if lo < hi:
                parts.append(p[lo - first:hi - first])
            first += p.shape[0]
        out.append(jnp.concatenate(parts, axis=0))
    return jnp.stack(out)


def _lanes(*pieces):
    v = jnp.concatenate([p.reshape(-1).astype(F32) for p in pieces])
    return jnp.pad(v, (0, LANE - v.shape[0])).reshape(1, LANE)


def _vector_params(p):
    d = p["norm1_w"].size
    gparams = jnp.concatenate([_lanes(jnp.zeros(HEADS), p["gdn_dt_bias"], p["fox_f_bias"]),
                               _lanes(jnp.zeros(HEADS), p["gdn_A_log"]), jnp.zeros((6, LANE), F32)])
    fox_nw = jnp.stack([jnp.tile(p["fox_q_norm_w"].reshape(-1), 2), jnp.tile(p["fox_k_norm_w"].reshape(-1), 2),
                        jnp.ones((LANE,), F32)])
    return dict(n1w=p["norm1_w"].reshape(1, d), n2w=p["norm2_w"].reshape(1, d), fw=p["final_norm_w"].reshape(1, d),
                onw=p["gdn_out_norm_w"].reshape(1, DH), gparams=gparams, fox_nw=fox_nw)


def _mixer_forward(x, vp, w_t, ws_t, conv_w, carry=None, carry_first=None):
    h1, pm, ps, *first = _inproj(x, vp["n1w"], w_t, ws_t, carry_first)
    gates, run, tot, run_t = _gates(ps, vp["gparams"])
    fqkv = _fox_prep(pm, vp["fox_nw"])
    o_fox, lse, *carried = _fox_fwd(fqkv, run, run_t, carry)
    if carry:
        pm, o_fox = lax.optimization_barrier((pm, o_fox))
    mixf = _fox_post(o_fox, pm)
    gqkv = _gdn_prep(pm, conv_w)
    o_gdn, states, inv = _gdn_fwd(gqkv, gates, run, tot, run_t)
    mixg = _gdn_post(o_gdn, pm, vp["onw"])
    return dict(h1=h1, pm=pm, ps=ps, gates=gates, run=run, tot=tot, run_t=run_t, gqkv=gqkv, o_gdn=o_gdn, states=states,
                inv=inv, mixg=mixg, fqkv=fqkv, o_fox=o_fox, lse=lse, mixf=mixf, carried=carried[0] if carried else None,
                carried_first=first[0] if first else None)


def _mixer_backward(x, vp, w_t, ws_t, conv_w, f, dx2, dmixg, dmixf, carry=None, scatter_own=False):
    pm = f["pm"]
    do_fox, dfg = _fox_post_bwd(f["o_fox"], pm, dmixf)
    dfqkv, dcol_f, drow_f, *carried = _fox_bwd(f["fqkv"], f["run"], f["run_t"], f["o_fox"], f["lse"], do_fox, carry)
    dfox, dfnw = _fox_prep_bwd(pm, vp["fox_nw"], dfqkv)
    do_gdn, dz, donw = _gdn_post_bwd(f["o_gdn"], pm, vp["onw"], dmixg)
    dgqkv, dcol_g, dtot_g, drow_g = _gdn_bwd(f["gqkv"], f["gates"], f["run"], f["tot"], f["run_t"], f["inv"], f["states"], do_gdn)
    dgdn, dconv = _gdn_prep_bwd(pm, conv_w, dgqkv)
    dps, gsum = _gates_bwd(f["ps"], vp["gparams"], dcol_g, dtot_g, dcol_f, drow_g, drow_f)
    dw_pieces = _wgrads([dgdn, dz, dfox, dfg, dps], f["h1"], "dw_in")
    dw_in = _merge_dw_in(*dw_pieces)
    own = ("scatter", [_merge_dw_in(*dw_pieces, shards=True), _cut(dconv, 1)]) if scatter_own else None
    grad_x, dn1w, *sent = _inproj_bwd(x, vp["n1w"], dx2, dgdn, dz, dfox, dfg, dps, w_t, ws_t, own)
    small = dict(dn1w=dn1w, gsum=gsum, donw=donw, dfnw=dfnw)
    return (grad_x, dw_in, dconv, small, *carried, *sent)


VECTORS = ("norm1_w", "norm2_w", "final_norm_w", "gdn_A_log", "gdn_dt_bias", "gdn_out_norm_w", "fox_f_bias",
           "fox_q_norm_w", "fox_k_norm_w")
VEC_ROWS = 16
LOSS_ROW = len(VECTORS)


def _pack_vectors(dn1w, dn2w, dfw, gsum, donw, dfnw, loss):
    d = dn1w.shape[1]

    def body(n1_ref, n2_ref, fw_ref, gs_ref, on_ref, fn_ref, loss_ref, o_ref):
        o_ref[...] = jnp.zeros_like(o_ref)
        o_ref[0:1, :] = n1_ref[...]
        o_ref[1:2, :] = n2_ref[...]
        o_ref[2:3, :] = fw_ref[...]
        o_ref[3:4, 0:HEADS] = gs_ref[0:1, 0:HEADS]
        o_ref[4:5, 0:HEADS] = gs_ref[1:2, 0:HEADS]
        o_ref[5:6, 0:DH] = on_ref[...]
        o_ref[6:7, 0:HEADS] = gs_ref[2:3, 0:HEADS]
        for kind in range(2):
            v = fn_ref[kind, 0]
            for j in range(1, fn_ref.shape[1]):
                v = v + fn_ref[kind, j]
            o_ref[7 + kind:8 + kind, 0:DH] = v[:, :DH] + v[:, DH:]
        o_ref[LOSS_ROW:LOSS_ROW + 1, 0:1] = loss_ref[...]

    return pl.pallas_call(body, name="pack_vectors", out_shape=jax.ShapeDtypeStruct((VEC_ROWS, d), F32),
                          compiler_params=_params())(dn1w, dn2w, dfw, gsum, donw, dfnw, loss)


def _late_grads(f, h2, act, dgate, dup, dx3, dx2):
    dw_gate, dw_up = _wgrads([dgate, dup], h2, "dw_gate_up")
    (dw_down,) = _wgrads([act], dx3, "dw_down")
    return {"w_out": jnp.concatenate(_wgrads([f["mixg"], f["mixf"]], dx2, "dw_out"), axis=0),
            "w_ffn_gate": dw_gate, "w_ffn_up": dw_up, "w_ffn_down": dw_down}


def _local_step(x, tgt, p, w_in_t, conv_w, wo, wg_t, wu_t, wd):
    vp = _vector_params(p)
    ws_t = _small_rows(w_in_t)
    f = _mixer_forward(x, vp, w_in_t, ws_t, conv_w)
    (h2, act, dgate, dup, dx3, dx2, dmixg, dmixf, dn2w, dfw, loss) = _tail(
        x, f["mixg"], f["mixf"], tgt, wo, vp["n2w"], wg_t, wu_t, wd, vp["fw"])
    grad_x, dw_in, dconv, small = _mixer_backward(x, vp, w_in_t, ws_t, conv_w, f, dx2, dmixg, dmixf)
    grads = {"w_in": dw_in, "gdn_conv_w": dconv, **_late_grads(f, h2, act, dgate, dup, dx3, dx2)}
    vec = _pack_vectors(small["dn1w"], dn2w, dfw, small["gsum"], small["donw"], small["dfnw"], loss)
    return grad_x[0], grads, vec


def _my_place():
    return lax.axis_index("x"), lax.axis_index("y"), lax.axis_index("c")


def _peers():
    x, y, c = _my_place()
    peers = []
    for k in range(1, N_DEV):
        px = 1 - x if k & 4 else x
        py = 1 - y if k & 2 else y
        pc = 1 - c if k & 1 else c
        peers.append(((px, py, pc), 4 * px + 2 * py + pc))
    return 4 * x + 2 * y + c, peers


def _spread_copies(kind, srcs, lands, send_sems, recv_sems):
    me, peers = _peers()
    kinds = [kind] * len(srcs) if isinstance(kind, str) else kind
    remote, local = [], []
    for i, (kd, src, land) in enumerate(zip(kinds, srcs, lands)):
        for k, (dev, idx) in enumerate(peers):
            remote.append(pltpu.make_async_remote_copy(
                src_ref=src if kd == "gather" else src.at[idx], dst_ref=land.at[me],
                send_sem=send_sems.at[i * (N_DEV - 1) + k], recv_sem=recv_sems.at[i * (N_DEV - 1) + k],
                device_id=dev, device_id_type=MESH))
        local.append((src if kd == "gather" else src.at[me], land.at[me]))
    return remote, local


def _gather_two_level(arrays, name):
    n = len(arrays)

    def body(*refs):
        srcs, lands = refs[:n], refs[n:2 * n]
        send_sems, recv_sems, local_sems = refs[2 * n:]
        x, y, c = _my_place()
        me, sibling = (x, y, c), (x, y, 1 - c)
        chips = [(1 - x, y), (x, 1 - y), (1 - x, 1 - y)]

        def index(p):
            return 4 * p[0] + 2 * p[1] + p[2]

        def copy(i, k, block, to, src=None):
            blk = lands[i].at[index(block)]
            return pltpu.make_async_remote_copy(
                src_ref=blk if src is None else src, dst_ref=blk, send_sem=send_sems.at[7 * i + k],
                recv_sem=recv_sems.at[7 * i + k], device_id=to, device_id_type=MESH)

        mine = [pltpu.make_async_copy(srcs[i], lands[i].at[index(me)], local_sems.at[i]) for i in range(n)]
        for cp in mine:
            cp.start()
        first = []
        for i in range(n):
            first.append(copy(i, 0, me, sibling, src=srcs[i]))
            first += [copy(i, 1 + j, me, (*chip, c), src=srcs[i]) for j, chip in enumerate(chips)]
        for cp in first:
            cp.start()
        passed = []
        for i in range(n):
            for j, chip in enumerate(chips):
                copy(i, 1 + j, (*chip, c), me).wait_recv()
                passed.append(copy(i, 4 + j, (*chip, c), sibling))
                passed[-1].start()
        for i in range(n):
            copy(i, 0, sibling, me).wait_recv()
            for j, chip in enumerate(chips):
                copy(i, 4 + j, (*chip, 1 - c), me).wait_recv()
        for cp in first + passed:
            cp.wait_send()
        for cp in mine:
            cp.wait()

    return pl.pallas_call(
        body, name=name,
        out_shape=[jax.ShapeDtypeStruct((N_DEV,) + a.shape, a.dtype) for a in arrays],
        in_specs=[pl.BlockSpec(memory_space=pl.ANY)] * n, out_specs=[pl.BlockSpec(memory_space=pl.ANY)] * n,
        scratch_shapes=[pltpu.SemaphoreType.DMA((7 * n,)), pltpu.SemaphoreType.DMA((7 * n,)),
                        pltpu.SemaphoreType.DMA((n,))],
    )(*arrays)


def _land_shape(kind, a):
    return (N_DEV,) + a.shape if kind == "gather" else a.shape


HBM = pl.BlockSpec(memory_space=pltpu.HBM)
SEM = pl.BlockSpec(memory_space=pltpu.SEMAPHORE)


def _hbm(a):
    return pltpu.with_memory_space_constraint(a, pltpu.HBM)


def _carry_operands(kind, arrays):
    n = len(arrays)
    kinds = [kind] * n if isinstance(kind, str) else kind
    lands = [lax.empty(_land_shape(kd, a), a.dtype) for kd, a in zip(kinds, arrays)]
    sems = [pltpu.SemaphoreType.DMA((n * (N_DEV - 1),))] * 2
    return ([_hbm(a) for a in list(arrays) + lands], [HBM] * (2 * n),
            sems + [pltpu.HBM(a.shape, a.dtype) for a in list(arrays) + lands], [SEM] * 2 + [HBM] * (2 * n),
            pltpu.SemaphoreType.DMA((n,)))


def _carry_step(kind, in_refs, send_sems, recv_sems, local_sems, step, n_steps):
    n = len(in_refs) // 2
    remote, local = _spread_copies(kind, in_refs[:n], in_refs[n:], send_sems, recv_sems)
    copies = [pltpu.make_async_copy(s, d, local_sems.at[i]) for i, (s, d) in enumerate(local)]
    per_step = -(-len(remote) // n_steps)
    for s in range(-(-len(remote) // per_step)):
        @pl.when(step == s)
        def _():
            for cp in remote[s * per_step:(s + 1) * per_step]:
                cp.start()
            if s == 0:
                for cp in copies:
                    cp.start()

    @pl.when(step == n_steps - 1)
    def _():
        for cp in copies:
            cp.wait()


def _carry_state(extra_out):
    n = (len(extra_out) - 2) // 2
    return list(extra_out[2:2 + n]), list(extra_out[2 + n:]), extra_out[0], extra_out[1]


def _spread_start(arrays, kind, name):
    n = len(arrays)

    def body(*refs):
        srcs, lands = refs[:n], refs[n:2 * n]
        send_sems, recv_sems = refs[2 * n], refs[2 * n + 1]
        token = refs[4 * n + 2]
        local_sems = refs[4 * n + 3]
        remote, local = _spread_copies(kind, srcs, lands, send_sems, recv_sems)
        for cp in remote:
            cp.start()
        copies = [pltpu.make_async_copy(s, d, local_sems.at[i]) for i, (s, d) in enumerate(local)]
        for cp in copies:
            cp.start()
        for cp in copies:
            cp.wait()
        token[...] = jnp.zeros_like(token)

    sems = (pltpu.SemaphoreType.DMA((n * (N_DEV - 1),)),) * 2
    kinds = [kind] * n if isinstance(kind, str) else kind
    lands = [lax.empty(_land_shape(kd, a), a.dtype) for kd, a in zip(kinds, arrays)]
    out = pl.pallas_call(
        body, name=name,
        out_shape=sems + tuple(pltpu.HBM(a.shape, a.dtype) for a in list(arrays) + lands)
        + (jax.ShapeDtypeStruct((8, LANE), F32),),
        in_specs=[HBM] * (2 * n), out_specs=tuple([SEM] * 2 + [HBM] * (2 * n) + [pl.BlockSpec(memory_space=pltpu.VMEM)]),
        input_output_aliases={j: 2 + j for j in range(2 * n)},
        scratch_shapes=[pltpu.SemaphoreType.DMA((n,))],
        compiler_params=pltpu.CompilerParams(has_side_effects=pltpu.SideEffectType.DATAFLOW_SIDE_EFFECTING),
    )(*[_hbm(a) for a in arrays], *[_hbm(a) for a in lands])
    return (list(out[2:2 + n]), list(out[2 + n:2 + 2 * n]), out[0], out[1]), out[-1]


def _spread_wait(state, kind, after, name):
    srcs, lands, send_sems, recv_sems = state
    n = len(srcs)
    after = list(after) if isinstance(after, (list, tuple)) else [after]

    def body(*refs):
        remote, _ = _spread_copies(kind, refs[:n], refs[n:2 * n], refs[2 * n], refs[2 * n + 1])
        for cp in remote:
            cp.wait_send()
        for cp in remote:
            cp.wait_recv()

    out = pl.pallas_call(
        body, name=name,
        out_shape=tuple(pltpu.HBM(a.shape, a.dtype) for a in srcs + lands),
        in_specs=[HBM] * (2 * n) + [SEM, SEM] + [pl.BlockSpec(memory_space=pl.ANY)] * len(after),
        out_specs=tuple([HBM] * (2 * n)),
        input_output_aliases={j: j for j in range(2 * n)},
        compiler_params=pltpu.CompilerParams(has_side_effects=pltpu.SideEffectType.DATAFLOW_SIDE_EFFECTING),
    )(*srcs, *lands, send_sems, recv_sems, *after)
    return list(out[n:])


ADAM_ROWS = 128
ADAM_COLS = 256


def _adam_math(g, w, m, v):
    nm = ADAM_B1 * m + (1.0 - ADAM_B1) * g
    nv = ADAM_B2 * v + (1.0 - ADAM_B2) * (g * g)
    m_hat = nm / (1.0 - ADAM_B1 ** ADAM_STEP)
    v_hat = nv / (1.0 - ADAM_B2 ** ADAM_STEP)
    return -ADAM_LR * (m_hat / (jnp.sqrt(v_hat) + ADAM_EPS) + ADAM_WD * w), nm, nv


def _sum_parts(p_ref):
    g = p_ref[0].astype(F32)
    for s in range(1, N_DEV):
        g = g + p_ref[s].astype(F32)
    return g


def _adam_matrix(parts, w, m, v, name):
    _, r, c = w.shape
    rb = ADAM_ROWS if r % ADAM_ROWS == 0 else r
    cb = ADAM_COLS if (rb == r and c % ADAM_COLS == 0) else c

    def body(p_ref, w_ref, m_ref, v_ref, g_ref, d_ref, nm_ref, nv_ref):
        g = _sum_parts(p_ref)
        g_ref[0] = g
        d_ref[0], nm_ref[0], nv_ref[0] = _adam_math(g, w_ref[0], m_ref[0], v_ref[0])

    blk = pl.BlockSpec((1, rb, cb), lambda i, j: (0, i, j))
    return pl.pallas_call(
        body, name=name, grid=(r // rb, c // cb),
        in_specs=[pl.BlockSpec((N_DEV, rb, cb), lambda i, j: (0, i, j)), blk, blk, blk],
        out_specs=[blk] * 4, out_shape=[jax.ShapeDtypeStruct(w.shape, F32)] * 4,
        compiler_params=_params(("arbitrary", "arbitrary")),
    )(parts, w, m, v)


def _copy_rows(a, name):
    _, r, c = a.shape
    rb = min(TB, r)

    def body(a_ref, o_ref):
        o_ref[...] = a_ref[...]

    blk = pl.BlockSpec((1, rb, c), lambda i: (0, i, 0))
    return pl.pallas_call(body, name=name, grid=(r // rb,), in_specs=[blk], out_specs=blk,
                          out_shape=jax.ShapeDtypeStruct(a.shape, a.dtype), compiler_params=_params(("arbitrary",)))(a)


def _adam_vectors(parts, ws, ms, vs):
    nv = len(ws)

    def body(*refs):
        p_ref = refs[0]
        w_refs, m_refs, v_refs = refs[1:1 + nv], refs[1 + nv:1 + 2 * nv], refs[1 + 2 * nv:1 + 3 * nv]
        outs = refs[1 + 3 * nv:]
        g_all = _sum_parts(p_ref)
        for i in range(nv):
            n = w_refs[i].shape[1]
            g = g_all[i:i + 1, 0:n]
            d, nm, nvv = _adam_math(g, w_refs[i][...], m_refs[i][...], v_refs[i][...])
            outs[i][...] = g
            outs[nv + i][...] = d
            outs[2 * nv + i][...] = nm
            outs[3 * nv + i][...] = nvv
        outs[4 * nv][...] = g_all[LOSS_ROW:LOSS_ROW + 1, 0:1]

    shapes = [jax.ShapeDtypeStruct(a.shape, F32) for a in ws]
    out = pl.pallas_call(body, name="adam_vectors", out_shape=shapes * 4 + [jax.ShapeDtypeStruct((1, 1), F32)],
                         compiler_params=_params())(parts, *ws, *ms, *vs)
    return out[:nv], out[nv:2 * nv], out[2 * nv:3 * nv], out[3 * nv:4 * nv], out[4 * nv]


MATRICES = (("w_in", 1), ("gdn_conv_w", 1), ("w_out", 0), ("w_ffn_gate", 1), ("w_ffn_up", 1), ("w_ffn_down", 0))
TRANSPOSED = ("w_in", "w_ffn_gate", "w_ffn_up")
WEIGHTS = ("norm1_w", "w_in", "gdn_conv_w", "gdn_A_log", "gdn_dt_bias", "gdn_out_norm_w", "fox_f_bias", "fox_q_norm_w",
           "fox_k_norm_w", "w_out", "norm2_w", "w_ffn_gate", "w_ffn_up", "w_ffn_down", "final_norm_w")


def _join(blocks, axis):
    _, r, c = blocks.shape
    if axis == 0:
        return blocks.reshape(N_DEV * r, c)
    return blocks.transpose(1, 0, 2).reshape(r, N_DEV * c)


def _cut(full, axis):
    r, c = full.shape
    if axis == 0:
        return full.reshape(N_DEV, r // N_DEV, c)
    return full.reshape(r, N_DEV, c // N_DEV).transpose(1, 0, 2)


def kernel(x, norm1_w, w_in, gdn_conv_w, gdn_A_log, gdn_dt_bias, gdn_out_norm_w, fox_f_bias, fox_q_norm_w, fox_k_norm_w, w_out, norm2_w, w_ffn_gate, w_ffn_up, w_ffn_down, final_norm_w, loss_target, m_norm1_w, m_w_in, m_gdn_conv_w, m_gdn_A_log, m_gdn_dt_bias, m_gdn_out_norm_w, m_fox_f_bias, m_fox_q_norm_w, m_fox_k_norm_w, m_w_out, m_norm2_w, m_w_ffn_gate, m_w_ffn_up, m_w_ffn_down, m_final_norm_w, v_norm1_w, v_w_in, v_gdn_conv_w, v_gdn_A_log, v_gdn_dt_bias, v_gdn_out_norm_w, v_fox_f_bias, v_fox_q_norm_w, v_fox_k_norm_w, v_w_out, v_norm2_w, v_w_ffn_gate, v_w_ffn_up, v_w_ffn_down, v_final_norm_w):
    w = dict(norm1_w=norm1_w, w_in=w_in, gdn_conv_w=gdn_conv_w, gdn_A_log=gdn_A_log, gdn_dt_bias=gdn_dt_bias,
             gdn_out_norm_w=gdn_out_norm_w, fox_f_bias=fox_f_bias, fox_q_norm_w=fox_q_norm_w, fox_k_norm_w=fox_k_norm_w,
             w_out=w_out, norm2_w=norm2_w, w_ffn_gate=w_ffn_gate, w_ffn_up=w_ffn_up, w_ffn_down=w_ffn_down,
             final_norm_w=final_norm_w)
    m = dict(norm1_w=m_norm1_w, w_in=m_w_in, gdn_conv_w=m_gdn_conv_w, gdn_A_log=m_gdn_A_log, gdn_dt_bias=m_gdn_dt_bias,
             gdn_out_norm_w=m_gdn_out_norm_w, fox_f_bias=m_fox_f_bias, fox_q_norm_w=m_fox_q_norm_w,
             fox_k_norm_w=m_fox_k_norm_w, w_out=m_w_out, norm2_w=m_norm2_w, w_ffn_gate=m_w_ffn_gate,
             w_ffn_up=m_w_ffn_up, w_ffn_down=m_w_ffn_down, final_norm_w=m_final_norm_w)
    v = dict(norm1_w=v_norm1_w, w_in=v_w_in, gdn_conv_w=v_gdn_conv_w, gdn_A_log=v_gdn_A_log, gdn_dt_bias=v_gdn_dt_bias,
             gdn_out_norm_w=v_gdn_out_norm_w, fox_f_bias=v_fox_f_bias, fox_q_norm_w=v_fox_q_norm_w,
             fox_k_norm_w=v_fox_k_norm_w, w_out=v_w_out, norm2_w=v_norm2_w, w_ffn_gate=v_w_ffn_gate,
             w_ffn_up=v_w_ffn_up, w_ffn_down=v_w_ffn_down, final_norm_w=v_final_norm_w)
    late = ("w_out", "w_ffn_gate", "w_ffn_up", "w_ffn_down")
    xs, tgt = x[0], loss_target[0]
    vp = _vector_params({n: w[n] for n in VECTORS})

    def rows_of(d, n):
        return d[n].transpose(0, 2, 1) if n in TRANSPOSED else d[n]

    wr, mr, vr = ({n: rows_of(d, n) for n, _ in MATRICES} for d in (w, m, v))

    w_in_blocks, conv_blocks = _gather_two_level([wr["w_in"][0].astype(WIRE), w["gdn_conv_w"][0]], "gather_in")
    w_t = _join(w_in_blocks, 0)
    conv_w = _join(conv_blocks, 1)
    early = ("w_ffn_gate",)
    rest = tuple(n for n in late if n not in early)
    f = _mixer_forward(xs, vp, w_t, _small_rows(w_t), conv_w, ("gather", [wr[n][0].astype(WIRE) for n in rest]),
                       ("gather", [wr[n][0].astype(WIRE) for n in early]))
    full = {n: _join(b, 0) for n, b in zip(early, _spread_wait(f["carried_first"], "gather", f["mixg"], "gather_early_wait"))}
    full.update({n: _join(b, 0)
                 for n, b in zip(rest, _spread_wait(f["carried"], "gather", full[early[0]], "gather_late_wait"))})

    (h2, act, dgate, dup, dx3, dx2, dmixg, dmixf, dn2w, dfw, loss) = _tail(
        xs, f["mixg"], f["mixf"], tgt, full["w_out"], vp["n2w"], full["w_ffn_gate"], full["w_ffn_up"],
        full["w_ffn_down"], vp["fw"])
    dlate = _late_grads(f, h2, act, dgate, dup, dx3, dx2)

    grad_x, dw_in, dconv, small, state_grads, state_own = _mixer_backward(
        xs, vp, w_t, _small_rows(w_t), conv_w, f, dx2, dmixg, dmixf, ("scatter", [_cut(dlate[n], 0) for n in late]),
        scatter_own=True)
    vec = _pack_vectors(small["dn1w"], dn2w, dfw, small["gsum"], small["donw"], small["dfnw"], loss)

    state_vec, token = _spread_start([vec], "gather", "vectors_start")
    parts = dict(zip(late, _spread_wait(state_grads, "scatter", token, "grads_late_wait")))
    results = [{}, {}, {}, {}]

    updated = {}

    def update(n):
        out = _adam_matrix(parts[n], wr[n], mr[n], vr[n], "adam_" + n)
        updated[n] = out[0]
        for d, a in zip(results, out):
            d[n] = a.transpose(0, 2, 1) if n in TRANSPOSED else a

    for n in late:
        update(n)
    grad_x = _copy_rows(grad_x, "grad_x_result")
    ready = [updated[n] for n in late] + [d["w_in"] for d in (wr, mr, vr)] + [grad_x]
    (parts_vec,) = _spread_wait(state_vec, "gather", ready, "vectors_wait")
    row = lambda a: a.reshape(1, -1)
    *vec_out, total_loss = _adam_vectors(parts_vec, [row(w[n]) for n in VECTORS], [row(m[n]) for n in VECTORS],
                                         [row(v[n]) for n in VECTORS])
    parts["w_in"], parts["gdn_conv_w"] = _spread_wait(state_own, "scatter", parts_vec, "own_wait")
    update("w_in")
    update("gdn_conv_w")
    for d, arrs in zip(results, vec_out):
        for n, a in zip(VECTORS, arrs):
            d[n] = a.reshape(w[n].shape)
    return (total_loss[0, 0], grad_x, *[d[n] for d in results for n in WEIGHTS])
```

```python
import functools

import jax
import jax.numpy as jnp
from jax import lax
from jax.experimental import pallas as pl
from jax.experimental.pallas import tpu as pltpu

F32 = jnp.float32
MXU = jnp.bfloat16
WIRE = jnp.bfloat16
HI = lax.Precision.HIGHEST
EPS = 1e-6

N_DEV = 8
HEADS = 8
DH = 64
WIDTH = HEADS * DH
CHUNK = 64
LANE = 128
ROW_ALIGN = 16
TB = 256
QB = 256
VMEM_LIMIT = 60 * 1024 * 1024

ADAM_LR = 0.001
ADAM_B1 = 0.9
ADAM_B2 = 0.999
ADAM_EPS = 1e-08
ADAM_WD = 0.01
ADAM_STEP = 10

MESH = pl.DeviceIdType.MESH


def _params(sem=None):
    return pltpu.CompilerParams(dimension_semantics=sem, vmem_limit_bytes=VMEM_LIMIT)


def _resident(shape):
    n = len(shape)
    return pl.BlockSpec(shape, lambda *_: (0,) * n, pipeline_mode=pl.Buffered(1))


def _dot(a, b):
    return jnp.dot(a.astype(MXU), b.astype(MXU), preferred_element_type=F32)


def _dot_nt(a, b):
    return lax.dot_general(a.astype(MXU), b.astype(MXU), (((1,), (1,)), ((), ())), preferred_element_type=F32)


def _dot_tn(a, b):
    return lax.dot_general(a.astype(MXU), b.astype(MXU), (((0,), (0,)), ((), ())), preferred_element_type=F32)


def _hdot(a, b):
    return jnp.dot(a, b, precision=HI, preferred_element_type=F32)


def _hdot_nt(a, b):
    return lax.dot_general(a, b, (((1,), (1,)), ((), ())), precision=HI, preferred_element_type=F32)


def _hdot_tn(a, b):
    return lax.dot_general(a, b, (((0,), (0,)), ((), ())), precision=HI, preferred_element_type=F32)


def _sigmoid(x):
    return 0.5 * jnp.tanh(0.5 * x) + 0.5


def _softplus(x):
    return jnp.maximum(x, 0.0) + jnp.log(1.0 + jnp.exp(-jnp.abs(x)))


def _head_sum_matrix():
    ri = lax.broadcasted_iota(jnp.int32, (LANE, LANE), 0) // DH
    ci = lax.broadcasted_iota(jnp.int32, (LANE, LANE), 1) // DH
    return (ri == ci).astype(F32)


def _group_sum(a, ones_matrix):
    hi = a.astype(jnp.bfloat16)
    lo = (a - hi.astype(F32)).astype(jnp.bfloat16)
    m = ones_matrix.astype(jnp.bfloat16)
    return jnp.dot(hi, m, preferred_element_type=F32) + jnp.dot(lo, m, preferred_element_type=F32)


def _shift_down(x, s):
    return pltpu.roll(x, s, 0)


def _shift_up(x, s):
    return pltpu.roll(x, x.shape[0] - s, 0)


ROWS_A = 4 * WIDTH
ROWS_B = ROWS_A + 2 * HEADS
ROWS_C = ROWS_B + 4 * WIDTH


def _small_rows(w_t):
    return jnp.concatenate([w_t[ROWS_A:ROWS_B], w_t[ROWS_C:], jnp.zeros((LANE - 3 * HEADS, w_t.shape[1]), w_t.dtype)])


def _inproj(x, n1w, w_t, ws_t, carry=None):
    t, d = x.shape
    tb = min(TB, t)
    c_in, c_in_specs, c_out_shape, c_out_specs, c_sem = _carry_operands(*carry) if carry else ([], [], [], [], None)
    n_c = len(c_in)

    def body(*refs):
        x_ref, nw_ref, wt_ref, ws_ref = refs[:4]
        h_ref, pm_ref, ps_ref = refs[4 + n_c:7 + n_c]
        if carry:
            _carry_step(carry[0], refs[4:4 + n_c], refs[7 + n_c], refs[8 + n_c], refs[-1], pl.program_id(0), t // tb)
        xv = x_ref[...]
        r = lax.rsqrt(jnp.mean(xv * xv, axis=-1, keepdims=True) + EPS)
        h = (xv * r * nw_ref[...]).astype(MXU)
        h_ref[...] = h
        pm_ref[:, 0:ROWS_A] = _dot_nt(h, wt_ref[0:ROWS_A, :])
        pm_ref[:, ROWS_A:2 * ROWS_A] = _dot_nt(h, wt_ref[ROWS_B:ROWS_C, :])
        ps_ref[...] = _dot_nt(h, ws_ref[...])

    out = pl.pallas_call(
        body, name="inproj", grid=(t // tb,),
        in_specs=[pl.BlockSpec((tb, d), lambda i: (i, 0)), _resident((1, d)), _resident(w_t.shape), _resident((LANE, d))]
        + c_in_specs,
        out_specs=[pl.BlockSpec((tb, d), lambda i: (i, 0)), pl.BlockSpec((tb, 2 * ROWS_A), lambda i: (i, 0)),
                   pl.BlockSpec((tb, LANE), lambda i: (i, 0))] + c_out_specs,
        out_shape=[jax.ShapeDtypeStruct((t, d), MXU), jax.ShapeDtypeStruct((t, 2 * ROWS_A), F32),
                   jax.ShapeDtypeStruct((t, LANE), F32)] + c_out_shape,
        input_output_aliases={4 + j: 5 + j for j in range(n_c)},
        scratch_shapes=[c_sem] if carry else [],
        compiler_params=_params(("arbitrary",)),
    )(x, n1w, w_t, ws_t, *c_in)
    return (out[0], out[1], out[2], _carry_state(out[3:])) if carry else tuple(out)


def _inproj_bwd(x, n1w, dx2, dgdn, dz, dfox, dfg, dps, w_t, ws_t, carry=None):
    t, d = x.shape
    tb = min(TB, t)
    w3 = 3 * WIDTH
    c_in, c_in_specs, c_out_shape, c_out_specs, c_sem = _carry_operands(*carry) if carry else ([], [], [], [], None)
    n_c = len(c_in)

    def body(*refs):
        x_ref, nw_ref, dx2_ref, dgdn_ref, dz_ref, dfox_ref, dfg_ref, dps_ref, wm_ref, ws_ref = refs[:10]
        gx_ref, dnw_ref = refs[10 + n_c:12 + n_c]
        if carry:
            _carry_step(carry[0], refs[10:10 + n_c], refs[12 + n_c], refs[13 + n_c], refs[-1], pl.program_id(0), t // tb)
        dh = _dot(dgdn_ref[...], wm_ref[0:w3, :])
        dh += _dot(dz_ref[...], wm_ref[w3:ROWS_A, :])
        dh += _dot(dfox_ref[...], wm_ref[ROWS_B:ROWS_B + w3, :])
        dh += _dot(dfg_ref[...], wm_ref[ROWS_B + w3:ROWS_C, :])
        dh += _dot(dps_ref[...], ws_ref[...])
        xv = x_ref[...]
        r = lax.rsqrt(jnp.mean(xv * xv, axis=-1, keepdims=True) + EPS)
        xn = xv * r

        @pl.when(pl.program_id(0) == 0)
        def _():
            dnw_ref[...] = jnp.zeros_like(dnw_ref)

        dnw_ref[...] += jnp.sum(dh * xn, axis=0, keepdims=True)
        g = dh * nw_ref[...]
        gx_ref[0] = dx2_ref[...] + r * (g - xn * jnp.mean(g * xn, axis=-1, keepdims=True))

    def tok(n):
        return pl.BlockSpec((tb, n), lambda i: (i, 0))

    out = pl.pallas_call(
        body, name="inproj_bwd", grid=(t // tb,),
        in_specs=[tok(d), _resident((1, d)), tok(d), tok(w3), tok(WIDTH), tok(w3), tok(WIDTH), tok(LANE),
                  _resident(w_t.shape), _resident(ws_t.shape)] + c_in_specs,
        out_specs=[pl.BlockSpec((1, tb, d), lambda i: (0, i, 0)), pl.BlockSpec((1, d), lambda i: (0, 0))] + c_out_specs,
        out_shape=[jax.ShapeDtypeStruct((1, t, d), F32), jax.ShapeDtypeStruct((1, d), F32)] + c_out_shape,
        input_output_aliases={10 + j: 4 + j for j in range(n_c)},
        scratch_shapes=[c_sem] if carry else [],
        compiler_params=_params(("arbitrary",)),
    )(x, n1w, dx2, dgdn, dz, dfox, dfg, dps, w_t, ws_t, *c_in)
    return (out[0], out[1], _carry_state(out[2:])) if carry else tuple(out)


def _gate_lanes(shape):
    lane = lax.broadcasted_iota(jnp.int32, shape, 1)
    return lane < HEADS, (lane >= HEADS) & (lane < 2 * HEADS), (lane >= 2 * HEADS) & (lane < 3 * HEADS)


def _block_masks():
    ri = lax.broadcasted_iota(jnp.int32, (LANE, LANE), 0)
    ci = lax.broadcasted_iota(jnp.int32, (LANE, LANE), 1)
    same = (ri // CHUNK) == (ci // CHUNK)
    return ((ri >= ci).astype(F32), (ri <= ci).astype(F32), (same & (ri >= ci)).astype(F32),
            (same & (ri <= ci)).astype(F32), same.astype(F32))


def _gates(ps, gparams):
    t = ps.shape[0]
    nb = t // LANE

    def body(ps_ref, gp_ref, out_ref, run_ref, tot_ref, runt_ref):
        p = ps_ref[...]
        is_b, is_a, is_f = _gate_lanes(p.shape)
        z = p + gp_ref[0:1, :]
        neg_exp_a = -jnp.exp(gp_ref[1:2, :])
        glog = neg_exp_a * _softplus(z)
        logf = -_softplus(-z)
        out_ref[...] = jnp.where(is_b, _sigmoid(p), jnp.where(is_a, glog, jnp.where(is_f, logf, 0.0)))
        tril, _, tril_c, _, same_c = _block_masks()
        off = jnp.zeros((1, LANE), F32)
        for b in range(nb):
            rows = slice(b * LANE, (b + 1) * LANE)
            blk = out_ref[rows, :]
            ga = jnp.where(is_a[:LANE], blk, 0.0)
            fb = _hdot(tril, jnp.where(is_f[:LANE], blk, 0.0)) + off
            run = fb + _hdot(tril_c, ga)
            run_ref[rows, :] = run
            runt_ref[:, rows] = run.T
            tot_ref[rows, :] = _hdot(same_c, ga)
            off = fb[LANE - 1:LANE, :]

    return pl.pallas_call(
        body, name="gates",
        out_shape=[jax.ShapeDtypeStruct((t, LANE), F32)] * 3 + [jax.ShapeDtypeStruct((LANE, t), F32)],
        compiler_params=_params(),
    )(ps, gparams)


def _gates_bwd(ps, gparams, dcol_g, dtot_g, dcol_f, drow_g, drow_f):
    t = ps.shape[0]
    nb = t // LANE

    def body(ps_ref, gp_ref, dcg_ref, dtg_ref, dcf_ref, drg_ref, drf_ref, dps_ref, sums_ref, dl_ref, d0_ref, tr_ref):
        p = ps_ref[...]
        is_b, is_a, is_f = _gate_lanes(p.shape)
        _, triu, _, triu_c, same_c = _block_masks()
        tr_ref[...] = jnp.zeros_like(tr_ref)
        off = jnp.zeros((1, LANE), F32)
        for b in reversed(range(nb)):
            rows = slice(b * LANE, (b + 1) * LANE)
            tr_ref[HEADS:2 * HEADS, :] = drg_ref[:, rows]
            tr_ref[2 * HEADS:3 * HEADS, :] = drf_ref[:, rows]
            d = dcg_ref[rows, :] + dcf_ref[rows, :] + tr_ref[...].T
            d0_ref[rows, :] = d
            dlf = _hdot(triu, jnp.where(is_f[:LANE], d, 0.0)) + off
            dla = (_hdot(triu_c, jnp.where(is_a[:LANE], d, 0.0))
                   + _hdot(same_c, jnp.where(is_a[:LANE], dtg_ref[rows, :], 0.0)))
            dl_ref[rows, :] = dlf + dla
            off = dlf[0:1, :]
        z = p + gp_ref[0:1, :]
        neg_exp_a = -jnp.exp(gp_ref[1:2, :])
        sb = _sigmoid(p)
        glog = neg_exp_a * _softplus(z)
        dl = dl_ref[...]
        dp = jnp.where(is_b, d0_ref[...] * sb * (1.0 - sb),
                       jnp.where(is_a, dl * neg_exp_a * _sigmoid(z), jnp.where(is_f, dl * _sigmoid(-z), 0.0)))
        dps_ref[...] = dp
        s_a = jnp.sum(jnp.where(is_a, dl * glog, 0.0), axis=0, keepdims=True)
        s_p = jnp.sum(jnp.where(is_b, 0.0, dp), axis=0, keepdims=True)
        row = lax.broadcasted_iota(jnp.int32, (8, LANE), 0)
        from_a = pltpu.roll(jnp.where(row == 0, s_a, jnp.where(row == 1, s_p, 0.0)), LANE - HEADS, 1)
        from_f = pltpu.roll(jnp.where(row == 2, s_p, 0.0), LANE - 2 * HEADS, 1)
        lane = lax.broadcasted_iota(jnp.int32, (8, LANE), 1)
        sums_ref[...] = jnp.where(lane < HEADS, from_a + from_f, 0.0)

    return pl.pallas_call(
        body, name="gates_bwd",
        out_shape=[jax.ShapeDtypeStruct((t, LANE), F32), jax.ShapeDtypeStruct((8, LANE), F32)],
        scratch_shapes=[pltpu.VMEM((t, LANE), F32), pltpu.VMEM((t, LANE), F32), pltpu.VMEM((LANE, LANE), F32)],
        compiler_params=_params(),
    )(ps, gparams, dcol_g, dtot_g, dcol_f, drow_g, drow_f)


def _conv(xv, w):
    acc = w[3:4, :] * xv
    for s in range(1, 4):
        acc += w[3 - s:4 - s, :] * _shift_down(xv, s)
    return acc


PREP_ROWS = 256
HALO = 8
PREP_UNROLL = 8


def _tile_loop(n, tile, init):
    if n % PREP_UNROLL:
        return lax.fori_loop(0, n, tile, init)

    def trip(g, carry):
        for u in range(PREP_UNROLL):
            carry = tile(g * PREP_UNROLL + u, carry)
        return carry

    return lax.fori_loop(0, n // PREP_UNROLL, trip, init)


def _tile_rows(r):
    return pl.ds(pl.multiple_of(r * PREP_ROWS, PREP_ROWS), PREP_ROWS)


def _gdn_prep(pm, conv_w):
    t = pm.shape[0]
    nj = WIDTH // LANE
    win = PREP_ROWS + HALO

    def body(x_ref, w_ref, o_ref, xp_ref):
        kind = pl.program_id(0)
        xp_ref[0:HALO, :] = jnp.zeros((HALO, LANE), F32)
        xp_ref[HALO:, :] = x_ref[...]
        w = w_ref[...]
        hs = _head_sum_matrix()

        def tile(r, _, normed):
            xw = xp_ref[pl.ds(pl.multiple_of(r * PREP_ROWS, PREP_ROWS), win), :]
            acc = _conv(xw, w)[HALO:]
            out = acc * _sigmoid(acc)
            if normed:
                out = out * lax.rsqrt(_group_sum(out * out, hs) + EPS)
            o_ref[0, 0, _tile_rows(r), :] = out[:, :DH]
            o_ref[0, 1, _tile_rows(r), :] = out[:, DH:]
            return 0

        @pl.when(kind < 2)
        def _():
            _tile_loop(t // PREP_ROWS, functools.partial(tile, normed=True), 0)

        @pl.when(kind == 2)
        def _():
            _tile_loop(t // PREP_ROWS, functools.partial(tile, normed=False), 0)

    return pl.pallas_call(
        body, name="gdn_prep", grid=(3, nj),
        in_specs=[pl.BlockSpec((t, LANE), lambda i, j: (0, i * nj + j)),
                  pl.BlockSpec((4, LANE), lambda i, j: (0, i * nj + j))],
        out_specs=pl.BlockSpec((1, 2, t, DH), lambda i, j: (i, j, 0, 0)),
        out_shape=jax.ShapeDtypeStruct((3, HEADS, t, DH), F32),
        scratch_shapes=[pltpu.VMEM((t + HALO, LANE), F32)],
        compiler_params=_params(("arbitrary", "arbitrary")),
    )(pm, conv_w)


def _gdn_prep_bwd(pm, conv_w, dqkv):
    t = pm.shape[0]
    nj = WIDTH // LANE
    win = PREP_ROWS + 2 * HALO

    def body(x_ref, w_ref, d_ref, dx_ref, dw_ref, xp_ref, dp_ref):
        kind = pl.program_id(0)
        zeros = jnp.zeros((HALO, LANE), F32)
        for ref in (xp_ref, dp_ref):
            ref[0:HALO, :] = zeros
            ref[HALO + t:, :] = zeros
        xp_ref[HALO:HALO + t, :] = x_ref[...]
        dp_ref[HALO:HALO + t, 0:DH] = d_ref[0, 0]
        dp_ref[HALO:HALO + t, DH:] = d_ref[0, 1]
        w = w_ref[...]
        hs = _head_sum_matrix()
        rows = lax.broadcasted_iota(jnp.int32, (win, LANE), 0)
        in_tile = (rows >= HALO) & (rows < HALO + PREP_ROWS)

        def tile(r, dw, normed):
            start = pl.multiple_of(r * PREP_ROWS, PREP_ROWS)
            xw = xp_ref[pl.ds(start, win), :]
            dy = dp_ref[pl.ds(start, win), :]
            acc = _conv(xw, w)
            sg = _sigmoid(acc)
            if normed:
                y = acc * sg
                rn = lax.rsqrt(_group_sum(y * y, hs) + EPS)
                yn = y * rn
                dy = rn * (dy - yn * _group_sum(dy * yn, hs))
            dacc = dy * sg * (1.0 + acc * (1.0 - sg))
            dx = w[3:4, :] * dacc
            for s in range(1, 4):
                dx += w[3 - s:4 - s, :] * _shift_up(dacc, s)
            dx_ref[_tile_rows(r), :] = dx[HALO:HALO + PREP_ROWS].astype(dx_ref.dtype)
            dm = jnp.where(in_tile, dacc, 0.0)
            return tuple(dw[i] + jnp.sum(dm * (xw if i == 3 else _shift_down(xw, 3 - i)), axis=0, keepdims=True)
                         for i in range(4))

        def run(normed):
            dw = _tile_loop(t // PREP_ROWS, functools.partial(tile, normed=normed),
                            tuple(jnp.zeros((1, LANE), F32) for _ in range(4)))
            for i in range(4):
                dw_ref[i:i + 1, :] = dw[i]

        pl.when(kind < 2)(functools.partial(run, True))
        pl.when(kind == 2)(functools.partial(run, False))

    return pl.pallas_call(
        body, name="gdn_prep_bwd", grid=(3, nj),
        in_specs=[pl.BlockSpec((t, LANE), lambda i, j: (0, i * nj + j)),
                  pl.BlockSpec((4, LANE), lambda i, j: (0, i * nj + j)),
                  pl.BlockSpec((1, 2, t, DH), lambda i, j: (i, j, 0, 0))],
        out_specs=[pl.BlockSpec((t, LANE), lambda i, j: (0, i * nj + j)),
                   pl.BlockSpec((4, LANE), lambda i, j: (0, i * nj + j))],
        out_shape=[jax.ShapeDtypeStruct((t, 3 * WIDTH), MXU), jax.ShapeDtypeStruct((4, 3 * WIDTH), F32)],
        scratch_shapes=[pltpu.VMEM((t + 2 * HALO, LANE), F32), pltpu.VMEM((t + 2 * HALO, LANE), F32)],
        compiler_params=_params(("arbitrary", "arbitrary")),
    )(pm, conv_w, dqkv)


FOX_COL0 = 4 * WIDTH // LANE


def _fox_prep(pm, nw):
    t = pm.shape[0]
    nj = WIDTH // LANE

    def body(x_ref, w_ref, o_ref):
        kind = pl.program_id(0)
        hs = _head_sum_matrix()
        wk = w_ref[pl.ds(kind, 1), :]

        def tile(r, _, normed):
            out = x_ref[_tile_rows(r), :]
            if normed:
                out = out * lax.rsqrt(_group_sum(out * out, hs) * (1.0 / DH) + EPS) * wk
            o_ref[0, 0, _tile_rows(r), :] = out[:, :DH]
            o_ref[0, 1, _tile_rows(r), :] = out[:, DH:]
            return 0

        @pl.when(kind < 2)
        def _():
            _tile_loop(t // PREP_ROWS, functools.partial(tile, normed=True), 0)

        @pl.when(kind == 2)
        def _():
            _tile_loop(t // PREP_ROWS, functools.partial(tile, normed=False), 0)

    return pl.pallas_call(
        body, name="fox_prep", grid=(3, nj),
        in_specs=[pl.BlockSpec((t, LANE), lambda i, j: (0, FOX_COL0 + i * nj + j)),
                  pl.BlockSpec((3, LANE), lambda i, j: (0, 0))],
        out_specs=pl.BlockSpec((1, 2, t, DH), lambda i, j: (i, j, 0, 0)),
        out_shape=jax.ShapeDtypeStruct((3, HEADS, t, DH), F32),
        compiler_params=_params(("arbitrary", "arbitrary")),
    )(pm, nw)


def _fox_prep_bwd(pm, nw, dqkv):
    t = pm.shape[0]
    nj = WIDTH // LANE

    def body(x_ref, w_ref, d_ref, dx_ref, dw_ref):
        kind = pl.program_id(0)
        hs = _head_sum_matrix()
        wk = w_ref[pl.ds(kind, 1), :]

        def tile(r, dw):
            xv = x_ref[_tile_rows(r), :]
            rn = lax.rsqrt(_group_sum(xv * xv, hs) * (1.0 / DH) + EPS)
            xn = xv * rn
            d = jnp.concatenate([d_ref[0, 0, _tile_rows(r), :], d_ref[0, 1, _tile_rows(r), :]], axis=1)
            g = d * wk
            dx_ref[_tile_rows(r), :] = (rn * (g - xn * _group_sum(g * xn, hs) * (1.0 / DH))).astype(dx_ref.dtype)
            return dw + jnp.sum(d * xn, axis=0, keepdims=True)

        def copy_tile(r, _):
            dx_ref[_tile_rows(r), :] = jnp.concatenate([d_ref[0, 0, _tile_rows(r), :], d_ref[0, 1, _tile_rows(r), :]],
                                                       axis=1).astype(dx_ref.dtype)
            return 0

        @pl.when(kind < 2)
        def _():
            dw_ref[0, 0] = _tile_loop(t // PREP_ROWS, tile, jnp.zeros((1, LANE), F32))

        @pl.when(kind == 2)
        def _():
            _tile_loop(t // PREP_ROWS, copy_tile, 0)
            dw_ref[0, 0] = jnp.zeros((1, LANE), F32)

    return pl.pallas_call(
        body, name="fox_prep_bwd", grid=(3, nj),
        in_specs=[pl.BlockSpec((t, LANE), lambda i, j: (0, FOX_COL0 + i * nj + j)),
                  pl.BlockSpec((3, LANE), lambda i, j: (0, 0)),
                  pl.BlockSpec((1, 2, t, DH), lambda i, j: (i, j, 0, 0))],
        out_specs=[pl.BlockSpec((t, LANE), lambda i, j: (0, i * nj + j)),
                   pl.BlockSpec((1, 1, 1, LANE), lambda i, j: (i, j, 0, 0))],
        out_shape=[jax.ShapeDtypeStruct((t, 3 * WIDTH), MXU), jax.ShapeDtypeStruct((3, nj, 1, LANE), F32)],
        compiler_params=_params(("arbitrary", "arbitrary")),
    )(pm, nw, dqkv)


SC = 256
CPS = SC // CHUNK
GDN_HP_FWD = 4
GDN_HP_BWD = 2
Q_SCALE = DH ** -0.5


def _sc_masks():
    ri = lax.broadcasted_iota(jnp.int32, (SC, SC), 0)
    ci = lax.broadcasted_iota(jnp.int32, (SC, SC), 1)
    same = (ri // CHUNK) == (ci // CHUNK)
    return same & (ri >= ci), same & (ri > ci), ri == ci


def _unit_lower_inverses(ms, eye):
    invs = [jnp.where(eye, 1.0, 0.0) + m for m in ms]
    ms = [_dot(m, m) for m in ms]
    for _ in range(4):
        both = [_dot(jnp.concatenate([inv, m], axis=0), m) for inv, m in zip(invs, ms)]
        invs = [inv + b[:SC] for inv, b in zip(invs, both)]
        ms = [b[SC:] for b in both]
    return [inv + _dot(inv, m) for inv, m in zip(invs, ms)]


def _lane_col(blk, lane_idx):
    lane = lax.broadcasted_iota(jnp.int32, blk.shape, 1)
    return jnp.sum(jnp.where(lane == lane_idx, blk, 0.0), axis=1, keepdims=True)


def _to_lane(col, lane_idx):
    lane = lax.broadcasted_iota(jnp.int32, (col.shape[0], LANE), 1)
    return jnp.where(lane == lane_idx, col, 0.0)


def _gdn_columns(gates_ref, run_ref, tot_ref, runt_ref, rows, h):
    return (_lane_col(gates_ref[rows, :], h), _lane_col(run_ref[rows, :], HEADS + h),
            _lane_col(tot_ref[rows, :], HEADS + h), runt_ref[pl.ds(h, 1), rows])


def _gdn_local(q, k, beta, gc, gl, grow, causal, with_kk=True):
    decay = jnp.exp(jnp.where(causal, gc - grow, -1e30))
    egc = jnp.exp(gc)
    ekd = jnp.exp(gl - gc)
    qs = q * Q_SCALE
    kb = k * beta
    if with_kk:
        both = _dot_nt(jnp.concatenate([kb, qs], axis=0), k)
        kk, qk = both[:SC], both[SC:]
    else:
        kk, qk = None, _dot_nt(qs, k)
    return beta, gl, decay, egc, ekd, qs, kb, kk, qk, jnp.where(causal, qk * decay, 0.0)


def _chunk_rows(c):
    return pl.ds(c * CHUNK if isinstance(c, int) else pl.multiple_of(c * CHUNK, CHUNK), CHUNK)


def _sc_rows(b):
    return pl.ds(b * SC if isinstance(b, int) else pl.multiple_of(b * SC, SC), SC)


def _gdn_fwd(qkv, gates, run, tot, run_t):
    t = qkv.shape[2]
    nc = t // CHUNK
    nsc = t // SC

    hp_n = GDN_HP_FWD
    heads = range(hp_n)

    def body(qkv_ref, gates_ref, run_ref, tot_ref, runt_ref, o_ref, st_ref, inv_ref, kc_s, qc_s, g_s, au_s):
        hp = pl.program_id(0)
        causal, strict, eye = _sc_masks()

        def local(b, _):
            rows = _sc_rows(b)
            loc = [_gdn_local(qkv_ref[0, hh, rows, :], qkv_ref[1, hh, rows, :],
                              *_gdn_columns(gates_ref, run_ref, tot_ref, runt_ref, rows, hp * hp_n + hh), causal)
                   for hh in heads]
            invs = _unit_lower_inverses([-jnp.where(strict, l[7] * l[2], 0.0) for l in loc], eye)
            uws = []
            for hh in heads:
                beta, _, _, egc, _, _, kb, _, _, _ = loc[hh]
                inv_ref[hh, rows, :] = invs[hh].astype(inv_ref.dtype)
                uws.append(_dot(invs[hh], jnp.concatenate([qkv_ref[2, hh, rows, :] * beta, kb * egc], axis=1)))
            for hh in heads:
                _, _, _, egc, ekd, qs, _, _, _, attn = loc[hh]
                auw = _dot(attn, uws[hh])
                g_s[hh, rows, :] = qs * egc - auw[:, DH:]
                au_s[hh, rows, :] = auw[:, :DH]
                kd = qkv_ref[1, hh, rows, :] * ekd
                for j in range(CPS):
                    sl = slice(j * CHUNK, (j + 1) * CHUNK)
                    both = _dot_tn(kd[sl], uws[hh][sl])
                    kc_s[hh, b * CPS + j] = both[:, DH:]
                    qc_s[hh, b * CPS + j] = both[:, :DH]
            return 0

        def step(c, states):
            rows = _chunk_rows(c)
            tot_row = tot_ref[pl.ds(c * CHUNK, 1), :]
            new = []
            for hh in heads:
                s = states[hh]
                st_ref[hh, c] = s
                o_ref[hh, rows, :] = _dot(g_s[hh, rows, :], s) + au_s[hh, rows, :]
                egl = jnp.exp(_lane_col(tot_row, HEADS + hp * hp_n + hh))
                new.append(egl * s - _dot(kc_s[hh, c], s) + qc_s[hh, c])
            return tuple(new)

        def steps_of(b, states):
            for j in range(CPS):
                states = step(b * CPS + j, states)
            return states

        def fused(b, states):
            states = steps_of(b - 1, states)
            local(b, 0)
            return states

        local(0, 0)
        states = lax.fori_loop(1, nsc, fused, tuple(jnp.zeros((DH, DH), F32) for _ in heads))
        steps_of(nsc - 1, states)

    whole = pl.BlockSpec((t, LANE), lambda h: (0, 0))
    once = dict(pipeline_mode=pl.Buffered(1))
    return pl.pallas_call(
        body, name="gdn_fwd", grid=(HEADS // hp_n,),
        in_specs=[pl.BlockSpec((3, hp_n, t, DH), lambda h: (0, h, 0, 0), **once), whole, whole, whole,
                  pl.BlockSpec((HEADS, t), lambda h: (1, 0))],
        out_specs=[pl.BlockSpec((hp_n, t, DH), lambda h: (h, 0, 0), **once),
                   pl.BlockSpec((hp_n, nc, DH, DH), lambda h: (h, 0, 0, 0), **once),
                   pl.BlockSpec((hp_n, t, SC), lambda h: (h, 0, 0), **once)],
        out_shape=[jax.ShapeDtypeStruct((HEADS, t, DH), F32), jax.ShapeDtypeStruct((HEADS, nc, DH, DH), F32),
                   jax.ShapeDtypeStruct((HEADS, t, SC), MXU)],
        scratch_shapes=[pltpu.VMEM((hp_n, nc, DH, DH), F32), pltpu.VMEM((hp_n, nc, DH, DH), F32),
                        pltpu.VMEM((hp_n, t, DH), F32), pltpu.VMEM((hp_n, t, DH), F32)],
        compiler_params=_params(("arbitrary",)),
    )(qkv, gates, run, tot, run_t)


def _gdn_bwd(qkv, gates, run, tot, run_t, inv, states, do):
    t = qkv.shape[2]
    nc = t // CHUNK
    nsc = t // SC

    hp_n = GDN_HP_BWD
    heads = range(hp_n)

    def body(qkv_ref, gates_ref, run_ref, tot_ref, runt_ref, inv_ref, st_ref, do_ref,
             dqkv_ref, dcol_ref, dtot_ref, drow_ref, uw_s, kc_s, h_s, dsn_s):
        hp = pl.program_id(0)
        causal, strict, _ = _sc_masks()

        @pl.when(hp == 0)
        def _():
            dcol_ref[...] = jnp.zeros_like(dcol_ref)
            dtot_ref[...] = jnp.zeros_like(dtot_ref)

        def local_of(hh, rows, with_kk=True):
            return _gdn_local(qkv_ref[0, hh, rows, :], qkv_ref[1, hh, rows, :],
                              *_gdn_columns(gates_ref, run_ref, tot_ref, runt_ref, rows, hp * hp_n + hh), causal, with_kk)

        def local(b, _):
            rows = _sc_rows(b)
            loc = [local_of(hh, rows, with_kk=False) for hh in heads]
            us, ws = [], []
            for hh in heads:
                beta, _, _, egc, _, _, kb, _, _, _ = loc[hh]
                inv_b = inv_ref[hh, rows, :]
                us.append(_dot(inv_b, qkv_ref[2, hh, rows, :] * beta))
                ws.append(_dot(inv_b, kb * egc))
            gs = [loc[hh][5] * loc[hh][3] - _dot(loc[hh][9], ws[hh]) for hh in heads]
            for hh in heads:
                uw_s[hh, rows, :] = jnp.concatenate([us[hh], ws[hh]], axis=1)
                kd = qkv_ref[1, hh, rows, :] * loc[hh][4]
                dout = do_ref[hh, rows, :]
                for j in range(CPS):
                    sl = slice(j * CHUNK, (j + 1) * CHUNK)
                    kc_s[hh, b * CPS + j] = _dot_tn(kd[sl], ws[hh][sl])
                    h_s[hh, b * CPS + j] = _dot_tn(gs[hh][sl], dout[sl])
            return 0

        def step(c, dss):
            tot_row = tot_ref[pl.ds(c * CHUNK, 1), :]
            new = []
            for hh in heads:
                ds = dss[hh]
                dsn_s[hh, c] = ds
                egl = jnp.exp(_lane_col(tot_row, HEADS + hp * hp_n + hh))
                new.append(egl * ds - _dot_tn(kc_s[hh, c], ds) + h_s[hh, c])
            return tuple(new)

        def steps_of(b, dss):
            for j in reversed(range(CPS)):
                dss = step(b * CPS + j, dss)
            return dss

        def fused(i, dss):
            b = nsc - 2 - i
            dss = steps_of(b + 1, dss)
            local(b, 0)
            return dss

        local(nsc - 1, 0)
        dss = lax.fori_loop(0, nsc - 1, fused, tuple(jnp.zeros((DH, DH), F32) for _ in heads))
        steps_of(0, dss)

        def back(b, _):
            rows = _sc_rows(b)
            first = lax.broadcasted_iota(jnp.int32, (CHUNK, 1), 0) == 0
            loc = [local_of(hh, rows) for hh in heads]
            sign = jnp.where(lax.broadcasted_iota(jnp.int32, (1, LANE), 1) < DH, 1.0, -1.0)
            mid = []
            for hh in heads:
                beta, gl, decay, egc, ekd, qs, kb, kk, qk, attn = loc[hh]
                uw = uw_s[hh, rows, :]
                kd = qkv_ref[1, hh, rows, :] * ekd
                dout = do_ref[hh, rows, :]
                dg_p, dkd_p, duw_p, dgl_p = [], [], [], []
                for j in range(CPS):
                    sl = slice(j * CHUNK, (j + 1) * CHUNK)
                    s = st_ref[hh, b * CPS + j]
                    dsn = dsn_s[hh, b * CPS + j]
                    both = _dot_nt(jnp.concatenate([dsn, dout[sl]], axis=0), s)
                    dg_p.append(both[CHUNK:])
                    dsc = jnp.concatenate([dsn, -both[:CHUNK]], axis=1)
                    dkd_p.append(_dot_nt(uw[sl], dsc))
                    duw_p.append(_dot(kd[sl], dsc))
                    degl = jnp.sum(jnp.sum(s * dsn, axis=1, keepdims=True), axis=0, keepdims=True)
                    dgl_p.append(jnp.where(first, degl * jnp.exp(gl[j * CHUNK:j * CHUNK + 1, :]), 0.0))
                dg, dkd = jnp.concatenate(dg_p, axis=0), jnp.concatenate(dkd_p, axis=0)
                dod = jnp.concatenate([dout, dg], axis=1)
                da = jnp.where(causal, _dot_nt(dod, uw * sign), 0.0)
                duw = jnp.concatenate(duw_p, axis=0) + _dot_tn(attn, dod) * sign
                mid.append((kd, dg, dkd, da, duw, jnp.concatenate(dgl_p, axis=0)))
            inv_ts = [inv_ref[hh, rows, :].astype(F32).T for hh in heads]
            its = [_dot(inv_ts[hh], mid[hh][4]) for hh in heads]
            dinvs = []
            for hh in heads:
                beta, _, _, egc, _, _, kb, _, _, _ = loc[hh]
                dinvs.append(_dot_nt(mid[hh][4], jnp.concatenate([qkv_ref[2, hh, rows, :] * beta, kb * egc], axis=1)))
            half = [_dot(inv_ts[hh], dinvs[hh]) for hh in heads]
            dls = [jnp.where(strict, -_dot(half[hh], inv_ts[hh]), 0.0) for hh in heads]
            for hh in heads:
                head = hp * hp_n + hh
                beta, gl, decay, egc, ekd, qs, kb, kk, qk, attn = loc[hh]
                kd, dg, dkd, da, _, dgl_first = mid[hh]
                k, v = qkv_ref[1, hh, rows, :], qkv_ref[2, hh, rows, :]
                dvb, dkbe = its[hh][:, :DH], its[hh][:, DH:]
                dl = dls[hh]
                dlogd = (dl * kk + da * qk) * decay
                dd = jnp.concatenate([dl * decay, da * decay], axis=0)
                ddk = _dot(dd, k)
                dkb = ddk[:SC] + dkbe * egc
                dqs = ddk[SC:] + dg * egc
                dk = _dot_tn(dd, jnp.concatenate([kb, qs], axis=0)) + dkd * ekd + dkb * beta
                dkd_kd = jnp.sum(dkd * kd, axis=1, keepdims=True)
                narrow = dg * (qs * egc) + dkbe * (kb * egc)
                wide = dlogd[:, :LANE] + dlogd[:, LANE:] + jnp.concatenate([narrow, jnp.zeros((SC, LANE - DH), F32)], axis=1)
                dgc = jnp.sum(wide, axis=1, keepdims=True) - dkd_kd
                dbeta = jnp.sum(dkb * k + dvb * v, axis=1, keepdims=True)
                dqkv_ref[0, hh, rows, :] = dqs * Q_SCALE
                dqkv_ref[1, hh, rows, :] = dk
                dqkv_ref[2, hh, rows, :] = dvb * beta
                dcol_ref[rows, :] += _to_lane(dbeta, head) + _to_lane(dgc, HEADS + head)
                dtot_ref[rows, :] += _to_lane(dkd_kd + dgl_first, HEADS + head)
                drow_ref[pl.ds(head, 1), rows] = -jnp.sum(dlogd, axis=0, keepdims=True)
            return 0

        lax.fori_loop(0, nsc, back, 0)

    whole = pl.BlockSpec((t, LANE), lambda h: (0, 0))
    rowspec = pl.BlockSpec((HEADS, t), lambda h: (0, 0))
    sq = pltpu.VMEM((hp_n, nc, DH, DH), F32)
    per_head = pltpu.VMEM((hp_n, t, 2 * DH), F32)
    return pl.pallas_call(
        body, name="gdn_bwd", grid=(HEADS // hp_n,),
        in_specs=[pl.BlockSpec((3, hp_n, t, DH), lambda h: (0, h, 0, 0)), whole, whole, whole,
                  pl.BlockSpec((HEADS, t), lambda h: (1, 0)),
                  pl.BlockSpec((hp_n, t, SC), lambda h: (h, 0, 0)), pl.BlockSpec((hp_n, nc, DH, DH), lambda h: (h, 0, 0, 0)),
                  pl.BlockSpec((hp_n, t, DH), lambda h: (h, 0, 0))],
        out_specs=[pl.BlockSpec((3, hp_n, t, DH), lambda h: (0, h, 0, 0)), whole, whole, rowspec],
        out_shape=[jax.ShapeDtypeStruct((3, HEADS, t, DH), F32), jax.ShapeDtypeStruct((t, LANE), F32),
                   jax.ShapeDtypeStruct((t, LANE), F32), jax.ShapeDtypeStruct((HEADS, t), F32)],
        scratch_shapes=[per_head, sq, sq, sq],
        compiler_params=_params(("arbitrary",)),
    )(qkv, gates, run, tot, run_t, inv, states, do)


FOX_HP_FWD = 4
FOX_HP_BWD = 4


def _wide_t(a):
    r = a.shape[0]
    return jnp.concatenate([a, jnp.zeros((r, LANE - DH), F32)], axis=1).T[:DH]


def _tall_t(a):
    r = a.shape[1]
    return jnp.concatenate([a, jnp.zeros((LANE - DH, r), F32)], axis=0).T[:, :DH]


def _key_side_f(run_blk, head, qb):
    col = jnp.broadcast_to(_lane_col(run_blk, 2 * HEADS + head), (run_blk.shape[0], LANE))
    return jnp.concatenate([col] * (qb // LANE), axis=1)


def _diag_mask(qb):
    return lax.broadcasted_iota(jnp.int32, (qb, qb), 0) <= lax.broadcasted_iota(jnp.int32, (qb, qb), 1)


def _fox_fwd(qkv, run, run_t, carry=None):
    t = qkv.shape[2]
    qb = min(QB, t)
    nq = t // qb
    hp_n = FOX_HP_FWD
    c_in, c_in_specs, c_out_shape, c_out_specs, c_sem = _carry_operands(*carry) if carry else ([], [], [], [], None)
    n_c = len(c_in)

    def body(*refs):
        q_ref, k_ref, v_ref, run_ref, runt_ref = refs[:5]
        o_ref, lse_ref = refs[5 + n_c:7 + n_c]
        s0 = 7 + n_c + len(c_out_shape)
        kb_s, vt_s, raw_s, m_s, l_s, acc_s = refs[s0:s0 + 6]
        hp = pl.program_id(0)
        i = pl.program_id(1)

        if carry:
            _carry_step(carry[0], refs[5:5 + n_c], refs[7 + n_c], refs[8 + n_c], refs[-1], hp * nq + i,
                        (HEADS // hp_n) * nq)

        @pl.when(i == 0)
        def _():
            for hh in range(hp_n):
                kb_s[hh] = k_ref[0, hh].astype(MXU)
                for b in range(nq):
                    rows = slice(b * qb, (b + 1) * qb)
                    vt_s[hh, :, rows] = _wide_t(v_ref[0, hh, rows, :]).astype(MXU)

        qrows = pl.ds(pl.multiple_of(i * qb, qb), qb)
        qs = [(q_ref[0, hh] * Q_SCALE).astype(MXU) for hh in range(hp_n)]
        fq = [runt_ref[pl.ds(hp * hp_n + hh, 1), qrows] for hh in range(hp_n)]

        def block_rows(j):
            return pl.ds(pl.multiple_of(j * qb, qb), qb)

        def scores(j):
            for hh in range(hp_n):
                raw_s[j % 2, hh] = _dot_nt(kb_s[hh, block_rows(j), :], qs[hh])

        def absorb(j, diagonal):
            rows = block_rows(j)
            run_blk = run_ref[rows, :]
            stats, pv = [], []
            for hh in range(hp_n):
                m, l = m_s[hh], l_s[hh]
                st = raw_s[j % 2, hh] + (fq[hh] - _key_side_f(run_blk, hp * hp_n + hh, qb))
                if diagonal:
                    st = jnp.where(_diag_mask(qb), st, -1e30)
                m_new = jnp.maximum(m, jnp.max(st, axis=0, keepdims=True))
                p = jnp.exp(st - m_new)
                alpha = jnp.exp(m - m_new)
                stats.append((m_new, alpha * l + jnp.sum(p, axis=0, keepdims=True), alpha))
                pv.append(_dot(vt_s[hh, :, rows], p))
            for hh in range(hp_n):
                m_s[hh], l_s[hh] = stats[hh][0], stats[hh][1]
                acc_s[hh] = stats[hh][2] * acc_s[hh] + pv[hh]

        def kstep(j, _):
            scores(j + 1)
            absorb(j, False)
            return 0

        for hh in range(hp_n):
            m_s[hh] = jnp.full((1, qb), -1e30, F32)
            l_s[hh] = jnp.zeros((1, qb), F32)
            acc_s[hh] = jnp.zeros((DH, qb), F32)
        scores(0)
        lax.fori_loop(0, i, kstep, 0)
        absorb(i, True)
        for hh in range(hp_n):
            l = l_s[hh]
            o_ref[hh] = _tall_t(acc_s[hh] / l)
            lse_ref[pl.ds(hp * hp_n + hh, 1), qrows] = m_s[hh] + jnp.log(l)

    out = pl.pallas_call(
        body, name="fox_fwd", grid=(HEADS // hp_n, nq),
        in_specs=[pl.BlockSpec((1, hp_n, qb, DH), lambda h, i: (0, h, i, 0)),
                  pl.BlockSpec((1, hp_n, t, DH), lambda h, i: (1, h, 0, 0)),
                  pl.BlockSpec((1, hp_n, t, DH), lambda h, i: (2, h, 0, 0)),
                  pl.BlockSpec((t, LANE), lambda h, i: (0, 0)),
                  pl.BlockSpec((HEADS, t), lambda h, i: (2, 0))] + c_in_specs,
        out_specs=[pl.BlockSpec((hp_n, qb, DH), lambda h, i: (h, i, 0)),
                   pl.BlockSpec((HEADS, t), lambda h, i: (0, 0))] + c_out_specs,
        out_shape=[jax.ShapeDtypeStruct((HEADS, t, DH), F32), jax.ShapeDtypeStruct((HEADS, t), F32)] + c_out_shape,
        input_output_aliases={5 + j: 4 + j for j in range(n_c)},
        scratch_shapes=[pltpu.VMEM((hp_n, t, DH), MXU), pltpu.VMEM((hp_n, DH, t), MXU), pltpu.VMEM((2, hp_n, qb, qb), F32),
                        pltpu.VMEM((hp_n, 1, qb), F32), pltpu.VMEM((hp_n, 1, qb), F32), pltpu.VMEM((hp_n, DH, qb), F32)]
        + ([c_sem] if carry else []),
        compiler_params=_params(("arbitrary", "arbitrary")),
    )(qkv, qkv, qkv, run, run_t, *c_in)
    return (out[0], out[1], _carry_state(out[2:])) if carry else (out[0], out[1])


def _fox_bwd(qkv, run, run_t, o, lse, do, carry=None):
    t = qkv.shape[2]
    qb = min(QB, t)
    nq = t // qb
    hp_n = FOX_HP_BWD
    c_in, c_in_specs, c_out_shape, c_out_specs, c_sem = _carry_operands(*carry) if carry else ([], [], [], [], None)
    n_c = len(c_in)

    def body(*refs):
        q_ref, k_ref, v_ref, run_ref, runt_ref, o_ref, lse_ref, do_ref = refs[:8]
        dqkv_ref, dcol_ref, drow_ref = refs[8 + n_c:11 + n_c]
        s0 = 11 + n_c + len(c_out_shape)
        dqt_s, raw_s, dpt_s, dro_s, dk_s, dv_s, dsum_s = refs[s0:s0 + 7]
        hp = pl.program_id(0)
        j = pl.program_id(1)

        if carry:
            _carry_step(carry[0], refs[8:8 + n_c], refs[11 + n_c], refs[12 + n_c], refs[-1], hp * nq + j,
                        (HEADS // hp_n) * nq)

        @pl.when(j == 0)
        def _():
            dqt_s[...] = jnp.zeros_like(dqt_s)

        @pl.when((j == 0) & (hp == 0))
        def _():
            dcol_ref[...] = jnp.zeros_like(dcol_ref)
            drow_ref[...] = jnp.zeros_like(drow_ref)

        krows = pl.ds(pl.multiple_of(j * qb, qb), qb)
        run_blk = run_ref[krows, :]
        ones8 = jnp.ones((8, DH), MXU)
        kb, kt, vb, fk = [], [], [], []
        for hh in range(hp_n):
            kf = k_ref[0, hh]
            kb.append(kf.astype(MXU))
            kt.append(_wide_t(kf).astype(MXU))
            vb.append(v_ref[0, hh].astype(MXU))
            fk.append(_key_side_f(run_blk, hp * hp_n + hh, qb))

        def block_rows(i):
            return pl.ds(i * qb if isinstance(i, int) else pl.multiple_of(i * qb, qb), qb)

        def products(i):
            rows = block_rows(i)
            slot = i % 2
            for hh in range(hp_n):
                dout = do_ref[hh, rows, :]
                x = dout * o_ref[hh, rows, :]
                x_hi = x.astype(MXU)
                raw_s[slot, hh] = _dot_nt(kb[hh], q_ref[0, hh, rows, :] * Q_SCALE)
                dpt_s[slot, hh] = _dot_nt(vb[hh], dout)
                dro_s[slot, hh] = _dot_nt(ones8, x_hi) + _dot_nt(ones8, x - x_hi.astype(F32))

        def absorb(i, diagonal):
            rows = block_rows(i)
            slot = i % 2
            pieces = []
            for hh in range(hp_n):
                head = hp * hp_n + hh
                raw, dpt, drow = raw_s[slot, hh], dpt_s[slot, hh], dro_s[slot, hh, 0:1, :]
                off = runt_ref[pl.ds(head, 1), rows] - lse_ref[pl.ds(head, 1), rows]
                st = raw + (off - fk[hh])
                if diagonal:
                    st = jnp.where(_diag_mask(qb), st, -1e30)
                pt = jnp.exp(st)
                dst = pt * (dpt - drow)
                drow_ref[pl.ds(head, 1), rows] += jnp.sum(dst, axis=0, keepdims=True)
                folded = dst[:, 0:LANE]
                for c in range(1, qb // LANE):
                    folded = folded + dst[:, c * LANE:(c + 1) * LANE]
                pieces.append((_dot(dst, q_ref[0, hh, rows, :] * Q_SCALE), _dot(pt, do_ref[hh, rows, :]),
                               _dot(kt[hh], dst), folded))
            for hh in range(hp_n):
                dqt_s[hh, :, rows] += pieces[hh][2]
                if diagonal:
                    dk_s[hh], dv_s[hh], dsum_s[hh] = pieces[hh][0], pieces[hh][1], pieces[hh][3]
                else:
                    dk_s[hh] += pieces[hh][0]
                    dv_s[hh] += pieces[hh][1]
                    dsum_s[hh] += pieces[hh][3]

        def qstep(i, _):
            products(i + 1)
            absorb(i, False)
            return 0

        products(j)
        products(jnp.minimum(j + 1, nq - 1))
        absorb(j, True)
        lax.fori_loop(j + 1, nq - 1, qstep, 0)

        @pl.when(j < nq - 1)
        def _():
            absorb(nq - 1, False)
        for hh in range(hp_n):
            dqkv_ref[1, hh, krows, :] = dk_s[hh]
            dqkv_ref[2, hh, krows, :] = dv_s[hh]
            dcol_ref[krows, :] += _to_lane(-jnp.sum(dsum_s[hh], axis=1, keepdims=True), 2 * HEADS + hp * hp_n + hh)

        @pl.when(j == nq - 1)
        def _():
            for hh in range(hp_n):
                for b in range(nq):
                    rows = slice(b * qb, (b + 1) * qb)
                    dqkv_ref[0, hh, rows, :] = _tall_t(dqt_s[hh, :, rows]) * Q_SCALE

    once = dict(pipeline_mode=pl.Buffered(1))
    full = pl.BlockSpec((hp_n, t, DH), lambda h, j: (h, 0, 0), **once)
    rows8 = pl.BlockSpec((HEADS, t), lambda h, j: (0, 0))
    out = pl.pallas_call(
        body, name="fox_bwd", grid=(HEADS // hp_n, nq),
        in_specs=[pl.BlockSpec((1, hp_n, t, DH), lambda h, j: (0, h, 0, 0), **once),
                  pl.BlockSpec((1, hp_n, qb, DH), lambda h, j: (1, h, j, 0)),
                  pl.BlockSpec((1, hp_n, qb, DH), lambda h, j: (2, h, j, 0)),
                  pl.BlockSpec((t, LANE), lambda h, j: (0, 0), **once), pl.BlockSpec((HEADS, t), lambda h, j: (2, 0)),
                  full, rows8, full] + c_in_specs,
        out_specs=[pl.BlockSpec((3, hp_n, t, DH), lambda h, j: (0, h, 0, 0), **once),
                   pl.BlockSpec((t, LANE), lambda h, j: (0, 0)), rows8] + c_out_specs,
        out_shape=[jax.ShapeDtypeStruct((3, HEADS, t, DH), F32), jax.ShapeDtypeStruct((t, LANE), F32),
                   jax.ShapeDtypeStruct((HEADS, t), F32)] + c_out_shape,
        input_output_aliases={8 + j: 5 + j for j in range(n_c)},
        scratch_shapes=[pltpu.VMEM((hp_n, DH, t), F32), pltpu.VMEM((2, hp_n, qb, qb), F32), pltpu.VMEM((2, hp_n, qb, qb), F32),
                        pltpu.VMEM((2, hp_n, 8, qb), F32), pltpu.VMEM((hp_n, qb, DH), F32), pltpu.VMEM((hp_n, qb, DH), F32),
                        pltpu.VMEM((hp_n, qb, LANE), F32)] + ([c_sem] if carry else []),
        compiler_params=_params(("arbitrary", "arbitrary")),
    )(qkv, qkv, qkv, run, run_t, o, lse, do, *c_in)
    return (out[0], out[1], out[2], _carry_state(out[3:])) if carry else tuple(out)


Z_COL0 = 3 * WIDTH // LANE
FGATE_COL0 = 7 * WIDTH // LANE


def _gdn_post(o, pm, onw):
    t = pm.shape[0]

    def body(o_ref, z_ref, w_ref, m_ref):
        z = z_ref[...]
        sz = z * _sigmoid(z)
        w2 = jnp.concatenate([w_ref[...], w_ref[...]], axis=1)
        ov = jnp.concatenate([o_ref[0], o_ref[1]], axis=1)
        n = ov * lax.rsqrt(_group_sum(ov * ov, _head_sum_matrix()) * (1.0 / DH) + EPS) * w2
        m_ref[...] = (n * sz).astype(m_ref.dtype)

    return pl.pallas_call(
        body, name="gdn_post", grid=(WIDTH // LANE,),
        in_specs=[pl.BlockSpec((2, t, DH), lambda j: (j, 0, 0)), pl.BlockSpec((t, LANE), lambda j: (0, Z_COL0 + j)),
                  pl.BlockSpec((1, DH), lambda j: (0, 0))],
        out_specs=pl.BlockSpec((t, LANE), lambda j: (0, j)),
        out_shape=jax.ShapeDtypeStruct((t, WIDTH), MXU),
        compiler_params=_params(("arbitrary",)),
    )(o, pm, onw)


def _gdn_post_bwd(o, pm, onw, dmix):
    t = pm.shape[0]

    def body(o_ref, z_ref, w_ref, dm_ref, do_ref, dz_ref, dw_ref):
        @pl.when(pl.program_id(0) == 0)
        def _():
            dw_ref[...] = jnp.zeros_like(dw_ref)

        z = z_ref[...]
        sg = _sigmoid(z)
        sz = z * sg
        dsz = sg * (1.0 + z * (1.0 - sg))
        dm = dm_ref[...]
        hs = _head_sum_matrix()
        w2 = jnp.concatenate([w_ref[...], w_ref[...]], axis=1)
        ov = jnp.concatenate([o_ref[0], o_ref[1]], axis=1)
        r = lax.rsqrt(_group_sum(ov * ov, hs) * (1.0 / DH) + EPS)
        xn = ov * r
        dn = dm * sz
        dz_ref[...] = (dm * (xn * w2) * dsz).astype(dz_ref.dtype)
        dw2 = jnp.sum(dn * xn, axis=0, keepdims=True)
        dw_ref[...] += dw2[:, :DH] + dw2[:, DH:]
        g = dn * w2
        do = r * (g - xn * (_group_sum(g * xn, hs) * (1.0 / DH)))
        do_ref[0] = do[:, :DH]
        do_ref[1] = do[:, DH:]

    return pl.pallas_call(
        body, name="gdn_post_bwd", grid=(WIDTH // LANE,),
        in_specs=[pl.BlockSpec((2, t, DH), lambda j: (j, 0, 0)), pl.BlockSpec((t, LANE), lambda j: (0, Z_COL0 + j)),
                  pl.BlockSpec((1, DH), lambda j: (0, 0)), pl.BlockSpec((t, LANE), lambda j: (0, j))],
        out_specs=[pl.BlockSpec((2, t, DH), lambda j: (j, 0, 0)), pl.BlockSpec((t, LANE), lambda j: (0, j)),
                   pl.BlockSpec((1, DH), lambda j: (0, 0))],
        out_shape=[jax.ShapeDtypeStruct((HEADS, t, DH), F32), jax.ShapeDtypeStruct((t, WIDTH), MXU),
                   jax.ShapeDtypeStruct((1, DH), F32)],
        compiler_params=_params(("arbitrary",)),
    )(o, pm, onw, dmix)


def _fox_post(o, pm):
    t = pm.shape[0]

    def body(o_ref, g_ref, m_ref):
        m_ref[...] = (jnp.concatenate([o_ref[0], o_ref[1]], axis=1) * _sigmoid(g_ref[...])).astype(m_ref.dtype)

    return pl.pallas_call(
        body, name="fox_post", grid=(WIDTH // LANE,),
        in_specs=[pl.BlockSpec((2, t, DH), lambda j: (j, 0, 0)), pl.BlockSpec((t, LANE), lambda j: (0, FGATE_COL0 + j))],
        out_specs=pl.BlockSpec((t, LANE), lambda j: (0, j)),
        out_shape=jax.ShapeDtypeStruct((t, WIDTH), MXU),
        compiler_params=_params(("arbitrary",)),
    )(o, pm)


def _fox_post_bwd(o, pm, dmix):
    t = pm.shape[0]

    def body(o_ref, g_ref, dm_ref, do_ref, dg_ref):
        sg = _sigmoid(g_ref[...])
        dm = dm_ref[...]
        for hh in range(2):
            cols = slice(hh * DH, (hh + 1) * DH)
            do_ref[hh] = dm[:, cols] * sg[:, cols]
            dg_ref[:, cols] = (dm[:, cols] * o_ref[hh] * (sg * (1.0 - sg))[:, cols]).astype(dg_ref.dtype)

    return pl.pallas_call(
        body, name="fox_post_bwd", grid=(WIDTH // LANE,),
        in_specs=[pl.BlockSpec((2, t, DH), lambda j: (j, 0, 0)), pl.BlockSpec((t, LANE), lambda j: (0, FGATE_COL0 + j)),
                  pl.BlockSpec((t, LANE), lambda j: (0, j))],
        out_specs=[pl.BlockSpec((2, t, DH), lambda j: (j, 0, 0)), pl.BlockSpec((t, LANE), lambda j: (0, j))],
        out_shape=[jax.ShapeDtypeStruct((HEADS, t, DH), F32), jax.ShapeDtypeStruct((t, WIDTH), MXU)],
        compiler_params=_params(("arbitrary",)),
    )(o, pm, dmix)


def _tail(x, mixg, mixf, tgt, wo, n2w, wg_t, wu_t, wd, fw):
    t, d = x.shape
    dff = wd.shape[0]
    tb = min(TB, t)

    def body(x_ref, mg_ref, mf_ref, t_ref, wo_ref, n2_ref, wg_ref, wu_ref, wd_ref, fw_ref,
             h2_ref, act_ref, dgate_ref, dup_ref, dx3_ref, dx2_ref, dmg_ref, dmf_ref, dn2_ref, dfw_ref, loss_ref):
        @pl.when(pl.program_id(0) == 0)
        def _():
            dn2_ref[...] = jnp.zeros_like(dn2_ref)
            dfw_ref[...] = jnp.zeros_like(dfw_ref)
            loss_ref[...] = jnp.zeros_like(loss_ref)

        x2 = x_ref[...] + _dot(mg_ref[...], wo_ref[0:WIDTH, :]) + _dot(mf_ref[...], wo_ref[WIDTH:2 * WIDTH, :])
        r2 = lax.rsqrt(jnp.mean(x2 * x2, axis=-1, keepdims=True) + EPS)
        xn2 = x2 * r2
        h2 = (xn2 * n2_ref[...]).astype(MXU)
        h2_ref[...] = h2
        gate = _dot_nt(h2, wg_ref[...])
        up = _dot_nt(h2, wu_ref[...])
        sg = _sigmoid(gate)
        sl = gate * sg
        act = (sl * up).astype(MXU)
        act_ref[...] = act
        x3 = x2 + _dot(act, wd_ref[...])
        r3 = lax.rsqrt(jnp.mean(x3 * x3, axis=-1, keepdims=True) + EPS)
        xn3 = x3 * r3
        err = xn3 * fw_ref[...] - t_ref[...]
        loss_ref[...] += 0.5 * jnp.sum(jnp.mean(err * err, axis=-1, keepdims=True), axis=0, keepdims=True)
        dy = err * (1.0 / d)
        dfw_ref[...] += jnp.sum(dy * xn3, axis=0, keepdims=True)
        g3 = dy * fw_ref[...]
        dx3 = r3 * (g3 - xn3 * jnp.mean(g3 * xn3, axis=-1, keepdims=True))
        dx3_ref[...] = dx3.astype(MXU)
        dact = _dot_nt(dx3, wd_ref[...])
        dgate = (dact * up * (sg * (1.0 + gate * (1.0 - sg)))).astype(MXU)
        dup = (dact * sl).astype(MXU)
        dgate_ref[...] = dgate
        dup_ref[...] = dup
        dh2 = _dot(dgate, wg_ref[...]) + _dot(dup, wu_ref[...])
        dn2_ref[...] += jnp.sum(dh2 * xn2, axis=0, keepdims=True)
        g2 = dh2 * n2_ref[...]
        dx2 = dx3 + r2 * (g2 - xn2 * jnp.mean(g2 * xn2, axis=-1, keepdims=True))
        dx2_ref[...] = dx2
        dmg_ref[...] = _dot_nt(dx2, wo_ref[0:WIDTH, :])
        dmf_ref[...] = _dot_nt(dx2, wo_ref[WIDTH:2 * WIDTH, :])

    def tok(n):
        return pl.BlockSpec((tb, n), lambda i: (i, 0))

    acc = pl.BlockSpec((1, d), lambda i: (0, 0))
    sds = jax.ShapeDtypeStruct
    return pl.pallas_call(
        body, name="tail", grid=(t // tb,),
        in_specs=[tok(d), tok(WIDTH), tok(WIDTH), tok(d), _resident(wo.shape), _resident((1, d)),
                  _resident(wg_t.shape), _resident(wu_t.shape), _resident(wd.shape), _resident((1, d))],
        out_specs=[tok(d), tok(dff), tok(dff), tok(dff), tok(d), tok(d), tok(WIDTH), tok(WIDTH), acc, acc,
                   pl.BlockSpec((1, 1), lambda i: (0, 0))],
        out_shape=[sds((t, d), MXU), sds((t, dff), MXU), sds((t, dff), MXU), sds((t, dff), MXU), sds((t, d), MXU),
                   sds((t, d), F32), sds((t, WIDTH), F32), sds((t, WIDTH), F32), sds((1, d), F32), sds((1, d), F32),
                   sds((1, 1), F32)],
        compiler_params=_params(("arbitrary",)),
    )(x, mixg, mixf, tgt, wo, n2w, wg_t, wu_t, wd, fw)


def _wgrads(a_list, b, name):
    t, n = b.shape
    ms = [a.shape[1] for a in a_list]
    bms = [256 if m % 256 == 0 else LANE for m in ms]
    nbs = [m // bm for m, bm in zip(ms, bms)]
    k = len(a_list)
    cast = b.dtype != jnp.dtype(MXU)

    def body(*refs):
        a_refs, b_ref, o_refs = refs[:k], refs[k], refs[k + 1:2 * k + 1]
        i = pl.program_id(0)
        if cast:
            @pl.when(i == 0)
            def _():
                refs[-1][...] = b_ref[...].astype(MXU)
        for a_ref, o_ref, nb in zip(a_refs, o_refs, nbs):
            @pl.when(i < nb)
            def _():
                o_ref[...] = _dot_tn(a_ref[...], refs[-1][...] if cast else b_ref[...]).astype(o_ref.dtype)

    def clamp(nb):
        return lambda i: jnp.minimum(i, nb - 1)

    return pl.pallas_call(
        body, name=name, grid=(max(nbs),),
        in_specs=[pl.BlockSpec((t, bm), lambda i, c=clamp(nb): (0, c(i))) for bm, nb in zip(bms, nbs)] + [_resident((t, n))],
        out_specs=[pl.BlockSpec((bm, n), lambda i, c=clamp(nb): (c(i), 0)) for bm, nb in zip(bms, nbs)],
        out_shape=[jax.ShapeDtypeStruct((m, n), WIRE) for m in ms],
        scratch_shapes=[pltpu.VMEM((t, n), MXU)] if cast else [],
        compiler_params=_params(("arbitrary",)),
    )(*a_list, b)


def _merge_dw_in(d_gdn, d_z, d_fox, d_fg, d_small, shards=False):
    pieces = [d_gdn, d_z, d_small[:2 * HEADS], d_fox, d_fg, d_small[2 * HEADS:3 * HEADS]]
    if not shards:
        return jnp.concatenate(pieces, axis=0)
    n = sum(p.shape[0] for p in pieces) // N_DEV
    out = []
    for dev in range(N_DEV):
        parts, first = [], 0
        for p in pieces:
            lo, hi = max(dev * n, first), min((dev + 1) * n, first + p.shape[0])
            if lo < hi:
                parts.append(p[lo - first:hi - first])
            first += p.shape[0]
        out.append(jnp.concatenate(parts, axis=0))
    return jnp.stack(out)


def _lanes(*pieces):
    v = jnp.concatenate([p.reshape(-1).astype(F32) for p in pieces])
    return jnp.pad(v, (0, LANE - v.shape[0])).reshape(1, LANE)


def _vector_params(p):
    d = p["norm1_w"].size
    gparams = jnp.concatenate([_lanes(jnp.zeros(HEADS), p["gdn_dt_bias"], p["fox_f_bias"]),
                               _lanes(jnp.zeros(HEADS), p["gdn_A_log"]), jnp.zeros((6, LANE), F32)])
    fox_nw = jnp.stack([jnp.tile(p["fox_q_norm_w"].reshape(-1), 2), jnp.tile(p["fox_k_norm_w"].reshape(-1), 2),
                        jnp.ones((LANE,), F32)])
    return dict(n1w=p["norm1_w"].reshape(1, d), n2w=p["norm2_w"].reshape(1, d), fw=p["final_norm_w"].reshape(1, d),
                onw=p["gdn_out_norm_w"].reshape(1, DH), gparams=gparams, fox_nw=fox_nw)


def _mixer_forward(x, vp, w_t, ws_t, conv_w, carry=None, carry_first=None):
    h1, pm, ps, *first = _inproj(x, vp["n1w"], w_t, ws_t, carry_first)
    gates, run, tot, run_t = _gates(ps, vp["gparams"])
    fqkv = _fox_prep(pm, vp["fox_nw"])
    o_fox, lse, *carried = _fox_fwd(fqkv, run, run_t, carry)
    if carry:
        pm, o_fox = lax.optimization_barrier((pm, o_fox))
    mixf = _fox_post(o_fox, pm)
    gqkv = _gdn_prep(pm, conv_w)
    o_gdn, states, inv = _gdn_fwd(gqkv, gates, run, tot, run_t)
    mixg = _gdn_post(o_gdn, pm, vp["onw"])
    return dict(h1=h1, pm=pm, ps=ps, gates=gates, run=run, tot=tot, run_t=run_t, gqkv=gqkv, o_gdn=o_gdn, states=states,
                inv=inv, mixg=mixg, fqkv=fqkv, o_fox=o_fox, lse=lse, mixf=mixf, carried=carried[0] if carried else None,
                carried_first=first[0] if first else None)


def _mixer_backward(x, vp, w_t, ws_t, conv_w, f, dx2, dmixg, dmixf, carry=None, scatter_own=False):
    pm = f["pm"]
    do_fox, dfg = _fox_post_bwd(f["o_fox"], pm, dmixf)
    dfqkv, dcol_f, drow_f, *carried = _fox_bwd(f["fqkv"], f["run"], f["run_t"], f["o_fox"], f["lse"], do_fox, carry)
    dfox, dfnw = _fox_prep_bwd(pm, vp["fox_nw"], dfqkv)
    do_gdn, dz, donw = _gdn_post_bwd(f["o_gdn"], pm, vp["onw"], dmixg)
    dgqkv, dcol_g, dtot_g, drow_g = _gdn_bwd(f["gqkv"], f["gates"], f["run"], f["tot"], f["run_t"], f["inv"], f["states"], do_gdn)
    dgdn, dconv = _gdn_prep_bwd(pm, conv_w, dgqkv)
    dps, gsum = _gates_bwd(f["ps"], vp["gparams"], dcol_g, dtot_g, dcol_f, drow_g, drow_f)
    dw_pieces = _wgrads([dgdn, dz, dfox, dfg, dps], f["h1"], "dw_in")
    dw_in = _merge_dw_in(*dw_pieces)
    own = ("scatter", [_merge_dw_in(*dw_pieces, shards=True), _cut(dconv, 1)]) if scatter_own else None
    grad_x, dn1w, *sent = _inproj_bwd(x, vp["n1w"], dx2, dgdn, dz, dfox, dfg, dps, w_t, ws_t, own)
    small = dict(dn1w=dn1w, gsum=gsum, donw=donw, dfnw=dfnw)
    return (grad_x, dw_in, dconv, small, *carried, *sent)


VECTORS = ("norm1_w", "norm2_w", "final_norm_w", "gdn_A_log", "gdn_dt_bias", "gdn_out_norm_w", "fox_f_bias",
           "fox_q_norm_w", "fox_k_norm_w")
VEC_ROWS = 16
LOSS_ROW = len(VECTORS)


def _pack_vectors(dn1w, dn2w, dfw, gsum, donw, dfnw, loss):
    d = dn1w.shape[1]

    def body(n1_ref, n2_ref, fw_ref, gs_ref, on_ref, fn_ref, loss_ref, o_ref):
        o_ref[...] = jnp.zeros_like(o_ref)
        o_ref[0:1, :] = n1_ref[...]
        o_ref[1:2, :] = n2_ref[...]
        o_ref[2:3, :] = fw_ref[...]
        o_ref[3:4, 0:HEADS] = gs_ref[0:1, 0:HEADS]
        o_ref[4:5, 0:HEADS] = gs_ref[1:2, 0:HEADS]
        o_ref[5:6, 0:DH] = on_ref[...]
        o_ref[6:7, 0:HEADS] = gs_ref[2:3, 0:HEADS]
        for kind in range(2):
            v = fn_ref[kind, 0]
            for j in range(1, fn_ref.shape[1]):
                v = v + fn_ref[kind, j]
            o_ref[7 + kind:8 + kind, 0:DH] = v[:, :DH] + v[:, DH:]
        o_ref[LOSS_ROW:LOSS_ROW + 1, 0:1] = loss_ref[...]

    return pl.pallas_call(body, name="pack_vectors", out_shape=jax.ShapeDtypeStruct((VEC_ROWS, d), F32),
                          compiler_params=_params())(dn1w, dn2w, dfw, gsum, donw, dfnw, loss)


def _late_grads(f, h2, act, dgate, dup, dx3, dx2):
    dw_gate, dw_up = _wgrads([dgate, dup], h2, "dw_gate_up")
    (dw_down,) = _wgrads([act], dx3, "dw_down")
    return {"w_out": jnp.concatenate(_wgrads([f["mixg"], f["mixf"]], dx2, "dw_out"), axis=0),
            "w_ffn_gate": dw_gate, "w_ffn_up": dw_up, "w_ffn_down": dw_down}


def _local_step(x, tgt, p, w_in_t, conv_w, wo, wg_t, wu_t, wd):
    vp = _vector_params(p)
    ws_t = _small_rows(w_in_t)
    f = _mixer_forward(x, vp, w_in_t, ws_t, conv_w)
    (h2, act, dgate, dup, dx3, dx2, dmixg, dmixf, dn2w, dfw, loss) = _tail(
        x, f["mixg"], f["mixf"], tgt, wo, vp["n2w"], wg_t, wu_t, wd, vp["fw"])
    grad_x, dw_in, dconv, small = _mixer_backward(x, vp, w_in_t, ws_t, conv_w, f, dx2, dmixg, dmixf)
    grads = {"w_in": dw_in, "gdn_conv_w": dconv, **_late_grads(f, h2, act, dgate, dup, dx3, dx2)}
    vec = _pack_vectors(small["dn1w"], dn2w, dfw, small["gsum"], small["donw"], small["dfnw"], loss)
    return grad_x[0], grads, vec


def _my_place():
    return lax.axis_index("x"), lax.axis_index("y"), lax.axis_index("c")


def _peers():
    x, y, c = _my_place()
    peers = []
    for k in range(1, N_DEV):
        px = 1 - x if k & 4 else x
        py = 1 - y if k & 2 else y
        pc = 1 - c if k & 1 else c
        peers.append(((px, py, pc), 4 * px + 2 * py + pc))
    return 4 * x + 2 * y + c, peers


def _spread_copies(kind, srcs, lands, send_sems, recv_sems):
    me, peers = _peers()
    kinds = [kind] * len(srcs) if isinstance(kind, str) else kind
    remote, local = [], []
    for i, (kd, src, land) in enumerate(zip(kinds, srcs, lands)):
        for k, (dev, idx) in enumerate(peers):
            remote.append(pltpu.make_async_remote_copy(
                src_ref=src if kd == "gather" else src.at[idx], dst_ref=land.at[me],
                send_sem=send_sems.at[i * (N_DEV - 1) + k], recv_sem=recv_sems.at[i * (N_DEV - 1) + k],
                device_id=dev, device_id_type=MESH))
        local.append((src if kd == "gather" else src.at[me], land.at[me]))
    return remote, local


def _gather_two_level(arrays, name):
    n = len(arrays)

    def body(*refs):
        srcs, lands = refs[:n], refs[n:2 * n]
        send_sems, recv_sems, local_sems = refs[2 * n:]
        x, y, c = _my_place()
        me, sibling = (x, y, c), (x, y, 1 - c)
        chips = [(1 - x, y), (x, 1 - y), (1 - x, 1 - y)]

        def index(p):
            return 4 * p[0] + 2 * p[1] + p[2]

        def copy(i, k, block, to, src=None):
            blk = lands[i].at[index(block)]
            return pltpu.make_async_remote_copy(
                src_ref=blk if src is None else src, dst_ref=blk, send_sem=send_sems.at[7 * i + k],
                recv_sem=recv_sems.at[7 * i + k], device_id=to, device_id_type=MESH)

        mine = [pltpu.make_async_copy(srcs[i], lands[i].at[index(me)], local_sems.at[i]) for i in range(n)]
        for cp in mine:
            cp.start()
        first = []
        for i in range(n):
            first.append(copy(i, 0, me, sibling, src=srcs[i]))
            first += [copy(i, 1 + j, me, (*chip, c), src=srcs[i]) for j, chip in enumerate(chips)]
        for cp in first:
            cp.start()
        passed = []
        for i in range(n):
            for j, chip in enumerate(chips):
                copy(i, 1 + j, (*chip, c), me).wait_recv()
                passed.append(copy(i, 4 + j, (*chip, c), sibling))
                passed[-1].start()
        for i in range(n):
            copy(i, 0, sibling, me).wait_recv()
            for j, chip in enumerate(chips):
                copy(i, 4 + j, (*chip, 1 - c), me).wait_recv()
        for cp in first + passed:
            cp.wait_send()
        for cp in mine:
            cp.wait()

    return pl.pallas_call(
        body, name=name,
        out_shape=[jax.ShapeDtypeStruct((N_DEV,) + a.shape, a.dtype) for a in arrays],
        in_specs=[pl.BlockSpec(memory_space=pl.ANY)] * n, out_specs=[pl.BlockSpec(memory_space=pl.ANY)] * n,
        scratch_shapes=[pltpu.SemaphoreType.DMA((7 * n,)), pltpu.SemaphoreType.DMA((7 * n,)),
                        pltpu.SemaphoreType.DMA((n,))],
    )(*arrays)


def _land_shape(kind, a):
    return (N_DEV,) + a.shape if kind == "gather" else a.shape


HBM = pl.BlockSpec(memory_space=pltpu.HBM)
SEM = pl.BlockSpec(memory_space=pltpu.SEMAPHORE)


def _hbm(a):
    return pltpu.with_memory_space_constraint(a, pltpu.HBM)


def _carry_operands(kind, arrays):
    n = len(arrays)
    kinds = [kind] * n if isinstance(kind, str) else kind
    lands = [lax.empty(_land_shape(kd, a), a.dtype) for kd, a in zip(kinds, arrays)]
    sems = [pltpu.SemaphoreType.DMA((n * (N_DEV - 1),))] * 2
    return ([_hbm(a) for a in list(arrays) + lands], [HBM] * (2 * n),
            sems + [pltpu.HBM(a.shape, a.dtype) for a in list(arrays) + lands], [SEM] * 2 + [HBM] * (2 * n),
            pltpu.SemaphoreType.DMA((n,)))


def _carry_step(kind, in_refs, send_sems, recv_sems, local_sems, step, n_steps):
    n = len(in_refs) // 2
    remote, local = _spread_copies(kind, in_refs[:n], in_refs[n:], send_sems, recv_sems)
    copies = [pltpu.make_async_copy(s, d, local_sems.at[i]) for i, (s, d) in enumerate(local)]
    per_step = -(-len(remote) // n_steps)
    for s in range(-(-len(remote) // per_step)):
        @pl.when(step == s)
        def _():
            for cp in remote[s * per_step:(s + 1) * per_step]:
                cp.start()
            if s == 0:
                for cp in copies:
                    cp.start()

    @pl.when(step == n_steps - 1)
    def _():
        for cp in copies:
            cp.wait()


def _carry_state(extra_out):
    n = (len(extra_out) - 2) // 2
    return list(extra_out[2:2 + n]), list(extra_out[2 + n:]), extra_out[0], extra_out[1]


def _spread_start(arrays, kind, name):
    n = len(arrays)

    def body(*refs):
        srcs, lands = refs[:n], refs[n:2 * n]
        send_sems, recv_sems = refs[2 * n], refs[2 * n + 1]
        token = refs[4 * n + 2]
        local_sems = refs[4 * n + 3]
        remote, local = _spread_copies(kind, srcs, lands, send_sems, recv_sems)
        for cp in remote:
            cp.start()
        copies = [pltpu.make_async_copy(s, d, local_sems.at[i]) for i, (s, d) in enumerate(local)]
        for cp in copies:
            cp.start()
        for cp in copies:
            cp.wait()
        token[...] = jnp.zeros_like(token)

    sems = (pltpu.SemaphoreType.DMA((n * (N_DEV - 1),)),) * 2
    kinds = [kind] * n if isinstance(kind, str) else kind
    lands = [lax.empty(_land_shape(kd, a), a.dtype) for kd, a in zip(kinds, arrays)]
    out = pl.pallas_call(
        body, name=name,
        out_shape=sems + tuple(pltpu.HBM(a.shape, a.dtype) for a in list(arrays) + lands)
        + (jax.ShapeDtypeStruct((8, LANE), F32),),
        in_specs=[HBM] * (2 * n), out_specs=tuple([SEM] * 2 + [HBM] * (2 * n) + [pl.BlockSpec(memory_space=pltpu.VMEM)]),
        input_output_aliases={j: 2 + j for j in range(2 * n)},
        scratch_shapes=[pltpu.SemaphoreType.DMA((n,))],
        compiler_params=pltpu.CompilerParams(has_side_effects=pltpu.SideEffectType.DATAFLOW_SIDE_EFFECTING),
    )(*[_hbm(a) for a in arrays], *[_hbm(a) for a in lands])
    return (list(out[2:2 + n]), list(out[2 + n:2 + 2 * n]), out[0], out[1]), out[-1]


def _spread_wait(state, kind, after, name):
    srcs, lands, send_sems, recv_sems = state
    n = len(srcs)
    after = list(after) if isinstance(after, (list, tuple)) else [after]

    def body(*refs):
        remote, _ = _spread_copies(kind, refs[:n], refs[n:2 * n], refs[2 * n], refs[2 * n + 1])
        for cp in remote:
            cp.wait_send()
        for cp in remote:
            cp.wait_recv()

    out = pl.pallas_call(
        body, name=name,
        out_shape=tuple(pltpu.HBM(a.shape, a.dtype) for a in srcs + lands),
        in_specs=[HBM] * (2 * n) + [SEM, SEM] + [pl.BlockSpec(memory_space=pl.ANY)] * len(after),
        out_specs=tuple([HBM] * (2 * n)),
        input_output_aliases={j: j for j in range(2 * n)},
        compiler_params=pltpu.CompilerParams(has_side_effects=pltpu.SideEffectType.DATAFLOW_SIDE_EFFECTING),
    )(*srcs, *lands, send_sems, recv_sems, *after)
    return list(out[n:])


ADAM_ROWS = 128
ADAM_COLS = 256


def _adam_math(g, w, m, v):
    nm = ADAM_B1 * m + (1.0 - ADAM_B1) * g
    nv = ADAM_B2 * v + (1.0 - ADAM_B2) * (g * g)
    m_hat = nm / (1.0 - ADAM_B1 ** ADAM_STEP)
    v_hat = nv / (1.0 - ADAM_B2 ** ADAM_STEP)
    return -ADAM_LR * (m_hat / (jnp.sqrt(v_hat) + ADAM_EPS) + ADAM_WD * w), nm, nv


def _sum_parts(p_ref):
    g = p_ref[0].astype(F32)
    for s in range(1, N_DEV):
        g = g + p_ref[s].astype(F32)
    return g


def _adam_matrix(parts, w, m, v, name):
    _, r, c = w.shape
    rb = ADAM_ROWS if r % ADAM_ROWS == 0 else r
    cb = ADAM_COLS if (rb == r and c % ADAM_COLS == 0) else c

    def body(p_ref, w_ref, m_ref, v_ref, g_ref, d_ref, nm_ref, nv_ref):
        g = _sum_parts(p_ref)
        g_ref[0] = g
        d_ref[0], nm_ref[0], nv_ref[0] = _adam_math(g, w_ref[0], m_ref[0], v_ref[0])

    blk = pl.BlockSpec((1, rb, cb), lambda i, j: (0, i, j))
    return pl.pallas_call(
        body, name=name, grid=(r // rb, c // cb),
        in_specs=[pl.BlockSpec((N_DEV, rb, cb), lambda i, j: (0, i, j)), blk, blk, blk],
        out_specs=[blk] * 4, out_shape=[jax.ShapeDtypeStruct(w.shape, F32)] * 4,
        compiler_params=_params(("arbitrary", "arbitrary")),
    )(parts, w, m, v)


def _copy_rows(a, name):
    _, r, c = a.shape
    rb = min(TB, r)

    def body(a_ref, o_ref):
        o_ref[...] = a_ref[...]

    blk = pl.BlockSpec((1, rb, c), lambda i: (0, i, 0))
    return pl.pallas_call(body, name=name, grid=(r // rb,), in_specs=[blk], out_specs=blk,
                          out_shape=jax.ShapeDtypeStruct(a.shape, a.dtype), compiler_params=_params(("arbitrary",)))(a)


def _adam_vectors(parts, ws, ms, vs):
    nv = len(ws)

    def body(*refs):
        p_ref = refs[0]
        w_refs, m_refs, v_refs = refs[1:1 + nv], refs[1 + nv:1 + 2 * nv], refs[1 + 2 * nv:1 + 3 * nv]
        outs = refs[1 + 3 * nv:]
        g_all = _sum_parts(p_ref)
        for i in range(nv):
            n = w_refs[i].shape[1]
            g = g_all[i:i + 1, 0:n]
            d, nm, nvv = _adam_math(g, w_refs[i][...], m_refs[i][...], v_refs[i][...])
            outs[i][...] = g
            outs[nv + i][...] = d
            outs[2 * nv + i][...] = nm
            outs[3 * nv + i][...] = nvv
        outs[4 * nv][...] = g_all[LOSS_ROW:LOSS_ROW + 1, 0:1]

    shapes = [jax.ShapeDtypeStruct(a.shape, F32) for a in ws]
    out = pl.pallas_call(body, name="adam_vectors", out_shape=shapes * 4 + [jax.ShapeDtypeStruct((1, 1), F32)],
                         compiler_params=_params())(parts, *ws, *ms, *vs)
    return out[:nv], out[nv:2 * nv], out[2 * nv:3 * nv], out[3 * nv:4 * nv], out[4 * nv]


MATRICES = (("w_in", 1), ("gdn_conv_w", 1), ("w_out", 0), ("w_ffn_gate", 1), ("w_ffn_up", 1), ("w_ffn_down", 0))
TRANSPOSED = ("w_in", "w_ffn_gate", "w_ffn_up")
WEIGHTS = ("norm1_w", "w_in", "gdn_conv_w", "gdn_A_log", "gdn_dt_bias", "gdn_out_norm_w", "fox_f_bias", "fox_q_norm_w",
           "fox_k_norm_w", "w_out", "norm2_w", "w_ffn_gate", "w_ffn_up", "w_ffn_down", "final_norm_w")


def _join(blocks, axis):
    _, r, c = blocks.shape
    if axis == 0:
        return blocks.reshape(N_DEV * r, c)
    return blocks.transpose(1, 0, 2).reshape(r, N_DEV * c)


def _cut(full, axis):
    r, c = full.shape
    if axis == 0:
        return full.reshape(N_DEV, r // N_DEV, c)
    return full.reshape(r, N_DEV, c // N_DEV).transpose(1, 0, 2)


def kernel(x, norm1_w, w_in, gdn_conv_w, gdn_A_log, gdn_dt_bias, gdn_out_norm_w, fox_f_bias, fox_q_norm_w, fox_k_norm_w, w_out, norm2_w, w_ffn_gate, w_ffn_up, w_ffn_down, final_norm_w, loss_target, m_norm1_w, m_w_in, m_gdn_conv_w, m_gdn_A_log, m_gdn_dt_bias, m_gdn_out_norm_w, m_fox_f_bias, m_fox_q_norm_w, m_fox_k_norm_w, m_w_out, m_norm2_w, m_w_ffn_gate, m_w_ffn_up, m_w_ffn_down, m_final_norm_w, v_norm1_w, v_w_in, v_gdn_conv_w, v_gdn_A_log, v_gdn_dt_bias, v_gdn_out_norm_w, v_fox_f_bias, v_fox_q_norm_w, v_fox_k_norm_w, v_w_out, v_norm2_w, v_w_ffn_gate, v_w_ffn_up, v_w_ffn_down, v_final_norm_w):
    w = dict(norm1_w=norm1_w, w_in=w_in, gdn_conv_w=gdn_conv_w, gdn_A_log=gdn_A_log, gdn_dt_bias=gdn_dt_bias,
             gdn_out_norm_w=gdn_out_norm_w, fox_f_bias=fox_f_bias, fox_q_norm_w=fox_q_norm_w, fox_k_norm_w=fox_k_norm_w,
             w_out=w_out, norm2_w=norm2_w, w_ffn_gate=w_ffn_gate, w_ffn_up=w_ffn_up, w_ffn_down=w_ffn_down,
             final_norm_w=final_norm_w)
    m = dict(norm1_w=m_norm1_w, w_in=m_w_in, gdn_conv_w=m_gdn_conv_w, gdn_A_log=m_gdn_A_log, gdn_dt_bias=m_gdn_dt_bias,
             gdn_out_norm_w=m_gdn_out_norm_w, fox_f_bias=m_fox_f_bias, fox_q_norm_w=m_fox_q_norm_w,
             fox_k_norm_w=m_fox_k_norm_w, w_out=m_w_out, norm2_w=m_norm2_w, w_ffn_gate=m_w_ffn_gate,
             w_ffn_up=m_w_ffn_up, w_ffn_down=m_w_ffn_down, final_norm_w=m_final_norm_w)
    v = dict(norm1_w=v_norm1_w, w_in=v_w_in, gdn_conv_w=v_gdn_conv_w, gdn_A_log=v_gdn_A_log, gdn_dt_bias=v_gdn_dt_bias,
             gdn_out_norm_w=v_gdn_out_norm_w, fox_f_bias=v_fox_f_bias, fox_q_norm_w=v_fox_q_norm_w,
             fox_k_norm_w=v_fox_k_norm_w, w_out=v_w_out, norm2_w=v_norm2_w, w_ffn_gate=v_w_ffn_gate,
             w_ffn_up=v_w_ffn_up, w_ffn_down=v_w_ffn_down, final_norm_w=v_final_norm_w)
    late = ("w_out", "w_ffn_gate", "w_ffn_up", "w_ffn_down")
    xs, tgt = x[0], loss_target[0]
    vp = _vector_params({n: w[n] for n in VECTORS})

    def rows_of(d, n):
        return d[n].transpose(0, 2, 1) if n in TRANSPOSED else d[n]

    wr, mr, vr = ({n: rows_of(d, n) for n, _ in MATRICES} for d in (w, m, v))

    w_in_blocks, conv_blocks = _gather_two_level([wr["w_in"][0].astype(WIRE), w["gdn_conv_w"][0]], "gather_in")
    w_t = _join(w_in_blocks, 0)
    conv_w = _join(conv_blocks, 1)
    early = ("w_ffn_gate",)
    rest = tuple(n for n in late if n not in early)
    f = _mixer_forward(xs, vp, w_t, _small_rows(w_t), conv_w, ("gather", [wr[n][0].astype(WIRE) for n in rest]),
                       ("gather", [wr[n][0].astype(WIRE) for n in early]))
    full = {n: _join(b, 0) for n, b in zip(early, _spread_wait(f["carried_first"], "gather", f["mixg"], "gather_early_wait"))}
    full.update({n: _join(b, 0)
                 for n, b in zip(rest, _spread_wait(f["carried"], "gather", full[early[0]], "gather_late_wait"))})

    (h2, act, dgate, dup, dx3, dx2, dmixg, dmixf, dn2w, dfw, loss) = _tail(
        xs, f["mixg"], f["mixf"], tgt, full["w_out"], vp["n2w"], full["w_ffn_gate"], full["w_ffn_up"],
        full["w_ffn_down"], vp["fw"])
    dlate = _late_grads(f, h2, act, dgate, dup, dx3, dx2)

    grad_x, dw_in, dconv, small, state_grads, state_own = _mixer_backward(
        xs, vp, w_t, _small_rows(w_t), conv_w, f, dx2, dmixg, dmixf, ("scatter", [_cut(dlate[n], 0) for n in late]),
        scatter_own=True)
    vec = _pack_vectors(small["dn1w"], dn2w, dfw, small["gsum"], small["donw"], small["dfnw"], loss)

    state_vec, token = _spread_start([vec], "gather", "vectors_start")
    parts = dict(zip(late, _spread_wait(state_grads, "scatter", token, "grads_late_wait")))
    results = [{}, {}, {}, {}]

    updated = {}

    def update(n):
        out = _adam_matrix(parts[n], wr[n], mr[n], vr[n], "adam_" + n)
        updated[n] = out[0]
        for d, a in zip(results, out):
            d[n] = a.transpose(0, 2, 1) if n in TRANSPOSED else a

    for n in late:
        update(n)
    grad_x = _copy_rows(grad_x, "grad_x_result")
    ready = [updated[n] for n in late] + [d["w_in"] for d in (wr, mr, vr)] + [grad_x]
    (parts_vec,) = _spread_wait(state_vec, "gather", ready, "vectors_wait")
    row = lambda a: a.reshape(1, -1)
    *vec_out, total_loss = _adam_vectors(parts_vec, [row(w[n]) for n in VECTORS], [row(m[n]) for n in VECTORS],
                                         [row(v[n]) for n in VECTORS])
    parts["w_in"], parts["gdn_conv_w"] = _spread_wait(state_own, "scatter", parts_vec, "own_wait")
    update("w_in")
    update("gdn_conv_w")
    for d, arrs in zip(results, vec_out):
        for n, a in zip(VECTORS, arrs):
            d[n] = a.reshape(w[n].shape)
    return (total_loss[0, 0], grad_x, *[d[n] for d in results for n in WEIGHTS])
```

```python
import functools

import jax
import jax.numpy as jnp
from jax import lax
from jax.experimental import pallas as pl
from jax.experimental.pallas import tpu as pltpu

F32 = jnp.float32
MXU = jnp.bfloat16
WIRE = jnp.bfloat16
HI = lax.Precision.HIGHEST
EPS = 1e-6

N_DEV = 8
HEADS = 8
DH = 64
WIDTH = HEADS * DH
CHUNK = 64
LANE = 128
ROW_ALIGN = 16
TB = 256
QB = 256
VMEM_LIMIT = 60 * 1024 * 1024

ADAM_LR = 0.001
ADAM_B1 = 0.9
ADAM_B2 = 0.999
ADAM_EPS = 1e-08
ADAM_WD = 0.01
ADAM_STEP = 10

MESH = pl.DeviceIdType.MESH


def _params(sem=None):
    return pltpu.CompilerParams(dimension_semantics=sem, vmem_limit_bytes=VMEM_LIMIT)


def _resident(shape):
    n = len(shape)
    return pl.BlockSpec(shape, lambda *_: (0,) * n, pipeline_mode=pl.Buffered(1))


def _dot(a, b):
    return jnp.dot(a.astype(MXU), b.astype(MXU), preferred_element_type=F32)


def _dot_nt(a, b):
    return lax.dot_general(a.astype(MXU), b.astype(MXU), (((1,), (1,)), ((), ())), preferred_element_type=F32)


def _dot_tn(a, b):
    return lax.dot_general(a.astype(MXU), b.astype(MXU), (((0,), (0,)), ((), ())), preferred_element_type=F32)


def _hdot(a, b):
    return jnp.dot(a, b, precision=HI, preferred_element_type=F32)


def _hdot_nt(a, b):
    return lax.dot_general(a, b, (((1,), (1,)), ((), ())), precision=HI, preferred_element_type=F32)


def _hdot_tn(a, b):
    return lax.dot_general(a, b, (((0,), (0,)), ((), ())), precision=HI, preferred_element_type=F32)


def _sigmoid(x):
    return 0.5 * jnp.tanh(0.5 * x) + 0.5


def _softplus(x):
    return jnp.maximum(x, 0.0) + jnp.log(1.0 + jnp.exp(-jnp.abs(x)))


def _head_sum_matrix():
    ri = lax.broadcasted_iota(jnp.int32, (LANE, LANE), 0) // DH
    ci = lax.broadcasted_iota(jnp.int32, (LANE, LANE), 1) // DH
    return (ri == ci).astype(F32)


def _group_sum(a, ones_matrix):
    hi = a.astype(jnp.bfloat16)
    lo = (a - hi.astype(F32)).astype(jnp.bfloat16)
    m = ones_matrix.astype(jnp.bfloat16)
    return jnp.dot(hi, m, preferred_element_type=F32) + jnp.dot(lo, m, preferred_element_type=F32)


def _shift_down(x, s):
    return pltpu.roll(x, s, 0)


def _shift_up(x, s):
    return pltpu.roll(x, x.shape[0] - s, 0)


ROWS_A = 4 * WIDTH
ROWS_B = ROWS_A + 2 * HEADS
ROWS_C = ROWS_B + 4 * WIDTH


def _small_rows(w_t):
    return jnp.concatenate([w_t[ROWS_A:ROWS_B], w_t[ROWS_C:], jnp.zeros((LANE - 3 * HEADS, w_t.shape[1]), w_t.dtype)])


def _inproj(x, n1w, w_t, ws_t, carry=None):
    t, d = x.shape
    tb = min(TB, t)
    c_in, c_in_specs, c_out_shape, c_out_specs, c_sem = _carry_operands(*carry) if carry else ([], [], [], [], None)
    n_c = len(c_in)

    def body(*refs):
        x_ref, nw_ref, wt_ref, ws_ref = refs[:4]
        h_ref, pm_ref, ps_ref = refs[4 + n_c:7 + n_c]
        if carry:
            _carry_step(carry[0], refs[4:4 + n_c], refs[7 + n_c], refs[8 + n_c], refs[-1], pl.program_id(0), t // tb)
        xv = x_ref[...]
        r = lax.rsqrt(jnp.mean(xv * xv, axis=-1, keepdims=True) + EPS)
        h = (xv * r * nw_ref[...]).astype(MXU)
        h_ref[...] = h
        pm_ref[:, 0:ROWS_A] = _dot_nt(h, wt_ref[0:ROWS_A, :])
        pm_ref[:, ROWS_A:2 * ROWS_A] = _dot_nt(h, wt_ref[ROWS_B:ROWS_C, :])
        ps_ref[...] = _dot_nt(h, ws_ref[...])

    out = pl.pallas_call(
        body, name="inproj", grid=(t // tb,),
        in_specs=[pl.BlockSpec((tb, d), lambda i: (i, 0)), _resident((1, d)), _resident(w_t.shape), _resident((LANE, d))]
        + c_in_specs,
        out_specs=[pl.BlockSpec((tb, d), lambda i: (i, 0)), pl.BlockSpec((tb, 2 * ROWS_A), lambda i: (i, 0)),
                   pl.BlockSpec((tb, LANE), lambda i: (i, 0))] + c_out_specs,
        out_shape=[jax.ShapeDtypeStruct((t, d), MXU), jax.ShapeDtypeStruct((t, 2 * ROWS_A), F32),
                   jax.ShapeDtypeStruct((t, LANE), F32)] + c_out_shape,
        input_output_aliases={4 + j: 5 + j for j in range(n_c)},
        scratch_shapes=[c_sem] if carry else [],
        compiler_params=_params(("arbitrary",)),
    )(x, n1w, w_t, ws_t, *c_in)
    return (out[0], out[1], out[2], _carry_state(out[3:])) if carry else tuple(out)


def _inproj_bwd(x, n1w, dx2, dgdn, dz, dfox, dfg, dps, w_t, ws_t, carry=None):
    t, d = x.shape
    tb = min(TB, t)
    w3 = 3 * WIDTH
    c_in, c_in_specs, c_out_shape, c_out_specs, c_sem = _carry_operands(*carry) if carry else ([], [], [], [], None)
    n_c = len(c_in)

    def body(*refs):
        x_ref, nw_ref, dx2_ref, dgdn_ref, dz_ref, dfox_ref, dfg_ref, dps_ref, wm_ref, ws_ref = refs[:10]
        gx_ref, dnw_ref = refs[10 + n_c:12 + n_c]
        if carry:
            _carry_step(carry[0], refs[10:10 + n_c], refs[12 + n_c], refs[13 + n_c], refs[-1], pl.program_id(0), t // tb)
        dh = _dot(dgdn_ref[...], wm_ref[0:w3, :])
        dh += _dot(dz_ref[...], wm_ref[w3:ROWS_A, :])
        dh += _dot(dfox_ref[...], wm_ref[ROWS_B:ROWS_B + w3, :])
        dh += _dot(dfg_ref[...], wm_ref[ROWS_B + w3:ROWS_C, :])
        dh += _dot(dps_ref[...], ws_ref[...])
        xv = x_ref[...]
        r = lax.rsqrt(jnp.mean(xv * xv, axis=-1, keepdims=True) + EPS)
        xn = xv * r

        @pl.when(pl.program_id(0) == 0)
        def _():
            dnw_ref[...] = jnp.zeros_like(dnw_ref)

        dnw_ref[...] += jnp.sum(dh * xn, axis=0, keepdims=True)
        g = dh * nw_ref[...]
        gx_ref[0] = dx2_ref[...] + r * (g - xn * jnp.mean(g * xn, axis=-1, keepdims=True))

    def tok(n):
        return pl.BlockSpec((tb, n), lambda i: (i, 0))

    out = pl.pallas_call(
        body, name="inproj_bwd", grid=(t // tb,),
        in_specs=[tok(d), _resident((1, d)), tok(d), tok(w3), tok(WIDTH), tok(w3), tok(WIDTH), tok(LANE),
                  _resident(w_t.shape), _resident(ws_t.shape)] + c_in_specs,
        out_specs=[pl.BlockSpec((1, tb, d), lambda i: (0, i, 0)), pl.BlockSpec((1, d), lambda i: (0, 0))] + c_out_specs,
        out_shape=[jax.ShapeDtypeStruct((1, t, d), F32), jax.ShapeDtypeStruct((1, d), F32)] + c_out_shape,
        input_output_aliases={10 + j: 4 + j for j in range(n_c)},
        scratch_shapes=[c_sem] if carry else [],
        compiler_params=_params(("arbitrary",)),
    )(x, n1w, dx2, dgdn, dz, dfox, dfg, dps, w_t, ws_t, *c_in)
    return (out[0], out[1], _carry_state(out[2:])) if carry else tuple(out)


def _gate_lanes(shape):
    lane = lax.broadcasted_iota(jnp.int32, shape, 1)
    return lane < HEADS, (lane >= HEADS) & (lane < 2 * HEADS), (lane >= 2 * HEADS) & (lane < 3 * HEADS)


def _block_masks():
    ri = lax.broadcasted_iota(jnp.int32, (LANE, LANE), 0)
    ci = lax.broadcasted_iota(jnp.int32, (LANE, LANE), 1)
    same = (ri // CHUNK) == (ci // CHUNK)
    return ((ri >= ci).astype(F32), (ri <= ci).astype(F32), (same & (ri >= ci)).astype(F32),
            (same & (ri <= ci)).astype(F32), same.astype(F32))


def _gates(ps, gparams):
    t = ps.shape[0]
    nb = t // LANE

    def body(ps_ref, gp_ref, out_ref, run_ref, tot_ref, runt_ref):
        p = ps_ref[...]
        is_b, is_a, is_f = _gate_lanes(p.shape)
        z = p + gp_ref[0:1, :]
        neg_exp_a = -jnp.exp(gp_ref[1:2, :])
        glog = neg_exp_a * _softplus(z)
        logf = -_softplus(-z)
        out_ref[...] = jnp.where(is_b, _sigmoid(p), jnp.where(is_a, glog, jnp.where(is_f, logf, 0.0)))
        tril, _, tril_c, _, same_c = _block_masks()
        off = jnp.zeros((1, LANE), F32)
        for b in range(nb):
            rows = slice(b * LANE, (b + 1) * LANE)
            blk = out_ref[rows, :]
            ga = jnp.where(is_a[:LANE], blk, 0.0)
            fb = _hdot(tril, jnp.where(is_f[:LANE], blk, 0.0)) + off
            run = fb + _hdot(tril_c, ga)
            run_ref[rows, :] = run
            runt_ref[:, rows] = run.T
            tot_ref[rows, :] = _hdot(same_c, ga)
            off = fb[LANE - 1:LANE, :]

    return pl.pallas_call(
        body, name="gates",
        out_shape=[jax.ShapeDtypeStruct((t, LANE), F32)] * 3 + [jax.ShapeDtypeStruct((LANE, t), F32)],
        compiler_params=_params(),
    )(ps, gparams)


def _gates_bwd(ps, gparams, dcol_g, dtot_g, dcol_f, drow_g, drow_f):
    t = ps.shape[0]
    nb = t // LANE

    def body(ps_ref, gp_ref, dcg_ref, dtg_ref, dcf_ref, drg_ref, drf_ref, dps_ref, sums_ref, dl_ref, d0_ref, tr_ref):
        p = ps_ref[...]
        is_b, is_a, is_f = _gate_lanes(p.shape)
        _, triu, _, triu_c, same_c = _block_masks()
        tr_ref[...] = jnp.zeros_like(tr_ref)
        off = jnp.zeros((1, LANE), F32)
        for b in reversed(range(nb)):
            rows = slice(b * LANE, (b + 1) * LANE)
            tr_ref[HEADS:2 * HEADS, :] = drg_ref[:, rows]
            tr_ref[2 * HEADS:3 * HEADS, :] = drf_ref[:, rows]
            d = dcg_ref[rows, :] + dcf_ref[rows, :] + tr_ref[...].T
            d0_ref[rows, :] = d
            dlf = _hdot(triu, jnp.where(is_f[:LANE], d, 0.0)) + off
            dla = (_hdot(triu_c, jnp.where(is_a[:LANE], d, 0.0))
                   + _hdot(same_c, jnp.where(is_a[:LANE], dtg_ref[rows, :], 0.0)))
            dl_ref[rows, :] = dlf + dla
            off = dlf[0:1, :]
        z = p + gp_ref[0:1, :]
        neg_exp_a = -jnp.exp(gp_ref[1:2, :])
        sb = _sigmoid(p)
        glog = neg_exp_a * _softplus(z)
        dl = dl_ref[...]
        dp = jnp.where(is_b, d0_ref[...] * sb * (1.0 - sb),
                       jnp.where(is_a, dl * neg_exp_a * _sigmoid(z), jnp.where(is_f, dl * _sigmoid(-z), 0.0)))
        dps_ref[...] = dp
        s_a = jnp.sum(jnp.where(is_a, dl * glog, 0.0), axis=0, keepdims=True)
        s_p = jnp.sum(jnp.where(is_b, 0.0, dp), axis=0, keepdims=True)
        row = lax.broadcasted_iota(jnp.int32, (8, LANE), 0)
        from_a = pltpu.roll(jnp.where(row == 0, s_a, jnp.where(row == 1, s_p, 0.0)), LANE - HEADS, 1)
        from_f = pltpu.roll(jnp.where(row == 2, s_p, 0.0), LANE - 2 * HEADS, 1)
        lane = lax.broadcasted_iota(jnp.int32, (8, LANE), 1)
        sums_ref[...] = jnp.where(lane < HEADS, from_a + from_f, 0.0)

    return pl.pallas_call(
        body, name="gates_bwd",
        out_shape=[jax.ShapeDtypeStruct((t, LANE), F32), jax.ShapeDtypeStruct((8, LANE), F32)],
        scratch_shapes=[pltpu.VMEM((t, LANE), F32), pltpu.VMEM((t, LANE), F32), pltpu.VMEM((LANE, LANE), F32)],
        compiler_params=_params(),
    )(ps, gparams, dcol_g, dtot_g, dcol_f, drow_g, drow_f)


def _conv(xv, w):
    acc = w[3:4, :] * xv
    for s in range(1, 4):
        acc += w[3 - s:4 - s, :] * _shift_down(xv, s)
    return acc


PREP_ROWS = 256
HALO = 8
PREP_UNROLL = 8


def _tile_loop(n, tile, init):
    if n % PREP_UNROLL:
        return lax.fori_loop(0, n, tile, init)

    def trip(g, carry):
        for u in range(PREP_UNROLL):
            carry = tile(g * PREP_UNROLL + u, carry)
        return carry

    return lax.fori_loop(0, n // PREP_UNROLL, trip, init)


def _tile_rows(r):
    return pl.ds(pl.multiple_of(r * PREP_ROWS, PREP_ROWS), PREP_ROWS)


def _gdn_prep(pm, conv_w):
    t = pm.shape[0]
    nj = WIDTH // LANE
    win = PREP_ROWS + HALO

    def body(x_ref, w_ref, o_ref, xp_ref):
        kind = pl.program_id(0)
        xp_ref[0:HALO, :] = jnp.zeros((HALO, LANE), F32)
        xp_ref[HALO:, :] = x_ref[...]
        w = w_ref[...]
        hs = _head_sum_matrix()

        def tile(r, _, normed):
            xw = xp_ref[pl.ds(pl.multiple_of(r * PREP_ROWS, PREP_ROWS), win), :]
            acc = _conv(xw, w)[HALO:]
            out = acc * _sigmoid(acc)
            if normed:
                out = out * lax.rsqrt(_group_sum(out * out, hs) + EPS)
            o_ref[0, 0, _tile_rows(r), :] = out[:, :DH]
            o_ref[0, 1, _tile_rows(r), :] = out[:, DH:]
            return 0

        @pl.when(kind < 2)
        def _():
            _tile_loop(t // PREP_ROWS, functools.partial(tile, normed=True), 0)

        @pl.when(kind == 2)
        def _():
            _tile_loop(t // PREP_ROWS, functools.partial(tile, normed=False), 0)

    return pl.pallas_call(
        body, name="gdn_prep", grid=(3, nj),
        in_specs=[pl.BlockSpec((t, LANE), lambda i, j: (0, i * nj + j)),
                  pl.BlockSpec((4, LANE), lambda i, j: (0, i * nj + j))],
        out_specs=pl.BlockSpec((1, 2, t, DH), lambda i, j: (i, j, 0, 0)),
        out_shape=jax.ShapeDtypeStruct((3, HEADS, t, DH), F32),
        scratch_shapes=[pltpu.VMEM((t + HALO, LANE), F32)],
        compiler_params=_params(("arbitrary", "arbitrary")),
    )(pm, conv_w)


def _gdn_prep_bwd(pm, conv_w, dqkv):
    t = pm.shape[0]
    nj = WIDTH // LANE
    win = PREP_ROWS + 2 * HALO

    def body(x_ref, w_ref, d_ref, dx_ref, dw_ref, xp_ref, dp_ref):
        kind = pl.program_id(0)
        zeros = jnp.zeros((HALO, LANE), F32)
        for ref in (xp_ref, dp_ref):
            ref[0:HALO, :] = zeros
            ref[HALO + t:, :] = zeros
        xp_ref[HALO:HALO + t, :] = x_ref[...]
        dp_ref[HALO:HALO + t, 0:DH] = d_ref[0, 0]
        dp_ref[HALO:HALO + t, DH:] = d_ref[0, 1]
        w = w_ref[...]
        hs = _head_sum_matrix()
        rows = lax.broadcasted_iota(jnp.int32, (win, LANE), 0)
        in_tile = (rows >= HALO) & (rows < HALO + PREP_ROWS)

        def tile(r, dw, normed):
            start = pl.multiple_of(r * PREP_ROWS, PREP_ROWS)
            xw = xp_ref[pl.ds(start, win), :]
            dy = dp_ref[pl.ds(start, win), :]
            acc = _conv(xw, w)
            sg = _sigmoid(acc)
            if normed:
                y = acc * sg
                rn = lax.rsqrt(_group_sum(y * y, hs) + EPS)
                yn = y * rn
                dy = rn * (dy - yn * _group_sum(dy * yn, hs))
            dacc = dy * sg * (1.0 + acc * (1.0 - sg))
            dx = w[3:4, :] * dacc
            for s in range(1, 4):
                dx += w[3 - s:4 - s, :] * _shift_up(dacc, s)
            dx_ref[_tile_rows(r), :] = dx[HALO:HALO + PREP_ROWS].astype(dx_ref.dtype)
            dm = jnp.where(in_tile, dacc, 0.0)
            return tuple(dw[i] + jnp.sum(dm * (xw if i == 3 else _shift_down(xw, 3 - i)), axis=0, keepdims=True)
                         for i in range(4))

        def run(normed):
            dw = _tile_loop(t // PREP_ROWS, functools.partial(tile, normed=normed),
                            tuple(jnp.zeros((1, LANE), F32) for _ in range(4)))
            for i in range(4):
                dw_ref[i:i + 1, :] = dw[i]

        pl.when(kind < 2)(functools.partial(run, True))
        pl.when(kind == 2)(functools.partial(run, False))

    return pl.pallas_call(
        body, name="gdn_prep_bwd", grid=(3, nj),
        in_specs=[pl.BlockSpec((t, LANE), lambda i, j: (0, i * nj + j)),
                  pl.BlockSpec((4, LANE), lambda i, j: (0, i * nj + j)),
                  pl.BlockSpec((1, 2, t, DH), lambda i, j: (i, j, 0, 0))],
        out_specs=[pl.BlockSpec((t, LANE), lambda i, j: (0, i * nj + j)),
                   pl.BlockSpec((4, LANE), lambda i, j: (0, i * nj + j))],
        out_shape=[jax.ShapeDtypeStruct((t, 3 * WIDTH), MXU), jax.ShapeDtypeStruct((4, 3 * WIDTH), F32)],
        scratch_shapes=[pltpu.VMEM((t + 2 * HALO, LANE), F32), pltpu.VMEM((t + 2 * HALO, LANE), F32)],
        compiler_params=_params(("arbitrary", "arbitrary")),
    )(pm, conv_w, dqkv)


FOX_COL0 = 4 * WIDTH // LANE


def _fox_prep(pm, nw):
    t = pm.shape[0]
    nj = WIDTH // LANE

    def body(x_ref, w_ref, o_ref):
        kind = pl.program_id(0)
        hs = _head_sum_matrix()
        wk = w_ref[pl.ds(kind, 1), :]

        def tile(r, _, normed):
            out = x_ref[_tile_rows(r), :]
            if normed:
                out = out * lax.rsqrt(_group_sum(out * out, hs) * (1.0 / DH) + EPS) * wk
            o_ref[0, 0, _tile_rows(r), :] = out[:, :DH]
            o_ref[0, 1, _tile_rows(r), :] = out[:, DH:]
            return 0

        @pl.when(kind < 2)
        def _():
            _tile_loop(t // PREP_ROWS, functools.partial(tile, normed=True), 0)

        @pl.when(kind == 2)
        def _():
            _tile_loop(t // PREP_ROWS, functools.partial(tile, normed=False), 0)

    return pl.pallas_call(
        body, name="fox_prep", grid=(3, nj),
        in_specs=[pl.BlockSpec((t, LANE), lambda i, j: (0, FOX_COL0 + i * nj + j)),
                  pl.BlockSpec((3, LANE), lambda i, j: (0, 0))],
        out_specs=pl.BlockSpec((1, 2, t, DH), lambda i, j: (i, j, 0, 0)),
        out_shape=jax.ShapeDtypeStruct((3, HEADS, t, DH), F32),
        compiler_params=_params(("arbitrary", "arbitrary")),
    )(pm, nw)


def _fox_prep_bwd(pm, nw, dqkv):
    t = pm.shape[0]
    nj = WIDTH // LANE

    def body(x_ref, w_ref, d_ref, dx_ref, dw_ref):
        kind = pl.program_id(0)
        hs = _head_sum_matrix()
        wk = w_ref[pl.ds(kind, 1), :]

        def tile(r, dw):
            xv = x_ref[_tile_rows(r), :]
            rn = lax.rsqrt(_group_sum(xv * xv, hs) * (1.0 / DH) + EPS)
            xn = xv * rn
            d = jnp.concatenate([d_ref[0, 0, _tile_rows(r), :], d_ref[0, 1, _tile_rows(r), :]], axis=1)
            g = d * wk
            dx_ref[_tile_rows(r), :] = (rn * (g - xn * _group_sum(g * xn, hs) * (1.0 / DH))).astype(dx_ref.dtype)
            return dw + jnp.sum(d * xn, axis=0, keepdims=True)

        def copy_tile(r, _):
            dx_ref[_tile_rows(r), :] = jnp.concatenate([d_ref[0, 0, _tile_rows(r), :], d_ref[0, 1, _tile_rows(r), :]],
                                                       axis=1).astype(dx_ref.dtype)
            return 0

        @pl.when(kind < 2)
        def _():
            dw_ref[0, 0] = _tile_loop(t // PREP_ROWS, tile, jnp.zeros((1, LANE), F32))

        @pl.when(kind == 2)
        def _():
            _tile_loop(t // PREP_ROWS, copy_tile, 0)
            dw_ref[0, 0] = jnp.zeros((1, LANE), F32)

    return pl.pallas_call(
        body, name="fox_prep_bwd", grid=(3, nj),
        in_specs=[pl.BlockSpec((t, LANE), lambda i, j: (0, FOX_COL0 + i * nj + j)),
                  pl.BlockSpec((3, LANE), lambda i, j: (0, 0)),
                  pl.BlockSpec((1, 2, t, DH), lambda i, j: (i, j, 0, 0))],
        out_specs=[pl.BlockSpec((t, LANE), lambda i, j: (0, i * nj + j)),
                   pl.BlockSpec((1, 1, 1, LANE), lambda i, j: (i, j, 0, 0))],
        out_shape=[jax.ShapeDtypeStruct((t, 3 * WIDTH), MXU), jax.ShapeDtypeStruct((3, nj, 1, LANE), F32)],
        compiler_params=_params(("arbitrary", "arbitrary")),
    )(pm, nw, dqkv)


SC = 256
CPS = SC // CHUNK
GDN_HP_FWD = 4
GDN_HP_BWD = 2
Q_SCALE = DH ** -0.5


def _sc_masks():
    ri = lax.broadcasted_iota(jnp.int32, (SC, SC), 0)
    ci = lax.broadcasted_iota(jnp.int32, (SC, SC), 1)
    same = (ri // CHUNK) == (ci // CHUNK)
    return same & (ri >= ci), same & (ri > ci), ri == ci


def _unit_lower_inverses(ms, eye):
    invs = [jnp.where(eye, 1.0, 0.0) + m for m in ms]
    ms = [_dot(m, m) for m in ms]
    for _ in range(4):
        both = [_dot(jnp.concatenate([inv, m], axis=0), m) for inv, m in zip(invs, ms)]
        invs = [inv + b[:SC] for inv, b in zip(invs, both)]
        ms = [b[SC:] for b in both]
    return [inv + _dot(inv, m) for inv, m in zip(invs, ms)]


def _lane_col(blk, lane_idx):
    lane = lax.broadcasted_iota(jnp.int32, blk.shape, 1)
    return jnp.sum(jnp.where(lane == lane_idx, blk, 0.0), axis=1, keepdims=True)


def _to_lane(col, lane_idx):
    lane = lax.broadcasted_iota(jnp.int32, (col.shape[0], LANE), 1)
    return jnp.where(lane == lane_idx, col, 0.0)


def _gdn_columns(gates_ref, run_ref, tot_ref, runt_ref, rows, h):
    return (_lane_col(gates_ref[rows, :], h), _lane_col(run_ref[rows, :], HEADS + h),
            _lane_col(tot_ref[rows, :], HEADS + h), runt_ref[pl.ds(h, 1), rows])


def _gdn_local(q, k, beta, gc, gl, grow, causal, with_kk=True):
    decay = jnp.exp(jnp.where(causal, gc - grow, -1e30))
    egc = jnp.exp(gc)
    ekd = jnp.exp(gl - gc)
    qs = q * Q_SCALE
    kb = k * beta
    if with_kk:
        both = _dot_nt(jnp.concatenate([kb, qs], axis=0), k)
        kk, qk = both[:SC], both[SC:]
    else:
        kk, qk = None, _dot_nt(qs, k)
    return beta, gl, decay, egc, ekd, qs, kb, kk, qk, jnp.where(causal, qk * decay, 0.0)


def _chunk_rows(c):
    return pl.ds(c * CHUNK if isinstance(c, int) else pl.multiple_of(c * CHUNK, CHUNK), CHUNK)


def _sc_rows(b):
    return pl.ds(b * SC if isinstance(b, int) else pl.multiple_of(b * SC, SC), SC)


def _gdn_fwd(qkv, gates, run, tot, run_t):
    t = qkv.shape[2]
    nc = t // CHUNK
    nsc = t // SC

    hp_n = GDN_HP_FWD
    heads = range(hp_n)

    def body(qkv_ref, gates_ref, run_ref, tot_ref, runt_ref, o_ref, st_ref, inv_ref, kc_s, qc_s, g_s, au_s):
        hp = pl.program_id(0)
        causal, strict, eye = _sc_masks()

        def local(b, _):
            rows = _sc_rows(b)
            loc = [_gdn_local(qkv_ref[0, hh, rows, :], qkv_ref[1, hh, rows, :],
                              *_gdn_columns(gates_ref, run_ref, tot_ref, runt_ref, rows, hp * hp_n + hh), causal)
                   for hh in heads]
            invs = _unit_lower_inverses([-jnp.where(strict, l[7] * l[2], 0.0) for l in loc], eye)
            uws = []
            for hh in heads:
                beta, _, _, egc, _, _, kb, _, _, _ = loc[hh]
                inv_ref[hh, rows, :] = invs[hh].astype(inv_ref.dtype)
                uws.append(_dot(invs[hh], jnp.concatenate([qkv_ref[2, hh, rows, :] * beta, kb * egc], axis=1)))
            for hh in heads:
                _, _, _, egc, ekd, qs, _, _, _, attn = loc[hh]
                auw = _dot(attn, uws[hh])
                g_s[hh, rows, :] = qs * egc - auw[:, DH:]
                au_s[hh, rows, :] = auw[:, :DH]
                kd = qkv_ref[1, hh, rows, :] * ekd
                for j in range(CPS):
                    sl = slice(j * CHUNK, (j + 1) * CHUNK)
                    both = _dot_tn(kd[sl], uws[hh][sl])
                    kc_s[hh, b * CPS + j] = both[:, DH:]
                    qc_s[hh, b * CPS + j] = both[:, :DH]
            return 0

        def step(c, states):
            rows = _chunk_rows(c)
            tot_row = tot_ref[pl.ds(c * CHUNK, 1), :]
            new = []
            for hh in heads:
                s = states[hh]
                st_ref[hh, c] = s
                o_ref[hh, rows, :] = _dot(g_s[hh, rows, :], s) + au_s[hh, rows, :]
                egl = jnp.exp(_lane_col(tot_row, HEADS + hp * hp_n + hh))
                new.append(egl * s - _dot(kc_s[hh, c], s) + qc_s[hh, c])
            return tuple(new)

        def steps_of(b, states):
            for j in range(CPS):
                states = step(b * CPS + j, states)
            return states

        def fused(b, states):
            states = steps_of(b - 1, states)
            local(b, 0)
            return states

        local(0, 0)
        states = lax.fori_loop(1, nsc, fused, tuple(jnp.zeros((DH, DH), F32) for _ in heads))
        steps_of(nsc - 1, states)

    whole = pl.BlockSpec((t, LANE), lambda h: (0, 0))
    once = dict(pipeline_mode=pl.Buffered(1))
    return pl.pallas_call(
        body, name="gdn_fwd", grid=(HEADS // hp_n,),
        in_specs=[pl.BlockSpec((3, hp_n, t, DH), lambda h: (0, h, 0, 0), **once), whole, whole, whole,
                  pl.BlockSpec((HEADS, t), lambda h: (1, 0))],
        out_specs=[pl.BlockSpec((hp_n, t, DH), lambda h: (h, 0, 0), **once),
                   pl.BlockSpec((hp_n, nc, DH, DH), lambda h: (h, 0, 0, 0), **once),
                   pl.BlockSpec((hp_n, t, SC), lambda h: (h, 0, 0), **once)],
        out_shape=[jax.ShapeDtypeStruct((HEADS, t, DH), F32), jax.ShapeDtypeStruct((HEADS, nc, DH, DH), F32),
                   jax.ShapeDtypeStruct((HEADS, t, SC), MXU)],
        scratch_shapes=[pltpu.VMEM((hp_n, nc, DH, DH), F32), pltpu.VMEM((hp_n, nc, DH, DH), F32),
                        pltpu.VMEM((hp_n, t, DH), F32), pltpu.VMEM((hp_n, t, DH), F32)],
        compiler_params=_params(("arbitrary",)),
    )(qkv, gates, run, tot, run_t)


def _gdn_bwd(qkv, gates, run, tot, run_t, inv, states, do):
    t = qkv.shape[2]
    nc = t // CHUNK
    nsc = t // SC

    hp_n = GDN_HP_BWD
    heads = range(hp_n)

    def body(qkv_ref, gates_ref, run_ref, tot_ref, runt_ref, inv_ref, st_ref, do_ref,
             dqkv_ref, dcol_ref, dtot_ref, drow_ref, uw_s, kc_s, h_s, dsn_s):
        hp = pl.program_id(0)
        causal, strict, _ = _sc_masks()

        @pl.when(hp == 0)
        def _():
            dcol_ref[...] = jnp.zeros_like(dcol_ref)
            dtot_ref[...] = jnp.zeros_like(dtot_ref)

        def local_of(hh, rows, with_kk=True):
            return _gdn_local(qkv_ref[0, hh, rows, :], qkv_ref[1, hh, rows, :],
                              *_gdn_columns(gates_ref, run_ref, tot_ref, runt_ref, rows, hp * hp_n + hh), causal, with_kk)

        def local(b, _):
            rows = _sc_rows(b)
            loc = [local_of(hh, rows, with_kk=False) for hh in heads]
            us, ws = [], []
            for hh in heads:
                beta, _, _, egc, _, _, kb, _, _, _ = loc[hh]
                inv_b = inv_ref[hh, rows, :]
                us.append(_dot(inv_b, qkv_ref[2, hh, rows, :] * beta))
                ws.append(_dot(inv_b, kb * egc))
            gs = [loc[hh][5] * loc[hh][3] - _dot(loc[hh][9], ws[hh]) for hh in heads]
            for hh in heads:
                uw_s[hh, rows, :] = jnp.concatenate([us[hh], ws[hh]], axis=1)
                kd = qkv_ref[1, hh, rows, :] * loc[hh][4]
                dout = do_ref[hh, rows, :]
                for j in range(CPS):
                    sl = slice(j * CHUNK, (j + 1) * CHUNK)
                    kc_s[hh, b * CPS + j] = _dot_tn(kd[sl], ws[hh][sl])
                    h_s[hh, b * CPS + j] = _dot_tn(gs[hh][sl], dout[sl])
            return 0

        def step(c, dss):
            tot_row = tot_ref[pl.ds(c * CHUNK, 1), :]
            new = []
            for hh in heads:
                ds = dss[hh]
                dsn_s[hh, c] = ds
                egl = jnp.exp(_lane_col(tot_row, HEADS + hp * hp_n + hh))
                new.append(egl * ds - _dot_tn(kc_s[hh, c], ds) + h_s[hh, c])
            return tuple(new)

        def steps_of(b, dss):
            for j in reversed(range(CPS)):
                dss = step(b * CPS + j, dss)
            return dss

        def fused(i, dss):
            b = nsc - 2 - i
            dss = steps_of(b + 1, dss)
            local(b, 0)
            return dss

        local(nsc - 1, 0)
        dss = lax.fori_loop(0, nsc - 1, fused, tuple(jnp.zeros((DH, DH), F32) for _ in heads))
        steps_of(0, dss)

        def back(b, _):
            rows = _sc_rows(b)
            first = lax.broadcasted_iota(jnp.int32, (CHUNK, 1), 0) == 0
            loc = [local_of(hh, rows) for hh in heads]
            sign = jnp.where(lax.broadcasted_iota(jnp.int32, (1, LANE), 1) < DH, 1.0, -1.0)
            mid = []
            for hh in heads:
                beta, gl, decay, egc, ekd, qs, kb, kk, qk, attn = loc[hh]
                uw = uw_s[hh, rows, :]
                kd = qkv_ref[1, hh, rows, :] * ekd
                dout = do_ref[hh, rows, :]
                dg_p, dkd_p, duw_p, dgl_p = [], [], [], []
                for j in range(CPS):
                    sl = slice(j * CHUNK, (j + 1) * CHUNK)
                    s = st_ref[hh, b * CPS + j]
                    dsn = dsn_s[hh, b * CPS + j]
                    both = _dot_nt(jnp.concatenate([dsn, dout[sl]], axis=0), s)
                    dg_p.append(both[CHUNK:])
                    dsc = jnp.concatenate([dsn, -both[:CHUNK]], axis=1)
                    dkd_p.append(_dot_nt(uw[sl], dsc))
                    duw_p.append(_dot(kd[sl], dsc))
                    degl = jnp.sum(jnp.sum(s * dsn, axis=1, keepdims=True), axis=0, keepdims=True)
                    dgl_p.append(jnp.where(first, degl * jnp.exp(gl[j * CHUNK:j * CHUNK + 1, :]), 0.0))
                dg, dkd = jnp.concatenate(dg_p, axis=0), jnp.concatenate(dkd_p, axis=0)
                dod = jnp.concatenate([dout, dg], axis=1)
                da = jnp.where(causal, _dot_nt(dod, uw * sign), 0.0)
                duw = jnp.concatenate(duw_p, axis=0) + _dot_tn(attn, dod) * sign
                mid.append((kd, dg, dkd, da, duw, jnp.concatenate(dgl_p, axis=0)))
            inv_ts = [inv_ref[hh, rows, :].astype(F32).T for hh in heads]
            its = [_dot(inv_ts[hh], mid[hh][4]) for hh in heads]
            dinvs = []
            for hh in heads:
                beta, _, _, egc, _, _, kb, _, _, _ = loc[hh]
                dinvs.append(_dot_nt(mid[hh][4], jnp.concatenate([qkv_ref[2, hh, rows, :] * beta, kb * egc], axis=1)))
            half = [_dot(inv_ts[hh], dinvs[hh]) for hh in heads]
            dls = [jnp.where(strict, -_dot(half[hh], inv_ts[hh]), 0.0) for hh in heads]
            for hh in heads:
                head = hp * hp_n + hh
                beta, gl, decay, egc, ekd, qs, kb, kk, qk, attn = loc[hh]
                kd, dg, dkd, da, _, dgl_first = mid[hh]
                k, v = qkv_ref[1, hh, rows, :], qkv_ref[2, hh, rows, :]
                dvb, dkbe = its[hh][:, :DH], its[hh][:, DH:]
                dl = dls[hh]
                dlogd = (dl * kk + da * qk) * decay
                dd = jnp.concatenate([dl * decay, da * decay], axis=0)
                ddk = _dot(dd, k)
                dkb = ddk[:SC] + dkbe * egc
                dqs = ddk[SC:] + dg * egc
                dk = _dot_tn(dd, jnp.concatenate([kb, qs], axis=0)) + dkd * ekd + dkb * beta
                dkd_kd = jnp.sum(dkd * kd, axis=1, keepdims=True)
                narrow = dg * (qs * egc) + dkbe * (kb * egc)
                wide = dlogd[:, :LANE] + dlogd[:, LANE:] + jnp.concatenate([narrow, jnp.zeros((SC, LANE - DH), F32)], axis=1)
                dgc = jnp.sum(wide, axis=1, keepdims=True) - dkd_kd
                dbeta = jnp.sum(dkb * k + dvb * v, axis=1, keepdims=True)
                dqkv_ref[0, hh, rows, :] = dqs * Q_SCALE
                dqkv_ref[1, hh, rows, :] = dk
                dqkv_ref[2, hh, rows, :] = dvb * beta
                dcol_ref[rows, :] += _to_lane(dbeta, head) + _to_lane(dgc, HEADS + head)
                dtot_ref[rows, :] += _to_lane(dkd_kd + dgl_first, HEADS + head)
                drow_ref[pl.ds(head, 1), rows] = -jnp.sum(dlogd, axis=0, keepdims=True)
            return 0

        lax.fori_loop(0, nsc, back, 0)

    whole = pl.BlockSpec((t, LANE), lambda h: (0, 0))
    rowspec = pl.BlockSpec((HEADS, t), lambda h: (0, 0))
    sq = pltpu.VMEM((hp_n, nc, DH, DH), F32)
    per_head = pltpu.VMEM((hp_n, t, 2 * DH), F32)
    return pl.pallas_call(
        body, name="gdn_bwd", grid=(HEADS // hp_n,),
        in_specs=[pl.BlockSpec((3, hp_n, t, DH), lambda h: (0, h, 0, 0)), whole, whole, whole,
                  pl.BlockSpec((HEADS, t), lambda h: (1, 0)),
                  pl.BlockSpec((hp_n, t, SC), lambda h: (h, 0, 0)), pl.BlockSpec((hp_n, nc, DH, DH), lambda h: (h, 0, 0, 0)),
                  pl.BlockSpec((hp_n, t, DH), lambda h: (h, 0, 0))],
        out_specs=[pl.BlockSpec((3, hp_n, t, DH), lambda h: (0, h, 0, 0)), whole, whole, rowspec],
        out_shape=[jax.ShapeDtypeStruct((3, HEADS, t, DH), F32), jax.ShapeDtypeStruct((t, LANE), F32),
                   jax.ShapeDtypeStruct((t, LANE), F32), jax.ShapeDtypeStruct((HEADS, t), F32)],
        scratch_shapes=[per_head, sq, sq, sq],
        compiler_params=_params(("arbitrary",)),
    )(qkv, gates, run, tot, run_t, inv, states, do)


FOX_HP_FWD = 4
FOX_HP_BWD = 4


def _wide_t(a):
    r = a.shape[0]
    return jnp.concatenate([a, jnp.zeros((r, LANE - DH), F32)], axis=1).T[:DH]


def _tall_t(a):
    r = a.shape[1]
    return jnp.concatenate([a, jnp.zeros((LANE - DH, r), F32)], axis=0).T[:, :DH]


def _key_side_f(run_blk, head, qb):
    col = jnp.broadcast_to(_lane_col(run_blk, 2 * HEADS + head), (run_blk.shape[0], LANE))
    return jnp.concatenate([col] * (qb // LANE), axis=1)


def _diag_mask(qb):
    return lax.broadcasted_iota(jnp.int32, (qb, qb), 0) <= lax.broadcasted_iota(jnp.int32, (qb, qb), 1)


def _fox_fwd(qkv, run, run_t, carry=None):
    t = qkv.shape[2]
    qb = min(QB, t)
    nq = t // qb
    hp_n = FOX_HP_FWD
    c_in, c_in_specs, c_out_shape, c_out_specs, c_sem = _carry_operands(*carry) if carry else ([], [], [], [], None)
    n_c = len(c_in)

    def body(*refs):
        q_ref, k_ref, v_ref, run_ref, runt_ref = refs[:5]
        o_ref, lse_ref = refs[5 + n_c:7 + n_c]
        s0 = 7 + n_c + len(c_out_shape)
        kb_s, vt_s, raw_s, m_s, l_s, acc_s = refs[s0:s0 + 6]
        hp = pl.program_id(0)
        i = pl.program_id(1)

        if carry:
            _carry_step(carry[0], refs[5:5 + n_c], refs[7 + n_c], refs[8 + n_c], refs[-1], hp * nq + i,
                        (HEADS // hp_n) * nq)

        @pl.when(i == 0)
        def _():
            for hh in range(hp_n):
                kb_s[hh] = k_ref[0, hh].astype(MXU)
                for b in range(nq):
                    rows = slice(b * qb, (b + 1) * qb)
                    vt_s[hh, :, rows] = _wide_t(v_ref[0, hh, rows, :]).astype(MXU)

        qrows = pl.ds(pl.multiple_of(i * qb, qb), qb)
        qs = [(q_ref[0, hh] * Q_SCALE).astype(MXU) for hh in range(hp_n)]
        fq = [runt_ref[pl.ds(hp * hp_n + hh, 1), qrows] for hh in range(hp_n)]

        def block_rows(j):
            return pl.ds(pl.multiple_of(j * qb, qb), qb)

        def scores(j):
            for hh in range(hp_n):
                raw_s[j % 2, hh] = _dot_nt(kb_s[hh, block_rows(j), :], qs[hh])

        def absorb(j, diagonal):
            rows = block_rows(j)
            run_blk = run_ref[rows, :]
            stats, pv = [], []
            for hh in range(hp_n):
                m, l = m_s[hh], l_s[hh]
                st = raw_s[j % 2, hh] + (fq[hh] - _key_side_f(run_blk, hp * hp_n + hh, qb))
                if diagonal:
                    st = jnp.where(_diag_mask(qb), st, -1e30)
                m_new = jnp.maximum(m, jnp.max(st, axis=0, keepdims=True))
                p = jnp.exp(st - m_new)
                alpha = jnp.exp(m - m_new)
                stats.append((m_new, alpha * l + jnp.sum(p, axis=0, keepdims=True), alpha))
                pv.append(_dot(vt_s[hh, :, rows], p))
            for hh in range(hp_n):
                m_s[hh], l_s[hh] = stats[hh][0], stats[hh][1]
                acc_s[hh] = stats[hh][2] * acc_s[hh] + pv[hh]

        def kstep(j, _):
            scores(j + 1)
            absorb(j, False)
            return 0

        for hh in range(hp_n):
            m_s[hh] = jnp.full((1, qb), -1e30, F32)
            l_s[hh] = jnp.zeros((1, qb), F32)
            acc_s[hh] = jnp.zeros((DH, qb), F32)
        scores(0)
        lax.fori_loop(0, i, kstep, 0)
        absorb(i, True)
        for hh in range(hp_n):
            l = l_s[hh]
            o_ref[hh] = _tall_t(acc_s[hh] / l)
            lse_ref[pl.ds(hp * hp_n + hh, 1), qrows] = m_s[hh] + jnp.log(l)

    out = pl.pallas_call(
        body, name="fox_fwd", grid=(HEADS // hp_n, nq),
        in_specs=[pl.BlockSpec((1, hp_n, qb, DH), lambda h, i: (0, h, i, 0)),
                  pl.BlockSpec((1, hp_n, t, DH), lambda h, i: (1, h, 0, 0)),
                  pl.BlockSpec((1, hp_n, t, DH), lambda h, i: (2, h, 0, 0)),
                  pl.BlockSpec((t, LANE), lambda h, i: (0, 0)),
                  pl.BlockSpec((HEADS, t), lambda h, i: (2, 0))] + c_in_specs,
        out_specs=[pl.BlockSpec((hp_n, qb, DH), lambda h, i: (h, i, 0)),
                   pl.BlockSpec((HEADS, t), lambda h, i: (0, 0))] + c_out_specs,
        out_shape=[jax.ShapeDtypeStruct((HEADS, t, DH), F32), jax.ShapeDtypeStruct((HEADS, t), F32)] + c_out_shape,
        input_output_aliases={5 + j: 4 + j for j in range(n_c)},
        scratch_shapes=[pltpu.VMEM((hp_n, t, DH), MXU), pltpu.VMEM((hp_n, DH, t), MXU), pltpu.VMEM((2, hp_n, qb, qb), F32),
                        pltpu.VMEM((hp_n, 1, qb), F32), pltpu.VMEM((hp_n, 1, qb), F32), pltpu.VMEM((hp_n, DH, qb), F32)]
        + ([c_sem] if carry else []),
        compiler_params=_params(("arbitrary", "arbitrary")),
    )(qkv, qkv, qkv, run, run_t, *c_in)
    return (out[0], out[1], _carry_state(out[2:])) if carry else (out[0], out[1])


def _fox_bwd(qkv, run, run_t, o, lse, do, carry=None):
    t = qkv.shape[2]
    qb = min(QB, t)
    nq = t // qb
    hp_n = FOX_HP_BWD
    c_in, c_in_specs, c_out_shape, c_out_specs, c_sem = _carry_operands(*carry) if carry else ([], [], [], [], None)
    n_c = len(c_in)

    def body(*refs):
        q_ref, k_ref, v_ref, run_ref, runt_ref, o_ref, lse_ref, do_ref = refs[:8]
        dqkv_ref, dcol_ref, drow_ref = refs[8 + n_c:11 + n_c]
        s0 = 11 + n_c + len(c_out_shape)
        dqt_s, raw_s, dpt_s, dro_s, dk_s, dv_s, dsum_s = refs[s0:s0 + 7]
        hp = pl.program_id(0)
        j = pl.program_id(1)

        if carry:
            _carry_step(carry[0], refs[8:8 + n_c], refs[11 + n_c], refs[12 + n_c], refs[-1], hp * nq + j,
                        (HEADS // hp_n) * nq)

        @pl.when(j == 0)
        def _():
            dqt_s[...] = jnp.zeros_like(dqt_s)

        @pl.when((j == 0) & (hp == 0))
        def _():
            dcol_ref[...] = jnp.zeros_like(dcol_ref)
            drow_ref[...] = jnp.zeros_like(drow_ref)

        krows = pl.ds(pl.multiple_of(j * qb, qb), qb)
        run_blk = run_ref[krows, :]
        ones8 = jnp.ones((8, DH), MXU)
        kb, kt, vb, fk = [], [], [], []
        for hh in range(hp_n):
            kf = k_ref[0, hh]
            kb.append(kf.astype(MXU))
            kt.append(_wide_t(kf).astype(MXU))
            vb.append(v_ref[0, hh].astype(MXU))
            fk.append(_key_side_f(run_blk, hp * hp_n + hh, qb))

        def block_rows(i):
            return pl.ds(i * qb if isinstance(i, int) else pl.multiple_of(i * qb, qb), qb)

        def products(i):
            rows = block_rows(i)
            slot = i % 2
            for hh in range(hp_n):
                dout = do_ref[hh, rows, :]
                x = dout * o_ref[hh, rows, :]
                x_hi = x.astype(MXU)
                raw_s[slot, hh] = _dot_nt(kb[hh], q_ref[0, hh, rows, :] * Q_SCALE)
                dpt_s[slot, hh] = _dot_nt(vb[hh], dout)
                dro_s[slot, hh] = _dot_nt(ones8, x_hi) + _dot_nt(ones8, x - x_hi.astype(F32))

        def absorb(i, diagonal):
            rows = block_rows(i)
            slot = i % 2
            pieces = []
            for hh in range(hp_n):
                head = hp * hp_n + hh
                raw, dpt, drow = raw_s[slot, hh], dpt_s[slot, hh], dro_s[slot, hh, 0:1, :]
                off = runt_ref[pl.ds(head, 1), rows] - lse_ref[pl.ds(head, 1), rows]
                st = raw + (off - fk[hh])
                if diagonal:
                    st = jnp.where(_diag_mask(qb), st, -1e30)
                pt = jnp.exp(st)
                dst = pt * (dpt - drow)
                drow_ref[pl.ds(head, 1), rows] += jnp.sum(dst, axis=0, keepdims=True)
                folded = dst[:, 0:LANE]
                for c in range(1, qb // LANE):
                    folded = folded + dst[:, c * LANE:(c + 1) * LANE]
                pieces.append((_dot(dst, q_ref[0, hh, rows, :] * Q_SCALE), _dot(pt, do_ref[hh, rows, :]),
                               _dot(kt[hh], dst), folded))
            for hh in range(hp_n):
                dqt_s[hh, :, rows] += pieces[hh][2]
                if diagonal:
                    dk_s[hh], dv_s[hh], dsum_s[hh] = pieces[hh][0], pieces[hh][1], pieces[hh][3]
                else:
                    dk_s[hh] += pieces[hh][0]
                    dv_s[hh] += pieces[hh][1]
                    dsum_s[hh] += pieces[hh][3]

        def qstep(i, _):
            products(i + 1)
            absorb(i, False)
            return 0

        products(j)
        products(jnp.minimum(j + 1, nq - 1))
        absorb(j, True)
        lax.fori_loop(j + 1, nq - 1, qstep, 0)

        @pl.when(j < nq - 1)
        def _():
            absorb(nq - 1, False)
        for hh in range(hp_n):
            dqkv_ref[1, hh, krows, :] = dk_s[hh]
            dqkv_ref[2, hh, krows, :] = dv_s[hh]
            dcol_ref[krows, :] += _to_lane(-jnp.sum(dsum_s[hh], axis=1, keepdims=True), 2 * HEADS + hp * hp_n + hh)

        @pl.when(j == nq - 1)
        def _():
            for hh in range(hp_n):
                for b in range(nq):
                    rows = slice(b * qb, (b + 1) * qb)
                    dqkv_ref[0, hh, rows, :] = _tall_t(dqt_s[hh, :, rows]) * Q_SCALE

    once = dict(pipeline_mode=pl.Buffered(1))
    full = pl.BlockSpec((hp_n, t, DH), lambda h, j: (h, 0, 0), **once)
    rows8 = pl.BlockSpec((HEADS, t), lambda h, j: (0, 0))
    out = pl.pallas_call(
        body, name="fox_bwd", grid=(HEADS // hp_n, nq),
        in_specs=[pl.BlockSpec((1, hp_n, t, DH), lambda h, j: (0, h, 0, 0), **once),
                  pl.BlockSpec((1, hp_n, qb, DH), lambda h, j: (1, h, j, 0)),
                  pl.BlockSpec((1, hp_n, qb, DH), lambda h, j: (2, h, j, 0)),
                  pl.BlockSpec((t, LANE), lambda h, j: (0, 0), **once), pl.BlockSpec((HEADS, t), lambda h, j: (2, 0)),
                  full, rows8, full] + c_in_specs,
        out_specs=[pl.BlockSpec((3, hp_n, t, DH), lambda h, j: (0, h, 0, 0), **once),
                   pl.BlockSpec((t, LANE), lambda h, j: (0, 0)), rows8] + c_out_specs,
        out_shape=[jax.ShapeDtypeStruct((3, HEADS, t, DH), F32), jax.ShapeDtypeStruct((t, LANE), F32),
                   jax.ShapeDtypeStruct((HEADS, t), F32)] + c_out_shape,
        input_output_aliases={8 + j: 5 + j for j in range(n_c)},
        scratch_shapes=[pltpu.VMEM((hp_n, DH, t), F32), pltpu.VMEM((2, hp_n, qb, qb), F32), pltpu.VMEM((2, hp_n, qb, qb), F32),
                        pltpu.VMEM((2, hp_n, 8, qb), F32), pltpu.VMEM((hp_n, qb, DH), F32), pltpu.VMEM((hp_n, qb, DH), F32),
                        pltpu.VMEM((hp_n, qb, LANE), F32)] + ([c_sem] if carry else []),
        compiler_params=_params(("arbitrary", "arbitrary")),
    )(qkv, qkv, qkv, run, run_t, o, lse, do, *c_in)
    return (out[0], out[1], out[2], _carry_state(out[3:])) if carry else tuple(out)


Z_COL0 = 3 * WIDTH // LANE
FGATE_COL0 = 7 * WIDTH // LANE


def _gdn_post(o, pm, onw):
    t = pm.shape[0]

    def body(o_ref, z_ref, w_ref, m_ref):
        z = z_ref[...]
        sz = z * _sigmoid(z)
        w2 = jnp.concatenate([w_ref[...], w_ref[...]], axis=1)
        ov = jnp.concatenate([o_ref[0], o_ref[1]], axis=1)
        n = ov * lax.rsqrt(_group_sum(ov * ov, _head_sum_matrix()) * (1.0 / DH) + EPS) * w2
        m_ref[...] = (n * sz).astype(m_ref.dtype)

    return pl.pallas_call(
        body, name="gdn_post", grid=(WIDTH // LANE,),
        in_specs=[pl.BlockSpec((2, t, DH), lambda j: (j, 0, 0)), pl.BlockSpec((t, LANE), lambda j: (0, Z_COL0 + j)),
                  pl.BlockSpec((1, DH), lambda j: (0, 0))],
        out_specs=pl.BlockSpec((t, LANE), lambda j: (0, j)),
        out_shape=jax.ShapeDtypeStruct((t, WIDTH), MXU),
        compiler_params=_params(("arbitrary",)),
    )(o, pm, onw)


def _gdn_post_bwd(o, pm, onw, dmix):
    t = pm.shape[0]

    def body(o_ref, z_ref, w_ref, dm_ref, do_ref, dz_ref, dw_ref):
        @pl.when(pl.program_id(0) == 0)
        def _():
            dw_ref[...] = jnp.zeros_like(dw_ref)

        z = z_ref[...]
        sg = _sigmoid(z)
        sz = z * sg
        dsz = sg * (1.0 + z * (1.0 - sg))
        dm = dm_ref[...]
        hs = _head_sum_matrix()
        w2 = jnp.concatenate([w_ref[...], w_ref[...]], axis=1)
        ov = jnp.concatenate([o_ref[0], o_ref[1]], axis=1)
        r = lax.rsqrt(_group_sum(ov * ov, hs) * (1.0 / DH) + EPS)
        xn = ov * r
        dn = dm * sz
        dz_ref[...] = (dm * (xn * w2) * dsz).astype(dz_ref.dtype)
        dw2 = jnp.sum(dn * xn, axis=0, keepdims=True)
        dw_ref[...] += dw2[:, :DH] + dw2[:, DH:]
        g = dn * w2
        do = r * (g - xn * (_group_sum(g * xn, hs) * (1.0 / DH)))
        do_ref[0] = do[:, :DH].astype(do_ref.dtype)
        do_ref[1] = do[:, DH:].astype(do_ref.dtype)

    return pl.pallas_call(
        body, name="gdn_post_bwd", grid=(WIDTH // LANE,),
        in_specs=[pl.BlockSpec((2, t, DH), lambda j: (j, 0, 0)), pl.BlockSpec((t, LANE), lambda j: (0, Z_COL0 + j)),
                  pl.BlockSpec((1, DH), lambda j: (0, 0)), pl.BlockSpec((t, LANE), lambda j: (0, j))],
        out_specs=[pl.BlockSpec((2, t, DH), lambda j: (j, 0, 0)), pl.BlockSpec((t, LANE), lambda j: (0, j)),
                   pl.BlockSpec((1, DH), lambda j: (0, 0))],
        out_shape=[jax.ShapeDtypeStruct((HEADS, t, DH), MXU), jax.ShapeDtypeStruct((t, WIDTH), MXU),
                   jax.ShapeDtypeStruct((1, DH), F32)],
        compiler_params=_params(("arbitrary",)),
    )(o, pm, onw, dmix)


def _fox_post(o, pm):
    t = pm.shape[0]

    def body(o_ref, g_ref, m_ref):
        m_ref[...] = (jnp.concatenate([o_ref[0], o_ref[1]], axis=1) * _sigmoid(g_ref[...])).astype(m_ref.dtype)

    return pl.pallas_call(
        body, name="fox_post", grid=(WIDTH // LANE,),
        in_specs=[pl.BlockSpec((2, t, DH), lambda j: (j, 0, 0)), pl.BlockSpec((t, LANE), lambda j: (0, FGATE_COL0 + j))],
        out_specs=pl.BlockSpec((t, LANE), lambda j: (0, j)),
        out_shape=jax.ShapeDtypeStruct((t, WIDTH), MXU),
        compiler_params=_params(("arbitrary",)),
    )(o, pm)


def _fox_post_bwd(o, pm, dmix):
    t = pm.shape[0]

    def body(o_ref, g_ref, dm_ref, do_ref, dg_ref):
        sg = _sigmoid(g_ref[...])
        dm = dm_ref[...]
        for hh in range(2):
            cols = slice(hh * DH, (hh + 1) * DH)
            do_ref[hh] = dm[:, cols] * sg[:, cols]
            dg_ref[:, cols] = (dm[:, cols] * o_ref[hh] * (sg * (1.0 - sg))[:, cols]).astype(dg_ref.dtype)

    return pl.pallas_call(
        body, name="fox_post_bwd", grid=(WIDTH // LANE,),
        in_specs=[pl.BlockSpec((2, t, DH), lambda j: (j, 0, 0)), pl.BlockSpec((t, LANE), lambda j: (0, FGATE_COL0 + j)),
                  pl.BlockSpec((t, LANE), lambda j: (0, j))],
        out_specs=[pl.BlockSpec((2, t, DH), lambda j: (j, 0, 0)), pl.BlockSpec((t, LANE), lambda j: (0, j))],
        out_shape=[jax.ShapeDtypeStruct((HEADS, t, DH), F32), jax.ShapeDtypeStruct((t, WIDTH), MXU)],
        compiler_params=_params(("arbitrary",)),
    )(o, pm, dmix)


def _tail(x, mixg, mixf, tgt, wo, n2w, wg_t, wu_t, wd, fw):
    t, d = x.shape
    dff = wd.shape[0]
    tb = min(TB, t)

    def body(x_ref, mg_ref, mf_ref, t_ref, wo_ref, n2_ref, wg_ref, wu_ref, wd_ref, fw_ref,
             h2_ref, act_ref, dgate_ref, dup_ref, dx3_ref, dx2_ref, dmg_ref, dmf_ref, dn2_ref, dfw_ref, loss_ref):
        @pl.when(pl.program_id(0) == 0)
        def _():
            dn2_ref[...] = jnp.zeros_like(dn2_ref)
            dfw_ref[...] = jnp.zeros_like(dfw_ref)
            loss_ref[...] = jnp.zeros_like(loss_ref)

        x2 = x_ref[...] + _dot(mg_ref[...], wo_ref[0:WIDTH, :]) + _dot(mf_ref[...], wo_ref[WIDTH:2 * WIDTH, :])
        r2 = lax.rsqrt(jnp.mean(x2 * x2, axis=-1, keepdims=True) + EPS)
        xn2 = x2 * r2
        h2 = (xn2 * n2_ref[...]).astype(MXU)
        h2_ref[...] = h2
        gate = _dot_nt(h2, wg_ref[...])
        up = _dot_nt(h2, wu_ref[...])
        sg = _sigmoid(gate)
        sl = gate * sg
        act = (sl * up).astype(MXU)
        act_ref[...] = act
        x3 = x2 + _dot(act, wd_ref[...])
        r3 = lax.rsqrt(jnp.mean(x3 * x3, axis=-1, keepdims=True) + EPS)
        xn3 = x3 * r3
        err = xn3 * fw_ref[...] - t_ref[...]
        loss_ref[...] += 0.5 * jnp.sum(jnp.mean(err * err, axis=-1, keepdims=True), axis=0, keepdims=True)
        dy = err * (1.0 / d)
        dfw_ref[...] += jnp.sum(dy * xn3, axis=0, keepdims=True)
        g3 = dy * fw_ref[...]
        dx3 = r3 * (g3 - xn3 * jnp.mean(g3 * xn3, axis=-1, keepdims=True))
        dx3_ref[...] = dx3.astype(MXU)
        dact = _dot_nt(dx3, wd_ref[...])
        dgate = (dact * up * (sg * (1.0 + gate * (1.0 - sg)))).astype(MXU)
        dup = (dact * sl).astype(MXU)
        dgate_ref[...] = dgate
        dup_ref[...] = dup
        dh2 = _dot(dgate, wg_ref[...]) + _dot(dup, wu_ref[...])
        dn2_ref[...] += jnp.sum(dh2 * xn2, axis=0, keepdims=True)
        g2 = dh2 * n2_ref[...]
        dx2 = dx3 + r2 * (g2 - xn2 * jnp.mean(g2 * xn2, axis=-1, keepdims=True))
        dx2_ref[...] = dx2
        dmg_ref[...] = _dot_nt(dx2, wo_ref[0:WIDTH, :])
        dmf_ref[...] = _dot_nt(dx2, wo_ref[WIDTH:2 * WIDTH, :])

    def tok(n):
        return pl.BlockSpec((tb, n), lambda i: (i, 0))

    acc = pl.BlockSpec((1, d), lambda i: (0, 0))
    sds = jax.ShapeDtypeStruct
    return pl.pallas_call(
        body, name="tail", grid=(t // tb,),
        in_specs=[tok(d), tok(WIDTH), tok(WIDTH), tok(d), _resident(wo.shape), _resident((1, d)),
                  _resident(wg_t.shape), _resident(wu_t.shape), _resident(wd.shape), _resident((1, d))],
        out_specs=[tok(d), tok(dff), tok(dff), tok(dff), tok(d), tok(d), tok(WIDTH), tok(WIDTH), acc, acc,
                   pl.BlockSpec((1, 1), lambda i: (0, 0))],
        out_shape=[sds((t, d), MXU), sds((t, dff), MXU), sds((t, dff), MXU), sds((t, dff), MXU), sds((t, d), MXU),
                   sds((t, d), F32), sds((t, WIDTH), F32), sds((t, WIDTH), F32), sds((1, d), F32), sds((1, d), F32),
                   sds((1, 1), F32)],
        compiler_params=_params(("arbitrary",)),
    )(x, mixg, mixf, tgt, wo, n2w, wg_t, wu_t, wd, fw)


def _wgrads(a_list, b, name):
    t, n = b.shape
    ms = [a.shape[1] for a in a_list]
    bms = [256 if m % 256 == 0 else LANE for m in ms]
    nbs = [m // bm for m, bm in zip(ms, bms)]
    k = len(a_list)
    cast = b.dtype != jnp.dtype(MXU)

    def body(*refs):
        a_refs, b_ref, o_refs = refs[:k], refs[k], refs[k + 1:2 * k + 1]
        i = pl.program_id(0)
        if cast:
            @pl.when(i == 0)
            def _():
                refs[-1][...] = b_ref[...].astype(MXU)
        for a_ref, o_ref, nb in zip(a_refs, o_refs, nbs):
            @pl.when(i < nb)
            def _():
                o_ref[...] = _dot_tn(a_ref[...], refs[-1][...] if cast else b_ref[...]).astype(o_ref.dtype)

    def clamp(nb):
        return lambda i: jnp.minimum(i, nb - 1)

    return pl.pallas_call(
        body, name=name, grid=(max(nbs),),
        in_specs=[pl.BlockSpec((t, bm), lambda i, c=clamp(nb): (0, c(i))) for bm, nb in zip(bms, nbs)] + [_resident((t, n))],
        out_specs=[pl.BlockSpec((bm, n), lambda i, c=clamp(nb): (c(i), 0)) for bm, nb in zip(bms, nbs)],
        out_shape=[jax.ShapeDtypeStruct((m, n), WIRE) for m in ms],
        scratch_shapes=[pltpu.VMEM((t, n), MXU)] if cast else [],
        compiler_params=_params(("arbitrary",)),
    )(*a_list, b)


def _merge_dw_in(d_gdn, d_z, d_fox, d_fg, d_small, shards=False):
    pieces = [d_gdn, d_z, d_small[:2 * HEADS], d_fox, d_fg, d_small[2 * HEADS:3 * HEADS]]
    if not shards:
        return jnp.concatenate(pieces, axis=0)
    n = sum(p.shape[0] for p in pieces) // N_DEV
    out = []
    for dev in range(N_DEV):
        parts, first = [], 0
        for p in pieces:
            lo, hi = max(dev * n, first), min((dev + 1) * n, first + p.shape[0])
            if lo < hi:
                parts.append(p[lo - first:hi - first])
            first += p.shape[0]
        out.append(jnp.concatenate(parts, axis=0))
    return jnp.stack(out)


def _lanes(*pieces):
    v = jnp.concatenate([p.reshape(-1).astype(F32) for p in pieces])
    return jnp.pad(v, (0, LANE - v.shape[0])).reshape(1, LANE)


def _vector_params(p):
    d = p["norm1_w"].size
    gparams = jnp.concatenate([_lanes(jnp.zeros(HEADS), p["gdn_dt_bias"], p["fox_f_bias"]),
                               _lanes(jnp.zeros(HEADS), p["gdn_A_log"]), jnp.zeros((6, LANE), F32)])
    fox_nw = jnp.stack([jnp.tile(p["fox_q_norm_w"].reshape(-1), 2), jnp.tile(p["fox_k_norm_w"].reshape(-1), 2),
                        jnp.ones((LANE,), F32)])
    return dict(n1w=p["norm1_w"].reshape(1, d), n2w=p["norm2_w"].reshape(1, d), fw=p["final_norm_w"].reshape(1, d),
                onw=p["gdn_out_norm_w"].reshape(1, DH), gparams=gparams, fox_nw=fox_nw)


def _mixer_forward(x, vp, w_t, ws_t, conv_w, carry=None, carry_first=None):
    h1, pm, ps, *first = _inproj(x, vp["n1w"], w_t, ws_t, carry_first)
    gates, run, tot, run_t = _gates(ps, vp["gparams"])
    fqkv = _fox_prep(pm, vp["fox_nw"])
    o_fox, lse, *carried = _fox_fwd(fqkv, run, run_t, carry)
    if carry:
        pm, o_fox = lax.optimization_barrier((pm, o_fox))
    mixf = _fox_post(o_fox, pm)
    gqkv = _gdn_prep(pm, conv_w)
    o_gdn, states, inv = _gdn_fwd(gqkv, gates, run, tot, run_t)
    mixg = _gdn_post(o_gdn, pm, vp["onw"])
    return dict(h1=h1, pm=pm, ps=ps, gates=gates, run=run, tot=tot, run_t=run_t, gqkv=gqkv, o_gdn=o_gdn, states=states,
                inv=inv, mixg=mixg, fqkv=fqkv, o_fox=o_fox, lse=lse, mixf=mixf, carried=carried[0] if carried else None,
                carried_first=first[0] if first else None)


def _mixer_backward(x, vp, w_t, ws_t, conv_w, f, dx2, dmixg, dmixf, carry=None, scatter_own=False):
    pm = f["pm"]
    do_fox, dfg = _fox_post_bwd(f["o_fox"], pm, dmixf)
    dfqkv, dcol_f, drow_f, *carried = _fox_bwd(f["fqkv"], f["run"], f["run_t"], f["o_fox"], f["lse"], do_fox, carry)
    dfox, dfnw = _fox_prep_bwd(pm, vp["fox_nw"], dfqkv)
    do_gdn, dz, donw = _gdn_post_bwd(f["o_gdn"], pm, vp["onw"], dmixg)
    dgqkv, dcol_g, dtot_g, drow_g = _gdn_bwd(f["gqkv"], f["gates"], f["run"], f["tot"], f["run_t"], f["inv"], f["states"], do_gdn)
    dgdn, dconv = _gdn_prep_bwd(pm, conv_w, dgqkv)
    dps, gsum = _gates_bwd(f["ps"], vp["gparams"], dcol_g, dtot_g, dcol_f, drow_g, drow_f)
    dw_pieces = _wgrads([dgdn, dz, dfox, dfg, dps], f["h1"], "dw_in")
    dw_in = _merge_dw_in(*dw_pieces)
    own = ("scatter", [_merge_dw_in(*dw_pieces, shards=True), _cut(dconv, 1)]) if scatter_own else None
    grad_x, dn1w, *sent = _inproj_bwd(x, vp["n1w"], dx2, dgdn, dz, dfox, dfg, dps, w_t, ws_t, own)
    small = dict(dn1w=dn1w, gsum=gsum, donw=donw, dfnw=dfnw)
    return (grad_x, dw_in, dconv, small, *carried, *sent)


VECTORS = ("norm1_w", "norm2_w", "final_norm_w", "gdn_A_log", "gdn_dt_bias", "gdn_out_norm_w", "fox_f_bias",
           "fox_q_norm_w", "fox_k_norm_w")
VEC_ROWS = 16
LOSS_ROW = len(VECTORS)


def _pack_vectors(dn1w, dn2w, dfw, gsum, donw, dfnw, loss):
    d = dn1w.shape[1]

    def body(n1_ref, n2_ref, fw_ref, gs_ref, on_ref, fn_ref, loss_ref, o_ref):
        o_ref[...] = jnp.zeros_like(o_ref)
        o_ref[0:1, :] = n1_ref[...]
        o_ref[1:2, :] = n2_ref[...]
        o_ref[2:3, :] = fw_ref[...]
        o_ref[3:4, 0:HEADS] = gs_ref[0:1, 0:HEADS]
        o_ref[4:5, 0:HEADS] = gs_ref[1:2, 0:HEADS]
        o_ref[5:6, 0:DH] = on_ref[...]
        o_ref[6:7, 0:HEADS] = gs_ref[2:3, 0:HEADS]
        for kind in range(2):
            v = fn_ref[kind, 0]
            for j in range(1, fn_ref.shape[1]):
                v = v + fn_ref[kind, j]
            o_ref[7 + kind:8 + kind, 0:DH] = v[:, :DH] + v[:, DH:]
        o_ref[LOSS_ROW:LOSS_ROW + 1, 0:1] = loss_ref[...]

    return pl.pallas_call(body, name="pack_vectors", out_shape=jax.ShapeDtypeStruct((VEC_ROWS, d), F32),
                          compiler_params=_params())(dn1w, dn2w, dfw, gsum, donw, dfnw, loss)


def _late_grads(f, h2, act, dgate, dup, dx3, dx2):
    dw_gate, dw_up = _wgrads([dgate, dup], h2, "dw_gate_up")
    (dw_down,) = _wgrads([act], dx3, "dw_down")
    return {"w_out": jnp.concatenate(_wgrads([f["mixg"], f["mixf"]], dx2, "dw_out"), axis=0),
            "w_ffn_gate": dw_gate, "w_ffn_up": dw_up, "w_ffn_down": dw_down}


def _local_step(x, tgt, p, w_in_t, conv_w, wo, wg_t, wu_t, wd):
    vp = _vector_params(p)
    ws_t = _small_rows(w_in_t)
    f = _mixer_forward(x, vp, w_in_t, ws_t, conv_w)
    (h2, act, dgate, dup, dx3, dx2, dmixg, dmixf, dn2w, dfw, loss) = _tail(
        x, f["mixg"], f["mixf"], tgt, wo, vp["n2w"], wg_t, wu_t, wd, vp["fw"])
    grad_x, dw_in, dconv, small = _mixer_backward(x, vp, w_in_t, ws_t, conv_w, f, dx2, dmixg, dmixf)
    grads = {"w_in": dw_in, "gdn_conv_w": dconv, **_late_grads(f, h2, act, dgate, dup, dx3, dx2)}
    vec = _pack_vectors(small["dn1w"], dn2w, dfw, small["gsum"], small["donw"], small["dfnw"], loss)
    return grad_x[0], grads, vec


def _my_place():
    return lax.axis_index("x"), lax.axis_index("y"), lax.axis_index("c")


def _peers():
    x, y, c = _my_place()
    peers = []
    for k in range(1, N_DEV):
        px = 1 - x if k & 4 else x
        py = 1 - y if k & 2 else y
        pc = 1 - c if k & 1 else c
        peers.append(((px, py, pc), 4 * px + 2 * py + pc))
    return 4 * x + 2 * y + c, peers


def _spread_copies(kind, srcs, lands, send_sems, recv_sems):
    me, peers = _peers()
    kinds = [kind] * len(srcs) if isinstance(kind, str) else kind
    remote, local = [], []
    for i, (kd, src, land) in enumerate(zip(kinds, srcs, lands)):
        for k, (dev, idx) in enumerate(peers):
            remote.append(pltpu.make_async_remote_copy(
                src_ref=src if kd == "gather" else src.at[idx], dst_ref=land.at[me],
                send_sem=send_sems.at[i * (N_DEV - 1) + k], recv_sem=recv_sems.at[i * (N_DEV - 1) + k],
                device_id=dev, device_id_type=MESH))
        local.append((src if kd == "gather" else src.at[me], land.at[me]))
    return remote, local


def _gather_two_level(arrays, name):
    n = len(arrays)

    def body(*refs):
        srcs, lands = refs[:n], refs[n:2 * n]
        send_sems, recv_sems, local_sems = refs[2 * n:]
        x, y, c = _my_place()
        me, sibling = (x, y, c), (x, y, 1 - c)
        chips = [(1 - x, y), (x, 1 - y), (1 - x, 1 - y)]

        def index(p):
            return 4 * p[0] + 2 * p[1] + p[2]

        def copy(i, k, block, to, src=None):
            blk = lands[i].at[index(block)]
            return pltpu.make_async_remote_copy(
                src_ref=blk if src is None else src, dst_ref=blk, send_sem=send_sems.at[7 * i + k],
                recv_sem=recv_sems.at[7 * i + k], device_id=to, device_id_type=MESH)

        mine = [pltpu.make_async_copy(srcs[i], lands[i].at[index(me)], local_sems.at[i]) for i in range(n)]
        for cp in mine:
            cp.start()
        first = []
        for i in range(n):
            first.append(copy(i, 0, me, sibling, src=srcs[i]))
            first += [copy(i, 1 + j, me, (*chip, c), src=srcs[i]) for j, chip in enumerate(chips)]
        for cp in first:
            cp.start()
        passed = []
        for i in range(n):
            for j, chip in enumerate(chips):
                copy(i, 1 + j, (*chip, c), me).wait_recv()
                passed.append(copy(i, 4 + j, (*chip, c), sibling))
                passed[-1].start()
        for i in range(n):
            copy(i, 0, sibling, me).wait_recv()
            for j, chip in enumerate(chips):
                copy(i, 4 + j, (*chip, 1 - c), me).wait_recv()
        for cp in first + passed:
            cp.wait_send()
        for cp in mine:
            cp.wait()

    return pl.pallas_call(
        body, name=name,
        out_shape=[jax.ShapeDtypeStruct((N_DEV,) + a.shape, a.dtype) for a in arrays],
        in_specs=[pl.BlockSpec(memory_space=pl.ANY)] * n, out_specs=[pl.BlockSpec(memory_space=pl.ANY)] * n,
        scratch_shapes=[pltpu.SemaphoreType.DMA((7 * n,)), pltpu.SemaphoreType.DMA((7 * n,)),
                        pltpu.SemaphoreType.DMA((n,))],
    )(*arrays)


def _land_shape(kind, a):
    return (N_DEV,) + a.shape if kind == "gather" else a.shape


HBM = pl.BlockSpec(memory_space=pltpu.HBM)
SEM = pl.BlockSpec(memory_space=pltpu.SEMAPHORE)


def _hbm(a):
    return pltpu.with_memory_space_constraint(a, pltpu.HBM)


def _carry_operands(kind, arrays):
    n = len(arrays)
    kinds = [kind] * n if isinstance(kind, str) else kind
    lands = [lax.empty(_land_shape(kd, a), a.dtype) for kd, a in zip(kinds, arrays)]
    sems = [pltpu.SemaphoreType.DMA((n * (N_DEV - 1),))] * 2
    return ([_hbm(a) for a in list(arrays) + lands], [HBM] * (2 * n),
            sems + [pltpu.HBM(a.shape, a.dtype) for a in list(arrays) + lands], [SEM] * 2 + [HBM] * (2 * n),
            pltpu.SemaphoreType.DMA((n,)))


def _carry_step(kind, in_refs, send_sems, recv_sems, local_sems, step, n_steps):
    n = len(in_refs) // 2
    remote, local = _spread_copies(kind, in_refs[:n], in_refs[n:], send_sems, recv_sems)
    copies = [pltpu.make_async_copy(s, d, local_sems.at[i]) for i, (s, d) in enumerate(local)]
    per_step = -(-len(remote) // n_steps)
    for s in range(-(-len(remote) // per_step)):
        @pl.when(step == s)
        def _():
            for cp in remote[s * per_step:(s + 1) * per_step]:
                cp.start()
            if s == 0:
                for cp in copies:
                    cp.start()

    @pl.when(step == n_steps - 1)
    def _():
        for cp in copies:
            cp.wait()


def _carry_state(extra_out):
    n = (len(extra_out) - 2) // 2
    return list(extra_out[2:2 + n]), list(extra_out[2 + n:]), extra_out[0], extra_out[1]


def _spread_start(arrays, kind, name):
    n = len(arrays)

    def body(*refs):
        srcs, lands = refs[:n], refs[n:2 * n]
        send_sems, recv_sems = refs[2 * n], refs[2 * n + 1]
        token = refs[4 * n + 2]
        local_sems = refs[4 * n + 3]
        remote, local = _spread_copies(kind, srcs, lands, send_sems, recv_sems)
        for cp in remote:
            cp.start()
        copies = [pltpu.make_async_copy(s, d, local_sems.at[i]) for i, (s, d) in enumerate(local)]
        for cp in copies:
            cp.start()
        for cp in copies:
            cp.wait()
        token[...] = jnp.zeros_like(token)

    sems = (pltpu.SemaphoreType.DMA((n * (N_DEV - 1),)),) * 2
    kinds = [kind] * n if isinstance(kind, str) else kind
    lands = [lax.empty(_land_shape(kd, a), a.dtype) for kd, a in zip(kinds, arrays)]
    out = pl.pallas_call(
        body, name=name,
        out_shape=sems + tuple(pltpu.HBM(a.shape, a.dtype) for a in list(arrays) + lands)
        + (jax.ShapeDtypeStruct((8, LANE), F32),),
        in_specs=[HBM] * (2 * n), out_specs=tuple([SEM] * 2 + [HBM] * (2 * n) + [pl.BlockSpec(memory_space=pltpu.VMEM)]),
        input_output_aliases={j: 2 + j for j in range(2 * n)},
        scratch_shapes=[pltpu.SemaphoreType.DMA((n,))],
        compiler_params=pltpu.CompilerParams(has_side_effects=pltpu.SideEffectType.DATAFLOW_SIDE_EFFECTING),
    )(*[_hbm(a) for a in arrays], *[_hbm(a) for a in lands])
    return (list(out[2:2 + n]), list(out[2 + n:2 + 2 * n]), out[0], out[1]), out[-1]


def _spread_wait(state, kind, after, name):
    srcs, lands, send_sems, recv_sems = state
    n = len(srcs)
    after = list(after) if isinstance(after, (list, tuple)) else [after]

    def body(*refs):
        remote, _ = _spread_copies(kind, refs[:n], refs[n:2 * n], refs[2 * n], refs[2 * n + 1])
        for cp in remote:
            cp.wait_send()
        for cp in remote:
            cp.wait_recv()

    out = pl.pallas_call(
        body, name=name,
        out_shape=tuple(pltpu.HBM(a.shape, a.dtype) for a in srcs + lands),
        in_specs=[HBM] * (2 * n) + [SEM, SEM] + [pl.BlockSpec(memory_space=pl.ANY)] * len(after),
        out_specs=tuple([HBM] * (2 * n)),
        input_output_aliases={j: j for j in range(2 * n)},
        compiler_params=pltpu.CompilerParams(has_side_effects=pltpu.SideEffectType.DATAFLOW_SIDE_EFFECTING),
    )(*srcs, *lands, send_sems, recv_sems, *after)
    return list(out[n:])


ADAM_ROWS = 128
ADAM_COLS = 256


def _adam_math(g, w, m, v):
    nm = ADAM_B1 * m + (1.0 - ADAM_B1) * g
    nv = ADAM_B2 * v + (1.0 - ADAM_B2) * (g * g)
    m_hat = nm / (1.0 - ADAM_B1 ** ADAM_STEP)
    v_hat = nv / (1.0 - ADAM_B2 ** ADAM_STEP)
    return -ADAM_LR * (m_hat / (jnp.sqrt(v_hat) + ADAM_EPS) + ADAM_WD * w), nm, nv


def _sum_parts(p_ref):
    g = p_ref[0].astype(F32)
    for s in range(1, N_DEV):
        g = g + p_ref[s].astype(F32)
    return g


def _adam_matrix(parts, w, m, v, name):
    _, r, c = w.shape
    rb = ADAM_ROWS if r % ADAM_ROWS == 0 else r
    cb = ADAM_COLS if (rb == r and c % ADAM_COLS == 0) else c

    def body(p_ref, w_ref, m_ref, v_ref, g_ref, d_ref, nm_ref, nv_ref):
        g = _sum_parts(p_ref)
        g_ref[0] = g
        d_ref[0], nm_ref[0], nv_ref[0] = _adam_math(g, w_ref[0], m_ref[0], v_ref[0])

    blk = pl.BlockSpec((1, rb, cb), lambda i, j: (0, i, j))
    return pl.pallas_call(
        body, name=name, grid=(r // rb, c // cb),
        in_specs=[pl.BlockSpec((N_DEV, rb, cb), lambda i, j: (0, i, j)), blk, blk, blk],
        out_specs=[blk] * 4, out_shape=[jax.ShapeDtypeStruct(w.shape, F32)] * 4,
        compiler_params=_params(("arbitrary", "arbitrary")),
    )(parts, w, m, v)


def _copy_rows(a, name):
    _, r, c = a.shape
    rb = min(TB, r)

    def body(a_ref, o_ref):
        o_ref[...] = a_ref[...]

    blk = pl.BlockSpec((1, rb, c), lambda i: (0, i, 0))
    return pl.pallas_call(body, name=name, grid=(r // rb,), in_specs=[blk], out_specs=blk,
                          out_shape=jax.ShapeDtypeStruct(a.shape, a.dtype), compiler_params=_params(("arbitrary",)))(a)


def _adam_vectors(parts, ws, ms, vs):
    nv = len(ws)

    def body(*refs):
        p_ref = refs[0]
        w_refs, m_refs, v_refs = refs[1:1 + nv], refs[1 + nv:1 + 2 * nv], refs[1 + 2 * nv:1 + 3 * nv]
        outs = refs[1 + 3 * nv:]
        g_all = _sum_parts(p_ref)
        for i in range(nv):
            n = w_refs[i].shape[1]
            g = g_all[i:i + 1, 0:n]
            d, nm, nvv = _adam_math(g, w_refs[i][...], m_refs[i][...], v_refs[i][...])
            outs[i][...] = g
            outs[nv + i][...] = d
            outs[2 * nv + i][...] = nm
            outs[3 * nv + i][...] = nvv
        outs[4 * nv][...] = g_all[LOSS_ROW:LOSS_ROW + 1, 0:1]

    shapes = [jax.ShapeDtypeStruct(a.shape, F32) for a in ws]
    out = pl.pallas_call(body, name="adam_vectors", out_shape=shapes * 4 + [jax.ShapeDtypeStruct((1, 1), F32)],
                         compiler_params=_params())(parts, *ws, *ms, *vs)
    return out[:nv], out[nv:2 * nv], out[2 * nv:3 * nv], out[3 * nv:4 * nv], out[4 * nv]


MATRICES = (("w_in", 1), ("gdn_conv_w", 1), ("w_out", 0), ("w_ffn_gate", 1), ("w_ffn_up", 1), ("w_ffn_down", 0))
TRANSPOSED = ("w_in", "w_ffn_gate", "w_ffn_up")
WEIGHTS = ("norm1_w", "w_in", "gdn_conv_w", "gdn_A_log", "gdn_dt_bias", "gdn_out_norm_w", "fox_f_bias", "fox_q_norm_w",
           "fox_k_norm_w", "w_out", "norm2_w", "w_ffn_gate", "w_ffn_up", "w_ffn_down", "final_norm_w")


def _join(blocks, axis):
    _, r, c = blocks.shape
    if axis == 0:
        return blocks.reshape(N_DEV * r, c)
    return blocks.transpose(1, 0, 2).reshape(r, N_DEV * c)


def _cut(full, axis):
    r, c = full.shape
    if axis == 0:
        return full.reshape(N_DEV, r // N_DEV, c)
    return full.reshape(r, N_DEV, c // N_DEV).transpose(1, 0, 2)


def kernel(x, norm1_w, w_in, gdn_conv_w, gdn_A_log, gdn_dt_bias, gdn_out_norm_w, fox_f_bias, fox_q_norm_w, fox_k_norm_w, w_out, norm2_w, w_ffn_gate, w_ffn_up, w_ffn_down, final_norm_w, loss_target, m_norm1_w, m_w_in, m_gdn_conv_w, m_gdn_A_log, m_gdn_dt_bias, m_gdn_out_norm_w, m_fox_f_bias, m_fox_q_norm_w, m_fox_k_norm_w, m_w_out, m_norm2_w, m_w_ffn_gate, m_w_ffn_up, m_w_ffn_down, m_final_norm_w, v_norm1_w, v_w_in, v_gdn_conv_w, v_gdn_A_log, v_gdn_dt_bias, v_gdn_out_norm_w, v_fox_f_bias, v_fox_q_norm_w, v_fox_k_norm_w, v_w_out, v_norm2_w, v_w_ffn_gate, v_w_ffn_up, v_w_ffn_down, v_final_norm_w):
    w = dict(norm1_w=norm1_w, w_in=w_in, gdn_conv_w=gdn_conv_w, gdn_A_log=gdn_A_log, gdn_dt_bias=gdn_dt_bias,
             gdn_out_norm_w=gdn_out_norm_w, fox_f_bias=fox_f_bias, fox_q_norm_w=fox_q_norm_w, fox_k_norm_w=fox_k_norm_w,
             w_out=w_out, norm2_w=norm2_w, w_ffn_gate=w_ffn_gate, w_ffn_up=w_ffn_up, w_ffn_down=w_ffn_down,
             final_norm_w=final_norm_w)
    m = dict(norm1_w=m_norm1_w, w_in=m_w_in, gdn_conv_w=m_gdn_conv_w, gdn_A_log=m_gdn_A_log, gdn_dt_bias=m_gdn_dt_bias,
             gdn_out_norm_w=m_gdn_out_norm_w, fox_f_bias=m_fox_f_bias, fox_q_norm_w=m_fox_q_norm_w,
             fox_k_norm_w=m_fox_k_norm_w, w_out=m_w_out, norm2_w=m_norm2_w, w_ffn_gate=m_w_ffn_gate,
             w_ffn_up=m_w_ffn_up, w_ffn_down=m_w_ffn_down, final_norm_w=m_final_norm_w)
    v = dict(norm1_w=v_norm1_w, w_in=v_w_in, gdn_conv_w=v_gdn_conv_w, gdn_A_log=v_gdn_A_log, gdn_dt_bias=v_gdn_dt_bias,
             gdn_out_norm_w=v_gdn_out_norm_w, fox_f_bias=v_fox_f_bias, fox_q_norm_w=v_fox_q_norm_w,
             fox_k_norm_w=v_fox_k_norm_w, w_out=v_w_out, norm2_w=v_norm2_w, w_ffn_gate=v_w_ffn_gate,
             w_ffn_up=v_w_ffn_up, w_ffn_down=v_w_ffn_down, final_norm_w=v_final_norm_w)
    late = ("w_out", "w_ffn_gate", "w_ffn_up", "w_ffn_down")
    xs, tgt = x[0], loss_target[0]
    vp = _vector_params({n: w[n] for n in VECTORS})

    def rows_of(d, n):
        return d[n].transpose(0, 2, 1) if n in TRANSPOSED else d[n]

    wr, mr, vr = ({n: rows_of(d, n) for n, _ in MATRICES} for d in (w, m, v))

    w_in_blocks, conv_blocks = _gather_two_level([wr["w_in"][0].astype(WIRE), w["gdn_conv_w"][0]], "gather_in")
    w_t = _join(w_in_blocks, 0)
    conv_w = _join(conv_blocks, 1)
    early = ("w_ffn_gate",)
    rest = tuple(n for n in late if n not in early)
    f = _mixer_forward(xs, vp, w_t, _small_rows(w_t), conv_w, ("gather", [wr[n][0].astype(WIRE) for n in rest]),
                       ("gather", [wr[n][0].astype(WIRE) for n in early]))
    full = {n: _join(b, 0) for n, b in zip(early, _spread_wait(f["carried_first"], "gather", f["mixg"], "gather_early_wait"))}
    full.update({n: _join(b, 0)
                 for n, b in zip(rest, _spread_wait(f["carried"], "gather", full[early[0]], "gather_late_wait"))})

    (h2, act, dgate, dup, dx3, dx2, dmixg, dmixf, dn2w, dfw, loss) = _tail(
        xs, f["mixg"], f["mixf"], tgt, full["w_out"], vp["n2w"], full["w_ffn_gate"], full["w_ffn_up"],
        full["w_ffn_down"], vp["fw"])
    dlate = _late_grads(f, h2, act, dgate, dup, dx3, dx2)

    grad_x, dw_in, dconv, small, state_grads, state_own = _mixer_backward(
        xs, vp, w_t, _small_rows(w_t), conv_w, f, dx2, dmixg, dmixf, ("scatter", [_cut(dlate[n], 0) for n in late]),
        scatter_own=True)
    vec = _pack_vectors(small["dn1w"], dn2w, dfw, small["gsum"], small["donw"], small["dfnw"], loss)

    state_vec, token = _spread_start([vec], "gather", "vectors_start")
    parts = dict(zip(late, _spread_wait(state_grads, "scatter", token, "grads_late_wait")))
    results = [{}, {}, {}, {}]

    updated = {}

    def update(n):
        out = _adam_matrix(parts[n], wr[n], mr[n], vr[n], "adam_" + n)
        updated[n] = out[0]
        for d, a in zip(results, out):
            d[n] = a.transpose(0, 2, 1) if n in TRANSPOSED else a

    for n in late:
        update(n)
    grad_x = _copy_rows(grad_x, "grad_x_result")
    ready = [updated[n] for n in late] + [d["w_in"] for d in (wr, mr, vr)] + [grad_x]
    (parts_vec,) = _spread_wait(state_vec, "gather", ready, "vectors_wait")
    row = lambda a: a.reshape(1, -1)
    *vec_out, total_loss = _adam_vectors(parts_vec, [row(w[n]) for n in VECTORS], [row(m[n]) for n in VECTORS],
                                         [row(v[n]) for n in VECTORS])
    parts["w_in"], parts["gdn_conv_w"] = _spread_wait(state_own, "scatter", parts_vec, "own_wait")
    update("w_in")
    update("gdn_conv_w")
    for d, arrs in zip(results, vec_out):
        for n, a in zip(VECTORS, arrs):
            d[n] = a.reshape(w[n].shape)
    return (total_loss[0, 0], grad_x, *[d[n] for d in results for n in WEIGHTS])
```

```python
import functools

import jax
import jax.numpy as jnp
from jax import lax
from jax.experimental import pallas as pl
from jax.experimental.pallas import tpu as pltpu

F32 = jnp.float32
MXU = jnp.bfloat16
WIRE = jnp.bfloat16
HI = lax.Precision.HIGHEST
EPS = 1e-6

N_DEV = 8
HEADS = 8
DH = 64
WIDTH = HEADS * DH
CHUNK = 64
LANE = 128
ROW_ALIGN = 16
TB = 256
QB = 256
VMEM_LIMIT = 60 * 1024 * 1024

ADAM_LR = 0.001
ADAM_B1 = 0.9
ADAM_B2 = 0.999
ADAM_EPS = 1e-08
ADAM_WD = 0.01
ADAM_STEP = 10

MESH = pl.DeviceIdType.MESH


def _params(sem=None):
    return pltpu.CompilerParams(dimension_semantics=sem, vmem_limit_bytes=VMEM_LIMIT)


def _resident(shape):
    n = len(shape)
    return pl.BlockSpec(shape, lambda *_: (0,) * n, pipeline_mode=pl.Buffered(1))


def _dot(a, b):
    return jnp.dot(a.astype(MXU), b.astype(MXU), preferred_element_type=F32)


def _dot_nt(a, b):
    return lax.dot_general(a.astype(MXU), b.astype(MXU), (((1,), (1,)), ((), ())), preferred_element_type=F32)


def _dot_tn(a, b):
    return lax.dot_general(a.astype(MXU), b.astype(MXU), (((0,), (0,)), ((), ())), preferred_element_type=F32)


def _hdot(a, b):
    return jnp.dot(a, b, precision=HI, preferred_element_type=F32)


def _hdot_nt(a, b):
    return lax.dot_general(a, b, (((1,), (1,)), ((), ())), precision=HI, preferred_element_type=F32)


def _hdot_tn(a, b):
    return lax.dot_general(a, b, (((0,), (0,)), ((), ())), precision=HI, preferred_element_type=F32)


def _sigmoid(x):
    return 0.5 * jnp.tanh(0.5 * x) + 0.5


def _softplus(x):
    return jnp.maximum(x, 0.0) + jnp.log(1.0 + jnp.exp(-jnp.abs(x)))


def _head_sum_matrix():
    ri = lax.broadcasted_iota(jnp.int32, (LANE, LANE), 0) // DH
    ci = lax.broadcasted_iota(jnp.int32, (LANE, LANE), 1) // DH
    return (ri == ci).astype(F32)


def _group_sum(a, ones_matrix):
    hi = a.astype(jnp.bfloat16)
    lo = (a - hi.astype(F32)).astype(jnp.bfloat16)
    m = ones_matrix.astype(jnp.bfloat16)
    return jnp.dot(hi, m, preferred_element_type=F32) + jnp.dot(lo, m, preferred_element_type=F32)


def _shift_down(x, s):
    return pltpu.roll(x, s, 0)


def _shift_up(x, s):
    return pltpu.roll(x, x.shape[0] - s, 0)


ROWS_A = 4 * WIDTH
ROWS_B = ROWS_A + 2 * HEADS
ROWS_C = ROWS_B + 4 * WIDTH


def _small_rows(w_t):
    return jnp.concatenate([w_t[ROWS_A:ROWS_B], w_t[ROWS_C:], jnp.zeros((LANE - 3 * HEADS, w_t.shape[1]), w_t.dtype)])


def _inproj(x, n1w, w_t, ws_t, carry=None):
    t, d = x.shape
    tb = min(TB, t)
    c_in, c_in_specs, c_out_shape, c_out_specs, c_sem = _carry_operands(*carry) if carry else ([], [], [], [], None)
    n_c = len(c_in)

    def body(*refs):
        x_ref, nw_ref, wt_ref, ws_ref = refs[:4]
        h_ref, pm_ref, ps_ref = refs[4 + n_c:7 + n_c]
        if carry:
            _carry_step(carry[0], refs[4:4 + n_c], refs[7 + n_c], refs[8 + n_c], refs[-1], pl.program_id(0), t // tb)
        xv = x_ref[...]
        r = lax.rsqrt(jnp.mean(xv * xv, axis=-1, keepdims=True) + EPS)
        h = (xv * r * nw_ref[...]).astype(MXU)
        h_ref[...] = h
        pm_ref[:, 0:ROWS_A] = _dot_nt(h, wt_ref[0:ROWS_A, :])
        pm_ref[:, ROWS_A:2 * ROWS_A] = _dot_nt(h, wt_ref[ROWS_B:ROWS_C, :])
        ps_ref[...] = _dot_nt(h, ws_ref[...])

    out = pl.pallas_call(
        body, name="inproj", grid=(t // tb,),
        in_specs=[pl.BlockSpec((tb, d), lambda i: (i, 0)), _resident((1, d)), _resident(w_t.shape), _resident((LANE, d))]
        + c_in_specs,
        out_specs=[pl.BlockSpec((tb, d), lambda i: (i, 0)), pl.BlockSpec((tb, 2 * ROWS_A), lambda i: (i, 0)),
                   pl.BlockSpec((tb, LANE), lambda i: (i, 0))] + c_out_specs,
        out_shape=[jax.ShapeDtypeStruct((t, d), MXU), jax.ShapeDtypeStruct((t, 2 * ROWS_A), F32),
                   jax.ShapeDtypeStruct((t, LANE), F32)] + c_out_shape,
        input_output_aliases={4 + j: 5 + j for j in range(n_c)},
        scratch_shapes=[c_sem] if carry else [],
        compiler_params=_params(("arbitrary",)),
    )(x, n1w, w_t, ws_t, *c_in)
    return (out[0], out[1], out[2], _carry_state(out[3:])) if carry else tuple(out)


def _inproj_bwd(x, n1w, dx2, dgdn, dz, dfox, dfg, dps, w_t, ws_t, carry=None):
    t, d = x.shape
    tb = min(TB, t)
    w3 = 3 * WIDTH
    c_in, c_in_specs, c_out_shape, c_out_specs, c_sem = _carry_operands(*carry) if carry else ([], [], [], [], None)
    n_c = len(c_in)

    def body(*refs):
        x_ref, nw_ref, dx2_ref, dgdn_ref, dz_ref, dfox_ref, dfg_ref, dps_ref, wm_ref, ws_ref = refs[:10]
        gx_ref, dnw_ref = refs[10 + n_c:12 + n_c]
        if carry:
            _carry_step(carry[0], refs[10:10 + n_c], refs[12 + n_c], refs[13 + n_c], refs[-1], pl.program_id(0), t // tb)
        dh = _dot(dgdn_ref[...], wm_ref[0:w3, :])
        dh += _dot(dz_ref[...], wm_ref[w3:ROWS_A, :])
        dh += _dot(dfox_ref[...], wm_ref[ROWS_B:ROWS_B + w3, :])
        dh += _dot(dfg_ref[...], wm_ref[ROWS_B + w3:ROWS_C, :])
        dh += _dot(dps_ref[...], ws_ref[...])
        xv = x_ref[...]
        r = lax.rsqrt(jnp.mean(xv * xv, axis=-1, keepdims=True) + EPS)
        xn = xv * r

        @pl.when(pl.program_id(0) == 0)
        def _():
            dnw_ref[...] = jnp.zeros_like(dnw_ref)

        dnw_ref[...] += jnp.sum(dh * xn, axis=0, keepdims=True)
        g = dh * nw_ref[...]
        gx_ref[0] = dx2_ref[...] + r * (g - xn * jnp.mean(g * xn, axis=-1, keepdims=True))

    def tok(n):
        return pl.BlockSpec((tb, n), lambda i: (i, 0))

    out = pl.pallas_call(
        body, name="inproj_bwd", grid=(t // tb,),
        in_specs=[tok(d), _resident((1, d)), tok(d), tok(w3), tok(WIDTH), tok(w3), tok(WIDTH), tok(LANE),
                  _resident(w_t.shape), _resident(ws_t.shape)] + c_in_specs,
        out_specs=[pl.BlockSpec((1, tb, d), lambda i: (0, i, 0)), pl.BlockSpec((1, d), lambda i: (0, 0))] + c_out_specs,
        out_shape=[jax.ShapeDtypeStruct((1, t, d), F32), jax.ShapeDtypeStruct((1, d), F32)] + c_out_shape,
        input_output_aliases={10 + j: 4 + j for j in range(n_c)},
        scratch_shapes=[c_sem] if carry else [],
        compiler_params=_params(("arbitrary",)),
    )(x, n1w, dx2, dgdn, dz, dfox, dfg, dps, w_t, ws_t, *c_in)
    return (out[0], out[1], _carry_state(out[2:])) if carry else tuple(out)


def _gate_lanes(shape):
    lane = lax.broadcasted_iota(jnp.int32, shape, 1)
    return lane < HEADS, (lane >= HEADS) & (lane < 2 * HEADS), (lane >= 2 * HEADS) & (lane < 3 * HEADS)


def _block_masks():
    ri = lax.broadcasted_iota(jnp.int32, (LANE, LANE), 0)
    ci = lax.broadcasted_iota(jnp.int32, (LANE, LANE), 1)
    same = (ri // CHUNK) == (ci // CHUNK)
    return ((ri >= ci).astype(F32), (ri <= ci).astype(F32), (same & (ri >= ci)).astype(F32),
            (same & (ri <= ci)).astype(F32), same.astype(F32))


def _gates(ps, gparams):
    t = ps.shape[0]
    nb = t // LANE

    def body(ps_ref, gp_ref, out_ref, run_ref, tot_ref, runt_ref):
        p = ps_ref[...]
        is_b, is_a, is_f = _gate_lanes(p.shape)
        z = p + gp_ref[0:1, :]
        neg_exp_a = -jnp.exp(gp_ref[1:2, :])
        glog = neg_exp_a * _softplus(z)
        logf = -_softplus(-z)
        out_ref[...] = jnp.where(is_b, _sigmoid(p), jnp.where(is_a, glog, jnp.where(is_f, logf, 0.0)))
        tril, _, tril_c, _, same_c = _block_masks()
        off = jnp.zeros((1, LANE), F32)
        for b in range(nb):
            rows = slice(b * LANE, (b + 1) * LANE)
            blk = out_ref[rows, :]
            ga = jnp.where(is_a[:LANE], blk, 0.0)
            fb = _hdot(tril, jnp.where(is_f[:LANE], blk, 0.0)) + off
            run = fb + _hdot(tril_c, ga)
            run_ref[rows, :] = run
            runt_ref[:, rows] = run.T
            tot_ref[rows, :] = _hdot(same_c, ga)
            off = fb[LANE - 1:LANE, :]

    return pl.pallas_call(
        body, name="gates",
        out_shape=[jax.ShapeDtypeStruct((t, LANE), F32)] * 3 + [jax.ShapeDtypeStruct((LANE, t), F32)],
        compiler_params=_params(),
    )(ps, gparams)


def _gates_bwd(ps, gparams, dcol_g, dtot_g, dcol_f, drow_g, drow_f):
    t = ps.shape[0]
    nb = t // LANE

    def body(ps_ref, gp_ref, dcg_ref, dtg_ref, dcf_ref, drg_ref, drf_ref, dps_ref, sums_ref, dl_ref, d0_ref, tr_ref):
        p = ps_ref[...]
        is_b, is_a, is_f = _gate_lanes(p.shape)
        _, triu, _, triu_c, same_c = _block_masks()
        tr_ref[...] = jnp.zeros_like(tr_ref)
        off = jnp.zeros((1, LANE), F32)
        for b in reversed(range(nb)):
            rows = slice(b * LANE, (b + 1) * LANE)
            tr_ref[HEADS:2 * HEADS, :] = drg_ref[:, rows]
            tr_ref[2 * HEADS:3 * HEADS, :] = drf_ref[:, rows]
            d = dcg_ref[rows, :] + dcf_ref[rows, :] + tr_ref[...].T
            d0_ref[rows, :] = d
            dlf = _hdot(triu, jnp.where(is_f[:LANE], d, 0.0)) + off
            dla = (_hdot(triu_c, jnp.where(is_a[:LANE], d, 0.0))
                   + _hdot(same_c, jnp.where(is_a[:LANE], dtg_ref[rows, :], 0.0)))
            dl_ref[rows, :] = dlf + dla
            off = dlf[0:1, :]
        z = p + gp_ref[0:1, :]
        neg_exp_a = -jnp.exp(gp_ref[1:2, :])
        sb = _sigmoid(p)
        glog = neg_exp_a * _softplus(z)
        dl = dl_ref[...]
        dp = jnp.where(is_b, d0_ref[...] * sb * (1.0 - sb),
                       jnp.where(is_a, dl * neg_exp_a * _sigmoid(z), jnp.where(is_f, dl * _sigmoid(-z), 0.0)))
        dps_ref[...] = dp
        s_a = jnp.sum(jnp.where(is_a, dl * glog, 0.0), axis=0, keepdims=True)
        s_p = jnp.sum(jnp.where(is_b, 0.0, dp), axis=0, keepdims=True)
        row = lax.broadcasted_iota(jnp.int32, (8, LANE), 0)
        from_a = pltpu.roll(jnp.where(row == 0, s_a, jnp.where(row == 1, s_p, 0.0)), LANE - HEADS, 1)
        from_f = pltpu.roll(jnp.where(row == 2, s_p, 0.0), LANE - 2 * HEADS, 1)
        lane = lax.broadcasted_iota(jnp.int32, (8, LANE), 1)
        sums_ref[...] = jnp.where(lane < HEADS, from_a + from_f, 0.0)

    return pl.pallas_call(
        body, name="gates_bwd",
        out_shape=[jax.ShapeDtypeStruct((t, LANE), F32), jax.ShapeDtypeStruct((8, LANE), F32)],
        scratch_shapes=[pltpu.VMEM((t, LANE), F32), pltpu.VMEM((t, LANE), F32), pltpu.VMEM((LANE, LANE), F32)],
        compiler_params=_params(),
    )(ps, gparams, dcol_g, dtot_g, dcol_f, drow_g, drow_f)


def _conv(xv, w):
    acc = w[3:4, :] * xv
    for s in range(1, 4):
        acc += w[3 - s:4 - s, :] * _shift_down(xv, s)
    return acc


PREP_ROWS = 256
HALO = 8
PREP_UNROLL = 8


def _tile_loop(n, tile, init):
    if n % PREP_UNROLL:
        return lax.fori_loop(0, n, tile, init)

    def trip(g, carry):
        for u in range(PREP_UNROLL):
            carry = tile(g * PREP_UNROLL + u, carry)
        return carry

    return lax.fori_loop(0, n // PREP_UNROLL, trip, init)


def _tile_rows(r):
    return pl.ds(pl.multiple_of(r * PREP_ROWS, PREP_ROWS), PREP_ROWS)


def _gdn_prep(pm, conv_w):
    t = pm.shape[0]
    nj = WIDTH // LANE
    win = PREP_ROWS + HALO

    def body(x_ref, w_ref, o_ref, xp_ref):
        kind = pl.program_id(0)
        xp_ref[0:HALO, :] = jnp.zeros((HALO, LANE), F32)
        xp_ref[HALO:, :] = x_ref[...]
        w = w_ref[...]
        hs = _head_sum_matrix()

        def tile(r, _, normed):
            xw = xp_ref[pl.ds(pl.multiple_of(r * PREP_ROWS, PREP_ROWS), win), :]
            acc = _conv(xw, w)[HALO:]
            out = acc * _sigmoid(acc)
            if normed:
                out = out * lax.rsqrt(_group_sum(out * out, hs) + EPS)
            o_ref[0, 0, _tile_rows(r), :] = out[:, :DH]
            o_ref[0, 1, _tile_rows(r), :] = out[:, DH:]
            return 0

        @pl.when(kind < 2)
        def _():
            _tile_loop(t // PREP_ROWS, functools.partial(tile, normed=True), 0)

        @pl.when(kind == 2)
        def _():
            _tile_loop(t // PREP_ROWS, functools.partial(tile, normed=False), 0)

    return pl.pallas_call(
        body, name="gdn_prep", grid=(3, nj),
        in_specs=[pl.BlockSpec((t, LANE), lambda i, j: (0, i * nj + j)),
                  pl.BlockSpec((4, LANE), lambda i, j: (0, i * nj + j))],
        out_specs=pl.BlockSpec((1, 2, t, DH), lambda i, j: (i, j, 0, 0)),
        out_shape=jax.ShapeDtypeStruct((3, HEADS, t, DH), F32),
        scratch_shapes=[pltpu.VMEM((t + HALO, LANE), F32)],
        compiler_params=_params(("arbitrary", "arbitrary")),
    )(pm, conv_w)


def _gdn_prep_bwd(pm, conv_w, dqkv):
    t = pm.shape[0]
    nj = WIDTH // LANE
    win = PREP_ROWS + 2 * HALO

    def body(x_ref, w_ref, d_ref, dx_ref, dw_ref, xp_ref, dp_ref):
        kind = pl.program_id(0)
        zeros = jnp.zeros((HALO, LANE), F32)
        for ref in (xp_ref, dp_ref):
            ref[0:HALO, :] = zeros
            ref[HALO + t:, :] = zeros
        xp_ref[HALO:HALO + t, :] = x_ref[...]
        dp_ref[HALO:HALO + t, 0:DH] = d_ref[0, 0]
        dp_ref[HALO:HALO + t, DH:] = d_ref[0, 1]
        w = w_ref[...]
        hs = _head_sum_matrix()
        rows = lax.broadcasted_iota(jnp.int32, (win, LANE), 0)
        in_tile = (rows >= HALO) & (rows < HALO + PREP_ROWS)

        def tile(r, dw, normed):
            start = pl.multiple_of(r * PREP_ROWS, PREP_ROWS)
            xw = xp_ref[pl.ds(start, win), :]
            dy = dp_ref[pl.ds(start, win), :]
            acc = _conv(xw, w)
            sg = _sigmoid(acc)
            if normed:
                y = acc * sg
                rn = lax.rsqrt(_group_sum(y * y, hs) + EPS)
                yn = y * rn
                dy = rn * (dy - yn * _group_sum(dy * yn, hs))
            dacc = dy * sg * (1.0 + acc * (1.0 - sg))
            dx = w[3:4, :] * dacc
            for s in range(1, 4):
                dx += w[3 - s:4 - s, :] * _shift_up(dacc, s)
            dx_ref[_tile_rows(r), :] = dx[HALO:HALO + PREP_ROWS].astype(dx_ref.dtype)
            dm = jnp.where(in_tile, dacc, 0.0)
            return tuple(dw[i] + jnp.sum(dm * (xw if i == 3 else _shift_down(xw, 3 - i)), axis=0, keepdims=True)
                         for i in range(4))

        def run(normed):
            dw = _tile_loop(t // PREP_ROWS, functools.partial(tile, normed=normed),
                            tuple(jnp.zeros((1, LANE), F32) for _ in range(4)))
            for i in range(4):
                dw_ref[i:i + 1, :] = dw[i]

        pl.when(kind < 2)(functools.partial(run, True))
        pl.when(kind == 2)(functools.partial(run, False))

    return pl.pallas_call(
        body, name="gdn_prep_bwd", grid=(3, nj),
        in_specs=[pl.BlockSpec((t, LANE), lambda i, j: (0, i * nj + j)),
                  pl.BlockSpec((4, LANE), lambda i, j: (0, i * nj + j)),
                  pl.BlockSpec((1, 2, t, DH), lambda i, j: (i, j, 0, 0))],
        out_specs=[pl.BlockSpec((t, LANE), lambda i, j: (0, i * nj + j)),
                   pl.BlockSpec((4, LANE), lambda i, j: (0, i * nj + j))],
        out_shape=[jax.ShapeDtypeStruct((t, 3 * WIDTH), MXU), jax.ShapeDtypeStruct((4, 3 * WIDTH), F32)],
        scratch_shapes=[pltpu.VMEM((t + 2 * HALO, LANE), F32), pltpu.VMEM((t + 2 * HALO, LANE), F32)],
        compiler_params=_params(("arbitrary", "arbitrary")),
    )(pm, conv_w, dqkv)


FOX_COL0 = 4 * WIDTH // LANE


def _fox_prep(pm, nw):
    t = pm.shape[0]
    nj = WIDTH // LANE

    def body(x_ref, w_ref, o_ref):
        kind = pl.program_id(0)
        hs = _head_sum_matrix()
        wk = w_ref[pl.ds(kind, 1), :]

        def tile(r, _, normed):
            out = x_ref[_tile_rows(r), :]
            if normed:
                out = out * lax.rsqrt(_group_sum(out * out, hs) * (1.0 / DH) + EPS) * wk
            o_ref[0, 0, _tile_rows(r), :] = out[:, :DH]
            o_ref[0, 1, _tile_rows(r), :] = out[:, DH:]
            return 0

        @pl.when(kind < 2)
        def _():
            _tile_loop(t // PREP_ROWS, functools.partial(tile, normed=True), 0)

        @pl.when(kind == 2)
        def _():
            _tile_loop(t // PREP_ROWS, functools.partial(tile, normed=False), 0)

    return pl.pallas_call(
        body, name="fox_prep", grid=(3, nj),
        in_specs=[pl.BlockSpec((t, LANE), lambda i, j: (0, FOX_COL0 + i * nj + j)),
                  pl.BlockSpec((3, LANE), lambda i, j: (0, 0))],
        out_specs=pl.BlockSpec((1, 2, t, DH), lambda i, j: (i, j, 0, 0)),
        out_shape=jax.ShapeDtypeStruct((3, HEADS, t, DH), F32),
        compiler_params=_params(("arbitrary", "arbitrary")),
    )(pm, nw)


def _fox_prep_bwd(pm, nw, dqkv):
    t = pm.shape[0]
    nj = WIDTH // LANE

    def body(x_ref, w_ref, d_ref, dx_ref, dw_ref):
        kind = pl.program_id(0)
        hs = _head_sum_matrix()
        wk = w_ref[pl.ds(kind, 1), :]

        def tile(r, dw):
            xv = x_ref[_tile_rows(r), :]
            rn = lax.rsqrt(_group_sum(xv * xv, hs) * (1.0 / DH) + EPS)
            xn = xv * rn
            d = jnp.concatenate([d_ref[0, 0, _tile_rows(r), :], d_ref[0, 1, _tile_rows(r), :]], axis=1)
            g = d * wk
            dx_ref[_tile_rows(r), :] = (rn * (g - xn * _group_sum(g * xn, hs) * (1.0 / DH))).astype(dx_ref.dtype)
            return dw + jnp.sum(d * xn, axis=0, keepdims=True)

        def copy_tile(r, _):
            dx_ref[_tile_rows(r), :] = jnp.concatenate([d_ref[0, 0, _tile_rows(r), :], d_ref[0, 1, _tile_rows(r), :]],
                                                       axis=1).astype(dx_ref.dtype)
            return 0

        @pl.when(kind < 2)
        def _():
            dw_ref[0, 0] = _tile_loop(t // PREP_ROWS, tile, jnp.zeros((1, LANE), F32))

        @pl.when(kind == 2)
        def _():
            _tile_loop(t // PREP_ROWS, copy_tile, 0)
            dw_ref[0, 0] = jnp.zeros((1, LANE), F32)

    return pl.pallas_call(
        body, name="fox_prep_bwd", grid=(3, nj),
        in_specs=[pl.BlockSpec((t, LANE), lambda i, j: (0, FOX_COL0 + i * nj + j)),
                  pl.BlockSpec((3, LANE), lambda i, j: (0, 0)),
                  pl.BlockSpec((1, 2, t, DH), lambda i, j: (i, j, 0, 0))],
        out_specs=[pl.BlockSpec((t, LANE), lambda i, j: (0, i * nj + j)),
                   pl.BlockSpec((1, 1, 1, LANE), lambda i, j: (i, j, 0, 0))],
        out_shape=[jax.ShapeDtypeStruct((t, 3 * WIDTH), MXU), jax.ShapeDtypeStruct((3, nj, 1, LANE), F32)],
        compiler_params=_params(("arbitrary", "arbitrary")),
    )(pm, nw, dqkv)


SC = 256
CPS = SC // CHUNK
GDN_HP_FWD = 4
GDN_HP_BWD = 2
Q_SCALE = DH ** -0.5


def _sc_masks():
    ri = lax.broadcasted_iota(jnp.int32, (SC, SC), 0)
    ci = lax.broadcasted_iota(jnp.int32, (SC, SC), 1)
    same = (ri // CHUNK) == (ci // CHUNK)
    return same & (ri >= ci), same & (ri > ci), ri == ci


def _unit_lower_inverses(ms, eye):
    invs = [jnp.where(eye, 1.0, 0.0) + m for m in ms]
    ms = [_dot(m, m) for m in ms]
    for _ in range(4):
        both = [_dot(jnp.concatenate([inv, m], axis=0), m) for inv, m in zip(invs, ms)]
        invs = [inv + b[:SC] for inv, b in zip(invs, both)]
        ms = [b[SC:] for b in both]
    return [inv + _dot(inv, m) for inv, m in zip(invs, ms)]


def _lane_col(blk, lane_idx):
    lane = lax.broadcasted_iota(jnp.int32, blk.shape, 1)
    return jnp.sum(jnp.where(lane == lane_idx, blk, 0.0), axis=1, keepdims=True)


def _to_lane(col, lane_idx):
    lane = lax.broadcasted_iota(jnp.int32, (col.shape[0], LANE), 1)
    return jnp.where(lane == lane_idx, col, 0.0)


def _gdn_columns(gates_ref, run_ref, tot_ref, runt_ref, rows, h):
    return (_lane_col(gates_ref[rows, :], h), _lane_col(run_ref[rows, :], HEADS + h),
            _lane_col(tot_ref[rows, :], HEADS + h), runt_ref[pl.ds(h, 1), rows])


def _gdn_local(q, k, beta, gc, gl, grow, causal, with_kk=True):
    decay = jnp.exp(jnp.where(causal, gc - grow, -1e30))
    egc = jnp.exp(gc)
    ekd = jnp.exp(gl - gc)
    qs = q * Q_SCALE
    kb = k * beta
    if with_kk:
        both = _dot_nt(jnp.concatenate([kb, qs], axis=0), k)
        kk, qk = both[:SC], both[SC:]
    else:
        kk, qk = None, _dot_nt(qs, k)
    return beta, gl, decay, egc, ekd, qs, kb, kk, qk, jnp.where(causal, qk * decay, 0.0)


def _chunk_rows(c):
    return pl.ds(c * CHUNK if isinstance(c, int) else pl.multiple_of(c * CHUNK, CHUNK), CHUNK)


def _sc_rows(b):
    return pl.ds(b * SC if isinstance(b, int) else pl.multiple_of(b * SC, SC), SC)


def _gdn_fwd(qkv, gates, run, tot, run_t):
    t = qkv.shape[2]
    nc = t // CHUNK
    nsc = t // SC

    hp_n = GDN_HP_FWD
    heads = range(hp_n)

    def body(qkv_ref, gates_ref, run_ref, tot_ref, runt_ref, o_ref, st_ref, inv_ref, kc_s, qc_s, g_s, au_s):
        hp = pl.program_id(0)
        causal, strict, eye = _sc_masks()

        def local(b, _):
            rows = _sc_rows(b)
            loc = [_gdn_local(qkv_ref[0, hh, rows, :], qkv_ref[1, hh, rows, :],
                              *_gdn_columns(gates_ref, run_ref, tot_ref, runt_ref, rows, hp * hp_n + hh), causal)
                   for hh in heads]
            invs = _unit_lower_inverses([-jnp.where(strict, l[7] * l[2], 0.0) for l in loc], eye)
            uws = []
            for hh in heads:
                beta, _, _, egc, _, _, kb, _, _, _ = loc[hh]
                inv_ref[hh, rows, :] = invs[hh].astype(inv_ref.dtype)
                uws.append(_dot(invs[hh], jnp.concatenate([qkv_ref[2, hh, rows, :] * beta, kb * egc], axis=1)))
            for hh in heads:
                _, _, _, egc, ekd, qs, _, _, _, attn = loc[hh]
                auw = _dot(attn, uws[hh])
                g_s[hh, rows, :] = qs * egc - auw[:, DH:]
                au_s[hh, rows, :] = auw[:, :DH]
                kd = qkv_ref[1, hh, rows, :] * ekd
                for j in range(CPS):
                    sl = slice(j * CHUNK, (j + 1) * CHUNK)
                    both = _dot_tn(kd[sl], uws[hh][sl])
                    kc_s[hh, b * CPS + j] = both[:, DH:]
                    qc_s[hh, b * CPS + j] = both[:, :DH]
            return 0

        def step(c, states):
            rows = _chunk_rows(c)
            tot_row = tot_ref[pl.ds(c * CHUNK, 1), :]
            new = []
            for hh in heads:
                s = states[hh]
                st_ref[hh, c] = s
                o_ref[hh, rows, :] = _dot(g_s[hh, rows, :], s) + au_s[hh, rows, :]
                egl = jnp.exp(_lane_col(tot_row, HEADS + hp * hp_n + hh))
                new.append(egl * s - _dot(kc_s[hh, c], s) + qc_s[hh, c])
            return tuple(new)

        def steps_of(b, states):
            for j in range(CPS):
                states = step(b * CPS + j, states)
            return states

        def fused(b, states):
            states = steps_of(b - 1, states)
            local(b, 0)
            return states

        local(0, 0)
        states = lax.fori_loop(1, nsc, fused, tuple(jnp.zeros((DH, DH), F32) for _ in heads))
        steps_of(nsc - 1, states)

    whole = pl.BlockSpec((t, LANE), lambda h: (0, 0))
    once = dict(pipeline_mode=pl.Buffered(1))
    return pl.pallas_call(
        body, name="gdn_fwd", grid=(HEADS // hp_n,),
        in_specs=[pl.BlockSpec((3, hp_n, t, DH), lambda h: (0, h, 0, 0), **once), whole, whole, whole,
                  pl.BlockSpec((HEADS, t), lambda h: (1, 0))],
        out_specs=[pl.BlockSpec((hp_n, t, DH), lambda h: (h, 0, 0), **once),
                   pl.BlockSpec((hp_n, nc, DH, DH), lambda h: (h, 0, 0, 0), **once),
                   pl.BlockSpec((hp_n, t, SC), lambda h: (h, 0, 0), **once)],
        out_shape=[jax.ShapeDtypeStruct((HEADS, t, DH), F32), jax.ShapeDtypeStruct((HEADS, nc, DH, DH), F32),
                   jax.ShapeDtypeStruct((HEADS, t, SC), MXU)],
        scratch_shapes=[pltpu.VMEM((hp_n, nc, DH, DH), F32), pltpu.VMEM((hp_n, nc, DH, DH), F32),
                        pltpu.VMEM((hp_n, t, DH), F32), pltpu.VMEM((hp_n, t, DH), F32)],
        compiler_params=_params(("arbitrary",)),
    )(qkv, gates, run, tot, run_t)


def _gdn_bwd(qkv, gates, run, tot, run_t, inv, states, do):
    t = qkv.shape[2]
    nc = t // CHUNK
    nsc = t // SC

    hp_n = GDN_HP_BWD
    heads = range(hp_n)

    def body(qkv_ref, gates_ref, run_ref, tot_ref, runt_ref, inv_ref, st_ref, do_ref,
             dqkv_ref, dcol_ref, dtot_ref, drow_ref, uw_s, kc_s, h_s, dsn_s):
        hp = pl.program_id(0)
        causal, strict, _ = _sc_masks()

        @pl.when(hp == 0)
        def _():
            dcol_ref[...] = jnp.zeros_like(dcol_ref)
            dtot_ref[...] = jnp.zeros_like(dtot_ref)

        def local_of(hh, rows, with_kk=True):
            return _gdn_local(qkv_ref[0, hh, rows, :], qkv_ref[1, hh, rows, :],
                              *_gdn_columns(gates_ref, run_ref, tot_ref, runt_ref, rows, hp * hp_n + hh), causal, with_kk)

        def local(b, _):
            rows = _sc_rows(b)
            loc = [local_of(hh, rows, with_kk=False) for hh in heads]
            us, ws = [], []
            for hh in heads:
                beta, _, _, egc, _, _, kb, _, _, _ = loc[hh]
                inv_b = inv_ref[hh, rows, :]
                us.append(_dot(inv_b, qkv_ref[2, hh, rows, :] * beta))
                ws.append(_dot(inv_b, kb * egc))
            gs = [loc[hh][5] * loc[hh][3] - _dot(loc[hh][9], ws[hh]) for hh in heads]
            for hh in heads:
                uw_s[hh, rows, :] = jnp.concatenate([us[hh], ws[hh]], axis=1)
                kd = qkv_ref[1, hh, rows, :] * loc[hh][4]
                dout = do_ref[hh, rows, :]
                for j in range(CPS):
                    sl = slice(j * CHUNK, (j + 1) * CHUNK)
                    kc_s[hh, b * CPS + j] = _dot_tn(kd[sl], ws[hh][sl])
                    h_s[hh, b * CPS + j] = _dot_tn(gs[hh][sl], dout[sl])
            return 0

        def step(c, dss):
            tot_row = tot_ref[pl.ds(c * CHUNK, 1), :]
            new = []
            for hh in heads:
                ds = dss[hh]
                dsn_s[hh, c] = ds
                egl = jnp.exp(_lane_col(tot_row, HEADS + hp * hp_n + hh))
                new.append(egl * ds - _dot_tn(kc_s[hh, c], ds) + h_s[hh, c])
            return tuple(new)

        def steps_of(b, dss):
            for j in reversed(range(CPS)):
                dss = step(b * CPS + j, dss)
            return dss

        def fused(i, dss):
            b = nsc - 2 - i
            dss = steps_of(b + 1, dss)
            local(b, 0)
            return dss

        local(nsc - 1, 0)
        dss = lax.fori_loop(0, nsc - 1, fused, tuple(jnp.zeros((DH, DH), F32) for _ in heads))
        steps_of(0, dss)

        def back(b, _):
            rows = _sc_rows(b)
            first = lax.broadcasted_iota(jnp.int32, (CHUNK, 1), 0) == 0
            loc = [local_of(hh, rows) for hh in heads]
            sign = jnp.where(lax.broadcasted_iota(jnp.int32, (1, LANE), 1) < DH, 1.0, -1.0)
            mid = []
            for hh in heads:
                beta, gl, decay, egc, ekd, qs, kb, kk, qk, attn = loc[hh]
                uw = uw_s[hh, rows, :]
                kd = qkv_ref[1, hh, rows, :] * ekd
                dout = do_ref[hh, rows, :]
                dg_p, dkd_p, duw_p, dgl_p = [], [], [], []
                for j in range(CPS):
                    sl = slice(j * CHUNK, (j + 1) * CHUNK)
                    s = st_ref[hh, b * CPS + j]
                    dsn = dsn_s[hh, b * CPS + j]
                    both = _dot_nt(jnp.concatenate([dsn, dout[sl]], axis=0), s)
                    dg_p.append(both[CHUNK:])
                    dsc = jnp.concatenate([dsn, -both[:CHUNK]], axis=1)
                    dkd_p.append(_dot_nt(uw[sl], dsc))
                    duw_p.append(_dot(kd[sl], dsc))
                    degl = jnp.sum(jnp.sum(s * dsn, axis=1, keepdims=True), axis=0, keepdims=True)
                    dgl_p.append(jnp.where(first, degl * jnp.exp(gl[j * CHUNK:j * CHUNK + 1, :]), 0.0))
                dg, dkd = jnp.concatenate(dg_p, axis=0), jnp.concatenate(dkd_p, axis=0)
                dod = jnp.concatenate([dout, dg], axis=1)
                da = jnp.where(causal, _dot_nt(dod, uw * sign), 0.0)
                duw = jnp.concatenate(duw_p, axis=0) + _dot_tn(attn, dod) * sign
                mid.append((kd, dg, dkd, da, duw, jnp.concatenate(dgl_p, axis=0)))
            inv_ts = [inv_ref[hh, rows, :].astype(F32).T for hh in heads]
            its = [_dot(inv_ts[hh], mid[hh][4]) for hh in heads]
            dinvs = []
            for hh in heads:
                beta, _, _, egc, _, _, kb, _, _, _ = loc[hh]
                dinvs.append(_dot_nt(mid[hh][4], jnp.concatenate([qkv_ref[2, hh, rows, :] * beta, kb * egc], axis=1)))
            half = [_dot(inv_ts[hh], dinvs[hh]) for hh in heads]
            dls = [jnp.where(strict, -_dot(half[hh], inv_ts[hh]), 0.0) for hh in heads]
            for hh in heads:
                head = hp * hp_n + hh
                beta, gl, decay, egc, ekd, qs, kb, kk, qk, attn = loc[hh]
                kd, dg, dkd, da, _, dgl_first = mid[hh]
                k, v = qkv_ref[1, hh, rows, :], qkv_ref[2, hh, rows, :]
                dvb, dkbe = its[hh][:, :DH], its[hh][:, DH:]
                dl = dls[hh]
                dlogd = (dl * kk + da * qk) * decay
                dd = jnp.concatenate([dl * decay, da * decay], axis=0)
                ddk = _dot(dd, k)
                dkb = ddk[:SC] + dkbe * egc
                dqs = ddk[SC:] + dg * egc
                dk = _dot_tn(dd, jnp.concatenate([kb, qs], axis=0)) + dkd * ekd + dkb * beta
                dkd_kd = jnp.sum(dkd * kd, axis=1, keepdims=True)
                narrow = dg * (qs * egc) + dkbe * (kb * egc)
                wide = dlogd[:, :LANE] + dlogd[:, LANE:] + jnp.concatenate([narrow, jnp.zeros((SC, LANE - DH), F32)], axis=1)
                dgc = jnp.sum(wide, axis=1, keepdims=True) - dkd_kd
                dbeta = jnp.sum(dkb * k + dvb * v, axis=1, keepdims=True)
                dqkv_ref[0, hh, rows, :] = dqs * Q_SCALE
                dqkv_ref[1, hh, rows, :] = dk
                dqkv_ref[2, hh, rows, :] = dvb * beta
                dcol_ref[rows, :] += _to_lane(dbeta, head) + _to_lane(dgc, HEADS + head)
                dtot_ref[rows, :] += _to_lane(dkd_kd + dgl_first, HEADS + head)
                drow_ref[pl.ds(head, 1), rows] = -jnp.sum(dlogd, axis=0, keepdims=True)
            return 0

        lax.fori_loop(0, nsc, back, 0)

    whole = pl.BlockSpec((t, LANE), lambda h: (0, 0))
    rowspec = pl.BlockSpec((HEADS, t), lambda h: (0, 0))
    sq = pltpu.VMEM((hp_n, nc, DH, DH), F32)
    per_head = pltpu.VMEM((hp_n, t, 2 * DH), F32)
    return pl.pallas_call(
        body, name="gdn_bwd", grid=(HEADS // hp_n,),
        in_specs=[pl.BlockSpec((3, hp_n, t, DH), lambda h: (0, h, 0, 0)), whole, whole, whole,
                  pl.BlockSpec((HEADS, t), lambda h: (1, 0)),
                  pl.BlockSpec((hp_n, t, SC), lambda h: (h, 0, 0)), pl.BlockSpec((hp_n, nc, DH, DH), lambda h: (h, 0, 0, 0)),
                  pl.BlockSpec((hp_n, t, DH), lambda h: (h, 0, 0))],
        out_specs=[pl.BlockSpec((3, hp_n, t, DH), lambda h: (0, h, 0, 0)), whole, whole, rowspec],
        out_shape=[jax.ShapeDtypeStruct((3, HEADS, t, DH), F32), jax.ShapeDtypeStruct((t, LANE), F32),
                   jax.ShapeDtypeStruct((t, LANE), F32), jax.ShapeDtypeStruct((HEADS, t), F32)],
        scratch_shapes=[per_head, sq, sq, sq],
        compiler_params=_params(("arbitrary",)),
    )(qkv, gates, run, tot, run_t, inv, states, do)


FOX_HP_FWD = 4
FOX_HP_BWD = 4


def _wide_t(a):
    r = a.shape[0]
    return jnp.concatenate([a, jnp.zeros((r, LANE - DH), F32)], axis=1).T[:DH]


def _tall_t(a):
    r = a.shape[1]
    return jnp.concatenate([a, jnp.zeros((LANE - DH, r), F32)], axis=0).T[:, :DH]


def _key_side_f(run_blk, head, qb):
    col = jnp.broadcast_to(_lane_col(run_blk, 2 * HEADS + head), (run_blk.shape[0], LANE))
    return jnp.concatenate([col] * (qb // LANE), axis=1)


def _diag_mask(qb):
    return lax.broadcasted_iota(jnp.int32, (qb, qb), 0) <= lax.broadcasted_iota(jnp.int32, (qb, qb), 1)


def _fox_fwd(qkv, run, run_t, carry=None):
    t = qkv.shape[2]
    qb = min(QB, t)
    nq = t // qb
    hp_n = FOX_HP_FWD
    c_in, c_in_specs, c_out_shape, c_out_specs, c_sem = _carry_operands(*carry) if carry else ([], [], [], [], None)
    n_c = len(c_in)

    def body(*refs):
        q_ref, k_ref, v_ref, run_ref, runt_ref = refs[:5]
        o_ref, lse_ref = refs[5 + n_c:7 + n_c]
        s0 = 7 + n_c + len(c_out_shape)
        kb_s, vt_s, raw_s, m_s, l_s, acc_s = refs[s0:s0 + 6]
        hp = pl.program_id(0)
        i = pl.program_id(1)

        if carry:
            _carry_step(carry[0], refs[5:5 + n_c], refs[7 + n_c], refs[8 + n_c], refs[-1], hp * nq + i,
                        (HEADS // hp_n) * nq)

        @pl.when(i == 0)
        def _():
            for hh in range(hp_n):
                kb_s[hh] = k_ref[0, hh].astype(MXU)
                for b in range(nq):
                    rows = slice(b * qb, (b + 1) * qb)
                    vt_s[hh, :, rows] = _wide_t(v_ref[0, hh, rows, :]).astype(MXU)

        qrows = pl.ds(pl.multiple_of(i * qb, qb), qb)
        qs = [(q_ref[0, hh] * Q_SCALE).astype(MXU) for hh in range(hp_n)]
        fq = [runt_ref[pl.ds(hp * hp_n + hh, 1), qrows] for hh in range(hp_n)]

        def block_rows(j):
            return pl.ds(pl.multiple_of(j * qb, qb), qb)

        def scores(j):
            for hh in range(hp_n):
                raw_s[j % 2, hh] = _dot_nt(kb_s[hh, block_rows(j), :], qs[hh])

        def absorb(j, diagonal):
            rows = block_rows(j)
            run_blk = run_ref[rows, :]
            stats, pv = [], []
            for hh in range(hp_n):
                m, l = m_s[hh], l_s[hh]
                st = raw_s[j % 2, hh] + (fq[hh] - _key_side_f(run_blk, hp * hp_n + hh, qb))
                if diagonal:
                    st = jnp.where(_diag_mask(qb), st, -1e30)
                m_new = jnp.maximum(m, jnp.max(st, axis=0, keepdims=True))
                p = jnp.exp(st - m_new)
                alpha = jnp.exp(m - m_new)
                stats.append((m_new, alpha * l + jnp.sum(p, axis=0, keepdims=True), alpha))
                pv.append(_dot(vt_s[hh, :, rows], p))
            for hh in range(hp_n):
                m_s[hh], l_s[hh] = stats[hh][0], stats[hh][1]
                acc_s[hh] = stats[hh][2] * acc_s[hh] + pv[hh]

        def kstep(j, _):
            scores(j + 1)
            absorb(j, False)
            return 0

        for hh in range(hp_n):
            m_s[hh] = jnp.full((1, qb), -1e30, F32)
            l_s[hh] = jnp.zeros((1, qb), F32)
            acc_s[hh] = jnp.zeros((DH, qb), F32)
        scores(0)
        lax.fori_loop(0, i, kstep, 0)
        absorb(i, True)
        for hh in range(hp_n):
            l = l_s[hh]
            o_ref[hh] = _tall_t(acc_s[hh] / l)
            lse_ref[pl.ds(hp * hp_n + hh, 1), qrows] = m_s[hh] + jnp.log(l)

    out = pl.pallas_call(
        body, name="fox_fwd", grid=(HEADS // hp_n, nq),
        in_specs=[pl.BlockSpec((1, hp_n, qb, DH), lambda h, i: (0, h, i, 0)),
                  pl.BlockSpec((1, hp_n, t, DH), lambda h, i: (1, h, 0, 0)),
                  pl.BlockSpec((1, hp_n, t, DH), lambda h, i: (2, h, 0, 0)),
                  pl.BlockSpec((t, LANE), lambda h, i: (0, 0)),
                  pl.BlockSpec((HEADS, t), lambda h, i: (2, 0))] + c_in_specs,
        out_specs=[pl.BlockSpec((hp_n, qb, DH), lambda h, i: (h, i, 0)),
                   pl.BlockSpec((HEADS, t), lambda h, i: (0, 0))] + c_out_specs,
        out_shape=[jax.ShapeDtypeStruct((HEADS, t, DH), F32), jax.ShapeDtypeStruct((HEADS, t), F32)] + c_out_shape,
        input_output_aliases={5 + j: 4 + j for j in range(n_c)},
        scratch_shapes=[pltpu.VMEM((hp_n, t, DH), MXU), pltpu.VMEM((hp_n, DH, t), MXU), pltpu.VMEM((2, hp_n, qb, qb), F32),
                        pltpu.VMEM((hp_n, 1, qb), F32), pltpu.VMEM((hp_n, 1, qb), F32), pltpu.VMEM((hp_n, DH, qb), F32)]
        + ([c_sem] if carry else []),
        compiler_params=_params(("arbitrary", "arbitrary")),
    )(qkv, qkv, qkv, run, run_t, *c_in)
    return (out[0], out[1], _carry_state(out[2:])) if carry else (out[0], out[1])


def _fox_bwd(qkv, run, run_t, o, lse, do, carry=None):
    t = qkv.shape[2]
    qb = min(QB, t)
    nq = t // qb
    hp_n = FOX_HP_BWD
    c_in, c_in_specs, c_out_shape, c_out_specs, c_sem = _carry_operands(*carry) if carry else ([], [], [], [], None)
    n_c = len(c_in)

    def body(*refs):
        q_ref, k_ref, v_ref, run_ref, runt_ref, o_ref, lse_ref, do_ref = refs[:8]
        dqkv_ref, dcol_ref, drow_ref = refs[8 + n_c:11 + n_c]
        s0 = 11 + n_c + len(c_out_shape)
        dqt_s, raw_s, dpt_s, dro_s, dk_s, dv_s, dsum_s = refs[s0:s0 + 7]
        hp = pl.program_id(0)
        j = pl.program_id(1)

        if carry:
            _carry_step(carry[0], refs[8:8 + n_c], refs[11 + n_c], refs[12 + n_c], refs[-1], hp * nq + j,
                        (HEADS // hp_n) * nq)

        @pl.when(j == 0)
        def _():
            dqt_s[...] = jnp.zeros_like(dqt_s)

        @pl.when((j == 0) & (hp == 0))
        def _():
            dcol_ref[...] = jnp.zeros_like(dcol_ref)
            drow_ref[...] = jnp.zeros_like(drow_ref)

        krows = pl.ds(pl.multiple_of(j * qb, qb), qb)
        run_blk = run_ref[krows, :]
        ones8 = jnp.ones((8, DH), MXU)
        kb, kt, vb, fk = [], [], [], []
        for hh in range(hp_n):
            kf = k_ref[0, hh]
            kb.append(kf.astype(MXU))
            kt.append(_wide_t(kf).astype(MXU))
            vb.append(v_ref[0, hh].astype(MXU))
            fk.append(_key_side_f(run_blk, hp * hp_n + hh, qb))

        def block_rows(i):
            return pl.ds(i * qb if isinstance(i, int) else pl.multiple_of(i * qb, qb), qb)

        def products(i):
            rows = block_rows(i)
            slot = i % 2
            for hh in range(hp_n):
                dout = do_ref[hh, rows, :]
                x = dout * o_ref[hh, rows, :]
                x_hi = x.astype(MXU)
                raw_s[slot, hh] = _dot_nt(kb[hh], q_ref[0, hh, rows, :] * Q_SCALE)
                dpt_s[slot, hh] = _dot_nt(vb[hh], dout)
                dro_s[slot, hh] = _dot_nt(ones8, x_hi) + _dot_nt(ones8, x - x_hi.astype(F32))

        def absorb(i, diagonal):
            rows = block_rows(i)
            slot = i % 2
            pieces = []
            for hh in range(hp_n):
                head = hp * hp_n + hh
                raw, dpt, drow = raw_s[slot, hh], dpt_s[slot, hh], dro_s[slot, hh, 0:1, :]
                off = runt_ref[pl.ds(head, 1), rows] - lse_ref[pl.ds(head, 1), rows]
                st = raw + (off - fk[hh])
                if diagonal:
                    st = jnp.where(_diag_mask(qb), st, -1e30)
                pt = jnp.exp(st)
                dst = pt * (dpt - drow)
                drow_ref[pl.ds(head, 1), rows] += jnp.sum(dst, axis=0, keepdims=True)
                folded = dst[:, 0:LANE]
                for c in range(1, qb // LANE):
                    folded = folded + dst[:, c * LANE:(c + 1) * LANE]
                pieces.append((_dot(dst, q_ref[0, hh, rows, :] * Q_SCALE), _dot(pt, do_ref[hh, rows, :]),
                               _dot(kt[hh], dst), folded))
            for hh in range(hp_n):
                dqt_s[hh, :, rows] += pieces[hh][2]
                if diagonal:
                    dk_s[hh], dv_s[hh], dsum_s[hh] = pieces[hh][0], pieces[hh][1], pieces[hh][3]
                else:
                    dk_s[hh] += pieces[hh][0]
                    dv_s[hh] += pieces[hh][1]
                    dsum_s[hh] += pieces[hh][3]

        def qstep(i, _):
            products(i + 1)
            absorb(i, False)
            return 0

        products(j)
        products(jnp.minimum(j + 1, nq - 1))
        absorb(j, True)
        lax.fori_loop(j + 1, nq - 1, qstep, 0)

        @pl.when(j < nq - 1)
        def _():
            absorb(nq - 1, False)
        for hh in range(hp_n):
            dqkv_ref[1, hh, krows, :] = dk_s[hh]
            dqkv_ref[2, hh, krows, :] = dv_s[hh]
            dcol_ref[krows, :] += _to_lane(-jnp.sum(dsum_s[hh], axis=1, keepdims=True), 2 * HEADS + hp * hp_n + hh)

        @pl.when(j == nq - 1)
        def _():
            for hh in range(hp_n):
                for b in range(nq):
                    rows = slice(b * qb, (b + 1) * qb)
                    dqkv_ref[0, hh, rows, :] = _tall_t(dqt_s[hh, :, rows]) * Q_SCALE

    once = dict(pipeline_mode=pl.Buffered(1))
    full = pl.BlockSpec((hp_n, t, DH), lambda h, j: (h, 0, 0), **once)
    rows8 = pl.BlockSpec((HEADS, t), lambda h, j: (0, 0))
    out = pl.pallas_call(
        body, name="fox_bwd", grid=(HEADS // hp_n, nq),
        in_specs=[pl.BlockSpec((1, hp_n, t, DH), lambda h, j: (0, h, 0, 0), **once),
                  pl.BlockSpec((1, hp_n, qb, DH), lambda h, j: (1, h, j, 0)),
                  pl.BlockSpec((1, hp_n, qb, DH), lambda h, j: (2, h, j, 0)),
                  pl.BlockSpec((t, LANE), lambda h, j: (0, 0), **once), pl.BlockSpec((HEADS, t), lambda h, j: (2, 0)),
                  full, rows8, full] + c_in_specs,
        out_specs=[pl.BlockSpec((3, hp_n, t, DH), lambda h, j: (0, h, 0, 0), **once),
                   pl.BlockSpec((t, LANE), lambda h, j: (0, 0)), rows8] + c_out_specs,
        out_shape=[jax.ShapeDtypeStruct((3, HEADS, t, DH), F32), jax.ShapeDtypeStruct((t, LANE), F32),
                   jax.ShapeDtypeStruct((HEADS, t), F32)] + c_out_shape,
        input_output_aliases={8 + j: 5 + j for j in range(n_c)},
        scratch_shapes=[pltpu.VMEM((hp_n, DH, t), F32), pltpu.VMEM((2, hp_n, qb, qb), F32), pltpu.VMEM((2, hp_n, qb, qb), F32),
                        pltpu.VMEM((2, hp_n, 8, qb), F32), pltpu.VMEM((hp_n, qb, DH), F32), pltpu.VMEM((hp_n, qb, DH), F32),
                        pltpu.VMEM((hp_n, qb, LANE), F32)] + ([c_sem] if carry else []),
        compiler_params=_params(("arbitrary", "arbitrary")),
    )(qkv, qkv, qkv, run, run_t, o, lse, do, *c_in)
    return (out[0], out[1], out[2], _carry_state(out[3:])) if carry else tuple(out)


Z_COL0 = 3 * WIDTH // LANE
FGATE_COL0 = 7 * WIDTH // LANE


def _gdn_post(o, pm, onw):
    t = pm.shape[0]

    def body(o_ref, z_ref, w_ref, m_ref):
        z = z_ref[...]
        sz = z * _sigmoid(z)
        w2 = jnp.concatenate([w_ref[...], w_ref[...]], axis=1)
        ov = jnp.concatenate([o_ref[0], o_ref[1]], axis=1)
        n = ov * lax.rsqrt(_group_sum(ov * ov, _head_sum_matrix()) * (1.0 / DH) + EPS) * w2
        m_ref[...] = (n * sz).astype(m_ref.dtype)

    return pl.pallas_call(
        body, name="gdn_post", grid=(WIDTH // LANE,),
        in_specs=[pl.BlockSpec((2, t, DH), lambda j: (j, 0, 0)), pl.BlockSpec((t, LANE), lambda j: (0, Z_COL0 + j)),
                  pl.BlockSpec((1, DH), lambda j: (0, 0))],
        out_specs=pl.BlockSpec((t, LANE), lambda j: (0, j)),
        out_shape=jax.ShapeDtypeStruct((t, WIDTH), MXU),
        compiler_params=_params(("arbitrary",)),
    )(o, pm, onw)


def _gdn_post_bwd(o, pm, onw, dmix):
    t = pm.shape[0]

    def body(o_ref, z_ref, w_ref, dm_ref, do_ref, dz_ref, dw_ref):
        @pl.when(pl.program_id(0) == 0)
        def _():
            dw_ref[...] = jnp.zeros_like(dw_ref)

        z = z_ref[...]
        sg = _sigmoid(z)
        sz = z * sg
        dsz = sg * (1.0 + z * (1.0 - sg))
        dm = dm_ref[...]
        hs = _head_sum_matrix()
        w2 = jnp.concatenate([w_ref[...], w_ref[...]], axis=1)
        ov = jnp.concatenate([o_ref[0], o_ref[1]], axis=1)
        r = lax.rsqrt(_group_sum(ov * ov, hs) * (1.0 / DH) + EPS)
        xn = ov * r
        dn = dm * sz
        dz_ref[...] = (dm * (xn * w2) * dsz).astype(dz_ref.dtype)
        dw2 = jnp.sum(dn * xn, axis=0, keepdims=True)
        dw_ref[...] += dw2[:, :DH] + dw2[:, DH:]
        g = dn * w2
        do = r * (g - xn * (_group_sum(g * xn, hs) * (1.0 / DH)))
        do_ref[0] = do[:, :DH]
        do_ref[1] = do[:, DH:]

    return pl.pallas_call(
        body, name="gdn_post_bwd", grid=(WIDTH // LANE,),
        in_specs=[pl.BlockSpec((2, t, DH), lambda j: (j, 0, 0)), pl.BlockSpec((t, LANE), lambda j: (0, Z_COL0 + j)),
                  pl.BlockSpec((1, DH), lambda j: (0, 0)), pl.BlockSpec((t, LANE), lambda j: (0, j))],
        out_specs=[pl.BlockSpec((2, t, DH), lambda j: (j, 0, 0)), pl.BlockSpec((t, LANE), lambda j: (0, j)),
                   pl.BlockSpec((1, DH), lambda j: (0, 0))],
        out_shape=[jax.ShapeDtypeStruct((HEADS, t, DH), F32), jax.ShapeDtypeStruct((t, WIDTH), MXU),
                   jax.ShapeDtypeStruct((1, DH), F32)],
        compiler_params=_params(("arbitrary",)),
    )(o, pm, onw, dmix)


def _fox_post(o, pm):
    t = pm.shape[0]

    def body(o_ref, g_ref, m_ref):
        m_ref[...] = (jnp.concatenate([o_ref[0], o_ref[1]], axis=1) * _sigmoid(g_ref[...])).astype(m_ref.dtype)

    return pl.pallas_call(
        body, name="fox_post", grid=(WIDTH // LANE,),
        in_specs=[pl.BlockSpec((2, t, DH), lambda j: (j, 0, 0)), pl.BlockSpec((t, LANE), lambda j: (0, FGATE_COL0 + j))],
        out_specs=pl.BlockSpec((t, LANE), lambda j: (0, j)),
        out_shape=jax.ShapeDtypeStruct((t, WIDTH), MXU),
        compiler_params=_params(("arbitrary",)),
    )(o, pm)


def _fox_post_bwd(o, pm, dmix):
    t = pm.shape[0]

    def body(o_ref, g_ref, dm_ref, do_ref, dg_ref):
        sg = _sigmoid(g_ref[...])
        dm = dm_ref[...]
        for hh in range(2):
            cols = slice(hh * DH, (hh + 1) * DH)
            do_ref[hh] = dm[:, cols] * sg[:, cols]
            dg_ref[:, cols] = (dm[:, cols] * o_ref[hh] * (sg * (1.0 - sg))[:, cols]).astype(dg_ref.dtype)

    return pl.pallas_call(
        body, name="fox_post_bwd", grid=(WIDTH // LANE,),
        in_specs=[pl.BlockSpec((2, t, DH), lambda j: (j, 0, 0)), pl.BlockSpec((t, LANE), lambda j: (0, FGATE_COL0 + j)),
                  pl.BlockSpec((t, LANE), lambda j: (0, j))],
        out_specs=[pl.BlockSpec((2, t, DH), lambda j: (j, 0, 0)), pl.BlockSpec((t, LANE), lambda j: (0, j))],
        out_shape=[jax.ShapeDtypeStruct((HEADS, t, DH), F32), jax.ShapeDtypeStruct((t, WIDTH), MXU)],
        compiler_params=_params(("arbitrary",)),
    )(o, pm, dmix)


def _tail(x, mixg, mixf, tgt, wo, n2w, wg_t, wu_t, wd, fw):
    t, d = x.shape
    dff = wd.shape[0]
    tb = min(TB, t)

    def body(x_ref, mg_ref, mf_ref, t_ref, wo_hbm, n2_ref, wg_hbm, wu_hbm, wd_hbm, fw_ref,
             h2_ref, act_ref, dgate_ref, dup_ref, dx3_ref, dx2_ref, dmg_ref, dmf_ref, dn2_ref, dfw_ref, loss_ref,
             wo_ref, wg_ref, wu_ref, wd_ref, w_sems):
        first = pl.program_id(0) == 0
        loads = [pltpu.make_async_copy(src, dst, w_sems.at[k]) for k, (src, dst) in enumerate(
            ((wo_hbm, wo_ref), (wg_hbm, wg_ref), (wu_hbm, wu_ref), (wd_hbm, wd_ref)))]

        def arrived(k):
            pl.when(first)(loads[k].wait)

        @pl.when(first)
        def _():
            for cp in loads:
                cp.start()
            dn2_ref[...] = jnp.zeros_like(dn2_ref)
            dfw_ref[...] = jnp.zeros_like(dfw_ref)
            loss_ref[...] = jnp.zeros_like(loss_ref)

        arrived(0)
        x2 = x_ref[...] + _dot(mg_ref[...], wo_ref[0:WIDTH, :]) + _dot(mf_ref[...], wo_ref[WIDTH:2 * WIDTH, :])
        r2 = lax.rsqrt(jnp.mean(x2 * x2, axis=-1, keepdims=True) + EPS)
        xn2 = x2 * r2
        h2 = (xn2 * n2_ref[...]).astype(MXU)
        h2_ref[...] = h2
        arrived(1)
        gate = _dot_nt(h2, wg_ref[...])
        arrived(2)
        up = _dot_nt(h2, wu_ref[...])
        sg = _sigmoid(gate)
        sl = gate * sg
        act = (sl * up).astype(MXU)
        act_ref[...] = act
        arrived(3)
        x3 = x2 + _dot(act, wd_ref[...])
        r3 = lax.rsqrt(jnp.mean(x3 * x3, axis=-1, keepdims=True) + EPS)
        xn3 = x3 * r3
        err = xn3 * fw_ref[...] - t_ref[...]
        loss_ref[...] += 0.5 * jnp.sum(jnp.mean(err * err, axis=-1, keepdims=True), axis=0, keepdims=True)
        dy = err * (1.0 / d)
        dfw_ref[...] += jnp.sum(dy * xn3, axis=0, keepdims=True)
        g3 = dy * fw_ref[...]
        dx3 = r3 * (g3 - xn3 * jnp.mean(g3 * xn3, axis=-1, keepdims=True))
        dx3_ref[...] = dx3.astype(MXU)
        dact = _dot_nt(dx3, wd_ref[...])
        dgate = (dact * up * (sg * (1.0 + gate * (1.0 - sg)))).astype(MXU)
        dup = (dact * sl).astype(MXU)
        dgate_ref[...] = dgate
        dup_ref[...] = dup
        dh2 = _dot(dgate, wg_ref[...]) + _dot(dup, wu_ref[...])
        dn2_ref[...] += jnp.sum(dh2 * xn2, axis=0, keepdims=True)
        g2 = dh2 * n2_ref[...]
        dx2 = dx3 + r2 * (g2 - xn2 * jnp.mean(g2 * xn2, axis=-1, keepdims=True))
        dx2_ref[...] = dx2
        dmg_ref[...] = _dot_nt(dx2, wo_ref[0:WIDTH, :])
        dmf_ref[...] = _dot_nt(dx2, wo_ref[WIDTH:2 * WIDTH, :])

    def tok(n):
        return pl.BlockSpec((tb, n), lambda i: (i, 0))

    acc = pl.BlockSpec((1, d), lambda i: (0, 0))
    anywhere = pl.BlockSpec(memory_space=pl.ANY)
    sds = jax.ShapeDtypeStruct
    return pl.pallas_call(
        body, name="tail", grid=(t // tb,),
        in_specs=[tok(d), tok(WIDTH), tok(WIDTH), tok(d), anywhere, _resident((1, d)),
                  anywhere, anywhere, anywhere, _resident((1, d))],
        out_specs=[tok(d), tok(dff), tok(dff), tok(dff), tok(d), tok(d), tok(WIDTH), tok(WIDTH), acc, acc,
                   pl.BlockSpec((1, 1), lambda i: (0, 0))],
        out_shape=[sds((t, d), MXU), sds((t, dff), MXU), sds((t, dff), MXU), sds((t, dff), MXU), sds((t, d), MXU),
                   sds((t, d), F32), sds((t, WIDTH), F32), sds((t, WIDTH), F32), sds((1, d), F32), sds((1, d), F32),
                   sds((1, 1), F32)],
        scratch_shapes=[pltpu.VMEM(w.shape, w.dtype) for w in (wo, wg_t, wu_t, wd)] + [pltpu.SemaphoreType.DMA((4,))],
        compiler_params=_params(("arbitrary",)),
    )(x, mixg, mixf, tgt, wo, n2w, wg_t, wu_t, wd, fw)


def _wgrads(a_list, b, name):
    t, n = b.shape
    ms = [a.shape[1] for a in a_list]
    bms = [256 if m % 256 == 0 else LANE for m in ms]
    nbs = [m // bm for m, bm in zip(ms, bms)]
    k = len(a_list)
    cast = b.dtype != jnp.dtype(MXU)

    def body(*refs):
        a_refs, b_ref, o_refs = refs[:k], refs[k], refs[k + 1:2 * k + 1]
        i = pl.program_id(0)
        if cast:
            @pl.when(i == 0)
            def _():
                refs[-1][...] = b_ref[...].astype(MXU)
        for a_ref, o_ref, nb in zip(a_refs, o_refs, nbs):
            @pl.when(i < nb)
            def _():
                o_ref[...] = _dot_tn(a_ref[...], refs[-1][...] if cast else b_ref[...]).astype(o_ref.dtype)

    def clamp(nb):
        return lambda i: jnp.minimum(i, nb - 1)

    return pl.pallas_call(
        body, name=name, grid=(max(nbs),),
        in_specs=[pl.BlockSpec((t, bm), lambda i, c=clamp(nb): (0, c(i))) for bm, nb in zip(bms, nbs)] + [_resident((t, n))],
        out_specs=[pl.BlockSpec((bm, n), lambda i, c=clamp(nb): (c(i), 0)) for bm, nb in zip(bms, nbs)],
        out_shape=[jax.ShapeDtypeStruct((m, n), WIRE) for m in ms],
        scratch_shapes=[pltpu.VMEM((t, n), MXU)] if cast else [],
        compiler_params=_params(("arbitrary",)),
    )(*a_list, b)


def _merge_dw_in(d_gdn, d_z, d_fox, d_fg, d_small, shards=False):
    pieces = [d_gdn, d_z, d_small[:2 * HEADS], d_fox, d_fg, d_small[2 * HEADS:3 * HEADS]]
    if not shards:
        return jnp.concatenate(pieces, axis=0)
    n = sum(p.shape[0] for p in pieces) // N_DEV
    out = []
    for dev in range(N_DEV):
        parts, first = [], 0
        for p in pieces:
            lo, hi = max(dev * n, first), min((dev + 1) * n, first + p.shape[0])
            if lo < hi:
                parts.append(p[lo - first:hi - first])
            first += p.shape[0]
        out.append(jnp.concatenate(parts, axis=0))
    return jnp.stack(out)


def _lanes(*pieces):
    v = jnp.concatenate([p.reshape(-1).astype(F32) for p in pieces])
    return jnp.pad(v, (0, LANE - v.shape[0])).reshape(1, LANE)


def _vector_params(p):
    d = p["norm1_w"].size
    gparams = jnp.concatenate([_lanes(jnp.zeros(HEADS), p["gdn_dt_bias"], p["fox_f_bias"]),
                               _lanes(jnp.zeros(HEADS), p["gdn_A_log"]), jnp.zeros((6, LANE), F32)])
    fox_nw = jnp.stack([jnp.tile(p["fox_q_norm_w"].reshape(-1), 2), jnp.tile(p["fox_k_norm_w"].reshape(-1), 2),
                        jnp.ones((LANE,), F32)])
    return dict(n1w=p["norm1_w"].reshape(1, d), n2w=p["norm2_w"].reshape(1, d), fw=p["final_norm_w"].reshape(1, d),
                onw=p["gdn_out_norm_w"].reshape(1, DH), gparams=gparams, fox_nw=fox_nw)


def _mixer_forward(x, vp, w_t, ws_t, conv_w, carry=None, carry_first=None):
    h1, pm, ps, *first = _inproj(x, vp["n1w"], w_t, ws_t, carry_first)
    gates, run, tot, run_t = _gates(ps, vp["gparams"])
    fqkv = _fox_prep(pm, vp["fox_nw"])
    o_fox, lse, *carried = _fox_fwd(fqkv, run, run_t, carry)
    if carry:
        pm, o_fox = lax.optimization_barrier((pm, o_fox))
    mixf = _fox_post(o_fox, pm)
    gqkv = _gdn_prep(pm, conv_w)
    o_gdn, states, inv = _gdn_fwd(gqkv, gates, run, tot, run_t)
    mixg = _gdn_post(o_gdn, pm, vp["onw"])
    return dict(h1=h1, pm=pm, ps=ps, gates=gates, run=run, tot=tot, run_t=run_t, gqkv=gqkv, o_gdn=o_gdn, states=states,
                inv=inv, mixg=mixg, fqkv=fqkv, o_fox=o_fox, lse=lse, mixf=mixf, carried=carried[0] if carried else None,
                carried_first=first[0] if first else None)


def _mixer_backward(x, vp, w_t, ws_t, conv_w, f, dx2, dmixg, dmixf, carry=None, scatter_own=False):
    pm = f["pm"]
    do_fox, dfg = _fox_post_bwd(f["o_fox"], pm, dmixf)
    dfqkv, dcol_f, drow_f, *carried = _fox_bwd(f["fqkv"], f["run"], f["run_t"], f["o_fox"], f["lse"], do_fox, carry)
    dfox, dfnw = _fox_prep_bwd(pm, vp["fox_nw"], dfqkv)
    do_gdn, dz, donw = _gdn_post_bwd(f["o_gdn"], pm, vp["onw"], dmixg)
    dgqkv, dcol_g, dtot_g, drow_g = _gdn_bwd(f["gqkv"], f["gates"], f["run"], f["tot"], f["run_t"], f["inv"], f["states"], do_gdn)
    dgdn, dconv = _gdn_prep_bwd(pm, conv_w, dgqkv)
    dps, gsum = _gates_bwd(f["ps"], vp["gparams"], dcol_g, dtot_g, dcol_f, drow_g, drow_f)
    dw_pieces = _wgrads([dgdn, dz, dfox, dfg, dps], f["h1"], "dw_in")
    dw_in = _merge_dw_in(*dw_pieces)
    own = ("scatter", [_merge_dw_in(*dw_pieces, shards=True), _cut(dconv, 1)]) if scatter_own else None
    grad_x, dn1w, *sent = _inproj_bwd(x, vp["n1w"], dx2, dgdn, dz, dfox, dfg, dps, w_t, ws_t, own)
    small = dict(dn1w=dn1w, gsum=gsum, donw=donw, dfnw=dfnw)
    return (grad_x, dw_in, dconv, small, *carried, *sent)


VECTORS = ("norm1_w", "norm2_w", "final_norm_w", "gdn_A_log", "gdn_dt_bias", "gdn_out_norm_w", "fox_f_bias",
           "fox_q_norm_w", "fox_k_norm_w")
VEC_ROWS = 16
LOSS_ROW = len(VECTORS)


def _pack_vectors(dn1w, dn2w, dfw, gsum, donw, dfnw, loss):
    d = dn1w.shape[1]

    def body(n1_ref, n2_ref, fw_ref, gs_ref, on_ref, fn_ref, loss_ref, o_ref):
        o_ref[...] = jnp.zeros_like(o_ref)
        o_ref[0:1, :] = n1_ref[...]
        o_ref[1:2, :] = n2_ref[...]
        o_ref[2:3, :] = fw_ref[...]
        o_ref[3:4, 0:HEADS] = gs_ref[0:1, 0:HEADS]
        o_ref[4:5, 0:HEADS] = gs_ref[1:2, 0:HEADS]
        o_ref[5:6, 0:DH] = on_ref[...]
        o_ref[6:7, 0:HEADS] = gs_ref[2:3, 0:HEADS]
        for kind in range(2):
            v = fn_ref[kind, 0]
            for j in range(1, fn_ref.shape[1]):
                v = v + fn_ref[kind, j]
            o_ref[7 + kind:8 + kind, 0:DH] = v[:, :DH] + v[:, DH:]
        o_ref[LOSS_ROW:LOSS_ROW + 1, 0:1] = loss_ref[...]

    return pl.pallas_call(body, name="pack_vectors", out_shape=jax.ShapeDtypeStruct((VEC_ROWS, d), F32),
                          compiler_params=_params())(dn1w, dn2w, dfw, gsum, donw, dfnw, loss)


def _late_grads(f, h2, act, dgate, dup, dx3, dx2):
    dw_gate, dw_up = _wgrads([dgate, dup], h2, "dw_gate_up")
    (dw_down,) = _wgrads([act], dx3, "dw_down")
    return {"w_out": jnp.concatenate(_wgrads([f["mixg"], f["mixf"]], dx2, "dw_out"), axis=0),
            "w_ffn_gate": dw_gate, "w_ffn_up": dw_up, "w_ffn_down": dw_down}


def _local_step(x, tgt, p, w_in_t, conv_w, wo, wg_t, wu_t, wd):
    vp = _vector_params(p)
    ws_t = _small_rows(w_in_t)
    f = _mixer_forward(x, vp, w_in_t, ws_t, conv_w)
    (h2, act, dgate, dup, dx3, dx2, dmixg, dmixf, dn2w, dfw, loss) = _tail(
        x, f["mixg"], f["mixf"], tgt, wo, vp["n2w"], wg_t, wu_t, wd, vp["fw"])
    grad_x, dw_in, dconv, small = _mixer_backward(x, vp, w_in_t, ws_t, conv_w, f, dx2, dmixg, dmixf)
    grads = {"w_in": dw_in, "gdn_conv_w": dconv, **_late_grads(f, h2, act, dgate, dup, dx3, dx2)}
    vec = _pack_vectors(small["dn1w"], dn2w, dfw, small["gsum"], small["donw"], small["dfnw"], loss)
    return grad_x[0], grads, vec


def _my_place():
    return lax.axis_index("x"), lax.axis_index("y"), lax.axis_index("c")


def _peers():
    x, y, c = _my_place()
    peers = []
    for k in range(1, N_DEV):
        px = 1 - x if k & 4 else x
        py = 1 - y if k & 2 else y
        pc = 1 - c if k & 1 else c
        peers.append(((px, py, pc), 4 * px + 2 * py + pc))
    return 4 * x + 2 * y + c, peers


def _spread_copies(kind, srcs, lands, send_sems, recv_sems):
    me, peers = _peers()
    kinds = [kind] * len(srcs) if isinstance(kind, str) else kind
    remote, local = [], []
    for i, (kd, src, land) in enumerate(zip(kinds, srcs, lands)):
        for k, (dev, idx) in enumerate(peers):
            remote.append(pltpu.make_async_remote_copy(
                src_ref=src if kd == "gather" else src.at[idx], dst_ref=land.at[me],
                send_sem=send_sems.at[i * (N_DEV - 1) + k], recv_sem=recv_sems.at[i * (N_DEV - 1) + k],
                device_id=dev, device_id_type=MESH))
        local.append((src if kd == "gather" else src.at[me], land.at[me]))
    return remote, local


def _gather_two_level(arrays, name):
    n = len(arrays)

    def body(*refs):
        srcs, lands = refs[:n], refs[n:2 * n]
        send_sems, recv_sems, local_sems = refs[2 * n:]
        x, y, c = _my_place()
        me, sibling = (x, y, c), (x, y, 1 - c)
        chips = [(1 - x, y), (x, 1 - y), (1 - x, 1 - y)]

        def index(p):
            return 4 * p[0] + 2 * p[1] + p[2]

        def copy(i, k, block, to, src=None):
            blk = lands[i].at[index(block)]
            return pltpu.make_async_remote_copy(
                src_ref=blk if src is None else src, dst_ref=blk, send_sem=send_sems.at[7 * i + k],
                recv_sem=recv_sems.at[7 * i + k], device_id=to, device_id_type=MESH)

        mine = [pltpu.make_async_copy(srcs[i], lands[i].at[index(me)], local_sems.at[i]) for i in range(n)]
        for cp in mine:
            cp.start()
        first = []
        for i in range(n):
            first.append(copy(i, 0, me, sibling, src=srcs[i]))
            first += [copy(i, 1 + j, me, (*chip, c), src=srcs[i]) for j, chip in enumerate(chips)]
        for cp in first:
            cp.start()
        passed = []
        for i in range(n):
            for j, chip in enumerate(chips):
                copy(i, 1 + j, (*chip, c), me).wait_recv()
                passed.append(copy(i, 4 + j, (*chip, c), sibling))
                passed[-1].start()
        for i in range(n):
            copy(i, 0, sibling, me).wait_recv()
            for j, chip in enumerate(chips):
                copy(i, 4 + j, (*chip, 1 - c), me).wait_recv()
        for cp in first + passed:
            cp.wait_send()
        for cp in mine:
            cp.wait()

    return pl.pallas_call(
        body, name=name,
        out_shape=[jax.ShapeDtypeStruct((N_DEV,) + a.shape, a.dtype) for a in arrays],
        in_specs=[pl.BlockSpec(memory_space=pl.ANY)] * n, out_specs=[pl.BlockSpec(memory_space=pl.ANY)] * n,
        scratch_shapes=[pltpu.SemaphoreType.DMA((7 * n,)), pltpu.SemaphoreType.DMA((7 * n,)),
                        pltpu.SemaphoreType.DMA((n,))],
    )(*arrays)


def _land_shape(kind, a):
    return (N_DEV,) + a.shape if kind == "gather" else a.shape


HBM = pl.BlockSpec(memory_space=pltpu.HBM)
SEM = pl.BlockSpec(memory_space=pltpu.SEMAPHORE)


def _hbm(a):
    return pltpu.with_memory_space_constraint(a, pltpu.HBM)


def _carry_operands(kind, arrays):
    n = len(arrays)
    kinds = [kind] * n if isinstance(kind, str) else kind
    lands = [lax.empty(_land_shape(kd, a), a.dtype) for kd, a in zip(kinds, arrays)]
    sems = [pltpu.SemaphoreType.DMA((n * (N_DEV - 1),))] * 2
    return ([_hbm(a) for a in list(arrays) + lands], [HBM] * (2 * n),
            sems + [pltpu.HBM(a.shape, a.dtype) for a in list(arrays) + lands], [SEM] * 2 + [HBM] * (2 * n),
            pltpu.SemaphoreType.DMA((n,)))


def _carry_step(kind, in_refs, send_sems, recv_sems, local_sems, step, n_steps):
    n = len(in_refs) // 2
    remote, local = _spread_copies(kind, in_refs[:n], in_refs[n:], send_sems, recv_sems)
    copies = [pltpu.make_async_copy(s, d, local_sems.at[i]) for i, (s, d) in enumerate(local)]
    per_step = -(-len(remote) // n_steps)
    for s in range(-(-len(remote) // per_step)):
        @pl.when(step == s)
        def _():
            for cp in remote[s * per_step:(s + 1) * per_step]:
                cp.start()
            if s == 0:
                for cp in copies:
                    cp.start()

    @pl.when(step == n_steps - 1)
    def _():
        for cp in copies:
            cp.wait()


def _carry_state(extra_out):
    n = (len(extra_out) - 2) // 2
    return list(extra_out[2:2 + n]), list(extra_out[2 + n:]), extra_out[0], extra_out[1]


def _spread_start(arrays, kind, name):
    n = len(arrays)

    def body(*refs):
        srcs, lands = refs[:n], refs[n:2 * n]
        send_sems, recv_sems = refs[2 * n], refs[2 * n + 1]
        token = refs[4 * n + 2]
        local_sems = refs[4 * n + 3]
        remote, local = _spread_copies(kind, srcs, lands, send_sems, recv_sems)
        for cp in remote:
            cp.start()
        copies = [pltpu.make_async_copy(s, d, local_sems.at[i]) for i, (s, d) in enumerate(local)]
        for cp in copies:
            cp.start()
        for cp in copies:
            cp.wait()
        token[...] = jnp.zeros_like(token)

    sems = (pltpu.SemaphoreType.DMA((n * (N_DEV - 1),)),) * 2
    kinds = [kind] * n if isinstance(kind, str) else kind
    lands = [lax.empty(_land_shape(kd, a), a.dtype) for kd, a in zip(kinds, arrays)]
    out = pl.pallas_call(
        body, name=name,
        out_shape=sems + tuple(pltpu.HBM(a.shape, a.dtype) for a in list(arrays) + lands)
        + (jax.ShapeDtypeStruct((8, LANE), F32),),
        in_specs=[HBM] * (2 * n), out_specs=tuple([SEM] * 2 + [HBM] * (2 * n) + [pl.BlockSpec(memory_space=pltpu.VMEM)]),
        input_output_aliases={j: 2 + j for j in range(2 * n)},
        scratch_shapes=[pltpu.SemaphoreType.DMA((n,))],
        compiler_params=pltpu.CompilerParams(has_side_effects=pltpu.SideEffectType.DATAFLOW_SIDE_EFFECTING),
    )(*[_hbm(a) for a in arrays], *[_hbm(a) for a in lands])
    return (list(out[2:2 + n]), list(out[2 + n:2 + 2 * n]), out[0], out[1]), out[-1]


def _spread_wait(state, kind, after, name):
    srcs, lands, send_sems, recv_sems = state
    n = len(srcs)
    after = list(after) if isinstance(after, (list, tuple)) else [after]

    def body(*refs):
        remote, _ = _spread_copies(kind, refs[:n], refs[n:2 * n], refs[2 * n], refs[2 * n + 1])
        for cp in remote:
            cp.wait_send()
        for cp in remote:
            cp.wait_recv()

    out = pl.pallas_call(
        body, name=name,
        out_shape=tuple(pltpu.HBM(a.shape, a.dtype) for a in srcs + lands),
        in_specs=[HBM] * (2 * n) + [SEM, SEM] + [pl.BlockSpec(memory_space=pl.ANY)] * len(after),
        out_specs=tuple([HBM] * (2 * n)),
        input_output_aliases={j: j for j in range(2 * n)},
        compiler_params=pltpu.CompilerParams(has_side_effects=pltpu.SideEffectType.DATAFLOW_SIDE_EFFECTING),
    )(*srcs, *lands, send_sems, recv_sems, *after)
    return list(out[n:])


ADAM_ROWS = 128
ADAM_COLS = 256


def _adam_math(g, w, m, v):
    nm = ADAM_B1 * m + (1.0 - ADAM_B1) * g
    nv = ADAM_B2 * v + (1.0 - ADAM_B2) * (g * g)
    m_hat = nm / (1.0 - ADAM_B1 ** ADAM_STEP)
    v_hat = nv / (1.0 - ADAM_B2 ** ADAM_STEP)
    return -ADAM_LR * (m_hat / (jnp.sqrt(v_hat) + ADAM_EPS) + ADAM_WD * w), nm, nv


def _sum_parts(p_ref):
    g = p_ref[0].astype(F32)
    for s in range(1, N_DEV):
        g = g + p_ref[s].astype(F32)
    return g


def _adam_matrix(parts, w, m, v, name):
    _, r, c = w.shape
    rb = ADAM_ROWS if r % ADAM_ROWS == 0 else r
    cb = ADAM_COLS if (rb == r and c % ADAM_COLS == 0) else c

    def body(p_ref, w_ref, m_ref, v_ref, g_ref, d_ref, nm_ref, nv_ref):
        g = _sum_parts(p_ref)
        g_ref[0] = g
        d_ref[0], nm_ref[0], nv_ref[0] = _adam_math(g, w_ref[0], m_ref[0], v_ref[0])

    blk = pl.BlockSpec((1, rb, cb), lambda i, j: (0, i, j))
    return pl.pallas_call(
        body, name=name, grid=(r // rb, c // cb),
        in_specs=[pl.BlockSpec((N_DEV, rb, cb), lambda i, j: (0, i, j)), blk, blk, blk],
        out_specs=[blk] * 4, out_shape=[jax.ShapeDtypeStruct(w.shape, F32)] * 4,
        compiler_params=_params(("arbitrary", "arbitrary")),
    )(parts, w, m, v)


def _copy_rows(a, name):
    _, r, c = a.shape
    rb = min(TB, r)

    def body(a_ref, o_ref):
        o_ref[...] = a_ref[...]

    blk = pl.BlockSpec((1, rb, c), lambda i: (0, i, 0))
    return pl.pallas_call(body, name=name, grid=(r // rb,), in_specs=[blk], out_specs=blk,
                          out_shape=jax.ShapeDtypeStruct(a.shape, a.dtype), compiler_params=_params(("arbitrary",)))(a)


def _adam_vectors(parts, ws, ms, vs):
    nv = len(ws)

    def body(*refs):
        p_ref = refs[0]
        w_refs, m_refs, v_refs = refs[1:1 + nv], refs[1 + nv:1 + 2 * nv], refs[1 + 2 * nv:1 + 3 * nv]
        outs = refs[1 + 3 * nv:]
        g_all = _sum_parts(p_ref)
        for i in range(nv):
            n = w_refs[i].shape[1]
            g = g_all[i:i + 1, 0:n]
            d, nm, nvv = _adam_math(g, w_refs[i][...], m_refs[i][...], v_refs[i][...])
            outs[i][...] = g
            outs[nv + i][...] = d
            outs[2 * nv + i][...] = nm
            outs[3 * nv + i][...] = nvv
        outs[4 * nv][...] = g_all[LOSS_ROW:LOSS_ROW + 1, 0:1]

    shapes = [jax.ShapeDtypeStruct(a.shape, F32) for a in ws]
    out = pl.pallas_call(body, name="adam_vectors", out_shape=shapes * 4 + [jax.ShapeDtypeStruct((1, 1), F32)],
                         compiler_params=_params())(parts, *ws, *ms, *vs)
    return out[:nv], out[nv:2 * nv], out[2 * nv:3 * nv], out[3 * nv:4 * nv], out[4 * nv]


MATRICES = (("w_in", 1), ("gdn_conv_w", 1), ("w_out", 0), ("w_ffn_gate", 1), ("w_ffn_up", 1), ("w_ffn_down", 0))
TRANSPOSED = ("w_in", "w_ffn_gate", "w_ffn_up")
WEIGHTS = ("norm1_w", "w_in", "gdn_conv_w", "gdn_A_log", "gdn_dt_bias", "gdn_out_norm_w", "fox_f_bias", "fox_q_norm_w",
           "fox_k_norm_w", "w_out", "norm2_w", "w_ffn_gate", "w_ffn_up", "w_ffn_down", "final_norm_w")


def _join(blocks, axis):
    _, r, c = blocks.shape
    if axis == 0:
        return blocks.reshape(N_DEV * r, c)
    return blocks.transpose(1, 0, 2).reshape(r, N_DEV * c)


def _cut(full, axis):
    r, c = full.shape
    if axis == 0:
        return full.reshape(N_DEV, r // N_DEV, c)
    return full.reshape(r, N_DEV, c // N_DEV).transpose(1, 0, 2)


def kernel(x, norm1_w, w_in, gdn_conv_w, gdn_A_log, gdn_dt_bias, gdn_out_norm_w, fox_f_bias, fox_q_norm_w, fox_k_norm_w, w_out, norm2_w, w_ffn_gate, w_ffn_up, w_ffn_down, final_norm_w, loss_target, m_norm1_w, m_w_in, m_gdn_conv_w, m_gdn_A_log, m_gdn_dt_bias, m_gdn_out_norm_w, m_fox_f_bias, m_fox_q_norm_w, m_fox_k_norm_w, m_w_out, m_norm2_w, m_w_ffn_gate, m_w_ffn_up, m_w_ffn_down, m_final_norm_w, v_norm1_w, v_w_in, v_gdn_conv_w, v_gdn_A_log, v_gdn_dt_bias, v_gdn_out_norm_w, v_fox_f_bias, v_fox_q_norm_w, v_fox_k_norm_w, v_w_out, v_norm2_w, v_w_ffn_gate, v_w_ffn_up, v_w_ffn_down, v_final_norm_w):
    w = dict(norm1_w=norm1_w, w_in=w_in, gdn_conv_w=gdn_conv_w, gdn_A_log=gdn_A_log, gdn_dt_bias=gdn_dt_bias,
             gdn_out_norm_w=gdn_out_norm_w, fox_f_bias=fox_f_bias, fox_q_norm_w=fox_q_norm_w, fox_k_norm_w=fox_k_norm_w,
             w_out=w_out, norm2_w=norm2_w, w_ffn_gate=w_ffn_gate, w_ffn_up=w_ffn_up, w_ffn_down=w_ffn_down,
             final_norm_w=final_norm_w)
    m = dict(norm1_w=m_norm1_w, w_in=m_w_in, gdn_conv_w=m_gdn_conv_w, gdn_A_log=m_gdn_A_log, gdn_dt_bias=m_gdn_dt_bias,
             gdn_out_norm_w=m_gdn_out_norm_w, fox_f_bias=m_fox_f_bias, fox_q_norm_w=m_fox_q_norm_w,
             fox_k_norm_w=m_fox_k_norm_w, w_out=m_w_out, norm2_w=m_norm2_w, w_ffn_gate=m_w_ffn_gate,
             w_ffn_up=m_w_ffn_up, w_ffn_down=m_w_ffn_down, final_norm_w=m_final_norm_w)
    v = dict(norm1_w=v_norm1_w, w_in=v_w_in, gdn_conv_w=v_gdn_conv_w, gdn_A_log=v_gdn_A_log, gdn_dt_bias=v_gdn_dt_bias,
             gdn_out_norm_w=v_gdn_out_norm_w, fox_f_bias=v_fox_f_bias, fox_q_norm_w=v_fox_q_norm_w,
             fox_k_norm_w=v_fox_k_norm_w, w_out=v_w_out, norm2_w=v_norm2_w, w_ffn_gate=v_w_ffn_gate,
             w_ffn_up=v_w_ffn_up, w_ffn_down=v_w_ffn_down, final_norm_w=v_final_norm_w)
    late = ("w_out", "w_ffn_gate", "w_ffn_up", "w_ffn_down")
    xs, tgt = x[0], loss_target[0]
    vp = _vector_params({n: w[n] for n in VECTORS})

    def rows_of(d, n):
        return d[n].transpose(0, 2, 1) if n in TRANSPOSED else d[n]

    wr, mr, vr = ({n: rows_of(d, n) for n, _ in MATRICES} for d in (w, m, v))

    w_in_blocks, conv_blocks = _gather_two_level([wr["w_in"][0].astype(WIRE), w["gdn_conv_w"][0]], "gather_in")
    w_t = _join(w_in_blocks, 0)
    conv_w = _join(conv_blocks, 1)
    early = ("w_ffn_gate",)
    rest = tuple(n for n in late if n not in early)
    f = _mixer_forward(xs, vp, w_t, _small_rows(w_t), conv_w, ("gather", [wr[n][0].astype(WIRE) for n in rest]),
                       ("gather", [wr[n][0].astype(WIRE) for n in early]))
    full = {n: _join(b, 0) for n, b in zip(early, _spread_wait(f["carried_first"], "gather", f["mixg"], "gather_early_wait"))}
    full.update({n: _join(b, 0)
                 for n, b in zip(rest, _spread_wait(f["carried"], "gather", full[early[0]], "gather_late_wait"))})

    (h2, act, dgate, dup, dx3, dx2, dmixg, dmixf, dn2w, dfw, loss) = _tail(
        xs, f["mixg"], f["mixf"], tgt, full["w_out"], vp["n2w"], full["w_ffn_gate"], full["w_ffn_up"],
        full["w_ffn_down"], vp["fw"])
    dlate = _late_grads(f, h2, act, dgate, dup, dx3, dx2)

    grad_x, dw_in, dconv, small, state_grads, state_own = _mixer_backward(
        xs, vp, w_t, _small_rows(w_t), conv_w, f, dx2, dmixg, dmixf, ("scatter", [_cut(dlate[n], 0) for n in late]),
        scatter_own=True)
    vec = _pack_vectors(small["dn1w"], dn2w, dfw, small["gsum"], small["donw"], small["dfnw"], loss)

    state_vec, token = _spread_start([vec], "gather", "vectors_start")
    parts = dict(zip(late, _spread_wait(state_grads, "scatter", token, "grads_late_wait")))
    results = [{}, {}, {}, {}]

    updated = {}

    def update(n):
        out = _adam_matrix(parts[n], wr[n], mr[n], vr[n], "adam_" + n)
        updated[n] = out[0]
        for d, a in zip(results, out):
            d[n] = a.transpose(0, 2, 1) if n in TRANSPOSED else a

    for n in late:
        update(n)
    grad_x = _copy_rows(grad_x, "grad_x_result")
    ready = [updated[n] for n in late] + [d["w_in"] for d in (wr, mr, vr)] + [grad_x]
    (parts_vec,) = _spread_wait(state_vec, "gather", ready, "vectors_wait")
    row = lambda a: a.reshape(1, -1)
    *vec_out, total_loss = _adam_vectors(parts_vec, [row(w[n]) for n in VECTORS], [row(m[n]) for n in VECTORS],
                                         [row(v[n]) for n in VECTORS])
    parts["w_in"], parts["gdn_conv_w"] = _spread_wait(state_own, "scatter", parts_vec, "own_wait")
    update("w_in")
    update("gdn_conv_w")
    for d, arrs in zip(results, vec_out):
        for n, a in zip(VECTORS, arrs):
            d[n] = a.reshape(w[n].shape)
    return (total_loss[0, 0], grad_x, *[d[n] for d in results for n in WEIGHTS])
```

```python
import functools

import jax
import jax.numpy as jnp
from jax import lax
from jax.experimental import pallas as pl
from jax.experimental.pallas import tpu as pltpu

F32 = jnp.float32
MXU = jnp.bfloat16
WIRE = jnp.bfloat16
HI = lax.Precision.HIGHEST
EPS = 1e-6

N_DEV = 8
HEADS = 8
DH = 64
WIDTH = HEADS * DH
CHUNK = 64
LANE = 128
ROW_ALIGN = 16
TB = 256
QB = 256
VMEM_LIMIT = 60 * 1024 * 1024

ADAM_LR = 0.001
ADAM_B1 = 0.9
ADAM_B2 = 0.999
ADAM_EPS = 1e-08
ADAM_WD = 0.01
ADAM_STEP = 10

MESH = pl.DeviceIdType.MESH


def _params(sem=None):
    return pltpu.CompilerParams(dimension_semantics=sem, vmem_limit_bytes=VMEM_LIMIT)


def _resident(shape):
    n = len(shape)
    return pl.BlockSpec(shape, lambda *_: (0,) * n, pipeline_mode=pl.Buffered(1))


def _dot(a, b):
    return jnp.dot(a.astype(MXU), b.astype(MXU), preferred_element_type=F32)


def _dot_nt(a, b):
    return lax.dot_general(a.astype(MXU), b.astype(MXU), (((1,), (1,)), ((), ())), preferred_element_type=F32)


def _dot_tn(a, b):
    return lax.dot_general(a.astype(MXU), b.astype(MXU), (((0,), (0,)), ((), ())), preferred_element_type=F32)


def _hdot(a, b):
    return jnp.dot(a, b, precision=HI, preferred_element_type=F32)


def _hdot_nt(a, b):
    return lax.dot_general(a, b, (((1,), (1,)), ((), ())), precision=HI, preferred_element_type=F32)


def _hdot_tn(a, b):
    return lax.dot_general(a, b, (((0,), (0,)), ((), ())), precision=HI, preferred_element_type=F32)


def _sigmoid(x):
    return 0.5 * jnp.tanh(0.5 * x) + 0.5


def _softplus(x):
    return jnp.maximum(x, 0.0) + jnp.log(1.0 + jnp.exp(-jnp.abs(x)))


def _head_sum_matrix():
    ri = lax.broadcasted_iota(jnp.int32, (LANE, LANE), 0) // DH
    ci = lax.broadcasted_iota(jnp.int32, (LANE, LANE), 1) // DH
    return (ri == ci).astype(F32)


def _group_sum(a, ones_matrix):
    hi = a.astype(jnp.bfloat16)
    lo = (a - hi.astype(F32)).astype(jnp.bfloat16)
    m = ones_matrix.astype(jnp.bfloat16)
    return jnp.dot(hi, m, preferred_element_type=F32) + jnp.dot(lo, m, preferred_element_type=F32)


def _shift_down(x, s):
    return pltpu.roll(x, s, 0)


def _shift_up(x, s):
    return pltpu.roll(x, x.shape[0] - s, 0)


ROWS_A = 4 * WIDTH
ROWS_B = ROWS_A + 2 * HEADS
ROWS_C = ROWS_B + 4 * WIDTH


def _small_rows(w_t):
    return jnp.concatenate([w_t[ROWS_A:ROWS_B], w_t[ROWS_C:], jnp.zeros((LANE - 3 * HEADS, w_t.shape[1]), w_t.dtype)])


def _inproj(x, n1w, w_t, ws_t, carry=None):
    t, d = x.shape
    tb = min(TB, t)
    c_in, c_in_specs, c_out_shape, c_out_specs, c_sem = _carry_operands(*carry) if carry else ([], [], [], [], None)
    n_c = len(c_in)

    def body(*refs):
        x_ref, nw_ref, wt_ref, ws_ref = refs[:4]
        h_ref, pm_ref, ps_ref = refs[4 + n_c:7 + n_c]
        if carry:
            _carry_step(carry[0], refs[4:4 + n_c], refs[7 + n_c], refs[8 + n_c], refs[-1], pl.program_id(0), t // tb)
        xv = x_ref[...]
        r = lax.rsqrt(jnp.mean(xv * xv, axis=-1, keepdims=True) + EPS)
        h = (xv * r * nw_ref[...]).astype(MXU)
        h_ref[...] = h
        pm_ref[:, 0:ROWS_A] = _dot_nt(h, wt_ref[0:ROWS_A, :])
        pm_ref[:, ROWS_A:2 * ROWS_A] = _dot_nt(h, wt_ref[ROWS_B:ROWS_C, :])
        ps_ref[...] = _dot_nt(h, ws_ref[...])

    out = pl.pallas_call(
        body, name="inproj", grid=(t // tb,),
        in_specs=[pl.BlockSpec((tb, d), lambda i: (i, 0)), _resident((1, d)), _resident(w_t.shape), _resident((LANE, d))]
        + c_in_specs,
        out_specs=[pl.BlockSpec((tb, d), lambda i: (i, 0)), pl.BlockSpec((tb, 2 * ROWS_A), lambda i: (i, 0)),
                   pl.BlockSpec((tb, LANE), lambda i: (i, 0))] + c_out_specs,
        out_shape=[jax.ShapeDtypeStruct((t, d), MXU), jax.ShapeDtypeStruct((t, 2 * ROWS_A), F32),
                   jax.ShapeDtypeStruct((t, LANE), F32)] + c_out_shape,
        input_output_aliases={4 + j: 5 + j for j in range(n_c)},
        scratch_shapes=[c_sem] if carry else [],
        compiler_params=_params(("arbitrary",)),
    )(x, n1w, w_t, ws_t, *c_in)
    return (out[0], out[1], out[2], _carry_state(out[3:])) if carry else tuple(out)


def _inproj_bwd(x, n1w, dx2, dgdn, dz, dfox, dfg, dps, w_t, ws_t, carry=None):
    t, d = x.shape
    tb = min(TB, t)
    w3 = 3 * WIDTH
    c_in, c_in_specs, c_out_shape, c_out_specs, c_sem = _carry_operands(*carry) if carry else ([], [], [], [], None)
    n_c = len(c_in)

    def body(*refs):
        x_ref, nw_ref, dx2_ref, dgdn_ref, dz_ref, dfox_ref, dfg_ref, dps_ref, wm_ref, ws_ref = refs[:10]
        gx_ref, dnw_ref = refs[10 + n_c:12 + n_c]
        if carry:
            _carry_step(carry[0], refs[10:10 + n_c], refs[12 + n_c], refs[13 + n_c], refs[-1], pl.program_id(0), t // tb)
        dh = _dot(dgdn_ref[...], wm_ref[0:w3, :])
        dh += _dot(dz_ref[...], wm_ref[w3:ROWS_A, :])
        dh += _dot(dfox_ref[...], wm_ref[ROWS_B:ROWS_B + w3, :])
        dh += _dot(dfg_ref[...], wm_ref[ROWS_B + w3:ROWS_C, :])
        dh += _dot(dps_ref[...], ws_ref[...])
        xv = x_ref[...]
        r = lax.rsqrt(jnp.mean(xv * xv, axis=-1, keepdims=True) + EPS)
        xn = xv * r

        @pl.when(pl.program_id(0) == 0)
        def _():
            dnw_ref[...] = jnp.zeros_like(dnw_ref)

        dnw_ref[...] += jnp.sum(dh * xn, axis=0, keepdims=True)
        g = dh * nw_ref[...]
        gx_ref[0] = dx2_ref[...] + r * (g - xn * jnp.mean(g * xn, axis=-1, keepdims=True))

    def tok(n):
        return pl.BlockSpec((tb, n), lambda i: (i, 0))

    out = pl.pallas_call(
        body, name="inproj_bwd", grid=(t // tb,),
        in_specs=[tok(d), _resident((1, d)), tok(d), tok(w3), tok(WIDTH), tok(w3), tok(WIDTH), tok(LANE),
                  _resident(w_t.shape), _resident(ws_t.shape)] + c_in_specs,
        out_specs=[pl.BlockSpec((1, tb, d), lambda i: (0, i, 0)), pl.BlockSpec((1, d), lambda i: (0, 0))] + c_out_specs,
        out_shape=[jax.ShapeDtypeStruct((1, t, d), F32), jax.ShapeDtypeStruct((1, d), F32)] + c_out_shape,
        input_output_aliases={10 + j: 4 + j for j in range(n_c)},
        scratch_shapes=[c_sem] if carry else [],
        compiler_params=_params(("arbitrary",)),
    )(x, n1w, dx2, dgdn, dz, dfox, dfg, dps, w_t, ws_t, *c_in)
    return (out[0], out[1], _carry_state(out[2:])) if carry else tuple(out)


def _gate_lanes(shape):
    lane = lax.broadcasted_iota(jnp.int32, shape, 1)
    return lane < HEADS, (lane >= HEADS) & (lane < 2 * HEADS), (lane >= 2 * HEADS) & (lane < 3 * HEADS)


def _block_masks():
    ri = lax.broadcasted_iota(jnp.int32, (LANE, LANE), 0)
    ci = lax.broadcasted_iota(jnp.int32, (LANE, LANE), 1)
    same = (ri // CHUNK) == (ci // CHUNK)
    return ((ri >= ci).astype(F32), (ri <= ci).astype(F32), (same & (ri >= ci)).astype(F32),
            (same & (ri <= ci)).astype(F32), same.astype(F32))


def _gates(ps, gparams):
    t = ps.shape[0]
    nb = t // LANE

    def body(ps_ref, gp_ref, out_ref, run_ref, tot_ref, runt_ref):
        p = ps_ref[...]
        is_b, is_a, is_f = _gate_lanes(p.shape)
        z = p + gp_ref[0:1, :]
        neg_exp_a = -jnp.exp(gp_ref[1:2, :])
        glog = neg_exp_a * _softplus(z)
        logf = -_softplus(-z)
        out_ref[...] = jnp.where(is_b, _sigmoid(p), jnp.where(is_a, glog, jnp.where(is_f, logf, 0.0)))
        tril, _, tril_c, _, same_c = _block_masks()
        off = jnp.zeros((1, LANE), F32)
        for b in range(nb):
            rows = slice(b * LANE, (b + 1) * LANE)
            blk = out_ref[rows, :]
            ga = jnp.where(is_a[:LANE], blk, 0.0)
            fb = _hdot(tril, jnp.where(is_f[:LANE], blk, 0.0)) + off
            run = fb + _hdot(tril_c, ga)
            run_ref[rows, :] = run
            runt_ref[:, rows] = run.T
            tot_ref[rows, :] = _hdot(same_c, ga)
            off = fb[LANE - 1:LANE, :]

    return pl.pallas_call(
        body, name="gates",
        out_shape=[jax.ShapeDtypeStruct((t, LANE), F32)] * 3 + [jax.ShapeDtypeStruct((LANE, t), F32)],
        compiler_params=_params(),
    )(ps, gparams)


def _gates_bwd(ps, gparams, dcol_g, dtot_g, dcol_f, drow_g, drow_f):
    t = ps.shape[0]
    nb = t // LANE

    def body(ps_ref, gp_ref, dcg_ref, dtg_ref, dcf_ref, drg_ref, drf_ref, dps_ref, sums_ref, dl_ref, d0_ref, tr_ref):
        p = ps_ref[...]
        is_b, is_a, is_f = _gate_lanes(p.shape)
        _, triu, _, triu_c, same_c = _block_masks()
        tr_ref[...] = jnp.zeros_like(tr_ref)
        off = jnp.zeros((1, LANE), F32)
        for b in reversed(range(nb)):
            rows = slice(b * LANE, (b + 1) * LANE)
            tr_ref[HEADS:2 * HEADS, :] = drg_ref[:, rows]
            tr_ref[2 * HEADS:3 * HEADS, :] = drf_ref[:, rows]
            d = dcg_ref[rows, :] + dcf_ref[rows, :] + tr_ref[...].T
            d0_ref[rows, :] = d
            dlf = _hdot(triu, jnp.where(is_f[:LANE], d, 0.0)) + off
            dla = (_hdot(triu_c, jnp.where(is_a[:LANE], d, 0.0))
                   + _hdot(same_c, jnp.where(is_a[:LANE], dtg_ref[rows, :], 0.0)))
            dl_ref[rows, :] = dlf + dla
            off = dlf[0:1, :]
        z = p + gp_ref[0:1, :]
        neg_exp_a = -jnp.exp(gp_ref[1:2, :])
        sb = _sigmoid(p)
        glog = neg_exp_a * _softplus(z)
        dl = dl_ref[...]
        dp = jnp.where(is_b, d0_ref[...] * sb * (1.0 - sb),
                       jnp.where(is_a, dl * neg_exp_a * _sigmoid(z), jnp.where(is_f, dl * _sigmoid(-z), 0.0)))
        dps_ref[...] = dp
        s_a = jnp.sum(jnp.where(is_a, dl * glog, 0.0), axis=0, keepdims=True)
        s_p = jnp.sum(jnp.where(is_b, 0.0, dp), axis=0, keepdims=True)
        row = lax.broadcasted_iota(jnp.int32, (8, LANE), 0)
        from_a = pltpu.roll(jnp.where(row == 0, s_a, jnp.where(row == 1, s_p, 0.0)), LANE - HEADS, 1)
        from_f = pltpu.roll(jnp.where(row == 2, s_p, 0.0), LANE - 2 * HEADS, 1)
        lane = lax.broadcasted_iota(jnp.int32, (8, LANE), 1)
        sums_ref[...] = jnp.where(lane < HEADS, from_a + from_f, 0.0)

    return pl.pallas_call(
        body, name="gates_bwd",
        out_shape=[jax.ShapeDtypeStruct((t, LANE), F32), jax.ShapeDtypeStruct((8, LANE), F32)],
        scratch_shapes=[pltpu.VMEM((t, LANE), F32), pltpu.VMEM((t, LANE), F32), pltpu.VMEM((LANE, LANE), F32)],
        compiler_params=_params(),
    )(ps, gparams, dcol_g, dtot_g, dcol_f, drow_g, drow_f)


def _conv(xv, w):
    acc = w[3:4, :] * xv
    for s in range(1, 4):
        acc += w[3 - s:4 - s, :] * _shift_down(xv, s)
    return acc


PREP_ROWS = 256
HALO = 8
PREP_UNROLL = 8


def _tile_loop(n, tile, init):
    if n % PREP_UNROLL:
        return lax.fori_loop(0, n, tile, init)

    def trip(g, carry):
        for u in range(PREP_UNROLL):
            carry = tile(g * PREP_UNROLL + u, carry)
        return carry

    return lax.fori_loop(0, n // PREP_UNROLL, trip, init)


def _tile_rows(r):
    return pl.ds(pl.multiple_of(r * PREP_ROWS, PREP_ROWS), PREP_ROWS)


def _gdn_prep(pm, conv_w):
    t = pm.shape[0]
    nj = WIDTH // LANE
    win = PREP_ROWS + HALO

    def body(x_ref, w_ref, o_ref, xp_ref):
        kind = pl.program_id(0)
        xp_ref[0:HALO, :] = jnp.zeros((HALO, LANE), F32)
        xp_ref[HALO:, :] = x_ref[...]
        w = w_ref[...]
        hs = _head_sum_matrix()

        def tile(r, _, normed):
            xw = xp_ref[pl.ds(pl.multiple_of(r * PREP_ROWS, PREP_ROWS), win), :]
            acc = _conv(xw, w)[HALO:]
            out = acc * _sigmoid(acc)
            if normed:
                out = out * lax.rsqrt(_group_sum(out * out, hs) + EPS)
            o_ref[0, 0, _tile_rows(r), :] = out[:, :DH]
            o_ref[0, 1, _tile_rows(r), :] = out[:, DH:]
            return 0

        @pl.when(kind < 2)
        def _():
            _tile_loop(t // PREP_ROWS, functools.partial(tile, normed=True), 0)

        @pl.when(kind == 2)
        def _():
            _tile_loop(t // PREP_ROWS, functools.partial(tile, normed=False), 0)

    return pl.pallas_call(
        body, name="gdn_prep", grid=(3, nj),
        in_specs=[pl.BlockSpec((t, LANE), lambda i, j: (0, i * nj + j)),
                  pl.BlockSpec((4, LANE), lambda i, j: (0, i * nj + j))],
        out_specs=pl.BlockSpec((1, 2, t, DH), lambda i, j: (i, j, 0, 0)),
        out_shape=jax.ShapeDtypeStruct((3, HEADS, t, DH), F32),
        scratch_shapes=[pltpu.VMEM((t + HALO, LANE), F32)],
        compiler_params=_params(("arbitrary", "arbitrary")),
    )(pm, conv_w)


def _gdn_prep_bwd(pm, conv_w, dqkv):
    t = pm.shape[0]
    nj = WIDTH // LANE
    win = PREP_ROWS + 2 * HALO

    def body(x_ref, w_ref, d_ref, dx_ref, dw_ref, xp_ref, dp_ref):
        kind = pl.program_id(0)
        zeros = jnp.zeros((HALO, LANE), F32)
        for ref in (xp_ref, dp_ref):
            ref[0:HALO, :] = zeros
            ref[HALO + t:, :] = zeros
        xp_ref[HALO:HALO + t, :] = x_ref[...]
        dp_ref[HALO:HALO + t, 0:DH] = d_ref[0, 0]
        dp_ref[HALO:HALO + t, DH:] = d_ref[0, 1]
        w = w_ref[...]
        hs = _head_sum_matrix()
        rows = lax.broadcasted_iota(jnp.int32, (win, LANE), 0)
        in_tile = (rows >= HALO) & (rows < HALO + PREP_ROWS)

        def tile(r, dw, normed):
            start = pl.multiple_of(r * PREP_ROWS, PREP_ROWS)
            xw = xp_ref[pl.ds(start, win), :]
            dy = dp_ref[pl.ds(start, win), :]
            acc = _conv(xw, w)
            sg = _sigmoid(acc)
            if normed:
                y = acc * sg
                rn = lax.rsqrt(_group_sum(y * y, hs) + EPS)
                yn = y * rn
                dy = rn * (dy - yn * _group_sum(dy * yn, hs))
            dacc = dy * sg * (1.0 + acc * (1.0 - sg))
            dx = w[3:4, :] * dacc
            for s in range(1, 4):
                dx += w[3 - s:4 - s, :] * _shift_up(dacc, s)
            dx_ref[_tile_rows(r), :] = dx[HALO:HALO + PREP_ROWS].astype(dx_ref.dtype)
            dm = jnp.where(in_tile, dacc, 0.0)
            return tuple(dw[i] + jnp.sum(dm * (xw if i == 3 else _shift_down(xw, 3 - i)), axis=0, keepdims=True)
                         for i in range(4))

        def run(normed):
            dw = _tile_loop(t // PREP_ROWS, functools.partial(tile, normed=normed),
                            tuple(jnp.zeros((1, LANE), F32) for _ in range(4)))
            for i in range(4):
                dw_ref[i:i + 1, :] = dw[i]

        pl.when(kind < 2)(functools.partial(run, True))
        pl.when(kind == 2)(functools.partial(run, False))

    return pl.pallas_call(
        body, name="gdn_prep_bwd", grid=(3, nj),
        in_specs=[pl.BlockSpec((t, LANE), lambda i, j: (0, i * nj + j)),
                  pl.BlockSpec((4, LANE), lambda i, j: (0, i * nj + j)),
                  pl.BlockSpec((1, 2, t, DH), lambda i, j: (i, j, 0, 0))],
        out_specs=[pl.BlockSpec((t, LANE), lambda i, j: (0, i * nj + j)),
                   pl.BlockSpec((4, LANE), lambda i, j: (0, i * nj + j))],
        out_shape=[jax.ShapeDtypeStruct((t, 3 * WIDTH), MXU), jax.ShapeDtypeStruct((4, 3 * WIDTH), F32)],
        scratch_shapes=[pltpu.VMEM((t + 2 * HALO, LANE), F32), pltpu.VMEM((t + 2 * HALO, LANE), F32)],
        compiler_params=_params(("arbitrary", "arbitrary")),
    )(pm, conv_w, dqkv)


FOX_COL0 = 4 * WIDTH // LANE


def _fox_prep(pm, nw):
    t = pm.shape[0]
    nj = WIDTH // LANE

    def body(x_ref, w_ref, o_ref):
        kind = pl.program_id(0)
        hs = _head_sum_matrix()
        wk = w_ref[pl.ds(kind, 1), :]

        def tile(r, _, normed):
            out = x_ref[_tile_rows(r), :]
            if normed:
                out = out * lax.rsqrt(_group_sum(out * out, hs) * (1.0 / DH) + EPS) * wk
            o_ref[0, 0, _tile_rows(r), :] = out[:, :DH]
            o_ref[0, 1, _tile_rows(r), :] = out[:, DH:]
            return 0

        @pl.when(kind < 2)
        def _():
            _tile_loop(t // PREP_ROWS, functools.partial(tile, normed=True), 0)

        @pl.when(kind == 2)
        def _():
            _tile_loop(t // PREP_ROWS, functools.partial(tile, normed=False), 0)

    return pl.pallas_call(
        body, name="fox_prep", grid=(3, nj),
        in_specs=[pl.BlockSpec((t, LANE), lambda i, j: (0, FOX_COL0 + i * nj + j)),
                  pl.BlockSpec((3, LANE), lambda i, j: (0, 0))],
        out_specs=pl.BlockSpec((1, 2, t, DH), lambda i, j: (i, j, 0, 0)),
        out_shape=jax.ShapeDtypeStruct((3, HEADS, t, DH), F32),
        compiler_params=_params(("arbitrary", "arbitrary")),
    )(pm, nw)


def _fox_prep_bwd(pm, nw, dqkv):
    t = pm.shape[0]
    nj = WIDTH // LANE

    def body(x_ref, w_ref, d_ref, dx_ref, dw_ref):
        kind = pl.program_id(0)
        hs = _head_sum_matrix()
        wk = w_ref[pl.ds(kind, 1), :]

        def tile(r, dw):
            xv = x_ref[_tile_rows(r), :]
            rn = lax.rsqrt(_group_sum(xv * xv, hs) * (1.0 / DH) + EPS)
            xn = xv * rn
            d = jnp.concatenate([d_ref[0, 0, _tile_rows(r), :], d_ref[0, 1, _tile_rows(r), :]], axis=1)
            g = d * wk
            dx_ref[_tile_rows(r), :] = (rn * (g - xn * _group_sum(g * xn, hs) * (1.0 / DH))).astype(dx_ref.dtype)
            return dw + jnp.sum(d * xn, axis=0, keepdims=True)

        def copy_tile(r, _):
            dx_ref[_tile_rows(r), :] = jnp.concatenate([d_ref[0, 0, _tile_rows(r), :], d_ref[0, 1, _tile_rows(r), :]],
                                                       axis=1).astype(dx_ref.dtype)
            return 0

        @pl.when(kind < 2)
        def _():
            dw_ref[0, 0] = _tile_loop(t // PREP_ROWS, tile, jnp.zeros((1, LANE), F32))

        @pl.when(kind == 2)
        def _():
            _tile_loop(t // PREP_ROWS, copy_tile, 0)
            dw_ref[0, 0] = jnp.zeros((1, LANE), F32)

    return pl.pallas_call(
        body, name="fox_prep_bwd", grid=(3, nj),
        in_specs=[pl.BlockSpec((t, LANE), lambda i, j: (0, FOX_COL0 + i * nj + j)),
                  pl.BlockSpec((3, LANE), lambda i, j: (0, 0)),
                  pl.BlockSpec((1, 2, t, DH), lambda i, j: (i, j, 0, 0))],
        out_specs=[pl.BlockSpec((t, LANE), lambda i, j: (0, i * nj + j)),
                   pl.BlockSpec((1, 1, 1, LANE), lambda i, j: (i, j, 0, 0))],
        out_shape=[jax.ShapeDtypeStruct((t, 3 * WIDTH), MXU), jax.ShapeDtypeStruct((3, nj, 1, LANE), F32)],
        compiler_params=_params(("arbitrary", "arbitrary")),
    )(pm, nw, dqkv)


SC = 256
CPS = SC // CHUNK
GDN_HP_FWD = 4
GDN_HP_BWD = 2
Q_SCALE = DH ** -0.5


def _sc_masks():
    ri = lax.broadcasted_iota(jnp.int32, (SC, SC), 0)
    ci = lax.broadcasted_iota(jnp.int32, (SC, SC), 1)
    same = (ri // CHUNK) == (ci // CHUNK)
    return same & (ri >= ci), same & (ri > ci), ri == ci


def _unit_lower_inverses(ms, eye):
    invs = [jnp.where(eye, 1.0, 0.0) + m for m in ms]
    ms = [_dot(m, m) for m in ms]
    for _ in range(4):
        both = [_dot(jnp.concatenate([inv, m], axis=0), m) for inv, m in zip(invs, ms)]
        invs = [inv + b[:SC] for inv, b in zip(invs, both)]
        ms = [b[SC:] for b in both]
    return [inv + _dot(inv, m) for inv, m in zip(invs, ms)]


def _lane_col(blk, lane_idx):
    lane = lax.broadcasted_iota(jnp.int32, blk.shape, 1)
    return jnp.sum(jnp.where(lane == lane_idx, blk, 0.0), axis=1, keepdims=True)


def _to_lane(col, lane_idx):
    lane = lax.broadcasted_iota(jnp.int32, (col.shape[0], LANE), 1)
    return jnp.where(lane == lane_idx, col, 0.0)


def _gdn_columns(gates_ref, run_ref, tot_ref, runt_ref, rows, h):
    return (_lane_col(gates_ref[rows, :], h), _lane_col(run_ref[rows, :], HEADS + h),
            _lane_col(tot_ref[rows, :], HEADS + h), runt_ref[pl.ds(h, 1), rows])


def _gdn_local(q, k, beta, gc, gl, grow, causal, with_kk=True):
    decay = jnp.exp(jnp.where(causal, gc - grow, -1e30))
    egc = jnp.exp(gc)
    ekd = jnp.exp(gl - gc)
    qs = q * Q_SCALE
    kb = k * beta
    if with_kk:
        both = _dot_nt(jnp.concatenate([kb, qs], axis=0), k)
        kk, qk = both[:SC], both[SC:]
    else:
        kk, qk = None, _dot_nt(qs, k)
    return beta, gl, decay, egc, ekd, qs, kb, kk, qk, jnp.where(causal, qk * decay, 0.0)


def _chunk_rows(c):
    return pl.ds(c * CHUNK if isinstance(c, int) else pl.multiple_of(c * CHUNK, CHUNK), CHUNK)


def _sc_rows(b):
    return pl.ds(b * SC if isinstance(b, int) else pl.multiple_of(b * SC, SC), SC)


def _gdn_fwd(qkv, gates, run, tot, run_t):
    t = qkv.shape[2]
    nc = t // CHUNK
    nsc = t // SC

    hp_n = GDN_HP_FWD
    heads = range(hp_n)

    def body(qkv_ref, gates_ref, run_ref, tot_ref, runt_ref, o_ref, st_ref, inv_ref, kc_s, qc_s, g_s, au_s):
        hp = pl.program_id(0)
        causal, strict, eye = _sc_masks()

        def local(b, _):
            rows = _sc_rows(b)
            loc = [_gdn_local(qkv_ref[0, hh, rows, :], qkv_ref[1, hh, rows, :],
                              *_gdn_columns(gates_ref, run_ref, tot_ref, runt_ref, rows, hp * hp_n + hh), causal)
                   for hh in heads]
            invs = _unit_lower_inverses([-jnp.where(strict, l[7] * l[2], 0.0) for l in loc], eye)
            uws = []
            for hh in heads:
                beta, _, _, egc, _, _, kb, _, _, _ = loc[hh]
                inv_ref[hh, rows, :] = invs[hh].astype(inv_ref.dtype)
                uws.append(_dot(invs[hh], jnp.concatenate([qkv_ref[2, hh, rows, :] * beta, kb * egc], axis=1)))
            for hh in heads:
                _, _, _, egc, ekd, qs, _, _, _, attn = loc[hh]
                auw = _dot(attn, uws[hh])
                g_s[hh, rows, :] = qs * egc - auw[:, DH:]
                au_s[hh, rows, :] = auw[:, :DH]
                kd = qkv_ref[1, hh, rows, :] * ekd
                for j in range(CPS):
                    sl = slice(j * CHUNK, (j + 1) * CHUNK)
                    both = _dot_tn(kd[sl], uws[hh][sl])
                    kc_s[hh, b * CPS + j] = both[:, DH:]
                    qc_s[hh, b * CPS + j] = both[:, :DH]
            return 0

        def step(c, states):
            rows = _chunk_rows(c)
            tot_row = tot_ref[pl.ds(c * CHUNK, 1), :]
            new = []
            for hh in heads:
                s = states[hh]
                st_ref[hh, c] = s
                o_ref[hh, rows, :] = _dot(g_s[hh, rows, :], s) + au_s[hh, rows, :]
                egl = jnp.exp(_lane_col(tot_row, HEADS + hp * hp_n + hh))
                new.append(egl * s - _dot(kc_s[hh, c], s) + qc_s[hh, c])
            return tuple(new)

        def steps_of(b, states):
            for j in range(CPS):
                states = step(b * CPS + j, states)
            return states

        def fused(b, states):
            states = steps_of(b - 1, states)
            local(b, 0)
            return states

        local(0, 0)
        states = lax.fori_loop(1, nsc, fused, tuple(jnp.zeros((DH, DH), F32) for _ in heads))
        steps_of(nsc - 1, states)

    whole = pl.BlockSpec((t, LANE), lambda h: (0, 0))
    once = dict(pipeline_mode=pl.Buffered(1))
    return pl.pallas_call(
        body, name="gdn_fwd", grid=(HEADS // hp_n,),
        in_specs=[pl.BlockSpec((3, hp_n, t, DH), lambda h: (0, h, 0, 0), **once), whole, whole, whole,
                  pl.BlockSpec((HEADS, t), lambda h: (1, 0))],
        out_specs=[pl.BlockSpec((hp_n, t, DH), lambda h: (h, 0, 0), **once),
                   pl.BlockSpec((hp_n, nc, DH, DH), lambda h: (h, 0, 0, 0), **once),
                   pl.BlockSpec((hp_n, t, SC), lambda h: (h, 0, 0), **once)],
        out_shape=[jax.ShapeDtypeStruct((HEADS, t, DH), F32), jax.ShapeDtypeStruct((HEADS, nc, DH, DH), F32),
                   jax.ShapeDtypeStruct((HEADS, t, SC), MXU)],
        scratch_shapes=[pltpu.VMEM((hp_n, nc, DH, DH), F32), pltpu.VMEM((hp_n, nc, DH, DH), F32),
                        pltpu.VMEM((hp_n, t, DH), F32), pltpu.VMEM((hp_n, t, DH), F32)],
        compiler_params=_params(("arbitrary",)),
    )(qkv, gates, run, tot, run_t)


def _gdn_bwd(qkv, gates, run, tot, run_t, inv, states, do):
    t = qkv.shape[2]
    nc = t // CHUNK
    nsc = t // SC

    hp_n = GDN_HP_BWD
    heads = range(hp_n)

    def body(qkv_ref, gates_ref, run_ref, tot_ref, runt_ref, inv_ref, st_ref, do_ref,
             dqkv_ref, dcol_ref, dtot_ref, drow_ref, uw_s, kc_s, h_s, dsn_s):
        hp = pl.program_id(0)
        causal, strict, _ = _sc_masks()

        @pl.when(hp == 0)
        def _():
            dcol_ref[...] = jnp.zeros_like(dcol_ref)
            dtot_ref[...] = jnp.zeros_like(dtot_ref)

        def local_of(hh, rows, with_kk=True):
            return _gdn_local(qkv_ref[0, hh, rows, :], qkv_ref[1, hh, rows, :],
                              *_gdn_columns(gates_ref, run_ref, tot_ref, runt_ref, rows, hp * hp_n + hh), causal, with_kk)

        def local(b, _):
            rows = _sc_rows(b)
            loc = [local_of(hh, rows, with_kk=False) for hh in heads]
            us, ws = [], []
            for hh in heads:
                beta, _, _, egc, _, _, kb, _, _, _ = loc[hh]
                inv_b = inv_ref[hh, rows, :]
                us.append(_dot(inv_b, qkv_ref[2, hh, rows, :] * beta))
                ws.append(_dot(inv_b, kb * egc))
            gs = [loc[hh][5] * loc[hh][3] - _dot(loc[hh][9], ws[hh]) for hh in heads]
            for hh in heads:
                uw_s[hh, rows, :] = jnp.concatenate([us[hh], ws[hh]], axis=1)
                kd = qkv_ref[1, hh, rows, :] * loc[hh][4]
                dout = do_ref[hh, rows, :]
                for j in range(CPS):
                    sl = slice(j * CHUNK, (j + 1) * CHUNK)
                    kc_s[hh, b * CPS + j] = _dot_tn(kd[sl], ws[hh][sl])
                    h_s[hh, b * CPS + j] = _dot_tn(gs[hh][sl], dout[sl])
            return 0

        def step(c, dss):
            tot_row = tot_ref[pl.ds(c * CHUNK, 1), :]
            new = []
            for hh in heads:
                ds = dss[hh]
                dsn_s[hh, c] = ds
                egl = jnp.exp(_lane_col(tot_row, HEADS + hp * hp_n + hh))
                new.append(egl * ds - _dot_tn(kc_s[hh, c], ds) + h_s[hh, c])
            return tuple(new)

        def steps_of(b, dss):
            for j in reversed(range(CPS)):
                dss = step(b * CPS + j, dss)
            return dss

        def fused(i, dss):
            b = nsc - 2 - i
            dss = steps_of(b + 1, dss)
            local(b, 0)
            return dss

        local(nsc - 1, 0)
        dss = lax.fori_loop(0, nsc - 1, fused, tuple(jnp.zeros((DH, DH), F32) for _ in heads))
        steps_of(0, dss)

        def back(b, _):
            rows = _sc_rows(b)
            first = lax.broadcasted_iota(jnp.int32, (CHUNK, 1), 0) == 0
            loc = [local_of(hh, rows) for hh in heads]
            sign = jnp.where(lax.broadcasted_iota(jnp.int32, (1, LANE), 1) < DH, 1.0, -1.0)
            mid = []
            for hh in heads:
                beta, gl, decay, egc, ekd, qs, kb, kk, qk, attn = loc[hh]
                uw = uw_s[hh, rows, :]
                kd = qkv_ref[1, hh, rows, :] * ekd
                dout = do_ref[hh, rows, :]
                dg_p, dkd_p, duw_p, dgl_p = [], [], [], []
                for j in range(CPS):
                    sl = slice(j * CHUNK, (j + 1) * CHUNK)
                    s = st_ref[hh, b * CPS + j]
                    dsn = dsn_s[hh, b * CPS + j]
                    both = _dot_nt(jnp.concatenate([dsn, dout[sl]], axis=0), s)
                    dg_p.append(both[CHUNK:])
                    dsc = jnp.concatenate([dsn, -both[:CHUNK]], axis=1)
                    dkd_p.append(_dot_nt(uw[sl], dsc))
                    duw_p.append(_dot(kd[sl], dsc))
                    degl = jnp.sum(jnp.sum(s * dsn, axis=1, keepdims=True), axis=0, keepdims=True)
                    dgl_p.append(jnp.where(first, degl * jnp.exp(gl[j * CHUNK:j * CHUNK + 1, :]), 0.0))
                dg, dkd = jnp.concatenate(dg_p, axis=0), jnp.concatenate(dkd_p, axis=0)
                dod = jnp.concatenate([dout, dg], axis=1)
                da = jnp.where(causal, _dot_nt(dod, uw * sign), 0.0)
                duw = jnp.concatenate(duw_p, axis=0) + _dot_tn(attn, dod) * sign
                mid.append((kd, dg, dkd, da, duw, jnp.concatenate(dgl_p, axis=0)))
            inv_ts = [inv_ref[hh, rows, :].astype(F32).T for hh in heads]
            its = [_dot(inv_ts[hh], mid[hh][4]) for hh in heads]
            dinvs = []
            for hh in heads:
                beta, _, _, egc, _, _, kb, _, _, _ = loc[hh]
                dinvs.append(_dot_nt(mid[hh][4], jnp.concatenate([qkv_ref[2, hh, rows, :] * beta, kb * egc], axis=1)))
            half = [_dot(inv_ts[hh], dinvs[hh]) for hh in heads]
            dls = [jnp.where(strict, -_dot(half[hh], inv_ts[hh]), 0.0) for hh in heads]
            for hh in heads:
                head = hp * hp_n + hh
                beta, gl, decay, egc, ekd, qs, kb, kk, qk, attn = loc[hh]
                kd, dg, dkd, da, _, dgl_first = mid[hh]
                k, v = qkv_ref[1, hh, rows, :], qkv_ref[2, hh, rows, :]
                dvb, dkbe = its[hh][:, :DH], its[hh][:, DH:]
                dl = dls[hh]
                dlogd = (dl * kk + da * qk) * decay
                dd = jnp.concatenate([dl * decay, da * decay], axis=0)
                ddk = _dot(dd, k)
                dkb = ddk[:SC] + dkbe * egc
                dqs = ddk[SC:] + dg * egc
                dk = _dot_tn(dd, jnp.concatenate([kb, qs], axis=0)) + dkd * ekd + dkb * beta
                dkd_kd = jnp.sum(dkd * kd, axis=1, keepdims=True)
                narrow = dg * (qs * egc) + dkbe * (kb * egc)
                wide = dlogd[:, :LANE] + dlogd[:, LANE:] + jnp.concatenate([narrow, jnp.zeros((SC, LANE - DH), F32)], axis=1)
                dgc = jnp.sum(wide, axis=1, keepdims=True) - dkd_kd
                dbeta = jnp.sum(dkb * k + dvb * v, axis=1, keepdims=True)
                dqkv_ref[0, hh, rows, :] = dqs * Q_SCALE
                dqkv_ref[1, hh, rows, :] = dk
                dqkv_ref[2, hh, rows, :] = dvb * beta
                dcol_ref[rows, :] += _to_lane(dbeta, head) + _to_lane(dgc, HEADS + head)
                dtot_ref[rows, :] += _to_lane(dkd_kd + dgl_first, HEADS + head)
                drow_ref[pl.ds(head, 1), rows] = -jnp.sum(dlogd, axis=0, keepdims=True)
            return 0

        lax.fori_loop(0, nsc, back, 0)

    whole = pl.BlockSpec((t, LANE), lambda h: (0, 0))
    rowspec = pl.BlockSpec((HEADS, t), lambda h: (0, 0))
    sq = pltpu.VMEM((hp_n, nc, DH, DH), F32)
    per_head = pltpu.VMEM((hp_n, t, 2 * DH), F32)
    return pl.pallas_call(
        body, name="gdn_bwd", grid=(HEADS // hp_n,),
        in_specs=[pl.BlockSpec((3, hp_n, t, DH), lambda h: (0, h, 0, 0)), whole, whole, whole,
                  pl.BlockSpec((HEADS, t), lambda h: (1, 0)),
                  pl.BlockSpec((hp_n, t, SC), lambda h: (h, 0, 0)), pl.BlockSpec((hp_n, nc, DH, DH), lambda h: (h, 0, 0, 0)),
                  pl.BlockSpec((hp_n, t, DH), lambda h: (h, 0, 0))],
        out_specs=[pl.BlockSpec((3, hp_n, t, DH), lambda h: (0, h, 0, 0)), whole, whole, rowspec],
        out_shape=[jax.ShapeDtypeStruct((3, HEADS, t, DH), F32), jax.ShapeDtypeStruct((t, LANE), F32),
                   jax.ShapeDtypeStruct((t, LANE), F32), jax.ShapeDtypeStruct((HEADS, t), F32)],
        scratch_shapes=[per_head, sq, sq, sq],
        compiler_params=_params(("arbitrary",)),
    )(qkv, gates, run, tot, run_t, inv, states, do)


FOX_HP_FWD = 4
FOX_HP_BWD = 4


def _wide_t(a):
    r = a.shape[0]
    return jnp.concatenate([a, jnp.zeros((r, LANE - DH), F32)], axis=1).T[:DH]


def _tall_t(a):
    r = a.shape[1]
    return jnp.concatenate([a, jnp.zeros((LANE - DH, r), F32)], axis=0).T[:, :DH]


def _key_side_f(run_blk, head, qb):
    col = jnp.broadcast_to(_lane_col(run_blk, 2 * HEADS + head), (run_blk.shape[0], LANE))
    return jnp.concatenate([col] * (qb // LANE), axis=1)


def _diag_mask(qb):
    return lax.broadcasted_iota(jnp.int32, (qb, qb), 0) <= lax.broadcasted_iota(jnp.int32, (qb, qb), 1)


def _fox_fwd(qkv, run, run_t, carry=None):
    t = qkv.shape[2]
    qb = min(QB, t)
    nq = t // qb
    hp_n = FOX_HP_FWD
    c_in, c_in_specs, c_out_shape, c_out_specs, c_sem = _carry_operands(*carry) if carry else ([], [], [], [], None)
    n_c = len(c_in)

    def body(*refs):
        q_ref, k_ref, v_ref, run_ref, runt_ref = refs[:5]
        o_ref, lse_ref = refs[5 + n_c:7 + n_c]
        s0 = 7 + n_c + len(c_out_shape)
        kb_s, vt_s, raw_s, m_s, l_s, acc_s = refs[s0:s0 + 6]
        hp = pl.program_id(0)
        i = pl.program_id(1)

        if carry:
            _carry_step(carry[0], refs[5:5 + n_c], refs[7 + n_c], refs[8 + n_c], refs[-1], hp * nq + i,
                        (HEADS // hp_n) * nq)

        @pl.when(i == 0)
        def _():
            for hh in range(hp_n):
                kb_s[hh] = k_ref[0, hh].astype(MXU)
                for b in range(nq):
                    rows = slice(b * qb, (b + 1) * qb)
                    vt_s[hh, :, rows] = _wide_t(v_ref[0, hh, rows, :]).astype(MXU)

        qrows = pl.ds(pl.multiple_of(i * qb, qb), qb)
        qs = [(q_ref[0, hh] * Q_SCALE).astype(MXU) for hh in range(hp_n)]
        fq = [runt_ref[pl.ds(hp * hp_n + hh, 1), qrows] for hh in range(hp_n)]

        def block_rows(j):
            return pl.ds(pl.multiple_of(j * qb, qb), qb)

        def scores(j):
            for hh in range(hp_n):
                raw_s[j % 2, hh] = _dot_nt(kb_s[hh, block_rows(j), :], qs[hh])

        def absorb(j, diagonal):
            rows = block_rows(j)
            run_blk = run_ref[rows, :]
            stats, pv = [], []
            for hh in range(hp_n):
                m, l = m_s[hh], l_s[hh]
                st = raw_s[j % 2, hh] + (fq[hh] - _key_side_f(run_blk, hp * hp_n + hh, qb))
                if diagonal:
                    st = jnp.where(_diag_mask(qb), st, -1e30)
                m_new = jnp.maximum(m, jnp.max(st, axis=0, keepdims=True))
                p = jnp.exp(st - m_new)
                alpha = jnp.exp(m - m_new)
                stats.append((m_new, alpha * l + jnp.sum(p, axis=0, keepdims=True), alpha))
                pv.append(_dot(vt_s[hh, :, rows], p))
            for hh in range(hp_n):
                m_s[hh], l_s[hh] = stats[hh][0], stats[hh][1]
                acc_s[hh] = stats[hh][2] * acc_s[hh] + pv[hh]

        def kstep(j, _):
            scores(j + 1)
            absorb(j, False)
            return 0

        for hh in range(hp_n):
            m_s[hh] = jnp.full((1, qb), -1e30, F32)
            l_s[hh] = jnp.zeros((1, qb), F32)
            acc_s[hh] = jnp.zeros((DH, qb), F32)
        scores(0)
        lax.fori_loop(0, i, kstep, 0)
        absorb(i, True)
        for hh in range(hp_n):
            l = l_s[hh]
            o_ref[hh] = _tall_t(acc_s[hh] / l)
            lse_ref[pl.ds(hp * hp_n + hh, 1), qrows] = m_s[hh] + jnp.log(l)

    out = pl.pallas_call(
        body, name="fox_fwd", grid=(HEADS // hp_n, nq),
        in_specs=[pl.BlockSpec((1, hp_n, qb, DH), lambda h, i: (0, h, i, 0)),
                  pl.BlockSpec((1, hp_n, t, DH), lambda h, i: (1, h, 0, 0)),
                  pl.BlockSpec((1, hp_n, t, DH), lambda h, i: (2, h, 0, 0)),
                  pl.BlockSpec((t, LANE), lambda h, i: (0, 0)),
                  pl.BlockSpec((HEADS, t), lambda h, i: (2, 0))] + c_in_specs,
        out_specs=[pl.BlockSpec((hp_n, qb, DH), lambda h, i: (h, i, 0)),
                   pl.BlockSpec((HEADS, t), lambda h, i: (0, 0))] + c_out_specs,
        out_shape=[jax.ShapeDtypeStruct((HEADS, t, DH), F32), jax.ShapeDtypeStruct((HEADS, t), F32)] + c_out_shape,
        input_output_aliases={5 + j: 4 + j for j in range(n_c)},
        scratch_shapes=[pltpu.VMEM((hp_n, t, DH), MXU), pltpu.VMEM((hp_n, DH, t), MXU), pltpu.VMEM((2, hp_n, qb, qb), F32),
                        pltpu.VMEM((hp_n, 1, qb), F32), pltpu.VMEM((hp_n, 1, qb), F32), pltpu.VMEM((hp_n, DH, qb), F32)]
        + ([c_sem] if carry else []),
        compiler_params=_params(("arbitrary", "arbitrary")),
    )(qkv, qkv, qkv, run, run_t, *c_in)
    return (out[0], out[1], _carry_state(out[2:])) if carry else (out[0], out[1])


def _fox_bwd(qkv, run, run_t, o, lse, do, carry=None):
    t = qkv.shape[2]
    qb = min(QB, t)
    nq = t // qb
    hp_n = FOX_HP_BWD
    c_in, c_in_specs, c_out_shape, c_out_specs, c_sem = _carry_operands(*carry) if carry else ([], [], [], [], None)
    n_c = len(c_in)

    def body(*refs):
        q_ref, k_ref, v_ref, run_ref, runt_ref, o_ref, lse_ref, do_ref = refs[:8]
        dqkv_ref, dcol_ref, drow_ref = refs[8 + n_c:11 + n_c]
        s0 = 11 + n_c + len(c_out_shape)
        dqt_s, raw_s, dpt_s, dro_s, dk_s, dv_s, dsum_s = refs[s0:s0 + 7]
        hp = pl.program_id(0)
        j = pl.program_id(1)

        if carry:
            _carry_step(carry[0], refs[8:8 + n_c], refs[11 + n_c], refs[12 + n_c], refs[-1], hp * nq + j,
                        (HEADS // hp_n) * nq)

        @pl.when(j == 0)
        def _():
            dqt_s[...] = jnp.zeros_like(dqt_s)

        @pl.when((j == 0) & (hp == 0))
        def _():
            dcol_ref[...] = jnp.zeros_like(dcol_ref)
            drow_ref[...] = jnp.zeros_like(drow_ref)

        krows = pl.ds(pl.multiple_of(j * qb, qb), qb)
        run_blk = run_ref[krows, :]
        ones8 = jnp.ones((8, DH), MXU)
        kb, kt, vb, fk = [], [], [], []
        for hh in range(hp_n):
            kf = k_ref[0, hh]
            kb.append(kf.astype(MXU))
            kt.append(_wide_t(kf).astype(MXU))
            vb.append(v_ref[0, hh].astype(MXU))
            fk.append(_key_side_f(run_blk, hp * hp_n + hh, qb))

        def block_rows(i):
            return pl.ds(i * qb if isinstance(i, int) else pl.multiple_of(i * qb, qb), qb)

        def products(i):
            rows = block_rows(i)
            slot = i % 2
            for hh in range(hp_n):
                dout = do_ref[hh, rows, :]
                x = dout * o_ref[hh, rows, :]
                x_hi = x.astype(MXU)
                raw_s[slot, hh] = _dot_nt(kb[hh], q_ref[0, hh, rows, :] * Q_SCALE)
                dpt_s[slot, hh] = _dot_nt(vb[hh], dout)
                dro_s[slot, hh] = _dot_nt(ones8, x_hi) + _dot_nt(ones8, x - x_hi.astype(F32))

        def absorb(i, diagonal):
            rows = block_rows(i)
            slot = i % 2
            pieces = []
            for hh in range(hp_n):
                head = hp * hp_n + hh
                raw, dpt, drow = raw_s[slot, hh], dpt_s[slot, hh], dro_s[slot, hh, 0:1, :]
                off = runt_ref[pl.ds(head, 1), rows] - lse_ref[pl.ds(head, 1), rows]
                st = raw + (off - fk[hh])
                if diagonal:
                    st = jnp.where(_diag_mask(qb), st, -1e30)
                pt = jnp.exp(st)
                dst = pt * (dpt - drow)
                drow_ref[pl.ds(head, 1), rows] += jnp.sum(dst, axis=0, keepdims=True)
                folded = dst[:, 0:LANE]
                for c in range(1, qb // LANE):
                    folded = folded + dst[:, c * LANE:(c + 1) * LANE]
                pieces.append((_dot(dst, q_ref[0, hh, rows, :] * Q_SCALE), _dot(pt, do_ref[hh, rows, :]),
                               _dot(kt[hh], dst), folded))
            for hh in range(hp_n):
                dqt_s[hh, :, rows] += pieces[hh][2]
                if diagonal:
                    dk_s[hh], dv_s[hh], dsum_s[hh] = pieces[hh][0], pieces[hh][1], pieces[hh][3]
                else:
                    dk_s[hh] += pieces[hh][0]
                    dv_s[hh] += pieces[hh][1]
                    dsum_s[hh] += pieces[hh][3]

        def qstep(i, _):
            products(i + 1)
            absorb(i, False)
            return 0

        products(j)
        products(jnp.minimum(j + 1, nq - 1))
        absorb(j, True)
        lax.fori_loop(j + 1, nq - 1, qstep, 0)

        @pl.when(j < nq - 1)
        def _():
            absorb(nq - 1, False)
        for hh in range(hp_n):
            dqkv_ref[1, hh, krows, :] = dk_s[hh]
            dqkv_ref[2, hh, krows, :] = dv_s[hh]
            dcol_ref[krows, :] += _to_lane(-jnp.sum(dsum_s[hh], axis=1, keepdims=True), 2 * HEADS + hp * hp_n + hh)

        @pl.when(j == nq - 1)
        def _():
            for hh in range(hp_n):
                for b in range(nq):
                    rows = slice(b * qb, (b + 1) * qb)
                    dqkv_ref[0, hh, rows, :] = _tall_t(dqt_s[hh, :, rows]) * Q_SCALE

    once = dict(pipeline_mode=pl.Buffered(1))
    full = pl.BlockSpec((hp_n, t, DH), lambda h, j: (h, 0, 0), **once)
    rows8 = pl.BlockSpec((HEADS, t), lambda h, j: (0, 0))
    out = pl.pallas_call(
        body, name="fox_bwd", grid=(HEADS // hp_n, nq),
        in_specs=[pl.BlockSpec((1, hp_n, t, DH), lambda h, j: (0, h, 0, 0), **once),
                  pl.BlockSpec((1, hp_n, qb, DH), lambda h, j: (1, h, j, 0)),
                  pl.BlockSpec((1, hp_n, qb, DH), lambda h, j: (2, h, j, 0)),
                  pl.BlockSpec((t, LANE), lambda h, j: (0, 0), **once), pl.BlockSpec((HEADS, t), lambda h, j: (2, 0)),
                  full, rows8, full] + c_in_specs,
        out_specs=[pl.BlockSpec((3, hp_n, t, DH), lambda h, j: (0, h, 0, 0), **once),
                   pl.BlockSpec((t, LANE), lambda h, j: (0, 0)), rows8] + c_out_specs,
        out_shape=[jax.ShapeDtypeStruct((3, HEADS, t, DH), F32), jax.ShapeDtypeStruct((t, LANE), F32),
                   jax.ShapeDtypeStruct((HEADS, t), F32)] + c_out_shape,
        input_output_aliases={8 + j: 5 + j for j in range(n_c)},
        scratch_shapes=[pltpu.VMEM((hp_n, DH, t), F32), pltpu.VMEM((2, hp_n, qb, qb), F32), pltpu.VMEM((2, hp_n, qb, qb), F32),
                        pltpu.VMEM((2, hp_n, 8, qb), F32), pltpu.VMEM((hp_n, qb, DH), F32), pltpu.VMEM((hp_n, qb, DH), F32),
                        pltpu.VMEM((hp_n, qb, LANE), F32)] + ([c_sem] if carry else []),
        compiler_params=_params(("arbitrary", "arbitrary")),
    )(qkv, qkv, qkv, run, run_t, o, lse, do, *c_in)
    return (out[0], out[1], out[2], _carry_state(out[3:])) if carry else tuple(out)


Z_COL0 = 3 * WIDTH // LANE
FGATE_COL0 = 7 * WIDTH // LANE


def _gdn_post(o, pm, onw):
    t = pm.shape[0]

    def body(o_ref, z_ref, w_ref, m_ref):
        z = z_ref[...]
        sz = z * _sigmoid(z)
        w2 = jnp.concatenate([w_ref[...], w_ref[...]], axis=1)
        ov = jnp.concatenate([o_ref[0], o_ref[1]], axis=1)
        n = ov * lax.rsqrt(_group_sum(ov * ov, _head_sum_matrix()) * (1.0 / DH) + EPS) * w2
        m_ref[...] = (n * sz).astype(m_ref.dtype)

    return pl.pallas_call(
        body, name="gdn_post", grid=(WIDTH // LANE,),
        in_specs=[pl.BlockSpec((2, t, DH), lambda j: (j, 0, 0)), pl.BlockSpec((t, LANE), lambda j: (0, Z_COL0 + j)),
                  pl.BlockSpec((1, DH), lambda j: (0, 0))],
        out_specs=pl.BlockSpec((t, LANE), lambda j: (0, j)),
        out_shape=jax.ShapeDtypeStruct((t, WIDTH), MXU),
        compiler_params=_params(("arbitrary",)),
    )(o, pm, onw)


def _gdn_post_bwd(o, pm, onw, dmix):
    t = pm.shape[0]

    def body(o_ref, z_ref, w_ref, dm_ref, do_ref, dz_ref, dw_ref):
        @pl.when(pl.program_id(0) == 0)
        def _():
            dw_ref[...] = jnp.zeros_like(dw_ref)

        z = z_ref[...]
        sg = _sigmoid(z)
        sz = z * sg
        dsz = sg * (1.0 + z * (1.0 - sg))
        dm = dm_ref[...]
        hs = _head_sum_matrix()
        w2 = jnp.concatenate([w_ref[...], w_ref[...]], axis=1)
        ov = jnp.concatenate([o_ref[0], o_ref[1]], axis=1)
        r = lax.rsqrt(_group_sum(ov * ov, hs) * (1.0 / DH) + EPS)
        xn = ov * r
        dn = dm * sz
        dz_ref[...] = (dm * (xn * w2) * dsz).astype(dz_ref.dtype)
        dw2 = jnp.sum(dn * xn, axis=0, keepdims=True)
        dw_ref[...] += dw2[:, :DH] + dw2[:, DH:]
        g = dn * w2
        do = r * (g - xn * (_group_sum(g * xn, hs) * (1.0 / DH)))
        do_ref[0] = do[:, :DH]
        do_ref[1] = do[:, DH:]

    return pl.pallas_call(
        body, name="gdn_post_bwd", grid=(WIDTH // LANE,),
        in_specs=[pl.BlockSpec((2, t, DH), lambda j: (j, 0, 0)), pl.BlockSpec((t, LANE), lambda j: (0, Z_COL0 + j)),
                  pl.BlockSpec((1, DH), lambda j: (0, 0)), pl.BlockSpec((t, LANE), lambda j: (0, j))],
        out_specs=[pl.BlockSpec((2, t, DH), lambda j: (j, 0, 0)), pl.BlockSpec((t, LANE), lambda j: (0, j)),
                   pl.BlockSpec((1, DH), lambda j: (0, 0))],
        out_shape=[jax.ShapeDtypeStruct((HEADS, t, DH), F32), jax.ShapeDtypeStruct((t, WIDTH), MXU),
                   jax.ShapeDtypeStruct((1, DH), F32)],
        compiler_params=_params(("arbitrary",)),
    )(o, pm, onw, dmix)


def _fox_post(o, pm):
    t = pm.shape[0]

    def body(o_ref, g_ref, m_ref):
        m_ref[...] = (jnp.concatenate([o_ref[0], o_ref[1]], axis=1) * _sigmoid(g_ref[...])).astype(m_ref.dtype)

    return pl.pallas_call(
        body, name="fox_post", grid=(WIDTH // LANE,),
        in_specs=[pl.BlockSpec((2, t, DH), lambda j: (j, 0, 0)), pl.BlockSpec((t, LANE), lambda j: (0, FGATE_COL0 + j))],
        out_specs=pl.BlockSpec((t, LANE), lambda j: (0, j)),
        out_shape=jax.ShapeDtypeStruct((t, WIDTH), MXU),
        compiler_params=_params(("arbitrary",)),
    )(o, pm)


def _fox_post_bwd(o, pm, dmix):
    t = pm.shape[0]

    def body(o_ref, g_ref, dm_ref, do_ref, dg_ref):
        sg = _sigmoid(g_ref[...])
        dm = dm_ref[...]
        for hh in range(2):
            cols = slice(hh * DH, (hh + 1) * DH)
            do_ref[hh] = dm[:, cols] * sg[:, cols]
            dg_ref[:, cols] = (dm[:, cols] * o_ref[hh] * (sg * (1.0 - sg))[:, cols]).astype(dg_ref.dtype)

    return pl.pallas_call(
        body, name="fox_post_bwd", grid=(WIDTH // LANE,),
        in_specs=[pl.BlockSpec((2, t, DH), lambda j: (j, 0, 0)), pl.BlockSpec((t, LANE), lambda j: (0, FGATE_COL0 + j)),
                  pl.BlockSpec((t, LANE), lambda j: (0, j))],
        out_specs=[pl.BlockSpec((2, t, DH), lambda j: (j, 0, 0)), pl.BlockSpec((t, LANE), lambda j: (0, j))],
        out_shape=[jax.ShapeDtypeStruct((HEADS, t, DH), F32), jax.ShapeDtypeStruct((t, WIDTH), MXU)],
        compiler_params=_params(("arbitrary",)),
    )(o, pm, dmix)


def _tail(x, mixg, mixf, tgt, wo, n2w, wg_t, wu_t, wd, fw):
    t, d = x.shape
    dff = wd.shape[0]
    tb = min(TB, t)

    def body(x_ref, mg_ref, mf_ref, t_ref, wo_ref, n2_ref, wg_ref, wu_ref, wd_ref, fw_ref,
             h2_ref, act_ref, dgate_ref, dup_ref, dx3_ref, dx2_ref, dmg_ref, dmf_ref, dn2_ref, dfw_ref, loss_ref):
        @pl.when(pl.program_id(0) == 0)
        def _():
            dn2_ref[...] = jnp.zeros_like(dn2_ref)
            dfw_ref[...] = jnp.zeros_like(dfw_ref)
            loss_ref[...] = jnp.zeros_like(loss_ref)

        x2 = x_ref[...] + _dot(mg_ref[...], wo_ref[0:WIDTH, :]) + _dot(mf_ref[...], wo_ref[WIDTH:2 * WIDTH, :])
        r2 = lax.rsqrt(jnp.mean(x2 * x2, axis=-1, keepdims=True) + EPS)
        xn2 = x2 * r2
        h2 = (xn2 * n2_ref[...]).astype(MXU)
        h2_ref[...] = h2
        gate = _dot_nt(h2, wg_ref[...])
        up = _dot_nt(h2, wu_ref[...])
        sg = _sigmoid(gate)
        sl = gate * sg
        act = (sl * up).astype(MXU)
        act_ref[...] = act
        x3 = x2 + _dot(act, wd_ref[...])
        r3 = lax.rsqrt(jnp.mean(x3 * x3, axis=-1, keepdims=True) + EPS)
        xn3 = x3 * r3
        err = xn3 * fw_ref[...] - t_ref[...]
        loss_ref[...] += 0.5 * jnp.sum(jnp.mean(err * err, axis=-1, keepdims=True), axis=0, keepdims=True)
        dy = err * (1.0 / d)
        dfw_ref[...] += jnp.sum(dy * xn3, axis=0, keepdims=True)
        g3 = dy * fw_ref[...]
        dx3 = r3 * (g3 - xn3 * jnp.mean(g3 * xn3, axis=-1, keepdims=True))
        dx3_ref[...] = dx3.astype(MXU)
        dact = _dot_nt(dx3, wd_ref[...])
        dgate = (dact * up * (sg * (1.0 + gate * (1.0 - sg)))).astype(MXU)
        dup = (dact * sl).astype(MXU)
        dgate_ref[...] = dgate
        dup_ref[...] = dup
        dh2 = _dot(dgate, wg_ref[...]) + _dot(dup, wu_ref[...])
        dn2_ref[...] += jnp.sum(dh2 * xn2, axis=0, keepdims=True)
        g2 = dh2 * n2_ref[...]
        dx2 = dx3 + r2 * (g2 - xn2 * jnp.mean(g2 * xn2, axis=-1, keepdims=True))
        dx2_ref[...] = dx2
        dmg_ref[...] = _dot_nt(dx2, wo_ref[0:WIDTH, :])
        dmf_ref[...] = _dot_nt(dx2, wo_ref[WIDTH:2 * WIDTH, :])

    def tok(n):
        return pl.BlockSpec((tb, n), lambda i: (i, 0))

    acc = pl.BlockSpec((1, d), lambda i: (0, 0))
    sds = jax.ShapeDtypeStruct
    return pl.pallas_call(
        body, name="tail", grid=(t // tb,),
        in_specs=[tok(d), tok(WIDTH), tok(WIDTH), tok(d), _resident(wo.shape), _resident((1, d)),
                  _resident(wg_t.shape), _resident(wu_t.shape), _resident(wd.shape), _resident((1, d))],
        out_specs=[tok(d), tok(dff), tok(dff), tok(dff), tok(d), tok(d), tok(WIDTH), tok(WIDTH), acc, acc,
                   pl.BlockSpec((1, 1), lambda i: (0, 0))],
        out_shape=[sds((t, d), MXU), sds((t, dff), MXU), sds((t, dff), MXU), sds((t, dff), MXU), sds((t, d), MXU),
                   sds((t, d), F32), sds((t, WIDTH), F32), sds((t, WIDTH), F32), sds((1, d), F32), sds((1, d), F32),
                   sds((1, 1), F32)],
        compiler_params=_params(("arbitrary",)),
    )(x, mixg, mixf, tgt, wo, n2w, wg_t, wu_t, wd, fw)


def _wgrads(a_list, b, name):
    t, n = b.shape
    ms = [a.shape[1] for a in a_list]
    bms = [256 if m % 256 == 0 else LANE for m in ms]
    nbs = [m // bm for m, bm in zip(ms, bms)]
    k = len(a_list)
    cast = b.dtype != jnp.dtype(MXU)

    def body(*refs):
        a_refs, b_ref, o_refs = refs[:k], refs[k], refs[k + 1:2 * k + 1]
        i = pl.program_id(0)
        if cast:
            @pl.when(i == 0)
            def _():
                refs[-1][...] = b_ref[...].astype(MXU)
        for a_ref, o_ref, nb in zip(a_refs, o_refs, nbs):
            @pl.when(i < nb)
            def _():
                o_ref[...] = _dot_tn(a_ref[...], refs[-1][...] if cast else b_ref[...]).astype(o_ref.dtype)

    def clamp(nb):
        return lambda i: jnp.minimum(i, nb - 1)

    return pl.pallas_call(
        body, name=name, grid=(max(nbs),),
        in_specs=[pl.BlockSpec((t, bm), lambda i, c=clamp(nb): (0, c(i))) for bm, nb in zip(bms, nbs)] + [_resident((t, n))],
        out_specs=[pl.BlockSpec((bm, n), lambda i, c=clamp(nb): (c(i), 0)) for bm, nb in zip(bms, nbs)],
        out_shape=[jax.ShapeDtypeStruct((m, n), WIRE) for m in ms],
        scratch_shapes=[pltpu.VMEM((t, n), MXU)] if cast else [],
        compiler_params=_params(("arbitrary",)),
    )(*a_list, b)


def _merge_dw_in(d_gdn, d_z, d_fox, d_fg, d_small, shards=False):
    pieces = [d_gdn, d_z, d_small[:2 * HEADS], d_fox, d_fg, d_small[2 * HEADS:3 * HEADS]]
    if not shards:
        return jnp.concatenate(pieces, axis=0)
    n = sum(p.shape[0] for p in pieces) // N_DEV
    out = []
    for dev in range(N_DEV):
        parts, first = [], 0
        for p in pieces:
            lo, hi = max(dev * n, first), min((dev + 1) * n, first + p.shape[0])
            if lo < hi:
                parts.append(p[lo - first:hi - first])
            first += p.shape[0]
        out.append(jnp.concatenate(parts, axis=0))
    return jnp.stack(out)


def _lanes(*pieces):
    v = jnp.concatenate([p.reshape(-1).astype(F32) for p in pieces])
    return jnp.pad(v, (0, LANE - v.shape[0])).reshape(1, LANE)


def _vector_params(p):
    d = p["norm1_w"].size
    gparams = jnp.concatenate([_lanes(jnp.zeros(HEADS), p["gdn_dt_bias"], p["fox_f_bias"]),
                               _lanes(jnp.zeros(HEADS), p["gdn_A_log"]), jnp.zeros((6, LANE), F32)])
    fox_nw = jnp.stack([jnp.tile(p["fox_q_norm_w"].reshape(-1), 2), jnp.tile(p["fox_k_norm_w"].reshape(-1), 2),
                        jnp.ones((LANE,), F32)])
    return dict(n1w=p["norm1_w"].reshape(1, d), n2w=p["norm2_w"].reshape(1, d), fw=p["final_norm_w"].reshape(1, d),
                onw=p["gdn_out_norm_w"].reshape(1, DH), gparams=gparams, fox_nw=fox_nw)


def _mixer_forward(x, vp, w_t, ws_t, conv_w, carry=None, carry_first=None):
    h1, pm, ps, *first = _inproj(x, vp["n1w"], w_t, ws_t, carry_first)
    gates, run, tot, run_t = _gates(ps, vp["gparams"])
    fqkv = _fox_prep(pm, vp["fox_nw"])
    o_fox, lse, *carried = _fox_fwd(fqkv, run, run_t, carry)
    if carry:
        pm, o_fox = lax.optimization_barrier((pm, o_fox))
    mixf = _fox_post(o_fox, pm)
    gqkv = _gdn_prep(pm, conv_w)
    o_gdn, states, inv = _gdn_fwd(gqkv, gates, run, tot, run_t)
    mixg = _gdn_post(o_gdn, pm, vp["onw"])
    return dict(h1=h1, pm=pm, ps=ps, gates=gates, run=run, tot=tot, run_t=run_t, gqkv=gqkv, o_gdn=o_gdn, states=states,
                inv=inv, mixg=mixg, fqkv=fqkv, o_fox=o_fox, lse=lse, mixf=mixf, carried=carried[0] if carried else None,
                carried_first=first[0] if first else None)


def _mixer_backward(x, vp, w_t, ws_t, conv_w, f, dx2, dmixg, dmixf, carry=None, scatter_own=False):
    pm = f["pm"]
    do_fox, dfg = _fox_post_bwd(f["o_fox"], pm, dmixf)
    dfqkv, dcol_f, drow_f, *carried = _fox_bwd(f["fqkv"], f["run"], f["run_t"], f["o_fox"], f["lse"], do_fox, carry)
    dfox, dfnw = _fox_prep_bwd(pm, vp["fox_nw"], dfqkv)
    do_gdn, dz, donw = _gdn_post_bwd(f["o_gdn"], pm, vp["onw"], dmixg)
    dgqkv, dcol_g, dtot_g, drow_g = _gdn_bwd(f["gqkv"], f["gates"], f["run"], f["tot"], f["run_t"], f["inv"], f["states"], do_gdn)
    dgdn, dconv = _gdn_prep_bwd(pm, conv_w, dgqkv)
    dps, gsum = _gates_bwd(f["ps"], vp["gparams"], dcol_g, dtot_g, dcol_f, drow_g, drow_f)
    dw_pieces = _wgrads([dgdn, dz, dfox, dfg, dps], f["h1"], "dw_in")
    dw_in = _merge_dw_in(*dw_pieces)
    own = ("scatter", [_merge_dw_in(*dw_pieces, shards=True), _cut(dconv, 1)]) if scatter_own else None
    grad_x, dn1w, *sent = _inproj_bwd(x, vp["n1w"], dx2, dgdn, dz, dfox, dfg, dps, w_t, ws_t, own)
    small = dict(dn1w=dn1w, gsum=gsum, donw=donw, dfnw=dfnw)
    return (grad_x, dw_in, dconv, small, *carried, *sent)


VECTORS = ("norm1_w", "norm2_w", "final_norm_w", "gdn_A_log", "gdn_dt_bias", "gdn_out_norm_w", "fox_f_bias",
           "fox_q_norm_w", "fox_k_norm_w")
VEC_ROWS = 16
LOSS_ROW = len(VECTORS)


def _pack_vectors(dn1w, dn2w, dfw, gsum, donw, dfnw, loss):
    d = dn1w.shape[1]

    def body(n1_ref, n2_ref, fw_ref, gs_ref, on_ref, fn_ref, loss_ref, o_ref):
        o_ref[...] = jnp.zeros_like(o_ref)
        o_ref[0:1, :] = n1_ref[...]
        o_ref[1:2, :] = n2_ref[...]
        o_ref[2:3, :] = fw_ref[...]
        o_ref[3:4, 0:HEADS] = gs_ref[0:1, 0:HEADS]
        o_ref[4:5, 0:HEADS] = gs_ref[1:2, 0:HEADS]
        o_ref[5:6, 0:DH] = on_ref[...]
        o_ref[6:7, 0:HEADS] = gs_ref[2:3, 0:HEADS]
        for kind in range(2):
            v = fn_ref[kind, 0]
            for j in range(1, fn_ref.shape[1]):
                v = v + fn_ref[kind, j]
            o_ref[7 + kind:8 + kind, 0:DH] = v[:, :DH] + v[:, DH:]
        o_ref[LOSS_ROW:LOSS_ROW + 1, 0:1] = loss_ref[...]

    return pl.pallas_call(body, name="pack_vectors", out_shape=jax.ShapeDtypeStruct((VEC_ROWS, d), F32),
                          compiler_params=_params())(dn1w, dn2w, dfw, gsum, donw, dfnw, loss)


def _late_grads(f, h2, act, dgate, dup, dx3, dx2):
    dw_gate, dw_up = _wgrads([dgate, dup], h2, "dw_gate_up")
    (dw_down,) = _wgrads([act], dx3, "dw_down")
    return {"w_out": jnp.concatenate(_wgrads([f["mixg"], f["mixf"]], dx2, "dw_out"), axis=0),
            "w_ffn_gate": dw_gate, "w_ffn_up": dw_up, "w_ffn_down": dw_down}


def _local_step(x, tgt, p, w_in_t, conv_w, wo, wg_t, wu_t, wd):
    vp = _vector_params(p)
    ws_t = _small_rows(w_in_t)
    f = _mixer_forward(x, vp, w_in_t, ws_t, conv_w)
    (h2, act, dgate, dup, dx3, dx2, dmixg, dmixf, dn2w, dfw, loss) = _tail(
        x, f["mixg"], f["mixf"], tgt, wo, vp["n2w"], wg_t, wu_t, wd, vp["fw"])
    grad_x, dw_in, dconv, small = _mixer_backward(x, vp, w_in_t, ws_t, conv_w, f, dx2, dmixg, dmixf)
    grads = {"w_in": dw_in, "gdn_conv_w": dconv, **_late_grads(f, h2, act, dgate, dup, dx3, dx2)}
    vec = _pack_vectors(small["dn1w"], dn2w, dfw, small["gsum"], small["donw"], small["dfnw"], loss)
    return grad_x[0], grads, vec


def _my_place():
    return lax.axis_index("x"), lax.axis_index("y"), lax.axis_index("c")


def _peers():
    x, y, c = _my_place()
    peers = []
    for k in range(1, N_DEV):
        px = 1 - x if k & 4 else x
        py = 1 - y if k & 2 else y
        pc = 1 - c if k & 1 else c
        peers.append(((px, py, pc), 4 * px + 2 * py + pc))
    return 4 * x + 2 * y + c, peers


def _spread_copies(kind, srcs, lands, send_sems, recv_sems):
    me, peers = _peers()
    kinds = [kind] * len(srcs) if isinstance(kind, str) else kind
    remote, local = [], []
    for i, (kd, src, land) in enumerate(zip(kinds, srcs, lands)):
        for k, (dev, idx) in reversed(list(enumerate(peers))):
            remote.append(pltpu.make_async_remote_copy(
                src_ref=src if kd == "gather" else src.at[idx], dst_ref=land.at[me],
                send_sem=send_sems.at[i * (N_DEV - 1) + k], recv_sem=recv_sems.at[i * (N_DEV - 1) + k],
                device_id=dev, device_id_type=MESH))
        local.append((src if kd == "gather" else src.at[me], land.at[me]))
    return remote, local


def _gather_two_level(arrays, name):
    n = len(arrays)

    def body(*refs):
        srcs, lands = refs[:n], refs[n:2 * n]
        send_sems, recv_sems, local_sems = refs[2 * n:]
        x, y, c = _my_place()
        me, sibling = (x, y, c), (x, y, 1 - c)
        chips = [(1 - x, y), (x, 1 - y), (1 - x, 1 - y)]

        def index(p):
            return 4 * p[0] + 2 * p[1] + p[2]

        def copy(i, k, block, to, src=None):
            blk = lands[i].at[index(block)]
            return pltpu.make_async_remote_copy(
                src_ref=blk if src is None else src, dst_ref=blk, send_sem=send_sems.at[7 * i + k],
                recv_sem=recv_sems.at[7 * i + k], device_id=to, device_id_type=MESH)

        mine = [pltpu.make_async_copy(srcs[i], lands[i].at[index(me)], local_sems.at[i]) for i in range(n)]
        for cp in mine:
            cp.start()
        first = []
        for i in range(n):
            first.append(copy(i, 0, me, sibling, src=srcs[i]))
            first += [copy(i, 1 + j, me, (*chip, c), src=srcs[i]) for j, chip in enumerate(chips)]
        for cp in first:
            cp.start()
        passed = []
        for i in range(n):
            for j, chip in enumerate(chips):
                copy(i, 1 + j, (*chip, c), me).wait_recv()
                passed.append(copy(i, 4 + j, (*chip, c), sibling))
                passed[-1].start()
        for i in range(n):
            copy(i, 0, sibling, me).wait_recv()
            for j, chip in enumerate(chips):
                copy(i, 4 + j, (*chip, 1 - c), me).wait_recv()
        for cp in first + passed:
            cp.wait_send()
        for cp in mine:
            cp.wait()

    return pl.pallas_call(
        body, name=name,
        out_shape=[jax.ShapeDtypeStruct((N_DEV,) + a.shape, a.dtype) for a in arrays],
        in_specs=[pl.BlockSpec(memory_space=pl.ANY)] * n, out_specs=[pl.BlockSpec(memory_space=pl.ANY)] * n,
        scratch_shapes=[pltpu.SemaphoreType.DMA((7 * n,)), pltpu.SemaphoreType.DMA((7 * n,)),
                        pltpu.SemaphoreType.DMA((n,))],
    )(*arrays)


def _land_shape(kind, a):
    return (N_DEV,) + a.shape if kind == "gather" else a.shape


HBM = pl.BlockSpec(memory_space=pltpu.HBM)
SEM = pl.BlockSpec(memory_space=pltpu.SEMAPHORE)


def _hbm(a):
    return pltpu.with_memory_space_constraint(a, pltpu.HBM)


def _carry_operands(kind, arrays):
    n = len(arrays)
    kinds = [kind] * n if isinstance(kind, str) else kind
    lands = [lax.empty(_land_shape(kd, a), a.dtype) for kd, a in zip(kinds, arrays)]
    sems = [pltpu.SemaphoreType.DMA((n * (N_DEV - 1),))] * 2
    return ([_hbm(a) for a in list(arrays) + lands], [HBM] * (2 * n),
            sems + [pltpu.HBM(a.shape, a.dtype) for a in list(arrays) + lands], [SEM] * 2 + [HBM] * (2 * n),
            pltpu.SemaphoreType.DMA((n,)))


def _carry_step(kind, in_refs, send_sems, recv_sems, local_sems, step, n_steps):
    n = len(in_refs) // 2
    remote, local = _spread_copies(kind, in_refs[:n], in_refs[n:], send_sems, recv_sems)
    copies = [pltpu.make_async_copy(s, d, local_sems.at[i]) for i, (s, d) in enumerate(local)]
    per_step = -(-len(remote) // n_steps)
    for s in range(-(-len(remote) // per_step)):
        @pl.when(step == s)
        def _():
            for cp in remote[s * per_step:(s + 1) * per_step]:
                cp.start()
            if s == 0:
                for cp in copies:
                    cp.start()

    @pl.when(step == n_steps - 1)
    def _():
        for cp in copies:
            cp.wait()


def _carry_state(extra_out):
    n = (len(extra_out) - 2) // 2
    return list(extra_out[2:2 + n]), list(extra_out[2 + n:]), extra_out[0], extra_out[1]


def _spread_start(arrays, kind, name):
    n = len(arrays)

    def body(*refs):
        srcs, lands = refs[:n], refs[n:2 * n]
        send_sems, recv_sems = refs[2 * n], refs[2 * n + 1]
        token = refs[4 * n + 2]
        local_sems = refs[4 * n + 3]
        remote, local = _spread_copies(kind, srcs, lands, send_sems, recv_sems)
        for cp in remote:
            cp.start()
        copies = [pltpu.make_async_copy(s, d, local_sems.at[i]) for i, (s, d) in enumerate(local)]
        for cp in copies:
            cp.start()
        for cp in copies:
            cp.wait()
        token[...] = jnp.zeros_like(token)

    sems = (pltpu.SemaphoreType.DMA((n * (N_DEV - 1),)),) * 2
    kinds = [kind] * n if isinstance(kind, str) else kind
    lands = [lax.empty(_land_shape(kd, a), a.dtype) for kd, a in zip(kinds, arrays)]
    out = pl.pallas_call(
        body, name=name,
        out_shape=sems + tuple(pltpu.HBM(a.shape, a.dtype) for a in list(arrays) + lands)
        + (jax.ShapeDtypeStruct((8, LANE), F32),),
        in_specs=[HBM] * (2 * n), out_specs=tuple([SEM] * 2 + [HBM] * (2 * n) + [pl.BlockSpec(memory_space=pltpu.VMEM)]),
        input_output_aliases={j: 2 + j for j in range(2 * n)},
        scratch_shapes=[pltpu.SemaphoreType.DMA((n,))],
        compiler_params=pltpu.CompilerParams(has_side_effects=pltpu.SideEffectType.DATAFLOW_SIDE_EFFECTING),
    )(*[_hbm(a) for a in arrays], *[_hbm(a) for a in lands])
    return (list(out[2:2 + n]), list(out[2 + n:2 + 2 * n]), out[0], out[1]), out[-1]


def _spread_wait(state, kind, after, name):
    srcs, lands, send_sems, recv_sems = state
    n = len(srcs)
    after = list(after) if isinstance(after, (list, tuple)) else [after]

    def body(*refs):
        remote, _ = _spread_copies(kind, refs[:n], refs[n:2 * n], refs[2 * n], refs[2 * n + 1])
        for cp in remote:
            cp.wait_send()
        for cp in remote:
            cp.wait_recv()

    out = pl.pallas_call(
        body, name=name,
        out_shape=tuple(pltpu.HBM(a.shape, a.dtype) for a in srcs + lands),
        in_specs=[HBM] * (2 * n) + [SEM, SEM] + [pl.BlockSpec(memory_space=pl.ANY)] * len(after),
        out_specs=tuple([HBM] * (2 * n)),
        input_output_aliases={j: j for j in range(2 * n)},
        compiler_params=pltpu.CompilerParams(has_side_effects=pltpu.SideEffectType.DATAFLOW_SIDE_EFFECTING),
    )(*srcs, *lands, send_sems, recv_sems, *after)
    return list(out[n:])


ADAM_ROWS = 128
ADAM_COLS = 256


def _adam_math(g, w, m, v):
    nm = ADAM_B1 * m + (1.0 - ADAM_B1) * g
    nv = ADAM_B2 * v + (1.0 - ADAM_B2) * (g * g)
    m_hat = nm / (1.0 - ADAM_B1 ** ADAM_STEP)
    v_hat = nv / (1.0 - ADAM_B2 ** ADAM_STEP)
    return -ADAM_LR * (m_hat / (jnp.sqrt(v_hat) + ADAM_EPS) + ADAM_WD * w), nm, nv


def _sum_parts(p_ref):
    g = p_ref[0].astype(F32)
    for s in range(1, N_DEV):
        g = g + p_ref[s].astype(F32)
    return g


def _adam_matrix(parts, w, m, v, name):
    _, r, c = w.shape
    rb = ADAM_ROWS if r % ADAM_ROWS == 0 else r
    cb = ADAM_COLS if (rb == r and c % ADAM_COLS == 0) else c

    def body(p_ref, w_ref, m_ref, v_ref, g_ref, d_ref, nm_ref, nv_ref):
        g = _sum_parts(p_ref)
        g_ref[0] = g
        d_ref[0], nm_ref[0], nv_ref[0] = _adam_math(g, w_ref[0], m_ref[0], v_ref[0])

    blk = pl.BlockSpec((1, rb, cb), lambda i, j: (0, i, j))
    return pl.pallas_call(
        body, name=name, grid=(r // rb, c // cb),
        in_specs=[pl.BlockSpec((N_DEV, rb, cb), lambda i, j: (0, i, j)), blk, blk, blk],
        out_specs=[blk] * 4, out_shape=[jax.ShapeDtypeStruct(w.shape, F32)] * 4,
        compiler_params=_params(("arbitrary", "arbitrary")),
    )(parts, w, m, v)


def _copy_rows(a, name):
    _, r, c = a.shape
    rb = min(TB, r)

    def body(a_ref, o_ref):
        o_ref[...] = a_ref[...]

    blk = pl.BlockSpec((1, rb, c), lambda i: (0, i, 0))
    return pl.pallas_call(body, name=name, grid=(r // rb,), in_specs=[blk], out_specs=blk,
                          out_shape=jax.ShapeDtypeStruct(a.shape, a.dtype), compiler_params=_params(("arbitrary",)))(a)


def _adam_vectors(parts, ws, ms, vs):
    nv = len(ws)

    def body(*refs):
        p_ref = refs[0]
        w_refs, m_refs, v_refs = refs[1:1 + nv], refs[1 + nv:1 + 2 * nv], refs[1 + 2 * nv:1 + 3 * nv]
        outs = refs[1 + 3 * nv:]
        g_all = _sum_parts(p_ref)
        for i in range(nv):
            n = w_refs[i].shape[1]
            g = g_all[i:i + 1, 0:n]
            d, nm, nvv = _adam_math(g, w_refs[i][...], m_refs[i][...], v_refs[i][...])
            outs[i][...] = g
            outs[nv + i][...] = d
            outs[2 * nv + i][...] = nm
            outs[3 * nv + i][...] = nvv
        outs[4 * nv][...] = g_all[LOSS_ROW:LOSS_ROW + 1, 0:1]

    shapes = [jax.ShapeDtypeStruct(a.shape, F32) for a in ws]
    out = pl.pallas_call(body, name="adam_vectors", out_shape=shapes * 4 + [jax.ShapeDtypeStruct((1, 1), F32)],
                         compiler_params=_params())(parts, *ws, *ms, *vs)
    return out[:nv], out[nv:2 * nv], out[2 * nv:3 * nv], out[3 * nv:4 * nv], out[4 * nv]


MATRICES = (("w_in", 1), ("gdn_conv_w", 1), ("w_out", 0), ("w_ffn_gate", 1), ("w_ffn_up", 1), ("w_ffn_down", 0))
TRANSPOSED = ("w_in", "w_ffn_gate", "w_ffn_up")
WEIGHTS = ("norm1_w", "w_in", "gdn_conv_w", "gdn_A_log", "gdn_dt_bias", "gdn_out_norm_w", "fox_f_bias", "fox_q_norm_w",
           "fox_k_norm_w", "w_out", "norm2_w", "w_ffn_gate", "w_ffn_up", "w_ffn_down", "final_norm_w")


def _join(blocks, axis):
    _, r, c = blocks.shape
    if axis == 0:
        return blocks.reshape(N_DEV * r, c)
    return blocks.transpose(1, 0, 2).reshape(r, N_DEV * c)


def _cut(full, axis):
    r, c = full.shape
    if axis == 0:
        return full.reshape(N_DEV, r // N_DEV, c)
    return full.reshape(r, N_DEV, c // N_DEV).transpose(1, 0, 2)


def kernel(x, norm1_w, w_in, gdn_conv_w, gdn_A_log, gdn_dt_bias, gdn_out_norm_w, fox_f_bias, fox_q_norm_w, fox_k_norm_w, w_out, norm2_w, w_ffn_gate, w_ffn_up, w_ffn_down, final_norm_w, loss_target, m_norm1_w, m_w_in, m_gdn_conv_w, m_gdn_A_log, m_gdn_dt_bias, m_gdn_out_norm_w, m_fox_f_bias, m_fox_q_norm_w, m_fox_k_norm_w, m_w_out, m_norm2_w, m_w_ffn_gate, m_w_ffn_up, m_w_ffn_down, m_final_norm_w, v_norm1_w, v_w_in, v_gdn_conv_w, v_gdn_A_log, v_gdn_dt_bias, v_gdn_out_norm_w, v_fox_f_bias, v_fox_q_norm_w, v_fox_k_norm_w, v_w_out, v_norm2_w, v_w_ffn_gate, v_w_ffn_up, v_w_ffn_down, v_final_norm_w):
    w = dict(norm1_w=norm1_w, w_in=w_in, gdn_conv_w=gdn_conv_w, gdn_A_log=gdn_A_log, gdn_dt_bias=gdn_dt_bias,
             gdn_out_norm_w=gdn_out_norm_w, fox_f_bias=fox_f_bias, fox_q_norm_w=fox_q_norm_w, fox_k_norm_w=fox_k_norm_w,
             w_out=w_out, norm2_w=norm2_w, w_ffn_gate=w_ffn_gate, w_ffn_up=w_ffn_up, w_ffn_down=w_ffn_down,
             final_norm_w=final_norm_w)
    m = dict(norm1_w=m_norm1_w, w_in=m_w_in, gdn_conv_w=m_gdn_conv_w, gdn_A_log=m_gdn_A_log, gdn_dt_bias=m_gdn_dt_bias,
             gdn_out_norm_w=m_gdn_out_norm_w, fox_f_bias=m_fox_f_bias, fox_q_norm_w=m_fox_q_norm_w,
             fox_k_norm_w=m_fox_k_norm_w, w_out=m_w_out, norm2_w=m_norm2_w, w_ffn_gate=m_w_ffn_gate,
             w_ffn_up=m_w_ffn_up, w_ffn_down=m_w_ffn_down, final_norm_w=m_final_norm_w)
    v = dict(norm1_w=v_norm1_w, w_in=v_w_in, gdn_conv_w=v_gdn_conv_w, gdn_A_log=v_gdn_A_log, gdn_dt_bias=v_gdn_dt_bias,
             gdn_out_norm_w=v_gdn_out_norm_w, fox_f_bias=v_fox_f_bias, fox_q_norm_w=v_fox_q_norm_w,
             fox_k_norm_w=v_fox_k_norm_w, w_out=v_w_out, norm2_w=v_norm2_w, w_ffn_gate=v_w_ffn_gate,
             w_ffn_up=v_w_ffn_up, w_ffn_down=v_w_ffn_down, final_norm_w=v_final_norm_w)
    late = ("w_out", "w_ffn_gate", "w_ffn_up", "w_ffn_down")
    xs, tgt = x[0], loss_target[0]
    vp = _vector_params({n: w[n] for n in VECTORS})

    def rows_of(d, n):
        return d[n].transpose(0, 2, 1) if n in TRANSPOSED else d[n]

    wr, mr, vr = ({n: rows_of(d, n) for n, _ in MATRICES} for d in (w, m, v))

    w_in_blocks, conv_blocks = _gather_two_level([wr["w_in"][0].astype(WIRE), w["gdn_conv_w"][0]], "gather_in")
    w_t = _join(w_in_blocks, 0)
    conv_w = _join(conv_blocks, 1)
    early = ("w_ffn_gate",)
    rest = tuple(n for n in late if n not in early)
    f = _mixer_forward(xs, vp, w_t, _small_rows(w_t), conv_w, ("gather", [wr[n][0].astype(WIRE) for n in rest]),
                       ("gather", [wr[n][0].astype(WIRE) for n in early]))
    full = {n: _join(b, 0) for n, b in zip(early, _spread_wait(f["carried_first"], "gather", f["mixg"], "gather_early_wait"))}
    full.update({n: _join(b, 0)
                 for n, b in zip(rest, _spread_wait(f["carried"], "gather", full[early[0]], "gather_late_wait"))})

    (h2, act, dgate, dup, dx3, dx2, dmixg, dmixf, dn2w, dfw, loss) = _tail(
        xs, f["mixg"], f["mixf"], tgt, full["w_out"], vp["n2w"], full["w_ffn_gate"], full["w_ffn_up"],
        full["w_ffn_down"], vp["fw"])
    dlate = _late_grads(f, h2, act, dgate, dup, dx3, dx2)

    grad_x, dw_in, dconv, small, state_grads, state_own = _mixer_backward(
        xs, vp, w_t, _small_rows(w_t), conv_w, f, dx2, dmixg, dmixf, ("scatter", [_cut(dlate[n], 0) for n in late]),
        scatter_own=True)
    vec = _pack_vectors(small["dn1w"], dn2w, dfw, small["gsum"], small["donw"], small["dfnw"], loss)

    state_vec, token = _spread_start([vec], "gather", "vectors_start")
    parts = dict(zip(late, _spread_wait(state_grads, "scatter", token, "grads_late_wait")))
    results = [{}, {}, {}, {}]

    updated = {}

    def update(n):
        out = _adam_matrix(parts[n], wr[n], mr[n], vr[n], "adam_" + n)
        updated[n] = out[0]
        for d, a in zip(results, out):
            d[n] = a.transpose(0, 2, 1) if n in TRANSPOSED else a

    for n in late:
        update(n)
    grad_x = _copy_rows(grad_x, "grad_x_result")
    ready = [updated[n] for n in late] + [d["w_in"] for d in (wr, mr, vr)] + [grad_x]
    (parts_vec,) = _spread_wait(state_vec, "gather", ready, "vectors_wait")
    row = lambda a: a.reshape(1, -1)
    *vec_out, total_loss = _adam_vectors(parts_vec, [row(w[n]) for n in VECTORS], [row(m[n]) for n in VECTORS],
                                         [row(v[n]) for n in VECTORS])
    parts["w_in"], parts["gdn_conv_w"] = _spread_wait(state_own, "scatter", parts_vec, "own_wait")
    update("w_in")
    update("gdn_conv_w")
    for d, arrs in zip(results, vec_out):
        for n, a in zip(VECTORS, arrs):
            d[n] = a.reshape(w[n].shape)
    return (total_loss[0, 0], grad_x, *[d[n] for d in results for n in WEIGHTS])
```

```python
import functools

import jax
import jax.numpy as jnp
from jax import lax
from jax.experimental import pallas as pl
from jax.experimental.pallas import tpu as pltpu

F32 = jnp.float32
MXU = jnp.bfloat16
WIRE = jnp.bfloat16
HI = lax.Precision.HIGHEST
EPS = 1e-6

N_DEV = 8
HEADS = 8
DH = 64
WIDTH = HEADS * DH
CHUNK = 64
LANE = 128
ROW_ALIGN = 16
TB = 256
QB = 256
VMEM_LIMIT = 60 * 1024 * 1024

ADAM_LR = 0.001
ADAM_B1 = 0.9
ADAM_B2 = 0.999
ADAM_EPS = 1e-08
ADAM_WD = 0.01
ADAM_STEP = 10

MESH = pl.DeviceIdType.MESH


def _params(sem=None):
    return pltpu.CompilerParams(dimension_semantics=sem, vmem_limit_bytes=VMEM_LIMIT)


def _resident(shape):
    n = len(shape)
    return pl.BlockSpec(shape, lambda *_: (0,) * n, pipeline_mode=pl.Buffered(1))


def _dot(a, b):
    return jnp.dot(a.astype(MXU), b.astype(MXU), preferred_element_type=F32)


def _dot_nt(a, b):
    return lax.dot_general(a.astype(MXU), b.astype(MXU), (((1,), (1,)), ((), ())), preferred_element_type=F32)


def _dot_tn(a, b):
    return lax.dot_general(a.astype(MXU), b.astype(MXU), (((0,), (0,)), ((), ())), preferred_element_type=F32)


def _hdot(a, b):
    return jnp.dot(a, b, precision=HI, preferred_element_type=F32)


def _hdot_nt(a, b):
    return lax.dot_general(a, b, (((1,), (1,)), ((), ())), precision=HI, preferred_element_type=F32)


def _hdot_tn(a, b):
    return lax.dot_general(a, b, (((0,), (0,)), ((), ())), precision=HI, preferred_element_type=F32)


def _sigmoid(x):
    return 0.5 * jnp.tanh(0.5 * x) + 0.5


def _softplus(x):
    return jnp.maximum(x, 0.0) + jnp.log(1.0 + jnp.exp(-jnp.abs(x)))


def _head_sum_matrix():
    ri = lax.broadcasted_iota(jnp.int32, (LANE, LANE), 0) // DH
    ci = lax.broadcasted_iota(jnp.int32, (LANE, LANE), 1) // DH
    return (ri == ci).astype(F32)


def _group_sum(a, ones_matrix):
    hi = a.astype(jnp.bfloat16)
    lo = (a - hi.astype(F32)).astype(jnp.bfloat16)
    m = ones_matrix.astype(jnp.bfloat16)
    return jnp.dot(hi, m, preferred_element_type=F32) + jnp.dot(lo, m, preferred_element_type=F32)


def _shift_down(x, s):
    return pltpu.roll(x, s, 0)


def _shift_up(x, s):
    return pltpu.roll(x, x.shape[0] - s, 0)


ROWS_A = 4 * WIDTH
ROWS_B = ROWS_A + 2 * HEADS
ROWS_C = ROWS_B + 4 * WIDTH


def _small_rows(w_t):
    return jnp.concatenate([w_t[ROWS_A:ROWS_B], w_t[ROWS_C:], jnp.zeros((LANE - 3 * HEADS, w_t.shape[1]), w_t.dtype)])


def _inproj(x, n1w, w_t, ws_t, carry=None):
    t, d = x.shape
    tb = min(TB, t)
    c_in, c_in_specs, c_out_shape, c_out_specs, c_sem = _carry_operands(*carry) if carry else ([], [], [], [], None)
    n_c = len(c_in)

    def body(*refs):
        x_ref, nw_ref, wt_ref, ws_ref = refs[:4]
        h_ref, pm_ref, ps_ref = refs[4 + n_c:7 + n_c]
        if carry:
            _carry_step(carry[0], refs[4:4 + n_c], refs[7 + n_c], refs[8 + n_c], refs[-1], pl.program_id(0), t // tb)
        xv = x_ref[...]
        r = lax.rsqrt(jnp.mean(xv * xv, axis=-1, keepdims=True) + EPS)
        h = (xv * r * nw_ref[...]).astype(MXU)
        h_ref[...] = h
        pm_ref[:, 0:ROWS_A] = _dot_nt(h, wt_ref[0:ROWS_A, :])
        pm_ref[:, ROWS_A:2 * ROWS_A] = _dot_nt(h, wt_ref[ROWS_B:ROWS_C, :])
        ps_ref[...] = _dot_nt(h, ws_ref[...])

    out = pl.pallas_call(
        body, name="inproj", grid=(t // tb,),
        in_specs=[pl.BlockSpec((tb, d), lambda i: (i, 0)), _resident((1, d)), _resident(w_t.shape), _resident((LANE, d))]
        + c_in_specs,
        out_specs=[pl.BlockSpec((tb, d), lambda i: (i, 0)), pl.BlockSpec((tb, 2 * ROWS_A), lambda i: (i, 0)),
                   pl.BlockSpec((tb, LANE), lambda i: (i, 0))] + c_out_specs,
        out_shape=[jax.ShapeDtypeStruct((t, d), MXU), jax.ShapeDtypeStruct((t, 2 * ROWS_A), F32),
                   jax.ShapeDtypeStruct((t, LANE), F32)] + c_out_shape,
        input_output_aliases={4 + j: 5 + j for j in range(n_c)},
        scratch_shapes=[c_sem] if carry else [],
        compiler_params=_params(("arbitrary",)),
    )(x, n1w, w_t, ws_t, *c_in)
    return (out[0], out[1], out[2], _carry_state(out[3:])) if carry else tuple(out)


def _inproj_bwd(x, n1w, dx2, dgdn, dz, dfox, dfg, dps, w_t, ws_t, carry=None):
    t, d = x.shape
    tb = min(TB, t)
    w3 = 3 * WIDTH
    c_in, c_in_specs, c_out_shape, c_out_specs, c_sem = _carry_operands(*carry) if carry else ([], [], [], [], None)
    n_c = len(c_in)

    def body(*refs):
        x_ref, nw_ref, dx2_ref, dgdn_ref, dz_ref, dfox_ref, dfg_ref, dps_ref, wm_ref, ws_ref = refs[:10]
        gx_ref, dnw_ref = refs[10 + n_c:12 + n_c]
        if carry:
            _carry_step(carry[0], refs[10:10 + n_c], refs[12 + n_c], refs[13 + n_c], refs[-1], pl.program_id(0), t // tb)
        dh = _dot(dgdn_ref[...], wm_ref[0:w3, :])
        dh += _dot(dz_ref[...], wm_ref[w3:ROWS_A, :])
        dh += _dot(dfox_ref[...], wm_ref[ROWS_B:ROWS_B + w3, :])
        dh += _dot(dfg_ref[...], wm_ref[ROWS_B + w3:ROWS_C, :])
        dh += _dot(dps_ref[...], ws_ref[...])
        xv = x_ref[...]
        r = lax.rsqrt(jnp.mean(xv * xv, axis=-1, keepdims=True) + EPS)
        xn = xv * r

        @pl.when(pl.program_id(0) == 0)
        def _():
            dnw_ref[...] = jnp.zeros_like(dnw_ref)

        dnw_ref[...] += jnp.sum(dh * xn, axis=0, keepdims=True)
        g = dh * nw_ref[...]
        gx_ref[0] = dx2_ref[...] + r * (g - xn * jnp.mean(g * xn, axis=-1, keepdims=True))

    def tok(n):
        return pl.BlockSpec((tb, n), lambda i: (i, 0))

    out = pl.pallas_call(
        body, name="inproj_bwd", grid=(t // tb,),
        in_specs=[tok(d), _resident((1, d)), tok(d), tok(w3), tok(WIDTH), tok(w3), tok(WIDTH), tok(LANE),
                  _resident(w_t.shape), _resident(ws_t.shape)] + c_in_specs,
        out_specs=[pl.BlockSpec((1, tb, d), lambda i: (0, i, 0)), pl.BlockSpec((1, d), lambda i: (0, 0))] + c_out_specs,
        out_shape=[jax.ShapeDtypeStruct((1, t, d), F32), jax.ShapeDtypeStruct((1, d), F32)] + c_out_shape,
        input_output_aliases={10 + j: 4 + j for j in range(n_c)},
        scratch_shapes=[c_sem] if carry else [],
        compiler_params=_params(("arbitrary",)),
    )(x, n1w, dx2, dgdn, dz, dfox, dfg, dps, w_t, ws_t, *c_in)
    return (out[0], out[1], _carry_state(out[2:])) if carry else tuple(out)


def _gate_lanes(shape):
    lane = lax.broadcasted_iota(jnp.int32, shape, 1)
    return lane < HEADS, (lane >= HEADS) & (lane < 2 * HEADS), (lane >= 2 * HEADS) & (lane < 3 * HEADS)


def _block_masks():
    ri = lax.broadcasted_iota(jnp.int32, (LANE, LANE), 0)
    ci = lax.broadcasted_iota(jnp.int32, (LANE, LANE), 1)
    same = (ri // CHUNK) == (ci // CHUNK)
    return ((ri >= ci).astype(F32), (ri <= ci).astype(F32), (same & (ri >= ci)).astype(F32),
            (same & (ri <= ci)).astype(F32), same.astype(F32))


def _gates(ps, gparams):
    t = ps.shape[0]
    nb = t // LANE

    def body(ps_ref, gp_ref, out_ref, run_ref, tot_ref, runt_ref):
        p = ps_ref[...]
        is_b, is_a, is_f = _gate_lanes(p.shape)
        z = p + gp_ref[0:1, :]
        neg_exp_a = -jnp.exp(gp_ref[1:2, :])
        glog = neg_exp_a * _softplus(z)
        logf = -_softplus(-z)
        out_ref[...] = jnp.where(is_b, _sigmoid(p), jnp.where(is_a, glog, jnp.where(is_f, logf, 0.0)))
        tril, _, tril_c, _, same_c = _block_masks()
        off = jnp.zeros((1, LANE), F32)
        for b in range(nb):
            rows = slice(b * LANE, (b + 1) * LANE)
            blk = out_ref[rows, :]
            ga = jnp.where(is_a[:LANE], blk, 0.0)
            fb = _hdot(tril, jnp.where(is_f[:LANE], blk, 0.0)) + off
            run = fb + _hdot(tril_c, ga)
            run_ref[rows, :] = run
            runt_ref[:, rows] = run.T
            tot_ref[rows, :] = _hdot(same_c, ga)
            off = fb[LANE - 1:LANE, :]

    return pl.pallas_call(
        body, name="gates",
        out_shape=[jax.ShapeDtypeStruct((t, LANE), F32)] * 3 + [jax.ShapeDtypeStruct((LANE, t), F32)],
        compiler_params=_params(),
    )(ps, gparams)


def _gates_bwd(ps, gparams, dcol_g, dtot_g, dcol_f, drow_g, drow_f):
    t = ps.shape[0]
    nb = t // LANE

    def body(ps_ref, gp_ref, dcg_ref, dtg_ref, dcf_ref, drg_ref, drf_ref, dps_ref, sums_ref, dl_ref, d0_ref, tr_ref):
        p = ps_ref[...]
        is_b, is_a, is_f = _gate_lanes(p.shape)
        _, triu, _, triu_c, same_c = _block_masks()
        tr_ref[...] = jnp.zeros_like(tr_ref)
        off = jnp.zeros((1, LANE), F32)
        for b in reversed(range(nb)):
            rows = slice(b * LANE, (b + 1) * LANE)
            tr_ref[HEADS:2 * HEADS, :] = drg_ref[:, rows]
            tr_ref[2 * HEADS:3 * HEADS, :] = drf_ref[:, rows]
            d = dcg_ref[rows, :] + dcf_ref[rows, :] + tr_ref[...].T
            d0_ref[rows, :] = d
            dlf = _hdot(triu, jnp.where(is_f[:LANE], d, 0.0)) + off
            dla = (_hdot(triu_c, jnp.where(is_a[:LANE], d, 0.0))
                   + _hdot(same_c, jnp.where(is_a[:LANE], dtg_ref[rows, :], 0.0)))
            dl_ref[rows, :] = dlf + dla
            off = dlf[0:1, :]
        z = p + gp_ref[0:1, :]
        neg_exp_a = -jnp.exp(gp_ref[1:2, :])
        sb = _sigmoid(p)
        glog = neg_exp_a * _softplus(z)
        dl = dl_ref[...]
        dp = jnp.where(is_b, d0_ref[...] * sb * (1.0 - sb),
                       jnp.where(is_a, dl * neg_exp_a * _sigmoid(z), jnp.where(is_f, dl * _sigmoid(-z), 0.0)))
        dps_ref[...] = dp
        s_a = jnp.sum(jnp.where(is_a, dl * glog, 0.0), axis=0, keepdims=True)
        s_p = jnp.sum(jnp.where(is_b, 0.0, dp), axis=0, keepdims=True)
        row = lax.broadcasted_iota(jnp.int32, (8, LANE), 0)
        from_a = pltpu.roll(jnp.where(row == 0, s_a, jnp.where(row == 1, s_p, 0.0)), LANE - HEADS, 1)
        from_f = pltpu.roll(jnp.where(row == 2, s_p, 0.0), LANE - 2 * HEADS, 1)
        lane = lax.broadcasted_iota(jnp.int32, (8, LANE), 1)
        sums_ref[...] = jnp.where(lane < HEADS, from_a + from_f, 0.0)

    return pl.pallas_call(
        body, name="gates_bwd",
        out_shape=[jax.ShapeDtypeStruct((t, LANE), F32), jax.ShapeDtypeStruct((8, LANE), F32)],
        scratch_shapes=[pltpu.VMEM((t, LANE), F32), pltpu.VMEM((t, LANE), F32), pltpu.VMEM((LANE, LANE), F32)],
        compiler_params=_params(),
    )(ps, gparams, dcol_g, dtot_g, dcol_f, drow_g, drow_f)


def _conv(xv, w):
    acc = w[3:4, :] * xv
    for s in range(1, 4):
        acc += w[3 - s:4 - s, :] * _shift_down(xv, s)
    return acc


PREP_ROWS = 256
HALO = 8
PREP_UNROLL = 8


def _tile_loop(n, tile, init):
    if n % PREP_UNROLL:
        return lax.fori_loop(0, n, tile, init)

    def trip(g, carry):
        for u in range(PREP_UNROLL):
            carry = tile(g * PREP_UNROLL + u, carry)
        return carry

    return lax.fori_loop(0, n // PREP_UNROLL, trip, init)


def _tile_rows(r):
    return pl.ds(pl.multiple_of(r * PREP_ROWS, PREP_ROWS), PREP_ROWS)


def _gdn_prep(pm, conv_w):
    t = pm.shape[0]
    nj = WIDTH // LANE
    win = PREP_ROWS + HALO

    def body(x_ref, w_ref, o_ref, xp_ref):
        kind = pl.program_id(0)
        xp_ref[0:HALO, :] = jnp.zeros((HALO, LANE), F32)
        xp_ref[HALO:, :] = x_ref[...]
        w = w_ref[...]
        hs = _head_sum_matrix()

        def tile(r, _, normed):
            xw = xp_ref[pl.ds(pl.multiple_of(r * PREP_ROWS, PREP_ROWS), win), :]
            acc = _conv(xw, w)[HALO:]
            out = acc * _sigmoid(acc)
            if normed:
                out = out * lax.rsqrt(_group_sum(out * out, hs) + EPS)
            o_ref[0, 0, _tile_rows(r), :] = out[:, :DH]
            o_ref[0, 1, _tile_rows(r), :] = out[:, DH:]
            return 0

        @pl.when(kind < 2)
        def _():
            _tile_loop(t // PREP_ROWS, functools.partial(tile, normed=True), 0)

        @pl.when(kind == 2)
        def _():
            _tile_loop(t // PREP_ROWS, functools.partial(tile, normed=False), 0)

    return pl.pallas_call(
        body, name="gdn_prep", grid=(3, nj),
        in_specs=[pl.BlockSpec((t, LANE), lambda i, j: (0, i * nj + j)),
                  pl.BlockSpec((4, LANE), lambda i, j: (0, i * nj + j))],
        out_specs=pl.BlockSpec((1, 2, t, DH), lambda i, j: (i, j, 0, 0)),
        out_shape=jax.ShapeDtypeStruct((3, HEADS, t, DH), F32),
        scratch_shapes=[pltpu.VMEM((t + HALO, LANE), F32)],
        compiler_params=_params(("arbitrary", "arbitrary")),
    )(pm, conv_w)


def _gdn_prep_bwd(pm, conv_w, dqkv):
    t = pm.shape[0]
    nj = WIDTH // LANE
    win = PREP_ROWS + 2 * HALO

    def body(x_ref, w_ref, d_ref, dx_ref, dw_ref, xp_ref, dp_ref):
        kind = pl.program_id(0)
        zeros = jnp.zeros((HALO, LANE), F32)
        for ref in (xp_ref, dp_ref):
            ref[0:HALO, :] = zeros
            ref[HALO + t:, :] = zeros
        xp_ref[HALO:HALO + t, :] = x_ref[...]
        dp_ref[HALO:HALO + t, 0:DH] = d_ref[0, 0]
        dp_ref[HALO:HALO + t, DH:] = d_ref[0, 1]
        w = w_ref[...]
        hs = _head_sum_matrix()
        rows = lax.broadcasted_iota(jnp.int32, (win, LANE), 0)
        in_tile = (rows >= HALO) & (rows < HALO + PREP_ROWS)

        def tile(r, dw, normed):
            start = pl.multiple_of(r * PREP_ROWS, PREP_ROWS)
            xw = xp_ref[pl.ds(start, win), :]
            dy = dp_ref[pl.ds(start, win), :]
            acc = _conv(xw, w)
            sg = _sigmoid(acc)
            if normed:
                y = acc * sg
                rn = lax.rsqrt(_group_sum(y * y, hs) + EPS)
                yn = y * rn
                dy = rn * (dy - yn * _group_sum(dy * yn, hs))
            dacc = dy * sg * (1.0 + acc * (1.0 - sg))
            dx = w[3:4, :] * dacc
            for s in range(1, 4):
                dx += w[3 - s:4 - s, :] * _shift_up(dacc, s)
            dx_ref[_tile_rows(r), :] = dx[HALO:HALO + PREP_ROWS].astype(dx_ref.dtype)
            dm = jnp.where(in_tile, dacc, 0.0)
            return tuple(dw[i] + jnp.sum(dm * (xw if i == 3 else _shift_down(xw, 3 - i)), axis=0, keepdims=True)
                         for i in range(4))

        def run(normed):
            dw = _tile_loop(t // PREP_ROWS, functools.partial(tile, normed=normed),
                            tuple(jnp.zeros((1, LANE), F32) for _ in range(4)))
            for i in range(4):
                dw_ref[i:i + 1, :] = dw[i]

        pl.when(kind < 2)(functools.partial(run, True))
        pl.when(kind == 2)(functools.partial(run, False))

    return pl.pallas_call(
        body, name="gdn_prep_bwd", grid=(3, nj),
        in_specs=[pl.BlockSpec((t, LANE), lambda i, j: (0, i * nj + j)),
                  pl.BlockSpec((4, LANE), lambda i, j: (0, i * nj + j)),
                  pl.BlockSpec((1, 2, t, DH), lambda i, j: (i, j, 0, 0))],
        out_specs=[pl.BlockSpec((t, LANE), lambda i, j: (0, i * nj + j)),
                   pl.BlockSpec((4, LANE), lambda i, j: (0, i * nj + j))],
        out_shape=[jax.ShapeDtypeStruct((t, 3 * WIDTH), MXU), jax.ShapeDtypeStruct((4, 3 * WIDTH), F32)],
        scratch_shapes=[pltpu.VMEM((t + 2 * HALO, LANE), F32), pltpu.VMEM((t + 2 * HALO, LANE), F32)],
        compiler_params=_params(("arbitrary", "arbitrary")),
    )(pm, conv_w, dqkv)


FOX_COL0 = 4 * WIDTH // LANE


def _fox_prep(pm, nw):
    t = pm.shape[0]
    nj = WIDTH // LANE

    def body(x_ref, w_ref, o_ref):
        kind = pl.program_id(0)
        hs = _head_sum_matrix()
        wk = w_ref[pl.ds(kind, 1), :]

        def tile(r, _, normed):
            out = x_ref[_tile_rows(r), :]
            if normed:
                out = out * lax.rsqrt(_group_sum(out * out, hs) * (1.0 / DH) + EPS) * wk
            o_ref[0, 0, _tile_rows(r), :] = out[:, :DH]
            o_ref[0, 1, _tile_rows(r), :] = out[:, DH:]
            return 0

        @pl.when(kind < 2)
        def _():
            _tile_loop(t // PREP_ROWS, functools.partial(tile, normed=True), 0)

        @pl.when(kind == 2)
        def _():
            _tile_loop(t // PREP_ROWS, functools.partial(tile, normed=False), 0)

    return pl.pallas_call(
        body, name="fox_prep", grid=(3, nj),
        in_specs=[pl.BlockSpec((t, LANE), lambda i, j: (0, FOX_COL0 + i * nj + j)),
                  pl.BlockSpec((3, LANE), lambda i, j: (0, 0))],
        out_specs=pl.BlockSpec((1, 2, t, DH), lambda i, j: (i, j, 0, 0)),
        out_shape=jax.ShapeDtypeStruct((3, HEADS, t, DH), F32),
        compiler_params=_params(("arbitrary", "arbitrary")),
    )(pm, nw)


def _fox_prep_bwd(pm, nw, dqkv):
    t = pm.shape[0]
    nj = WIDTH // LANE

    def body(x_ref, w_ref, d_ref, dx_ref, dw_ref):
        kind = pl.program_id(0)
        hs = _head_sum_matrix()
        wk = w_ref[pl.ds(kind, 1), :]

        def tile(r, dw):
            xv = x_ref[_tile_rows(r), :]
            rn = lax.rsqrt(_group_sum(xv * xv, hs) * (1.0 / DH) + EPS)
            xn = xv * rn
            d = jnp.concatenate([d_ref[0, 0, _tile_rows(r), :], d_ref[0, 1, _tile_rows(r), :]], axis=1)
            g = d * wk
            dx_ref[_tile_rows(r), :] = (rn * (g - xn * _group_sum(g * xn, hs) * (1.0 / DH))).astype(dx_ref.dtype)
            return dw + jnp.sum(d * xn, axis=0, keepdims=True)

        def copy_tile(r, _):
            dx_ref[_tile_rows(r), :] = jnp.concatenate([d_ref[0, 0, _tile_rows(r), :], d_ref[0, 1, _tile_rows(r), :]],
                                                       axis=1).astype(dx_ref.dtype)
            return 0

        @pl.when(kind < 2)
        def _():
            dw_ref[0, 0] = _tile_loop(t // PREP_ROWS, tile, jnp.zeros((1, LANE), F32))

        @pl.when(kind == 2)
        def _():
            _tile_loop(t // PREP_ROWS, copy_tile, 0)
            dw_ref[0, 0] = jnp.zeros((1, LANE), F32)

    return pl.pallas_call(
        body, name="fox_prep_bwd", grid=(3, nj),
        in_specs=[pl.BlockSpec((t, LANE), lambda i, j: (0, FOX_COL0 + i * nj + j)),
                  pl.BlockSpec((3, LANE), lambda i, j: (0, 0)),
                  pl.BlockSpec((1, 2, t, DH), lambda i, j: (i, j, 0, 0))],
        out_specs=[pl.BlockSpec((t, LANE), lambda i, j: (0, i * nj + j)),
                   pl.BlockSpec((1, 1, 1, LANE), lambda i, j: (i, j, 0, 0))],
        out_shape=[jax.ShapeDtypeStruct((t, 3 * WIDTH), MXU), jax.ShapeDtypeStruct((3, nj, 1, LANE), F32)],
        compiler_params=_params(("arbitrary", "arbitrary")),
    )(pm, nw, dqkv)


SC = 256
CPS = SC // CHUNK
GDN_HP_FWD = 4
GDN_HP_BWD = 2
Q_SCALE = DH ** -0.5


def _sc_masks():
    ri = lax.broadcasted_iota(jnp.int32, (SC, SC), 0)
    ci = lax.broadcasted_iota(jnp.int32, (SC, SC), 1)
    same = (ri // CHUNK) == (ci // CHUNK)
    return same & (ri >= ci), same & (ri > ci), ri == ci


def _unit_lower_inverses(ms, eye):
    invs = [jnp.where(eye, 1.0, 0.0) + m for m in ms]
    ms = [_dot(m, m) for m in ms]
    for _ in range(4):
        both = [_dot(jnp.concatenate([inv, m], axis=0), m) for inv, m in zip(invs, ms)]
        invs = [inv + b[:SC] for inv, b in zip(invs, both)]
        ms = [b[SC:] for b in both]
    return [inv + _dot(inv, m) for inv, m in zip(invs, ms)]


def _lane_col(blk, lane_idx):
    lane = lax.broadcasted_iota(jnp.int32, blk.shape, 1)
    return jnp.sum(jnp.where(lane == lane_idx, blk, 0.0), axis=1, keepdims=True)


def _to_lane(col, lane_idx):
    lane = lax.broadcasted_iota(jnp.int32, (col.shape[0], LANE), 1)
    return jnp.where(lane == lane_idx, col, 0.0)


def _gdn_columns(gates_ref, run_ref, tot_ref, runt_ref, rows, h):
    return (_lane_col(gates_ref[rows, :], h), _lane_col(run_ref[rows, :], HEADS + h),
            _lane_col(tot_ref[rows, :], HEADS + h), runt_ref[pl.ds(h, 1), rows])


def _gdn_local(q, k, beta, gc, gl, grow, causal, with_kk=True):
    decay = jnp.exp(jnp.where(causal, gc - grow, -1e30))
    egc = jnp.exp(gc)
    ekd = jnp.exp(gl - gc)
    qs = q * Q_SCALE
    kb = k * beta
    if with_kk:
        both = _dot_nt(jnp.concatenate([kb, qs], axis=0), k)
        kk, qk = both[:SC], both[SC:]
    else:
        kk, qk = None, _dot_nt(qs, k)
    return beta, gl, decay, egc, ekd, qs, kb, kk, qk, jnp.where(causal, qk * decay, 0.0)


def _chunk_rows(c):
    return pl.ds(c * CHUNK if isinstance(c, int) else pl.multiple_of(c * CHUNK, CHUNK), CHUNK)


def _sc_rows(b):
    return pl.ds(b * SC if isinstance(b, int) else pl.multiple_of(b * SC, SC), SC)


def _gdn_fwd(qkv, gates, run, tot, run_t):
    t = qkv.shape[2]
    nc = t // CHUNK
    nsc = t // SC

    hp_n = GDN_HP_FWD
    heads = range(hp_n)

    def body(qkv_ref, gates_ref, run_ref, tot_ref, runt_ref, o_ref, st_ref, inv_ref, kc_s, qc_s, g_s, au_s):
        hp = pl.program_id(0)
        causal, strict, eye = _sc_masks()

        def local(b, _):
            rows = _sc_rows(b)
            loc = [_gdn_local(qkv_ref[0, hh, rows, :], qkv_ref[1, hh, rows, :],
                              *_gdn_columns(gates_ref, run_ref, tot_ref, runt_ref, rows, hp * hp_n + hh), causal)
                   for hh in heads]
            invs = _unit_lower_inverses([-jnp.where(strict, l[7] * l[2], 0.0) for l in loc], eye)
            uws = []
            for hh in heads:
                beta, _, _, egc, _, _, kb, _, _, _ = loc[hh]
                inv_ref[hh, rows, :] = invs[hh].astype(inv_ref.dtype)
                uws.append(_dot(invs[hh], jnp.concatenate([qkv_ref[2, hh, rows, :] * beta, kb * egc], axis=1)))
            for hh in heads:
                _, _, _, egc, ekd, qs, _, _, _, attn = loc[hh]
                auw = _dot(attn, uws[hh])
                g_s[hh, rows, :] = qs * egc - auw[:, DH:]
                au_s[hh, rows, :] = auw[:, :DH]
                kd = qkv_ref[1, hh, rows, :] * ekd
                for j in range(CPS):
                    sl = slice(j * CHUNK, (j + 1) * CHUNK)
                    both = _dot_tn(kd[sl], uws[hh][sl])
                    kc_s[hh, b * CPS + j] = both[:, DH:]
                    qc_s[hh, b * CPS + j] = both[:, :DH]
            return 0

        def step(c, states):
            rows = _chunk_rows(c)
            tot_row = tot_ref[pl.ds(c * CHUNK, 1), :]
            new = []
            for hh in heads:
                s = states[hh]
                st_ref[hh, c] = s
                o_ref[hh, rows, :] = _dot(g_s[hh, rows, :], s) + au_s[hh, rows, :]
                egl = jnp.exp(_lane_col(tot_row, HEADS + hp * hp_n + hh))
                new.append(egl * s - _dot(kc_s[hh, c], s) + qc_s[hh, c])
            return tuple(new)

        def steps_of(b, states):
            for j in range(CPS):
                states = step(b * CPS + j, states)
            return states

        def fused(b, states):
            states = steps_of(b - 1, states)
            local(b, 0)
            return states

        local(0, 0)
        states = lax.fori_loop(1, nsc, fused, tuple(jnp.zeros((DH, DH), F32) for _ in heads))
        steps_of(nsc - 1, states)

    whole = pl.BlockSpec((t, LANE), lambda h: (0, 0))
    once = dict(pipeline_mode=pl.Buffered(1))
    return pl.pallas_call(
        body, name="gdn_fwd", grid=(HEADS // hp_n,),
        in_specs=[pl.BlockSpec((3, hp_n, t, DH), lambda h: (0, h, 0, 0), **once), whole, whole, whole,
                  pl.BlockSpec((HEADS, t), lambda h: (1, 0))],
        out_specs=[pl.BlockSpec((hp_n, t, DH), lambda h: (h, 0, 0), **once),
                   pl.BlockSpec((hp_n, nc, DH, DH), lambda h: (h, 0, 0, 0), **once),
                   pl.BlockSpec((hp_n, t, SC), lambda h: (h, 0, 0), **once)],
        out_shape=[jax.ShapeDtypeStruct((HEADS, t, DH), F32), jax.ShapeDtypeStruct((HEADS, nc, DH, DH), F32),
                   jax.ShapeDtypeStruct((HEADS, t, SC), MXU)],
        scratch_shapes=[pltpu.VMEM((hp_n, nc, DH, DH), F32), pltpu.VMEM((hp_n, nc, DH, DH), F32),
                        pltpu.VMEM((hp_n, t, DH), F32), pltpu.VMEM((hp_n, t, DH), F32)],
        compiler_params=_params(("arbitrary",)),
    )(qkv, gates, run, tot, run_t)


def _gdn_bwd(qkv, gates, run, tot, run_t, inv, states, do):
    t = qkv.shape[2]
    nc = t // CHUNK
    nsc = t // SC

    hp_n = GDN_HP_BWD
    heads = range(hp_n)

    def body(qkv_ref, gates_ref, run_ref, tot_ref, runt_ref, inv_ref, st_ref, do_ref,
             dqkv_ref, dcol_ref, dtot_ref, drow_ref, uw_s, kc_s, h_s, dsn_s):
        hp = pl.program_id(0)
        causal, strict, _ = _sc_masks()

        @pl.when(hp == 0)
        def _():
            dcol_ref[...] = jnp.zeros_like(dcol_ref)
            dtot_ref[...] = jnp.zeros_like(dtot_ref)

        def local_of(hh, rows, with_kk=True):
            return _gdn_local(qkv_ref[0, hh, rows, :], qkv_ref[1, hh, rows, :],
                              *_gdn_columns(gates_ref, run_ref, tot_ref, runt_ref, rows, hp * hp_n + hh), causal, with_kk)

        def local(b, _):
            rows = _sc_rows(b)
            loc = [local_of(hh, rows, with_kk=False) for hh in heads]
            us, ws = [], []
            for hh in heads:
                beta, _, _, egc, _, _, kb, _, _, _ = loc[hh]
                inv_b = inv_ref[hh, rows, :]
                us.append(_dot(inv_b, qkv_ref[2, hh, rows, :] * beta))
                ws.append(_dot(inv_b, kb * egc))
            gs = [loc[hh][5] * loc[hh][3] - _dot(loc[hh][9], ws[hh]) for hh in heads]
            for hh in heads:
                uw_s[hh, rows, :] = jnp.concatenate([us[hh], ws[hh]], axis=1)
                kd = qkv_ref[1, hh, rows, :] * loc[hh][4]
                dout = do_ref[hh, rows, :]
                for j in range(CPS):
                    sl = slice(j * CHUNK, (j + 1) * CHUNK)
                    kc_s[hh, b * CPS + j] = _dot_tn(kd[sl], ws[hh][sl])
                    h_s[hh, b * CPS + j] = _dot_tn(gs[hh][sl], dout[sl])
            return 0

        def step(c, dss):
            tot_row = tot_ref[pl.ds(c * CHUNK, 1), :]
            new = []
            for hh in heads:
                ds = dss[hh]
                dsn_s[hh, c] = ds
                egl = jnp.exp(_lane_col(tot_row, HEADS + hp * hp_n + hh))
                new.append(egl * ds - _dot_tn(kc_s[hh, c], ds) + h_s[hh, c])
            return tuple(new)

        def steps_of(b, dss):
            for j in reversed(range(CPS)):
                dss = step(b * CPS + j, dss)
            return dss

        def fused(i, dss):
            b = nsc - 2 - i
            dss = steps_of(b + 1, dss)
            local(b, 0)
            return dss

        local(nsc - 1, 0)
        dss = lax.fori_loop(0, nsc - 1, fused, tuple(jnp.zeros((DH, DH), F32) for _ in heads))
        steps_of(0, dss)

        def back(b, _):
            rows = _sc_rows(b)
            first = lax.broadcasted_iota(jnp.int32, (CHUNK, 1), 0) == 0
            loc = [local_of(hh, rows) for hh in heads]
            sign = jnp.where(lax.broadcasted_iota(jnp.int32, (1, LANE), 1) < DH, 1.0, -1.0)
            mid = []
            for hh in heads:
                beta, gl, decay, egc, ekd, qs, kb, kk, qk, attn = loc[hh]
                uw = uw_s[hh, rows, :]
                kd = qkv_ref[1, hh, rows, :] * ekd
                dout = do_ref[hh, rows, :]
                dg_p, dkd_p, duw_p, dgl_p = [], [], [], []
                for j in range(CPS):
                    sl = slice(j * CHUNK, (j + 1) * CHUNK)
                    s = st_ref[hh, b * CPS + j]
                    dsn = dsn_s[hh, b * CPS + j]
                    both = _dot_nt(jnp.concatenate([dsn, dout[sl]], axis=0), s)
                    dg_p.append(both[CHUNK:])
                    dsc = jnp.concatenate([dsn, -both[:CHUNK]], axis=1)
                    dkd_p.append(_dot_nt(uw[sl], dsc))
                    duw_p.append(_dot(kd[sl], dsc))
                    degl = jnp.sum(jnp.sum(s * dsn, axis=1, keepdims=True), axis=0, keepdims=True)
                    dgl_p.append(jnp.where(first, degl * jnp.exp(gl[j * CHUNK:j * CHUNK + 1, :]), 0.0))
                dg, dkd = jnp.concatenate(dg_p, axis=0), jnp.concatenate(dkd_p, axis=0)
                dod = jnp.concatenate([dout, dg], axis=1)
                da = jnp.where(causal, _dot_nt(dod, uw * sign), 0.0)
                duw = jnp.concatenate(duw_p, axis=0) + _dot_tn(attn, dod) * sign
                mid.append((kd, dg, dkd, da, duw, jnp.concatenate(dgl_p, axis=0)))
            inv_ts = [inv_ref[hh, rows, :].astype(F32).T for hh in heads]
            its = [_dot(inv_ts[hh], mid[hh][4]) for hh in heads]
            dinvs = []
            for hh in heads:
                beta, _, _, egc, _, _, kb, _, _, _ = loc[hh]
                dinvs.append(_dot_nt(mid[hh][4], jnp.concatenate([qkv_ref[2, hh, rows, :] * beta, kb * egc], axis=1)))
            half = [_dot(inv_ts[hh], dinvs[hh]) for hh in heads]
            dls = [jnp.where(strict, -_dot(half[hh], inv_ts[hh]), 0.0) for hh in heads]
            for hh in heads:
                head = hp * hp_n + hh
                beta, gl, decay, egc, ekd, qs, kb, kk, qk, attn = loc[hh]
                kd, dg, dkd, da, _, dgl_first = mid[hh]
                k, v = qkv_ref[1, hh, rows, :], qkv_ref[2, hh, rows, :]
                dvb, dkbe = its[hh][:, :DH], its[hh][:, DH:]
                dl = dls[hh]
                dlogd = (dl * kk + da * qk) * decay
                dd = jnp.concatenate([dl * decay, da * decay], axis=0)
                ddk = _dot(dd, k)
                dkb = ddk[:SC] + dkbe * egc
                dqs = ddk[SC:] + dg * egc
                dk = _dot_tn(dd, jnp.concatenate([kb, qs], axis=0)) + dkd * ekd + dkb * beta
                dkd_kd = jnp.sum(dkd * kd, axis=1, keepdims=True)
                narrow = dg * (qs * egc) + dkbe * (kb * egc)
                wide = dlogd[:, :LANE] + dlogd[:, LANE:] + jnp.concatenate([narrow, jnp.zeros((SC, LANE - DH), F32)], axis=1)
                dgc = jnp.sum(wide, axis=1, keepdims=True) - dkd_kd
                dbeta = jnp.sum(dkb * k + dvb * v, axis=1, keepdims=True)
                dqkv_ref[0, hh, rows, :] = dqs * Q_SCALE
                dqkv_ref[1, hh, rows, :] = dk
                dqkv_ref[2, hh, rows, :] = dvb * beta
                dcol_ref[rows, :] += _to_lane(dbeta, head) + _to_lane(dgc, HEADS + head)
                dtot_ref[rows, :] += _to_lane(dkd_kd + dgl_first, HEADS + head)
                drow_ref[pl.ds(head, 1), rows] = -jnp.sum(dlogd, axis=0, keepdims=True)
            return 0

        lax.fori_loop(0, nsc, back, 0)

    whole = pl.BlockSpec((t, LANE), lambda h: (0, 0))
    rowspec = pl.BlockSpec((HEADS, t), lambda h: (0, 0))
    sq = pltpu.VMEM((hp_n, nc, DH, DH), F32)
    per_head = pltpu.VMEM((hp_n, t, 2 * DH), F32)
    return pl.pallas_call(
        body, name="gdn_bwd", grid=(HEADS // hp_n,),
        in_specs=[pl.BlockSpec((3, hp_n, t, DH), lambda h: (0, h, 0, 0)), whole, whole, whole,
                  pl.BlockSpec((HEADS, t), lambda h: (1, 0)),
                  pl.BlockSpec((hp_n, t, SC), lambda h: (h, 0, 0)), pl.BlockSpec((hp_n, nc, DH, DH), lambda h: (h, 0, 0, 0)),
                  pl.BlockSpec((hp_n, t, DH), lambda h: (h, 0, 0))],
        out_specs=[pl.BlockSpec((3, hp_n, t, DH), lambda h: (0, h, 0, 0)), whole, whole, rowspec],
        out_shape=[jax.ShapeDtypeStruct((3, HEADS, t, DH), F32), jax.ShapeDtypeStruct((t, LANE), F32),
                   jax.ShapeDtypeStruct((t, LANE), F32), jax.ShapeDtypeStruct((HEADS, t), F32)],
        scratch_shapes=[per_head, sq, sq, sq],
        compiler_params=_params(("arbitrary",)),
    )(qkv, gates, run, tot, run_t, inv, states, do)


FOX_HP_FWD = 4
FOX_HP_BWD = 4


def _wide_t(a):
    r = a.shape[0]
    return jnp.concatenate([a, jnp.zeros((r, LANE - DH), F32)], axis=1).T[:DH]


def _tall_t(a):
    r = a.shape[1]
    return jnp.concatenate([a, jnp.zeros((LANE - DH, r), F32)], axis=0).T[:, :DH]


def _key_side_f(run_blk, head, qb):
    col = jnp.broadcast_to(_lane_col(run_blk, 2 * HEADS + head), (run_blk.shape[0], LANE))
    return jnp.concatenate([col] * (qb // LANE), axis=1)


def _diag_mask(qb):
    return lax.broadcasted_iota(jnp.int32, (qb, qb), 0) <= lax.broadcasted_iota(jnp.int32, (qb, qb), 1)


def _fox_fwd(qkv, run, run_t, carry=None):
    t = qkv.shape[2]
    qb = min(QB, t)
    nq = t // qb
    hp_n = FOX_HP_FWD
    c_in, c_in_specs, c_out_shape, c_out_specs, c_sem = _carry_operands(*carry) if carry else ([], [], [], [], None)
    n_c = len(c_in)

    def body(*refs):
        q_ref, k_ref, v_ref, run_ref, runt_ref = refs[:5]
        o_ref, lse_ref = refs[5 + n_c:7 + n_c]
        s0 = 7 + n_c + len(c_out_shape)
        kb_s, vt_s, raw_s, m_s, l_s, acc_s = refs[s0:s0 + 6]
        hp = pl.program_id(0)
        i = pl.program_id(1)

        if carry:
            _carry_step(carry[0], refs[5:5 + n_c], refs[7 + n_c], refs[8 + n_c], refs[-1], hp * nq + i,
                        (HEADS // hp_n) * nq)

        @pl.when(i == 0)
        def _():
            for hh in range(hp_n):
                kb_s[hh] = k_ref[0, hh].astype(MXU)
                for b in range(nq):
                    rows = slice(b * qb, (b + 1) * qb)
                    vt_s[hh, :, rows] = _wide_t(v_ref[0, hh, rows, :]).astype(MXU)

        qrows = pl.ds(pl.multiple_of(i * qb, qb), qb)
        qs = [(q_ref[0, hh] * Q_SCALE).astype(MXU) for hh in range(hp_n)]
        fq = [runt_ref[pl.ds(hp * hp_n + hh, 1), qrows] for hh in range(hp_n)]

        def block_rows(j):
            return pl.ds(pl.multiple_of(j * qb, qb), qb)

        def scores(j):
            for hh in range(hp_n):
                raw_s[j % 2, hh] = _dot_nt(kb_s[hh, block_rows(j), :], qs[hh])

        def absorb(j, diagonal):
            rows = block_rows(j)
            run_blk = run_ref[rows, :]
            stats, pv = [], []
            for hh in range(hp_n):
                m, l = m_s[hh], l_s[hh]
                st = raw_s[j % 2, hh] + (fq[hh] - _key_side_f(run_blk, hp * hp_n + hh, qb))
                if diagonal:
                    st = jnp.where(_diag_mask(qb), st, -1e30)
                m_new = jnp.maximum(m, jnp.max(st, axis=0, keepdims=True))
                p = jnp.exp(st - m_new)
                alpha = jnp.exp(m - m_new)
                stats.append((m_new, alpha * l + jnp.sum(p, axis=0, keepdims=True), alpha))
                pv.append(_dot(vt_s[hh, :, rows], p))
            for hh in range(hp_n):
                m_s[hh], l_s[hh] = stats[hh][0], stats[hh][1]
                acc_s[hh] = stats[hh][2] * acc_s[hh] + pv[hh]

        def kstep(j, _):
            scores(j + 1)
            absorb(j, False)
            return 0

        for hh in range(hp_n):
            m_s[hh] = jnp.full((1, qb), -1e30, F32)
            l_s[hh] = jnp.zeros((1, qb), F32)
            acc_s[hh] = jnp.zeros((DH, qb), F32)
        scores(0)
        lax.fori_loop(0, i, kstep, 0)
        absorb(i, True)
        for hh in range(hp_n):
            l = l_s[hh]
            o_ref[hh] = _tall_t(acc_s[hh] / l)
            lse_ref[pl.ds(hp * hp_n + hh, 1), qrows] = m_s[hh] + jnp.log(l)

    out = pl.pallas_call(
        body, name="fox_fwd", grid=(HEADS // hp_n, nq),
        in_specs=[pl.BlockSpec((1, hp_n, qb, DH), lambda h, i: (0, h, i, 0)),
                  pl.BlockSpec((1, hp_n, t, DH), lambda h, i: (1, h, 0, 0)),
                  pl.BlockSpec((1, hp_n, t, DH), lambda h, i: (2, h, 0, 0)),
                  pl.BlockSpec((t, LANE), lambda h, i: (0, 0)),
                  pl.BlockSpec((HEADS, t), lambda h, i: (2, 0))] + c_in_specs,
        out_specs=[pl.BlockSpec((hp_n, qb, DH), lambda h, i: (h, i, 0)),
                   pl.BlockSpec((HEADS, t), lambda h, i: (0, 0))] + c_out_specs,
        out_shape=[jax.ShapeDtypeStruct((HEADS, t, DH), F32), jax.ShapeDtypeStruct((HEADS, t), F32)] + c_out_shape,
        input_output_aliases={5 + j: 4 + j for j in range(n_c)},
        scratch_shapes=[pltpu.VMEM((hp_n, t, DH), MXU), pltpu.VMEM((hp_n, DH, t), MXU), pltpu.VMEM((2, hp_n, qb, qb), F32),
                        pltpu.VMEM((hp_n, 1, qb), F32), pltpu.VMEM((hp_n, 1, qb), F32), pltpu.VMEM((hp_n, DH, qb), F32)]
        + ([c_sem] if carry else []),
        compiler_params=_params(("arbitrary", "arbitrary")),
    )(qkv, qkv, qkv, run, run_t, *c_in)
    return (out[0], out[1], _carry_state(out[2:])) if carry else (out[0], out[1])


def _fox_bwd(qkv, run, run_t, o, lse, do, carry=None):
    t = qkv.shape[2]
    qb = min(QB, t)
    nq = t // qb
    hp_n = FOX_HP_BWD
    c_in, c_in_specs, c_out_shape, c_out_specs, c_sem = _carry_operands(*carry) if carry else ([], [], [], [], None)
    n_c = len(c_in)

    def body(*refs):
        q_ref, k_ref, v_ref, run_ref, runt_ref, o_ref, lse_ref, do_ref = refs[:8]
        dqkv_ref, dcol_ref, drow_ref = refs[8 + n_c:11 + n_c]
        s0 = 11 + n_c + len(c_out_shape)
        dqt_s, raw_s, dpt_s, dro_s, dk_s, dv_s, dsum_s = refs[s0:s0 + 7]
        hp = pl.program_id(0)
        j = pl.program_id(1)

        if carry:
            _carry_step(carry[0], refs[8:8 + n_c], refs[11 + n_c], refs[12 + n_c], refs[-1], hp * nq + j,
                        (HEADS // hp_n) * nq)

        @pl.when(j == 0)
        def _():
            dqt_s[...] = jnp.zeros_like(dqt_s)

        @pl.when((j == 0) & (hp == 0))
        def _():
            dcol_ref[...] = jnp.zeros_like(dcol_ref)
            drow_ref[...] = jnp.zeros_like(drow_ref)

        krows = pl.ds(pl.multiple_of(j * qb, qb), qb)
        run_blk = run_ref[krows, :]
        ones8 = jnp.ones((8, DH), MXU)
        kb, kt, vb, fk = [], [], [], []
        for hh in range(hp_n):
            kf = k_ref[0, hh]
            kb.append(kf.astype(MXU))
            kt.append(_wide_t(kf).astype(MXU))
            vb.append(v_ref[0, hh].astype(MXU))
            fk.append(_key_side_f(run_blk, hp * hp_n + hh, qb))

        def block_rows(i):
            return pl.ds(i * qb if isinstance(i, int) else pl.multiple_of(i * qb, qb), qb)

        def products(i):
            rows = block_rows(i)
            slot = i % 2
            for hh in range(hp_n):
                dout = do_ref[hh, rows, :]
                x = dout * o_ref[hh, rows, :]
                x_hi = x.astype(MXU)
                raw_s[slot, hh] = _dot_nt(kb[hh], q_ref[0, hh, rows, :] * Q_SCALE)
                dpt_s[slot, hh] = _dot_nt(vb[hh], dout)
                dro_s[slot, hh] = _dot_nt(ones8, x_hi) + _dot_nt(ones8, x - x_hi.astype(F32))

        def absorb(i, diagonal):
            rows = block_rows(i)
            slot = i % 2
            pieces = []
            for hh in range(hp_n):
                head = hp * hp_n + hh
                raw, dpt, drow = raw_s[slot, hh], dpt_s[slot, hh], dro_s[slot, hh, 0:1, :]
                off = runt_ref[pl.ds(head, 1), rows] - lse_ref[pl.ds(head, 1), rows]
                st = raw + (off - fk[hh])
                if diagonal:
                    st = jnp.where(_diag_mask(qb), st, -1e30)
                pt = jnp.exp(st)
                dst = pt * (dpt - drow)
                drow_ref[pl.ds(head, 1), rows] += jnp.sum(dst, axis=0, keepdims=True)
                folded = dst[:, 0:LANE]
                for c in range(1, qb // LANE):
                    folded = folded + dst[:, c * LANE:(c + 1) * LANE]
                pieces.append((_dot(dst, q_ref[0, hh, rows, :] * Q_SCALE), _dot(pt, do_ref[hh, rows, :]),
                               _dot(kt[hh], dst), folded))
            for hh in range(hp_n):
                dqt_s[hh, :, rows] += pieces[hh][2]
                if diagonal:
                    dk_s[hh], dv_s[hh], dsum_s[hh] = pieces[hh][0], pieces[hh][1], pieces[hh][3]
                else:
                    dk_s[hh] += pieces[hh][0]
                    dv_s[hh] += pieces[hh][1]
                    dsum_s[hh] += pieces[hh][3]

        def qstep(i, _):
            products(i + 1)
            absorb(i, False)
            return 0

        products(j)
        products(jnp.minimum(j + 1, nq - 1))
        absorb(j, True)
        lax.fori_loop(j + 1, nq - 1, qstep, 0)

        @pl.when(j < nq - 1)
        def _():
            absorb(nq - 1, False)
        for hh in range(hp_n):
            dqkv_ref[1, hh, krows, :] = dk_s[hh]
            dqkv_ref[2, hh, krows, :] = dv_s[hh]
            dcol_ref[krows, :] += _to_lane(-jnp.sum(dsum_s[hh], axis=1, keepdims=True), 2 * HEADS + hp * hp_n + hh)

        @pl.when(j == nq - 1)
        def _():
            for hh in range(hp_n):
                for b in range(nq):
                    rows = slice(b * qb, (b + 1) * qb)
                    dqkv_ref[0, hh, rows, :] = _tall_t(dqt_s[hh, :, rows]) * Q_SCALE

    once = dict(pipeline_mode=pl.Buffered(1))
    full = pl.BlockSpec((hp_n, t, DH), lambda h, j: (h, 0, 0), **once)
    rows8 = pl.BlockSpec((HEADS, t), lambda h, j: (0, 0))
    out = pl.pallas_call(
        body, name="fox_bwd", grid=(HEADS // hp_n, nq),
        in_specs=[pl.BlockSpec((1, hp_n, t, DH), lambda h, j: (0, h, 0, 0), **once),
                  pl.BlockSpec((1, hp_n, qb, DH), lambda h, j: (1, h, j, 0)),
                  pl.BlockSpec((1, hp_n, qb, DH), lambda h, j: (2, h, j, 0)),
                  pl.BlockSpec((t, LANE), lambda h, j: (0, 0), **once), pl.BlockSpec((HEADS, t), lambda h, j: (2, 0)),
                  full, rows8, full] + c_in_specs,
        out_specs=[pl.BlockSpec((3, hp_n, t, DH), lambda h, j: (0, h, 0, 0), **once),
                   pl.BlockSpec((t, LANE), lambda h, j: (0, 0)), rows8] + c_out_specs,
        out_shape=[jax.ShapeDtypeStruct((3, HEADS, t, DH), F32), jax.ShapeDtypeStruct((t, LANE), F32),
                   jax.ShapeDtypeStruct((HEADS, t), F32)] + c_out_shape,
        input_output_aliases={8 + j: 5 + j for j in range(n_c)},
        scratch_shapes=[pltpu.VMEM((hp_n, DH, t), F32), pltpu.VMEM((2, hp_n, qb, qb), F32), pltpu.VMEM((2, hp_n, qb, qb), F32),
                        pltpu.VMEM((2, hp_n, 8, qb), F32), pltpu.VMEM((hp_n, qb, DH), F32), pltpu.VMEM((hp_n, qb, DH), F32),
                        pltpu.VMEM((hp_n, qb, LANE), F32)] + ([c_sem] if carry else []),
        compiler_params=_params(("arbitrary", "arbitrary")),
    )(qkv, qkv, qkv, run, run_t, o, lse, do, *c_in)
    return (out[0], out[1], out[2], _carry_state(out[3:])) if carry else tuple(out)


Z_COL0 = 3 * WIDTH // LANE
FGATE_COL0 = 7 * WIDTH // LANE


def _gdn_post(o, pm, onw):
    t = pm.shape[0]

    def body(o_ref, z_ref, w_ref, m_ref):
        z = z_ref[...]
        sz = z * _sigmoid(z)
        w2 = jnp.concatenate([w_ref[...], w_ref[...]], axis=1)
        ov = jnp.concatenate([o_ref[0], o_ref[1]], axis=1)
        n = ov * lax.rsqrt(_group_sum(ov * ov, _head_sum_matrix()) * (1.0 / DH) + EPS) * w2
        m_ref[...] = (n * sz).astype(m_ref.dtype)

    return pl.pallas_call(
        body, name="gdn_post", grid=(WIDTH // LANE,),
        in_specs=[pl.BlockSpec((2, t, DH), lambda j: (j, 0, 0)), pl.BlockSpec((t, LANE), lambda j: (0, Z_COL0 + j)),
                  pl.BlockSpec((1, DH), lambda j: (0, 0))],
        out_specs=pl.BlockSpec((t, LANE), lambda j: (0, j)),
        out_shape=jax.ShapeDtypeStruct((t, WIDTH), MXU),
        compiler_params=_params(("arbitrary",)),
    )(o, pm, onw)


def _gdn_post_bwd(o, pm, onw, dmix):
    t = pm.shape[0]

    def body(o_ref, z_ref, w_ref, dm_ref, do_ref, dz_ref, dw_ref):
        @pl.when(pl.program_id(0) == 0)
        def _():
            dw_ref[...] = jnp.zeros_like(dw_ref)

        z = z_ref[...]
        sg = _sigmoid(z)
        sz = z * sg
        dsz = sg * (1.0 + z * (1.0 - sg))
        dm = dm_ref[...]
        hs = _head_sum_matrix()
        w2 = jnp.concatenate([w_ref[...], w_ref[...]], axis=1)
        ov = jnp.concatenate([o_ref[0], o_ref[1]], axis=1)
        r = lax.rsqrt(_group_sum(ov * ov, hs) * (1.0 / DH) + EPS)
        xn = ov * r
        dn = dm * sz
        dz_ref[...] = (dm * (xn * w2) * dsz).astype(dz_ref.dtype)
        dw2 = jnp.sum(dn * xn, axis=0, keepdims=True)
        dw_ref[...] += dw2[:, :DH] + dw2[:, DH:]
        g = dn * w2
        do = r * (g - xn * (_group_sum(g * xn, hs) * (1.0 / DH)))
        do_ref[0] = do[:, :DH]
        do_ref[1] = do[:, DH:]

    return pl.pallas_call(
        body, name="gdn_post_bwd", grid=(WIDTH // LANE,),
        in_specs=[pl.BlockSpec((2, t, DH), lambda j: (j, 0, 0)), pl.BlockSpec((t, LANE), lambda j: (0, Z_COL0 + j)),
                  pl.BlockSpec((1, DH), lambda j: (0, 0)), pl.BlockSpec((t, LANE), lambda j: (0, j))],
        out_specs=[pl.BlockSpec((2, t, DH), lambda j: (j, 0, 0)), pl.BlockSpec((t, LANE), lambda j: (0, j)),
                   pl.BlockSpec((1, DH), lambda j: (0, 0))],
        out_shape=[jax.ShapeDtypeStruct((HEADS, t, DH), F32), jax.ShapeDtypeStruct((t, WIDTH), MXU),
                   jax.ShapeDtypeStruct((1, DH), F32)],
        compiler_params=_params(("arbitrary",)),
    )(o, pm, onw, dmix)


def _fox_post(o, pm):
    t = pm.shape[0]

    def body(o_ref, g_ref, m_ref):
        m_ref[...] = (jnp.concatenate([o_ref[0], o_ref[1]], axis=1) * _sigmoid(g_ref[...])).astype(m_ref.dtype)

    return pl.pallas_call(
        body, name="fox_post", grid=(WIDTH // LANE,),
        in_specs=[pl.BlockSpec((2, t, DH), lambda j: (j, 0, 0)), pl.BlockSpec((t, LANE), lambda j: (0, FGATE_COL0 + j))],
        out_specs=pl.BlockSpec((t, LANE), lambda j: (0, j)),
        out_shape=jax.ShapeDtypeStruct((t, WIDTH), MXU),
        compiler_params=_params(("arbitrary",)),
    )(o, pm)


def _fox_post_bwd(o, pm, dmix):
    t = pm.shape[0]

    def body(o_ref, g_ref, dm_ref, do_ref, dg_ref):
        sg = _sigmoid(g_ref[...])
        dm = dm_ref[...]
        for hh in range(2):
            cols = slice(hh * DH, (hh + 1) * DH)
            do_ref[hh] = dm[:, cols] * sg[:, cols]
            dg_ref[:, cols] = (dm[:, cols] * o_ref[hh] * (sg * (1.0 - sg))[:, cols]).astype(dg_ref.dtype)

    return pl.pallas_call(
        body, name="fox_post_bwd", grid=(WIDTH // LANE,),
        in_specs=[pl.BlockSpec((2, t, DH), lambda j: (j, 0, 0)), pl.BlockSpec((t, LANE), lambda j: (0, FGATE_COL0 + j)),
                  pl.BlockSpec((t, LANE), lambda j: (0, j))],
        out_specs=[pl.BlockSpec((2, t, DH), lambda j: (j, 0, 0)), pl.BlockSpec((t, LANE), lambda j: (0, j))],
        out_shape=[jax.ShapeDtypeStruct((HEADS, t, DH), F32), jax.ShapeDtypeStruct((t, WIDTH), MXU)],
        compiler_params=_params(("arbitrary",)),
    )(o, pm, dmix)


def _tail(x, mixg, mixf, tgt, wo, n2w, wg_t, wu_t, wd, fw):
    t, d = x.shape
    dff = wd.shape[0]
    tb = min(TB, t)

    def body(x_ref, mg_ref, mf_ref, t_ref, wo_ref, n2_ref, wg_ref, wu_ref, wd_ref, fw_ref,
             h2_ref, act_ref, dgate_ref, dup_ref, dx3_ref, dx2_ref, dmg_ref, dmf_ref, dn2_ref, dfw_ref, loss_ref):
        @pl.when(pl.program_id(0) == 0)
        def _():
            dn2_ref[...] = jnp.zeros_like(dn2_ref)
            dfw_ref[...] = jnp.zeros_like(dfw_ref)
            loss_ref[...] = jnp.zeros_like(loss_ref)

        x2 = x_ref[...] + _dot(mg_ref[...], wo_ref[0:WIDTH, :]) + _dot(mf_ref[...], wo_ref[WIDTH:2 * WIDTH, :])
        r2 = lax.rsqrt(jnp.mean(x2 * x2, axis=-1, keepdims=True) + EPS)
        xn2 = x2 * r2
        h2 = (xn2 * n2_ref[...]).astype(MXU)
        h2_ref[...] = h2
        gate = _dot_nt(h2, wg_ref[...])
        up = _dot_nt(h2, wu_ref[...])
        sg = _sigmoid(gate)
        sl = gate * sg
        act = (sl * up).astype(MXU)
        act_ref[...] = act
        x3 = x2 + _dot(act, wd_ref[...])
        r3 = lax.rsqrt(jnp.mean(x3 * x3, axis=-1, keepdims=True) + EPS)
        xn3 = x3 * r3
        err = xn3 * fw_ref[...] - t_ref[...]
        loss_ref[...] += 0.5 * jnp.sum(jnp.mean(err * err, axis=-1, keepdims=True), axis=0, keepdims=True)
        dy = err * (1.0 / d)
        dfw_ref[...] += jnp.sum(dy * xn3, axis=0, keepdims=True)
        g3 = dy * fw_ref[...]
        dx3 = r3 * (g3 - xn3 * jnp.mean(g3 * xn3, axis=-1, keepdims=True))
        dx3_ref[...] = dx3.astype(MXU)
        dact = _dot_nt(dx3, wd_ref[...])
        dgate = (dact * up * (sg * (1.0 + gate * (1.0 - sg)))).astype(MXU)
        dup = (dact * sl).astype(MXU)
        dgate_ref[...] = dgate
        dup_ref[...] = dup
        dh2 = _dot(dgate, wg_ref[...]) + _dot(dup, wu_ref[...])
        dn2_ref[...] += jnp.sum(dh2 * xn2, axis=0, keepdims=True)
        g2 = dh2 * n2_ref[...]
        dx2 = dx3 + r2 * (g2 - xn2 * jnp.mean(g2 * xn2, axis=-1, keepdims=True))
        dx2_ref[...] = dx2
        dmg_ref[...] = _dot_nt(dx2, wo_ref[0:WIDTH, :])
        dmf_ref[...] = _dot_nt(dx2, wo_ref[WIDTH:2 * WIDTH, :])

    def tok(n):
        return pl.BlockSpec((tb, n), lambda i: (i, 0))

    acc = pl.BlockSpec((1, d), lambda i: (0, 0))
    sds = jax.ShapeDtypeStruct
    return pl.pallas_call(
        body, name="tail", grid=(t // tb,),
        in_specs=[tok(d), tok(WIDTH), tok(WIDTH), tok(d), _resident(wo.shape), _resident((1, d)),
                  _resident(wg_t.shape), _resident(wu_t.shape), _resident(wd.shape), _resident((1, d))],
        out_specs=[tok(d), tok(dff), tok(dff), tok(dff), tok(d), tok(d), tok(WIDTH), tok(WIDTH), acc, acc,
                   pl.BlockSpec((1, 1), lambda i: (0, 0))],
        out_shape=[sds((t, d), MXU), sds((t, dff), MXU), sds((t, dff), MXU), sds((t, dff), MXU), sds((t, d), MXU),
                   sds((t, d), F32), sds((t, WIDTH), F32), sds((t, WIDTH), F32), sds((1, d), F32), sds((1, d), F32),
                   sds((1, 1), F32)],
        compiler_params=_params(("arbitrary",)),
    )(x, mixg, mixf, tgt, wo, n2w, wg_t, wu_t, wd, fw)


def _wgrads(a_list, b, name):
    t, n = b.shape
    ms = [a.shape[1] for a in a_list]
    bms = [256 if m % 256 == 0 else LANE for m in ms]
    nbs = [m // bm for m, bm in zip(ms, bms)]
    k = len(a_list)
    cast = b.dtype != jnp.dtype(MXU)

    def body(*refs):
        a_refs, b_ref, o_refs = refs[:k], refs[k], refs[k + 1:2 * k + 1]
        i = pl.program_id(0)
        if cast:
            @pl.when(i == 0)
            def _():
                refs[-1][...] = b_ref[...].astype(MXU)
        for a_ref, o_ref, nb in zip(a_refs, o_refs, nbs):
            @pl.when(i < nb)
            def _():
                o_ref[...] = _dot_tn(a_ref[...], refs[-1][...] if cast else b_ref[...]).astype(o_ref.dtype)

    def clamp(nb):
        return lambda i: jnp.minimum(i, nb - 1)

    return pl.pallas_call(
        body, name=name, grid=(max(nbs),),
        in_specs=[pl.BlockSpec((t, bm), lambda i, c=clamp(nb): (0, c(i))) for bm, nb in zip(bms, nbs)] + [_resident((t, n))],
        out_specs=[pl.BlockSpec((bm, n), lambda i, c=clamp(nb): (c(i), 0)) for bm, nb in zip(bms, nbs)],
        out_shape=[jax.ShapeDtypeStruct((m, n), WIRE) for m in ms],
        scratch_shapes=[pltpu.VMEM((t, n), MXU)] if cast else [],
        compiler_params=_params(("arbitrary",)),
    )(*a_list, b)


def _merge_dw_in(d_gdn, d_z, d_fox, d_fg, d_small, shards=False):
    pieces = [d_gdn, d_z, d_small[:2 * HEADS], d_fox, d_fg, d_small[2 * HEADS:3 * HEADS]]
    if not shards:
        return jnp.concatenate(pieces, axis=0)
    n = sum(p.shape[0] for p in pieces) // N_DEV
    out = []
    for dev in range(N_DEV):
        parts, first = [], 0
        for p in pieces:
            lo, hi = max(dev * n, first), min((dev + 1) * n, first + p.shape[0])
            if lo < hi:
                parts.append(p[lo - first:hi - first])
            first += p.shape[0]
        out.append(jnp.concatenate(parts, axis=0))
    return jnp.stack(out)


def _lanes(*pieces):
    v = jnp.concatenate([p.reshape(-1).astype(F32) for p in pieces])
    return jnp.pad(v, (0, LANE - v.shape[0])).reshape(1, LANE)


def _vector_params(p):
    d = p["norm1_w"].size
    gparams = jnp.concatenate([_lanes(jnp.zeros(HEADS), p["gdn_dt_bias"], p["fox_f_bias"]),
                               _lanes(jnp.zeros(HEADS), p["gdn_A_log"]), jnp.zeros((6, LANE), F32)])
    fox_nw = jnp.stack([jnp.tile(p["fox_q_norm_w"].reshape(-1), 2), jnp.tile(p["fox_k_norm_w"].reshape(-1), 2),
                        jnp.ones((LANE,), F32)])
    return dict(n1w=p["norm1_w"].reshape(1, d), n2w=p["norm2_w"].reshape(1, d), fw=p["final_norm_w"].reshape(1, d),
                onw=p["gdn_out_norm_w"].reshape(1, DH), gparams=gparams, fox_nw=fox_nw)


def _mixer_forward(x, vp, w_t, ws_t, conv_w, carry=None, carry_first=None):
    h1, pm, ps, *first = _inproj(x, vp["n1w"], w_t, ws_t, carry_first)
    gates, run, tot, run_t = _gates(ps, vp["gparams"])
    fqkv = _fox_prep(pm, vp["fox_nw"])
    o_fox, lse, *carried = _fox_fwd(fqkv, run, run_t, carry)
    if carry:
        pm, o_fox = lax.optimization_barrier((pm, o_fox))
    mixf = _fox_post(o_fox, pm)
    gqkv = _gdn_prep(pm, conv_w)
    o_gdn, states, inv = _gdn_fwd(gqkv, gates, run, tot, run_t)
    mixg = _gdn_post(o_gdn, pm, vp["onw"])
    return dict(h1=h1, pm=pm, ps=ps, gates=gates, run=run, tot=tot, run_t=run_t, gqkv=gqkv, o_gdn=o_gdn, states=states,
                inv=inv, mixg=mixg, fqkv=fqkv, o_fox=o_fox, lse=lse, mixf=mixf, carried=carried[0] if carried else None,
                carried_first=first[0] if first else None)


def _mixer_backward(x, vp, w_t, ws_t, conv_w, f, dx2, dmixg, dmixf, carry=None, scatter_own=False):
    pm = f["pm"]
    do_fox, dfg = _fox_post_bwd(f["o_fox"], pm, dmixf)
    dfqkv, dcol_f, drow_f, *carried = _fox_bwd(f["fqkv"], f["run"], f["run_t"], f["o_fox"], f["lse"], do_fox, carry)
    dfox, dfnw = _fox_prep_bwd(pm, vp["fox_nw"], dfqkv)
    do_gdn, dz, donw = _gdn_post_bwd(f["o_gdn"], pm, vp["onw"], dmixg)
    dgqkv, dcol_g, dtot_g, drow_g = _gdn_bwd(f["gqkv"], f["gates"], f["run"], f["tot"], f["run_t"], f["inv"], f["states"], do_gdn)
    dgdn, dconv = _gdn_prep_bwd(pm, conv_w, dgqkv)
    dps, gsum = _gates_bwd(f["ps"], vp["gparams"], dcol_g, dtot_g, dcol_f, drow_g, drow_f)
    dw_pieces = _wgrads([dgdn, dz, dfox, dfg, dps], f["h1"], "dw_in")
    dw_in = _merge_dw_in(*dw_pieces)
    own = ("scatter", [_merge_dw_in(*dw_pieces, shards=True), _cut(dconv, 1)]) if scatter_own else None
    grad_x, dn1w, *sent = _inproj_bwd(x, vp["n1w"], dx2, dgdn, dz, dfox, dfg, dps, w_t, ws_t, own)
    small = dict(dn1w=dn1w, gsum=gsum, donw=donw, dfnw=dfnw)
    return (grad_x, dw_in, dconv, small, *carried, *sent)


VECTORS = ("norm1_w", "norm2_w", "final_norm_w", "gdn_A_log", "gdn_dt_bias", "gdn_out_norm_w", "fox_f_bias",
           "fox_q_norm_w", "fox_k_norm_w")
VEC_ROWS = 16
LOSS_ROW = len(VECTORS)


def _pack_vectors(dn1w, dn2w, dfw, gsum, donw, dfnw, loss):
    d = dn1w.shape[1]

    def body(n1_ref, n2_ref, fw_ref, gs_ref, on_ref, fn_ref, loss_ref, o_ref):
        o_ref[...] = jnp.zeros_like(o_ref)
        o_ref[0:1, :] = n1_ref[...]
        o_ref[1:2, :] = n2_ref[...]
        o_ref[2:3, :] = fw_ref[...]
        o_ref[3:4, 0:HEADS] = gs_ref[0:1, 0:HEADS]
        o_ref[4:5, 0:HEADS] = gs_ref[1:2, 0:HEADS]
        o_ref[5:6, 0:DH] = on_ref[...]
        o_ref[6:7, 0:HEADS] = gs_ref[2:3, 0:HEADS]
        for kind in range(2):
            v = fn_ref[kind, 0]
            for j in range(1, fn_ref.shape[1]):
                v = v + fn_ref[kind, j]
            o_ref[7 + kind:8 + kind, 0:DH] = v[:, :DH] + v[:, DH:]
        o_ref[LOSS_ROW:LOSS_ROW + 1, 0:1] = loss_ref[...]

    return pl.pallas_call(body, name="pack_vectors", out_shape=jax.ShapeDtypeStruct((VEC_ROWS, d), F32),
                          compiler_params=_params())(dn1w, dn2w, dfw, gsum, donw, dfnw, loss)


def _late_grads(f, h2, act, dgate, dup, dx3, dx2):
    dw_gate, dw_up = _wgrads([dgate, dup], h2, "dw_gate_up")
    (dw_down,) = _wgrads([act], dx3, "dw_down")
    return {"w_out": jnp.concatenate(_wgrads([f["mixg"], f["mixf"]], dx2, "dw_out"), axis=0),
            "w_ffn_gate": dw_gate, "w_ffn_up": dw_up, "w_ffn_down": dw_down}


def _local_step(x, tgt, p, w_in_t, conv_w, wo, wg_t, wu_t, wd):
    vp = _vector_params(p)
    ws_t = _small_rows(w_in_t)
    f = _mixer_forward(x, vp, w_in_t, ws_t, conv_w)
    (h2, act, dgate, dup, dx3, dx2, dmixg, dmixf, dn2w, dfw, loss) = _tail(
        x, f["mixg"], f["mixf"], tgt, wo, vp["n2w"], wg_t, wu_t, wd, vp["fw"])
    grad_x, dw_in, dconv, small = _mixer_backward(x, vp, w_in_t, ws_t, conv_w, f, dx2, dmixg, dmixf)
    grads = {"w_in": dw_in, "gdn_conv_w": dconv, **_late_grads(f, h2, act, dgate, dup, dx3, dx2)}
    vec = _pack_vectors(small["dn1w"], dn2w, dfw, small["gsum"], small["donw"], small["dfnw"], loss)
    return grad_x[0], grads, vec


def _my_place():
    return lax.axis_index("x"), lax.axis_index("y"), lax.axis_index("c")


def _peers():
    x, y, c = _my_place()
    peers = []
    for k in range(1, N_DEV):
        px = 1 - x if k & 4 else x
        py = 1 - y if k & 2 else y
        pc = 1 - c if k & 1 else c
        peers.append(((px, py, pc), 4 * px + 2 * py + pc))
    return 4 * x + 2 * y + c, peers


def _spread_copies(kind, srcs, lands, send_sems, recv_sems):
    me, peers = _peers()
    kinds = [kind] * len(srcs) if isinstance(kind, str) else kind
    remote, local = [], []
    for i, (kd, src, land) in enumerate(zip(kinds, srcs, lands)):
        for k, (dev, idx) in reversed(list(enumerate(peers))):
            remote.append(pltpu.make_async_remote_copy(
                src_ref=src if kd == "gather" else src.at[idx], dst_ref=land.at[me],
                send_sem=send_sems.at[i * (N_DEV - 1) + k], recv_sem=recv_sems.at[i * (N_DEV - 1) + k],
                device_id=dev, device_id_type=MESH))
        local.append((src if kd == "gather" else src.at[me], land.at[me]))
    return remote, local


def _gather_two_level(arrays, name):
    n = len(arrays)

    def body(*refs):
        srcs, lands = refs[:n], refs[n:2 * n]
        send_sems, recv_sems, local_sems = refs[2 * n:]
        x, y, c = _my_place()
        me, sibling = (x, y, c), (x, y, 1 - c)
        chips = [(1 - x, y), (x, 1 - y), (1 - x, 1 - y)]

        def index(p):
            return 4 * p[0] + 2 * p[1] + p[2]

        def copy(i, k, block, to, src=None):
            blk = lands[i].at[index(block)]
            return pltpu.make_async_remote_copy(
                src_ref=blk if src is None else src, dst_ref=blk, send_sem=send_sems.at[7 * i + k],
                recv_sem=recv_sems.at[7 * i + k], device_id=to, device_id_type=MESH)

        mine = [pltpu.make_async_copy(srcs[i], lands[i].at[index(me)], local_sems.at[i]) for i in range(n)]
        for cp in mine:
            cp.start()
        first = []
        for i in range(n):
            first += [copy(i, 1 + j, me, (*chip, c), src=srcs[i]) for j, chip in reversed(list(enumerate(chips)))]
            first.append(copy(i, 0, me, sibling, src=srcs[i]))
        for cp in first:
            cp.start()
        passed = []
        for i in range(n):
            for j, chip in enumerate(chips):
                copy(i, 1 + j, (*chip, c), me).wait_recv()
                passed.append(copy(i, 4 + j, (*chip, c), sibling))
                passed[-1].start()
        for i in range(n):
            copy(i, 0, sibling, me).wait_recv()
            for j, chip in enumerate(chips):
                copy(i, 4 + j, (*chip, 1 - c), me).wait_recv()
        for cp in first + passed:
            cp.wait_send()
        for cp in mine:
            cp.wait()

    return pl.pallas_call(
        body, name=name,
        out_shape=[jax.ShapeDtypeStruct((N_DEV,) + a.shape, a.dtype) for a in arrays],
        in_specs=[pl.BlockSpec(memory_space=pl.ANY)] * n, out_specs=[pl.BlockSpec(memory_space=pl.ANY)] * n,
        scratch_shapes=[pltpu.SemaphoreType.DMA((7 * n,)), pltpu.SemaphoreType.DMA((7 * n,)),
                        pltpu.SemaphoreType.DMA((n,))],
    )(*arrays)


def _land_shape(kind, a):
    return (N_DEV,) + a.shape if kind == "gather" else a.shape


HBM = pl.BlockSpec(memory_space=pltpu.HBM)
SEM = pl.BlockSpec(memory_space=pltpu.SEMAPHORE)


def _hbm(a):
    return pltpu.with_memory_space_constraint(a, pltpu.HBM)


def _carry_operands(kind, arrays):
    n = len(arrays)
    kinds = [kind] * n if isinstance(kind, str) else kind
    lands = [lax.empty(_land_shape(kd, a), a.dtype) for kd, a in zip(kinds, arrays)]
    sems = [pltpu.SemaphoreType.DMA((n * (N_DEV - 1),))] * 2
    return ([_hbm(a) for a in list(arrays) + lands], [HBM] * (2 * n),
            sems + [pltpu.HBM(a.shape, a.dtype) for a in list(arrays) + lands], [SEM] * 2 + [HBM] * (2 * n),
            pltpu.SemaphoreType.DMA((n,)))


def _carry_step(kind, in_refs, send_sems, recv_sems, local_sems, step, n_steps):
    n = len(in_refs) // 2
    remote, local = _spread_copies(kind, in_refs[:n], in_refs[n:], send_sems, recv_sems)
    copies = [pltpu.make_async_copy(s, d, local_sems.at[i]) for i, (s, d) in enumerate(local)]
    per_step = -(-len(remote) // n_steps)
    for s in range(-(-len(remote) // per_step)):
        @pl.when(step == s)
        def _():
            for cp in remote[s * per_step:(s + 1) * per_step]:
                cp.start()
            if s == 0:
                for cp in copies:
                    cp.start()

    @pl.when(step == n_steps - 1)
    def _():
        for cp in copies:
            cp.wait()


def _carry_state(extra_out):
    n = (len(extra_out) - 2) // 2
    return list(extra_out[2:2 + n]), list(extra_out[2 + n:]), extra_out[0], extra_out[1]


def _spread_start(arrays, kind, name):
    n = len(arrays)

    def body(*refs):
        srcs, lands = refs[:n], refs[n:2 * n]
        send_sems, recv_sems = refs[2 * n], refs[2 * n + 1]
        token = refs[4 * n + 2]
        local_sems = refs[4 * n + 3]
        remote, local = _spread_copies(kind, srcs, lands, send_sems, recv_sems)
        for cp in remote:
            cp.start()
        copies = [pltpu.make_async_copy(s, d, local_sems.at[i]) for i, (s, d) in enumerate(local)]
        for cp in copies:
            cp.start()
        for cp in copies:
            cp.wait()
        token[...] = jnp.zeros_like(token)

    sems = (pltpu.SemaphoreType.DMA((n * (N_DEV - 1),)),) * 2
    kinds = [kind] * n if isinstance(kind, str) else kind
    lands = [lax.empty(_land_shape(kd, a), a.dtype) for kd, a in zip(kinds, arrays)]
    out = pl.pallas_call(
        body, name=name,
        out_shape=sems + tuple(pltpu.HBM(a.shape, a.dtype) for a in list(arrays) + lands)
        + (jax.ShapeDtypeStruct((8, LANE), F32),),
        in_specs=[HBM] * (2 * n), out_specs=tuple([SEM] * 2 + [HBM] * (2 * n) + [pl.BlockSpec(memory_space=pltpu.VMEM)]),
        input_output_aliases={j: 2 + j for j in range(2 * n)},
        scratch_shapes=[pltpu.SemaphoreType.DMA((n,))],
        compiler_params=pltpu.CompilerParams(has_side_effects=pltpu.SideEffectType.DATAFLOW_SIDE_EFFECTING),
    )(*[_hbm(a) for a in arrays], *[_hbm(a) for a in lands])
    return (list(out[2:2 + n]), list(out[2 + n:2 + 2 * n]), out[0], out[1]), out[-1]


def _spread_wait(state, kind, after, name):
    srcs, lands, send_sems, recv_sems = state
    n = len(srcs)
    after = list(after) if isinstance(after, (list, tuple)) else [after]

    def body(*refs):
        remote, _ = _spread_copies(kind, refs[:n], refs[n:2 * n], refs[2 * n], refs[2 * n + 1])
        for cp in remote:
            cp.wait_send()
        for cp in remote:
            cp.wait_recv()

    out = pl.pallas_call(
        body, name=name,
        out_shape=tuple(pltpu.HBM(a.shape, a.dtype) for a in srcs + lands),
        in_specs=[HBM] * (2 * n) + [SEM, SEM] + [pl.BlockSpec(memory_space=pl.ANY)] * len(after),
        out_specs=tuple([HBM] * (2 * n)),
        input_output_aliases={j: j for j in range(2 * n)},
        compiler_params=pltpu.CompilerParams(has_side_effects=pltpu.SideEffectType.DATAFLOW_SIDE_EFFECTING),
    )(*srcs, *lands, send_sems, recv_sems, *after)
    return list(out[n:])


ADAM_ROWS = 128
ADAM_COLS = 256


def _adam_math(g, w, m, v):
    nm = ADAM_B1 * m + (1.0 - ADAM_B1) * g
    nv = ADAM_B2 * v + (1.0 - ADAM_B2) * (g * g)
    m_hat = nm / (1.0 - ADAM_B1 ** ADAM_STEP)
    v_hat = nv / (1.0 - ADAM_B2 ** ADAM_STEP)
    return -ADAM_LR * (m_hat / (jnp.sqrt(v_hat) + ADAM_EPS) + ADAM_WD * w), nm, nv


def _sum_parts(p_ref):
    g = p_ref[0].astype(F32)
    for s in range(1, N_DEV):
        g = g + p_ref[s].astype(F32)
    return g


def _adam_matrix(parts, w, m, v, name):
    _, r, c = w.shape
    rb = ADAM_ROWS if r % ADAM_ROWS == 0 else r
    cb = ADAM_COLS if (rb == r and c % ADAM_COLS == 0) else c

    def body(p_ref, w_ref, m_ref, v_ref, g_ref, d_ref, nm_ref, nv_ref):
        g = _sum_parts(p_ref)
        g_ref[0] = g
        d_ref[0], nm_ref[0], nv_ref[0] = _adam_math(g, w_ref[0], m_ref[0], v_ref[0])

    blk = pl.BlockSpec((1, rb, cb), lambda i, j: (0, i, j))
    return pl.pallas_call(
        body, name=name, grid=(r // rb, c // cb),
        in_specs=[pl.BlockSpec((N_DEV, rb, cb), lambda i, j: (0, i, j)), blk, blk, blk],
        out_specs=[blk] * 4, out_shape=[jax.ShapeDtypeStruct(w.shape, F32)] * 4,
        compiler_params=_params(("arbitrary", "arbitrary")),
    )(parts, w, m, v)


def _copy_rows(a, name):
    _, r, c = a.shape
    rb = min(TB, r)

    def body(a_ref, o_ref):
        o_ref[...] = a_ref[...]

    blk = pl.BlockSpec((1, rb, c), lambda i: (0, i, 0))
    return pl.pallas_call(body, name=name, grid=(r // rb,), in_specs=[blk], out_specs=blk,
                          out_shape=jax.ShapeDtypeStruct(a.shape, a.dtype), compiler_params=_params(("arbitrary",)))(a)


def _adam_vectors(parts, ws, ms, vs):
    nv = len(ws)

    def body(*refs):
        p_ref = refs[0]
        w_refs, m_refs, v_refs = refs[1:1 + nv], refs[1 + nv:1 + 2 * nv], refs[1 + 2 * nv:1 + 3 * nv]
        outs = refs[1 + 3 * nv:]
        g_all = _sum_parts(p_ref)
        for i in range(nv):
            n = w_refs[i].shape[1]
            g = g_all[i:i + 1, 0:n]
            d, nm, nvv = _adam_math(g, w_refs[i][...], m_refs[i][...], v_refs[i][...])
            outs[i][...] = g
            outs[nv + i][...] = d
            outs[2 * nv + i][...] = nm
            outs[3 * nv + i][...] = nvv
        outs[4 * nv][...] = g_all[LOSS_ROW:LOSS_ROW + 1, 0:1]

    shapes = [jax.ShapeDtypeStruct(a.shape, F32) for a in ws]
    out = pl.pallas_call(body, name="adam_vectors", out_shape=shapes * 4 + [jax.ShapeDtypeStruct((1, 1), F32)],
                         compiler_params=_params())(parts, *ws, *ms, *vs)
    return out[:nv], out[nv:2 * nv], out[2 * nv:3 * nv], out[3 * nv:4 * nv], out[4 * nv]


MATRICES = (("w_in", 1), ("gdn_conv_w", 1), ("w_out", 0), ("w_ffn_gate", 1), ("w_ffn_up", 1), ("w_ffn_down", 0))
TRANSPOSED = ("w_in", "w_ffn_gate", "w_ffn_up")
WEIGHTS = ("norm1_w", "w_in", "gdn_conv_w", "gdn_A_log", "gdn_dt_bias", "gdn_out_norm_w", "fox_f_bias", "fox_q_norm_w",
           "fox_k_norm_w", "w_out", "norm2_w", "w_ffn_gate", "w_ffn_up", "w_ffn_down", "final_norm_w")


def _join(blocks, axis):
    _, r, c = blocks.shape
    if axis == 0:
        return blocks.reshape(N_DEV * r, c)
    return blocks.transpose(1, 0, 2).reshape(r, N_DEV * c)


def _cut(full, axis):
    r, c = full.shape
    if axis == 0:
        return full.reshape(N_DEV, r // N_DEV, c)
    return full.reshape(r, N_DEV, c // N_DEV).transpose(1, 0, 2)


def kernel(x, norm1_w, w_in, gdn_conv_w, gdn_A_log, gdn_dt_bias, gdn_out_norm_w, fox_f_bias, fox_q_norm_w, fox_k_norm_w, w_out, norm2_w, w_ffn_gate, w_ffn_up, w_ffn_down, final_norm_w, loss_target, m_norm1_w, m_w_in, m_gdn_conv_w, m_gdn_A_log, m_gdn_dt_bias, m_gdn_out_norm_w, m_fox_f_bias, m_fox_q_norm_w, m_fox_k_norm_w, m_w_out, m_norm2_w, m_w_ffn_gate, m_w_ffn_up, m_w_ffn_down, m_final_norm_w, v_norm1_w, v_w_in, v_gdn_conv_w, v_gdn_A_log, v_gdn_dt_bias, v_gdn_out_norm_w, v_fox_f_bias, v_fox_q_norm_w, v_fox_k_norm_w, v_w_out, v_norm2_w, v_w_ffn_gate, v_w_ffn_up, v_w_ffn_down, v_final_norm_w):
    w = dict(norm1_w=norm1_w, w_in=w_in, gdn_conv_w=gdn_conv_w, gdn_A_log=gdn_A_log, gdn_dt_bias=gdn_dt_bias,
             gdn_out_norm_w=gdn_out_norm_w, fox_f_bias=fox_f_bias, fox_q_norm_w=fox_q_norm_w, fox_k_norm_w=fox_k_norm_w,
             w_out=w_out, norm2_w=norm2_w, w_ffn_gate=w_ffn_gate, w_ffn_up=w_ffn_up, w_ffn_down=w_ffn_down,
             final_norm_w=final_norm_w)
    m = dict(norm1_w=m_norm1_w, w_in=m_w_in, gdn_conv_w=m_gdn_conv_w, gdn_A_log=m_gdn_A_log, gdn_dt_bias=m_gdn_dt_bias,
             gdn_out_norm_w=m_gdn_out_norm_w, fox_f_bias=m_fox_f_bias, fox_q_norm_w=m_fox_q_norm_w,
             fox_k_norm_w=m_fox_k_norm_w, w_out=m_w_out, norm2_w=m_norm2_w, w_ffn_gate=m_w_ffn_gate,
             w_ffn_up=m_w_ffn_up, w_ffn_down=m_w_ffn_down, final_norm_w=m_final_norm_w)
    v = dict(norm1_w=v_norm1_w, w_in=v_w_in, gdn_conv_w=v_gdn_conv_w, gdn_A_log=v_gdn_A_log, gdn_dt_bias=v_gdn_dt_bias,
             gdn_out_norm_w=v_gdn_out_norm_w, fox_f_bias=v_fox_f_bias, fox_q_norm_w=v_fox_q_norm_w,
             fox_k_norm_w=v_fox_k_norm_w, w_out=v_w_out, norm2_w=v_norm2_w, w_ffn_gate=v_w_ffn_gate,
             w_ffn_up=v_w_ffn_up, w_ffn_down=v_w_ffn_down, final_norm_w=v_final_norm_w)
    late = ("w_out", "w_ffn_gate", "w_ffn_up", "w_ffn_down")
    xs, tgt = x[0], loss_target[0]
    vp = _vector_params({n: w[n] for n in VECTORS})

    def rows_of(d, n):
        return d[n].transpose(0, 2, 1) if n in TRANSPOSED else d[n]

    wr, mr, vr = ({n: rows_of(d, n) for n, _ in MATRICES} for d in (w, m, v))

    w_in_blocks, conv_blocks = _gather_two_level([wr["w_in"][0].astype(WIRE), w["gdn_conv_w"][0]], "gather_in")
    w_t = _join(w_in_blocks, 0)
    conv_w = _join(conv_blocks, 1)
    early = ("w_ffn_gate",)
    rest = tuple(n for n in late if n not in early)
    f = _mixer_forward(xs, vp, w_t, _small_rows(w_t), conv_w, ("gather", [wr[n][0].astype(WIRE) for n in rest]),
                       ("gather", [wr[n][0].astype(WIRE) for n in early]))
    full = {n: _join(b, 0) for n, b in zip(early, _spread_wait(f["carried_first"], "gather", f["mixg"], "gather_early_wait"))}
    full.update({n: _join(b, 0)
                 for n, b in zip(rest, _spread_wait(f["carried"], "gather", full[early[0]], "gather_late_wait"))})

    (h2, act, dgate, dup, dx3, dx2, dmixg, dmixf, dn2w, dfw, loss) = _tail(
        xs, f["mixg"], f["mixf"], tgt, full["w_out"], vp["n2w"], full["w_ffn_gate"], full["w_ffn_up"],
        full["w_ffn_down"], vp["fw"])
    dlate = _late_grads(f, h2, act, dgate, dup, dx3, dx2)

    grad_x, dw_in, dconv, small, state_grads, state_own = _mixer_backward(
        xs, vp, w_t, _small_rows(w_t), conv_w, f, dx2, dmixg, dmixf, ("scatter", [_cut(dlate[n], 0) for n in late]),
        scatter_own=True)
    vec = _pack_vectors(small["dn1w"], dn2w, dfw, small["gsum"], small["donw"], small["dfnw"], loss)

    state_vec, token = _spread_start([vec], "gather", "vectors_start")
    parts = dict(zip(late, _spread_wait(state_grads, "scatter", token, "grads_late_wait")))
    results = [{}, {}, {}, {}]

    updated = {}

    def update(n):
        out = _adam_matrix(parts[n], wr[n], mr[n], vr[n], "adam_" + n)
        updated[n] = out[0]
        for d, a in zip(results, out):
            d[n] = a.transpose(0, 2, 1) if n in TRANSPOSED else a

    for n in late:
        update(n)
    grad_x = _copy_rows(grad_x, "grad_x_result")
    ready = [updated[n] for n in late] + [d["w_in"] for d in (wr, mr, vr)] + [grad_x]
    (parts_vec,) = _spread_wait(state_vec, "gather", ready, "vectors_wait")
    row = lambda a: a.reshape(1, -1)
    *vec_out, total_loss = _adam_vectors(parts_vec, [row(w[n]) for n in VECTORS], [row(m[n]) for n in VECTORS],
                                         [row(v[n]) for n in VECTORS])
    parts["w_in"], parts["gdn_conv_w"] = _spread_wait(state_own, "scatter", parts_vec, "own_wait")
    update("w_in")
    update("gdn_conv_w")
    for d, arrs in zip(results, vec_out):
        for n, a in zip(VECTORS, arrs):
            d[n] = a.reshape(w[n].shape)
    return (total_loss[0, 0], grad_x, *[d[n] for d in results for n in WEIGHTS])
```

```python
import functools

import jax
import jax.numpy as jnp
from jax import lax
from jax.experimental import pallas as pl
from jax.experimental.pallas import tpu as pltpu

F32 = jnp.float32
MXU = jnp.bfloat16
WIRE = jnp.bfloat16
HI = lax.Precision.HIGHEST
EPS = 1e-6

N_DEV = 8
HEADS = 8
DH = 64
WIDTH = HEADS * DH
CHUNK = 64
LANE = 128
ROW_ALIGN = 16
TB = 256
QB = 256
VMEM_LIMIT = 60 * 1024 * 1024

ADAM_LR = 0.001
ADAM_B1 = 0.9
ADAM_B2 = 0.999
ADAM_EPS = 1e-08
ADAM_WD = 0.01
ADAM_STEP = 10

MESH = pl.DeviceIdType.MESH


def _params(sem=None):
    return pltpu.CompilerParams(dimension_semantics=sem, vmem_limit_bytes=VMEM_LIMIT)


def _resident(shape):
    n = len(shape)
    return pl.BlockSpec(shape, lambda *_: (0,) * n, pipeline_mode=pl.Buffered(1))


def _dot(a, b):
    return jnp.dot(a.astype(MXU), b.astype(MXU), preferred_element_type=F32)


def _dot_nt(a, b):
    return lax.dot_general(a.astype(MXU), b.astype(MXU), (((1,), (1,)), ((), ())), preferred_element_type=F32)


def _dot_tn(a, b):
    return lax.dot_general(a.astype(MXU), b.astype(MXU), (((0,), (0,)), ((), ())), preferred_element_type=F32)


def _hdot(a, b):
    return jnp.dot(a, b, precision=HI, preferred_element_type=F32)


def _hdot_nt(a, b):
    return lax.dot_general(a, b, (((1,), (1,)), ((), ())), precision=HI, preferred_element_type=F32)


def _hdot_tn(a, b):
    return lax.dot_general(a, b, (((0,), (0,)), ((), ())), precision=HI, preferred_element_type=F32)


def _sigmoid(x):
    return 0.5 * jnp.tanh(0.5 * x) + 0.5


def _softplus(x):
    return jnp.maximum(x, 0.0) + jnp.log(1.0 + jnp.exp(-jnp.abs(x)))


def _head_sum_matrix():
    ri = lax.broadcasted_iota(jnp.int32, (LANE, LANE), 0) // DH
    ci = lax.broadcasted_iota(jnp.int32, (LANE, LANE), 1) // DH
    return (ri == ci).astype(F32)


def _group_sum(a, ones_matrix):
    hi = a.astype(jnp.bfloat16)
    lo = (a - hi.astype(F32)).astype(jnp.bfloat16)
    m = ones_matrix.astype(jnp.bfloat16)
    return jnp.dot(hi, m, preferred_element_type=F32) + jnp.dot(lo, m, preferred_element_type=F32)


def _shift_down(x, s):
    return pltpu.roll(x, s, 0)


def _shift_up(x, s):
    return pltpu.roll(x, x.shape[0] - s, 0)


ROWS_A = 4 * WIDTH
ROWS_B = ROWS_A + 2 * HEADS
ROWS_C = ROWS_B + 4 * WIDTH


def _small_rows(w_t):
    return jnp.concatenate([w_t[ROWS_A:ROWS_B], w_t[ROWS_C:], jnp.zeros((LANE - 3 * HEADS, w_t.shape[1]), w_t.dtype)])


def _inproj(x, n1w, w_t, ws_t, carry=None):
    t, d = x.shape
    tb = min(TB, t)
    c_in, c_in_specs, c_out_shape, c_out_specs, c_sem = _carry_operands(*carry) if carry else ([], [], [], [], None)
    n_c = len(c_in)

    def body(*refs):
        x_ref, nw_ref, wt_ref, ws_ref = refs[:4]
        h_ref, pm_ref, ps_ref = refs[4 + n_c:7 + n_c]
        if carry:
            _carry_step(carry[0], refs[4:4 + n_c], refs[7 + n_c], refs[8 + n_c], refs[-1], pl.program_id(0), t // tb)
        xv = x_ref[...]
        r = lax.rsqrt(jnp.mean(xv * xv, axis=-1, keepdims=True) + EPS)
        h = (xv * r * nw_ref[...]).astype(MXU)
        h_ref[...] = h
        pm_ref[:, 0:ROWS_A] = _dot_nt(h, wt_ref[0:ROWS_A, :])
        pm_ref[:, ROWS_A:2 * ROWS_A] = _dot_nt(h, wt_ref[ROWS_B:ROWS_C, :])
        ps_ref[...] = _dot_nt(h, ws_ref[...])

    out = pl.pallas_call(
        body, name="inproj", grid=(t // tb,),
        in_specs=[pl.BlockSpec((tb, d), lambda i: (i, 0)), _resident((1, d)), _resident(w_t.shape), _resident((LANE, d))]
        + c_in_specs,
        out_specs=[pl.BlockSpec((tb, d), lambda i: (i, 0)), pl.BlockSpec((tb, 2 * ROWS_A), lambda i: (i, 0)),
                   pl.BlockSpec((tb, LANE), lambda i: (i, 0))] + c_out_specs,
        out_shape=[jax.ShapeDtypeStruct((t, d), MXU), jax.ShapeDtypeStruct((t, 2 * ROWS_A), F32),
                   jax.ShapeDtypeStruct((t, LANE), F32)] + c_out_shape,
        input_output_aliases={4 + j: 5 + j for j in range(n_c)},
        scratch_shapes=[c_sem] if carry else [],
        compiler_params=_params(("arbitrary",)),
    )(x, n1w, w_t, ws_t, *c_in)
    return (out[0], out[1], out[2], _carry_state(out[3:])) if carry else tuple(out)


def _inproj_bwd(x, n1w, dx2, dgdn, dz, dfox, dfg, dps, w_t, ws_t, carry=None):
    t, d = x.shape
    tb = min(TB, t)
    w3 = 3 * WIDTH
    c_in, c_in_specs, c_out_shape, c_out_specs, c_sem = _carry_operands(*carry) if carry else ([], [], [], [], None)
    n_c = len(c_in)

    def body(*refs):
        x_ref, nw_ref, dx2_ref, dgdn_ref, dz_ref, dfox_ref, dfg_ref, dps_ref, wm_ref, ws_ref = refs[:10]
        gx_ref, dnw_ref = refs[10 + n_c:12 + n_c]
        if carry:
            _carry_step(carry[0], refs[10:10 + n_c], refs[12 + n_c], refs[13 + n_c], refs[-1], pl.program_id(0), t // tb)
        dh = _dot(dgdn_ref[...], wm_ref[0:w3, :])
        dh += _dot(dz_ref[...], wm_ref[w3:ROWS_A, :])
        dh += _dot(dfox_ref[...], wm_ref[ROWS_B:ROWS_B + w3, :])
        dh += _dot(dfg_ref[...], wm_ref[ROWS_B + w3:ROWS_C, :])
        dh += _dot(dps_ref[...], ws_ref[...])
        xv = x_ref[...]
        r = lax.rsqrt(jnp.mean(xv * xv, axis=-1, keepdims=True) + EPS)
        xn = xv * r

        @pl.when(pl.program_id(0) == 0)
        def _():
            dnw_ref[...] = jnp.zeros_like(dnw_ref)

        dnw_ref[...] += jnp.sum(dh * xn, axis=0, keepdims=True)
        g = dh * nw_ref[...]
        gx_ref[0] = dx2_ref[...] + r * (g - xn * jnp.mean(g * xn, axis=-1, keepdims=True))

    def tok(n):
        return pl.BlockSpec((tb, n), lambda i: (i, 0))

    out = pl.pallas_call(
        body, name="inproj_bwd", grid=(t // tb,),
        in_specs=[tok(d), _resident((1, d)), tok(d), tok(w3), tok(WIDTH), tok(w3), tok(WIDTH), tok(LANE),
                  _resident(w_t.shape), _resident(ws_t.shape)] + c_in_specs,
        out_specs=[pl.BlockSpec((1, tb, d), lambda i: (0, i, 0)), pl.BlockSpec((1, d), lambda i: (0, 0))] + c_out_specs,
        out_shape=[jax.ShapeDtypeStruct((1, t, d), F32), jax.ShapeDtypeStruct((1, d), F32)] + c_out_shape,
        input_output_aliases={10 + j: 4 + j for j in range(n_c)},
        scratch_shapes=[c_sem] if carry else [],
        compiler_params=_params(("arbitrary",)),
    )(x, n1w, dx2, dgdn, dz, dfox, dfg, dps, w_t, ws_t, *c_in)
    return (out[0], out[1], _carry_state(out[2:])) if carry else tuple(out)


def _gate_lanes(shape):
    lane = lax.broadcasted_iota(jnp.int32, shape, 1)
    return lane < HEADS, (lane >= HEADS) & (lane < 2 * HEADS), (lane >= 2 * HEADS) & (lane < 3 * HEADS)


def _block_masks():
    ri = lax.broadcasted_iota(jnp.int32, (LANE, LANE), 0)
    ci = lax.broadcasted_iota(jnp.int32, (LANE, LANE), 1)
    same = (ri // CHUNK) == (ci // CHUNK)
    return ((ri >= ci).astype(F32), (ri <= ci).astype(F32), (same & (ri >= ci)).astype(F32),
            (same & (ri <= ci)).astype(F32), same.astype(F32))


def _gates(ps, gparams):
    t = ps.shape[0]
    nb = t // LANE

    def body(ps_ref, gp_ref, out_ref, run_ref, tot_ref, runt_ref):
        p = ps_ref[...]
        is_b, is_a, is_f = _gate_lanes(p.shape)
        z = p + gp_ref[0:1, :]
        neg_exp_a = -jnp.exp(gp_ref[1:2, :])
        glog = neg_exp_a * _softplus(z)
        logf = -_softplus(-z)
        out_ref[...] = jnp.where(is_b, _sigmoid(p), jnp.where(is_a, glog, jnp.where(is_f, logf, 0.0)))
        tril, _, tril_c, _, same_c = _block_masks()
        off = jnp.zeros((1, LANE), F32)
        for b in range(nb):
            rows = slice(b * LANE, (b + 1) * LANE)
            blk = out_ref[rows, :]
            ga = jnp.where(is_a[:LANE], blk, 0.0)
            fb = _hdot(tril, jnp.where(is_f[:LANE], blk, 0.0)) + off
            run = fb + _hdot(tril_c, ga)
            run_ref[rows, :] = run
            runt_ref[:, rows] = run.T
            tot_ref[rows, :] = _hdot(same_c, ga)
            off = fb[LANE - 1:LANE, :]

    return pl.pallas_call(
        body, name="gates",
        out_shape=[jax.ShapeDtypeStruct((t, LANE), F32)] * 3 + [jax.ShapeDtypeStruct((LANE, t), F32)],
        compiler_params=_params(),
    )(ps, gparams)


def _gates_bwd(ps, gparams, dcol_g, dtot_g, dcol_f, drow_g, drow_f):
    t = ps.shape[0]
    nb = t // LANE

    def body(ps_ref, gp_ref, dcg_ref, dtg_ref, dcf_ref, drg_ref, drf_ref, dps_ref, sums_ref, dl_ref, d0_ref, tr_ref):
        p = ps_ref[...]
        is_b, is_a, is_f = _gate_lanes(p.shape)
        _, triu, _, triu_c, same_c = _block_masks()
        tr_ref[...] = jnp.zeros_like(tr_ref)
        off = jnp.zeros((1, LANE), F32)
        for b in reversed(range(nb)):
            rows = slice(b * LANE, (b + 1) * LANE)
            tr_ref[HEADS:2 * HEADS, :] = drg_ref[:, rows]
            tr_ref[2 * HEADS:3 * HEADS, :] = drf_ref[:, rows]
            d = dcg_ref[rows, :] + dcf_ref[rows, :] + tr_ref[...].T
            d0_ref[rows, :] = d
            dlf = _hdot(triu, jnp.where(is_f[:LANE], d, 0.0)) + off
            dla = (_hdot(triu_c, jnp.where(is_a[:LANE], d, 0.0))
                   + _hdot(same_c, jnp.where(is_a[:LANE], dtg_ref[rows, :], 0.0)))
            dl_ref[rows, :] = dlf + dla
            off = dlf[0:1, :]
        z = p + gp_ref[0:1, :]
        neg_exp_a = -jnp.exp(gp_ref[1:2, :])
        sb = _sigmoid(p)
        glog = neg_exp_a * _softplus(z)
        dl = dl_ref[...]
        dp = jnp.where(is_b, d0_ref[...] * sb * (1.0 - sb),
                       jnp.where(is_a, dl * neg_exp_a * _sigmoid(z), jnp.where(is_f, dl * _sigmoid(-z), 0.0)))
        dps_ref[...] = dp
        s_a = jnp.sum(jnp.where(is_a, dl * glog, 0.0), axis=0, keepdims=True)
        s_p = jnp.sum(jnp.where(is_b, 0.0, dp), axis=0, keepdims=True)
        row = lax.broadcasted_iota(jnp.int32, (8, LANE), 0)
        from_a = pltpu.roll(jnp.where(row == 0, s_a, jnp.where(row == 1, s_p, 0.0)), LANE - HEADS, 1)
        from_f = pltpu.roll(jnp.where(row == 2, s_p, 0.0), LANE - 2 * HEADS, 1)
        lane = lax.broadcasted_iota(jnp.int32, (8, LANE), 1)
        sums_ref[...] = jnp.where(lane < HEADS, from_a + from_f, 0.0)

    return pl.pallas_call(
        body, name="gates_bwd",
        out_shape=[jax.ShapeDtypeStruct((t, LANE), F32), jax.ShapeDtypeStruct((8, LANE), F32)],
        scratch_shapes=[pltpu.VMEM((t, LANE), F32), pltpu.VMEM((t, LANE), F32), pltpu.VMEM((LANE, LANE), F32)],
        compiler_params=_params(),
    )(ps, gparams, dcol_g, dtot_g, dcol_f, drow_g, drow_f)


def _conv(xv, w):
    acc = w[3:4, :] * xv
    for s in range(1, 4):
        acc += w[3 - s:4 - s, :] * _shift_down(xv, s)
    return acc


PREP_ROWS = 256
HALO = 8
PREP_UNROLL = 8


def _tile_loop(n, tile, init):
    if n % PREP_UNROLL:
        return lax.fori_loop(0, n, tile, init)

    def trip(g, carry):
        for u in range(PREP_UNROLL):
            carry = tile(g * PREP_UNROLL + u, carry)
        return carry

    return lax.fori_loop(0, n // PREP_UNROLL, trip, init)


def _tile_rows(r):
    return pl.ds(pl.multiple_of(r * PREP_ROWS, PREP_ROWS), PREP_ROWS)


def _gdn_prep(pm, conv_w):
    t = pm.shape[0]
    nj = WIDTH // LANE
    win = PREP_ROWS + HALO

    def body(x_ref, w_ref, o_ref, xp_ref):
        kind = pl.program_id(0)
        xp_ref[0:HALO, :] = jnp.zeros((HALO, LANE), F32)
        xp_ref[HALO:, :] = x_ref[...]
        w = w_ref[...]
        hs = _head_sum_matrix()

        def tile(r, _, normed):
            xw = xp_ref[pl.ds(pl.multiple_of(r * PREP_ROWS, PREP_ROWS), win), :]
            acc = _conv(xw, w)[HALO:]
            out = acc * _sigmoid(acc)
            if normed:
                out = out * lax.rsqrt(_group_sum(out * out, hs) + EPS)
            o_ref[0, 0, _tile_rows(r), :] = out[:, :DH]
            o_ref[0, 1, _tile_rows(r), :] = out[:, DH:]
            return 0

        @pl.when(kind < 2)
        def _():
            _tile_loop(t // PREP_ROWS, functools.partial(tile, normed=True), 0)

        @pl.when(kind == 2)
        def _():
            _tile_loop(t // PREP_ROWS, functools.partial(tile, normed=False), 0)

    return pl.pallas_call(
        body, name="gdn_prep", grid=(3, nj),
        in_specs=[pl.BlockSpec((t, LANE), lambda i, j: (0, i * nj + j)),
                  pl.BlockSpec((4, LANE), lambda i, j: (0, i * nj + j))],
        out_specs=pl.BlockSpec((1, 2, t, DH), lambda i, j: (i, j, 0, 0)),
        out_shape=jax.ShapeDtypeStruct((3, HEADS, t, DH), F32),
        scratch_shapes=[pltpu.VMEM((t + HALO, LANE), F32)],
        compiler_params=_params(("arbitrary", "arbitrary")),
    )(pm, conv_w)


def _gdn_prep_bwd(pm, conv_w, dqkv):
    t = pm.shape[0]
    nj = WIDTH // LANE
    win = PREP_ROWS + 2 * HALO

    def body(x_ref, w_ref, d_ref, dx_ref, dw_ref, xp_ref, dp_ref):
        kind = pl.program_id(0)
        zeros = jnp.zeros((HALO, LANE), F32)
        for ref in (xp_ref, dp_ref):
            ref[0:HALO, :] = zeros
            ref[HALO + t:, :] = zeros
        xp_ref[HALO:HALO + t, :] = x_ref[...]
        dp_ref[HALO:HALO + t, 0:DH] = d_ref[0, 0]
        dp_ref[HALO:HALO + t, DH:] = d_ref[0, 1]
        w = w_ref[...]
        hs = _head_sum_matrix()
        rows = lax.broadcasted_iota(jnp.int32, (win, LANE), 0)
        in_tile = (rows >= HALO) & (rows < HALO + PREP_ROWS)

        def tile(r, dw, normed):
            start = pl.multiple_of(r * PREP_ROWS, PREP_ROWS)
            xw = xp_ref[pl.ds(start, win), :]
            dy = dp_ref[pl.ds(start, win), :]
            acc = _conv(xw, w)
            sg = _sigmoid(acc)
            if normed:
                y = acc * sg
                rn = lax.rsqrt(_group_sum(y * y, hs) + EPS)
                yn = y * rn
                dy = rn * (dy - yn * _group_sum(dy * yn, hs))
            dacc = dy * sg * (1.0 + acc * (1.0 - sg))
            dx = w[3:4, :] * dacc
            for s in range(1, 4):
                dx += w[3 - s:4 - s, :] * _shift_up(dacc, s)
            dx_ref[_tile_rows(r), :] = dx[HALO:HALO + PREP_ROWS].astype(dx_ref.dtype)
            dm = jnp.where(in_tile, dacc, 0.0)
            return tuple(dw[i] + jnp.sum(dm * (xw if i == 3 else _shift_down(xw, 3 - i)), axis=0, keepdims=True)
                         for i in range(4))

        def run(normed):
            dw = _tile_loop(t // PREP_ROWS, functools.partial(tile, normed=normed),
                            tuple(jnp.zeros((1, LANE), F32) for _ in range(4)))
            for i in range(4):
                dw_ref[i:i + 1, :] = dw[i]

        pl.when(kind < 2)(functools.partial(run, True))
        pl.when(kind == 2)(functools.partial(run, False))

    return pl.pallas_call(
        body, name="gdn_prep_bwd", grid=(3, nj),
        in_specs=[pl.BlockSpec((t, LANE), lambda i, j: (0, i * nj + j)),
                  pl.BlockSpec((4, LANE), lambda i, j: (0, i * nj + j)),
                  pl.BlockSpec((1, 2, t, DH), lambda i, j: (i, j, 0, 0))],
        out_specs=[pl.BlockSpec((t, LANE), lambda i, j: (0, i * nj + j)),
                   pl.BlockSpec((4, LANE), lambda i, j: (0, i * nj + j))],
        out_shape=[jax.ShapeDtypeStruct((t, 3 * WIDTH), MXU), jax.ShapeDtypeStruct((4, 3 * WIDTH), F32)],
        scratch_shapes=[pltpu.VMEM((t + 2 * HALO, LANE), F32), pltpu.VMEM((t + 2 * HALO, LANE), F32)],
        compiler_params=_params(("arbitrary", "arbitrary")),
    )(pm, conv_w, dqkv)


FOX_COL0 = 4 * WIDTH // LANE


def _fox_prep(pm, nw):
    t = pm.shape[0]
    nj = WIDTH // LANE

    def body(x_ref, w_ref, o_ref):
        kind = pl.program_id(0)
        hs = _head_sum_matrix()
        wk = w_ref[pl.ds(kind, 1), :]

        def tile(r, _, normed):
            out = x_ref[_tile_rows(r), :]
            if normed:
                out = out * lax.rsqrt(_group_sum(out * out, hs) * (1.0 / DH) + EPS) * wk
            o_ref[0, 0, _tile_rows(r), :] = out[:, :DH]
            o_ref[0, 1, _tile_rows(r), :] = out[:, DH:]
            return 0

        @pl.when(kind < 2)
        def _():
            _tile_loop(t // PREP_ROWS, functools.partial(tile, normed=True), 0)

        @pl.when(kind == 2)
        def _():
            _tile_loop(t // PREP_ROWS, functools.partial(tile, normed=False), 0)

    return pl.pallas_call(
        body, name="fox_prep", grid=(3, nj),
        in_specs=[pl.BlockSpec((t, LANE), lambda i, j: (0, FOX_COL0 + i * nj + j)),
                  pl.BlockSpec((3, LANE), lambda i, j: (0, 0))],
        out_specs=pl.BlockSpec((1, 2, t, DH), lambda i, j: (i, j, 0, 0)),
        out_shape=jax.ShapeDtypeStruct((3, HEADS, t, DH), F32),
        compiler_params=_params(("arbitrary", "arbitrary")),
    )(pm, nw)


def _fox_prep_bwd(pm, nw, dqkv):
    t = pm.shape[0]
    nj = WIDTH // LANE

    def body(x_ref, w_ref, d_ref, dx_ref, dw_ref):
        kind = pl.program_id(0)
        hs = _head_sum_matrix()
        wk = w_ref[pl.ds(kind, 1), :]

        def tile(r, dw):
            xv = x_ref[_tile_rows(r), :]
            rn = lax.rsqrt(_group_sum(xv * xv, hs) * (1.0 / DH) + EPS)
            xn = xv * rn
            d = jnp.concatenate([d_ref[0, 0, _tile_rows(r), :], d_ref[0, 1, _tile_rows(r), :]], axis=1)
            g = d * wk
            dx_ref[_tile_rows(r), :] = (rn * (g - xn * _group_sum(g * xn, hs) * (1.0 / DH))).astype(dx_ref.dtype)
            return dw + jnp.sum(d * xn, axis=0, keepdims=True)

        def copy_tile(r, _):
            dx_ref[_tile_rows(r), :] = jnp.concatenate([d_ref[0, 0, _tile_rows(r), :], d_ref[0, 1, _tile_rows(r), :]],
                                                       axis=1).astype(dx_ref.dtype)
            return 0

        @pl.when(kind < 2)
        def _():
            dw_ref[0, 0] = _tile_loop(t // PREP_ROWS, tile, jnp.zeros((1, LANE), F32))

        @pl.when(kind == 2)
        def _():
            _tile_loop(t // PREP_ROWS, copy_tile, 0)
            dw_ref[0, 0] = jnp.zeros((1, LANE), F32)

    return pl.pallas_call(
        body, name="fox_prep_bwd", grid=(3, nj),
        in_specs=[pl.BlockSpec((t, LANE), lambda i, j: (0, FOX_COL0 + i * nj + j)),
                  pl.BlockSpec((3, LANE), lambda i, j: (0, 0)),
                  pl.BlockSpec((1, 2, t, DH), lambda i, j: (i, j, 0, 0))],
        out_specs=[pl.BlockSpec((t, LANE), lambda i, j: (0, i * nj + j)),
                   pl.BlockSpec((1, 1, 1, LANE), lambda i, j: (i, j, 0, 0))],
        out_shape=[jax.ShapeDtypeStruct((t, 3 * WIDTH), MXU), jax.ShapeDtypeStruct((3, nj, 1, LANE), F32)],
        compiler_params=_params(("arbitrary", "arbitrary")),
    )(pm, nw, dqkv)


SC = 256
CPS = SC // CHUNK
GDN_HP_FWD = 4
GDN_HP_BWD = 2
Q_SCALE = DH ** -0.5


def _sc_masks():
    ri = lax.broadcasted_iota(jnp.int32, (SC, SC), 0)
    ci = lax.broadcasted_iota(jnp.int32, (SC, SC), 1)
    same = (ri // CHUNK) == (ci // CHUNK)
    return same & (ri >= ci), same & (ri > ci), ri == ci


def _unit_lower_inverses(ms, eye):
    invs = [jnp.where(eye, 1.0, 0.0) + m for m in ms]
    ms = [_dot(m, m) for m in ms]
    for _ in range(4):
        both = [_dot(jnp.concatenate([inv, m], axis=0), m) for inv, m in zip(invs, ms)]
        invs = [inv + b[:SC] for inv, b in zip(invs, both)]
        ms = [b[SC:] for b in both]
    return [inv + _dot(inv, m) for inv, m in zip(invs, ms)]


def _lane_col(blk, lane_idx):
    lane = lax.broadcasted_iota(jnp.int32, blk.shape, 1)
    return jnp.sum(jnp.where(lane == lane_idx, blk, 0.0), axis=1, keepdims=True)


def _to_lane(col, lane_idx):
    lane = lax.broadcasted_iota(jnp.int32, (col.shape[0], LANE), 1)
    return jnp.where(lane == lane_idx, col, 0.0)


def _gdn_columns(gates_ref, run_ref, tot_ref, runt_ref, rows, h):
    return (_lane_col(gates_ref[rows, :], h), _lane_col(run_ref[rows, :], HEADS + h),
            _lane_col(tot_ref[rows, :], HEADS + h), runt_ref[pl.ds(h, 1), rows])


def _gdn_local(q, k, beta, gc, gl, grow, causal, with_kk=True):
    decay = jnp.exp(jnp.where(causal, gc - grow, -1e30))
    egc = jnp.exp(gc)
    ekd = jnp.exp(gl - gc)
    qs = q * Q_SCALE
    kb = k * beta
    if with_kk:
        both = _dot_nt(jnp.concatenate([kb, qs], axis=0), k)
        kk, qk = both[:SC], both[SC:]
    else:
        kk, qk = None, _dot_nt(qs, k)
    return beta, gl, decay, egc, ekd, qs, kb, kk, qk, jnp.where(causal, qk * decay, 0.0)


def _chunk_rows(c):
    return pl.ds(c * CHUNK if isinstance(c, int) else pl.multiple_of(c * CHUNK, CHUNK), CHUNK)


def _sc_rows(b):
    return pl.ds(b * SC if isinstance(b, int) else pl.multiple_of(b * SC, SC), SC)


def _gdn_fwd(qkv, gates, run, tot, run_t):
    t = qkv.shape[2]
    nc = t // CHUNK
    nsc = t // SC

    hp_n = GDN_HP_FWD
    heads = range(hp_n)

    def body(qkv_ref, gates_ref, run_ref, tot_ref, runt_ref, o_ref, st_ref, inv_ref, kc_s, qc_s, g_s, au_s):
        hp = pl.program_id(0)
        causal, strict, eye = _sc_masks()

        def local(b, _):
            rows = _sc_rows(b)
            loc = [_gdn_local(qkv_ref[0, hh, rows, :], qkv_ref[1, hh, rows, :],
                              *_gdn_columns(gates_ref, run_ref, tot_ref, runt_ref, rows, hp * hp_n + hh), causal)
                   for hh in heads]
            invs = _unit_lower_inverses([-jnp.where(strict, l[7] * l[2], 0.0) for l in loc], eye)
            uws = []
            for hh in heads:
                beta, _, _, egc, _, _, kb, _, _, _ = loc[hh]
                inv_ref[hh, rows, :] = invs[hh].astype(inv_ref.dtype)
                uws.append(_dot(invs[hh], jnp.concatenate([qkv_ref[2, hh, rows, :] * beta, kb * egc], axis=1)))
            for hh in heads:
                _, _, _, egc, ekd, qs, _, _, _, attn = loc[hh]
                auw = _dot(attn, uws[hh])
                g_s[hh, rows, :] = qs * egc - auw[:, DH:]
                au_s[hh, rows, :] = auw[:, :DH]
                kd = qkv_ref[1, hh, rows, :] * ekd
                for j in range(CPS):
                    sl = slice(j * CHUNK, (j + 1) * CHUNK)
                    both = _dot_tn(kd[sl], uws[hh][sl])
                    kc_s[hh, b * CPS + j] = both[:, DH:]
                    qc_s[hh, b * CPS + j] = both[:, :DH]
            return 0

        def step(c, states):
            rows = _chunk_rows(c)
            tot_row = tot_ref[pl.ds(c * CHUNK, 1), :]
            new = []
            for hh in heads:
                s = states[hh]
                st_ref[hh, c] = s
                o_ref[hh, rows, :] = _dot(g_s[hh, rows, :], s) + au_s[hh, rows, :]
                egl = jnp.exp(_lane_col(tot_row, HEADS + hp * hp_n + hh))
                new.append(egl * s - _dot(kc_s[hh, c], s) + qc_s[hh, c])
            return tuple(new)

        def steps_of(b, states):
            for j in range(CPS):
                states = step(b * CPS + j, states)
            return states

        def fused(b, states):
            states = steps_of(b - 1, states)
            local(b, 0)
            return states

        local(0, 0)
        states = lax.fori_loop(1, nsc, fused, tuple(jnp.zeros((DH, DH), F32) for _ in heads))
        steps_of(nsc - 1, states)

    whole = pl.BlockSpec((t, LANE), lambda h: (0, 0))
    once = dict(pipeline_mode=pl.Buffered(1))
    return pl.pallas_call(
        body, name="gdn_fwd", grid=(HEADS // hp_n,),
        in_specs=[pl.BlockSpec((3, hp_n, t, DH), lambda h: (0, h, 0, 0), **once), whole, whole, whole,
                  pl.BlockSpec((HEADS, t), lambda h: (1, 0))],
        out_specs=[pl.BlockSpec((hp_n, t, DH), lambda h: (h, 0, 0), **once),
                   pl.BlockSpec((hp_n, nc, DH, DH), lambda h: (h, 0, 0, 0), **once),
                   pl.BlockSpec((hp_n, t, SC), lambda h: (h, 0, 0), **once)],
        out_shape=[jax.ShapeDtypeStruct((HEADS, t, DH), F32), jax.ShapeDtypeStruct((HEADS, nc, DH, DH), F32),
                   jax.ShapeDtypeStruct((HEADS, t, SC), MXU)],
        scratch_shapes=[pltpu.VMEM((hp_n, nc, DH, DH), F32), pltpu.VMEM((hp_n, nc, DH, DH), F32),
                        pltpu.VMEM((hp_n, t, DH), F32), pltpu.VMEM((hp_n, t, DH), F32)],
        compiler_params=_params(("arbitrary",)),
    )(qkv, gates, run, tot, run_t)


def _gdn_bwd(qkv, gates, run, tot, run_t, inv, states, do):
    t = qkv.shape[2]
    nc = t // CHUNK
    nsc = t // SC

    hp_n = GDN_HP_BWD
    heads = range(hp_n)

    def body(qkv_ref, gates_ref, run_ref, tot_ref, runt_ref, inv_ref, st_ref, do_ref,
             dqkv_ref, dcol_ref, dtot_ref, drow_ref, uw_s, kc_s, h_s, dsn_s):
        hp = pl.program_id(0)
        causal, strict, _ = _sc_masks()

        @pl.when(hp == 0)
        def _():
            dcol_ref[...] = jnp.zeros_like(dcol_ref)
            dtot_ref[...] = jnp.zeros_like(dtot_ref)

        def local_of(hh, rows, with_kk=True):
            return _gdn_local(qkv_ref[0, hh, rows, :], qkv_ref[1, hh, rows, :],
                              *_gdn_columns(gates_ref, run_ref, tot_ref, runt_ref, rows, hp * hp_n + hh), causal, with_kk)

        def local(b, _):
            rows = _sc_rows(b)
            loc = [local_of(hh, rows, with_kk=False) for hh in heads]
            us, ws = [], []
            for hh in heads:
                beta, _, _, egc, _, _, kb, _, _, _ = loc[hh]
                inv_b = inv_ref[hh, rows, :]
                us.append(_dot(inv_b, qkv_ref[2, hh, rows, :] * beta))
                ws.append(_dot(inv_b, kb * egc))
            gs = [loc[hh][5] * loc[hh][3] - _dot(loc[hh][9], ws[hh]) for hh in heads]
            for hh in heads:
                uw_s[hh, rows, :] = jnp.concatenate([us[hh], ws[hh]], axis=1)
                kd = qkv_ref[1, hh, rows, :] * loc[hh][4]
                dout = do_ref[hh, rows, :]
                for j in range(CPS):
                    sl = slice(j * CHUNK, (j + 1) * CHUNK)
                    kc_s[hh, b * CPS + j] = _dot_tn(kd[sl], ws[hh][sl])
                    h_s[hh, b * CPS + j] = _dot_tn(gs[hh][sl], dout[sl])
            return 0

        def step(c, dss):
            tot_row = tot_ref[pl.ds(c * CHUNK, 1), :]
            new = []
            for hh in heads:
                ds = dss[hh]
                dsn_s[hh, c] = ds
                egl = jnp.exp(_lane_col(tot_row, HEADS + hp * hp_n + hh))
                new.append(egl * ds - _dot_tn(kc_s[hh, c], ds) + h_s[hh, c])
            return tuple(new)

        def steps_of(b, dss):
            for j in reversed(range(CPS)):
                dss = step(b * CPS + j, dss)
            return dss

        def fused(i, dss):
            b = nsc - 2 - i
            dss = steps_of(b + 1, dss)
            local(b, 0)
            return dss

        local(nsc - 1, 0)
        dss = lax.fori_loop(0, nsc - 1, fused, tuple(jnp.zeros((DH, DH), F32) for _ in heads))
        steps_of(0, dss)

        def back(b, _):
            rows = _sc_rows(b)
            first = lax.broadcasted_iota(jnp.int32, (CHUNK, 1), 0) == 0
            loc = [local_of(hh, rows) for hh in heads]
            sign = jnp.where(lax.broadcasted_iota(jnp.int32, (1, LANE), 1) < DH, 1.0, -1.0)
            mid = []
            for hh in heads:
                beta, gl, decay, egc, ekd, qs, kb, kk, qk, attn = loc[hh]
                uw = uw_s[hh, rows, :]
                kd = qkv_ref[1, hh, rows, :] * ekd
                dout = do_ref[hh, rows, :]
                dg_p, dkd_p, duw_p, dgl_p = [], [], [], []
                for j in range(CPS):
                    sl = slice(j * CHUNK, (j + 1) * CHUNK)
                    s = st_ref[hh, b * CPS + j]
                    dsn = dsn_s[hh, b * CPS + j]
                    both = _dot_nt(jnp.concatenate([dsn, dout[sl]], axis=0), s)
                    dg_p.append(both[CHUNK:])
                    dsc = jnp.concatenate([dsn, -both[:CHUNK]], axis=1)
                    dkd_p.append(_dot_nt(uw[sl], dsc))
                    duw_p.append(_dot(kd[sl], dsc))
                    degl = jnp.sum(jnp.sum(s * dsn, axis=1, keepdims=True), axis=0, keepdims=True)
                    dgl_p.append(jnp.where(first, degl * jnp.exp(gl[j * CHUNK:j * CHUNK + 1, :]), 0.0))
                dg, dkd = jnp.concatenate(dg_p, axis=0), jnp.concatenate(dkd_p, axis=0)
                dod = jnp.concatenate([dout, dg], axis=1)
                da = jnp.where(causal, _dot_nt(dod, uw * sign), 0.0)
                duw = jnp.concatenate(duw_p, axis=0) + _dot_tn(attn, dod) * sign
                mid.append((kd, dg, dkd, da, duw, jnp.concatenate(dgl_p, axis=0)))
            inv_ts = [inv_ref[hh, rows, :].astype(F32).T for hh in heads]
            its = [_dot(inv_ts[hh], mid[hh][4]) for hh in heads]
            dinvs = []
            for hh in heads:
                beta, _, _, egc, _, _, kb, _, _, _ = loc[hh]
                dinvs.append(_dot_nt(mid[hh][4], jnp.concatenate([qkv_ref[2, hh, rows, :] * beta, kb * egc], axis=1)))
            half = [_dot(inv_ts[hh], dinvs[hh]) for hh in heads]
            dls = [jnp.where(strict, -_dot(half[hh], inv_ts[hh]), 0.0) for hh in heads]
            for hh in heads:
                head = hp * hp_n + hh
                beta, gl, decay, egc, ekd, qs, kb, kk, qk, attn = loc[hh]
                kd, dg, dkd, da, _, dgl_first = mid[hh]
                k, v = qkv_ref[1, hh, rows, :], qkv_ref[2, hh, rows, :]
                dvb, dkbe = its[hh][:, :DH], its[hh][:, DH:]
                dl = dls[hh]
                dlogd = (dl * kk + da * qk) * decay
                dd = jnp.concatenate([dl * decay, da * decay], axis=0)
                ddk = _dot(dd, k)
                dkb = ddk[:SC] + dkbe * egc
                dqs = ddk[SC:] + dg * egc
                dk = _dot_tn(dd, jnp.concatenate([kb, qs], axis=0)) + dkd * ekd + dkb * beta
                dkd_kd = jnp.sum(dkd * kd, axis=1, keepdims=True)
                narrow = dg * (qs * egc) + dkbe * (kb * egc)
                wide = dlogd[:, :LANE] + dlogd[:, LANE:] + jnp.concatenate([narrow, jnp.zeros((SC, LANE - DH), F32)], axis=1)
                dgc = jnp.sum(wide, axis=1, keepdims=True) - dkd_kd
                dbeta = jnp.sum(dkb * k + dvb * v, axis=1, keepdims=True)
                dqkv_ref[0, hh, rows, :] = dqs * Q_SCALE
                dqkv_ref[1, hh, rows, :] = dk
                dqkv_ref[2, hh, rows, :] = dvb * beta
                dcol_ref[rows, :] += _to_lane(dbeta, head) + _to_lane(dgc, HEADS + head)
                dtot_ref[rows, :] += _to_lane(dkd_kd + dgl_first, HEADS + head)
                drow_ref[pl.ds(head, 1), rows] = -jnp.sum(dlogd, axis=0, keepdims=True)
            return 0

        lax.fori_loop(0, nsc, back, 0)

    whole = pl.BlockSpec((t, LANE), lambda h: (0, 0))
    rowspec = pl.BlockSpec((HEADS, t), lambda h: (0, 0))
    sq = pltpu.VMEM((hp_n, nc, DH, DH), F32)
    per_head = pltpu.VMEM((hp_n, t, 2 * DH), F32)
    return pl.pallas_call(
        body, name="gdn_bwd", grid=(HEADS // hp_n,),
        in_specs=[pl.BlockSpec((3, hp_n, t, DH), lambda h: (0, h, 0, 0)), whole, whole, whole,
                  pl.BlockSpec((HEADS, t), lambda h: (1, 0)),
                  pl.BlockSpec((hp_n, t, SC), lambda h: (h, 0, 0)), pl.BlockSpec((hp_n, nc, DH, DH), lambda h: (h, 0, 0, 0)),
                  pl.BlockSpec((hp_n, t, DH), lambda h: (h, 0, 0))],
        out_specs=[pl.BlockSpec((3, hp_n, t, DH), lambda h: (0, h, 0, 0)), whole, whole, rowspec],
        out_shape=[jax.ShapeDtypeStruct((3, HEADS, t, DH), F32), jax.ShapeDtypeStruct((t, LANE), F32),
                   jax.ShapeDtypeStruct((t, LANE), F32), jax.ShapeDtypeStruct((HEADS, t), F32)],
        scratch_shapes=[per_head, sq, sq, sq],
        compiler_params=_params(("arbitrary",)),
    )(qkv, gates, run, tot, run_t, inv, states, do)


FOX_HP_FWD = 4
FOX_HP_BWD = 4


def _wide_t(a):
    r = a.shape[0]
    return jnp.concatenate([a, jnp.zeros((r, LANE - DH), F32)], axis=1).T[:DH]


def _tall_t(a):
    r = a.shape[1]
    return jnp.concatenate([a, jnp.zeros((LANE - DH, r), F32)], axis=0).T[:, :DH]


def _key_side_f(run_blk, head, qb):
    col = jnp.broadcast_to(_lane_col(run_blk, 2 * HEADS + head), (run_blk.shape[0], LANE))
    return jnp.concatenate([col] * (qb // LANE), axis=1)


def _diag_mask(qb):
    return lax.broadcasted_iota(jnp.int32, (qb, qb), 0) <= lax.broadcasted_iota(jnp.int32, (qb, qb), 1)


def _fox_fwd(qkv, run, run_t, carry=None):
    t = qkv.shape[2]
    qb = min(QB, t)
    nq = t // qb
    hp_n = FOX_HP_FWD
    c_in, c_in_specs, c_out_shape, c_out_specs, c_sem = _carry_operands(*carry) if carry else ([], [], [], [], None)
    n_c = len(c_in)

    def body(*refs):
        q_ref, k_ref, v_ref, run_ref, runt_ref = refs[:5]
        o_ref, lse_ref = refs[5 + n_c:7 + n_c]
        s0 = 7 + n_c + len(c_out_shape)
        kb_s, vt_s, raw_s, m_s, l_s, acc_s = refs[s0:s0 + 6]
        hp = pl.program_id(0)
        i = pl.program_id(1)

        if carry:
            _carry_step(carry[0], refs[5:5 + n_c], refs[7 + n_c], refs[8 + n_c], refs[-1], hp * nq + i,
                        (HEADS // hp_n) * nq)

        @pl.when(i == 0)
        def _():
            for hh in range(hp_n):
                kb_s[hh] = k_ref[0, hh].astype(MXU)
                for b in range(nq):
                    rows = slice(b * qb, (b + 1) * qb)
                    vt_s[hh, :, rows] = _wide_t(v_ref[0, hh, rows, :]).astype(MXU)

        qrows = pl.ds(pl.multiple_of(i * qb, qb), qb)
        qs = [(q_ref[0, hh] * Q_SCALE).astype(MXU) for hh in range(hp_n)]
        fq = [runt_ref[pl.ds(hp * hp_n + hh, 1), qrows] for hh in range(hp_n)]

        def block_rows(j):
            return pl.ds(pl.multiple_of(j * qb, qb), qb)

        def scores(j):
            for hh in range(hp_n):
                raw_s[j % 2, hh] = _dot_nt(kb_s[hh, block_rows(j), :], qs[hh])

        def absorb(j, diagonal):
            rows = block_rows(j)
            run_blk = run_ref[rows, :]
            stats, pv = [], []
            for hh in range(hp_n):
                m, l = m_s[hh], l_s[hh]
                st = raw_s[j % 2, hh] + (fq[hh] - _key_side_f(run_blk, hp * hp_n + hh, qb))
                if diagonal:
                    st = jnp.where(_diag_mask(qb), st, -1e30)
                m_new = jnp.maximum(m, jnp.max(st, axis=0, keepdims=True))
                p = jnp.exp(st - m_new)
                alpha = jnp.exp(m - m_new)
                stats.append((m_new, alpha * l + jnp.sum(p, axis=0, keepdims=True), alpha))
                pv.append(_dot(vt_s[hh, :, rows], p))
            for hh in range(hp_n):
                m_s[hh], l_s[hh] = stats[hh][0], stats[hh][1]
                acc_s[hh] = stats[hh][2] * acc_s[hh] + pv[hh]

        def kstep(j, _):
            scores(j + 1)
            absorb(j, False)
            return 0

        for hh in range(hp_n):
            m_s[hh] = jnp.full((1, qb), -1e30, F32)
            l_s[hh] = jnp.zeros((1, qb), F32)
            acc_s[hh] = jnp.zeros((DH, qb), F32)
        scores(0)
        lax.fori_loop(0, i, kstep, 0)
        absorb(i, True)
        for hh in range(hp_n):
            l = l_s[hh]
            o_ref[hh] = _tall_t(acc_s[hh] / l)
            lse_ref[pl.ds(hp * hp_n + hh, 1), qrows] = m_s[hh] + jnp.log(l)

    out = pl.pallas_call(
        body, name="fox_fwd", grid=(HEADS // hp_n, nq),
        in_specs=[pl.BlockSpec((1, hp_n, qb, DH), lambda h, i: (0, h, i, 0)),
                  pl.BlockSpec((1, hp_n, t, DH), lambda h, i: (1, h, 0, 0)),
                  pl.BlockSpec((1, hp_n, t, DH), lambda h, i: (2, h, 0, 0)),
                  pl.BlockSpec((t, LANE), lambda h, i: (0, 0)),
                  pl.BlockSpec((HEADS, t), lambda h, i: (2, 0))] + c_in_specs,
        out_specs=[pl.BlockSpec((hp_n, qb, DH), lambda h, i: (h, i, 0)),
                   pl.BlockSpec((HEADS, t), lambda h, i: (0, 0))] + c_out_specs,
        out_shape=[jax.ShapeDtypeStruct((HEADS, t, DH), F32), jax.ShapeDtypeStruct((HEADS, t), F32)] + c_out_shape,
        input_output_aliases={5 + j: 4 + j for j in range(n_c)},
        scratch_shapes=[pltpu.VMEM((hp_n, t, DH), MXU), pltpu.VMEM((hp_n, DH, t), MXU), pltpu.VMEM((2, hp_n, qb, qb), F32),
                        pltpu.VMEM((hp_n, 1, qb), F32), pltpu.VMEM((hp_n, 1, qb), F32), pltpu.VMEM((hp_n, DH, qb), F32)]
        + ([c_sem] if carry else []),
        compiler_params=_params(("arbitrary", "arbitrary")),
    )(qkv, qkv, qkv, run, run_t, *c_in)
    return (out[0], out[1], _carry_state(out[2:])) if carry else (out[0], out[1])


def _fox_bwd(qkv, run, run_t, o, lse, do, carry=None):
    t = qkv.shape[2]
    qb = min(QB, t)
    nq = t // qb
    hp_n = FOX_HP_BWD
    c_in, c_in_specs, c_out_shape, c_out_specs, c_sem = _carry_operands(*carry) if carry else ([], [], [], [], None)
    n_c = len(c_in)

    def body(*refs):
        q_ref, k_ref, v_ref, run_ref, runt_ref, o_ref, lse_ref, do_ref = refs[:8]
        dqkv_ref, dcol_ref, drow_ref = refs[8 + n_c:11 + n_c]
        s0 = 11 + n_c + len(c_out_shape)
        dqt_s, raw_s, dpt_s, dro_s, dk_s, dv_s, dsum_s = refs[s0:s0 + 7]
        hp = pl.program_id(0)
        j = pl.program_id(1)

        if carry:
            _carry_step(carry[0], refs[8:8 + n_c], refs[11 + n_c], refs[12 + n_c], refs[-1], hp * nq + j,
                        (HEADS // hp_n) * nq)

        @pl.when(j == 0)
        def _():
            dqt_s[...] = jnp.zeros_like(dqt_s)

        @pl.when((j == 0) & (hp == 0))
        def _():
            dcol_ref[...] = jnp.zeros_like(dcol_ref)
            drow_ref[...] = jnp.zeros_like(drow_ref)

        krows = pl.ds(pl.multiple_of(j * qb, qb), qb)
        run_blk = run_ref[krows, :]
        ones8 = jnp.ones((8, DH), MXU)
        kb, kt, vb, fk = [], [], [], []
        for hh in range(hp_n):
            kf = k_ref[0, hh]
            kb.append(kf.astype(MXU))
            kt.append(_wide_t(kf).astype(MXU))
            vb.append(v_ref[0, hh].astype(MXU))
            fk.append(_key_side_f(run_blk, hp * hp_n + hh, qb))

        def block_rows(i):
            return pl.ds(i * qb if isinstance(i, int) else pl.multiple_of(i * qb, qb), qb)

        def products(i):
            rows = block_rows(i)
            slot = i % 2
            for hh in range(hp_n):
                dout = do_ref[hh, rows, :]
                x = dout * o_ref[hh, rows, :]
                x_hi = x.astype(MXU)
                raw_s[slot, hh] = _dot_nt(kb[hh], q_ref[0, hh, rows, :] * Q_SCALE)
                dpt_s[slot, hh] = _dot_nt(vb[hh], dout)
                dro_s[slot, hh] = _dot_nt(ones8, x_hi) + _dot_nt(ones8, x - x_hi.astype(F32))

        def absorb(i, diagonal):
            rows = block_rows(i)
            slot = i % 2
            pieces = []
            for hh in range(hp_n):
                head = hp * hp_n + hh
                raw, dpt, drow = raw_s[slot, hh], dpt_s[slot, hh], dro_s[slot, hh, 0:1, :]
                off = runt_ref[pl.ds(head, 1), rows] - lse_ref[pl.ds(head, 1), rows]
                st = raw + (off - fk[hh])
                if diagonal:
                    st = jnp.where(_diag_mask(qb), st, -1e30)
                pt = jnp.exp(st)
                dst = pt * (dpt - drow)
                drow_ref[pl.ds(head, 1), rows] += jnp.sum(dst, axis=0, keepdims=True)
                folded = dst[:, 0:LANE]
                for c in range(1, qb // LANE):
                    folded = folded + dst[:, c * LANE:(c + 1) * LANE]
                pieces.append((_dot(dst, q_ref[0, hh, rows, :] * Q_SCALE), _dot(pt, do_ref[hh, rows, :]),
                               _dot(kt[hh], dst), folded))
            for hh in range(hp_n):
                dqt_s[hh, :, rows] += pieces[hh][2]
                if diagonal:
                    dk_s[hh], dv_s[hh], dsum_s[hh] = pieces[hh][0], pieces[hh][1], pieces[hh][3]
                else:
                    dk_s[hh] += pieces[hh][0]
                    dv_s[hh] += pieces[hh][1]
                    dsum_s[hh] += pieces[hh][3]

        def qstep(i, _):
            products(i + 1)
            absorb(i, False)
            return 0

        products(j)
        products(jnp.minimum(j + 1, nq - 1))
        absorb(j, True)
        lax.fori_loop(j + 1, nq - 1, qstep, 0)

        @pl.when(j < nq - 1)
        def _():
            absorb(nq - 1, False)
        for hh in range(hp_n):
            dqkv_ref[1, hh, krows, :] = dk_s[hh]
            dqkv_ref[2, hh, krows, :] = dv_s[hh]
            dcol_ref[krows, :] += _to_lane(-jnp.sum(dsum_s[hh], axis=1, keepdims=True), 2 * HEADS + hp * hp_n + hh)

        @pl.when(j == nq - 1)
        def _():
            for hh in range(hp_n):
                for b in range(nq):
                    rows = slice(b * qb, (b + 1) * qb)
                    dqkv_ref[0, hh, rows, :] = _tall_t(dqt_s[hh, :, rows]) * Q_SCALE

    once = dict(pipeline_mode=pl.Buffered(1))
    full = pl.BlockSpec((hp_n, t, DH), lambda h, j: (h, 0, 0), **once)
    rows8 = pl.BlockSpec((HEADS, t), lambda h, j: (0, 0))
    out = pl.pallas_call(
        body, name="fox_bwd", grid=(HEADS // hp_n, nq),
        in_specs=[pl.BlockSpec((1, hp_n, t, DH), lambda h, j: (0, h, 0, 0), **once),
                  pl.BlockSpec((1, hp_n, qb, DH), lambda h, j: (1, h, j, 0)),
                  pl.BlockSpec((1, hp_n, qb, DH), lambda h, j: (2, h, j, 0)),
                  pl.BlockSpec((t, LANE), lambda h, j: (0, 0), **once), pl.BlockSpec((HEADS, t), lambda h, j: (2, 0)),
                  full, rows8, full] + c_in_specs,
        out_specs=[pl.BlockSpec((3, hp_n, t, DH), lambda h, j: (0, h, 0, 0), **once),
                   pl.BlockSpec((t, LANE), lambda h, j: (0, 0)), rows8] + c_out_specs,
        out_shape=[jax.ShapeDtypeStruct((3, HEADS, t, DH), F32), jax.ShapeDtypeStruct((t, LANE), F32),
                   jax.ShapeDtypeStruct((HEADS, t), F32)] + c_out_shape,
        input_output_aliases={8 + j: 5 + j for j in range(n_c)},
        scratch_shapes=[pltpu.VMEM((hp_n, DH, t), F32), pltpu.VMEM((2, hp_n, qb, qb), F32), pltpu.VMEM((2, hp_n, qb, qb), F32),
                        pltpu.VMEM((2, hp_n, 8, qb), F32), pltpu.VMEM((hp_n, qb, DH), F32), pltpu.VMEM((hp_n, qb, DH), F32),
                        pltpu.VMEM((hp_n, qb, LANE), F32)] + ([c_sem] if carry else []),
        compiler_params=_params(("arbitrary", "arbitrary")),
    )(qkv, qkv, qkv, run, run_t, o, lse, do, *c_in)
    return (out[0], out[1], out[2], _carry_state(out[3:])) if carry else tuple(out)


Z_COL0 = 3 * WIDTH // LANE
FGATE_COL0 = 7 * WIDTH // LANE


def _gdn_post(o, pm, onw):
    t = pm.shape[0]

    def body(o_ref, z_ref, w_ref, m_ref):
        z = z_ref[...]
        sz = z * _sigmoid(z)
        w2 = jnp.concatenate([w_ref[...], w_ref[...]], axis=1)
        ov = jnp.concatenate([o_ref[0], o_ref[1]], axis=1)
        n = ov * lax.rsqrt(_group_sum(ov * ov, _head_sum_matrix()) * (1.0 / DH) + EPS) * w2
        m_ref[...] = (n * sz).astype(m_ref.dtype)

    return pl.pallas_call(
        body, name="gdn_post", grid=(WIDTH // LANE,),
        in_specs=[pl.BlockSpec((2, t, DH), lambda j: (j, 0, 0)), pl.BlockSpec((t, LANE), lambda j: (0, Z_COL0 + j)),
                  pl.BlockSpec((1, DH), lambda j: (0, 0))],
        out_specs=pl.BlockSpec((t, LANE), lambda j: (0, j)),
        out_shape=jax.ShapeDtypeStruct((t, WIDTH), MXU),
        compiler_params=_params(("arbitrary",)),
    )(o, pm, onw)


def _gdn_post_bwd(o, pm, onw, dmix):
    t = pm.shape[0]

    def body(o_ref, z_ref, w_ref, dm_ref, do_ref, dz_ref, dw_ref):
        @pl.when(pl.program_id(0) == 0)
        def _():
            dw_ref[...] = jnp.zeros_like(dw_ref)

        z = z_ref[...]
        sg = _sigmoid(z)
        sz = z * sg
        dsz = sg * (1.0 + z * (1.0 - sg))
        dm = dm_ref[...]
        hs = _head_sum_matrix()
        w2 = jnp.concatenate([w_ref[...], w_ref[...]], axis=1)
        ov = jnp.concatenate([o_ref[0], o_ref[1]], axis=1)
        r = lax.rsqrt(_group_sum(ov * ov, hs) * (1.0 / DH) + EPS)
        xn = ov * r
        dn = dm * sz
        dz_ref[...] = (dm * (xn * w2) * dsz).astype(dz_ref.dtype)
        dw2 = jnp.sum(dn * xn, axis=0, keepdims=True)
        dw_ref[...] += dw2[:, :DH] + dw2[:, DH:]
        g = dn * w2
        do = r * (g - xn * (_group_sum(g * xn, hs) * (1.0 / DH)))
        do_ref[0] = do[:, :DH]
        do_ref[1] = do[:, DH:]

    return pl.pallas_call(
        body, name="gdn_post_bwd", grid=(WIDTH // LANE,),
        in_specs=[pl.BlockSpec((2, t, DH), lambda j: (j, 0, 0)), pl.BlockSpec((t, LANE), lambda j: (0, Z_COL0 + j)),
                  pl.BlockSpec((1, DH), lambda j: (0, 0)), pl.BlockSpec((t, LANE), lambda j: (0, j))],
        out_specs=[pl.BlockSpec((2, t, DH), lambda j: (j, 0, 0)), pl.BlockSpec((t, LANE), lambda j: (0, j)),
                   pl.BlockSpec((1, DH), lambda j: (0, 0))],
        out_shape=[jax.ShapeDtypeStruct((HEADS, t, DH), F32), jax.ShapeDtypeStruct((t, WIDTH), MXU),
                   jax.ShapeDtypeStruct((1, DH), F32)],
        compiler_params=_params(("arbitrary",)),
    )(o, pm, onw, dmix)


def _fox_post(o, pm):
    t = pm.shape[0]

    def body(o_ref, g_ref, m_ref):
        m_ref[...] = (jnp.concatenate([o_ref[0], o_ref[1]], axis=1) * _sigmoid(g_ref[...])).astype(m_ref.dtype)

    return pl.pallas_call(
        body, name="fox_post", grid=(WIDTH // LANE,),
        in_specs=[pl.BlockSpec((2, t, DH), lambda j: (j, 0, 0)), pl.BlockSpec((t, LANE), lambda j: (0, FGATE_COL0 + j))],
        out_specs=pl.BlockSpec((t, LANE), lambda j: (0, j)),
        out_shape=jax.ShapeDtypeStruct((t, WIDTH), MXU),
        compiler_params=_params(("arbitrary",)),
    )(o, pm)


def _fox_post_bwd(o, pm, dmix):
    t = pm.shape[0]

    def body(o_ref, g_ref, dm_ref, do_ref, dg_ref):
        sg = _sigmoid(g_ref[...])
        dm = dm_ref[...]
        for hh in range(2):
            cols = slice(hh * DH, (hh + 1) * DH)
            do_ref[hh] = dm[:, cols] * sg[:, cols]
            dg_ref[:, cols] = (dm[:, cols] * o_ref[hh] * (sg * (1.0 - sg))[:, cols]).astype(dg_ref.dtype)

    return pl.pallas_call(
        body, name="fox_post_bwd", grid=(WIDTH // LANE,),
        in_specs=[pl.BlockSpec((2, t, DH), lambda j: (j, 0, 0)), pl.BlockSpec((t, LANE), lambda j: (0, FGATE_COL0 + j)),
                  pl.BlockSpec((t, LANE), lambda j: (0, j))],
        out_specs=[pl.BlockSpec((2, t, DH), lambda j: (j, 0, 0)), pl.BlockSpec((t, LANE), lambda j: (0, j))],
        out_shape=[jax.ShapeDtypeStruct((HEADS, t, DH), F32), jax.ShapeDtypeStruct((t, WIDTH), MXU)],
        compiler_params=_params(("arbitrary",)),
    )(o, pm, dmix)


def _tail(x, mixg, mixf, tgt, wo, n2w, wg_t, wu_t, wd, fw):
    t, d = x.shape
    dff = wd.shape[0]
    tb = min(TB, t)

    def body(x_ref, mg_ref, mf_ref, t_ref, wo_ref, n2_ref, wg_ref, wu_ref, wd_ref, fw_ref,
             h2_ref, act_ref, dgate_ref, dup_ref, dx3_ref, dx2_ref, dmg_ref, dmf_ref, dn2_ref, dfw_ref, loss_ref):
        @pl.when(pl.program_id(0) == 0)
        def _():
            dn2_ref[...] = jnp.zeros_like(dn2_ref)
            dfw_ref[...] = jnp.zeros_like(dfw_ref)
            loss_ref[...] = jnp.zeros_like(loss_ref)

        x2 = x_ref[...] + _dot(mg_ref[...], wo_ref[0:WIDTH, :]) + _dot(mf_ref[...], wo_ref[WIDTH:2 * WIDTH, :])
        r2 = lax.rsqrt(jnp.mean(x2 * x2, axis=-1, keepdims=True) + EPS)
        xn2 = x2 * r2
        h2 = (xn2 * n2_ref[...]).astype(MXU)
        h2_ref[...] = h2
        gate = _dot_nt(h2, wg_ref[...])
        up = _dot_nt(h2, wu_ref[...])
        sg = _sigmoid(gate)
        sl = gate * sg
        act = (sl * up).astype(MXU)
        act_ref[...] = act
        x3 = x2 + _dot(act, wd_ref[...])
        r3 = lax.rsqrt(jnp.mean(x3 * x3, axis=-1, keepdims=True) + EPS)
        xn3 = x3 * r3
        err = xn3 * fw_ref[...] - t_ref[...]
        loss_ref[...] += 0.5 * jnp.sum(jnp.mean(err * err, axis=-1, keepdims=True), axis=0, keepdims=True)
        dy = err * (1.0 / d)
        dfw_ref[...] += jnp.sum(dy * xn3, axis=0, keepdims=True)
        g3 = dy * fw_ref[...]
        dx3 = r3 * (g3 - xn3 * jnp.mean(g3 * xn3, axis=-1, keepdims=True))
        dx3_ref[...] = dx3.astype(MXU)
        dact = _dot_nt(dx3, wd_ref[...])
        dgate = (dact * up * (sg * (1.0 + gate * (1.0 - sg)))).astype(MXU)
        dup = (dact * sl).astype(MXU)
        dgate_ref[...] = dgate
        dup_ref[...] = dup
        dh2 = _dot(dgate, wg_ref[...]) + _dot(dup, wu_ref[...])
        dn2_ref[...] += jnp.sum(dh2 * xn2, axis=0, keepdims=True)
        g2 = dh2 * n2_ref[...]
        dx2 = dx3 + r2 * (g2 - xn2 * jnp.mean(g2 * xn2, axis=-1, keepdims=True))
        dx2_ref[...] = dx2
        dmg_ref[...] = _dot_nt(dx2, wo_ref[0:WIDTH, :])
        dmf_ref[...] = _dot_nt(dx2, wo_ref[WIDTH:2 * WIDTH, :])

    def tok(n):
        return pl.BlockSpec((tb, n), lambda i: (i, 0))

    acc = pl.BlockSpec((1, d), lambda i: (0, 0))
    sds = jax.ShapeDtypeStruct
    return pl.pallas_call(
        body, name="tail", grid=(t // tb,),
        in_specs=[tok(d), tok(WIDTH), tok(WIDTH), tok(d), _resident(wo.shape), _resident((1, d)),
                  _resident(wg_t.shape), _resident(wu_t.shape), _resident(wd.shape), _resident((1, d))],
        out_specs=[tok(d), tok(dff), tok(dff), tok(dff), tok(d), tok(d), tok(WIDTH), tok(WIDTH), acc, acc,
                   pl.BlockSpec((1, 1), lambda i: (0, 0))],
        out_shape=[sds((t, d), MXU), sds((t, dff), MXU), sds((t, dff), MXU), sds((t, dff), MXU), sds((t, d), MXU),
                   sds((t, d), F32), sds((t, WIDTH), F32), sds((t, WIDTH), F32), sds((1, d), F32), sds((1, d), F32),
                   sds((1, 1), F32)],
        compiler_params=_params(("arbitrary",)),
    )(x, mixg, mixf, tgt, wo, n2w, wg_t, wu_t, wd, fw)


WGRAD_SLOTS = 3


def _wgrads(a_list, b, name):
    t, n = b.shape
    ms = [a.shape[1] for a in a_list]
    bms = [256 if m % 256 == 0 else LANE for m in ms]
    nbs = [m // bm for m, bm in zip(ms, bms)]
    k = len(a_list)
    cast = b.dtype != jnp.dtype(MXU)

    def body(*refs):
        a_refs, b_ref, o_refs = refs[:k], refs[k], refs[k + 1:2 * k + 1]
        bufs, sems = refs[2 * k + 1:3 * k + 1], refs[3 * k + 1:4 * k + 1]
        i = pl.program_id(0)
        if cast:
            @pl.when(i == 0)
            def _():
                refs[-1][...] = b_ref[...].astype(MXU)
        for a_ref, o_ref, buf, sem, bm, nb in zip(a_refs, o_refs, bufs, sems, bms, nbs):
            def fetch(blk, a_ref=a_ref, buf=buf, sem=sem, bm=bm):
                start = blk * bm if isinstance(blk, int) else pl.multiple_of(blk * bm, bm)
                slot = blk % WGRAD_SLOTS
                return pltpu.make_async_copy(a_ref.at[:, pl.ds(start, bm)], buf.at[slot], sem.at[slot])

            @pl.when(i == 0)
            def _(fetch=fetch, nb=nb):
                for blk in range(min(WGRAD_SLOTS - 1, nb)):
                    fetch(blk).start()

            @pl.when(i + WGRAD_SLOTS - 1 < nb)
            def _(fetch=fetch):
                fetch(i + WGRAD_SLOTS - 1).start()

            @pl.when(i < nb)
            def _(fetch=fetch, o_ref=o_ref, buf=buf):
                fetch(i).wait()
                o_ref[...] = _dot_tn(buf[i % WGRAD_SLOTS], refs[-1][...] if cast else b_ref[...]).astype(o_ref.dtype)

    def clamp(nb):
        return lambda i: jnp.minimum(i, nb - 1)

    return pl.pallas_call(
        body, name=name, grid=(max(nbs),),
        in_specs=[pl.BlockSpec(memory_space=pl.ANY)] * k + [_resident((t, n))],
        out_specs=[pl.BlockSpec((bm, n), lambda i, c=clamp(nb): (c(i), 0)) for bm, nb in zip(bms, nbs)],
        out_shape=[jax.ShapeDtypeStruct((m, n), WIRE) for m in ms],
        scratch_shapes=[pltpu.VMEM((WGRAD_SLOTS, t, bm), a.dtype) for a, bm in zip(a_list, bms)]
        + [pltpu.SemaphoreType.DMA((WGRAD_SLOTS,)) for _ in a_list] + ([pltpu.VMEM((t, n), MXU)] if cast else []),
        compiler_params=_params(("arbitrary",)),
    )(*a_list, b)


def _merge_dw_in(d_gdn, d_z, d_fox, d_fg, d_small, shards=False):
    pieces = [d_gdn, d_z, d_small[:2 * HEADS], d_fox, d_fg, d_small[2 * HEADS:3 * HEADS]]
    if not shards:
        return jnp.concatenate(pieces, axis=0)
    n = sum(p.shape[0] for p in pieces) // N_DEV
    out = []
    for dev in range(N_DEV):
        parts, first = [], 0
        for p in pieces:
            lo, hi = max(dev * n, first), min((dev + 1) * n, first + p.shape[0])
            if lo < hi:
                parts.append(p[lo - first:hi - first])
            first += p.shape[0]
        out.append(jnp.concatenate(parts, axis=0))
    return jnp.stack(out)


def _lanes(*pieces):
    v = jnp.concatenate([p.reshape(-1).astype(F32) for p in pieces])
    return jnp.pad(v, (0, LANE - v.shape[0])).reshape(1, LANE)


def _vector_params(p):
    d = p["norm1_w"].size
    gparams = jnp.concatenate([_lanes(jnp.zeros(HEADS), p["gdn_dt_bias"], p["fox_f_bias"]),
                               _lanes(jnp.zeros(HEADS), p["gdn_A_log"]), jnp.zeros((6, LANE), F32)])
    fox_nw = jnp.stack([jnp.tile(p["fox_q_norm_w"].reshape(-1), 2), jnp.tile(p["fox_k_norm_w"].reshape(-1), 2),
                        jnp.ones((LANE,), F32)])
    return dict(n1w=p["norm1_w"].reshape(1, d), n2w=p["norm2_w"].reshape(1, d), fw=p["final_norm_w"].reshape(1, d),
                onw=p["gdn_out_norm_w"].reshape(1, DH), gparams=gparams, fox_nw=fox_nw)


def _mixer_forward(x, vp, w_t, ws_t, conv_w, carry=None, carry_first=None):
    h1, pm, ps, *first = _inproj(x, vp["n1w"], w_t, ws_t, carry_first)
    gates, run, tot, run_t = _gates(ps, vp["gparams"])
    fqkv = _fox_prep(pm, vp["fox_nw"])
    o_fox, lse, *carried = _fox_fwd(fqkv, run, run_t, carry)
    if carry:
        pm, o_fox = lax.optimization_barrier((pm, o_fox))
    mixf = _fox_post(o_fox, pm)
    gqkv = _gdn_prep(pm, conv_w)
    o_gdn, states, inv = _gdn_fwd(gqkv, gates, run, tot, run_t)
    mixg = _gdn_post(o_gdn, pm, vp["onw"])
    return dict(h1=h1, pm=pm, ps=ps, gates=gates, run=run, tot=tot, run_t=run_t, gqkv=gqkv, o_gdn=o_gdn, states=states,
                inv=inv, mixg=mixg, fqkv=fqkv, o_fox=o_fox, lse=lse, mixf=mixf, carried=carried[0] if carried else None,
                carried_first=first[0] if first else None)


def _mixer_backward(x, vp, w_t, ws_t, conv_w, f, dx2, dmixg, dmixf, carry=None, scatter_own=False):
    pm = f["pm"]
    do_fox, dfg = _fox_post_bwd(f["o_fox"], pm, dmixf)
    dfqkv, dcol_f, drow_f, *carried = _fox_bwd(f["fqkv"], f["run"], f["run_t"], f["o_fox"], f["lse"], do_fox, carry)
    dfox, dfnw = _fox_prep_bwd(pm, vp["fox_nw"], dfqkv)
    do_gdn, dz, donw = _gdn_post_bwd(f["o_gdn"], pm, vp["onw"], dmixg)
    dgqkv, dcol_g, dtot_g, drow_g = _gdn_bwd(f["gqkv"], f["gates"], f["run"], f["tot"], f["run_t"], f["inv"], f["states"], do_gdn)
    dgdn, dconv = _gdn_prep_bwd(pm, conv_w, dgqkv)
    dps, gsum = _gates_bwd(f["ps"], vp["gparams"], dcol_g, dtot_g, dcol_f, drow_g, drow_f)
    dw_pieces = _wgrads([dgdn, dz, dfox, dfg, dps], f["h1"], "dw_in")
    dw_in = _merge_dw_in(*dw_pieces)
    own = ("scatter", [_merge_dw_in(*dw_pieces, shards=True), _cut(dconv, 1)]) if scatter_own else None
    grad_x, dn1w, *sent = _inproj_bwd(x, vp["n1w"], dx2, dgdn, dz, dfox, dfg, dps, w_t, ws_t, own)
    small = dict(dn1w=dn1w, gsum=gsum, donw=donw, dfnw=dfnw)
    return (grad_x, dw_in, dconv, small, *carried, *sent)


VECTORS = ("norm1_w", "norm2_w", "final_norm_w", "gdn_A_log", "gdn_dt_bias", "gdn_out_norm_w", "fox_f_bias",
           "fox_q_norm_w", "fox_k_norm_w")
VEC_ROWS = 16
LOSS_ROW = len(VECTORS)


def _pack_vectors(dn1w, dn2w, dfw, gsum, donw, dfnw, loss):
    d = dn1w.shape[1]

    def body(n1_ref, n2_ref, fw_ref, gs_ref, on_ref, fn_ref, loss_ref, o_ref):
        o_ref[...] = jnp.zeros_like(o_ref)
        o_ref[0:1, :] = n1_ref[...]
        o_ref[1:2, :] = n2_ref[...]
        o_ref[2:3, :] = fw_ref[...]
        o_ref[3:4, 0:HEADS] = gs_ref[0:1, 0:HEADS]
        o_ref[4:5, 0:HEADS] = gs_ref[1:2, 0:HEADS]
        o_ref[5:6, 0:DH] = on_ref[...]
        o_ref[6:7, 0:HEADS] = gs_ref[2:3, 0:HEADS]
        for kind in range(2):
            v = fn_ref[kind, 0]
            for j in range(1, fn_ref.shape[1]):
                v = v + fn_ref[kind, j]
            o_ref[7 + kind:8 + kind, 0:DH] = v[:, :DH] + v[:, DH:]
        o_ref[LOSS_ROW:LOSS_ROW + 1, 0:1] = loss_ref[...]

    return pl.pallas_call(body, name="pack_vectors", out_shape=jax.ShapeDtypeStruct((VEC_ROWS, d), F32),
                          compiler_params=_params())(dn1w, dn2w, dfw, gsum, donw, dfnw, loss)


def _late_grads(f, h2, act, dgate, dup, dx3, dx2):
    dw_gate, dw_up = _wgrads([dgate, dup], h2, "dw_gate_up")
    (dw_down,) = _wgrads([act], dx3, "dw_down")
    return {"w_out": jnp.concatenate(_wgrads([f["mixg"], f["mixf"]], dx2, "dw_out"), axis=0),
            "w_ffn_gate": dw_gate, "w_ffn_up": dw_up, "w_ffn_down": dw_down}


def _local_step(x, tgt, p, w_in_t, conv_w, wo, wg_t, wu_t, wd):
    vp = _vector_params(p)
    ws_t = _small_rows(w_in_t)
    f = _mixer_forward(x, vp, w_in_t, ws_t, conv_w)
    (h2, act, dgate, dup, dx3, dx2, dmixg, dmixf, dn2w, dfw, loss) = _tail(
        x, f["mixg"], f["mixf"], tgt, wo, vp["n2w"], wg_t, wu_t, wd, vp["fw"])
    grad_x, dw_in, dconv, small = _mixer_backward(x, vp, w_in_t, ws_t, conv_w, f, dx2, dmixg, dmixf)
    grads = {"w_in": dw_in, "gdn_conv_w": dconv, **_late_grads(f, h2, act, dgate, dup, dx3, dx2)}
    vec = _pack_vectors(small["dn1w"], dn2w, dfw, small["gsum"], small["donw"], small["dfnw"], loss)
    return grad_x[0], grads, vec


def _my_place():
    return lax.axis_index("x"), lax.axis_index("y"), lax.axis_index("c")


def _peers():
    x, y, c = _my_place()
    peers = []
    for k in range(1, N_DEV):
        px = 1 - x if k & 4 else x
        py = 1 - y if k & 2 else y
        pc = 1 - c if k & 1 else c
        peers.append(((px, py, pc), 4 * px + 2 * py + pc))
    return 4 * x + 2 * y + c, peers


def _spread_copies(kind, srcs, lands, send_sems, recv_sems):
    me, peers = _peers()
    kinds = [kind] * len(srcs) if isinstance(kind, str) else kind
    remote, local = [], []
    for i, (kd, src, land) in enumerate(zip(kinds, srcs, lands)):
        for k, (dev, idx) in reversed(list(enumerate(peers))):
            remote.append(pltpu.make_async_remote_copy(
                src_ref=src if kd == "gather" else src.at[idx], dst_ref=land.at[me],
                send_sem=send_sems.at[i * (N_DEV - 1) + k], recv_sem=recv_sems.at[i * (N_DEV - 1) + k],
                device_id=dev, device_id_type=MESH))
        local.append((src if kd == "gather" else src.at[me], land.at[me]))
    return remote, local


def _gather_two_level(arrays, name):
    n = len(arrays)

    def body(*refs):
        srcs, lands = refs[:n], refs[n:2 * n]
        send_sems, recv_sems, local_sems = refs[2 * n:]
        x, y, c = _my_place()
        me, sibling = (x, y, c), (x, y, 1 - c)
        chips = [(1 - x, y), (x, 1 - y), (1 - x, 1 - y)]

        def index(p):
            return 4 * p[0] + 2 * p[1] + p[2]

        def copy(i, k, block, to, src=None):
            blk = lands[i].at[index(block)]
            return pltpu.make_async_remote_copy(
                src_ref=blk if src is None else src, dst_ref=blk, send_sem=send_sems.at[7 * i + k],
                recv_sem=recv_sems.at[7 * i + k], device_id=to, device_id_type=MESH)

        mine = [pltpu.make_async_copy(srcs[i], lands[i].at[index(me)], local_sems.at[i]) for i in range(n)]
        for cp in mine:
            cp.start()
        first = []
        for i in range(n):
            first.append(copy(i, 0, me, sibling, src=srcs[i]))
            first += [copy(i, 1 + j, me, (*chip, c), src=srcs[i]) for j, chip in enumerate(chips)]
        for cp in first:
            cp.start()
        passed = []
        for i in range(n):
            for j, chip in enumerate(chips):
                copy(i, 1 + j, (*chip, c), me).wait_recv()
                passed.append(copy(i, 4 + j, (*chip, c), sibling))
                passed[-1].start()
        for i in range(n):
            copy(i, 0, sibling, me).wait_recv()
            for j, chip in enumerate(chips):
                copy(i, 4 + j, (*chip, 1 - c), me).wait_recv()
        for cp in first + passed:
            cp.wait_send()
        for cp in mine:
            cp.wait()

    return pl.pallas_call(
        body, name=name,
        out_shape=[jax.ShapeDtypeStruct((N_DEV,) + a.shape, a.dtype) for a in arrays],
        in_specs=[pl.BlockSpec(memory_space=pl.ANY)] * n, out_specs=[pl.BlockSpec(memory_space=pl.ANY)] * n,
        scratch_shapes=[pltpu.SemaphoreType.DMA((7 * n,)), pltpu.SemaphoreType.DMA((7 * n,)),
                        pltpu.SemaphoreType.DMA((n,))],
    )(*arrays)


def _land_shape(kind, a):
    return (N_DEV,) + a.shape if kind == "gather" else a.shape


HBM = pl.BlockSpec(memory_space=pltpu.HBM)
SEM = pl.BlockSpec(memory_space=pltpu.SEMAPHORE)


def _hbm(a):
    return pltpu.with_memory_space_constraint(a, pltpu.HBM)


def _carry_operands(kind, arrays):
    n = len(arrays)
    kinds = [kind] * n if isinstance(kind, str) else kind
    lands = [lax.empty(_land_shape(kd, a), a.dtype) for kd, a in zip(kinds, arrays)]
    sems = [pltpu.SemaphoreType.DMA((n * (N_DEV - 1),))] * 2
    return ([_hbm(a) for a in list(arrays) + lands], [HBM] * (2 * n),
            sems + [pltpu.HBM(a.shape, a.dtype) for a in list(arrays) + lands], [SEM] * 2 + [HBM] * (2 * n),
            pltpu.SemaphoreType.DMA((n,)))


def _carry_step(kind, in_refs, send_sems, recv_sems, local_sems, step, n_steps):
    n = len(in_refs) // 2
    remote, local = _spread_copies(kind, in_refs[:n], in_refs[n:], send_sems, recv_sems)
    copies = [pltpu.make_async_copy(s, d, local_sems.at[i]) for i, (s, d) in enumerate(local)]
    per_step = -(-len(remote) // n_steps)
    for s in range(-(-len(remote) // per_step)):
        @pl.when(step == s)
        def _():
            for cp in remote[s * per_step:(s + 1) * per_step]:
                cp.start()
            if s == 0:
                for cp in copies:
                    cp.start()

    @pl.when(step == n_steps - 1)
    def _():
        for cp in copies:
            cp.wait()


def _carry_state(extra_out):
    n = (len(extra_out) - 2) // 2
    return list(extra_out[2:2 + n]), list(extra_out[2 + n:]), extra_out[0], extra_out[1]


def _spread_start(arrays, kind, name):
    n = len(arrays)

    def body(*refs):
        srcs, lands = refs[:n], refs[n:2 * n]
        send_sems, recv_sems = refs[2 * n], refs[2 * n + 1]
        token = refs[4 * n + 2]
        local_sems = refs[4 * n + 3]
        remote, local = _spread_copies(kind, srcs, lands, send_sems, recv_sems)
        for cp in remote:
            cp.start()
        copies = [pltpu.make_async_copy(s, d, local_sems.at[i]) for i, (s, d) in enumerate(local)]
        for cp in copies:
            cp.start()
        for cp in copies:
            cp.wait()
        token[...] = jnp.zeros_like(token)

    sems = (pltpu.SemaphoreType.DMA((n * (N_DEV - 1),)),) * 2
    kinds = [kind] * n if isinstance(kind, str) else kind
    lands = [lax.empty(_land_shape(kd, a), a.dtype) for kd, a in zip(kinds, arrays)]
    out = pl.pallas_call(
        body, name=name,
        out_shape=sems + tuple(pltpu.HBM(a.shape, a.dtype) for a in list(arrays) + lands)
        + (jax.ShapeDtypeStruct((8, LANE), F32),),
        in_specs=[HBM] * (2 * n), out_specs=tuple([SEM] * 2 + [HBM] * (2 * n) + [pl.BlockSpec(memory_space=pltpu.VMEM)]),
        input_output_aliases={j: 2 + j for j in range(2 * n)},
        scratch_shapes=[pltpu.SemaphoreType.DMA((n,))],
        compiler_params=pltpu.CompilerParams(has_side_effects=pltpu.SideEffectType.DATAFLOW_SIDE_EFFECTING),
    )(*[_hbm(a) for a in arrays], *[_hbm(a) for a in lands])
    return (list(out[2:2 + n]), list(out[2 + n:2 + 2 * n]), out[0], out[1]), out[-1]


def _spread_wait(state, kind, after, name):
    srcs, lands, send_sems, recv_sems = state
    n = len(srcs)
    after = list(after) if isinstance(after, (list, tuple)) else [after]

    def body(*refs):
        remote, _ = _spread_copies(kind, refs[:n], refs[n:2 * n], refs[2 * n], refs[2 * n + 1])
        for cp in remote:
            cp.wait_send()
        for cp in remote:
            cp.wait_recv()

    out = pl.pallas_call(
        body, name=name,
        out_shape=tuple(pltpu.HBM(a.shape, a.dtype) for a in srcs + lands),
        in_specs=[HBM] * (2 * n) + [SEM, SEM] + [pl.BlockSpec(memory_space=pl.ANY)] * len(after),
        out_specs=tuple([HBM] * (2 * n)),
        input_output_aliases={j: j for j in range(2 * n)},
        compiler_params=pltpu.CompilerParams(has_side_effects=pltpu.SideEffectType.DATAFLOW_SIDE_EFFECTING),
    )(*srcs, *lands, send_sems, recv_sems, *after)
    return list(out[n:])


ADAM_ROWS = 128
ADAM_COLS = 256


def _adam_math(g, w, m, v):
    nm = ADAM_B1 * m + (1.0 - ADAM_B1) * g
    nv = ADAM_B2 * v + (1.0 - ADAM_B2) * (g * g)
    m_hat = nm / (1.0 - ADAM_B1 ** ADAM_STEP)
    v_hat = nv / (1.0 - ADAM_B2 ** ADAM_STEP)
    return -ADAM_LR * (m_hat / (jnp.sqrt(v_hat) + ADAM_EPS) + ADAM_WD * w), nm, nv


def _sum_parts(p_ref):
    g = p_ref[0].astype(F32)
    for s in range(1, N_DEV):
        g = g + p_ref[s].astype(F32)
    return g


def _adam_matrix(parts, w, m, v, name):
    _, r, c = w.shape
    rb = ADAM_ROWS if r % ADAM_ROWS == 0 else r
    cb = ADAM_COLS if (rb == r and c % ADAM_COLS == 0) else c

    def body(p_ref, w_ref, m_ref, v_ref, g_ref, d_ref, nm_ref, nv_ref):
        g = _sum_parts(p_ref)
        g_ref[0] = g
        d_ref[0], nm_ref[0], nv_ref[0] = _adam_math(g, w_ref[0], m_ref[0], v_ref[0])

    blk = pl.BlockSpec((1, rb, cb), lambda i, j: (0, i, j))
    return pl.pallas_call(
        body, name=name, grid=(r // rb, c // cb),
        in_specs=[pl.BlockSpec((N_DEV, rb, cb), lambda i, j: (0, i, j)), blk, blk, blk],
        out_specs=[blk] * 4, out_shape=[jax.ShapeDtypeStruct(w.shape, F32)] * 4,
        compiler_params=_params(("arbitrary", "arbitrary")),
    )(parts, w, m, v)


def _copy_rows(a, name):
    _, r, c = a.shape
    rb = min(TB, r)

    def body(a_ref, o_ref):
        o_ref[...] = a_ref[...]

    blk = pl.BlockSpec((1, rb, c), lambda i: (0, i, 0))
    return pl.pallas_call(body, name=name, grid=(r // rb,), in_specs=[blk], out_specs=blk,
                          out_shape=jax.ShapeDtypeStruct(a.shape, a.dtype), compiler_params=_params(("arbitrary",)))(a)


def _adam_vectors(parts, ws, ms, vs):
    nv = len(ws)

    def body(*refs):
        p_ref = refs[0]
        w_refs, m_refs, v_refs = refs[1:1 + nv], refs[1 + nv:1 + 2 * nv], refs[1 + 2 * nv:1 + 3 * nv]
        outs = refs[1 + 3 * nv:]
        g_all = _sum_parts(p_ref)
        for i in range(nv):
            n = w_refs[i].shape[1]
            g = g_all[i:i + 1, 0:n]
            d, nm, nvv = _adam_math(g, w_refs[i][...], m_refs[i][...], v_refs[i][...])
            outs[i][...] = g
            outs[nv + i][...] = d
            outs[2 * nv + i][...] = nm
            outs[3 * nv + i][...] = nvv
        outs[4 * nv][...] = g_all[LOSS_ROW:LOSS_ROW + 1, 0:1]

    shapes = [jax.ShapeDtypeStruct(a.shape, F32) for a in ws]
    out = pl.pallas_call(body, name="adam_vectors", out_shape=shapes * 4 + [jax.ShapeDtypeStruct((1, 1), F32)],
                         compiler_params=_params())(parts, *ws, *ms, *vs)
    return out[:nv], out[nv:2 * nv], out[2 * nv:3 * nv], out[3 * nv:4 * nv], out[4 * nv]


MATRICES = (("w_in", 1), ("gdn_conv_w", 1), ("w_out", 0), ("w_ffn_gate", 1), ("w_ffn_up", 1), ("w_ffn_down", 0))
TRANSPOSED = ("w_in", "w_ffn_gate", "w_ffn_up")
WEIGHTS = ("norm1_w", "w_in", "gdn_conv_w", "gdn_A_log", "gdn_dt_bias", "gdn_out_norm_w", "fox_f_bias", "fox_q_norm_w",
           "fox_k_norm_w", "w_out", "norm2_w", "w_ffn_gate", "w_ffn_up", "w_ffn_down", "final_norm_w")


def _join(blocks, axis):
    _, r, c = blocks.shape
    if axis == 0:
        return blocks.reshape(N_DEV * r, c)
    return blocks.transpose(1, 0, 2).reshape(r, N_DEV * c)


def _cut(full, axis):
    r, c = full.shape
    if axis == 0:
        return full.reshape(N_DEV, r // N_DEV, c)
    return full.reshape(r, N_DEV, c // N_DEV).transpose(1, 0, 2)


def kernel(x, norm1_w, w_in, gdn_conv_w, gdn_A_log, gdn_dt_bias, gdn_out_norm_w, fox_f_bias, fox_q_norm_w, fox_k_norm_w, w_out, norm2_w, w_ffn_gate, w_ffn_up, w_ffn_down, final_norm_w, loss_target, m_norm1_w, m_w_in, m_gdn_conv_w, m_gdn_A_log, m_gdn_dt_bias, m_gdn_out_norm_w, m_fox_f_bias, m_fox_q_norm_w, m_fox_k_norm_w, m_w_out, m_norm2_w, m_w_ffn_gate, m_w_ffn_up, m_w_ffn_down, m_final_norm_w, v_norm1_w, v_w_in, v_gdn_conv_w, v_gdn_A_log, v_gdn_dt_bias, v_gdn_out_norm_w, v_fox_f_bias, v_fox_q_norm_w, v_fox_k_norm_w, v_w_out, v_norm2_w, v_w_ffn_gate, v_w_ffn_up, v_w_ffn_down, v_final_norm_w):
    w = dict(norm1_w=norm1_w, w_in=w_in, gdn_conv_w=gdn_conv_w, gdn_A_log=gdn_A_log, gdn_dt_bias=gdn_dt_bias,
             gdn_out_norm_w=gdn_out_norm_w, fox_f_bias=fox_f_bias, fox_q_norm_w=fox_q_norm_w, fox_k_norm_w=fox_k_norm_w,
             w_out=w_out, norm2_w=norm2_w, w_ffn_gate=w_ffn_gate, w_ffn_up=w_ffn_up, w_ffn_down=w_ffn_down,
             final_norm_w=final_norm_w)
    m = dict(norm1_w=m_norm1_w, w_in=m_w_in, gdn_conv_w=m_gdn_conv_w, gdn_A_log=m_gdn_A_log, gdn_dt_bias=m_gdn_dt_bias,
             gdn_out_norm_w=m_gdn_out_norm_w, fox_f_bias=m_fox_f_bias, fox_q_norm_w=m_fox_q_norm_w,
             fox_k_norm_w=m_fox_k_norm_w, w_out=m_w_out, norm2_w=m_norm2_w, w_ffn_gate=m_w_ffn_gate,
             w_ffn_up=m_w_ffn_up, w_ffn_down=m_w_ffn_down, final_norm_w=m_final_norm_w)
    v = dict(norm1_w=v_norm1_w, w_in=v_w_in, gdn_conv_w=v_gdn_conv_w, gdn_A_log=v_gdn_A_log, gdn_dt_bias=v_gdn_dt_bias,
             gdn_out_norm_w=v_gdn_out_norm_w, fox_f_bias=v_fox_f_bias, fox_q_norm_w=v_fox_q_norm_w,
             fox_k_norm_w=v_fox_k_norm_w, w_out=v_w_out, norm2_w=v_norm2_w, w_ffn_gate=v_w_ffn_gate,
             w_ffn_up=v_w_ffn_up, w_ffn_down=v_w_ffn_down, final_norm_w=v_final_norm_w)
    late = ("w_out", "w_ffn_gate", "w_ffn_up", "w_ffn_down")
    xs, tgt = x[0], loss_target[0]
    vp = _vector_params({n: w[n] for n in VECTORS})

    def rows_of(d, n):
        return d[n].transpose(0, 2, 1) if n in TRANSPOSED else d[n]

    wr, mr, vr = ({n: rows_of(d, n) for n, _ in MATRICES} for d in (w, m, v))

    w_in_blocks, conv_blocks = _gather_two_level([wr["w_in"][0].astype(WIRE), w["gdn_conv_w"][0]], "gather_in")
    w_t = _join(w_in_blocks, 0)
    conv_w = _join(conv_blocks, 1)
    early = ("w_ffn_gate",)
    rest = tuple(n for n in late if n not in early)
    f = _mixer_forward(xs, vp, w_t, _small_rows(w_t), conv_w, ("gather", [wr[n][0].astype(WIRE) for n in rest]),
                       ("gather", [wr[n][0].astype(WIRE) for n in early]))
    full = {n: _join(b, 0) for n, b in zip(early, _spread_wait(f["carried_first"], "gather", f["mixg"], "gather_early_wait"))}
    full.update({n: _join(b, 0)
                 for n, b in zip(rest, _spread_wait(f["carried"], "gather", full[early[0]], "gather_late_wait"))})

    (h2, act, dgate, dup, dx3, dx2, dmixg, dmixf, dn2w, dfw, loss) = _tail(
        xs, f["mixg"], f["mixf"], tgt, full["w_out"], vp["n2w"], full["w_ffn_gate"], full["w_ffn_up"],
        full["w_ffn_down"], vp["fw"])
    dlate = _late_grads(f, h2, act, dgate, dup, dx3, dx2)

    grad_x, dw_in, dconv, small, state_grads, state_own = _mixer_backward(
        xs, vp, w_t, _small_rows(w_t), conv_w, f, dx2, dmixg, dmixf, ("scatter", [_cut(dlate[n], 0) for n in late]),
        scatter_own=True)
    vec = _pack_vectors(small["dn1w"], dn2w, dfw, small["gsum"], small["donw"], small["dfnw"], loss)

    state_vec, token = _spread_start([vec], "gather", "vectors_start")
    parts = dict(zip(late, _spread_wait(state_grads, "scatter", token, "grads_late_wait")))
    results = [{}, {}, {}, {}]

    updated = {}

    def update(n):
        out = _adam_matrix(parts[n], wr[n], mr[n], vr[n], "adam_" + n)
        updated[n] = out[0]
        for d, a in zip(results, out):
            d[n] = a.transpose(0, 2, 1) if n in TRANSPOSED else a

    for n in late:
        update(n)
    grad_x = _copy_rows(grad_x, "grad_x_result")
    ready = [updated[n] for n in late] + [d["w_in"] for d in (wr, mr, vr)] + [grad_x]
    (parts_vec,) = _spread_wait(state_vec, "gather", ready, "vectors_wait")
    row = lambda a: a.reshape(1, -1)
    *vec_out, total_loss = _adam_vectors(parts_vec, [row(w[n]) for n in VECTORS], [row(m[n]) for n in VECTORS],
                                         [row(v[n]) for n in VECTORS])
    parts["w_in"], parts["gdn_conv_w"] = _spread_wait(state_own, "scatter", parts_vec, "own_wait")
    update("w_in")
    update("gdn_conv_w")
    for d, arrs in zip(results, vec_out):
        for n, a in zip(VECTORS, arrs):
            d[n] = a.reshape(w[n].shape)
    return (total_loss[0, 0], grad_x, *[d[n] for d in results for n in WEIGHTS])
```
